```python
import jax, jax.numpy as jnp
from jax import lax
import numpy as np

D_MODEL = 2048
BATCH = 8
SEQ = 8192
DEPTH = 1

CHUNK = 64
EPS = 1e-5
N_BRANCHES = 2
SGU_WIDTH = D_MODEL
SGU_BLOCK = 128
SGU_GROUPS = 16
SGU_GROUP_DIM = SGU_WIDTH // SGU_GROUPS
SSD_WIDTH = D_MODEL
SSD_HEADDIM = 64
SSD_HEADS = SSD_WIDTH // SSD_HEADDIM
SSD_GROUPS = 4
SSD_HEADS_PER_GROUP = SSD_HEADS // SSD_GROUPS
SSD_STATE = 128
CONV_WIDTH = 4
XBC_WIDTH = SSD_WIDTH + 2 * SSD_GROUPS * SSD_STATE
IN_SPLITS = (SGU_WIDTH, SGU_WIDTH, SGU_WIDTH, SSD_WIDTH, XBC_WIDTH, SSD_HEADS, N_BRANCHES * D_MODEL)
IN_PROJ_WIDTH = sum(IN_SPLITS)
IN_OFFSETS = tuple(int(v) for v in np.cumsum(IN_SPLITS)[:-1])

kernel_name = "hybrid_sgu_ssd_gated_block"


def rmsnorm(x, w):
    xf = x.astype(jnp.float32)
    xf = xf * lax.rsqrt(jnp.mean(xf * xf, axis=-1, keepdims=True) + EPS)
    return (xf * w.astype(jnp.float32)).astype(x.dtype)


def layernorm(x, g, b):
    xf = x.astype(jnp.float32)
    mu = jnp.mean(xf, axis=-1, keepdims=True)
    var = jnp.mean(jnp.square(xf - mu), axis=-1, keepdims=True)
    y = (xf - mu) * lax.rsqrt(var + EPS) * g.astype(jnp.float32) + b.astype(jnp.float32)
    return y.astype(x.dtype)


def gated_group_rmsnorm(y, z, w):
    h = (y * jax.nn.silu(z)).astype(jnp.float32)
    shp = h.shape
    h = h.reshape(shp[:-1] + (SSD_GROUPS, shp[-1] // SSD_GROUPS))
    h = h * lax.rsqrt(jnp.mean(h * h, axis=-1, keepdims=True) + EPS)
    return (h.reshape(shp) * w.astype(jnp.float32)).astype(y.dtype)


def causal_depthwise_conv(x, w, b):
    S = x.shape[1]
    xp = jnp.pad(x, ((0, 0), (CONV_WIDTH - 1, 0), (0, 0)))
    out = b
    for k in range(CONV_WIDTH):
        out = out + xp[:, k:k + S, :] * w[k]
    return out


def sgu_mixer(u, v, z, norm_g, norm_b, w_s, b_s):
    Bsz, S, _ = u.shape
    nb = S // SGU_BLOCK
    vn = layernorm(v, norm_g, norm_b).reshape(Bsz, nb, SGU_BLOCK, SGU_GROUPS, SGU_GROUP_DIM)
    pos_chunk = jnp.arange(SGU_BLOCK) // CHUNK
    mask = pos_chunk[None, :] <= pos_chunk[:, None]
    w = jnp.where(mask[None], w_s, jnp.zeros_like(w_s))
    mixed = jnp.einsum("gij,bnjgc->bnigc", w, vn) + b_s.T[None, None, :, :, None]
    mixed = mixed.reshape(Bsz, S, SGU_WIDTH)
    return u * mixed * jax.nn.silu(z)


def ssd_mixer(xbc, z, dt_raw, conv_w, conv_b, dt_bias, A_log, D_skip, norm_w):
    Bsz, S, _ = xbc.shape
    nc = S // CHUNK
    G, R, P, N, L = SSD_GROUPS, SSD_HEADS_PER_GROUP, SSD_HEADDIM, SSD_STATE, CHUNK
    xbc = jax.nn.silu(causal_depthwise_conv(xbc, conv_w, conv_b))
    xs, Bm, Cm = jnp.split(xbc, (SSD_WIDTH, SSD_WIDTH + G * N), axis=-1)
    x = xs.reshape(Bsz, nc, L, G, R, P)
    Bm = Bm.reshape(Bsz, nc, L, G, N)
    Cm = Cm.reshape(Bsz, nc, L, G, N)
    dt = jax.nn.softplus(dt_raw.astype(jnp.float32) + dt_bias.astype(jnp.float32))
    dt = dt.reshape(Bsz, nc, L, G, R)
    A = -jnp.exp(A_log.astype(jnp.float32)).reshape(G, R)
    a = jnp.transpose(dt * A, (0, 1, 3, 4, 2))
    acs = jnp.cumsum(a, axis=-1)
    idx = jnp.arange(L)
    causal = idx[:, None] >= idx[None, :]
    seg = acs[..., :, None] - acs[..., None, :]
    Lmat = jnp.exp(jnp.where(causal, seg, -jnp.inf))
    x_dt = x.astype(jnp.float32) * dt[..., None]
    cb = jnp.einsum("bclgn,bcsgn->bcgls", Cm, Bm)
    y_diag = jnp.einsum("bcgls,bcgrls,bcsgrp->bclgrp", cb, Lmat, x_dt)
    decay_states = jnp.exp(acs[..., -1:] - acs)
    states = jnp.einsum("bclgn,bcgrl,bclgrp->bcgrpn", Bm, decay_states, x_dt)
    chunk_decay = jnp.exp(acs[..., -1])

    def step(h, inp):
        dec, st = inp
        return dec[..., None, None] * h + st, h

    h0 = jnp.zeros((Bsz, G, R, P, N), dtype=states.dtype)
    _, prev = lax.scan(step, h0, (jnp.moveaxis(chunk_decay, 1, 0), jnp.moveaxis(states, 1, 0)))
    prev = jnp.moveaxis(prev, 0, 1)
    y_off = jnp.einsum("bclgn,bcgrl,bcgrpn->bclgrp", Cm, jnp.exp(acs), prev)
    y = y_diag + y_off + x.astype(jnp.float32) * D_skip.astype(jnp.float32).reshape(G, R)[..., None]
    y = y.reshape(Bsz, S, SSD_WIDTH).astype(xbc.dtype)
    return gated_group_rmsnorm(y, z, norm_w)


def _fwd_setup_inputs(seed: int = 0) -> dict:
    key = jax.random.key(seed)
    ks = jax.random.split(key, 20)
    f32 = jnp.float32
    nrm = lambda k, shp, s: jax.random.normal(k, shp, f32) * s
    dt_init = jnp.exp(jax.random.uniform(ks[9], (DEPTH, SSD_HEADS), f32, np.log(1e-3), np.log(1e-1)))
    return {
        "x": jax.random.normal(ks[0], (BATCH, SEQ, D_MODEL), f32),
        "norm_w": 1.0 + nrm(ks[1], (DEPTH, D_MODEL), 0.02),
        "w_in": nrm(ks[2], (DEPTH, D_MODEL, IN_PROJ_WIDTH), D_MODEL ** -0.5),
        "gate_b": nrm(ks[3], (DEPTH, N_BRANCHES * D_MODEL), 0.1),
        "sgu_norm_g": 1.0 + nrm(ks[4], (DEPTH, SGU_WIDTH), 0.02),
        "sgu_norm_b": nrm(ks[5], (DEPTH, SGU_WIDTH), 0.02),
        "sgu_w": nrm(ks[6], (DEPTH, SGU_GROUPS, SGU_BLOCK, SGU_BLOCK), SGU_BLOCK ** -0.5),
        "sgu_b": 1.0 + nrm(ks[7], (DEPTH, SGU_GROUPS, SGU_BLOCK), 0.1),
        "conv_w": nrm(ks[8], (DEPTH, CONV_WIDTH, XBC_WIDTH), CONV_WIDTH ** -0.5),
        "conv_b": nrm(ks[10], (DEPTH, XBC_WIDTH), 0.02),
        "dt_bias": dt_init + jnp.log(-jnp.expm1(-dt_init)),
        "A_log": jnp.log(jax.random.uniform(ks[11], (DEPTH, SSD_HEADS), f32, 1.0, 16.0)),
        "D_skip": 1.0 + nrm(ks[12], (DEPTH, SSD_HEADS), 0.1),
        "ssd_norm_w": 1.0 + nrm(ks[13], (DEPTH, SSD_WIDTH), 0.02),
        "w_out": nrm(ks[14], (DEPTH, D_MODEL, D_MODEL), D_MODEL ** -0.5),
        "final_norm_w": 1.0 + nrm(ks[15], (D_MODEL,), 0.02),
    }


def _fwd_reference(x, norm_w, w_in, gate_b, sgu_norm_g, sgu_norm_b, sgu_w, sgu_b, conv_w, conv_b,
              dt_bias, A_log, D_skip, ssd_norm_w, w_out, final_norm_w):
    h = x
    Bsz, S, _ = x.shape
    for l in range(DEPTH):
        xn = rmsnorm(h, norm_w[l])
        proj = jnp.einsum("bsd,de->bse", xn, w_in[l])
        u_a, v_a, z_a, z_b, xbc, dt_raw, gate_logits = jnp.split(proj, IN_OFFSETS, axis=-1)
        y_a = sgu_mixer(u_a, v_a, z_a, sgu_norm_g[l], sgu_norm_b[l], sgu_w[l], sgu_b[l])
        y_b = ssd_mixer(xbc, z_b, dt_raw, conv_w[l], conv_b[l], dt_bias[l], A_log[l], D_skip[l], ssd_norm_w[l])
        gates = jax.nn.sigmoid(gate_logits + gate_b[l]).reshape(Bsz, S, N_BRANCHES, D_MODEL)
        merged = gates[:, :, 0, :] * y_a + gates[:, :, 1, :] * y_b
        h = h + jnp.einsum("bsd,de->bse", merged, w_out[l])
    return rmsnorm(h, final_norm_w)


import jax as _jax
import jax.numpy as _jnp

TWIN_FORMAT = 'train_step'
FWD_PARAMS = ['x', 'norm_w', 'w_in', 'gate_b', 'sgu_norm_g', 'sgu_norm_b', 'sgu_w', 'sgu_b', 'conv_w', 'conv_b', 'dt_bias', 'A_log', 'D_skip', 'ssd_norm_w', 'w_out', 'final_norm_w']
TWIN_WEIGHTS = ['norm_w', 'w_in', 'gate_b', 'sgu_norm_g', 'sgu_norm_b', 'sgu_w', 'sgu_b', 'conv_w', 'conv_b', 'dt_bias', 'A_log', 'D_skip', 'ssd_norm_w', 'w_out', 'final_norm_w']
TWIN_DIFF_INPUT = 'x'
TWIN_INPUTS = ['x', 'norm_w', 'w_in', 'gate_b', 'sgu_norm_g', 'sgu_norm_b', 'sgu_w', 'sgu_b', 'conv_w', 'conv_b', 'dt_bias', 'A_log', 'D_skip', 'ssd_norm_w', 'w_out', 'final_norm_w', 'loss_target', 'm_norm_w', 'm_w_in', 'm_gate_b', 'm_sgu_norm_g', 'm_sgu_norm_b', 'm_sgu_w', 'm_sgu_b', 'm_conv_w', 'm_conv_b', 'm_dt_bias', 'm_A_log', 'm_D_skip', 'm_ssd_norm_w', 'm_w_out', 'm_final_norm_w', 'v_norm_w', 'v_w_in', 'v_gate_b', 'v_sgu_norm_g', 'v_sgu_norm_b', 'v_sgu_w', 'v_sgu_b', 'v_conv_w', 'v_conv_b', 'v_dt_bias', 'v_A_log', 'v_D_skip', 'v_ssd_norm_w', 'v_w_out', 'v_final_norm_w']
TWIN_OUTPUTS = ['loss', 'grad_x', 'grad_norm_w', 'grad_w_in', 'grad_gate_b', 'grad_sgu_norm_g', 'grad_sgu_norm_b', 'grad_sgu_w', 'grad_sgu_b', 'grad_conv_w', 'grad_conv_b', 'grad_dt_bias', 'grad_A_log', 'grad_D_skip', 'grad_ssd_norm_w', 'grad_w_out', 'grad_final_norm_w', 'delta_norm_w', 'delta_w_in', 'delta_gate_b', 'delta_sgu_norm_g', 'delta_sgu_norm_b', 'delta_sgu_w', 'delta_sgu_b', 'delta_conv_w', 'delta_conv_b', 'delta_dt_bias', 'delta_A_log', 'delta_D_skip', 'delta_ssd_norm_w', 'delta_w_out', 'delta_final_norm_w', 'new_m_norm_w', 'new_m_w_in', 'new_m_gate_b', 'new_m_sgu_norm_g', 'new_m_sgu_norm_b', 'new_m_sgu_w', 'new_m_sgu_b', 'new_m_conv_w', 'new_m_conv_b', 'new_m_dt_bias', 'new_m_A_log', 'new_m_D_skip', 'new_m_ssd_norm_w', 'new_m_w_out', 'new_m_final_norm_w', 'new_v_norm_w', 'new_v_w_in', 'new_v_gate_b', 'new_v_sgu_norm_g', 'new_v_sgu_norm_b', 'new_v_sgu_w', 'new_v_sgu_b', 'new_v_conv_w', 'new_v_conv_b', 'new_v_dt_bias', 'new_v_A_log', 'new_v_D_skip', 'new_v_ssd_norm_w', 'new_v_w_out', 'new_v_final_norm_w']
TWIN_LEAF_KINDS = {'loss': 'loss', 'grad_x': 'grad_x', 'grad_norm_w': 'grad_w', 'grad_w_in': 'grad_w', 'grad_gate_b': 'grad_w', 'grad_sgu_norm_g': 'grad_w', 'grad_sgu_norm_b': 'grad_w', 'grad_sgu_w': 'grad_w', 'grad_sgu_b': 'grad_w', 'grad_conv_w': 'grad_w', 'grad_conv_b': 'grad_w', 'grad_dt_bias': 'grad_w', 'grad_A_log': 'grad_w', 'grad_D_skip': 'grad_w', 'grad_ssd_norm_w': 'grad_w', 'grad_w_out': 'grad_w', 'grad_final_norm_w': 'grad_w', 'delta_norm_w': 'delta_w', 'delta_w_in': 'delta_w', 'delta_gate_b': 'delta_w', 'delta_sgu_norm_g': 'delta_w', 'delta_sgu_norm_b': 'delta_w', 'delta_sgu_w': 'delta_w', 'delta_sgu_b': 'delta_w', 'delta_conv_w': 'delta_w', 'delta_conv_b': 'delta_w', 'delta_dt_bias': 'delta_w', 'delta_A_log': 'delta_w', 'delta_D_skip': 'delta_w', 'delta_ssd_norm_w': 'delta_w', 'delta_w_out': 'delta_w', 'delta_final_norm_w': 'delta_w', 'new_m_norm_w': 'new_m', 'new_m_w_in': 'new_m', 'new_m_gate_b': 'new_m', 'new_m_sgu_norm_g': 'new_m', 'new_m_sgu_norm_b': 'new_m', 'new_m_sgu_w': 'new_m', 'new_m_sgu_b': 'new_m', 'new_m_conv_w': 'new_m', 'new_m_conv_b': 'new_m', 'new_m_dt_bias': 'new_m', 'new_m_A_log': 'new_m', 'new_m_D_skip': 'new_m', 'new_m_ssd_norm_w': 'new_m', 'new_m_w_out': 'new_m', 'new_m_final_norm_w': 'new_m', 'new_v_norm_w': 'new_v', 'new_v_w_in': 'new_v', 'new_v_gate_b': 'new_v', 'new_v_sgu_norm_g': 'new_v', 'new_v_sgu_norm_b': 'new_v', 'new_v_sgu_w': 'new_v', 'new_v_sgu_b': 'new_v', 'new_v_conv_w': 'new_v', 'new_v_conv_b': 'new_v', 'new_v_dt_bias': 'new_v', 'new_v_A_log': 'new_v', 'new_v_D_skip': 'new_v', 'new_v_ssd_norm_w': 'new_v', 'new_v_w_out': 'new_v', 'new_v_final_norm_w': 'new_v'}


def _forward(args):
    return _fwd_reference(*[args[k] for k in FWD_PARAMS])


def _output_shape():
    def fwd():
        inp = _fwd_setup_inputs(0)
        return _fwd_reference(*[inp[k] for k in FWD_PARAMS])
    out = _jax.eval_shape(fwd)
    return out.shape, out.dtype

N_MICROBATCH = 1
ADAM_LR = 0.001
ADAM_B1 = 0.9
ADAM_B2 = 0.999
ADAM_EPS = 1e-08
ADAM_WD = 0.01
ADAM_STEP = 10
PER_EXAMPLE_BATCH_AXIS = {'x': 0, 'loss_target': 0}
SHARED_INPUTS = []
_WEIGHT_DTYPES = {'norm_w': _jnp.float32, 'w_in': _jnp.float32, 'gate_b': _jnp.float32, 'sgu_norm_g': _jnp.float32, 'sgu_norm_b': _jnp.float32, 'sgu_w': _jnp.float32, 'sgu_b': _jnp.float32, 'conv_w': _jnp.float32, 'conv_b': _jnp.float32, 'dt_bias': _jnp.float32, 'A_log': _jnp.float32, 'D_skip': _jnp.float32, 'ssd_norm_w': _jnp.float32, 'w_out': _jnp.float32, 'final_norm_w': _jnp.float32}
MOMENT_SCALE = {'norm_w': 1.151321e-01, 'w_in': 4.113750e-02, 'gate_b': 2.133321e-02, 'sgu_norm_g': 3.005907e-02, 'sgu_norm_b': 2.805533e-02, 'sgu_w': 2.868186e-02, 'sgu_b': 3.423964e-02, 'conv_w': 5.100377e-02, 'conv_b': 7.099041e-02, 'dt_bias': 2.155813e-01, 'A_log': 2.180861e-01, 'D_skip': 3.354107e-01, 'ssd_norm_w': 6.225429e-02, 'w_out': 7.146048e-02, 'final_norm_w': 3.194046e+01}


def _to_microbatches(a, axis):
    t = _jnp.moveaxis(a, axis, 0)
    t = t.reshape((N_MICROBATCH, t.shape[0] // N_MICROBATCH) + t.shape[1:])
    return _jnp.moveaxis(t, 1, axis + 1)


def setup_inputs(seed: int = 0) -> dict:
    inp = _fwd_setup_inputs(seed)
    key = _jax.random.fold_in(_jax.random.key(seed), 7919)
    shape, _ = _output_shape()
    out = dict(inp)
    out["loss_target"] = _jax.random.normal(_jax.random.fold_in(key, 0), shape, _jnp.float32)
    for i, name in enumerate(TWIN_WEIGHTS):
        w = inp[name].astype(_jnp.float32)
        if MOMENT_SCALE is None:
            s = _jnp.sqrt(_jnp.mean(_jnp.square(w)) + 1e-30)
        else:
            s = MOMENT_SCALE[name]
        km, kv = _jax.random.split(_jax.random.fold_in(key, i + 1))
        out[name] = w
        out["m_" + name] = s * _jax.random.normal(km, w.shape, _jnp.float32)
        out["v_" + name] = (s * s) * _jax.random.uniform(kv, w.shape, _jnp.float32, 0.5, 1.5)
    if N_MICROBATCH > 1:
        for name, axis in PER_EXAMPLE_BATCH_AXIS.items():
            out[name] = _to_microbatches(out[name], axis)
    return {'x': out['x'], 'norm_w': out['norm_w'], 'w_in': out['w_in'], 'gate_b': out['gate_b'], 'sgu_norm_g': out['sgu_norm_g'], 'sgu_norm_b': out['sgu_norm_b'], 'sgu_w': out['sgu_w'], 'sgu_b': out['sgu_b'], 'conv_w': out['conv_w'], 'conv_b': out['conv_b'], 'dt_bias': out['dt_bias'], 'A_log': out['A_log'], 'D_skip': out['D_skip'], 'ssd_norm_w': out['ssd_norm_w'], 'w_out': out['w_out'], 'final_norm_w': out['final_norm_w'], 'loss_target': out['loss_target'], 'm_norm_w': out['m_norm_w'], 'm_w_in': out['m_w_in'], 'm_gate_b': out['m_gate_b'], 'm_sgu_norm_g': out['m_sgu_norm_g'], 'm_sgu_norm_b': out['m_sgu_norm_b'], 'm_sgu_w': out['m_sgu_w'], 'm_sgu_b': out['m_sgu_b'], 'm_conv_w': out['m_conv_w'], 'm_conv_b': out['m_conv_b'], 'm_dt_bias': out['m_dt_bias'], 'm_A_log': out['m_A_log'], 'm_D_skip': out['m_D_skip'], 'm_ssd_norm_w': out['m_ssd_norm_w'], 'm_w_out': out['m_w_out'], 'm_final_norm_w': out['m_final_norm_w'], 'v_norm_w': out['v_norm_w'], 'v_w_in': out['v_w_in'], 'v_gate_b': out['v_gate_b'], 'v_sgu_norm_g': out['v_sgu_norm_g'], 'v_sgu_norm_b': out['v_sgu_norm_b'], 'v_sgu_w': out['v_sgu_w'], 'v_sgu_b': out['v_sgu_b'], 'v_conv_w': out['v_conv_w'], 'v_conv_b': out['v_conv_b'], 'v_dt_bias': out['v_dt_bias'], 'v_A_log': out['v_A_log'], 'v_D_skip': out['v_D_skip'], 'v_ssd_norm_w': out['v_ssd_norm_w'], 'v_w_out': out['v_w_out'], 'v_final_norm_w': out['v_final_norm_w']}


def _loss(weights, diff, rest, loss_target):
    with _jax.named_scope("forward"):
        args = {**rest, TWIN_DIFF_INPUT: diff, **{k: w.astype(_WEIGHT_DTYPES[k]) for k, w in weights.items()}}
        y = _forward(args)
    with _jax.named_scope("loss_head"):
        err = _jnp.square(y.astype(_jnp.float32) - loss_target)
        return 0.5 * _jnp.sum(_jnp.mean(err, axis=-1)) if err.ndim else 0.5 * err


def _adamw(w, g, m, v):
    m = ADAM_B1 * m + (1.0 - ADAM_B1) * g
    v = ADAM_B2 * v + (1.0 - ADAM_B2) * _jnp.square(g)
    m_hat = m / (1.0 - ADAM_B1 ** ADAM_STEP)
    v_hat = v / (1.0 - ADAM_B2 ** ADAM_STEP)
    delta = -ADAM_LR * (m_hat / (_jnp.sqrt(v_hat) + ADAM_EPS) + ADAM_WD * w)
    return delta, m, v


def reference(x, norm_w, w_in, gate_b, sgu_norm_g, sgu_norm_b, sgu_w, sgu_b, conv_w, conv_b, dt_bias, A_log, D_skip, ssd_norm_w, w_out, final_norm_w, loss_target, m_norm_w, m_w_in, m_gate_b, m_sgu_norm_g, m_sgu_norm_b, m_sgu_w, m_sgu_b, m_conv_w, m_conv_b, m_dt_bias, m_A_log, m_D_skip, m_ssd_norm_w, m_w_out, m_final_norm_w, v_norm_w, v_w_in, v_gate_b, v_sgu_norm_g, v_sgu_norm_b, v_sgu_w, v_sgu_b, v_conv_w, v_conv_b, v_dt_bias, v_A_log, v_D_skip, v_ssd_norm_w, v_w_out, v_final_norm_w):
    given = dict(x=x, norm_w=norm_w, w_in=w_in, gate_b=gate_b, sgu_norm_g=sgu_norm_g, sgu_norm_b=sgu_norm_b, sgu_w=sgu_w, sgu_b=sgu_b, conv_w=conv_w, conv_b=conv_b, dt_bias=dt_bias, A_log=A_log, D_skip=D_skip, ssd_norm_w=ssd_norm_w, w_out=w_out, final_norm_w=final_norm_w, loss_target=loss_target, m_norm_w=m_norm_w, m_w_in=m_w_in, m_gate_b=m_gate_b, m_sgu_norm_g=m_sgu_norm_g, m_sgu_norm_b=m_sgu_norm_b, m_sgu_w=m_sgu_w, m_sgu_b=m_sgu_b, m_conv_w=m_conv_w, m_conv_b=m_conv_b, m_dt_bias=m_dt_bias, m_A_log=m_A_log, m_D_skip=m_D_skip, m_ssd_norm_w=m_ssd_norm_w, m_w_out=m_w_out, m_final_norm_w=m_final_norm_w, v_norm_w=v_norm_w, v_w_in=v_w_in, v_gate_b=v_gate_b, v_sgu_norm_g=v_sgu_norm_g, v_sgu_norm_b=v_sgu_norm_b, v_sgu_w=v_sgu_w, v_sgu_b=v_sgu_b, v_conv_w=v_conv_w, v_conv_b=v_conv_b, v_dt_bias=v_dt_bias, v_A_log=v_A_log, v_D_skip=v_D_skip, v_ssd_norm_w=v_ssd_norm_w, v_w_out=v_w_out, v_final_norm_w=v_final_norm_w)
    weights = {n: given[n] for n in TWIN_WEIGHTS}
    shared = {n: given[n] for n in SHARED_INPUTS}
    per_example = {n: given[n] for n in ['x']}
    grad_fn = _jax.value_and_grad(_loss, argnums=(0, 1))

    def one_microbatch(ex, loss_target):
        ex = dict(ex)
        diff = ex.pop(TWIN_DIFF_INPUT)
        return grad_fn(weights, diff, {**shared, **ex}, loss_target)

    if N_MICROBATCH == 1:
        loss, (grad_w, grad_x) = one_microbatch(per_example, given["loss_target"])
    else:
        def body(carry, xs):
            loss_sum, grad_sum = carry
            l_k, (gw_k, gx_k) = one_microbatch(xs[0], xs[1])
            with _jax.named_scope("update"):
                return (loss_sum + l_k, _jax.tree.map(_jnp.add, grad_sum, gw_k)), gx_k

        init = (_jnp.zeros((), _jnp.float32), _jax.tree.map(_jnp.zeros_like, weights))
        (loss, grad_w), grad_x = _jax.lax.scan(body, init, (per_example, given["loss_target"]))
    with _jax.named_scope("update"):
        delta_w, new_m, new_v = {}, {}, {}
        for n in TWIN_WEIGHTS:
            delta_w[n], new_m[n], new_v[n] = _adamw(weights[n], grad_w[n], given["m_" + n], given["v_" + n])
    return (loss, grad_x, *[grad_w[n] for n in TWIN_WEIGHTS], *[delta_w[n] for n in TWIN_WEIGHTS],
            *[new_m[n] for n in TWIN_WEIGHTS], *[new_v[n] for n in TWIN_WEIGHTS])
```

```python
import functools

import numpy as np
import jax
import jax.numpy as jnp
from jax import lax
from jax.experimental import pallas as pl
from jax.experimental.pallas import tpu as pltpu

F32 = jnp.float32
BF16 = jnp.bfloat16
HI = lax.Precision.HIGHEST
MESH = pl.DeviceIdType.MESH

D = 2048
EPS = 1e-5
SGU_BLOCK = 128
SGU_GROUPS = 16
CHUNK = 64
HEADS = 32
HEADDIM = 64
SSD_GROUPS = 4
GROUP_W = D // SSD_GROUPS
STATE = 128
CONV_K = 4
XBC_W = D + 2 * SSD_GROUPS * STATE
W_IN = 15392
N_DEV = 8
SHARD_IN = W_IN // N_DEV
ADAM_LR, ADAM_B1, ADAM_B2, ADAM_EPS, ADAM_WD, ADAM_STEP = 0.001, 0.9, 0.999, 1e-08, 0.01, 10

LANE = 128
DT_W = LANE
OFF_U, OFF_V, OFF_ZA, OFF_G0, OFF_G1, OFF_ZB, OFF_XBC, OFF_DT = 0, 2048, 4096, 6144, 8192, 10240, 12288, 15360
WP = OFF_DT + DT_W
SEG_SGU = (0, 6144)
SEG_GATE = (6144, 4096)
SEG_SSD = (10240, WP - 10240)
VMEM_LIMIT = 56 * 1024 * 1024


def _cp(sem=None, vmem=VMEM_LIMIT):
    return pltpu.CompilerParams(dimension_semantics=sem, vmem_limit_bytes=vmem)


def _sigmoid(x):
    return 1.0 / (1.0 + jnp.exp(-x))


def _softplus(x):
    return jnp.maximum(x, 0.0) + jnp.log(1.0 + jnp.exp(-jnp.abs(x)))


def _dot(a, b, precision=None):
    return jnp.dot(a, b, preferred_element_type=F32, precision=precision)


def _dot_nt(a, b, precision=None):
    return lax.dot_general(a, b, (((1,), (1,)), ((), ())), preferred_element_type=F32, precision=precision)


def _dot_tn(a, b, precision=None):
    return lax.dot_general(a, b, (((0,), (0,)), ((), ())), preferred_element_type=F32, precision=precision)


def _matmul(a, b, *, trans_b=False, out_dtype=F32, tm, tn, tk, add=None, name):
    M, K = a.shape
    N = b.shape[0] if trans_b else b.shape[1]
    assert M % tm == 0 and N % tn == 0 and K % tk == 0, (name, M, N, K, tm, tn, tk)
    nk = K // tk

    def body(*refs):
        if add is None:
            a_ref, b_ref, o_ref, acc_ref = refs
            add_ref = None
        else:
            a_ref, b_ref, add_ref, o_ref, acc_ref = refs
        k = pl.program_id(2)
        part = _dot_nt(a_ref[...], b_ref[...]) if trans_b else _dot(a_ref[...], b_ref[...])

        @pl.when(k == 0)
        def _():
            acc_ref[...] = part

        @pl.when(k > 0)
        def _():
            acc_ref[...] += part

        @pl.when(k == nk - 1)
        def _():
            r = acc_ref[...]
            if add_ref is not None:
                r = r + add_ref[...]
            o_ref[...] = r.astype(out_dtype)

    in_specs = [pl.BlockSpec((tm, tk), lambda i, j, k: (i, k)),
                pl.BlockSpec((tn, tk), lambda i, j, k: (j, k)) if trans_b else pl.BlockSpec((tk, tn), lambda i, j, k: (k, j))]
    args = [a, b]
    if add is not None:
        in_specs.append(pl.BlockSpec((tm, tn), lambda i, j, k: (i, j)))
        args.append(add)
    return pl.pallas_call(
        body, name=name, grid=(M // tm, N // tn, nk), in_specs=in_specs,
        out_specs=pl.BlockSpec((tm, tn), lambda i, j, k: (i, j)),
        out_shape=jax.ShapeDtypeStruct((M, N), out_dtype),
        scratch_shapes=[pltpu.VMEM((tm, tn), F32)],
        compiler_params=_cp(("parallel", "parallel", "arbitrary")),
    )(*args)


def _norm_fwd(x, w, *, tm):
    S = x.shape[0]

    def body(x_ref, w_ref, o_ref):
        xv = x_ref[...]
        r = lax.rsqrt(jnp.mean(xv * xv, axis=-1, keepdims=True) + EPS)
        o_ref[...] = (xv * r * w_ref[...]).astype(BF16)

    return pl.pallas_call(
        body, name="norm_fwd", grid=(S // tm,),
        in_specs=[pl.BlockSpec((tm, D), lambda i: (i, 0)), pl.BlockSpec((1, D), lambda i: (0, 0))],
        out_specs=pl.BlockSpec((tm, D), lambda i: (i, 0)),
        out_shape=jax.ShapeDtypeStruct((S, D), BF16), compiler_params=_cp(("parallel",)),
    )(x, w)


def _norm_bwd(x, w, dxn, dh, *, tm):
    S = x.shape[0]

    def body(x_ref, w_ref, dxn_ref, dh_ref, gx_ref, dw_ref):
        xv = x_ref[...]
        r = lax.rsqrt(jnp.mean(xv * xv, axis=-1, keepdims=True) + EPS)
        xh = xv * r
        dxn_v = dxn_ref[...]
        dxh = dxn_v * w_ref[...]
        gx_ref[...] = dh_ref[...] + r * (dxh - xh * jnp.mean(dxh * xh, axis=-1, keepdims=True))

        @pl.when(pl.program_id(0) == 0)
        def _():
            dw_ref[...] = jnp.zeros_like(dw_ref)

        dw_ref[0:1, :] += jnp.sum(dxn_v * xh, axis=0, keepdims=True)

    row = pl.BlockSpec((tm, D), lambda i: (i, 0))
    return pl.pallas_call(
        body, name="norm_bwd", grid=(S // tm,),
        in_specs=[row, pl.BlockSpec((1, D), lambda i: (0, 0)), row, row],
        out_specs=[row, pl.BlockSpec((8, D), lambda i: (0, 0))],
        out_shape=[jax.ShapeDtypeStruct((S, D), F32), jax.ShapeDtypeStruct((8, D), F32)],
        compiler_params=_cp(("arbitrary",)),
    )(x, w, dxn, dh)


def _sgu_core(u_ref, v_ref, z_ref, g_ref, b_ref, wm_ref, bias_ref, vnb_ref, mixed_ref, tm):
    v = v_ref[...]
    mu = jnp.mean(v, axis=-1, keepdims=True)
    vc = v - mu
    rs = lax.rsqrt(jnp.mean(vc * vc, axis=-1, keepdims=True) + EPS)
    vh = vc * rs
    vnb_ref[...] = (vh * g_ref[...] + b_ref[...]).astype(BF16)
    for blk in range(tm // SGU_BLOCK):
        rows = pl.ds(blk * SGU_BLOCK, SGU_BLOCK)
        for gi in range(SGU_GROUPS):
            cols = pl.ds(gi * LANE, LANE)
            mixed_ref[rows, cols] = _dot(wm_ref[gi], vnb_ref[rows, cols]) + bias_ref[:, cols]
    return vh, rs


def _sgu_fwd(proj, g, b, wm, bias_full, *, tm):
    S = proj.shape[0]

    def body(u_ref, v_ref, z_ref, g_ref, b_ref, wm_ref, bias_ref, y_ref, vnb_ref, mixed_ref):
        _sgu_core(u_ref, v_ref, z_ref, g_ref, b_ref, wm_ref, bias_ref, vnb_ref, mixed_ref, tm)
        z = z_ref[...]
        y_ref[...] = u_ref[...] * mixed_ref[...] * (z * _sigmoid(z))

    seg = lambda off: pl.BlockSpec((tm, D), lambda i: (i, off // D))
    full = lambda a: pl.BlockSpec(a.shape, lambda i: (0,) * a.ndim)
    return pl.pallas_call(
        body, name="sgu_fwd", grid=(S // tm,),
        in_specs=[seg(OFF_U), seg(OFF_V), seg(OFF_ZA), full(g), full(b), full(wm), full(bias_full)],
        out_specs=pl.BlockSpec((tm, D), lambda i: (i, 0)),
        out_shape=jax.ShapeDtypeStruct((S, D), F32),
        scratch_shapes=[pltpu.VMEM((tm, D), BF16), pltpu.VMEM((tm, D), F32)],
        compiler_params=_cp(("parallel",)),
    )(proj, proj, proj, g, b, wm, bias_full)


def _sgu_bwd(proj, dy, g, b, wm, wmT, bias_full, mask, sel, *, tm):
    S = proj.shape[0]
    nsteps = S // tm

    def body(u_ref, v_ref, z_ref, dy_ref, g_ref, b_ref, wm_ref, wmT_ref, bias_ref, mask_ref, sel_ref,
             dp_ref, dws_ref, dbs_ref, dg_ref, db_ref, vnb_ref, mixed_ref, dmb_ref, dvn_ref, dbias_ref):
        i = pl.program_id(0)

        @pl.when(i == 0)
        def _():
            dws_ref[...] = jnp.zeros_like(dws_ref)
            dg_ref[...] = jnp.zeros_like(dg_ref)
            db_ref[...] = jnp.zeros_like(db_ref)
            dbias_ref[...] = jnp.zeros_like(dbias_ref)

        vh, rs = _sgu_core(u_ref, v_ref, z_ref, g_ref, b_ref, wm_ref, bias_ref, vnb_ref, mixed_ref, tm)
        u = u_ref[...]
        z = z_ref[...]
        dy_v = dy_ref[...]
        mixed = mixed_ref[...]
        sg = _sigmoid(z)
        sz = z * sg
        dp_ref[:, 0:D] = (dy_v * mixed * sz).astype(BF16)
        dp_ref[:, 2 * D:3 * D] = (dy_v * u * mixed * (sg * (1.0 + z * (1.0 - sg)))).astype(BF16)
        dmixed = dy_v * u * sz
        dmb_ref[...] = dmixed.astype(BF16)
        for blk in range(tm // SGU_BLOCK):
            dbias_ref[...] += dmixed[blk * SGU_BLOCK:(blk + 1) * SGU_BLOCK, :]
        for blk in range(tm // SGU_BLOCK):
            rows = pl.ds(blk * SGU_BLOCK, SGU_BLOCK)
            for gi in range(SGU_GROUPS):
                cols = pl.ds(gi * LANE, LANE)
                dm = dmb_ref[rows, cols]
                dvn_ref[rows, cols] = _dot(wmT_ref[gi], dm)
                dws_ref[gi] += _dot_nt(dm, vnb_ref[rows, cols])
        dvn = dvn_ref[...]
        dg_ref[0:1, :] += jnp.sum(dvn * vh, axis=0, keepdims=True)
        db_ref[0:1, :] += jnp.sum(dvn, axis=0, keepdims=True)
        dvh = dvn * g_ref[...]
        dv = rs * (dvh - jnp.mean(dvh, axis=-1, keepdims=True) - vh * jnp.mean(dvh * vh, axis=-1, keepdims=True))
        dp_ref[:, D:2 * D] = dv.astype(BF16)

        @pl.when(i == nsteps - 1)
        def _():
            for gi in range(SGU_GROUPS):
                dws_ref[gi] = dws_ref[gi] * mask_ref[...]
            dbs_ref[...] = _dot(dbias_ref[...], sel_ref[...], precision=HI)

    seg = lambda off: pl.BlockSpec((tm, D), lambda i: (i, off // D))
    full = lambda a: pl.BlockSpec(a.shape, lambda i: (0,) * a.ndim)
    return pl.pallas_call(
        body, name="sgu_bwd", grid=(nsteps,),
        in_specs=[seg(OFF_U), seg(OFF_V), seg(OFF_ZA), pl.BlockSpec((tm, D), lambda i: (i, 0)),
                  full(g), full(b), full(wm), full(wmT), full(bias_full), full(mask), full(sel)],
        out_specs=[pl.BlockSpec((tm, 3 * D), lambda i: (i, 0)),
                   pl.BlockSpec((SGU_GROUPS, SGU_BLOCK, SGU_BLOCK), lambda i: (0, 0, 0)),
                   pl.BlockSpec((SGU_BLOCK, LANE), lambda i: (0, 0)),
                   pl.BlockSpec((8, D), lambda i: (0, 0)), pl.BlockSpec((8, D), lambda i: (0, 0))],
        out_shape=[jax.ShapeDtypeStruct((S, 3 * D), BF16),
                   jax.ShapeDtypeStruct((SGU_GROUPS, SGU_BLOCK, SGU_BLOCK), F32),
                   jax.ShapeDtypeStruct((SGU_BLOCK, LANE), F32),
                   jax.ShapeDtypeStruct((8, D), F32), jax.ShapeDtypeStruct((8, D), F32)],
        scratch_shapes=[pltpu.VMEM((tm, D), BF16), pltpu.VMEM((tm, D), F32), pltpu.VMEM((tm, D), BF16),
                        pltpu.VMEM((tm, D), F32), pltpu.VMEM((SGU_BLOCK, D), F32)],
        compiler_params=_cp(("arbitrary",)),
    )(proj, proj, proj, dy, g, b, wm, wmT, bias_full, mask, sel)


SSD_T = 2 * CHUNK
HALO = 8


def _pair_masks():
    row = lax.broadcasted_iota(jnp.int32, (CHUNK, LANE), 0)
    lane = lax.broadcasted_iota(jnp.int32, (CHUNK, LANE), 1)
    pos = jnp.where(lane >= CHUNK, lane - CHUNK, lane)
    diag = (row == pos).astype(F32)
    causal = row >= pos
    lo = (lane < CHUNK).astype(F32)
    return diag, causal, lo, 1.0 - lo


def _ssd_chunk_fwd(c, ext_ref, dt_ref, cw_ref, cb_ref, dtb_ref, alog_ref, tri_ref, exp_ref, ht_ref):
    r0 = c * CHUNK
    pre = cb_ref[...] + sum(cw_ref[k:k + 1, :] * ext_ref[pl.ds(r0 + HALO - (CONV_K - 1) + k, CHUNK), :] for k in range(CONV_K))
    sg = _sigmoid(pre)
    xc = pre * sg
    dtr = dt_ref[pl.ds(r0, CHUNK), :] + dtb_ref[...]
    dtv = _softplus(dtr)
    A = -jnp.exp(alog_ref[...])
    acs = _dot(tri_ref[...], dtv * A, precision=HI)
    E = _dot(acs, exp_ref[...], precision=HI)
    dtE = _dot(dtv, exp_ref[...], precision=HI)
    return dict(pre=pre, sg=sg, xc=xc, dtr=dtr, dtv=dtv, A=A, E=E, dtE=dtE)


def _ssd_fwd(proj, conv_w, conv_b, dtb_p, alog_p, d_exp, norm_w, tri, expand):
    S = proj.shape[0]
    T = SSD_T
    nsteps = S // T
    ncl = T // CHUNK

    def body(zb_ref, xbc_ref, halo_ref, dt_ref, cw_ref, cb_ref, dtb_ref, alog_ref, dexp_ref, nw_ref, tri_ref, exp_ref,
             y_ref, yb_ref, st_ref, ht_ref, ext_ref):
        i = pl.program_id(0)

        @pl.when(i == 0)
        def _():
            ht_ref[...] = jnp.zeros_like(ht_ref)
            ext_ref[0:HALO, :] = jnp.zeros((HALO, XBC_W), F32)

        @pl.when(i > 0)
        def _():
            ext_ref[0:HALO, :] = halo_ref[...]

        ext_ref[HALO:HALO + T, :] = xbc_ref[...]
        diag, causal, lo, hi = _pair_masks()
        for c in range(ncl):
            q = _ssd_chunk_fwd(c, ext_ref, dt_ref, cw_ref, cb_ref, dtb_ref, alog_ref, tri_ref, exp_ref, ht_ref)
            rows = pl.ds(c * CHUNK, CHUNK)
            xc, E, dtE = q["xc"], q["E"], q["dtE"]
            xs = xc[:, 0:D]
            total = E[CHUNK - 1:CHUNK, :]
            x_dt = xs * dtE
            eE = jnp.exp(E)
            xw = x_dt * jnp.exp(total - E)
            st_ref[c] = ht_ref[...]
            for g in range(SSD_GROUPS):
                gc = slice(g * GROUP_W, (g + 1) * GROUP_W)
                Bg = xc[:, D + g * STATE:D + (g + 1) * STATE].astype(BF16)
                Cg = xc[:, D + SSD_GROUPS * STATE + g * STATE:D + SSD_GROUPS * STATE + (g + 1) * STATE].astype(BF16)
                cb2 = _dot_nt(Cg, jnp.concatenate([Bg, Bg], axis=0))
                htg = ht_ref[:, gc]
                y_ref[rows, gc] = eE[:, gc] * _dot(Cg, htg.astype(BF16)) + xs[:, gc] * dexp_ref[:, gc]
                for jj in range(GROUP_W // LANE):
                    pc = slice(g * GROUP_W + jj * LANE, g * GROUP_W + (jj + 1) * LANE)
                    Ej = E[:, pc]
                    e2 = jnp.sum(Ej * diag, axis=0, keepdims=True)
                    Mp = cb2 * jnp.exp(jnp.where(causal, Ej - e2, -1e30))
                    xj = x_dt[:, pc]
                    xbd = jnp.concatenate([xj * lo, xj * hi], axis=0).astype(BF16)
                    y_ref[rows, pc] += _dot(Mp.astype(BF16), xbd)
                ht_ref[:, gc] = jnp.exp(total[:, gc]) * htg + _dot_tn(Bg, xw[:, gc].astype(BF16))
            zb = zb_ref[rows, :]
            hh = y_ref[rows, :] * (zb * _sigmoid(zb))
            for g in range(SSD_GROUPS):
                gc = slice(g * GROUP_W, (g + 1) * GROUP_W)
                hg = hh[:, gc]
                r = lax.rsqrt(jnp.mean(hg * hg, axis=-1, keepdims=True) + EPS)
                yb_ref[rows, gc] = hg * r * nw_ref[:, gc]

    full = lambda a: pl.BlockSpec(a.shape, lambda i: (0,) * a.ndim)
    hb = T // HALO
    return pl.pallas_call(
        body, name="ssd_fwd", grid=(nsteps,),
        in_specs=[pl.BlockSpec((T, D), lambda i: (i, OFF_ZB // D)),
                  pl.BlockSpec((T, XBC_W), lambda i: (i, OFF_XBC // XBC_W)),
                  pl.BlockSpec((HALO, XBC_W), lambda i: (jnp.maximum(i * hb - 1, 0), OFF_XBC // XBC_W)),
                  pl.BlockSpec((T, DT_W), lambda i: (i, OFF_DT // DT_W)),
                  full(conv_w), full(conv_b), full(dtb_p), full(alog_p), full(d_exp), full(norm_w), full(tri), full(expand)],
        out_specs=[pl.BlockSpec((T, D), lambda i: (i, 0)), pl.BlockSpec((T, D), lambda i: (i, 0)),
                   pl.BlockSpec((ncl, STATE, D), lambda i: (i, 0, 0))],
        out_shape=[jax.ShapeDtypeStruct((S, D), F32), jax.ShapeDtypeStruct((S, D), F32),
                   jax.ShapeDtypeStruct((S // CHUNK, STATE, D), F32)],
        scratch_shapes=[pltpu.VMEM((STATE, D), F32), pltpu.VMEM((HALO + T, XBC_W), F32)],
        compiler_params=_cp(("arbitrary",)),
    )(proj, proj, proj, proj, conv_w, conv_b, dtb_p, alog_p, d_exp, norm_w, tri, expand)


def _ssd_bwd(proj, dyb, y, states, conv_w, conv_b, dtb_p, alog_p, d_exp, norm_w, tri, triT, expand, expandT):
    S = proj.shape[0]
    T = SSD_T
    nsteps = S // T
    ncl = T // CHUNK
    SSD_W = SEG_SSD[1]

    def body(zb_ref, xbc_ref, halo_ref, dt_ref, dyb_ref, y_ref, st_ref, cw_ref, cb_ref, dtb_ref, alog_ref, dexp_ref, nw_ref,
             tri_ref, triT_ref, exp_ref, expT_ref,
             dp_ref, dcw_ref, dcb_ref, ddtb_ref, dalog_ref, dD_ref, dnw_ref,
             dht_ref, ext_ref, dpre_ref, dy_s, dE_s, dxdt_s, dxc_s, dDacc_ref, dAacc_ref):
        i = pl.program_id(0)

        @pl.when(i == 0)
        def _():
            for r in (dht_ref, dcw_ref, dcb_ref, ddtb_ref, dnw_ref, dDacc_ref, dAacc_ref):
                r[...] = jnp.zeros_like(r)
            dpre_ref[T:T + HALO, :] = jnp.zeros((HALO, XBC_W), F32)

        @pl.when(i == nsteps - 1)
        def _():
            ext_ref[0:HALO, :] = jnp.zeros((HALO, XBC_W), F32)

        @pl.when(i < nsteps - 1)
        def _():
            ext_ref[0:HALO, :] = halo_ref[...]

        ext_ref[HALO:HALO + T, :] = xbc_ref[...]
        diag, causal, lo, hi = _pair_masks()
        last_row = (lax.broadcasted_iota(jnp.int32, (CHUNK, 1), 0) == CHUNK - 1).astype(F32)
        for c in reversed(range(ncl)):
            q = _ssd_chunk_fwd(c, ext_ref, dt_ref, cw_ref, cb_ref, dtb_ref, alog_ref, tri_ref, exp_ref, None)
            rows = pl.ds(c * CHUNK, CHUNK)
            pre, sg, xc, dtr, dtv, A, E, dtE = (q[k] for k in ("pre", "sg", "xc", "dtr", "dtv", "A", "E", "dtE"))
            xs = xc[:, 0:D]
            total = E[CHUNK - 1:CHUNK, :]
            x_dt = xs * dtE
            eE = jnp.exp(E)
            wdec = jnp.exp(total - E)
            zb = zb_ref[rows, :]
            yv = y_ref[rows, :]
            sgz = _sigmoid(zb)
            sz = zb * sgz
            hh = yv * sz
            for g in range(SSD_GROUPS):
                gc = slice(g * GROUP_W, (g + 1) * GROUP_W)
                hg = hh[:, gc]
                r = lax.rsqrt(jnp.mean(hg * hg, axis=-1, keepdims=True) + EPS)
                dyb_g = dyb_ref[rows, gc]
                dn = dyb_g * nw_ref[:, gc]
                dnw_ref[0:1, gc] += jnp.sum(dyb_g * hg * r, axis=0, keepdims=True)
                dy_s[:, gc] = r * dn - hg * (r * r * r) * jnp.mean(dn * hg, axis=-1, keepdims=True)
            dhh = dy_s[...]
            dp_ref[rows, 0:D] = (dhh * yv * (sgz * (1.0 + zb * (1.0 - sgz)))).astype(BF16)
            dy = dhh * sz
            dy_s[...] = dy
            dDacc_ref[0:1, :] += jnp.sum(dy * xs, axis=0, keepdims=True)
            dxc_s[:, 0:D] = dy * dexp_ref[...]
            for g in range(SSD_GROUPS):
                gc = slice(g * GROUP_W, (g + 1) * GROUP_W)
                bcol = slice(D + g * STATE, D + (g + 1) * STATE)
                ccol = slice(D + SSD_GROUPS * STATE + g * STATE, D + SSD_GROUPS * STATE + (g + 1) * STATE)
                Bg = xc[:, bcol].astype(BF16)
                Cg = xc[:, ccol].astype(BF16)
                B2 = jnp.concatenate([Bg, Bg], axis=0)
                cb2 = _dot_nt(Cg, B2)
                htg = st_ref[c, :, gc]
                htb = htg.astype(BF16)
                dhn = dht_ref[:, gc]
                dhnb = dhn.astype(BF16)
                dyg = dy[:, gc]
                eEg = eE[:, gc]
                wg = wdec[:, gc]
                xdg = x_dt[:, gc]
                CH = _dot(Cg, htb)
                dCHb = (dyg * eEg).astype(BF16)
                dC = _dot_nt(dCHb, htb)
                dl = jnp.exp(total[:, gc])
                dht_prev = _dot_tn(Cg, dCHb) + dl * dhn
                dtot = jnp.sum(dhn * htg, axis=0, keepdims=True) * dl
                dxw = _dot(Bg, dhnb)
                dB = _dot_nt((xdg * wg).astype(BF16), dhnb)
                dwd = dxw * xdg * wg
                dtot = dtot + jnp.sum(dwd, axis=0, keepdims=True)
                dE_s[:, gc] = dyg * eEg * CH - dwd + last_row * dtot
                dxdt_s[:, gc] = dxw * wg
                dcb2 = jnp.zeros((CHUNK, LANE), F32)
                for jj in range(GROUP_W // LANE):
                    pc = slice(g * GROUP_W + jj * LANE, g * GROUP_W + (jj + 1) * LANE)
                    Ej = E[:, pc]
                    e2 = jnp.sum(Ej * diag, axis=0, keepdims=True)
                    Lp = jnp.exp(jnp.where(causal, Ej - e2, -1e30))
                    Mp = cb2 * Lp
                    xj = x_dt[:, pc]
                    xbd = jnp.concatenate([xj * lo, xj * hi], axis=0).astype(BF16)
                    dyj = dy[:, pc].astype(BF16)
                    dMp = _dot_nt(dyj, xbd)
                    dxbd = _dot_tn(Mp.astype(BF16), dyj)
                    dxdt_s[:, pc] += dxbd[0:CHUNK, :] * lo + dxbd[CHUNK:2 * CHUNK, :] * hi
                    dcb2 = dcb2 + dMp * Lp
                    dseg = dMp * Mp
                    dE_s[:, pc] += dseg - diag * jnp.sum(dseg, axis=0, keepdims=True)
                dcb2b = dcb2.astype(BF16)
                dC = dC + _dot(dcb2b, B2)
                dB2 = _dot_tn(dcb2b, Cg)
                dB = dB + dB2[0:CHUNK, :] + dB2[CHUNK:2 * CHUNK, :]
                dxc_s[:, bcol] = dB
                dxc_s[:, ccol] = dC
                dht_ref[:, gc] = dht_prev
            dx_dt = dxdt_s[...]
            dxc_s[:, 0:D] += dx_dt * dtE
            red = _dot(jnp.concatenate([dE_s[...], dx_dt * xs], axis=0), expT_ref[...], precision=HI)
            da = _dot(triT_ref[...], red[0:CHUNK, :], precision=HI)
            ddtv = red[CHUNK:2 * CHUNK, :] + da * A
            dAacc_ref[0:1, :] += jnp.sum(da * dtv, axis=0, keepdims=True)
            ddtr = ddtv * _sigmoid(dtr)
            ddtb_ref[0:1, :] += jnp.sum(ddtr, axis=0, keepdims=True)
            dp_ref[rows, D + XBC_W:D + XBC_W + DT_W] = ddtr.astype(BF16)
            dpre = dxc_s[...] * (sg * (1.0 + pre * (1.0 - sg)))
            dpre_ref[rows, :] = dpre
            dcb_ref[0:1, :] += jnp.sum(dpre, axis=0, keepdims=True)
        dpre_t = dpre_ref[0:T, :]
        dxbc = jnp.zeros((T, XBC_W), F32)
        for k in range(CONV_K):
            dcw_ref[k:k + 1, :] += jnp.sum(dpre_t * ext_ref[pl.ds(HALO - (CONV_K - 1) + k, T), :], axis=0, keepdims=True)
            dxbc = dxbc + cw_ref[k:k + 1, :] * dpre_ref[pl.ds(CONV_K - 1 - k, T), :]
        dp_ref[:, D:D + XBC_W] = dxbc.astype(BF16)
        dpre_ref[T:T + HALO, :] = dpre_ref[0:HALO, :]

        @pl.when(i == nsteps - 1)
        def _():
            dalog_ref[...] = dAacc_ref[...] * (-jnp.exp(alog_ref[...]))
            dD_ref[...] = _dot(dDacc_ref[...], expT_ref[...], precision=HI)

    full = lambda a: pl.BlockSpec(a.shape, lambda i: (0,) * a.ndim)
    hb = T // HALO
    rev = lambda i: nsteps - 1 - i
    acc = lambda w: pl.BlockSpec((8, w), lambda i: (0, 0))
    return pl.pallas_call(
        body, name="ssd_bwd", grid=(nsteps,),
        in_specs=[pl.BlockSpec((T, D), lambda i: (rev(i), OFF_ZB // D)),
                  pl.BlockSpec((T, XBC_W), lambda i: (rev(i), OFF_XBC // XBC_W)),
                  pl.BlockSpec((HALO, XBC_W), lambda i: (jnp.maximum(rev(i) * hb - 1, 0), OFF_XBC // XBC_W)),
                  pl.BlockSpec((T, DT_W), lambda i: (rev(i), OFF_DT // DT_W)),
                  pl.BlockSpec((T, D), lambda i: (rev(i), 0)), pl.BlockSpec((T, D), lambda i: (rev(i), 0)),
                  pl.BlockSpec((ncl, STATE, D), lambda i: (rev(i), 0, 0)),
                  full(conv_w), full(conv_b), full(dtb_p), full(alog_p), full(d_exp), full(norm_w),
                  full(tri), full(triT), full(expand), full(expandT)],
        out_specs=[pl.BlockSpec((T, SSD_W), lambda i: (rev(i), 0)),
                   acc(XBC_W), acc(XBC_W), acc(DT_W), acc(DT_W), acc(DT_W), acc(D)],
        out_shape=[jax.ShapeDtypeStruct((S, SSD_W), BF16),
                   jax.ShapeDtypeStruct((8, XBC_W), F32), jax.ShapeDtypeStruct((8, XBC_W), F32),
                   jax.ShapeDtypeStruct((8, DT_W), F32), jax.ShapeDtypeStruct((8, DT_W), F32),
                   jax.ShapeDtypeStruct((8, DT_W), F32), jax.ShapeDtypeStruct((8, D), F32)],
        scratch_shapes=[pltpu.VMEM((STATE, D), F32), pltpu.VMEM((HALO + T, XBC_W), F32), pltpu.VMEM((T + HALO, XBC_W), F32),
                        pltpu.VMEM((CHUNK, D), F32), pltpu.VMEM((CHUNK, D), F32), pltpu.VMEM((CHUNK, D), F32),
                        pltpu.VMEM((CHUNK, XBC_W), F32), pltpu.VMEM((8, D), F32), pltpu.VMEM((8, DT_W), F32)],
        compiler_params=_cp(("arbitrary",)),
    )(proj, proj, proj, proj, dyb, y, states, conv_w, conv_b, dtb_p, alog_p, d_exp, norm_w, tri, triT, expand, expandT)


def _head(x, ya, yb, proj, target, gate_b, wout, fw, *, tm):
    S = x.shape[0]

    def body(x_ref, ya_ref, yb_ref, gl0_ref, gl1_ref, t_ref, gb_ref, w_ref, fw_ref,
             dh_ref, dhb_ref, mb_ref, dya_ref, dyb_ref, dgl_ref, loss_ref, dfw_ref, dgb_ref):
        @pl.when(pl.program_id(0) == 0)
        def _():
            loss_ref[...] = jnp.zeros_like(loss_ref)
            dfw_ref[...] = jnp.zeros_like(dfw_ref)
            dgb_ref[...] = jnp.zeros_like(dgb_ref)

        ya_v = ya_ref[...]
        yb_v = yb_ref[...]
        g0 = _sigmoid(gl0_ref[...] + gb_ref[:, 0:D])
        g1 = _sigmoid(gl1_ref[...] + gb_ref[:, D:2 * D])
        mb = (g0 * ya_v + g1 * yb_v).astype(BF16)
        mb_ref[...] = mb
        h = x_ref[...] + _dot(mb, w_ref[...])
        r = lax.rsqrt(jnp.mean(h * h, axis=-1, keepdims=True) + EPS)
        hn = h * r
        err = hn * fw_ref[...] - t_ref[...]
        loss_ref[...] += 0.5 * jnp.sum(jnp.mean(err * err, axis=-1, keepdims=True))
        dyf = err * (1.0 / D)
        dfw_ref[0:1, :] += jnp.sum(dyf * hn, axis=0, keepdims=True)
        dhn = dyf * fw_ref[...]
        dh = r * (dhn - hn * jnp.mean(dhn * hn, axis=-1, keepdims=True))
        dh_ref[...] = dh
        dhb = dh.astype(BF16)
        dhb_ref[...] = dhb
        dm = _dot_nt(dhb, w_ref[...])
        dya_ref[...] = dm * g0
        dyb_ref[...] = dm * g1
        dgl0 = dm * ya_v * g0 * (1.0 - g0)
        dgl1 = dm * yb_v * g1 * (1.0 - g1)
        dgl_ref[:, 0:D] = dgl0.astype(BF16)
        dgl_ref[:, D:2 * D] = dgl1.astype(BF16)
        dgb_ref[0:1, 0:D] += jnp.sum(dgl0, axis=0, keepdims=True)
        dgb_ref[0:1, D:2 * D] += jnp.sum(dgl1, axis=0, keepdims=True)

    row = pl.BlockSpec((tm, D), lambda i: (i, 0))
    seg = lambda off: pl.BlockSpec((tm, D), lambda i: (i, off // D))
    full = lambda a: pl.BlockSpec(a.shape, lambda i: (0,) * a.ndim)
    acc = lambda w: pl.BlockSpec((8, w), lambda i: (0, 0))
    return pl.pallas_call(
        body, name="head", grid=(S // tm,),
        in_specs=[row, row, row, seg(OFF_G0), seg(OFF_G1), row, full(gate_b), full(wout), full(fw)],
        out_specs=[row, row, row, row, row, pl.BlockSpec((tm, 2 * D), lambda i: (i, 0)), acc(LANE), acc(D), acc(2 * D)],
        out_shape=[jax.ShapeDtypeStruct((S, D), F32), jax.ShapeDtypeStruct((S, D), BF16), jax.ShapeDtypeStruct((S, D), BF16),
                   jax.ShapeDtypeStruct((S, D), F32), jax.ShapeDtypeStruct((S, D), F32), jax.ShapeDtypeStruct((S, 2 * D), BF16),
                   jax.ShapeDtypeStruct((8, LANE), F32), jax.ShapeDtypeStruct((8, D), F32), jax.ShapeDtypeStruct((8, 2 * D), F32)],
        compiler_params=_cp(("arbitrary",)),
    )(x, ya, yb, proj, proj, target, gate_b, wout, fw)


def _adamw(parts, w, m, v, *, tr, name):
    _, R, C = parts.shape
    assert R % tr == 0, (name, R, tr)

    def body(p_ref, w_ref, m_ref, v_ref, g_ref, d_ref, m2_ref, v2_ref):
        g = p_ref[0].astype(F32)
        for k in range(1, N_DEV):
            g = g + p_ref[k].astype(F32)
        m2 = ADAM_B1 * m_ref[...] + (1.0 - ADAM_B1) * g
        v2 = ADAM_B2 * v_ref[...] + (1.0 - ADAM_B2) * (g * g)
        m_hat = m2 / (1.0 - ADAM_B1 ** ADAM_STEP)
        v_hat = v2 / (1.0 - ADAM_B2 ** ADAM_STEP)
        g_ref[...] = g
        d_ref[...] = -ADAM_LR * (m_hat / (jnp.sqrt(v_hat) + ADAM_EPS) + ADAM_WD * w_ref[...])
        m2_ref[...] = m2
        v2_ref[...] = v2

    row = pl.BlockSpec((tr, C), lambda i: (i, 0))
    return pl.pallas_call(
        body, name=name, grid=(R // tr,),
        in_specs=[pl.BlockSpec((N_DEV, tr, C), lambda i: (0, i, 0)), row, row, row],
        out_specs=[row, row, row, row],
        out_shape=[jax.ShapeDtypeStruct((R, C), F32)] * 4,
        compiler_params=_cp(("parallel",)),
    )(parts, w, m, v)


def _place():
    x, y, c = lax.axis_index("x"), lax.axis_index("y"), lax.axis_index("c")
    return x, y, c


def _all_gather(arrs, *, name):
    n = len(arrs)

    def body(*refs):
        ins, outs = refs[:n], refs[n:2 * n]
        send_sems, recv_sems, local_sems = refs[2 * n:]
        x, y, c = _place()
        me, sibling = (x, y, c), (x, y, 1 - c)
        chips = [(1 - x, y), (x, 1 - y), (1 - x, 1 - y)]

        def idx(px, py, pc):
            return 4 * px + 2 * py + pc

        def copy(k, a, block, to, src=None):
            slab = outs[a].at[idx(*block)]
            return pltpu.make_async_remote_copy(
                src_ref=slab if src is None else src, dst_ref=slab,
                send_sem=send_sems.at[k, a], recv_sem=recv_sems.at[k, a], device_id=to, device_id_type=MESH)

        mine = [pltpu.make_async_copy(ins[a], outs[a].at[idx(*me)], local_sems.at[a]) for a in range(n)]
        for cp in mine:
            cp.start()
        first = []
        for a in range(n):
            first.append(copy(0, a, me, sibling, src=ins[a]))
            first += [copy(1 + j, a, me, (*chip, c), src=ins[a]) for j, chip in enumerate(chips)]
        for cp in first:
            cp.start()
        passed = []
        for j, chip in enumerate(chips):
            for a in range(n):
                copy(1 + j, a, (*chip, c), me).wait_recv()
                fwd = copy(4 + j, a, (*chip, c), sibling)
                fwd.start()
                passed.append(fwd)
        for a in range(n):
            copy(0, a, sibling, me).wait_recv()
            for j, chip in enumerate(chips):
                copy(4 + j, a, (*chip, 1 - c), me).wait_recv()
        for cp in first + passed:
            cp.wait_send()
        for cp in mine:
            cp.wait()

    anyspec = pl.BlockSpec(memory_space=pl.ANY)
    return pl.pallas_call(
        body, name=name,
        in_specs=[anyspec] * n, out_specs=[anyspec] * n,
        out_shape=[jax.ShapeDtypeStruct((N_DEV,) + a.shape, a.dtype) for a in arrs],
        scratch_shapes=[pltpu.SemaphoreType.DMA((7, n)), pltpu.SemaphoreType.DMA((7, n)), pltpu.SemaphoreType.DMA((n,))],
    )(*arrs)


def _exchange(parts, *, name):
    n = len(parts)
    rel = [(dx, dy, dc) for dx in (0, 1) for dy in (0, 1) for dc in (0, 1)][1:]

    def body(*refs):
        ins, outs = refs[:n], refs[n:2 * n]
        send_sems, recv_sems, local_sems = refs[2 * n:]
        x, y, c = _place()
        me = 4 * x + 2 * y + c

        def flip(v, d):
            return 1 - v if d else v

        def copy(k, a):
            dx, dy, dc = rel[k]
            px, py, pc = flip(x, dx), flip(y, dy), flip(c, dc)
            return pltpu.make_async_remote_copy(
                src_ref=ins[a].at[4 * px + 2 * py + pc], dst_ref=outs[a].at[me],
                send_sem=send_sems.at[k, a], recv_sem=recv_sems.at[k, a], device_id=(px, py, pc), device_id_type=MESH)

        def landed(k, a):
            dx, dy, dc = rel[k]
            px, py, pc = flip(x, dx), flip(y, dy), flip(c, dc)
            return pltpu.make_async_remote_copy(
                src_ref=ins[a].at[me], dst_ref=outs[a].at[4 * px + 2 * py + pc],
                send_sem=send_sems.at[k, a], recv_sem=recv_sems.at[k, a], device_id=(px, py, pc), device_id_type=MESH)

        mine = [pltpu.make_async_copy(ins[a].at[me], outs[a].at[me], local_sems.at[a]) for a in range(n)]
        for cp in mine:
            cp.start()
        sends = [copy(k, a) for a in range(n) for k in range(len(rel))]
        for cp in sends:
            cp.start()
        for a in range(n):
            for k in range(len(rel)):
                landed(k, a).wait_recv()
        for cp in sends:
            cp.wait_send()
        for cp in mine:
            cp.wait()

    anyspec = pl.BlockSpec(memory_space=pl.ANY)
    return pl.pallas_call(
        body, name=name,
        in_specs=[anyspec] * n, out_specs=[anyspec] * n,
        out_shape=[jax.ShapeDtypeStruct(a.shape, a.dtype) for a in parts],
        scratch_shapes=[pltpu.SemaphoreType.DMA((7, n)), pltpu.SemaphoreType.DMA((7, n)), pltpu.SemaphoreType.DMA((n,))],
    )(*parts)


WEIGHTS = ('norm_w', 'w_in', 'gate_b', 'sgu_norm_g', 'sgu_norm_b', 'sgu_w', 'sgu_b', 'conv_w', 'conv_b', 'dt_bias', 'A_log',
           'D_skip', 'ssd_norm_w', 'w_out', 'final_norm_w')
SHARDED = ('w_in', 'conv_w', 'w_out')
PACK_ROW = 8 * LANE


def _constants():
    tri = np.tril(np.ones((CHUNK, CHUNK), np.float32))
    expand = np.zeros((DT_W, D), np.float32)
    for h in range(HEADS):
        expand[h, h * HEADDIM:(h + 1) * HEADDIM] = 1.0
    sel = np.zeros((D, LANE), np.float32)
    for g in range(SGU_GROUPS):
        sel[g * LANE:(g + 1) * LANE, g] = 1.0
    pos_chunk = np.arange(SGU_BLOCK) // CHUNK
    mask = (pos_chunk[None, :] <= pos_chunk[:, None]).astype(np.float32)
    return dict(tri=jnp.asarray(tri), triT=jnp.asarray(tri.T.copy()), expand=jnp.asarray(expand),
                expandT=jnp.asarray(expand.T.copy()), sel=jnp.asarray(sel), mask=jnp.asarray(mask))


def _permute_w(w):
    return jnp.concatenate([w[:, :6144], w[:, 11296:], w[:, 6144:11296], jnp.zeros((D, DT_W - HEADS), w.dtype)], axis=1)


def _local_step(x2, tgt, wp, wout, cw, p):
    S = x2.shape[0]
    k = _constants()
    xn = _norm_fwd(x2, p['norm_w'], tm=min(512, S))
    proj = _matmul(xn, wp, tm=min(1024, S), tn=1408, tk=D, name="in_proj")
    wm32 = p['sgu_w'][0] * k['mask']
    wm = wm32.astype(BF16)
    wmT = jnp.swapaxes(wm32, 1, 2).astype(BF16)
    bias_full = jnp.repeat(p['sgu_b'][0].T, LANE, axis=1)
    tm_sgu = min(256, S)
    ya = _sgu_fwd(proj, p['sgu_norm_g'], p['sgu_norm_b'], wm, bias_full, tm=tm_sgu)
    pad32 = lambda a: jnp.pad(a, ((0, 0), (0, DT_W - HEADS)))
    dtb_p, alog_p = pad32(p['dt_bias']), pad32(p['A_log'])
    d_exp = jnp.repeat(p['D_skip'], HEADDIM, axis=1)
    ssd_args = (cw, p['conv_b'], dtb_p, alog_p, d_exp, p['ssd_norm_w'])
    y, yb, states = _ssd_fwd(proj, *ssd_args, k['tri'], k['expand'])
    dh, dhb, mb, dya, dyb, dgl, loss, dfw, dgb = _head(
        x2, ya, yb, proj, tgt, p['gate_b'], wout, p['final_norm_w'][None, :], tm=min(128, S))
    dsgu, dws, dbsT, dsg, dsb = _sgu_bwd(proj, dya, p['sgu_norm_g'], p['sgu_norm_b'], wm, wmT, bias_full, k['mask'], k['sel'],
                                         tm=tm_sgu)
    dssd, dcw, dcb, ddtb, dalog, dD, dnw = _ssd_bwd(proj, dyb, y, states, *ssd_args, k['tri'], k['triT'], k['expand'], k['expandT'])
    tm = min(512, S)
    dxn = _matmul(dsgu, wp[:, :SEG_GATE[0]], trans_b=True, tm=tm, tn=1024, tk=2048, name="dxn_sgu")
    dxn = _matmul(dgl, wp[:, SEG_GATE[0]:SEG_SSD[0]], trans_b=True, tm=tm, tn=1024, tk=2048, add=dxn, name="dxn_gate")
    dxn = _matmul(dssd, wp[:, SEG_SSD[0]:], trans_b=True, tm=tm, tn=512, tk=SEG_SSD[1], add=dxn, name="dxn_ssd")
    grad_x, dnorm = _norm_bwd(x2, p['norm_w'], dxn, dh, tm=min(256, S))
    xnT = xn.T
    tk = min(512, S)
    dw_sgu = _matmul(xnT, dsgu, tm=1024, tn=1024, tk=tk, name="dw_in_sgu")
    dw_gate = _matmul(xnT, dgl, tm=1024, tn=1024, tk=tk, name="dw_in_gate")
    dw_ssd = _matmul(xnT, dssd, tm=512, tn=SEG_SSD[1], tk=tk, name="dw_in_ssd")
    dw_in = jnp.concatenate([dw_sgu, dw_ssd[:, :W_IN - SEG_SSD[0]], dw_gate], axis=1)
    dw_out = _matmul(mb.T, dhb, tm=1024, tn=1024, tk=tk, name="dw_out")
    grads = dict(
        norm_w=dnorm[0:1], w_in=dw_in[None], gate_b=dgb[0:1], sgu_norm_g=dsg[0:1], sgu_norm_b=dsb[0:1], sgu_w=dws[None],
        sgu_b=dbsT[:, :SGU_GROUPS].T[None], conv_w=dcw[0:CONV_K][None], conv_b=dcb[0:1], dt_bias=ddtb[0:1, :HEADS],
        A_log=dalog[0:1, :HEADS], D_skip=dD[0:1, :HEADS], ssd_norm_w=dnw[0:1], w_out=dw_out[None], final_norm_w=dfw[0])
    return loss[0, 0], grad_x, grads


def _pack(arrs):
    rows, offs, r = [], [], 0
    for a in arrs:
        n = a.size
        nr = -(-n // PACK_ROW) * 8
        rows.append(jnp.pad(a.reshape(-1).astype(F32), (0, nr * LANE - n)).reshape(nr, LANE))
        offs.append(r)
        r += nr
    return jnp.concatenate(rows, axis=0), offs


def kernel(x, norm_w, w_in, gate_b, sgu_norm_g, sgu_norm_b, sgu_w, sgu_b, conv_w, conv_b, dt_bias, A_log, D_skip, ssd_norm_w, w_out, final_norm_w, loss_target, m_norm_w, m_w_in, m_gate_b, m_sgu_norm_g, m_sgu_norm_b, m_sgu_w, m_sgu_b, m_conv_w, m_conv_b, m_dt_bias, m_A_log, m_D_skip, m_ssd_norm_w, m_w_out, m_final_norm_w, v_norm_w, v_w_in, v_gate_b, v_sgu_norm_g, v_sgu_norm_b, v_sgu_w, v_sgu_b, v_conv_w, v_conv_b, v_dt_bias, v_A_log, v_D_skip, v_ssd_norm_w, v_w_out, v_final_norm_w):
    w = dict(norm_w=norm_w, w_in=w_in, gate_b=gate_b, sgu_norm_g=sgu_norm_g, sgu_norm_b=sgu_norm_b, sgu_w=sgu_w, sgu_b=sgu_b,
             conv_w=conv_w, conv_b=conv_b, dt_bias=dt_bias, A_log=A_log, D_skip=D_skip, ssd_norm_w=ssd_norm_w, w_out=w_out,
             final_norm_w=final_norm_w)
    m = dict(norm_w=m_norm_w, w_in=m_w_in, gate_b=m_gate_b, sgu_norm_g=m_sgu_norm_g, sgu_norm_b=m_sgu_norm_b, sgu_w=m_sgu_w,
             sgu_b=m_sgu_b, conv_w=m_conv_w, conv_b=m_conv_b, dt_bias=m_dt_bias, A_log=m_A_log, D_skip=m_D_skip,
             ssd_norm_w=m_ssd_norm_w, w_out=m_w_out, final_norm_w=m_final_norm_w)
    v = dict(norm_w=v_norm_w, w_in=v_w_in, gate_b=v_gate_b, sgu_norm_g=v_sgu_norm_g, sgu_norm_b=v_sgu_norm_b, sgu_w=v_sgu_w,
             sgu_b=v_sgu_b, conv_w=v_conv_w, conv_b=v_conv_b, dt_bias=v_dt_bias, A_log=v_A_log, D_skip=v_D_skip,
             ssd_norm_w=v_ssd_norm_w, w_out=v_w_out, final_norm_w=v_final_norm_w)
    me = 4 * lax.axis_index("x") + 2 * lax.axis_index("y") + lax.axis_index("c")
    shard_cw = XBC_W // N_DEV

    g_in, g_out, g_cw = _all_gather([w_in[0].astype(BF16), w_out[0].astype(BF16), conv_w[0]], name="gather_weights")
    wp = _permute_w(jnp.swapaxes(g_in, 0, 1).reshape(D, W_IN))
    wout_full = g_out.reshape(D, D)
    cw_full = jnp.swapaxes(g_cw, 0, 1).reshape(CONV_K, XBC_W)

    loss_part, grad_x, grads = _local_step(x[0], loss_target[0], wp, wout_full, cw_full, w)

    part_in = jnp.swapaxes(grads['w_in'][0].astype(BF16).reshape(D, N_DEV, SHARD_IN), 0, 1)
    part_out = grads['w_out'][0].astype(BF16).reshape(N_DEV, D // N_DEV, D)
    recv_in, recv_out = _exchange([part_in, part_out], name="exchange_grads")
    res = {}
    res['w_in'] = _adamw(recv_in, w_in[0], m_w_in[0], v_w_in[0], tr=128, name="adamw_w_in")
    res['w_out'] = _adamw(recv_out, w_out[0], m_w_out[0], v_w_out[0], tr=128, name="adamw_w_out")

    small = [n for n in WEIGHTS if n not in SHARDED]
    packed, offs = _pack([grads[n] for n in small] + [loss_part, grads['conv_w']])
    (gathered,) = _all_gather([packed], name="gather_small")
    off_loss, off_cw = offs[-2], offs[-1]
    cw_parts = gathered[:, off_cw:, :].reshape(N_DEV, CONV_K, XBC_W)
    cw_parts = lax.dynamic_slice_in_dim(cw_parts, me * shard_cw, shard_cw, axis=2)
    cw_rows = _pack([cw_parts[0]])[0].shape[0]
    cw_parts = jnp.pad(cw_parts.reshape(N_DEV, -1), ((0, 0), (0, cw_rows * LANE - CONV_K * shard_cw))).reshape(N_DEV, cw_rows, LANE)
    parts = jnp.concatenate([gathered[:, :off_cw, :], cw_parts], axis=1)
    zero = jnp.zeros((), F32)
    packs = [_pack([d[n] for n in small] + [zero, d['conv_w']])[0] for d in (w, m, v)]
    outs = _adamw(parts, *packs, tr=parts.shape[1], name="adamw_small")

    def unpack(o, name):
        if name == 'conv_w':
            return o[off_cw:off_cw + cw_rows].reshape(-1)[:CONV_K * shard_cw].reshape(w['conv_w'].shape)
        r0 = offs[small.index(name)]
        n = w[name].size
        return o[r0:r0 + -(-n // PACK_ROW) * 8].reshape(-1)[:n].reshape(w[name].shape)

    for n in small + ['conv_w']:
        res[n] = [unpack(o, n) for o in outs]
    for n in ('w_in', 'w_out'):
        res[n] = [o[None] for o in res[n]]
    loss = outs[0][off_loss, 0]
    return (loss, grad_x[None], *[res[n][0] for n in WEIGHTS], *[res[n][1] for n in WEIGHTS],
            *[res[n][2] for n in WEIGHTS], *[res[n][3] for n in WEIGHTS])
```

```python
import functools

import numpy as np
import jax
import jax.numpy as jnp
from jax import lax
from jax.experimental import pallas as pl
from jax.experimental.pallas import tpu as pltpu

F32 = jnp.float32
BF16 = jnp.bfloat16
HI = lax.Precision.HIGHEST
MESH = pl.DeviceIdType.MESH

D = 2048
EPS = 1e-5
SGU_BLOCK = 128
SGU_GROUPS = 16
CHUNK = 64
HEADS = 32
HEADDIM = 64
SSD_GROUPS = 4
GROUP_W = D // SSD_GROUPS
STATE = 128
CONV_K = 4
XBC_W = D + 2 * SSD_GROUPS * STATE
W_IN = 15392
N_DEV = 8
SHARD_IN = W_IN // N_DEV
ADAM_LR, ADAM_B1, ADAM_B2, ADAM_EPS, ADAM_WD, ADAM_STEP = 0.001, 0.9, 0.999, 1e-08, 0.01, 10

LANE = 128
DT_W = LANE
OFF_U, OFF_V, OFF_ZA, OFF_G0, OFF_G1, OFF_ZB, OFF_XBC, OFF_DT = 0, 2048, 4096, 6144, 8192, 10240, 12288, 15360
WP = OFF_DT + DT_W
SEG_SGU = (0, 6144)
SEG_GATE = (6144, 4096)
SEG_SSD = (10240, WP - 10240)
VMEM_LIMIT = 56 * 1024 * 1024


def _cp(sem=None, vmem=VMEM_LIMIT):
    return pltpu.CompilerParams(dimension_semantics=sem, vmem_limit_bytes=vmem)


def _sigmoid(x):
    return 1.0 / (1.0 + jnp.exp(-x))


def _softplus(x):
    return jnp.maximum(x, 0.0) + jnp.log(1.0 + jnp.exp(-jnp.abs(x)))


def _dot(a, b, precision=None):
    return jnp.dot(a, b, preferred_element_type=F32, precision=precision)


def _dot_nt(a, b, precision=None):
    return lax.dot_general(a, b, (((1,), (1,)), ((), ())), preferred_element_type=F32, precision=precision)


def _dot_tn(a, b, precision=None):
    return lax.dot_general(a, b, (((0,), (0,)), ((), ())), preferred_element_type=F32, precision=precision)


def _matmul(a, b, *, trans_b=False, out_dtype=F32, tm, tn, tk, add=None, after=None, name):
    M, K = a.shape
    N = b.shape[0] if trans_b else b.shape[1]
    assert M % tm == 0 and N % tn == 0 and K % tk == 0, (name, M, N, K, tm, tn, tk)
    nk = K // tk

    def body(*refs):
        a_ref, b_ref = refs[:2]
        add_ref = refs[2] if add is not None else None
        o_ref, acc_ref = refs[-2:]
        k = pl.program_id(2)
        part = _dot_nt(a_ref[...], b_ref[...]) if trans_b else _dot(a_ref[...], b_ref[...])

        @pl.when(k == 0)
        def _():
            acc_ref[...] = part

        @pl.when(k > 0)
        def _():
            acc_ref[...] += part

        @pl.when(k == nk - 1)
        def _():
            r = acc_ref[...]
            if add_ref is not None:
                r = r + add_ref[...]
            o_ref[...] = r.astype(out_dtype)

    in_specs = [pl.BlockSpec((tm, tk), lambda i, j, k: (i, k)),
                pl.BlockSpec((tn, tk), lambda i, j, k: (j, k)) if trans_b else pl.BlockSpec((tk, tn), lambda i, j, k: (k, j))]
    args = [a, b]
    if add is not None:
        in_specs.append(pl.BlockSpec((tm, tn), lambda i, j, k: (i, j)))
        args.append(add)
    if after is not None:
        in_specs.append(pl.BlockSpec(memory_space=pl.ANY))
        args.append(after)
    return pl.pallas_call(
        body, name=name, grid=(M // tm, N // tn, nk), in_specs=in_specs,
        out_specs=pl.BlockSpec((tm, tn), lambda i, j, k: (i, j)),
        out_shape=jax.ShapeDtypeStruct((M, N), out_dtype),
        scratch_shapes=[pltpu.VMEM((tm, tn), F32)],
        compiler_params=_cp(("parallel", "parallel", "arbitrary")),
    )(*args)


def _norm_fwd(x, w, *, tm):
    S = x.shape[0]

    def body(x_ref, w_ref, o_ref):
        xv = x_ref[...]
        r = lax.rsqrt(jnp.mean(xv * xv, axis=-1, keepdims=True) + EPS)
        o_ref[...] = (xv * r * w_ref[...]).astype(BF16)

    return pl.pallas_call(
        body, name="norm_fwd", grid=(S // tm,),
        in_specs=[pl.BlockSpec((tm, D), lambda i: (i, 0)), pl.BlockSpec((1, D), lambda i: (0, 0))],
        out_specs=pl.BlockSpec((tm, D), lambda i: (i, 0)),
        out_shape=jax.ShapeDtypeStruct((S, D), BF16), compiler_params=_cp(("parallel",)),
    )(x, w)


def _norm_bwd(x, w, dxn, dh, *, tm):
    S = x.shape[0]

    def body(x_ref, w_ref, dxn_ref, dh_ref, gx_ref, dw_ref):
        xv = x_ref[...]
        r = lax.rsqrt(jnp.mean(xv * xv, axis=-1, keepdims=True) + EPS)
        xh = xv * r
        dxn_v = dxn_ref[...]
        dxh = dxn_v * w_ref[...]
        gx_ref[...] = dh_ref[...] + r * (dxh - xh * jnp.mean(dxh * xh, axis=-1, keepdims=True))

        @pl.when(pl.program_id(0) == 0)
        def _():
            dw_ref[...] = jnp.zeros_like(dw_ref)

        dw_ref[0:1, :] += jnp.sum(dxn_v * xh, axis=0, keepdims=True)

    row = pl.BlockSpec((tm, D), lambda i: (i, 0))
    return pl.pallas_call(
        body, name="norm_bwd", grid=(S // tm,),
        in_specs=[row, pl.BlockSpec((1, D), lambda i: (0, 0)), row, row],
        out_specs=[row, pl.BlockSpec((8, D), lambda i: (0, 0))],
        out_shape=[jax.ShapeDtypeStruct((S, D), F32), jax.ShapeDtypeStruct((8, D), F32)],
        compiler_params=_cp(("arbitrary",)),
    )(x, w, dxn, dh)


def _sgu_core(u_ref, v_ref, z_ref, g_ref, b_ref, wm_ref, bias_ref, vnb_ref, mixed_ref, tm):
    v = v_ref[...]
    mu = jnp.mean(v, axis=-1, keepdims=True)
    vc = v - mu
    rs = lax.rsqrt(jnp.mean(vc * vc, axis=-1, keepdims=True) + EPS)
    vh = vc * rs
    vnb_ref[...] = (vh * g_ref[...] + b_ref[...]).astype(BF16)
    for blk in range(tm // SGU_BLOCK):
        rows = pl.ds(blk * SGU_BLOCK, SGU_BLOCK)
        for gi in range(SGU_GROUPS):
            cols = pl.ds(gi * LANE, LANE)
            mixed_ref[rows, cols] = _dot(wm_ref[gi], vnb_ref[rows, cols]) + bias_ref[:, cols]
    return vh, rs


def _sgu_fwd(proj, g, b, wm, bias_full, *, tm):
    S = proj.shape[0]

    def body(u_ref, v_ref, z_ref, g_ref, b_ref, wm_ref, bias_ref, y_ref, vnb_ref, mixed_ref):
        _sgu_core(u_ref, v_ref, z_ref, g_ref, b_ref, wm_ref, bias_ref, vnb_ref, mixed_ref, tm)
        z = z_ref[...]
        y_ref[...] = u_ref[...] * mixed_ref[...] * (z * _sigmoid(z))

    seg = lambda off: pl.BlockSpec((tm, D), lambda i: (i, off // D))
    full = lambda a: pl.BlockSpec(a.shape, lambda i: (0,) * a.ndim)
    return pl.pallas_call(
        body, name="sgu_fwd", grid=(S // tm,),
        in_specs=[seg(OFF_U), seg(OFF_V), seg(OFF_ZA), full(g), full(b), full(wm), full(bias_full)],
        out_specs=pl.BlockSpec((tm, D), lambda i: (i, 0)),
        out_shape=jax.ShapeDtypeStruct((S, D), F32),
        scratch_shapes=[pltpu.VMEM((tm, D), BF16), pltpu.VMEM((tm, D), F32)],
        compiler_params=_cp(("parallel",)),
    )(proj, proj, proj, g, b, wm, bias_full)


def _sgu_bwd(proj, dy, g, b, wm, wmT, bias_full, mask, sel, *, tm):
    S = proj.shape[0]
    nsteps = S // tm

    def body(u_ref, v_ref, z_ref, dy_ref, g_ref, b_ref, wm_ref, wmT_ref, bias_ref, mask_ref, sel_ref,
             dp_ref, dws_ref, dbs_ref, dg_ref, db_ref, vnb_ref, mixed_ref, dmb_ref, dvn_ref, dbias_ref):
        i = pl.program_id(0)

        @pl.when(i == 0)
        def _():
            dws_ref[...] = jnp.zeros_like(dws_ref)
            dg_ref[...] = jnp.zeros_like(dg_ref)
            db_ref[...] = jnp.zeros_like(db_ref)
            dbias_ref[...] = jnp.zeros_like(dbias_ref)

        vh, rs = _sgu_core(u_ref, v_ref, z_ref, g_ref, b_ref, wm_ref, bias_ref, vnb_ref, mixed_ref, tm)
        u = u_ref[...]
        z = z_ref[...]
        dy_v = dy_ref[...]
        mixed = mixed_ref[...]
        sg = _sigmoid(z)
        sz = z * sg
        dp_ref[:, 0:D] = (dy_v * mixed * sz).astype(BF16)
        dp_ref[:, 2 * D:3 * D] = (dy_v * u * mixed * (sg * (1.0 + z * (1.0 - sg)))).astype(BF16)
        dmixed = dy_v * u * sz
        dmb_ref[...] = dmixed.astype(BF16)
        for blk in range(tm // SGU_BLOCK):
            dbias_ref[...] += dmixed[blk * SGU_BLOCK:(blk + 1) * SGU_BLOCK, :]
        for blk in range(tm // SGU_BLOCK):
            rows = pl.ds(blk * SGU_BLOCK, SGU_BLOCK)
            for gi in range(SGU_GROUPS):
                cols = pl.ds(gi * LANE, LANE)
                dm = dmb_ref[rows, cols]
                dvn_ref[rows, cols] = _dot(wmT_ref[gi], dm)
                dws_ref[gi] += _dot_nt(dm, vnb_ref[rows, cols])
        dvn = dvn_ref[...]
        dg_ref[0:1, :] += jnp.sum(dvn * vh, axis=0, keepdims=True)
        db_ref[0:1, :] += jnp.sum(dvn, axis=0, keepdims=True)
        dvh = dvn * g_ref[...]
        dv = rs * (dvh - jnp.mean(dvh, axis=-1, keepdims=True) - vh * jnp.mean(dvh * vh, axis=-1, keepdims=True))
        dp_ref[:, D:2 * D] = dv.astype(BF16)

        @pl.when(i == nsteps - 1)
        def _():
            for gi in range(SGU_GROUPS):
                dws_ref[gi] = dws_ref[gi] * mask_ref[...]
            dbs_ref[...] = _dot(dbias_ref[...], sel_ref[...], precision=HI)

    seg = lambda off: pl.BlockSpec((tm, D), lambda i: (i, off // D))
    full = lambda a: pl.BlockSpec(a.shape, lambda i: (0,) * a.ndim)
    return pl.pallas_call(
        body, name="sgu_bwd", grid=(nsteps,),
        in_specs=[seg(OFF_U), seg(OFF_V), seg(OFF_ZA), pl.BlockSpec((tm, D), lambda i: (i, 0)),
                  full(g), full(b), full(wm), full(wmT), full(bias_full), full(mask), full(sel)],
        out_specs=[pl.BlockSpec((tm, 3 * D), lambda i: (i, 0)),
                   pl.BlockSpec((SGU_GROUPS, SGU_BLOCK, SGU_BLOCK), lambda i: (0, 0, 0)),
                   pl.BlockSpec((SGU_BLOCK, LANE), lambda i: (0, 0)),
                   pl.BlockSpec((8, D), lambda i: (0, 0)), pl.BlockSpec((8, D), lambda i: (0, 0))],
        out_shape=[jax.ShapeDtypeStruct((S, 3 * D), BF16),
                   jax.ShapeDtypeStruct((SGU_GROUPS, SGU_BLOCK, SGU_BLOCK), F32),
                   jax.ShapeDtypeStruct((SGU_BLOCK, LANE), F32),
                   jax.ShapeDtypeStruct((8, D), F32), jax.ShapeDtypeStruct((8, D), F32)],
        scratch_shapes=[pltpu.VMEM((tm, D), BF16), pltpu.VMEM((tm, D), F32), pltpu.VMEM((tm, D), BF16),
                        pltpu.VMEM((tm, D), F32), pltpu.VMEM((SGU_BLOCK, D), F32)],
        compiler_params=_cp(("arbitrary",)),
    )(proj, proj, proj, dy, g, b, wm, wmT, bias_full, mask, sel)


SSD_T = 2 * CHUNK
HALO = 8


def _pair_masks():
    row = lax.broadcasted_iota(jnp.int32, (CHUNK, LANE), 0)
    lane = lax.broadcasted_iota(jnp.int32, (CHUNK, LANE), 1)
    pos = jnp.where(lane >= CHUNK, lane - CHUNK, lane)
    diag = (row == pos).astype(F32)
    causal = row >= pos
    lo = (lane < CHUNK).astype(F32)
    return diag, causal, lo, 1.0 - lo


def _ssd_chunk_fwd(c, ext_ref, dt_ref, cw_ref, cb_ref, dtb_ref, alog_ref, tri_ref, exp_ref, ht_ref):
    r0 = c * CHUNK
    pre = cb_ref[...] + sum(cw_ref[k:k + 1, :] * ext_ref[pl.ds(r0 + HALO - (CONV_K - 1) + k, CHUNK), :] for k in range(CONV_K))
    sg = _sigmoid(pre)
    xc = pre * sg
    dtr = dt_ref[pl.ds(r0, CHUNK), :] + dtb_ref[...]
    dtv = _softplus(dtr)
    A = -jnp.exp(alog_ref[...])
    acs = _dot(tri_ref[...], dtv * A, precision=HI)
    E = _dot(acs, exp_ref[...], precision=HI)
    dtE = _dot(dtv, exp_ref[...], precision=HI)
    return dict(pre=pre, sg=sg, xc=xc, dtr=dtr, dtv=dtv, A=A, E=E, dtE=dtE)


def _ssd_fwd(proj, conv_w, conv_b, dtb_p, alog_p, d_exp, norm_w, tri, expand):
    S = proj.shape[0]
    T = SSD_T
    nsteps = S // T
    ncl = T // CHUNK

    def body(zb_ref, xbc_ref, halo_ref, dt_ref, cw_ref, cb_ref, dtb_ref, alog_ref, dexp_ref, nw_ref, tri_ref, exp_ref,
             y_ref, yb_ref, st_ref, ht_ref, ext_ref):
        i = pl.program_id(0)

        @pl.when(i == 0)
        def _():
            ht_ref[...] = jnp.zeros_like(ht_ref)
            ext_ref[0:HALO, :] = jnp.zeros((HALO, XBC_W), F32)

        @pl.when(i > 0)
        def _():
            ext_ref[0:HALO, :] = halo_ref[...]

        ext_ref[HALO:HALO + T, :] = xbc_ref[...]
        diag, causal, lo, hi = _pair_masks()
        for c in range(ncl):
            q = _ssd_chunk_fwd(c, ext_ref, dt_ref, cw_ref, cb_ref, dtb_ref, alog_ref, tri_ref, exp_ref, ht_ref)
            rows = pl.ds(c * CHUNK, CHUNK)
            xc, E, dtE = q["xc"], q["E"], q["dtE"]
            xs = xc[:, 0:D]
            total = E[CHUNK - 1:CHUNK, :]
            x_dt = xs * dtE
            eE = jnp.exp(E)
            xw = x_dt * jnp.exp(total - E)
            st_ref[c] = ht_ref[...]
            for g in range(SSD_GROUPS):
                gc = slice(g * GROUP_W, (g + 1) * GROUP_W)
                Bg = xc[:, D + g * STATE:D + (g + 1) * STATE].astype(BF16)
                Cg = xc[:, D + SSD_GROUPS * STATE + g * STATE:D + SSD_GROUPS * STATE + (g + 1) * STATE].astype(BF16)
                cb2 = _dot_nt(Cg, jnp.concatenate([Bg, Bg], axis=0))
                htg = ht_ref[:, gc]
                y_ref[rows, gc] = eE[:, gc] * _dot(Cg, htg.astype(BF16)) + xs[:, gc] * dexp_ref[:, gc]
                for jj in range(GROUP_W // LANE):
                    pc = slice(g * GROUP_W + jj * LANE, g * GROUP_W + (jj + 1) * LANE)
                    Ej = E[:, pc]
                    e2 = jnp.sum(Ej * diag, axis=0, keepdims=True)
                    Mp = cb2 * jnp.exp(jnp.where(causal, Ej - e2, -1e30))
                    xj = x_dt[:, pc]
                    xbd = jnp.concatenate([xj * lo, xj * hi], axis=0).astype(BF16)
                    y_ref[rows, pc] += _dot(Mp.astype(BF16), xbd)
                ht_ref[:, gc] = jnp.exp(total[:, gc]) * htg + _dot_tn(Bg, xw[:, gc].astype(BF16))
            zb = zb_ref[rows, :]
            hh = y_ref[rows, :] * (zb * _sigmoid(zb))
            for g in range(SSD_GROUPS):
                gc = slice(g * GROUP_W, (g + 1) * GROUP_W)
                hg = hh[:, gc]
                r = lax.rsqrt(jnp.mean(hg * hg, axis=-1, keepdims=True) + EPS)
                yb_ref[rows, gc] = hg * r * nw_ref[:, gc]

    full = lambda a: pl.BlockSpec(a.shape, lambda i: (0,) * a.ndim)
    hb = T // HALO
    return pl.pallas_call(
        body, name="ssd_fwd", grid=(nsteps,),
        in_specs=[pl.BlockSpec((T, D), lambda i: (i, OFF_ZB // D)),
                  pl.BlockSpec((T, XBC_W), lambda i: (i, OFF_XBC // XBC_W)),
                  pl.BlockSpec((HALO, XBC_W), lambda i: (jnp.maximum(i * hb - 1, 0), OFF_XBC // XBC_W)),
                  pl.BlockSpec((T, DT_W), lambda i: (i, OFF_DT // DT_W)),
                  full(conv_w), full(conv_b), full(dtb_p), full(alog_p), full(d_exp), full(norm_w), full(tri), full(expand)],
        out_specs=[pl.BlockSpec((T, D), lambda i: (i, 0)), pl.BlockSpec((T, D), lambda i: (i, 0)),
                   pl.BlockSpec((ncl, STATE, D), lambda i: (i, 0, 0))],
        out_shape=[jax.ShapeDtypeStruct((S, D), F32), jax.ShapeDtypeStruct((S, D), F32),
                   jax.ShapeDtypeStruct((S // CHUNK, STATE, D), F32)],
        scratch_shapes=[pltpu.VMEM((STATE, D), F32), pltpu.VMEM((HALO + T, XBC_W), F32)],
        compiler_params=_cp(("arbitrary",)),
    )(proj, proj, proj, proj, conv_w, conv_b, dtb_p, alog_p, d_exp, norm_w, tri, expand)


def _ssd_bwd(proj, dyb, y, states, conv_w, conv_b, dtb_p, alog_p, d_exp, norm_w, tri, triT, expand, expandT):
    S = proj.shape[0]
    T = SSD_T
    nsteps = S // T
    ncl = T // CHUNK
    SSD_W = SEG_SSD[1]

    def body(zb_ref, xbc_ref, halo_ref, dt_ref, dyb_ref, y_ref, st_ref, cw_ref, cb_ref, dtb_ref, alog_ref, dexp_ref, nw_ref,
             tri_ref, triT_ref, exp_ref, expT_ref,
             dp_ref, dcw_ref, dcb_ref, ddtb_ref, dalog_ref, dD_ref, dnw_ref,
             dht_ref, ext_ref, dpre_ref, dy_s, dE_s, dxdt_s, dxc_s, dDacc_ref, dAacc_ref):
        i = pl.program_id(0)

        @pl.when(i == 0)
        def _():
            for r in (dht_ref, dcw_ref, dcb_ref, ddtb_ref, dnw_ref, dDacc_ref, dAacc_ref):
                r[...] = jnp.zeros_like(r)
            dpre_ref[T:T + HALO, :] = jnp.zeros((HALO, XBC_W), F32)

        @pl.when(i == nsteps - 1)
        def _():
            ext_ref[0:HALO, :] = jnp.zeros((HALO, XBC_W), F32)

        @pl.when(i < nsteps - 1)
        def _():
            ext_ref[0:HALO, :] = halo_ref[...]

        ext_ref[HALO:HALO + T, :] = xbc_ref[...]
        diag, causal, lo, hi = _pair_masks()
        last_row = (lax.broadcasted_iota(jnp.int32, (CHUNK, 1), 0) == CHUNK - 1).astype(F32)
        for c in reversed(range(ncl)):
            q = _ssd_chunk_fwd(c, ext_ref, dt_ref, cw_ref, cb_ref, dtb_ref, alog_ref, tri_ref, exp_ref, None)
            rows = pl.ds(c * CHUNK, CHUNK)
            pre, sg, xc, dtr, dtv, A, E, dtE = (q[k] for k in ("pre", "sg", "xc", "dtr", "dtv", "A", "E", "dtE"))
            xs = xc[:, 0:D]
            total = E[CHUNK - 1:CHUNK, :]
            x_dt = xs * dtE
            eE = jnp.exp(E)
            wdec = jnp.exp(total - E)
            zb = zb_ref[rows, :]
            yv = y_ref[rows, :]
            sgz = _sigmoid(zb)
            sz = zb * sgz
            hh = yv * sz
            for g in range(SSD_GROUPS):
                gc = slice(g * GROUP_W, (g + 1) * GROUP_W)
                hg = hh[:, gc]
                r = lax.rsqrt(jnp.mean(hg * hg, axis=-1, keepdims=True) + EPS)
                dyb_g = dyb_ref[rows, gc]
                dn = dyb_g * nw_ref[:, gc]
                dnw_ref[0:1, gc] += jnp.sum(dyb_g * hg * r, axis=0, keepdims=True)
                dy_s[:, gc] = r * dn - hg * (r * r * r) * jnp.mean(dn * hg, axis=-1, keepdims=True)
            dhh = dy_s[...]
            dp_ref[rows, 0:D] = (dhh * yv * (sgz * (1.0 + zb * (1.0 - sgz)))).astype(BF16)
            dy = dhh * sz
            dy_s[...] = dy
            dDacc_ref[0:1, :] += jnp.sum(dy * xs, axis=0, keepdims=True)
            dxc_s[:, 0:D] = dy * dexp_ref[...]
            for g in range(SSD_GROUPS):
                gc = slice(g * GROUP_W, (g + 1) * GROUP_W)
                bcol = slice(D + g * STATE, D + (g + 1) * STATE)
                ccol = slice(D + SSD_GROUPS * STATE + g * STATE, D + SSD_GROUPS * STATE + (g + 1) * STATE)
                Bg = xc[:, bcol].astype(BF16)
                Cg = xc[:, ccol].astype(BF16)
                B2 = jnp.concatenate([Bg, Bg], axis=0)
                cb2 = _dot_nt(Cg, B2)
                htg = st_ref[c, :, gc]
                htb = htg.astype(BF16)
                dhn = dht_ref[:, gc]
                dhnb = dhn.astype(BF16)
                dyg = dy[:, gc]
                eEg = eE[:, gc]
                wg = wdec[:, gc]
                xdg = x_dt[:, gc]
                CH = _dot(Cg, htb)
                dCHb = (dyg * eEg).astype(BF16)
                dC = _dot_nt(dCHb, htb)
                dl = jnp.exp(total[:, gc])
                dht_prev = _dot_tn(Cg, dCHb) + dl * dhn
                dtot = jnp.sum(dhn * htg, axis=0, keepdims=True) * dl
                dxw = _dot(Bg, dhnb)
                dB = _dot_nt((xdg * wg).astype(BF16), dhnb)
                dwd = dxw * xdg * wg
                dtot = dtot + jnp.sum(dwd, axis=0, keepdims=True)
                dE_s[:, gc] = dyg * eEg * CH - dwd + last_row * dtot
                dxdt_s[:, gc] = dxw * wg
                dcb2 = jnp.zeros((CHUNK, LANE), F32)
                for jj in range(GROUP_W // LANE):
                    pc = slice(g * GROUP_W + jj * LANE, g * GROUP_W + (jj + 1) * LANE)
                    Ej = E[:, pc]
                    e2 = jnp.sum(Ej * diag, axis=0, keepdims=True)
                    Lp = jnp.exp(jnp.where(causal, Ej - e2, -1e30))
                    Mp = cb2 * Lp
                    xj = x_dt[:, pc]
                    xbd = jnp.concatenate([xj * lo, xj * hi], axis=0).astype(BF16)
                    dyj = dy[:, pc].astype(BF16)
                    dMp = _dot_nt(dyj, xbd)
                    dxbd = _dot_tn(Mp.astype(BF16), dyj)
                    dxdt_s[:, pc] += dxbd[0:CHUNK, :] * lo + dxbd[CHUNK:2 * CHUNK, :] * hi
                    dcb2 = dcb2 + dMp * Lp
                    dseg = dMp * Mp
                    dE_s[:, pc] += dseg - diag * jnp.sum(dseg, axis=0, keepdims=True)
                dcb2b = dcb2.astype(BF16)
                dC = dC + _dot(dcb2b, B2)
                dB2 = _dot_tn(dcb2b, Cg)
                dB = dB + dB2[0:CHUNK, :] + dB2[CHUNK:2 * CHUNK, :]
                dxc_s[:, bcol] = dB
                dxc_s[:, ccol] = dC
                dht_ref[:, gc] = dht_prev
            dx_dt = dxdt_s[...]
            dxc_s[:, 0:D] += dx_dt * dtE
            red = _dot(jnp.concatenate([dE_s[...], dx_dt * xs], axis=0), expT_ref[...], precision=HI)
            da = _dot(triT_ref[...], red[0:CHUNK, :], precision=HI)
            ddtv = red[CHUNK:2 * CHUNK, :] + da * A
            dAacc_ref[0:1, :] += jnp.sum(da * dtv, axis=0, keepdims=True)
            ddtr = ddtv * _sigmoid(dtr)
            ddtb_ref[0:1, :] += jnp.sum(ddtr, axis=0, keepdims=True)
            dp_ref[rows, D + XBC_W:D + XBC_W + DT_W] = ddtr.astype(BF16)
            dpre = dxc_s[...] * (sg * (1.0 + pre * (1.0 - sg)))
            dpre_ref[rows, :] = dpre
            dcb_ref[0:1, :] += jnp.sum(dpre, axis=0, keepdims=True)
        dpre_t = dpre_ref[0:T, :]
        dxbc = jnp.zeros((T, XBC_W), F32)
        for k in range(CONV_K):
            dcw_ref[k:k + 1, :] += jnp.sum(dpre_t * ext_ref[pl.ds(HALO - (CONV_K - 1) + k, T), :], axis=0, keepdims=True)
            dxbc = dxbc + cw_ref[k:k + 1, :] * dpre_ref[pl.ds(CONV_K - 1 - k, T), :]
        dp_ref[:, D:D + XBC_W] = dxbc.astype(BF16)
        dpre_ref[T:T + HALO, :] = dpre_ref[0:HALO, :]

        @pl.when(i == nsteps - 1)
        def _():
            dalog_ref[...] = dAacc_ref[...] * (-jnp.exp(alog_ref[...]))
            dD_ref[...] = _dot(dDacc_ref[...], expT_ref[...], precision=HI)

    full = lambda a: pl.BlockSpec(a.shape, lambda i: (0,) * a.ndim)
    hb = T // HALO
    rev = lambda i: nsteps - 1 - i
    acc = lambda w: pl.BlockSpec((8, w), lambda i: (0, 0))
    return pl.pallas_call(
        body, name="ssd_bwd", grid=(nsteps,),
        in_specs=[pl.BlockSpec((T, D), lambda i: (rev(i), OFF_ZB // D)),
                  pl.BlockSpec((T, XBC_W), lambda i: (rev(i), OFF_XBC // XBC_W)),
                  pl.BlockSpec((HALO, XBC_W), lambda i: (jnp.maximum(rev(i) * hb - 1, 0), OFF_XBC // XBC_W)),
                  pl.BlockSpec((T, DT_W), lambda i: (rev(i), OFF_DT // DT_W)),
                  pl.BlockSpec((T, D), lambda i: (rev(i), 0)), pl.BlockSpec((T, D), lambda i: (rev(i), 0)),
                  pl.BlockSpec((ncl, STATE, D), lambda i: (rev(i), 0, 0)),
                  full(conv_w), full(conv_b), full(dtb_p), full(alog_p), full(d_exp), full(norm_w),
                  full(tri), full(triT), full(expand), full(expandT)],
        out_specs=[pl.BlockSpec((T, SSD_W), lambda i: (rev(i), 0)),
                   acc(XBC_W), acc(XBC_W), acc(DT_W), acc(DT_W), acc(DT_W), acc(D)],
        out_shape=[jax.ShapeDtypeStruct((S, SSD_W), BF16),
                   jax.ShapeDtypeStruct((8, XBC_W), F32), jax.ShapeDtypeStruct((8, XBC_W), F32),
                   jax.ShapeDtypeStruct((8, DT_W), F32), jax.ShapeDtypeStruct((8, DT_W), F32),
                   jax.ShapeDtypeStruct((8, DT_W), F32), jax.ShapeDtypeStruct((8, D), F32)],
        scratch_shapes=[pltpu.VMEM((STATE, D), F32), pltpu.VMEM((HALO + T, XBC_W), F32), pltpu.VMEM((T + HALO, XBC_W), F32),
                        pltpu.VMEM((CHUNK, D), F32), pltpu.VMEM((CHUNK, D), F32), pltpu.VMEM((CHUNK, D), F32),
                        pltpu.VMEM((CHUNK, XBC_W), F32), pltpu.VMEM((8, D), F32), pltpu.VMEM((8, DT_W), F32)],
        compiler_params=_cp(("arbitrary",)),
    )(proj, proj, proj, proj, dyb, y, states, conv_w, conv_b, dtb_p, alog_p, d_exp, norm_w, tri, triT, expand, expandT)


def _head(x, ya, yb, proj, target, gate_b, wout, fw, *, tm):
    S = x.shape[0]

    def body(x_ref, ya_ref, yb_ref, gl0_ref, gl1_ref, t_ref, gb_ref, w_ref, fw_ref,
             dh_ref, dhb_ref, mb_ref, dya_ref, dyb_ref, dgl_ref, loss_ref, dfw_ref, dgb_ref):
        @pl.when(pl.program_id(0) == 0)
        def _():
            loss_ref[...] = jnp.zeros_like(loss_ref)
            dfw_ref[...] = jnp.zeros_like(dfw_ref)
            dgb_ref[...] = jnp.zeros_like(dgb_ref)

        ya_v = ya_ref[...]
        yb_v = yb_ref[...]
        g0 = _sigmoid(gl0_ref[...] + gb_ref[:, 0:D])
        g1 = _sigmoid(gl1_ref[...] + gb_ref[:, D:2 * D])
        mb = (g0 * ya_v + g1 * yb_v).astype(BF16)
        mb_ref[...] = mb
        h = x_ref[...] + _dot(mb, w_ref[...])
        r = lax.rsqrt(jnp.mean(h * h, axis=-1, keepdims=True) + EPS)
        hn = h * r
        err = hn * fw_ref[...] - t_ref[...]
        loss_ref[...] += 0.5 * jnp.sum(jnp.mean(err * err, axis=-1, keepdims=True))
        dyf = err * (1.0 / D)
        dfw_ref[0:1, :] += jnp.sum(dyf * hn, axis=0, keepdims=True)
        dhn = dyf * fw_ref[...]
        dh = r * (dhn - hn * jnp.mean(dhn * hn, axis=-1, keepdims=True))
        dh_ref[...] = dh
        dhb = dh.astype(BF16)
        dhb_ref[...] = dhb
        dm = _dot_nt(dhb, w_ref[...])
        dya_ref[...] = dm * g0
        dyb_ref[...] = dm * g1
        dgl0 = dm * ya_v * g0 * (1.0 - g0)
        dgl1 = dm * yb_v * g1 * (1.0 - g1)
        dgl_ref[:, 0:D] = dgl0.astype(BF16)
        dgl_ref[:, D:2 * D] = dgl1.astype(BF16)
        dgb_ref[0:1, 0:D] += jnp.sum(dgl0, axis=0, keepdims=True)
        dgb_ref[0:1, D:2 * D] += jnp.sum(dgl1, axis=0, keepdims=True)

    row = pl.BlockSpec((tm, D), lambda i: (i, 0))
    seg = lambda off: pl.BlockSpec((tm, D), lambda i: (i, off // D))
    full = lambda a: pl.BlockSpec(a.shape, lambda i: (0,) * a.ndim)
    acc = lambda w: pl.BlockSpec((8, w), lambda i: (0, 0))
    return pl.pallas_call(
        body, name="head", grid=(S // tm,),
        in_specs=[row, row, row, seg(OFF_G0), seg(OFF_G1), row, full(gate_b), full(wout), full(fw)],
        out_specs=[row, row, row, row, row, pl.BlockSpec((tm, 2 * D), lambda i: (i, 0)), acc(LANE), acc(D), acc(2 * D)],
        out_shape=[jax.ShapeDtypeStruct((S, D), F32), jax.ShapeDtypeStruct((S, D), BF16), jax.ShapeDtypeStruct((S, D), BF16),
                   jax.ShapeDtypeStruct((S, D), F32), jax.ShapeDtypeStruct((S, D), F32), jax.ShapeDtypeStruct((S, 2 * D), BF16),
                   jax.ShapeDtypeStruct((8, LANE), F32), jax.ShapeDtypeStruct((8, D), F32), jax.ShapeDtypeStruct((8, 2 * D), F32)],
        compiler_params=_cp(("arbitrary",)),
    )(x, ya, yb, proj, proj, target, gate_b, wout, fw)


def _adam_update(g, w_ref, m_ref, v_ref, g_ref, d_ref, m2_ref, v2_ref):
    m2 = ADAM_B1 * m_ref[...] + (1.0 - ADAM_B1) * g
    v2 = ADAM_B2 * v_ref[...] + (1.0 - ADAM_B2) * (g * g)
    m_hat = m2 / (1.0 - ADAM_B1 ** ADAM_STEP)
    v_hat = v2 / (1.0 - ADAM_B2 ** ADAM_STEP)
    g_ref[...] = g
    d_ref[...] = -ADAM_LR * (m_hat / (jnp.sqrt(v_hat) + ADAM_EPS) + ADAM_WD * w_ref[...])
    m2_ref[...] = m2
    v2_ref[...] = v2


def _adamw_own(me, own, landed, w, m, v, *, tr, name):
    _, R, C = landed.shape
    assert R % tr == 0, (name, R, tr)

    def body(me_ref, own_ref, p_ref, w_ref, m_ref, v_ref, g_ref, d_ref, m2_ref, v2_ref):
        mine = own_ref[0].astype(F32)
        g = jnp.where(me_ref[0] == 0, mine, p_ref[0].astype(F32))
        for k in range(1, N_DEV):
            g = g + jnp.where(me_ref[0] == k, mine, p_ref[k].astype(F32))
        _adam_update(g, w_ref, m_ref, v_ref, g_ref, d_ref, m2_ref, v2_ref)

    row = pl.BlockSpec((tr, C), lambda i, me_ref: (i, 0))
    return pl.pallas_call(
        body, name=name,
        grid_spec=pltpu.PrefetchScalarGridSpec(
            num_scalar_prefetch=1, grid=(R // tr,),
            in_specs=[pl.BlockSpec((1, tr, C), lambda i, me_ref: (me_ref[0], i, 0)),
                      pl.BlockSpec((N_DEV, tr, C), lambda i, me_ref: (0, i, 0)), row, row, row],
            out_specs=[row, row, row, row]),
        out_shape=[jax.ShapeDtypeStruct((R, C), F32)] * 4,
        compiler_params=_cp(("parallel",)),
    )(me, own, landed, w, m, v)


def _adamw(parts, w, m, v, *, tr, name):
    _, R, C = parts.shape
    assert R % tr == 0, (name, R, tr)

    def body(p_ref, w_ref, m_ref, v_ref, g_ref, d_ref, m2_ref, v2_ref):
        g = p_ref[0].astype(F32)
        for k in range(1, N_DEV):
            g = g + p_ref[k].astype(F32)
        _adam_update(g, w_ref, m_ref, v_ref, g_ref, d_ref, m2_ref, v2_ref)

    row = pl.BlockSpec((tr, C), lambda i: (i, 0))
    return pl.pallas_call(
        body, name=name, grid=(R // tr,),
        in_specs=[pl.BlockSpec((N_DEV, tr, C), lambda i: (0, i, 0)), row, row, row],
        out_specs=[row, row, row, row],
        out_shape=[jax.ShapeDtypeStruct((R, C), F32)] * 4,
        compiler_params=_cp(("parallel",)),
    )(parts, w, m, v)


def _place():
    x, y, c = lax.axis_index("x"), lax.axis_index("y"), lax.axis_index("c")
    return x, y, c


def _all_gather(arrs, *, name):
    n = len(arrs)

    def body(*refs):
        ins, outs = refs[:n], refs[n:2 * n]
        send_sems, recv_sems, local_sems = refs[2 * n:]
        x, y, c = _place()
        me, sibling = (x, y, c), (x, y, 1 - c)
        chips = [(1 - x, y), (x, 1 - y), (1 - x, 1 - y)]

        def idx(px, py, pc):
            return 4 * px + 2 * py + pc

        def copy(k, a, block, to, src=None):
            slab = outs[a].at[idx(*block)]
            return pltpu.make_async_remote_copy(
                src_ref=slab if src is None else src, dst_ref=slab,
                send_sem=send_sems.at[k, a], recv_sem=recv_sems.at[k, a], device_id=to, device_id_type=MESH)

        mine = [pltpu.make_async_copy(ins[a], outs[a].at[idx(*me)], local_sems.at[a]) for a in range(n)]
        for cp in mine:
            cp.start()
        first = []
        for a in range(n):
            first.append(copy(0, a, me, sibling, src=ins[a]))
            first += [copy(1 + j, a, me, (*chip, c), src=ins[a]) for j, chip in enumerate(chips)]
        for cp in first:
            cp.start()
        passed = []
        for j, chip in enumerate(chips):
            for a in range(n):
                copy(1 + j, a, (*chip, c), me).wait_recv()
                fwd = copy(4 + j, a, (*chip, c), sibling)
                fwd.start()
                passed.append(fwd)
        for a in range(n):
            copy(0, a, sibling, me).wait_recv()
            for j, chip in enumerate(chips):
                copy(4 + j, a, (*chip, 1 - c), me).wait_recv()
        for cp in first + passed:
            cp.wait_send()
        for cp in mine:
            cp.wait()

    anyspec = pl.BlockSpec(memory_space=pl.ANY)
    return pl.pallas_call(
        body, name=name,
        in_specs=[anyspec] * n, out_specs=[anyspec] * n,
        out_shape=[jax.ShapeDtypeStruct((N_DEV,) + a.shape, a.dtype) for a in arrs],
        scratch_shapes=[pltpu.SemaphoreType.DMA((7, n)), pltpu.SemaphoreType.DMA((7, n)), pltpu.SemaphoreType.DMA((n,))],
    )(*arrs)


_REL = [(dx, dy, dc) for dx in (0, 1) for dy in (0, 1) for dc in (0, 1)][1:]
_HBM = pl.BlockSpec(memory_space=pltpu.HBM)
_SEM = pl.BlockSpec(memory_space=pltpu.SEMAPHORE)
_EFFECT = pltpu.SideEffectType.DATAFLOW_SIDE_EFFECTING


def _peer(k):
    x, y, c = _place()
    dx, dy, dc = _REL[k]
    return (1 - x if dx else x, 1 - y if dy else y, 1 - c if dc else c)


def _exchange_start(parts, *, name):
    n = len(parts)

    def body(*refs):
        ins, lands = refs[:n], refs[n:2 * n]
        send_sems, recv_sems, token = refs[2 * n], refs[2 * n + 1], refs[-1]
        x, y, c = _place()
        me = 4 * x + 2 * y + c
        for a in range(n):
            for k in range(len(_REL)):
                px, py, pc = _peer(k)
                pltpu.make_async_remote_copy(
                    src_ref=ins[a].at[4 * px + 2 * py + pc], dst_ref=lands[a].at[me],
                    send_sem=send_sems.at[len(_REL) * a + k], recv_sem=recv_sems.at[len(_REL) * a + k],
                    device_id=(px, py, pc), device_id_type=MESH).start()
        token[...] = jnp.zeros_like(token)

    sem = pltpu.SemaphoreType.DMA((len(_REL) * n,))
    bufs = [pltpu.HBM(p.shape, p.dtype) for p in parts]
    outs = pl.pallas_call(
        body, name=name,
        out_shape=(sem, sem, *bufs, *bufs, jax.ShapeDtypeStruct((8, LANE), F32)),
        in_specs=(_HBM,) * (2 * n), out_specs=(_SEM, _SEM, *(_HBM,) * (2 * n), pl.BlockSpec(memory_space=pltpu.VMEM)),
        input_output_aliases={i: 2 + i for i in range(2 * n)},
        compiler_params=pltpu.CompilerParams(has_side_effects=_EFFECT),
    )(*[pltpu.with_memory_space_constraint(p, pltpu.HBM) for p in parts],
      *[pltpu.with_memory_space_constraint(lax.empty(p.shape, p.dtype), pltpu.HBM) for p in parts])
    return outs[0], outs[1], outs[2:2 + n], outs[2 + n:2 + 2 * n], outs[-1]


def _exchange_wait(send_sems, recv_sems, parts, lands, after, *, name):
    n = len(parts)

    def body(*refs):
        ins, lands_ = refs[:n], refs[n:2 * n]
        ssem, rsem = refs[2 * n], refs[2 * n + 1]
        for a in range(n):
            for k in range(len(_REL)):
                px, py, pc = _peer(k)
                p = 4 * px + 2 * py + pc
                cp = pltpu.make_async_remote_copy(
                    src_ref=ins[a].at[p], dst_ref=lands_[a].at[p],
                    send_sem=ssem.at[len(_REL) * a + k], recv_sem=rsem.at[len(_REL) * a + k],
                    device_id=(px, py, pc), device_id_type=MESH)
                cp.wait_send()
                cp.wait_recv()

    bufs = [pltpu.HBM(p.shape, p.dtype) for p in parts]
    outs = pl.pallas_call(
        body, name=name, out_shape=(*bufs, *bufs),
        in_specs=(*(_HBM,) * (2 * n), _SEM, _SEM, pl.BlockSpec(memory_space=pl.ANY)), out_specs=(_HBM,) * (2 * n),
        input_output_aliases={i: i for i in range(2 * n)},
        compiler_params=pltpu.CompilerParams(has_side_effects=_EFFECT),
    )(*parts, *lands, send_sems, recv_sems, after)
    return outs[:n], outs[n:]


WEIGHTS = ('norm_w', 'w_in', 'gate_b', 'sgu_norm_g', 'sgu_norm_b', 'sgu_w', 'sgu_b', 'conv_w', 'conv_b', 'dt_bias', 'A_log',
           'D_skip', 'ssd_norm_w', 'w_out', 'final_norm_w')
SHARDED = ('w_in', 'conv_w', 'w_out')
PACK_ROW = 8 * LANE


def _constants():
    tri = np.tril(np.ones((CHUNK, CHUNK), np.float32))
    expand = np.zeros((DT_W, D), np.float32)
    for h in range(HEADS):
        expand[h, h * HEADDIM:(h + 1) * HEADDIM] = 1.0
    sel = np.zeros((D, LANE), np.float32)
    for g in range(SGU_GROUPS):
        sel[g * LANE:(g + 1) * LANE, g] = 1.0
    pos_chunk = np.arange(SGU_BLOCK) // CHUNK
    mask = (pos_chunk[None, :] <= pos_chunk[:, None]).astype(np.float32)
    return dict(tri=jnp.asarray(tri), triT=jnp.asarray(tri.T.copy()), expand=jnp.asarray(expand),
                expandT=jnp.asarray(expand.T.copy()), sel=jnp.asarray(sel), mask=jnp.asarray(mask))


def _permute_w(w):
    return jnp.concatenate([w[:, :6144], w[:, 11296:], w[:, 6144:11296], jnp.zeros((D, DT_W - HEADS), w.dtype)], axis=1)


def _local_step(x2, tgt, wp, wout, cw, p, exchange=None):
    S = x2.shape[0]
    k = _constants()
    xn = _norm_fwd(x2, p['norm_w'], tm=min(512, S))
    proj = _matmul(xn, wp, tm=min(1024, S), tn=1408, tk=D, name="in_proj")
    wm32 = p['sgu_w'][0] * k['mask']
    wm = wm32.astype(BF16)
    wmT = jnp.swapaxes(wm32, 1, 2).astype(BF16)
    bias_full = jnp.repeat(p['sgu_b'][0].T, LANE, axis=1)
    tm_sgu = min(256, S)
    ya = _sgu_fwd(proj, p['sgu_norm_g'], p['sgu_norm_b'], wm, bias_full, tm=tm_sgu)
    pad32 = lambda a: jnp.pad(a, ((0, 0), (0, DT_W - HEADS)))
    dtb_p, alog_p = pad32(p['dt_bias']), pad32(p['A_log'])
    d_exp = jnp.repeat(p['D_skip'], HEADDIM, axis=1)
    ssd_args = (cw, p['conv_b'], dtb_p, alog_p, d_exp, p['ssd_norm_w'])
    y, yb, states = _ssd_fwd(proj, *ssd_args, k['tri'], k['expand'])
    dh, dhb, mb, dya, dyb, dgl, loss, dfw, dgb = _head(
        x2, ya, yb, proj, tgt, p['gate_b'], wout, p['final_norm_w'][None, :], tm=min(128, S))
    dsgu, dws, dbsT, dsg, dsb = _sgu_bwd(proj, dya, p['sgu_norm_g'], p['sgu_norm_b'], wm, wmT, bias_full, k['mask'], k['sel'],
                                         tm=tm_sgu)
    dssd, dcw, dcb, ddtb, dalog, dD, dnw = _ssd_bwd(proj, dyb, y, states, *ssd_args, k['tri'], k['triT'], k['expand'], k['expandT'])
    xnT = xn.T
    tk = min(2048, S)
    dw_sgu = _matmul(xnT, dsgu, tm=1024, tn=1024, tk=tk, name="dw_in_sgu")
    dw_gate = _matmul(xnT, dgl, tm=1024, tn=1024, tk=tk, name="dw_in_gate")
    dw_ssd = _matmul(xnT, dssd, tm=512, tn=SEG_SSD[1], tk=min(512, S), name="dw_in_ssd")
    dw_in = jnp.concatenate([dw_sgu, dw_ssd[:, :W_IN - SEG_SSD[0]], dw_gate], axis=1)
    dw_out = _matmul(mb.T, dhb, tm=1024, tn=1024, tk=tk, name="dw_out")
    token = None if exchange is None else exchange(dw_in, dw_out)
    tm = min(1024, S)
    dxn = _matmul(dsgu, wp[:, :SEG_GATE[0]], trans_b=True, tm=tm, tn=1024, tk=2048, after=token, name="dxn_sgu")
    dxn = _matmul(dgl, wp[:, SEG_GATE[0]:SEG_SSD[0]], trans_b=True, tm=tm, tn=1024, tk=2048, add=dxn, name="dxn_gate")
    dxn = _matmul(dssd, wp[:, SEG_SSD[0]:], trans_b=True, tm=tm, tn=512, tk=SEG_SSD[1], add=dxn, name="dxn_ssd")
    grad_x, dnorm = _norm_bwd(x2, p['norm_w'], dxn, dh, tm=min(256, S))
    grads = dict(
        norm_w=dnorm[0:1], w_in=dw_in[None], gate_b=dgb[0:1], sgu_norm_g=dsg[0:1], sgu_norm_b=dsb[0:1], sgu_w=dws[None],
        sgu_b=dbsT[:, :SGU_GROUPS].T[None], conv_w=dcw[0:CONV_K][None], conv_b=dcb[0:1], dt_bias=ddtb[0:1, :HEADS],
        A_log=dalog[0:1, :HEADS], D_skip=dD[0:1, :HEADS], ssd_norm_w=dnw[0:1], w_out=dw_out[None], final_norm_w=dfw[0])
    return loss[0, 0], grad_x, grads


def _pack(arrs):
    rows, offs, r = [], [], 0
    for a in arrs:
        n = a.size
        nr = -(-n // PACK_ROW) * 8
        rows.append(jnp.pad(a.reshape(-1).astype(F32), (0, nr * LANE - n)).reshape(nr, LANE))
        offs.append(r)
        r += nr
    return jnp.concatenate(rows, axis=0), offs


def kernel(x, norm_w, w_in, gate_b, sgu_norm_g, sgu_norm_b, sgu_w, sgu_b, conv_w, conv_b, dt_bias, A_log, D_skip, ssd_norm_w, w_out, final_norm_w, loss_target, m_norm_w, m_w_in, m_gate_b, m_sgu_norm_g, m_sgu_norm_b, m_sgu_w, m_sgu_b, m_conv_w, m_conv_b, m_dt_bias, m_A_log, m_D_skip, m_ssd_norm_w, m_w_out, m_final_norm_w, v_norm_w, v_w_in, v_gate_b, v_sgu_norm_g, v_sgu_norm_b, v_sgu_w, v_sgu_b, v_conv_w, v_conv_b, v_dt_bias, v_A_log, v_D_skip, v_ssd_norm_w, v_w_out, v_final_norm_w):
    w = dict(norm_w=norm_w, w_in=w_in, gate_b=gate_b, sgu_norm_g=sgu_norm_g, sgu_norm_b=sgu_norm_b, sgu_w=sgu_w, sgu_b=sgu_b,
             conv_w=conv_w, conv_b=conv_b, dt_bias=dt_bias, A_log=A_log, D_skip=D_skip, ssd_norm_w=ssd_norm_w, w_out=w_out,
             final_norm_w=final_norm_w)
    m = dict(norm_w=m_norm_w, w_in=m_w_in, gate_b=m_gate_b, sgu_norm_g=m_sgu_norm_g, sgu_norm_b=m_sgu_norm_b, sgu_w=m_sgu_w,
             sgu_b=m_sgu_b, conv_w=m_conv_w, conv_b=m_conv_b, dt_bias=m_dt_bias, A_log=m_A_log, D_skip=m_D_skip,
             ssd_norm_w=m_ssd_norm_w, w_out=m_w_out, final_norm_w=m_final_norm_w)
    v = dict(norm_w=v_norm_w, w_in=v_w_in, gate_b=v_gate_b, sgu_norm_g=v_sgu_norm_g, sgu_norm_b=v_sgu_norm_b, sgu_w=v_sgu_w,
             sgu_b=v_sgu_b, conv_w=v_conv_w, conv_b=v_conv_b, dt_bias=v_dt_bias, A_log=v_A_log, D_skip=v_D_skip,
             ssd_norm_w=v_ssd_norm_w, w_out=v_w_out, final_norm_w=v_final_norm_w)
    me = 4 * lax.axis_index("x") + 2 * lax.axis_index("y") + lax.axis_index("c")
    shard_cw = XBC_W // N_DEV

    g_in, g_out, g_cw = _all_gather([w_in[0].astype(BF16), w_out[0].astype(BF16), conv_w[0]], name="gather_weights")
    wp = _permute_w(jnp.swapaxes(g_in, 0, 1).reshape(D, W_IN))
    wout_full = g_out.reshape(D, D)
    cw_full = jnp.swapaxes(g_cw, 0, 1).reshape(CONV_K, XBC_W)

    flight = {}

    def exchange(dw_in, dw_out):
        part_in = jnp.swapaxes(dw_in.astype(BF16).reshape(D, N_DEV, SHARD_IN), 0, 1)
        part_out = dw_out.astype(BF16).reshape(N_DEV, D // N_DEV, D)
        flight['sems'], flight['rsems'], flight['parts'], flight['lands'], token = _exchange_start(
            [part_in, part_out], name="exchange_start")
        return token

    loss_part, grad_x, grads = _local_step(x[0], loss_target[0], wp, wout_full, cw_full, w, exchange)
    (own_in, own_out), (land_in, land_out) = _exchange_wait(
        flight['sems'], flight['rsems'], flight['parts'], flight['lands'], grad_x, name="exchange_wait")
    me_arr = jnp.reshape(me, (1,)).astype(jnp.int32)
    res = {}
    res['w_in'] = _adamw_own(me_arr, own_in, land_in, w_in[0], m_w_in[0], v_w_in[0], tr=128, name="adamw_w_in")
    res['w_out'] = _adamw_own(me_arr, own_out, land_out, w_out[0], m_w_out[0], v_w_out[0], tr=128, name="adamw_w_out")

    small = [n for n in WEIGHTS if n not in SHARDED]
    packed, offs = _pack([grads[n] for n in small] + [loss_part, grads['conv_w']])
    (gathered,) = _all_gather([packed], name="gather_small")
    off_loss, off_cw = offs[-2], offs[-1]
    cw_parts = gathered[:, off_cw:, :].reshape(N_DEV, CONV_K, XBC_W)
    cw_parts = lax.dynamic_slice_in_dim(cw_parts, me * shard_cw, shard_cw, axis=2)
    cw_rows = _pack([cw_parts[0]])[0].shape[0]
    cw_parts = jnp.pad(cw_parts.reshape(N_DEV, -1), ((0, 0), (0, cw_rows * LANE - CONV_K * shard_cw))).reshape(N_DEV, cw_rows, LANE)
    parts = jnp.concatenate([gathered[:, :off_cw, :], cw_parts], axis=1)
    zero = jnp.zeros((), F32)
    packs = [_pack([d[n] for n in small] + [zero, d['conv_w']])[0] for d in (w, m, v)]
    outs = _adamw(parts, *packs, tr=parts.shape[1], name="adamw_small")

    def unpack(o, name):
        if name == 'conv_w':
            return o[off_cw:off_cw + cw_rows].reshape(-1)[:CONV_K * shard_cw].reshape(w['conv_w'].shape)
        r0 = offs[small.index(name)]
        n = w[name].size
        return o[r0:r0 + -(-n // PACK_ROW) * 8].reshape(-1)[:n].reshape(w[name].shape)

    for n in small + ['conv_w']:
        res[n] = [unpack(o, n) for o in outs]
    for n in ('w_in', 'w_out'):
        res[n] = [o[None] for o in res[n]]
    loss = outs[0][off_loss, 0]
    return (loss, grad_x[None], *[res[n][0] for n in WEIGHTS], *[res[n][1] for n in WEIGHTS],
            *[res[n][2] for n in WEIGHTS], *[res[n][3] for n in WEIGHTS])
```

```python
import functools

import numpy as np
import jax
import jax.numpy as jnp
from jax import lax
from jax.experimental import pallas as pl
from jax.experimental.pallas import tpu as pltpu

F32 = jnp.float32
BF16 = jnp.bfloat16
HI = lax.Precision.HIGHEST
MESH = pl.DeviceIdType.MESH

D = 2048
EPS = 1e-5
SGU_BLOCK = 128
SGU_GROUPS = 16
CHUNK = 64
HEADS = 32
HEADDIM = 64
SSD_GROUPS = 4
GROUP_W = D // SSD_GROUPS
STATE = 128
CONV_K = 4
XBC_W = D + 2 * SSD_GROUPS * STATE
W_IN = 15392
N_DEV = 8
SHARD_IN = W_IN // N_DEV
ADAM_LR, ADAM_B1, ADAM_B2, ADAM_EPS, ADAM_WD, ADAM_STEP = 0.001, 0.9, 0.999, 1e-08, 0.01, 10

LANE = 128
DT_W = LANE
OFF_U, OFF_V, OFF_ZA, OFF_G0, OFF_G1, OFF_ZB, OFF_XBC, OFF_DT = 0, 2048, 4096, 6144, 8192, 10240, 12288, 15360
WP = OFF_DT + DT_W
SEG_SGU = (0, 6144)
SEG_GATE = (6144, 4096)
SEG_SSD = (10240, WP - 10240)
SSD_PAD_W = 6144
VMEM_LIMIT = 56 * 1024 * 1024


def _cp(sem=None, vmem=VMEM_LIMIT):
    return pltpu.CompilerParams(dimension_semantics=sem, vmem_limit_bytes=vmem)


def _sigmoid(x):
    return 1.0 / (1.0 + jnp.exp(-x))


def _softplus(x):
    return jnp.maximum(x, 0.0) + jnp.log(1.0 + jnp.exp(-jnp.abs(x)))


def _dot(a, b, precision=None):
    return jnp.dot(a, b, preferred_element_type=F32, precision=precision)


def _dot_nt(a, b, precision=None):
    return lax.dot_general(a, b, (((1,), (1,)), ((), ())), preferred_element_type=F32, precision=precision)


def _dot_tn(a, b, precision=None):
    return lax.dot_general(a, b, (((0,), (0,)), ((), ())), preferred_element_type=F32, precision=precision)


def _matmul(a, b, *, trans_a=False, trans_b=False, b_koff=0, out_dtype=F32, tm, tn, tk, add=None, after=None, name):
    K, M = a.shape if trans_a else a.shape[::-1]
    N = b.shape[0] if trans_b else b.shape[1]
    assert M % tm == 0 and N % tn == 0 and K % tk == 0 and not (trans_a and trans_b), (name, M, N, K, tm, tn, tk)
    nk = K // tk

    def body(*refs):
        a_ref, b_ref = refs[:2]
        add_ref = refs[2] if add is not None else None
        o_ref, acc_ref = refs[-2:]
        k = pl.program_id(2)
        if trans_a:
            part = _dot_tn(a_ref[...], b_ref[...])
        else:
            part = _dot_nt(a_ref[...], b_ref[...]) if trans_b else _dot(a_ref[...], b_ref[...])

        def result(r):
            if add_ref is not None:
                r = r + add_ref[...]
            return r.astype(out_dtype)

        if nk == 1:
            o_ref[...] = result(part)
        else:
            @pl.when(k == 0)
            def _():
                acc_ref[...] = part

            @pl.when(jnp.logical_and(k > 0, k < nk - 1))
            def _():
                acc_ref[...] += part

            @pl.when(k == nk - 1)
            def _():
                o_ref[...] = result(acc_ref[...] + part)

    in_specs = [pl.BlockSpec((tk, tm), lambda i, j, k: (k, i)) if trans_a else pl.BlockSpec((tm, tk), lambda i, j, k: (i, k)),
                pl.BlockSpec((tn, tk), lambda i, j, k: (j, k)) if trans_b else pl.BlockSpec((tk, tn), lambda i, j, k: (k + b_koff, j))]
    args = [a, b]
    if add is not None:
        in_specs.append(pl.BlockSpec((tm, tn), lambda i, j, k: (i, j)))
        args.append(add)
    if after is not None:
        in_specs.append(pl.BlockSpec(memory_space=pl.ANY))
        args.append(after)
    return pl.pallas_call(
        body, name=name, grid=(M // tm, N // tn, nk), in_specs=in_specs,
        out_specs=pl.BlockSpec((tm, tn), lambda i, j, k: (i, j)),
        out_shape=jax.ShapeDtypeStruct((M, N), out_dtype),
        scratch_shapes=[pltpu.VMEM((tm, tn), F32)],
        compiler_params=_cp(("parallel", "parallel", "arbitrary")),
    )(*args)


def _norm_fwd(x, w, *, tm):
    S = x.shape[0]

    def body(x_ref, w_ref, o_ref):
        xv = x_ref[...]
        r = lax.rsqrt(jnp.mean(xv * xv, axis=-1, keepdims=True) + EPS)
        o_ref[...] = (xv * r * w_ref[...]).astype(BF16)

    return pl.pallas_call(
        body, name="norm_fwd", grid=(S // tm,),
        in_specs=[pl.BlockSpec((tm, D), lambda i: (i, 0)), pl.BlockSpec((1, D), lambda i: (0, 0))],
        out_specs=pl.BlockSpec((tm, D), lambda i: (i, 0)),
        out_shape=jax.ShapeDtypeStruct((S, D), BF16), compiler_params=_cp(("parallel",)),
    )(x, w)


def _norm_bwd(x, w, dxn, dh, *, tm):
    S = x.shape[0]

    def body(x_ref, w_ref, dxn_ref, dh_ref, gx_ref, dw_ref):
        xv = x_ref[...]
        r = lax.rsqrt(jnp.mean(xv * xv, axis=-1, keepdims=True) + EPS)
        xh = xv * r
        dxn_v = dxn_ref[...]
        dxh = dxn_v * w_ref[...]
        gx_ref[...] = dh_ref[...] + r * (dxh - xh * jnp.mean(dxh * xh, axis=-1, keepdims=True))

        @pl.when(pl.program_id(0) == 0)
        def _():
            dw_ref[...] = jnp.zeros_like(dw_ref)

        dw_ref[0:1, :] += jnp.sum(dxn_v * xh, axis=0, keepdims=True)

    row = pl.BlockSpec((tm, D), lambda i: (i, 0))
    return pl.pallas_call(
        body, name="norm_bwd", grid=(S // tm,),
        in_specs=[row, pl.BlockSpec((1, D), lambda i: (0, 0)), row, row],
        out_specs=[row, pl.BlockSpec((8, D), lambda i: (0, 0))],
        out_shape=[jax.ShapeDtypeStruct((S, D), F32), jax.ShapeDtypeStruct((8, D), F32)],
        compiler_params=_cp(("arbitrary",)),
    )(x, w, dxn, dh)


def _sgu_core(u_ref, v_ref, z_ref, g_ref, b_ref, wm_ref, bias_ref, vnb_ref, mixed_ref, tm):
    v = v_ref[...]
    mu = jnp.mean(v, axis=-1, keepdims=True)
    vc = v - mu
    rs = lax.rsqrt(jnp.mean(vc * vc, axis=-1, keepdims=True) + EPS)
    vh = vc * rs
    vnb_ref[...] = (vh * g_ref[...] + b_ref[...]).astype(BF16)
    for blk in range(tm // SGU_BLOCK):
        rows = pl.ds(blk * SGU_BLOCK, SGU_BLOCK)
        for gi in range(SGU_GROUPS):
            cols = pl.ds(gi * LANE, LANE)
            mixed_ref[rows, cols] = _dot(wm_ref[gi], vnb_ref[rows, cols]) + bias_ref[:, cols]
    return vh, rs


def _sgu_fwd(proj, g, b, wm, bias_full, *, tm):
    S = proj.shape[0]

    def body(u_ref, v_ref, z_ref, g_ref, b_ref, wm_ref, bias_ref, y_ref, vnb_ref, mixed_ref):
        _sgu_core(u_ref, v_ref, z_ref, g_ref, b_ref, wm_ref, bias_ref, vnb_ref, mixed_ref, tm)
        z = z_ref[...]
        y_ref[...] = u_ref[...] * mixed_ref[...] * (z * _sigmoid(z))

    seg = lambda off: pl.BlockSpec((tm, D), lambda i: (i, off // D))
    full = lambda a: pl.BlockSpec(a.shape, lambda i: (0,) * a.ndim)
    return pl.pallas_call(
        body, name="sgu_fwd", grid=(S // tm,),
        in_specs=[seg(OFF_U), seg(OFF_V), seg(OFF_ZA), full(g), full(b), full(wm), full(bias_full)],
        out_specs=pl.BlockSpec((tm, D), lambda i: (i, 0)),
        out_shape=jax.ShapeDtypeStruct((S, D), F32),
        scratch_shapes=[pltpu.VMEM((tm, D), BF16), pltpu.VMEM((tm, D), F32)],
        compiler_params=_cp(("parallel",)),
    )(proj, proj, proj, g, b, wm, bias_full)


def _sgu_bwd(proj, dy, g, b, wm, wmT, bias_full, mask, sel, *, tm):
    S = proj.shape[0]
    nsteps = S // tm

    def body(u_ref, v_ref, z_ref, dy_ref, g_ref, b_ref, wm_ref, wmT_ref, bias_ref, mask_ref, sel_ref,
             dp_ref, dws_ref, dbs_ref, dg_ref, db_ref, vnb_ref, mixed_ref, dmb_ref, dvn_ref, dbias_ref):
        i = pl.program_id(0)

        @pl.when(i == 0)
        def _():
            dws_ref[...] = jnp.zeros_like(dws_ref)
            dg_ref[...] = jnp.zeros_like(dg_ref)
            db_ref[...] = jnp.zeros_like(db_ref)
            dbias_ref[...] = jnp.zeros_like(dbias_ref)

        vh, rs = _sgu_core(u_ref, v_ref, z_ref, g_ref, b_ref, wm_ref, bias_ref, vnb_ref, mixed_ref, tm)
        u = u_ref[...]
        z = z_ref[...]
        dy_v = dy_ref[...]
        mixed = mixed_ref[...]
        sg = _sigmoid(z)
        sz = z * sg
        dp_ref[:, 0:D] = (dy_v * mixed * sz).astype(BF16)
        dp_ref[:, 2 * D:3 * D] = (dy_v * u * mixed * (sg * (1.0 + z * (1.0 - sg)))).astype(BF16)
        dmixed = dy_v * u * sz
        dmb_ref[...] = dmixed.astype(BF16)
        for blk in range(tm // SGU_BLOCK):
            dbias_ref[...] += dmixed[blk * SGU_BLOCK:(blk + 1) * SGU_BLOCK, :]
        for blk in range(tm // SGU_BLOCK):
            rows = pl.ds(blk * SGU_BLOCK, SGU_BLOCK)
            for gi in range(SGU_GROUPS):
                cols = pl.ds(gi * LANE, LANE)
                dm = dmb_ref[rows, cols]
                dvn_ref[rows, cols] = _dot(wmT_ref[gi], dm)
                dws_ref[gi] += _dot_nt(dm, vnb_ref[rows, cols])
        dvn = dvn_ref[...]
        dg_ref[0:1, :] += jnp.sum(dvn * vh, axis=0, keepdims=True)
        db_ref[0:1, :] += jnp.sum(dvn, axis=0, keepdims=True)
        dvh = dvn * g_ref[...]
        dv = rs * (dvh - jnp.mean(dvh, axis=-1, keepdims=True) - vh * jnp.mean(dvh * vh, axis=-1, keepdims=True))
        dp_ref[:, D:2 * D] = dv.astype(BF16)

        @pl.when(i == nsteps - 1)
        def _():
            for gi in range(SGU_GROUPS):
                dws_ref[gi] = dws_ref[gi] * mask_ref[...]
            dbs_ref[...] = _dot(dbias_ref[...], sel_ref[...], precision=HI)

    seg = lambda off: pl.BlockSpec((tm, D), lambda i: (i, off // D))
    full = lambda a: pl.BlockSpec(a.shape, lambda i: (0,) * a.ndim)
    return pl.pallas_call(
        body, name="sgu_bwd", grid=(nsteps,),
        in_specs=[seg(OFF_U), seg(OFF_V), seg(OFF_ZA), pl.BlockSpec((tm, D), lambda i: (i, 0)),
                  full(g), full(b), full(wm), full(wmT), full(bias_full), full(mask), full(sel)],
        out_specs=[pl.BlockSpec((tm, 3 * D), lambda i: (i, 0)),
                   pl.BlockSpec((SGU_GROUPS, SGU_BLOCK, SGU_BLOCK), lambda i: (0, 0, 0)),
                   pl.BlockSpec((SGU_BLOCK, LANE), lambda i: (0, 0)),
                   pl.BlockSpec((8, D), lambda i: (0, 0)), pl.BlockSpec((8, D), lambda i: (0, 0))],
        out_shape=[jax.ShapeDtypeStruct((S, 3 * D), BF16),
                   jax.ShapeDtypeStruct((SGU_GROUPS, SGU_BLOCK, SGU_BLOCK), F32),
                   jax.ShapeDtypeStruct((SGU_BLOCK, LANE), F32),
                   jax.ShapeDtypeStruct((8, D), F32), jax.ShapeDtypeStruct((8, D), F32)],
        scratch_shapes=[pltpu.VMEM((tm, D), BF16), pltpu.VMEM((tm, D), F32), pltpu.VMEM((tm, D), BF16),
                        pltpu.VMEM((tm, D), F32), pltpu.VMEM((SGU_BLOCK, D), F32)],
        compiler_params=_cp(("arbitrary",)),
    )(proj, proj, proj, dy, g, b, wm, wmT, bias_full, mask, sel)


SSD_T = 2 * CHUNK
HALO = 8


def _pair_masks():
    row = lax.broadcasted_iota(jnp.int32, (CHUNK, LANE), 0)
    lane = lax.broadcasted_iota(jnp.int32, (CHUNK, LANE), 1)
    pos = jnp.where(lane >= CHUNK, lane - CHUNK, lane)
    diag = (row == pos).astype(F32)
    causal = row >= pos
    lo = (lane < CHUNK).astype(F32)
    return diag, causal, lo, 1.0 - lo


def _ssd_chunk_fwd(c, ext_ref, dt_ref, cw_ref, cb_ref, dtb_ref, alog_ref, tri_ref, exp_ref, ht_ref):
    r0 = c * CHUNK
    pre = cb_ref[...] + sum(cw_ref[k:k + 1, :] * ext_ref[pl.ds(r0 + HALO - (CONV_K - 1) + k, CHUNK), :] for k in range(CONV_K))
    sg = _sigmoid(pre)
    xc = pre * sg
    dtr = dt_ref[pl.ds(r0, CHUNK), :] + dtb_ref[...]
    dtv = _softplus(dtr)
    A = -jnp.exp(alog_ref[...])
    acs = _dot(tri_ref[...], dtv * A, precision=HI)
    E = _dot(acs, exp_ref[...], precision=HI)
    dtE = _dot(dtv, exp_ref[...], precision=HI)
    return dict(pre=pre, sg=sg, xc=xc, dtr=dtr, dtv=dtv, A=A, E=E, dtE=dtE)


def _ssd_fwd(proj, conv_w, conv_b, dtb_p, alog_p, d_exp, norm_w, tri, expand):
    S = proj.shape[0]
    T = SSD_T
    nsteps = S // T
    ncl = T // CHUNK

    def body(zb_ref, xbc_ref, halo_ref, dt_ref, cw_ref, cb_ref, dtb_ref, alog_ref, dexp_ref, nw_ref, tri_ref, exp_ref,
             y_ref, yb_ref, st_ref, ht_ref, ext_ref):
        i = pl.program_id(0)

        @pl.when(i == 0)
        def _():
            ht_ref[...] = jnp.zeros_like(ht_ref)
            ext_ref[0:HALO, :] = jnp.zeros((HALO, XBC_W), F32)

        @pl.when(i > 0)
        def _():
            ext_ref[0:HALO, :] = halo_ref[...]

        ext_ref[HALO:HALO + T, :] = xbc_ref[...]
        diag, causal, lo, hi = _pair_masks()
        for c in range(ncl):
            q = _ssd_chunk_fwd(c, ext_ref, dt_ref, cw_ref, cb_ref, dtb_ref, alog_ref, tri_ref, exp_ref, ht_ref)
            rows = pl.ds(c * CHUNK, CHUNK)
            xc, E, dtE = q["xc"], q["E"], q["dtE"]
            xs = xc[:, 0:D]
            total = E[CHUNK - 1:CHUNK, :]
            x_dt = xs * dtE
            eE = jnp.exp(E)
            xw = x_dt * jnp.exp(total - E)
            st_ref[c] = ht_ref[...]
            for g in range(SSD_GROUPS):
                gc = slice(g * GROUP_W, (g + 1) * GROUP_W)
                Bg = xc[:, D + g * STATE:D + (g + 1) * STATE].astype(BF16)
                Cg = xc[:, D + SSD_GROUPS * STATE + g * STATE:D + SSD_GROUPS * STATE + (g + 1) * STATE].astype(BF16)
                cb2 = _dot_nt(Cg, jnp.concatenate([Bg, Bg], axis=0))
                htg = ht_ref[:, gc]
                y_ref[rows, gc] = eE[:, gc] * _dot(Cg, htg.astype(BF16)) + xs[:, gc] * dexp_ref[:, gc]
                for jj in range(GROUP_W // LANE):
                    pc = slice(g * GROUP_W + jj * LANE, g * GROUP_W + (jj + 1) * LANE)
                    Ej = E[:, pc]
                    e2 = jnp.sum(Ej * diag, axis=0, keepdims=True)
                    Mp = cb2 * jnp.exp(jnp.where(causal, Ej - e2, -1e30))
                    xj = x_dt[:, pc]
                    xbd = jnp.concatenate([xj * lo, xj * hi], axis=0).astype(BF16)
                    y_ref[rows, pc] += _dot(Mp.astype(BF16), xbd)
                ht_ref[:, gc] = jnp.exp(total[:, gc]) * htg + _dot_tn(Bg, xw[:, gc].astype(BF16))
            zb = zb_ref[rows, :]
            hh = y_ref[rows, :] * (zb * _sigmoid(zb))
            for g in range(SSD_GROUPS):
                gc = slice(g * GROUP_W, (g + 1) * GROUP_W)
                hg = hh[:, gc]
                r = lax.rsqrt(jnp.mean(hg * hg, axis=-1, keepdims=True) + EPS)
                yb_ref[rows, gc] = hg * r * nw_ref[:, gc]

    full = lambda a: pl.BlockSpec(a.shape, lambda i: (0,) * a.ndim)
    hb = T // HALO
    return pl.pallas_call(
        body, name="ssd_fwd", grid=(nsteps,),
        in_specs=[pl.BlockSpec((T, D), lambda i: (i, OFF_ZB // D)),
                  pl.BlockSpec((T, XBC_W), lambda i: (i, OFF_XBC // XBC_W)),
                  pl.BlockSpec((HALO, XBC_W), lambda i: (jnp.maximum(i * hb - 1, 0), OFF_XBC // XBC_W)),
                  pl.BlockSpec((T, DT_W), lambda i: (i, OFF_DT // DT_W)),
                  full(conv_w), full(conv_b), full(dtb_p), full(alog_p), full(d_exp), full(norm_w), full(tri), full(expand)],
        out_specs=[pl.BlockSpec((T, D), lambda i: (i, 0)), pl.BlockSpec((T, D), lambda i: (i, 0)),
                   pl.BlockSpec((ncl, STATE, D), lambda i: (i, 0, 0))],
        out_shape=[jax.ShapeDtypeStruct((S, D), F32), jax.ShapeDtypeStruct((S, D), F32),
                   jax.ShapeDtypeStruct((S // CHUNK, STATE, D), F32)],
        scratch_shapes=[pltpu.VMEM((STATE, D), F32), pltpu.VMEM((HALO + T, XBC_W), F32)],
        compiler_params=_cp(("arbitrary",)),
    )(proj, proj, proj, proj, conv_w, conv_b, dtb_p, alog_p, d_exp, norm_w, tri, expand)


def _ssd_bwd(proj, dyb, y, states, conv_w, conv_b, dtb_p, alog_p, d_exp, norm_w, tri, triT, expand, expandT):
    S = proj.shape[0]
    T = SSD_T
    nsteps = S // T
    ncl = T // CHUNK
    SSD_W = SSD_PAD_W

    def body(zb_ref, xbc_ref, halo_ref, dt_ref, dyb_ref, y_ref, st_ref, cw_ref, cb_ref, dtb_ref, alog_ref, dexp_ref, nw_ref,
             tri_ref, triT_ref, exp_ref, expT_ref,
             dp_ref, dcw_ref, dcb_ref, ddtb_ref, dalog_ref, dD_ref, dnw_ref,
             dht_ref, ext_ref, dpre_ref, dy_s, dE_s, dxdt_s, dxc_s, dDacc_ref, dAacc_ref):
        i = pl.program_id(0)

        @pl.when(i == 0)
        def _():
            for r in (dht_ref, dcw_ref, dcb_ref, ddtb_ref, dnw_ref, dDacc_ref, dAacc_ref):
                r[...] = jnp.zeros_like(r)
            dpre_ref[T:T + HALO, :] = jnp.zeros((HALO, XBC_W), F32)

        @pl.when(i == nsteps - 1)
        def _():
            ext_ref[0:HALO, :] = jnp.zeros((HALO, XBC_W), F32)

        @pl.when(i < nsteps - 1)
        def _():
            ext_ref[0:HALO, :] = halo_ref[...]

        ext_ref[HALO:HALO + T, :] = xbc_ref[...]
        diag, causal, lo, hi = _pair_masks()
        last_row = (lax.broadcasted_iota(jnp.int32, (CHUNK, 1), 0) == CHUNK - 1).astype(F32)
        for c in reversed(range(ncl)):
            q = _ssd_chunk_fwd(c, ext_ref, dt_ref, cw_ref, cb_ref, dtb_ref, alog_ref, tri_ref, exp_ref, None)
            rows = pl.ds(c * CHUNK, CHUNK)
            pre, sg, xc, dtr, dtv, A, E, dtE = (q[k] for k in ("pre", "sg", "xc", "dtr", "dtv", "A", "E", "dtE"))
            xs = xc[:, 0:D]
            total = E[CHUNK - 1:CHUNK, :]
            x_dt = xs * dtE
            eE = jnp.exp(E)
            wdec = jnp.exp(total - E)
            zb = zb_ref[rows, :]
            yv = y_ref[rows, :]
            sgz = _sigmoid(zb)
            sz = zb * sgz
            hh = yv * sz
            for g in range(SSD_GROUPS):
                gc = slice(g * GROUP_W, (g + 1) * GROUP_W)
                hg = hh[:, gc]
                r = lax.rsqrt(jnp.mean(hg * hg, axis=-1, keepdims=True) + EPS)
                dyb_g = dyb_ref[rows, gc]
                dn = dyb_g * nw_ref[:, gc]
                dnw_ref[0:1, gc] += jnp.sum(dyb_g * hg * r, axis=0, keepdims=True)
                dy_s[:, gc] = r * dn - hg * (r * r * r) * jnp.mean(dn * hg, axis=-1, keepdims=True)
            dhh = dy_s[...]
            dp_ref[rows, 0:D] = (dhh * yv * (sgz * (1.0 + zb * (1.0 - sgz)))).astype(BF16)
            dy = dhh * sz
            dy_s[...] = dy
            dDacc_ref[0:1, :] += jnp.sum(dy * xs, axis=0, keepdims=True)
            dxc_s[:, 0:D] = dy * dexp_ref[...]
            for g in range(SSD_GROUPS):
                gc = slice(g * GROUP_W, (g + 1) * GROUP_W)
                bcol = slice(D + g * STATE, D + (g + 1) * STATE)
                ccol = slice(D + SSD_GROUPS * STATE + g * STATE, D + SSD_GROUPS * STATE + (g + 1) * STATE)
                Bg = xc[:, bcol].astype(BF16)
                Cg = xc[:, ccol].astype(BF16)
                B2 = jnp.concatenate([Bg, Bg], axis=0)
                cb2 = _dot_nt(Cg, B2)
                htg = st_ref[c, :, gc]
                htb = htg.astype(BF16)
                dhn = dht_ref[:, gc]
                dhnb = dhn.astype(BF16)
                dyg = dy[:, gc]
                eEg = eE[:, gc]
                wg = wdec[:, gc]
                xdg = x_dt[:, gc]
                CH = _dot(Cg, htb)
                dCHb = (dyg * eEg).astype(BF16)
                dC = _dot_nt(dCHb, htb)
                dl = jnp.exp(total[:, gc])
                dht_prev = _dot_tn(Cg, dCHb) + dl * dhn
                dtot = jnp.sum(dhn * htg, axis=0, keepdims=True) * dl
                dxw = _dot(Bg, dhnb)
                dB = _dot_nt((xdg * wg).astype(BF16), dhnb)
                dwd = dxw * xdg * wg
                dtot = dtot + jnp.sum(dwd, axis=0, keepdims=True)
                dE_s[:, gc] = dyg * eEg * CH - dwd + last_row * dtot
                dxdt_s[:, gc] = dxw * wg
                dcb2 = jnp.zeros((CHUNK, LANE), F32)
                for jj in range(GROUP_W // LANE):
                    pc = slice(g * GROUP_W + jj * LANE, g * GROUP_W + (jj + 1) * LANE)
                    Ej = E[:, pc]
                    e2 = jnp.sum(Ej * diag, axis=0, keepdims=True)
                    Lp = jnp.exp(jnp.where(causal, Ej - e2, -1e30))
                    Mp = cb2 * Lp
                    xj = x_dt[:, pc]
                    xbd = jnp.concatenate([xj * lo, xj * hi], axis=0).astype(BF16)
                    dyj = dy[:, pc].astype(BF16)
                    dMp = _dot_nt(dyj, xbd)
                    dxbd = _dot_tn(Mp.astype(BF16), dyj)
                    dxdt_s[:, pc] += dxbd[0:CHUNK, :] * lo + dxbd[CHUNK:2 * CHUNK, :] * hi
                    dcb2 = dcb2 + dMp * Lp
                    dseg = dMp * Mp
                    dE_s[:, pc] += dseg - diag * jnp.sum(dseg, axis=0, keepdims=True)
                dcb2b = dcb2.astype(BF16)
                dC = dC + _dot(dcb2b, B2)
                dB2 = _dot_tn(dcb2b, Cg)
                dB = dB + dB2[0:CHUNK, :] + dB2[CHUNK:2 * CHUNK, :]
                dxc_s[:, bcol] = dB
                dxc_s[:, ccol] = dC
                dht_ref[:, gc] = dht_prev
            dx_dt = dxdt_s[...]
            dxc_s[:, 0:D] += dx_dt * dtE
            red = _dot(jnp.concatenate([dE_s[...], dx_dt * xs], axis=0), expT_ref[...], precision=HI)
            da = _dot(triT_ref[...], red[0:CHUNK, :], precision=HI)
            ddtv = red[CHUNK:2 * CHUNK, :] + da * A
            dAacc_ref[0:1, :] += jnp.sum(da * dtv, axis=0, keepdims=True)
            ddtr = ddtv * _sigmoid(dtr)
            ddtb_ref[0:1, :] += jnp.sum(ddtr, axis=0, keepdims=True)
            dp_ref[rows, D + XBC_W:D + XBC_W + DT_W] = ddtr.astype(BF16)
            dpre = dxc_s[...] * (sg * (1.0 + pre * (1.0 - sg)))
            dpre_ref[rows, :] = dpre
            dcb_ref[0:1, :] += jnp.sum(dpre, axis=0, keepdims=True)
        dpre_t = dpre_ref[0:T, :]
        dxbc = jnp.zeros((T, XBC_W), F32)
        for k in range(CONV_K):
            dcw_ref[k:k + 1, :] += jnp.sum(dpre_t * ext_ref[pl.ds(HALO - (CONV_K - 1) + k, T), :], axis=0, keepdims=True)
            dxbc = dxbc + cw_ref[k:k + 1, :] * dpre_ref[pl.ds(CONV_K - 1 - k, T), :]
        dp_ref[:, D:D + XBC_W] = dxbc.astype(BF16)
        dp_ref[:, SEG_SSD[1]:SSD_W] = jnp.zeros((T, SSD_W - SEG_SSD[1]), BF16)
        dpre_ref[T:T + HALO, :] = dpre_ref[0:HALO, :]

        @pl.when(i == nsteps - 1)
        def _():
            dalog_ref[...] = dAacc_ref[...] * (-jnp.exp(alog_ref[...]))
            dD_ref[...] = _dot(dDacc_ref[...], expT_ref[...], precision=HI)

    full = lambda a: pl.BlockSpec(a.shape, lambda i: (0,) * a.ndim)
    hb = T // HALO
    rev = lambda i: nsteps - 1 - i
    acc = lambda w: pl.BlockSpec((8, w), lambda i: (0, 0))
    return pl.pallas_call(
        body, name="ssd_bwd", grid=(nsteps,),
        in_specs=[pl.BlockSpec((T, D), lambda i: (rev(i), OFF_ZB // D)),
                  pl.BlockSpec((T, XBC_W), lambda i: (rev(i), OFF_XBC // XBC_W)),
                  pl.BlockSpec((HALO, XBC_W), lambda i: (jnp.maximum(rev(i) * hb - 1, 0), OFF_XBC // XBC_W)),
                  pl.BlockSpec((T, DT_W), lambda i: (rev(i), OFF_DT // DT_W)),
                  pl.BlockSpec((T, D), lambda i: (rev(i), 0)), pl.BlockSpec((T, D), lambda i: (rev(i), 0)),
                  pl.BlockSpec((ncl, STATE, D), lambda i: (rev(i), 0, 0)),
                  full(conv_w), full(conv_b), full(dtb_p), full(alog_p), full(d_exp), full(norm_w),
                  full(tri), full(triT), full(expand), full(expandT)],
        out_specs=[pl.BlockSpec((T, SSD_W), lambda i: (rev(i), 0)),
                   acc(XBC_W), acc(XBC_W), acc(DT_W), acc(DT_W), acc(DT_W), acc(D)],
        out_shape=[jax.ShapeDtypeStruct((S, SSD_W), BF16),
                   jax.ShapeDtypeStruct((8, XBC_W), F32), jax.ShapeDtypeStruct((8, XBC_W), F32),
                   jax.ShapeDtypeStruct((8, DT_W), F32), jax.ShapeDtypeStruct((8, DT_W), F32),
                   jax.ShapeDtypeStruct((8, DT_W), F32), jax.ShapeDtypeStruct((8, D), F32)],
        scratch_shapes=[pltpu.VMEM((STATE, D), F32), pltpu.VMEM((HALO + T, XBC_W), F32), pltpu.VMEM((T + HALO, XBC_W), F32),
                        pltpu.VMEM((CHUNK, D), F32), pltpu.VMEM((CHUNK, D), F32), pltpu.VMEM((CHUNK, D), F32),
                        pltpu.VMEM((CHUNK, XBC_W), F32), pltpu.VMEM((8, D), F32), pltpu.VMEM((8, DT_W), F32)],
        compiler_params=_cp(("arbitrary",)),
    )(proj, proj, proj, proj, dyb, y, states, conv_w, conv_b, dtb_p, alog_p, d_exp, norm_w, tri, triT, expand, expandT)


def _head(x, ya, yb, proj, target, gate_b, wout, fw, *, tm):
    S = x.shape[0]

    def body(x_ref, ya_ref, yb_ref, gl0_ref, gl1_ref, t_ref, gb_ref, w_ref, fw_ref,
             dh_ref, dhb_ref, mb_ref, dya_ref, dyb_ref, dgl_ref, loss_ref, dfw_ref, dgb_ref):
        @pl.when(pl.program_id(0) == 0)
        def _():
            loss_ref[...] = jnp.zeros_like(loss_ref)
            dfw_ref[...] = jnp.zeros_like(dfw_ref)
            dgb_ref[...] = jnp.zeros_like(dgb_ref)

        ya_v = ya_ref[...]
        yb_v = yb_ref[...]
        g0 = _sigmoid(gl0_ref[...] + gb_ref[:, 0:D])
        g1 = _sigmoid(gl1_ref[...] + gb_ref[:, D:2 * D])
        mb = (g0 * ya_v + g1 * yb_v).astype(BF16)
        mb_ref[...] = mb
        h = x_ref[...] + _dot(mb, w_ref[...])
        r = lax.rsqrt(jnp.mean(h * h, axis=-1, keepdims=True) + EPS)
        hn = h * r
        err = hn * fw_ref[...] - t_ref[...]
        loss_ref[...] += 0.5 * jnp.sum(jnp.mean(err * err, axis=-1, keepdims=True))
        dyf = err * (1.0 / D)
        dfw_ref[0:1, :] += jnp.sum(dyf * hn, axis=0, keepdims=True)
        dhn = dyf * fw_ref[...]
        dh = r * (dhn - hn * jnp.mean(dhn * hn, axis=-1, keepdims=True))
        dh_ref[...] = dh
        dhb = dh.astype(BF16)
        dhb_ref[...] = dhb
        dm = _dot_nt(dhb, w_ref[...])
        dya_ref[...] = dm * g0
        dyb_ref[...] = dm * g1
        dgl0 = dm * ya_v * g0 * (1.0 - g0)
        dgl1 = dm * yb_v * g1 * (1.0 - g1)
        dgl_ref[:, 0:D] = dgl0.astype(BF16)
        dgl_ref[:, D:2 * D] = dgl1.astype(BF16)
        dgb_ref[0:1, 0:D] += jnp.sum(dgl0, axis=0, keepdims=True)
        dgb_ref[0:1, D:2 * D] += jnp.sum(dgl1, axis=0, keepdims=True)

    row = pl.BlockSpec((tm, D), lambda i: (i, 0))
    seg = lambda off: pl.BlockSpec((tm, D), lambda i: (i, off // D))
    full = lambda a: pl.BlockSpec(a.shape, lambda i: (0,) * a.ndim)
    acc = lambda w: pl.BlockSpec((8, w), lambda i: (0, 0))
    return pl.pallas_call(
        body, name="head", grid=(S // tm,),
        in_specs=[row, row, row, seg(OFF_G0), seg(OFF_G1), row, full(gate_b), full(wout), full(fw)],
        out_specs=[row, row, row, row, row, pl.BlockSpec((tm, 2 * D), lambda i: (i, 0)), acc(LANE), acc(D), acc(2 * D)],
        out_shape=[jax.ShapeDtypeStruct((S, D), F32), jax.ShapeDtypeStruct((S, D), BF16), jax.ShapeDtypeStruct((S, D), BF16),
                   jax.ShapeDtypeStruct((S, D), F32), jax.ShapeDtypeStruct((S, D), F32), jax.ShapeDtypeStruct((S, 2 * D), BF16),
                   jax.ShapeDtypeStruct((8, LANE), F32), jax.ShapeDtypeStruct((8, D), F32), jax.ShapeDtypeStruct((8, 2 * D), F32)],
        compiler_params=_cp(("arbitrary",)),
    )(x, ya, yb, proj, proj, target, gate_b, wout, fw)


def _adam_update(g, w_ref, m_ref, v_ref, g_ref, d_ref, m2_ref, v2_ref):
    m2 = ADAM_B1 * m_ref[...] + (1.0 - ADAM_B1) * g
    v2 = ADAM_B2 * v_ref[...] + (1.0 - ADAM_B2) * (g * g)
    m_hat = m2 / (1.0 - ADAM_B1 ** ADAM_STEP)
    v_hat = v2 / (1.0 - ADAM_B2 ** ADAM_STEP)
    g_ref[...] = g
    d_ref[...] = -ADAM_LR * (m_hat / (jnp.sqrt(v_hat) + ADAM_EPS) + ADAM_WD * w_ref[...])
    m2_ref[...] = m2
    v2_ref[...] = v2


def _adamw_own(me, own, landed, w, m, v, *, tr, tc, name):
    _, R, C = landed.shape
    assert R % tr == 0 and C % tc == 0, (name, R, C, tr, tc)

    def body(me_ref, own_ref, p_ref, w_ref, m_ref, v_ref, g_ref, d_ref, m2_ref, v2_ref):
        mine = own_ref[0].astype(F32)
        g = jnp.where(me_ref[0] == 0, mine, p_ref[0].astype(F32))
        for k in range(1, N_DEV):
            g = g + jnp.where(me_ref[0] == k, mine, p_ref[k].astype(F32))
        _adam_update(g, w_ref, m_ref, v_ref, g_ref, d_ref, m2_ref, v2_ref)

    tile = pl.BlockSpec((tr, tc), lambda i, j, me_ref: (i, j))
    return pl.pallas_call(
        body, name=name,
        grid_spec=pltpu.PrefetchScalarGridSpec(
            num_scalar_prefetch=1, grid=(R // tr, C // tc),
            in_specs=[pl.BlockSpec((1, tr, tc), lambda i, j, me_ref: (me_ref[0], i, j)),
                      pl.BlockSpec((N_DEV, tr, tc), lambda i, j, me_ref: (0, i, j)), tile, tile, tile],
            out_specs=[tile, tile, tile, tile]),
        out_shape=[jax.ShapeDtypeStruct((R, C), F32)] * 4,
        compiler_params=_cp(("parallel", "parallel")),
    )(me, own, landed, w, m, v)


def _adamw(parts, w, m, v, *, tr, name):
    _, R, C = parts.shape
    assert R % tr == 0, (name, R, tr)

    def body(p_ref, w_ref, m_ref, v_ref, g_ref, d_ref, m2_ref, v2_ref):
        g = p_ref[0].astype(F32)
        for k in range(1, N_DEV):
            g = g + p_ref[k].astype(F32)
        _adam_update(g, w_ref, m_ref, v_ref, g_ref, d_ref, m2_ref, v2_ref)

    row = pl.BlockSpec((tr, C), lambda i: (i, 0))
    return pl.pallas_call(
        body, name=name, grid=(R // tr,),
        in_specs=[pl.BlockSpec((N_DEV, tr, C), lambda i: (0, i, 0)), row, row, row],
        out_specs=[row, row, row, row],
        out_shape=[jax.ShapeDtypeStruct((R, C), F32)] * 4,
        compiler_params=_cp(("parallel",)),
    )(parts, w, m, v)


def _place():
    x, y, c = lax.axis_index("x"), lax.axis_index("y"), lax.axis_index("c")
    return x, y, c


def _all_gather(arrs, *, name):
    n = len(arrs)

    def body(*refs):
        ins, outs = refs[:n], refs[n:2 * n]
        send_sems, recv_sems, local_sems = refs[2 * n:]
        x, y, c = _place()
        me, sibling = (x, y, c), (x, y, 1 - c)
        chips = [(1 - x, y), (x, 1 - y), (1 - x, 1 - y)]

        def idx(px, py, pc):
            return 4 * px + 2 * py + pc

        def copy(k, a, block, to, src=None):
            slab = outs[a].at[idx(*block)]
            return pltpu.make_async_remote_copy(
                src_ref=slab if src is None else src, dst_ref=slab,
                send_sem=send_sems.at[k, a], recv_sem=recv_sems.at[k, a], device_id=to, device_id_type=MESH)

        mine = [pltpu.make_async_copy(ins[a], outs[a].at[idx(*me)], local_sems.at[a]) for a in range(n)]
        for cp in mine:
            cp.start()
        first = []
        for a in range(n):
            first.append(copy(0, a, me, sibling, src=ins[a]))
            first += [copy(1 + j, a, me, (*chip, c), src=ins[a]) for j, chip in enumerate(chips)]
        for cp in first:
            cp.start()
        passed = []
        for j, chip in enumerate(chips):
            for a in range(n):
                copy(1 + j, a, (*chip, c), me).wait_recv()
                fwd = copy(4 + j, a, (*chip, c), sibling)
                fwd.start()
                passed.append(fwd)
        for a in range(n):
            copy(0, a, sibling, me).wait_recv()
            for j, chip in enumerate(chips):
                copy(4 + j, a, (*chip, 1 - c), me).wait_recv()
        for cp in first + passed:
            cp.wait_send()
        for cp in mine:
            cp.wait()

    anyspec = pl.BlockSpec(memory_space=pl.ANY)
    return pl.pallas_call(
        body, name=name,
        in_specs=[anyspec] * n, out_specs=[anyspec] * n,
        out_shape=[jax.ShapeDtypeStruct((N_DEV,) + a.shape, a.dtype) for a in arrs],
        scratch_shapes=[pltpu.SemaphoreType.DMA((7, n)), pltpu.SemaphoreType.DMA((7, n)), pltpu.SemaphoreType.DMA((n,))],
    )(*arrs)


_REL = [(dx, dy, dc) for dx in (0, 1) for dy in (0, 1) for dc in (0, 1)][1:]
_HBM = pl.BlockSpec(memory_space=pltpu.HBM)
_SEM = pl.BlockSpec(memory_space=pltpu.SEMAPHORE)
_EFFECT = pltpu.SideEffectType.DATAFLOW_SIDE_EFFECTING


def _peer(k):
    x, y, c = _place()
    dx, dy, dc = _REL[k]
    return (1 - x if dx else x, 1 - y if dy else y, 1 - c if dc else c)


def _exchange_start(parts, *, name):
    n = len(parts)

    def body(*refs):
        ins, lands = refs[:n], refs[n:2 * n]
        send_sems, recv_sems, token = refs[2 * n], refs[2 * n + 1], refs[-1]
        x, y, c = _place()
        me = 4 * x + 2 * y + c
        for a in range(n):
            for k in range(len(_REL)):
                px, py, pc = _peer(k)
                pltpu.make_async_remote_copy(
                    src_ref=ins[a].at[4 * px + 2 * py + pc], dst_ref=lands[a].at[me],
                    send_sem=send_sems.at[len(_REL) * a + k], recv_sem=recv_sems.at[len(_REL) * a + k],
                    device_id=(px, py, pc), device_id_type=MESH).start()
        token[...] = jnp.zeros_like(token)

    sem = pltpu.SemaphoreType.DMA((len(_REL) * n,))
    bufs = [pltpu.HBM(p.shape, p.dtype) for p in parts]
    outs = pl.pallas_call(
        body, name=name,
        out_shape=(sem, sem, *bufs, *bufs, jax.ShapeDtypeStruct((8, LANE), F32)),
        in_specs=(_HBM,) * (2 * n), out_specs=(_SEM, _SEM, *(_HBM,) * (2 * n), pl.BlockSpec(memory_space=pltpu.VMEM)),
        input_output_aliases={i: 2 + i for i in range(2 * n)},
        compiler_params=pltpu.CompilerParams(has_side_effects=_EFFECT),
    )(*[pltpu.with_memory_space_constraint(p, pltpu.HBM) for p in parts],
      *[pltpu.with_memory_space_constraint(lax.empty(p.shape, p.dtype), pltpu.HBM) for p in parts])
    return outs[0], outs[1], outs[2:2 + n], outs[2 + n:2 + 2 * n], outs[-1]


def _exchange_wait(send_sems, recv_sems, parts, lands, after, *, name):
    n = len(parts)

    def body(*refs):
        ins, lands_ = refs[:n], refs[n:2 * n]
        ssem, rsem = refs[2 * n], refs[2 * n + 1]
        for a in range(n):
            for k in range(len(_REL)):
                px, py, pc = _peer(k)
                p = 4 * px + 2 * py + pc
                cp = pltpu.make_async_remote_copy(
                    src_ref=ins[a].at[p], dst_ref=lands_[a].at[p],
                    send_sem=ssem.at[len(_REL) * a + k], recv_sem=rsem.at[len(_REL) * a + k],
                    device_id=(px, py, pc), device_id_type=MESH)
                cp.wait_send()
                cp.wait_recv()

    bufs = [pltpu.HBM(p.shape, p.dtype) for p in parts]
    outs = pl.pallas_call(
        body, name=name, out_shape=(*bufs, *bufs),
        in_specs=(*(_HBM,) * (2 * n), _SEM, _SEM, pl.BlockSpec(memory_space=pl.ANY)), out_specs=(_HBM,) * (2 * n),
        input_output_aliases={i: i for i in range(2 * n)},
        compiler_params=pltpu.CompilerParams(has_side_effects=_EFFECT),
    )(*parts, *lands, send_sems, recv_sems, after)
    return outs[:n], outs[n:]


WEIGHTS = ('norm_w', 'w_in', 'gate_b', 'sgu_norm_g', 'sgu_norm_b', 'sgu_w', 'sgu_b', 'conv_w', 'conv_b', 'dt_bias', 'A_log',
           'D_skip', 'ssd_norm_w', 'w_out', 'final_norm_w')
SHARDED = ('w_in', 'conv_w', 'w_out')
PACK_ROW = 8 * LANE


def _constants():
    tri = np.tril(np.ones((CHUNK, CHUNK), np.float32))
    expand = np.zeros((DT_W, D), np.float32)
    for h in range(HEADS):
        expand[h, h * HEADDIM:(h + 1) * HEADDIM] = 1.0
    sel = np.zeros((D, LANE), np.float32)
    for g in range(SGU_GROUPS):
        sel[g * LANE:(g + 1) * LANE, g] = 1.0
    pos_chunk = np.arange(SGU_BLOCK) // CHUNK
    mask = (pos_chunk[None, :] <= pos_chunk[:, None]).astype(np.float32)
    return dict(tri=jnp.asarray(tri), triT=jnp.asarray(tri.T.copy()), expand=jnp.asarray(expand),
                expandT=jnp.asarray(expand.T.copy()), sel=jnp.asarray(sel), mask=jnp.asarray(mask))


def _permute_rows(wT):
    return jnp.concatenate([wT[:6144], wT[11296:], wT[6144:11296], jnp.zeros((DT_W - HEADS, D), wT.dtype)], axis=0)


def _local_step(x2, tgt, wpT, wout, cw, p, exchange):
    S = x2.shape[0]
    k = _constants()
    xn = _norm_fwd(x2, p['norm_w'], tm=min(512, S))
    proj = _matmul(xn, wpT, trans_b=True, tm=min(1024, S), tn=1408, tk=D, name="in_proj")
    wm32 = p['sgu_w'][0] * k['mask']
    wm = wm32.astype(BF16)
    wmT = jnp.swapaxes(wm32, 1, 2).astype(BF16)
    bias_full = jnp.repeat(p['sgu_b'][0].T, LANE, axis=1)
    tm_sgu = min(256, S)
    ya = _sgu_fwd(proj, p['sgu_norm_g'], p['sgu_norm_b'], wm, bias_full, tm=tm_sgu)
    pad32 = lambda a: jnp.pad(a, ((0, 0), (0, DT_W - HEADS)))
    dtb_p, alog_p = pad32(p['dt_bias']), pad32(p['A_log'])
    d_exp = jnp.repeat(p['D_skip'], HEADDIM, axis=1)
    ssd_args = (cw, p['conv_b'], dtb_p, alog_p, d_exp, p['ssd_norm_w'])
    y, yb, states = _ssd_fwd(proj, *ssd_args, k['tri'], k['expand'])
    dh, dhb, mb, dya, dyb, dgl, loss, dfw, dgb = _head(
        x2, ya, yb, proj, tgt, p['gate_b'], wout, p['final_norm_w'][None, :], tm=min(128, S))
    dsgu, dws, dbsT, dsg, dsb = _sgu_bwd(proj, dya, p['sgu_norm_g'], p['sgu_norm_b'], wm, wmT, bias_full, k['mask'], k['sel'],
                                         tm=tm_sgu)
    dssd, dcw, dcb, ddtb, dalog, dD, dnw = _ssd_bwd(proj, dyb, y, states, *ssd_args, k['tri'], k['triT'], k['expand'], k['expandT'])
    tk = min(2048, S)
    tn = 1024
    dwT_sgu = _matmul(dsgu, xn, trans_a=True, out_dtype=BF16, tm=1024, tn=tn, tk=tk, name="dw_in_sgu")
    dwT_gate = _matmul(dgl, xn, trans_a=True, out_dtype=BF16, tm=1024, tn=tn, tk=tk, name="dw_in_gate")
    dwT_ssd = _matmul(dssd, xn, trans_a=True, out_dtype=BF16, tm=1024, tn=tn, tk=tk, name="dw_in_ssd")
    dw_inT = jnp.concatenate([dwT_sgu, dwT_ssd[:W_IN - SEG_SSD[0]], dwT_gate], axis=0)
    dw_out = _matmul(mb, dhb, trans_a=True, out_dtype=BF16, tm=1024, tn=tn, tk=tk, name="dw_out")
    token = exchange(dw_inT, dw_out)
    tm = min(1024, S)
    wpT_ssd = jnp.pad(wpT[SEG_SSD[0]:], ((0, SSD_PAD_W - SEG_SSD[1]), (0, 0)))
    dxn = _matmul(dsgu, wpT, tm=tm, tn=tn, tk=2048, after=token, name="dxn_sgu")
    dxn = _matmul(dgl, wpT, b_koff=SEG_GATE[0] // 2048, tm=tm, tn=tn, tk=2048, add=dxn, name="dxn_gate")
    dxn = _matmul(dssd, wpT_ssd, tm=tm, tn=tn, tk=2048, add=dxn, name="dxn_ssd")
    grad_x, dnorm = _norm_bwd(x2, p['norm_w'], dxn, dh, tm=min(256, S))
    grads = dict(
        norm_w=dnorm[0:1], gate_b=dgb[0:1], sgu_norm_g=dsg[0:1], sgu_norm_b=dsb[0:1], sgu_w=dws[None],
        sgu_b=dbsT[:, :SGU_GROUPS].T[None], conv_w=dcw[0:CONV_K][None], conv_b=dcb[0:1], dt_bias=ddtb[0:1, :HEADS],
        A_log=dalog[0:1, :HEADS], D_skip=dD[0:1, :HEADS], ssd_norm_w=dnw[0:1], final_norm_w=dfw[0])
    return loss[0, 0], grad_x, grads


def _pack(arrs):
    rows, offs, r = [], [], 0
    for a in arrs:
        n = a.size
        nr = -(-n // PACK_ROW) * 8
        rows.append(jnp.pad(a.reshape(-1).astype(F32), (0, nr * LANE - n)).reshape(nr, LANE))
        offs.append(r)
        r += nr
    return jnp.concatenate(rows, axis=0), offs


def kernel(x, norm_w, w_in, gate_b, sgu_norm_g, sgu_norm_b, sgu_w, sgu_b, conv_w, conv_b, dt_bias, A_log, D_skip, ssd_norm_w, w_out, final_norm_w, loss_target, m_norm_w, m_w_in, m_gate_b, m_sgu_norm_g, m_sgu_norm_b, m_sgu_w, m_sgu_b, m_conv_w, m_conv_b, m_dt_bias, m_A_log, m_D_skip, m_ssd_norm_w, m_w_out, m_final_norm_w, v_norm_w, v_w_in, v_gate_b, v_sgu_norm_g, v_sgu_norm_b, v_sgu_w, v_sgu_b, v_conv_w, v_conv_b, v_dt_bias, v_A_log, v_D_skip, v_ssd_norm_w, v_w_out, v_final_norm_w):
    w = dict(norm_w=norm_w, w_in=w_in, gate_b=gate_b, sgu_norm_g=sgu_norm_g, sgu_norm_b=sgu_norm_b, sgu_w=sgu_w, sgu_b=sgu_b,
             conv_w=conv_w, conv_b=conv_b, dt_bias=dt_bias, A_log=A_log, D_skip=D_skip, ssd_norm_w=ssd_norm_w, w_out=w_out,
             final_norm_w=final_norm_w)
    m = dict(norm_w=m_norm_w, w_in=m_w_in, gate_b=m_gate_b, sgu_norm_g=m_sgu_norm_g, sgu_norm_b=m_sgu_norm_b, sgu_w=m_sgu_w,
             sgu_b=m_sgu_b, conv_w=m_conv_w, conv_b=m_conv_b, dt_bias=m_dt_bias, A_log=m_A_log, D_skip=m_D_skip,
             ssd_norm_w=m_ssd_norm_w, w_out=m_w_out, final_norm_w=m_final_norm_w)
    v = dict(norm_w=v_norm_w, w_in=v_w_in, gate_b=v_gate_b, sgu_norm_g=v_sgu_norm_g, sgu_norm_b=v_sgu_norm_b, sgu_w=v_sgu_w,
             sgu_b=v_sgu_b, conv_w=v_conv_w, conv_b=v_conv_b, dt_bias=v_dt_bias, A_log=v_A_log, D_skip=v_D_skip,
             ssd_norm_w=v_ssd_norm_w, w_out=v_w_out, final_norm_w=v_final_norm_w)
    me = 4 * lax.axis_index("x") + 2 * lax.axis_index("y") + lax.axis_index("c")
    shard_cw = XBC_W // N_DEV

    tpose = lambda a: jnp.swapaxes(a[0], 0, 1)
    g_in, g_out, g_cw = _all_gather([tpose(w_in).astype(BF16), w_out[0].astype(BF16), conv_w[0]], name="gather_weights")
    wpT = _permute_rows(g_in.reshape(W_IN, D))
    wout_full = g_out.reshape(D, D)
    cw_full = jnp.swapaxes(g_cw, 0, 1).reshape(CONV_K, XBC_W)

    flight = {}

    def exchange(dw_inT, dw_out):
        parts = [dw_inT.reshape(N_DEV, SHARD_IN, D), dw_out.reshape(N_DEV, D // N_DEV, D)]
        flight['sems'], flight['rsems'], flight['parts'], flight['lands'], token = _exchange_start(parts, name="exchange_start")
        return token

    loss_part, grad_x, grads = _local_step(x[0], loss_target[0], wpT, wout_full, cw_full, w, exchange)
    (own_in, own_out), (land_in, land_out) = _exchange_wait(
        flight['sems'], flight['rsems'], flight['parts'], flight['lands'], grad_x, name="exchange_wait")
    me_arr = jnp.reshape(me, (1,)).astype(jnp.int32)
    res = {}
    res['w_in'] = [jnp.swapaxes(o, 0, 1) for o in _adamw_own(
        me_arr, own_in, land_in, tpose(w_in), tpose(m_w_in), tpose(v_w_in), tr=SHARD_IN, tc=256, name="adamw_w_in")]
    res['w_out'] = _adamw_own(me_arr, own_out, land_out, w_out[0], m_w_out[0], v_w_out[0], tr=128, tc=D, name="adamw_w_out")

    small = [n for n in WEIGHTS if n not in SHARDED]
    packed, offs = _pack([grads[n] for n in small] + [loss_part, grads['conv_w']])
    (gathered,) = _all_gather([packed], name="gather_small")
    off_loss, off_cw = offs[-2], offs[-1]
    cw_parts = gathered[:, off_cw:, :].reshape(N_DEV, CONV_K, XBC_W)
    cw_parts = lax.dynamic_slice_in_dim(cw_parts, me * shard_cw, shard_cw, axis=2)
    cw_rows = _pack([cw_parts[0]])[0].shape[0]
    cw_parts = jnp.pad(cw_parts.reshape(N_DEV, -1), ((0, 0), (0, cw_rows * LANE - CONV_K * shard_cw))).reshape(N_DEV, cw_rows, LANE)
    parts = jnp.concatenate([gathered[:, :off_cw, :], cw_parts], axis=1)
    zero = jnp.zeros((), F32)
    packs = [_pack([d[n] for n in small] + [zero, d['conv_w']])[0] for d in (w, m, v)]
    outs = _adamw(parts, *packs, tr=parts.shape[1], name="adamw_small")

    def unpack(o, name):
        if name == 'conv_w':
            return o[off_cw:off_cw + cw_rows].reshape(-1)[:CONV_K * shard_cw].reshape(w['conv_w'].shape)
        r0 = offs[small.index(name)]
        n = w[name].size
        return o[r0:r0 + -(-n // PACK_ROW) * 8].reshape(-1)[:n].reshape(w[name].shape)

    for n in small + ['conv_w']:
        res[n] = [unpack(o, n) for o in outs]
    for n in ('w_in', 'w_out'):
        res[n] = [o[None] for o in res[n]]
    loss = outs[0][off_loss, 0]
    return (loss, grad_x[None], *[res[n][0] for n in WEIGHTS], *[res[n][1] for n in WEIGHTS],
            *[res[n][2] for n in WEIGHTS], *[res[n][3] for n in WEIGHTS])
```

```python
import functools

import numpy as np
import jax
import jax.numpy as jnp
from jax import lax
from jax.experimental import pallas as pl
from jax.experimental.pallas import tpu as pltpu

F32 = jnp.float32
BF16 = jnp.bfloat16
HI = lax.Precision.HIGHEST
MESH = pl.DeviceIdType.MESH

D = 2048
EPS = 1e-5
SGU_BLOCK = 128
SGU_GROUPS = 16
CHUNK = 64
HEADS = 32
HEADDIM = 64
SSD_GROUPS = 4
GROUP_W = D // SSD_GROUPS
STATE = 128
CONV_K = 4
XBC_W = D + 2 * SSD_GROUPS * STATE
W_IN = 15392
N_DEV = 8
SHARD_IN = W_IN // N_DEV
ADAM_LR, ADAM_B1, ADAM_B2, ADAM_EPS, ADAM_WD, ADAM_STEP = 0.001, 0.9, 0.999, 1e-08, 0.01, 10

LANE = 128
DT_W = LANE
OFF_U, OFF_V, OFF_ZA, OFF_G0, OFF_G1, OFF_ZB, OFF_XBC, OFF_DT = 0, 2048, 4096, 6144, 8192, 10240, 12288, 15360
WP = OFF_DT + DT_W
SEG_SGU = (0, 6144)
SEG_GATE = (6144, 4096)
SEG_SSD = (10240, WP - 10240)
SSD_PAD_W = 6144
VMEM_LIMIT = 56 * 1024 * 1024


def _cp(sem=None, vmem=VMEM_LIMIT):
    return pltpu.CompilerParams(dimension_semantics=sem, vmem_limit_bytes=vmem)


def _sigmoid(x):
    return 1.0 / (1.0 + jnp.exp(-x))


def _softplus(x):
    return jnp.maximum(x, 0.0) + jnp.log(1.0 + jnp.exp(-jnp.abs(x)))


def _dot(a, b, precision=None):
    return jnp.dot(a, b, preferred_element_type=F32, precision=precision)


def _dot_nt(a, b, precision=None):
    return lax.dot_general(a, b, (((1,), (1,)), ((), ())), preferred_element_type=F32, precision=precision)


def _dot_tn(a, b, precision=None):
    return lax.dot_general(a, b, (((0,), (0,)), ((), ())), preferred_element_type=F32, precision=precision)


def _matmul(a, b, *, trans_a=False, trans_b=False, b_koff=0, out_dtype=F32, tm, tn, tk, add=None, after=None, name):
    K, M = a.shape if trans_a else a.shape[::-1]
    N = b.shape[0] if trans_b else b.shape[1]
    assert M % tm == 0 and N % tn == 0 and K % tk == 0 and not (trans_a and trans_b), (name, M, N, K, tm, tn, tk)
    nk = K // tk

    def body(*refs):
        a_ref, b_ref = refs[:2]
        add_ref = refs[2] if add is not None else None
        o_ref, acc_ref = refs[-2:]
        k = pl.program_id(2)
        if trans_a:
            part = _dot_tn(a_ref[...], b_ref[...])
        else:
            part = _dot_nt(a_ref[...], b_ref[...]) if trans_b else _dot(a_ref[...], b_ref[...])

        def result(r):
            if add_ref is not None:
                r = r + add_ref[...]
            return r.astype(out_dtype)

        if nk == 1:
            o_ref[...] = result(part)
        else:
            @pl.when(k == 0)
            def _():
                acc_ref[...] = part

            @pl.when(jnp.logical_and(k > 0, k < nk - 1))
            def _():
                acc_ref[...] += part

            @pl.when(k == nk - 1)
            def _():
                o_ref[...] = result(acc_ref[...] + part)

    in_specs = [pl.BlockSpec((tk, tm), lambda i, j, k: (k, i)) if trans_a else pl.BlockSpec((tm, tk), lambda i, j, k: (i, k)),
                pl.BlockSpec((tn, tk), lambda i, j, k: (j, k)) if trans_b else pl.BlockSpec((tk, tn), lambda i, j, k: (k + b_koff, j))]
    args = [a, b]
    if add is not None:
        in_specs.append(pl.BlockSpec((tm, tn), lambda i, j, k: (i, j)))
        args.append(add)
    if after is not None:
        in_specs.append(pl.BlockSpec(memory_space=pl.ANY))
        args.append(after)
    return pl.pallas_call(
        body, name=name, grid=(M // tm, N // tn, nk), in_specs=in_specs,
        out_specs=pl.BlockSpec((tm, tn), lambda i, j, k: (i, j)),
        out_shape=jax.ShapeDtypeStruct((M, N), out_dtype),
        scratch_shapes=[pltpu.VMEM((tm, tn), F32)],
        compiler_params=_cp(("parallel", "parallel", "arbitrary")),
    )(*args)


def _norm_fwd(x, w, *, tm):
    S = x.shape[0]

    def body(x_ref, w_ref, o_ref):
        xv = x_ref[...]
        r = lax.rsqrt(jnp.mean(xv * xv, axis=-1, keepdims=True) + EPS)
        o_ref[...] = (xv * r * w_ref[...]).astype(BF16)

    return pl.pallas_call(
        body, name="norm_fwd", grid=(S // tm,),
        in_specs=[pl.BlockSpec((tm, D), lambda i: (i, 0)), pl.BlockSpec((1, D), lambda i: (0, 0))],
        out_specs=pl.BlockSpec((tm, D), lambda i: (i, 0)),
        out_shape=jax.ShapeDtypeStruct((S, D), BF16), compiler_params=_cp(("parallel",)),
    )(x, w)


def _norm_bwd(x, w, dxn, dh, *, tm):
    S = x.shape[0]

    def body(x_ref, w_ref, dxn_ref, dh_ref, gx_ref, dw_ref):
        xv = x_ref[...]
        r = lax.rsqrt(jnp.mean(xv * xv, axis=-1, keepdims=True) + EPS)
        xh = xv * r
        dxn_v = dxn_ref[...]
        dxh = dxn_v * w_ref[...]
        gx_ref[...] = dh_ref[...] + r * (dxh - xh * jnp.mean(dxh * xh, axis=-1, keepdims=True))

        @pl.when(pl.program_id(0) == 0)
        def _():
            dw_ref[...] = jnp.zeros_like(dw_ref)

        dw_ref[0:1, :] += jnp.sum(dxn_v * xh, axis=0, keepdims=True)

    row = pl.BlockSpec((tm, D), lambda i: (i, 0))
    return pl.pallas_call(
        body, name="norm_bwd", grid=(S // tm,),
        in_specs=[row, pl.BlockSpec((1, D), lambda i: (0, 0)), row, row],
        out_specs=[row, pl.BlockSpec((8, D), lambda i: (0, 0))],
        out_shape=[jax.ShapeDtypeStruct((S, D), F32), jax.ShapeDtypeStruct((8, D), F32)],
        compiler_params=_cp(("arbitrary",)),
    )(x, w, dxn, dh)


def _sgu_core(u_ref, v_ref, z_ref, g_ref, b_ref, wm_ref, bias_ref, vnb_ref, mixed_ref, tm):
    v = v_ref[...]
    mu = jnp.mean(v, axis=-1, keepdims=True)
    vc = v - mu
    rs = lax.rsqrt(jnp.mean(vc * vc, axis=-1, keepdims=True) + EPS)
    vh = vc * rs
    vnb_ref[...] = (vh * g_ref[...] + b_ref[...]).astype(BF16)
    for blk in range(tm // SGU_BLOCK):
        rows = pl.ds(blk * SGU_BLOCK, SGU_BLOCK)
        for gi in range(SGU_GROUPS):
            cols = pl.ds(gi * LANE, LANE)
            mixed_ref[rows, cols] = _dot(wm_ref[gi], vnb_ref[rows, cols]) + bias_ref[:, cols]
    return vh, rs


def _sgu_fwd(proj, g, b, wm, bias_full, *, tm):
    S = proj.shape[0]

    def body(u_ref, v_ref, z_ref, g_ref, b_ref, wm_ref, bias_ref, y_ref, vnb_ref, mixed_ref):
        _sgu_core(u_ref, v_ref, z_ref, g_ref, b_ref, wm_ref, bias_ref, vnb_ref, mixed_ref, tm)
        z = z_ref[...]
        y_ref[...] = u_ref[...] * mixed_ref[...] * (z * _sigmoid(z))

    seg = lambda off: pl.BlockSpec((tm, D), lambda i: (i, off // D))
    full = lambda a: pl.BlockSpec(a.shape, lambda i: (0,) * a.ndim)
    return pl.pallas_call(
        body, name="sgu_fwd", grid=(S // tm,),
        in_specs=[seg(OFF_U), seg(OFF_V), seg(OFF_ZA), full(g), full(b), full(wm), full(bias_full)],
        out_specs=pl.BlockSpec((tm, D), lambda i: (i, 0)),
        out_shape=jax.ShapeDtypeStruct((S, D), F32),
        scratch_shapes=[pltpu.VMEM((tm, D), BF16), pltpu.VMEM((tm, D), F32)],
        compiler_params=_cp(("parallel",)),
    )(proj, proj, proj, g, b, wm, bias_full)


def _sgu_bwd(proj, dy, g, b, wm, wmT, bias_full, mask, sel, *, tm):
    S = proj.shape[0]
    nsteps = S // tm

    def body(u_ref, v_ref, z_ref, dy_ref, g_ref, b_ref, wm_ref, wmT_ref, bias_ref, mask_ref, sel_ref,
             dp_ref, dws_ref, dbs_ref, dg_ref, db_ref, vnb_ref, mixed_ref, dmb_ref, dvn_ref, dbias_ref):
        i = pl.program_id(0)

        @pl.when(i == 0)
        def _():
            dws_ref[...] = jnp.zeros_like(dws_ref)
            dg_ref[...] = jnp.zeros_like(dg_ref)
            db_ref[...] = jnp.zeros_like(db_ref)
            dbias_ref[...] = jnp.zeros_like(dbias_ref)

        vh, rs = _sgu_core(u_ref, v_ref, z_ref, g_ref, b_ref, wm_ref, bias_ref, vnb_ref, mixed_ref, tm)
        u = u_ref[...]
        z = z_ref[...]
        dy_v = dy_ref[...]
        mixed = mixed_ref[...]
        sg = _sigmoid(z)
        sz = z * sg
        dp_ref[:, 0:D] = (dy_v * mixed * sz).astype(BF16)
        dp_ref[:, 2 * D:3 * D] = (dy_v * u * mixed * (sg * (1.0 + z * (1.0 - sg)))).astype(BF16)
        dmixed = dy_v * u * sz
        dmb_ref[...] = dmixed.astype(BF16)
        for blk in range(tm // SGU_BLOCK):
            dbias_ref[...] += dmixed[blk * SGU_BLOCK:(blk + 1) * SGU_BLOCK, :]
        for blk in range(tm // SGU_BLOCK):
            rows = pl.ds(blk * SGU_BLOCK, SGU_BLOCK)
            for gi in range(SGU_GROUPS):
                cols = pl.ds(gi * LANE, LANE)
                dm = dmb_ref[rows, cols]
                dvn_ref[rows, cols] = _dot(wmT_ref[gi], dm)
                dws_ref[gi] += _dot_nt(dm, vnb_ref[rows, cols])
        dvn = dvn_ref[...]
        dg_ref[0:1, :] += jnp.sum(dvn * vh, axis=0, keepdims=True)
        db_ref[0:1, :] += jnp.sum(dvn, axis=0, keepdims=True)
        dvh = dvn * g_ref[...]
        dv = rs * (dvh - jnp.mean(dvh, axis=-1, keepdims=True) - vh * jnp.mean(dvh * vh, axis=-1, keepdims=True))
        dp_ref[:, D:2 * D] = dv.astype(BF16)

        @pl.when(i == nsteps - 1)
        def _():
            for gi in range(SGU_GROUPS):
                dws_ref[gi] = dws_ref[gi] * mask_ref[...]
            dbs_ref[...] = _dot(dbias_ref[...], sel_ref[...], precision=HI)

    seg = lambda off: pl.BlockSpec((tm, D), lambda i: (i, off // D))
    full = lambda a: pl.BlockSpec(a.shape, lambda i: (0,) * a.ndim)
    return pl.pallas_call(
        body, name="sgu_bwd", grid=(nsteps,),
        in_specs=[seg(OFF_U), seg(OFF_V), seg(OFF_ZA), pl.BlockSpec((tm, D), lambda i: (i, 0)),
                  full(g), full(b), full(wm), full(wmT), full(bias_full), full(mask), full(sel)],
        out_specs=[pl.BlockSpec((tm, 3 * D), lambda i: (i, 0)),
                   pl.BlockSpec((SGU_GROUPS, SGU_BLOCK, SGU_BLOCK), lambda i: (0, 0, 0)),
                   pl.BlockSpec((SGU_BLOCK, LANE), lambda i: (0, 0)),
                   pl.BlockSpec((8, D), lambda i: (0, 0)), pl.BlockSpec((8, D), lambda i: (0, 0))],
        out_shape=[jax.ShapeDtypeStruct((S, 3 * D), BF16),
                   jax.ShapeDtypeStruct((SGU_GROUPS, SGU_BLOCK, SGU_BLOCK), F32),
                   jax.ShapeDtypeStruct((SGU_BLOCK, LANE), F32),
                   jax.ShapeDtypeStruct((8, D), F32), jax.ShapeDtypeStruct((8, D), F32)],
        scratch_shapes=[pltpu.VMEM((tm, D), BF16), pltpu.VMEM((tm, D), F32), pltpu.VMEM((tm, D), BF16),
                        pltpu.VMEM((tm, D), F32), pltpu.VMEM((SGU_BLOCK, D), F32)],
        compiler_params=_cp(("arbitrary",)),
    )(proj, proj, proj, dy, g, b, wm, wmT, bias_full, mask, sel)


SSD_T = 2 * CHUNK
HALO = 8


def _pair_masks():
    row = lax.broadcasted_iota(jnp.int32, (CHUNK, LANE), 0)
    lane = lax.broadcasted_iota(jnp.int32, (CHUNK, LANE), 1)
    pos = jnp.where(lane >= CHUNK, lane - CHUNK, lane)
    diag = (row == pos).astype(F32)
    causal = row >= pos
    lo = (lane < CHUNK).astype(F32)
    return diag, causal, lo, 1.0 - lo


def _ssd_chunk_fwd(c, ext_ref, dt_ref, cw_ref, cb_ref, dtb_ref, alog_ref, tri_ref, exp_ref, ht_ref):
    r0 = c * CHUNK
    pre = cb_ref[...] + sum(cw_ref[k:k + 1, :] * ext_ref[pl.ds(r0 + HALO - (CONV_K - 1) + k, CHUNK), :] for k in range(CONV_K))
    sg = _sigmoid(pre)
    xc = pre * sg
    dtr = dt_ref[pl.ds(r0, CHUNK), :] + dtb_ref[...]
    dtv = _softplus(dtr)
    A = -jnp.exp(alog_ref[...])
    acs = _dot(tri_ref[...], dtv * A, precision=HI)
    E = _dot(acs, exp_ref[...], precision=HI)
    dtE = _dot(dtv, exp_ref[...], precision=HI)
    return dict(pre=pre, sg=sg, xc=xc, dtr=dtr, dtv=dtv, A=A, E=E, dtE=dtE)


def _ssd_fwd(proj, conv_w, conv_b, dtb_p, alog_p, d_exp, norm_w, tri, expand):
    S = proj.shape[0]
    T = SSD_T
    nsteps = S // T
    ncl = T // CHUNK

    def body(zb_ref, xbc_ref, halo_ref, dt_ref, cw_ref, cb_ref, dtb_ref, alog_ref, dexp_ref, nw_ref, tri_ref, exp_ref,
             y_ref, yb_ref, st_ref, ht_ref, ext_ref):
        i = pl.program_id(0)

        @pl.when(i == 0)
        def _():
            ht_ref[...] = jnp.zeros_like(ht_ref)
            ext_ref[0:HALO, :] = jnp.zeros((HALO, XBC_W), F32)

        @pl.when(i > 0)
        def _():
            ext_ref[0:HALO, :] = halo_ref[...]

        ext_ref[HALO:HALO + T, :] = xbc_ref[...]
        diag, causal, lo, hi = _pair_masks()
        for c in range(ncl):
            q = _ssd_chunk_fwd(c, ext_ref, dt_ref, cw_ref, cb_ref, dtb_ref, alog_ref, tri_ref, exp_ref, ht_ref)
            rows = pl.ds(c * CHUNK, CHUNK)
            xc, E, dtE = q["xc"], q["E"], q["dtE"]
            xs = xc[:, 0:D]
            total = E[CHUNK - 1:CHUNK, :]
            x_dt = xs * dtE
            eE = jnp.exp(E)
            xw = x_dt * jnp.exp(total - E)
            st_ref[c] = ht_ref[...]
            for g in range(SSD_GROUPS):
                gc = slice(g * GROUP_W, (g + 1) * GROUP_W)
                Bg = xc[:, D + g * STATE:D + (g + 1) * STATE].astype(BF16)
                Cg = xc[:, D + SSD_GROUPS * STATE + g * STATE:D + SSD_GROUPS * STATE + (g + 1) * STATE].astype(BF16)
                cb2 = _dot_nt(Cg, jnp.concatenate([Bg, Bg], axis=0))
                htg = ht_ref[:, gc]
                y_ref[rows, gc] = eE[:, gc] * _dot(Cg, htg.astype(BF16)) + xs[:, gc] * dexp_ref[:, gc]
                for jj in range(GROUP_W // LANE):
                    pc = slice(g * GROUP_W + jj * LANE, g * GROUP_W + (jj + 1) * LANE)
                    Ej = E[:, pc]
                    e2 = jnp.sum(Ej * diag, axis=0, keepdims=True)
                    Mp = cb2 * jnp.exp(jnp.where(causal, Ej - e2, -1e30))
                    xj = x_dt[:, pc]
                    xbd = jnp.concatenate([xj * lo, xj * hi], axis=0).astype(BF16)
                    y_ref[rows, pc] += _dot(Mp.astype(BF16), xbd)
                ht_ref[:, gc] = jnp.exp(total[:, gc]) * htg + _dot_tn(Bg, xw[:, gc].astype(BF16))
            zb = zb_ref[rows, :]
            hh = y_ref[rows, :] * (zb * _sigmoid(zb))
            for g in range(SSD_GROUPS):
                gc = slice(g * GROUP_W, (g + 1) * GROUP_W)
                hg = hh[:, gc]
                r = lax.rsqrt(jnp.mean(hg * hg, axis=-1, keepdims=True) + EPS)
                yb_ref[rows, gc] = hg * r * nw_ref[:, gc]

    full = lambda a: pl.BlockSpec(a.shape, lambda i: (0,) * a.ndim)
    hb = T // HALO
    return pl.pallas_call(
        body, name="ssd_fwd", grid=(nsteps,),
        in_specs=[pl.BlockSpec((T, D), lambda i: (i, OFF_ZB // D)),
                  pl.BlockSpec((T, XBC_W), lambda i: (i, OFF_XBC // XBC_W)),
                  pl.BlockSpec((HALO, XBC_W), lambda i: (jnp.maximum(i * hb - 1, 0), OFF_XBC // XBC_W)),
                  pl.BlockSpec((T, DT_W), lambda i: (i, OFF_DT // DT_W)),
                  full(conv_w), full(conv_b), full(dtb_p), full(alog_p), full(d_exp), full(norm_w), full(tri), full(expand)],
        out_specs=[pl.BlockSpec((T, D), lambda i: (i, 0)), pl.BlockSpec((T, D), lambda i: (i, 0)),
                   pl.BlockSpec((ncl, STATE, D), lambda i: (i, 0, 0))],
        out_shape=[jax.ShapeDtypeStruct((S, D), F32), jax.ShapeDtypeStruct((S, D), F32),
                   jax.ShapeDtypeStruct((S // CHUNK, STATE, D), F32)],
        scratch_shapes=[pltpu.VMEM((STATE, D), F32), pltpu.VMEM((HALO + T, XBC_W), F32)],
        compiler_params=_cp(("arbitrary",)),
    )(proj, proj, proj, proj, conv_w, conv_b, dtb_p, alog_p, d_exp, norm_w, tri, expand)


def _ssd_bwd(proj, dyb, y, states, conv_w, conv_b, dtb_p, alog_p, d_exp, norm_w, tri, triT, expand, expandT):
    S = proj.shape[0]
    T = SSD_T
    nsteps = S // T
    ncl = T // CHUNK
    SSD_W = SSD_PAD_W

    def body(zb_ref, xbc_ref, halo_ref, dt_ref, dyb_ref, y_ref, st_ref, cw_ref, cb_ref, dtb_ref, alog_ref, dexp_ref, nw_ref,
             tri_ref, triT_ref, exp_ref, expT_ref,
             dp_ref, dcw_ref, dcb_ref, ddtb_ref, dalog_ref, dD_ref, dnw_ref,
             dht_ref, ext_ref, dpre_ref, dy_s, dE_s, dxdt_s, dxc_s, dDacc_ref, dAacc_ref):
        i = pl.program_id(0)

        @pl.when(i == 0)
        def _():
            for r in (dht_ref, dcw_ref, dcb_ref, ddtb_ref, dnw_ref, dDacc_ref, dAacc_ref):
                r[...] = jnp.zeros_like(r)
            dpre_ref[T:T + HALO, :] = jnp.zeros((HALO, XBC_W), F32)

        @pl.when(i == nsteps - 1)
        def _():
            ext_ref[0:HALO, :] = jnp.zeros((HALO, XBC_W), F32)

        @pl.when(i < nsteps - 1)
        def _():
            ext_ref[0:HALO, :] = halo_ref[...]

        ext_ref[HALO:HALO + T, :] = xbc_ref[...]
        diag, causal, lo, hi = _pair_masks()
        last_row = (lax.broadcasted_iota(jnp.int32, (CHUNK, 1), 0) == CHUNK - 1).astype(F32)
        for c in reversed(range(ncl)):
            q = _ssd_chunk_fwd(c, ext_ref, dt_ref, cw_ref, cb_ref, dtb_ref, alog_ref, tri_ref, exp_ref, None)
            rows = pl.ds(c * CHUNK, CHUNK)
            pre, sg, xc, dtr, dtv, A, E, dtE = (q[k] for k in ("pre", "sg", "xc", "dtr", "dtv", "A", "E", "dtE"))
            xs = xc[:, 0:D]
            total = E[CHUNK - 1:CHUNK, :]
            x_dt = xs * dtE
            eE = jnp.exp(E)
            wdec = jnp.exp(total - E)
            zb = zb_ref[rows, :]
            yv = y_ref[rows, :]
            sgz = _sigmoid(zb)
            sz = zb * sgz
            hh = yv * sz
            for g in range(SSD_GROUPS):
                gc = slice(g * GROUP_W, (g + 1) * GROUP_W)
                hg = hh[:, gc]
                r = lax.rsqrt(jnp.mean(hg * hg, axis=-1, keepdims=True) + EPS)
                dyb_g = dyb_ref[rows, gc]
                dn = dyb_g * nw_ref[:, gc]
                dnw_ref[0:1, gc] += jnp.sum(dyb_g * hg * r, axis=0, keepdims=True)
                dy_s[:, gc] = r * dn - hg * (r * r * r) * jnp.mean(dn * hg, axis=-1, keepdims=True)
            dhh = dy_s[...]
            dp_ref[rows, 0:D] = (dhh * yv * (sgz * (1.0 + zb * (1.0 - sgz)))).astype(BF16)
            dy = dhh * sz
            dy_s[...] = dy
            dDacc_ref[0:1, :] += jnp.sum(dy * xs, axis=0, keepdims=True)
            dxc_s[:, 0:D] = dy * dexp_ref[...]
            for g in range(SSD_GROUPS):
                gc = slice(g * GROUP_W, (g + 1) * GROUP_W)
                bcol = slice(D + g * STATE, D + (g + 1) * STATE)
                ccol = slice(D + SSD_GROUPS * STATE + g * STATE, D + SSD_GROUPS * STATE + (g + 1) * STATE)
                Bg = xc[:, bcol].astype(BF16)
                Cg = xc[:, ccol].astype(BF16)
                B2 = jnp.concatenate([Bg, Bg], axis=0)
                cb2 = _dot_nt(Cg, B2)
                htg = st_ref[c, :, gc]
                htb = htg.astype(BF16)
                dhn = dht_ref[:, gc]
                dhnb = dhn.astype(BF16)
                dyg = dy[:, gc]
                eEg = eE[:, gc]
                wg = wdec[:, gc]
                xdg = x_dt[:, gc]
                CH = _dot(Cg, htb)
                dCHb = (dyg * eEg).astype(BF16)
                dC = _dot_nt(dCHb, htb)
                dl = jnp.exp(total[:, gc])
                dht_prev = _dot_tn(Cg, dCHb) + dl * dhn
                dtot = jnp.sum(dhn * htg, axis=0, keepdims=True) * dl
                dxw = _dot(Bg, dhnb)
                dB = _dot_nt((xdg * wg).astype(BF16), dhnb)
                dwd = dxw * xdg * wg
                dtot = dtot + jnp.sum(dwd, axis=0, keepdims=True)
                dE_s[:, gc] = dyg * eEg * CH - dwd + last_row * dtot
                dxdt_s[:, gc] = dxw * wg
                dcb2 = jnp.zeros((CHUNK, LANE), F32)
                for jj in range(GROUP_W // LANE):
                    pc = slice(g * GROUP_W + jj * LANE, g * GROUP_W + (jj + 1) * LANE)
                    Ej = E[:, pc]
                    e2 = jnp.sum(Ej * diag, axis=0, keepdims=True)
                    Lp = jnp.exp(jnp.where(causal, Ej - e2, -1e30))
                    Mp = cb2 * Lp
                    xj = x_dt[:, pc]
                    xbd = jnp.concatenate([xj * lo, xj * hi], axis=0).astype(BF16)
                    dyj = dy[:, pc].astype(BF16)
                    dMp = _dot_nt(dyj, xbd)
                    dxbd = _dot_tn(Mp.astype(BF16), dyj)
                    dxdt_s[:, pc] += dxbd[0:CHUNK, :] * lo + dxbd[CHUNK:2 * CHUNK, :] * hi
                    dcb2 = dcb2 + dMp * Lp
                    dseg = dMp * Mp
                    dE_s[:, pc] += dseg - diag * jnp.sum(dseg, axis=0, keepdims=True)
                dcb2b = dcb2.astype(BF16)
                dC = dC + _dot(dcb2b, B2)
                dB2 = _dot_tn(dcb2b, Cg)
                dB = dB + dB2[0:CHUNK, :] + dB2[CHUNK:2 * CHUNK, :]
                dxc_s[:, bcol] = dB
                dxc_s[:, ccol] = dC
                dht_ref[:, gc] = dht_prev
            dx_dt = dxdt_s[...]
            dxc_s[:, 0:D] += dx_dt * dtE
            red = _dot(jnp.concatenate([dE_s[...], dx_dt * xs], axis=0), expT_ref[...], precision=HI)
            da = _dot(triT_ref[...], red[0:CHUNK, :], precision=HI)
            ddtv = red[CHUNK:2 * CHUNK, :] + da * A
            dAacc_ref[0:1, :] += jnp.sum(da * dtv, axis=0, keepdims=True)
            ddtr = ddtv * _sigmoid(dtr)
            ddtb_ref[0:1, :] += jnp.sum(ddtr, axis=0, keepdims=True)
            dp_ref[rows, D + XBC_W:D + XBC_W + DT_W] = ddtr.astype(BF16)
            dpre = dxc_s[...] * (sg * (1.0 + pre * (1.0 - sg)))
            dpre_ref[rows, :] = dpre
            dcb_ref[0:1, :] += jnp.sum(dpre, axis=0, keepdims=True)
        dpre_t = dpre_ref[0:T, :]
        dxbc = jnp.zeros((T, XBC_W), F32)
        for k in range(CONV_K):
            dcw_ref[k:k + 1, :] += jnp.sum(dpre_t * ext_ref[pl.ds(HALO - (CONV_K - 1) + k, T), :], axis=0, keepdims=True)
            dxbc = dxbc + cw_ref[k:k + 1, :] * dpre_ref[pl.ds(CONV_K - 1 - k, T), :]
        dp_ref[:, D:D + XBC_W] = dxbc.astype(BF16)
        dp_ref[:, SEG_SSD[1]:SSD_W] = jnp.zeros((T, SSD_W - SEG_SSD[1]), BF16)
        dpre_ref[T:T + HALO, :] = dpre_ref[0:HALO, :]

        @pl.when(i == nsteps - 1)
        def _():
            dalog_ref[...] = dAacc_ref[...] * (-jnp.exp(alog_ref[...]))
            dD_ref[...] = _dot(dDacc_ref[...], expT_ref[...], precision=HI)

    full = lambda a: pl.BlockSpec(a.shape, lambda i: (0,) * a.ndim)
    hb = T // HALO
    rev = lambda i: nsteps - 1 - i
    acc = lambda w: pl.BlockSpec((8, w), lambda i: (0, 0))
    return pl.pallas_call(
        body, name="ssd_bwd", grid=(nsteps,),
        in_specs=[pl.BlockSpec((T, D), lambda i: (rev(i), OFF_ZB // D)),
                  pl.BlockSpec((T, XBC_W), lambda i: (rev(i), OFF_XBC // XBC_W)),
                  pl.BlockSpec((HALO, XBC_W), lambda i: (jnp.maximum(rev(i) * hb - 1, 0), OFF_XBC // XBC_W)),
                  pl.BlockSpec((T, DT_W), lambda i: (rev(i), OFF_DT // DT_W)),
                  pl.BlockSpec((T, D), lambda i: (rev(i), 0)), pl.BlockSpec((T, D), lambda i: (rev(i), 0)),
                  pl.BlockSpec((ncl, STATE, D), lambda i: (rev(i), 0, 0)),
                  full(conv_w), full(conv_b), full(dtb_p), full(alog_p), full(d_exp), full(norm_w),
                  full(tri), full(triT), full(expand), full(expandT)],
        out_specs=[pl.BlockSpec((T, SSD_W), lambda i: (rev(i), 0)),
                   acc(XBC_W), acc(XBC_W), acc(DT_W), acc(DT_W), acc(DT_W), acc(D)],
        out_shape=[jax.ShapeDtypeStruct((S, SSD_W), BF16),
                   jax.ShapeDtypeStruct((8, XBC_W), F32), jax.ShapeDtypeStruct((8, XBC_W), F32),
                   jax.ShapeDtypeStruct((8, DT_W), F32), jax.ShapeDtypeStruct((8, DT_W), F32),
                   jax.ShapeDtypeStruct((8, DT_W), F32), jax.ShapeDtypeStruct((8, D), F32)],
        scratch_shapes=[pltpu.VMEM((STATE, D), F32), pltpu.VMEM((HALO + T, XBC_W), F32), pltpu.VMEM((T + HALO, XBC_W), F32),
                        pltpu.VMEM((CHUNK, D), F32), pltpu.VMEM((CHUNK, D), F32), pltpu.VMEM((CHUNK, D), F32),
                        pltpu.VMEM((CHUNK, XBC_W), F32), pltpu.VMEM((8, D), F32), pltpu.VMEM((8, DT_W), F32)],
        compiler_params=_cp(("arbitrary",)),
    )(proj, proj, proj, proj, dyb, y, states, conv_w, conv_b, dtb_p, alog_p, d_exp, norm_w, tri, triT, expand, expandT)


def _head(x, ya, yb, proj, target, gate_b, wout, fw, *, tm):
    S = x.shape[0]

    def body(x_ref, ya_ref, yb_ref, gl0_ref, gl1_ref, t_ref, gb_ref, w_ref, fw_ref,
             dh_ref, dhb_ref, mb_ref, dya_ref, dyb_ref, dgl_ref, loss_ref, dfw_ref, dgb_ref):
        @pl.when(pl.program_id(0) == 0)
        def _():
            loss_ref[...] = jnp.zeros_like(loss_ref)
            dfw_ref[...] = jnp.zeros_like(dfw_ref)
            dgb_ref[...] = jnp.zeros_like(dgb_ref)

        ya_v = ya_ref[...]
        yb_v = yb_ref[...]
        g0 = _sigmoid(gl0_ref[...] + gb_ref[:, 0:D])
        g1 = _sigmoid(gl1_ref[...] + gb_ref[:, D:2 * D])
        mb = (g0 * ya_v + g1 * yb_v).astype(BF16)
        mb_ref[...] = mb
        h = x_ref[...] + _dot(mb, w_ref[...])
        r = lax.rsqrt(jnp.mean(h * h, axis=-1, keepdims=True) + EPS)
        hn = h * r
        err = hn * fw_ref[...] - t_ref[...]
        loss_ref[...] += 0.5 * jnp.sum(jnp.mean(err * err, axis=-1, keepdims=True))
        dyf = err * (1.0 / D)
        dfw_ref[0:1, :] += jnp.sum(dyf * hn, axis=0, keepdims=True)
        dhn = dyf * fw_ref[...]
        dh = r * (dhn - hn * jnp.mean(dhn * hn, axis=-1, keepdims=True))
        dh_ref[...] = dh
        dhb = dh.astype(BF16)
        dhb_ref[...] = dhb
        dm = _dot_nt(dhb, w_ref[...])
        dya_ref[...] = dm * g0
        dyb_ref[...] = dm * g1
        dgl0 = dm * ya_v * g0 * (1.0 - g0)
        dgl1 = dm * yb_v * g1 * (1.0 - g1)
        dgl_ref[:, 0:D] = dgl0.astype(BF16)
        dgl_ref[:, D:2 * D] = dgl1.astype(BF16)
        dgb_ref[0:1, 0:D] += jnp.sum(dgl0, axis=0, keepdims=True)
        dgb_ref[0:1, D:2 * D] += jnp.sum(dgl1, axis=0, keepdims=True)

    row = pl.BlockSpec((tm, D), lambda i: (i, 0))
    seg = lambda off: pl.BlockSpec((tm, D), lambda i: (i, off // D))
    full = lambda a: pl.BlockSpec(a.shape, lambda i: (0,) * a.ndim)
    acc = lambda w: pl.BlockSpec((8, w), lambda i: (0, 0))
    return pl.pallas_call(
        body, name="head", grid=(S // tm,),
        in_specs=[row, row, row, seg(OFF_G0), seg(OFF_G1), row, full(gate_b), full(wout), full(fw)],
        out_specs=[row, row, row, row, row, pl.BlockSpec((tm, 2 * D), lambda i: (i, 0)), acc(LANE), acc(D), acc(2 * D)],
        out_shape=[jax.ShapeDtypeStruct((S, D), F32), jax.ShapeDtypeStruct((S, D), BF16), jax.ShapeDtypeStruct((S, D), BF16),
                   jax.ShapeDtypeStruct((S, D), F32), jax.ShapeDtypeStruct((S, D), F32), jax.ShapeDtypeStruct((S, 2 * D), BF16),
                   jax.ShapeDtypeStruct((8, LANE), F32), jax.ShapeDtypeStruct((8, D), F32), jax.ShapeDtypeStruct((8, 2 * D), F32)],
        compiler_params=_cp(("arbitrary",)),
    )(x, ya, yb, proj, proj, target, gate_b, wout, fw)


def _adam_update(g, w_ref, m_ref, v_ref, g_ref, d_ref, m2_ref, v2_ref):
    m2 = ADAM_B1 * m_ref[...] + (1.0 - ADAM_B1) * g
    v2 = ADAM_B2 * v_ref[...] + (1.0 - ADAM_B2) * (g * g)
    m_hat = m2 / (1.0 - ADAM_B1 ** ADAM_STEP)
    v_hat = v2 / (1.0 - ADAM_B2 ** ADAM_STEP)
    g_ref[...] = g
    d_ref[...] = -ADAM_LR * (m_hat / (jnp.sqrt(v_hat) + ADAM_EPS) + ADAM_WD * w_ref[...])
    m2_ref[...] = m2
    v2_ref[...] = v2


def _adamw_own(me, own, landed, w, m, v, *, tr, tc, name):
    _, R, C = landed.shape
    assert R % tr == 0 and C % tc == 0, (name, R, C, tr, tc)

    def body(me_ref, own_ref, p_ref, w_ref, m_ref, v_ref, g_ref, d_ref, m2_ref, v2_ref):
        mine = own_ref[0].astype(F32)
        g = jnp.where(me_ref[0] == 0, mine, p_ref[0].astype(F32))
        for k in range(1, N_DEV):
            g = g + jnp.where(me_ref[0] == k, mine, p_ref[k].astype(F32))
        _adam_update(g, w_ref, m_ref, v_ref, g_ref, d_ref, m2_ref, v2_ref)

    tile = pl.BlockSpec((tr, tc), lambda i, j, me_ref: (i, j))
    return pl.pallas_call(
        body, name=name,
        grid_spec=pltpu.PrefetchScalarGridSpec(
            num_scalar_prefetch=1, grid=(R // tr, C // tc),
            in_specs=[pl.BlockSpec((1, tr, tc), lambda i, j, me_ref: (me_ref[0], i, j)),
                      pl.BlockSpec((N_DEV, tr, tc), lambda i, j, me_ref: (0, i, j)), tile, tile, tile],
            out_specs=[tile, tile, tile, tile]),
        out_shape=[jax.ShapeDtypeStruct((R, C), F32)] * 4,
        compiler_params=_cp(("parallel", "parallel")),
    )(me, own, landed, w, m, v)


def _adamw(parts, w, m, v, *, tr, name):
    _, R, C = parts.shape
    assert R % tr == 0, (name, R, tr)

    def body(p_ref, w_ref, m_ref, v_ref, g_ref, d_ref, m2_ref, v2_ref):
        g = p_ref[0].astype(F32)
        for k in range(1, N_DEV):
            g = g + p_ref[k].astype(F32)
        _adam_update(g, w_ref, m_ref, v_ref, g_ref, d_ref, m2_ref, v2_ref)

    row = pl.BlockSpec((tr, C), lambda i: (i, 0))
    return pl.pallas_call(
        body, name=name, grid=(R // tr,),
        in_specs=[pl.BlockSpec((N_DEV, tr, C), lambda i: (0, i, 0)), row, row, row],
        out_specs=[row, row, row, row],
        out_shape=[jax.ShapeDtypeStruct((R, C), F32)] * 4,
        compiler_params=_cp(("parallel",)),
    )(parts, w, m, v)


def _place():
    x, y, c = lax.axis_index("x"), lax.axis_index("y"), lax.axis_index("c")
    return x, y, c


def _all_gather(arrs, *, name):
    n = len(arrs)

    def body(*refs):
        ins, outs = refs[:n], refs[n:2 * n]
        send_sems, recv_sems, local_sems = refs[2 * n:]
        x, y, c = _place()
        me, sibling = (x, y, c), (x, y, 1 - c)
        chips = [(1 - x, y), (x, 1 - y), (1 - x, 1 - y)]

        def idx(px, py, pc):
            return 4 * px + 2 * py + pc

        def copy(k, a, block, to, src=None):
            slab = outs[a].at[idx(*block)]
            return pltpu.make_async_remote_copy(
                src_ref=slab if src is None else src, dst_ref=slab,
                send_sem=send_sems.at[k, a], recv_sem=recv_sems.at[k, a], device_id=to, device_id_type=MESH)

        mine = [pltpu.make_async_copy(ins[a], outs[a].at[idx(*me)], local_sems.at[a]) for a in range(n)]
        for cp in mine:
            cp.start()
        first = []
        for a in range(n):
            first.append(copy(0, a, me, sibling, src=ins[a]))
            first += [copy(1 + j, a, me, (*chip, c), src=ins[a]) for j, chip in enumerate(chips)]
        for cp in first:
            cp.start()
        passed = []
        for j, chip in enumerate(chips):
            for a in range(n):
                copy(1 + j, a, (*chip, c), me).wait_recv()
                fwd = copy(4 + j, a, (*chip, c), sibling)
                fwd.start()
                passed.append(fwd)
        for a in range(n):
            copy(0, a, sibling, me).wait_recv()
            for j, chip in enumerate(chips):
                copy(4 + j, a, (*chip, 1 - c), me).wait_recv()
        for cp in first + passed:
            cp.wait_send()
        for cp in mine:
            cp.wait()

    anyspec = pl.BlockSpec(memory_space=pl.ANY)
    return pl.pallas_call(
        body, name=name,
        in_specs=[anyspec] * n, out_specs=[anyspec] * n,
        out_shape=[jax.ShapeDtypeStruct((N_DEV,) + a.shape, a.dtype) for a in arrs],
        scratch_shapes=[pltpu.SemaphoreType.DMA((7, n)), pltpu.SemaphoreType.DMA((7, n)), pltpu.SemaphoreType.DMA((n,))],
    )(*arrs)


_REL = [(dx, dy, dc) for dx in (0, 1) for dy in (0, 1) for dc in (0, 1)][1:]
_HBM = pl.BlockSpec(memory_space=pltpu.HBM)
_SEM = pl.BlockSpec(memory_space=pltpu.SEMAPHORE)
_EFFECT = pltpu.SideEffectType.DATAFLOW_SIDE_EFFECTING


def _peer(k):
    x, y, c = _place()
    dx, dy, dc = _REL[k]
    return (1 - x if dx else x, 1 - y if dy else y, 1 - c if dc else c)


def _exchange_start(parts, *, name):
    n = len(parts)

    def body(*refs):
        ins, lands = refs[:n], refs[n:2 * n]
        send_sems, recv_sems, token = refs[2 * n], refs[2 * n + 1], refs[-1]
        x, y, c = _place()
        me = 4 * x + 2 * y + c
        for a in range(n):
            for k in range(len(_REL)):
                px, py, pc = _peer(k)
                pltpu.make_async_remote_copy(
                    src_ref=ins[a].at[4 * px + 2 * py + pc], dst_ref=lands[a].at[me],
                    send_sem=send_sems.at[len(_REL) * a + k], recv_sem=recv_sems.at[len(_REL) * a + k],
                    device_id=(px, py, pc), device_id_type=MESH).start()
        token[...] = jnp.zeros_like(token)

    sem = pltpu.SemaphoreType.DMA((len(_REL) * n,))
    bufs = [pltpu.HBM(p.shape, p.dtype) for p in parts]
    outs = pl.pallas_call(
        body, name=name,
        out_shape=(sem, sem, *bufs, *bufs, jax.ShapeDtypeStruct((8, LANE), F32)),
        in_specs=(_HBM,) * (2 * n), out_specs=(_SEM, _SEM, *(_HBM,) * (2 * n), pl.BlockSpec(memory_space=pltpu.VMEM)),
        input_output_aliases={i: 2 + i for i in range(2 * n)},
        compiler_params=pltpu.CompilerParams(has_side_effects=_EFFECT),
    )(*[pltpu.with_memory_space_constraint(p, pltpu.HBM) for p in parts],
      *[pltpu.with_memory_space_constraint(lax.empty(p.shape, p.dtype), pltpu.HBM) for p in parts])
    return outs[0], outs[1], outs[2:2 + n], outs[2 + n:2 + 2 * n], outs[-1]


def _exchange_wait(send_sems, recv_sems, parts, lands, after, *, name):
    n = len(parts)

    def body(*refs):
        ins, lands_ = refs[:n], refs[n:2 * n]
        ssem, rsem = refs[2 * n], refs[2 * n + 1]
        for a in range(n):
            for k in range(len(_REL)):
                px, py, pc = _peer(k)
                p = 4 * px + 2 * py + pc
                cp = pltpu.make_async_remote_copy(
                    src_ref=ins[a].at[p], dst_ref=lands_[a].at[p],
                    send_sem=ssem.at[len(_REL) * a + k], recv_sem=rsem.at[len(_REL) * a + k],
                    device_id=(px, py, pc), device_id_type=MESH)
                cp.wait_send()
                cp.wait_recv()

    bufs = [pltpu.HBM(p.shape, p.dtype) for p in parts]
    outs = pl.pallas_call(
        body, name=name, out_shape=(*bufs, *bufs),
        in_specs=(*(_HBM,) * (2 * n), _SEM, _SEM, pl.BlockSpec(memory_space=pl.ANY)), out_specs=(_HBM,) * (2 * n),
        input_output_aliases={i: i for i in range(2 * n)},
        compiler_params=pltpu.CompilerParams(has_side_effects=_EFFECT),
    )(*parts, *lands, send_sems, recv_sems, after)
    return outs[:n], outs[n:]


WEIGHTS = ('norm_w', 'w_in', 'gate_b', 'sgu_norm_g', 'sgu_norm_b', 'sgu_w', 'sgu_b', 'conv_w', 'conv_b', 'dt_bias', 'A_log',
           'D_skip', 'ssd_norm_w', 'w_out', 'final_norm_w')
SHARDED = ('w_in', 'conv_w', 'w_out')
PACK_ROW = 8 * LANE


def _constants():
    tri = np.tril(np.ones((CHUNK, CHUNK), np.float32))
    expand = np.zeros((DT_W, D), np.float32)
    for h in range(HEADS):
        expand[h, h * HEADDIM:(h + 1) * HEADDIM] = 1.0
    sel = np.zeros((D, LANE), np.float32)
    for g in range(SGU_GROUPS):
        sel[g * LANE:(g + 1) * LANE, g] = 1.0
    pos_chunk = np.arange(SGU_BLOCK) // CHUNK
    mask = (pos_chunk[None, :] <= pos_chunk[:, None]).astype(np.float32)
    return dict(tri=jnp.asarray(tri), triT=jnp.asarray(tri.T.copy()), expand=jnp.asarray(expand),
                expandT=jnp.asarray(expand.T.copy()), sel=jnp.asarray(sel), mask=jnp.asarray(mask))


def _shard_rows(g, a, b):
    out = []
    for k in range(N_DEV):
        lo, hi = max(a, k * SHARD_IN), min(b, (k + 1) * SHARD_IN)
        if lo < hi:
            out.append(g[k, lo - k * SHARD_IN:hi - k * SHARD_IN])
    return out


def _permute_rows(g):
    return jnp.concatenate(_shard_rows(g, 0, 6144) + _shard_rows(g, 11296, W_IN) + _shard_rows(g, 6144, 11296)
                           + [jnp.zeros((DT_W - HEADS, D), g.dtype)], axis=0)


def _to_shards(segs):
    starts = np.cumsum([0] + [s.shape[0] for s in segs])
    assert starts[-1] == W_IN
    slabs = []
    for k in range(N_DEV):
        pieces = []
        for s, s0 in zip(segs, starts[:-1]):
            lo, hi = max(k * SHARD_IN, s0), min((k + 1) * SHARD_IN, s0 + s.shape[0])
            if lo < hi:
                pieces.append(s[lo - s0:hi - s0])
        slabs.append(jnp.concatenate(pieces, axis=0))
    return jnp.stack(slabs)


def _local_step(x2, tgt, wpT, wout, cw, p, exchange):
    S = x2.shape[0]
    k = _constants()
    xn = _norm_fwd(x2, p['norm_w'], tm=min(512, S))
    proj = _matmul(xn, wpT, trans_b=True, tm=min(1024, S), tn=1408, tk=D, name="in_proj")
    wm32 = p['sgu_w'][0] * k['mask']
    wm = wm32.astype(BF16)
    wmT = jnp.swapaxes(wm32, 1, 2).astype(BF16)
    bias_full = jnp.repeat(p['sgu_b'][0].T, LANE, axis=1)
    tm_sgu = min(256, S)
    ya = _sgu_fwd(proj, p['sgu_norm_g'], p['sgu_norm_b'], wm, bias_full, tm=tm_sgu)
    pad32 = lambda a: jnp.pad(a, ((0, 0), (0, DT_W - HEADS)))
    dtb_p, alog_p = pad32(p['dt_bias']), pad32(p['A_log'])
    d_exp = jnp.repeat(p['D_skip'], HEADDIM, axis=1)
    ssd_args = (cw, p['conv_b'], dtb_p, alog_p, d_exp, p['ssd_norm_w'])
    y, yb, states = _ssd_fwd(proj, *ssd_args, k['tri'], k['expand'])
    dh, dhb, mb, dya, dyb, dgl, loss, dfw, dgb = _head(
        x2, ya, yb, proj, tgt, p['gate_b'], wout, p['final_norm_w'][None, :], tm=min(128, S))
    dsgu, dws, dbsT, dsg, dsb = _sgu_bwd(proj, dya, p['sgu_norm_g'], p['sgu_norm_b'], wm, wmT, bias_full, k['mask'], k['sel'],
                                         tm=tm_sgu)
    dssd, dcw, dcb, ddtb, dalog, dD, dnw = _ssd_bwd(proj, dyb, y, states, *ssd_args, k['tri'], k['triT'], k['expand'], k['expandT'])
    tk = min(2048, S)
    tn = 1024
    dwT_sgu = _matmul(dsgu, xn, trans_a=True, out_dtype=BF16, tm=1024, tn=tn, tk=tk, name="dw_in_sgu")
    dwT_gate = _matmul(dgl, xn, trans_a=True, out_dtype=BF16, tm=1024, tn=tn, tk=tk, name="dw_in_gate")
    dwT_ssd = _matmul(dssd, xn, trans_a=True, out_dtype=BF16, tm=1024, tn=tn, tk=tk, name="dw_in_ssd")
    dw_out = _matmul(mb, dhb, trans_a=True, out_dtype=BF16, tm=1024, tn=tn, tk=tk, name="dw_out")
    token = exchange([dwT_sgu, dwT_ssd[:W_IN - SEG_SSD[0]], dwT_gate], dw_out)
    tm = min(1024, S)
    wpT_ssd = jnp.pad(wpT[SEG_SSD[0]:], ((0, SSD_PAD_W - SEG_SSD[1]), (0, 0)))
    dxn = _matmul(dsgu, wpT, tm=tm, tn=tn, tk=2048, after=token, name="dxn_sgu")
    dxn = _matmul(dgl, wpT, b_koff=SEG_GATE[0] // 2048, tm=tm, tn=tn, tk=2048, add=dxn, name="dxn_gate")
    dxn = _matmul(dssd, wpT_ssd, tm=tm, tn=tn, tk=2048, add=dxn, name="dxn_ssd")
    grad_x, dnorm = _norm_bwd(x2, p['norm_w'], dxn, dh, tm=min(256, S))
    grads = dict(
        norm_w=dnorm[0:1], gate_b=dgb[0:1], sgu_norm_g=dsg[0:1], sgu_norm_b=dsb[0:1], sgu_w=dws[None],
        sgu_b=dbsT[:, :SGU_GROUPS].T[None], conv_w=dcw[0:CONV_K][None], conv_b=dcb[0:1], dt_bias=ddtb[0:1, :HEADS],
        A_log=dalog[0:1, :HEADS], D_skip=dD[0:1, :HEADS], ssd_norm_w=dnw[0:1], final_norm_w=dfw[0])
    return loss[0, 0], grad_x, grads


def _pack(arrs):
    rows, offs, r = [], [], 0
    for a in arrs:
        n = a.size
        nr = -(-n // PACK_ROW) * 8
        rows.append(jnp.pad(a.reshape(-1).astype(F32), (0, nr * LANE - n)).reshape(nr, LANE))
        offs.append(r)
        r += nr
    return jnp.concatenate(rows, axis=0), offs


def kernel(x, norm_w, w_in, gate_b, sgu_norm_g, sgu_norm_b, sgu_w, sgu_b, conv_w, conv_b, dt_bias, A_log, D_skip, ssd_norm_w, w_out, final_norm_w, loss_target, m_norm_w, m_w_in, m_gate_b, m_sgu_norm_g, m_sgu_norm_b, m_sgu_w, m_sgu_b, m_conv_w, m_conv_b, m_dt_bias, m_A_log, m_D_skip, m_ssd_norm_w, m_w_out, m_final_norm_w, v_norm_w, v_w_in, v_gate_b, v_sgu_norm_g, v_sgu_norm_b, v_sgu_w, v_sgu_b, v_conv_w, v_conv_b, v_dt_bias, v_A_log, v_D_skip, v_ssd_norm_w, v_w_out, v_final_norm_w):
    w = dict(norm_w=norm_w, w_in=w_in, gate_b=gate_b, sgu_norm_g=sgu_norm_g, sgu_norm_b=sgu_norm_b, sgu_w=sgu_w, sgu_b=sgu_b,
             conv_w=conv_w, conv_b=conv_b, dt_bias=dt_bias, A_log=A_log, D_skip=D_skip, ssd_norm_w=ssd_norm_w, w_out=w_out,
             final_norm_w=final_norm_w)
    m = dict(norm_w=m_norm_w, w_in=m_w_in, gate_b=m_gate_b, sgu_norm_g=m_sgu_norm_g, sgu_norm_b=m_sgu_norm_b, sgu_w=m_sgu_w,
             sgu_b=m_sgu_b, conv_w=m_conv_w, conv_b=m_conv_b, dt_bias=m_dt_bias, A_log=m_A_log, D_skip=m_D_skip,
             ssd_norm_w=m_ssd_norm_w, w_out=m_w_out, final_norm_w=m_final_norm_w)
    v = dict(norm_w=v_norm_w, w_in=v_w_in, gate_b=v_gate_b, sgu_norm_g=v_sgu_norm_g, sgu_norm_b=v_sgu_norm_b, sgu_w=v_sgu_w,
             sgu_b=v_sgu_b, conv_w=v_conv_w, conv_b=v_conv_b, dt_bias=v_dt_bias, A_log=v_A_log, D_skip=v_D_skip,
             ssd_norm_w=v_ssd_norm_w, w_out=v_w_out, final_norm_w=v_final_norm_w)
    me = 4 * lax.axis_index("x") + 2 * lax.axis_index("y") + lax.axis_index("c")
    shard_cw = XBC_W // N_DEV

    tpose = lambda a: jnp.swapaxes(a[0], 0, 1)
    g_in, g_out, g_cw = _all_gather([tpose(w_in).astype(BF16), w_out[0].astype(BF16), conv_w[0]], name="gather_weights")
    wpT = _permute_rows(g_in)
    wout_full = g_out.reshape(D, D)
    cw_full = jnp.swapaxes(g_cw, 0, 1).reshape(CONV_K, XBC_W)

    flight = {}

    def exchange(dw_inT_segs, dw_out):
        parts = [_to_shards(dw_inT_segs), dw_out.reshape(N_DEV, D // N_DEV, D)]
        flight['sems'], flight['rsems'], flight['parts'], flight['lands'], token = _exchange_start(parts, name="exchange_start")
        return token

    loss_part, grad_x, grads = _local_step(x[0], loss_target[0], wpT, wout_full, cw_full, w, exchange)
    (own_in, own_out), (land_in, land_out) = _exchange_wait(
        flight['sems'], flight['rsems'], flight['parts'], flight['lands'], grad_x, name="exchange_wait")
    me_arr = jnp.reshape(me, (1,)).astype(jnp.int32)
    res = {}
    res['w_in'] = [jnp.swapaxes(o, 0, 1) for o in _adamw_own(
        me_arr, own_in, land_in, tpose(w_in), tpose(m_w_in), tpose(v_w_in), tr=SHARD_IN, tc=256, name="adamw_w_in")]
    res['w_out'] = _adamw_own(me_arr, own_out, land_out, w_out[0], m_w_out[0], v_w_out[0], tr=128, tc=D, name="adamw_w_out")

    small = [n for n in WEIGHTS if n not in SHARDED]
    packed, offs = _pack([grads[n] for n in small] + [loss_part, grads['conv_w']])
    (gathered,) = _all_gather([packed], name="gather_small")
    off_loss, off_cw = offs[-2], offs[-1]
    cw_parts = gathered[:, off_cw:, :].reshape(N_DEV, CONV_K, XBC_W)
    cw_parts = lax.dynamic_slice_in_dim(cw_parts, me * shard_cw, shard_cw, axis=2)
    cw_rows = _pack([cw_parts[0]])[0].shape[0]
    cw_parts = jnp.pad(cw_parts.reshape(N_DEV, -1), ((0, 0), (0, cw_rows * LANE - CONV_K * shard_cw))).reshape(N_DEV, cw_rows, LANE)
    parts = jnp.concatenate([gathered[:, :off_cw, :], cw_parts], axis=1)
    zero = jnp.zeros((), F32)
    packs = [_pack([d[n] for n in small] + [zero, d['conv_w']])[0] for d in (w, m, v)]
    outs = _adamw(parts, *packs, tr=parts.shape[1], name="adamw_small")

    def unpack(o, name):
        if name == 'conv_w':
            return o[off_cw:off_cw + cw_rows].reshape(-1)[:CONV_K * shard_cw].reshape(w['conv_w'].shape)
        r0 = offs[small.index(name)]
        n = w[name].size
        return o[r0:r0 + -(-n // PACK_ROW) * 8].reshape(-1)[:n].reshape(w[name].shape)

    for n in small + ['conv_w']:
        res[n] = [unpack(o, n) for o in outs]
    for n in ('w_in', 'w_out'):
        res[n] = [o[None] for o in res[n]]
    loss = outs[0][off_loss, 0]
    return (loss, grad_x[None], *[res[n][0] for n in WEIGHTS], *[res[n][1] for n in WEIGHTS],
            *[res[n][2] for n in WEIGHTS], *[res[n][3] for n in WEIGHTS])
```

```python
import functools

import numpy as np
import jax
import jax.numpy as jnp
from jax import lax
from jax.experimental import pallas as pl
from jax.experimental.pallas import tpu as pltpu

F32 = jnp.float32
BF16 = jnp.bfloat16
HI = lax.Precision.HIGHEST
MESH = pl.DeviceIdType.MESH

D = 2048
EPS = 1e-5
SGU_BLOCK = 128
SGU_GROUPS = 16
CHUNK = 64
HEADS = 32
HEADDIM = 64
SSD_GROUPS = 4
GROUP_W = D // SSD_GROUPS
STATE = 128
CONV_K = 4
XBC_W = D + 2 * SSD_GROUPS * STATE
W_IN = 15392
N_DEV = 8
SHARD_IN = W_IN // N_DEV
ADAM_LR, ADAM_B1, ADAM_B2, ADAM_EPS, ADAM_WD, ADAM_STEP = 0.001, 0.9, 0.999, 1e-08, 0.01, 10

LANE = 128
DT_W = LANE
OFF_U, OFF_V, OFF_ZA, OFF_G0, OFF_G1, OFF_ZB, OFF_XBC, OFF_DT = 0, 2048, 4096, 6144, 8192, 10240, 12288, 15360
WP = OFF_DT + DT_W
SEG_SGU = (0, 6144)
SEG_GATE = (6144, 4096)
SEG_SSD = (10240, WP - 10240)
SSD_PAD_W = 6144
VMEM_LIMIT = 56 * 1024 * 1024


def _cp(sem=None, vmem=VMEM_LIMIT):
    return pltpu.CompilerParams(dimension_semantics=sem, vmem_limit_bytes=vmem)


def _sigmoid(x):
    return 1.0 / (1.0 + jnp.exp(-x))


def _softplus(x):
    return jnp.maximum(x, 0.0) + jnp.log(1.0 + jnp.exp(-jnp.abs(x)))


def _dot(a, b, precision=None):
    return jnp.dot(a, b, preferred_element_type=F32, precision=precision)


def _dot_nt(a, b, precision=None):
    return lax.dot_general(a, b, (((1,), (1,)), ((), ())), preferred_element_type=F32, precision=precision)


def _dot_tn(a, b, precision=None):
    return lax.dot_general(a, b, (((0,), (0,)), ((), ())), preferred_element_type=F32, precision=precision)


def _split3(a):
    hi = a.astype(BF16)
    r = a - hi.astype(F32)
    mid = r.astype(BF16)
    return hi, mid, (r - mid.astype(F32)).astype(BF16)


def _sel_right(a, sel01):
    m = a.shape[0]
    r = _dot(jnp.concatenate(_split3(a), axis=0), sel01)
    return (r[0:m] + r[m:2 * m]) + r[2 * m:3 * m]


def _sel_left(sel01, a):
    n = a.shape[1]
    r = _dot(sel01, jnp.concatenate(_split3(a), axis=1))
    return (r[:, 0:n] + r[:, n:2 * n]) + r[:, 2 * n:3 * n]


def _matmul(a, b, *, trans_a=False, trans_b=False, b_koff=0, out_dtype=F32, tm, tn, tk, add=None, after=None, name):
    K, M = a.shape if trans_a else a.shape[::-1]
    N = b.shape[0] if trans_b else b.shape[1]
    assert M % tm == 0 and N % tn == 0 and K % tk == 0 and not (trans_a and trans_b), (name, M, N, K, tm, tn, tk)
    nk = K // tk

    def body(*refs):
        a_ref, b_ref = refs[:2]
        add_ref = refs[2] if add is not None else None
        o_ref, acc_ref = refs[-2:]
        k = pl.program_id(2)
        if trans_a:
            part = _dot_tn(a_ref[...], b_ref[...])
        else:
            part = _dot_nt(a_ref[...], b_ref[...]) if trans_b else _dot(a_ref[...], b_ref[...])

        def result(r):
            if add_ref is not None:
                r = r + add_ref[...]
            return r.astype(out_dtype)

        if nk == 1:
            o_ref[...] = result(part)
        else:
            @pl.when(k == 0)
            def _():
                acc_ref[...] = part

            @pl.when(jnp.logical_and(k > 0, k < nk - 1))
            def _():
                acc_ref[...] += part

            @pl.when(k == nk - 1)
            def _():
                o_ref[...] = result(acc_ref[...] + part)

    in_specs = [pl.BlockSpec((tk, tm), lambda i, j, k: (k, i)) if trans_a else pl.BlockSpec((tm, tk), lambda i, j, k: (i, k)),
                pl.BlockSpec((tn, tk), lambda i, j, k: (j, k)) if trans_b else pl.BlockSpec((tk, tn), lambda i, j, k: (k + b_koff, j))]
    args = [a, b]
    if add is not None:
        in_specs.append(pl.BlockSpec((tm, tn), lambda i, j, k: (i, j)))
        args.append(add)
    if after is not None:
        in_specs.append(pl.BlockSpec(memory_space=pl.ANY))
        args.append(after)
    return pl.pallas_call(
        body, name=name, grid=(M // tm, N // tn, nk), in_specs=in_specs,
        out_specs=pl.BlockSpec((tm, tn), lambda i, j, k: (i, j)),
        out_shape=jax.ShapeDtypeStruct((M, N), out_dtype),
        scratch_shapes=[pltpu.VMEM((tm, tn), F32)],
        compiler_params=_cp(("parallel", "parallel", "arbitrary")),
    )(*args)


def _norm_fwd(x, w, *, tm):
    S = x.shape[0]

    def body(x_ref, w_ref, o_ref):
        xv = x_ref[...]
        r = lax.rsqrt(jnp.mean(xv * xv, axis=-1, keepdims=True) + EPS)
        o_ref[...] = (xv * r * w_ref[...]).astype(BF16)

    return pl.pallas_call(
        body, name="norm_fwd", grid=(S // tm,),
        in_specs=[pl.BlockSpec((tm, D), lambda i: (i, 0)), pl.BlockSpec((1, D), lambda i: (0, 0))],
        out_specs=pl.BlockSpec((tm, D), lambda i: (i, 0)),
        out_shape=jax.ShapeDtypeStruct((S, D), BF16), compiler_params=_cp(("parallel",)),
    )(x, w)


def _norm_bwd(x, w, dxn, dh, *, tm):
    S = x.shape[0]

    def body(x_ref, w_ref, dxn_ref, dh_ref, gx_ref, dw_ref):
        xv = x_ref[...]
        r = lax.rsqrt(jnp.mean(xv * xv, axis=-1, keepdims=True) + EPS)
        xh = xv * r
        dxn_v = dxn_ref[...]
        dxh = dxn_v * w_ref[...]
        gx_ref[...] = dh_ref[...] + r * (dxh - xh * jnp.mean(dxh * xh, axis=-1, keepdims=True))

        @pl.when(pl.program_id(0) == 0)
        def _():
            dw_ref[...] = jnp.zeros_like(dw_ref)

        dw_ref[0:1, :] += jnp.sum(dxn_v * xh, axis=0, keepdims=True)

    row = pl.BlockSpec((tm, D), lambda i: (i, 0))
    return pl.pallas_call(
        body, name="norm_bwd", grid=(S // tm,),
        in_specs=[row, pl.BlockSpec((1, D), lambda i: (0, 0)), row, row],
        out_specs=[row, pl.BlockSpec((8, D), lambda i: (0, 0))],
        out_shape=[jax.ShapeDtypeStruct((S, D), F32), jax.ShapeDtypeStruct((8, D), F32)],
        compiler_params=_cp(("arbitrary",)),
    )(x, w, dxn, dh)


def _sgu_core(u_ref, v_ref, z_ref, g_ref, b_ref, wm_ref, bias_ref, vnb_ref, mixed_ref, tm):
    v = v_ref[...].astype(F32)
    mu = jnp.mean(v, axis=-1, keepdims=True)
    vc = v - mu
    rs = lax.rsqrt(jnp.mean(vc * vc, axis=-1, keepdims=True) + EPS)
    vh = vc * rs
    vnb_ref[...] = (vh * g_ref[...] + b_ref[...]).astype(BF16)
    for blk in range(tm // SGU_BLOCK):
        rows = pl.ds(blk * SGU_BLOCK, SGU_BLOCK)
        for gi in range(SGU_GROUPS):
            cols = pl.ds(gi * LANE, LANE)
            mixed_ref[rows, cols] = _dot(wm_ref[gi], vnb_ref[rows, cols]) + bias_ref[:, cols]
    return vh, rs


def _sgu_fwd(proj, g, b, wm, bias_full, *, tm):
    S = proj.shape[0]

    def body(u_ref, v_ref, z_ref, g_ref, b_ref, wm_ref, bias_ref, y_ref, vnb_ref, mixed_ref):
        _sgu_core(u_ref, v_ref, z_ref, g_ref, b_ref, wm_ref, bias_ref, vnb_ref, mixed_ref, tm)
        z = z_ref[...].astype(F32)
        y_ref[...] = (u_ref[...].astype(F32) * mixed_ref[...] * (z * _sigmoid(z))).astype(BF16)

    seg = lambda off: pl.BlockSpec((tm, D), lambda i: (i, off // D))
    full = lambda a: pl.BlockSpec(a.shape, lambda i: (0,) * a.ndim)
    return pl.pallas_call(
        body, name="sgu_fwd", grid=(S // tm,),
        in_specs=[seg(OFF_U), seg(OFF_V), seg(OFF_ZA), full(g), full(b), full(wm), full(bias_full)],
        out_specs=pl.BlockSpec((tm, D), lambda i: (i, 0)),
        out_shape=jax.ShapeDtypeStruct((S, D), BF16),
        scratch_shapes=[pltpu.VMEM((tm, D), BF16), pltpu.VMEM((tm, D), F32)],
        compiler_params=_cp(("parallel",)),
    )(proj, proj, proj, g, b, wm, bias_full)


def _sgu_bwd(proj, dy, g, b, wm, wmT, bias_full, mask, sel, *, tm):
    S = proj.shape[0]
    nsteps = S // tm

    def body(u_ref, v_ref, z_ref, dy_ref, g_ref, b_ref, wm_ref, wmT_ref, bias_ref, mask_ref, sel_ref,
             dp_ref, dws_ref, dbs_ref, dg_ref, db_ref, vnb_ref, mixed_ref, dmb_ref, dvn_ref, dbias_ref):
        i = pl.program_id(0)

        @pl.when(i == 0)
        def _():
            dws_ref[...] = jnp.zeros_like(dws_ref)
            dg_ref[...] = jnp.zeros_like(dg_ref)
            db_ref[...] = jnp.zeros_like(db_ref)
            dbias_ref[...] = jnp.zeros_like(dbias_ref)

        vh, rs = _sgu_core(u_ref, v_ref, z_ref, g_ref, b_ref, wm_ref, bias_ref, vnb_ref, mixed_ref, tm)
        u = u_ref[...].astype(F32)
        z = z_ref[...].astype(F32)
        dy_v = dy_ref[...].astype(F32)
        mixed = mixed_ref[...]
        sg = _sigmoid(z)
        sz = z * sg
        dp_ref[:, 0:D] = (dy_v * mixed * sz).astype(BF16)
        dp_ref[:, 2 * D:3 * D] = (dy_v * u * mixed * (sg * (1.0 + z * (1.0 - sg)))).astype(BF16)
        dmixed = dy_v * u * sz
        dmb_ref[...] = dmixed.astype(BF16)
        for blk in range(tm // SGU_BLOCK):
            dbias_ref[...] += dmixed[blk * SGU_BLOCK:(blk + 1) * SGU_BLOCK, :]
        for blk in range(tm // SGU_BLOCK):
            rows = pl.ds(blk * SGU_BLOCK, SGU_BLOCK)
            for gi in range(SGU_GROUPS):
                cols = pl.ds(gi * LANE, LANE)
                dm = dmb_ref[rows, cols]
                dvn_ref[rows, cols] = _dot(wmT_ref[gi], dm)
                dws_ref[gi] += _dot_nt(dm, vnb_ref[rows, cols])
        dvn = dvn_ref[...]
        dg_ref[0:1, :] += jnp.sum(dvn * vh, axis=0, keepdims=True)
        db_ref[0:1, :] += jnp.sum(dvn, axis=0, keepdims=True)
        dvh = dvn * g_ref[...]
        dv = rs * (dvh - jnp.mean(dvh, axis=-1, keepdims=True) - vh * jnp.mean(dvh * vh, axis=-1, keepdims=True))
        dp_ref[:, D:2 * D] = dv.astype(BF16)

        @pl.when(i == nsteps - 1)
        def _():
            for gi in range(SGU_GROUPS):
                dws_ref[gi] = dws_ref[gi] * mask_ref[...]
            dbs_ref[...] = _dot(dbias_ref[...], sel_ref[...], precision=HI)

    seg = lambda off: pl.BlockSpec((tm, D), lambda i: (i, off // D))
    full = lambda a: pl.BlockSpec(a.shape, lambda i: (0,) * a.ndim)
    return pl.pallas_call(
        body, name="sgu_bwd", grid=(nsteps,),
        in_specs=[seg(OFF_U), seg(OFF_V), seg(OFF_ZA), pl.BlockSpec((tm, D), lambda i: (i, 0)),
                  full(g), full(b), full(wm), full(wmT), full(bias_full), full(mask), full(sel)],
        out_specs=[pl.BlockSpec((tm, 3 * D), lambda i: (i, 0)),
                   pl.BlockSpec((SGU_GROUPS, SGU_BLOCK, SGU_BLOCK), lambda i: (0, 0, 0)),
                   pl.BlockSpec((SGU_BLOCK, LANE), lambda i: (0, 0)),
                   pl.BlockSpec((8, D), lambda i: (0, 0)), pl.BlockSpec((8, D), lambda i: (0, 0))],
        out_shape=[jax.ShapeDtypeStruct((S, 3 * D), BF16),
                   jax.ShapeDtypeStruct((SGU_GROUPS, SGU_BLOCK, SGU_BLOCK), F32),
                   jax.ShapeDtypeStruct((SGU_BLOCK, LANE), F32),
                   jax.ShapeDtypeStruct((8, D), F32), jax.ShapeDtypeStruct((8, D), F32)],
        scratch_shapes=[pltpu.VMEM((tm, D), BF16), pltpu.VMEM((tm, D), F32), pltpu.VMEM((tm, D), BF16),
                        pltpu.VMEM((tm, D), F32), pltpu.VMEM((SGU_BLOCK, D), F32)],
        compiler_params=_cp(("arbitrary",)),
    )(proj, proj, proj, dy, g, b, wm, wmT, bias_full, mask, sel)


SSD_T = 2 * CHUNK
HALO = 8
HALO_BLK = 16


def _pair_masks():
    row = lax.broadcasted_iota(jnp.int32, (CHUNK, LANE), 0)
    lane = lax.broadcasted_iota(jnp.int32, (CHUNK, LANE), 1)
    pos = jnp.where(lane >= CHUNK, lane - CHUNK, lane)
    diag = (row == pos).astype(F32)
    causal = row >= pos
    lo = (lane < CHUNK).astype(F32)
    return diag, causal, lo, 1.0 - lo


def _ssd_chunk_fwd(c, ext_ref, dt_ref, cw_ref, cb_ref, dtb_ref, alog_ref, tri_ref, exp_ref, ht_ref):
    r0 = c * CHUNK
    pre = cb_ref[...] + sum(cw_ref[k:k + 1, :] * ext_ref[pl.ds(r0 + HALO - (CONV_K - 1) + k, CHUNK), :] for k in range(CONV_K))
    sg = _sigmoid(pre)
    xc = pre * sg
    dtr = dt_ref[pl.ds(r0, CHUNK), :].astype(F32) + dtb_ref[...]
    dtv = _softplus(dtr)
    A = -jnp.exp(alog_ref[...])
    acs = _sel_left(tri_ref[...], dtv * A)
    both = _sel_right(jnp.concatenate([acs, dtv], axis=0), exp_ref[...])
    E, dtE = both[0:CHUNK], both[CHUNK:2 * CHUNK]
    return dict(pre=pre, sg=sg, xc=xc, dtr=dtr, dtv=dtv, A=A, E=E, dtE=dtE)


def _ssd_fwd(proj, conv_w, conv_b, dtb_p, alog_p, d_exp, norm_w, tri, expand):
    S = proj.shape[0]
    T = SSD_T
    nsteps = S // T
    ncl = T // CHUNK

    def body(zb_ref, xbc_ref, halo_ref, dt_ref, cw_ref, cb_ref, dtb_ref, alog_ref, dexp_ref, nw_ref, tri_ref, exp_ref,
             y_ref, yb_ref, st_ref, ht_ref, ext_ref):
        i = pl.program_id(0)

        @pl.when(i == 0)
        def _():
            ht_ref[...] = jnp.zeros_like(ht_ref)
            ext_ref[0:HALO, :] = jnp.zeros((HALO, XBC_W), F32)

        @pl.when(i > 0)
        def _():
            ext_ref[0:HALO, :] = halo_ref[...].astype(F32)[HALO_BLK - HALO:HALO_BLK, :]

        ext_ref[HALO:HALO + T, :] = xbc_ref[...].astype(F32)
        diag, causal, lo, hi = _pair_masks()
        for c in range(ncl):
            q = _ssd_chunk_fwd(c, ext_ref, dt_ref, cw_ref, cb_ref, dtb_ref, alog_ref, tri_ref, exp_ref, ht_ref)
            rows = pl.ds(c * CHUNK, CHUNK)
            xc, E, dtE = q["xc"], q["E"], q["dtE"]
            xs = xc[:, 0:D]
            total = E[CHUNK - 1:CHUNK, :]
            x_dt = xs * dtE
            eE = jnp.exp(E)
            xw = x_dt * jnp.exp(total - E)
            st_ref[c] = ht_ref[...]
            for g in range(SSD_GROUPS):
                gc = slice(g * GROUP_W, (g + 1) * GROUP_W)
                Bg = xc[:, D + g * STATE:D + (g + 1) * STATE].astype(BF16)
                Cg = xc[:, D + SSD_GROUPS * STATE + g * STATE:D + SSD_GROUPS * STATE + (g + 1) * STATE].astype(BF16)
                cb2 = _dot_nt(Cg, jnp.concatenate([Bg, Bg], axis=0))
                htg = ht_ref[:, gc]
                y_ref[rows, gc] = eE[:, gc] * _dot(Cg, htg.astype(BF16)) + xs[:, gc] * dexp_ref[:, gc]
                for jj in range(GROUP_W // LANE):
                    pc = slice(g * GROUP_W + jj * LANE, g * GROUP_W + (jj + 1) * LANE)
                    Ej = E[:, pc]
                    e2 = jnp.sum(Ej * diag, axis=0, keepdims=True)
                    Mp = cb2 * jnp.exp(jnp.where(causal, Ej - e2, -1e30))
                    xj = x_dt[:, pc]
                    xbd = jnp.concatenate([xj * lo, xj * hi], axis=0).astype(BF16)
                    y_ref[rows, pc] += _dot(Mp.astype(BF16), xbd)
                ht_ref[:, gc] = jnp.exp(total[:, gc]) * htg + _dot_tn(Bg, xw[:, gc].astype(BF16))
            zb = zb_ref[rows, :].astype(F32)
            hh = y_ref[rows, :] * (zb * _sigmoid(zb))
            for g in range(SSD_GROUPS):
                gc = slice(g * GROUP_W, (g + 1) * GROUP_W)
                hg = hh[:, gc]
                r = lax.rsqrt(jnp.mean(hg * hg, axis=-1, keepdims=True) + EPS)
                yb_ref[rows, gc] = (hg * r * nw_ref[:, gc]).astype(BF16)

    full = lambda a: pl.BlockSpec(a.shape, lambda i: (0,) * a.ndim)
    hb = T // HALO_BLK
    return pl.pallas_call(
        body, name="ssd_fwd", grid=(nsteps,),
        in_specs=[pl.BlockSpec((T, D), lambda i: (i, OFF_ZB // D)),
                  pl.BlockSpec((T, XBC_W), lambda i: (i, OFF_XBC // XBC_W)),
                  pl.BlockSpec((HALO_BLK, XBC_W), lambda i: (jnp.maximum(i * hb - 1, 0), OFF_XBC // XBC_W)),
                  pl.BlockSpec((T, DT_W), lambda i: (i, OFF_DT // DT_W)),
                  full(conv_w), full(conv_b), full(dtb_p), full(alog_p), full(d_exp), full(norm_w), full(tri), full(expand)],
        out_specs=[pl.BlockSpec((T, D), lambda i: (i, 0)), pl.BlockSpec((T, D), lambda i: (i, 0)),
                   pl.BlockSpec((ncl, STATE, D), lambda i: (i, 0, 0))],
        out_shape=[jax.ShapeDtypeStruct((S, D), F32), jax.ShapeDtypeStruct((S, D), BF16),
                   jax.ShapeDtypeStruct((S // CHUNK, STATE, D), F32)],
        scratch_shapes=[pltpu.VMEM((STATE, D), F32), pltpu.VMEM((HALO + T, XBC_W), F32)],
        compiler_params=_cp(("arbitrary",)),
    )(proj, proj, proj, proj, conv_w, conv_b, dtb_p, alog_p, d_exp, norm_w, tri, expand)


def _ssd_bwd(proj, dyb, y, states, conv_w, conv_b, dtb_p, alog_p, d_exp, norm_w, tri, triT, expand, expandT):
    S = proj.shape[0]
    T = SSD_T
    nsteps = S // T
    ncl = T // CHUNK
    SSD_W = SSD_PAD_W

    def body(zb_ref, xbc_ref, halo_ref, dt_ref, dyb_ref, y_ref, st_ref, cw_ref, cb_ref, dtb_ref, alog_ref, dexp_ref, nw_ref,
             tri_ref, triT_ref, exp_ref, expT_ref,
             dp_ref, dcw_ref, dcb_ref, ddtb_ref, dalog_ref, dD_ref, dnw_ref,
             dht_ref, ext_ref, dpre_ref, dy_s, dE_s, dxdt_s, dxc_s, dDacc_ref, dAacc_ref):
        i = pl.program_id(0)

        @pl.when(i == 0)
        def _():
            for r in (dht_ref, dcw_ref, dcb_ref, ddtb_ref, dnw_ref, dDacc_ref, dAacc_ref):
                r[...] = jnp.zeros_like(r)
            dpre_ref[T:T + HALO, :] = jnp.zeros((HALO, XBC_W), F32)

        @pl.when(i == nsteps - 1)
        def _():
            ext_ref[0:HALO, :] = jnp.zeros((HALO, XBC_W), F32)

        @pl.when(i < nsteps - 1)
        def _():
            ext_ref[0:HALO, :] = halo_ref[...].astype(F32)[HALO_BLK - HALO:HALO_BLK, :]

        ext_ref[HALO:HALO + T, :] = xbc_ref[...].astype(F32)
        diag, causal, lo, hi = _pair_masks()
        last_row = (lax.broadcasted_iota(jnp.int32, (CHUNK, 1), 0) == CHUNK - 1).astype(F32)
        for c in reversed(range(ncl)):
            q = _ssd_chunk_fwd(c, ext_ref, dt_ref, cw_ref, cb_ref, dtb_ref, alog_ref, tri_ref, exp_ref, None)
            rows = pl.ds(c * CHUNK, CHUNK)
            pre, sg, xc, dtr, dtv, A, E, dtE = (q[k] for k in ("pre", "sg", "xc", "dtr", "dtv", "A", "E", "dtE"))
            xs = xc[:, 0:D]
            total = E[CHUNK - 1:CHUNK, :]
            x_dt = xs * dtE
            eE = jnp.exp(E)
            wdec = jnp.exp(total - E)
            zb = zb_ref[rows, :].astype(F32)
            yv = y_ref[rows, :]
            sgz = _sigmoid(zb)
            sz = zb * sgz
            hh = yv * sz
            for g in range(SSD_GROUPS):
                gc = slice(g * GROUP_W, (g + 1) * GROUP_W)
                hg = hh[:, gc]
                r = lax.rsqrt(jnp.mean(hg * hg, axis=-1, keepdims=True) + EPS)
                dyb_g = dyb_ref[rows, gc].astype(F32)
                dn = dyb_g * nw_ref[:, gc]
                dnw_ref[0:1, gc] += jnp.sum(dyb_g * hg * r, axis=0, keepdims=True)
                dy_s[:, gc] = r * dn - hg * (r * r * r) * jnp.mean(dn * hg, axis=-1, keepdims=True)
            dhh = dy_s[...]
            dp_ref[rows, 0:D] = (dhh * yv * (sgz * (1.0 + zb * (1.0 - sgz)))).astype(BF16)
            dy = dhh * sz
            dy_s[...] = dy
            dDacc_ref[0:1, :] += jnp.sum(dy * xs, axis=0, keepdims=True)
            dxc_s[:, 0:D] = dy * dexp_ref[...]
            for g in range(SSD_GROUPS):
                gc = slice(g * GROUP_W, (g + 1) * GROUP_W)
                bcol = slice(D + g * STATE, D + (g + 1) * STATE)
                ccol = slice(D + SSD_GROUPS * STATE + g * STATE, D + SSD_GROUPS * STATE + (g + 1) * STATE)
                Bg = xc[:, bcol].astype(BF16)
                Cg = xc[:, ccol].astype(BF16)
                B2 = jnp.concatenate([Bg, Bg], axis=0)
                cb2 = _dot_nt(Cg, B2)
                htg = st_ref[c, :, gc]
                htb = htg.astype(BF16)
                dhn = dht_ref[:, gc]
                dhnb = dhn.astype(BF16)
                dyg = dy[:, gc]
                eEg = eE[:, gc]
                wg = wdec[:, gc]
                xdg = x_dt[:, gc]
                CH = _dot(Cg, htb)
                dCHb = (dyg * eEg).astype(BF16)
                dC = _dot_nt(dCHb, htb)
                dl = jnp.exp(total[:, gc])
                dht_prev = _dot_tn(Cg, dCHb) + dl * dhn
                dtot = jnp.sum(dhn * htg, axis=0, keepdims=True) * dl
                dxw = _dot(Bg, dhnb)
                dB = _dot_nt((xdg * wg).astype(BF16), dhnb)
                dwd = dxw * xdg * wg
                dtot = dtot + jnp.sum(dwd, axis=0, keepdims=True)
                dE_s[:, gc] = dyg * eEg * CH - dwd + last_row * dtot
                dxdt_s[:, gc] = dxw * wg
                dcb2 = jnp.zeros((CHUNK, LANE), F32)
                for jj in range(GROUP_W // LANE):
                    pc = slice(g * GROUP_W + jj * LANE, g * GROUP_W + (jj + 1) * LANE)
                    Ej = E[:, pc]
                    e2 = jnp.sum(Ej * diag, axis=0, keepdims=True)
                    Lp = jnp.exp(jnp.where(causal, Ej - e2, -1e30))
                    Mp = cb2 * Lp
                    xj = x_dt[:, pc]
                    xbd = jnp.concatenate([xj * lo, xj * hi], axis=0).astype(BF16)
                    dyj = dy[:, pc].astype(BF16)
                    dMp = _dot_nt(dyj, xbd)
                    dxbd = _dot_tn(Mp.astype(BF16), dyj)
                    dxdt_s[:, pc] += dxbd[0:CHUNK, :] * lo + dxbd[CHUNK:2 * CHUNK, :] * hi
                    dcb2 = dcb2 + dMp * Lp
                    dseg = dMp * Mp
                    dE_s[:, pc] += dseg - diag * jnp.sum(dseg, axis=0, keepdims=True)
                dcb2b = dcb2.astype(BF16)
                dC = dC + _dot(dcb2b, B2)
                dB2 = _dot_tn(dcb2b, Cg)
                dB = dB + dB2[0:CHUNK, :] + dB2[CHUNK:2 * CHUNK, :]
                dxc_s[:, bcol] = dB
                dxc_s[:, ccol] = dC
                dht_ref[:, gc] = dht_prev
            dx_dt = dxdt_s[...]
            dxc_s[:, 0:D] += dx_dt * dtE
            red = _sel_right(jnp.concatenate([dE_s[...], dx_dt * xs], axis=0), expT_ref[...])
            da = _sel_left(triT_ref[...], red[0:CHUNK, :])
            ddtv = red[CHUNK:2 * CHUNK, :] + da * A
            dAacc_ref[0:1, :] += jnp.sum(da * dtv, axis=0, keepdims=True)
            ddtr = ddtv * _sigmoid(dtr)
            ddtb_ref[0:1, :] += jnp.sum(ddtr, axis=0, keepdims=True)
            dp_ref[rows, D + XBC_W:D + XBC_W + DT_W] = ddtr.astype(BF16)
            dpre = dxc_s[...] * (sg * (1.0 + pre * (1.0 - sg)))
            dpre_ref[rows, :] = dpre
            dcb_ref[0:1, :] += jnp.sum(dpre, axis=0, keepdims=True)
        dpre_t = dpre_ref[0:T, :]
        dxbc = jnp.zeros((T, XBC_W), F32)
        for k in range(CONV_K):
            dcw_ref[k:k + 1, :] += jnp.sum(dpre_t * ext_ref[pl.ds(HALO - (CONV_K - 1) + k, T), :], axis=0, keepdims=True)
            dxbc = dxbc + cw_ref[k:k + 1, :] * dpre_ref[pl.ds(CONV_K - 1 - k, T), :]
        dp_ref[:, D:D + XBC_W] = dxbc.astype(BF16)
        dp_ref[:, SEG_SSD[1]:SSD_W] = jnp.zeros((T, SSD_W - SEG_SSD[1]), BF16)
        dpre_ref[T:T + HALO, :] = dpre_ref[0:HALO, :]

        @pl.when(i == nsteps - 1)
        def _():
            dalog_ref[...] = dAacc_ref[...] * (-jnp.exp(alog_ref[...]))
            dD_ref[...] = _dot(dDacc_ref[...], expT_ref[...].astype(F32), precision=HI)

    full = lambda a: pl.BlockSpec(a.shape, lambda i: (0,) * a.ndim)
    hb = T // HALO_BLK
    rev = lambda i: nsteps - 1 - i
    acc = lambda w: pl.BlockSpec((8, w), lambda i: (0, 0))
    return pl.pallas_call(
        body, name="ssd_bwd", grid=(nsteps,),
        in_specs=[pl.BlockSpec((T, D), lambda i: (rev(i), OFF_ZB // D)),
                  pl.BlockSpec((T, XBC_W), lambda i: (rev(i), OFF_XBC // XBC_W)),
                  pl.BlockSpec((HALO_BLK, XBC_W), lambda i: (jnp.maximum(rev(i) * hb - 1, 0), OFF_XBC // XBC_W)),
                  pl.BlockSpec((T, DT_W), lambda i: (rev(i), OFF_DT // DT_W)),
                  pl.BlockSpec((T, D), lambda i: (rev(i), 0)), pl.BlockSpec((T, D), lambda i: (rev(i), 0)),
                  pl.BlockSpec((ncl, STATE, D), lambda i: (rev(i), 0, 0)),
                  full(conv_w), full(conv_b), full(dtb_p), full(alog_p), full(d_exp), full(norm_w),
                  full(tri), full(triT), full(expand), full(expandT)],
        out_specs=[pl.BlockSpec((T, SSD_W), lambda i: (rev(i), 0)),
                   acc(XBC_W), acc(XBC_W), acc(DT_W), acc(DT_W), acc(DT_W), acc(D)],
        out_shape=[jax.ShapeDtypeStruct((S, SSD_W), BF16),
                   jax.ShapeDtypeStruct((8, XBC_W), F32), jax.ShapeDtypeStruct((8, XBC_W), F32),
                   jax.ShapeDtypeStruct((8, DT_W), F32), jax.ShapeDtypeStruct((8, DT_W), F32),
                   jax.ShapeDtypeStruct((8, DT_W), F32), jax.ShapeDtypeStruct((8, D), F32)],
        scratch_shapes=[pltpu.VMEM((STATE, D), F32), pltpu.VMEM((HALO + T, XBC_W), F32), pltpu.VMEM((T + HALO, XBC_W), F32),
                        pltpu.VMEM((CHUNK, D), F32), pltpu.VMEM((CHUNK, D), F32), pltpu.VMEM((CHUNK, D), F32),
                        pltpu.VMEM((CHUNK, XBC_W), F32), pltpu.VMEM((8, D), F32), pltpu.VMEM((8, DT_W), F32)],
        compiler_params=_cp(("arbitrary",)),
    )(proj, proj, proj, proj, dyb, y, states, conv_w, conv_b, dtb_p, alog_p, d_exp, norm_w, tri, triT, expand, expandT)


def _head(x, ya, yb, proj, target, gate_b, wout, fw, *, tm):
    S = x.shape[0]

    def body(x_ref, ya_ref, yb_ref, gl0_ref, gl1_ref, t_ref, gb_ref, w_ref, fw_ref,
             dh_ref, dhb_ref, mb_ref, dya_ref, dyb_ref, dgl_ref, loss_ref, dfw_ref, dgb_ref):
        @pl.when(pl.program_id(0) == 0)
        def _():
            loss_ref[...] = jnp.zeros_like(loss_ref)
            dfw_ref[...] = jnp.zeros_like(dfw_ref)
            dgb_ref[...] = jnp.zeros_like(dgb_ref)

        ya_v = ya_ref[...].astype(F32)
        yb_v = yb_ref[...].astype(F32)
        g0 = _sigmoid(gl0_ref[...].astype(F32) + gb_ref[:, 0:D])
        g1 = _sigmoid(gl1_ref[...].astype(F32) + gb_ref[:, D:2 * D])
        mb = (g0 * ya_v + g1 * yb_v).astype(BF16)
        mb_ref[...] = mb
        h = x_ref[...] + _dot(mb, w_ref[...])
        r = lax.rsqrt(jnp.mean(h * h, axis=-1, keepdims=True) + EPS)
        hn = h * r
        err = hn * fw_ref[...] - t_ref[...]
        loss_ref[...] += 0.5 * jnp.sum(jnp.mean(err * err, axis=-1, keepdims=True))
        dyf = err * (1.0 / D)
        dfw_ref[0:1, :] += jnp.sum(dyf * hn, axis=0, keepdims=True)
        dhn = dyf * fw_ref[...]
        dh = r * (dhn - hn * jnp.mean(dhn * hn, axis=-1, keepdims=True))
        dh_ref[...] = dh
        dhb = dh.astype(BF16)
        dhb_ref[...] = dhb
        dm = _dot_nt(dhb, w_ref[...])
        dya_ref[...] = (dm * g0).astype(BF16)
        dyb_ref[...] = (dm * g1).astype(BF16)
        dgl0 = dm * ya_v * g0 * (1.0 - g0)
        dgl1 = dm * yb_v * g1 * (1.0 - g1)
        dgl_ref[:, 0:D] = dgl0.astype(BF16)
        dgl_ref[:, D:2 * D] = dgl1.astype(BF16)
        dgb_ref[0:1, 0:D] += jnp.sum(dgl0, axis=0, keepdims=True)
        dgb_ref[0:1, D:2 * D] += jnp.sum(dgl1, axis=0, keepdims=True)

    row = pl.BlockSpec((tm, D), lambda i: (i, 0))
    seg = lambda off: pl.BlockSpec((tm, D), lambda i: (i, off // D))
    full = lambda a: pl.BlockSpec(a.shape, lambda i: (0,) * a.ndim)
    acc = lambda w: pl.BlockSpec((8, w), lambda i: (0, 0))
    return pl.pallas_call(
        body, name="head", grid=(S // tm,),
        in_specs=[row, row, row, seg(OFF_G0), seg(OFF_G1), row, full(gate_b), full(wout), full(fw)],
        out_specs=[row, row, row, row, row, pl.BlockSpec((tm, 2 * D), lambda i: (i, 0)), acc(LANE), acc(D), acc(2 * D)],
        out_shape=[jax.ShapeDtypeStruct((S, D), F32), jax.ShapeDtypeStruct((S, D), BF16), jax.ShapeDtypeStruct((S, D), BF16),
                   jax.ShapeDtypeStruct((S, D), BF16), jax.ShapeDtypeStruct((S, D), BF16), jax.ShapeDtypeStruct((S, 2 * D), BF16),
                   jax.ShapeDtypeStruct((8, LANE), F32), jax.ShapeDtypeStruct((8, D), F32), jax.ShapeDtypeStruct((8, 2 * D), F32)],
        compiler_params=_cp(("arbitrary",)),
    )(x, ya, yb, proj, proj, target, gate_b, wout, fw)


def _adam_update(g, w_ref, m_ref, v_ref, g_ref, d_ref, m2_ref, v2_ref):
    m2 = ADAM_B1 * m_ref[...] + (1.0 - ADAM_B1) * g
    v2 = ADAM_B2 * v_ref[...] + (1.0 - ADAM_B2) * (g * g)
    m_hat = m2 / (1.0 - ADAM_B1 ** ADAM_STEP)
    v_hat = v2 / (1.0 - ADAM_B2 ** ADAM_STEP)
    g_ref[...] = g
    d_ref[...] = -ADAM_LR * (m_hat / (jnp.sqrt(v_hat) + ADAM_EPS) + ADAM_WD * w_ref[...])
    m2_ref[...] = m2
    v2_ref[...] = v2


def _adamw_own(me, own, landed, w, m, v, *, tr, tc, name):
    _, R, C = landed.shape
    assert R % tr == 0 and C % tc == 0, (name, R, C, tr, tc)

    def body(me_ref, own_ref, p_ref, w_ref, m_ref, v_ref, g_ref, d_ref, m2_ref, v2_ref):
        mine = own_ref[0].astype(F32)
        g = jnp.where(me_ref[0] == 0, mine, p_ref[0].astype(F32))
        for k in range(1, N_DEV):
            g = g + jnp.where(me_ref[0] == k, mine, p_ref[k].astype(F32))
        _adam_update(g, w_ref, m_ref, v_ref, g_ref, d_ref, m2_ref, v2_ref)

    tile = pl.BlockSpec((tr, tc), lambda i, j, me_ref: (i, j))
    return pl.pallas_call(
        body, name=name,
        grid_spec=pltpu.PrefetchScalarGridSpec(
            num_scalar_prefetch=1, grid=(R // tr, C // tc),
            in_specs=[pl.BlockSpec((1, tr, tc), lambda i, j, me_ref: (me_ref[0], i, j)),
                      pl.BlockSpec((N_DEV, tr, tc), lambda i, j, me_ref: (0, i, j)), tile, tile, tile],
            out_specs=[tile, tile, tile, tile]),
        out_shape=[jax.ShapeDtypeStruct((R, C), F32)] * 4,
        compiler_params=_cp(("parallel", "parallel")),
    )(me, own, landed, w, m, v)


def _adamw(parts, w, m, v, *, tr, name):
    _, R, C = parts.shape
    assert R % tr == 0, (name, R, tr)

    def body(p_ref, w_ref, m_ref, v_ref, g_ref, d_ref, m2_ref, v2_ref):
        g = p_ref[0].astype(F32)
        for k in range(1, N_DEV):
            g = g + p_ref[k].astype(F32)
        _adam_update(g, w_ref, m_ref, v_ref, g_ref, d_ref, m2_ref, v2_ref)

    row = pl.BlockSpec((tr, C), lambda i: (i, 0))
    return pl.pallas_call(
        body, name=name, grid=(R // tr,),
        in_specs=[pl.BlockSpec((N_DEV, tr, C), lambda i: (0, i, 0)), row, row, row],
        out_specs=[row, row, row, row],
        out_shape=[jax.ShapeDtypeStruct((R, C), F32)] * 4,
        compiler_params=_cp(("parallel",)),
    )(parts, w, m, v)


def _place():
    x, y, c = lax.axis_index("x"), lax.axis_index("y"), lax.axis_index("c")
    return x, y, c


def _all_gather(arrs, *, name):
    n = len(arrs)

    def body(*refs):
        ins, outs = refs[:n], refs[n:2 * n]
        send_sems, recv_sems, local_sems = refs[2 * n:]
        x, y, c = _place()
        me, sibling = (x, y, c), (x, y, 1 - c)
        chips = [(1 - x, y), (x, 1 - y), (1 - x, 1 - y)]

        def idx(px, py, pc):
            return 4 * px + 2 * py + pc

        def copy(k, a, block, to, src=None):
            slab = outs[a].at[idx(*block)]
            return pltpu.make_async_remote_copy(
                src_ref=slab if src is None else src, dst_ref=slab,
                send_sem=send_sems.at[k, a], recv_sem=recv_sems.at[k, a], device_id=to, device_id_type=MESH)

        mine = [pltpu.make_async_copy(ins[a], outs[a].at[idx(*me)], local_sems.at[a]) for a in range(n)]
        for cp in mine:
            cp.start()
        first = []
        for a in range(n):
            first.append(copy(0, a, me, sibling, src=ins[a]))
            first += [copy(1 + j, a, me, (*chip, c), src=ins[a]) for j, chip in enumerate(chips)]
        for cp in first:
            cp.start()
        passed = []
        for j, chip in enumerate(chips):
            for a in range(n):
                copy(1 + j, a, (*chip, c), me).wait_recv()
                fwd = copy(4 + j, a, (*chip, c), sibling)
                fwd.start()
                passed.append(fwd)
        for a in range(n):
            copy(0, a, sibling, me).wait_recv()
            for j, chip in enumerate(chips):
                copy(4 + j, a, (*chip, 1 - c), me).wait_recv()
        for cp in first + passed:
            cp.wait_send()
        for cp in mine:
            cp.wait()

    anyspec = pl.BlockSpec(memory_space=pl.ANY)
    return pl.pallas_call(
        body, name=name,
        in_specs=[anyspec] * n, out_specs=[anyspec] * n,
        out_shape=[jax.ShapeDtypeStruct((N_DEV,) + a.shape, a.dtype) for a in arrs],
        scratch_shapes=[pltpu.SemaphoreType.DMA((7, n)), pltpu.SemaphoreType.DMA((7, n)), pltpu.SemaphoreType.DMA((n,))],
    )(*arrs)


_REL = [(dx, dy, dc) for dx in (0, 1) for dy in (0, 1) for dc in (0, 1)][1:]
_HBM = pl.BlockSpec(memory_space=pltpu.HBM)
_SEM = pl.BlockSpec(memory_space=pltpu.SEMAPHORE)
_EFFECT = pltpu.SideEffectType.DATAFLOW_SIDE_EFFECTING


def _peer(k):
    x, y, c = _place()
    dx, dy, dc = _REL[k]
    return (1 - x if dx else x, 1 - y if dy else y, 1 - c if dc else c)


def _exchange_start(parts, *, name):
    n = len(parts)

    def body(*refs):
        ins, lands = refs[:n], refs[n:2 * n]
        send_sems, recv_sems, token = refs[2 * n], refs[2 * n + 1], refs[-1]
        x, y, c = _place()
        me = 4 * x + 2 * y + c
        for a in range(n):
            for k in range(len(_REL)):
                px, py, pc = _peer(k)
                pltpu.make_async_remote_copy(
                    src_ref=ins[a].at[4 * px + 2 * py + pc], dst_ref=lands[a].at[me],
                    send_sem=send_sems.at[len(_REL) * a + k], recv_sem=recv_sems.at[len(_REL) * a + k],
                    device_id=(px, py, pc), device_id_type=MESH).start()
        token[...] = jnp.zeros_like(token)

    sem = pltpu.SemaphoreType.DMA((len(_REL) * n,))
    bufs = [pltpu.HBM(p.shape, p.dtype) for p in parts]
    outs = pl.pallas_call(
        body, name=name,
        out_shape=(sem, sem, *bufs, *bufs, jax.ShapeDtypeStruct((8, LANE), F32)),
        in_specs=(_HBM,) * (2 * n), out_specs=(_SEM, _SEM, *(_HBM,) * (2 * n), pl.BlockSpec(memory_space=pltpu.VMEM)),
        input_output_aliases={i: 2 + i for i in range(2 * n)},
        compiler_params=pltpu.CompilerParams(has_side_effects=_EFFECT),
    )(*[pltpu.with_memory_space_constraint(p, pltpu.HBM) for p in parts],
      *[pltpu.with_memory_space_constraint(lax.empty(p.shape, p.dtype), pltpu.HBM) for p in parts])
    return outs[0], outs[1], outs[2:2 + n], outs[2 + n:2 + 2 * n], outs[-1]


def _exchange_wait(send_sems, recv_sems, parts, lands, after, *, name):
    n = len(parts)

    def body(*refs):
        ins, lands_ = refs[:n], refs[n:2 * n]
        ssem, rsem = refs[2 * n], refs[2 * n + 1]
        for a in range(n):
            for k in range(len(_REL)):
                px, py, pc = _peer(k)
                p = 4 * px + 2 * py + pc
                cp = pltpu.make_async_remote_copy(
                    src_ref=ins[a].at[p], dst_ref=lands_[a].at[p],
                    send_sem=ssem.at[len(_REL) * a + k], recv_sem=rsem.at[len(_REL) * a + k],
                    device_id=(px, py, pc), device_id_type=MESH)
                cp.wait_send()
                cp.wait_recv()

    bufs = [pltpu.HBM(p.shape, p.dtype) for p in parts]
    outs = pl.pallas_call(
        body, name=name, out_shape=(*bufs, *bufs),
        in_specs=(*(_HBM,) * (2 * n), _SEM, _SEM, pl.BlockSpec(memory_space=pl.ANY)), out_specs=(_HBM,) * (2 * n),
        input_output_aliases={i: i for i in range(2 * n)},
        compiler_params=pltpu.CompilerParams(has_side_effects=_EFFECT),
    )(*parts, *lands, send_sems, recv_sems, after)
    return outs[:n], outs[n:]


WEIGHTS = ('norm_w', 'w_in', 'gate_b', 'sgu_norm_g', 'sgu_norm_b', 'sgu_w', 'sgu_b', 'conv_w', 'conv_b', 'dt_bias', 'A_log',
           'D_skip', 'ssd_norm_w', 'w_out', 'final_norm_w')
SHARDED = ('w_in', 'conv_w', 'w_out')
PACK_ROW = 8 * LANE


def _constants():
    tri = np.tril(np.ones((CHUNK, CHUNK), np.float32))
    expand = np.zeros((DT_W, D), np.float32)
    for h in range(HEADS):
        expand[h, h * HEADDIM:(h + 1) * HEADDIM] = 1.0
    sel = np.zeros((D, LANE), np.float32)
    for g in range(SGU_GROUPS):
        sel[g * LANE:(g + 1) * LANE, g] = 1.0
    pos_chunk = np.arange(SGU_BLOCK) // CHUNK
    mask = (pos_chunk[None, :] <= pos_chunk[:, None]).astype(np.float32)
    return dict(tri=jnp.asarray(tri, BF16), triT=jnp.asarray(tri.T.copy(), BF16), expand=jnp.asarray(expand, BF16),
                expandT=jnp.asarray(expand.T.copy(), BF16), sel=jnp.asarray(sel), mask=jnp.asarray(mask))


def _permute_rows(g):
    wT = g.reshape(W_IN, D)
    return jnp.concatenate([wT[:6144], wT[11296:], wT[6144:11296], jnp.zeros((DT_W - HEADS, D), g.dtype)], axis=0)


def _to_shards(segs):
    starts = np.cumsum([0] + [s.shape[0] for s in segs])
    assert starts[-1] == W_IN
    slabs = []
    for k in range(N_DEV):
        pieces = []
        for s, s0 in zip(segs, starts[:-1]):
            lo, hi = max(k * SHARD_IN, s0), min((k + 1) * SHARD_IN, s0 + s.shape[0])
            if lo < hi:
                pieces.append(s[lo - s0:hi - s0])
        slabs.append(jnp.concatenate(pieces, axis=0))
    return jnp.stack(slabs)


def _local_step(x2, tgt, wpT, wout, cw, p, exchange):
    S = x2.shape[0]
    k = _constants()
    xn = _norm_fwd(x2, p['norm_w'], tm=min(512, S))
    proj = _matmul(xn, wpT, trans_b=True, out_dtype=BF16, tm=min(1024, S), tn=1408, tk=D, name="in_proj")
    wm32 = p['sgu_w'][0] * k['mask']
    wm = wm32.astype(BF16)
    wmT = jnp.swapaxes(wm32, 1, 2).astype(BF16)
    bias_full = jnp.repeat(p['sgu_b'][0].T, LANE, axis=1)
    tm_sgu = min(256, S)
    ya = _sgu_fwd(proj, p['sgu_norm_g'], p['sgu_norm_b'], wm, bias_full, tm=tm_sgu)
    pad32 = lambda a: jnp.pad(a, ((0, 0), (0, DT_W - HEADS)))
    dtb_p, alog_p = pad32(p['dt_bias']), pad32(p['A_log'])
    d_exp = jnp.repeat(p['D_skip'], HEADDIM, axis=1)
    ssd_args = (cw, p['conv_b'], dtb_p, alog_p, d_exp, p['ssd_norm_w'])
    y, yb, states = _ssd_fwd(proj, *ssd_args, k['tri'], k['expand'])
    dh, dhb, mb, dya, dyb, dgl, loss, dfw, dgb = _head(
        x2, ya, yb, proj, tgt, p['gate_b'], wout, p['final_norm_w'][None, :], tm=min(256, S))
    dsgu, dws, dbsT, dsg, dsb = _sgu_bwd(proj, dya, p['sgu_norm_g'], p['sgu_norm_b'], wm, wmT, bias_full, k['mask'], k['sel'],
                                         tm=tm_sgu)
    dssd, dcw, dcb, ddtb, dalog, dD, dnw = _ssd_bwd(proj, dyb, y, states, *ssd_args, k['tri'], k['triT'], k['expand'], k['expandT'])
    tk = min(2048, S)
    tn = 1024
    dwT_sgu = _matmul(dsgu, xn, trans_a=True, out_dtype=BF16, tm=1024, tn=tn, tk=tk, name="dw_in_sgu")
    dwT_gate = _matmul(dgl, xn, trans_a=True, out_dtype=BF16, tm=1024, tn=tn, tk=tk, name="dw_in_gate")
    dwT_ssd = _matmul(dssd, xn, trans_a=True, out_dtype=BF16, tm=1024, tn=tn, tk=tk, name="dw_in_ssd")
    dw_out = _matmul(mb, dhb, trans_a=True, out_dtype=BF16, tm=1024, tn=tn, tk=tk, name="dw_out")
    token = exchange([dwT_sgu, dwT_ssd[:W_IN - SEG_SSD[0]], dwT_gate], dw_out)
    tm = min(1024, S)
    wpT_ssd = jnp.pad(wpT[SEG_SSD[0]:], ((0, SSD_PAD_W - SEG_SSD[1]), (0, 0)))
    dxn = _matmul(dsgu, wpT, tm=tm, tn=tn, tk=2048, after=token, name="dxn_sgu")
    dxn = _matmul(dgl, wpT, b_koff=SEG_GATE[0] // 2048, tm=tm, tn=tn, tk=2048, add=dxn, name="dxn_gate")
    dxn = _matmul(dssd, wpT_ssd, tm=tm, tn=tn, tk=2048, add=dxn, name="dxn_ssd")
    grad_x, dnorm = _norm_bwd(x2, p['norm_w'], dxn, dh, tm=min(256, S))
    grads = dict(
        norm_w=dnorm[0:1], gate_b=dgb[0:1], sgu_norm_g=dsg[0:1], sgu_norm_b=dsb[0:1], sgu_w=dws[None],
        sgu_b=dbsT[:, :SGU_GROUPS].T[None], conv_w=dcw[0:CONV_K][None], conv_b=dcb[0:1], dt_bias=ddtb[0:1, :HEADS],
        A_log=dalog[0:1, :HEADS], D_skip=dD[0:1, :HEADS], ssd_norm_w=dnw[0:1], final_norm_w=dfw[0])
    return loss[0, 0], grad_x, grads


def _pack(arrs):
    rows, offs, r = [], [], 0
    for a in arrs:
        n = a.size
        nr = -(-n // PACK_ROW) * 8
        rows.append(jnp.pad(a.reshape(-1).astype(F32), (0, nr * LANE - n)).reshape(nr, LANE))
        offs.append(r)
        r += nr
    return jnp.concatenate(rows, axis=0), offs


def kernel(x, norm_w, w_in, gate_b, sgu_norm_g, sgu_norm_b, sgu_w, sgu_b, conv_w, conv_b, dt_bias, A_log, D_skip, ssd_norm_w, w_out, final_norm_w, loss_target, m_norm_w, m_w_in, m_gate_b, m_sgu_norm_g, m_sgu_norm_b, m_sgu_w, m_sgu_b, m_conv_w, m_conv_b, m_dt_bias, m_A_log, m_D_skip, m_ssd_norm_w, m_w_out, m_final_norm_w, v_norm_w, v_w_in, v_gate_b, v_sgu_norm_g, v_sgu_norm_b, v_sgu_w, v_sgu_b, v_conv_w, v_conv_b, v_dt_bias, v_A_log, v_D_skip, v_ssd_norm_w, v_w_out, v_final_norm_w):
    w = dict(norm_w=norm_w, w_in=w_in, gate_b=gate_b, sgu_norm_g=sgu_norm_g, sgu_norm_b=sgu_norm_b, sgu_w=sgu_w, sgu_b=sgu_b,
             conv_w=conv_w, conv_b=conv_b, dt_bias=dt_bias, A_log=A_log, D_skip=D_skip, ssd_norm_w=ssd_norm_w, w_out=w_out,
             final_norm_w=final_norm_w)
    m = dict(norm_w=m_norm_w, w_in=m_w_in, gate_b=m_gate_b, sgu_norm_g=m_sgu_norm_g, sgu_norm_b=m_sgu_norm_b, sgu_w=m_sgu_w,
             sgu_b=m_sgu_b, conv_w=m_conv_w, conv_b=m_conv_b, dt_bias=m_dt_bias, A_log=m_A_log, D_skip=m_D_skip,
             ssd_norm_w=m_ssd_norm_w, w_out=m_w_out, final_norm_w=m_final_norm_w)
    v = dict(norm_w=v_norm_w, w_in=v_w_in, gate_b=v_gate_b, sgu_norm_g=v_sgu_norm_g, sgu_norm_b=v_sgu_norm_b, sgu_w=v_sgu_w,
             sgu_b=v_sgu_b, conv_w=v_conv_w, conv_b=v_conv_b, dt_bias=v_dt_bias, A_log=v_A_log, D_skip=v_D_skip,
             ssd_norm_w=v_ssd_norm_w, w_out=v_w_out, final_norm_w=v_final_norm_w)
    me = 4 * lax.axis_index("x") + 2 * lax.axis_index("y") + lax.axis_index("c")
    shard_cw = XBC_W // N_DEV

    tpose = lambda a: jnp.swapaxes(a[0], 0, 1)
    g_in, g_out, g_cw = _all_gather([tpose(w_in).astype(BF16), w_out[0].astype(BF16), conv_w[0]], name="gather_weights")
    wpT = _permute_rows(g_in)
    wout_full = g_out.reshape(D, D)
    cw_full = jnp.swapaxes(g_cw, 0, 1).reshape(CONV_K, XBC_W)

    flight = {}

    def exchange(dw_inT_segs, dw_out):
        parts = [_to_shards(dw_inT_segs), dw_out.reshape(N_DEV, D // N_DEV, D)]
        flight['sems'], flight['rsems'], flight['parts'], flight['lands'], token = _exchange_start(parts, name="exchange_start")
        return token

    loss_part, grad_x, grads = _local_step(x[0], loss_target[0], wpT, wout_full, cw_full, w, exchange)
    (own_in, own_out), (land_in, land_out) = _exchange_wait(
        flight['sems'], flight['rsems'], flight['parts'], flight['lands'], grad_x, name="exchange_wait")
    me_arr = jnp.reshape(me, (1,)).astype(jnp.int32)
    res = {}
    res['w_in'] = [jnp.swapaxes(o, 0, 1) for o in _adamw_own(
        me_arr, own_in, land_in, tpose(w_in), tpose(m_w_in), tpose(v_w_in), tr=SHARD_IN, tc=256, name="adamw_w_in")]
    res['w_out'] = _adamw_own(me_arr, own_out, land_out, w_out[0], m_w_out[0], v_w_out[0], tr=128, tc=D, name="adamw_w_out")

    small = [n for n in WEIGHTS if n not in SHARDED]
    packed, offs = _pack([grads[n] for n in small] + [loss_part, grads['conv_w']])
    (gathered,) = _all_gather([packed], name="gather_small")
    off_loss, off_cw = offs[-2], offs[-1]
    cw_parts = gathered[:, off_cw:, :].reshape(N_DEV, CONV_K, XBC_W)
    cw_parts = lax.dynamic_slice_in_dim(cw_parts, me * shard_cw, shard_cw, axis=2)
    cw_rows = _pack([cw_parts[0]])[0].shape[0]
    cw_parts = jnp.pad(cw_parts.reshape(N_DEV, -1), ((0, 0), (0, cw_rows * LANE - CONV_K * shard_cw))).reshape(N_DEV, cw_rows, LANE)
    parts = jnp.concatenate([gathered[:, :off_cw, :], cw_parts], axis=1)
    zero = jnp.zeros((), F32)
    packs = [_pack([d[n] for n in small] + [zero, d['conv_w']])[0] for d in (w, m, v)]
    outs = _adamw(parts, *packs, tr=parts.shape[1], name="adamw_small")

    def unpack(o, name):
        if name == 'conv_w':
            return o[off_cw:off_cw + cw_rows].reshape(-1)[:CONV_K * shard_cw].reshape(w['conv_w'].shape)
        r0 = offs[small.index(name)]
        n = w[name].size
        return o[r0:r0 + -(-n // PACK_ROW) * 8].reshape(-1)[:n].reshape(w[name].shape)

    for n in small + ['conv_w']:
        res[n] = [unpack(o, n) for o in outs]
    for n in ('w_in', 'w_out'):
        res[n] = [o[None] for o in res[n]]
    loss = outs[0][off_loss, 0]
    return (loss, grad_x[None], *[res[n][0] for n in WEIGHTS], *[res[n][1] for n in WEIGHTS],
            *[res[n][2] for n in WEIGHTS], *[res[n][3] for n in WEIGHTS])
```

```python
import functools

import numpy as np
import jax
import jax.numpy as jnp
from jax import lax
from jax.experimental import pallas as pl
from jax.experimental.pallas import tpu as pltpu

F32 = jnp.float32
BF16 = jnp.bfloat16
HI = lax.Precision.HIGHEST
MESH = pl.DeviceIdType.MESH

D = 2048
EPS = 1e-5
SGU_BLOCK = 128
SGU_GROUPS = 16
CHUNK = 64
HEADS = 32
HEADDIM = 64
SSD_GROUPS = 4
GROUP_W = D // SSD_GROUPS
STATE = 128
CONV_K = 4
XBC_W = D + 2 * SSD_GROUPS * STATE
W_IN = 15392
N_DEV = 8
SHARD_IN = W_IN // N_DEV
ADAM_LR, ADAM_B1, ADAM_B2, ADAM_EPS, ADAM_WD, ADAM_STEP = 0.001, 0.9, 0.999, 1e-08, 0.01, 10

LANE = 128
DT_W = LANE
OFF_U, OFF_V, OFF_ZA, OFF_G0, OFF_G1, OFF_ZB, OFF_XBC, OFF_DT = 0, 2048, 4096, 6144, 8192, 10240, 12288, 15360
WP = OFF_DT + DT_W
SEG_SGU = (0, 6144)
SEG_GATE = (6144, 4096)
SEG_SSD = (10240, WP - 10240)
SSD_PAD_W = 6144
VMEM_LIMIT = 56 * 1024 * 1024


def _cp(sem=None, vmem=VMEM_LIMIT):
    return pltpu.CompilerParams(dimension_semantics=sem, vmem_limit_bytes=vmem)


def _sigmoid(x):
    return 1.0 / (1.0 + jnp.exp(-x))


def _softplus(x):
    return jnp.maximum(x, 0.0) + jnp.log(1.0 + jnp.exp(-jnp.abs(x)))


def _dot(a, b, precision=None):
    return jnp.dot(a, b, preferred_element_type=F32, precision=precision)


def _dot_nt(a, b, precision=None):
    return lax.dot_general(a, b, (((1,), (1,)), ((), ())), preferred_element_type=F32, precision=precision)


def _dot_tn(a, b, precision=None):
    return lax.dot_general(a, b, (((0,), (0,)), ((), ())), preferred_element_type=F32, precision=precision)


def _split3(a):
    hi = a.astype(BF16)
    r = a - hi.astype(F32)
    mid = r.astype(BF16)
    return hi, mid, (r - mid.astype(F32)).astype(BF16)


def _sel_right(a, sel01):
    m = a.shape[0]
    r = _dot(jnp.concatenate(_split3(a), axis=0), sel01)
    return (r[0:m] + r[m:2 * m]) + r[2 * m:3 * m]


def _sel_left(sel01, a):
    n = a.shape[1]
    r = _dot(sel01, jnp.concatenate(_split3(a), axis=1))
    return (r[:, 0:n] + r[:, n:2 * n]) + r[:, 2 * n:3 * n]


def _matmul(a, b, *, trans_a=False, trans_b=False, b_koff=0, n=None, out_dtype=F32, tm, tn, tk, add=None, after=None, name):
    K, M = a.shape if trans_a else a.shape[::-1]
    N = (n or b.shape[0]) if trans_b else b.shape[1]
    assert M % tm == 0 and N % tn == 0 and K % tk == 0 and not (trans_a and trans_b), (name, M, N, K, tm, tn, tk)
    nk = K // tk

    def body(*refs):
        a_ref, b_ref = refs[:2]
        add_ref = refs[2] if add is not None else None
        o_ref, acc_ref = refs[-2:]
        k = pl.program_id(2)
        if trans_a:
            part = _dot_tn(a_ref[...], b_ref[...])
        else:
            part = _dot_nt(a_ref[...], b_ref[...]) if trans_b else _dot(a_ref[...], b_ref[...])

        def result(r):
            if add_ref is not None:
                r = r + add_ref[...]
            return r.astype(out_dtype)

        if nk == 1:
            o_ref[...] = result(part)
        else:
            @pl.when(k == 0)
            def _():
                acc_ref[...] = part

            @pl.when(jnp.logical_and(k > 0, k < nk - 1))
            def _():
                acc_ref[...] += part

            @pl.when(k == nk - 1)
            def _():
                o_ref[...] = result(acc_ref[...] + part)

    in_specs = [pl.BlockSpec((tk, tm), lambda i, j, k: (k, i)) if trans_a else pl.BlockSpec((tm, tk), lambda i, j, k: (i, k)),
                pl.BlockSpec((tn, tk), lambda i, j, k: (j, k)) if trans_b else pl.BlockSpec((tk, tn), lambda i, j, k: (k + b_koff, j))]
    args = [a, b]
    if add is not None:
        in_specs.append(pl.BlockSpec((tm, tn), lambda i, j, k: (i, j)))
        args.append(add)
    if after is not None:
        in_specs.append(pl.BlockSpec(memory_space=pl.ANY))
        args.append(after)
    return pl.pallas_call(
        body, name=name, grid=(M // tm, N // tn, nk), in_specs=in_specs,
        out_specs=pl.BlockSpec((tm, tn), lambda i, j, k: (i, j)),
        out_shape=jax.ShapeDtypeStruct((M, N), out_dtype),
        scratch_shapes=[pltpu.VMEM((tm, tn), F32)],
        compiler_params=_cp(("parallel", "parallel", "arbitrary")),
    )(*args)


def _norm_fwd(x, w, *, tm):
    S = x.shape[0]

    def body(x_ref, w_ref, o_ref):
        xv = x_ref[...]
        r = lax.rsqrt(jnp.mean(xv * xv, axis=-1, keepdims=True) + EPS)
        o_ref[...] = (xv * r * w_ref[...]).astype(BF16)

    return pl.pallas_call(
        body, name="norm_fwd", grid=(S // tm,),
        in_specs=[pl.BlockSpec((tm, D), lambda i: (i, 0)), pl.BlockSpec((1, D), lambda i: (0, 0))],
        out_specs=pl.BlockSpec((tm, D), lambda i: (i, 0)),
        out_shape=jax.ShapeDtypeStruct((S, D), BF16), compiler_params=_cp(("parallel",)),
    )(x, w)


def _norm_bwd(x, w, dxn, dh, *, tm):
    S = x.shape[0]

    def body(x_ref, w_ref, dxn_ref, dh_ref, gx_ref, dw_ref):
        xv = x_ref[...]
        r = lax.rsqrt(jnp.mean(xv * xv, axis=-1, keepdims=True) + EPS)
        xh = xv * r
        dxn_v = dxn_ref[...]
        dxh = dxn_v * w_ref[...]
        gx_ref[...] = dh_ref[...] + r * (dxh - xh * jnp.mean(dxh * xh, axis=-1, keepdims=True))

        @pl.when(pl.program_id(0) == 0)
        def _():
            dw_ref[...] = jnp.zeros_like(dw_ref)

        dw_ref[0:1, :] += jnp.sum(dxn_v * xh, axis=0, keepdims=True)

    row = pl.BlockSpec((tm, D), lambda i: (i, 0))
    return pl.pallas_call(
        body, name="norm_bwd", grid=(S // tm,),
        in_specs=[row, pl.BlockSpec((1, D), lambda i: (0, 0)), row, row],
        out_specs=[row, pl.BlockSpec((8, D), lambda i: (0, 0))],
        out_shape=[jax.ShapeDtypeStruct((S, D), F32), jax.ShapeDtypeStruct((8, D), F32)],
        compiler_params=_cp(("arbitrary",)),
    )(x, w, dxn, dh)


def _sgu_core(u_ref, v_ref, z_ref, g_ref, b_ref, wm_ref, bias_ref, vnb_ref, mixed_ref, tm):
    v = v_ref[...].astype(F32)
    mu = jnp.mean(v, axis=-1, keepdims=True)
    vc = v - mu
    rs = lax.rsqrt(jnp.mean(vc * vc, axis=-1, keepdims=True) + EPS)
    vh = vc * rs
    vnb_ref[...] = (vh * g_ref[...] + b_ref[...]).astype(BF16)
    for blk in range(tm // SGU_BLOCK):
        rows = pl.ds(blk * SGU_BLOCK, SGU_BLOCK)
        for gi in range(SGU_GROUPS):
            cols = pl.ds(gi * LANE, LANE)
            mixed_ref[rows, cols] = _dot(wm_ref[gi], vnb_ref[rows, cols]) + bias_ref[:, cols]
    return vh, rs


def _sgu_fwd(proj, g, b, wm, bias_full, *, tm):
    S = proj.shape[0]

    def body(u_ref, v_ref, z_ref, g_ref, b_ref, wm_ref, bias_ref, y_ref, vnb_ref, mixed_ref):
        _sgu_core(u_ref, v_ref, z_ref, g_ref, b_ref, wm_ref, bias_ref, vnb_ref, mixed_ref, tm)
        z = z_ref[...].astype(F32)
        y_ref[...] = (u_ref[...].astype(F32) * mixed_ref[...] * (z * _sigmoid(z))).astype(BF16)

    seg = lambda off: pl.BlockSpec((tm, D), lambda i: (i, off // D))
    full = lambda a: pl.BlockSpec(a.shape, lambda i: (0,) * a.ndim)
    return pl.pallas_call(
        body, name="sgu_fwd", grid=(S // tm,),
        in_specs=[seg(OFF_U), seg(OFF_V), seg(OFF_ZA), full(g), full(b), full(wm), full(bias_full)],
        out_specs=pl.BlockSpec((tm, D), lambda i: (i, 0)),
        out_shape=jax.ShapeDtypeStruct((S, D), BF16),
        scratch_shapes=[pltpu.VMEM((tm, D), BF16), pltpu.VMEM((tm, D), F32)],
        compiler_params=_cp(("parallel",)),
    )(proj, proj, proj, g, b, wm, bias_full)


def _sgu_bwd(proj, dy, g, b, wm, wmT, bias_full, mask, sel, *, tm):
    S = proj.shape[0]
    nsteps = S // tm

    def body(u_ref, v_ref, z_ref, dy_ref, g_ref, b_ref, wm_ref, wmT_ref, bias_ref, mask_ref, sel_ref,
             dp_ref, dws_ref, dbs_ref, dg_ref, db_ref, vnb_ref, mixed_ref, dmb_ref, dvn_ref, dbias_ref):
        i = pl.program_id(0)

        @pl.when(i == 0)
        def _():
            dws_ref[...] = jnp.zeros_like(dws_ref)
            dg_ref[...] = jnp.zeros_like(dg_ref)
            db_ref[...] = jnp.zeros_like(db_ref)
            dbias_ref[...] = jnp.zeros_like(dbias_ref)

        vh, rs = _sgu_core(u_ref, v_ref, z_ref, g_ref, b_ref, wm_ref, bias_ref, vnb_ref, mixed_ref, tm)
        u = u_ref[...].astype(F32)
        z = z_ref[...].astype(F32)
        dy_v = dy_ref[...].astype(F32)
        mixed = mixed_ref[...]
        sg = _sigmoid(z)
        sz = z * sg
        dp_ref[:, 0:D] = (dy_v * mixed * sz).astype(BF16)
        dp_ref[:, 2 * D:3 * D] = (dy_v * u * mixed * (sg * (1.0 + z * (1.0 - sg)))).astype(BF16)
        dmixed = dy_v * u * sz
        dmb_ref[...] = dmixed.astype(BF16)
        for blk in range(tm // SGU_BLOCK):
            dbias_ref[...] += dmixed[blk * SGU_BLOCK:(blk + 1) * SGU_BLOCK, :]
        for blk in range(tm // SGU_BLOCK):
            rows = pl.ds(blk * SGU_BLOCK, SGU_BLOCK)
            for gi in range(SGU_GROUPS):
                cols = pl.ds(gi * LANE, LANE)
                dm = dmb_ref[rows, cols]
                dvn_ref[rows, cols] = _dot(wmT_ref[gi], dm)
                dws_ref[gi] += _dot_nt(dm, vnb_ref[rows, cols])
        dvn = dvn_ref[...]
        dg_ref[0:1, :] += jnp.sum(dvn * vh, axis=0, keepdims=True)
        db_ref[0:1, :] += jnp.sum(dvn, axis=0, keepdims=True)
        dvh = dvn * g_ref[...]
        dv = rs * (dvh - jnp.mean(dvh, axis=-1, keepdims=True) - vh * jnp.mean(dvh * vh, axis=-1, keepdims=True))
        dp_ref[:, D:2 * D] = dv.astype(BF16)

        @pl.when(i == nsteps - 1)
        def _():
            for gi in range(SGU_GROUPS):
                dws_ref[gi] = dws_ref[gi] * mask_ref[...]
            dbs_ref[...] = _dot(dbias_ref[...], sel_ref[...], precision=HI)

    seg = lambda off: pl.BlockSpec((tm, D), lambda i: (i, off // D))
    full = lambda a: pl.BlockSpec(a.shape, lambda i: (0,) * a.ndim)
    return pl.pallas_call(
        body, name="sgu_bwd", grid=(nsteps,),
        in_specs=[seg(OFF_U), seg(OFF_V), seg(OFF_ZA), pl.BlockSpec((tm, D), lambda i: (i, 0)),
                  full(g), full(b), full(wm), full(wmT), full(bias_full), full(mask), full(sel)],
        out_specs=[pl.BlockSpec((tm, 3 * D), lambda i: (i, 0)),
                   pl.BlockSpec((SGU_GROUPS, SGU_BLOCK, SGU_BLOCK), lambda i: (0, 0, 0)),
                   pl.BlockSpec((SGU_BLOCK, LANE), lambda i: (0, 0)),
                   pl.BlockSpec((8, D), lambda i: (0, 0)), pl.BlockSpec((8, D), lambda i: (0, 0))],
        out_shape=[jax.ShapeDtypeStruct((S, 3 * D), BF16),
                   jax.ShapeDtypeStruct((SGU_GROUPS, SGU_BLOCK, SGU_BLOCK), F32),
                   jax.ShapeDtypeStruct((SGU_BLOCK, LANE), F32),
                   jax.ShapeDtypeStruct((8, D), F32), jax.ShapeDtypeStruct((8, D), F32)],
        scratch_shapes=[pltpu.VMEM((tm, D), BF16), pltpu.VMEM((tm, D), F32), pltpu.VMEM((tm, D), BF16),
                        pltpu.VMEM((tm, D), F32), pltpu.VMEM((SGU_BLOCK, D), F32)],
        compiler_params=_cp(("arbitrary",)),
    )(proj, proj, proj, dy, g, b, wm, wmT, bias_full, mask, sel)


SSD_T = 2 * CHUNK
HALO = 8
HALO_BLK = 16


def _pair_masks():
    row = lax.broadcasted_iota(jnp.int32, (CHUNK, LANE), 0)
    lane = lax.broadcasted_iota(jnp.int32, (CHUNK, LANE), 1)
    pos = jnp.where(lane >= CHUNK, lane - CHUNK, lane)
    diag = (row == pos).astype(F32)
    causal = row >= pos
    lo = (lane < CHUNK).astype(F32)
    return diag, causal, lo, 1.0 - lo


def _ssd_chunk_fwd(c, ext_ref, dt_ref, cw_ref, cb_ref, dtb_ref, alog_ref, tri_ref, exp_ref, ht_ref):
    r0 = c * CHUNK
    pre = cb_ref[...] + sum(cw_ref[k:k + 1, :] * ext_ref[pl.ds(r0 + HALO - (CONV_K - 1) + k, CHUNK), :] for k in range(CONV_K))
    sg = _sigmoid(pre)
    xc = pre * sg
    dtr = dt_ref[pl.ds(r0, CHUNK), :].astype(F32) + dtb_ref[...]
    dtv = _softplus(dtr)
    A = -jnp.exp(alog_ref[...])
    acs = _sel_left(tri_ref[...], dtv * A)
    both = _sel_right(jnp.concatenate([acs, dtv], axis=0), exp_ref[...])
    E, dtE = both[0:CHUNK], both[CHUNK:2 * CHUNK]
    return dict(pre=pre, sg=sg, xc=xc, dtr=dtr, dtv=dtv, A=A, E=E, dtE=dtE)


def _ssd_fwd(proj, conv_w, conv_b, dtb_p, alog_p, d_exp, norm_w, tri, expand):
    S = proj.shape[0]
    T = SSD_T
    nsteps = S // T
    ncl = T // CHUNK

    def body(zb_ref, xbc_ref, halo_ref, dt_ref, cw_ref, cb_ref, dtb_ref, alog_ref, dexp_ref, nw_ref, tri_ref, exp_ref,
             y_ref, yb_ref, st_ref, ht_ref, ext_ref):
        i = pl.program_id(0)

        @pl.when(i == 0)
        def _():
            ht_ref[...] = jnp.zeros_like(ht_ref)
            ext_ref[0:HALO, :] = jnp.zeros((HALO, XBC_W), F32)

        @pl.when(i > 0)
        def _():
            ext_ref[0:HALO, :] = halo_ref[...].astype(F32)[HALO_BLK - HALO:HALO_BLK, :]

        ext_ref[HALO:HALO + T, :] = xbc_ref[...].astype(F32)
        diag, causal, lo, hi = _pair_masks()
        for c in range(ncl):
            q = _ssd_chunk_fwd(c, ext_ref, dt_ref, cw_ref, cb_ref, dtb_ref, alog_ref, tri_ref, exp_ref, ht_ref)
            rows = pl.ds(c * CHUNK, CHUNK)
            xc, E, dtE = q["xc"], q["E"], q["dtE"]
            xs = xc[:, 0:D]
            total = E[CHUNK - 1:CHUNK, :]
            x_dt = xs * dtE
            eE = jnp.exp(E)
            xw = x_dt * jnp.exp(total - E)
            st_ref[c] = ht_ref[...]
            for g in range(SSD_GROUPS):
                gc = slice(g * GROUP_W, (g + 1) * GROUP_W)
                Bg = xc[:, D + g * STATE:D + (g + 1) * STATE].astype(BF16)
                Cg = xc[:, D + SSD_GROUPS * STATE + g * STATE:D + SSD_GROUPS * STATE + (g + 1) * STATE].astype(BF16)
                cb2 = _dot_nt(Cg, jnp.concatenate([Bg, Bg], axis=0))
                htg = ht_ref[:, gc]
                y_ref[rows, gc] = eE[:, gc] * _dot(Cg, htg.astype(BF16)) + xs[:, gc] * dexp_ref[:, gc]
                for jj in range(GROUP_W // LANE):
                    pc = slice(g * GROUP_W + jj * LANE, g * GROUP_W + (jj + 1) * LANE)
                    Ej = E[:, pc]
                    e2 = jnp.sum(Ej * diag, axis=0, keepdims=True)
                    Mp = cb2 * jnp.exp(jnp.where(causal, Ej - e2, -1e30))
                    xj = x_dt[:, pc]
                    xbd = jnp.concatenate([xj * lo, xj * hi], axis=0).astype(BF16)
                    y_ref[rows, pc] += _dot(Mp.astype(BF16), xbd)
                ht_ref[:, gc] = jnp.exp(total[:, gc]) * htg + _dot_tn(Bg, xw[:, gc].astype(BF16))
            zb = zb_ref[rows, :].astype(F32)
            hh = y_ref[rows, :] * (zb * _sigmoid(zb))
            for g in range(SSD_GROUPS):
                gc = slice(g * GROUP_W, (g + 1) * GROUP_W)
                hg = hh[:, gc]
                r = lax.rsqrt(jnp.mean(hg * hg, axis=-1, keepdims=True) + EPS)
                yb_ref[rows, gc] = (hg * r * nw_ref[:, gc]).astype(BF16)

    full = lambda a: pl.BlockSpec(a.shape, lambda i: (0,) * a.ndim)
    hb = T // HALO_BLK
    return pl.pallas_call(
        body, name="ssd_fwd", grid=(nsteps,),
        in_specs=[pl.BlockSpec((T, D), lambda i: (i, OFF_ZB // D)),
                  pl.BlockSpec((T, XBC_W), lambda i: (i, OFF_XBC // XBC_W)),
                  pl.BlockSpec((HALO_BLK, XBC_W), lambda i: (jnp.maximum(i * hb - 1, 0), OFF_XBC // XBC_W)),
                  pl.BlockSpec((T, DT_W), lambda i: (i, OFF_DT // DT_W)),
                  full(conv_w), full(conv_b), full(dtb_p), full(alog_p), full(d_exp), full(norm_w), full(tri), full(expand)],
        out_specs=[pl.BlockSpec((T, D), lambda i: (i, 0)), pl.BlockSpec((T, D), lambda i: (i, 0)),
                   pl.BlockSpec((ncl, STATE, D), lambda i: (i, 0, 0))],
        out_shape=[jax.ShapeDtypeStruct((S, D), F32), jax.ShapeDtypeStruct((S, D), BF16),
                   jax.ShapeDtypeStruct((S // CHUNK, STATE, D), F32)],
        scratch_shapes=[pltpu.VMEM((STATE, D), F32), pltpu.VMEM((HALO + T, XBC_W), F32)],
        compiler_params=_cp(("arbitrary",)),
    )(proj, proj, proj, proj, conv_w, conv_b, dtb_p, alog_p, d_exp, norm_w, tri, expand)


def _ssd_bwd(proj, dyb, y, states, conv_w, conv_b, dtb_p, alog_p, d_exp, norm_w, tri, triT, expand, expandT):
    S = proj.shape[0]
    T = SSD_T
    nsteps = S // T
    ncl = T // CHUNK
    SSD_W = SSD_PAD_W

    def body(zb_ref, xbc_ref, halo_ref, dt_ref, dyb_ref, y_ref, st_ref, cw_ref, cb_ref, dtb_ref, alog_ref, dexp_ref, nw_ref,
             tri_ref, triT_ref, exp_ref, expT_ref,
             dp_ref, dcw_ref, dcb_ref, ddtb_ref, dalog_ref, dD_ref, dnw_ref,
             dht_ref, ext_ref, dpre_ref, dy_s, dE_s, dxdt_s, dxc_s, dDacc_ref, dAacc_ref):
        i = pl.program_id(0)

        @pl.when(i == 0)
        def _():
            for r in (dht_ref, dcw_ref, dcb_ref, ddtb_ref, dnw_ref, dDacc_ref, dAacc_ref):
                r[...] = jnp.zeros_like(r)
            dpre_ref[T:T + HALO, :] = jnp.zeros((HALO, XBC_W), F32)

        @pl.when(i == nsteps - 1)
        def _():
            ext_ref[0:HALO, :] = jnp.zeros((HALO, XBC_W), F32)

        @pl.when(i < nsteps - 1)
        def _():
            ext_ref[0:HALO, :] = halo_ref[...].astype(F32)[HALO_BLK - HALO:HALO_BLK, :]

        ext_ref[HALO:HALO + T, :] = xbc_ref[...].astype(F32)
        diag, causal, lo, hi = _pair_masks()
        last_row = (lax.broadcasted_iota(jnp.int32, (CHUNK, 1), 0) == CHUNK - 1).astype(F32)
        for c in reversed(range(ncl)):
            q = _ssd_chunk_fwd(c, ext_ref, dt_ref, cw_ref, cb_ref, dtb_ref, alog_ref, tri_ref, exp_ref, None)
            rows = pl.ds(c * CHUNK, CHUNK)
            pre, sg, xc, dtr, dtv, A, E, dtE = (q[k] for k in ("pre", "sg", "xc", "dtr", "dtv", "A", "E", "dtE"))
            xs = xc[:, 0:D]
            total = E[CHUNK - 1:CHUNK, :]
            x_dt = xs * dtE
            eE = jnp.exp(E)
            wdec = jnp.exp(total - E)
            zb = zb_ref[rows, :].astype(F32)
            yv = y_ref[rows, :]
            sgz = _sigmoid(zb)
            sz = zb * sgz
            hh = yv * sz
            for g in range(SSD_GROUPS):
                gc = slice(g * GROUP_W, (g + 1) * GROUP_W)
                hg = hh[:, gc]
                r = lax.rsqrt(jnp.mean(hg * hg, axis=-1, keepdims=True) + EPS)
                dyb_g = dyb_ref[rows, gc].astype(F32)
                dn = dyb_g * nw_ref[:, gc]
                dnw_ref[0:1, gc] += jnp.sum(dyb_g * hg * r, axis=0, keepdims=True)
                dy_s[:, gc] = r * dn - hg * (r * r * r) * jnp.mean(dn * hg, axis=-1, keepdims=True)
            dhh = dy_s[...]
            dp_ref[rows, 0:D] = (dhh * yv * (sgz * (1.0 + zb * (1.0 - sgz)))).astype(BF16)
            dy = dhh * sz
            dy_s[...] = dy
            dDacc_ref[0:1, :] += jnp.sum(dy * xs, axis=0, keepdims=True)
            dxc_s[:, 0:D] = dy * dexp_ref[...]
            for g in range(SSD_GROUPS):
                gc = slice(g * GROUP_W, (g + 1) * GROUP_W)
                bcol = slice(D + g * STATE, D + (g + 1) * STATE)
                ccol = slice(D + SSD_GROUPS * STATE + g * STATE, D + SSD_GROUPS * STATE + (g + 1) * STATE)
                Bg = xc[:, bcol].astype(BF16)
                Cg = xc[:, ccol].astype(BF16)
                B2 = jnp.concatenate([Bg, Bg], axis=0)
                cb2 = _dot_nt(Cg, B2)
                htg = st_ref[c, :, gc]
                htb = htg.astype(BF16)
                dhn = dht_ref[:, gc]
                dhnb = dhn.astype(BF16)
                dyg = dy[:, gc]
                eEg = eE[:, gc]
                wg = wdec[:, gc]
                xdg = x_dt[:, gc]
                CH = _dot(Cg, htb)
                dCHb = (dyg * eEg).astype(BF16)
                dC = _dot_nt(dCHb, htb)
                dl = jnp.exp(total[:, gc])
                dht_prev = _dot_tn(Cg, dCHb) + dl * dhn
                dtot = jnp.sum(dhn * htg, axis=0, keepdims=True) * dl
                dxw = _dot(Bg, dhnb)
                dB = _dot_nt((xdg * wg).astype(BF16), dhnb)
                dwd = dxw * xdg * wg
                dtot = dtot + jnp.sum(dwd, axis=0, keepdims=True)
                dE_s[:, gc] = dyg * eEg * CH - dwd + last_row * dtot
                dxdt_s[:, gc] = dxw * wg
                dcb2 = jnp.zeros((CHUNK, LANE), F32)
                for jj in range(GROUP_W // LANE):
                    pc = slice(g * GROUP_W + jj * LANE, g * GROUP_W + (jj + 1) * LANE)
                    Ej = E[:, pc]
                    e2 = jnp.sum(Ej * diag, axis=0, keepdims=True)
                    Lp = jnp.exp(jnp.where(causal, Ej - e2, -1e30))
                    Mp = cb2 * Lp
                    xj = x_dt[:, pc]
                    xbd = jnp.concatenate([xj * lo, xj * hi], axis=0).astype(BF16)
                    dyj = dy[:, pc].astype(BF16)
                    dMp = _dot_nt(dyj, xbd)
                    dxbd = _dot_tn(Mp.astype(BF16), dyj)
                    dxdt_s[:, pc] += dxbd[0:CHUNK, :] * lo + dxbd[CHUNK:2 * CHUNK, :] * hi
                    dcb2 = dcb2 + dMp * Lp
                    dseg = dMp * Mp
                    dE_s[:, pc] += dseg - diag * jnp.sum(dseg, axis=0, keepdims=True)
                dcb2b = dcb2.astype(BF16)
                dC = dC + _dot(dcb2b, B2)
                dB2 = _dot_tn(dcb2b, Cg)
                dB = dB + dB2[0:CHUNK, :] + dB2[CHUNK:2 * CHUNK, :]
                dxc_s[:, bcol] = dB
                dxc_s[:, ccol] = dC
                dht_ref[:, gc] = dht_prev
            dx_dt = dxdt_s[...]
            dxc_s[:, 0:D] += dx_dt * dtE
            red = _sel_right(jnp.concatenate([dE_s[...], dx_dt * xs], axis=0), expT_ref[...])
            da = _sel_left(triT_ref[...], red[0:CHUNK, :])
            ddtv = red[CHUNK:2 * CHUNK, :] + da * A
            dAacc_ref[0:1, :] += jnp.sum(da * dtv, axis=0, keepdims=True)
            ddtr = ddtv * _sigmoid(dtr)
            ddtb_ref[0:1, :] += jnp.sum(ddtr, axis=0, keepdims=True)
            dp_ref[rows, D + XBC_W:D + XBC_W + DT_W] = ddtr.astype(BF16)
            dpre = dxc_s[...] * (sg * (1.0 + pre * (1.0 - sg)))
            dpre_ref[rows, :] = dpre
            dcb_ref[0:1, :] += jnp.sum(dpre, axis=0, keepdims=True)
        dpre_t = dpre_ref[0:T, :]
        dxbc = jnp.zeros((T, XBC_W), F32)
        for k in range(CONV_K):
            dcw_ref[k:k + 1, :] += jnp.sum(dpre_t * ext_ref[pl.ds(HALO - (CONV_K - 1) + k, T), :], axis=0, keepdims=True)
            dxbc = dxbc + cw_ref[k:k + 1, :] * dpre_ref[pl.ds(CONV_K - 1 - k, T), :]
        dp_ref[:, D:D + XBC_W] = dxbc.astype(BF16)
        dp_ref[:, SEG_SSD[1]:SSD_W] = jnp.zeros((T, SSD_W - SEG_SSD[1]), BF16)
        dpre_ref[T:T + HALO, :] = dpre_ref[0:HALO, :]

        @pl.when(i == nsteps - 1)
        def _():
            dalog_ref[...] = dAacc_ref[...] * (-jnp.exp(alog_ref[...]))
            dD_ref[...] = _dot(dDacc_ref[...], expT_ref[...].astype(F32), precision=HI)

    full = lambda a: pl.BlockSpec(a.shape, lambda i: (0,) * a.ndim)
    hb = T // HALO_BLK
    rev = lambda i: nsteps - 1 - i
    acc = lambda w: pl.BlockSpec((8, w), lambda i: (0, 0))
    return pl.pallas_call(
        body, name="ssd_bwd", grid=(nsteps,),
        in_specs=[pl.BlockSpec((T, D), lambda i: (rev(i), OFF_ZB // D)),
                  pl.BlockSpec((T, XBC_W), lambda i: (rev(i), OFF_XBC // XBC_W)),
                  pl.BlockSpec((HALO_BLK, XBC_W), lambda i: (jnp.maximum(rev(i) * hb - 1, 0), OFF_XBC // XBC_W)),
                  pl.BlockSpec((T, DT_W), lambda i: (rev(i), OFF_DT // DT_W)),
                  pl.BlockSpec((T, D), lambda i: (rev(i), 0)), pl.BlockSpec((T, D), lambda i: (rev(i), 0)),
                  pl.BlockSpec((ncl, STATE, D), lambda i: (rev(i), 0, 0)),
                  full(conv_w), full(conv_b), full(dtb_p), full(alog_p), full(d_exp), full(norm_w),
                  full(tri), full(triT), full(expand), full(expandT)],
        out_specs=[pl.BlockSpec((T, SSD_W), lambda i: (rev(i), 0)),
                   acc(XBC_W), acc(XBC_W), acc(DT_W), acc(DT_W), acc(DT_W), acc(D)],
        out_shape=[jax.ShapeDtypeStruct((S, SSD_W), BF16),
                   jax.ShapeDtypeStruct((8, XBC_W), F32), jax.ShapeDtypeStruct((8, XBC_W), F32),
                   jax.ShapeDtypeStruct((8, DT_W), F32), jax.ShapeDtypeStruct((8, DT_W), F32),
                   jax.ShapeDtypeStruct((8, DT_W), F32), jax.ShapeDtypeStruct((8, D), F32)],
        scratch_shapes=[pltpu.VMEM((STATE, D), F32), pltpu.VMEM((HALO + T, XBC_W), F32), pltpu.VMEM((T + HALO, XBC_W), F32),
                        pltpu.VMEM((CHUNK, D), F32), pltpu.VMEM((CHUNK, D), F32), pltpu.VMEM((CHUNK, D), F32),
                        pltpu.VMEM((CHUNK, XBC_W), F32), pltpu.VMEM((8, D), F32), pltpu.VMEM((8, DT_W), F32)],
        compiler_params=_cp(("arbitrary",)),
    )(proj, proj, proj, proj, dyb, y, states, conv_w, conv_b, dtb_p, alog_p, d_exp, norm_w, tri, triT, expand, expandT)


def _head(x, ya, yb, proj, target, gate_b, wout, fw, *, tm):
    S = x.shape[0]

    def body(x_ref, ya_ref, yb_ref, gl0_ref, gl1_ref, t_ref, gb_ref, w_ref, fw_ref,
             dh_ref, dhb_ref, mb_ref, dya_ref, dyb_ref, dgl_ref, loss_ref, dfw_ref, dgb_ref):
        @pl.when(pl.program_id(0) == 0)
        def _():
            loss_ref[...] = jnp.zeros_like(loss_ref)
            dfw_ref[...] = jnp.zeros_like(dfw_ref)
            dgb_ref[...] = jnp.zeros_like(dgb_ref)

        ya_v = ya_ref[...].astype(F32)
        yb_v = yb_ref[...].astype(F32)
        g0 = _sigmoid(gl0_ref[...].astype(F32) + gb_ref[:, 0:D])
        g1 = _sigmoid(gl1_ref[...].astype(F32) + gb_ref[:, D:2 * D])
        mb = (g0 * ya_v + g1 * yb_v).astype(BF16)
        mb_ref[...] = mb
        h = x_ref[...] + _dot(mb, w_ref[...])
        r = lax.rsqrt(jnp.mean(h * h, axis=-1, keepdims=True) + EPS)
        hn = h * r
        err = hn * fw_ref[...] - t_ref[...]
        loss_ref[...] += 0.5 * jnp.sum(jnp.mean(err * err, axis=-1, keepdims=True))
        dyf = err * (1.0 / D)
        dfw_ref[0:1, :] += jnp.sum(dyf * hn, axis=0, keepdims=True)
        dhn = dyf * fw_ref[...]
        dh = r * (dhn - hn * jnp.mean(dhn * hn, axis=-1, keepdims=True))
        dh_ref[...] = dh
        dhb = dh.astype(BF16)
        dhb_ref[...] = dhb
        dm = _dot_nt(dhb, w_ref[...])
        dya_ref[...] = (dm * g0).astype(BF16)
        dyb_ref[...] = (dm * g1).astype(BF16)
        dgl0 = dm * ya_v * g0 * (1.0 - g0)
        dgl1 = dm * yb_v * g1 * (1.0 - g1)
        dgl_ref[:, 0:D] = dgl0.astype(BF16)
        dgl_ref[:, D:2 * D] = dgl1.astype(BF16)
        dgb_ref[0:1, 0:D] += jnp.sum(dgl0, axis=0, keepdims=True)
        dgb_ref[0:1, D:2 * D] += jnp.sum(dgl1, axis=0, keepdims=True)

    row = pl.BlockSpec((tm, D), lambda i: (i, 0))
    seg = lambda off: pl.BlockSpec((tm, D), lambda i: (i, off // D))
    full = lambda a: pl.BlockSpec(a.shape, lambda i: (0,) * a.ndim)
    acc = lambda w: pl.BlockSpec((8, w), lambda i: (0, 0))
    return pl.pallas_call(
        body, name="head", grid=(S // tm,),
        in_specs=[row, row, row, seg(OFF_G0), seg(OFF_G1), row, full(gate_b), full(wout), full(fw)],
        out_specs=[row, row, row, row, row, pl.BlockSpec((tm, 2 * D), lambda i: (i, 0)), acc(LANE), acc(D), acc(2 * D)],
        out_shape=[jax.ShapeDtypeStruct((S, D), F32), jax.ShapeDtypeStruct((S, D), BF16), jax.ShapeDtypeStruct((S, D), BF16),
                   jax.ShapeDtypeStruct((S, D), BF16), jax.ShapeDtypeStruct((S, D), BF16), jax.ShapeDtypeStruct((S, 2 * D), BF16),
                   jax.ShapeDtypeStruct((8, LANE), F32), jax.ShapeDtypeStruct((8, D), F32), jax.ShapeDtypeStruct((8, 2 * D), F32)],
        compiler_params=_cp(("arbitrary",)),
    )(x, ya, yb, proj, proj, target, gate_b, wout, fw)


def _adam_update(g, w_ref, m_ref, v_ref, g_ref, d_ref, m2_ref, v2_ref):
    m2 = ADAM_B1 * m_ref[...] + (1.0 - ADAM_B1) * g
    v2 = ADAM_B2 * v_ref[...] + (1.0 - ADAM_B2) * (g * g)
    m_hat = m2 / (1.0 - ADAM_B1 ** ADAM_STEP)
    v_hat = v2 / (1.0 - ADAM_B2 ** ADAM_STEP)
    g_ref[...] = g
    d_ref[...] = -ADAM_LR * (m_hat / (jnp.sqrt(v_hat) + ADAM_EPS) + ADAM_WD * w_ref[...])
    m2_ref[...] = m2
    v2_ref[...] = v2


def _adamw_own(me, own, landed, w, m, v, *, tr, tc, name):
    _, R, C = landed.shape
    assert R % tr == 0 and C % tc == 0, (name, R, C, tr, tc)

    def body(me_ref, own_ref, p_ref, w_ref, m_ref, v_ref, g_ref, d_ref, m2_ref, v2_ref):
        mine = own_ref[0].astype(F32)
        g = jnp.where(me_ref[0] == 0, mine, p_ref[0].astype(F32))
        for k in range(1, N_DEV):
            g = g + jnp.where(me_ref[0] == k, mine, p_ref[k].astype(F32))
        _adam_update(g, w_ref, m_ref, v_ref, g_ref, d_ref, m2_ref, v2_ref)

    tile = pl.BlockSpec((tr, tc), lambda i, j, me_ref: (i, j))
    return pl.pallas_call(
        body, name=name,
        grid_spec=pltpu.PrefetchScalarGridSpec(
            num_scalar_prefetch=1, grid=(R // tr, C // tc),
            in_specs=[pl.BlockSpec((1, tr, tc), lambda i, j, me_ref: (me_ref[0], i, j)),
                      pl.BlockSpec((N_DEV, tr, tc), lambda i, j, me_ref: (0, i, j)), tile, tile, tile],
            out_specs=[tile, tile, tile, tile]),
        out_shape=[jax.ShapeDtypeStruct((R, C), F32)] * 4,
        compiler_params=_cp(("parallel", "parallel")),
    )(me, own, landed, w, m, v)


def _adamw(parts, w, m, v, *, tr, name):
    _, R, C = parts.shape
    assert R % tr == 0, (name, R, tr)

    def body(p_ref, w_ref, m_ref, v_ref, g_ref, d_ref, m2_ref, v2_ref):
        g = p_ref[0].astype(F32)
        for k in range(1, N_DEV):
            g = g + p_ref[k].astype(F32)
        _adam_update(g, w_ref, m_ref, v_ref, g_ref, d_ref, m2_ref, v2_ref)

    row = pl.BlockSpec((tr, C), lambda i: (i, 0))
    return pl.pallas_call(
        body, name=name, grid=(R // tr,),
        in_specs=[pl.BlockSpec((N_DEV, tr, C), lambda i: (0, i, 0)), row, row, row],
        out_specs=[row, row, row, row],
        out_shape=[jax.ShapeDtypeStruct((R, C), F32)] * 4,
        compiler_params=_cp(("parallel",)),
    )(parts, w, m, v)


def _place():
    x, y, c = lax.axis_index("x"), lax.axis_index("y"), lax.axis_index("c")
    return x, y, c


def _all_gather(arrs, *, name):
    n = len(arrs)

    def body(*refs):
        ins, outs = refs[:n], refs[n:2 * n]
        send_sems, recv_sems, local_sems = refs[2 * n:]
        x, y, c = _place()
        me, sibling = (x, y, c), (x, y, 1 - c)
        chips = [(1 - x, y), (x, 1 - y), (1 - x, 1 - y)]

        def idx(px, py, pc):
            return 4 * px + 2 * py + pc

        def copy(k, a, block, to, src=None):
            slab = outs[a].at[idx(*block)]
            return pltpu.make_async_remote_copy(
                src_ref=slab if src is None else src, dst_ref=slab,
                send_sem=send_sems.at[k, a], recv_sem=recv_sems.at[k, a], device_id=to, device_id_type=MESH)

        mine = [pltpu.make_async_copy(ins[a], outs[a].at[idx(*me)], local_sems.at[a]) for a in range(n)]
        for cp in mine:
            cp.start()
        first = []
        for a in range(n):
            first.append(copy(0, a, me, sibling, src=ins[a]))
            first += [copy(1 + j, a, me, (*chip, c), src=ins[a]) for j, chip in enumerate(chips)]
        for cp in first:
            cp.start()
        passed = []
        for j, chip in enumerate(chips):
            for a in range(n):
                copy(1 + j, a, (*chip, c), me).wait_recv()
                fwd = copy(4 + j, a, (*chip, c), sibling)
                fwd.start()
                passed.append(fwd)
        for a in range(n):
            copy(0, a, sibling, me).wait_recv()
            for j, chip in enumerate(chips):
                copy(4 + j, a, (*chip, 1 - c), me).wait_recv()
        for cp in first + passed:
            cp.wait_send()
        for cp in mine:
            cp.wait()

    anyspec = pl.BlockSpec(memory_space=pl.ANY)
    return pl.pallas_call(
        body, name=name,
        in_specs=[anyspec] * n, out_specs=[anyspec] * n,
        out_shape=[jax.ShapeDtypeStruct((N_DEV,) + a.shape, a.dtype) for a in arrs],
        scratch_shapes=[pltpu.SemaphoreType.DMA((7, n)), pltpu.SemaphoreType.DMA((7, n)), pltpu.SemaphoreType.DMA((n,))],
    )(*arrs)


W_ROWS = SEG_SSD[0] + SSD_PAD_W
ROW_TILE = (16, LANE)


def _shard_pieces(k):
    out = []
    for lo, hi, dst in ((0, 6144, 0), (6144, 11296, SEG_SSD[0]), (11296, W_IN, SEG_GATE[0])):
        a, b = max(lo, k * SHARD_IN), min(hi, (k + 1) * SHARD_IN)
        if a < b:
            out.append((a - k * SHARD_IN, b - a, dst + a - lo))
    return out


def _gather_weights(win3, wout, cw, zeros3):
    n_zero = zeros3.shape[0]
    assert W_IN + n_zero == W_ROWS

    def run(k, win_ref, wout_ref, cw_ref, z_ref, w_out_ref, gout_ref, gcw_ref, send_sems, recv_sems, local_sems):
        x, y, c = k // 4, (k // 2) % 2, k % 2
        idx = lambda p: 4 * p[0] + 2 * p[1] + p[2]
        me, sib = (x, y, c), (x, y, 1 - c)
        xn, yn, dg = (1 - x, y, c), (x, 1 - y, c), (1 - x, 1 - y, c)

        def copies(slot, block, to, own=False):
            kb = idx(block)
            out = []
            for j, (s0, n, d0) in enumerate(_shard_pieces(kb)):
                dst = w_out_ref.at[pl.ds(d0, n)]
                out.append((win_ref.at[pl.ds(s0, n)] if own else dst, dst, j))
            out.append((wout_ref if own else gout_ref.at[kb], gout_ref.at[kb], 2))
            out.append((cw_ref if own else gcw_ref.at[kb], gcw_ref.at[kb], 3))
            return [pltpu.make_async_remote_copy(src_ref=s, dst_ref=d, send_sem=send_sems.at[slot, j], recv_sem=recv_sems.at[slot, j],
                                                 device_id=to, device_id_type=MESH) for s, d, j in out]

        def start(cps):
            for cp in cps:
                cp.start()
            return cps

        def arrived(slot, block):
            for cp in copies(slot, block, me):
                cp.wait_recv()

        local = [pltpu.make_async_copy(s, d, local_sems.at[j]) for j, (s, d) in enumerate(
            [(win_ref.at[pl.ds(s0, n)], w_out_ref.at[pl.ds(d0, n)]) for s0, n, d0 in _shard_pieces(k)]
            + [(wout_ref, gout_ref.at[k]), (cw_ref, gcw_ref.at[k]), (z_ref, w_out_ref.at[pl.ds(W_IN, n_zero)])])]
        for cp in local:
            cp.start()
        sent = start(copies(0, me, sib, own=True)) + start(copies(1, me, xn, own=True)) + start(copies(2, me, yn, own=True))
        arrived(1, xn)
        sent += start(copies(4, xn, sib))
        if c == 1:
            sent += start(copies(3, xn, yn))
        arrived(2, yn)
        sent += start(copies(5, yn, sib))
        if c == 0:
            sent += start(copies(3, yn, xn))
        arrived(3, dg)
        sent += start(copies(6, dg, sib))
        arrived(0, sib)
        arrived(4, (1 - x, y, 1 - c))
        arrived(5, (x, 1 - y, 1 - c))
        arrived(6, (1 - x, 1 - y, 1 - c))
        for cp in sent:
            cp.wait_send()
        for cp in local:
            cp.wait()

    def body(*refs):
        x, y, c = _place()
        me = 4 * x + 2 * y + c
        for k in range(N_DEV):
            pl.when(me == k)(functools.partial(run, k, *refs))

    anyspec = pl.BlockSpec(memory_space=pl.ANY)
    return pl.pallas_call(
        body, name="gather_weights", in_specs=[anyspec] * 4, out_specs=[anyspec] * 3,
        out_shape=[jax.ShapeDtypeStruct((W_ROWS,) + ROW_TILE, win3.dtype),
                   jax.ShapeDtypeStruct((N_DEV,) + wout.shape, wout.dtype), jax.ShapeDtypeStruct((N_DEV,) + cw.shape, cw.dtype)],
        scratch_shapes=[pltpu.SemaphoreType.DMA((7, 4)), pltpu.SemaphoreType.DMA((7, 4)), pltpu.SemaphoreType.DMA((5,))],
    )(win3, wout, cw, zeros3)


_REL = [(dx, dy, dc) for dx in (0, 1) for dy in (0, 1) for dc in (0, 1)][1:]
_HBM = pl.BlockSpec(memory_space=pltpu.HBM)
_SEM = pl.BlockSpec(memory_space=pltpu.SEMAPHORE)
_EFFECT = pltpu.SideEffectType.DATAFLOW_SIDE_EFFECTING


def _peer(k):
    x, y, c = _place()
    dx, dy, dc = _REL[k]
    return (1 - x if dx else x, 1 - y if dy else y, 1 - c if dc else c)


def _exchange_start(parts, *, name):
    n = len(parts)

    def body(*refs):
        ins, lands = refs[:n], refs[n:2 * n]
        send_sems, recv_sems, token = refs[2 * n], refs[2 * n + 1], refs[-1]
        x, y, c = _place()
        me = 4 * x + 2 * y + c
        for a in range(n):
            for k in range(len(_REL)):
                px, py, pc = _peer(k)
                pltpu.make_async_remote_copy(
                    src_ref=ins[a].at[4 * px + 2 * py + pc], dst_ref=lands[a].at[me],
                    send_sem=send_sems.at[len(_REL) * a + k], recv_sem=recv_sems.at[len(_REL) * a + k],
                    device_id=(px, py, pc), device_id_type=MESH).start()
        token[...] = jnp.zeros_like(token)

    sem = pltpu.SemaphoreType.DMA((len(_REL) * n,))
    bufs = [pltpu.HBM(p.shape, p.dtype) for p in parts]
    outs = pl.pallas_call(
        body, name=name,
        out_shape=(sem, sem, *bufs, *bufs, jax.ShapeDtypeStruct((8, LANE), F32)),
        in_specs=(_HBM,) * (2 * n), out_specs=(_SEM, _SEM, *(_HBM,) * (2 * n), pl.BlockSpec(memory_space=pltpu.VMEM)),
        input_output_aliases={i: 2 + i for i in range(2 * n)},
        compiler_params=pltpu.CompilerParams(has_side_effects=_EFFECT),
    )(*[pltpu.with_memory_space_constraint(p, pltpu.HBM) for p in parts],
      *[pltpu.with_memory_space_constraint(lax.empty(p.shape, p.dtype), pltpu.HBM) for p in parts])
    return outs[0], outs[1], outs[2:2 + n], outs[2 + n:2 + 2 * n], outs[-1]


def _exchange_wait(send_sems, recv_sems, parts, lands, after, *, name):
    n = len(parts)

    def body(*refs):
        ins, lands_ = refs[:n], refs[n:2 * n]
        ssem, rsem = refs[2 * n], refs[2 * n + 1]
        for a in range(n):
            for k in range(len(_REL)):
                px, py, pc = _peer(k)
                p = 4 * px + 2 * py + pc
                cp = pltpu.make_async_remote_copy(
                    src_ref=ins[a].at[p], dst_ref=lands_[a].at[p],
                    send_sem=ssem.at[len(_REL) * a + k], recv_sem=rsem.at[len(_REL) * a + k],
                    device_id=(px, py, pc), device_id_type=MESH)
                cp.wait_send()
                cp.wait_recv()

    bufs = [pltpu.HBM(p.shape, p.dtype) for p in parts]
    outs = pl.pallas_call(
        body, name=name, out_shape=(*bufs, *bufs),
        in_specs=(*(_HBM,) * (2 * n), _SEM, _SEM, pl.BlockSpec(memory_space=pl.ANY)), out_specs=(_HBM,) * (2 * n),
        input_output_aliases={i: i for i in range(2 * n)},
        compiler_params=pltpu.CompilerParams(has_side_effects=_EFFECT),
    )(*parts, *lands, send_sems, recv_sems, after)
    return outs[:n], outs[n:]


WEIGHTS = ('norm_w', 'w_in', 'gate_b', 'sgu_norm_g', 'sgu_norm_b', 'sgu_w', 'sgu_b', 'conv_w', 'conv_b', 'dt_bias', 'A_log',
           'D_skip', 'ssd_norm_w', 'w_out', 'final_norm_w')
SHARDED = ('w_in', 'conv_w', 'w_out')
PACK_ROW = 8 * LANE


def _constants():
    tri = np.tril(np.ones((CHUNK, CHUNK), np.float32))
    expand = np.zeros((DT_W, D), np.float32)
    for h in range(HEADS):
        expand[h, h * HEADDIM:(h + 1) * HEADDIM] = 1.0
    sel = np.zeros((D, LANE), np.float32)
    for g in range(SGU_GROUPS):
        sel[g * LANE:(g + 1) * LANE, g] = 1.0
    pos_chunk = np.arange(SGU_BLOCK) // CHUNK
    mask = (pos_chunk[None, :] <= pos_chunk[:, None]).astype(np.float32)
    return dict(tri=jnp.asarray(tri, BF16), triT=jnp.asarray(tri.T.copy(), BF16), expand=jnp.asarray(expand, BF16),
                expandT=jnp.asarray(expand.T.copy(), BF16), sel=jnp.asarray(sel), mask=jnp.asarray(mask))


def _to_shards(segs):
    starts = np.cumsum([0] + [s.shape[0] for s in segs])
    assert starts[-1] == W_IN
    slabs = []
    for k in range(N_DEV):
        pieces = []
        for s, s0 in zip(segs, starts[:-1]):
            lo, hi = max(k * SHARD_IN, s0), min((k + 1) * SHARD_IN, s0 + s.shape[0])
            if lo < hi:
                pieces.append(s[lo - s0:hi - s0])
        slabs.append(jnp.concatenate(pieces, axis=0))
    return jnp.stack(slabs)


def _local_step(x2, tgt, wpT, wout, cw, p, exchange):
    S = x2.shape[0]
    k = _constants()
    xn = _norm_fwd(x2, p['norm_w'], tm=min(512, S))
    proj = _matmul(xn, wpT, trans_b=True, n=WP, out_dtype=BF16, tm=min(1024, S), tn=1408, tk=D, name="in_proj")
    wm32 = p['sgu_w'][0] * k['mask']
    wm = wm32.astype(BF16)
    wmT = jnp.swapaxes(wm32, 1, 2).astype(BF16)
    bias_full = jnp.repeat(p['sgu_b'][0].T, LANE, axis=1)
    tm_sgu = min(256, S)
    ya = _sgu_fwd(proj, p['sgu_norm_g'], p['sgu_norm_b'], wm, bias_full, tm=tm_sgu)
    pad32 = lambda a: jnp.pad(a, ((0, 0), (0, DT_W - HEADS)))
    dtb_p, alog_p = pad32(p['dt_bias']), pad32(p['A_log'])
    d_exp = jnp.repeat(p['D_skip'], HEADDIM, axis=1)
    ssd_args = (cw, p['conv_b'], dtb_p, alog_p, d_exp, p['ssd_norm_w'])
    y, yb, states = _ssd_fwd(proj, *ssd_args, k['tri'], k['expand'])
    dh, dhb, mb, dya, dyb, dgl, loss, dfw, dgb = _head(
        x2, ya, yb, proj, tgt, p['gate_b'], wout, p['final_norm_w'][None, :], tm=min(256, S))
    dsgu, dws, dbsT, dsg, dsb = _sgu_bwd(proj, dya, p['sgu_norm_g'], p['sgu_norm_b'], wm, wmT, bias_full, k['mask'], k['sel'],
                                         tm=tm_sgu)
    dssd, dcw, dcb, ddtb, dalog, dD, dnw = _ssd_bwd(proj, dyb, y, states, *ssd_args, k['tri'], k['triT'], k['expand'], k['expandT'])
    tk = min(2048, S)
    tn = 1024
    dwT_sgu = _matmul(dsgu, xn, trans_a=True, out_dtype=BF16, tm=1024, tn=tn, tk=tk, name="dw_in_sgu")
    dwT_gate = _matmul(dgl, xn, trans_a=True, out_dtype=BF16, tm=1024, tn=tn, tk=tk, name="dw_in_gate")
    dwT_ssd = _matmul(dssd, xn, trans_a=True, out_dtype=BF16, tm=1024, tn=tn, tk=tk, name="dw_in_ssd")
    dw_out = _matmul(mb, dhb, trans_a=True, out_dtype=BF16, tm=1024, tn=tn, tk=tk, name="dw_out")
    token = exchange([dwT_sgu, dwT_ssd[:W_IN - SEG_SSD[0]], dwT_gate], dw_out)
    tm = min(1024, S)
    dxn = _matmul(dsgu, wpT, tm=tm, tn=tn, tk=2048, after=token, name="dxn_sgu")
    dxn = _matmul(dgl, wpT, b_koff=SEG_GATE[0] // 2048, tm=tm, tn=tn, tk=2048, add=dxn, name="dxn_gate")
    dxn = _matmul(dssd, wpT, b_koff=SEG_SSD[0] // 2048, tm=tm, tn=tn, tk=2048, add=dxn, name="dxn_ssd")
    grad_x, dnorm = _norm_bwd(x2, p['norm_w'], dxn, dh, tm=min(256, S))
    grads = dict(
        norm_w=dnorm[0:1], gate_b=dgb[0:1], sgu_norm_g=dsg[0:1], sgu_norm_b=dsb[0:1], sgu_w=dws[None],
        sgu_b=dbsT[:, :SGU_GROUPS].T[None], conv_w=dcw[0:CONV_K][None], conv_b=dcb[0:1], dt_bias=ddtb[0:1, :HEADS],
        A_log=dalog[0:1, :HEADS], D_skip=dD[0:1, :HEADS], ssd_norm_w=dnw[0:1], final_norm_w=dfw[0])
    return loss[0, 0], grad_x, grads


def _pack(arrs):
    rows, offs, r = [], [], 0
    for a in arrs:
        n = a.size
        nr = -(-n // PACK_ROW) * 8
        rows.append(jnp.pad(a.reshape(-1).astype(F32), (0, nr * LANE - n)).reshape(nr, LANE))
        offs.append(r)
        r += nr
    return jnp.concatenate(rows, axis=0), offs


def kernel(x, norm_w, w_in, gate_b, sgu_norm_g, sgu_norm_b, sgu_w, sgu_b, conv_w, conv_b, dt_bias, A_log, D_skip, ssd_norm_w, w_out, final_norm_w, loss_target, m_norm_w, m_w_in, m_gate_b, m_sgu_norm_g, m_sgu_norm_b, m_sgu_w, m_sgu_b, m_conv_w, m_conv_b, m_dt_bias, m_A_log, m_D_skip, m_ssd_norm_w, m_w_out, m_final_norm_w, v_norm_w, v_w_in, v_gate_b, v_sgu_norm_g, v_sgu_norm_b, v_sgu_w, v_sgu_b, v_conv_w, v_conv_b, v_dt_bias, v_A_log, v_D_skip, v_ssd_norm_w, v_w_out, v_final_norm_w):
    w = dict(norm_w=norm_w, w_in=w_in, gate_b=gate_b, sgu_norm_g=sgu_norm_g, sgu_norm_b=sgu_norm_b, sgu_w=sgu_w, sgu_b=sgu_b,
             conv_w=conv_w, conv_b=conv_b, dt_bias=dt_bias, A_log=A_log, D_skip=D_skip, ssd_norm_w=ssd_norm_w, w_out=w_out,
             final_norm_w=final_norm_w)
    m = dict(norm_w=m_norm_w, w_in=m_w_in, gate_b=m_gate_b, sgu_norm_g=m_sgu_norm_g, sgu_norm_b=m_sgu_norm_b, sgu_w=m_sgu_w,
             sgu_b=m_sgu_b, conv_w=m_conv_w, conv_b=m_conv_b, dt_bias=m_dt_bias, A_log=m_A_log, D_skip=m_D_skip,
             ssd_norm_w=m_ssd_norm_w, w_out=m_w_out, final_norm_w=m_final_norm_w)
    v = dict(norm_w=v_norm_w, w_in=v_w_in, gate_b=v_gate_b, sgu_norm_g=v_sgu_norm_g, sgu_norm_b=v_sgu_norm_b, sgu_w=v_sgu_w,
             sgu_b=v_sgu_b, conv_w=v_conv_w, conv_b=v_conv_b, dt_bias=v_dt_bias, A_log=v_A_log, D_skip=v_D_skip,
             ssd_norm_w=v_ssd_norm_w, w_out=v_w_out, final_norm_w=v_final_norm_w)
    me = 4 * lax.axis_index("x") + 2 * lax.axis_index("y") + lax.axis_index("c")
    shard_cw = XBC_W // N_DEV

    tpose = lambda a: jnp.swapaxes(a[0], 0, 1)
    w3, g_out, g_cw = _gather_weights(tpose(w_in).astype(BF16).reshape((SHARD_IN,) + ROW_TILE), w_out[0].astype(BF16),
                                      conv_w[0], jnp.zeros((W_ROWS - W_IN,) + ROW_TILE, BF16))
    wpT = w3.reshape(W_ROWS, D)
    wout_full = g_out.reshape(D, D)
    cw_full = jnp.swapaxes(g_cw, 0, 1).reshape(CONV_K, XBC_W)

    flight = {}

    def exchange(dw_inT_segs, dw_out):
        parts = [_to_shards(dw_inT_segs), dw_out.reshape(N_DEV, D // N_DEV, D)]
        flight['sems'], flight['rsems'], flight['parts'], flight['lands'], token = _exchange_start(parts, name="exchange_start")
        return token

    loss_part, grad_x, grads = _local_step(x[0], loss_target[0], wpT, wout_full, cw_full, w, exchange)
    (own_in, own_out), (land_in, land_out) = _exchange_wait(
        flight['sems'], flight['rsems'], flight['parts'], flight['lands'], grad_x, name="exchange_wait")
    me_arr = jnp.reshape(me, (1,)).astype(jnp.int32)
    res = {}
    res['w_in'] = [jnp.swapaxes(o, 0, 1) for o in _adamw_own(
        me_arr, own_in, land_in, tpose(w_in), tpose(m_w_in), tpose(v_w_in), tr=SHARD_IN, tc=256, name="adamw_w_in")]
    res['w_out'] = _adamw_own(me_arr, own_out, land_out, w_out[0], m_w_out[0], v_w_out[0], tr=128, tc=D, name="adamw_w_out")

    small = [n for n in WEIGHTS if n not in SHARDED]
    packed, offs = _pack([grads[n] for n in small] + [loss_part, grads['conv_w']])
    (gathered,) = _all_gather([packed], name="gather_small")
    off_loss, off_cw = offs[-2], offs[-1]
    cw_parts = gathered[:, off_cw:, :].reshape(N_DEV, CONV_K, XBC_W)
    cw_parts = lax.dynamic_slice_in_dim(cw_parts, me * shard_cw, shard_cw, axis=2)
    cw_rows = _pack([cw_parts[0]])[0].shape[0]
    cw_parts = jnp.pad(cw_parts.reshape(N_DEV, -1), ((0, 0), (0, cw_rows * LANE - CONV_K * shard_cw))).reshape(N_DEV, cw_rows, LANE)
    parts = jnp.concatenate([gathered[:, :off_cw, :], cw_parts], axis=1)
    zero = jnp.zeros((), F32)
    packs = [_pack([d[n] for n in small] + [zero, d['conv_w']])[0] for d in (w, m, v)]
    outs = _adamw(parts, *packs, tr=parts.shape[1], name="adamw_small")

    def unpack(o, name):
        if name == 'conv_w':
            return o[off_cw:off_cw + cw_rows].reshape(-1)[:CONV_K * shard_cw].reshape(w['conv_w'].shape)
        r0 = offs[small.index(name)]
        n = w[name].size
        return o[r0:r0 + -(-n // PACK_ROW) * 8].reshape(-1)[:n].reshape(w[name].shape)

    for n in small + ['conv_w']:
        res[n] = [unpack(o, n) for o in outs]
    for n in ('w_in', 'w_out'):
        res[n] = [o[None] for o in res[n]]
    loss = outs[0][off_loss, 0]
    return (loss, grad_x[None], *[res[n][0] for n in WEIGHTS], *[res[n][1] for n in WEIGHTS],
            *[res[n][2] for n in WEIGHTS], *[res[n][3] for n in WEIGHTS])
```

```python
import functools

import numpy as np
import jax
import jax.numpy as jnp
from jax import lax
from jax.experimental import pallas as pl
from jax.experimental.pallas import tpu as pltpu

F32 = jnp.float32
BF16 = jnp.bfloat16
HI = lax.Precision.HIGHEST
MESH = pl.DeviceIdType.MESH

D = 2048
EPS = 1e-5
SGU_BLOCK = 128
SGU_GROUPS = 16
CHUNK = 64
HEADS = 32
HEADDIM = 64
SSD_GROUPS = 4
GROUP_W = D // SSD_GROUPS
STATE = 128
CONV_K = 4
XBC_W = D + 2 * SSD_GROUPS * STATE
W_IN = 15392
N_DEV = 8
SHARD_IN = W_IN // N_DEV
ADAM_LR, ADAM_B1, ADAM_B2, ADAM_EPS, ADAM_WD, ADAM_STEP = 0.001, 0.9, 0.999, 1e-08, 0.01, 10

LANE = 128
DT_W = LANE
OFF_U, OFF_V, OFF_ZA, OFF_G0, OFF_G1, OFF_ZB, OFF_XBC, OFF_DT = 0, 2048, 4096, 6144, 8192, 10240, 12288, 15360
WP = OFF_DT + DT_W
SEG_SGU = (0, 6144)
SEG_GATE = (6144, 4096)
SEG_SSD = (10240, WP - 10240)
SSD_PAD_W = 6144
VMEM_LIMIT = 56 * 1024 * 1024


def _cp(sem=None, vmem=VMEM_LIMIT):
    return pltpu.CompilerParams(dimension_semantics=sem, vmem_limit_bytes=vmem)


def _sigmoid(x):
    return 1.0 / (1.0 + jnp.exp(-x))


def _softplus(x):
    return jnp.maximum(x, 0.0) + jnp.log(1.0 + jnp.exp(-jnp.abs(x)))


def _dot(a, b, precision=None):
    return jnp.dot(a, b, preferred_element_type=F32, precision=precision)


def _dot_nt(a, b, precision=None):
    return lax.dot_general(a, b, (((1,), (1,)), ((), ())), preferred_element_type=F32, precision=precision)


def _dot_tn(a, b, precision=None):
    return lax.dot_general(a, b, (((0,), (0,)), ((), ())), preferred_element_type=F32, precision=precision)


def _split3(a):
    hi = a.astype(BF16)
    r = a - hi.astype(F32)
    mid = r.astype(BF16)
    return hi, mid, (r - mid.astype(F32)).astype(BF16)


def _sel_right(a, sel01):
    m = a.shape[0]
    r = _dot(jnp.concatenate(_split3(a), axis=0), sel01)
    return (r[0:m] + r[m:2 * m]) + r[2 * m:3 * m]


def _sel_left(sel01, a):
    n = a.shape[1]
    r = _dot(sel01, jnp.concatenate(_split3(a), axis=1))
    return (r[:, 0:n] + r[:, n:2 * n]) + r[:, 2 * n:3 * n]


def _matmul(a, b, *, trans_a=False, trans_b=False, b_koff=0, n=None, out_dtype=F32, tm, tn, tk, add=None, after=None, name):
    K, M = a.shape if trans_a else a.shape[::-1]
    N = (n or b.shape[0]) if trans_b else b.shape[1]
    assert M % tm == 0 and N % tn == 0 and K % tk == 0 and not (trans_a and trans_b), (name, M, N, K, tm, tn, tk)
    nk = K // tk

    def body(*refs):
        a_ref, b_ref = refs[:2]
        add_ref = refs[2] if add is not None else None
        o_ref, acc_ref = refs[-2:]
        k = pl.program_id(2)
        if trans_a:
            part = _dot_tn(a_ref[...], b_ref[...])
        else:
            part = _dot_nt(a_ref[...], b_ref[...]) if trans_b else _dot(a_ref[...], b_ref[...])

        def result(r):
            if add_ref is not None:
                r = r + add_ref[...]
            return r.astype(out_dtype)

        if nk == 1:
            o_ref[...] = result(part)
        else:
            @pl.when(k == 0)
            def _():
                acc_ref[...] = part

            @pl.when(jnp.logical_and(k > 0, k < nk - 1))
            def _():
                acc_ref[...] += part

            @pl.when(k == nk - 1)
            def _():
                o_ref[...] = result(acc_ref[...] + part)

    in_specs = [pl.BlockSpec((tk, tm), lambda i, j, k: (k, i)) if trans_a else pl.BlockSpec((tm, tk), lambda i, j, k: (i, k)),
                pl.BlockSpec((tn, tk), lambda i, j, k: (j, k)) if trans_b else pl.BlockSpec((tk, tn), lambda i, j, k: (k + b_koff, j))]
    args = [a, b]
    if add is not None:
        in_specs.append(pl.BlockSpec((tm, tn), lambda i, j, k: (i, j)))
        args.append(add)
    if after is not None:
        in_specs.append(pl.BlockSpec(memory_space=pl.ANY))
        args.append(after)
    return pl.pallas_call(
        body, name=name, grid=(M // tm, N // tn, nk), in_specs=in_specs,
        out_specs=pl.BlockSpec((tm, tn), lambda i, j, k: (i, j)),
        out_shape=jax.ShapeDtypeStruct((M, N), out_dtype),
        scratch_shapes=[pltpu.VMEM((tm, tn), F32)],
        compiler_params=_cp(("parallel", "parallel", "arbitrary")),
    )(*args)


def _norm_fwd(x, w, *, tm):
    S = x.shape[0]

    def body(x_ref, w_ref, o_ref):
        xv = x_ref[...]
        r = lax.rsqrt(jnp.mean(xv * xv, axis=-1, keepdims=True) + EPS)
        o_ref[...] = (xv * r * w_ref[...]).astype(BF16)

    return pl.pallas_call(
        body, name="norm_fwd", grid=(S // tm,),
        in_specs=[pl.BlockSpec((tm, D), lambda i: (i, 0)), pl.BlockSpec((1, D), lambda i: (0, 0))],
        out_specs=pl.BlockSpec((tm, D), lambda i: (i, 0)),
        out_shape=jax.ShapeDtypeStruct((S, D), BF16), compiler_params=_cp(("parallel",)),
    )(x, w)


def _norm_bwd(x, w, dxn, dh, *, tm):
    S = x.shape[0]

    def body(x_ref, w_ref, dxn_ref, dh_ref, gx_ref, dw_ref):
        xv = x_ref[...]
        r = lax.rsqrt(jnp.mean(xv * xv, axis=-1, keepdims=True) + EPS)
        xh = xv * r
        dxn_v = dxn_ref[...]
        dxh = dxn_v * w_ref[...]
        gx_ref[...] = dh_ref[...] + r * (dxh - xh * jnp.mean(dxh * xh, axis=-1, keepdims=True))

        @pl.when(pl.program_id(0) == 0)
        def _():
            dw_ref[...] = jnp.zeros_like(dw_ref)

        dw_ref[0:1, :] += jnp.sum(dxn_v * xh, axis=0, keepdims=True)

    row = pl.BlockSpec((tm, D), lambda i: (i, 0))
    return pl.pallas_call(
        body, name="norm_bwd", grid=(S // tm,),
        in_specs=[row, pl.BlockSpec((1, D), lambda i: (0, 0)), row, row],
        out_specs=[row, pl.BlockSpec((8, D), lambda i: (0, 0))],
        out_shape=[jax.ShapeDtypeStruct((S, D), F32), jax.ShapeDtypeStruct((8, D), F32)],
        compiler_params=_cp(("arbitrary",)),
    )(x, w, dxn, dh)


def _sgu_core(u_ref, v_ref, z_ref, g_ref, b_ref, wm_ref, bias_ref, vnb_ref, mixed_ref, tm):
    v = v_ref[...].astype(F32)
    mu = jnp.mean(v, axis=-1, keepdims=True)
    vc = v - mu
    rs = lax.rsqrt(jnp.mean(vc * vc, axis=-1, keepdims=True) + EPS)
    vh = vc * rs
    vnb_ref[...] = (vh * g_ref[...] + b_ref[...]).astype(BF16)
    for blk in range(tm // SGU_BLOCK):
        rows = pl.ds(blk * SGU_BLOCK, SGU_BLOCK)
        for gi in range(SGU_GROUPS):
            cols = pl.ds(gi * LANE, LANE)
            mixed_ref[rows, cols] = _dot(wm_ref[gi], vnb_ref[rows, cols]) + bias_ref[:, cols]
    return vh, rs


def _sgu_fwd(proj, g, b, wm, bias_full, *, tm):
    S = proj.shape[0]

    def body(u_ref, v_ref, z_ref, g_ref, b_ref, wm_ref, bias_ref, y_ref, vnb_ref, mixed_ref):
        _sgu_core(u_ref, v_ref, z_ref, g_ref, b_ref, wm_ref, bias_ref, vnb_ref, mixed_ref, tm)
        z = z_ref[...].astype(F32)
        y_ref[...] = (u_ref[...].astype(F32) * mixed_ref[...] * (z * _sigmoid(z))).astype(BF16)

    seg = lambda off: pl.BlockSpec((tm, D), lambda i: (i, off // D))
    full = lambda a: pl.BlockSpec(a.shape, lambda i: (0,) * a.ndim)
    return pl.pallas_call(
        body, name="sgu_fwd", grid=(S // tm,),
        in_specs=[seg(OFF_U), seg(OFF_V), seg(OFF_ZA), full(g), full(b), full(wm), full(bias_full)],
        out_specs=pl.BlockSpec((tm, D), lambda i: (i, 0)),
        out_shape=jax.ShapeDtypeStruct((S, D), BF16),
        scratch_shapes=[pltpu.VMEM((tm, D), BF16), pltpu.VMEM((tm, D), F32)],
        compiler_params=_cp(("parallel",)),
    )(proj, proj, proj, g, b, wm, bias_full)


def _sgu_bwd(proj, dy, g, b, wm, wmT, bias_full, mask, sel, *, tm):
    S = proj.shape[0]
    nsteps = S // tm

    def body(u_ref, v_ref, z_ref, dy_ref, g_ref, b_ref, wm_ref, wmT_ref, bias_ref, mask_ref, sel_ref,
             dp_ref, dws_ref, dbs_ref, dg_ref, db_ref, vnb_ref, mixed_ref, dmb_ref, dvn_ref, dbias_ref):
        i = pl.program_id(0)

        @pl.when(i == 0)
        def _():
            dws_ref[...] = jnp.zeros_like(dws_ref)
            dg_ref[...] = jnp.zeros_like(dg_ref)
            db_ref[...] = jnp.zeros_like(db_ref)
            dbias_ref[...] = jnp.zeros_like(dbias_ref)

        vh, rs = _sgu_core(u_ref, v_ref, z_ref, g_ref, b_ref, wm_ref, bias_ref, vnb_ref, mixed_ref, tm)
        u = u_ref[...].astype(F32)
        z = z_ref[...].astype(F32)
        dy_v = dy_ref[...].astype(F32)
        mixed = mixed_ref[...]
        sg = _sigmoid(z)
        sz = z * sg
        dp_ref[:, 0:D] = (dy_v * mixed * sz).astype(BF16)
        dp_ref[:, 2 * D:3 * D] = (dy_v * u * mixed * (sg * (1.0 + z * (1.0 - sg)))).astype(BF16)
        dmixed = dy_v * u * sz
        dmb_ref[...] = dmixed.astype(BF16)
        for blk in range(tm // SGU_BLOCK):
            dbias_ref[...] += dmixed[blk * SGU_BLOCK:(blk + 1) * SGU_BLOCK, :]
        for blk in range(tm // SGU_BLOCK):
            rows = pl.ds(blk * SGU_BLOCK, SGU_BLOCK)
            for gi in range(SGU_GROUPS):
                cols = pl.ds(gi * LANE, LANE)
                dm = dmb_ref[rows, cols]
                dvn_ref[rows, cols] = _dot(wmT_ref[gi], dm)
                dws_ref[gi] += _dot_nt(dm, vnb_ref[rows, cols])
        dvn = dvn_ref[...]
        dg_ref[0:1, :] += jnp.sum(dvn * vh, axis=0, keepdims=True)
        db_ref[0:1, :] += jnp.sum(dvn, axis=0, keepdims=True)
        dvh = dvn * g_ref[...]
        dv = rs * (dvh - jnp.mean(dvh, axis=-1, keepdims=True) - vh * jnp.mean(dvh * vh, axis=-1, keepdims=True))
        dp_ref[:, D:2 * D] = dv.astype(BF16)

        @pl.when(i == nsteps - 1)
        def _():
            for gi in range(SGU_GROUPS):
                dws_ref[gi] = dws_ref[gi] * mask_ref[...]
            dbs_ref[...] = _dot(dbias_ref[...], sel_ref[...], precision=HI)

    seg = lambda off: pl.BlockSpec((tm, D), lambda i: (i, off // D))
    full = lambda a: pl.BlockSpec(a.shape, lambda i: (0,) * a.ndim)
    return pl.pallas_call(
        body, name="sgu_bwd", grid=(nsteps,),
        in_specs=[seg(OFF_U), seg(OFF_V), seg(OFF_ZA), pl.BlockSpec((tm, D), lambda i: (i, 0)),
                  full(g), full(b), full(wm), full(wmT), full(bias_full), full(mask), full(sel)],
        out_specs=[pl.BlockSpec((tm, 3 * D), lambda i: (i, 0)),
                   pl.BlockSpec((SGU_GROUPS, SGU_BLOCK, SGU_BLOCK), lambda i: (0, 0, 0)),
                   pl.BlockSpec((SGU_BLOCK, LANE), lambda i: (0, 0)),
                   pl.BlockSpec((8, D), lambda i: (0, 0)), pl.BlockSpec((8, D), lambda i: (0, 0))],
        out_shape=[jax.ShapeDtypeStruct((S, 3 * D), BF16),
                   jax.ShapeDtypeStruct((SGU_GROUPS, SGU_BLOCK, SGU_BLOCK), F32),
                   jax.ShapeDtypeStruct((SGU_BLOCK, LANE), F32),
                   jax.ShapeDtypeStruct((8, D), F32), jax.ShapeDtypeStruct((8, D), F32)],
        scratch_shapes=[pltpu.VMEM((tm, D), BF16), pltpu.VMEM((tm, D), F32), pltpu.VMEM((tm, D), BF16),
                        pltpu.VMEM((tm, D), F32), pltpu.VMEM((SGU_BLOCK, D), F32)],
        compiler_params=_cp(("arbitrary",)),
    )(proj, proj, proj, dy, g, b, wm, wmT, bias_full, mask, sel)


SSD_T = 2 * CHUNK
HALO = 8
HALO_BLK = 16


def _pair_masks():
    row = lax.broadcasted_iota(jnp.int32, (CHUNK, LANE), 0)
    lane = lax.broadcasted_iota(jnp.int32, (CHUNK, LANE), 1)
    pos = jnp.where(lane >= CHUNK, lane - CHUNK, lane)
    diag = (row == pos).astype(F32)
    causal = row >= pos
    lo = (lane < CHUNK).astype(F32)
    return diag, causal, lo, 1.0 - lo


def _ssd_chunk_fwd(c, ext_ref, shift_ref, dt_ref, cw_ref, cb_ref, dtb_ref, alog_ref, tri_ref, exp_ref):
    r0 = c * CHUNK
    win = ext_ref[pl.ds(r0, HALO_BLK + CHUNK), :]
    sh = _dot(shift_ref[...], win)
    taps = [sh[k * CHUNK:(k + 1) * CHUNK] for k in range(CONV_K - 1)] + [win[HALO_BLK:].astype(F32)]
    pre = cb_ref[...] + sum(cw_ref[k:k + 1, :] * taps[k] for k in range(CONV_K))
    sg = _sigmoid(pre)
    xc = pre * sg
    dtr = dt_ref[pl.ds(r0, CHUNK), :].astype(F32) + dtb_ref[...]
    dtv = _softplus(dtr)
    A = -jnp.exp(alog_ref[...])
    acs = _sel_left(tri_ref[...], dtv * A)
    both = _sel_right(jnp.concatenate([acs, dtv], axis=0), exp_ref[...])
    E, dtE = both[0:CHUNK], both[CHUNK:2 * CHUNK]
    return dict(taps=taps, pre=pre, sg=sg, xc=xc, dtr=dtr, dtv=dtv, A=A, E=E, dtE=dtE)


def _ssd_fwd(proj, conv_w, conv_b, dtb_p, alog_p, d_exp, norm_w, tri, expand, shift):
    S = proj.shape[0]
    T = SSD_T
    nsteps = S // T
    ncl = T // CHUNK

    def body(zb_ref, xbc_ref, halo_ref, dt_ref, cw_ref, cb_ref, dtb_ref, alog_ref, dexp_ref, nw_ref, tri_ref, exp_ref, shift_ref,
             y_ref, yb_ref, st_ref, ht_ref, ext_ref):
        i = pl.program_id(0)

        @pl.when(i == 0)
        def _():
            ht_ref[...] = jnp.zeros_like(ht_ref)
            ext_ref[0:HALO_BLK, :] = jnp.zeros((HALO_BLK, XBC_W), BF16)

        @pl.when(i > 0)
        def _():
            ext_ref[0:HALO_BLK, :] = halo_ref[...]

        ext_ref[HALO_BLK:HALO_BLK + T, :] = xbc_ref[...]
        diag, causal, lo, hi = _pair_masks()
        for c in range(ncl):
            q = _ssd_chunk_fwd(c, ext_ref, shift_ref, dt_ref, cw_ref, cb_ref, dtb_ref, alog_ref, tri_ref, exp_ref)
            rows = pl.ds(c * CHUNK, CHUNK)
            xc, E, dtE = q["xc"], q["E"], q["dtE"]
            xs = xc[:, 0:D]
            total = E[CHUNK - 1:CHUNK, :]
            x_dt = xs * dtE
            eE = jnp.exp(E)
            xw = x_dt * jnp.exp(total - E)
            st_ref[c] = ht_ref[...]
            for g in range(SSD_GROUPS):
                gc = slice(g * GROUP_W, (g + 1) * GROUP_W)
                Bg = xc[:, D + g * STATE:D + (g + 1) * STATE].astype(BF16)
                Cg = xc[:, D + SSD_GROUPS * STATE + g * STATE:D + SSD_GROUPS * STATE + (g + 1) * STATE].astype(BF16)
                cb2 = _dot_nt(Cg, jnp.concatenate([Bg, Bg], axis=0))
                htg = ht_ref[:, gc]
                y_ref[rows, gc] = eE[:, gc] * _dot(Cg, htg.astype(BF16)) + xs[:, gc] * dexp_ref[:, gc]
                for jj in range(GROUP_W // LANE):
                    pc = slice(g * GROUP_W + jj * LANE, g * GROUP_W + (jj + 1) * LANE)
                    Ej = E[:, pc]
                    e2 = jnp.sum(Ej * diag, axis=0, keepdims=True)
                    Mp = cb2 * jnp.exp(jnp.where(causal, Ej - e2, -1e30))
                    xj = x_dt[:, pc]
                    xbd = jnp.concatenate([xj * lo, xj * hi], axis=0).astype(BF16)
                    y_ref[rows, pc] += _dot(Mp.astype(BF16), xbd)
                ht_ref[:, gc] = jnp.exp(total[:, gc]) * htg + _dot_tn(Bg, xw[:, gc].astype(BF16))
            zb = zb_ref[rows, :].astype(F32)
            hh = y_ref[rows, :] * (zb * _sigmoid(zb))
            for g in range(SSD_GROUPS):
                gc = slice(g * GROUP_W, (g + 1) * GROUP_W)
                hg = hh[:, gc]
                r = lax.rsqrt(jnp.mean(hg * hg, axis=-1, keepdims=True) + EPS)
                yb_ref[rows, gc] = (hg * r * nw_ref[:, gc]).astype(BF16)

    full = lambda a: pl.BlockSpec(a.shape, lambda i: (0,) * a.ndim)
    hb = T // HALO_BLK
    return pl.pallas_call(
        body, name="ssd_fwd", grid=(nsteps,),
        in_specs=[pl.BlockSpec((T, D), lambda i: (i, OFF_ZB // D)),
                  pl.BlockSpec((T, XBC_W), lambda i: (i, OFF_XBC // XBC_W)),
                  pl.BlockSpec((HALO_BLK, XBC_W), lambda i: (jnp.maximum(i * hb - 1, 0), OFF_XBC // XBC_W)),
                  pl.BlockSpec((T, DT_W), lambda i: (i, OFF_DT // DT_W)),
                  full(conv_w), full(conv_b), full(dtb_p), full(alog_p), full(d_exp), full(norm_w), full(tri), full(expand),
                  full(shift)],
        out_specs=[pl.BlockSpec((T, D), lambda i: (i, 0)), pl.BlockSpec((T, D), lambda i: (i, 0)),
                   pl.BlockSpec((ncl, STATE, D), lambda i: (i, 0, 0))],
        out_shape=[jax.ShapeDtypeStruct((S, D), F32), jax.ShapeDtypeStruct((S, D), BF16),
                   jax.ShapeDtypeStruct((S // CHUNK, STATE, D), F32)],
        scratch_shapes=[pltpu.VMEM((STATE, D), F32), pltpu.VMEM((HALO_BLK + T, XBC_W), BF16)],
        compiler_params=_cp(("arbitrary",)),
    )(proj, proj, proj, proj, conv_w, conv_b, dtb_p, alog_p, d_exp, norm_w, tri, expand, shift)


def _ssd_bwd(proj, dyb, y, states, conv_w, conv_b, dtb_p, alog_p, d_exp, norm_w, tri, triT, expand, expandT, shift):
    S = proj.shape[0]
    T = SSD_T
    nsteps = S // T
    ncl = T // CHUNK
    SSD_W = SSD_PAD_W

    def body(zb_ref, xbc_ref, halo_ref, dt_ref, dyb_ref, y_ref, st_ref, cw_ref, cb_ref, dtb_ref, alog_ref, dexp_ref, nw_ref,
             tri_ref, triT_ref, exp_ref, expT_ref, shift_ref,
             dp_ref, dcw_ref, dcb_ref, ddtb_ref, dalog_ref, dD_ref, dnw_ref,
             dht_ref, ext_ref, dpre_ref, dy_s, dE_s, dxdt_s, dxc_s, dDacc_ref, dAacc_ref):
        i = pl.program_id(0)

        @pl.when(i == 0)
        def _():
            for r in (dht_ref, dcw_ref, dcb_ref, ddtb_ref, dnw_ref, dDacc_ref, dAacc_ref):
                r[...] = jnp.zeros_like(r)
            dpre_ref[T:T + HALO, :] = jnp.zeros((HALO, XBC_W), F32)

        @pl.when(i == nsteps - 1)
        def _():
            ext_ref[0:HALO_BLK, :] = jnp.zeros((HALO_BLK, XBC_W), BF16)

        @pl.when(i < nsteps - 1)
        def _():
            ext_ref[0:HALO_BLK, :] = halo_ref[...]

        ext_ref[HALO_BLK:HALO_BLK + T, :] = xbc_ref[...]
        diag, causal, lo, hi = _pair_masks()
        last_row = (lax.broadcasted_iota(jnp.int32, (CHUNK, 1), 0) == CHUNK - 1).astype(F32)
        for c in reversed(range(ncl)):
            q = _ssd_chunk_fwd(c, ext_ref, shift_ref, dt_ref, cw_ref, cb_ref, dtb_ref, alog_ref, tri_ref, exp_ref)
            rows = pl.ds(c * CHUNK, CHUNK)
            pre, sg, xc, dtr, dtv, A, E, dtE = (q[k] for k in ("pre", "sg", "xc", "dtr", "dtv", "A", "E", "dtE"))
            xs = xc[:, 0:D]
            total = E[CHUNK - 1:CHUNK, :]
            x_dt = xs * dtE
            eE = jnp.exp(E)
            wdec = jnp.exp(total - E)
            zb = zb_ref[rows, :].astype(F32)
            yv = y_ref[rows, :]
            sgz = _sigmoid(zb)
            sz = zb * sgz
            hh = yv * sz
            for g in range(SSD_GROUPS):
                gc = slice(g * GROUP_W, (g + 1) * GROUP_W)
                hg = hh[:, gc]
                r = lax.rsqrt(jnp.mean(hg * hg, axis=-1, keepdims=True) + EPS)
                dyb_g = dyb_ref[rows, gc].astype(F32)
                dn = dyb_g * nw_ref[:, gc]
                dnw_ref[0:1, gc] += jnp.sum(dyb_g * hg * r, axis=0, keepdims=True)
                dy_s[:, gc] = r * dn - hg * (r * r * r) * jnp.mean(dn * hg, axis=-1, keepdims=True)
            dhh = dy_s[...]
            dp_ref[rows, 0:D] = (dhh * yv * (sgz * (1.0 + zb * (1.0 - sgz)))).astype(BF16)
            dy = dhh * sz
            dy_s[...] = dy
            dDacc_ref[0:1, :] += jnp.sum(dy * xs, axis=0, keepdims=True)
            dxc_s[:, 0:D] = dy * dexp_ref[...]
            for g in range(SSD_GROUPS):
                gc = slice(g * GROUP_W, (g + 1) * GROUP_W)
                bcol = slice(D + g * STATE, D + (g + 1) * STATE)
                ccol = slice(D + SSD_GROUPS * STATE + g * STATE, D + SSD_GROUPS * STATE + (g + 1) * STATE)
                Bg = xc[:, bcol].astype(BF16)
                Cg = xc[:, ccol].astype(BF16)
                B2 = jnp.concatenate([Bg, Bg], axis=0)
                cb2 = _dot_nt(Cg, B2)
                htg = st_ref[c, :, gc]
                htb = htg.astype(BF16)
                dhn = dht_ref[:, gc]
                dhnb = dhn.astype(BF16)
                dyg = dy[:, gc]
                eEg = eE[:, gc]
                wg = wdec[:, gc]
                xdg = x_dt[:, gc]
                CH = _dot(Cg, htb)
                dCHb = (dyg * eEg).astype(BF16)
                dC = _dot_nt(dCHb, htb)
                dl = jnp.exp(total[:, gc])
                dht_prev = _dot_tn(Cg, dCHb) + dl * dhn
                dtot = jnp.sum(dhn * htg, axis=0, keepdims=True) * dl
                dxw = _dot(Bg, dhnb)
                dB = _dot_nt((xdg * wg).astype(BF16), dhnb)
                dwd = dxw * xdg * wg
                dtot = dtot + jnp.sum(dwd, axis=0, keepdims=True)
                dE_s[:, gc] = dyg * eEg * CH - dwd + last_row * dtot
                dxdt_s[:, gc] = dxw * wg
                dcb2 = jnp.zeros((CHUNK, LANE), F32)
                for jj in range(GROUP_W // LANE):
                    pc = slice(g * GROUP_W + jj * LANE, g * GROUP_W + (jj + 1) * LANE)
                    Ej = E[:, pc]
                    e2 = jnp.sum(Ej * diag, axis=0, keepdims=True)
                    Lp = jnp.exp(jnp.where(causal, Ej - e2, -1e30))
                    Mp = cb2 * Lp
                    xj = x_dt[:, pc]
                    xbd = jnp.concatenate([xj * lo, xj * hi], axis=0).astype(BF16)
                    dyj = dy[:, pc].astype(BF16)
                    dMp = _dot_nt(dyj, xbd)
                    dxbd = _dot_tn(Mp.astype(BF16), dyj)
                    dxdt_s[:, pc] += dxbd[0:CHUNK, :] * lo + dxbd[CHUNK:2 * CHUNK, :] * hi
                    dcb2 = dcb2 + dMp * Lp
                    dseg = dMp * Mp
                    dE_s[:, pc] += dseg - diag * jnp.sum(dseg, axis=0, keepdims=True)
                dcb2b = dcb2.astype(BF16)
                dC = dC + _dot(dcb2b, B2)
                dB2 = _dot_tn(dcb2b, Cg)
                dB = dB + dB2[0:CHUNK, :] + dB2[CHUNK:2 * CHUNK, :]
                dxc_s[:, bcol] = dB
                dxc_s[:, ccol] = dC
                dht_ref[:, gc] = dht_prev
            dx_dt = dxdt_s[...]
            dxc_s[:, 0:D] += dx_dt * dtE
            red = _sel_right(jnp.concatenate([dE_s[...], dx_dt * xs], axis=0), expT_ref[...])
            da = _sel_left(triT_ref[...], red[0:CHUNK, :])
            ddtv = red[CHUNK:2 * CHUNK, :] + da * A
            dAacc_ref[0:1, :] += jnp.sum(da * dtv, axis=0, keepdims=True)
            ddtr = ddtv * _sigmoid(dtr)
            ddtb_ref[0:1, :] += jnp.sum(ddtr, axis=0, keepdims=True)
            dp_ref[rows, D + XBC_W:D + XBC_W + DT_W] = ddtr.astype(BF16)
            dpre = dxc_s[...] * (sg * (1.0 + pre * (1.0 - sg)))
            dpre_ref[rows, :] = dpre
            dcb_ref[0:1, :] += jnp.sum(dpre, axis=0, keepdims=True)
            for k in range(CONV_K):
                dcw_ref[k:k + 1, :] += jnp.sum(dpre * q["taps"][k], axis=0, keepdims=True)
        dxbc = jnp.zeros((T, XBC_W), F32)
        for k in range(CONV_K):
            dxbc = dxbc + cw_ref[k:k + 1, :] * dpre_ref[pl.ds(CONV_K - 1 - k, T), :]
        dp_ref[:, D:D + XBC_W] = dxbc.astype(BF16)
        dp_ref[:, SEG_SSD[1]:SSD_W] = jnp.zeros((T, SSD_W - SEG_SSD[1]), BF16)
        dpre_ref[T:T + HALO, :] = dpre_ref[0:HALO, :]

        @pl.when(i == nsteps - 1)
        def _():
            dalog_ref[...] = dAacc_ref[...] * (-jnp.exp(alog_ref[...]))
            dD_ref[...] = _dot(dDacc_ref[...], expT_ref[...].astype(F32), precision=HI)

    full = lambda a: pl.BlockSpec(a.shape, lambda i: (0,) * a.ndim)
    hb = T // HALO_BLK
    rev = lambda i: nsteps - 1 - i
    acc = lambda w: pl.BlockSpec((8, w), lambda i: (0, 0))
    return pl.pallas_call(
        body, name="ssd_bwd", grid=(nsteps,),
        in_specs=[pl.BlockSpec((T, D), lambda i: (rev(i), OFF_ZB // D)),
                  pl.BlockSpec((T, XBC_W), lambda i: (rev(i), OFF_XBC // XBC_W)),
                  pl.BlockSpec((HALO_BLK, XBC_W), lambda i: (jnp.maximum(rev(i) * hb - 1, 0), OFF_XBC // XBC_W)),
                  pl.BlockSpec((T, DT_W), lambda i: (rev(i), OFF_DT // DT_W)),
                  pl.BlockSpec((T, D), lambda i: (rev(i), 0)), pl.BlockSpec((T, D), lambda i: (rev(i), 0)),
                  pl.BlockSpec((ncl, STATE, D), lambda i: (rev(i), 0, 0)),
                  full(conv_w), full(conv_b), full(dtb_p), full(alog_p), full(d_exp), full(norm_w),
                  full(tri), full(triT), full(expand), full(expandT), full(shift)],
        out_specs=[pl.BlockSpec((T, SSD_W), lambda i: (rev(i), 0)),
                   acc(XBC_W), acc(XBC_W), acc(DT_W), acc(DT_W), acc(DT_W), acc(D)],
        out_shape=[jax.ShapeDtypeStruct((S, SSD_W), BF16),
                   jax.ShapeDtypeStruct((8, XBC_W), F32), jax.ShapeDtypeStruct((8, XBC_W), F32),
                   jax.ShapeDtypeStruct((8, DT_W), F32), jax.ShapeDtypeStruct((8, DT_W), F32),
                   jax.ShapeDtypeStruct((8, DT_W), F32), jax.ShapeDtypeStruct((8, D), F32)],
        scratch_shapes=[pltpu.VMEM((STATE, D), F32), pltpu.VMEM((HALO_BLK + T, XBC_W), BF16), pltpu.VMEM((T + HALO, XBC_W), F32),
                        pltpu.VMEM((CHUNK, D), F32), pltpu.VMEM((CHUNK, D), F32), pltpu.VMEM((CHUNK, D), F32),
                        pltpu.VMEM((CHUNK, XBC_W), F32), pltpu.VMEM((8, D), F32), pltpu.VMEM((8, DT_W), F32)],
        compiler_params=_cp(("arbitrary",)),
    )(proj, proj, proj, proj, dyb, y, states, conv_w, conv_b, dtb_p, alog_p, d_exp, norm_w, tri, triT, expand, expandT, shift)


def _head(x, ya, yb, proj, target, gate_b, wout, fw, *, tm):
    S = x.shape[0]

    def body(x_ref, ya_ref, yb_ref, gl0_ref, gl1_ref, t_ref, gb_ref, w_ref, fw_ref,
             dh_ref, dhb_ref, mb_ref, dya_ref, dyb_ref, dgl_ref, loss_ref, dfw_ref, dgb_ref):
        @pl.when(pl.program_id(0) == 0)
        def _():
            loss_ref[...] = jnp.zeros_like(loss_ref)
            dfw_ref[...] = jnp.zeros_like(dfw_ref)
            dgb_ref[...] = jnp.zeros_like(dgb_ref)

        ya_v = ya_ref[...].astype(F32)
        yb_v = yb_ref[...].astype(F32)
        g0 = _sigmoid(gl0_ref[...].astype(F32) + gb_ref[:, 0:D])
        g1 = _sigmoid(gl1_ref[...].astype(F32) + gb_ref[:, D:2 * D])
        mb = (g0 * ya_v + g1 * yb_v).astype(BF16)
        mb_ref[...] = mb
        h = x_ref[...] + _dot(mb, w_ref[...])
        r = lax.rsqrt(jnp.mean(h * h, axis=-1, keepdims=True) + EPS)
        hn = h * r
        err = hn * fw_ref[...] - t_ref[...]
        loss_ref[...] += 0.5 * jnp.sum(jnp.mean(err * err, axis=-1, keepdims=True))
        dyf = err * (1.0 / D)
        dfw_ref[0:1, :] += jnp.sum(dyf * hn, axis=0, keepdims=True)
        dhn = dyf * fw_ref[...]
        dh = r * (dhn - hn * jnp.mean(dhn * hn, axis=-1, keepdims=True))
        dh_ref[...] = dh
        dhb = dh.astype(BF16)
        dhb_ref[...] = dhb
        dm = _dot_nt(dhb, w_ref[...])
        dya_ref[...] = (dm * g0).astype(BF16)
        dyb_ref[...] = (dm * g1).astype(BF16)
        dgl0 = dm * ya_v * g0 * (1.0 - g0)
        dgl1 = dm * yb_v * g1 * (1.0 - g1)
        dgl_ref[:, 0:D] = dgl0.astype(BF16)
        dgl_ref[:, D:2 * D] = dgl1.astype(BF16)
        dgb_ref[0:1, 0:D] += jnp.sum(dgl0, axis=0, keepdims=True)
        dgb_ref[0:1, D:2 * D] += jnp.sum(dgl1, axis=0, keepdims=True)

    row = pl.BlockSpec((tm, D), lambda i: (i, 0))
    seg = lambda off: pl.BlockSpec((tm, D), lambda i: (i, off // D))
    full = lambda a: pl.BlockSpec(a.shape, lambda i: (0,) * a.ndim)
    acc = lambda w: pl.BlockSpec((8, w), lambda i: (0, 0))
    return pl.pallas_call(
        body, name="head", grid=(S // tm,),
        in_specs=[row, row, row, seg(OFF_G0), seg(OFF_G1), row, full(gate_b), full(wout), full(fw)],
        out_specs=[row, row, row, row, row, pl.BlockSpec((tm, 2 * D), lambda i: (i, 0)), acc(LANE), acc(D), acc(2 * D)],
        out_shape=[jax.ShapeDtypeStruct((S, D), F32), jax.ShapeDtypeStruct((S, D), BF16), jax.ShapeDtypeStruct((S, D), BF16),
                   jax.ShapeDtypeStruct((S, D), BF16), jax.ShapeDtypeStruct((S, D), BF16), jax.ShapeDtypeStruct((S, 2 * D), BF16),
                   jax.ShapeDtypeStruct((8, LANE), F32), jax.ShapeDtypeStruct((8, D), F32), jax.ShapeDtypeStruct((8, 2 * D), F32)],
        compiler_params=_cp(("arbitrary",)),
    )(x, ya, yb, proj, proj, target, gate_b, wout, fw)


def _adam_update(g, w_ref, m_ref, v_ref, g_ref, d_ref, m2_ref, v2_ref):
    m2 = ADAM_B1 * m_ref[...] + (1.0 - ADAM_B1) * g
    v2 = ADAM_B2 * v_ref[...] + (1.0 - ADAM_B2) * (g * g)
    m_hat = m2 / (1.0 - ADAM_B1 ** ADAM_STEP)
    v_hat = v2 / (1.0 - ADAM_B2 ** ADAM_STEP)
    g_ref[...] = g
    d_ref[...] = -ADAM_LR * (m_hat / (jnp.sqrt(v_hat) + ADAM_EPS) + ADAM_WD * w_ref[...])
    m2_ref[...] = m2
    v2_ref[...] = v2


def _adamw_own(me, own, landed, w, m, v, *, tr, tc, name):
    _, R, C = landed.shape
    assert R % tr == 0 and C % tc == 0, (name, R, C, tr, tc)

    def body(me_ref, own_ref, p_ref, w_ref, m_ref, v_ref, g_ref, d_ref, m2_ref, v2_ref):
        mine = own_ref[0].astype(F32)
        g = jnp.where(me_ref[0] == 0, mine, p_ref[0].astype(F32))
        for k in range(1, N_DEV):
            g = g + jnp.where(me_ref[0] == k, mine, p_ref[k].astype(F32))
        _adam_update(g, w_ref, m_ref, v_ref, g_ref, d_ref, m2_ref, v2_ref)

    tile = pl.BlockSpec((tr, tc), lambda i, j, me_ref: (i, j))
    return pl.pallas_call(
        body, name=name,
        grid_spec=pltpu.PrefetchScalarGridSpec(
            num_scalar_prefetch=1, grid=(R // tr, C // tc),
            in_specs=[pl.BlockSpec((1, tr, tc), lambda i, j, me_ref: (me_ref[0], i, j)),
                      pl.BlockSpec((N_DEV, tr, tc), lambda i, j, me_ref: (0, i, j)), tile, tile, tile],
            out_specs=[tile, tile, tile, tile]),
        out_shape=[jax.ShapeDtypeStruct((R, C), F32)] * 4,
        compiler_params=_cp(("parallel", "parallel")),
    )(me, own, landed, w, m, v)


def _adamw(parts, w, m, v, *, tr, name):
    _, R, C = parts.shape
    assert R % tr == 0, (name, R, tr)

    def body(p_ref, w_ref, m_ref, v_ref, g_ref, d_ref, m2_ref, v2_ref):
        g = p_ref[0].astype(F32)
        for k in range(1, N_DEV):
            g = g + p_ref[k].astype(F32)
        _adam_update(g, w_ref, m_ref, v_ref, g_ref, d_ref, m2_ref, v2_ref)

    row = pl.BlockSpec((tr, C), lambda i: (i, 0))
    return pl.pallas_call(
        body, name=name, grid=(R // tr,),
        in_specs=[pl.BlockSpec((N_DEV, tr, C), lambda i: (0, i, 0)), row, row, row],
        out_specs=[row, row, row, row],
        out_shape=[jax.ShapeDtypeStruct((R, C), F32)] * 4,
        compiler_params=_cp(("parallel",)),
    )(parts, w, m, v)


def _place():
    x, y, c = lax.axis_index("x"), lax.axis_index("y"), lax.axis_index("c")
    return x, y, c


def _all_gather(arrs, *, name):
    n = len(arrs)

    def body(*refs):
        ins, outs = refs[:n], refs[n:2 * n]
        send_sems, recv_sems, local_sems = refs[2 * n:]
        x, y, c = _place()
        me, sibling = (x, y, c), (x, y, 1 - c)
        chips = [(1 - x, y), (x, 1 - y), (1 - x, 1 - y)]

        def idx(px, py, pc):
            return 4 * px + 2 * py + pc

        def copy(k, a, block, to, src=None):
            slab = outs[a].at[idx(*block)]
            return pltpu.make_async_remote_copy(
                src_ref=slab if src is None else src, dst_ref=slab,
                send_sem=send_sems.at[k, a], recv_sem=recv_sems.at[k, a], device_id=to, device_id_type=MESH)

        mine = [pltpu.make_async_copy(ins[a], outs[a].at[idx(*me)], local_sems.at[a]) for a in range(n)]
        for cp in mine:
            cp.start()
        first = []
        for a in range(n):
            first.append(copy(0, a, me, sibling, src=ins[a]))
            first += [copy(1 + j, a, me, (*chip, c), src=ins[a]) for j, chip in enumerate(chips)]
        for cp in first:
            cp.start()
        passed = []
        for j, chip in enumerate(chips):
            for a in range(n):
                copy(1 + j, a, (*chip, c), me).wait_recv()
                fwd = copy(4 + j, a, (*chip, c), sibling)
                fwd.start()
                passed.append(fwd)
        for a in range(n):
            copy(0, a, sibling, me).wait_recv()
            for j, chip in enumerate(chips):
                copy(4 + j, a, (*chip, 1 - c), me).wait_recv()
        for cp in first + passed:
            cp.wait_send()
        for cp in mine:
            cp.wait()

    anyspec = pl.BlockSpec(memory_space=pl.ANY)
    return pl.pallas_call(
        body, name=name,
        in_specs=[anyspec] * n, out_specs=[anyspec] * n,
        out_shape=[jax.ShapeDtypeStruct((N_DEV,) + a.shape, a.dtype) for a in arrs],
        scratch_shapes=[pltpu.SemaphoreType.DMA((7, n)), pltpu.SemaphoreType.DMA((7, n)), pltpu.SemaphoreType.DMA((n,))],
    )(*arrs)


W_ROWS = SEG_SSD[0] + SSD_PAD_W
ROW_TILE = (16, LANE)


def _shard_pieces(k):
    out = []
    for lo, hi, dst in ((0, 6144, 0), (6144, 11296, SEG_SSD[0]), (11296, W_IN, SEG_GATE[0])):
        a, b = max(lo, k * SHARD_IN), min(hi, (k + 1) * SHARD_IN)
        if a < b:
            out.append((a - k * SHARD_IN, b - a, dst + a - lo))
    return out


def _gather_weights(win3, wout, cw, zeros3):
    n_zero = zeros3.shape[0]
    assert W_IN + n_zero == W_ROWS

    def run(k, win_ref, wout_ref, cw_ref, z_ref, w_out_ref, gout_ref, gcw_ref, send_sems, recv_sems, local_sems):
        x, y, c = k // 4, (k // 2) % 2, k % 2
        idx = lambda p: 4 * p[0] + 2 * p[1] + p[2]
        me, sib = (x, y, c), (x, y, 1 - c)
        xn, yn, dg = (1 - x, y, c), (x, 1 - y, c), (1 - x, 1 - y, c)

        def copies(slot, block, to, own=False):
            kb = idx(block)
            out = []
            for j, (s0, n, d0) in enumerate(_shard_pieces(kb)):
                dst = w_out_ref.at[pl.ds(d0, n)]
                out.append((win_ref.at[pl.ds(s0, n)] if own else dst, dst, j))
            out.append((wout_ref if own else gout_ref.at[kb], gout_ref.at[kb], 2))
            out.append((cw_ref if own else gcw_ref.at[kb], gcw_ref.at[kb], 3))
            return [pltpu.make_async_remote_copy(src_ref=s, dst_ref=d, send_sem=send_sems.at[slot, j], recv_sem=recv_sems.at[slot, j],
                                                 device_id=to, device_id_type=MESH) for s, d, j in out]

        def start(cps):
            for cp in cps:
                cp.start()
            return cps

        def arrived(slot, block):
            for cp in copies(slot, block, me):
                cp.wait_recv()

        local = [pltpu.make_async_copy(s, d, local_sems.at[j]) for j, (s, d) in enumerate(
            [(win_ref.at[pl.ds(s0, n)], w_out_ref.at[pl.ds(d0, n)]) for s0, n, d0 in _shard_pieces(k)]
            + [(wout_ref, gout_ref.at[k]), (cw_ref, gcw_ref.at[k]), (z_ref, w_out_ref.at[pl.ds(W_IN, n_zero)])])]
        for cp in local:
            cp.start()
        sent = start(copies(0, me, sib, own=True)) + start(copies(1, me, xn, own=True)) + start(copies(2, me, yn, own=True))
        arrived(1, xn)
        sent += start(copies(4, xn, sib))
        if c == 1:
            sent += start(copies(3, xn, yn))
        arrived(2, yn)
        sent += start(copies(5, yn, sib))
        if c == 0:
            sent += start(copies(3, yn, xn))
        arrived(3, dg)
        sent += start(copies(6, dg, sib))
        arrived(0, sib)
        arrived(4, (1 - x, y, 1 - c))
        arrived(5, (x, 1 - y, 1 - c))
        arrived(6, (1 - x, 1 - y, 1 - c))
        for cp in sent:
            cp.wait_send()
        for cp in local:
            cp.wait()

    def body(*refs):
        x, y, c = _place()
        me = 4 * x + 2 * y + c
        for k in range(N_DEV):
            pl.when(me == k)(functools.partial(run, k, *refs))

    anyspec = pl.BlockSpec(memory_space=pl.ANY)
    return pl.pallas_call(
        body, name="gather_weights", in_specs=[anyspec] * 4, out_specs=[anyspec] * 3,
        out_shape=[jax.ShapeDtypeStruct((W_ROWS,) + ROW_TILE, win3.dtype),
                   jax.ShapeDtypeStruct((N_DEV,) + wout.shape, wout.dtype), jax.ShapeDtypeStruct((N_DEV,) + cw.shape, cw.dtype)],
        scratch_shapes=[pltpu.SemaphoreType.DMA((7, 4)), pltpu.SemaphoreType.DMA((7, 4)), pltpu.SemaphoreType.DMA((5,))],
    )(win3, wout, cw, zeros3)


_REL = [(dx, dy, dc) for dx in (0, 1) for dy in (0, 1) for dc in (0, 1)][1:]
_HBM = pl.BlockSpec(memory_space=pltpu.HBM)
_SEM = pl.BlockSpec(memory_space=pltpu.SEMAPHORE)
_EFFECT = pltpu.SideEffectType.DATAFLOW_SIDE_EFFECTING


def _peer(k):
    x, y, c = _place()
    dx, dy, dc = _REL[k]
    return (1 - x if dx else x, 1 - y if dy else y, 1 - c if dc else c)


def _exchange_start(parts, *, name):
    n = len(parts)

    def body(*refs):
        ins, lands = refs[:n], refs[n:2 * n]
        send_sems, recv_sems, token = refs[2 * n], refs[2 * n + 1], refs[-1]
        x, y, c = _place()
        me = 4 * x + 2 * y + c
        for a in range(n):
            for k in range(len(_REL)):
                px, py, pc = _peer(k)
                pltpu.make_async_remote_copy(
                    src_ref=ins[a].at[4 * px + 2 * py + pc], dst_ref=lands[a].at[me],
                    send_sem=send_sems.at[len(_REL) * a + k], recv_sem=recv_sems.at[len(_REL) * a + k],
                    device_id=(px, py, pc), device_id_type=MESH).start()
        token[...] = jnp.zeros_like(token)

    sem = pltpu.SemaphoreType.DMA((len(_REL) * n,))
    bufs = [pltpu.HBM(p.shape, p.dtype) for p in parts]
    outs = pl.pallas_call(
        body, name=name,
        out_shape=(sem, sem, *bufs, *bufs, jax.ShapeDtypeStruct((8, LANE), F32)),
        in_specs=(_HBM,) * (2 * n), out_specs=(_SEM, _SEM, *(_HBM,) * (2 * n), pl.BlockSpec(memory_space=pltpu.VMEM)),
        input_output_aliases={i: 2 + i for i in range(2 * n)},
        compiler_params=pltpu.CompilerParams(has_side_effects=_EFFECT),
    )(*[pltpu.with_memory_space_constraint(p, pltpu.HBM) for p in parts],
      *[pltpu.with_memory_space_constraint(lax.empty(p.shape, p.dtype), pltpu.HBM) for p in parts])
    return outs[0], outs[1], outs[2:2 + n], outs[2 + n:2 + 2 * n], outs[-1]


def _exchange_wait(send_sems, recv_sems, parts, lands, after, *, name):
    n = len(parts)

    def body(*refs):
        ins, lands_ = refs[:n], refs[n:2 * n]
        ssem, rsem = refs[2 * n], refs[2 * n + 1]
        for a in range(n):
            for k in range(len(_REL)):
                px, py, pc = _peer(k)
                p = 4 * px + 2 * py + pc
                cp = pltpu.make_async_remote_copy(
                    src_ref=ins[a].at[p], dst_ref=lands_[a].at[p],
                    send_sem=ssem.at[len(_REL) * a + k], recv_sem=rsem.at[len(_REL) * a + k],
                    device_id=(px, py, pc), device_id_type=MESH)
                cp.wait_send()
                cp.wait_recv()

    bufs = [pltpu.HBM(p.shape, p.dtype) for p in parts]
    outs = pl.pallas_call(
        body, name=name, out_shape=(*bufs, *bufs),
        in_specs=(*(_HBM,) * (2 * n), _SEM, _SEM, pl.BlockSpec(memory_space=pl.ANY)), out_specs=(_HBM,) * (2 * n),
        input_output_aliases={i: i for i in range(2 * n)},
        compiler_params=pltpu.CompilerParams(has_side_effects=_EFFECT),
    )(*parts, *lands, send_sems, recv_sems, after)
    return outs[:n], outs[n:]


WEIGHTS = ('norm_w', 'w_in', 'gate_b', 'sgu_norm_g', 'sgu_norm_b', 'sgu_w', 'sgu_b', 'conv_w', 'conv_b', 'dt_bias', 'A_log',
           'D_skip', 'ssd_norm_w', 'w_out', 'final_norm_w')
SHARDED = ('w_in', 'conv_w', 'w_out')
PACK_ROW = 8 * LANE


def _constants():
    tri = np.tril(np.ones((CHUNK, CHUNK), np.float32))
    expand = np.zeros((DT_W, D), np.float32)
    for h in range(HEADS):
        expand[h, h * HEADDIM:(h + 1) * HEADDIM] = 1.0
    sel = np.zeros((D, LANE), np.float32)
    for g in range(SGU_GROUPS):
        sel[g * LANE:(g + 1) * LANE, g] = 1.0
    pos_chunk = np.arange(SGU_BLOCK) // CHUNK
    mask = (pos_chunk[None, :] <= pos_chunk[:, None]).astype(np.float32)
    shift = np.zeros(((CONV_K - 1) * CHUNK, HALO_BLK + CHUNK), np.float32)
    for kk in range(CONV_K - 1):
        for t in range(CHUNK):
            shift[kk * CHUNK + t, HALO_BLK - (CONV_K - 1) + t + kk] = 1.0
    return dict(tri=jnp.asarray(tri, BF16), triT=jnp.asarray(tri.T.copy(), BF16), expand=jnp.asarray(expand, BF16),
                shift=jnp.asarray(shift, BF16),
                expandT=jnp.asarray(expand.T.copy(), BF16), sel=jnp.asarray(sel), mask=jnp.asarray(mask))


def _to_shards(segs):
    starts = np.cumsum([0] + [s.shape[0] for s in segs])
    assert starts[-1] == W_IN
    slabs = []
    for k in range(N_DEV):
        pieces = []
        for s, s0 in zip(segs, starts[:-1]):
            lo, hi = max(k * SHARD_IN, s0), min((k + 1) * SHARD_IN, s0 + s.shape[0])
            if lo < hi:
                pieces.append(s[lo - s0:hi - s0])
        slabs.append(jnp.concatenate(pieces, axis=0))
    return jnp.stack(slabs)


def _local_step(x2, tgt, wpT, wout, cw, p, exchange):
    S = x2.shape[0]
    k = _constants()
    xn = _norm_fwd(x2, p['norm_w'], tm=min(512, S))
    proj = _matmul(xn, wpT, trans_b=True, n=WP, out_dtype=BF16, tm=min(1024, S), tn=1408, tk=D, name="in_proj")
    wm32 = p['sgu_w'][0] * k['mask']
    wm = wm32.astype(BF16)
    wmT = jnp.swapaxes(wm32, 1, 2).astype(BF16)
    bias_full = jnp.repeat(p['sgu_b'][0].T, LANE, axis=1)
    tm_sgu = min(256, S)
    ya = _sgu_fwd(proj, p['sgu_norm_g'], p['sgu_norm_b'], wm, bias_full, tm=tm_sgu)
    pad32 = lambda a: jnp.pad(a, ((0, 0), (0, DT_W - HEADS)))
    dtb_p, alog_p = pad32(p['dt_bias']), pad32(p['A_log'])
    d_exp = jnp.repeat(p['D_skip'], HEADDIM, axis=1)
    ssd_args = (cw, p['conv_b'], dtb_p, alog_p, d_exp, p['ssd_norm_w'])
    y, yb, states = _ssd_fwd(proj, *ssd_args, k['tri'], k['expand'], k['shift'])
    dh, dhb, mb, dya, dyb, dgl, loss, dfw, dgb = _head(
        x2, ya, yb, proj, tgt, p['gate_b'], wout, p['final_norm_w'][None, :], tm=min(256, S))
    dsgu, dws, dbsT, dsg, dsb = _sgu_bwd(proj, dya, p['sgu_norm_g'], p['sgu_norm_b'], wm, wmT, bias_full, k['mask'], k['sel'],
                                         tm=tm_sgu)
    dssd, dcw, dcb, ddtb, dalog, dD, dnw = _ssd_bwd(proj, dyb, y, states, *ssd_args, k['tri'], k['triT'], k['expand'], k['expandT'],
                                                    k['shift'])
    tk = min(4096, S)
    tn = 1024
    dwT_sgu = _matmul(dsgu, xn, trans_a=True, out_dtype=BF16, tm=1024, tn=tn, tk=tk, name="dw_in_sgu")
    dwT_gate = _matmul(dgl, xn, trans_a=True, out_dtype=BF16, tm=1024, tn=tn, tk=tk, name="dw_in_gate")
    dwT_ssd = _matmul(dssd, xn, trans_a=True, out_dtype=BF16, tm=1024, tn=tn, tk=tk, name="dw_in_ssd")
    dw_out = _matmul(mb, dhb, trans_a=True, out_dtype=BF16, tm=1024, tn=tn, tk=tk, name="dw_out")
    token = exchange([dwT_sgu, dwT_ssd[:W_IN - SEG_SSD[0]], dwT_gate], dw_out)
    tm = min(1024, S)
    dxn = _matmul(dsgu, wpT, tm=tm, tn=tn, tk=3072, after=token, name="dxn_sgu")
    dxn = _matmul(dgl, wpT, b_koff=SEG_GATE[0] // 2048, tm=tm, tn=tn, tk=2048, add=dxn, name="dxn_gate")
    dxn = _matmul(dssd, wpT, b_koff=SEG_SSD[0] // 2048, tm=tm, tn=tn, tk=2048, add=dxn, name="dxn_ssd")
    grad_x, dnorm = _norm_bwd(x2, p['norm_w'], dxn, dh, tm=min(256, S))
    grads = dict(
        norm_w=dnorm[0:1], gate_b=dgb[0:1], sgu_norm_g=dsg[0:1], sgu_norm_b=dsb[0:1], sgu_w=dws[None],
        sgu_b=dbsT[:, :SGU_GROUPS].T[None], conv_w=dcw[0:CONV_K][None], conv_b=dcb[0:1], dt_bias=ddtb[0:1, :HEADS],
        A_log=dalog[0:1, :HEADS], D_skip=dD[0:1, :HEADS], ssd_norm_w=dnw[0:1], final_norm_w=dfw[0])
    return loss[0, 0], grad_x, grads


def _pack(arrs):
    rows, offs, r = [], [], 0
    for a in arrs:
        n = a.size
        nr = -(-n // PACK_ROW) * 8
        rows.append(jnp.pad(a.reshape(-1).astype(F32), (0, nr * LANE - n)).reshape(nr, LANE))
        offs.append(r)
        r += nr
    return jnp.concatenate(rows, axis=0), offs


def kernel(x, norm_w, w_in, gate_b, sgu_norm_g, sgu_norm_b, sgu_w, sgu_b, conv_w, conv_b, dt_bias, A_log, D_skip, ssd_norm_w, w_out, final_norm_w, loss_target, m_norm_w, m_w_in, m_gate_b, m_sgu_norm_g, m_sgu_norm_b, m_sgu_w, m_sgu_b, m_conv_w, m_conv_b, m_dt_bias, m_A_log, m_D_skip, m_ssd_norm_w, m_w_out, m_final_norm_w, v_norm_w, v_w_in, v_gate_b, v_sgu_norm_g, v_sgu_norm_b, v_sgu_w, v_sgu_b, v_conv_w, v_conv_b, v_dt_bias, v_A_log, v_D_skip, v_ssd_norm_w, v_w_out, v_final_norm_w):
    w = dict(norm_w=norm_w, w_in=w_in, gate_b=gate_b, sgu_norm_g=sgu_norm_g, sgu_norm_b=sgu_norm_b, sgu_w=sgu_w, sgu_b=sgu_b,
             conv_w=conv_w, conv_b=conv_b, dt_bias=dt_bias, A_log=A_log, D_skip=D_skip, ssd_norm_w=ssd_norm_w, w_out=w_out,
             final_norm_w=final_norm_w)
    m = dict(norm_w=m_norm_w, w_in=m_w_in, gate_b=m_gate_b, sgu_norm_g=m_sgu_norm_g, sgu_norm_b=m_sgu_norm_b, sgu_w=m_sgu_w,
             sgu_b=m_sgu_b, conv_w=m_conv_w, conv_b=m_conv_b, dt_bias=m_dt_bias, A_log=m_A_log, D_skip=m_D_skip,
             ssd_norm_w=m_ssd_norm_w, w_out=m_w_out, final_norm_w=m_final_norm_w)
    v = dict(norm_w=v_norm_w, w_in=v_w_in, gate_b=v_gate_b, sgu_norm_g=v_sgu_norm_g, sgu_norm_b=v_sgu_norm_b, sgu_w=v_sgu_w,
             sgu_b=v_sgu_b, conv_w=v_conv_w, conv_b=v_conv_b, dt_bias=v_dt_bias, A_log=v_A_log, D_skip=v_D_skip,
             ssd_norm_w=v_ssd_norm_w, w_out=v_w_out, final_norm_w=v_final_norm_w)
    me = 4 * lax.axis_index("x") + 2 * lax.axis_index("y") + lax.axis_index("c")
    shard_cw = XBC_W // N_DEV

    tpose = lambda a: jnp.swapaxes(a[0], 0, 1)
    w3, g_out, g_cw = _gather_weights(tpose(w_in).astype(BF16).reshape((SHARD_IN,) + ROW_TILE), w_out[0].astype(BF16),
                                      conv_w[0], jnp.zeros((W_ROWS - W_IN,) + ROW_TILE, BF16))
    wpT = w3.reshape(W_ROWS, D)
    wout_full = g_out.reshape(D, D)
    cw_full = jnp.swapaxes(g_cw, 0, 1).reshape(CONV_K, XBC_W)

    flight = {}

    def exchange(dw_inT_segs, dw_out):
        parts = [_to_shards(dw_inT_segs), dw_out.reshape(N_DEV, D // N_DEV, D)]
        flight['sems'], flight['rsems'], flight['parts'], flight['lands'], token = _exchange_start(parts, name="exchange_start")
        return token

    loss_part, grad_x, grads = _local_step(x[0], loss_target[0], wpT, wout_full, cw_full, w, exchange)
    (own_in, own_out), (land_in, land_out) = _exchange_wait(
        flight['sems'], flight['rsems'], flight['parts'], flight['lands'], grad_x, name="exchange_wait")
    me_arr = jnp.reshape(me, (1,)).astype(jnp.int32)
    res = {}
    res['w_in'] = [jnp.swapaxes(o, 0, 1) for o in _adamw_own(
        me_arr, own_in, land_in, tpose(w_in), tpose(m_w_in), tpose(v_w_in), tr=SHARD_IN, tc=256, name="adamw_w_in")]
    res['w_out'] = _adamw_own(me_arr, own_out, land_out, w_out[0], m_w_out[0], v_w_out[0], tr=128, tc=D, name="adamw_w_out")

    small = [n for n in WEIGHTS if n not in SHARDED]
    packed, offs = _pack([grads[n] for n in small] + [loss_part, grads['conv_w']])
    (gathered,) = _all_gather([packed], name="gather_small")
    off_loss, off_cw = offs[-2], offs[-1]
    cw_parts = gathered[:, off_cw:, :].reshape(N_DEV, CONV_K, XBC_W)
    cw_parts = lax.dynamic_slice_in_dim(cw_parts, me * shard_cw, shard_cw, axis=2)
    cw_rows = _pack([cw_parts[0]])[0].shape[0]
    cw_parts = jnp.pad(cw_parts.reshape(N_DEV, -1), ((0, 0), (0, cw_rows * LANE - CONV_K * shard_cw))).reshape(N_DEV, cw_rows, LANE)
    parts = jnp.concatenate([gathered[:, :off_cw, :], cw_parts], axis=1)
    zero = jnp.zeros((), F32)
    packs = [_pack([d[n] for n in small] + [zero, d['conv_w']])[0] for d in (w, m, v)]
    outs = _adamw(parts, *packs, tr=parts.shape[1], name="adamw_small")

    def unpack(o, name):
        if name == 'conv_w':
            return o[off_cw:off_cw + cw_rows].reshape(-1)[:CONV_K * shard_cw].reshape(w['conv_w'].shape)
        r0 = offs[small.index(name)]
        n = w[name].size
        return o[r0:r0 + -(-n // PACK_ROW) * 8].reshape(-1)[:n].reshape(w[name].shape)

    for n in small + ['conv_w']:
        res[n] = [unpack(o, n) for o in outs]
    for n in ('w_in', 'w_out'):
        res[n] = [o[None] for o in res[n]]
    loss = outs[0][off_loss, 0]
    return (loss, grad_x[None], *[res[n][0] for n in WEIGHTS], *[res[n][1] for n in WEIGHTS],
            *[res[n][2] for n in WEIGHTS], *[res[n][3] for n in WEIGHTS])
```

```python
import functools

import numpy as np
import jax
import jax.numpy as jnp
from jax import lax
from jax.experimental import pallas as pl
from jax.experimental.pallas import tpu as pltpu

F32 = jnp.float32
BF16 = jnp.bfloat16
HI = lax.Precision.HIGHEST
MESH = pl.DeviceIdType.MESH

D = 2048
EPS = 1e-5
SGU_BLOCK = 128
SGU_GROUPS = 16
CHUNK = 64
HEADS = 32
HEADDIM = 64
SSD_GROUPS = 4
GROUP_W = D // SSD_GROUPS
STATE = 128
CONV_K = 4
XBC_W = D + 2 * SSD_GROUPS * STATE
W_IN = 15392
N_DEV = 8
SHARD_IN = W_IN // N_DEV
ADAM_LR, ADAM_B1, ADAM_B2, ADAM_EPS, ADAM_WD, ADAM_STEP = 0.001, 0.9, 0.999, 1e-08, 0.01, 10

LANE = 128
DT_W = LANE
OFF_U, OFF_V, OFF_ZA, OFF_G0, OFF_G1, OFF_ZB, OFF_XBC, OFF_DT = 0, 2048, 4096, 6144, 8192, 10240, 12288, 15360
WP = OFF_DT + DT_W
SEG_SGU = (0, 6144)
SEG_GATE = (6144, 4096)
SEG_SSD = (10240, WP - 10240)
SSD_PAD_W = 6144
VMEM_LIMIT = 56 * 1024 * 1024


def _cp(sem=None, vmem=VMEM_LIMIT):
    return pltpu.CompilerParams(dimension_semantics=sem, vmem_limit_bytes=vmem)


def _sigmoid(x):
    return 1.0 / (1.0 + jnp.exp(-x))


def _softplus(x):
    return jnp.maximum(x, 0.0) + jnp.log(1.0 + jnp.exp(-jnp.abs(x)))


def _dot(a, b, precision=None):
    return jnp.dot(a, b, preferred_element_type=F32, precision=precision)


def _dot_nt(a, b, precision=None):
    return lax.dot_general(a, b, (((1,), (1,)), ((), ())), preferred_element_type=F32, precision=precision)


def _dot_tn(a, b, precision=None):
    return lax.dot_general(a, b, (((0,), (0,)), ((), ())), preferred_element_type=F32, precision=precision)


def _split3(a):
    hi = a.astype(BF16)
    r = a - hi.astype(F32)
    mid = r.astype(BF16)
    return hi, mid, (r - mid.astype(F32)).astype(BF16)


def _sel_right(a, sel01):
    m = a.shape[0]
    r = _dot(jnp.concatenate(_split3(a), axis=0), sel01)
    return (r[0:m] + r[m:2 * m]) + r[2 * m:3 * m]


def _sel_left(sel01, a):
    n = a.shape[1]
    r = _dot(sel01, jnp.concatenate(_split3(a), axis=1))
    return (r[:, 0:n] + r[:, n:2 * n]) + r[:, 2 * n:3 * n]


def _matmul(a, b, *, trans_a=False, trans_b=False, b_koff=0, n=None, out_dtype=F32, tm, tn, tk, add=None, after=None, name):
    K, M = a.shape if trans_a else a.shape[::-1]
    N = (n or b.shape[0]) if trans_b else b.shape[1]
    assert M % tm == 0 and N % tn == 0 and K % tk == 0 and not (trans_a and trans_b), (name, M, N, K, tm, tn, tk)
    nk = K // tk

    def body(*refs):
        a_ref, b_ref = refs[:2]
        add_ref = refs[2] if add is not None else None
        o_ref, acc_ref = refs[-2:]
        k = pl.program_id(2)
        if trans_a:
            part = _dot_tn(a_ref[...], b_ref[...])
        else:
            part = _dot_nt(a_ref[...], b_ref[...]) if trans_b else _dot(a_ref[...], b_ref[...])

        def result(r):
            if add_ref is not None:
                r = r + add_ref[...]
            return r.astype(out_dtype)

        if nk == 1:
            o_ref[...] = result(part)
        else:
            @pl.when(k == 0)
            def _():
                acc_ref[...] = part

            @pl.when(jnp.logical_and(k > 0, k < nk - 1))
            def _():
                acc_ref[...] += part

            @pl.when(k == nk - 1)
            def _():
                o_ref[...] = result(acc_ref[...] + part)

    in_specs = [pl.BlockSpec((tk, tm), lambda i, j, k: (k, i)) if trans_a else pl.BlockSpec((tm, tk), lambda i, j, k: (i, k)),
                pl.BlockSpec((tn, tk), lambda i, j, k: (j, k)) if trans_b else pl.BlockSpec((tk, tn), lambda i, j, k: (k + b_koff, j))]
    args = [a, b]
    if add is not None:
        in_specs.append(pl.BlockSpec((tm, tn), lambda i, j, k: (i, j)))
        args.append(add)
    if after is not None:
        in_specs.append(pl.BlockSpec(memory_space=pl.ANY))
        args.append(after)
    return pl.pallas_call(
        body, name=name, grid=(M // tm, N // tn, nk), in_specs=in_specs,
        out_specs=pl.BlockSpec((tm, tn), lambda i, j, k: (i, j)),
        out_shape=jax.ShapeDtypeStruct((M, N), out_dtype),
        scratch_shapes=[pltpu.VMEM((tm, tn), F32)],
        compiler_params=_cp(("parallel", "parallel", "arbitrary")),
    )(*args)


def _norm_fwd(x, w, *, tm):
    S = x.shape[0]

    def body(x_ref, w_ref, o_ref):
        xv = x_ref[...]
        r = lax.rsqrt(jnp.mean(xv * xv, axis=-1, keepdims=True) + EPS)
        o_ref[...] = (xv * r * w_ref[...]).astype(BF16)

    return pl.pallas_call(
        body, name="norm_fwd", grid=(S // tm,),
        in_specs=[pl.BlockSpec((tm, D), lambda i: (i, 0)), pl.BlockSpec((1, D), lambda i: (0, 0))],
        out_specs=pl.BlockSpec((tm, D), lambda i: (i, 0)),
        out_shape=jax.ShapeDtypeStruct((S, D), BF16), compiler_params=_cp(("parallel",)),
    )(x, w)


def _norm_bwd(x, w, dxn, dh, *, tm):
    S = x.shape[0]

    def body(x_ref, w_ref, dxn_ref, dh_ref, gx_ref, dw_ref):
        xv = x_ref[...]
        r = lax.rsqrt(jnp.mean(xv * xv, axis=-1, keepdims=True) + EPS)
        xh = xv * r
        dxn_v = dxn_ref[...]
        dxh = dxn_v * w_ref[...]
        gx_ref[...] = dh_ref[...] + r * (dxh - xh * jnp.mean(dxh * xh, axis=-1, keepdims=True))

        @pl.when(pl.program_id(0) == 0)
        def _():
            dw_ref[...] = jnp.zeros_like(dw_ref)

        dw_ref[0:1, :] += jnp.sum(dxn_v * xh, axis=0, keepdims=True)

    row = pl.BlockSpec((tm, D), lambda i: (i, 0))
    return pl.pallas_call(
        body, name="norm_bwd", grid=(S // tm,),
        in_specs=[row, pl.BlockSpec((1, D), lambda i: (0, 0)), row, row],
        out_specs=[row, pl.BlockSpec((8, D), lambda i: (0, 0))],
        out_shape=[jax.ShapeDtypeStruct((S, D), F32), jax.ShapeDtypeStruct((8, D), F32)],
        compiler_params=_cp(("arbitrary",)),
    )(x, w, dxn, dh)


def _sgu_core(u_ref, v_ref, z_ref, g_ref, b_ref, wm_ref, bias_ref, vnb_ref, mixed_ref, tm):
    v = v_ref[...].astype(F32)
    mu = jnp.mean(v, axis=-1, keepdims=True)
    vc = v - mu
    rs = lax.rsqrt(jnp.mean(vc * vc, axis=-1, keepdims=True) + EPS)
    vh = vc * rs
    vnb_ref[...] = (vh * g_ref[...] + b_ref[...]).astype(BF16)
    for blk in range(tm // SGU_BLOCK):
        rows = pl.ds(blk * SGU_BLOCK, SGU_BLOCK)
        for gi in range(SGU_GROUPS):
            cols = pl.ds(gi * LANE, LANE)
            mixed_ref[rows, cols] = _dot(wm_ref[gi], vnb_ref[rows, cols]) + bias_ref[:, cols]
    return vh, rs


def _sgu_fwd(proj, g, b, wm, bias_full, *, tm):
    S = proj.shape[0]

    def body(u_ref, v_ref, z_ref, g_ref, b_ref, wm_ref, bias_ref, y_ref, vnb_ref, mixed_ref):
        _sgu_core(u_ref, v_ref, z_ref, g_ref, b_ref, wm_ref, bias_ref, vnb_ref, mixed_ref, tm)
        z = z_ref[...].astype(F32)
        y_ref[...] = (u_ref[...].astype(F32) * mixed_ref[...] * (z * _sigmoid(z))).astype(BF16)

    seg = lambda off: pl.BlockSpec((tm, D), lambda i: (i, off // D))
    full = lambda a: pl.BlockSpec(a.shape, lambda i: (0,) * a.ndim)
    return pl.pallas_call(
        body, name="sgu_fwd", grid=(S // tm,),
        in_specs=[seg(OFF_U), seg(OFF_V), seg(OFF_ZA), full(g), full(b), full(wm), full(bias_full)],
        out_specs=pl.BlockSpec((tm, D), lambda i: (i, 0)),
        out_shape=jax.ShapeDtypeStruct((S, D), BF16),
        scratch_shapes=[pltpu.VMEM((tm, D), BF16), pltpu.VMEM((tm, D), F32)],
        compiler_params=_cp(("parallel",)),
    )(proj, proj, proj, g, b, wm, bias_full)


def _sgu_bwd(proj, dy, g, b, wm, wmT, bias_full, mask, sel, *, tm):
    S = proj.shape[0]
    nsteps = S // tm

    def body(u_ref, v_ref, z_ref, dy_ref, g_ref, b_ref, wm_ref, wmT_ref, bias_ref, mask_ref, sel_ref,
             dp_ref, dws_ref, dbs_ref, dg_ref, db_ref, vnb_ref, mixed_ref, dmb_ref, dvn_ref, dbias_ref):
        i = pl.program_id(0)

        @pl.when(i == 0)
        def _():
            dws_ref[...] = jnp.zeros_like(dws_ref)
            dg_ref[...] = jnp.zeros_like(dg_ref)
            db_ref[...] = jnp.zeros_like(db_ref)
            dbias_ref[...] = jnp.zeros_like(dbias_ref)

        vh, rs = _sgu_core(u_ref, v_ref, z_ref, g_ref, b_ref, wm_ref, bias_ref, vnb_ref, mixed_ref, tm)
        u = u_ref[...].astype(F32)
        z = z_ref[...].astype(F32)
        dy_v = dy_ref[...].astype(F32)
        mixed = mixed_ref[...]
        sg = _sigmoid(z)
        sz = z * sg
        dp_ref[:, 0:D] = (dy_v * mixed * sz).astype(BF16)
        dp_ref[:, 2 * D:3 * D] = (dy_v * u * mixed * (sg * (1.0 + z * (1.0 - sg)))).astype(BF16)
        dmixed = dy_v * u * sz
        dmb_ref[...] = dmixed.astype(BF16)
        for blk in range(tm // SGU_BLOCK):
            dbias_ref[...] += dmixed[blk * SGU_BLOCK:(blk + 1) * SGU_BLOCK, :]
        for blk in range(tm // SGU_BLOCK):
            rows = pl.ds(blk * SGU_BLOCK, SGU_BLOCK)
            for gi in range(SGU_GROUPS):
                cols = pl.ds(gi * LANE, LANE)
                dm = dmb_ref[rows, cols]
                dvn_ref[rows, cols] = _dot(wmT_ref[gi], dm)
                dws_ref[gi] += _dot_nt(dm, vnb_ref[rows, cols])
        dvn = dvn_ref[...]
        dg_ref[0:1, :] += jnp.sum(dvn * vh, axis=0, keepdims=True)
        db_ref[0:1, :] += jnp.sum(dvn, axis=0, keepdims=True)
        dvh = dvn * g_ref[...]
        dv = rs * (dvh - jnp.mean(dvh, axis=-1, keepdims=True) - vh * jnp.mean(dvh * vh, axis=-1, keepdims=True))
        dp_ref[:, D:2 * D] = dv.astype(BF16)

        @pl.when(i == nsteps - 1)
        def _():
            for gi in range(SGU_GROUPS):
                dws_ref[gi] = dws_ref[gi] * mask_ref[...]
            dbs_ref[...] = _dot(dbias_ref[...], sel_ref[...], precision=HI)

    seg = lambda off: pl.BlockSpec((tm, D), lambda i: (i, off // D))
    full = lambda a: pl.BlockSpec(a.shape, lambda i: (0,) * a.ndim)
    return pl.pallas_call(
        body, name="sgu_bwd", grid=(nsteps,),
        in_specs=[seg(OFF_U), seg(OFF_V), seg(OFF_ZA), pl.BlockSpec((tm, D), lambda i: (i, 0)),
                  full(g), full(b), full(wm), full(wmT), full(bias_full), full(mask), full(sel)],
        out_specs=[pl.BlockSpec((tm, 3 * D), lambda i: (i, 0)),
                   pl.BlockSpec((SGU_GROUPS, SGU_BLOCK, SGU_BLOCK), lambda i: (0, 0, 0)),
                   pl.BlockSpec((SGU_BLOCK, LANE), lambda i: (0, 0)),
                   pl.BlockSpec((8, D), lambda i: (0, 0)), pl.BlockSpec((8, D), lambda i: (0, 0))],
        out_shape=[jax.ShapeDtypeStruct((S, 3 * D), BF16),
                   jax.ShapeDtypeStruct((SGU_GROUPS, SGU_BLOCK, SGU_BLOCK), F32),
                   jax.ShapeDtypeStruct((SGU_BLOCK, LANE), F32),
                   jax.ShapeDtypeStruct((8, D), F32), jax.ShapeDtypeStruct((8, D), F32)],
        scratch_shapes=[pltpu.VMEM((tm, D), BF16), pltpu.VMEM((tm, D), F32), pltpu.VMEM((tm, D), BF16),
                        pltpu.VMEM((tm, D), F32), pltpu.VMEM((SGU_BLOCK, D), F32)],
        compiler_params=_cp(("arbitrary",)),
    )(proj, proj, proj, dy, g, b, wm, wmT, bias_full, mask, sel)


SSD_T = 2 * CHUNK
HALO = 8
HALO_BLK = 16


def _pair_masks():
    row = lax.broadcasted_iota(jnp.int32, (CHUNK, LANE), 0)
    lane = lax.broadcasted_iota(jnp.int32, (CHUNK, LANE), 1)
    pos = jnp.where(lane >= CHUNK, lane - CHUNK, lane)
    diag = (row == pos).astype(F32)
    causal = row >= pos
    lo = (lane < CHUNK).astype(F32)
    return diag, causal, lo, 1.0 - lo


def _ssd_chunk_fwd(c, ext_ref, shift_ref, dt_ref, cw_ref, cb_ref, dtb_ref, alog_ref, tri_ref, exp_ref):
    r0 = c * CHUNK
    win = ext_ref[pl.ds(r0, HALO_BLK + CHUNK), :]
    sh = _dot(shift_ref[...], win)
    taps = [sh[k * CHUNK:(k + 1) * CHUNK] for k in range(CONV_K - 1)] + [win[HALO_BLK:].astype(F32)]
    pre = cb_ref[...] + sum(cw_ref[k:k + 1, :] * taps[k] for k in range(CONV_K))
    sg = _sigmoid(pre)
    xc = pre * sg
    dtr = dt_ref[pl.ds(r0, CHUNK), :].astype(F32) + dtb_ref[...]
    dtv = _softplus(dtr)
    A = -jnp.exp(alog_ref[...])
    acs = _sel_left(tri_ref[...], dtv * A)
    both = _sel_right(jnp.concatenate([acs, dtv], axis=0), exp_ref[...])
    E, dtE = both[0:CHUNK], both[CHUNK:2 * CHUNK]
    return dict(taps=taps, pre=pre, sg=sg, xc=xc, dtr=dtr, dtv=dtv, A=A, E=E, dtE=dtE)


def _ssd_fwd(proj, conv_w, conv_b, dtb_p, alog_p, d_exp, norm_w, tri, expand, shift):
    S = proj.shape[0]
    T = SSD_T
    nsteps = S // T
    ncl = T // CHUNK

    def body(zb_ref, xbc_ref, halo_ref, dt_ref, cw_ref, cb_ref, dtb_ref, alog_ref, dexp_ref, nw_ref, tri_ref, exp_ref, shift_ref,
             y_ref, yb_ref, st_ref, ht_ref, ext_ref):
        i = pl.program_id(0)

        @pl.when(i == 0)
        def _():
            ht_ref[...] = jnp.zeros_like(ht_ref)
            ext_ref[0:HALO_BLK, :] = jnp.zeros((HALO_BLK, XBC_W), BF16)

        @pl.when(i > 0)
        def _():
            ext_ref[0:HALO_BLK, :] = halo_ref[...]

        ext_ref[HALO_BLK:HALO_BLK + T, :] = xbc_ref[...]
        diag, causal, lo, hi = _pair_masks()
        for c in range(ncl):
            q = _ssd_chunk_fwd(c, ext_ref, shift_ref, dt_ref, cw_ref, cb_ref, dtb_ref, alog_ref, tri_ref, exp_ref)
            rows = pl.ds(c * CHUNK, CHUNK)
            xc, E, dtE = q["xc"], q["E"], q["dtE"]
            xs = xc[:, 0:D]
            total = E[CHUNK - 1:CHUNK, :]
            x_dt = xs * dtE
            eE = jnp.exp(E)
            xw = x_dt * jnp.exp(total - E)
            st_ref[c] = ht_ref[...]
            for g in range(SSD_GROUPS):
                gc = slice(g * GROUP_W, (g + 1) * GROUP_W)
                Bg = xc[:, D + g * STATE:D + (g + 1) * STATE].astype(BF16)
                Cg = xc[:, D + SSD_GROUPS * STATE + g * STATE:D + SSD_GROUPS * STATE + (g + 1) * STATE].astype(BF16)
                cb2 = _dot_nt(Cg, jnp.concatenate([Bg, Bg], axis=0))
                htg = ht_ref[:, gc]
                y_ref[rows, gc] = eE[:, gc] * _dot(Cg, htg.astype(BF16)) + xs[:, gc] * dexp_ref[:, gc]
                for jj in range(GROUP_W // LANE):
                    pc = slice(g * GROUP_W + jj * LANE, g * GROUP_W + (jj + 1) * LANE)
                    Ej = E[:, pc]
                    e2 = jnp.sum(Ej * diag, axis=0, keepdims=True)
                    Mp = cb2 * jnp.exp(jnp.where(causal, Ej - e2, -1e30))
                    xj = x_dt[:, pc]
                    xbd = jnp.concatenate([xj * lo, xj * hi], axis=0).astype(BF16)
                    y_ref[rows, pc] += _dot(Mp.astype(BF16), xbd)
                ht_ref[:, gc] = jnp.exp(total[:, gc]) * htg + _dot_tn(Bg, xw[:, gc].astype(BF16))
            zb = zb_ref[rows, :].astype(F32)
            hh = y_ref[rows, :] * (zb * _sigmoid(zb))
            for g in range(SSD_GROUPS):
                gc = slice(g * GROUP_W, (g + 1) * GROUP_W)
                hg = hh[:, gc]
                r = lax.rsqrt(jnp.mean(hg * hg, axis=-1, keepdims=True) + EPS)
                yb_ref[rows, gc] = (hg * r * nw_ref[:, gc]).astype(BF16)

    full = lambda a: pl.BlockSpec(a.shape, lambda i: (0,) * a.ndim)
    hb = T // HALO_BLK
    return pl.pallas_call(
        body, name="ssd_fwd", grid=(nsteps,),
        in_specs=[pl.BlockSpec((T, D), lambda i: (i, OFF_ZB // D)),
                  pl.BlockSpec((T, XBC_W), lambda i: (i, OFF_XBC // XBC_W)),
                  pl.BlockSpec((HALO_BLK, XBC_W), lambda i: (jnp.maximum(i * hb - 1, 0), OFF_XBC // XBC_W)),
                  pl.BlockSpec((T, DT_W), lambda i: (i, OFF_DT // DT_W)),
                  full(conv_w), full(conv_b), full(dtb_p), full(alog_p), full(d_exp), full(norm_w), full(tri), full(expand),
                  full(shift)],
        out_specs=[pl.BlockSpec((T, D), lambda i: (i, 0)), pl.BlockSpec((T, D), lambda i: (i, 0)),
                   pl.BlockSpec((ncl, STATE, D), lambda i: (i, 0, 0))],
        out_shape=[jax.ShapeDtypeStruct((S, D), F32), jax.ShapeDtypeStruct((S, D), BF16),
                   jax.ShapeDtypeStruct((S // CHUNK, STATE, D), F32)],
        scratch_shapes=[pltpu.VMEM((STATE, D), F32), pltpu.VMEM((HALO_BLK + T, XBC_W), BF16)],
        compiler_params=_cp(("arbitrary",)),
    )(proj, proj, proj, proj, conv_w, conv_b, dtb_p, alog_p, d_exp, norm_w, tri, expand, shift)


def _ssd_bwd(proj, dyb, y, states, conv_w, conv_b, dtb_p, alog_p, d_exp, norm_w, tri, triT, expand, expandT, shift):
    S = proj.shape[0]
    T = SSD_T
    nsteps = S // T
    ncl = T // CHUNK
    SSD_W = SSD_PAD_W

    def body(zb_ref, xbc_ref, halo_ref, dt_ref, dyb_ref, y_ref, st_ref, cw_ref, cb_ref, dtb_ref, alog_ref, dexp_ref, nw_ref,
             tri_ref, triT_ref, exp_ref, expT_ref, shift_ref,
             dp_ref, dcw_ref, dcb_ref, ddtb_ref, dalog_ref, dD_ref, dnw_ref,
             dht_ref, ext_ref, dpre_ref, dy_s, dE_s, dxdt_s, dxc_s, dDacc_ref, dAacc_ref):
        i = pl.program_id(0)

        @pl.when(i == 0)
        def _():
            for r in (dht_ref, dcw_ref, dcb_ref, ddtb_ref, dnw_ref, dDacc_ref, dAacc_ref):
                r[...] = jnp.zeros_like(r)
            dpre_ref[T:T + HALO, :] = jnp.zeros((HALO, XBC_W), F32)

        @pl.when(i == nsteps - 1)
        def _():
            ext_ref[0:HALO_BLK, :] = jnp.zeros((HALO_BLK, XBC_W), BF16)

        @pl.when(i < nsteps - 1)
        def _():
            ext_ref[0:HALO_BLK, :] = halo_ref[...]

        ext_ref[HALO_BLK:HALO_BLK + T, :] = xbc_ref[...]
        diag, causal, lo, hi = _pair_masks()
        last_row = (lax.broadcasted_iota(jnp.int32, (CHUNK, 1), 0) == CHUNK - 1).astype(F32)
        for c in reversed(range(ncl)):
            q = _ssd_chunk_fwd(c, ext_ref, shift_ref, dt_ref, cw_ref, cb_ref, dtb_ref, alog_ref, tri_ref, exp_ref)
            rows = pl.ds(c * CHUNK, CHUNK)
            pre, sg, xc, dtr, dtv, A, E, dtE = (q[k] for k in ("pre", "sg", "xc", "dtr", "dtv", "A", "E", "dtE"))
            xs = xc[:, 0:D]
            total = E[CHUNK - 1:CHUNK, :]
            x_dt = xs * dtE
            eE = jnp.exp(E)
            wdec = jnp.exp(total - E)
            zb = zb_ref[rows, :].astype(F32)
            yv = y_ref[rows, :]
            sgz = _sigmoid(zb)
            sz = zb * sgz
            hh = yv * sz
            for g in range(SSD_GROUPS):
                gc = slice(g * GROUP_W, (g + 1) * GROUP_W)
                hg = hh[:, gc]
                r = lax.rsqrt(jnp.mean(hg * hg, axis=-1, keepdims=True) + EPS)
                dyb_g = dyb_ref[rows, gc].astype(F32)
                dn = dyb_g * nw_ref[:, gc]
                dnw_ref[0:1, gc] += jnp.sum(dyb_g * hg * r, axis=0, keepdims=True)
                dy_s[:, gc] = r * dn - hg * (r * r * r) * jnp.mean(dn * hg, axis=-1, keepdims=True)
            dhh = dy_s[...]
            dp_ref[rows, 0:D] = (dhh * yv * (sgz * (1.0 + zb * (1.0 - sgz)))).astype(BF16)
            dy = dhh * sz
            dy_s[...] = dy
            dDacc_ref[0:1, :] += jnp.sum(dy * xs, axis=0, keepdims=True)
            dxc_s[:, 0:D] = dy * dexp_ref[...]
            for g in range(SSD_GROUPS):
                gc = slice(g * GROUP_W, (g + 1) * GROUP_W)
                bcol = slice(D + g * STATE, D + (g + 1) * STATE)
                ccol = slice(D + SSD_GROUPS * STATE + g * STATE, D + SSD_GROUPS * STATE + (g + 1) * STATE)
                Bg = xc[:, bcol].astype(BF16)
                Cg = xc[:, ccol].astype(BF16)
                B2 = jnp.concatenate([Bg, Bg], axis=0)
                cb2 = _dot_nt(Cg, B2)
                htg = st_ref[c, :, gc]
                htb = htg.astype(BF16)
                dhn = dht_ref[:, gc]
                dhnb = dhn.astype(BF16)
                dyg = dy[:, gc]
                eEg = eE[:, gc]
                wg = wdec[:, gc]
                xdg = x_dt[:, gc]
                CH = _dot(Cg, htb)
                dCHb = (dyg * eEg).astype(BF16)
                dC = _dot_nt(dCHb, htb)
                dl = jnp.exp(total[:, gc])
                dht_prev = _dot_tn(Cg, dCHb) + dl * dhn
                dtot = jnp.sum(dhn * htg, axis=0, keepdims=True) * dl
                dxw = _dot(Bg, dhnb)
                dB = _dot_nt((xdg * wg).astype(BF16), dhnb)
                dwd = dxw * xdg * wg
                dtot = dtot + jnp.sum(dwd, axis=0, keepdims=True)
                dE_s[:, gc] = dyg * eEg * CH - dwd + last_row * dtot
                dxdt_s[:, gc] = dxw * wg
                dcb2 = jnp.zeros((CHUNK, LANE), F32)
                for jj in range(GROUP_W // LANE):
                    pc = slice(g * GROUP_W + jj * LANE, g * GROUP_W + (jj + 1) * LANE)
                    Ej = E[:, pc]
                    e2 = jnp.sum(Ej * diag, axis=0, keepdims=True)
                    Lp = jnp.exp(jnp.where(causal, Ej - e2, -1e30))
                    Mp = cb2 * Lp
                    xj = x_dt[:, pc]
                    xbd = jnp.concatenate([xj * lo, xj * hi], axis=0).astype(BF16)
                    dyj = dy[:, pc].astype(BF16)
                    dMp = _dot_nt(dyj, xbd)
                    dxbd = _dot_tn(Mp.astype(BF16), dyj)
                    dxdt_s[:, pc] += dxbd[0:CHUNK, :] * lo + dxbd[CHUNK:2 * CHUNK, :] * hi
                    dcb2 = dcb2 + dMp * Lp
                    dseg = dMp * Mp
                    dE_s[:, pc] += dseg - diag * jnp.sum(dseg, axis=0, keepdims=True)
                dcb2b = dcb2.astype(BF16)
                dC = dC + _dot(dcb2b, B2)
                dB2 = _dot_tn(dcb2b, Cg)
                dB = dB + dB2[0:CHUNK, :] + dB2[CHUNK:2 * CHUNK, :]
                dxc_s[:, bcol] = dB
                dxc_s[:, ccol] = dC
                dht_ref[:, gc] = dht_prev
            dx_dt = dxdt_s[...]
            dxc_s[:, 0:D] += dx_dt * dtE
            red = _sel_right(jnp.concatenate([dE_s[...], dx_dt * xs], axis=0), expT_ref[...])
            da = _sel_left(triT_ref[...], red[0:CHUNK, :])
            ddtv = red[CHUNK:2 * CHUNK, :] + da * A
            dAacc_ref[0:1, :] += jnp.sum(da * dtv, axis=0, keepdims=True)
            ddtr = ddtv * _sigmoid(dtr)
            ddtb_ref[0:1, :] += jnp.sum(ddtr, axis=0, keepdims=True)
            dp_ref[rows, D + XBC_W:D + XBC_W + DT_W] = ddtr.astype(BF16)
            dpre = dxc_s[...] * (sg * (1.0 + pre * (1.0 - sg)))
            dpre_ref[rows, :] = dpre
            dcb_ref[0:1, :] += jnp.sum(dpre, axis=0, keepdims=True)
            for k in range(CONV_K):
                dcw_ref[k:k + 1, :] += jnp.sum(dpre * q["taps"][k], axis=0, keepdims=True)
        dxbc = jnp.zeros((T, XBC_W), F32)
        for k in range(CONV_K):
            dxbc = dxbc + cw_ref[k:k + 1, :] * dpre_ref[pl.ds(CONV_K - 1 - k, T), :]
        dp_ref[:, D:D + XBC_W] = dxbc.astype(BF16)
        dp_ref[:, SEG_SSD[1]:SSD_W] = jnp.zeros((T, SSD_W - SEG_SSD[1]), BF16)
        dpre_ref[T:T + HALO, :] = dpre_ref[0:HALO, :]

        @pl.when(i == nsteps - 1)
        def _():
            dalog_ref[...] = dAacc_ref[...] * (-jnp.exp(alog_ref[...]))
            dD_ref[...] = _dot(dDacc_ref[...], expT_ref[...].astype(F32), precision=HI)

    full = lambda a: pl.BlockSpec(a.shape, lambda i: (0,) * a.ndim)
    hb = T // HALO_BLK
    rev = lambda i: nsteps - 1 - i
    acc = lambda w: pl.BlockSpec((8, w), lambda i: (0, 0))
    return pl.pallas_call(
        body, name="ssd_bwd", grid=(nsteps,),
        in_specs=[pl.BlockSpec((T, D), lambda i: (rev(i), OFF_ZB // D)),
                  pl.BlockSpec((T, XBC_W), lambda i: (rev(i), OFF_XBC // XBC_W)),
                  pl.BlockSpec((HALO_BLK, XBC_W), lambda i: (jnp.maximum(rev(i) * hb - 1, 0), OFF_XBC // XBC_W)),
                  pl.BlockSpec((T, DT_W), lambda i: (rev(i), OFF_DT // DT_W)),
                  pl.BlockSpec((T, D), lambda i: (rev(i), 0)), pl.BlockSpec((T, D), lambda i: (rev(i), 0)),
                  pl.BlockSpec((ncl, STATE, D), lambda i: (rev(i), 0, 0)),
                  full(conv_w), full(conv_b), full(dtb_p), full(alog_p), full(d_exp), full(norm_w),
                  full(tri), full(triT), full(expand), full(expandT), full(shift)],
        out_specs=[pl.BlockSpec((T, SSD_W), lambda i: (rev(i), 0)),
                   acc(XBC_W), acc(XBC_W), acc(DT_W), acc(DT_W), acc(DT_W), acc(D)],
        out_shape=[jax.ShapeDtypeStruct((S, SSD_W), BF16),
                   jax.ShapeDtypeStruct((8, XBC_W), F32), jax.ShapeDtypeStruct((8, XBC_W), F32),
                   jax.ShapeDtypeStruct((8, DT_W), F32), jax.ShapeDtypeStruct((8, DT_W), F32),
                   jax.ShapeDtypeStruct((8, DT_W), F32), jax.ShapeDtypeStruct((8, D), F32)],
        scratch_shapes=[pltpu.VMEM((STATE, D), F32), pltpu.VMEM((HALO_BLK + T, XBC_W), BF16), pltpu.VMEM((T + HALO, XBC_W), F32),
                        pltpu.VMEM((CHUNK, D), F32), pltpu.VMEM((CHUNK, D), F32), pltpu.VMEM((CHUNK, D), F32),
                        pltpu.VMEM((CHUNK, XBC_W), F32), pltpu.VMEM((8, D), F32), pltpu.VMEM((8, DT_W), F32)],
        compiler_params=_cp(("arbitrary",)),
    )(proj, proj, proj, proj, dyb, y, states, conv_w, conv_b, dtb_p, alog_p, d_exp, norm_w, tri, triT, expand, expandT, shift)


def _head(x, ya, yb, proj, target, gate_b, wout, fw, *, tm):
    S = x.shape[0]

    def body(x_ref, ya_ref, yb_ref, gl0_ref, gl1_ref, t_ref, gb_ref, w_ref, fw_ref,
             dh_ref, dhb_ref, mb_ref, dya_ref, dyb_ref, dgl_ref, loss_ref, dfw_ref, dgb_ref):
        @pl.when(pl.program_id(0) == 0)
        def _():
            loss_ref[...] = jnp.zeros_like(loss_ref)
            dfw_ref[...] = jnp.zeros_like(dfw_ref)
            dgb_ref[...] = jnp.zeros_like(dgb_ref)

        ya_v = ya_ref[...].astype(F32)
        yb_v = yb_ref[...].astype(F32)
        g0 = _sigmoid(gl0_ref[...].astype(F32) + gb_ref[:, 0:D])
        g1 = _sigmoid(gl1_ref[...].astype(F32) + gb_ref[:, D:2 * D])
        mb = (g0 * ya_v + g1 * yb_v).astype(BF16)
        mb_ref[...] = mb
        h = x_ref[...] + _dot(mb, w_ref[...])
        r = lax.rsqrt(jnp.mean(h * h, axis=-1, keepdims=True) + EPS)
        hn = h * r
        err = hn * fw_ref[...] - t_ref[...]
        loss_ref[...] += 0.5 * jnp.sum(jnp.mean(err * err, axis=-1, keepdims=True))
        dyf = err * (1.0 / D)
        dfw_ref[0:1, :] += jnp.sum(dyf * hn, axis=0, keepdims=True)
        dhn = dyf * fw_ref[...]
        dh = r * (dhn - hn * jnp.mean(dhn * hn, axis=-1, keepdims=True))
        dh_ref[...] = dh
        dhb = dh.astype(BF16)
        dhb_ref[...] = dhb
        dm = _dot_nt(dhb, w_ref[...])
        dya_ref[...] = (dm * g0).astype(BF16)
        dyb_ref[...] = (dm * g1).astype(BF16)
        dgl0 = dm * ya_v * g0 * (1.0 - g0)
        dgl1 = dm * yb_v * g1 * (1.0 - g1)
        dgl_ref[:, 0:D] = dgl0.astype(BF16)
        dgl_ref[:, D:2 * D] = dgl1.astype(BF16)
        dgb_ref[0:1, 0:D] += jnp.sum(dgl0, axis=0, keepdims=True)
        dgb_ref[0:1, D:2 * D] += jnp.sum(dgl1, axis=0, keepdims=True)

    row = pl.BlockSpec((tm, D), lambda i: (i, 0))
    seg = lambda off: pl.BlockSpec((tm, D), lambda i: (i, off // D))
    full = lambda a: pl.BlockSpec(a.shape, lambda i: (0,) * a.ndim)
    acc = lambda w: pl.BlockSpec((8, w), lambda i: (0, 0))
    return pl.pallas_call(
        body, name="head", grid=(S // tm,),
        in_specs=[row, row, row, seg(OFF_G0), seg(OFF_G1), row, full(gate_b), full(wout), full(fw)],
        out_specs=[row, row, row, row, row, pl.BlockSpec((tm, 2 * D), lambda i: (i, 0)), acc(LANE), acc(D), acc(2 * D)],
        out_shape=[jax.ShapeDtypeStruct((S, D), F32), jax.ShapeDtypeStruct((S, D), BF16), jax.ShapeDtypeStruct((S, D), BF16),
                   jax.ShapeDtypeStruct((S, D), BF16), jax.ShapeDtypeStruct((S, D), BF16), jax.ShapeDtypeStruct((S, 2 * D), BF16),
                   jax.ShapeDtypeStruct((8, LANE), F32), jax.ShapeDtypeStruct((8, D), F32), jax.ShapeDtypeStruct((8, 2 * D), F32)],
        compiler_params=_cp(("arbitrary",)),
    )(x, ya, yb, proj, proj, target, gate_b, wout, fw)


def _adam_update(g, w_ref, m_ref, v_ref, g_ref, d_ref, m2_ref, v2_ref):
    m2 = ADAM_B1 * m_ref[...] + (1.0 - ADAM_B1) * g
    v2 = ADAM_B2 * v_ref[...] + (1.0 - ADAM_B2) * (g * g)
    m_hat = m2 / (1.0 - ADAM_B1 ** ADAM_STEP)
    v_hat = v2 / (1.0 - ADAM_B2 ** ADAM_STEP)
    g_ref[...] = g
    d_ref[...] = -ADAM_LR * (m_hat / (jnp.sqrt(v_hat) + ADAM_EPS) + ADAM_WD * w_ref[...])
    m2_ref[...] = m2
    v2_ref[...] = v2


def _adamw_own(me, own, landed, w, m, v, *, tr, tc, name):
    _, R, C = landed.shape
    assert R % tr == 0 and C % tc == 0, (name, R, C, tr, tc)

    def body(me_ref, own_ref, p_ref, w_ref, m_ref, v_ref, g_ref, d_ref, m2_ref, v2_ref):
        mine = own_ref[0].astype(F32)
        g = jnp.where(me_ref[0] == 0, mine, p_ref[0].astype(F32))
        for k in range(1, N_DEV):
            g = g + jnp.where(me_ref[0] == k, mine, p_ref[k].astype(F32))
        _adam_update(g, w_ref, m_ref, v_ref, g_ref, d_ref, m2_ref, v2_ref)

    tile = pl.BlockSpec((tr, tc), lambda i, j, me_ref: (i, j))
    return pl.pallas_call(
        body, name=name,
        grid_spec=pltpu.PrefetchScalarGridSpec(
            num_scalar_prefetch=1, grid=(R // tr, C // tc),
            in_specs=[pl.BlockSpec((1, tr, tc), lambda i, j, me_ref: (me_ref[0], i, j)),
                      pl.BlockSpec((N_DEV, tr, tc), lambda i, j, me_ref: (0, i, j)), tile, tile, tile],
            out_specs=[tile, tile, tile, tile]),
        out_shape=[jax.ShapeDtypeStruct((R, C), F32)] * 4,
        compiler_params=_cp(("parallel", "parallel")),
    )(me, own, landed, w, m, v)


def _adamw(parts, w, m, v, *, tr, name):
    _, R, C = parts.shape
    assert R % tr == 0, (name, R, tr)

    def body(p_ref, w_ref, m_ref, v_ref, g_ref, d_ref, m2_ref, v2_ref):
        g = p_ref[0].astype(F32)
        for k in range(1, N_DEV):
            g = g + p_ref[k].astype(F32)
        _adam_update(g, w_ref, m_ref, v_ref, g_ref, d_ref, m2_ref, v2_ref)

    row = pl.BlockSpec((tr, C), lambda i: (i, 0))
    return pl.pallas_call(
        body, name=name, grid=(R // tr,),
        in_specs=[pl.BlockSpec((N_DEV, tr, C), lambda i: (0, i, 0)), row, row, row],
        out_specs=[row, row, row, row],
        out_shape=[jax.ShapeDtypeStruct((R, C), F32)] * 4,
        compiler_params=_cp(("parallel",)),
    )(parts, w, m, v)


def _place():
    x, y, c = lax.axis_index("x"), lax.axis_index("y"), lax.axis_index("c")
    return x, y, c


def _all_gather(arrs, *, name):
    n = len(arrs)

    def body(*refs):
        ins, outs = refs[:n], refs[n:2 * n]
        send_sems, recv_sems, local_sems = refs[2 * n:]
        x, y, c = _place()
        me, sibling = (x, y, c), (x, y, 1 - c)
        chips = [(1 - x, y), (x, 1 - y), (1 - x, 1 - y)]

        def idx(px, py, pc):
            return 4 * px + 2 * py + pc

        def copy(k, a, block, to, src=None):
            slab = outs[a].at[idx(*block)]
            return pltpu.make_async_remote_copy(
                src_ref=slab if src is None else src, dst_ref=slab,
                send_sem=send_sems.at[k, a], recv_sem=recv_sems.at[k, a], device_id=to, device_id_type=MESH)

        mine = [pltpu.make_async_copy(ins[a], outs[a].at[idx(*me)], local_sems.at[a]) for a in range(n)]
        for cp in mine:
            cp.start()
        first = []
        for a in range(n):
            first.append(copy(0, a, me, sibling, src=ins[a]))
            first += [copy(1 + j, a, me, (*chip, c), src=ins[a]) for j, chip in enumerate(chips)]
        for cp in first:
            cp.start()
        passed = []
        for j, chip in enumerate(chips):
            for a in range(n):
                copy(1 + j, a, (*chip, c), me).wait_recv()
                fwd = copy(4 + j, a, (*chip, c), sibling)
                fwd.start()
                passed.append(fwd)
        for a in range(n):
            copy(0, a, sibling, me).wait_recv()
            for j, chip in enumerate(chips):
                copy(4 + j, a, (*chip, 1 - c), me).wait_recv()
        for cp in first + passed:
            cp.wait_send()
        for cp in mine:
            cp.wait()

    anyspec = pl.BlockSpec(memory_space=pl.ANY)
    return pl.pallas_call(
        body, name=name,
        in_specs=[anyspec] * n, out_specs=[anyspec] * n,
        out_shape=[jax.ShapeDtypeStruct((N_DEV,) + a.shape, a.dtype) for a in arrs],
        scratch_shapes=[pltpu.SemaphoreType.DMA((7, n)), pltpu.SemaphoreType.DMA((7, n)), pltpu.SemaphoreType.DMA((n,))],
    )(*arrs)


W_ROWS = SEG_SSD[0] + SSD_PAD_W


GROUP = 16
INTERIOR = 1920


def _interior(k):
    lo = -(-(k * SHARD_IN) // GROUP) * GROUP
    hi = ((k + 1) * SHARD_IN) // GROUP * GROUP
    return lo, hi


def _dest_row(r):
    return r if r < 6144 else (r - 6144 + SEG_SSD[0] if r < 11296 else r - 11296 + SEG_GATE[0])


def _shard_pieces(k):
    lo_k, hi_k = _interior(k)
    out = []
    for lo, hi in ((0, 6144), (6144, 11296), (11296, W_IN)):
        a, b = max(lo, lo_k), min(hi, hi_k)
        if a < b:
            out.append((a - lo_k, b - a, _dest_row(a)))
    return out


def _patch_straddlers(wpT, heads, tails):
    for k in range(1, N_DEV):
        m = (k * SHARD_IN) % GROUP
        if m:
            group = jnp.concatenate([tails[k - 1, GROUP - m:], heads[k, :GROUP - m]], axis=0)
            wpT = lax.dynamic_update_slice(wpT, group, (_dest_row(k * SHARD_IN - m), 0))
    return wpT


def _gather_weights(win, head, tail, wout, cw, zeros):
    n_zero = zeros.shape[0]
    assert W_IN + n_zero == W_ROWS and W_IN % GROUP == 0
    small_in = (wout, cw, head, tail)

    def run(k, win_ref, wout_ref, cw_ref, head_ref, tail_ref, z_ref, w_out_ref, gout_ref, gcw_ref, ghead_ref, gtail_ref,
            send_sems, recv_sems, local_sems):
        x, y, c = k // 4, (k // 2) % 2, k % 2
        idx = lambda p: 4 * p[0] + 2 * p[1] + p[2]
        me, sib = (x, y, c), (x, y, 1 - c)
        xn, yn, dg = (1 - x, y, c), (x, 1 - y, c), (1 - x, 1 - y, c)
        small = ((wout_ref, gout_ref), (cw_ref, gcw_ref), (head_ref, ghead_ref), (tail_ref, gtail_ref))

        def copies(slot, block, to, own=False):
            kb = idx(block)
            out = []
            for j, (s0, n, d0) in enumerate(_shard_pieces(kb)):
                dst = w_out_ref.at[pl.ds(d0, n)]
                out.append((win_ref.at[pl.ds(s0, n)] if own else dst, dst, j))
            for j, (src, gathered) in enumerate(small):
                out.append((src if own else gathered.at[kb], gathered.at[kb], 2 + j))
            return [pltpu.make_async_remote_copy(src_ref=s, dst_ref=d, send_sem=send_sems.at[slot, j], recv_sem=recv_sems.at[slot, j],
                                                 device_id=to, device_id_type=MESH) for s, d, j in out]

        def start(cps):
            for cp in cps:
                cp.start()
            return cps

        def arrived(slot, block):
            for cp in copies(slot, block, me):
                cp.wait_recv()

        local = [pltpu.make_async_copy(s, d, local_sems.at[j]) for j, (s, d) in enumerate(
            [(win_ref.at[pl.ds(s0, n)], w_out_ref.at[pl.ds(d0, n)]) for s0, n, d0 in _shard_pieces(k)]
            + [(src, gathered.at[k]) for src, gathered in small] + [(z_ref, w_out_ref.at[pl.ds(W_IN, n_zero)])])]
        for cp in local:
            cp.start()
        sent = start(copies(0, me, sib, own=True)) + start(copies(1, me, xn, own=True)) + start(copies(2, me, yn, own=True))
        arrived(1, xn)
        sent += start(copies(4, xn, sib))
        if c == 1:
            sent += start(copies(3, xn, yn))
        arrived(2, yn)
        sent += start(copies(5, yn, sib))
        if c == 0:
            sent += start(copies(3, yn, xn))
        arrived(3, dg)
        sent += start(copies(6, dg, sib))
        arrived(0, sib)
        arrived(4, (1 - x, y, 1 - c))
        arrived(5, (x, 1 - y, 1 - c))
        arrived(6, (1 - x, 1 - y, 1 - c))
        for cp in sent:
            cp.wait_send()
        for cp in local:
            cp.wait()

    def body(*refs):
        x, y, c = _place()
        me = 4 * x + 2 * y + c
        for k in range(N_DEV):
            pl.when(me == k)(functools.partial(run, k, *refs))

    anyspec = pl.BlockSpec(memory_space=pl.ANY)
    n_arr = 2 + len(small_in)
    return pl.pallas_call(
        body, name="gather_weights", in_specs=[anyspec] * 6, out_specs=[anyspec] * 5,
        out_shape=[jax.ShapeDtypeStruct((W_ROWS, D), win.dtype)]
        + [jax.ShapeDtypeStruct((N_DEV,) + a.shape, a.dtype) for a in small_in],
        scratch_shapes=[pltpu.SemaphoreType.DMA((7, n_arr)), pltpu.SemaphoreType.DMA((7, n_arr)),
                        pltpu.SemaphoreType.DMA((n_arr + 1,))],
    )(win, wout, cw, head, tail, zeros)


_REL = [(dx, dy, dc) for dx in (0, 1) for dy in (0, 1) for dc in (0, 1)][1:]
_HBM = pl.BlockSpec(memory_space=pltpu.HBM)
_SEM = pl.BlockSpec(memory_space=pltpu.SEMAPHORE)
_EFFECT = pltpu.SideEffectType.DATAFLOW_SIDE_EFFECTING


def _peer(k):
    x, y, c = _place()
    dx, dy, dc = _REL[k]
    return (1 - x if dx else x, 1 - y if dy else y, 1 - c if dc else c)


def _exchange_start(parts, *, name):
    n = len(parts)

    def body(*refs):
        ins, lands = refs[:n], refs[n:2 * n]
        send_sems, recv_sems, token = refs[2 * n], refs[2 * n + 1], refs[-1]
        x, y, c = _place()
        me = 4 * x + 2 * y + c
        for a in range(n):
            for k in range(len(_REL)):
                px, py, pc = _peer(k)
                pltpu.make_async_remote_copy(
                    src_ref=ins[a].at[4 * px + 2 * py + pc], dst_ref=lands[a].at[me],
                    send_sem=send_sems.at[len(_REL) * a + k], recv_sem=recv_sems.at[len(_REL) * a + k],
                    device_id=(px, py, pc), device_id_type=MESH).start()
        token[...] = jnp.zeros_like(token)

    sem = pltpu.SemaphoreType.DMA((len(_REL) * n,))
    bufs = [pltpu.HBM(p.shape, p.dtype) for p in parts]
    outs = pl.pallas_call(
        body, name=name,
        out_shape=(sem, sem, *bufs, *bufs, jax.ShapeDtypeStruct((8, LANE), F32)),
        in_specs=(_HBM,) * (2 * n), out_specs=(_SEM, _SEM, *(_HBM,) * (2 * n), pl.BlockSpec(memory_space=pltpu.VMEM)),
        input_output_aliases={i: 2 + i for i in range(2 * n)},
        compiler_params=pltpu.CompilerParams(has_side_effects=_EFFECT),
    )(*[pltpu.with_memory_space_constraint(p, pltpu.HBM) for p in parts],
      *[pltpu.with_memory_space_constraint(lax.empty(p.shape, p.dtype), pltpu.HBM) for p in parts])
    return outs[0], outs[1], outs[2:2 + n], outs[2 + n:2 + 2 * n], outs[-1]


def _exchange_wait(send_sems, recv_sems, parts, lands, after, *, name):
    n = len(parts)

    def body(*refs):
        ins, lands_ = refs[:n], refs[n:2 * n]
        ssem, rsem = refs[2 * n], refs[2 * n + 1]
        for a in range(n):
            for k in range(len(_REL)):
                px, py, pc = _peer(k)
                p = 4 * px + 2 * py + pc
                cp = pltpu.make_async_remote_copy(
                    src_ref=ins[a].at[p], dst_ref=lands_[a].at[p],
                    send_sem=ssem.at[len(_REL) * a + k], recv_sem=rsem.at[len(_REL) * a + k],
                    device_id=(px, py, pc), device_id_type=MESH)
                cp.wait_send()
                cp.wait_recv()

    bufs = [pltpu.HBM(p.shape, p.dtype) for p in parts]
    outs = pl.pallas_call(
        body, name=name, out_shape=(*bufs, *bufs),
        in_specs=(*(_HBM,) * (2 * n), _SEM, _SEM, pl.BlockSpec(memory_space=pl.ANY)), out_specs=(_HBM,) * (2 * n),
        input_output_aliases={i: i for i in range(2 * n)},
        compiler_params=pltpu.CompilerParams(has_side_effects=_EFFECT),
    )(*parts, *lands, send_sems, recv_sems, after)
    return outs[:n], outs[n:]


WEIGHTS = ('norm_w', 'w_in', 'gate_b', 'sgu_norm_g', 'sgu_norm_b', 'sgu_w', 'sgu_b', 'conv_w', 'conv_b', 'dt_bias', 'A_log',
           'D_skip', 'ssd_norm_w', 'w_out', 'final_norm_w')
SHARDED = ('w_in', 'conv_w', 'w_out')
PACK_ROW = 8 * LANE


def _constants():
    tri = np.tril(np.ones((CHUNK, CHUNK), np.float32))
    expand = np.zeros((DT_W, D), np.float32)
    for h in range(HEADS):
        expand[h, h * HEADDIM:(h + 1) * HEADDIM] = 1.0
    sel = np.zeros((D, LANE), np.float32)
    for g in range(SGU_GROUPS):
        sel[g * LANE:(g + 1) * LANE, g] = 1.0
    pos_chunk = np.arange(SGU_BLOCK) // CHUNK
    mask = (pos_chunk[None, :] <= pos_chunk[:, None]).astype(np.float32)
    shift = np.zeros(((CONV_K - 1) * CHUNK, HALO_BLK + CHUNK), np.float32)
    for kk in range(CONV_K - 1):
        for t in range(CHUNK):
            shift[kk * CHUNK + t, HALO_BLK - (CONV_K - 1) + t + kk] = 1.0
    return dict(tri=jnp.asarray(tri, BF16), triT=jnp.asarray(tri.T.copy(), BF16), expand=jnp.asarray(expand, BF16),
                shift=jnp.asarray(shift, BF16),
                expandT=jnp.asarray(expand.T.copy(), BF16), sel=jnp.asarray(sel), mask=jnp.asarray(mask))


def _to_shards(segs):
    starts = np.cumsum([0] + [s.shape[0] for s in segs])
    assert starts[-1] == W_IN
    slabs = []
    for k in range(N_DEV):
        pieces = []
        for s, s0 in zip(segs, starts[:-1]):
            lo, hi = max(k * SHARD_IN, s0), min((k + 1) * SHARD_IN, s0 + s.shape[0])
            if lo < hi:
                pieces.append(s[lo - s0:hi - s0])
        slabs.append(jnp.concatenate(pieces, axis=0))
    return jnp.stack(slabs)


def _local_step(x2, tgt, wpT, wout, cw, p, exchange):
    S = x2.shape[0]
    k = _constants()
    xn = _norm_fwd(x2, p['norm_w'], tm=min(512, S))
    proj = _matmul(xn, wpT, trans_b=True, n=WP, out_dtype=BF16, tm=min(1024, S), tn=1408, tk=D, name="in_proj")
    wm32 = p['sgu_w'][0] * k['mask']
    wm = wm32.astype(BF16)
    wmT = jnp.swapaxes(wm32, 1, 2).astype(BF16)
    bias_full = jnp.repeat(p['sgu_b'][0].T, LANE, axis=1)
    tm_sgu = min(256, S)
    ya = _sgu_fwd(proj, p['sgu_norm_g'], p['sgu_norm_b'], wm, bias_full, tm=tm_sgu)
    pad32 = lambda a: jnp.pad(a, ((0, 0), (0, DT_W - HEADS)))
    dtb_p, alog_p = pad32(p['dt_bias']), pad32(p['A_log'])
    d_exp = jnp.repeat(p['D_skip'], HEADDIM, axis=1)
    ssd_args = (cw, p['conv_b'], dtb_p, alog_p, d_exp, p['ssd_norm_w'])
    y, yb, states = _ssd_fwd(proj, *ssd_args, k['tri'], k['expand'], k['shift'])
    dh, dhb, mb, dya, dyb, dgl, loss, dfw, dgb = _head(
        x2, ya, yb, proj, tgt, p['gate_b'], wout, p['final_norm_w'][None, :], tm=min(256, S))
    dsgu, dws, dbsT, dsg, dsb = _sgu_bwd(proj, dya, p['sgu_norm_g'], p['sgu_norm_b'], wm, wmT, bias_full, k['mask'], k['sel'],
                                         tm=tm_sgu)
    dssd, dcw, dcb, ddtb, dalog, dD, dnw = _ssd_bwd(proj, dyb, y, states, *ssd_args, k['tri'], k['triT'], k['expand'], k['expandT'],
                                                    k['shift'])
    tk = min(4096, S)
    tn = 1024
    dwT_sgu = _matmul(dsgu, xn, trans_a=True, out_dtype=BF16, tm=1024, tn=tn, tk=tk, name="dw_in_sgu")
    dwT_gate = _matmul(dgl, xn, trans_a=True, out_dtype=BF16, tm=1024, tn=tn, tk=tk, name="dw_in_gate")
    dwT_ssd = _matmul(dssd, xn, trans_a=True, out_dtype=BF16, tm=1024, tn=tn, tk=tk, name="dw_in_ssd")
    dw_out = _matmul(mb, dhb, trans_a=True, out_dtype=BF16, tm=1024, tn=tn, tk=tk, name="dw_out")
    token = exchange([dwT_sgu, dwT_ssd[:W_IN - SEG_SSD[0]], dwT_gate], dw_out)
    tm = min(1024, S)
    dxn = _matmul(dsgu, wpT, tm=tm, tn=tn, tk=3072, after=token, name="dxn_sgu")
    dxn = _matmul(dgl, wpT, b_koff=SEG_GATE[0] // 2048, tm=tm, tn=tn, tk=2048, add=dxn, name="dxn_gate")
    dxn = _matmul(dssd, wpT, b_koff=SEG_SSD[0] // 2048, tm=tm, tn=tn, tk=2048, add=dxn, name="dxn_ssd")
    grad_x, dnorm = _norm_bwd(x2, p['norm_w'], dxn, dh, tm=min(256, S))
    grads = dict(
        norm_w=dnorm[0:1], gate_b=dgb[0:1], sgu_norm_g=dsg[0:1], sgu_norm_b=dsb[0:1], sgu_w=dws[None],
        sgu_b=dbsT[:, :SGU_GROUPS].T[None], conv_w=dcw[0:CONV_K][None], conv_b=dcb[0:1], dt_bias=ddtb[0:1, :HEADS],
        A_log=dalog[0:1, :HEADS], D_skip=dD[0:1, :HEADS], ssd_norm_w=dnw[0:1], final_norm_w=dfw[0])
    return loss[0, 0], grad_x, grads


def _pack(arrs):
    rows, offs, r = [], [], 0
    for a in arrs:
        n = a.size
        nr = -(-n // PACK_ROW) * 8
        rows.append(jnp.pad(a.reshape(-1).astype(F32), (0, nr * LANE - n)).reshape(nr, LANE))
        offs.append(r)
        r += nr
    return jnp.concatenate(rows, axis=0), offs


def kernel(x, norm_w, w_in, gate_b, sgu_norm_g, sgu_norm_b, sgu_w, sgu_b, conv_w, conv_b, dt_bias, A_log, D_skip, ssd_norm_w, w_out, final_norm_w, loss_target, m_norm_w, m_w_in, m_gate_b, m_sgu_norm_g, m_sgu_norm_b, m_sgu_w, m_sgu_b, m_conv_w, m_conv_b, m_dt_bias, m_A_log, m_D_skip, m_ssd_norm_w, m_w_out, m_final_norm_w, v_norm_w, v_w_in, v_gate_b, v_sgu_norm_g, v_sgu_norm_b, v_sgu_w, v_sgu_b, v_conv_w, v_conv_b, v_dt_bias, v_A_log, v_D_skip, v_ssd_norm_w, v_w_out, v_final_norm_w):
    w = dict(norm_w=norm_w, w_in=w_in, gate_b=gate_b, sgu_norm_g=sgu_norm_g, sgu_norm_b=sgu_norm_b, sgu_w=sgu_w, sgu_b=sgu_b,
             conv_w=conv_w, conv_b=conv_b, dt_bias=dt_bias, A_log=A_log, D_skip=D_skip, ssd_norm_w=ssd_norm_w, w_out=w_out,
             final_norm_w=final_norm_w)
    m = dict(norm_w=m_norm_w, w_in=m_w_in, gate_b=m_gate_b, sgu_norm_g=m_sgu_norm_g, sgu_norm_b=m_sgu_norm_b, sgu_w=m_sgu_w,
             sgu_b=m_sgu_b, conv_w=m_conv_w, conv_b=m_conv_b, dt_bias=m_dt_bias, A_log=m_A_log, D_skip=m_D_skip,
             ssd_norm_w=m_ssd_norm_w, w_out=m_w_out, final_norm_w=m_final_norm_w)
    v = dict(norm_w=v_norm_w, w_in=v_w_in, gate_b=v_gate_b, sgu_norm_g=v_sgu_norm_g, sgu_norm_b=v_sgu_norm_b, sgu_w=v_sgu_w,
             sgu_b=v_sgu_b, conv_w=v_conv_w, conv_b=v_conv_b, dt_bias=v_dt_bias, A_log=v_A_log, D_skip=v_D_skip,
             ssd_norm_w=v_ssd_norm_w, w_out=v_w_out, final_norm_w=v_final_norm_w)
    me = 4 * lax.axis_index("x") + 2 * lax.axis_index("y") + lax.axis_index("c")
    shard_cw = XBC_W // N_DEV

    tpose = lambda a: jnp.swapaxes(a[0], 0, 1)
    wT = tpose(w_in).astype(BF16)
    first_group = (GROUP - (me * SHARD_IN) % GROUP) % GROUP
    window = lax.dynamic_slice(jnp.pad(wT, ((0, GROUP), (0, 0))), (first_group, 0), (INTERIOR, D))
    wpT, g_out, g_cw, heads, tails = _gather_weights(window, wT[:GROUP], wT[SHARD_IN - GROUP:], w_out[0].astype(BF16),
                                                     conv_w[0], jnp.zeros((W_ROWS - W_IN, D), BF16))
    wpT = _patch_straddlers(wpT, heads, tails)
    wout_full = g_out.reshape(D, D)
    cw_full = jnp.swapaxes(g_cw, 0, 1).reshape(CONV_K, XBC_W)

    flight = {}

    def exchange(dw_inT_segs, dw_out):
        parts = [_to_shards(dw_inT_segs), dw_out.reshape(N_DEV, D // N_DEV, D)]
        flight['sems'], flight['rsems'], flight['parts'], flight['lands'], token = _exchange_start(parts, name="exchange_start")
        return token

    loss_part, grad_x, grads = _local_step(x[0], loss_target[0], wpT, wout_full, cw_full, w, exchange)
    (own_in, own_out), (land_in, land_out) = _exchange_wait(
        flight['sems'], flight['rsems'], flight['parts'], flight['lands'], grad_x, name="exchange_wait")
    me_arr = jnp.reshape(me, (1,)).astype(jnp.int32)
    res = {}
    res['w_in'] = [jnp.swapaxes(o, 0, 1) for o in _adamw_own(
        me_arr, own_in, land_in, tpose(w_in), tpose(m_w_in), tpose(v_w_in), tr=SHARD_IN, tc=256, name="adamw_w_in")]
    res['w_out'] = _adamw_own(me_arr, own_out, land_out, w_out[0], m_w_out[0], v_w_out[0], tr=128, tc=D, name="adamw_w_out")

    small = [n for n in WEIGHTS if n not in SHARDED]
    packed, offs = _pack([grads[n] for n in small] + [loss_part, grads['conv_w']])
    (gathered,) = _all_gather([packed], name="gather_small")
    off_loss, off_cw = offs[-2], offs[-1]
    cw_parts = gathered[:, off_cw:, :].reshape(N_DEV, CONV_K, XBC_W)
    cw_parts = lax.dynamic_slice_in_dim(cw_parts, me * shard_cw, shard_cw, axis=2)
    cw_rows = _pack([cw_parts[0]])[0].shape[0]
    cw_parts = jnp.pad(cw_parts.reshape(N_DEV, -1), ((0, 0), (0, cw_rows * LANE - CONV_K * shard_cw))).reshape(N_DEV, cw_rows, LANE)
    parts = jnp.concatenate([gathered[:, :off_cw, :], cw_parts], axis=1)
    zero = jnp.zeros((), F32)
    packs = [_pack([d[n] for n in small] + [zero, d['conv_w']])[0] for d in (w, m, v)]
    outs = _adamw(parts, *packs, tr=parts.shape[1], name="adamw_small")

    def unpack(o, name):
        if name == 'conv_w':
            return o[off_cw:off_cw + cw_rows].reshape(-1)[:CONV_K * shard_cw].reshape(w['conv_w'].shape)
        r0 = offs[small.index(name)]
        n = w[name].size
        return o[r0:r0 + -(-n // PACK_ROW) * 8].reshape(-1)[:n].reshape(w[name].shape)

    for n in small + ['conv_w']:
        res[n] = [unpack(o, n) for o in outs]
    for n in ('w_in', 'w_out'):
        res[n] = [o[None] for o in res[n]]
    loss = outs[0][off_loss, 0]
    return (loss, grad_x[None], *[res[n][0] for n in WEIGHTS], *[res[n][1] for n in WEIGHTS],
            *[res[n][2] for n in WEIGHTS], *[res[n][3] for n in WEIGHTS])
```

```python
import functools

import numpy as np
import jax
import jax.numpy as jnp
from jax import lax
from jax.experimental import pallas as pl
from jax.experimental.pallas import tpu as pltpu

F32 = jnp.float32
BF16 = jnp.bfloat16
HI = lax.Precision.HIGHEST
MESH = pl.DeviceIdType.MESH

D = 2048
EPS = 1e-5
SGU_BLOCK = 128
SGU_GROUPS = 16
CHUNK = 64
HEADS = 32
HEADDIM = 64
SSD_GROUPS = 4
GROUP_W = D // SSD_GROUPS
STATE = 128
CONV_K = 4
XBC_W = D + 2 * SSD_GROUPS * STATE
W_IN = 15392
N_DEV = 8
SHARD_IN = W_IN // N_DEV
ADAM_LR, ADAM_B1, ADAM_B2, ADAM_EPS, ADAM_WD, ADAM_STEP = 0.001, 0.9, 0.999, 1e-08, 0.01, 10

LANE = 128
DT_W = LANE
OFF_U, OFF_V, OFF_ZA, OFF_G0, OFF_G1, OFF_ZB, OFF_XBC, OFF_DT = 0, 2048, 4096, 6144, 8192, 10240, 12288, 15360
WP = OFF_DT + DT_W
SEG_SGU = (0, 6144)
SEG_GATE = (6144, 4096)
SEG_SSD = (10240, WP - 10240)
SSD_PAD_W = 6144
VMEM_LIMIT = 56 * 1024 * 1024


def _cp(sem=None, vmem=VMEM_LIMIT):
    return pltpu.CompilerParams(dimension_semantics=sem, vmem_limit_bytes=vmem)


def _sigmoid(x):
    return 1.0 / (1.0 + jnp.exp(-x))


def _softplus(x):
    return jnp.maximum(x, 0.0) + jnp.log(1.0 + jnp.exp(-jnp.abs(x)))


def _dot(a, b, precision=None):
    return jnp.dot(a, b, preferred_element_type=F32, precision=precision)


def _dot_nt(a, b, precision=None):
    return lax.dot_general(a, b, (((1,), (1,)), ((), ())), preferred_element_type=F32, precision=precision)


def _dot_tn(a, b, precision=None):
    return lax.dot_general(a, b, (((0,), (0,)), ((), ())), preferred_element_type=F32, precision=precision)


def _split3(a):
    hi = a.astype(BF16)
    r = a - hi.astype(F32)
    mid = r.astype(BF16)
    return hi, mid, (r - mid.astype(F32)).astype(BF16)


def _sel_right(a, sel01):
    m = a.shape[0]
    r = _dot(jnp.concatenate(_split3(a), axis=0), sel01)
    return (r[0:m] + r[m:2 * m]) + r[2 * m:3 * m]


def _sel_left(sel01, a):
    n = a.shape[1]
    r = _dot(sel01, jnp.concatenate(_split3(a), axis=1))
    return (r[:, 0:n] + r[:, n:2 * n]) + r[:, 2 * n:3 * n]


def _matmul(a, b, *, trans_a=False, trans_b=False, b_koff=0, n=None, out_dtype=F32, tm, tn, tk, add=None, after=None, name):
    K, M = a.shape if trans_a else a.shape[::-1]
    N = (n or b.shape[0]) if trans_b else b.shape[1]
    assert M % tm == 0 and N % tn == 0 and K % tk == 0 and not (trans_a and trans_b), (name, M, N, K, tm, tn, tk)
    nk = K // tk

    def body(*refs):
        a_ref, b_ref = refs[:2]
        add_ref = refs[2] if add is not None else None
        o_ref, acc_ref = refs[-2:]
        k = pl.program_id(2)
        if trans_a:
            part = _dot_tn(a_ref[...], b_ref[...])
        else:
            part = _dot_nt(a_ref[...], b_ref[...]) if trans_b else _dot(a_ref[...], b_ref[...])

        def result(r):
            if add_ref is not None:
                r = r + add_ref[...]
            return r.astype(out_dtype)

        if nk == 1:
            o_ref[...] = result(part)
        else:
            @pl.when(k == 0)
            def _():
                acc_ref[...] = part

            @pl.when(jnp.logical_and(k > 0, k < nk - 1))
            def _():
                acc_ref[...] += part

            @pl.when(k == nk - 1)
            def _():
                o_ref[...] = result(acc_ref[...] + part)

    in_specs = [pl.BlockSpec((tk, tm), lambda i, j, k: (k, i)) if trans_a else pl.BlockSpec((tm, tk), lambda i, j, k: (i, k)),
                pl.BlockSpec((tn, tk), lambda i, j, k: (j, k)) if trans_b else pl.BlockSpec((tk, tn), lambda i, j, k: (k + b_koff, j))]
    args = [a, b]
    if add is not None:
        in_specs.append(pl.BlockSpec((tm, tn), lambda i, j, k: (i, j)))
        args.append(add)
    if after is not None:
        in_specs.append(pl.BlockSpec(memory_space=pl.ANY))
        args.append(after)
    return pl.pallas_call(
        body, name=name, grid=(M // tm, N // tn, nk), in_specs=in_specs,
        out_specs=pl.BlockSpec((tm, tn), lambda i, j, k: (i, j)),
        out_shape=jax.ShapeDtypeStruct((M, N), out_dtype),
        scratch_shapes=[pltpu.VMEM((tm, tn), F32)],
        compiler_params=_cp(("parallel", "parallel", "arbitrary")),
    )(*args)


def _norm_fwd(x, w, *, tm):
    S = x.shape[0]

    def body(x_ref, w_ref, o_ref):
        xv = x_ref[...]
        r = lax.rsqrt(jnp.mean(xv * xv, axis=-1, keepdims=True) + EPS)
        o_ref[...] = (xv * r * w_ref[...]).astype(BF16)

    return pl.pallas_call(
        body, name="norm_fwd", grid=(S // tm,),
        in_specs=[pl.BlockSpec((tm, D), lambda i: (i, 0)), pl.BlockSpec((1, D), lambda i: (0, 0))],
        out_specs=pl.BlockSpec((tm, D), lambda i: (i, 0)),
        out_shape=jax.ShapeDtypeStruct((S, D), BF16), compiler_params=_cp(("parallel",)),
    )(x, w)


def _norm_bwd(x, w, dxn, dh, *, tm):
    S = x.shape[0]

    def body(x_ref, w_ref, dxn_ref, dh_ref, gx_ref, dw_ref):
        xv = x_ref[...]
        r = lax.rsqrt(jnp.mean(xv * xv, axis=-1, keepdims=True) + EPS)
        xh = xv * r
        dxn_v = dxn_ref[...]
        dxh = dxn_v * w_ref[...]
        gx_ref[...] = dh_ref[...] + r * (dxh - xh * jnp.mean(dxh * xh, axis=-1, keepdims=True))

        @pl.when(pl.program_id(0) == 0)
        def _():
            dw_ref[...] = jnp.zeros_like(dw_ref)

        dw_ref[0:1, :] += jnp.sum(dxn_v * xh, axis=0, keepdims=True)

    row = pl.BlockSpec((tm, D), lambda i: (i, 0))
    return pl.pallas_call(
        body, name="norm_bwd", grid=(S // tm,),
        in_specs=[row, pl.BlockSpec((1, D), lambda i: (0, 0)), row, row],
        out_specs=[row, pl.BlockSpec((8, D), lambda i: (0, 0))],
        out_shape=[jax.ShapeDtypeStruct((S, D), F32), jax.ShapeDtypeStruct((8, D), F32)],
        compiler_params=_cp(("arbitrary",)),
    )(x, w, dxn, dh)


def _sgu_core(u_ref, v_ref, z_ref, g_ref, b_ref, wm_ref, bias_ref, vnb_ref, mixed_ref, tm):
    v = v_ref[...].astype(F32)
    mu = jnp.mean(v, axis=-1, keepdims=True)
    vc = v - mu
    rs = lax.rsqrt(jnp.mean(vc * vc, axis=-1, keepdims=True) + EPS)
    vh = vc * rs
    vnb_ref[...] = (vh * g_ref[...] + b_ref[...]).astype(BF16)
    for blk in range(tm // SGU_BLOCK):
        rows = pl.ds(blk * SGU_BLOCK, SGU_BLOCK)
        for gi in range(SGU_GROUPS):
            cols = pl.ds(gi * LANE, LANE)
            mixed_ref[rows, cols] = _dot(wm_ref[gi], vnb_ref[rows, cols]) + bias_ref[:, cols]
    return vh, rs


def _sgu_fwd(proj, g, b, wm, bias_full, *, tm):
    S = proj.shape[0]

    def body(u_ref, v_ref, z_ref, g_ref, b_ref, wm_ref, bias_ref, y_ref, vnb_ref, mixed_ref):
        _sgu_core(u_ref, v_ref, z_ref, g_ref, b_ref, wm_ref, bias_ref, vnb_ref, mixed_ref, tm)
        z = z_ref[...].astype(F32)
        y_ref[...] = (u_ref[...].astype(F32) * mixed_ref[...] * (z * _sigmoid(z))).astype(BF16)

    seg = lambda off: pl.BlockSpec((tm, D), lambda i: (i, off // D))
    full = lambda a: pl.BlockSpec(a.shape, lambda i: (0,) * a.ndim)
    return pl.pallas_call(
        body, name="sgu_fwd", grid=(S // tm,),
        in_specs=[seg(OFF_U), seg(OFF_V), seg(OFF_ZA), full(g), full(b), full(wm), full(bias_full)],
        out_specs=pl.BlockSpec((tm, D), lambda i: (i, 0)),
        out_shape=jax.ShapeDtypeStruct((S, D), BF16),
        scratch_shapes=[pltpu.VMEM((tm, D), BF16), pltpu.VMEM((tm, D), F32)],
        compiler_params=_cp(("parallel",)),
    )(proj, proj, proj, g, b, wm, bias_full)


def _sgu_bwd(proj, dy, g, b, wm, wmT, bias_full, mask, sel, *, tm):
    S = proj.shape[0]
    nsteps = S // tm

    def body(u_ref, v_ref, z_ref, dy_ref, g_ref, b_ref, wm_ref, wmT_ref, bias_ref, mask_ref, sel_ref,
             dp_ref, dws_ref, dbs_ref, dg_ref, db_ref, vnb_ref, mixed_ref, dmb_ref, dvn_ref, dbias_ref):
        i = pl.program_id(0)

        @pl.when(i == 0)
        def _():
            dws_ref[...] = jnp.zeros_like(dws_ref)
            dg_ref[...] = jnp.zeros_like(dg_ref)
            db_ref[...] = jnp.zeros_like(db_ref)
            dbias_ref[...] = jnp.zeros_like(dbias_ref)

        vh, rs = _sgu_core(u_ref, v_ref, z_ref, g_ref, b_ref, wm_ref, bias_ref, vnb_ref, mixed_ref, tm)
        u = u_ref[...].astype(F32)
        z = z_ref[...].astype(F32)
        dy_v = dy_ref[...].astype(F32)
        mixed = mixed_ref[...]
        sg = _sigmoid(z)
        sz = z * sg
        dp_ref[:, 0:D] = (dy_v * mixed * sz).astype(BF16)
        dp_ref[:, 2 * D:3 * D] = (dy_v * u * mixed * (sg * (1.0 + z * (1.0 - sg)))).astype(BF16)
        dmixed = dy_v * u * sz
        dmb_ref[...] = dmixed.astype(BF16)
        for blk in range(tm // SGU_BLOCK):
            dbias_ref[...] += dmixed[blk * SGU_BLOCK:(blk + 1) * SGU_BLOCK, :]
        for blk in range(tm // SGU_BLOCK):
            rows = pl.ds(blk * SGU_BLOCK, SGU_BLOCK)
            for gi in range(SGU_GROUPS):
                cols = pl.ds(gi * LANE, LANE)
                dm = dmb_ref[rows, cols]
                dvn_ref[rows, cols] = _dot(wmT_ref[gi], dm)
                dws_ref[gi] += _dot_nt(dm, vnb_ref[rows, cols])
        dvn = dvn_ref[...]
        dg_ref[0:1, :] += jnp.sum(dvn * vh, axis=0, keepdims=True)
        db_ref[0:1, :] += jnp.sum(dvn, axis=0, keepdims=True)
        dvh = dvn * g_ref[...]
        dv = rs * (dvh - jnp.mean(dvh, axis=-1, keepdims=True) - vh * jnp.mean(dvh * vh, axis=-1, keepdims=True))
        dp_ref[:, D:2 * D] = dv.astype(BF16)

        @pl.when(i == nsteps - 1)
        def _():
            for gi in range(SGU_GROUPS):
                dws_ref[gi] = dws_ref[gi] * mask_ref[...]
            dbs_ref[...] = _dot(dbias_ref[...], sel_ref[...], precision=HI)

    seg = lambda off: pl.BlockSpec((tm, D), lambda i: (i, off // D))
    full = lambda a: pl.BlockSpec(a.shape, lambda i: (0,) * a.ndim)
    return pl.pallas_call(
        body, name="sgu_bwd", grid=(nsteps,),
        in_specs=[seg(OFF_U), seg(OFF_V), seg(OFF_ZA), pl.BlockSpec((tm, D), lambda i: (i, 0)),
                  full(g), full(b), full(wm), full(wmT), full(bias_full), full(mask), full(sel)],
        out_specs=[pl.BlockSpec((tm, 3 * D), lambda i: (i, 0)),
                   pl.BlockSpec((SGU_GROUPS, SGU_BLOCK, SGU_BLOCK), lambda i: (0, 0, 0)),
                   pl.BlockSpec((SGU_BLOCK, LANE), lambda i: (0, 0)),
                   pl.BlockSpec((8, D), lambda i: (0, 0)), pl.BlockSpec((8, D), lambda i: (0, 0))],
        out_shape=[jax.ShapeDtypeStruct((S, 3 * D), BF16),
                   jax.ShapeDtypeStruct((SGU_GROUPS, SGU_BLOCK, SGU_BLOCK), F32),
                   jax.ShapeDtypeStruct((SGU_BLOCK, LANE), F32),
                   jax.ShapeDtypeStruct((8, D), F32), jax.ShapeDtypeStruct((8, D), F32)],
        scratch_shapes=[pltpu.VMEM((tm, D), BF16), pltpu.VMEM((tm, D), F32), pltpu.VMEM((tm, D), BF16),
                        pltpu.VMEM((tm, D), F32), pltpu.VMEM((SGU_BLOCK, D), F32)],
        compiler_params=_cp(("arbitrary",)),
    )(proj, proj, proj, dy, g, b, wm, wmT, bias_full, mask, sel)


SSD_T = 2 * CHUNK
HALO = 8
HALO_BLK = 16


def _pair_masks():
    row = lax.broadcasted_iota(jnp.int32, (CHUNK, LANE), 0)
    lane = lax.broadcasted_iota(jnp.int32, (CHUNK, LANE), 1)
    pos = jnp.where(lane >= CHUNK, lane - CHUNK, lane)
    diag = (row == pos).astype(F32)
    causal = row >= pos
    lo = (lane < CHUNK).astype(F32)
    return diag, causal, lo, 1.0 - lo


def _ssd_chunk_fwd(c, ext_ref, shift_ref, dt_ref, cw_ref, cb_ref, dtb_ref, alog_ref, tri_ref, exp_ref):
    r0 = c * CHUNK
    win = ext_ref[pl.ds(r0, HALO_BLK + CHUNK), :]
    sh = _dot(shift_ref[...], win)
    taps = [sh[k * CHUNK:(k + 1) * CHUNK] for k in range(CONV_K - 1)] + [win[HALO_BLK:].astype(F32)]
    pre = cb_ref[...] + sum(cw_ref[k:k + 1, :] * taps[k] for k in range(CONV_K))
    sg = _sigmoid(pre)
    xc = pre * sg
    dtr = dt_ref[pl.ds(r0, CHUNK), :].astype(F32) + dtb_ref[...]
    dtv = _softplus(dtr)
    A = -jnp.exp(alog_ref[...])
    acs = _sel_left(tri_ref[...], dtv * A)
    both = _sel_right(jnp.concatenate([acs, dtv], axis=0), exp_ref[...])
    E, dtE = both[0:CHUNK], both[CHUNK:2 * CHUNK]
    return dict(taps=taps, pre=pre, sg=sg, xc=xc, dtr=dtr, dtv=dtv, A=A, E=E, dtE=dtE)


def _ssd_fwd(proj, conv_w, conv_b, dtb_p, alog_p, d_exp, norm_w, tri, expand, shift):
    S = proj.shape[0]
    T = SSD_T
    nsteps = S // T
    ncl = T // CHUNK

    def body(zb_ref, xbc_ref, halo_ref, dt_ref, cw_ref, cb_ref, dtb_ref, alog_ref, dexp_ref, nw_ref, tri_ref, exp_ref, shift_ref,
             y_ref, yb_ref, st_ref, ht_ref, ext_ref):
        i = pl.program_id(0)

        @pl.when(i == 0)
        def _():
            ht_ref[...] = jnp.zeros_like(ht_ref)
            ext_ref[0:HALO_BLK, :] = jnp.zeros((HALO_BLK, XBC_W), BF16)

        @pl.when(i > 0)
        def _():
            ext_ref[0:HALO_BLK, :] = halo_ref[...]

        ext_ref[HALO_BLK:HALO_BLK + T, :] = xbc_ref[...]
        diag, causal, lo, hi = _pair_masks()
        for c in range(ncl):
            q = _ssd_chunk_fwd(c, ext_ref, shift_ref, dt_ref, cw_ref, cb_ref, dtb_ref, alog_ref, tri_ref, exp_ref)
            rows = pl.ds(c * CHUNK, CHUNK)
            xc, E, dtE = q["xc"], q["E"], q["dtE"]
            xs = xc[:, 0:D]
            total = E[CHUNK - 1:CHUNK, :]
            x_dt = xs * dtE
            eE = jnp.exp(E)
            xw = x_dt * jnp.exp(total - E)
            st_ref[c] = ht_ref[...]
            for g in range(SSD_GROUPS):
                gc = slice(g * GROUP_W, (g + 1) * GROUP_W)
                Bg = xc[:, D + g * STATE:D + (g + 1) * STATE].astype(BF16)
                Cg = xc[:, D + SSD_GROUPS * STATE + g * STATE:D + SSD_GROUPS * STATE + (g + 1) * STATE].astype(BF16)
                cb2 = _dot_nt(Cg, jnp.concatenate([Bg, Bg], axis=0))
                htg = ht_ref[:, gc]
                y_ref[rows, gc] = eE[:, gc] * _dot(Cg, htg.astype(BF16)) + xs[:, gc] * dexp_ref[:, gc]
                for jj in range(GROUP_W // LANE):
                    pc = slice(g * GROUP_W + jj * LANE, g * GROUP_W + (jj + 1) * LANE)
                    Ej = E[:, pc]
                    e2 = jnp.sum(Ej * diag, axis=0, keepdims=True)
                    Mp = cb2 * jnp.exp(jnp.where(causal, Ej - e2, -1e30))
                    xj = x_dt[:, pc]
                    xbd = jnp.concatenate([xj * lo, xj * hi], axis=0).astype(BF16)
                    y_ref[rows, pc] += _dot(Mp.astype(BF16), xbd)
                ht_ref[:, gc] = jnp.exp(total[:, gc]) * htg + _dot_tn(Bg, xw[:, gc].astype(BF16))
            zb = zb_ref[rows, :].astype(F32)
            hh = y_ref[rows, :] * (zb * _sigmoid(zb))
            for g in range(SSD_GROUPS):
                gc = slice(g * GROUP_W, (g + 1) * GROUP_W)
                hg = hh[:, gc]
                r = lax.rsqrt(jnp.mean(hg * hg, axis=-1, keepdims=True) + EPS)
                yb_ref[rows, gc] = (hg * r * nw_ref[:, gc]).astype(BF16)

    full = lambda a: pl.BlockSpec(a.shape, lambda i: (0,) * a.ndim)
    hb = T // HALO_BLK
    return pl.pallas_call(
        body, name="ssd_fwd", grid=(nsteps,),
        in_specs=[pl.BlockSpec((T, D), lambda i: (i, OFF_ZB // D)),
                  pl.BlockSpec((T, XBC_W), lambda i: (i, OFF_XBC // XBC_W)),
                  pl.BlockSpec((HALO_BLK, XBC_W), lambda i: (jnp.maximum(i * hb - 1, 0), OFF_XBC // XBC_W)),
                  pl.BlockSpec((T, DT_W), lambda i: (i, OFF_DT // DT_W)),
                  full(conv_w), full(conv_b), full(dtb_p), full(alog_p), full(d_exp), full(norm_w), full(tri), full(expand),
                  full(shift)],
        out_specs=[pl.BlockSpec((T, D), lambda i: (i, 0)), pl.BlockSpec((T, D), lambda i: (i, 0)),
                   pl.BlockSpec((ncl, STATE, D), lambda i: (i, 0, 0))],
        out_shape=[jax.ShapeDtypeStruct((S, D), F32), jax.ShapeDtypeStruct((S, D), BF16),
                   jax.ShapeDtypeStruct((S // CHUNK, STATE, D), F32)],
        scratch_shapes=[pltpu.VMEM((STATE, D), F32), pltpu.VMEM((HALO_BLK + T, XBC_W), BF16)],
        compiler_params=_cp(("arbitrary",)),
    )(proj, proj, proj, proj, conv_w, conv_b, dtb_p, alog_p, d_exp, norm_w, tri, expand, shift)


def _ssd_bwd(proj, dyb, y, states, conv_w, conv_b, dtb_p, alog_p, d_exp, norm_w, tri, triT, expand, expandT, shift):
    S = proj.shape[0]
    T = SSD_T
    nsteps = S // T
    ncl = T // CHUNK
    SSD_W = SSD_PAD_W

    def body(zb_ref, xbc_ref, halo_ref, dt_ref, dyb_ref, y_ref, st_ref, cw_ref, cb_ref, dtb_ref, alog_ref, dexp_ref, nw_ref,
             tri_ref, triT_ref, exp_ref, expT_ref, shift_ref,
             dp_ref, dcw_ref, dcb_ref, ddtb_ref, dalog_ref, dD_ref, dnw_ref,
             dht_ref, ext_ref, dpre_ref, dy_s, dE_s, dxdt_s, dxc_s, dDacc_ref, dAacc_ref):
        i = pl.program_id(0)

        @pl.when(i == 0)
        def _():
            for r in (dht_ref, dcw_ref, dcb_ref, ddtb_ref, dnw_ref, dDacc_ref, dAacc_ref):
                r[...] = jnp.zeros_like(r)
            dpre_ref[T:T + HALO, :] = jnp.zeros((HALO, XBC_W), F32)

        @pl.when(i == nsteps - 1)
        def _():
            ext_ref[0:HALO_BLK, :] = jnp.zeros((HALO_BLK, XBC_W), BF16)

        @pl.when(i < nsteps - 1)
        def _():
            ext_ref[0:HALO_BLK, :] = halo_ref[...]

        ext_ref[HALO_BLK:HALO_BLK + T, :] = xbc_ref[...]
        diag, causal, lo, hi = _pair_masks()
        last_row = (lax.broadcasted_iota(jnp.int32, (CHUNK, 1), 0) == CHUNK - 1).astype(F32)
        for c in reversed(range(ncl)):
            q = _ssd_chunk_fwd(c, ext_ref, shift_ref, dt_ref, cw_ref, cb_ref, dtb_ref, alog_ref, tri_ref, exp_ref)
            rows = pl.ds(c * CHUNK, CHUNK)
            pre, sg, xc, dtr, dtv, A, E, dtE = (q[k] for k in ("pre", "sg", "xc", "dtr", "dtv", "A", "E", "dtE"))
            xs = xc[:, 0:D]
            total = E[CHUNK - 1:CHUNK, :]
            x_dt = xs * dtE
            eE = jnp.exp(E)
            wdec = jnp.exp(total - E)
            zb = zb_ref[rows, :].astype(F32)
            yv = y_ref[rows, :]
            sgz = _sigmoid(zb)
            sz = zb * sgz
            hh = yv * sz
            for g in range(SSD_GROUPS):
                gc = slice(g * GROUP_W, (g + 1) * GROUP_W)
                hg = hh[:, gc]
                r = lax.rsqrt(jnp.mean(hg * hg, axis=-1, keepdims=True) + EPS)
                dyb_g = dyb_ref[rows, gc].astype(F32)
                dn = dyb_g * nw_ref[:, gc]
                dnw_ref[0:1, gc] += jnp.sum(dyb_g * hg * r, axis=0, keepdims=True)
                dy_s[:, gc] = r * dn - hg * (r * r * r) * jnp.mean(dn * hg, axis=-1, keepdims=True)
            dhh = dy_s[...]
            dp_ref[rows, 0:D] = (dhh * yv * (sgz * (1.0 + zb * (1.0 - sgz)))).astype(BF16)
            dy = dhh * sz
            dy_s[...] = dy
            dDacc_ref[0:1, :] += jnp.sum(dy * xs, axis=0, keepdims=True)
            dxc_s[:, 0:D] = dy * dexp_ref[...]
            for g in range(SSD_GROUPS):
                gc = slice(g * GROUP_W, (g + 1) * GROUP_W)
                bcol = slice(D + g * STATE, D + (g + 1) * STATE)
                ccol = slice(D + SSD_GROUPS * STATE + g * STATE, D + SSD_GROUPS * STATE + (g + 1) * STATE)
                Bg = xc[:, bcol].astype(BF16)
                Cg = xc[:, ccol].astype(BF16)
                B2 = jnp.concatenate([Bg, Bg], axis=0)
                cb2 = _dot_nt(Cg, B2)
                htg = st_ref[c, :, gc]
                htb = htg.astype(BF16)
                dhn = dht_ref[:, gc]
                dhnb = dhn.astype(BF16)
                dyg = dy[:, gc]
                eEg = eE[:, gc]
                wg = wdec[:, gc]
                xdg = x_dt[:, gc]
                CH = _dot(Cg, htb)
                dCHb = (dyg * eEg).astype(BF16)
                dC = _dot_nt(dCHb, htb)
                dl = jnp.exp(total[:, gc])
                dht_prev = _dot_tn(Cg, dCHb) + dl * dhn
                dtot = jnp.sum(dhn * htg, axis=0, keepdims=True) * dl
                dxw = _dot(Bg, dhnb)
                dB = _dot_nt((xdg * wg).astype(BF16), dhnb)
                dwd = dxw * xdg * wg
                dtot = dtot + jnp.sum(dwd, axis=0, keepdims=True)
                dE_s[:, gc] = dyg * eEg * CH - dwd + last_row * dtot
                dxdt_s[:, gc] = dxw * wg
                dcb2 = jnp.zeros((CHUNK, LANE), F32)
                for jj in range(GROUP_W // LANE):
                    pc = slice(g * GROUP_W + jj * LANE, g * GROUP_W + (jj + 1) * LANE)
                    Ej = E[:, pc]
                    e2 = jnp.sum(Ej * diag, axis=0, keepdims=True)
                    Lp = jnp.exp(jnp.where(causal, Ej - e2, -1e30))
                    Mp = cb2 * Lp
                    xj = x_dt[:, pc]
                    xbd = jnp.concatenate([xj * lo, xj * hi], axis=0).astype(BF16)
                    dyj = dy[:, pc].astype(BF16)
                    dMp = _dot_nt(dyj, xbd)
                    dxbd = _dot_tn(Mp.astype(BF16), dyj)
                    dxdt_s[:, pc] += dxbd[0:CHUNK, :] * lo + dxbd[CHUNK:2 * CHUNK, :] * hi
                    dcb2 = dcb2 + dMp * Lp
                    dseg = dMp * Mp
                    dE_s[:, pc] += dseg - diag * jnp.sum(dseg, axis=0, keepdims=True)
                dcb2b = dcb2.astype(BF16)
                dC = dC + _dot(dcb2b, B2)
                dB2 = _dot_tn(dcb2b, Cg)
                dB = dB + dB2[0:CHUNK, :] + dB2[CHUNK:2 * CHUNK, :]
                dxc_s[:, bcol] = dB
                dxc_s[:, ccol] = dC
                dht_ref[:, gc] = dht_prev
            dx_dt = dxdt_s[...]
            dxc_s[:, 0:D] += dx_dt * dtE
            red = _sel_right(jnp.concatenate([dE_s[...], dx_dt * xs], axis=0), expT_ref[...])
            da = _sel_left(triT_ref[...], red[0:CHUNK, :])
            ddtv = red[CHUNK:2 * CHUNK, :] + da * A
            dAacc_ref[0:1, :] += jnp.sum(da * dtv, axis=0, keepdims=True)
            ddtr = ddtv * _sigmoid(dtr)
            ddtb_ref[0:1, :] += jnp.sum(ddtr, axis=0, keepdims=True)
            dp_ref[rows, D + XBC_W:D + XBC_W + DT_W] = ddtr.astype(BF16)
            dpre = dxc_s[...] * (sg * (1.0 + pre * (1.0 - sg)))
            dpre_ref[rows, :] = dpre
            dcb_ref[0:1, :] += jnp.sum(dpre, axis=0, keepdims=True)
            for k in range(CONV_K):
                dcw_ref[k:k + 1, :] += jnp.sum(dpre * q["taps"][k], axis=0, keepdims=True)
        dxbc = jnp.zeros((T, XBC_W), F32)
        for k in range(CONV_K):
            dxbc = dxbc + cw_ref[k:k + 1, :] * dpre_ref[pl.ds(CONV_K - 1 - k, T), :]
        dp_ref[:, D:D + XBC_W] = dxbc.astype(BF16)
        dp_ref[:, SEG_SSD[1]:SSD_W] = jnp.zeros((T, SSD_W - SEG_SSD[1]), BF16)
        dpre_ref[T:T + HALO, :] = dpre_ref[0:HALO, :]

        @pl.when(i == nsteps - 1)
        def _():
            dalog_ref[...] = dAacc_ref[...] * (-jnp.exp(alog_ref[...]))
            dD_ref[...] = _dot(dDacc_ref[...], expT_ref[...].astype(F32), precision=HI)

    full = lambda a: pl.BlockSpec(a.shape, lambda i: (0,) * a.ndim)
    hb = T // HALO_BLK
    rev = lambda i: nsteps - 1 - i
    acc = lambda w: pl.BlockSpec((8, w), lambda i: (0, 0))
    return pl.pallas_call(
        body, name="ssd_bwd", grid=(nsteps,),
        in_specs=[pl.BlockSpec((T, D), lambda i: (rev(i), OFF_ZB // D)),
                  pl.BlockSpec((T, XBC_W), lambda i: (rev(i), OFF_XBC // XBC_W)),
                  pl.BlockSpec((HALO_BLK, XBC_W), lambda i: (jnp.maximum(rev(i) * hb - 1, 0), OFF_XBC // XBC_W)),
                  pl.BlockSpec((T, DT_W), lambda i: (rev(i), OFF_DT // DT_W)),
                  pl.BlockSpec((T, D), lambda i: (rev(i), 0)), pl.BlockSpec((T, D), lambda i: (rev(i), 0)),
                  pl.BlockSpec((ncl, STATE, D), lambda i: (rev(i), 0, 0)),
                  full(conv_w), full(conv_b), full(dtb_p), full(alog_p), full(d_exp), full(norm_w),
                  full(tri), full(triT), full(expand), full(expandT), full(shift)],
        out_specs=[pl.BlockSpec((T, SSD_W), lambda i: (rev(i), 0)),
                   acc(XBC_W), acc(XBC_W), acc(DT_W), acc(DT_W), acc(DT_W), acc(D)],
        out_shape=[jax.ShapeDtypeStruct((S, SSD_W), BF16),
                   jax.ShapeDtypeStruct((8, XBC_W), F32), jax.ShapeDtypeStruct((8, XBC_W), F32),
                   jax.ShapeDtypeStruct((8, DT_W), F32), jax.ShapeDtypeStruct((8, DT_W), F32),
                   jax.ShapeDtypeStruct((8, DT_W), F32), jax.ShapeDtypeStruct((8, D), F32)],
        scratch_shapes=[pltpu.VMEM((STATE, D), F32), pltpu.VMEM((HALO_BLK + T, XBC_W), BF16), pltpu.VMEM((T + HALO, XBC_W), F32),
                        pltpu.VMEM((CHUNK, D), F32), pltpu.VMEM((CHUNK, D), F32), pltpu.VMEM((CHUNK, D), F32),
                        pltpu.VMEM((CHUNK, XBC_W), F32), pltpu.VMEM((8, D), F32), pltpu.VMEM((8, DT_W), F32)],
        compiler_params=_cp(("arbitrary",)),
    )(proj, proj, proj, proj, dyb, y, states, conv_w, conv_b, dtb_p, alog_p, d_exp, norm_w, tri, triT, expand, expandT, shift)


def _head(x, ya, yb, proj, target, gate_b, wout, fw, *, tm):
    S = x.shape[0]

    def body(x_ref, ya_ref, yb_ref, gl0_ref, gl1_ref, t_ref, gb_ref, w_ref, fw_ref,
             dh_ref, dhb_ref, mb_ref, dya_ref, dyb_ref, dgl_ref, loss_ref, dfw_ref, dgb_ref):
        @pl.when(pl.program_id(0) == 0)
        def _():
            loss_ref[...] = jnp.zeros_like(loss_ref)
            dfw_ref[...] = jnp.zeros_like(dfw_ref)
            dgb_ref[...] = jnp.zeros_like(dgb_ref)

        ya_v = ya_ref[...].astype(F32)
        yb_v = yb_ref[...].astype(F32)
        g0 = _sigmoid(gl0_ref[...].astype(F32) + gb_ref[:, 0:D])
        g1 = _sigmoid(gl1_ref[...].astype(F32) + gb_ref[:, D:2 * D])
        mb = (g0 * ya_v + g1 * yb_v).astype(BF16)
        mb_ref[...] = mb
        h = x_ref[...] + _dot(mb, w_ref[...])
        r = lax.rsqrt(jnp.mean(h * h, axis=-1, keepdims=True) + EPS)
        hn = h * r
        err = hn * fw_ref[...] - t_ref[...]
        loss_ref[...] += 0.5 * jnp.sum(jnp.mean(err * err, axis=-1, keepdims=True))
        dyf = err * (1.0 / D)
        dfw_ref[0:1, :] += jnp.sum(dyf * hn, axis=0, keepdims=True)
        dhn = dyf * fw_ref[...]
        dh = r * (dhn - hn * jnp.mean(dhn * hn, axis=-1, keepdims=True))
        dh_ref[...] = dh
        dhb = dh.astype(BF16)
        dhb_ref[...] = dhb
        dm = _dot_nt(dhb, w_ref[...])
        dya_ref[...] = (dm * g0).astype(BF16)
        dyb_ref[...] = (dm * g1).astype(BF16)
        dgl0 = dm * ya_v * g0 * (1.0 - g0)
        dgl1 = dm * yb_v * g1 * (1.0 - g1)
        dgl_ref[:, 0:D] = dgl0.astype(BF16)
        dgl_ref[:, D:2 * D] = dgl1.astype(BF16)
        dgb_ref[0:1, 0:D] += jnp.sum(dgl0, axis=0, keepdims=True)
        dgb_ref[0:1, D:2 * D] += jnp.sum(dgl1, axis=0, keepdims=True)

    row = pl.BlockSpec((tm, D), lambda i: (i, 0))
    seg = lambda off: pl.BlockSpec((tm, D), lambda i: (i, off // D))
    full = lambda a: pl.BlockSpec(a.shape, lambda i: (0,) * a.ndim)
    acc = lambda w: pl.BlockSpec((8, w), lambda i: (0, 0))
    return pl.pallas_call(
        body, name="head", grid=(S // tm,),
        in_specs=[row, row, row, seg(OFF_G0), seg(OFF_G1), row, full(gate_b), full(wout), full(fw)],
        out_specs=[row, row, row, row, row, pl.BlockSpec((tm, 2 * D), lambda i: (i, 0)), acc(LANE), acc(D), acc(2 * D)],
        out_shape=[jax.ShapeDtypeStruct((S, D), F32), jax.ShapeDtypeStruct((S, D), BF16), jax.ShapeDtypeStruct((S, D), BF16),
                   jax.ShapeDtypeStruct((S, D), BF16), jax.ShapeDtypeStruct((S, D), BF16), jax.ShapeDtypeStruct((S, 2 * D), BF16),
                   jax.ShapeDtypeStruct((8, LANE), F32), jax.ShapeDtypeStruct((8, D), F32), jax.ShapeDtypeStruct((8, 2 * D), F32)],
        compiler_params=_cp(("arbitrary",)),
    )(x, ya, yb, proj, proj, target, gate_b, wout, fw)


def _adam_update(g, w_ref, m_ref, v_ref, g_ref, d_ref, m2_ref, v2_ref):
    m2 = ADAM_B1 * m_ref[...] + (1.0 - ADAM_B1) * g
    v2 = ADAM_B2 * v_ref[...] + (1.0 - ADAM_B2) * (g * g)
    m_hat = m2 / (1.0 - ADAM_B1 ** ADAM_STEP)
    v_hat = v2 / (1.0 - ADAM_B2 ** ADAM_STEP)
    g_ref[...] = g
    d_ref[...] = -ADAM_LR * (m_hat / (jnp.sqrt(v_hat) + ADAM_EPS) + ADAM_WD * w_ref[...])
    m2_ref[...] = m2
    v2_ref[...] = v2


def _adamw_own(me, own, landed, w, m, v, *, tr, tc, name):
    _, R, C = landed.shape
    assert R % tr == 0 and C % tc == 0, (name, R, C, tr, tc)

    def body(me_ref, own_ref, p_ref, w_ref, m_ref, v_ref, g_ref, d_ref, m2_ref, v2_ref):
        mine = own_ref[0].astype(F32)
        g = jnp.where(me_ref[0] == 0, mine, p_ref[0].astype(F32))
        for k in range(1, N_DEV):
            g = g + jnp.where(me_ref[0] == k, mine, p_ref[k].astype(F32))
        _adam_update(g, w_ref, m_ref, v_ref, g_ref, d_ref, m2_ref, v2_ref)

    tile = pl.BlockSpec((tr, tc), lambda i, j, me_ref: (i, j))
    return pl.pallas_call(
        body, name=name,
        grid_spec=pltpu.PrefetchScalarGridSpec(
            num_scalar_prefetch=1, grid=(R // tr, C // tc),
            in_specs=[pl.BlockSpec((1, tr, tc), lambda i, j, me_ref: (me_ref[0], i, j)),
                      pl.BlockSpec((N_DEV, tr, tc), lambda i, j, me_ref: (0, i, j)), tile, tile, tile],
            out_specs=[tile, tile, tile, tile]),
        out_shape=[jax.ShapeDtypeStruct((R, C), F32)] * 4,
        compiler_params=_cp(("parallel", "parallel")),
    )(me, own, landed, w, m, v)


def _adamw(parts, w, m, v, *, tr, name):
    _, R, C = parts.shape
    assert R % tr == 0, (name, R, tr)

    def body(p_ref, w_ref, m_ref, v_ref, g_ref, d_ref, m2_ref, v2_ref):
        g = p_ref[0].astype(F32)
        for k in range(1, N_DEV):
            g = g + p_ref[k].astype(F32)
        _adam_update(g, w_ref, m_ref, v_ref, g_ref, d_ref, m2_ref, v2_ref)

    row = pl.BlockSpec((tr, C), lambda i: (i, 0))
    return pl.pallas_call(
        body, name=name, grid=(R // tr,),
        in_specs=[pl.BlockSpec((N_DEV, tr, C), lambda i: (0, i, 0)), row, row, row],
        out_specs=[row, row, row, row],
        out_shape=[jax.ShapeDtypeStruct((R, C), F32)] * 4,
        compiler_params=_cp(("parallel",)),
    )(parts, w, m, v)


def _place():
    x, y, c = lax.axis_index("x"), lax.axis_index("y"), lax.axis_index("c")
    return x, y, c


def _all_gather(arrs, *, name):
    n = len(arrs)

    def body(*refs):
        ins, outs = refs[:n], refs[n:2 * n]
        send_sems, recv_sems, local_sems = refs[2 * n:]
        x, y, c = _place()
        me, sibling = (x, y, c), (x, y, 1 - c)
        chips = [(1 - x, y), (x, 1 - y), (1 - x, 1 - y)]

        def idx(px, py, pc):
            return 4 * px + 2 * py + pc

        def copy(k, a, block, to, src=None):
            slab = outs[a].at[idx(*block)]
            return pltpu.make_async_remote_copy(
                src_ref=slab if src is None else src, dst_ref=slab,
                send_sem=send_sems.at[k, a], recv_sem=recv_sems.at[k, a], device_id=to, device_id_type=MESH)

        mine = [pltpu.make_async_copy(ins[a], outs[a].at[idx(*me)], local_sems.at[a]) for a in range(n)]
        for cp in mine:
            cp.start()
        first = []
        for a in range(n):
            first.append(copy(0, a, me, sibling, src=ins[a]))
            first += [copy(1 + j, a, me, (*chip, c), src=ins[a]) for j, chip in enumerate(chips)]
        for cp in first:
            cp.start()
        passed = []
        for j, chip in enumerate(chips):
            for a in range(n):
                copy(1 + j, a, (*chip, c), me).wait_recv()
                fwd = copy(4 + j, a, (*chip, c), sibling)
                fwd.start()
                passed.append(fwd)
        for a in range(n):
            copy(0, a, sibling, me).wait_recv()
            for j, chip in enumerate(chips):
                copy(4 + j, a, (*chip, 1 - c), me).wait_recv()
        for cp in first + passed:
            cp.wait_send()
        for cp in mine:
            cp.wait()

    anyspec = pl.BlockSpec(memory_space=pl.ANY)
    return pl.pallas_call(
        body, name=name,
        in_specs=[anyspec] * n, out_specs=[anyspec] * n,
        out_shape=[jax.ShapeDtypeStruct((N_DEV,) + a.shape, a.dtype) for a in arrs],
        scratch_shapes=[pltpu.SemaphoreType.DMA((7, n)), pltpu.SemaphoreType.DMA((7, n)), pltpu.SemaphoreType.DMA((n,))],
    )(*arrs)


W_ROWS = SEG_SSD[0] + SSD_PAD_W


GROUP = 16
INTERIOR = 1920


def _interior(k):
    lo = -(-(k * SHARD_IN) // GROUP) * GROUP
    hi = ((k + 1) * SHARD_IN) // GROUP * GROUP
    return lo, hi


def _dest_row(r):
    return r if r < 6144 else (r - 6144 + SEG_SSD[0] if r < 11296 else r - 11296 + SEG_GATE[0])


def _shard_pieces(k):
    lo_k, hi_k = _interior(k)
    out = []
    for lo, hi in ((0, 6144), (6144, 11296), (11296, W_IN)):
        a, b = max(lo, lo_k), min(hi, hi_k)
        if a < b:
            out.append((a - lo_k, b - a, _dest_row(a)))
    return out


def _patch_straddlers(wpT, heads, tails):
    for k in range(1, N_DEV):
        m = (k * SHARD_IN) % GROUP
        if m:
            group = jnp.concatenate([tails[k - 1, GROUP - m:], heads[k, :GROUP - m]], axis=0)
            wpT = lax.dynamic_update_slice(wpT, group, (_dest_row(k * SHARD_IN - m), 0))
    return wpT


def _gather_weights(win, head, tail, wout, cw, zeros):
    n_zero = zeros.shape[0]
    assert W_IN + n_zero == W_ROWS and W_IN % GROUP == 0
    small_in = (wout, cw, head, tail)

    def run(k, win_ref, wout_ref, cw_ref, head_ref, tail_ref, z_ref, w_out_ref, gout_ref, gcw_ref, ghead_ref, gtail_ref,
            send_sems, recv_sems, local_sems):
        x, y, c = k // 4, (k // 2) % 2, k % 2
        idx = lambda p: 4 * p[0] + 2 * p[1] + p[2]
        me, sib = (x, y, c), (x, y, 1 - c)
        xn, yn, dg = (1 - x, y, c), (x, 1 - y, c), (1 - x, 1 - y, c)
        small = ((wout_ref, gout_ref), (cw_ref, gcw_ref), (head_ref, ghead_ref), (tail_ref, gtail_ref))

        def copies(slot, block, to, own=False):
            kb = idx(block)
            out = []
            for j, (s0, n, d0) in enumerate(_shard_pieces(kb)):
                dst = w_out_ref.at[pl.ds(d0, n)]
                out.append((win_ref.at[pl.ds(s0, n)] if own else dst, dst, j))
            for j, (src, gathered) in enumerate(small):
                out.append((src if own else gathered.at[kb], gathered.at[kb], 2 + j))
            return [pltpu.make_async_remote_copy(src_ref=s, dst_ref=d, send_sem=send_sems.at[slot, j], recv_sem=recv_sems.at[slot, j],
                                                 device_id=to, device_id_type=MESH) for s, d, j in out]

        def start(cps):
            for cp in cps:
                cp.start()
            return cps

        def arrived(slot, block):
            for cp in copies(slot, block, me):
                cp.wait_recv()

        local = [pltpu.make_async_copy(s, d, local_sems.at[j]) for j, (s, d) in enumerate(
            [(win_ref.at[pl.ds(s0, n)], w_out_ref.at[pl.ds(d0, n)]) for s0, n, d0 in _shard_pieces(k)]
            + [(src, gathered.at[k]) for src, gathered in small] + [(z_ref, w_out_ref.at[pl.ds(W_IN, n_zero)])])]
        for cp in local:
            cp.start()
        sent = start(copies(0, me, sib, own=True)) + start(copies(1, me, xn, own=True)) + start(copies(2, me, yn, own=True))
        arrived(1, xn)
        sent += start(copies(4, xn, sib))
        if c == 1:
            sent += start(copies(3, xn, yn))
        arrived(2, yn)
        sent += start(copies(5, yn, sib))
        if c == 0:
            sent += start(copies(3, yn, xn))
        arrived(3, dg)
        sent += start(copies(6, dg, sib))
        arrived(0, sib)
        arrived(4, (1 - x, y, 1 - c))
        arrived(5, (x, 1 - y, 1 - c))
        arrived(6, (1 - x, 1 - y, 1 - c))
        for cp in sent:
            cp.wait_send()
        for cp in local:
            cp.wait()

    def body(*refs):
        x, y, c = _place()
        me = 4 * x + 2 * y + c
        for k in range(N_DEV):
            pl.when(me == k)(functools.partial(run, k, *refs))

    anyspec = pl.BlockSpec(memory_space=pl.ANY)
    n_arr = 2 + len(small_in)
    return pl.pallas_call(
        body, name="gather_weights", in_specs=[anyspec] * 6, out_specs=[anyspec] * 5,
        out_shape=[jax.ShapeDtypeStruct((W_ROWS, D), win.dtype)]
        + [jax.ShapeDtypeStruct((N_DEV,) + a.shape, a.dtype) for a in small_in],
        scratch_shapes=[pltpu.SemaphoreType.DMA((7, n_arr)), pltpu.SemaphoreType.DMA((7, n_arr)),
                        pltpu.SemaphoreType.DMA((n_arr + 1,))],
    )(win, wout, cw, head, tail, zeros)


_REL = [(dx, dy, dc) for dx in (0, 1) for dy in (0, 1) for dc in (0, 1)][1:]
_HBM = pl.BlockSpec(memory_space=pltpu.HBM)
_SEM = pl.BlockSpec(memory_space=pltpu.SEMAPHORE)
_EFFECT = pltpu.SideEffectType.DATAFLOW_SIDE_EFFECTING


def _peer(k):
    x, y, c = _place()
    dx, dy, dc = _REL[k]
    return (1 - x if dx else x, 1 - y if dy else y, 1 - c if dc else c)


def _exchange_start(parts, *, name):
    n = len(parts)

    def body(*refs):
        ins, lands = refs[:n], refs[n:2 * n]
        send_sems, recv_sems, token = refs[2 * n], refs[2 * n + 1], refs[-1]
        x, y, c = _place()
        me = 4 * x + 2 * y + c
        for a in range(n):
            for k in range(len(_REL)):
                px, py, pc = _peer(k)
                pltpu.make_async_remote_copy(
                    src_ref=ins[a].at[4 * px + 2 * py + pc], dst_ref=lands[a].at[me],
                    send_sem=send_sems.at[len(_REL) * a + k], recv_sem=recv_sems.at[len(_REL) * a + k],
                    device_id=(px, py, pc), device_id_type=MESH).start()
        token[...] = jnp.zeros_like(token)

    sem = pltpu.SemaphoreType.DMA((len(_REL) * n,))
    bufs = [pltpu.HBM(p.shape, p.dtype) for p in parts]
    outs = pl.pallas_call(
        body, name=name,
        out_shape=(sem, sem, *bufs, *bufs, jax.ShapeDtypeStruct((8, LANE), F32)),
        in_specs=(_HBM,) * (2 * n), out_specs=(_SEM, _SEM, *(_HBM,) * (2 * n), pl.BlockSpec(memory_space=pltpu.VMEM)),
        input_output_aliases={i: 2 + i for i in range(2 * n)},
        compiler_params=pltpu.CompilerParams(has_side_effects=_EFFECT),
    )(*[pltpu.with_memory_space_constraint(p, pltpu.HBM) for p in parts],
      *[pltpu.with_memory_space_constraint(lax.empty(p.shape, p.dtype), pltpu.HBM) for p in parts])
    return outs[0], outs[1], outs[2:2 + n], outs[2 + n:2 + 2 * n], outs[-1]


def _exchange_wait(send_sems, recv_sems, parts, lands, after, *, name):
    n = len(parts)

    def body(*refs):
        ins, lands_ = refs[:n], refs[n:2 * n]
        ssem, rsem = refs[2 * n], refs[2 * n + 1]
        for a in range(n):
            for k in range(len(_REL)):
                px, py, pc = _peer(k)
                p = 4 * px + 2 * py + pc
                cp = pltpu.make_async_remote_copy(
                    src_ref=ins[a].at[p], dst_ref=lands_[a].at[p],
                    send_sem=ssem.at[len(_REL) * a + k], recv_sem=rsem.at[len(_REL) * a + k],
                    device_id=(px, py, pc), device_id_type=MESH)
                cp.wait_send()
                cp.wait_recv()

    bufs = [pltpu.HBM(p.shape, p.dtype) for p in parts]
    outs = pl.pallas_call(
        body, name=name, out_shape=(*bufs, *bufs),
        in_specs=(*(_HBM,) * (2 * n), _SEM, _SEM, pl.BlockSpec(memory_space=pl.ANY)), out_specs=(_HBM,) * (2 * n),
        input_output_aliases={i: i for i in range(2 * n)},
        compiler_params=pltpu.CompilerParams(has_side_effects=_EFFECT),
    )(*parts, *lands, send_sems, recv_sems, after)
    return outs[:n], outs[n:]


WEIGHTS = ('norm_w', 'w_in', 'gate_b', 'sgu_norm_g', 'sgu_norm_b', 'sgu_w', 'sgu_b', 'conv_w', 'conv_b', 'dt_bias', 'A_log',
           'D_skip', 'ssd_norm_w', 'w_out', 'final_norm_w')
SHARDED = ('w_in', 'conv_w', 'w_out')
PACK_ROW = 8 * LANE


def _constants():
    tri = np.tril(np.ones((CHUNK, CHUNK), np.float32))
    expand = np.zeros((DT_W, D), np.float32)
    for h in range(HEADS):
        expand[h, h * HEADDIM:(h + 1) * HEADDIM] = 1.0
    sel = np.zeros((D, LANE), np.float32)
    for g in range(SGU_GROUPS):
        sel[g * LANE:(g + 1) * LANE, g] = 1.0
    pos_chunk = np.arange(SGU_BLOCK) // CHUNK
    mask = (pos_chunk[None, :] <= pos_chunk[:, None]).astype(np.float32)
    shift = np.zeros(((CONV_K - 1) * CHUNK, HALO_BLK + CHUNK), np.float32)
    for kk in range(CONV_K - 1):
        for t in range(CHUNK):
            shift[kk * CHUNK + t, HALO_BLK - (CONV_K - 1) + t + kk] = 1.0
    return dict(tri=jnp.asarray(tri, BF16), triT=jnp.asarray(tri.T.copy(), BF16), expand=jnp.asarray(expand, BF16),
                shift=jnp.asarray(shift, BF16),
                expandT=jnp.asarray(expand.T.copy(), BF16), sel=jnp.asarray(sel), mask=jnp.asarray(mask))


def _to_shards(segs):
    starts = np.cumsum([0] + [s.shape[0] for s in segs])
    assert starts[-1] == W_IN
    slabs = []
    for k in range(N_DEV):
        pieces = []
        for s, s0 in zip(segs, starts[:-1]):
            lo, hi = max(k * SHARD_IN, s0), min((k + 1) * SHARD_IN, s0 + s.shape[0])
            if lo < hi:
                pieces.append(s[lo - s0:hi - s0])
        slabs.append(jnp.concatenate(pieces, axis=0))
    return jnp.stack(slabs)


def _local_step(x2, tgt, wpT, wout, cw, p, exchange):
    S = x2.shape[0]
    k = _constants()
    xn = _norm_fwd(x2, p['norm_w'], tm=min(512, S))
    proj = _matmul(xn, wpT, trans_b=True, out_dtype=BF16, tm=min(1024, S), tn=2048, tk=D, name="in_proj")
    wm32 = p['sgu_w'][0] * k['mask']
    wm = wm32.astype(BF16)
    wmT = jnp.swapaxes(wm32, 1, 2).astype(BF16)
    bias_full = jnp.repeat(p['sgu_b'][0].T, LANE, axis=1)
    tm_sgu = min(256, S)
    ya = _sgu_fwd(proj, p['sgu_norm_g'], p['sgu_norm_b'], wm, bias_full, tm=tm_sgu)
    pad32 = lambda a: jnp.pad(a, ((0, 0), (0, DT_W - HEADS)))
    dtb_p, alog_p = pad32(p['dt_bias']), pad32(p['A_log'])
    d_exp = jnp.repeat(p['D_skip'], HEADDIM, axis=1)
    ssd_args = (cw, p['conv_b'], dtb_p, alog_p, d_exp, p['ssd_norm_w'])
    y, yb, states = _ssd_fwd(proj, *ssd_args, k['tri'], k['expand'], k['shift'])
    dh, dhb, mb, dya, dyb, dgl, loss, dfw, dgb = _head(
        x2, ya, yb, proj, tgt, p['gate_b'], wout, p['final_norm_w'][None, :], tm=min(256, S))
    dsgu, dws, dbsT, dsg, dsb = _sgu_bwd(proj, dya, p['sgu_norm_g'], p['sgu_norm_b'], wm, wmT, bias_full, k['mask'], k['sel'],
                                         tm=tm_sgu)
    dssd, dcw, dcb, ddtb, dalog, dD, dnw = _ssd_bwd(proj, dyb, y, states, *ssd_args, k['tri'], k['triT'], k['expand'], k['expandT'],
                                                    k['shift'])
    tk = min(4096, S)
    tn = 1024
    dwT_sgu = _matmul(dsgu, xn, trans_a=True, out_dtype=BF16, tm=1024, tn=tn, tk=tk, name="dw_in_sgu")
    dwT_gate = _matmul(dgl, xn, trans_a=True, out_dtype=BF16, tm=1024, tn=tn, tk=tk, name="dw_in_gate")
    dwT_ssd = _matmul(dssd, xn, trans_a=True, out_dtype=BF16, tm=1024, tn=tn, tk=tk, name="dw_in_ssd")
    dw_out = _matmul(mb, dhb, trans_a=True, out_dtype=BF16, tm=1024, tn=tn, tk=tk, name="dw_out")
    grads = dict(
        gate_b=dgb[0:1], sgu_norm_g=dsg[0:1], sgu_norm_b=dsb[0:1], sgu_w=dws[None],
        sgu_b=dbsT[:, :SGU_GROUPS].T[None], conv_w=dcw[0:CONV_K][None], conv_b=dcb[0:1], dt_bias=ddtb[0:1, :HEADS],
        A_log=dalog[0:1, :HEADS], D_skip=dD[0:1, :HEADS], ssd_norm_w=dnw[0:1], final_norm_w=dfw[0])
    token = exchange([dwT_sgu, dwT_ssd[:W_IN - SEG_SSD[0]], dwT_gate], dw_out, loss[0, 0], grads)
    tm = min(1024, S)
    dxn = _matmul(dsgu, wpT, tm=tm, tn=tn, tk=3072, after=token, name="dxn_sgu")
    dxn = _matmul(dgl, wpT, b_koff=SEG_GATE[0] // 2048, tm=tm, tn=tn, tk=2048, add=dxn, name="dxn_gate")
    dxn = _matmul(dssd, wpT, b_koff=SEG_SSD[0] // 2048, tm=tm, tn=tn, tk=2048, add=dxn, name="dxn_ssd")
    grad_x, dnorm = _norm_bwd(x2, p['norm_w'], dxn, dh, tm=min(256, S))
    return grad_x, dnorm[0:1]


def _pack(arrs):
    rows, offs, r = [], [], 0
    for a in arrs:
        n = a.size
        nr = -(-n // PACK_ROW) * 8
        rows.append(jnp.pad(a.reshape(-1).astype(F32), (0, nr * LANE - n)).reshape(nr, LANE))
        offs.append(r)
        r += nr
    return jnp.concatenate(rows, axis=0), offs


def kernel(x, norm_w, w_in, gate_b, sgu_norm_g, sgu_norm_b, sgu_w, sgu_b, conv_w, conv_b, dt_bias, A_log, D_skip, ssd_norm_w, w_out, final_norm_w, loss_target, m_norm_w, m_w_in, m_gate_b, m_sgu_norm_g, m_sgu_norm_b, m_sgu_w, m_sgu_b, m_conv_w, m_conv_b, m_dt_bias, m_A_log, m_D_skip, m_ssd_norm_w, m_w_out, m_final_norm_w, v_norm_w, v_w_in, v_gate_b, v_sgu_norm_g, v_sgu_norm_b, v_sgu_w, v_sgu_b, v_conv_w, v_conv_b, v_dt_bias, v_A_log, v_D_skip, v_ssd_norm_w, v_w_out, v_final_norm_w):
    w = dict(norm_w=norm_w, w_in=w_in, gate_b=gate_b, sgu_norm_g=sgu_norm_g, sgu_norm_b=sgu_norm_b, sgu_w=sgu_w, sgu_b=sgu_b,
             conv_w=conv_w, conv_b=conv_b, dt_bias=dt_bias, A_log=A_log, D_skip=D_skip, ssd_norm_w=ssd_norm_w, w_out=w_out,
             final_norm_w=final_norm_w)
    m = dict(norm_w=m_norm_w, w_in=m_w_in, gate_b=m_gate_b, sgu_norm_g=m_sgu_norm_g, sgu_norm_b=m_sgu_norm_b, sgu_w=m_sgu_w,
             sgu_b=m_sgu_b, conv_w=m_conv_w, conv_b=m_conv_b, dt_bias=m_dt_bias, A_log=m_A_log, D_skip=m_D_skip,
             ssd_norm_w=m_ssd_norm_w, w_out=m_w_out, final_norm_w=m_final_norm_w)
    v = dict(norm_w=v_norm_w, w_in=v_w_in, gate_b=v_gate_b, sgu_norm_g=v_sgu_norm_g, sgu_norm_b=v_sgu_norm_b, sgu_w=v_sgu_w,
             sgu_b=v_sgu_b, conv_w=v_conv_w, conv_b=v_conv_b, dt_bias=v_dt_bias, A_log=v_A_log, D_skip=v_D_skip,
             ssd_norm_w=v_ssd_norm_w, w_out=v_w_out, final_norm_w=v_final_norm_w)
    me = 4 * lax.axis_index("x") + 2 * lax.axis_index("y") + lax.axis_index("c")
    shard_cw = XBC_W // N_DEV

    tpose = lambda a: jnp.swapaxes(a[0], 0, 1)
    wT = tpose(w_in).astype(BF16)
    first_group = (GROUP - (me * SHARD_IN) % GROUP) % GROUP
    window = lax.dynamic_slice(jnp.pad(wT, ((0, GROUP), (0, 0))), (first_group, 0), (INTERIOR, D))
    wpT, g_out, g_cw, heads, tails = _gather_weights(window, wT[:GROUP], wT[SHARD_IN - GROUP:], w_out[0].astype(BF16),
                                                     conv_w[0], jnp.zeros((W_ROWS - W_IN, D), BF16))
    wpT = _patch_straddlers(wpT, heads, tails)
    wout_full = g_out.reshape(D, D)
    cw_full = jnp.swapaxes(g_cw, 0, 1).reshape(CONV_K, XBC_W)

    flight = {}

    small = [n for n in WEIGHTS if n not in SHARDED and n != 'norm_w']

    def exchange(dw_inT_segs, dw_out, loss_part, grads):
        flight['packed'], flight['offs'] = _pack([grads[n] for n in small] + [loss_part, grads['conv_w']])
        parts = [_to_shards(dw_inT_segs), dw_out.reshape(N_DEV, D // N_DEV, D),
                 jnp.broadcast_to(flight['packed'][None], (N_DEV,) + flight['packed'].shape)]
        flight['sems'], flight['rsems'], flight['parts'], flight['lands'], token = _exchange_start(parts, name="exchange_start")
        return token

    grad_x, dnorm = _local_step(x[0], loss_target[0], wpT, wout_full, cw_full, w, exchange)
    (own_in, own_out, _), (land_in, land_out, land_small) = _exchange_wait(
        flight['sems'], flight['rsems'], flight['parts'], flight['lands'], grad_x, name="exchange_wait")
    me_arr = jnp.reshape(me, (1,)).astype(jnp.int32)
    res = {}
    res['w_in'] = [jnp.swapaxes(o, 0, 1) for o in _adamw_own(
        me_arr, own_in, land_in, tpose(w_in), tpose(m_w_in), tpose(v_w_in), tr=SHARD_IN, tc=256, name="adamw_w_in")]
    res['w_out'] = _adamw_own(me_arr, own_out, land_out, w_out[0], m_w_out[0], v_w_out[0], tr=128, tc=D, name="adamw_w_out")

    (norm_parts,) = _all_gather([_pack([dnorm])[0]], name="gather_norm")
    norm_outs = _adamw(norm_parts, *[_pack([d['norm_w']])[0] for d in (w, m, v)], tr=norm_parts.shape[1], name="adamw_norm")
    res['norm_w'] = [o.reshape(-1)[:D].reshape(w['norm_w'].shape) for o in norm_outs]

    offs = flight['offs']
    gathered = lax.dynamic_update_slice(land_small, flight['packed'][None], (me, 0, 0))
    off_loss, off_cw = offs[-2], offs[-1]
    cw_parts = gathered[:, off_cw:, :].reshape(N_DEV, CONV_K, XBC_W)
    cw_parts = lax.dynamic_slice_in_dim(cw_parts, me * shard_cw, shard_cw, axis=2)
    cw_rows = _pack([cw_parts[0]])[0].shape[0]
    cw_parts = jnp.pad(cw_parts.reshape(N_DEV, -1), ((0, 0), (0, cw_rows * LANE - CONV_K * shard_cw))).reshape(N_DEV, cw_rows, LANE)
    parts = jnp.concatenate([gathered[:, :off_cw, :], cw_parts], axis=1)
    zero = jnp.zeros((), F32)
    packs = [_pack([d[n] for n in small] + [zero, d['conv_w']])[0] for d in (w, m, v)]
    outs = _adamw(parts, *packs, tr=parts.shape[1], name="adamw_small")

    def unpack(o, name):
        if name == 'conv_w':
            return o[off_cw:off_cw + cw_rows].reshape(-1)[:CONV_K * shard_cw].reshape(w['conv_w'].shape)
        r0 = offs[small.index(name)]
        n = w[name].size
        return o[r0:r0 + -(-n // PACK_ROW) * 8].reshape(-1)[:n].reshape(w[name].shape)

    for n in small + ['conv_w']:
        res[n] = [unpack(o, n) for o in outs]
    for n in ('w_in', 'w_out'):
        res[n] = [o[None] for o in res[n]]
    loss = outs[0][off_loss, 0]
    return (loss, grad_x[None], *[res[n][0] for n in WEIGHTS], *[res[n][1] for n in WEIGHTS],
            *[res[n][2] for n in WEIGHTS], *[res[n][3] for n in WEIGHTS])
```

```python
import functools

import numpy as np
import jax
import jax.numpy as jnp
from jax import lax
from jax.experimental import pallas as pl
from jax.experimental.pallas import tpu as pltpu

F32 = jnp.float32
BF16 = jnp.bfloat16
HI = lax.Precision.HIGHEST
MESH = pl.DeviceIdType.MESH

D = 2048
EPS = 1e-5
SGU_BLOCK = 128
SGU_GROUPS = 16
CHUNK = 64
HEADS = 32
HEADDIM = 64
SSD_GROUPS = 4
GROUP_W = D // SSD_GROUPS
STATE = 128
CONV_K = 4
XBC_W = D + 2 * SSD_GROUPS * STATE
W_IN = 15392
N_DEV = 8
SHARD_IN = W_IN // N_DEV
ADAM_LR, ADAM_B1, ADAM_B2, ADAM_EPS, ADAM_WD, ADAM_STEP = 0.001, 0.9, 0.999, 1e-08, 0.01, 10

LANE = 128
DT_W = LANE
OFF_U, OFF_V, OFF_ZA, OFF_G0, OFF_G1, OFF_ZB, OFF_XBC, OFF_DT = 0, 2048, 4096, 6144, 8192, 10240, 12288, 15360
WP = OFF_DT + DT_W
SEG_SGU = (0, 6144)
SEG_GATE = (6144, 4096)
SEG_SSD = (10240, WP - 10240)
SSD_PAD_W = 6144
VMEM_LIMIT = 56 * 1024 * 1024


def _cp(sem=None, vmem=VMEM_LIMIT):
    return pltpu.CompilerParams(dimension_semantics=sem, vmem_limit_bytes=vmem)


def _sigmoid(x):
    return 1.0 / (1.0 + jnp.exp(-x))


def _softplus(x):
    return jnp.maximum(x, 0.0) + jnp.log(1.0 + jnp.exp(-jnp.abs(x)))


def _dot(a, b, precision=None):
    return jnp.dot(a, b, preferred_element_type=F32, precision=precision)


def _dot_nt(a, b, precision=None):
    return lax.dot_general(a, b, (((1,), (1,)), ((), ())), preferred_element_type=F32, precision=precision)


def _dot_tn(a, b, precision=None):
    return lax.dot_general(a, b, (((0,), (0,)), ((), ())), preferred_element_type=F32, precision=precision)


def _split3(a):
    hi = a.astype(BF16)
    r = a - hi.astype(F32)
    mid = r.astype(BF16)
    return hi, mid, (r - mid.astype(F32)).astype(BF16)


def _sel_right(a, sel01):
    m = a.shape[0]
    r = _dot(jnp.concatenate(_split3(a), axis=0), sel01)
    return (r[0:m] + r[m:2 * m]) + r[2 * m:3 * m]


def _sel_left(sel01, a):
    n = a.shape[1]
    r = _dot(sel01, jnp.concatenate(_split3(a), axis=1))
    return (r[:, 0:n] + r[:, n:2 * n]) + r[:, 2 * n:3 * n]


def _matmul(a, b, *, trans_a=False, trans_b=False, b_koff=0, n=None, out_dtype=F32, tm, tn, tk, add=None, after=None, name):
    K, M = a.shape if trans_a else a.shape[::-1]
    N = (n or b.shape[0]) if trans_b else b.shape[1]
    assert M % tm == 0 and N % tn == 0 and K % tk == 0 and not (trans_a and trans_b), (name, M, N, K, tm, tn, tk)
    nk = K // tk

    def body(*refs):
        a_ref, b_ref = refs[:2]
        add_ref = refs[2] if add is not None else None
        o_ref, acc_ref = refs[-2:]
        k = pl.program_id(2)
        if trans_a:
            part = _dot_tn(a_ref[...], b_ref[...])
        else:
            part = _dot_nt(a_ref[...], b_ref[...]) if trans_b else _dot(a_ref[...], b_ref[...])

        def result(r):
            if add_ref is not None:
                r = r + add_ref[...]
            return r.astype(out_dtype)

        if nk == 1:
            o_ref[...] = result(part)
        else:
            @pl.when(k == 0)
            def _():
                acc_ref[...] = part

            @pl.when(jnp.logical_and(k > 0, k < nk - 1))
            def _():
                acc_ref[...] += part

            @pl.when(k == nk - 1)
            def _():
                o_ref[...] = result(acc_ref[...] + part)

    in_specs = [pl.BlockSpec((tk, tm), lambda i, j, k: (k, i)) if trans_a else pl.BlockSpec((tm, tk), lambda i, j, k: (i, k)),
                pl.BlockSpec((tn, tk), lambda i, j, k: (j, k)) if trans_b else pl.BlockSpec((tk, tn), lambda i, j, k: (k + b_koff, j))]
    args = [a, b]
    if add is not None:
        in_specs.append(pl.BlockSpec((tm, tn), lambda i, j, k: (i, j)))
        args.append(add)
    if after is not None:
        in_specs.append(pl.BlockSpec(memory_space=pl.ANY))
        args.append(after)
    return pl.pallas_call(
        body, name=name, grid=(M // tm, N // tn, nk), in_specs=in_specs,
        out_specs=pl.BlockSpec((tm, tn), lambda i, j, k: (i, j)),
        out_shape=jax.ShapeDtypeStruct((M, N), out_dtype),
        scratch_shapes=[pltpu.VMEM((tm, tn), F32)],
        compiler_params=_cp(("parallel", "parallel", "arbitrary")),
    )(*args)


def _norm_fwd(x, w, *, tm):
    S = x.shape[0]

    def body(x_ref, w_ref, o_ref):
        xv = x_ref[...]
        r = lax.rsqrt(jnp.mean(xv * xv, axis=-1, keepdims=True) + EPS)
        o_ref[...] = (xv * r * w_ref[...]).astype(BF16)

    return pl.pallas_call(
        body, name="norm_fwd", grid=(S // tm,),
        in_specs=[pl.BlockSpec((tm, D), lambda i: (i, 0)), pl.BlockSpec((1, D), lambda i: (0, 0))],
        out_specs=pl.BlockSpec((tm, D), lambda i: (i, 0)),
        out_shape=jax.ShapeDtypeStruct((S, D), BF16), compiler_params=_cp(("parallel",)),
    )(x, w)


def _norm_bwd(x, w, dxn, dh, *, tm):
    S = x.shape[0]

    def body(x_ref, w_ref, dxn_ref, dh_ref, gx_ref, dw_ref):
        xv = x_ref[...]
        r = lax.rsqrt(jnp.mean(xv * xv, axis=-1, keepdims=True) + EPS)
        xh = xv * r
        dxn_v = dxn_ref[...]
        dxh = dxn_v * w_ref[...]
        gx_ref[...] = dh_ref[...] + r * (dxh - xh * jnp.mean(dxh * xh, axis=-1, keepdims=True))

        @pl.when(pl.program_id(0) == 0)
        def _():
            dw_ref[...] = jnp.zeros_like(dw_ref)

        dw_ref[0:1, :] += jnp.sum(dxn_v * xh, axis=0, keepdims=True)

    row = pl.BlockSpec((tm, D), lambda i: (i, 0))
    return pl.pallas_call(
        body, name="norm_bwd", grid=(S // tm,),
        in_specs=[row, pl.BlockSpec((1, D), lambda i: (0, 0)), row, row],
        out_specs=[row, pl.BlockSpec((8, D), lambda i: (0, 0))],
        out_shape=[jax.ShapeDtypeStruct((S, D), F32), jax.ShapeDtypeStruct((8, D), F32)],
        compiler_params=_cp(("arbitrary",)),
    )(x, w, dxn, dh)


def _sgu_core(u_ref, v_ref, z_ref, g_ref, b_ref, wm_ref, bias_ref, vnb_ref, mixed_ref, tm):
    v = v_ref[...].astype(F32)
    mu = jnp.mean(v, axis=-1, keepdims=True)
    vc = v - mu
    rs = lax.rsqrt(jnp.mean(vc * vc, axis=-1, keepdims=True) + EPS)
    vh = vc * rs
    vnb_ref[...] = (vh * g_ref[...] + b_ref[...]).astype(BF16)
    for blk in range(tm // SGU_BLOCK):
        rows = pl.ds(blk * SGU_BLOCK, SGU_BLOCK)
        for gi in range(SGU_GROUPS):
            cols = pl.ds(gi * LANE, LANE)
            mixed_ref[rows, cols] = _dot(wm_ref[gi], vnb_ref[rows, cols]) + bias_ref[:, cols]
    return vh, rs


def _sgu_fwd(proj, g, b, wm, bias_full, *, tm):
    S = proj.shape[0]

    def body(u_ref, v_ref, z_ref, g_ref, b_ref, wm_ref, bias_ref, y_ref, vnb_ref, mixed_ref):
        _sgu_core(u_ref, v_ref, z_ref, g_ref, b_ref, wm_ref, bias_ref, vnb_ref, mixed_ref, tm)
        z = z_ref[...].astype(F32)
        y_ref[...] = (u_ref[...].astype(F32) * mixed_ref[...] * (z * _sigmoid(z))).astype(BF16)

    seg = lambda off: pl.BlockSpec((tm, D), lambda i: (i, off // D))
    full = lambda a: pl.BlockSpec(a.shape, lambda i: (0,) * a.ndim)
    return pl.pallas_call(
        body, name="sgu_fwd", grid=(S // tm,),
        in_specs=[seg(OFF_U), seg(OFF_V), seg(OFF_ZA), full(g), full(b), full(wm), full(bias_full)],
        out_specs=pl.BlockSpec((tm, D), lambda i: (i, 0)),
        out_shape=jax.ShapeDtypeStruct((S, D), BF16),
        scratch_shapes=[pltpu.VMEM((tm, D), BF16), pltpu.VMEM((tm, D), F32)],
        compiler_params=_cp(("parallel",)),
    )(proj, proj, proj, g, b, wm, bias_full)


def _sgu_bwd(proj, dy, g, b, wm, wmT, bias_full, mask, sel, *, tm):
    S = proj.shape[0]
    nsteps = S // tm

    def body(u_ref, v_ref, z_ref, dy_ref, g_ref, b_ref, wm_ref, wmT_ref, bias_ref, mask_ref, sel_ref,
             dp_ref, dws_ref, dbs_ref, dg_ref, db_ref, vnb_ref, mixed_ref, dmb_ref, dvn_ref, dbias_ref):
        i = pl.program_id(0)

        @pl.when(i == 0)
        def _():
            dws_ref[...] = jnp.zeros_like(dws_ref)
            dg_ref[...] = jnp.zeros_like(dg_ref)
            db_ref[...] = jnp.zeros_like(db_ref)
            dbias_ref[...] = jnp.zeros_like(dbias_ref)

        vh, rs = _sgu_core(u_ref, v_ref, z_ref, g_ref, b_ref, wm_ref, bias_ref, vnb_ref, mixed_ref, tm)
        u = u_ref[...].astype(F32)
        z = z_ref[...].astype(F32)
        dy_v = dy_ref[...].astype(F32)
        mixed = mixed_ref[...]
        sg = _sigmoid(z)
        sz = z * sg
        dp_ref[:, 0:D] = (dy_v * mixed * sz).astype(BF16)
        dp_ref[:, 2 * D:3 * D] = (dy_v * u * mixed * (sg * (1.0 + z * (1.0 - sg)))).astype(BF16)
        dmixed = dy_v * u * sz
        dmb_ref[...] = dmixed.astype(BF16)
        for blk in range(tm // SGU_BLOCK):
            dbias_ref[...] += dmixed[blk * SGU_BLOCK:(blk + 1) * SGU_BLOCK, :]
        for blk in range(tm // SGU_BLOCK):
            rows = pl.ds(blk * SGU_BLOCK, SGU_BLOCK)
            for gi in range(SGU_GROUPS):
                cols = pl.ds(gi * LANE, LANE)
                dm = dmb_ref[rows, cols]
                dvn_ref[rows, cols] = _dot(wmT_ref[gi], dm)
                dws_ref[gi] += _dot_nt(dm, vnb_ref[rows, cols])
        dvn = dvn_ref[...]
        dg_ref[0:1, :] += jnp.sum(dvn * vh, axis=0, keepdims=True)
        db_ref[0:1, :] += jnp.sum(dvn, axis=0, keepdims=True)
        dvh = dvn * g_ref[...]
        dv = rs * (dvh - jnp.mean(dvh, axis=-1, keepdims=True) - vh * jnp.mean(dvh * vh, axis=-1, keepdims=True))
        dp_ref[:, D:2 * D] = dv.astype(BF16)

        @pl.when(i == nsteps - 1)
        def _():
            for gi in range(SGU_GROUPS):
                dws_ref[gi] = dws_ref[gi] * mask_ref[...]
            dbs_ref[...] = _dot(dbias_ref[...], sel_ref[...], precision=HI)

    seg = lambda off: pl.BlockSpec((tm, D), lambda i: (i, off // D))
    full = lambda a: pl.BlockSpec(a.shape, lambda i: (0,) * a.ndim)
    return pl.pallas_call(
        body, name="sgu_bwd", grid=(nsteps,),
        in_specs=[seg(OFF_U), seg(OFF_V), seg(OFF_ZA), pl.BlockSpec((tm, D), lambda i: (i, 0)),
                  full(g), full(b), full(wm), full(wmT), full(bias_full), full(mask), full(sel)],
        out_specs=[pl.BlockSpec((tm, 3 * D), lambda i: (i, 0)),
                   pl.BlockSpec((SGU_GROUPS, SGU_BLOCK, SGU_BLOCK), lambda i: (0, 0, 0)),
                   pl.BlockSpec((SGU_BLOCK, LANE), lambda i: (0, 0)),
                   pl.BlockSpec((8, D), lambda i: (0, 0)), pl.BlockSpec((8, D), lambda i: (0, 0))],
        out_shape=[jax.ShapeDtypeStruct((S, 3 * D), BF16),
                   jax.ShapeDtypeStruct((SGU_GROUPS, SGU_BLOCK, SGU_BLOCK), F32),
                   jax.ShapeDtypeStruct((SGU_BLOCK, LANE), F32),
                   jax.ShapeDtypeStruct((8, D), F32), jax.ShapeDtypeStruct((8, D), F32)],
        scratch_shapes=[pltpu.VMEM((tm, D), BF16), pltpu.VMEM((tm, D), F32), pltpu.VMEM((tm, D), BF16),
                        pltpu.VMEM((tm, D), F32), pltpu.VMEM((SGU_BLOCK, D), F32)],
        compiler_params=_cp(("arbitrary",)),
    )(proj, proj, proj, dy, g, b, wm, wmT, bias_full, mask, sel)


SSD_T = 2 * CHUNK
HALO = 8
HALO_BLK = 16


def _pair_masks():
    row = lax.broadcasted_iota(jnp.int32, (CHUNK, LANE), 0)
    lane = lax.broadcasted_iota(jnp.int32, (CHUNK, LANE), 1)
    pos = jnp.where(lane >= CHUNK, lane - CHUNK, lane)
    diag = (row == pos).astype(F32)
    causal = row >= pos
    lo = (lane < CHUNK).astype(F32)
    return diag, causal, lo, 1.0 - lo


def _ssd_chunk_fwd(c, ext_ref, shift_ref, dt_ref, cw_ref, cb_ref, dtb_ref, alog_ref, tri_ref, exp_ref):
    r0 = c * CHUNK
    win = ext_ref[pl.ds(r0, HALO_BLK + CHUNK), :]
    sh = _dot(shift_ref[...], win)
    taps = [sh[k * CHUNK:(k + 1) * CHUNK] for k in range(CONV_K - 1)] + [win[HALO_BLK:].astype(F32)]
    pre = cb_ref[...] + sum(cw_ref[k:k + 1, :] * taps[k] for k in range(CONV_K))
    sg = _sigmoid(pre)
    xc = pre * sg
    dtr = dt_ref[pl.ds(r0, CHUNK), :].astype(F32) + dtb_ref[...]
    dtv = _softplus(dtr)
    A = -jnp.exp(alog_ref[...])
    acs = _sel_left(tri_ref[...], dtv * A)
    both = _sel_right(jnp.concatenate([acs, dtv], axis=0), exp_ref[...])
    E, dtE = both[0:CHUNK], both[CHUNK:2 * CHUNK]
    return dict(taps=taps, pre=pre, sg=sg, xc=xc, dtr=dtr, dtv=dtv, A=A, E=E, dtE=dtE)


def _ssd_fwd(proj, conv_w, conv_b, dtb_p, alog_p, d_exp, norm_w, tri, expand, shift):
    S = proj.shape[0]
    T = SSD_T
    nsteps = S // T
    ncl = T // CHUNK

    def body(zb_ref, xbc_ref, halo_ref, dt_ref, cw_ref, cb_ref, dtb_ref, alog_ref, dexp_ref, nw_ref, tri_ref, exp_ref, shift_ref,
             y_ref, yb_ref, st_ref, ht_ref, ext_ref):
        i = pl.program_id(0)

        @pl.when(i == 0)
        def _():
            ht_ref[...] = jnp.zeros_like(ht_ref)
            ext_ref[0:HALO_BLK, :] = jnp.zeros((HALO_BLK, XBC_W), BF16)

        @pl.when(i > 0)
        def _():
            ext_ref[0:HALO_BLK, :] = halo_ref[...]

        ext_ref[HALO_BLK:HALO_BLK + T, :] = xbc_ref[...]
        diag, causal, lo, hi = _pair_masks()
        for c in range(ncl):
            q = _ssd_chunk_fwd(c, ext_ref, shift_ref, dt_ref, cw_ref, cb_ref, dtb_ref, alog_ref, tri_ref, exp_ref)
            rows = pl.ds(c * CHUNK, CHUNK)
            xc, E, dtE = q["xc"], q["E"], q["dtE"]
            xs = xc[:, 0:D]
            total = E[CHUNK - 1:CHUNK, :]
            x_dt = xs * dtE
            eE = jnp.exp(E)
            xw = x_dt * jnp.exp(total - E)
            st_ref[c] = ht_ref[...]
            for g in range(SSD_GROUPS):
                gc = slice(g * GROUP_W, (g + 1) * GROUP_W)
                Bg = xc[:, D + g * STATE:D + (g + 1) * STATE].astype(BF16)
                Cg = xc[:, D + SSD_GROUPS * STATE + g * STATE:D + SSD_GROUPS * STATE + (g + 1) * STATE].astype(BF16)
                cb2 = _dot_nt(Cg, jnp.concatenate([Bg, Bg], axis=0))
                htg = ht_ref[:, gc]
                y_ref[rows, gc] = eE[:, gc] * _dot(Cg, htg.astype(BF16)) + xs[:, gc] * dexp_ref[:, gc]
                for jj in range(GROUP_W // LANE):
                    pc = slice(g * GROUP_W + jj * LANE, g * GROUP_W + (jj + 1) * LANE)
                    Ej = E[:, pc]
                    e2 = jnp.sum(Ej * diag, axis=0, keepdims=True)
                    Mp = cb2 * jnp.exp(jnp.where(causal, Ej - e2, -1e30))
                    xj = x_dt[:, pc]
                    xbd = jnp.concatenate([xj * lo, xj * hi], axis=0).astype(BF16)
                    y_ref[rows, pc] += _dot(Mp.astype(BF16), xbd)
                ht_ref[:, gc] = jnp.exp(total[:, gc]) * htg + _dot_tn(Bg, xw[:, gc].astype(BF16))
            zb = zb_ref[rows, :].astype(F32)
            hh = y_ref[rows, :] * (zb * _sigmoid(zb))
            for g in range(SSD_GROUPS):
                gc = slice(g * GROUP_W, (g + 1) * GROUP_W)
                hg = hh[:, gc]
                r = lax.rsqrt(jnp.mean(hg * hg, axis=-1, keepdims=True) + EPS)
                yb_ref[rows, gc] = (hg * r * nw_ref[:, gc]).astype(BF16)

    full = lambda a: pl.BlockSpec(a.shape, lambda i: (0,) * a.ndim)
    hb = T // HALO_BLK
    return pl.pallas_call(
        body, name="ssd_fwd", grid=(nsteps,),
        in_specs=[pl.BlockSpec((T, D), lambda i: (i, OFF_ZB // D)),
                  pl.BlockSpec((T, XBC_W), lambda i: (i, OFF_XBC // XBC_W)),
                  pl.BlockSpec((HALO_BLK, XBC_W), lambda i: (jnp.maximum(i * hb - 1, 0), OFF_XBC // XBC_W)),
                  pl.BlockSpec((T, DT_W), lambda i: (i, OFF_DT // DT_W)),
                  full(conv_w), full(conv_b), full(dtb_p), full(alog_p), full(d_exp), full(norm_w), full(tri), full(expand),
                  full(shift)],
        out_specs=[pl.BlockSpec((T, D), lambda i: (i, 0)), pl.BlockSpec((T, D), lambda i: (i, 0)),
                   pl.BlockSpec((ncl, STATE, D), lambda i: (i, 0, 0))],
        out_shape=[jax.ShapeDtypeStruct((S, D), F32), jax.ShapeDtypeStruct((S, D), BF16),
                   jax.ShapeDtypeStruct((S // CHUNK, STATE, D), F32)],
        scratch_shapes=[pltpu.VMEM((STATE, D), F32), pltpu.VMEM((HALO_BLK + T, XBC_W), BF16)],
        compiler_params=_cp(("arbitrary",)),
    )(proj, proj, proj, proj, conv_w, conv_b, dtb_p, alog_p, d_exp, norm_w, tri, expand, shift)


def _ssd_bwd(proj, dyb, y, states, conv_w, conv_b, dtb_p, alog_p, d_exp, norm_w, tri, triT, expand, expandT, shift):
    S = proj.shape[0]
    T = SSD_T
    nsteps = S // T
    ncl = T // CHUNK
    SSD_W = SSD_PAD_W

    def body(zb_ref, xbc_ref, halo_ref, dt_ref, dyb_ref, y_ref, st_ref, cw_ref, cb_ref, dtb_ref, alog_ref, dexp_ref, nw_ref,
             tri_ref, triT_ref, exp_ref, expT_ref, shift_ref,
             dp_ref, dcw_ref, dcb_ref, ddtb_ref, dalog_ref, dD_ref, dnw_ref,
             dht_ref, ext_ref, dpre_ref, dy_s, dE_s, dxdt_s, dxc_s, dDacc_ref, dAacc_ref):
        i = pl.program_id(0)

        @pl.when(i == 0)
        def _():
            for r in (dht_ref, dcw_ref, dcb_ref, ddtb_ref, dnw_ref, dDacc_ref, dAacc_ref):
                r[...] = jnp.zeros_like(r)
            dpre_ref[T:T + HALO, :] = jnp.zeros((HALO, XBC_W), F32)

        @pl.when(i == nsteps - 1)
        def _():
            ext_ref[0:HALO_BLK, :] = jnp.zeros((HALO_BLK, XBC_W), BF16)

        @pl.when(i < nsteps - 1)
        def _():
            ext_ref[0:HALO_BLK, :] = halo_ref[...]

        ext_ref[HALO_BLK:HALO_BLK + T, :] = xbc_ref[...]
        diag, causal, lo, hi = _pair_masks()
        last_row = (lax.broadcasted_iota(jnp.int32, (CHUNK, 1), 0) == CHUNK - 1).astype(F32)
        for c in reversed(range(ncl)):
            q = _ssd_chunk_fwd(c, ext_ref, shift_ref, dt_ref, cw_ref, cb_ref, dtb_ref, alog_ref, tri_ref, exp_ref)
            rows = pl.ds(c * CHUNK, CHUNK)
            pre, sg, xc, dtr, dtv, A, E, dtE = (q[k] for k in ("pre", "sg", "xc", "dtr", "dtv", "A", "E", "dtE"))
            xs = xc[:, 0:D]
            total = E[CHUNK - 1:CHUNK, :]
            x_dt = xs * dtE
            eE = jnp.exp(E)
            wdec = jnp.exp(total - E)
            zb = zb_ref[rows, :].astype(F32)
            yv = y_ref[rows, :]
            sgz = _sigmoid(zb)
            sz = zb * sgz
            hh = yv * sz
            for g in range(SSD_GROUPS):
                gc = slice(g * GROUP_W, (g + 1) * GROUP_W)
                hg = hh[:, gc]
                r = lax.rsqrt(jnp.mean(hg * hg, axis=-1, keepdims=True) + EPS)
                dyb_g = dyb_ref[rows, gc].astype(F32)
                dn = dyb_g * nw_ref[:, gc]
                dnw_ref[0:1, gc] += jnp.sum(dyb_g * hg * r, axis=0, keepdims=True)
                dy_s[:, gc] = r * dn - hg * (r * r * r) * jnp.mean(dn * hg, axis=-1, keepdims=True)
            dhh = dy_s[...]
            dp_ref[rows, 0:D] = (dhh * yv * (sgz * (1.0 + zb * (1.0 - sgz)))).astype(BF16)
            dy = dhh * sz
            dy_s[...] = dy
            dDacc_ref[0:1, :] += jnp.sum(dy * xs, axis=0, keepdims=True)
            dxc_s[:, 0:D] = dy * dexp_ref[...]
            for g in range(SSD_GROUPS):
                gc = slice(g * GROUP_W, (g + 1) * GROUP_W)
                bcol = slice(D + g * STATE, D + (g + 1) * STATE)
                ccol = slice(D + SSD_GROUPS * STATE + g * STATE, D + SSD_GROUPS * STATE + (g + 1) * STATE)
                Bg = xc[:, bcol].astype(BF16)
                Cg = xc[:, ccol].astype(BF16)
                B2 = jnp.concatenate([Bg, Bg], axis=0)
                cb2 = _dot_nt(Cg, B2)
                htg = st_ref[c, :, gc]
                htb = htg.astype(BF16)
                dhn = dht_ref[:, gc]
                dhnb = dhn.astype(BF16)
                dyg = dy[:, gc]
                eEg = eE[:, gc]
                wg = wdec[:, gc]
                xdg = x_dt[:, gc]
                CH = _dot(Cg, htb)
                dCHb = (dyg * eEg).astype(BF16)
                dC = _dot_nt(dCHb, htb)
                dl = jnp.exp(total[:, gc])
                dht_prev = _dot_tn(Cg, dCHb) + dl * dhn
                dtot = jnp.sum(dhn * htg, axis=0, keepdims=True) * dl
                dxw = _dot(Bg, dhnb)
                dB = _dot_nt((xdg * wg).astype(BF16), dhnb)
                dwd = dxw * xdg * wg
                dtot = dtot + jnp.sum(dwd, axis=0, keepdims=True)
                dE_s[:, gc] = dyg * eEg * CH - dwd + last_row * dtot
                dxdt_s[:, gc] = dxw * wg
                dcb2 = jnp.zeros((CHUNK, LANE), F32)
                for jj in range(GROUP_W // LANE):
                    pc = slice(g * GROUP_W + jj * LANE, g * GROUP_W + (jj + 1) * LANE)
                    Ej = E[:, pc]
                    e2 = jnp.sum(Ej * diag, axis=0, keepdims=True)
                    Lp = jnp.exp(jnp.where(causal, Ej - e2, -1e30))
                    Mp = cb2 * Lp
                    xj = x_dt[:, pc]
                    xbd = jnp.concatenate([xj * lo, xj * hi], axis=0).astype(BF16)
                    dyj = dy[:, pc].astype(BF16)
                    dMp = _dot_nt(dyj, xbd)
                    dxbd = _dot_tn(Mp.astype(BF16), dyj)
                    dxdt_s[:, pc] += dxbd[0:CHUNK, :] * lo + dxbd[CHUNK:2 * CHUNK, :] * hi
                    dcb2 = dcb2 + dMp * Lp
                    dseg = dMp * Mp
                    dE_s[:, pc] += dseg - diag * jnp.sum(dseg, axis=0, keepdims=True)
                dcb2b = dcb2.astype(BF16)
                dC = dC + _dot(dcb2b, B2)
                dB2 = _dot_tn(dcb2b, Cg)
                dB = dB + dB2[0:CHUNK, :] + dB2[CHUNK:2 * CHUNK, :]
                dxc_s[:, bcol] = dB
                dxc_s[:, ccol] = dC
                dht_ref[:, gc] = dht_prev
            dx_dt = dxdt_s[...]
            dxc_s[:, 0:D] += dx_dt * dtE
            red = _sel_right(jnp.concatenate([dE_s[...], dx_dt * xs], axis=0), expT_ref[...])
            da = _sel_left(triT_ref[...], red[0:CHUNK, :])
            ddtv = red[CHUNK:2 * CHUNK, :] + da * A
            dAacc_ref[0:1, :] += jnp.sum(da * dtv, axis=0, keepdims=True)
            ddtr = ddtv * _sigmoid(dtr)
            ddtb_ref[0:1, :] += jnp.sum(ddtr, axis=0, keepdims=True)
            dp_ref[rows, D + XBC_W:D + XBC_W + DT_W] = ddtr.astype(BF16)
            dpre = dxc_s[...] * (sg * (1.0 + pre * (1.0 - sg)))
            dpre_ref[rows, :] = dpre
            dcb_ref[0:1, :] += jnp.sum(dpre, axis=0, keepdims=True)
            for k in range(CONV_K):
                dcw_ref[k:k + 1, :] += jnp.sum(dpre * q["taps"][k], axis=0, keepdims=True)
        dxbc = jnp.zeros((T, XBC_W), F32)
        for k in range(CONV_K):
            dxbc = dxbc + cw_ref[k:k + 1, :] * dpre_ref[pl.ds(CONV_K - 1 - k, T), :]
        dp_ref[:, D:D + XBC_W] = dxbc.astype(BF16)
        dp_ref[:, SEG_SSD[1]:SSD_W] = jnp.zeros((T, SSD_W - SEG_SSD[1]), BF16)
        dpre_ref[T:T + HALO, :] = dpre_ref[0:HALO, :]

        @pl.when(i == nsteps - 1)
        def _():
            dalog_ref[...] = dAacc_ref[...] * (-jnp.exp(alog_ref[...]))
            dD_ref[...] = _dot(dDacc_ref[...], expT_ref[...].astype(F32), precision=HI)

    full = lambda a: pl.BlockSpec(a.shape, lambda i: (0,) * a.ndim)
    hb = T // HALO_BLK
    rev = lambda i: nsteps - 1 - i
    acc = lambda w: pl.BlockSpec((8, w), lambda i: (0, 0))
    return pl.pallas_call(
        body, name="ssd_bwd", grid=(nsteps,),
        in_specs=[pl.BlockSpec((T, D), lambda i: (rev(i), OFF_ZB // D)),
                  pl.BlockSpec((T, XBC_W), lambda i: (rev(i), OFF_XBC // XBC_W)),
                  pl.BlockSpec((HALO_BLK, XBC_W), lambda i: (jnp.maximum(rev(i) * hb - 1, 0), OFF_XBC // XBC_W)),
                  pl.BlockSpec((T, DT_W), lambda i: (rev(i), OFF_DT // DT_W)),
                  pl.BlockSpec((T, D), lambda i: (rev(i), 0)), pl.BlockSpec((T, D), lambda i: (rev(i), 0)),
                  pl.BlockSpec((ncl, STATE, D), lambda i: (rev(i), 0, 0)),
                  full(conv_w), full(conv_b), full(dtb_p), full(alog_p), full(d_exp), full(norm_w),
                  full(tri), full(triT), full(expand), full(expandT), full(shift)],
        out_specs=[pl.BlockSpec((T, SSD_W), lambda i: (rev(i), 0)),
                   acc(XBC_W), acc(XBC_W), acc(DT_W), acc(DT_W), acc(DT_W), acc(D)],
        out_shape=[jax.ShapeDtypeStruct((S, SSD_W), BF16),
                   jax.ShapeDtypeStruct((8, XBC_W), F32), jax.ShapeDtypeStruct((8, XBC_W), F32),
                   jax.ShapeDtypeStruct((8, DT_W), F32), jax.ShapeDtypeStruct((8, DT_W), F32),
                   jax.ShapeDtypeStruct((8, DT_W), F32), jax.ShapeDtypeStruct((8, D), F32)],
        scratch_shapes=[pltpu.VMEM((STATE, D), F32), pltpu.VMEM((HALO_BLK + T, XBC_W), BF16), pltpu.VMEM((T + HALO, XBC_W), F32),
                        pltpu.VMEM((CHUNK, D), F32), pltpu.VMEM((CHUNK, D), F32), pltpu.VMEM((CHUNK, D), F32),
                        pltpu.VMEM((CHUNK, XBC_W), F32), pltpu.VMEM((8, D), F32), pltpu.VMEM((8, DT_W), F32)],
        compiler_params=_cp(("arbitrary",)),
    )(proj, proj, proj, proj, dyb, y, states, conv_w, conv_b, dtb_p, alog_p, d_exp, norm_w, tri, triT, expand, expandT, shift)


def _head(x, ya, yb, proj, target, gate_b, wout, fw, *, tm):
    S = x.shape[0]

    def body(x_ref, ya_ref, yb_ref, gl0_ref, gl1_ref, t_ref, gb_ref, w_ref, fw_ref,
             dh_ref, dhb_ref, mb_ref, dya_ref, dyb_ref, dgl_ref, loss_ref, dfw_ref, dgb_ref):
        @pl.when(pl.program_id(0) == 0)
        def _():
            loss_ref[...] = jnp.zeros_like(loss_ref)
            dfw_ref[...] = jnp.zeros_like(dfw_ref)
            dgb_ref[...] = jnp.zeros_like(dgb_ref)

        ya_v = ya_ref[...].astype(F32)
        yb_v = yb_ref[...].astype(F32)
        g0 = _sigmoid(gl0_ref[...].astype(F32) + gb_ref[:, 0:D])
        g1 = _sigmoid(gl1_ref[...].astype(F32) + gb_ref[:, D:2 * D])
        mb = (g0 * ya_v + g1 * yb_v).astype(BF16)
        mb_ref[...] = mb
        h = x_ref[...] + _dot(mb, w_ref[...])
        r = lax.rsqrt(jnp.mean(h * h, axis=-1, keepdims=True) + EPS)
        hn = h * r
        err = hn * fw_ref[...] - t_ref[...]
        loss_ref[...] += 0.5 * jnp.sum(jnp.mean(err * err, axis=-1, keepdims=True))
        dyf = err * (1.0 / D)
        dfw_ref[0:1, :] += jnp.sum(dyf * hn, axis=0, keepdims=True)
        dhn = dyf * fw_ref[...]
        dh = r * (dhn - hn * jnp.mean(dhn * hn, axis=-1, keepdims=True))
        dh_ref[...] = dh
        dhb = dh.astype(BF16)
        dhb_ref[...] = dhb
        dm = _dot_nt(dhb, w_ref[...])
        dya_ref[...] = (dm * g0).astype(BF16)
        dyb_ref[...] = (dm * g1).astype(BF16)
        dgl0 = dm * ya_v * g0 * (1.0 - g0)
        dgl1 = dm * yb_v * g1 * (1.0 - g1)
        dgl_ref[:, 0:D] = dgl0.astype(BF16)
        dgl_ref[:, D:2 * D] = dgl1.astype(BF16)
        dgb_ref[0:1, 0:D] += jnp.sum(dgl0, axis=0, keepdims=True)
        dgb_ref[0:1, D:2 * D] += jnp.sum(dgl1, axis=0, keepdims=True)

    row = pl.BlockSpec((tm, D), lambda i: (i, 0))
    seg = lambda off: pl.BlockSpec((tm, D), lambda i: (i, off // D))
    full = lambda a: pl.BlockSpec(a.shape, lambda i: (0,) * a.ndim)
    acc = lambda w: pl.BlockSpec((8, w), lambda i: (0, 0))
    return pl.pallas_call(
        body, name="head", grid=(S // tm,),
        in_specs=[row, row, row, seg(OFF_G0), seg(OFF_G1), row, full(gate_b), full(wout), full(fw)],
        out_specs=[row, row, row, row, row, pl.BlockSpec((tm, 2 * D), lambda i: (i, 0)), acc(LANE), acc(D), acc(2 * D)],
        out_shape=[jax.ShapeDtypeStruct((S, D), F32), jax.ShapeDtypeStruct((S, D), BF16), jax.ShapeDtypeStruct((S, D), BF16),
                   jax.ShapeDtypeStruct((S, D), BF16), jax.ShapeDtypeStruct((S, D), BF16), jax.ShapeDtypeStruct((S, 2 * D), BF16),
                   jax.ShapeDtypeStruct((8, LANE), F32), jax.ShapeDtypeStruct((8, D), F32), jax.ShapeDtypeStruct((8, 2 * D), F32)],
        compiler_params=_cp(("arbitrary",)),
    )(x, ya, yb, proj, proj, target, gate_b, wout, fw)


def _adam_update(g, w_ref, m_ref, v_ref, g_ref, d_ref, m2_ref, v2_ref):
    m2 = ADAM_B1 * m_ref[...] + (1.0 - ADAM_B1) * g
    v2 = ADAM_B2 * v_ref[...] + (1.0 - ADAM_B2) * (g * g)
    m_hat = m2 / (1.0 - ADAM_B1 ** ADAM_STEP)
    v_hat = v2 / (1.0 - ADAM_B2 ** ADAM_STEP)
    g_ref[...] = g
    d_ref[...] = -ADAM_LR * (m_hat / (jnp.sqrt(v_hat) + ADAM_EPS) + ADAM_WD * w_ref[...])
    m2_ref[...] = m2
    v2_ref[...] = v2


def _adamw_own(me, own, landed, w, m, v, *, tr, tc, name):
    _, R, C = landed.shape
    assert R % tr == 0 and C % tc == 0, (name, R, C, tr, tc)

    def body(me_ref, own_ref, p_ref, w_ref, m_ref, v_ref, g_ref, d_ref, m2_ref, v2_ref):
        mine = own_ref[0].astype(F32)
        g = jnp.where(me_ref[0] == 0, mine, p_ref[0].astype(F32))
        for k in range(1, N_DEV):
            g = g + jnp.where(me_ref[0] == k, mine, p_ref[k].astype(F32))
        _adam_update(g, w_ref, m_ref, v_ref, g_ref, d_ref, m2_ref, v2_ref)

    tile = pl.BlockSpec((tr, tc), lambda i, j, me_ref: (i, j))
    return pl.pallas_call(
        body, name=name,
        grid_spec=pltpu.PrefetchScalarGridSpec(
            num_scalar_prefetch=1, grid=(R // tr, C // tc),
            in_specs=[pl.BlockSpec((1, tr, tc), lambda i, j, me_ref: (me_ref[0], i, j)),
                      pl.BlockSpec((N_DEV, tr, tc), lambda i, j, me_ref: (0, i, j)), tile, tile, tile],
            out_specs=[tile, tile, tile, tile]),
        out_shape=[jax.ShapeDtypeStruct((R, C), F32)] * 4,
        compiler_params=_cp(("parallel", "parallel")),
    )(me, own, landed, w, m, v)


def _adamw(parts, w, m, v, *, tr, name):
    _, R, C = parts.shape
    assert R % tr == 0, (name, R, tr)

    def body(p_ref, w_ref, m_ref, v_ref, g_ref, d_ref, m2_ref, v2_ref):
        g = p_ref[0].astype(F32)
        for k in range(1, N_DEV):
            g = g + p_ref[k].astype(F32)
        _adam_update(g, w_ref, m_ref, v_ref, g_ref, d_ref, m2_ref, v2_ref)

    row = pl.BlockSpec((tr, C), lambda i: (i, 0))
    return pl.pallas_call(
        body, name=name, grid=(R // tr,),
        in_specs=[pl.BlockSpec((N_DEV, tr, C), lambda i: (0, i, 0)), row, row, row],
        out_specs=[row, row, row, row],
        out_shape=[jax.ShapeDtypeStruct((R, C), F32)] * 4,
        compiler_params=_cp(("parallel",)),
    )(parts, w, m, v)


def _place():
    x, y, c = lax.axis_index("x"), lax.axis_index("y"), lax.axis_index("c")
    return x, y, c


def _all_gather(arrs, *, name):
    n = len(arrs)

    def body(*refs):
        ins, outs = refs[:n], refs[n:2 * n]
        send_sems, recv_sems, local_sems = refs[2 * n:]
        x, y, c = _place()
        me, sibling = (x, y, c), (x, y, 1 - c)
        chips = [(1 - x, y), (x, 1 - y), (1 - x, 1 - y)]

        def idx(px, py, pc):
            return 4 * px + 2 * py + pc

        def copy(k, a, block, to, src=None):
            slab = outs[a].at[idx(*block)]
            return pltpu.make_async_remote_copy(
                src_ref=slab if src is None else src, dst_ref=slab,
                send_sem=send_sems.at[k, a], recv_sem=recv_sems.at[k, a], device_id=to, device_id_type=MESH)

        mine = [pltpu.make_async_copy(ins[a], outs[a].at[idx(*me)], local_sems.at[a]) for a in range(n)]
        for cp in mine:
            cp.start()
        first = []
        for a in range(n):
            first.append(copy(0, a, me, sibling, src=ins[a]))
            first += [copy(1 + j, a, me, (*chip, c), src=ins[a]) for j, chip in enumerate(chips)]
        for cp in first:
            cp.start()
        passed = []
        for j, chip in enumerate(chips):
            for a in range(n):
                copy(1 + j, a, (*chip, c), me).wait_recv()
                fwd = copy(4 + j, a, (*chip, c), sibling)
                fwd.start()
                passed.append(fwd)
        for a in range(n):
            copy(0, a, sibling, me).wait_recv()
            for j, chip in enumerate(chips):
                copy(4 + j, a, (*chip, 1 - c), me).wait_recv()
        for cp in first + passed:
            cp.wait_send()
        for cp in mine:
            cp.wait()

    anyspec = pl.BlockSpec(memory_space=pl.ANY)
    return pl.pallas_call(
        body, name=name,
        in_specs=[anyspec] * n, out_specs=[anyspec] * n,
        out_shape=[jax.ShapeDtypeStruct((N_DEV,) + a.shape, a.dtype) for a in arrs],
        scratch_shapes=[pltpu.SemaphoreType.DMA((7, n)), pltpu.SemaphoreType.DMA((7, n)), pltpu.SemaphoreType.DMA((n,))],
    )(*arrs)


W_ROWS = SEG_SSD[0] + SSD_PAD_W


GROUP = 16
INTERIOR = 1920


def _interior(k):
    lo = -(-(k * SHARD_IN) // GROUP) * GROUP
    hi = ((k + 1) * SHARD_IN) // GROUP * GROUP
    return lo, hi


def _dest_row(r):
    return r if r < 6144 else (r - 6144 + SEG_SSD[0] if r < 11296 else r - 11296 + SEG_GATE[0])


def _shard_pieces(k):
    lo_k, hi_k = _interior(k)
    out = []
    for lo, hi in ((0, 6144), (6144, 11296), (11296, W_IN)):
        a, b = max(lo, lo_k), min(hi, hi_k)
        if a < b:
            out.append((a - lo_k, b - a, _dest_row(a)))
    return out


def _patch_straddlers(wpT, heads, tails):
    for k in range(1, N_DEV):
        m = (k * SHARD_IN) % GROUP
        if m:
            group = jnp.concatenate([tails[k - 1, GROUP - m:], heads[k, :GROUP - m]], axis=0)
            wpT = lax.dynamic_update_slice(wpT, group, (_dest_row(k * SHARD_IN - m), 0))
    return wpT


def _gather_weights(win, head, tail, wout, cw, zeros):
    n_zero = zeros.shape[0]
    assert W_IN + n_zero == W_ROWS and W_IN % GROUP == 0
    small_in = (wout, cw, head, tail)

    def run(k, win_ref, wout_ref, cw_ref, head_ref, tail_ref, z_ref, w_out_ref, gout_ref, gcw_ref, ghead_ref, gtail_ref,
            send_sems, recv_sems, local_sems):
        x, y, c = k // 4, (k // 2) % 2, k % 2
        idx = lambda p: 4 * p[0] + 2 * p[1] + p[2]
        me, sib = (x, y, c), (x, y, 1 - c)
        xn, yn, dg = (1 - x, y, c), (x, 1 - y, c), (1 - x, 1 - y, c)
        small = ((wout_ref, gout_ref), (cw_ref, gcw_ref), (head_ref, ghead_ref), (tail_ref, gtail_ref))

        def copies(slot, block, to, own=False):
            kb = idx(block)
            out = []
            for j, (s0, n, d0) in enumerate(_shard_pieces(kb)):
                dst = w_out_ref.at[pl.ds(d0, n)]
                out.append((win_ref.at[pl.ds(s0, n)] if own else dst, dst, j))
            for j, (src, gathered) in enumerate(small):
                out.append((src if own else gathered.at[kb], gathered.at[kb], 2 + j))
            return [pltpu.make_async_remote_copy(src_ref=s, dst_ref=d, send_sem=send_sems.at[slot, j], recv_sem=recv_sems.at[slot, j],
                                                 device_id=to, device_id_type=MESH) for s, d, j in out]

        def start(cps):
            for cp in cps:
                cp.start()
            return cps

        def arrived(slot, block):
            for cp in copies(slot, block, me):
                cp.wait_recv()

        local = [pltpu.make_async_copy(s, d, local_sems.at[j]) for j, (s, d) in enumerate(
            [(win_ref.at[pl.ds(s0, n)], w_out_ref.at[pl.ds(d0, n)]) for s0, n, d0 in _shard_pieces(k)]
            + [(src, gathered.at[k]) for src, gathered in small] + [(z_ref, w_out_ref.at[pl.ds(W_IN, n_zero)])])]
        for cp in local:
            cp.start()
        sent = start(copies(0, me, sib, own=True)) + start(copies(1, me, xn, own=True)) + start(copies(2, me, yn, own=True))
        arrived(1, xn)
        sent += start(copies(4, xn, sib))
        if c == 1:
            sent += start(copies(3, xn, yn))
        arrived(2, yn)
        sent += start(copies(5, yn, sib))
        if c == 0:
            sent += start(copies(3, yn, xn))
        arrived(3, dg)
        sent += start(copies(6, dg, sib))
        arrived(0, sib)
        arrived(4, (1 - x, y, 1 - c))
        arrived(5, (x, 1 - y, 1 - c))
        arrived(6, (1 - x, 1 - y, 1 - c))
        for cp in sent:
            cp.wait_send()
        for cp in local:
            cp.wait()

    def body(*refs):
        x, y, c = _place()
        me = 4 * x + 2 * y + c
        for k in range(N_DEV):
            pl.when(me == k)(functools.partial(run, k, *refs))

    anyspec = pl.BlockSpec(memory_space=pl.ANY)
    n_arr = 2 + len(small_in)
    return pl.pallas_call(
        body, name="gather_weights", in_specs=[anyspec] * 6, out_specs=[anyspec] * 5,
        out_shape=[jax.ShapeDtypeStruct((W_ROWS, D), win.dtype)]
        + [jax.ShapeDtypeStruct((N_DEV,) + a.shape, a.dtype) for a in small_in],
        scratch_shapes=[pltpu.SemaphoreType.DMA((7, n_arr)), pltpu.SemaphoreType.DMA((7, n_arr)),
                        pltpu.SemaphoreType.DMA((n_arr + 1,))],
    )(win, wout, cw, head, tail, zeros)


_REL = [(dx, dy, dc) for dx in (0, 1) for dy in (0, 1) for dc in (0, 1)][1:]
_HBM = pl.BlockSpec(memory_space=pltpu.HBM)
_SEM = pl.BlockSpec(memory_space=pltpu.SEMAPHORE)
_EFFECT = pltpu.SideEffectType.DATAFLOW_SIDE_EFFECTING


def _peer(k):
    x, y, c = _place()
    dx, dy, dc = _REL[k]
    return (1 - x if dx else x, 1 - y if dy else y, 1 - c if dc else c)


def _exchange_start(parts, *, name):
    n = len(parts)

    def body(*refs):
        ins, lands = refs[:n], refs[n:2 * n]
        send_sems, recv_sems, token = refs[2 * n], refs[2 * n + 1], refs[-1]
        x, y, c = _place()
        me = 4 * x + 2 * y + c
        for a in range(n):
            for k in range(len(_REL)):
                px, py, pc = _peer(k)
                pltpu.make_async_remote_copy(
                    src_ref=ins[a].at[4 * px + 2 * py + pc], dst_ref=lands[a].at[me],
                    send_sem=send_sems.at[len(_REL) * a + k], recv_sem=recv_sems.at[len(_REL) * a + k],
                    device_id=(px, py, pc), device_id_type=MESH).start()
        token[...] = jnp.zeros_like(token)

    sem = pltpu.SemaphoreType.DMA((len(_REL) * n,))
    bufs = [pltpu.HBM(p.shape, p.dtype) for p in parts]
    outs = pl.pallas_call(
        body, name=name,
        out_shape=(sem, sem, *bufs, *bufs, jax.ShapeDtypeStruct((8, LANE), F32)),
        in_specs=(_HBM,) * (2 * n), out_specs=(_SEM, _SEM, *(_HBM,) * (2 * n), pl.BlockSpec(memory_space=pltpu.VMEM)),
        input_output_aliases={i: 2 + i for i in range(2 * n)},
        compiler_params=pltpu.CompilerParams(has_side_effects=_EFFECT),
    )(*[pltpu.with_memory_space_constraint(p, pltpu.HBM) for p in parts],
      *[pltpu.with_memory_space_constraint(lax.empty(p.shape, p.dtype), pltpu.HBM) for p in parts])
    return outs[0], outs[1], outs[2:2 + n], outs[2 + n:2 + 2 * n], outs[-1]


def _exchange_wait(send_sems, recv_sems, parts, lands, after, *, name):
    n = len(parts)

    def body(*refs):
        ins, lands_ = refs[:n], refs[n:2 * n]
        ssem, rsem = refs[2 * n], refs[2 * n + 1]
        for a in range(n):
            for k in range(len(_REL)):
                px, py, pc = _peer(k)
                p = 4 * px + 2 * py + pc
                cp = pltpu.make_async_remote_copy(
                    src_ref=ins[a].at[p], dst_ref=lands_[a].at[p],
                    send_sem=ssem.at[len(_REL) * a + k], recv_sem=rsem.at[len(_REL) * a + k],
                    device_id=(px, py, pc), device_id_type=MESH)
                cp.wait_send()
                cp.wait_recv()

    bufs = [pltpu.HBM(p.shape, p.dtype) for p in parts]
    outs = pl.pallas_call(
        body, name=name, out_shape=(*bufs, *bufs),
        in_specs=(*(_HBM,) * (2 * n), _SEM, _SEM, pl.BlockSpec(memory_space=pl.ANY)), out_specs=(_HBM,) * (2 * n),
        input_output_aliases={i: i for i in range(2 * n)},
        compiler_params=pltpu.CompilerParams(has_side_effects=_EFFECT),
    )(*parts, *lands, send_sems, recv_sems, after)
    return outs[:n], outs[n:]


WEIGHTS = ('norm_w', 'w_in', 'gate_b', 'sgu_norm_g', 'sgu_norm_b', 'sgu_w', 'sgu_b', 'conv_w', 'conv_b', 'dt_bias', 'A_log',
           'D_skip', 'ssd_norm_w', 'w_out', 'final_norm_w')
SHARDED = ('w_in', 'conv_w', 'w_out')
PACK_ROW = 8 * LANE


def _constants():
    tri = np.tril(np.ones((CHUNK, CHUNK), np.float32))
    expand = np.zeros((DT_W, D), np.float32)
    for h in range(HEADS):
        expand[h, h * HEADDIM:(h + 1) * HEADDIM] = 1.0
    sel = np.zeros((D, LANE), np.float32)
    for g in range(SGU_GROUPS):
        sel[g * LANE:(g + 1) * LANE, g] = 1.0
    pos_chunk = np.arange(SGU_BLOCK) // CHUNK
    mask = (pos_chunk[None, :] <= pos_chunk[:, None]).astype(np.float32)
    shift = np.zeros(((CONV_K - 1) * CHUNK, HALO_BLK + CHUNK), np.float32)
    for kk in range(CONV_K - 1):
        for t in range(CHUNK):
            shift[kk * CHUNK + t, HALO_BLK - (CONV_K - 1) + t + kk] = 1.0
    return dict(tri=jnp.asarray(tri, BF16), triT=jnp.asarray(tri.T.copy(), BF16), expand=jnp.asarray(expand, BF16),
                shift=jnp.asarray(shift, BF16),
                expandT=jnp.asarray(expand.T.copy(), BF16), sel=jnp.asarray(sel), mask=jnp.asarray(mask))


def _to_shards(segs):
    starts = np.cumsum([0] + [s.shape[0] for s in segs])
    assert starts[-1] == W_IN
    slabs = []
    for k in range(N_DEV):
        pieces = []
        for s, s0 in zip(segs, starts[:-1]):
            lo, hi = max(k * SHARD_IN, s0), min((k + 1) * SHARD_IN, s0 + s.shape[0])
            if lo < hi:
                pieces.append(s[lo - s0:hi - s0])
        slabs.append(jnp.concatenate(pieces, axis=0))
    return jnp.stack(slabs)


def _local_step(x2, tgt, wpT, wout, cw, p, exchange_small, exchange):
    S = x2.shape[0]
    k = _constants()
    xn = _norm_fwd(x2, p['norm_w'], tm=min(512, S))
    proj = _matmul(xn, wpT, trans_b=True, out_dtype=BF16, tm=min(1024, S), tn=2048, tk=D, name="in_proj")
    wm32 = p['sgu_w'][0] * k['mask']
    wm = wm32.astype(BF16)
    wmT = jnp.swapaxes(wm32, 1, 2).astype(BF16)
    bias_full = jnp.repeat(p['sgu_b'][0].T, LANE, axis=1)
    tm_sgu = min(256, S)
    ya = _sgu_fwd(proj, p['sgu_norm_g'], p['sgu_norm_b'], wm, bias_full, tm=tm_sgu)
    pad32 = lambda a: jnp.pad(a, ((0, 0), (0, DT_W - HEADS)))
    dtb_p, alog_p = pad32(p['dt_bias']), pad32(p['A_log'])
    d_exp = jnp.repeat(p['D_skip'], HEADDIM, axis=1)
    ssd_args = (cw, p['conv_b'], dtb_p, alog_p, d_exp, p['ssd_norm_w'])
    y, yb, states = _ssd_fwd(proj, *ssd_args, k['tri'], k['expand'], k['shift'])
    dh, dhb, mb, dya, dyb, dgl, loss, dfw, dgb = _head(
        x2, ya, yb, proj, tgt, p['gate_b'], wout, p['final_norm_w'][None, :], tm=min(256, S))
    dsgu, dws, dbsT, dsg, dsb = _sgu_bwd(proj, dya, p['sgu_norm_g'], p['sgu_norm_b'], wm, wmT, bias_full, k['mask'], k['sel'],
                                         tm=tm_sgu)
    dssd, dcw, dcb, ddtb, dalog, dD, dnw = _ssd_bwd(proj, dyb, y, states, *ssd_args, k['tri'], k['triT'], k['expand'], k['expandT'],
                                                    k['shift'])
    grads = dict(
        gate_b=dgb[0:1], sgu_norm_g=dsg[0:1], sgu_norm_b=dsb[0:1], sgu_w=dws[None],
        sgu_b=dbsT[:, :SGU_GROUPS].T[None], conv_w=dcw[0:CONV_K][None], conv_b=dcb[0:1], dt_bias=ddtb[0:1, :HEADS],
        A_log=dalog[0:1, :HEADS], D_skip=dD[0:1, :HEADS], ssd_norm_w=dnw[0:1], final_norm_w=dfw[0])
    token = exchange_small(loss[0, 0], grads)
    tk = min(4096, S)
    tn = 1024
    dwT_sgu = _matmul(dsgu, xn, trans_a=True, out_dtype=BF16, tm=1024, tn=tn, tk=tk, after=token, name="dw_in_sgu")
    dwT_gate = _matmul(dgl, xn, trans_a=True, out_dtype=BF16, tm=1024, tn=tn, tk=tk, name="dw_in_gate")
    dwT_ssd = _matmul(dssd, xn, trans_a=True, out_dtype=BF16, tm=1024, tn=tn, tk=tk, name="dw_in_ssd")
    dw_out = _matmul(mb, dhb, trans_a=True, out_dtype=BF16, tm=1024, tn=tn, tk=tk, name="dw_out")
    token = exchange([dwT_sgu, dwT_ssd[:W_IN - SEG_SSD[0]], dwT_gate], dw_out)
    tm = min(1024, S)
    dxn = _matmul(dsgu, wpT, tm=tm, tn=tn, tk=3072, after=token, name="dxn_sgu")
    dxn = _matmul(dgl, wpT, b_koff=SEG_GATE[0] // 2048, tm=tm, tn=tn, tk=2048, add=dxn, name="dxn_gate")
    dxn = _matmul(dssd, wpT, b_koff=SEG_SSD[0] // 2048, tm=tm, tn=tn, tk=2048, add=dxn, name="dxn_ssd")
    grad_x, dnorm = _norm_bwd(x2, p['norm_w'], dxn, dh, tm=min(256, S))
    return grad_x, dnorm[0:1]


def _pack(arrs):
    rows, offs, r = [], [], 0
    for a in arrs:
        n = a.size
        nr = -(-n // PACK_ROW) * 8
        rows.append(jnp.pad(a.reshape(-1).astype(F32), (0, nr * LANE - n)).reshape(nr, LANE))
        offs.append(r)
        r += nr
    return jnp.concatenate(rows, axis=0), offs


def kernel(x, norm_w, w_in, gate_b, sgu_norm_g, sgu_norm_b, sgu_w, sgu_b, conv_w, conv_b, dt_bias, A_log, D_skip, ssd_norm_w, w_out, final_norm_w, loss_target, m_norm_w, m_w_in, m_gate_b, m_sgu_norm_g, m_sgu_norm_b, m_sgu_w, m_sgu_b, m_conv_w, m_conv_b, m_dt_bias, m_A_log, m_D_skip, m_ssd_norm_w, m_w_out, m_final_norm_w, v_norm_w, v_w_in, v_gate_b, v_sgu_norm_g, v_sgu_norm_b, v_sgu_w, v_sgu_b, v_conv_w, v_conv_b, v_dt_bias, v_A_log, v_D_skip, v_ssd_norm_w, v_w_out, v_final_norm_w):
    w = dict(norm_w=norm_w, w_in=w_in, gate_b=gate_b, sgu_norm_g=sgu_norm_g, sgu_norm_b=sgu_norm_b, sgu_w=sgu_w, sgu_b=sgu_b,
             conv_w=conv_w, conv_b=conv_b, dt_bias=dt_bias, A_log=A_log, D_skip=D_skip, ssd_norm_w=ssd_norm_w, w_out=w_out,
             final_norm_w=final_norm_w)
    m = dict(norm_w=m_norm_w, w_in=m_w_in, gate_b=m_gate_b, sgu_norm_g=m_sgu_norm_g, sgu_norm_b=m_sgu_norm_b, sgu_w=m_sgu_w,
             sgu_b=m_sgu_b, conv_w=m_conv_w, conv_b=m_conv_b, dt_bias=m_dt_bias, A_log=m_A_log, D_skip=m_D_skip,
             ssd_norm_w=m_ssd_norm_w, w_out=m_w_out, final_norm_w=m_final_norm_w)
    v = dict(norm_w=v_norm_w, w_in=v_w_in, gate_b=v_gate_b, sgu_norm_g=v_sgu_norm_g, sgu_norm_b=v_sgu_norm_b, sgu_w=v_sgu_w,
             sgu_b=v_sgu_b, conv_w=v_conv_w, conv_b=v_conv_b, dt_bias=v_dt_bias, A_log=v_A_log, D_skip=v_D_skip,
             ssd_norm_w=v_ssd_norm_w, w_out=v_w_out, final_norm_w=v_final_norm_w)
    me = 4 * lax.axis_index("x") + 2 * lax.axis_index("y") + lax.axis_index("c")
    shard_cw = XBC_W // N_DEV

    tpose = lambda a: jnp.swapaxes(a[0], 0, 1)
    wT = tpose(w_in).astype(BF16)
    first_group = (GROUP - (me * SHARD_IN) % GROUP) % GROUP
    window = lax.dynamic_slice(jnp.pad(wT, ((0, GROUP), (0, 0))), (first_group, 0), (INTERIOR, D))
    wpT, g_out, g_cw, heads, tails = _gather_weights(window, wT[:GROUP], wT[SHARD_IN - GROUP:], w_out[0].astype(BF16),
                                                     conv_w[0], jnp.zeros((W_ROWS - W_IN, D), BF16))
    wpT = _patch_straddlers(wpT, heads, tails)
    wout_full = g_out.reshape(D, D)
    cw_full = jnp.swapaxes(g_cw, 0, 1).reshape(CONV_K, XBC_W)

    flight = {}

    small = [n for n in WEIGHTS if n not in SHARDED and n != 'norm_w']
    early = {}

    def exchange_small(loss_part, grads):
        early['packed'], early['offs'] = _pack([grads[n] for n in small] + [loss_part, grads['conv_w']])
        parts = [jnp.broadcast_to(early['packed'][None], (N_DEV,) + early['packed'].shape)]
        early['sems'], early['rsems'], early['parts'], early['lands'], token = _exchange_start(parts, name="small_start")
        return token

    def exchange(dw_inT_segs, dw_out):
        parts = [_to_shards(dw_inT_segs), dw_out.reshape(N_DEV, D // N_DEV, D)]
        flight['sems'], flight['rsems'], flight['parts'], flight['lands'], token = _exchange_start(parts, name="exchange_start")
        return token

    grad_x, dnorm = _local_step(x[0], loss_target[0], wpT, wout_full, cw_full, w, exchange_small, exchange)
    _, (land_small,) = _exchange_wait(early['sems'], early['rsems'], early['parts'], early['lands'], grad_x, name="small_wait")
    (own_in, own_out), (land_in, land_out) = _exchange_wait(
        flight['sems'], flight['rsems'], flight['parts'], flight['lands'], grad_x, name="exchange_wait")
    me_arr = jnp.reshape(me, (1,)).astype(jnp.int32)
    res = {}
    res['w_in'] = [jnp.swapaxes(o, 0, 1) for o in _adamw_own(
        me_arr, own_in, land_in, tpose(w_in), tpose(m_w_in), tpose(v_w_in), tr=SHARD_IN, tc=256, name="adamw_w_in")]
    res['w_out'] = _adamw_own(me_arr, own_out, land_out, w_out[0], m_w_out[0], v_w_out[0], tr=128, tc=D, name="adamw_w_out")

    (norm_parts,) = _all_gather([_pack([dnorm])[0]], name="gather_norm")
    norm_outs = _adamw(norm_parts, *[_pack([d['norm_w']])[0] for d in (w, m, v)], tr=norm_parts.shape[1], name="adamw_norm")
    res['norm_w'] = [o.reshape(-1)[:D].reshape(w['norm_w'].shape) for o in norm_outs]

    offs = early['offs']
    gathered = lax.dynamic_update_slice(land_small, early['packed'][None], (me, 0, 0))
    off_loss, off_cw = offs[-2], offs[-1]
    cw_parts = gathered[:, off_cw:, :].reshape(N_DEV, CONV_K, XBC_W)
    cw_parts = lax.dynamic_slice_in_dim(cw_parts, me * shard_cw, shard_cw, axis=2)
    cw_rows = _pack([cw_parts[0]])[0].shape[0]
    cw_parts = jnp.pad(cw_parts.reshape(N_DEV, -1), ((0, 0), (0, cw_rows * LANE - CONV_K * shard_cw))).reshape(N_DEV, cw_rows, LANE)
    parts = jnp.concatenate([gathered[:, :off_cw, :], cw_parts], axis=1)
    zero = jnp.zeros((), F32)
    packs = [_pack([d[n] for n in small] + [zero, d['conv_w']])[0] for d in (w, m, v)]
    outs = _adamw(parts, *packs, tr=parts.shape[1], name="adamw_small")

    def unpack(o, name):
        if name == 'conv_w':
            return o[off_cw:off_cw + cw_rows].reshape(-1)[:CONV_K * shard_cw].reshape(w['conv_w'].shape)
        r0 = offs[small.index(name)]
        n = w[name].size
        return o[r0:r0 + -(-n // PACK_ROW) * 8].reshape(-1)[:n].reshape(w[name].shape)

    for n in small + ['conv_w']:
        res[n] = [unpack(o, n) for o in outs]
    for n in ('w_in', 'w_out'):
        res[n] = [o[None] for o in res[n]]
    loss = outs[0][off_loss, 0]
    return (loss, grad_x[None], *[res[n][0] for n in WEIGHTS], *[res[n][1] for n in WEIGHTS],
            *[res[n][2] for n in WEIGHTS], *[res[n][3] for n in WEIGHTS])
```

```python
import functools

import numpy as np
import jax
import jax.numpy as jnp
from jax import lax
from jax.experimental import pallas as pl
from jax.experimental.pallas import tpu as pltpu

F32 = jnp.float32
BF16 = jnp.bfloat16
HI = lax.Precision.HIGHEST
MESH = pl.DeviceIdType.MESH

D = 2048
EPS = 1e-5
SGU_BLOCK = 128
SGU_GROUPS = 16
CHUNK = 64
HEADS = 32
HEADDIM = 64
SSD_GROUPS = 4
GROUP_W = D // SSD_GROUPS
STATE = 128
CONV_K = 4
XBC_W = D + 2 * SSD_GROUPS * STATE
W_IN = 15392
N_DEV = 8
SHARD_IN = W_IN // N_DEV
ADAM_LR, ADAM_B1, ADAM_B2, ADAM_EPS, ADAM_WD, ADAM_STEP = 0.001, 0.9, 0.999, 1e-08, 0.01, 10

LANE = 128
DT_W = LANE
OFF_U, OFF_V, OFF_ZA, OFF_G0, OFF_G1, OFF_ZB, OFF_XBC, OFF_DT = 0, 2048, 4096, 6144, 8192, 10240, 12288, 15360
WP = OFF_DT + DT_W
SEG_SGU = (0, 6144)
SEG_GATE = (6144, 4096)
SEG_SSD = (10240, WP - 10240)
SSD_PAD_W = 6144
VMEM_LIMIT = 56 * 1024 * 1024


def _cp(sem=None, vmem=VMEM_LIMIT):
    return pltpu.CompilerParams(dimension_semantics=sem, vmem_limit_bytes=vmem)


def _sigmoid(x):
    return 1.0 / (1.0 + jnp.exp(-x))


def _softplus(x):
    return jnp.maximum(x, 0.0) + jnp.log(1.0 + jnp.exp(-jnp.abs(x)))


def _dot(a, b, precision=None):
    return jnp.dot(a, b, preferred_element_type=F32, precision=precision)


def _dot_nt(a, b, precision=None):
    return lax.dot_general(a, b, (((1,), (1,)), ((), ())), preferred_element_type=F32, precision=precision)


def _dot_tn(a, b, precision=None):
    return lax.dot_general(a, b, (((0,), (0,)), ((), ())), preferred_element_type=F32, precision=precision)


def _split3(a):
    hi = a.astype(BF16)
    r = a - hi.astype(F32)
    mid = r.astype(BF16)
    return hi, mid, (r - mid.astype(F32)).astype(BF16)


def _sel_right(a, sel01):
    m = a.shape[0]
    r = _dot(jnp.concatenate(_split3(a), axis=0), sel01)
    return (r[0:m] + r[m:2 * m]) + r[2 * m:3 * m]


def _sel_right_k(a, sel01_x3):
    return _dot(jnp.concatenate(_split3(a), axis=1), sel01_x3)


def _sel_left(sel01, a):
    n = a.shape[1]
    r = _dot(sel01, jnp.concatenate(_split3(a), axis=1))
    return (r[:, 0:n] + r[:, n:2 * n]) + r[:, 2 * n:3 * n]


def _matmul(a, b, *, trans_a=False, trans_b=False, b_koff=0, n=None, out_dtype=F32, tm, tn, tk, add=None, after=None, name):
    K, M = a.shape if trans_a else a.shape[::-1]
    N = (n or b.shape[0]) if trans_b else b.shape[1]
    assert M % tm == 0 and N % tn == 0 and K % tk == 0 and not (trans_a and trans_b), (name, M, N, K, tm, tn, tk)
    nk = K // tk

    def body(*refs):
        a_ref, b_ref = refs[:2]
        add_ref = refs[2] if add is not None else None
        o_ref, acc_ref = refs[-2:]
        k = pl.program_id(2)
        if trans_a:
            part = _dot_tn(a_ref[...], b_ref[...])
        else:
            part = _dot_nt(a_ref[...], b_ref[...]) if trans_b else _dot(a_ref[...], b_ref[...])

        def result(r):
            if add_ref is not None:
                r = r + add_ref[...]
            return r.astype(out_dtype)

        if nk == 1:
            o_ref[...] = result(part)
        else:
            @pl.when(k == 0)
            def _():
                acc_ref[...] = part

            @pl.when(jnp.logical_and(k > 0, k < nk - 1))
            def _():
                acc_ref[...] += part

            @pl.when(k == nk - 1)
            def _():
                o_ref[...] = result(acc_ref[...] + part)

    in_specs = [pl.BlockSpec((tk, tm), lambda i, j, k: (k, i)) if trans_a else pl.BlockSpec((tm, tk), lambda i, j, k: (i, k)),
                pl.BlockSpec((tn, tk), lambda i, j, k: (j, k)) if trans_b else pl.BlockSpec((tk, tn), lambda i, j, k: (k + b_koff, j))]
    args = [a, b]
    if add is not None:
        in_specs.append(pl.BlockSpec((tm, tn), lambda i, j, k: (i, j)))
        args.append(add)
    if after is not None:
        in_specs.append(pl.BlockSpec(memory_space=pl.ANY))
        args.append(after)
    return pl.pallas_call(
        body, name=name, grid=(M // tm, N // tn, nk), in_specs=in_specs,
        out_specs=pl.BlockSpec((tm, tn), lambda i, j, k: (i, j)),
        out_shape=jax.ShapeDtypeStruct((M, N), out_dtype),
        scratch_shapes=[pltpu.VMEM((tm, tn), F32)],
        compiler_params=_cp(("parallel", "parallel", "arbitrary")),
    )(*args)


def _norm_fwd(x, w, *, tm):
    S = x.shape[0]

    def body(x_ref, w_ref, o_ref):
        xv = x_ref[...]
        r = lax.rsqrt(jnp.mean(xv * xv, axis=-1, keepdims=True) + EPS)
        o_ref[...] = (xv * r * w_ref[...]).astype(BF16)

    return pl.pallas_call(
        body, name="norm_fwd", grid=(S // tm,),
        in_specs=[pl.BlockSpec((tm, D), lambda i: (i, 0)), pl.BlockSpec((1, D), lambda i: (0, 0))],
        out_specs=pl.BlockSpec((tm, D), lambda i: (i, 0)),
        out_shape=jax.ShapeDtypeStruct((S, D), BF16), compiler_params=_cp(("parallel",)),
    )(x, w)


def _norm_bwd(x, w, dxn, dh, *, tm):
    S = x.shape[0]

    def body(x_ref, w_ref, dxn_ref, dh_ref, gx_ref, dw_ref):
        xv = x_ref[...]
        r = lax.rsqrt(jnp.mean(xv * xv, axis=-1, keepdims=True) + EPS)
        xh = xv * r
        dxn_v = dxn_ref[...]
        dxh = dxn_v * w_ref[...]
        gx_ref[...] = dh_ref[...] + r * (dxh - xh * jnp.mean(dxh * xh, axis=-1, keepdims=True))

        @pl.when(pl.program_id(0) == 0)
        def _():
            dw_ref[...] = jnp.zeros_like(dw_ref)

        dw_ref[0:1, :] += jnp.sum(dxn_v * xh, axis=0, keepdims=True)

    row = pl.BlockSpec((tm, D), lambda i: (i, 0))
    return pl.pallas_call(
        body, name="norm_bwd", grid=(S // tm,),
        in_specs=[row, pl.BlockSpec((1, D), lambda i: (0, 0)), row, row],
        out_specs=[row, pl.BlockSpec((8, D), lambda i: (0, 0))],
        out_shape=[jax.ShapeDtypeStruct((S, D), F32), jax.ShapeDtypeStruct((8, D), F32)],
        compiler_params=_cp(("arbitrary",)),
    )(x, w, dxn, dh)


def _sgu_core(u_ref, v_ref, z_ref, g_ref, b_ref, wm_ref, bias_ref, vnb_ref, mixed_ref, tm):
    v = v_ref[...].astype(F32)
    mu = jnp.mean(v, axis=-1, keepdims=True)
    vc = v - mu
    rs = lax.rsqrt(jnp.mean(vc * vc, axis=-1, keepdims=True) + EPS)
    vh = vc * rs
    vnb_ref[...] = (vh * g_ref[...] + b_ref[...]).astype(BF16)
    for blk in range(tm // SGU_BLOCK):
        rows = pl.ds(blk * SGU_BLOCK, SGU_BLOCK)
        for gi in range(SGU_GROUPS):
            cols = pl.ds(gi * LANE, LANE)
            mixed_ref[rows, cols] = _dot(wm_ref[gi], vnb_ref[rows, cols]) + bias_ref[:, cols]
    return vh, rs


def _sgu_fwd(proj, g, b, wm, bias_full, *, tm):
    S = proj.shape[0]

    def body(u_ref, v_ref, z_ref, g_ref, b_ref, wm_ref, bias_ref, y_ref, vnb_ref, mixed_ref):
        _sgu_core(u_ref, v_ref, z_ref, g_ref, b_ref, wm_ref, bias_ref, vnb_ref, mixed_ref, tm)
        z = z_ref[...].astype(F32)
        y_ref[...] = (u_ref[...].astype(F32) * mixed_ref[...] * (z * _sigmoid(z))).astype(BF16)

    seg = lambda off: pl.BlockSpec((tm, D), lambda i: (i, off // D))
    full = lambda a: pl.BlockSpec(a.shape, lambda i: (0,) * a.ndim)
    return pl.pallas_call(
        body, name="sgu_fwd", grid=(S // tm,),
        in_specs=[seg(OFF_U), seg(OFF_V), seg(OFF_ZA), full(g), full(b), full(wm), full(bias_full)],
        out_specs=pl.BlockSpec((tm, D), lambda i: (i, 0)),
        out_shape=jax.ShapeDtypeStruct((S, D), BF16),
        scratch_shapes=[pltpu.VMEM((tm, D), BF16), pltpu.VMEM((tm, D), F32)],
        compiler_params=_cp(("parallel",)),
    )(proj, proj, proj, g, b, wm, bias_full)


def _sgu_bwd(proj, dy, g, b, wm, wmT, bias_full, mask, sel, *, tm):
    S = proj.shape[0]
    nsteps = S // tm

    def body(u_ref, v_ref, z_ref, dy_ref, g_ref, b_ref, wm_ref, wmT_ref, bias_ref, mask_ref, sel_ref,
             dp_ref, dws_ref, dbs_ref, dg_ref, db_ref, vnb_ref, mixed_ref, dmb_ref, dvn_ref, dbias_ref):
        i = pl.program_id(0)

        @pl.when(i == 0)
        def _():
            dws_ref[...] = jnp.zeros_like(dws_ref)
            dg_ref[...] = jnp.zeros_like(dg_ref)
            db_ref[...] = jnp.zeros_like(db_ref)
            dbias_ref[...] = jnp.zeros_like(dbias_ref)

        vh, rs = _sgu_core(u_ref, v_ref, z_ref, g_ref, b_ref, wm_ref, bias_ref, vnb_ref, mixed_ref, tm)
        u = u_ref[...].astype(F32)
        z = z_ref[...].astype(F32)
        dy_v = dy_ref[...].astype(F32)
        mixed = mixed_ref[...]
        sg = _sigmoid(z)
        sz = z * sg
        dp_ref[:, 0:D] = (dy_v * mixed * sz).astype(BF16)
        dp_ref[:, 2 * D:3 * D] = (dy_v * u * mixed * (sg * (1.0 + z * (1.0 - sg)))).astype(BF16)
        dmixed = dy_v * u * sz
        dmb_ref[...] = dmixed.astype(BF16)
        for blk in range(tm // SGU_BLOCK):
            dbias_ref[...] += dmixed[blk * SGU_BLOCK:(blk + 1) * SGU_BLOCK, :]
        for blk in range(tm // SGU_BLOCK):
            rows = pl.ds(blk * SGU_BLOCK, SGU_BLOCK)
            for gi in range(SGU_GROUPS):
                cols = pl.ds(gi * LANE, LANE)
                dm = dmb_ref[rows, cols]
                dvn_ref[rows, cols] = _dot(wmT_ref[gi], dm)
                dws_ref[gi] += _dot_nt(dm, vnb_ref[rows, cols])
        dvn = dvn_ref[...]
        dg_ref[0:1, :] += jnp.sum(dvn * vh, axis=0, keepdims=True)
        db_ref[0:1, :] += jnp.sum(dvn, axis=0, keepdims=True)
        dvh = dvn * g_ref[...]
        dv = rs * (dvh - jnp.mean(dvh, axis=-1, keepdims=True) - vh * jnp.mean(dvh * vh, axis=-1, keepdims=True))
        dp_ref[:, D:2 * D] = dv.astype(BF16)

        @pl.when(i == nsteps - 1)
        def _():
            for gi in range(SGU_GROUPS):
                dws_ref[gi] = dws_ref[gi] * mask_ref[...]
            dbs_ref[...] = _dot(dbias_ref[...], sel_ref[...], precision=HI)

    seg = lambda off: pl.BlockSpec((tm, D), lambda i: (i, off // D))
    full = lambda a: pl.BlockSpec(a.shape, lambda i: (0,) * a.ndim)
    return pl.pallas_call(
        body, name="sgu_bwd", grid=(nsteps,),
        in_specs=[seg(OFF_U), seg(OFF_V), seg(OFF_ZA), pl.BlockSpec((tm, D), lambda i: (i, 0)),
                  full(g), full(b), full(wm), full(wmT), full(bias_full), full(mask), full(sel)],
        out_specs=[pl.BlockSpec((tm, 3 * D), lambda i: (i, 0)),
                   pl.BlockSpec((SGU_GROUPS, SGU_BLOCK, SGU_BLOCK), lambda i: (0, 0, 0)),
                   pl.BlockSpec((SGU_BLOCK, LANE), lambda i: (0, 0)),
                   pl.BlockSpec((8, D), lambda i: (0, 0)), pl.BlockSpec((8, D), lambda i: (0, 0))],
        out_shape=[jax.ShapeDtypeStruct((S, 3 * D), BF16),
                   jax.ShapeDtypeStruct((SGU_GROUPS, SGU_BLOCK, SGU_BLOCK), F32),
                   jax.ShapeDtypeStruct((SGU_BLOCK, LANE), F32),
                   jax.ShapeDtypeStruct((8, D), F32), jax.ShapeDtypeStruct((8, D), F32)],
        scratch_shapes=[pltpu.VMEM((tm, D), BF16), pltpu.VMEM((tm, D), F32), pltpu.VMEM((tm, D), BF16),
                        pltpu.VMEM((tm, D), F32), pltpu.VMEM((SGU_BLOCK, D), F32)],
        compiler_params=_cp(("arbitrary",)),
    )(proj, proj, proj, dy, g, b, wm, wmT, bias_full, mask, sel)


SSD_T = 2 * CHUNK
HALO = 8
HALO_BLK = 16


def _pair_masks():
    row = lax.broadcasted_iota(jnp.int32, (CHUNK, LANE), 0)
    lane = lax.broadcasted_iota(jnp.int32, (CHUNK, LANE), 1)
    pos = jnp.where(lane >= CHUNK, lane - CHUNK, lane)
    diag = (row == pos).astype(F32)
    causal = row >= pos
    lo = (lane < CHUNK).astype(F32)
    return diag, causal, lo, 1.0 - lo


def _ssd_chunk_fwd(c, ext_ref, shift_ref, dt_ref, cw_ref, cb_ref, dtb_ref, alog_ref, tri_ref, exp_ref):
    r0 = c * CHUNK
    win = ext_ref[pl.ds(r0, HALO_BLK + CHUNK), :]
    sh = _dot(shift_ref[...], win)
    taps = [sh[k * CHUNK:(k + 1) * CHUNK] for k in range(CONV_K - 1)] + [win[HALO_BLK:].astype(F32)]
    pre = cb_ref[...] + sum(cw_ref[k:k + 1, :] * taps[k] for k in range(CONV_K))
    sg = _sigmoid(pre)
    xc = pre * sg
    dtr = dt_ref[pl.ds(r0, CHUNK), :].astype(F32) + dtb_ref[...]
    dtv = _softplus(dtr)
    A = -jnp.exp(alog_ref[...])
    acs = _sel_left(tri_ref[...], dtv * A)
    both = _sel_right_k(jnp.concatenate([acs, dtv], axis=0), exp_ref[...])
    E, dtE = both[0:CHUNK], both[CHUNK:2 * CHUNK]
    return dict(taps=taps, pre=pre, sg=sg, xc=xc, dtr=dtr, dtv=dtv, A=A, E=E, dtE=dtE)


def _ssd_fwd(proj, conv_w, conv_b, dtb_p, alog_p, d_exp, norm_w, tri, expand, shift):
    S = proj.shape[0]
    T = SSD_T
    nsteps = S // T
    ncl = T // CHUNK

    def body(zb_ref, xbc_ref, halo_ref, dt_ref, cw_ref, cb_ref, dtb_ref, alog_ref, dexp_ref, nw_ref, tri_ref, exp_ref, shift_ref,
             y_ref, yb_ref, st_ref, ht_ref, ext_ref):
        i = pl.program_id(0)

        @pl.when(i == 0)
        def _():
            ht_ref[...] = jnp.zeros_like(ht_ref)
            ext_ref[0:HALO_BLK, :] = jnp.zeros((HALO_BLK, XBC_W), BF16)

        @pl.when(i > 0)
        def _():
            ext_ref[0:HALO_BLK, :] = halo_ref[...]

        ext_ref[HALO_BLK:HALO_BLK + T, :] = xbc_ref[...]
        diag, causal, lo, hi = _pair_masks()
        for c in range(ncl):
            q = _ssd_chunk_fwd(c, ext_ref, shift_ref, dt_ref, cw_ref, cb_ref, dtb_ref, alog_ref, tri_ref, exp_ref)
            rows = pl.ds(c * CHUNK, CHUNK)
            xc, E, dtE = q["xc"], q["E"], q["dtE"]
            xs = xc[:, 0:D]
            total = E[CHUNK - 1:CHUNK, :]
            x_dt = xs * dtE
            eE = jnp.exp(E)
            xw = x_dt * jnp.exp(total - E)
            st_ref[c] = ht_ref[...]
            for g in range(SSD_GROUPS):
                gc = slice(g * GROUP_W, (g + 1) * GROUP_W)
                Bg = xc[:, D + g * STATE:D + (g + 1) * STATE].astype(BF16)
                Cg = xc[:, D + SSD_GROUPS * STATE + g * STATE:D + SSD_GROUPS * STATE + (g + 1) * STATE].astype(BF16)
                cb2 = _dot_nt(Cg, jnp.concatenate([Bg, Bg], axis=0))
                htg = ht_ref[:, gc]
                y_ref[rows, gc] = eE[:, gc] * _dot(Cg, htg.astype(BF16)) + xs[:, gc] * dexp_ref[:, gc]
                for jj in range(GROUP_W // LANE):
                    pc = slice(g * GROUP_W + jj * LANE, g * GROUP_W + (jj + 1) * LANE)
                    Ej = E[:, pc]
                    e2 = jnp.sum(Ej * diag, axis=0, keepdims=True)
                    Mp = cb2 * jnp.exp(jnp.where(causal, Ej - e2, -1e30))
                    xj = x_dt[:, pc]
                    xbd = jnp.concatenate([xj * lo, xj * hi], axis=0).astype(BF16)
                    y_ref[rows, pc] += _dot(Mp.astype(BF16), xbd)
                ht_ref[:, gc] = jnp.exp(total[:, gc]) * htg + _dot_tn(Bg, xw[:, gc].astype(BF16))
            zb = zb_ref[rows, :].astype(F32)
            hh = y_ref[rows, :] * (zb * _sigmoid(zb))
            for g in range(SSD_GROUPS):
                gc = slice(g * GROUP_W, (g + 1) * GROUP_W)
                hg = hh[:, gc]
                r = lax.rsqrt(jnp.mean(hg * hg, axis=-1, keepdims=True) + EPS)
                yb_ref[rows, gc] = (hg * r * nw_ref[:, gc]).astype(BF16)

    full = lambda a: pl.BlockSpec(a.shape, lambda i: (0,) * a.ndim)
    hb = T // HALO_BLK
    return pl.pallas_call(
        body, name="ssd_fwd", grid=(nsteps,),
        in_specs=[pl.BlockSpec((T, D), lambda i: (i, OFF_ZB // D)),
                  pl.BlockSpec((T, XBC_W), lambda i: (i, OFF_XBC // XBC_W)),
                  pl.BlockSpec((HALO_BLK, XBC_W), lambda i: (jnp.maximum(i * hb - 1, 0), OFF_XBC // XBC_W)),
                  pl.BlockSpec((T, DT_W), lambda i: (i, OFF_DT // DT_W)),
                  full(conv_w), full(conv_b), full(dtb_p), full(alog_p), full(d_exp), full(norm_w), full(tri), full(expand),
                  full(shift)],
        out_specs=[pl.BlockSpec((T, D), lambda i: (i, 0)), pl.BlockSpec((T, D), lambda i: (i, 0)),
                   pl.BlockSpec((ncl, STATE, D), lambda i: (i, 0, 0))],
        out_shape=[jax.ShapeDtypeStruct((S, D), F32), jax.ShapeDtypeStruct((S, D), BF16),
                   jax.ShapeDtypeStruct((S // CHUNK, STATE, D), F32)],
        scratch_shapes=[pltpu.VMEM((STATE, D), F32), pltpu.VMEM((HALO_BLK + T, XBC_W), BF16)],
        compiler_params=_cp(("arbitrary",)),
    )(proj, proj, proj, proj, conv_w, conv_b, dtb_p, alog_p, d_exp, norm_w, tri, expand, shift)


def _ssd_bwd(proj, dyb, y, states, conv_w, conv_b, dtb_p, alog_p, d_exp, norm_w, tri, triT, expand, expandT, shift):
    S = proj.shape[0]
    T = SSD_T
    nsteps = S // T
    ncl = T // CHUNK
    SSD_W = SSD_PAD_W

    def body(zb_ref, xbc_ref, halo_ref, dt_ref, dyb_ref, y_ref, st_ref, cw_ref, cb_ref, dtb_ref, alog_ref, dexp_ref, nw_ref,
             tri_ref, triT_ref, exp_ref, expT_ref, shift_ref,
             dp_ref, dcw_ref, dcb_ref, ddtb_ref, dalog_ref, dD_ref, dnw_ref,
             dht_ref, ext_ref, dpre_ref, dy_s, dE_s, dxdt_s, dxc_s, dDacc_ref, dAacc_ref):
        i = pl.program_id(0)

        @pl.when(i == 0)
        def _():
            for r in (dht_ref, dcw_ref, dcb_ref, ddtb_ref, dnw_ref, dDacc_ref, dAacc_ref):
                r[...] = jnp.zeros_like(r)
            dpre_ref[T:T + HALO_BLK, :] = jnp.zeros((HALO_BLK, XBC_W), F32)

        @pl.when(i == nsteps - 1)
        def _():
            ext_ref[0:HALO_BLK, :] = jnp.zeros((HALO_BLK, XBC_W), BF16)

        @pl.when(i < nsteps - 1)
        def _():
            ext_ref[0:HALO_BLK, :] = halo_ref[...]

        ext_ref[HALO_BLK:HALO_BLK + T, :] = xbc_ref[...]
        diag, causal, lo, hi = _pair_masks()
        last_row = (lax.broadcasted_iota(jnp.int32, (CHUNK, 1), 0) == CHUNK - 1).astype(F32)
        for c in reversed(range(ncl)):
            q = _ssd_chunk_fwd(c, ext_ref, shift_ref, dt_ref, cw_ref, cb_ref, dtb_ref, alog_ref, tri_ref, exp_ref)
            rows = pl.ds(c * CHUNK, CHUNK)
            pre, sg, xc, dtr, dtv, A, E, dtE = (q[k] for k in ("pre", "sg", "xc", "dtr", "dtv", "A", "E", "dtE"))
            xs = xc[:, 0:D]
            total = E[CHUNK - 1:CHUNK, :]
            x_dt = xs * dtE
            eE = jnp.exp(E)
            wdec = jnp.exp(total - E)
            zb = zb_ref[rows, :].astype(F32)
            yv = y_ref[rows, :]
            sgz = _sigmoid(zb)
            sz = zb * sgz
            hh = yv * sz
            for g in range(SSD_GROUPS):
                gc = slice(g * GROUP_W, (g + 1) * GROUP_W)
                hg = hh[:, gc]
                r = lax.rsqrt(jnp.mean(hg * hg, axis=-1, keepdims=True) + EPS)
                dyb_g = dyb_ref[rows, gc].astype(F32)
                dn = dyb_g * nw_ref[:, gc]
                dnw_ref[0:1, gc] += jnp.sum(dyb_g * hg * r, axis=0, keepdims=True)
                dy_s[:, gc] = r * dn - hg * (r * r * r) * jnp.mean(dn * hg, axis=-1, keepdims=True)
            dhh = dy_s[...]
            dp_ref[rows, 0:D] = (dhh * yv * (sgz * (1.0 + zb * (1.0 - sgz)))).astype(BF16)
            dy = dhh * sz
            dy_s[...] = dy
            dDacc_ref[0:1, :] += jnp.sum(dy * xs, axis=0, keepdims=True)
            dxc_s[:, 0:D] = dy * dexp_ref[...]
            for g in range(SSD_GROUPS):
                gc = slice(g * GROUP_W, (g + 1) * GROUP_W)
                bcol = slice(D + g * STATE, D + (g + 1) * STATE)
                ccol = slice(D + SSD_GROUPS * STATE + g * STATE, D + SSD_GROUPS * STATE + (g + 1) * STATE)
                Bg = xc[:, bcol].astype(BF16)
                Cg = xc[:, ccol].astype(BF16)
                B2 = jnp.concatenate([Bg, Bg], axis=0)
                cb2 = _dot_nt(Cg, B2)
                htg = st_ref[c, :, gc]
                htb = htg.astype(BF16)
                dhn = dht_ref[:, gc]
                dhnb = dhn.astype(BF16)
                dyg = dy[:, gc]
                eEg = eE[:, gc]
                wg = wdec[:, gc]
                xdg = x_dt[:, gc]
                CH = _dot(Cg, htb)
                dCHb = (dyg * eEg).astype(BF16)
                dC = _dot_nt(dCHb, htb)
                dl = jnp.exp(total[:, gc])
                dht_prev = _dot_tn(Cg, dCHb) + dl * dhn
                dtot = jnp.sum(dhn * htg, axis=0, keepdims=True) * dl
                dxw = _dot(Bg, dhnb)
                dB = _dot_nt((xdg * wg).astype(BF16), dhnb)
                dwd = dxw * xdg * wg
                dtot = dtot + jnp.sum(dwd, axis=0, keepdims=True)
                dE_s[:, gc] = dyg * eEg * CH - dwd + last_row * dtot
                dxdt_s[:, gc] = dxw * wg
                dcb2 = jnp.zeros((CHUNK, LANE), F32)
                for jj in range(GROUP_W // LANE):
                    pc = slice(g * GROUP_W + jj * LANE, g * GROUP_W + (jj + 1) * LANE)
                    Ej = E[:, pc]
                    e2 = jnp.sum(Ej * diag, axis=0, keepdims=True)
                    Lp = jnp.exp(jnp.where(causal, Ej - e2, -1e30))
                    Mp = cb2 * Lp
                    xj = x_dt[:, pc]
                    xbd = jnp.concatenate([xj * lo, xj * hi], axis=0).astype(BF16)
                    dyj = dy[:, pc].astype(BF16)
                    dMp = _dot_nt(dyj, xbd)
                    dxbd = _dot_tn(Mp.astype(BF16), dyj)
                    dxdt_s[:, pc] += dxbd[0:CHUNK, :] * lo + dxbd[CHUNK:2 * CHUNK, :] * hi
                    dcb2 = dcb2 + dMp * Lp
                    dseg = dMp * Mp
                    dE_s[:, pc] += dseg - diag * jnp.sum(dseg, axis=0, keepdims=True)
                dcb2b = dcb2.astype(BF16)
                dC = dC + _dot(dcb2b, B2)
                dB2 = _dot_tn(dcb2b, Cg)
                dB = dB + dB2[0:CHUNK, :] + dB2[CHUNK:2 * CHUNK, :]
                dxc_s[:, bcol] = dB
                dxc_s[:, ccol] = dC
                dht_ref[:, gc] = dht_prev
            dx_dt = dxdt_s[...]
            dxc_s[:, 0:D] += dx_dt * dtE
            red = _sel_right(jnp.concatenate([dE_s[...], dx_dt * xs], axis=0), expT_ref[...])
            da = _sel_left(triT_ref[...], red[0:CHUNK, :])
            ddtv = red[CHUNK:2 * CHUNK, :] + da * A
            dAacc_ref[0:1, :] += jnp.sum(da * dtv, axis=0, keepdims=True)
            ddtr = ddtv * _sigmoid(dtr)
            ddtb_ref[0:1, :] += jnp.sum(ddtr, axis=0, keepdims=True)
            dp_ref[rows, D + XBC_W:D + XBC_W + DT_W] = ddtr.astype(BF16)
            dpre = dxc_s[...] * (sg * (1.0 + pre * (1.0 - sg)))
            dpre_ref[rows, :] = dpre
            dcb_ref[0:1, :] += jnp.sum(dpre, axis=0, keepdims=True)
            for k in range(CONV_K):
                dcw_ref[k:k + 1, :] += jnp.sum(dpre * q["taps"][k], axis=0, keepdims=True)
        dxbc = jnp.zeros((T, XBC_W), F32)
        for k in range(CONV_K):
            dxbc = dxbc + cw_ref[k:k + 1, :] * dpre_ref[pl.ds(CONV_K - 1 - k, T), :]
        dp_ref[:, D:D + XBC_W] = dxbc.astype(BF16)
        dp_ref[:, SEG_SSD[1]:SSD_W] = jnp.zeros((T, SSD_W - SEG_SSD[1]), BF16)
        dpre_ref[T:T + HALO, :] = dpre_ref[0:HALO, :]

        @pl.when(i == nsteps - 1)
        def _():
            dalog_ref[...] = dAacc_ref[...] * (-jnp.exp(alog_ref[...]))
            dD_ref[...] = _dot(dDacc_ref[...], expT_ref[...].astype(F32), precision=HI)

    full = lambda a: pl.BlockSpec(a.shape, lambda i: (0,) * a.ndim)
    hb = T // HALO_BLK
    rev = lambda i: nsteps - 1 - i
    acc = lambda w: pl.BlockSpec((8, w), lambda i: (0, 0))
    return pl.pallas_call(
        body, name="ssd_bwd", grid=(nsteps,),
        in_specs=[pl.BlockSpec((T, D), lambda i: (rev(i), OFF_ZB // D)),
                  pl.BlockSpec((T, XBC_W), lambda i: (rev(i), OFF_XBC // XBC_W)),
                  pl.BlockSpec((HALO_BLK, XBC_W), lambda i: (jnp.maximum(rev(i) * hb - 1, 0), OFF_XBC // XBC_W)),
                  pl.BlockSpec((T, DT_W), lambda i: (rev(i), OFF_DT // DT_W)),
                  pl.BlockSpec((T, D), lambda i: (rev(i), 0)), pl.BlockSpec((T, D), lambda i: (rev(i), 0)),
                  pl.BlockSpec((ncl, STATE, D), lambda i: (rev(i), 0, 0)),
                  full(conv_w), full(conv_b), full(dtb_p), full(alog_p), full(d_exp), full(norm_w),
                  full(tri), full(triT), full(expand), full(expandT), full(shift)],
        out_specs=[pl.BlockSpec((T, SSD_W), lambda i: (rev(i), 0)),
                   acc(XBC_W), acc(XBC_W), acc(DT_W), acc(DT_W), acc(DT_W), acc(D)],
        out_shape=[jax.ShapeDtypeStruct((S, SSD_W), BF16),
                   jax.ShapeDtypeStruct((8, XBC_W), F32), jax.ShapeDtypeStruct((8, XBC_W), F32),
                   jax.ShapeDtypeStruct((8, DT_W), F32), jax.ShapeDtypeStruct((8, DT_W), F32),
                   jax.ShapeDtypeStruct((8, DT_W), F32), jax.ShapeDtypeStruct((8, D), F32)],
        scratch_shapes=[pltpu.VMEM((STATE, D), F32), pltpu.VMEM((HALO_BLK + T, XBC_W), BF16), pltpu.VMEM((T + HALO_BLK, XBC_W), F32),
                        pltpu.VMEM((CHUNK, D), F32), pltpu.VMEM((CHUNK, D), F32), pltpu.VMEM((CHUNK, D), F32),
                        pltpu.VMEM((CHUNK, XBC_W), F32), pltpu.VMEM((8, D), F32), pltpu.VMEM((8, DT_W), F32)],
        compiler_params=_cp(("arbitrary",)),
    )(proj, proj, proj, proj, dyb, y, states, conv_w, conv_b, dtb_p, alog_p, d_exp, norm_w, tri, triT, expand, expandT, shift)


def _head(x, ya, yb, proj, target, gate_b, wout, fw, *, tm):
    S = x.shape[0]

    def body(x_ref, ya_ref, yb_ref, gl0_ref, gl1_ref, t_ref, gb_ref, w_ref, fw_ref,
             dh_ref, dhb_ref, mb_ref, dya_ref, dyb_ref, dgl_ref, loss_ref, dfw_ref, dgb_ref):
        @pl.when(pl.program_id(0) == 0)
        def _():
            loss_ref[...] = jnp.zeros_like(loss_ref)
            dfw_ref[...] = jnp.zeros_like(dfw_ref)
            dgb_ref[...] = jnp.zeros_like(dgb_ref)

        ya_v = ya_ref[...].astype(F32)
        yb_v = yb_ref[...].astype(F32)
        g0 = _sigmoid(gl0_ref[...].astype(F32) + gb_ref[:, 0:D])
        g1 = _sigmoid(gl1_ref[...].astype(F32) + gb_ref[:, D:2 * D])
        mb = (g0 * ya_v + g1 * yb_v).astype(BF16)
        mb_ref[...] = mb
        h = x_ref[...] + _dot(mb, w_ref[...])
        r = lax.rsqrt(jnp.mean(h * h, axis=-1, keepdims=True) + EPS)
        hn = h * r
        err = hn * fw_ref[...] - t_ref[...]
        loss_ref[...] += 0.5 * jnp.sum(jnp.mean(err * err, axis=-1, keepdims=True))
        dyf = err * (1.0 / D)
        dfw_ref[0:1, :] += jnp.sum(dyf * hn, axis=0, keepdims=True)
        dhn = dyf * fw_ref[...]
        dh = r * (dhn - hn * jnp.mean(dhn * hn, axis=-1, keepdims=True))
        dh_ref[...] = dh
        dhb = dh.astype(BF16)
        dhb_ref[...] = dhb
        dm = _dot_nt(dhb, w_ref[...])
        dya_ref[...] = (dm * g0).astype(BF16)
        dyb_ref[...] = (dm * g1).astype(BF16)
        dgl0 = dm * ya_v * g0 * (1.0 - g0)
        dgl1 = dm * yb_v * g1 * (1.0 - g1)
        dgl_ref[:, 0:D] = dgl0.astype(BF16)
        dgl_ref[:, D:2 * D] = dgl1.astype(BF16)
        dgb_ref[0:1, 0:D] += jnp.sum(dgl0, axis=0, keepdims=True)
        dgb_ref[0:1, D:2 * D] += jnp.sum(dgl1, axis=0, keepdims=True)

    row = pl.BlockSpec((tm, D), lambda i: (i, 0))
    seg = lambda off: pl.BlockSpec((tm, D), lambda i: (i, off // D))
    full = lambda a: pl.BlockSpec(a.shape, lambda i: (0,) * a.ndim)
    acc = lambda w: pl.BlockSpec((8, w), lambda i: (0, 0))
    return pl.pallas_call(
        body, name="head", grid=(S // tm,),
        in_specs=[row, row, row, seg(OFF_G0), seg(OFF_G1), row, full(gate_b), full(wout), full(fw)],
        out_specs=[row, row, row, row, row, pl.BlockSpec((tm, 2 * D), lambda i: (i, 0)), acc(LANE), acc(D), acc(2 * D)],
        out_shape=[jax.ShapeDtypeStruct((S, D), F32), jax.ShapeDtypeStruct((S, D), BF16), jax.ShapeDtypeStruct((S, D), BF16),
                   jax.ShapeDtypeStruct((S, D), BF16), jax.ShapeDtypeStruct((S, D), BF16), jax.ShapeDtypeStruct((S, 2 * D), BF16),
                   jax.ShapeDtypeStruct((8, LANE), F32), jax.ShapeDtypeStruct((8, D), F32), jax.ShapeDtypeStruct((8, 2 * D), F32)],
        compiler_params=_cp(("arbitrary",)),
    )(x, ya, yb, proj, proj, target, gate_b, wout, fw)


def _adam_update(g, w_ref, m_ref, v_ref, g_ref, d_ref, m2_ref, v2_ref):
    m2 = ADAM_B1 * m_ref[...] + (1.0 - ADAM_B1) * g
    v2 = ADAM_B2 * v_ref[...] + (1.0 - ADAM_B2) * (g * g)
    m_hat = m2 / (1.0 - ADAM_B1 ** ADAM_STEP)
    v_hat = v2 / (1.0 - ADAM_B2 ** ADAM_STEP)
    g_ref[...] = g
    d_ref[...] = -ADAM_LR * (m_hat / (jnp.sqrt(v_hat) + ADAM_EPS) + ADAM_WD * w_ref[...])
    m2_ref[...] = m2
    v2_ref[...] = v2


def _adamw_own(me, own, landed, w, m, v, *, tr, tc, name):
    _, R, C = landed.shape
    assert R % tr == 0 and C % tc == 0, (name, R, C, tr, tc)

    def body(me_ref, own_ref, p_ref, w_ref, m_ref, v_ref, g_ref, d_ref, m2_ref, v2_ref):
        mine = own_ref[0].astype(F32)
        g = jnp.where(me_ref[0] == 0, mine, p_ref[0].astype(F32))
        for k in range(1, N_DEV):
            g = g + jnp.where(me_ref[0] == k, mine, p_ref[k].astype(F32))
        _adam_update(g, w_ref, m_ref, v_ref, g_ref, d_ref, m2_ref, v2_ref)

    tile = pl.BlockSpec((tr, tc), lambda i, j, me_ref: (i, j))
    return pl.pallas_call(
        body, name=name,
        grid_spec=pltpu.PrefetchScalarGridSpec(
            num_scalar_prefetch=1, grid=(R // tr, C // tc),
            in_specs=[pl.BlockSpec((1, tr, tc), lambda i, j, me_ref: (me_ref[0], i, j)),
                      pl.BlockSpec((N_DEV, tr, tc), lambda i, j, me_ref: (0, i, j)), tile, tile, tile],
            out_specs=[tile, tile, tile, tile]),
        out_shape=[jax.ShapeDtypeStruct((R, C), F32)] * 4,
        compiler_params=_cp(("parallel", "parallel")),
    )(me, own, landed, w, m, v)


def _adamw(parts, w, m, v, *, tr, name):
    _, R, C = parts.shape
    assert R % tr == 0, (name, R, tr)

    def body(p_ref, w_ref, m_ref, v_ref, g_ref, d_ref, m2_ref, v2_ref):
        g = p_ref[0].astype(F32)
        for k in range(1, N_DEV):
            g = g + p_ref[k].astype(F32)
        _adam_update(g, w_ref, m_ref, v_ref, g_ref, d_ref, m2_ref, v2_ref)

    row = pl.BlockSpec((tr, C), lambda i: (i, 0))
    return pl.pallas_call(
        body, name=name, grid=(R // tr,),
        in_specs=[pl.BlockSpec((N_DEV, tr, C), lambda i: (0, i, 0)), row, row, row],
        out_specs=[row, row, row, row],
        out_shape=[jax.ShapeDtypeStruct((R, C), F32)] * 4,
        compiler_params=_cp(("parallel",)),
    )(parts, w, m, v)


def _place():
    x, y, c = lax.axis_index("x"), lax.axis_index("y"), lax.axis_index("c")
    return x, y, c


def _all_gather(arrs, *, name):
    n = len(arrs)

    def body(*refs):
        ins, outs = refs[:n], refs[n:2 * n]
        send_sems, recv_sems, local_sems = refs[2 * n:]
        x, y, c = _place()
        me, sibling = (x, y, c), (x, y, 1 - c)
        chips = [(1 - x, y), (x, 1 - y), (1 - x, 1 - y)]

        def idx(px, py, pc):
            return 4 * px + 2 * py + pc

        def copy(k, a, block, to, src=None):
            slab = outs[a].at[idx(*block)]
            return pltpu.make_async_remote_copy(
                src_ref=slab if src is None else src, dst_ref=slab,
                send_sem=send_sems.at[k, a], recv_sem=recv_sems.at[k, a], device_id=to, device_id_type=MESH)

        mine = [pltpu.make_async_copy(ins[a], outs[a].at[idx(*me)], local_sems.at[a]) for a in range(n)]
        for cp in mine:
            cp.start()
        first = []
        for a in range(n):
            first.append(copy(0, a, me, sibling, src=ins[a]))
            first += [copy(1 + j, a, me, (*chip, c), src=ins[a]) for j, chip in enumerate(chips)]
        for cp in first:
            cp.start()
        passed = []
        for j, chip in enumerate(chips):
            for a in range(n):
                copy(1 + j, a, (*chip, c), me).wait_recv()
                fwd = copy(4 + j, a, (*chip, c), sibling)
                fwd.start()
                passed.append(fwd)
        for a in range(n):
            copy(0, a, sibling, me).wait_recv()
            for j, chip in enumerate(chips):
                copy(4 + j, a, (*chip, 1 - c), me).wait_recv()
        for cp in first + passed:
            cp.wait_send()
        for cp in mine:
            cp.wait()

    anyspec = pl.BlockSpec(memory_space=pl.ANY)
    return pl.pallas_call(
        body, name=name,
        in_specs=[anyspec] * n, out_specs=[anyspec] * n,
        out_shape=[jax.ShapeDtypeStruct((N_DEV,) + a.shape, a.dtype) for a in arrs],
        scratch_shapes=[pltpu.SemaphoreType.DMA((7, n)), pltpu.SemaphoreType.DMA((7, n)), pltpu.SemaphoreType.DMA((n,))],
    )(*arrs)


W_ROWS = SEG_SSD[0] + SSD_PAD_W


GROUP = 16
INTERIOR = 1920


def _interior(k):
    lo = -(-(k * SHARD_IN) // GROUP) * GROUP
    hi = ((k + 1) * SHARD_IN) // GROUP * GROUP
    return lo, hi


def _dest_row(r):
    return r if r < 6144 else (r - 6144 + SEG_SSD[0] if r < 11296 else r - 11296 + SEG_GATE[0])


def _shard_pieces(k):
    lo_k, hi_k = _interior(k)
    out = []
    for lo, hi in ((0, 6144), (6144, 11296), (11296, W_IN)):
        a, b = max(lo, lo_k), min(hi, hi_k)
        if a < b:
            out.append((a - lo_k, b - a, _dest_row(a)))
    return out


def _patch_straddlers(wpT, heads, tails):
    for k in range(1, N_DEV):
        m = (k * SHARD_IN) % GROUP
        if m:
            group = jnp.concatenate([tails[k - 1, GROUP - m:], heads[k, :GROUP - m]], axis=0)
            wpT = lax.dynamic_update_slice(wpT, group, (_dest_row(k * SHARD_IN - m), 0))
    return wpT


def _gather_weights(win, head, tail, wout, cw, zeros):
    n_zero = zeros.shape[0]
    assert W_IN + n_zero == W_ROWS and W_IN % GROUP == 0
    small_in = (wout, cw, head, tail)

    def run(k, win_ref, wout_ref, cw_ref, head_ref, tail_ref, z_ref, w_out_ref, gout_ref, gcw_ref, ghead_ref, gtail_ref,
            send_sems, recv_sems, local_sems):
        x, y, c = k // 4, (k // 2) % 2, k % 2
        idx = lambda p: 4 * p[0] + 2 * p[1] + p[2]
        me, sib = (x, y, c), (x, y, 1 - c)
        xn, yn, dg = (1 - x, y, c), (x, 1 - y, c), (1 - x, 1 - y, c)
        small = ((wout_ref, gout_ref), (cw_ref, gcw_ref), (head_ref, ghead_ref), (tail_ref, gtail_ref))

        def copies(slot, block, to, own=False):
            kb = idx(block)
            out = []
            for j, (s0, n, d0) in enumerate(_shard_pieces(kb)):
                dst = w_out_ref.at[pl.ds(d0, n)]
                out.append((win_ref.at[pl.ds(s0, n)] if own else dst, dst, j))
            for j, (src, gathered) in enumerate(small):
                out.append((src if own else gathered.at[kb], gathered.at[kb], 2 + j))
            return [pltpu.make_async_remote_copy(src_ref=s, dst_ref=d, send_sem=send_sems.at[slot, j], recv_sem=recv_sems.at[slot, j],
                                                 device_id=to, device_id_type=MESH) for s, d, j in out]

        def start(cps):
            for cp in cps:
                cp.start()
            return cps

        def arrived(slot, block):
            for cp in copies(slot, block, me):
                cp.wait_recv()

        local = [pltpu.make_async_copy(s, d, local_sems.at[j]) for j, (s, d) in enumerate(
            [(win_ref.at[pl.ds(s0, n)], w_out_ref.at[pl.ds(d0, n)]) for s0, n, d0 in _shard_pieces(k)]
            + [(src, gathered.at[k]) for src, gathered in small] + [(z_ref, w_out_ref.at[pl.ds(W_IN, n_zero)])])]
        for cp in local:
            cp.start()
        sent = start(copies(0, me, sib, own=True)) + start(copies(1, me, xn, own=True)) + start(copies(2, me, yn, own=True))
        arrived(1, xn)
        sent += start(copies(4, xn, sib))
        if c == 1:
            sent += start(copies(3, xn, yn))
        arrived(2, yn)
        sent += start(copies(5, yn, sib))
        if c == 0:
            sent += start(copies(3, yn, xn))
        arrived(3, dg)
        sent += start(copies(6, dg, sib))
        arrived(0, sib)
        arrived(4, (1 - x, y, 1 - c))
        arrived(5, (x, 1 - y, 1 - c))
        arrived(6, (1 - x, 1 - y, 1 - c))
        for cp in sent:
            cp.wait_send()
        for cp in local:
            cp.wait()

    def body(*refs):
        x, y, c = _place()
        me = 4 * x + 2 * y + c
        for k in range(N_DEV):
            pl.when(me == k)(functools.partial(run, k, *refs))

    anyspec = pl.BlockSpec(memory_space=pl.ANY)
    n_arr = 2 + len(small_in)
    return pl.pallas_call(
        body, name="gather_weights", in_specs=[anyspec] * 6, out_specs=[anyspec] * 5,
        out_shape=[jax.ShapeDtypeStruct((W_ROWS, D), win.dtype)]
        + [jax.ShapeDtypeStruct((N_DEV,) + a.shape, a.dtype) for a in small_in],
        scratch_shapes=[pltpu.SemaphoreType.DMA((7, n_arr)), pltpu.SemaphoreType.DMA((7, n_arr)),
                        pltpu.SemaphoreType.DMA((n_arr + 1,))],
    )(win, wout, cw, head, tail, zeros)


_REL = [(dx, dy, dc) for dx in (0, 1) for dy in (0, 1) for dc in (0, 1)][1:]
_HBM = pl.BlockSpec(memory_space=pltpu.HBM)
_SEM = pl.BlockSpec(memory_space=pltpu.SEMAPHORE)
_EFFECT = pltpu.SideEffectType.DATAFLOW_SIDE_EFFECTING


def _peer(k):
    x, y, c = _place()
    dx, dy, dc = _REL[k]
    return (1 - x if dx else x, 1 - y if dy else y, 1 - c if dc else c)


def _exchange_start(parts, *, name):
    n = len(parts)

    def body(*refs):
        ins, lands = refs[:n], refs[n:2 * n]
        send_sems, recv_sems, token = refs[2 * n], refs[2 * n + 1], refs[-1]
        x, y, c = _place()
        me = 4 * x + 2 * y + c
        for a in range(n):
            for k in range(len(_REL)):
                px, py, pc = _peer(k)
                pltpu.make_async_remote_copy(
                    src_ref=ins[a].at[4 * px + 2 * py + pc], dst_ref=lands[a].at[me],
                    send_sem=send_sems.at[len(_REL) * a + k], recv_sem=recv_sems.at[len(_REL) * a + k],
                    device_id=(px, py, pc), device_id_type=MESH).start()
        token[...] = jnp.zeros_like(token)

    sem = pltpu.SemaphoreType.DMA((len(_REL) * n,))
    bufs = [pltpu.HBM(p.shape, p.dtype) for p in parts]
    outs = pl.pallas_call(
        body, name=name,
        out_shape=(sem, sem, *bufs, *bufs, jax.ShapeDtypeStruct((8, LANE), F32)),
        in_specs=(_HBM,) * (2 * n), out_specs=(_SEM, _SEM, *(_HBM,) * (2 * n), pl.BlockSpec(memory_space=pltpu.VMEM)),
        input_output_aliases={i: 2 + i for i in range(2 * n)},
        compiler_params=pltpu.CompilerParams(has_side_effects=_EFFECT),
    )(*[pltpu.with_memory_space_constraint(p, pltpu.HBM) for p in parts],
      *[pltpu.with_memory_space_constraint(lax.empty(p.shape, p.dtype), pltpu.HBM) for p in parts])
    return outs[0], outs[1], outs[2:2 + n], outs[2 + n:2 + 2 * n], outs[-1]


def _exchange_wait(send_sems, recv_sems, parts, lands, after, *, name):
    n = len(parts)

    def body(*refs):
        ins, lands_ = refs[:n], refs[n:2 * n]
        ssem, rsem = refs[2 * n], refs[2 * n + 1]
        for a in range(n):
            for k in range(len(_REL)):
                px, py, pc = _peer(k)
                p = 4 * px + 2 * py + pc
                cp = pltpu.make_async_remote_copy(
                    src_ref=ins[a].at[p], dst_ref=lands_[a].at[p],
                    send_sem=ssem.at[len(_REL) * a + k], recv_sem=rsem.at[len(_REL) * a + k],
                    device_id=(px, py, pc), device_id_type=MESH)
                cp.wait_send()
                cp.wait_recv()

    bufs = [pltpu.HBM(p.shape, p.dtype) for p in parts]
    outs = pl.pallas_call(
        body, name=name, out_shape=(*bufs, *bufs),
        in_specs=(*(_HBM,) * (2 * n), _SEM, _SEM, pl.BlockSpec(memory_space=pl.ANY)), out_specs=(_HBM,) * (2 * n),
        input_output_aliases={i: i for i in range(2 * n)},
        compiler_params=pltpu.CompilerParams(has_side_effects=_EFFECT),
    )(*parts, *lands, send_sems, recv_sems, after)
    return outs[:n], outs[n:]


WEIGHTS = ('norm_w', 'w_in', 'gate_b', 'sgu_norm_g', 'sgu_norm_b', 'sgu_w', 'sgu_b', 'conv_w', 'conv_b', 'dt_bias', 'A_log',
           'D_skip', 'ssd_norm_w', 'w_out', 'final_norm_w')
SHARDED = ('w_in', 'conv_w', 'w_out')
PACK_ROW = 8 * LANE


def _constants():
    tri = np.tril(np.ones((CHUNK, CHUNK), np.float32))
    expand = np.zeros((DT_W, D), np.float32)
    for h in range(HEADS):
        expand[h, h * HEADDIM:(h + 1) * HEADDIM] = 1.0
    sel = np.zeros((D, LANE), np.float32)
    for g in range(SGU_GROUPS):
        sel[g * LANE:(g + 1) * LANE, g] = 1.0
    pos_chunk = np.arange(SGU_BLOCK) // CHUNK
    mask = (pos_chunk[None, :] <= pos_chunk[:, None]).astype(np.float32)
    shift = np.zeros(((CONV_K - 1) * CHUNK, HALO_BLK + CHUNK), np.float32)
    for kk in range(CONV_K - 1):
        for t in range(CHUNK):
            shift[kk * CHUNK + t, HALO_BLK - (CONV_K - 1) + t + kk] = 1.0
    return dict(tri=jnp.asarray(tri, BF16), triT=jnp.asarray(tri.T.copy(), BF16), expand=jnp.asarray(np.tile(expand, (3, 1)), BF16),
                shift=jnp.asarray(shift, BF16),
                expandT=jnp.asarray(expand.T.copy(), BF16), sel=jnp.asarray(sel), mask=jnp.asarray(mask))


def _to_shards(segs):
    starts = np.cumsum([0] + [s.shape[0] for s in segs])
    assert starts[-1] == W_IN
    slabs = []
    for k in range(N_DEV):
        pieces = []
        for s, s0 in zip(segs, starts[:-1]):
            lo, hi = max(k * SHARD_IN, s0), min((k + 1) * SHARD_IN, s0 + s.shape[0])
            if lo < hi:
                pieces.append(s[lo - s0:hi - s0])
        slabs.append(jnp.concatenate(pieces, axis=0))
    return jnp.stack(slabs)


def _local_step(x2, tgt, wpT, wout, cw, p, exchange_small, exchange):
    S = x2.shape[0]
    k = _constants()
    xn = _norm_fwd(x2, p['norm_w'], tm=min(512, S))
    proj = _matmul(xn, wpT, trans_b=True, out_dtype=BF16, tm=min(1024, S), tn=2048, tk=D, name="in_proj")
    wm32 = p['sgu_w'][0] * k['mask']
    wm = wm32.astype(BF16)
    wmT = jnp.swapaxes(wm32, 1, 2).astype(BF16)
    bias_full = jnp.repeat(p['sgu_b'][0].T, LANE, axis=1)
    tm_sgu = min(256, S)
    ya = _sgu_fwd(proj, p['sgu_norm_g'], p['sgu_norm_b'], wm, bias_full, tm=tm_sgu)
    pad32 = lambda a: jnp.pad(a, ((0, 0), (0, DT_W - HEADS)))
    dtb_p, alog_p = pad32(p['dt_bias']), pad32(p['A_log'])
    d_exp = jnp.repeat(p['D_skip'], HEADDIM, axis=1)
    ssd_args = (cw, p['conv_b'], dtb_p, alog_p, d_exp, p['ssd_norm_w'])
    y, yb, states = _ssd_fwd(proj, *ssd_args, k['tri'], k['expand'], k['shift'])
    dh, dhb, mb, dya, dyb, dgl, loss, dfw, dgb = _head(
        x2, ya, yb, proj, tgt, p['gate_b'], wout, p['final_norm_w'][None, :], tm=min(256, S))
    dsgu, dws, dbsT, dsg, dsb = _sgu_bwd(proj, dya, p['sgu_norm_g'], p['sgu_norm_b'], wm, wmT, bias_full, k['mask'], k['sel'],
                                         tm=tm_sgu)
    dssd, dcw, dcb, ddtb, dalog, dD, dnw = _ssd_bwd(proj, dyb, y, states, *ssd_args, k['tri'], k['triT'], k['expand'], k['expandT'],
                                                    k['shift'])
    grads = dict(
        gate_b=dgb[0:1], sgu_norm_g=dsg[0:1], sgu_norm_b=dsb[0:1], sgu_w=dws[None],
        sgu_b=dbsT[:, :SGU_GROUPS].T[None], conv_w=dcw[0:CONV_K][None], conv_b=dcb[0:1], dt_bias=ddtb[0:1, :HEADS],
        A_log=dalog[0:1, :HEADS], D_skip=dD[0:1, :HEADS], ssd_norm_w=dnw[0:1], final_norm_w=dfw[0])
    token = exchange_small(loss[0, 0], grads)
    tk = min(2048, S)
    dwT_sgu = _matmul(dsgu, xn, trans_a=True, out_dtype=BF16, tm=1024, tn=D, tk=tk, after=token, name="dw_in_sgu")
    dwT_gate = _matmul(dgl, xn, trans_a=True, out_dtype=BF16, tm=1024, tn=D, tk=tk, name="dw_in_gate")
    dwT_ssd = _matmul(dssd, xn, trans_a=True, out_dtype=BF16, tm=1024, tn=D, tk=tk, name="dw_in_ssd")
    dw_out = _matmul(mb, dhb, trans_a=True, out_dtype=BF16, tm=1024, tn=D, tk=tk, name="dw_out")
    tn = 1024
    token = exchange([dwT_sgu, dwT_ssd[:W_IN - SEG_SSD[0]], dwT_gate], dw_out)
    tm = min(1024, S)
    dxn = _matmul(dsgu, wpT, tm=tm, tn=tn, tk=3072, after=token, name="dxn_sgu")
    dxn = _matmul(dgl, wpT, b_koff=SEG_GATE[0] // 2048, tm=tm, tn=tn, tk=2048, add=dxn, name="dxn_gate")
    dxn = _matmul(dssd, wpT, b_koff=SEG_SSD[0] // 2048, tm=tm, tn=tn, tk=2048, add=dxn, name="dxn_ssd")
    grad_x, dnorm = _norm_bwd(x2, p['norm_w'], dxn, dh, tm=min(256, S))
    return grad_x, dnorm[0:1]


def _pack(arrs):
    rows, offs, r = [], [], 0
    for a in arrs:
        n = a.size
        nr = -(-n // PACK_ROW) * 8
        rows.append(jnp.pad(a.reshape(-1).astype(F32), (0, nr * LANE - n)).reshape(nr, LANE))
        offs.append(r)
        r += nr
    return jnp.concatenate(rows, axis=0), offs


def kernel(x, norm_w, w_in, gate_b, sgu_norm_g, sgu_norm_b, sgu_w, sgu_b, conv_w, conv_b, dt_bias, A_log, D_skip, ssd_norm_w, w_out, final_norm_w, loss_target, m_norm_w, m_w_in, m_gate_b, m_sgu_norm_g, m_sgu_norm_b, m_sgu_w, m_sgu_b, m_conv_w, m_conv_b, m_dt_bias, m_A_log, m_D_skip, m_ssd_norm_w, m_w_out, m_final_norm_w, v_norm_w, v_w_in, v_gate_b, v_sgu_norm_g, v_sgu_norm_b, v_sgu_w, v_sgu_b, v_conv_w, v_conv_b, v_dt_bias, v_A_log, v_D_skip, v_ssd_norm_w, v_w_out, v_final_norm_w):
    w = dict(norm_w=norm_w, w_in=w_in, gate_b=gate_b, sgu_norm_g=sgu_norm_g, sgu_norm_b=sgu_norm_b, sgu_w=sgu_w, sgu_b=sgu_b,
             conv_w=conv_w, conv_b=conv_b, dt_bias=dt_bias, A_log=A_log, D_skip=D_skip, ssd_norm_w=ssd_norm_w, w_out=w_out,
             final_norm_w=final_norm_w)
    m = dict(norm_w=m_norm_w, w_in=m_w_in, gate_b=m_gate_b, sgu_norm_g=m_sgu_norm_g, sgu_norm_b=m_sgu_norm_b, sgu_w=m_sgu_w,
             sgu_b=m_sgu_b, conv_w=m_conv_w, conv_b=m_conv_b, dt_bias=m_dt_bias, A_log=m_A_log, D_skip=m_D_skip,
             ssd_norm_w=m_ssd_norm_w, w_out=m_w_out, final_norm_w=m_final_norm_w)
    v = dict(norm_w=v_norm_w, w_in=v_w_in, gate_b=v_gate_b, sgu_norm_g=v_sgu_norm_g, sgu_norm_b=v_sgu_norm_b, sgu_w=v_sgu_w,
             sgu_b=v_sgu_b, conv_w=v_conv_w, conv_b=v_conv_b, dt_bias=v_dt_bias, A_log=v_A_log, D_skip=v_D_skip,
             ssd_norm_w=v_ssd_norm_w, w_out=v_w_out, final_norm_w=v_final_norm_w)
    me = 4 * lax.axis_index("x") + 2 * lax.axis_index("y") + lax.axis_index("c")
    shard_cw = XBC_W // N_DEV

    tpose = lambda a: jnp.swapaxes(a[0], 0, 1)
    wT = tpose(w_in).astype(BF16)
    first_group = (GROUP - (me * SHARD_IN) % GROUP) % GROUP
    window = lax.dynamic_slice(jnp.pad(wT, ((0, GROUP), (0, 0))), (first_group, 0), (INTERIOR, D))
    wpT, g_out, g_cw, heads, tails = _gather_weights(window, wT[:GROUP], wT[SHARD_IN - GROUP:], w_out[0].astype(BF16),
                                                     conv_w[0], jnp.zeros((W_ROWS - W_IN, D), BF16))
    wpT = _patch_straddlers(wpT, heads, tails)
    wout_full = g_out.reshape(D, D)
    cw_full = jnp.swapaxes(g_cw, 0, 1).reshape(CONV_K, XBC_W)

    flight = {}

    small = [n for n in WEIGHTS if n not in SHARDED and n != 'norm_w']
    early = {}

    def exchange_small(loss_part, grads):
        early['packed'], early['offs'] = _pack([grads[n] for n in small] + [loss_part, grads['conv_w']])
        parts = [jnp.broadcast_to(early['packed'][None], (N_DEV,) + early['packed'].shape)]
        early['sems'], early['rsems'], early['parts'], early['lands'], token = _exchange_start(parts, name="small_start")
        return token

    def exchange(dw_inT_segs, dw_out):
        parts = [_to_shards(dw_inT_segs), dw_out.reshape(N_DEV, D // N_DEV, D)]
        flight['sems'], flight['rsems'], flight['parts'], flight['lands'], token = _exchange_start(parts, name="exchange_start")
        return token

    grad_x, dnorm = _local_step(x[0], loss_target[0], wpT, wout_full, cw_full, w, exchange_small, exchange)
    _, (land_small,) = _exchange_wait(early['sems'], early['rsems'], early['parts'], early['lands'], grad_x, name="small_wait")
    (own_in, own_out), (land_in, land_out) = _exchange_wait(
        flight['sems'], flight['rsems'], flight['parts'], flight['lands'], grad_x, name="exchange_wait")
    me_arr = jnp.reshape(me, (1,)).astype(jnp.int32)
    res = {}
    res['w_in'] = [jnp.swapaxes(o, 0, 1) for o in _adamw_own(
        me_arr, own_in, land_in, tpose(w_in), tpose(m_w_in), tpose(v_w_in), tr=SHARD_IN, tc=256, name="adamw_w_in")]
    res['w_out'] = _adamw_own(me_arr, own_out, land_out, w_out[0], m_w_out[0], v_w_out[0], tr=128, tc=D, name="adamw_w_out")

    (norm_parts,) = _all_gather([_pack([dnorm])[0]], name="gather_norm")
    norm_outs = _adamw(norm_parts, *[_pack([d['norm_w']])[0] for d in (w, m, v)], tr=norm_parts.shape[1], name="adamw_norm")
    res['norm_w'] = [o.reshape(-1)[:D].reshape(w['norm_w'].shape) for o in norm_outs]

    offs = early['offs']
    gathered = lax.dynamic_update_slice(land_small, early['packed'][None], (me, 0, 0))
    off_loss, off_cw = offs[-2], offs[-1]
    cw_parts = gathered[:, off_cw:, :].reshape(N_DEV, CONV_K, XBC_W)
    cw_parts = lax.dynamic_slice_in_dim(cw_parts, me * shard_cw, shard_cw, axis=2)
    cw_rows = _pack([cw_parts[0]])[0].shape[0]
    cw_parts = jnp.pad(cw_parts.reshape(N_DEV, -1), ((0, 0), (0, cw_rows * LANE - CONV_K * shard_cw))).reshape(N_DEV, cw_rows, LANE)
    parts = jnp.concatenate([gathered[:, :off_cw, :], cw_parts], axis=1)
    zero = jnp.zeros((), F32)
    packs = [_pack([d[n] for n in small] + [zero, d['conv_w']])[0] for d in (w, m, v)]
    outs = _adamw(parts, *packs, tr=parts.shape[1], name="adamw_small")

    def unpack(o, name):
        if name == 'conv_w':
            return o[off_cw:off_cw + cw_rows].reshape(-1)[:CONV_K * shard_cw].reshape(w['conv_w'].shape)
        r0 = offs[small.index(name)]
        n = w[name].size
        return o[r0:r0 + -(-n // PACK_ROW) * 8].reshape(-1)[:n].reshape(w[name].shape)

    for n in small + ['conv_w']:
        res[n] = [unpack(o, n) for o in outs]
    for n in ('w_in', 'w_out'):
        res[n] = [o[None] for o in res[n]]
    loss = outs[0][off_loss, 0]
    return (loss, grad_x[None], *[res[n][0] for n in WEIGHTS], *[res[n][1] for n in WEIGHTS],
            *[res[n][2] for n in WEIGHTS], *[res[n][3] for n in WEIGHTS])
```

```python
import functools

import numpy as np
import jax
import jax.numpy as jnp
from jax import lax
from jax.experimental import pallas as pl
from jax.experimental.pallas import tpu as pltpu

F32 = jnp.float32
BF16 = jnp.bfloat16
HI = lax.Precision.HIGHEST
MESH = pl.DeviceIdType.MESH

D = 2048
EPS = 1e-5
SGU_BLOCK = 128
SGU_GROUPS = 16
CHUNK = 64
HEADS = 32
HEADDIM = 64
SSD_GROUPS = 4
GROUP_W = D // SSD_GROUPS
STATE = 128
CONV_K = 4
XBC_W = D + 2 * SSD_GROUPS * STATE
W_IN = 15392
N_DEV = 8
SHARD_IN = W_IN // N_DEV
ADAM_LR, ADAM_B1, ADAM_B2, ADAM_EPS, ADAM_WD, ADAM_STEP = 0.001, 0.9, 0.999, 1e-08, 0.01, 10

LANE = 128
DT_W = LANE
OFF_U, OFF_V, OFF_ZA, OFF_G0, OFF_G1, OFF_ZB, OFF_XBC, OFF_DT = 0, 2048, 4096, 6144, 8192, 10240, 12288, 15360
WP = OFF_DT + DT_W
SEG_SGU = (0, 6144)
SEG_GATE = (6144, 4096)
SEG_SSD = (10240, WP - 10240)
SSD_PAD_W = 6144
VMEM_LIMIT = 56 * 1024 * 1024


def _cp(sem=None, vmem=VMEM_LIMIT):
    return pltpu.CompilerParams(dimension_semantics=sem, vmem_limit_bytes=vmem)


def _sigmoid(x):
    return 1.0 / (1.0 + jnp.exp(-x))


def _softplus(x):
    return jnp.maximum(x, 0.0) + jnp.log(1.0 + jnp.exp(-jnp.abs(x)))


def _dot(a, b, precision=None):
    return jnp.dot(a, b, preferred_element_type=F32, precision=precision)


def _dot_nt(a, b, precision=None):
    return lax.dot_general(a, b, (((1,), (1,)), ((), ())), preferred_element_type=F32, precision=precision)


def _dot_tn(a, b, precision=None):
    return lax.dot_general(a, b, (((0,), (0,)), ((), ())), preferred_element_type=F32, precision=precision)


def _split3(a):
    hi = a.astype(BF16)
    r = a - hi.astype(F32)
    mid = r.astype(BF16)
    return hi, mid, (r - mid.astype(F32)).astype(BF16)


def _sel_right(a, sel01):
    m = a.shape[0]
    r = _dot(jnp.concatenate(_split3(a), axis=0), sel01)
    return (r[0:m] + r[m:2 * m]) + r[2 * m:3 * m]


def _sel_right_k(a, sel01_x3):
    return _dot(jnp.concatenate(_split3(a), axis=1), sel01_x3)


def _sel_left(sel01, a):
    n = a.shape[1]
    r = _dot(sel01, jnp.concatenate(_split3(a), axis=1))
    return (r[:, 0:n] + r[:, n:2 * n]) + r[:, 2 * n:3 * n]


def _matmul(a, b, *, trans_a=False, trans_b=False, b_koff=0, n=None, out_dtype=F32, tm, tn, tk, add=None, after=None, name):
    K, M = a.shape if trans_a else a.shape[::-1]
    N = (n or b.shape[0]) if trans_b else b.shape[1]
    assert M % tm == 0 and N % tn == 0 and K % tk == 0 and not (trans_a and trans_b), (name, M, N, K, tm, tn, tk)
    nk = K // tk

    def body(*refs):
        a_ref, b_ref = refs[:2]
        add_ref = refs[2] if add is not None else None
        o_ref, acc_ref = refs[-2:]
        k = pl.program_id(2)
        if trans_a:
            part = _dot_tn(a_ref[...], b_ref[...])
        else:
            part = _dot_nt(a_ref[...], b_ref[...]) if trans_b else _dot(a_ref[...], b_ref[...])

        def result(r):
            if add_ref is not None:
                r = r + add_ref[...]
            return r.astype(out_dtype)

        if nk == 1:
            o_ref[...] = result(part)
        else:
            @pl.when(k == 0)
            def _():
                acc_ref[...] = part

            @pl.when(jnp.logical_and(k > 0, k < nk - 1))
            def _():
                acc_ref[...] += part

            @pl.when(k == nk - 1)
            def _():
                o_ref[...] = result(acc_ref[...] + part)

    in_specs = [pl.BlockSpec((tk, tm), lambda i, j, k: (k, i)) if trans_a else pl.BlockSpec((tm, tk), lambda i, j, k: (i, k)),
                pl.BlockSpec((tn, tk), lambda i, j, k: (j, k)) if trans_b else pl.BlockSpec((tk, tn), lambda i, j, k: (k + b_koff, j))]
    args = [a, b]
    if add is not None:
        in_specs.append(pl.BlockSpec((tm, tn), lambda i, j, k: (i, j)))
        args.append(add)
    if after is not None:
        in_specs.append(pl.BlockSpec(memory_space=pl.ANY))
        args.append(after)
    return pl.pallas_call(
        body, name=name, grid=(M // tm, N // tn, nk), in_specs=in_specs,
        out_specs=pl.BlockSpec((tm, tn), lambda i, j, k: (i, j)),
        out_shape=jax.ShapeDtypeStruct((M, N), out_dtype),
        scratch_shapes=[pltpu.VMEM((tm, tn), F32)],
        compiler_params=_cp(("parallel", "parallel", "arbitrary")),
    )(*args)


def _norm_fwd(x, w, *, tm):
    S = x.shape[0]

    def body(x_ref, w_ref, o_ref):
        xv = x_ref[...]
        r = lax.rsqrt(jnp.mean(xv * xv, axis=-1, keepdims=True) + EPS)
        o_ref[...] = (xv * r * w_ref[...]).astype(BF16)

    return pl.pallas_call(
        body, name="norm_fwd", grid=(S // tm,),
        in_specs=[pl.BlockSpec((tm, D), lambda i: (i, 0)), pl.BlockSpec((1, D), lambda i: (0, 0))],
        out_specs=pl.BlockSpec((tm, D), lambda i: (i, 0)),
        out_shape=jax.ShapeDtypeStruct((S, D), BF16), compiler_params=_cp(("parallel",)),
    )(x, w)


def _norm_bwd(x, w, dxn, dh, *, tm):
    S = x.shape[0]

    def body(x_ref, w_ref, dxn_ref, dh_ref, gx_ref, dw_ref):
        xv = x_ref[...]
        r = lax.rsqrt(jnp.mean(xv * xv, axis=-1, keepdims=True) + EPS)
        xh = xv * r
        dxn_v = dxn_ref[...]
        dxh = dxn_v * w_ref[...]
        gx_ref[...] = dh_ref[...] + r * (dxh - xh * jnp.mean(dxh * xh, axis=-1, keepdims=True))

        @pl.when(pl.program_id(0) == 0)
        def _():
            dw_ref[...] = jnp.zeros_like(dw_ref)

        dw_ref[0:1, :] += jnp.sum(dxn_v * xh, axis=0, keepdims=True)

    row = pl.BlockSpec((tm, D), lambda i: (i, 0))
    return pl.pallas_call(
        body, name="norm_bwd", grid=(S // tm,),
        in_specs=[row, pl.BlockSpec((1, D), lambda i: (0, 0)), row, row],
        out_specs=[row, pl.BlockSpec((8, D), lambda i: (0, 0))],
        out_shape=[jax.ShapeDtypeStruct((S, D), F32), jax.ShapeDtypeStruct((8, D), F32)],
        compiler_params=_cp(("arbitrary",)),
    )(x, w, dxn, dh)


def _sgu_core(u_ref, v_ref, z_ref, g_ref, b_ref, wm_ref, bias_ref, vnb_ref, mixed_ref, tm):
    v = v_ref[...].astype(F32)
    mu = jnp.mean(v, axis=-1, keepdims=True)
    vc = v - mu
    rs = lax.rsqrt(jnp.mean(vc * vc, axis=-1, keepdims=True) + EPS)
    vh = vc * rs
    vnb_ref[...] = (vh * g_ref[...] + b_ref[...]).astype(BF16)
    for blk in range(tm // SGU_BLOCK):
        rows = pl.ds(blk * SGU_BLOCK, SGU_BLOCK)
        for gi in range(SGU_GROUPS):
            cols = pl.ds(gi * LANE, LANE)
            mixed_ref[rows, cols] = _dot(wm_ref[gi], vnb_ref[rows, cols]) + bias_ref[:, cols]
    return vh, rs


def _sgu_fwd(proj, g, b, wm, bias_full, *, tm):
    S = proj.shape[0]

    def body(u_ref, v_ref, z_ref, g_ref, b_ref, wm_ref, bias_ref, y_ref, vnb_ref, mixed_ref):
        _sgu_core(u_ref, v_ref, z_ref, g_ref, b_ref, wm_ref, bias_ref, vnb_ref, mixed_ref, tm)
        z = z_ref[...].astype(F32)
        y_ref[...] = (u_ref[...].astype(F32) * mixed_ref[...] * (z * _sigmoid(z))).astype(BF16)

    seg = lambda off: pl.BlockSpec((tm, D), lambda i: (i, off // D))
    full = lambda a: pl.BlockSpec(a.shape, lambda i: (0,) * a.ndim)
    return pl.pallas_call(
        body, name="sgu_fwd", grid=(S // tm,),
        in_specs=[seg(OFF_U), seg(OFF_V), seg(OFF_ZA), full(g), full(b), full(wm), full(bias_full)],
        out_specs=pl.BlockSpec((tm, D), lambda i: (i, 0)),
        out_shape=jax.ShapeDtypeStruct((S, D), BF16),
        scratch_shapes=[pltpu.VMEM((tm, D), BF16), pltpu.VMEM((tm, D), F32)],
        compiler_params=_cp(("parallel",)),
    )(proj, proj, proj, g, b, wm, bias_full)


def _sgu_bwd(proj, dy, g, b, wm, wmT, bias_full, mask, sel, *, tm):
    S = proj.shape[0]
    nsteps = S // tm

    def body(u_ref, v_ref, z_ref, dy_ref, g_ref, b_ref, wm_ref, wmT_ref, bias_ref, mask_ref, sel_ref,
             dp_ref, dws_ref, dbs_ref, dg_ref, db_ref, vnb_ref, mixed_ref, dmb_ref, dvn_ref, dbias_ref):
        i = pl.program_id(0)

        @pl.when(i == 0)
        def _():
            dws_ref[...] = jnp.zeros_like(dws_ref)
            dg_ref[...] = jnp.zeros_like(dg_ref)
            db_ref[...] = jnp.zeros_like(db_ref)
            dbias_ref[...] = jnp.zeros_like(dbias_ref)

        vh, rs = _sgu_core(u_ref, v_ref, z_ref, g_ref, b_ref, wm_ref, bias_ref, vnb_ref, mixed_ref, tm)
        u = u_ref[...].astype(F32)
        z = z_ref[...].astype(F32)
        dy_v = dy_ref[...].astype(F32)
        mixed = mixed_ref[...]
        sg = _sigmoid(z)
        sz = z * sg
        dp_ref[:, 0:D] = (dy_v * mixed * sz).astype(BF16)
        dp_ref[:, 2 * D:3 * D] = (dy_v * u * mixed * (sg * (1.0 + z * (1.0 - sg)))).astype(BF16)
        dmixed = dy_v * u * sz
        dmb_ref[...] = dmixed.astype(BF16)
        for blk in range(tm // SGU_BLOCK):
            dbias_ref[...] += dmixed[blk * SGU_BLOCK:(blk + 1) * SGU_BLOCK, :]
        for blk in range(tm // SGU_BLOCK):
            rows = pl.ds(blk * SGU_BLOCK, SGU_BLOCK)
            for gi in range(SGU_GROUPS):
                cols = pl.ds(gi * LANE, LANE)
                dm = dmb_ref[rows, cols]
                dvn_ref[rows, cols] = _dot(wmT_ref[gi], dm)
                dws_ref[gi] += _dot_nt(dm, vnb_ref[rows, cols])
        dvn = dvn_ref[...]
        dg_ref[0:1, :] += jnp.sum(dvn * vh, axis=0, keepdims=True)
        db_ref[0:1, :] += jnp.sum(dvn, axis=0, keepdims=True)
        dvh = dvn * g_ref[...]
        dv = rs * (dvh - jnp.mean(dvh, axis=-1, keepdims=True) - vh * jnp.mean(dvh * vh, axis=-1, keepdims=True))
        dp_ref[:, D:2 * D] = dv.astype(BF16)

        @pl.when(i == nsteps - 1)
        def _():
            for gi in range(SGU_GROUPS):
                dws_ref[gi] = dws_ref[gi] * mask_ref[...]
            dbs_ref[...] = _dot(dbias_ref[...], sel_ref[...], precision=HI)

    seg = lambda off: pl.BlockSpec((tm, D), lambda i: (i, off // D))
    full = lambda a: pl.BlockSpec(a.shape, lambda i: (0,) * a.ndim)
    return pl.pallas_call(
        body, name="sgu_bwd", grid=(nsteps,),
        in_specs=[seg(OFF_U), seg(OFF_V), seg(OFF_ZA), pl.BlockSpec((tm, D), lambda i: (i, 0)),
                  full(g), full(b), full(wm), full(wmT), full(bias_full), full(mask), full(sel)],
        out_specs=[pl.BlockSpec((tm, 3 * D), lambda i: (i, 0)),
                   pl.BlockSpec((SGU_GROUPS, SGU_BLOCK, SGU_BLOCK), lambda i: (0, 0, 0)),
                   pl.BlockSpec((SGU_BLOCK, LANE), lambda i: (0, 0)),
                   pl.BlockSpec((8, D), lambda i: (0, 0)), pl.BlockSpec((8, D), lambda i: (0, 0))],
        out_shape=[jax.ShapeDtypeStruct((S, 3 * D), BF16),
                   jax.ShapeDtypeStruct((SGU_GROUPS, SGU_BLOCK, SGU_BLOCK), F32),
                   jax.ShapeDtypeStruct((SGU_BLOCK, LANE), F32),
                   jax.ShapeDtypeStruct((8, D), F32), jax.ShapeDtypeStruct((8, D), F32)],
        scratch_shapes=[pltpu.VMEM((tm, D), BF16), pltpu.VMEM((tm, D), F32), pltpu.VMEM((tm, D), BF16),
                        pltpu.VMEM((tm, D), F32), pltpu.VMEM((SGU_BLOCK, D), F32)],
        compiler_params=_cp(("arbitrary",)),
    )(proj, proj, proj, dy, g, b, wm, wmT, bias_full, mask, sel)


SSD_T = 2 * CHUNK
HALO = 8
HALO_BLK = 16


def _pair_masks():
    row = lax.broadcasted_iota(jnp.int32, (CHUNK, LANE), 0)
    lane = lax.broadcasted_iota(jnp.int32, (CHUNK, LANE), 1)
    pos = jnp.where(lane >= CHUNK, lane - CHUNK, lane)
    diag = (row == pos).astype(F32)
    causal = row >= pos
    lo = (lane < CHUNK).astype(F32)
    return diag, causal, lo, 1.0 - lo


def _ssd_chunk_fwd(c, ext_ref, shift_ref, dt_ref, cw_ref, cb_ref, dtb_ref, alog_ref, tri_ref, exp_ref):
    r0 = c * CHUNK
    win = ext_ref[pl.ds(r0, HALO_BLK + CHUNK), :]
    sh = _dot(shift_ref[...], win)
    taps = [sh[k * CHUNK:(k + 1) * CHUNK] for k in range(CONV_K - 1)] + [win[HALO_BLK:].astype(F32)]
    pre = cb_ref[...] + sum(cw_ref[k:k + 1, :] * taps[k] for k in range(CONV_K))
    sg = _sigmoid(pre)
    xc = pre * sg
    dtr = dt_ref[pl.ds(r0, CHUNK), :].astype(F32) + dtb_ref[...]
    dtv = _softplus(dtr)
    A = -jnp.exp(alog_ref[...])
    acs = _sel_left(tri_ref[...], dtv * A)
    both = _sel_right_k(jnp.concatenate([acs, dtv], axis=0), exp_ref[...])
    E, dtE = both[0:CHUNK], both[CHUNK:2 * CHUNK]
    return dict(taps=taps, pre=pre, sg=sg, xc=xc, dtr=dtr, dtv=dtv, A=A, E=E, dtE=dtE)


def _ssd_fwd(proj, conv_w, conv_b, dtb_p, alog_p, d_exp, norm_w, tri, expand, shift):
    S = proj.shape[0]
    T = SSD_T
    nsteps = S // T
    ncl = T // CHUNK

    def body(zb_ref, xbc_ref, halo_ref, dt_ref, cw_ref, cb_ref, dtb_ref, alog_ref, dexp_ref, nw_ref, tri_ref, exp_ref, shift_ref,
             y_ref, yb_ref, st_ref, ht_ref, ext_ref):
        i = pl.program_id(0)

        @pl.when(i == 0)
        def _():
            ht_ref[...] = jnp.zeros_like(ht_ref)
            ext_ref[0:HALO_BLK, :] = jnp.zeros((HALO_BLK, XBC_W), BF16)

        @pl.when(i > 0)
        def _():
            ext_ref[0:HALO_BLK, :] = halo_ref[...]

        ext_ref[HALO_BLK:HALO_BLK + T, :] = xbc_ref[...]
        diag, causal, lo, hi = _pair_masks()
        for c in range(ncl):
            q = _ssd_chunk_fwd(c, ext_ref, shift_ref, dt_ref, cw_ref, cb_ref, dtb_ref, alog_ref, tri_ref, exp_ref)
            rows = pl.ds(c * CHUNK, CHUNK)
            xc, E, dtE = q["xc"], q["E"], q["dtE"]
            xs = xc[:, 0:D]
            total = E[CHUNK - 1:CHUNK, :]
            x_dt = xs * dtE
            eE = jnp.exp(E)
            xw = x_dt * jnp.exp(total - E)
            st_ref[c] = ht_ref[...]
            for g in range(SSD_GROUPS):
                gc = slice(g * GROUP_W, (g + 1) * GROUP_W)
                Bg = xc[:, D + g * STATE:D + (g + 1) * STATE].astype(BF16)
                Cg = xc[:, D + SSD_GROUPS * STATE + g * STATE:D + SSD_GROUPS * STATE + (g + 1) * STATE].astype(BF16)
                cb2 = _dot_nt(Cg, jnp.concatenate([Bg, Bg], axis=0))
                htg = ht_ref[:, gc]
                y_ref[rows, gc] = eE[:, gc] * _dot(Cg, htg.astype(BF16)) + xs[:, gc] * dexp_ref[:, gc]
                for jj in range(GROUP_W // LANE):
                    pc = slice(g * GROUP_W + jj * LANE, g * GROUP_W + (jj + 1) * LANE)
                    Ej = E[:, pc]
                    e2 = jnp.sum(Ej * diag, axis=0, keepdims=True)
                    Mp = cb2 * jnp.exp(jnp.where(causal, Ej - e2, -1e30))
                    xj = x_dt[:, pc]
                    xbd = jnp.concatenate([xj * lo, xj * hi], axis=0).astype(BF16)
                    y_ref[rows, pc] += _dot(Mp.astype(BF16), xbd)
                ht_ref[:, gc] = jnp.exp(total[:, gc]) * htg + _dot_tn(Bg, xw[:, gc].astype(BF16))
            zb = zb_ref[rows, :].astype(F32)
            hh = y_ref[rows, :] * (zb * _sigmoid(zb))
            for g in range(SSD_GROUPS):
                gc = slice(g * GROUP_W, (g + 1) * GROUP_W)
                hg = hh[:, gc]
                r = lax.rsqrt(jnp.mean(hg * hg, axis=-1, keepdims=True) + EPS)
                yb_ref[rows, gc] = (hg * r * nw_ref[:, gc]).astype(BF16)

    full = lambda a: pl.BlockSpec(a.shape, lambda i: (0,) * a.ndim)
    hb = T // HALO_BLK
    return pl.pallas_call(
        body, name="ssd_fwd", grid=(nsteps,),
        in_specs=[pl.BlockSpec((T, D), lambda i: (i, OFF_ZB // D)),
                  pl.BlockSpec((T, XBC_W), lambda i: (i, OFF_XBC // XBC_W)),
                  pl.BlockSpec((HALO_BLK, XBC_W), lambda i: (jnp.maximum(i * hb - 1, 0), OFF_XBC // XBC_W)),
                  pl.BlockSpec((T, DT_W), lambda i: (i, OFF_DT // DT_W)),
                  full(conv_w), full(conv_b), full(dtb_p), full(alog_p), full(d_exp), full(norm_w), full(tri), full(expand),
                  full(shift)],
        out_specs=[pl.BlockSpec((T, D), lambda i: (i, 0)), pl.BlockSpec((T, D), lambda i: (i, 0)),
                   pl.BlockSpec((ncl, STATE, D), lambda i: (i, 0, 0))],
        out_shape=[jax.ShapeDtypeStruct((S, D), F32), jax.ShapeDtypeStruct((S, D), BF16),
                   jax.ShapeDtypeStruct((S // CHUNK, STATE, D), F32)],
        scratch_shapes=[pltpu.VMEM((STATE, D), F32), pltpu.VMEM((HALO_BLK + T, XBC_W), BF16)],
        compiler_params=_cp(("arbitrary",)),
    )(proj, proj, proj, proj, conv_w, conv_b, dtb_p, alog_p, d_exp, norm_w, tri, expand, shift)


def _ssd_bwd(proj, dyb, y, states, conv_w, conv_b, dtb_p, alog_p, d_exp, norm_w, tri, triT, expand, expandT, shift):
    S = proj.shape[0]
    T = SSD_T
    nsteps = S // T
    ncl = T // CHUNK
    SSD_W = SSD_PAD_W

    def body(zb_ref, xbc_ref, halo_ref, dt_ref, dyb_ref, y_ref, st_ref, cw_ref, cb_ref, dtb_ref, alog_ref, dexp_ref, nw_ref,
             tri_ref, triT_ref, exp_ref, expT_ref, shift_ref,
             dp_ref, dcw_ref, dcb_ref, ddtb_ref, dalog_ref, dD_ref, dnw_ref,
             dht_ref, ext_ref, dpre_ref, dy_s, dE_s, dxdt_s, dxc_s, dDacc_ref, dAacc_ref):
        i = pl.program_id(0)

        @pl.when(i == 0)
        def _():
            for r in (dht_ref, dcw_ref, dcb_ref, ddtb_ref, dnw_ref, dDacc_ref, dAacc_ref):
                r[...] = jnp.zeros_like(r)
            dpre_ref[T:T + HALO_BLK, :] = jnp.zeros((HALO_BLK, XBC_W), F32)

        @pl.when(i == nsteps - 1)
        def _():
            ext_ref[0:HALO_BLK, :] = jnp.zeros((HALO_BLK, XBC_W), BF16)

        @pl.when(i < nsteps - 1)
        def _():
            ext_ref[0:HALO_BLK, :] = halo_ref[...]

        ext_ref[HALO_BLK:HALO_BLK + T, :] = xbc_ref[...]
        diag, causal, lo, hi = _pair_masks()
        last_row = (lax.broadcasted_iota(jnp.int32, (CHUNK, 1), 0) == CHUNK - 1).astype(F32)
        for c in reversed(range(ncl)):
            q = _ssd_chunk_fwd(c, ext_ref, shift_ref, dt_ref, cw_ref, cb_ref, dtb_ref, alog_ref, tri_ref, exp_ref)
            rows = pl.ds(c * CHUNK, CHUNK)
            pre, sg, xc, dtr, dtv, A, E, dtE = (q[k] for k in ("pre", "sg", "xc", "dtr", "dtv", "A", "E", "dtE"))
            xs = xc[:, 0:D]
            total = E[CHUNK - 1:CHUNK, :]
            x_dt = xs * dtE
            eE = jnp.exp(E)
            wdec = jnp.exp(total - E)
            zb = zb_ref[rows, :].astype(F32)
            yv = y_ref[rows, :]
            sgz = _sigmoid(zb)
            sz = zb * sgz
            hh = yv * sz
            for g in range(SSD_GROUPS):
                gc = slice(g * GROUP_W, (g + 1) * GROUP_W)
                hg = hh[:, gc]
                r = lax.rsqrt(jnp.mean(hg * hg, axis=-1, keepdims=True) + EPS)
                dyb_g = dyb_ref[rows, gc].astype(F32)
                dn = dyb_g * nw_ref[:, gc]
                dnw_ref[0:1, gc] += jnp.sum(dyb_g * hg * r, axis=0, keepdims=True)
                dy_s[:, gc] = r * dn - hg * (r * r * r) * jnp.mean(dn * hg, axis=-1, keepdims=True)
            dhh = dy_s[...]
            dp_ref[rows, 0:D] = (dhh * yv * (sgz * (1.0 + zb * (1.0 - sgz)))).astype(BF16)
            dy = dhh * sz
            dy_s[...] = dy
            dDacc_ref[0:1, :] += jnp.sum(dy * xs, axis=0, keepdims=True)
            dxc_s[:, 0:D] = dy * dexp_ref[...]
            for g in range(SSD_GROUPS):
                gc = slice(g * GROUP_W, (g + 1) * GROUP_W)
                bcol = slice(D + g * STATE, D + (g + 1) * STATE)
                ccol = slice(D + SSD_GROUPS * STATE + g * STATE, D + SSD_GROUPS * STATE + (g + 1) * STATE)
                Bg = xc[:, bcol].astype(BF16)
                Cg = xc[:, ccol].astype(BF16)
                B2 = jnp.concatenate([Bg, Bg], axis=0)
                cb2 = _dot_nt(Cg, B2)
                htg = st_ref[c, :, gc]
                htb = htg.astype(BF16)
                dhn = dht_ref[:, gc]
                dhnb = dhn.astype(BF16)
                dyg = dy[:, gc]
                eEg = eE[:, gc]
                wg = wdec[:, gc]
                xdg = x_dt[:, gc]
                CH = _dot(Cg, htb)
                dCHb = (dyg * eEg).astype(BF16)
                dC = _dot_nt(dCHb, htb)
                dl = jnp.exp(total[:, gc])
                dht_prev = _dot_tn(Cg, dCHb) + dl * dhn
                dtot = jnp.sum(dhn * htg, axis=0, keepdims=True) * dl
                dxw = _dot(Bg, dhnb)
                dB = _dot_nt((xdg * wg).astype(BF16), dhnb)
                dwd = dxw * xdg * wg
                dtot = dtot + jnp.sum(dwd, axis=0, keepdims=True)
                dE_s[:, gc] = dyg * eEg * CH - dwd + last_row * dtot
                dxdt_s[:, gc] = dxw * wg
                dcb2 = jnp.zeros((CHUNK, LANE), F32)
                for jj in range(GROUP_W // LANE):
                    pc = slice(g * GROUP_W + jj * LANE, g * GROUP_W + (jj + 1) * LANE)
                    Ej = E[:, pc]
                    e2 = jnp.sum(Ej * diag, axis=0, keepdims=True)
                    Lp = jnp.exp(jnp.where(causal, Ej - e2, -1e30))
                    Mp = cb2 * Lp
                    xj = x_dt[:, pc]
                    xbd = jnp.concatenate([xj * lo, xj * hi], axis=0).astype(BF16)
                    dyj = dy[:, pc].astype(BF16)
                    dMp = _dot_nt(dyj, xbd)
                    dxbd = _dot_tn(Mp.astype(BF16), dyj)
                    dxdt_s[:, pc] += dxbd[0:CHUNK, :] * lo + dxbd[CHUNK:2 * CHUNK, :] * hi
                    dcb2 = dcb2 + dMp * Lp
                    dseg = dMp * Mp
                    dE_s[:, pc] += dseg - diag * jnp.sum(dseg, axis=0, keepdims=True)
                dcb2b = dcb2.astype(BF16)
                dC = dC + _dot(dcb2b, B2)
                dB2 = _dot_tn(dcb2b, Cg)
                dB = dB + dB2[0:CHUNK, :] + dB2[CHUNK:2 * CHUNK, :]
                dxc_s[:, bcol] = dB
                dxc_s[:, ccol] = dC
                dht_ref[:, gc] = dht_prev
            dx_dt = dxdt_s[...]
            dxc_s[:, 0:D] += dx_dt * dtE
            red = _sel_right(jnp.concatenate([dE_s[...], dx_dt * xs], axis=0), expT_ref[...])
            da = _sel_left(triT_ref[...], red[0:CHUNK, :])
            ddtv = red[CHUNK:2 * CHUNK, :] + da * A
            dAacc_ref[0:1, :] += jnp.sum(da * dtv, axis=0, keepdims=True)
            ddtr = ddtv * _sigmoid(dtr)
            ddtb_ref[0:1, :] += jnp.sum(ddtr, axis=0, keepdims=True)
            dp_ref[rows, D + XBC_W:D + XBC_W + DT_W] = ddtr.astype(BF16)
            dpre = dxc_s[...] * (sg * (1.0 + pre * (1.0 - sg)))
            dpre_ref[rows, :] = dpre
            dcb_ref[0:1, :] += jnp.sum(dpre, axis=0, keepdims=True)
            for k in range(CONV_K):
                dcw_ref[k:k + 1, :] += jnp.sum(dpre * q["taps"][k], axis=0, keepdims=True)
        dxbc = jnp.zeros((T, XBC_W), F32)
        for k in range(CONV_K):
            dxbc = dxbc + cw_ref[k:k + 1, :] * dpre_ref[pl.ds(CONV_K - 1 - k, T), :]
        dp_ref[:, D:D + XBC_W] = dxbc.astype(BF16)
        dp_ref[:, SEG_SSD[1]:SSD_W] = jnp.zeros((T, SSD_W - SEG_SSD[1]), BF16)
        dpre_ref[T:T + HALO, :] = dpre_ref[0:HALO, :]

        @pl.when(i == nsteps - 1)
        def _():
            dalog_ref[...] = dAacc_ref[...] * (-jnp.exp(alog_ref[...]))
            dD_ref[...] = _dot(dDacc_ref[...], expT_ref[...].astype(F32), precision=HI)

    full = lambda a: pl.BlockSpec(a.shape, lambda i: (0,) * a.ndim)
    hb = T // HALO_BLK
    rev = lambda i: nsteps - 1 - i
    acc = lambda w: pl.BlockSpec((8, w), lambda i: (0, 0))
    return pl.pallas_call(
        body, name="ssd_bwd", grid=(nsteps,),
        in_specs=[pl.BlockSpec((T, D), lambda i: (rev(i), OFF_ZB // D)),
                  pl.BlockSpec((T, XBC_W), lambda i: (rev(i), OFF_XBC // XBC_W)),
                  pl.BlockSpec((HALO_BLK, XBC_W), lambda i: (jnp.maximum(rev(i) * hb - 1, 0), OFF_XBC // XBC_W)),
                  pl.BlockSpec((T, DT_W), lambda i: (rev(i), OFF_DT // DT_W)),
                  pl.BlockSpec((T, D), lambda i: (rev(i), 0)), pl.BlockSpec((T, D), lambda i: (rev(i), 0)),
                  pl.BlockSpec((ncl, STATE, D), lambda i: (rev(i), 0, 0)),
                  full(conv_w), full(conv_b), full(dtb_p), full(alog_p), full(d_exp), full(norm_w),
                  full(tri), full(triT), full(expand), full(expandT), full(shift)],
        out_specs=[pl.BlockSpec((T, SSD_W), lambda i: (rev(i), 0)),
                   acc(XBC_W), acc(XBC_W), acc(DT_W), acc(DT_W), acc(DT_W), acc(D)],
        out_shape=[jax.ShapeDtypeStruct((S, SSD_W), BF16),
                   jax.ShapeDtypeStruct((8, XBC_W), F32), jax.ShapeDtypeStruct((8, XBC_W), F32),
                   jax.ShapeDtypeStruct((8, DT_W), F32), jax.ShapeDtypeStruct((8, DT_W), F32),
                   jax.ShapeDtypeStruct((8, DT_W), F32), jax.ShapeDtypeStruct((8, D), F32)],
        scratch_shapes=[pltpu.VMEM((STATE, D), F32), pltpu.VMEM((HALO_BLK + T, XBC_W), BF16), pltpu.VMEM((T + HALO_BLK, XBC_W), F32),
                        pltpu.VMEM((CHUNK, D), F32), pltpu.VMEM((CHUNK, D), F32), pltpu.VMEM((CHUNK, D), F32),
                        pltpu.VMEM((CHUNK, XBC_W), F32), pltpu.VMEM((8, D), F32), pltpu.VMEM((8, DT_W), F32)],
        compiler_params=_cp(("arbitrary",)),
    )(proj, proj, proj, proj, dyb, y, states, conv_w, conv_b, dtb_p, alog_p, d_exp, norm_w, tri, triT, expand, expandT, shift)


def _head(x, ya, yb, proj, target, gate_b, wout, fw, *, tm):
    S = x.shape[0]

    def body(x_ref, ya_ref, yb_ref, gl0_ref, gl1_ref, t_ref, gb_ref, w_ref, fw_ref,
             dh_ref, dhb_ref, mb_ref, dya_ref, dyb_ref, dgl_ref, loss_ref, dfw_ref, dgb_ref):
        @pl.when(pl.program_id(0) == 0)
        def _():
            loss_ref[...] = jnp.zeros_like(loss_ref)
            dfw_ref[...] = jnp.zeros_like(dfw_ref)
            dgb_ref[...] = jnp.zeros_like(dgb_ref)

        ya_v = ya_ref[...].astype(F32)
        yb_v = yb_ref[...].astype(F32)
        g0 = _sigmoid(gl0_ref[...].astype(F32) + gb_ref[:, 0:D])
        g1 = _sigmoid(gl1_ref[...].astype(F32) + gb_ref[:, D:2 * D])
        mb = (g0 * ya_v + g1 * yb_v).astype(BF16)
        mb_ref[...] = mb
        h = x_ref[...] + _dot(mb, w_ref[...])
        r = lax.rsqrt(jnp.mean(h * h, axis=-1, keepdims=True) + EPS)
        hn = h * r
        err = hn * fw_ref[...] - t_ref[...]
        loss_ref[...] += 0.5 * jnp.sum(jnp.mean(err * err, axis=-1, keepdims=True))
        dyf = err * (1.0 / D)
        dfw_ref[0:1, :] += jnp.sum(dyf * hn, axis=0, keepdims=True)
        dhn = dyf * fw_ref[...]
        dh = r * (dhn - hn * jnp.mean(dhn * hn, axis=-1, keepdims=True))
        dh_ref[...] = dh
        dhb = dh.astype(BF16)
        dhb_ref[...] = dhb
        dm = _dot_nt(dhb, w_ref[...])
        dya_ref[...] = (dm * g0).astype(BF16)
        dyb_ref[...] = (dm * g1).astype(BF16)
        dgl0 = dm * ya_v * g0 * (1.0 - g0)
        dgl1 = dm * yb_v * g1 * (1.0 - g1)
        dgl_ref[:, 0:D] = dgl0.astype(BF16)
        dgl_ref[:, D:2 * D] = dgl1.astype(BF16)
        dgb_ref[0:1, 0:D] += jnp.sum(dgl0, axis=0, keepdims=True)
        dgb_ref[0:1, D:2 * D] += jnp.sum(dgl1, axis=0, keepdims=True)

    row = pl.BlockSpec((tm, D), lambda i: (i, 0))
    seg = lambda off: pl.BlockSpec((tm, D), lambda i: (i, off // D))
    full = lambda a: pl.BlockSpec(a.shape, lambda i: (0,) * a.ndim)
    acc = lambda w: pl.BlockSpec((8, w), lambda i: (0, 0))
    return pl.pallas_call(
        body, name="head", grid=(S // tm,),
        in_specs=[row, row, row, seg(OFF_G0), seg(OFF_G1), row, full(gate_b), full(wout), full(fw)],
        out_specs=[row, row, row, row, row, pl.BlockSpec((tm, 2 * D), lambda i: (i, 0)), acc(LANE), acc(D), acc(2 * D)],
        out_shape=[jax.ShapeDtypeStruct((S, D), F32), jax.ShapeDtypeStruct((S, D), BF16), jax.ShapeDtypeStruct((S, D), BF16),
                   jax.ShapeDtypeStruct((S, D), BF16), jax.ShapeDtypeStruct((S, D), BF16), jax.ShapeDtypeStruct((S, 2 * D), BF16),
                   jax.ShapeDtypeStruct((8, LANE), F32), jax.ShapeDtypeStruct((8, D), F32), jax.ShapeDtypeStruct((8, 2 * D), F32)],
        compiler_params=_cp(("arbitrary",)),
    )(x, ya, yb, proj, proj, target, gate_b, wout, fw)


def _adam_update(g, w_ref, m_ref, v_ref, g_ref, d_ref, m2_ref, v2_ref):
    m2 = ADAM_B1 * m_ref[...] + (1.0 - ADAM_B1) * g
    v2 = ADAM_B2 * v_ref[...] + (1.0 - ADAM_B2) * (g * g)
    m_hat = m2 / (1.0 - ADAM_B1 ** ADAM_STEP)
    v_hat = v2 / (1.0 - ADAM_B2 ** ADAM_STEP)
    g_ref[...] = g
    d_ref[...] = -ADAM_LR * (m_hat / (jnp.sqrt(v_hat) + ADAM_EPS) + ADAM_WD * w_ref[...])
    m2_ref[...] = m2
    v2_ref[...] = v2


def _adamw_own(me, own, landed, w, m, v, *, tr, tc, name):
    _, R, C = landed.shape
    assert R % tr == 0 and C % tc == 0, (name, R, C, tr, tc)

    def body(me_ref, own_ref, p_ref, w_ref, m_ref, v_ref, g_ref, d_ref, m2_ref, v2_ref):
        mine = own_ref[0].astype(F32)
        g = jnp.where(me_ref[0] == 0, mine, p_ref[0].astype(F32))
        for k in range(1, N_DEV):
            g = g + jnp.where(me_ref[0] == k, mine, p_ref[k].astype(F32))
        _adam_update(g, w_ref, m_ref, v_ref, g_ref, d_ref, m2_ref, v2_ref)

    tile = pl.BlockSpec((tr, tc), lambda i, j, me_ref: (i, j))
    return pl.pallas_call(
        body, name=name,
        grid_spec=pltpu.PrefetchScalarGridSpec(
            num_scalar_prefetch=1, grid=(R // tr, C // tc),
            in_specs=[pl.BlockSpec((1, tr, tc), lambda i, j, me_ref: (me_ref[0], i, j)),
                      pl.BlockSpec((N_DEV, tr, tc), lambda i, j, me_ref: (0, i, j)), tile, tile, tile],
            out_specs=[tile, tile, tile, tile]),
        out_shape=[jax.ShapeDtypeStruct((R, C), F32)] * 4,
        compiler_params=_cp(("parallel", "parallel")),
    )(me, own, landed, w, m, v)


def _adamw(parts, w, m, v, *, tr, name):
    _, R, C = parts.shape
    assert R % tr == 0, (name, R, tr)

    def body(p_ref, w_ref, m_ref, v_ref, g_ref, d_ref, m2_ref, v2_ref):
        g = p_ref[0].astype(F32)
        for k in range(1, N_DEV):
            g = g + p_ref[k].astype(F32)
        _adam_update(g, w_ref, m_ref, v_ref, g_ref, d_ref, m2_ref, v2_ref)

    row = pl.BlockSpec((tr, C), lambda i: (i, 0))
    return pl.pallas_call(
        body, name=name, grid=(R // tr,),
        in_specs=[pl.BlockSpec((N_DEV, tr, C), lambda i: (0, i, 0)), row, row, row],
        out_specs=[row, row, row, row],
        out_shape=[jax.ShapeDtypeStruct((R, C), F32)] * 4,
        compiler_params=_cp(("parallel",)),
    )(parts, w, m, v)


def _place():
    x, y, c = lax.axis_index("x"), lax.axis_index("y"), lax.axis_index("c")
    return x, y, c


def _all_gather(arrs, *, name):
    n = len(arrs)

    def body(*refs):
        ins, outs = refs[:n], refs[n:2 * n]
        send_sems, recv_sems, local_sems = refs[2 * n:]
        x, y, c = _place()
        me, sibling = (x, y, c), (x, y, 1 - c)
        chips = [(1 - x, y), (x, 1 - y), (1 - x, 1 - y)]

        def idx(px, py, pc):
            return 4 * px + 2 * py + pc

        def copy(k, a, block, to, src=None):
            slab = outs[a].at[idx(*block)]
            return pltpu.make_async_remote_copy(
                src_ref=slab if src is None else src, dst_ref=slab,
                send_sem=send_sems.at[k, a], recv_sem=recv_sems.at[k, a], device_id=to, device_id_type=MESH)

        mine = [pltpu.make_async_copy(ins[a], outs[a].at[idx(*me)], local_sems.at[a]) for a in range(n)]
        for cp in mine:
            cp.start()
        first = []
        for a in range(n):
            first.append(copy(0, a, me, sibling, src=ins[a]))
            first += [copy(1 + j, a, me, (*chip, c), src=ins[a]) for j, chip in enumerate(chips)]
        for cp in first:
            cp.start()
        passed = []
        for j, chip in enumerate(chips):
            for a in range(n):
                copy(1 + j, a, (*chip, c), me).wait_recv()
                fwd = copy(4 + j, a, (*chip, c), sibling)
                fwd.start()
                passed.append(fwd)
        for a in range(n):
            copy(0, a, sibling, me).wait_recv()
            for j, chip in enumerate(chips):
                copy(4 + j, a, (*chip, 1 - c), me).wait_recv()
        for cp in first + passed:
            cp.wait_send()
        for cp in mine:
            cp.wait()

    anyspec = pl.BlockSpec(memory_space=pl.ANY)
    return pl.pallas_call(
        body, name=name,
        in_specs=[anyspec] * n, out_specs=[anyspec] * n,
        out_shape=[jax.ShapeDtypeStruct((N_DEV,) + a.shape, a.dtype) for a in arrs],
        scratch_shapes=[pltpu.SemaphoreType.DMA((7, n)), pltpu.SemaphoreType.DMA((7, n)), pltpu.SemaphoreType.DMA((n,))],
    )(*arrs)


W_ROWS = SEG_SSD[0] + SSD_PAD_W


GROUP = 16
INTERIOR = 1920


def _interior(k):
    lo = -(-(k * SHARD_IN) // GROUP) * GROUP
    hi = ((k + 1) * SHARD_IN) // GROUP * GROUP
    return lo, hi


def _dest_row(r):
    return r if r < 6144 else (r - 6144 + SEG_SSD[0] if r < 11296 else r - 11296 + SEG_GATE[0])


def _shard_pieces(k):
    lo_k, hi_k = _interior(k)
    out = []
    for lo, hi in ((0, 6144), (6144, 11296), (11296, W_IN)):
        a, b = max(lo, lo_k), min(hi, hi_k)
        if a < b:
            out.append((a - lo_k, b - a, _dest_row(a)))
    return out


GATHER_PARTS = 2


def _shard_parts(k):
    parts = [[] for _ in range(GATHER_PARTS)]
    for s0, n, d0 in _shard_pieces(k):
        step = -(-(n // GROUP) // GATHER_PARTS) * GROUP
        for p in range(GATHER_PARTS):
            a, b = min(p * step, n), min((p + 1) * step, n)
            if a < b:
                parts[p].append((s0 + a, b - a, d0 + a))
    return parts


def _patch_straddlers(wpT, heads, tails):
    for k in range(1, N_DEV):
        m = (k * SHARD_IN) % GROUP
        if m:
            group = jnp.concatenate([tails[k - 1, GROUP - m:], heads[k, :GROUP - m]], axis=0)
            wpT = lax.dynamic_update_slice(wpT, group, (_dest_row(k * SHARD_IN - m), 0))
    return wpT


def _gather_weights(win, head, tail, wout, cw, zeros):
    n_zero = zeros.shape[0]
    assert W_IN + n_zero == W_ROWS and W_IN % GROUP == 0
    small_in = (wout, cw, head, tail)

    def run(k, win_ref, wout_ref, cw_ref, head_ref, tail_ref, z_ref, w_out_ref, gout_ref, gcw_ref, ghead_ref, gtail_ref,
            send_sems, recv_sems, local_sems):
        x, y, c = k // 4, (k // 2) % 2, k % 2
        idx = lambda p: 4 * p[0] + 2 * p[1] + p[2]
        me, sib = (x, y, c), (x, y, 1 - c)
        xn, yn, dg = (1 - x, y, c), (x, 1 - y, c), (1 - x, 1 - y, c)
        small = ((wout_ref, gout_ref), (cw_ref, gcw_ref), (head_ref, ghead_ref), (tail_ref, gtail_ref))

        def copies(slot, block, to, part, own=False):
            kb = idx(block)
            out = []
            for j, (s0, n, d0) in enumerate(_shard_parts(kb)[part]):
                dst = w_out_ref.at[pl.ds(d0, n)]
                out.append((win_ref.at[pl.ds(s0, n)] if own else dst, dst, 2 * part + j))
            if part == 0:
                for j, (src, gathered) in enumerate(small):
                    out.append((src if own else gathered.at[kb], gathered.at[kb], 2 * GATHER_PARTS + j))
            return [pltpu.make_async_remote_copy(src_ref=s, dst_ref=d, send_sem=send_sems.at[slot, j], recv_sem=recv_sems.at[slot, j],
                                                 device_id=to, device_id_type=MESH) for s, d, j in out]

        def start(cps):
            for cp in cps:
                cp.start()
            return cps

        def arrived(slot, block, part):
            for cp in copies(slot, block, me, part):
                cp.wait_recv()

        local = [pltpu.make_async_copy(s, d, local_sems.at[j]) for j, (s, d) in enumerate(
            [(win_ref.at[pl.ds(s0, n)], w_out_ref.at[pl.ds(d0, n)]) for s0, n, d0 in _shard_pieces(k)]
            + [(src, gathered.at[k]) for src, gathered in small] + [(z_ref, w_out_ref.at[pl.ds(W_IN, n_zero)])])]
        for cp in local:
            cp.start()
        parts = range(GATHER_PARTS)
        sent = []
        for p in parts:
            sent += start(copies(0, me, sib, p, own=True)) + start(copies(1, me, xn, p, own=True)) + start(copies(2, me, yn, p, own=True))
        for p in parts:
            arrived(1, xn, p)
            sent += start(copies(4, xn, sib, p))
            if c == 1:
                sent += start(copies(3, xn, yn, p))
            arrived(2, yn, p)
            sent += start(copies(5, yn, sib, p))
            if c == 0:
                sent += start(copies(3, yn, xn, p))
        for p in parts:
            arrived(3, dg, p)
            sent += start(copies(6, dg, sib, p))
        for p in parts:
            arrived(0, sib, p)
            arrived(4, (1 - x, y, 1 - c), p)
            arrived(5, (x, 1 - y, 1 - c), p)
            arrived(6, (1 - x, 1 - y, 1 - c), p)
        for cp in sent:
            cp.wait_send()
        for cp in local:
            cp.wait()

    def body(*refs):
        x, y, c = _place()
        me = 4 * x + 2 * y + c
        for k in range(N_DEV):
            pl.when(me == k)(functools.partial(run, k, *refs))

    anyspec = pl.BlockSpec(memory_space=pl.ANY)
    n_arr = 2 * GATHER_PARTS + len(small_in)
    return pl.pallas_call(
        body, name="gather_weights", in_specs=[anyspec] * 6, out_specs=[anyspec] * 5,
        out_shape=[jax.ShapeDtypeStruct((W_ROWS, D), win.dtype)]
        + [jax.ShapeDtypeStruct((N_DEV,) + a.shape, a.dtype) for a in small_in],
        scratch_shapes=[pltpu.SemaphoreType.DMA((7, n_arr)), pltpu.SemaphoreType.DMA((7, n_arr)),
                        pltpu.SemaphoreType.DMA((n_arr + 1,))],
    )(win, wout, cw, head, tail, zeros)


_REL = [(dx, dy, dc) for dx in (0, 1) for dy in (0, 1) for dc in (0, 1)][1:]
_HBM = pl.BlockSpec(memory_space=pltpu.HBM)
_SEM = pl.BlockSpec(memory_space=pltpu.SEMAPHORE)
_EFFECT = pltpu.SideEffectType.DATAFLOW_SIDE_EFFECTING


def _peer(k):
    x, y, c = _place()
    dx, dy, dc = _REL[k]
    return (1 - x if dx else x, 1 - y if dy else y, 1 - c if dc else c)


def _exchange_start(parts, *, name):
    n = len(parts)

    def body(*refs):
        ins, lands = refs[:n], refs[n:2 * n]
        send_sems, recv_sems, token = refs[2 * n], refs[2 * n + 1], refs[-1]
        x, y, c = _place()
        me = 4 * x + 2 * y + c
        for a in range(n):
            for k in range(len(_REL)):
                px, py, pc = _peer(k)
                pltpu.make_async_remote_copy(
                    src_ref=ins[a].at[4 * px + 2 * py + pc], dst_ref=lands[a].at[me],
                    send_sem=send_sems.at[len(_REL) * a + k], recv_sem=recv_sems.at[len(_REL) * a + k],
                    device_id=(px, py, pc), device_id_type=MESH).start()
        token[...] = jnp.zeros_like(token)

    sem = pltpu.SemaphoreType.DMA((len(_REL) * n,))
    bufs = [pltpu.HBM(p.shape, p.dtype) for p in parts]
    outs = pl.pallas_call(
        body, name=name,
        out_shape=(sem, sem, *bufs, *bufs, jax.ShapeDtypeStruct((8, LANE), F32)),
        in_specs=(_HBM,) * (2 * n), out_specs=(_SEM, _SEM, *(_HBM,) * (2 * n), pl.BlockSpec(memory_space=pltpu.VMEM)),
        input_output_aliases={i: 2 + i for i in range(2 * n)},
        compiler_params=pltpu.CompilerParams(has_side_effects=_EFFECT),
    )(*[pltpu.with_memory_space_constraint(p, pltpu.HBM) for p in parts],
      *[pltpu.with_memory_space_constraint(lax.empty(p.shape, p.dtype), pltpu.HBM) for p in parts])
    return outs[0], outs[1], outs[2:2 + n], outs[2 + n:2 + 2 * n], outs[-1]


def _exchange_wait(send_sems, recv_sems, parts, lands, after, *, name):
    n = len(parts)

    def body(*refs):
        ins, lands_ = refs[:n], refs[n:2 * n]
        ssem, rsem = refs[2 * n], refs[2 * n + 1]
        for a in range(n):
            for k in range(len(_REL)):
                px, py, pc = _peer(k)
                p = 4 * px + 2 * py + pc
                cp = pltpu.make_async_remote_copy(
                    src_ref=ins[a].at[p], dst_ref=lands_[a].at[p],
                    send_sem=ssem.at[len(_REL) * a + k], recv_sem=rsem.at[len(_REL) * a + k],
                    device_id=(px, py, pc), device_id_type=MESH)
                cp.wait_send()
                cp.wait_recv()

    bufs = [pltpu.HBM(p.shape, p.dtype) for p in parts]
    outs = pl.pallas_call(
        body, name=name, out_shape=(*bufs, *bufs),
        in_specs=(*(_HBM,) * (2 * n), _SEM, _SEM, pl.BlockSpec(memory_space=pl.ANY)), out_specs=(_HBM,) * (2 * n),
        input_output_aliases={i: i for i in range(2 * n)},
        compiler_params=pltpu.CompilerParams(has_side_effects=_EFFECT),
    )(*parts, *lands, send_sems, recv_sems, after)
    return outs[:n], outs[n:]


WEIGHTS = ('norm_w', 'w_in', 'gate_b', 'sgu_norm_g', 'sgu_norm_b', 'sgu_w', 'sgu_b', 'conv_w', 'conv_b', 'dt_bias', 'A_log',
           'D_skip', 'ssd_norm_w', 'w_out', 'final_norm_w')
SHARDED = ('w_in', 'conv_w', 'w_out')
PACK_ROW = 8 * LANE


def _constants():
    tri = np.tril(np.ones((CHUNK, CHUNK), np.float32))
    expand = np.zeros((DT_W, D), np.float32)
    for h in range(HEADS):
        expand[h, h * HEADDIM:(h + 1) * HEADDIM] = 1.0
    sel = np.zeros((D, LANE), np.float32)
    for g in range(SGU_GROUPS):
        sel[g * LANE:(g + 1) * LANE, g] = 1.0
    pos_chunk = np.arange(SGU_BLOCK) // CHUNK
    mask = (pos_chunk[None, :] <= pos_chunk[:, None]).astype(np.float32)
    shift = np.zeros(((CONV_K - 1) * CHUNK, HALO_BLK + CHUNK), np.float32)
    for kk in range(CONV_K - 1):
        for t in range(CHUNK):
            shift[kk * CHUNK + t, HALO_BLK - (CONV_K - 1) + t + kk] = 1.0
    return dict(tri=jnp.asarray(tri, BF16), triT=jnp.asarray(tri.T.copy(), BF16), expand=jnp.asarray(np.tile(expand, (3, 1)), BF16),
                shift=jnp.asarray(shift, BF16),
                expandT=jnp.asarray(expand.T.copy(), BF16), sel=jnp.asarray(sel), mask=jnp.asarray(mask))


def _to_shards(segs):
    starts = np.cumsum([0] + [n for _, n in segs])
    assert starts[-1] == W_IN
    slabs = []
    for k in range(N_DEV):
        pieces = []
        for (s, n), s0 in zip(segs, starts[:-1]):
            lo, hi = max(k * SHARD_IN, s0), min((k + 1) * SHARD_IN, s0 + n)
            if lo < hi:
                pieces.append(s[lo - s0:hi - s0])
        slabs.append(jnp.concatenate(pieces, axis=0))
    return jnp.stack(slabs)


def _local_step(x2, tgt, wpT, wout, cw, p, exchange_small, exchange):
    S = x2.shape[0]
    k = _constants()
    xn = _norm_fwd(x2, p['norm_w'], tm=min(512, S))
    proj = _matmul(xn, wpT, trans_b=True, out_dtype=BF16, tm=min(1024, S), tn=2048, tk=D, name="in_proj")
    wm32 = p['sgu_w'][0] * k['mask']
    wm = wm32.astype(BF16)
    wmT = jnp.swapaxes(wm32, 1, 2).astype(BF16)
    bias_full = jnp.repeat(p['sgu_b'][0].T, LANE, axis=1)
    tm_sgu = min(256, S)
    ya = _sgu_fwd(proj, p['sgu_norm_g'], p['sgu_norm_b'], wm, bias_full, tm=tm_sgu)
    pad32 = lambda a: jnp.pad(a, ((0, 0), (0, DT_W - HEADS)))
    dtb_p, alog_p = pad32(p['dt_bias']), pad32(p['A_log'])
    d_exp = jnp.repeat(p['D_skip'], HEADDIM, axis=1)
    ssd_args = (cw, p['conv_b'], dtb_p, alog_p, d_exp, p['ssd_norm_w'])
    y, yb, states = _ssd_fwd(proj, *ssd_args, k['tri'], k['expand'], k['shift'])
    dh, dhb, mb, dya, dyb, dgl, loss, dfw, dgb = _head(
        x2, ya, yb, proj, tgt, p['gate_b'], wout, p['final_norm_w'][None, :], tm=min(256, S))
    dsgu, dws, dbsT, dsg, dsb = _sgu_bwd(proj, dya, p['sgu_norm_g'], p['sgu_norm_b'], wm, wmT, bias_full, k['mask'], k['sel'],
                                         tm=tm_sgu)
    dssd, dcw, dcb, ddtb, dalog, dD, dnw = _ssd_bwd(proj, dyb, y, states, *ssd_args, k['tri'], k['triT'], k['expand'], k['expandT'],
                                                    k['shift'])
    grads = dict(
        gate_b=dgb[0:1], sgu_norm_g=dsg[0:1], sgu_norm_b=dsb[0:1], sgu_w=dws[None],
        sgu_b=dbsT[:, :SGU_GROUPS].T[None], conv_w=dcw[0:CONV_K][None], conv_b=dcb[0:1], dt_bias=ddtb[0:1, :HEADS],
        A_log=dalog[0:1, :HEADS], D_skip=dD[0:1, :HEADS], ssd_norm_w=dnw[0:1], final_norm_w=dfw[0])
    token = exchange_small(loss[0, 0], grads)
    tk = min(4096, S)
    tn = 1024
    dwT_sgu = _matmul(dsgu, xn, trans_a=True, out_dtype=BF16, tm=1024, tn=tn, tk=tk, after=token, name="dw_in_sgu")
    dwT_gate = _matmul(dgl, xn, trans_a=True, out_dtype=BF16, tm=1024, tn=tn, tk=tk, name="dw_in_gate")
    dwT_ssd = _matmul(dssd, xn, trans_a=True, out_dtype=BF16, tm=1024, tn=tn, tk=tk, name="dw_in_ssd")
    dw_out = _matmul(mb, dhb, trans_a=True, out_dtype=BF16, tm=1024, tn=tn, tk=tk, name="dw_out")
    token = exchange([(dwT_sgu, SEG_SGU[1]), (dwT_ssd, W_IN - SEG_SSD[0]), (dwT_gate, SEG_GATE[1])], dw_out)
    tm = min(1024, S)
    dxn = _matmul(dsgu, wpT, tm=tm, tn=tn, tk=3072, after=token, name="dxn_sgu")
    dxn = _matmul(dgl, wpT, b_koff=SEG_GATE[0] // 2048, tm=tm, tn=tn, tk=2048, add=dxn, name="dxn_gate")
    dxn = _matmul(dssd, wpT, b_koff=SEG_SSD[0] // 2048, tm=tm, tn=tn, tk=2048, add=dxn, name="dxn_ssd")
    grad_x, dnorm = _norm_bwd(x2, p['norm_w'], dxn, dh, tm=min(256, S))
    return grad_x, dnorm[0:1]


def _pack(arrs):
    rows, offs, r = [], [], 0
    for a in arrs:
        n = a.size
        nr = -(-n // PACK_ROW) * 8
        rows.append(jnp.pad(a.reshape(-1).astype(F32), (0, nr * LANE - n)).reshape(nr, LANE))
        offs.append(r)
        r += nr
    return jnp.concatenate(rows, axis=0), offs


def kernel(x, norm_w, w_in, gate_b, sgu_norm_g, sgu_norm_b, sgu_w, sgu_b, conv_w, conv_b, dt_bias, A_log, D_skip, ssd_norm_w, w_out, final_norm_w, loss_target, m_norm_w, m_w_in, m_gate_b, m_sgu_norm_g, m_sgu_norm_b, m_sgu_w, m_sgu_b, m_conv_w, m_conv_b, m_dt_bias, m_A_log, m_D_skip, m_ssd_norm_w, m_w_out, m_final_norm_w, v_norm_w, v_w_in, v_gate_b, v_sgu_norm_g, v_sgu_norm_b, v_sgu_w, v_sgu_b, v_conv_w, v_conv_b, v_dt_bias, v_A_log, v_D_skip, v_ssd_norm_w, v_w_out, v_final_norm_w):
    w = dict(norm_w=norm_w, w_in=w_in, gate_b=gate_b, sgu_norm_g=sgu_norm_g, sgu_norm_b=sgu_norm_b, sgu_w=sgu_w, sgu_b=sgu_b,
             conv_w=conv_w, conv_b=conv_b, dt_bias=dt_bias, A_log=A_log, D_skip=D_skip, ssd_norm_w=ssd_norm_w, w_out=w_out,
             final_norm_w=final_norm_w)
    m = dict(norm_w=m_norm_w, w_in=m_w_in, gate_b=m_gate_b, sgu_norm_g=m_sgu_norm_g, sgu_norm_b=m_sgu_norm_b, sgu_w=m_sgu_w,
             sgu_b=m_sgu_b, conv_w=m_conv_w, conv_b=m_conv_b, dt_bias=m_dt_bias, A_log=m_A_log, D_skip=m_D_skip,
             ssd_norm_w=m_ssd_norm_w, w_out=m_w_out, final_norm_w=m_final_norm_w)
    v = dict(norm_w=v_norm_w, w_in=v_w_in, gate_b=v_gate_b, sgu_norm_g=v_sgu_norm_g, sgu_norm_b=v_sgu_norm_b, sgu_w=v_sgu_w,
             sgu_b=v_sgu_b, conv_w=v_conv_w, conv_b=v_conv_b, dt_bias=v_dt_bias, A_log=v_A_log, D_skip=v_D_skip,
             ssd_norm_w=v_ssd_norm_w, w_out=v_w_out, final_norm_w=v_final_norm_w)
    me = 4 * lax.axis_index("x") + 2 * lax.axis_index("y") + lax.axis_index("c")
    shard_cw = XBC_W // N_DEV

    tpose = lambda a: jnp.swapaxes(a[0], 0, 1)
    wT = tpose(w_in).astype(BF16)
    first_group = (GROUP - (me * SHARD_IN) % GROUP) % GROUP
    window = lax.dynamic_slice(jnp.pad(wT, ((0, GROUP), (0, 0))), (first_group, 0), (INTERIOR, D))
    wpT, g_out, g_cw, heads, tails = _gather_weights(window, wT[:GROUP], wT[SHARD_IN - GROUP:], w_out[0].astype(BF16),
                                                     conv_w[0], jnp.zeros((W_ROWS - W_IN, D), BF16))
    wpT = _patch_straddlers(wpT, heads, tails)
    wout_full = g_out.reshape(D, D)
    cw_full = jnp.swapaxes(g_cw, 0, 1).reshape(CONV_K, XBC_W)

    flight = {}

    small = [n for n in WEIGHTS if n not in SHARDED and n != 'norm_w']
    early = {}

    def exchange_small(loss_part, grads):
        early['packed'], early['offs'] = _pack([grads[n] for n in small] + [loss_part, grads['conv_w']])
        parts = [jnp.broadcast_to(early['packed'][None], (N_DEV,) + early['packed'].shape)]
        early['sems'], early['rsems'], early['parts'], early['lands'], token = _exchange_start(parts, name="small_start")
        return token

    def exchange(dw_inT_segs, dw_out):
        parts = [_to_shards(dw_inT_segs), dw_out.reshape(N_DEV, D // N_DEV, D)]
        flight['sems'], flight['rsems'], flight['parts'], flight['lands'], token = _exchange_start(parts, name="exchange_start")
        return token

    grad_x, dnorm = _local_step(x[0], loss_target[0], wpT, wout_full, cw_full, w, exchange_small, exchange)
    _, (land_small,) = _exchange_wait(early['sems'], early['rsems'], early['parts'], early['lands'], grad_x, name="small_wait")
    (own_in, own_out), (land_in, land_out) = _exchange_wait(
        flight['sems'], flight['rsems'], flight['parts'], flight['lands'], grad_x, name="exchange_wait")
    me_arr = jnp.reshape(me, (1,)).astype(jnp.int32)
    res = {}
    res['w_in'] = [jnp.swapaxes(o, 0, 1) for o in _adamw_own(
        me_arr, own_in, land_in, tpose(w_in), tpose(m_w_in), tpose(v_w_in), tr=SHARD_IN, tc=256, name="adamw_w_in")]
    res['w_out'] = _adamw_own(me_arr, own_out, land_out, w_out[0], m_w_out[0], v_w_out[0], tr=128, tc=D, name="adamw_w_out")

    (norm_parts,) = _all_gather([_pack([dnorm])[0]], name="gather_norm")
    norm_outs = _adamw(norm_parts, *[_pack([d['norm_w']])[0] for d in (w, m, v)], tr=norm_parts.shape[1], name="adamw_norm")
    res['norm_w'] = [o.reshape(-1)[:D].reshape(w['norm_w'].shape) for o in norm_outs]

    offs = early['offs']
    gathered = lax.dynamic_update_slice(land_small, early['packed'][None], (me, 0, 0))
    off_loss, off_cw = offs[-2], offs[-1]
    cw_parts = gathered[:, off_cw:, :].reshape(N_DEV, CONV_K, XBC_W)
    cw_parts = lax.dynamic_slice_in_dim(cw_parts, me * shard_cw, shard_cw, axis=2)
    cw_rows = _pack([cw_parts[0]])[0].shape[0]
    cw_parts = jnp.pad(cw_parts.reshape(N_DEV, -1), ((0, 0), (0, cw_rows * LANE - CONV_K * shard_cw))).reshape(N_DEV, cw_rows, LANE)
    parts = jnp.concatenate([gathered[:, :off_cw, :], cw_parts], axis=1)
    zero = jnp.zeros((), F32)
    packs = [_pack([d[n] for n in small] + [zero, d['conv_w']])[0] for d in (w, m, v)]
    outs = _adamw(parts, *packs, tr=parts.shape[1], name="adamw_small")

    def unpack(o, name):
        if name == 'conv_w':
            return o[off_cw:off_cw + cw_rows].reshape(-1)[:CONV_K * shard_cw].reshape(w['conv_w'].shape)
        r0 = offs[small.index(name)]
        n = w[name].size
        return o[r0:r0 + -(-n // PACK_ROW) * 8].reshape(-1)[:n].reshape(w[name].shape)

    for n in small + ['conv_w']:
        res[n] = [unpack(o, n) for o in outs]
    for n in ('w_in', 'w_out'):
        res[n] = [o[None] for o in res[n]]
    loss = outs[0][off_loss, 0]
    return (loss, grad_x[None], *[res[n][0] for n in WEIGHTS], *[res[n][1] for n in WEIGHTS],
            *[res[n][2] for n in WEIGHTS], *[res[n][3] for n in WEIGHTS])
```

```python
import functools

import numpy as np
import jax
import jax.numpy as jnp
from jax import lax
from jax.experimental import pallas as pl
from jax.experimental.pallas import tpu as pltpu

F32 = jnp.float32
BF16 = jnp.bfloat16
HI = lax.Precision.HIGHEST
MESH = pl.DeviceIdType.MESH

D = 2048
EPS = 1e-5
SGU_BLOCK = 128
SGU_GROUPS = 16
CHUNK = 64
HEADS = 32
HEADDIM = 64
SSD_GROUPS = 4
GROUP_W = D // SSD_GROUPS
STATE = 128
CONV_K = 4
XBC_W = D + 2 * SSD_GROUPS * STATE
W_IN = 15392
N_DEV = 8
SHARD_IN = W_IN // N_DEV
ADAM_LR, ADAM_B1, ADAM_B2, ADAM_EPS, ADAM_WD, ADAM_STEP = 0.001, 0.9, 0.999, 1e-08, 0.01, 10

LANE = 128
DT_W = LANE
OFF_U, OFF_V, OFF_ZA = 0, 2048, 4096
OFF_G0, OFF_G1, OFF_ZB, OFF_XBC, OFF_DT = 0, 2048, 4096, 6144, 9216
SEG_SGU = (0, 6144)
SEG_GATE = (6144, 4096)
SEG_SSD = (10240, 5248)
WP = SEG_SSD[0] + SEG_SSD[1]
SSD_PAD_W = 6144
VMEM_LIMIT = 56 * 1024 * 1024


def _cp(sem=None, vmem=VMEM_LIMIT):
    return pltpu.CompilerParams(dimension_semantics=sem, vmem_limit_bytes=vmem)


def _sigmoid(x):
    return 1.0 / (1.0 + jnp.exp(-x))


def _softplus(x):
    return jnp.maximum(x, 0.0) + jnp.log(1.0 + jnp.exp(-jnp.abs(x)))


def _dot(a, b, precision=None):
    return jnp.dot(a, b, preferred_element_type=F32, precision=precision)


def _dot_nt(a, b, precision=None):
    return lax.dot_general(a, b, (((1,), (1,)), ((), ())), preferred_element_type=F32, precision=precision)


def _dot_tn(a, b, precision=None):
    return lax.dot_general(a, b, (((0,), (0,)), ((), ())), preferred_element_type=F32, precision=precision)


def _split3(a):
    hi = a.astype(BF16)
    r = a - hi.astype(F32)
    mid = r.astype(BF16)
    return hi, mid, (r - mid.astype(F32)).astype(BF16)


def _sel_right(a, sel01):
    m = a.shape[0]
    r = _dot(jnp.concatenate(_split3(a), axis=0), sel01)
    return (r[0:m] + r[m:2 * m]) + r[2 * m:3 * m]


def _sel_right_k(a, sel01_x3):
    return _dot(jnp.concatenate(_split3(a), axis=1), sel01_x3)


def _sel_left(sel01, a):
    n = a.shape[1]
    r = _dot(sel01, jnp.concatenate(_split3(a), axis=1))
    return (r[:, 0:n] + r[:, n:2 * n]) + r[:, 2 * n:3 * n]


def _matmul(a, b, *, trans_a=False, trans_b=False, b_koff=0, b_joff=0, n=None, out_dtype=F32, tm, tn, tk, add=None, after=None,
            name):
    K, M = a.shape if trans_a else a.shape[::-1]
    N = (n or b.shape[0]) if trans_b else b.shape[1]
    assert M % tm == 0 and N % tn == 0 and K % tk == 0 and not (trans_a and trans_b), (name, M, N, K, tm, tn, tk)
    nk = K // tk

    def body(*refs):
        a_ref, b_ref = refs[:2]
        add_ref = refs[2] if add is not None else None
        o_ref, acc_ref = refs[-2:]
        k = pl.program_id(2)
        if trans_a:
            part = _dot_tn(a_ref[...], b_ref[...])
        else:
            part = _dot_nt(a_ref[...], b_ref[...]) if trans_b else _dot(a_ref[...], b_ref[...])

        def result(r):
            if add_ref is not None:
                r = r + add_ref[...]
            return r.astype(out_dtype)

        if nk == 1:
            o_ref[...] = result(part)
        else:
            @pl.when(k == 0)
            def _():
                acc_ref[...] = part

            @pl.when(jnp.logical_and(k > 0, k < nk - 1))
            def _():
                acc_ref[...] += part

            @pl.when(k == nk - 1)
            def _():
                o_ref[...] = result(acc_ref[...] + part)

    in_specs = [pl.BlockSpec((tk, tm), lambda i, j, k: (k, i)) if trans_a else pl.BlockSpec((tm, tk), lambda i, j, k: (i, k)),
                pl.BlockSpec((tn, tk), lambda i, j, k: (j + b_joff, k)) if trans_b
                else pl.BlockSpec((tk, tn), lambda i, j, k: (k + b_koff, j))]
    args = [a, b]
    if add is not None:
        in_specs.append(pl.BlockSpec((tm, tn), lambda i, j, k: (i, j)))
        args.append(add)
    if after is not None:
        in_specs.append(pl.BlockSpec(memory_space=pl.ANY))
        args.append(after)
    return pl.pallas_call(
        body, name=name, grid=(M // tm, N // tn, nk), in_specs=in_specs,
        out_specs=pl.BlockSpec((tm, tn), lambda i, j, k: (i, j)),
        out_shape=jax.ShapeDtypeStruct((M, N), out_dtype),
        scratch_shapes=[pltpu.VMEM((tm, tn), F32)],
        compiler_params=_cp(("parallel", "parallel", "arbitrary")),
    )(*args)


def _norm_fwd(x, w, *, tm):
    S = x.shape[0]

    def body(x_ref, w_ref, o_ref):
        xv = x_ref[...]
        r = lax.rsqrt(jnp.mean(xv * xv, axis=-1, keepdims=True) + EPS)
        o_ref[...] = (xv * r * w_ref[...]).astype(BF16)

    return pl.pallas_call(
        body, name="norm_fwd", grid=(S // tm,),
        in_specs=[pl.BlockSpec((tm, D), lambda i: (i, 0)), pl.BlockSpec((1, D), lambda i: (0, 0))],
        out_specs=pl.BlockSpec((tm, D), lambda i: (i, 0)),
        out_shape=jax.ShapeDtypeStruct((S, D), BF16), compiler_params=_cp(("parallel",)),
    )(x, w)


def _norm_bwd(x, w, dxn, dh, *, tm):
    S = x.shape[0]

    def body(x_ref, w_ref, dxn_ref, dh_ref, gx_ref, dw_ref):
        xv = x_ref[...]
        r = lax.rsqrt(jnp.mean(xv * xv, axis=-1, keepdims=True) + EPS)
        xh = xv * r
        dxn_v = dxn_ref[...]
        dxh = dxn_v * w_ref[...]
        gx_ref[...] = dh_ref[...] + r * (dxh - xh * jnp.mean(dxh * xh, axis=-1, keepdims=True))

        @pl.when(pl.program_id(0) == 0)
        def _():
            dw_ref[...] = jnp.zeros_like(dw_ref)

        dw_ref[0:1, :] += jnp.sum(dxn_v * xh, axis=0, keepdims=True)

    row = pl.BlockSpec((tm, D), lambda i: (i, 0))
    return pl.pallas_call(
        body, name="norm_bwd", grid=(S // tm,),
        in_specs=[row, pl.BlockSpec((1, D), lambda i: (0, 0)), row, row],
        out_specs=[row, pl.BlockSpec((8, D), lambda i: (0, 0))],
        out_shape=[jax.ShapeDtypeStruct((S, D), F32), jax.ShapeDtypeStruct((8, D), F32)],
        compiler_params=_cp(("arbitrary",)),
    )(x, w, dxn, dh)


def _sgu_core(u_ref, v_ref, z_ref, g_ref, b_ref, wm_ref, bias_ref, vnb_ref, mixed_ref, tm):
    v = v_ref[...].astype(F32)
    mu = jnp.mean(v, axis=-1, keepdims=True)
    vc = v - mu
    rs = lax.rsqrt(jnp.mean(vc * vc, axis=-1, keepdims=True) + EPS)
    vh = vc * rs
    vnb_ref[...] = (vh * g_ref[...] + b_ref[...]).astype(BF16)
    for blk in range(tm // SGU_BLOCK):
        rows = pl.ds(blk * SGU_BLOCK, SGU_BLOCK)
        for gi in range(SGU_GROUPS):
            cols = pl.ds(gi * LANE, LANE)
            mixed_ref[rows, cols] = _dot(wm_ref[gi], vnb_ref[rows, cols]) + bias_ref[:, cols]
    return vh, rs


def _sgu_fwd(proj, g, b, wm, bias_full, *, tm):
    S = proj.shape[0]

    def body(u_ref, v_ref, z_ref, g_ref, b_ref, wm_ref, bias_ref, y_ref, vnb_ref, mixed_ref):
        _sgu_core(u_ref, v_ref, z_ref, g_ref, b_ref, wm_ref, bias_ref, vnb_ref, mixed_ref, tm)
        z = z_ref[...].astype(F32)
        y_ref[...] = (u_ref[...].astype(F32) * mixed_ref[...] * (z * _sigmoid(z))).astype(BF16)

    seg = lambda off: pl.BlockSpec((tm, D), lambda i: (i, off // D))
    full = lambda a: pl.BlockSpec(a.shape, lambda i: (0,) * a.ndim)
    return pl.pallas_call(
        body, name="sgu_fwd", grid=(S // tm,),
        in_specs=[seg(OFF_U), seg(OFF_V), seg(OFF_ZA), full(g), full(b), full(wm), full(bias_full)],
        out_specs=pl.BlockSpec((tm, D), lambda i: (i, 0)),
        out_shape=jax.ShapeDtypeStruct((S, D), BF16),
        scratch_shapes=[pltpu.VMEM((tm, D), BF16), pltpu.VMEM((tm, D), F32)],
        compiler_params=_cp(("parallel",)),
    )(proj, proj, proj, g, b, wm, bias_full)


def _sgu_bwd(proj, dy, g, b, wm, wmT, bias_full, mask, sel, *, tm):
    S = proj.shape[0]
    nsteps = S // tm

    def body(u_ref, v_ref, z_ref, dy_ref, g_ref, b_ref, wm_ref, wmT_ref, bias_ref, mask_ref, sel_ref,
             dp_ref, dws_ref, dbs_ref, dg_ref, db_ref, vnb_ref, mixed_ref, dmb_ref, dvn_ref, dbias_ref):
        i = pl.program_id(0)

        @pl.when(i == 0)
        def _():
            dws_ref[...] = jnp.zeros_like(dws_ref)
            dg_ref[...] = jnp.zeros_like(dg_ref)
            db_ref[...] = jnp.zeros_like(db_ref)
            dbias_ref[...] = jnp.zeros_like(dbias_ref)

        vh, rs = _sgu_core(u_ref, v_ref, z_ref, g_ref, b_ref, wm_ref, bias_ref, vnb_ref, mixed_ref, tm)
        u = u_ref[...].astype(F32)
        z = z_ref[...].astype(F32)
        dy_v = dy_ref[...].astype(F32)
        mixed = mixed_ref[...]
        sg = _sigmoid(z)
        sz = z * sg
        dp_ref[:, 0:D] = (dy_v * mixed * sz).astype(BF16)
        dp_ref[:, 2 * D:3 * D] = (dy_v * u * mixed * (sg * (1.0 + z * (1.0 - sg)))).astype(BF16)
        dmixed = dy_v * u * sz
        dmb_ref[...] = dmixed.astype(BF16)
        for blk in range(tm // SGU_BLOCK):
            dbias_ref[...] += dmixed[blk * SGU_BLOCK:(blk + 1) * SGU_BLOCK, :]
        for blk in range(tm // SGU_BLOCK):
            rows = pl.ds(blk * SGU_BLOCK, SGU_BLOCK)
            for gi in range(SGU_GROUPS):
                cols = pl.ds(gi * LANE, LANE)
                dm = dmb_ref[rows, cols]
                dvn_ref[rows, cols] = _dot(wmT_ref[gi], dm)
                dws_ref[gi] += _dot_nt(dm, vnb_ref[rows, cols])
        dvn = dvn_ref[...]
        dg_ref[0:1, :] += jnp.sum(dvn * vh, axis=0, keepdims=True)
        db_ref[0:1, :] += jnp.sum(dvn, axis=0, keepdims=True)
        dvh = dvn * g_ref[...]
        dv = rs * (dvh - jnp.mean(dvh, axis=-1, keepdims=True) - vh * jnp.mean(dvh * vh, axis=-1, keepdims=True))
        dp_ref[:, D:2 * D] = dv.astype(BF16)

        @pl.when(i == nsteps - 1)
        def _():
            for gi in range(SGU_GROUPS):
                dws_ref[gi] = dws_ref[gi] * mask_ref[...]
            dbs_ref[...] = _dot(dbias_ref[...], sel_ref[...], precision=HI)

    seg = lambda off: pl.BlockSpec((tm, D), lambda i: (i, off // D))
    full = lambda a: pl.BlockSpec(a.shape, lambda i: (0,) * a.ndim)
    return pl.pallas_call(
        body, name="sgu_bwd", grid=(nsteps,),
        in_specs=[seg(OFF_U), seg(OFF_V), seg(OFF_ZA), pl.BlockSpec((tm, D), lambda i: (i, 0)),
                  full(g), full(b), full(wm), full(wmT), full(bias_full), full(mask), full(sel)],
        out_specs=[pl.BlockSpec((tm, 3 * D), lambda i: (i, 0)),
                   pl.BlockSpec((SGU_GROUPS, SGU_BLOCK, SGU_BLOCK), lambda i: (0, 0, 0)),
                   pl.BlockSpec((SGU_BLOCK, LANE), lambda i: (0, 0)),
                   pl.BlockSpec((8, D), lambda i: (0, 0)), pl.BlockSpec((8, D), lambda i: (0, 0))],
        out_shape=[jax.ShapeDtypeStruct((S, 3 * D), BF16),
                   jax.ShapeDtypeStruct((SGU_GROUPS, SGU_BLOCK, SGU_BLOCK), F32),
                   jax.ShapeDtypeStruct((SGU_BLOCK, LANE), F32),
                   jax.ShapeDtypeStruct((8, D), F32), jax.ShapeDtypeStruct((8, D), F32)],
        scratch_shapes=[pltpu.VMEM((tm, D), BF16), pltpu.VMEM((tm, D), F32), pltpu.VMEM((tm, D), BF16),
                        pltpu.VMEM((tm, D), F32), pltpu.VMEM((SGU_BLOCK, D), F32)],
        compiler_params=_cp(("arbitrary",)),
    )(proj, proj, proj, dy, g, b, wm, wmT, bias_full, mask, sel)


SSD_T = 2 * CHUNK
HALO = 8
HALO_BLK = 16


def _pair_masks():
    row = lax.broadcasted_iota(jnp.int32, (CHUNK, LANE), 0)
    lane = lax.broadcasted_iota(jnp.int32, (CHUNK, LANE), 1)
    pos = jnp.where(lane >= CHUNK, lane - CHUNK, lane)
    diag = (row == pos).astype(F32)
    causal = row >= pos
    lo = (lane < CHUNK).astype(F32)
    return diag, causal, lo, 1.0 - lo


def _ssd_chunk_fwd(c, ext_ref, shift_ref, dt_ref, cw_ref, cb_ref, dtb_ref, alog_ref, tri_ref, exp_ref):
    r0 = c * CHUNK
    win = ext_ref[pl.ds(r0, HALO_BLK + CHUNK), :]
    sh = _dot(shift_ref[...], win)
    taps = [sh[k * CHUNK:(k + 1) * CHUNK] for k in range(CONV_K - 1)] + [win[HALO_BLK:].astype(F32)]
    pre = cb_ref[...] + sum(cw_ref[k:k + 1, :] * taps[k] for k in range(CONV_K))
    sg = _sigmoid(pre)
    xc = pre * sg
    dtr = dt_ref[pl.ds(r0, CHUNK), :].astype(F32) + dtb_ref[...]
    dtv = _softplus(dtr)
    A = -jnp.exp(alog_ref[...])
    acs = _sel_left(tri_ref[...], dtv * A)
    both = _sel_right_k(jnp.concatenate([acs, dtv], axis=0), exp_ref[...])
    E, dtE = both[0:CHUNK], both[CHUNK:2 * CHUNK]
    return dict(taps=taps, pre=pre, sg=sg, xc=xc, dtr=dtr, dtv=dtv, A=A, E=E, dtE=dtE)


def _ssd_fwd(proj, conv_w, conv_b, dtb_p, alog_p, d_exp, norm_w, tri, expand, shift):
    S = proj.shape[0]
    T = SSD_T
    nsteps = S // T
    ncl = T // CHUNK

    def body(zb_ref, xbc_ref, halo_ref, dt_ref, cw_ref, cb_ref, dtb_ref, alog_ref, dexp_ref, nw_ref, tri_ref, exp_ref, shift_ref,
             y_ref, yb_ref, st_ref, ht_ref, ext_ref):
        i = pl.program_id(0)

        @pl.when(i == 0)
        def _():
            ht_ref[...] = jnp.zeros_like(ht_ref)
            ext_ref[0:HALO_BLK, :] = jnp.zeros((HALO_BLK, XBC_W), BF16)

        @pl.when(i > 0)
        def _():
            ext_ref[0:HALO_BLK, :] = halo_ref[...]

        ext_ref[HALO_BLK:HALO_BLK + T, :] = xbc_ref[...]
        diag, causal, lo, hi = _pair_masks()
        for c in range(ncl):
            q = _ssd_chunk_fwd(c, ext_ref, shift_ref, dt_ref, cw_ref, cb_ref, dtb_ref, alog_ref, tri_ref, exp_ref)
            rows = pl.ds(c * CHUNK, CHUNK)
            xc, E, dtE = q["xc"], q["E"], q["dtE"]
            xs = xc[:, 0:D]
            total = E[CHUNK - 1:CHUNK, :]
            x_dt = xs * dtE
            eE = jnp.exp(E)
            xw = x_dt * jnp.exp(total - E)
            st_ref[c] = ht_ref[...]
            for g in range(SSD_GROUPS):
                gc = slice(g * GROUP_W, (g + 1) * GROUP_W)
                Bg = xc[:, D + g * STATE:D + (g + 1) * STATE].astype(BF16)
                Cg = xc[:, D + SSD_GROUPS * STATE + g * STATE:D + SSD_GROUPS * STATE + (g + 1) * STATE].astype(BF16)
                cb2 = _dot_nt(Cg, jnp.concatenate([Bg, Bg], axis=0))
                htg = ht_ref[:, gc]
                y_ref[rows, gc] = eE[:, gc] * _dot(Cg, htg.astype(BF16)) + xs[:, gc] * dexp_ref[:, gc]
                for jj in range(GROUP_W // LANE):
                    pc = slice(g * GROUP_W + jj * LANE, g * GROUP_W + (jj + 1) * LANE)
                    Ej = E[:, pc]
                    e2 = jnp.sum(Ej * diag, axis=0, keepdims=True)
                    Mp = cb2 * jnp.exp(jnp.where(causal, Ej - e2, -1e30))
                    xj = x_dt[:, pc]
                    xbd = jnp.concatenate([xj * lo, xj * hi], axis=0).astype(BF16)
                    y_ref[rows, pc] += _dot(Mp.astype(BF16), xbd)
                ht_ref[:, gc] = jnp.exp(total[:, gc]) * htg + _dot_tn(Bg, xw[:, gc].astype(BF16))
            zb = zb_ref[rows, :].astype(F32)
            hh = y_ref[rows, :] * (zb * _sigmoid(zb))
            for g in range(SSD_GROUPS):
                gc = slice(g * GROUP_W, (g + 1) * GROUP_W)
                hg = hh[:, gc]
                r = lax.rsqrt(jnp.mean(hg * hg, axis=-1, keepdims=True) + EPS)
                yb_ref[rows, gc] = (hg * r * nw_ref[:, gc]).astype(BF16)

    full = lambda a: pl.BlockSpec(a.shape, lambda i: (0,) * a.ndim)
    hb = T // HALO_BLK
    return pl.pallas_call(
        body, name="ssd_fwd", grid=(nsteps,),
        in_specs=[pl.BlockSpec((T, D), lambda i: (i, OFF_ZB // D)),
                  pl.BlockSpec((T, XBC_W), lambda i: (i, OFF_XBC // XBC_W)),
                  pl.BlockSpec((HALO_BLK, XBC_W), lambda i: (jnp.maximum(i * hb - 1, 0), OFF_XBC // XBC_W)),
                  pl.BlockSpec((T, DT_W), lambda i: (i, OFF_DT // DT_W)),
                  full(conv_w), full(conv_b), full(dtb_p), full(alog_p), full(d_exp), full(norm_w), full(tri), full(expand),
                  full(shift)],
        out_specs=[pl.BlockSpec((T, D), lambda i: (i, 0)), pl.BlockSpec((T, D), lambda i: (i, 0)),
                   pl.BlockSpec((ncl, STATE, D), lambda i: (i, 0, 0))],
        out_shape=[jax.ShapeDtypeStruct((S, D), F32), jax.ShapeDtypeStruct((S, D), BF16),
                   jax.ShapeDtypeStruct((S // CHUNK, STATE, D), F32)],
        scratch_shapes=[pltpu.VMEM((STATE, D), F32), pltpu.VMEM((HALO_BLK + T, XBC_W), BF16)],
        compiler_params=_cp(("arbitrary",)),
    )(proj, proj, proj, proj, conv_w, conv_b, dtb_p, alog_p, d_exp, norm_w, tri, expand, shift)


def _ssd_bwd(proj, dyb, y, states, conv_w, conv_b, dtb_p, alog_p, d_exp, norm_w, tri, triT, expand, expandT, shift):
    S = proj.shape[0]
    T = SSD_T
    nsteps = S // T
    ncl = T // CHUNK
    SSD_W = SSD_PAD_W

    def body(zb_ref, xbc_ref, halo_ref, dt_ref, dyb_ref, y_ref, st_ref, cw_ref, cb_ref, dtb_ref, alog_ref, dexp_ref, nw_ref,
             tri_ref, triT_ref, exp_ref, expT_ref, shift_ref,
             dp_ref, dcw_ref, dcb_ref, ddtb_ref, dalog_ref, dD_ref, dnw_ref,
             dht_ref, ext_ref, dpre_ref, dy_s, dE_s, dxdt_s, dxc_s, dDacc_ref, dAacc_ref):
        i = pl.program_id(0)

        @pl.when(i == 0)
        def _():
            for r in (dht_ref, dcw_ref, dcb_ref, ddtb_ref, dnw_ref, dDacc_ref, dAacc_ref):
                r[...] = jnp.zeros_like(r)
            dpre_ref[T:T + HALO_BLK, :] = jnp.zeros((HALO_BLK, XBC_W), F32)

        @pl.when(i == nsteps - 1)
        def _():
            ext_ref[0:HALO_BLK, :] = jnp.zeros((HALO_BLK, XBC_W), BF16)

        @pl.when(i < nsteps - 1)
        def _():
            ext_ref[0:HALO_BLK, :] = halo_ref[...]

        ext_ref[HALO_BLK:HALO_BLK + T, :] = xbc_ref[...]
        diag, causal, lo, hi = _pair_masks()
        last_row = (lax.broadcasted_iota(jnp.int32, (CHUNK, 1), 0) == CHUNK - 1).astype(F32)
        for c in reversed(range(ncl)):
            q = _ssd_chunk_fwd(c, ext_ref, shift_ref, dt_ref, cw_ref, cb_ref, dtb_ref, alog_ref, tri_ref, exp_ref)
            rows = pl.ds(c * CHUNK, CHUNK)
            pre, sg, xc, dtr, dtv, A, E, dtE = (q[k] for k in ("pre", "sg", "xc", "dtr", "dtv", "A", "E", "dtE"))
            xs = xc[:, 0:D]
            total = E[CHUNK - 1:CHUNK, :]
            x_dt = xs * dtE
            eE = jnp.exp(E)
            wdec = jnp.exp(total - E)
            zb = zb_ref[rows, :].astype(F32)
            yv = y_ref[rows, :]
            sgz = _sigmoid(zb)
            sz = zb * sgz
            hh = yv * sz
            for g in range(SSD_GROUPS):
                gc = slice(g * GROUP_W, (g + 1) * GROUP_W)
                hg = hh[:, gc]
                r = lax.rsqrt(jnp.mean(hg * hg, axis=-1, keepdims=True) + EPS)
                dyb_g = dyb_ref[rows, gc].astype(F32)
                dn = dyb_g * nw_ref[:, gc]
                dnw_ref[0:1, gc] += jnp.sum(dyb_g * hg * r, axis=0, keepdims=True)
                dy_s[:, gc] = r * dn - hg * (r * r * r) * jnp.mean(dn * hg, axis=-1, keepdims=True)
            dhh = dy_s[...]
            dp_ref[rows, 0:D] = (dhh * yv * (sgz * (1.0 + zb * (1.0 - sgz)))).astype(BF16)
            dy = dhh * sz
            dy_s[...] = dy
            dDacc_ref[0:1, :] += jnp.sum(dy * xs, axis=0, keepdims=True)
            dxc_s[:, 0:D] = dy * dexp_ref[...]
            for g in range(SSD_GROUPS):
                gc = slice(g * GROUP_W, (g + 1) * GROUP_W)
                bcol = slice(D + g * STATE, D + (g + 1) * STATE)
                ccol = slice(D + SSD_GROUPS * STATE + g * STATE, D + SSD_GROUPS * STATE + (g + 1) * STATE)
                Bg = xc[:, bcol].astype(BF16)
                Cg = xc[:, ccol].astype(BF16)
                B2 = jnp.concatenate([Bg, Bg], axis=0)
                cb2 = _dot_nt(Cg, B2)
                htg = st_ref[c, :, gc]
                htb = htg.astype(BF16)
                dhn = dht_ref[:, gc]
                dhnb = dhn.astype(BF16)
                dyg = dy[:, gc]
                eEg = eE[:, gc]
                wg = wdec[:, gc]
                xdg = x_dt[:, gc]
                CH = _dot(Cg, htb)
                dCHb = (dyg * eEg).astype(BF16)
                dC = _dot_nt(dCHb, htb)
                dl = jnp.exp(total[:, gc])
                dht_prev = _dot_tn(Cg, dCHb) + dl * dhn
                dtot = jnp.sum(dhn * htg, axis=0, keepdims=True) * dl
                dxw = _dot(Bg, dhnb)
                dB = _dot_nt((xdg * wg).astype(BF16), dhnb)
                dwd = dxw * xdg * wg
                dtot = dtot + jnp.sum(dwd, axis=0, keepdims=True)
                dE_s[:, gc] = dyg * eEg * CH - dwd + last_row * dtot
                dxdt_s[:, gc] = dxw * wg
                dcb2 = jnp.zeros((CHUNK, LANE), F32)
                for jj in range(GROUP_W // LANE):
                    pc = slice(g * GROUP_W + jj * LANE, g * GROUP_W + (jj + 1) * LANE)
                    Ej = E[:, pc]
                    e2 = jnp.sum(Ej * diag, axis=0, keepdims=True)
                    Lp = jnp.exp(jnp.where(causal, Ej - e2, -1e30))
                    Mp = cb2 * Lp
                    xj = x_dt[:, pc]
                    xbd = jnp.concatenate([xj * lo, xj * hi], axis=0).astype(BF16)
                    dyj = dy[:, pc].astype(BF16)
                    dMp = _dot_nt(dyj, xbd)
                    dxbd = _dot_tn(Mp.astype(BF16), dyj)
                    dxdt_s[:, pc] += dxbd[0:CHUNK, :] * lo + dxbd[CHUNK:2 * CHUNK, :] * hi
                    dcb2 = dcb2 + dMp * Lp
                    dseg = dMp * Mp
                    dE_s[:, pc] += dseg - diag * jnp.sum(dseg, axis=0, keepdims=True)
                dcb2b = dcb2.astype(BF16)
                dC = dC + _dot(dcb2b, B2)
                dB2 = _dot_tn(dcb2b, Cg)
                dB = dB + dB2[0:CHUNK, :] + dB2[CHUNK:2 * CHUNK, :]
                dxc_s[:, bcol] = dB
                dxc_s[:, ccol] = dC
                dht_ref[:, gc] = dht_prev
            dx_dt = dxdt_s[...]
            dxc_s[:, 0:D] += dx_dt * dtE
            red = _sel_right(jnp.concatenate([dE_s[...], dx_dt * xs], axis=0), expT_ref[...])
            da = _sel_left(triT_ref[...], red[0:CHUNK, :])
            ddtv = red[CHUNK:2 * CHUNK, :] + da * A
            dAacc_ref[0:1, :] += jnp.sum(da * dtv, axis=0, keepdims=True)
            ddtr = ddtv * _sigmoid(dtr)
            ddtb_ref[0:1, :] += jnp.sum(ddtr, axis=0, keepdims=True)
            dp_ref[rows, D + XBC_W:D + XBC_W + DT_W] = ddtr.astype(BF16)
            dpre = dxc_s[...] * (sg * (1.0 + pre * (1.0 - sg)))
            dpre_ref[rows, :] = dpre
            dcb_ref[0:1, :] += jnp.sum(dpre, axis=0, keepdims=True)
            for k in range(CONV_K):
                dcw_ref[k:k + 1, :] += jnp.sum(dpre * q["taps"][k], axis=0, keepdims=True)
        dxbc = jnp.zeros((T, XBC_W), F32)
        for k in range(CONV_K):
            dxbc = dxbc + cw_ref[k:k + 1, :] * dpre_ref[pl.ds(CONV_K - 1 - k, T), :]
        dp_ref[:, D:D + XBC_W] = dxbc.astype(BF16)
        dp_ref[:, SEG_SSD[1]:SSD_W] = jnp.zeros((T, SSD_W - SEG_SSD[1]), BF16)
        dpre_ref[T:T + HALO, :] = dpre_ref[0:HALO, :]

        @pl.when(i == nsteps - 1)
        def _():
            dalog_ref[...] = dAacc_ref[...] * (-jnp.exp(alog_ref[...]))
            dD_ref[...] = _dot(dDacc_ref[...], expT_ref[...].astype(F32), precision=HI)

    full = lambda a: pl.BlockSpec(a.shape, lambda i: (0,) * a.ndim)
    hb = T // HALO_BLK
    rev = lambda i: nsteps - 1 - i
    acc = lambda w: pl.BlockSpec((8, w), lambda i: (0, 0))
    return pl.pallas_call(
        body, name="ssd_bwd", grid=(nsteps,),
        in_specs=[pl.BlockSpec((T, D), lambda i: (rev(i), OFF_ZB // D)),
                  pl.BlockSpec((T, XBC_W), lambda i: (rev(i), OFF_XBC // XBC_W)),
                  pl.BlockSpec((HALO_BLK, XBC_W), lambda i: (jnp.maximum(rev(i) * hb - 1, 0), OFF_XBC // XBC_W)),
                  pl.BlockSpec((T, DT_W), lambda i: (rev(i), OFF_DT // DT_W)),
                  pl.BlockSpec((T, D), lambda i: (rev(i), 0)), pl.BlockSpec((T, D), lambda i: (rev(i), 0)),
                  pl.BlockSpec((ncl, STATE, D), lambda i: (rev(i), 0, 0)),
                  full(conv_w), full(conv_b), full(dtb_p), full(alog_p), full(d_exp), full(norm_w),
                  full(tri), full(triT), full(expand), full(expandT), full(shift)],
        out_specs=[pl.BlockSpec((T, SSD_W), lambda i: (rev(i), 0)),
                   acc(XBC_W), acc(XBC_W), acc(DT_W), acc(DT_W), acc(DT_W), acc(D)],
        out_shape=[jax.ShapeDtypeStruct((S, SSD_W), BF16),
                   jax.ShapeDtypeStruct((8, XBC_W), F32), jax.ShapeDtypeStruct((8, XBC_W), F32),
                   jax.ShapeDtypeStruct((8, DT_W), F32), jax.ShapeDtypeStruct((8, DT_W), F32),
                   jax.ShapeDtypeStruct((8, DT_W), F32), jax.ShapeDtypeStruct((8, D), F32)],
        scratch_shapes=[pltpu.VMEM((STATE, D), F32), pltpu.VMEM((HALO_BLK + T, XBC_W), BF16), pltpu.VMEM((T + HALO_BLK, XBC_W), F32),
                        pltpu.VMEM((CHUNK, D), F32), pltpu.VMEM((CHUNK, D), F32), pltpu.VMEM((CHUNK, D), F32),
                        pltpu.VMEM((CHUNK, XBC_W), F32), pltpu.VMEM((8, D), F32), pltpu.VMEM((8, DT_W), F32)],
        compiler_params=_cp(("arbitrary",)),
    )(proj, proj, proj, proj, dyb, y, states, conv_w, conv_b, dtb_p, alog_p, d_exp, norm_w, tri, triT, expand, expandT, shift)


def _head(x, ya, yb, proj, target, gate_b, wout, fw, *, tm):
    S = x.shape[0]

    def body(x_ref, ya_ref, yb_ref, gl0_ref, gl1_ref, t_ref, gb_ref, w_ref, fw_ref,
             dh_ref, dhb_ref, mb_ref, dya_ref, dyb_ref, dgl_ref, loss_ref, dfw_ref, dgb_ref):
        @pl.when(pl.program_id(0) == 0)
        def _():
            loss_ref[...] = jnp.zeros_like(loss_ref)
            dfw_ref[...] = jnp.zeros_like(dfw_ref)
            dgb_ref[...] = jnp.zeros_like(dgb_ref)

        ya_v = ya_ref[...].astype(F32)
        yb_v = yb_ref[...].astype(F32)
        g0 = _sigmoid(gl0_ref[...].astype(F32) + gb_ref[:, 0:D])
        g1 = _sigmoid(gl1_ref[...].astype(F32) + gb_ref[:, D:2 * D])
        mb = (g0 * ya_v + g1 * yb_v).astype(BF16)
        mb_ref[...] = mb
        h = x_ref[...] + _dot(mb, w_ref[...])
        r = lax.rsqrt(jnp.mean(h * h, axis=-1, keepdims=True) + EPS)
        hn = h * r
        err = hn * fw_ref[...] - t_ref[...]
        loss_ref[...] += 0.5 * jnp.sum(jnp.mean(err * err, axis=-1, keepdims=True))
        dyf = err * (1.0 / D)
        dfw_ref[0:1, :] += jnp.sum(dyf * hn, axis=0, keepdims=True)
        dhn = dyf * fw_ref[...]
        dh = r * (dhn - hn * jnp.mean(dhn * hn, axis=-1, keepdims=True))
        dh_ref[...] = dh
        dhb = dh.astype(BF16)
        dhb_ref[...] = dhb
        dm = _dot_nt(dhb, w_ref[...])
        dya_ref[...] = (dm * g0).astype(BF16)
        dyb_ref[...] = (dm * g1).astype(BF16)
        dgl0 = dm * ya_v * g0 * (1.0 - g0)
        dgl1 = dm * yb_v * g1 * (1.0 - g1)
        dgl_ref[:, 0:D] = dgl0.astype(BF16)
        dgl_ref[:, D:2 * D] = dgl1.astype(BF16)
        dgb_ref[0:1, 0:D] += jnp.sum(dgl0, axis=0, keepdims=True)
        dgb_ref[0:1, D:2 * D] += jnp.sum(dgl1, axis=0, keepdims=True)

    row = pl.BlockSpec((tm, D), lambda i: (i, 0))
    seg = lambda off: pl.BlockSpec((tm, D), lambda i: (i, off // D))
    full = lambda a: pl.BlockSpec(a.shape, lambda i: (0,) * a.ndim)
    acc = lambda w: pl.BlockSpec((8, w), lambda i: (0, 0))
    return pl.pallas_call(
        body, name="head", grid=(S // tm,),
        in_specs=[row, row, row, seg(OFF_G0), seg(OFF_G1), row, full(gate_b), full(wout), full(fw)],
        out_specs=[row, row, row, row, row, pl.BlockSpec((tm, 2 * D), lambda i: (i, 0)), acc(LANE), acc(D), acc(2 * D)],
        out_shape=[jax.ShapeDtypeStruct((S, D), F32), jax.ShapeDtypeStruct((S, D), BF16), jax.ShapeDtypeStruct((S, D), BF16),
                   jax.ShapeDtypeStruct((S, D), BF16), jax.ShapeDtypeStruct((S, D), BF16), jax.ShapeDtypeStruct((S, 2 * D), BF16),
                   jax.ShapeDtypeStruct((8, LANE), F32), jax.ShapeDtypeStruct((8, D), F32), jax.ShapeDtypeStruct((8, 2 * D), F32)],
        compiler_params=_cp(("arbitrary",)),
    )(x, ya, yb, proj, proj, target, gate_b, wout, fw)


def _adam_update(g, w_ref, m_ref, v_ref, g_ref, d_ref, m2_ref, v2_ref):
    m2 = ADAM_B1 * m_ref[...] + (1.0 - ADAM_B1) * g
    v2 = ADAM_B2 * v_ref[...] + (1.0 - ADAM_B2) * (g * g)
    m_hat = m2 / (1.0 - ADAM_B1 ** ADAM_STEP)
    v_hat = v2 / (1.0 - ADAM_B2 ** ADAM_STEP)
    g_ref[...] = g
    d_ref[...] = -ADAM_LR * (m_hat / (jnp.sqrt(v_hat) + ADAM_EPS) + ADAM_WD * w_ref[...])
    m2_ref[...] = m2
    v2_ref[...] = v2


def _adamw_own(me, own, landed, w, m, v, *, tr, tc, name):
    _, R, C = landed.shape
    assert R % tr == 0 and C % tc == 0, (name, R, C, tr, tc)

    def body(me_ref, own_ref, p_ref, w_ref, m_ref, v_ref, g_ref, d_ref, m2_ref, v2_ref):
        mine = own_ref[0].astype(F32)
        g = jnp.where(me_ref[0] == 0, mine, p_ref[0].astype(F32))
        for k in range(1, N_DEV):
            g = g + jnp.where(me_ref[0] == k, mine, p_ref[k].astype(F32))
        _adam_update(g, w_ref, m_ref, v_ref, g_ref, d_ref, m2_ref, v2_ref)

    tile = pl.BlockSpec((tr, tc), lambda i, j, me_ref: (i, j))
    return pl.pallas_call(
        body, name=name,
        grid_spec=pltpu.PrefetchScalarGridSpec(
            num_scalar_prefetch=1, grid=(R // tr, C // tc),
            in_specs=[pl.BlockSpec((1, tr, tc), lambda i, j, me_ref: (me_ref[0], i, j)),
                      pl.BlockSpec((N_DEV, tr, tc), lambda i, j, me_ref: (0, i, j)), tile, tile, tile],
            out_specs=[tile, tile, tile, tile]),
        out_shape=[jax.ShapeDtypeStruct((R, C), F32)] * 4,
        compiler_params=_cp(("parallel", "parallel")),
    )(me, own, landed, w, m, v)


def _adamw(parts, w, m, v, *, tr, name):
    _, R, C = parts.shape
    assert R % tr == 0, (name, R, tr)

    def body(p_ref, w_ref, m_ref, v_ref, g_ref, d_ref, m2_ref, v2_ref):
        g = p_ref[0].astype(F32)
        for k in range(1, N_DEV):
            g = g + p_ref[k].astype(F32)
        _adam_update(g, w_ref, m_ref, v_ref, g_ref, d_ref, m2_ref, v2_ref)

    row = pl.BlockSpec((tr, C), lambda i: (i, 0))
    return pl.pallas_call(
        body, name=name, grid=(R // tr,),
        in_specs=[pl.BlockSpec((N_DEV, tr, C), lambda i: (0, i, 0)), row, row, row],
        out_specs=[row, row, row, row],
        out_shape=[jax.ShapeDtypeStruct((R, C), F32)] * 4,
        compiler_params=_cp(("parallel",)),
    )(parts, w, m, v)


def _place():
    x, y, c = lax.axis_index("x"), lax.axis_index("y"), lax.axis_index("c")
    return x, y, c


def _all_gather(arrs, *, name):
    n = len(arrs)

    def body(*refs):
        ins, outs = refs[:n], refs[n:2 * n]
        send_sems, recv_sems, local_sems = refs[2 * n:]
        x, y, c = _place()
        me, sibling = (x, y, c), (x, y, 1 - c)
        chips = [(1 - x, y), (x, 1 - y), (1 - x, 1 - y)]

        def idx(px, py, pc):
            return 4 * px + 2 * py + pc

        def copy(k, a, block, to, src=None):
            slab = outs[a].at[idx(*block)]
            return pltpu.make_async_remote_copy(
                src_ref=slab if src is None else src, dst_ref=slab,
                send_sem=send_sems.at[k, a], recv_sem=recv_sems.at[k, a], device_id=to, device_id_type=MESH)

        mine = [pltpu.make_async_copy(ins[a], outs[a].at[idx(*me)], local_sems.at[a]) for a in range(n)]
        for cp in mine:
            cp.start()
        first = []
        for a in range(n):
            first.append(copy(0, a, me, sibling, src=ins[a]))
            first += [copy(1 + j, a, me, (*chip, c), src=ins[a]) for j, chip in enumerate(chips)]
        for cp in first:
            cp.start()
        passed = []
        for j, chip in enumerate(chips):
            for a in range(n):
                copy(1 + j, a, (*chip, c), me).wait_recv()
                fwd = copy(4 + j, a, (*chip, c), sibling)
                fwd.start()
                passed.append(fwd)
        for a in range(n):
            copy(0, a, sibling, me).wait_recv()
            for j, chip in enumerate(chips):
                copy(4 + j, a, (*chip, 1 - c), me).wait_recv()
        for cp in first + passed:
            cp.wait_send()
        for cp in mine:
            cp.wait()

    anyspec = pl.BlockSpec(memory_space=pl.ANY)
    return pl.pallas_call(
        body, name=name,
        in_specs=[anyspec] * n, out_specs=[anyspec] * n,
        out_shape=[jax.ShapeDtypeStruct((N_DEV,) + a.shape, a.dtype) for a in arrs],
        scratch_shapes=[pltpu.SemaphoreType.DMA((7, n)), pltpu.SemaphoreType.DMA((7, n)), pltpu.SemaphoreType.DMA((n,))],
    )(*arrs)


W_ROWS = SEG_SSD[0] + SSD_PAD_W


GROUP = 16
INTERIOR = 1920


def _interior(k):
    lo = -(-(k * SHARD_IN) // GROUP) * GROUP
    hi = ((k + 1) * SHARD_IN) // GROUP * GROUP
    return lo, hi


def _dest_row(r):
    return r if r < 6144 else (r - 6144 + SEG_SSD[0] if r < 11296 else r - 11296 + SEG_GATE[0])


def _shard_pieces(k):
    lo_k, hi_k = _interior(k)
    out = []
    for lo, hi in ((0, 6144), (6144, 11296), (11296, W_IN)):
        a, b = max(lo, lo_k), min(hi, hi_k)
        if a < b:
            out.append((a - lo_k, b - a, _dest_row(a)))
    return out


GATHER_PARTS = 2


def _shard_parts(k):
    parts = [[] for _ in range(GATHER_PARTS)]
    for s0, n, d0 in _shard_pieces(k):
        step = -(-(n // GROUP) // GATHER_PARTS) * GROUP
        for p in range(GATHER_PARTS):
            a, b = min(p * step, n), min((p + 1) * step, n)
            if a < b:
                parts[p].append((s0 + a, b - a, d0 + a))
    return parts


def _patch_straddlers(wpT, heads, tails, rows):
    for k in range(1, N_DEV):
        m = (k * SHARD_IN) % GROUP
        dst = _dest_row(k * SHARD_IN - m)
        if m and rows[0] <= dst < rows[1]:
            group = jnp.concatenate([tails[k - 1, GROUP - m:], heads[k, :GROUP - m]], axis=0)
            wpT = lax.dynamic_update_slice(wpT, group, (dst, 0))
    return wpT


FIRST_ROWS = SEG_SGU[1]


def _gather_stages(k, rows, win_ref, small, z_ref, n_zero, w_ref, send_sems, recv_sems, local_sems):
    x, y, c = k // 4, (k // 2) % 2, k % 2
    idx = lambda p: 4 * p[0] + 2 * p[1] + p[2]
    me, sib = (x, y, c), (x, y, 1 - c)
    xn, yn, dg = (1 - x, y, c), (x, 1 - y, c), (1 - x, 1 - y, c)
    keep = lambda pieces: [p for p in pieces if rows[0] <= p[2] < rows[1]]
    parts = range(GATHER_PARTS)

    def copies(slot, block, to, part, own=False):
        kb = idx(block)
        out = []
        for j, (s0, n, d0) in enumerate(keep(_shard_parts(kb)[part])):
            dst = w_ref.at[pl.ds(d0, n)]
            out.append((win_ref.at[pl.ds(s0, n)] if own else dst, dst, 2 * part + j))
        if part == 0:
            for j, (src, gathered) in enumerate(small):
                out.append((src if own else gathered.at[kb], gathered.at[kb], 2 * GATHER_PARTS + j))
        return [pltpu.make_async_remote_copy(src_ref=s, dst_ref=d, send_sem=send_sems.at[slot, j], recv_sem=recv_sems.at[slot, j],
                                             device_id=to, device_id_type=MESH) for s, d, j in out]

    def start(cps):
        for cp in cps:
            cp.start()

    def arrived(slot, block, part):
        for cp in copies(slot, block, me, part):
            cp.wait_recv()

    def local():
        pairs = [(win_ref.at[pl.ds(s0, n)], w_ref.at[pl.ds(d0, n)]) for s0, n, d0 in keep(_shard_pieces(k))]
        pairs += [(src, gathered.at[k]) for src, gathered in small]
        if z_ref is not None:
            pairs.append((z_ref, w_ref.at[pl.ds(W_IN, n_zero)]))
        return [pltpu.make_async_copy(s, d, local_sems.at[j]) for j, (s, d) in enumerate(pairs)]

    relay = (xn, yn) if c == 1 else (yn, xn)

    def first():
        start(local())
        for p in parts:
            start(copies(0, me, sib, p, own=True) + copies(1, me, xn, p, own=True) + copies(2, me, yn, p, own=True))

    def hand_on():
        for p in parts:
            arrived(1, xn, p)
            start(copies(4, xn, sib, p))
            if c == 1:
                start(copies(3, *relay, p))
            arrived(2, yn, p)
            start(copies(5, yn, sib, p))
            if c == 0:
                start(copies(3, *relay, p))

    def finish():
        for p in parts:
            arrived(3, dg, p)
            start(copies(6, dg, sib, p))
        for p in parts:
            arrived(0, sib, p)
            arrived(4, (1 - x, y, 1 - c), p)
            arrived(5, (x, 1 - y, 1 - c), p)
            arrived(6, (1 - x, 1 - y, 1 - c), p)
        for p in parts:
            sent = (copies(0, me, sib, p, own=True) + copies(1, me, xn, p, own=True) + copies(2, me, yn, p, own=True)
                    + copies(3, *relay, p) + copies(4, xn, sib, p) + copies(5, yn, sib, p) + copies(6, dg, sib, p))
            for cp in sent:
                cp.wait_send()
        for cp in local():
            cp.wait()

    return first, hand_on, finish


def _gather_sems(n_small):
    n_arr = 2 * GATHER_PARTS + n_small
    return [pltpu.SemaphoreType.DMA((7, n_arr)), pltpu.SemaphoreType.DMA((7, n_arr)), pltpu.SemaphoreType.DMA((n_arr + 1,))]


def _gather_first(win, head, tail, wout, cw):
    small_in = (wout, cw, head, tail)

    def body(win_ref, wout_ref, cw_ref, head_ref, tail_ref, w_ref, gout_ref, gcw_ref, ghead_ref, gtail_ref, *sems):
        x, y, c = _place()
        me = 4 * x + 2 * y + c
        small = ((wout_ref, gout_ref), (cw_ref, gcw_ref), (head_ref, ghead_ref), (tail_ref, gtail_ref))

        def run(k):
            for stage in _gather_stages(k, (0, FIRST_ROWS), win_ref, small, None, 0, w_ref, *sems):
                stage()

        for k in range(N_DEV):
            pl.when(me == k)(functools.partial(run, k))

    anyspec = pl.BlockSpec(memory_space=pl.ANY)
    return pl.pallas_call(
        body, name="gather_first", in_specs=[anyspec] * 5, out_specs=[anyspec] * 5,
        out_shape=[jax.ShapeDtypeStruct((W_ROWS, D), win.dtype)]
        + [jax.ShapeDtypeStruct((N_DEV,) + a.shape, a.dtype) for a in small_in],
        scratch_shapes=_gather_sems(len(small_in)),
    )(win, wout, cw, head, tail)


def _in_proj_first(xn, w_buf, win, zeros, *, tm, tn):
    S = xn.shape[0]
    nj, ni = FIRST_ROWS // tn, S // tm
    n_zero = zeros.shape[0]
    assert FIRST_ROWS % tn == 0 and S % tm == 0 and W_IN + n_zero == W_ROWS
    stage_steps = (0, (nj * ni) // 2, nj * ni - 1)

    def body(xn_ref, w_in_ref, win_ref, z_ref, proj_ref, w_ref, btile_ref, bsem, *sems):
        del w_in_ref
        j, i = pl.program_id(0), pl.program_id(1)
        step = j * ni + i
        x, y, c = _place()
        me = 4 * x + 2 * y + c
        for k in range(N_DEV):
            stages = _gather_stages(k, (FIRST_ROWS, W_ROWS), win_ref, (), z_ref, n_zero, w_ref, *sems)
            for at, stage in zip(stage_steps, stages):
                pl.when(jnp.logical_and(me == k, step == at))(stage)

        @pl.when(i == 0)
        def _():
            cp = pltpu.make_async_copy(w_ref.at[pl.ds(j * tn, tn)], btile_ref, bsem)
            cp.start()
            cp.wait()

        proj_ref[...] = _dot_nt(xn_ref[...], btile_ref[...]).astype(BF16)

    anyspec = pl.BlockSpec(memory_space=pl.ANY)
    return pl.pallas_call(
        body, name="in_proj_first", grid=(nj, ni),
        in_specs=[pl.BlockSpec((tm, D), lambda j, i: (i, 0)), anyspec, anyspec, anyspec],
        out_specs=[pl.BlockSpec((tm, tn), lambda j, i: (i, j)), anyspec],
        out_shape=[jax.ShapeDtypeStruct((S, FIRST_ROWS), BF16), jax.ShapeDtypeStruct(w_buf.shape, w_buf.dtype)],
        input_output_aliases={1: 1},
        scratch_shapes=[pltpu.VMEM((tn, D), BF16), pltpu.SemaphoreType.DMA] + _gather_sems(0),
        compiler_params=_cp(("arbitrary", "arbitrary")),
    )(xn, w_buf, win, zeros)


_REL = [(dx, dy, dc) for dx in (0, 1) for dy in (0, 1) for dc in (0, 1)][1:]
_HBM = pl.BlockSpec(memory_space=pltpu.HBM)
_SEM = pl.BlockSpec(memory_space=pltpu.SEMAPHORE)
_EFFECT = pltpu.SideEffectType.DATAFLOW_SIDE_EFFECTING


def _peer(k):
    x, y, c = _place()
    dx, dy, dc = _REL[k]
    return (1 - x if dx else x, 1 - y if dy else y, 1 - c if dc else c)


def _exchange_start(parts, *, name):
    n = len(parts)

    def body(*refs):
        ins, lands = refs[:n], refs[n:2 * n]
        send_sems, recv_sems, token = refs[2 * n], refs[2 * n + 1], refs[-1]
        x, y, c = _place()
        me = 4 * x + 2 * y + c
        for a in range(n):
            for k in range(len(_REL)):
                px, py, pc = _peer(k)
                pltpu.make_async_remote_copy(
                    src_ref=ins[a].at[4 * px + 2 * py + pc], dst_ref=lands[a].at[me],
                    send_sem=send_sems.at[len(_REL) * a + k], recv_sem=recv_sems.at[len(_REL) * a + k],
                    device_id=(px, py, pc), device_id_type=MESH).start()
        token[...] = jnp.zeros_like(token)

    sem = pltpu.SemaphoreType.DMA((len(_REL) * n,))
    bufs = [pltpu.HBM(p.shape, p.dtype) for p in parts]
    outs = pl.pallas_call(
        body, name=name,
        out_shape=(sem, sem, *bufs, *bufs, jax.ShapeDtypeStruct((8, LANE), F32)),
        in_specs=(_HBM,) * (2 * n), out_specs=(_SEM, _SEM, *(_HBM,) * (2 * n), pl.BlockSpec(memory_space=pltpu.VMEM)),
        input_output_aliases={i: 2 + i for i in range(2 * n)},
        compiler_params=pltpu.CompilerParams(has_side_effects=_EFFECT),
    )(*[pltpu.with_memory_space_constraint(p, pltpu.HBM) for p in parts],
      *[pltpu.with_memory_space_constraint(lax.empty(p.shape, p.dtype), pltpu.HBM) for p in parts])
    return outs[0], outs[1], outs[2:2 + n], outs[2 + n:2 + 2 * n], outs[-1]


def _exchange_wait(send_sems, recv_sems, parts, lands, after, *, name):
    n = len(parts)

    def body(*refs):
        ins, lands_ = refs[:n], refs[n:2 * n]
        ssem, rsem = refs[2 * n], refs[2 * n + 1]
        for a in range(n):
            for k in range(len(_REL)):
                px, py, pc = _peer(k)
                p = 4 * px + 2 * py + pc
                cp = pltpu.make_async_remote_copy(
                    src_ref=ins[a].at[p], dst_ref=lands_[a].at[p],
                    send_sem=ssem.at[len(_REL) * a + k], recv_sem=rsem.at[len(_REL) * a + k],
                    device_id=(px, py, pc), device_id_type=MESH)
                cp.wait_send()
                cp.wait_recv()

    bufs = [pltpu.HBM(p.shape, p.dtype) for p in parts]
    outs = pl.pallas_call(
        body, name=name, out_shape=(*bufs, *bufs),
        in_specs=(*(_HBM,) * (2 * n), _SEM, _SEM, pl.BlockSpec(memory_space=pl.ANY)), out_specs=(_HBM,) * (2 * n),
        input_output_aliases={i: i for i in range(2 * n)},
        compiler_params=pltpu.CompilerParams(has_side_effects=_EFFECT),
    )(*parts, *lands, send_sems, recv_sems, after)
    return outs[:n], outs[n:]


WEIGHTS = ('norm_w', 'w_in', 'gate_b', 'sgu_norm_g', 'sgu_norm_b', 'sgu_w', 'sgu_b', 'conv_w', 'conv_b', 'dt_bias', 'A_log',
           'D_skip', 'ssd_norm_w', 'w_out', 'final_norm_w')
SHARDED = ('w_in', 'conv_w', 'w_out')
PACK_ROW = 8 * LANE


def _constants():
    tri = np.tril(np.ones((CHUNK, CHUNK), np.float32))
    expand = np.zeros((DT_W, D), np.float32)
    for h in range(HEADS):
        expand[h, h * HEADDIM:(h + 1) * HEADDIM] = 1.0
    sel = np.zeros((D, LANE), np.float32)
    for g in range(SGU_GROUPS):
        sel[g * LANE:(g + 1) * LANE, g] = 1.0
    pos_chunk = np.arange(SGU_BLOCK) // CHUNK
    mask = (pos_chunk[None, :] <= pos_chunk[:, None]).astype(np.float32)
    shift = np.zeros(((CONV_K - 1) * CHUNK, HALO_BLK + CHUNK), np.float32)
    for kk in range(CONV_K - 1):
        for t in range(CHUNK):
            shift[kk * CHUNK + t, HALO_BLK - (CONV_K - 1) + t + kk] = 1.0
    return dict(tri=jnp.asarray(tri, BF16), triT=jnp.asarray(tri.T.copy(), BF16), expand=jnp.asarray(np.tile(expand, (3, 1)), BF16),
                shift=jnp.asarray(shift, BF16),
                expandT=jnp.asarray(expand.T.copy(), BF16), sel=jnp.asarray(sel), mask=jnp.asarray(mask))


def _to_shards(segs):
    starts = np.cumsum([0] + [n for _, n in segs])
    assert starts[-1] == W_IN
    slabs = []
    for k in range(N_DEV):
        pieces = []
        for (s, n), s0 in zip(segs, starts[:-1]):
            lo, hi = max(k * SHARD_IN, s0), min((k + 1) * SHARD_IN, s0 + n)
            if lo < hi:
                pieces.append(s[lo - s0:hi - s0])
        slabs.append(jnp.concatenate(pieces, axis=0))
    return jnp.stack(slabs)


def _local_step(x2, tgt, project_first, wout, cw, p, exchange_small, exchange):
    S = x2.shape[0]
    k = _constants()
    xn = _norm_fwd(x2, p['norm_w'], tm=min(512, S))
    proj1, wpT = project_first(xn)
    proj2 = _matmul(xn, wpT, trans_b=True, b_joff=FIRST_ROWS // 2048, n=W_ROWS - FIRST_ROWS, out_dtype=BF16,
                    tm=min(1024, S), tn=2048, tk=D, name="in_proj")
    wm32 = p['sgu_w'][0] * k['mask']
    wm = wm32.astype(BF16)
    wmT = jnp.swapaxes(wm32, 1, 2).astype(BF16)
    bias_full = jnp.repeat(p['sgu_b'][0].T, LANE, axis=1)
    tm_sgu = min(256, S)
    ya = _sgu_fwd(proj1, p['sgu_norm_g'], p['sgu_norm_b'], wm, bias_full, tm=tm_sgu)
    pad32 = lambda a: jnp.pad(a, ((0, 0), (0, DT_W - HEADS)))
    dtb_p, alog_p = pad32(p['dt_bias']), pad32(p['A_log'])
    d_exp = jnp.repeat(p['D_skip'], HEADDIM, axis=1)
    ssd_args = (cw, p['conv_b'], dtb_p, alog_p, d_exp, p['ssd_norm_w'])
    y, yb, states = _ssd_fwd(proj2, *ssd_args, k['tri'], k['expand'], k['shift'])
    dh, dhb, mb, dya, dyb, dgl, loss, dfw, dgb = _head(
        x2, ya, yb, proj2, tgt, p['gate_b'], wout, p['final_norm_w'][None, :], tm=min(256, S))
    dsgu, dws, dbsT, dsg, dsb = _sgu_bwd(proj1, dya, p['sgu_norm_g'], p['sgu_norm_b'], wm, wmT, bias_full, k['mask'], k['sel'],
                                         tm=tm_sgu)
    dssd, dcw, dcb, ddtb, dalog, dD, dnw = _ssd_bwd(proj2, dyb, y, states, *ssd_args, k['tri'], k['triT'], k['expand'], k['expandT'],
                                                    k['shift'])
    grads = dict(
        gate_b=dgb[0:1], sgu_norm_g=dsg[0:1], sgu_norm_b=dsb[0:1], sgu_w=dws[None],
        sgu_b=dbsT[:, :SGU_GROUPS].T[None], conv_w=dcw[0:CONV_K][None], conv_b=dcb[0:1], dt_bias=ddtb[0:1, :HEADS],
        A_log=dalog[0:1, :HEADS], D_skip=dD[0:1, :HEADS], ssd_norm_w=dnw[0:1], final_norm_w=dfw[0])
    token = exchange_small(loss[0, 0], grads)
    tk = min(4096, S)
    tn = 1024
    dwT_sgu = _matmul(dsgu, xn, trans_a=True, out_dtype=BF16, tm=1024, tn=tn, tk=tk, after=token, name="dw_in_sgu")
    dwT_gate = _matmul(dgl, xn, trans_a=True, out_dtype=BF16, tm=1024, tn=tn, tk=tk, name="dw_in_gate")
    dwT_ssd = _matmul(dssd, xn, trans_a=True, out_dtype=BF16, tm=1024, tn=tn, tk=tk, name="dw_in_ssd")
    dw_out = _matmul(mb, dhb, trans_a=True, out_dtype=BF16, tm=1024, tn=tn, tk=tk, name="dw_out")
    token = exchange([(dwT_sgu, SEG_SGU[1]), (dwT_ssd, W_IN - SEG_SSD[0]), (dwT_gate, SEG_GATE[1])], dw_out)
    tm = min(1024, S)
    dxn = _matmul(dsgu, wpT, tm=tm, tn=tn, tk=3072, after=token, name="dxn_sgu")
    dxn = _matmul(dgl, wpT, b_koff=SEG_GATE[0] // 2048, tm=tm, tn=tn, tk=2048, add=dxn, name="dxn_gate")
    dxn = _matmul(dssd, wpT, b_koff=SEG_SSD[0] // 2048, tm=tm, tn=tn, tk=2048, add=dxn, name="dxn_ssd")
    grad_x, dnorm = _norm_bwd(x2, p['norm_w'], dxn, dh, tm=min(256, S))
    return grad_x, dnorm[0:1]


def _pack(arrs):
    rows, offs, r = [], [], 0
    for a in arrs:
        n = a.size
        nr = -(-n // PACK_ROW) * 8
        rows.append(jnp.pad(a.reshape(-1).astype(F32), (0, nr * LANE - n)).reshape(nr, LANE))
        offs.append(r)
        r += nr
    return jnp.concatenate(rows, axis=0), offs


def kernel(x, norm_w, w_in, gate_b, sgu_norm_g, sgu_norm_b, sgu_w, sgu_b, conv_w, conv_b, dt_bias, A_log, D_skip, ssd_norm_w, w_out, final_norm_w, loss_target, m_norm_w, m_w_in, m_gate_b, m_sgu_norm_g, m_sgu_norm_b, m_sgu_w, m_sgu_b, m_conv_w, m_conv_b, m_dt_bias, m_A_log, m_D_skip, m_ssd_norm_w, m_w_out, m_final_norm_w, v_norm_w, v_w_in, v_gate_b, v_sgu_norm_g, v_sgu_norm_b, v_sgu_w, v_sgu_b, v_conv_w, v_conv_b, v_dt_bias, v_A_log, v_D_skip, v_ssd_norm_w, v_w_out, v_final_norm_w):
    w = dict(norm_w=norm_w, w_in=w_in, gate_b=gate_b, sgu_norm_g=sgu_norm_g, sgu_norm_b=sgu_norm_b, sgu_w=sgu_w, sgu_b=sgu_b,
             conv_w=conv_w, conv_b=conv_b, dt_bias=dt_bias, A_log=A_log, D_skip=D_skip, ssd_norm_w=ssd_norm_w, w_out=w_out,
             final_norm_w=final_norm_w)
    m = dict(norm_w=m_norm_w, w_in=m_w_in, gate_b=m_gate_b, sgu_norm_g=m_sgu_norm_g, sgu_norm_b=m_sgu_norm_b, sgu_w=m_sgu_w,
             sgu_b=m_sgu_b, conv_w=m_conv_w, conv_b=m_conv_b, dt_bias=m_dt_bias, A_log=m_A_log, D_skip=m_D_skip,
             ssd_norm_w=m_ssd_norm_w, w_out=m_w_out, final_norm_w=m_final_norm_w)
    v = dict(norm_w=v_norm_w, w_in=v_w_in, gate_b=v_gate_b, sgu_norm_g=v_sgu_norm_g, sgu_norm_b=v_sgu_norm_b, sgu_w=v_sgu_w,
             sgu_b=v_sgu_b, conv_w=v_conv_w, conv_b=v_conv_b, dt_bias=v_dt_bias, A_log=v_A_log, D_skip=v_D_skip,
             ssd_norm_w=v_ssd_norm_w, w_out=v_w_out, final_norm_w=v_final_norm_w)
    me = 4 * lax.axis_index("x") + 2 * lax.axis_index("y") + lax.axis_index("c")
    shard_cw = XBC_W // N_DEV

    tpose = lambda a: jnp.swapaxes(a[0], 0, 1)
    wT = tpose(w_in).astype(BF16)
    first_group = (GROUP - (me * SHARD_IN) % GROUP) % GROUP
    window = lax.dynamic_slice(jnp.pad(wT, ((0, GROUP), (0, 0))), (first_group, 0), (INTERIOR, D))
    w_first, g_out, g_cw, heads, tails = _gather_first(window, wT[:GROUP], wT[SHARD_IN - GROUP:], w_out[0].astype(BF16), conv_w[0])
    w_first = _patch_straddlers(w_first, heads, tails, (0, FIRST_ROWS))
    wout_full = g_out.reshape(D, D)
    cw_full = jnp.swapaxes(g_cw, 0, 1).reshape(CONV_K, XBC_W)

    def project_first(xn):
        proj1, w_all = _in_proj_first(xn, w_first, window, jnp.zeros((W_ROWS - W_IN, D), BF16), tm=1024, tn=2048)
        return proj1, _patch_straddlers(w_all, heads, tails, (FIRST_ROWS, W_ROWS))

    flight = {}

    small = [n for n in WEIGHTS if n not in SHARDED and n != 'norm_w']
    early = {}

    def exchange_small(loss_part, grads):
        early['packed'], early['offs'] = _pack([grads[n] for n in small] + [loss_part, grads['conv_w']])
        parts = [jnp.broadcast_to(early['packed'][None], (N_DEV,) + early['packed'].shape)]
        early['sems'], early['rsems'], early['parts'], early['lands'], token = _exchange_start(parts, name="small_start")
        return token

    def exchange(dw_inT_segs, dw_out):
        parts = [_to_shards(dw_inT_segs), dw_out.reshape(N_DEV, D // N_DEV, D)]
        flight['sems'], flight['rsems'], flight['parts'], flight['lands'], token = _exchange_start(parts, name="exchange_start")
        return token

    grad_x, dnorm = _local_step(x[0], loss_target[0], project_first, wout_full, cw_full, w, exchange_small, exchange)
    _, (land_small,) = _exchange_wait(early['sems'], early['rsems'], early['parts'], early['lands'], grad_x, name="small_wait")
    (own_in, own_out), (land_in, land_out) = _exchange_wait(
        flight['sems'], flight['rsems'], flight['parts'], flight['lands'], grad_x, name="exchange_wait")
    me_arr = jnp.reshape(me, (1,)).astype(jnp.int32)
    res = {}
    res['w_in'] = [jnp.swapaxes(o, 0, 1) for o in _adamw_own(
        me_arr, own_in, land_in, tpose(w_in), tpose(m_w_in), tpose(v_w_in), tr=SHARD_IN, tc=256, name="adamw_w_in")]
    res['w_out'] = _adamw_own(me_arr, own_out, land_out, w_out[0], m_w_out[0], v_w_out[0], tr=128, tc=D, name="adamw_w_out")

    (norm_parts,) = _all_gather([_pack([dnorm])[0]], name="gather_norm")
    norm_outs = _adamw(norm_parts, *[_pack([d['norm_w']])[0] for d in (w, m, v)], tr=norm_parts.shape[1], name="adamw_norm")
    res['norm_w'] = [o.reshape(-1)[:D].reshape(w['norm_w'].shape) for o in norm_outs]

    offs = early['offs']
    gathered = lax.dynamic_update_slice(land_small, early['packed'][None], (me, 0, 0))
    off_loss, off_cw = offs[-2], offs[-1]
    cw_parts = gathered[:, off_cw:, :].reshape(N_DEV, CONV_K, XBC_W)
    cw_parts = lax.dynamic_slice_in_dim(cw_parts, me * shard_cw, shard_cw, axis=2)
    cw_rows = _pack([cw_parts[0]])[0].shape[0]
    cw_parts = jnp.pad(cw_parts.reshape(N_DEV, -1), ((0, 0), (0, cw_rows * LANE - CONV_K * shard_cw))).reshape(N_DEV, cw_rows, LANE)
    parts = jnp.concatenate([gathered[:, :off_cw, :], cw_parts], axis=1)
    zero = jnp.zeros((), F32)
    packs = [_pack([d[n] for n in small] + [zero, d['conv_w']])[0] for d in (w, m, v)]
    outs = _adamw(parts, *packs, tr=parts.shape[1], name="adamw_small")

    def unpack(o, name):
        if name == 'conv_w':
            return o[off_cw:off_cw + cw_rows].reshape(-1)[:CONV_K * shard_cw].reshape(w['conv_w'].shape)
        r0 = offs[small.index(name)]
        n = w[name].size
        return o[r0:r0 + -(-n // PACK_ROW) * 8].reshape(-1)[:n].reshape(w[name].shape)

    for n in small + ['conv_w']:
        res[n] = [unpack(o, n) for o in outs]
    for n in ('w_in', 'w_out'):
        res[n] = [o[None] for o in res[n]]
    loss = outs[0][off_loss, 0]
    return (loss, grad_x[None], *[res[n][0] for n in WEIGHTS], *[res[n][1] for n in WEIGHTS],
            *[res[n][2] for n in WEIGHTS], *[res[n][3] for n in WEIGHTS])
```

```python
import functools

import numpy as np
import jax
import jax.numpy as jnp
from jax import lax
from jax.experimental import pallas as pl
from jax.experimental.pallas import tpu as pltpu

F32 = jnp.float32
BF16 = jnp.bfloat16
HI = lax.Precision.HIGHEST
MESH = pl.DeviceIdType.MESH

D = 2048
EPS = 1e-5
SGU_BLOCK = 128
SGU_GROUPS = 16
CHUNK = 64
HEADS = 32
HEADDIM = 64
SSD_GROUPS = 4
GROUP_W = D // SSD_GROUPS
STATE = 128
CONV_K = 4
XBC_W = D + 2 * SSD_GROUPS * STATE
W_IN = 15392
N_DEV = 8
SHARD_IN = W_IN // N_DEV
ADAM_LR, ADAM_B1, ADAM_B2, ADAM_EPS, ADAM_WD, ADAM_STEP = 0.001, 0.9, 0.999, 1e-08, 0.01, 10

LANE = 128
DT_W = LANE
OFF_U, OFF_V, OFF_ZA, OFF_G0, OFF_G1, OFF_ZB, OFF_XBC, OFF_DT = 0, 2048, 4096, 6144, 8192, 10240, 12288, 15360
SEG_SGU = (0, 6144)
SEG_GATE = (6144, 4096)
SEG_SSD = (10240, 5248)
WP = SEG_SSD[0] + SEG_SSD[1]
SSD_PAD_W = 6144
VMEM_LIMIT = 56 * 1024 * 1024


def _cp(sem=None, vmem=VMEM_LIMIT):
    return pltpu.CompilerParams(dimension_semantics=sem, vmem_limit_bytes=vmem)


def _sigmoid(x):
    return 1.0 / (1.0 + jnp.exp(-x))


def _softplus(x):
    return jnp.maximum(x, 0.0) + jnp.log(1.0 + jnp.exp(-jnp.abs(x)))


def _dot(a, b, precision=None):
    return jnp.dot(a, b, preferred_element_type=F32, precision=precision)


def _dot_nt(a, b, precision=None):
    return lax.dot_general(a, b, (((1,), (1,)), ((), ())), preferred_element_type=F32, precision=precision)


def _dot_tn(a, b, precision=None):
    return lax.dot_general(a, b, (((0,), (0,)), ((), ())), preferred_element_type=F32, precision=precision)


def _split3(a):
    hi = a.astype(BF16)
    r = a - hi.astype(F32)
    mid = r.astype(BF16)
    return hi, mid, (r - mid.astype(F32)).astype(BF16)


def _sel_right(a, sel01):
    m = a.shape[0]
    r = _dot(jnp.concatenate(_split3(a), axis=0), sel01)
    return (r[0:m] + r[m:2 * m]) + r[2 * m:3 * m]


def _sel_right_k(a, sel01_x3):
    return _dot(jnp.concatenate(_split3(a), axis=1), sel01_x3)


def _sel_left(sel01, a):
    n = a.shape[1]
    r = _dot(sel01, jnp.concatenate(_split3(a), axis=1))
    return (r[:, 0:n] + r[:, n:2 * n]) + r[:, 2 * n:3 * n]


def _matmul(a, b, *, trans_a=False, trans_b=False, b_koff=0, out_dtype=F32, tm, tn, tk, add=None, after=None, name):
    K, M = a.shape if trans_a else a.shape[::-1]
    N = b.shape[0] if trans_b else b.shape[1]
    assert M % tm == 0 and N % tn == 0 and K % tk == 0 and not (trans_a and trans_b), (name, M, N, K, tm, tn, tk)
    nk = K // tk

    def body(*refs):
        a_ref, b_ref = refs[:2]
        add_ref = refs[2] if add is not None else None
        o_ref, acc_ref = refs[-2:]
        k = pl.program_id(2)
        if trans_a:
            part = _dot_tn(a_ref[...], b_ref[...])
        else:
            part = _dot_nt(a_ref[...], b_ref[...]) if trans_b else _dot(a_ref[...], b_ref[...])

        def result(r):
            if add_ref is not None:
                r = r + add_ref[...]
            return r.astype(out_dtype)

        if nk == 1:
            o_ref[...] = result(part)
        else:
            @pl.when(k == 0)
            def _():
                acc_ref[...] = part

            @pl.when(jnp.logical_and(k > 0, k < nk - 1))
            def _():
                acc_ref[...] += part

            @pl.when(k == nk - 1)
            def _():
                o_ref[...] = result(acc_ref[...] + part)

    in_specs = [pl.BlockSpec((tk, tm), lambda i, j, k: (k, i)) if trans_a else pl.BlockSpec((tm, tk), lambda i, j, k: (i, k)),
                pl.BlockSpec((tn, tk), lambda i, j, k: (j, k)) if trans_b else pl.BlockSpec((tk, tn), lambda i, j, k: (k + b_koff, j))]
    args = [a, b]
    if add is not None:
        in_specs.append(pl.BlockSpec((tm, tn), lambda i, j, k: (i, j)))
        args.append(add)
    if after is not None:
        in_specs.append(pl.BlockSpec(memory_space=pl.ANY))
        args.append(after)
    return pl.pallas_call(
        body, name=name, grid=(M // tm, N // tn, nk), in_specs=in_specs,
        out_specs=pl.BlockSpec((tm, tn), lambda i, j, k: (i, j)),
        out_shape=jax.ShapeDtypeStruct((M, N), out_dtype),
        scratch_shapes=[pltpu.VMEM((tm, tn), F32)],
        compiler_params=_cp(("parallel", "parallel", "arbitrary")),
    )(*args)


def _norm_fwd(x, w, *, tm):
    S = x.shape[0]

    def body(x_ref, w_ref, o_ref):
        xv = x_ref[...]
        r = lax.rsqrt(jnp.mean(xv * xv, axis=-1, keepdims=True) + EPS)
        o_ref[...] = (xv * r * w_ref[...]).astype(BF16)

    return pl.pallas_call(
        body, name="norm_fwd", grid=(S // tm,),
        in_specs=[pl.BlockSpec((tm, D), lambda i: (i, 0)), pl.BlockSpec((1, D), lambda i: (0, 0))],
        out_specs=pl.BlockSpec((tm, D), lambda i: (i, 0)),
        out_shape=jax.ShapeDtypeStruct((S, D), BF16), compiler_params=_cp(("parallel",)),
    )(x, w)


def _norm_bwd(x, w, dxn, dh, *, tm):
    S = x.shape[0]

    def body(x_ref, w_ref, dxn_ref, dh_ref, gx_ref, dw_ref):
        xv = x_ref[...]
        r = lax.rsqrt(jnp.mean(xv * xv, axis=-1, keepdims=True) + EPS)
        xh = xv * r
        dxn_v = dxn_ref[...]
        dxh = dxn_v * w_ref[...]
        gx_ref[...] = dh_ref[...] + r * (dxh - xh * jnp.mean(dxh * xh, axis=-1, keepdims=True))

        @pl.when(pl.program_id(0) == 0)
        def _():
            dw_ref[...] = jnp.zeros_like(dw_ref)

        dw_ref[0:1, :] += jnp.sum(dxn_v * xh, axis=0, keepdims=True)

    row = pl.BlockSpec((tm, D), lambda i: (i, 0))
    return pl.pallas_call(
        body, name="norm_bwd", grid=(S // tm,),
        in_specs=[row, pl.BlockSpec((1, D), lambda i: (0, 0)), row, row],
        out_specs=[row, pl.BlockSpec((8, D), lambda i: (0, 0))],
        out_shape=[jax.ShapeDtypeStruct((S, D), F32), jax.ShapeDtypeStruct((8, D), F32)],
        compiler_params=_cp(("arbitrary",)),
    )(x, w, dxn, dh)


def _sgu_core(u_ref, v_ref, z_ref, g_ref, b_ref, wm_ref, bias_ref, vnb_ref, mixed_ref, tm):
    v = v_ref[...].astype(F32)
    mu = jnp.mean(v, axis=-1, keepdims=True)
    vc = v - mu
    rs = lax.rsqrt(jnp.mean(vc * vc, axis=-1, keepdims=True) + EPS)
    vh = vc * rs
    vnb_ref[...] = (vh * g_ref[...] + b_ref[...]).astype(BF16)
    for blk in range(tm // SGU_BLOCK):
        rows = pl.ds(blk * SGU_BLOCK, SGU_BLOCK)
        for gi in range(SGU_GROUPS):
            cols = pl.ds(gi * LANE, LANE)
            mixed_ref[rows, cols] = _dot(wm_ref[gi], vnb_ref[rows, cols]) + bias_ref[:, cols]
    return vh, rs


def _sgu_fwd(proj, g, b, wm, bias_full, *, tm):
    S = proj.shape[0]

    def body(u_ref, v_ref, z_ref, g_ref, b_ref, wm_ref, bias_ref, y_ref, vnb_ref, mixed_ref):
        _sgu_core(u_ref, v_ref, z_ref, g_ref, b_ref, wm_ref, bias_ref, vnb_ref, mixed_ref, tm)
        z = z_ref[...].astype(F32)
        y_ref[...] = (u_ref[...].astype(F32) * mixed_ref[...] * (z * _sigmoid(z))).astype(BF16)

    seg = lambda off: pl.BlockSpec((tm, D), lambda i: (i, off // D))
    full = lambda a: pl.BlockSpec(a.shape, lambda i: (0,) * a.ndim)
    return pl.pallas_call(
        body, name="sgu_fwd", grid=(S // tm,),
        in_specs=[seg(OFF_U), seg(OFF_V), seg(OFF_ZA), full(g), full(b), full(wm), full(bias_full)],
        out_specs=pl.BlockSpec((tm, D), lambda i: (i, 0)),
        out_shape=jax.ShapeDtypeStruct((S, D), BF16),
        scratch_shapes=[pltpu.VMEM((tm, D), BF16), pltpu.VMEM((tm, D), F32)],
        compiler_params=_cp(("parallel",)),
    )(proj, proj, proj, g, b, wm, bias_full)


def _sgu_bwd(proj, dy, g, b, wm, wmT, bias_full, mask, sel, *, tm):
    S = proj.shape[0]
    nsteps = S // tm

    def body(u_ref, v_ref, z_ref, dy_ref, g_ref, b_ref, wm_ref, wmT_ref, bias_ref, mask_ref, sel_ref,
             dp_ref, dws_ref, dbs_ref, dg_ref, db_ref, vnb_ref, mixed_ref, dmb_ref, dvn_ref, dbias_ref):
        i = pl.program_id(0)

        @pl.when(i == 0)
        def _():
            dws_ref[...] = jnp.zeros_like(dws_ref)
            dg_ref[...] = jnp.zeros_like(dg_ref)
            db_ref[...] = jnp.zeros_like(db_ref)
            dbias_ref[...] = jnp.zeros_like(dbias_ref)

        vh, rs = _sgu_core(u_ref, v_ref, z_ref, g_ref, b_ref, wm_ref, bias_ref, vnb_ref, mixed_ref, tm)
        u = u_ref[...].astype(F32)
        z = z_ref[...].astype(F32)
        dy_v = dy_ref[...].astype(F32)
        mixed = mixed_ref[...]
        sg = _sigmoid(z)
        sz = z * sg
        dp_ref[:, 0:D] = (dy_v * mixed * sz).astype(BF16)
        dp_ref[:, 2 * D:3 * D] = (dy_v * u * mixed * (sg * (1.0 + z * (1.0 - sg)))).astype(BF16)
        dmixed = dy_v * u * sz
        dmb_ref[...] = dmixed.astype(BF16)
        for blk in range(tm // SGU_BLOCK):
            dbias_ref[...] += dmixed[blk * SGU_BLOCK:(blk + 1) * SGU_BLOCK, :]
        for blk in range(tm // SGU_BLOCK):
            rows = pl.ds(blk * SGU_BLOCK, SGU_BLOCK)
            for gi in range(SGU_GROUPS):
                cols = pl.ds(gi * LANE, LANE)
                dm = dmb_ref[rows, cols]
                dvn_ref[rows, cols] = _dot(wmT_ref[gi], dm)
                dws_ref[gi] += _dot_nt(dm, vnb_ref[rows, cols])
        dvn = dvn_ref[...]
        dg_ref[0:1, :] += jnp.sum(dvn * vh, axis=0, keepdims=True)
        db_ref[0:1, :] += jnp.sum(dvn, axis=0, keepdims=True)
        dvh = dvn * g_ref[...]
        dv = rs * (dvh - jnp.mean(dvh, axis=-1, keepdims=True) - vh * jnp.mean(dvh * vh, axis=-1, keepdims=True))
        dp_ref[:, D:2 * D] = dv.astype(BF16)

        @pl.when(i == nsteps - 1)
        def _():
            for gi in range(SGU_GROUPS):
                dws_ref[gi] = dws_ref[gi] * mask_ref[...]
            dbs_ref[...] = _dot(dbias_ref[...], sel_ref[...], precision=HI)

    seg = lambda off: pl.BlockSpec((tm, D), lambda i: (i, off // D))
    full = lambda a: pl.BlockSpec(a.shape, lambda i: (0,) * a.ndim)
    return pl.pallas_call(
        body, name="sgu_bwd", grid=(nsteps,),
        in_specs=[seg(OFF_U), seg(OFF_V), seg(OFF_ZA), pl.BlockSpec((tm, D), lambda i: (i, 0)),
                  full(g), full(b), full(wm), full(wmT), full(bias_full), full(mask), full(sel)],
        out_specs=[pl.BlockSpec((tm, 3 * D), lambda i: (i, 0)),
                   pl.BlockSpec((SGU_GROUPS, SGU_BLOCK, SGU_BLOCK), lambda i: (0, 0, 0)),
                   pl.BlockSpec((SGU_BLOCK, LANE), lambda i: (0, 0)),
                   pl.BlockSpec((8, D), lambda i: (0, 0)), pl.BlockSpec((8, D), lambda i: (0, 0))],
        out_shape=[jax.ShapeDtypeStruct((S, 3 * D), BF16),
                   jax.ShapeDtypeStruct((SGU_GROUPS, SGU_BLOCK, SGU_BLOCK), F32),
                   jax.ShapeDtypeStruct((SGU_BLOCK, LANE), F32),
                   jax.ShapeDtypeStruct((8, D), F32), jax.ShapeDtypeStruct((8, D), F32)],
        scratch_shapes=[pltpu.VMEM((tm, D), BF16), pltpu.VMEM((tm, D), F32), pltpu.VMEM((tm, D), BF16),
                        pltpu.VMEM((tm, D), F32), pltpu.VMEM((SGU_BLOCK, D), F32)],
        compiler_params=_cp(("arbitrary",)),
    )(proj, proj, proj, dy, g, b, wm, wmT, bias_full, mask, sel)


SSD_T = 2 * CHUNK
HALO = 8
HALO_BLK = 16


def _pair_masks():
    row = lax.broadcasted_iota(jnp.int32, (CHUNK, LANE), 0)
    lane = lax.broadcasted_iota(jnp.int32, (CHUNK, LANE), 1)
    pos = jnp.where(lane >= CHUNK, lane - CHUNK, lane)
    diag = (row == pos).astype(F32)
    causal = row >= pos
    lo = (lane < CHUNK).astype(F32)
    return diag, causal, lo, 1.0 - lo


def _ssd_chunk_fwd(c, ext_ref, shift_ref, dt_ref, cw_ref, cb_ref, dtb_ref, alog_ref, tri_ref, exp_ref):
    r0 = c * CHUNK
    win = ext_ref[pl.ds(r0, HALO_BLK + CHUNK), :]
    sh = _dot(shift_ref[...], win)
    taps = [sh[k * CHUNK:(k + 1) * CHUNK] for k in range(CONV_K - 1)] + [win[HALO_BLK:].astype(F32)]
    pre = cb_ref[...] + sum(cw_ref[k:k + 1, :] * taps[k] for k in range(CONV_K))
    sg = _sigmoid(pre)
    xc = pre * sg
    dtr = dt_ref[pl.ds(r0, CHUNK), :].astype(F32) + dtb_ref[...]
    dtv = _softplus(dtr)
    A = -jnp.exp(alog_ref[...])
    acs = _sel_left(tri_ref[...], dtv * A)
    both = _sel_right_k(jnp.concatenate([acs, dtv], axis=0), exp_ref[...])
    E, dtE = both[0:CHUNK], both[CHUNK:2 * CHUNK]
    return dict(taps=taps, pre=pre, sg=sg, xc=xc, dtr=dtr, dtv=dtv, A=A, E=E, dtE=dtE)


def _ssd_fwd(proj, conv_w, conv_b, dtb_p, alog_p, d_exp, norm_w, tri, expand, shift):
    S = proj.shape[0]
    T = SSD_T
    nsteps = S // T
    ncl = T // CHUNK

    def body(zb_ref, xbc_ref, halo_ref, dt_ref, cw_ref, cb_ref, dtb_ref, alog_ref, dexp_ref, nw_ref, tri_ref, exp_ref, shift_ref,
             y_ref, yb_ref, st_ref, ht_ref, ext_ref):
        i = pl.program_id(0)

        @pl.when(i == 0)
        def _():
            ht_ref[...] = jnp.zeros_like(ht_ref)
            ext_ref[0:HALO_BLK, :] = jnp.zeros((HALO_BLK, XBC_W), BF16)

        @pl.when(i > 0)
        def _():
            ext_ref[0:HALO_BLK, :] = halo_ref[...]

        ext_ref[HALO_BLK:HALO_BLK + T, :] = xbc_ref[...]
        diag, causal, lo, hi = _pair_masks()
        for c in range(ncl):
            q = _ssd_chunk_fwd(c, ext_ref, shift_ref, dt_ref, cw_ref, cb_ref, dtb_ref, alog_ref, tri_ref, exp_ref)
            rows = pl.ds(c * CHUNK, CHUNK)
            xc, E, dtE = q["xc"], q["E"], q["dtE"]
            xs = xc[:, 0:D]
            total = E[CHUNK - 1:CHUNK, :]
            x_dt = xs * dtE
            eE = jnp.exp(E)
            xw = x_dt * jnp.exp(total - E)
            st_ref[c] = ht_ref[...]
            for g in range(SSD_GROUPS):
                gc = slice(g * GROUP_W, (g + 1) * GROUP_W)
                Bg = xc[:, D + g * STATE:D + (g + 1) * STATE].astype(BF16)
                Cg = xc[:, D + SSD_GROUPS * STATE + g * STATE:D + SSD_GROUPS * STATE + (g + 1) * STATE].astype(BF16)
                cb2 = _dot_nt(Cg, jnp.concatenate([Bg, Bg], axis=0))
                htg = ht_ref[:, gc]
                y_ref[rows, gc] = eE[:, gc] * _dot(Cg, htg.astype(BF16)) + xs[:, gc] * dexp_ref[:, gc]
                for jj in range(GROUP_W // LANE):
                    pc = slice(g * GROUP_W + jj * LANE, g * GROUP_W + (jj + 1) * LANE)
                    Ej = E[:, pc]
                    e2 = jnp.sum(Ej * diag, axis=0, keepdims=True)
                    Mp = cb2 * jnp.exp(jnp.where(causal, Ej - e2, -1e30))
                    xj = x_dt[:, pc]
                    xbd = jnp.concatenate([xj * lo, xj * hi], axis=0).astype(BF16)
                    y_ref[rows, pc] += _dot(Mp.astype(BF16), xbd)
                ht_ref[:, gc] = jnp.exp(total[:, gc]) * htg + _dot_tn(Bg, xw[:, gc].astype(BF16))
            zb = zb_ref[rows, :].astype(F32)
            hh = y_ref[rows, :] * (zb * _sigmoid(zb))
            for g in range(SSD_GROUPS):
                gc = slice(g * GROUP_W, (g + 1) * GROUP_W)
                hg = hh[:, gc]
                r = lax.rsqrt(jnp.mean(hg * hg, axis=-1, keepdims=True) + EPS)
                yb_ref[rows, gc] = (hg * r * nw_ref[:, gc]).astype(BF16)

    full = lambda a: pl.BlockSpec(a.shape, lambda i: (0,) * a.ndim)
    hb = T // HALO_BLK
    return pl.pallas_call(
        body, name="ssd_fwd", grid=(nsteps,),
        in_specs=[pl.BlockSpec((T, D), lambda i: (i, OFF_ZB // D)),
                  pl.BlockSpec((T, XBC_W), lambda i: (i, OFF_XBC // XBC_W)),
                  pl.BlockSpec((HALO_BLK, XBC_W), lambda i: (jnp.maximum(i * hb - 1, 0), OFF_XBC // XBC_W)),
                  pl.BlockSpec((T, DT_W), lambda i: (i, OFF_DT // DT_W)),
                  full(conv_w), full(conv_b), full(dtb_p), full(alog_p), full(d_exp), full(norm_w), full(tri), full(expand),
                  full(shift)],
        out_specs=[pl.BlockSpec((T, D), lambda i: (i, 0)), pl.BlockSpec((T, D), lambda i: (i, 0)),
                   pl.BlockSpec((ncl, STATE, D), lambda i: (i, 0, 0))],
        out_shape=[jax.ShapeDtypeStruct((S, D), F32), jax.ShapeDtypeStruct((S, D), BF16),
                   jax.ShapeDtypeStruct((S // CHUNK, STATE, D), F32)],
        scratch_shapes=[pltpu.VMEM((STATE, D), F32), pltpu.VMEM((HALO_BLK + T, XBC_W), BF16)],
        compiler_params=_cp(("arbitrary",)),
    )(proj, proj, proj, proj, conv_w, conv_b, dtb_p, alog_p, d_exp, norm_w, tri, expand, shift)


def _ssd_bwd(proj, dyb, y, states, conv_w, conv_b, dtb_p, alog_p, d_exp, norm_w, tri, triT, expand, expandT, shift):
    S = proj.shape[0]
    T = SSD_T
    nsteps = S // T
    ncl = T // CHUNK
    SSD_W = SSD_PAD_W

    def body(zb_ref, xbc_ref, halo_ref, dt_ref, dyb_ref, y_ref, st_ref, cw_ref, cb_ref, dtb_ref, alog_ref, dexp_ref, nw_ref,
             tri_ref, triT_ref, exp_ref, expT_ref, shift_ref,
             dp_ref, dcw_ref, dcb_ref, ddtb_ref, dalog_ref, dD_ref, dnw_ref,
             dht_ref, ext_ref, dpre_ref, dy_s, dE_s, dxdt_s, dxc_s, dDacc_ref, dAacc_ref):
        i = pl.program_id(0)

        @pl.when(i == 0)
        def _():
            for r in (dht_ref, dcw_ref, dcb_ref, ddtb_ref, dnw_ref, dDacc_ref, dAacc_ref):
                r[...] = jnp.zeros_like(r)
            dpre_ref[T:T + HALO_BLK, :] = jnp.zeros((HALO_BLK, XBC_W), F32)

        @pl.when(i == nsteps - 1)
        def _():
            ext_ref[0:HALO_BLK, :] = jnp.zeros((HALO_BLK, XBC_W), BF16)

        @pl.when(i < nsteps - 1)
        def _():
            ext_ref[0:HALO_BLK, :] = halo_ref[...]

        ext_ref[HALO_BLK:HALO_BLK + T, :] = xbc_ref[...]
        diag, causal, lo, hi = _pair_masks()
        last_row = (lax.broadcasted_iota(jnp.int32, (CHUNK, 1), 0) == CHUNK - 1).astype(F32)
        for c in reversed(range(ncl)):
            q = _ssd_chunk_fwd(c, ext_ref, shift_ref, dt_ref, cw_ref, cb_ref, dtb_ref, alog_ref, tri_ref, exp_ref)
            rows = pl.ds(c * CHUNK, CHUNK)
            pre, sg, xc, dtr, dtv, A, E, dtE = (q[k] for k in ("pre", "sg", "xc", "dtr", "dtv", "A", "E", "dtE"))
            xs = xc[:, 0:D]
            total = E[CHUNK - 1:CHUNK, :]
            x_dt = xs * dtE
            eE = jnp.exp(E)
            wdec = jnp.exp(total - E)
            zb = zb_ref[rows, :].astype(F32)
            yv = y_ref[rows, :]
            sgz = _sigmoid(zb)
            sz = zb * sgz
            hh = yv * sz
            for g in range(SSD_GROUPS):
                gc = slice(g * GROUP_W, (g + 1) * GROUP_W)
                hg = hh[:, gc]
                r = lax.rsqrt(jnp.mean(hg * hg, axis=-1, keepdims=True) + EPS)
                dyb_g = dyb_ref[rows, gc].astype(F32)
                dn = dyb_g * nw_ref[:, gc]
                dnw_ref[0:1, gc] += jnp.sum(dyb_g * hg * r, axis=0, keepdims=True)
                dy_s[:, gc] = r * dn - hg * (r * r * r) * jnp.mean(dn * hg, axis=-1, keepdims=True)
            dhh = dy_s[...]
            dp_ref[rows, 0:D] = (dhh * yv * (sgz * (1.0 + zb * (1.0 - sgz)))).astype(BF16)
            dy = dhh * sz
            dy_s[...] = dy
            dDacc_ref[0:1, :] += jnp.sum(dy * xs, axis=0, keepdims=True)
            dxc_s[:, 0:D] = dy * dexp_ref[...]
            for g in range(SSD_GROUPS):
                gc = slice(g * GROUP_W, (g + 1) * GROUP_W)
                bcol = slice(D + g * STATE, D + (g + 1) * STATE)
                ccol = slice(D + SSD_GROUPS * STATE + g * STATE, D + SSD_GROUPS * STATE + (g + 1) * STATE)
                Bg = xc[:, bcol].astype(BF16)
                Cg = xc[:, ccol].astype(BF16)
                B2 = jnp.concatenate([Bg, Bg], axis=0)
                cb2 = _dot_nt(Cg, B2)
                htg = st_ref[c, :, gc]
                htb = htg.astype(BF16)
                dhn = dht_ref[:, gc]
                dhnb = dhn.astype(BF16)
                dyg = dy[:, gc]
                eEg = eE[:, gc]
                wg = wdec[:, gc]
                xdg = x_dt[:, gc]
                CH = _dot(Cg, htb)
                dCHb = (dyg * eEg).astype(BF16)
                dC = _dot_nt(dCHb, htb)
                dl = jnp.exp(total[:, gc])
                dht_prev = _dot_tn(Cg, dCHb) + dl * dhn
                dtot = jnp.sum(dhn * htg, axis=0, keepdims=True) * dl
                dxw = _dot(Bg, dhnb)
                dB = _dot_nt((xdg * wg).astype(BF16), dhnb)
                dwd = dxw * xdg * wg
                dtot = dtot + jnp.sum(dwd, axis=0, keepdims=True)
                dE_s[:, gc] = dyg * eEg * CH - dwd + last_row * dtot
                dxdt_s[:, gc] = dxw * wg
                dcb2 = jnp.zeros((CHUNK, LANE), F32)
                for jj in range(GROUP_W // LANE):
                    pc = slice(g * GROUP_W + jj * LANE, g * GROUP_W + (jj + 1) * LANE)
                    Ej = E[:, pc]
                    e2 = jnp.sum(Ej * diag, axis=0, keepdims=True)
                    Lp = jnp.exp(jnp.where(causal, Ej - e2, -1e30))
                    Mp = cb2 * Lp
                    xj = x_dt[:, pc]
                    xbd = jnp.concatenate([xj * lo, xj * hi], axis=0).astype(BF16)
                    dyj = dy[:, pc].astype(BF16)
                    dMp = _dot_nt(dyj, xbd)
                    dxbd = _dot_tn(Mp.astype(BF16), dyj)
                    dxdt_s[:, pc] += dxbd[0:CHUNK, :] * lo + dxbd[CHUNK:2 * CHUNK, :] * hi
                    dcb2 = dcb2 + dMp * Lp
                    dseg = dMp * Mp
                    dE_s[:, pc] += dseg - diag * jnp.sum(dseg, axis=0, keepdims=True)
                dcb2b = dcb2.astype(BF16)
                dC = dC + _dot(dcb2b, B2)
                dB2 = _dot_tn(dcb2b, Cg)
                dB = dB + dB2[0:CHUNK, :] + dB2[CHUNK:2 * CHUNK, :]
                dxc_s[:, bcol] = dB
                dxc_s[:, ccol] = dC
                dht_ref[:, gc] = dht_prev
            dx_dt = dxdt_s[...]
            dxc_s[:, 0:D] += dx_dt * dtE
            red = _sel_right(jnp.concatenate([dE_s[...], dx_dt * xs], axis=0), expT_ref[...])
            da = _sel_left(triT_ref[...], red[0:CHUNK, :])
            ddtv = red[CHUNK:2 * CHUNK, :] + da * A
            dAacc_ref[0:1, :] += jnp.sum(da * dtv, axis=0, keepdims=True)
            ddtr = ddtv * _sigmoid(dtr)
            ddtb_ref[0:1, :] += jnp.sum(ddtr, axis=0, keepdims=True)
            dp_ref[rows, D + XBC_W:D + XBC_W + DT_W] = ddtr.astype(BF16)
            dpre = dxc_s[...] * (sg * (1.0 + pre * (1.0 - sg)))
            dpre_ref[rows, :] = dpre
            dcb_ref[0:1, :] += jnp.sum(dpre, axis=0, keepdims=True)
            for k in range(CONV_K):
                dcw_ref[k:k + 1, :] += jnp.sum(dpre * q["taps"][k], axis=0, keepdims=True)
        dxbc = jnp.zeros((T, XBC_W), F32)
        for k in range(CONV_K):
            dxbc = dxbc + cw_ref[k:k + 1, :] * dpre_ref[pl.ds(CONV_K - 1 - k, T), :]
        dp_ref[:, D:D + XBC_W] = dxbc.astype(BF16)
        dp_ref[:, SEG_SSD[1]:SSD_W] = jnp.zeros((T, SSD_W - SEG_SSD[1]), BF16)
        dpre_ref[T:T + HALO, :] = dpre_ref[0:HALO, :]

        @pl.when(i == nsteps - 1)
        def _():
            dalog_ref[...] = dAacc_ref[...] * (-jnp.exp(alog_ref[...]))
            dD_ref[...] = _dot(dDacc_ref[...], expT_ref[...].astype(F32), precision=HI)

    full = lambda a: pl.BlockSpec(a.shape, lambda i: (0,) * a.ndim)
    hb = T // HALO_BLK
    rev = lambda i: nsteps - 1 - i
    acc = lambda w: pl.BlockSpec((8, w), lambda i: (0, 0))
    return pl.pallas_call(
        body, name="ssd_bwd", grid=(nsteps,),
        in_specs=[pl.BlockSpec((T, D), lambda i: (rev(i), OFF_ZB // D)),
                  pl.BlockSpec((T, XBC_W), lambda i: (rev(i), OFF_XBC // XBC_W)),
                  pl.BlockSpec((HALO_BLK, XBC_W), lambda i: (jnp.maximum(rev(i) * hb - 1, 0), OFF_XBC // XBC_W)),
                  pl.BlockSpec((T, DT_W), lambda i: (rev(i), OFF_DT // DT_W)),
                  pl.BlockSpec((T, D), lambda i: (rev(i), 0)), pl.BlockSpec((T, D), lambda i: (rev(i), 0)),
                  pl.BlockSpec((ncl, STATE, D), lambda i: (rev(i), 0, 0)),
                  full(conv_w), full(conv_b), full(dtb_p), full(alog_p), full(d_exp), full(norm_w),
                  full(tri), full(triT), full(expand), full(expandT), full(shift)],
        out_specs=[pl.BlockSpec((T, SSD_W), lambda i: (rev(i), 0)),
                   acc(XBC_W), acc(XBC_W), acc(DT_W), acc(DT_W), acc(DT_W), acc(D)],
        out_shape=[jax.ShapeDtypeStruct((S, SSD_W), BF16),
                   jax.ShapeDtypeStruct((8, XBC_W), F32), jax.ShapeDtypeStruct((8, XBC_W), F32),
                   jax.ShapeDtypeStruct((8, DT_W), F32), jax.ShapeDtypeStruct((8, DT_W), F32),
                   jax.ShapeDtypeStruct((8, DT_W), F32), jax.ShapeDtypeStruct((8, D), F32)],
        scratch_shapes=[pltpu.VMEM((STATE, D), F32), pltpu.VMEM((HALO_BLK + T, XBC_W), BF16), pltpu.VMEM((T + HALO_BLK, XBC_W), F32),
                        pltpu.VMEM((CHUNK, D), F32), pltpu.VMEM((CHUNK, D), F32), pltpu.VMEM((CHUNK, D), F32),
                        pltpu.VMEM((CHUNK, XBC_W), F32), pltpu.VMEM((8, D), F32), pltpu.VMEM((8, DT_W), F32)],
        compiler_params=_cp(("arbitrary",)),
    )(proj, proj, proj, proj, dyb, y, states, conv_w, conv_b, dtb_p, alog_p, d_exp, norm_w, tri, triT, expand, expandT, shift)


def _head(x, ya, yb, proj, target, gate_b, wout, fw, *, tm):
    S = x.shape[0]

    def body(x_ref, ya_ref, yb_ref, gl0_ref, gl1_ref, t_ref, gb_ref, w_ref, fw_ref,
             dh_ref, dhb_ref, mb_ref, dya_ref, dyb_ref, dgl_ref, loss_ref, dfw_ref, dgb_ref):
        @pl.when(pl.program_id(0) == 0)
        def _():
            loss_ref[...] = jnp.zeros_like(loss_ref)
            dfw_ref[...] = jnp.zeros_like(dfw_ref)
            dgb_ref[...] = jnp.zeros_like(dgb_ref)

        ya_v = ya_ref[...].astype(F32)
        yb_v = yb_ref[...].astype(F32)
        g0 = _sigmoid(gl0_ref[...].astype(F32) + gb_ref[:, 0:D])
        g1 = _sigmoid(gl1_ref[...].astype(F32) + gb_ref[:, D:2 * D])
        mb = (g0 * ya_v + g1 * yb_v).astype(BF16)
        mb_ref[...] = mb
        h = x_ref[...] + _dot(mb, w_ref[...])
        r = lax.rsqrt(jnp.mean(h * h, axis=-1, keepdims=True) + EPS)
        hn = h * r
        err = hn * fw_ref[...] - t_ref[...]
        loss_ref[...] += 0.5 * jnp.sum(jnp.mean(err * err, axis=-1, keepdims=True))
        dyf = err * (1.0 / D)
        dfw_ref[0:1, :] += jnp.sum(dyf * hn, axis=0, keepdims=True)
        dhn = dyf * fw_ref[...]
        dh = r * (dhn - hn * jnp.mean(dhn * hn, axis=-1, keepdims=True))
        dh_ref[...] = dh
        dhb = dh.astype(BF16)
        dhb_ref[...] = dhb
        dm = _dot_nt(dhb, w_ref[...])
        dya_ref[...] = (dm * g0).astype(BF16)
        dyb_ref[...] = (dm * g1).astype(BF16)
        dgl0 = dm * ya_v * g0 * (1.0 - g0)
        dgl1 = dm * yb_v * g1 * (1.0 - g1)
        dgl_ref[:, 0:D] = dgl0.astype(BF16)
        dgl_ref[:, D:2 * D] = dgl1.astype(BF16)
        dgb_ref[0:1, 0:D] += jnp.sum(dgl0, axis=0, keepdims=True)
        dgb_ref[0:1, D:2 * D] += jnp.sum(dgl1, axis=0, keepdims=True)

    row = pl.BlockSpec((tm, D), lambda i: (i, 0))
    seg = lambda off: pl.BlockSpec((tm, D), lambda i: (i, off // D))
    full = lambda a: pl.BlockSpec(a.shape, lambda i: (0,) * a.ndim)
    acc = lambda w: pl.BlockSpec((8, w), lambda i: (0, 0))
    return pl.pallas_call(
        body, name="head", grid=(S // tm,),
        in_specs=[row, row, row, seg(OFF_G0), seg(OFF_G1), row, full(gate_b), full(wout), full(fw)],
        out_specs=[row, row, row, row, row, pl.BlockSpec((tm, 2 * D), lambda i: (i, 0)), acc(LANE), acc(D), acc(2 * D)],
        out_shape=[jax.ShapeDtypeStruct((S, D), F32), jax.ShapeDtypeStruct((S, D), BF16), jax.ShapeDtypeStruct((S, D), BF16),
                   jax.ShapeDtypeStruct((S, D), BF16), jax.ShapeDtypeStruct((S, D), BF16), jax.ShapeDtypeStruct((S, 2 * D), BF16),
                   jax.ShapeDtypeStruct((8, LANE), F32), jax.ShapeDtypeStruct((8, D), F32), jax.ShapeDtypeStruct((8, 2 * D), F32)],
        compiler_params=_cp(("arbitrary",)),
    )(x, ya, yb, proj, proj, target, gate_b, wout, fw)


def _adam_update(g, w_ref, m_ref, v_ref, g_ref, d_ref, m2_ref, v2_ref):
    m2 = ADAM_B1 * m_ref[...] + (1.0 - ADAM_B1) * g
    v2 = ADAM_B2 * v_ref[...] + (1.0 - ADAM_B2) * (g * g)
    m_hat = m2 / (1.0 - ADAM_B1 ** ADAM_STEP)
    v_hat = v2 / (1.0 - ADAM_B2 ** ADAM_STEP)
    g_ref[...] = g
    d_ref[...] = -ADAM_LR * (m_hat / (jnp.sqrt(v_hat) + ADAM_EPS) + ADAM_WD * w_ref[...])
    m2_ref[...] = m2
    v2_ref[...] = v2


def _adamw_own(me, own, landed, w, m, v, *, tr, tc, name):
    _, R, C = landed.shape
    assert R % tr == 0 and C % tc == 0, (name, R, C, tr, tc)

    def body(me_ref, own_ref, p_ref, w_ref, m_ref, v_ref, g_ref, d_ref, m2_ref, v2_ref):
        mine = own_ref[0].astype(F32)
        g = jnp.where(me_ref[0] == 0, mine, p_ref[0].astype(F32))
        for k in range(1, N_DEV):
            g = g + jnp.where(me_ref[0] == k, mine, p_ref[k].astype(F32))
        _adam_update(g, w_ref, m_ref, v_ref, g_ref, d_ref, m2_ref, v2_ref)

    tile = pl.BlockSpec((tr, tc), lambda i, j, me_ref: (i, j))
    return pl.pallas_call(
        body, name=name,
        grid_spec=pltpu.PrefetchScalarGridSpec(
            num_scalar_prefetch=1, grid=(R // tr, C // tc),
            in_specs=[pl.BlockSpec((1, tr, tc), lambda i, j, me_ref: (me_ref[0], i, j)),
                      pl.BlockSpec((N_DEV, tr, tc), lambda i, j, me_ref: (0, i, j)), tile, tile, tile],
            out_specs=[tile, tile, tile, tile]),
        out_shape=[jax.ShapeDtypeStruct((R, C), F32)] * 4,
        compiler_params=_cp(("parallel", "parallel")),
    )(me, own, landed, w, m, v)


def _adamw(parts, w, m, v, *, tr, name):
    _, R, C = parts.shape
    assert R % tr == 0, (name, R, tr)

    def body(p_ref, w_ref, m_ref, v_ref, g_ref, d_ref, m2_ref, v2_ref):
        g = p_ref[0].astype(F32)
        for k in range(1, N_DEV):
            g = g + p_ref[k].astype(F32)
        _adam_update(g, w_ref, m_ref, v_ref, g_ref, d_ref, m2_ref, v2_ref)

    row = pl.BlockSpec((tr, C), lambda i: (i, 0))
    return pl.pallas_call(
        body, name=name, grid=(R // tr,),
        in_specs=[pl.BlockSpec((N_DEV, tr, C), lambda i: (0, i, 0)), row, row, row],
        out_specs=[row, row, row, row],
        out_shape=[jax.ShapeDtypeStruct((R, C), F32)] * 4,
        compiler_params=_cp(("parallel",)),
    )(parts, w, m, v)


def _place():
    x, y, c = lax.axis_index("x"), lax.axis_index("y"), lax.axis_index("c")
    return x, y, c


def _all_gather(arrs, *, name):
    n = len(arrs)

    def body(*refs):
        ins, outs = refs[:n], refs[n:2 * n]
        send_sems, recv_sems, local_sems = refs[2 * n:]
        x, y, c = _place()
        me, sibling = (x, y, c), (x, y, 1 - c)
        chips = [(1 - x, y), (x, 1 - y), (1 - x, 1 - y)]

        def idx(px, py, pc):
            return 4 * px + 2 * py + pc

        def copy(k, a, block, to, src=None):
            slab = outs[a].at[idx(*block)]
            return pltpu.make_async_remote_copy(
                src_ref=slab if src is None else src, dst_ref=slab,
                send_sem=send_sems.at[k, a], recv_sem=recv_sems.at[k, a], device_id=to, device_id_type=MESH)

        mine = [pltpu.make_async_copy(ins[a], outs[a].at[idx(*me)], local_sems.at[a]) for a in range(n)]
        for cp in mine:
            cp.start()
        first = []
        for a in range(n):
            first.append(copy(0, a, me, sibling, src=ins[a]))
            first += [copy(1 + j, a, me, (*chip, c), src=ins[a]) for j, chip in enumerate(chips)]
        for cp in first:
            cp.start()
        passed = []
        for j, chip in enumerate(chips):
            for a in range(n):
                copy(1 + j, a, (*chip, c), me).wait_recv()
                fwd = copy(4 + j, a, (*chip, c), sibling)
                fwd.start()
                passed.append(fwd)
        for a in range(n):
            copy(0, a, sibling, me).wait_recv()
            for j, chip in enumerate(chips):
                copy(4 + j, a, (*chip, 1 - c), me).wait_recv()
        for cp in first + passed:
            cp.wait_send()
        for cp in mine:
            cp.wait()

    anyspec = pl.BlockSpec(memory_space=pl.ANY)
    return pl.pallas_call(
        body, name=name,
        in_specs=[anyspec] * n, out_specs=[anyspec] * n,
        out_shape=[jax.ShapeDtypeStruct((N_DEV,) + a.shape, a.dtype) for a in arrs],
        scratch_shapes=[pltpu.SemaphoreType.DMA((7, n)), pltpu.SemaphoreType.DMA((7, n)), pltpu.SemaphoreType.DMA((n,))],
    )(*arrs)


W_ROWS = SEG_SSD[0] + SSD_PAD_W


GROUP = 16
INTERIOR = 1920


def _interior(k):
    lo = -(-(k * SHARD_IN) // GROUP) * GROUP
    hi = ((k + 1) * SHARD_IN) // GROUP * GROUP
    return lo, hi


def _dest_row(r):
    return r if r < 6144 else (r - 6144 + SEG_SSD[0] if r < 11296 else r - 11296 + SEG_GATE[0])


def _shard_pieces(k):
    lo_k, hi_k = _interior(k)
    out = []
    for lo, hi in ((0, 6144), (6144, 11296), (11296, W_IN)):
        a, b = max(lo, lo_k), min(hi, hi_k)
        if a < b:
            out.append((a - lo_k, b - a, _dest_row(a)))
    return out


GATHER_PARTS = 2


def _shard_parts(k):
    parts = [[] for _ in range(GATHER_PARTS)]
    for s0, n, d0 in _shard_pieces(k):
        step = -(-(n // GROUP) // GATHER_PARTS) * GROUP
        for p in range(GATHER_PARTS):
            a, b = min(p * step, n), min((p + 1) * step, n)
            if a < b:
                parts[p].append((s0 + a, b - a, d0 + a))
    return parts


def _patch_straddlers(wpT, heads, tails):
    for k in range(1, N_DEV):
        m = (k * SHARD_IN) % GROUP
        if m:
            group = jnp.concatenate([tails[k - 1, GROUP - m:], heads[k, :GROUP - m]], axis=0)
            wpT = lax.dynamic_update_slice(wpT, group, (_dest_row(k * SHARD_IN - m), 0))
    return wpT


def _gather_stages(k, win_ref, small, z_ref, n_zero, w_ref, send_sems, recv_sems, local_sems):
    x, y, c = k // 4, (k // 2) % 2, k % 2
    idx = lambda p: 4 * p[0] + 2 * p[1] + p[2]
    me, sib = (x, y, c), (x, y, 1 - c)
    xn, yn, dg = (1 - x, y, c), (x, 1 - y, c), (1 - x, 1 - y, c)
    parts = range(GATHER_PARTS)

    def copies(slot, block, to, part, own=False):
        kb = idx(block)
        out = []
        for j, (s0, n, d0) in enumerate(_shard_parts(kb)[part]):
            dst = w_ref.at[pl.ds(d0, n)]
            out.append((win_ref.at[pl.ds(s0, n)] if own else dst, dst, 2 * part + j))
        if part == 0:
            for j, (src, gathered) in enumerate(small):
                out.append((src if own else gathered.at[kb], gathered.at[kb], 2 * GATHER_PARTS + j))
        return [pltpu.make_async_remote_copy(src_ref=s, dst_ref=d, send_sem=send_sems.at[slot, j], recv_sem=recv_sems.at[slot, j],
                                             device_id=to, device_id_type=MESH) for s, d, j in out]

    def start(cps):
        for cp in cps:
            cp.start()

    def arrived(slot, block, part):
        for cp in copies(slot, block, me, part):
            cp.wait_recv()

    def local():
        pairs = [(win_ref.at[pl.ds(s0, n)], w_ref.at[pl.ds(d0, n)]) for s0, n, d0 in _shard_pieces(k)]
        pairs += [(src, gathered.at[k]) for src, gathered in small] + [(z_ref, w_ref.at[pl.ds(W_IN, n_zero)])]
        return [pltpu.make_async_copy(s, d, local_sems.at[j]) for j, (s, d) in enumerate(pairs)]

    relay = (xn, yn) if c == 1 else (yn, xn)

    def first():
        start(local())
        for p in parts:
            start(copies(0, me, sib, p, own=True) + copies(1, me, xn, p, own=True) + copies(2, me, yn, p, own=True))

    def hand_on():
        for p in parts:
            arrived(1, xn, p)
            start(copies(4, xn, sib, p))
            if c == 1:
                start(copies(3, *relay, p))
            arrived(2, yn, p)
            start(copies(5, yn, sib, p))
            if c == 0:
                start(copies(3, *relay, p))

    def finish():
        for p in parts:
            arrived(3, dg, p)
            start(copies(6, dg, sib, p))
        for p in parts:
            arrived(0, sib, p)
            arrived(4, (1 - x, y, 1 - c), p)
            arrived(5, (x, 1 - y, 1 - c), p)
            arrived(6, (1 - x, 1 - y, 1 - c), p)
        for p in parts:
            sent = (copies(0, me, sib, p, own=True) + copies(1, me, xn, p, own=True) + copies(2, me, yn, p, own=True)
                    + copies(3, *relay, p) + copies(4, xn, sib, p) + copies(5, yn, sib, p) + copies(6, dg, sib, p))
            for cp in sent:
                cp.wait_send()
        for cp in local():
            cp.wait()

    return first, hand_on, finish


def _gather_sems(n_small):
    n_arr = 2 * GATHER_PARTS + n_small
    return [pltpu.SemaphoreType.DMA((7, n_arr)), pltpu.SemaphoreType.DMA((7, n_arr)), pltpu.SemaphoreType.DMA((n_arr + 1,))]


def _gather_weights(win, head, tail, wout, cw, zeros):
    small_in = (wout, cw, head, tail)
    n_zero = zeros.shape[0]
    assert W_IN + n_zero == W_ROWS and W_IN % GROUP == 0

    def body(win_ref, wout_ref, cw_ref, head_ref, tail_ref, z_ref, w_ref, gout_ref, gcw_ref, ghead_ref, gtail_ref, *sems):
        x, y, c = _place()
        me = 4 * x + 2 * y + c
        small = ((wout_ref, gout_ref), (cw_ref, gcw_ref), (head_ref, ghead_ref), (tail_ref, gtail_ref))

        def run(k):
            for stage in _gather_stages(k, win_ref, small, z_ref, n_zero, w_ref, *sems):
                stage()

        for k in range(N_DEV):
            pl.when(me == k)(functools.partial(run, k))

    anyspec = pl.BlockSpec(memory_space=pl.ANY)
    return pl.pallas_call(
        body, name="gather_weights", in_specs=[anyspec] * 6, out_specs=[anyspec] * 5,
        out_shape=[jax.ShapeDtypeStruct((W_ROWS, D), win.dtype)]
        + [jax.ShapeDtypeStruct((N_DEV,) + a.shape, a.dtype) for a in small_in],
        scratch_shapes=_gather_sems(len(small_in)),
    )(win, wout, cw, head, tail, zeros)


_REL = [(dx, dy, dc) for dx in (0, 1) for dy in (0, 1) for dc in (0, 1)][1:]
_HBM = pl.BlockSpec(memory_space=pltpu.HBM)
_SEM = pl.BlockSpec(memory_space=pltpu.SEMAPHORE)
_EFFECT = pltpu.SideEffectType.DATAFLOW_SIDE_EFFECTING


def _peer(k):
    x, y, c = _place()
    dx, dy, dc = _REL[k]
    return (1 - x if dx else x, 1 - y if dy else y, 1 - c if dc else c)


def _exchange_start(parts, *, name):
    n = len(parts)

    def body(*refs):
        ins, lands = refs[:n], refs[n:2 * n]
        send_sems, recv_sems, token = refs[2 * n], refs[2 * n + 1], refs[-1]
        x, y, c = _place()
        me = 4 * x + 2 * y + c
        for a in range(n):
            for k in range(len(_REL)):
                px, py, pc = _peer(k)
                pltpu.make_async_remote_copy(
                    src_ref=ins[a].at[4 * px + 2 * py + pc], dst_ref=lands[a].at[me],
                    send_sem=send_sems.at[len(_REL) * a + k], recv_sem=recv_sems.at[len(_REL) * a + k],
                    device_id=(px, py, pc), device_id_type=MESH).start()
        token[...] = jnp.zeros_like(token)

    sem = pltpu.SemaphoreType.DMA((len(_REL) * n,))
    bufs = [pltpu.HBM(p.shape, p.dtype) for p in parts]
    outs = pl.pallas_call(
        body, name=name,
        out_shape=(sem, sem, *bufs, *bufs, jax.ShapeDtypeStruct((8, LANE), F32)),
        in_specs=(_HBM,) * (2 * n), out_specs=(_SEM, _SEM, *(_HBM,) * (2 * n), pl.BlockSpec(memory_space=pltpu.VMEM)),
        input_output_aliases={i: 2 + i for i in range(2 * n)},
        compiler_params=pltpu.CompilerParams(has_side_effects=_EFFECT),
    )(*[pltpu.with_memory_space_constraint(p, pltpu.HBM) for p in parts],
      *[pltpu.with_memory_space_constraint(lax.empty(p.shape, p.dtype), pltpu.HBM) for p in parts])
    return outs[0], outs[1], outs[2:2 + n], outs[2 + n:2 + 2 * n], outs[-1]


def _exchange_wait(send_sems, recv_sems, parts, lands, after, *, name):
    n = len(parts)

    def body(*refs):
        ins, lands_ = refs[:n], refs[n:2 * n]
        ssem, rsem = refs[2 * n], refs[2 * n + 1]
        for a in range(n):
            for k in range(len(_REL)):
                px, py, pc = _peer(k)
                p = 4 * px + 2 * py + pc
                cp = pltpu.make_async_remote_copy(
                    src_ref=ins[a].at[p], dst_ref=lands_[a].at[p],
                    send_sem=ssem.at[len(_REL) * a + k], recv_sem=rsem.at[len(_REL) * a + k],
                    device_id=(px, py, pc), device_id_type=MESH)
                cp.wait_send()
                cp.wait_recv()

    bufs = [pltpu.HBM(p.shape, p.dtype) for p in parts]
    outs = pl.pallas_call(
        body, name=name, out_shape=(*bufs, *bufs),
        in_specs=(*(_HBM,) * (2 * n), _SEM, _SEM, pl.BlockSpec(memory_space=pl.ANY)), out_specs=(_HBM,) * (2 * n),
        input_output_aliases={i: i for i in range(2 * n)},
        compiler_params=pltpu.CompilerParams(has_side_effects=_EFFECT),
    )(*parts, *lands, send_sems, recv_sems, after)
    return outs[:n], outs[n:]


WEIGHTS = ('norm_w', 'w_in', 'gate_b', 'sgu_norm_g', 'sgu_norm_b', 'sgu_w', 'sgu_b', 'conv_w', 'conv_b', 'dt_bias', 'A_log',
           'D_skip', 'ssd_norm_w', 'w_out', 'final_norm_w')
SHARDED = ('w_in', 'conv_w', 'w_out')
PACK_ROW = 8 * LANE


def _constants():
    tri = np.tril(np.ones((CHUNK, CHUNK), np.float32))
    expand = np.zeros((DT_W, D), np.float32)
    for h in range(HEADS):
        expand[h, h * HEADDIM:(h + 1) * HEADDIM] = 1.0
    sel = np.zeros((D, LANE), np.float32)
    for g in range(SGU_GROUPS):
        sel[g * LANE:(g + 1) * LANE, g] = 1.0
    pos_chunk = np.arange(SGU_BLOCK) // CHUNK
    mask = (pos_chunk[None, :] <= pos_chunk[:, None]).astype(np.float32)
    shift = np.zeros(((CONV_K - 1) * CHUNK, HALO_BLK + CHUNK), np.float32)
    for kk in range(CONV_K - 1):
        for t in range(CHUNK):
            shift[kk * CHUNK + t, HALO_BLK - (CONV_K - 1) + t + kk] = 1.0
    return dict(tri=jnp.asarray(tri, BF16), triT=jnp.asarray(tri.T.copy(), BF16), expand=jnp.asarray(np.tile(expand, (3, 1)), BF16),
                shift=jnp.asarray(shift, BF16),
                expandT=jnp.asarray(expand.T.copy(), BF16), sel=jnp.asarray(sel), mask=jnp.asarray(mask))


def _to_shards(segs):
    starts = np.cumsum([0] + [n for _, n in segs])
    assert starts[-1] == W_IN
    slabs = []
    for k in range(N_DEV):
        pieces = []
        for (s, n), s0 in zip(segs, starts[:-1]):
            lo, hi = max(k * SHARD_IN, s0), min((k + 1) * SHARD_IN, s0 + n)
            if lo < hi:
                pieces.append(s[lo - s0:hi - s0])
        slabs.append(jnp.concatenate(pieces, axis=0))
    return jnp.stack(slabs)


def _local_step(x2, tgt, wpT, wout, cw, p, exchange_small, exchange):
    S = x2.shape[0]
    k = _constants()
    xn = _norm_fwd(x2, p['norm_w'], tm=min(512, S))
    proj = _matmul(xn, wpT, trans_b=True, out_dtype=BF16, tm=min(1024, S), tn=2048, tk=D, name="in_proj")
    wm32 = p['sgu_w'][0] * k['mask']
    wm = wm32.astype(BF16)
    wmT = jnp.swapaxes(wm32, 1, 2).astype(BF16)
    bias_full = jnp.repeat(p['sgu_b'][0].T, LANE, axis=1)
    tm_sgu = min(256, S)
    ya = _sgu_fwd(proj, p['sgu_norm_g'], p['sgu_norm_b'], wm, bias_full, tm=tm_sgu)
    pad32 = lambda a: jnp.pad(a, ((0, 0), (0, DT_W - HEADS)))
    dtb_p, alog_p = pad32(p['dt_bias']), pad32(p['A_log'])
    d_exp = jnp.repeat(p['D_skip'], HEADDIM, axis=1)
    ssd_args = (cw, p['conv_b'], dtb_p, alog_p, d_exp, p['ssd_norm_w'])
    y, yb, states = _ssd_fwd(proj, *ssd_args, k['tri'], k['expand'], k['shift'])
    dh, dhb, mb, dya, dyb, dgl, loss, dfw, dgb = _head(
        x2, ya, yb, proj, tgt, p['gate_b'], wout, p['final_norm_w'][None, :], tm=min(256, S))
    dsgu, dws, dbsT, dsg, dsb = _sgu_bwd(proj, dya, p['sgu_norm_g'], p['sgu_norm_b'], wm, wmT, bias_full, k['mask'], k['sel'],
                                         tm=tm_sgu)
    dssd, dcw, dcb, ddtb, dalog, dD, dnw = _ssd_bwd(proj, dyb, y, states, *ssd_args, k['tri'], k['triT'], k['expand'], k['expandT'],
                                                    k['shift'])
    grads = dict(
        gate_b=dgb[0:1], sgu_norm_g=dsg[0:1], sgu_norm_b=dsb[0:1], sgu_w=dws[None],
        sgu_b=dbsT[:, :SGU_GROUPS].T[None], conv_w=dcw[0:CONV_K][None], conv_b=dcb[0:1], dt_bias=ddtb[0:1, :HEADS],
        A_log=dalog[0:1, :HEADS], D_skip=dD[0:1, :HEADS], ssd_norm_w=dnw[0:1], final_norm_w=dfw[0])
    token = exchange_small(loss[0, 0], grads)
    tk = min(4096, S)
    tn = 1024
    dwT_sgu = _matmul(dsgu, xn, trans_a=True, out_dtype=BF16, tm=1024, tn=tn, tk=tk, after=token, name="dw_in_sgu")
    dwT_gate = _matmul(dgl, xn, trans_a=True, out_dtype=BF16, tm=1024, tn=tn, tk=tk, name="dw_in_gate")
    dwT_ssd = _matmul(dssd, xn, trans_a=True, out_dtype=BF16, tm=1024, tn=tn, tk=tk, name="dw_in_ssd")
    dw_out = _matmul(mb, dhb, trans_a=True, out_dtype=BF16, tm=1024, tn=tn, tk=tk, name="dw_out")
    token = exchange([(dwT_sgu, SEG_SGU[1]), (dwT_ssd, W_IN - SEG_SSD[0]), (dwT_gate, SEG_GATE[1])], dw_out)
    tm = min(1024, S)
    dxn = _matmul(dsgu, wpT, tm=tm, tn=tn, tk=3072, after=token, name="dxn_sgu")
    dxn = _matmul(dgl, wpT, b_koff=SEG_GATE[0] // 2048, tm=tm, tn=tn, tk=2048, add=dxn, name="dxn_gate")
    dxn = _matmul(dssd, wpT, b_koff=SEG_SSD[0] // 2048, tm=tm, tn=tn, tk=2048, add=dxn, name="dxn_ssd")
    grad_x, dnorm = _norm_bwd(x2, p['norm_w'], dxn, dh, tm=min(256, S))
    return grad_x, dnorm[0:1]


def _pack(arrs):
    rows, offs, r = [], [], 0
    for a in arrs:
        n = a.size
        nr = -(-n // PACK_ROW) * 8
        rows.append(jnp.pad(a.reshape(-1).astype(F32), (0, nr * LANE - n)).reshape(nr, LANE))
        offs.append(r)
        r += nr
    return jnp.concatenate(rows, axis=0), offs


def kernel(x, norm_w, w_in, gate_b, sgu_norm_g, sgu_norm_b, sgu_w, sgu_b, conv_w, conv_b, dt_bias, A_log, D_skip, ssd_norm_w, w_out, final_norm_w, loss_target, m_norm_w, m_w_in, m_gate_b, m_sgu_norm_g, m_sgu_norm_b, m_sgu_w, m_sgu_b, m_conv_w, m_conv_b, m_dt_bias, m_A_log, m_D_skip, m_ssd_norm_w, m_w_out, m_final_norm_w, v_norm_w, v_w_in, v_gate_b, v_sgu_norm_g, v_sgu_norm_b, v_sgu_w, v_sgu_b, v_conv_w, v_conv_b, v_dt_bias, v_A_log, v_D_skip, v_ssd_norm_w, v_w_out, v_final_norm_w):
    w = dict(norm_w=norm_w, w_in=w_in, gate_b=gate_b, sgu_norm_g=sgu_norm_g, sgu_norm_b=sgu_norm_b, sgu_w=sgu_w, sgu_b=sgu_b,
             conv_w=conv_w, conv_b=conv_b, dt_bias=dt_bias, A_log=A_log, D_skip=D_skip, ssd_norm_w=ssd_norm_w, w_out=w_out,
             final_norm_w=final_norm_w)
    m = dict(norm_w=m_norm_w, w_in=m_w_in, gate_b=m_gate_b, sgu_norm_g=m_sgu_norm_g, sgu_norm_b=m_sgu_norm_b, sgu_w=m_sgu_w,
             sgu_b=m_sgu_b, conv_w=m_conv_w, conv_b=m_conv_b, dt_bias=m_dt_bias, A_log=m_A_log, D_skip=m_D_skip,
             ssd_norm_w=m_ssd_norm_w, w_out=m_w_out, final_norm_w=m_final_norm_w)
    v = dict(norm_w=v_norm_w, w_in=v_w_in, gate_b=v_gate_b, sgu_norm_g=v_sgu_norm_g, sgu_norm_b=v_sgu_norm_b, sgu_w=v_sgu_w,
             sgu_b=v_sgu_b, conv_w=v_conv_w, conv_b=v_conv_b, dt_bias=v_dt_bias, A_log=v_A_log, D_skip=v_D_skip,
             ssd_norm_w=v_ssd_norm_w, w_out=v_w_out, final_norm_w=v_final_norm_w)
    me = 4 * lax.axis_index("x") + 2 * lax.axis_index("y") + lax.axis_index("c")
    shard_cw = XBC_W // N_DEV

    tpose = lambda a: jnp.swapaxes(a[0], 0, 1)
    wT = tpose(w_in).astype(BF16)
    first_group = (GROUP - (me * SHARD_IN) % GROUP) % GROUP
    window = lax.dynamic_slice(jnp.pad(wT, ((0, GROUP), (0, 0))), (first_group, 0), (INTERIOR, D))
    wpT, g_out, g_cw, heads, tails = _gather_weights(window, wT[:GROUP], wT[SHARD_IN - GROUP:], w_out[0].astype(BF16),
                                                     conv_w[0], jnp.zeros((W_ROWS - W_IN, D), BF16))
    wpT = _patch_straddlers(wpT, heads, tails)
    wout_full = g_out.reshape(D, D)
    cw_full = jnp.swapaxes(g_cw, 0, 1).reshape(CONV_K, XBC_W)

    flight = {}

    small = [n for n in WEIGHTS if n not in SHARDED and n != 'norm_w']
    early = {}

    def exchange_small(loss_part, grads):
        early['packed'], early['offs'] = _pack([grads[n] for n in small] + [loss_part, grads['conv_w']])
        parts = [jnp.broadcast_to(early['packed'][None], (N_DEV,) + early['packed'].shape)]
        early['sems'], early['rsems'], early['parts'], early['lands'], token = _exchange_start(parts, name="small_start")
        return token

    def exchange(dw_inT_segs, dw_out):
        parts = [_to_shards(dw_inT_segs), dw_out.reshape(N_DEV, D // N_DEV, D)]
        flight['sems'], flight['rsems'], flight['parts'], flight['lands'], token = _exchange_start(parts, name="exchange_start")
        return token

    grad_x, dnorm = _local_step(x[0], loss_target[0], wpT, wout_full, cw_full, w, exchange_small, exchange)
    _, (land_small,) = _exchange_wait(early['sems'], early['rsems'], early['parts'], early['lands'], grad_x, name="small_wait")
    (own_in, own_out), (land_in, land_out) = _exchange_wait(
        flight['sems'], flight['rsems'], flight['parts'], flight['lands'], grad_x, name="exchange_wait")
    me_arr = jnp.reshape(me, (1,)).astype(jnp.int32)
    res = {}
    res['w_in'] = [jnp.swapaxes(o, 0, 1) for o in _adamw_own(
        me_arr, own_in, land_in, tpose(w_in), tpose(m_w_in), tpose(v_w_in), tr=SHARD_IN, tc=256, name="adamw_w_in")]
    res['w_out'] = _adamw_own(me_arr, own_out, land_out, w_out[0], m_w_out[0], v_w_out[0], tr=128, tc=D, name="adamw_w_out")

    (norm_parts,) = _all_gather([_pack([dnorm])[0]], name="gather_norm")
    norm_outs = _adamw(norm_parts, *[_pack([d['norm_w']])[0] for d in (w, m, v)], tr=norm_parts.shape[1], name="adamw_norm")
    res['norm_w'] = [o.reshape(-1)[:D].reshape(w['norm_w'].shape) for o in norm_outs]

    offs = early['offs']
    gathered = lax.dynamic_update_slice(land_small, early['packed'][None], (me, 0, 0))
    off_loss, off_cw = offs[-2], offs[-1]
    cw_parts = gathered[:, off_cw:, :].reshape(N_DEV, CONV_K, XBC_W)
    cw_parts = lax.dynamic_slice_in_dim(cw_parts, me * shard_cw, shard_cw, axis=2)
    cw_rows = _pack([cw_parts[0]])[0].shape[0]
    cw_parts = jnp.pad(cw_parts.reshape(N_DEV, -1), ((0, 0), (0, cw_rows * LANE - CONV_K * shard_cw))).reshape(N_DEV, cw_rows, LANE)
    parts = jnp.concatenate([gathered[:, :off_cw, :], cw_parts], axis=1)
    zero = jnp.zeros((), F32)
    packs = [_pack([d[n] for n in small] + [zero, d['conv_w']])[0] for d in (w, m, v)]
    outs = _adamw(parts, *packs, tr=parts.shape[1], name="adamw_small")

    def unpack(o, name):
        if name == 'conv_w':
            return o[off_cw:off_cw + cw_rows].reshape(-1)[:CONV_K * shard_cw].reshape(w['conv_w'].shape)
        r0 = offs[small.index(name)]
        n = w[name].size
        return o[r0:r0 + -(-n // PACK_ROW) * 8].reshape(-1)[:n].reshape(w[name].shape)

    for n in small + ['conv_w']:
        res[n] = [unpack(o, n) for o in outs]
    for n in ('w_in', 'w_out'):
        res[n] = [o[None] for o in res[n]]
    loss = outs[0][off_loss, 0]
    return (loss, grad_x[None], *[res[n][0] for n in WEIGHTS], *[res[n][1] for n in WEIGHTS],
            *[res[n][2] for n in WEIGHTS], *[res[n][3] for n in WEIGHTS])
```

```python
import functools

import numpy as np
import jax
import jax.numpy as jnp
from jax import lax
from jax.experimental import pallas as pl
from jax.experimental.pallas import tpu as pltpu

F32 = jnp.float32
BF16 = jnp.bfloat16
HI = lax.Precision.HIGHEST
MESH = pl.DeviceIdType.MESH

D = 2048
EPS = 1e-5
SGU_BLOCK = 128
SGU_GROUPS = 16
CHUNK = 64
HEADS = 32
HEADDIM = 64
SSD_GROUPS = 4
GROUP_W = D // SSD_GROUPS
STATE = 128
CONV_K = 4
XBC_W = D + 2 * SSD_GROUPS * STATE
W_IN = 15392
N_DEV = 8
SHARD_IN = W_IN // N_DEV
ADAM_LR, ADAM_B1, ADAM_B2, ADAM_EPS, ADAM_WD, ADAM_STEP = 0.001, 0.9, 0.999, 1e-08, 0.01, 10

REF_SGU_END = 3 * D
REF_GATE_START = W_IN - 2 * D
LANE = 128
DT_W = LANE
OFF_U, OFF_V, OFF_ZA, OFF_G0, OFF_G1, OFF_ZB = (i * D for i in range(6))
OFF_XBC = OFF_ZB + D
OFF_DT = OFF_XBC + XBC_W
SEG_SGU = (OFF_U, 3 * D)
SEG_GATE = (OFF_G0, 2 * D)
SEG_SSD = (OFF_ZB, D + XBC_W + DT_W)
WP = SEG_SSD[0] + SEG_SSD[1]
SSD_PAD_W = 3 * D
VMEM_BYTES = 64 * 1024 * 1024
VMEM_LIMIT = VMEM_BYTES - 8 * 1024 * 1024


def _cp(sem=None, vmem=VMEM_LIMIT):
    return pltpu.CompilerParams(dimension_semantics=sem, vmem_limit_bytes=vmem)


def _sigmoid(x):
    return 1.0 / (1.0 + jnp.exp(-x))


def _softplus(x):
    return jnp.maximum(x, 0.0) + jnp.log(1.0 + jnp.exp(-jnp.abs(x)))


def _dot(a, b, precision=None):
    return jnp.dot(a, b, preferred_element_type=F32, precision=precision)


def _dot_nt(a, b, precision=None):
    return lax.dot_general(a, b, (((1,), (1,)), ((), ())), preferred_element_type=F32, precision=precision)


def _dot_tn(a, b, precision=None):
    return lax.dot_general(a, b, (((0,), (0,)), ((), ())), preferred_element_type=F32, precision=precision)


def _split3(a):
    hi = a.astype(BF16)
    r = a - hi.astype(F32)
    mid = r.astype(BF16)
    return hi, mid, (r - mid.astype(F32)).astype(BF16)


def _sel_right(a, sel01):
    m = a.shape[0]
    r = _dot(jnp.concatenate(_split3(a), axis=0), sel01)
    return (r[0:m] + r[m:2 * m]) + r[2 * m:3 * m]


def _sel_right_k(a, sel01_x3):
    return _dot(jnp.concatenate(_split3(a), axis=1), sel01_x3)


def _sel_left(sel01, a):
    n = a.shape[1]
    r = _dot(sel01, jnp.concatenate(_split3(a), axis=1))
    return (r[:, 0:n] + r[:, n:2 * n]) + r[:, 2 * n:3 * n]


def _matmul(a, b, *, trans_a=False, trans_b=False, b_koff=0, out_dtype=F32, tm, tn, tk, add=None, after=None, name):
    K, M = a.shape if trans_a else a.shape[::-1]
    N = b.shape[0] if trans_b else b.shape[1]
    assert M % tm == 0 and N % tn == 0 and K % tk == 0 and not (trans_a and trans_b), (name, M, N, K, tm, tn, tk)
    nk = K // tk

    def body(*refs):
        a_ref, b_ref = refs[:2]
        add_ref = refs[2] if add is not None else None
        o_ref, acc_ref = refs[-2:]
        k = pl.program_id(2)
        if trans_a:
            part = _dot_tn(a_ref[...], b_ref[...])
        else:
            part = _dot_nt(a_ref[...], b_ref[...]) if trans_b else _dot(a_ref[...], b_ref[...])

        def result(r):
            if add_ref is not None:
                r = r + add_ref[...]
            return r.astype(out_dtype)

        if nk == 1:
            o_ref[...] = result(part)
        else:
            @pl.when(k == 0)
            def _():
                acc_ref[...] = part

            @pl.when(jnp.logical_and(k > 0, k < nk - 1))
            def _():
                acc_ref[...] += part

            @pl.when(k == nk - 1)
            def _():
                o_ref[...] = result(acc_ref[...] + part)

    in_specs = [pl.BlockSpec((tk, tm), lambda i, j, k: (k, i)) if trans_a else pl.BlockSpec((tm, tk), lambda i, j, k: (i, k)),
                pl.BlockSpec((tn, tk), lambda i, j, k: (j, k)) if trans_b else pl.BlockSpec((tk, tn), lambda i, j, k: (k + b_koff, j))]
    args = [a, b]
    if add is not None:
        in_specs.append(pl.BlockSpec((tm, tn), lambda i, j, k: (i, j)))
        args.append(add)
    if after is not None:
        in_specs.append(pl.BlockSpec(memory_space=pl.ANY))
        args.append(after)
    return pl.pallas_call(
        body, name=name, grid=(M // tm, N // tn, nk), in_specs=in_specs,
        out_specs=pl.BlockSpec((tm, tn), lambda i, j, k: (i, j)),
        out_shape=jax.ShapeDtypeStruct((M, N), out_dtype),
        scratch_shapes=[pltpu.VMEM((tm, tn), F32)],
        compiler_params=_cp(("parallel", "parallel", "arbitrary")),
    )(*args)


def _norm_fwd(x, w, *, tm):
    S = x.shape[0]

    def body(x_ref, w_ref, o_ref):
        xv = x_ref[...]
        r = lax.rsqrt(jnp.mean(xv * xv, axis=-1, keepdims=True) + EPS)
        o_ref[...] = (xv * r * w_ref[...]).astype(BF16)

    return pl.pallas_call(
        body, name="norm_fwd", grid=(S // tm,),
        in_specs=[pl.BlockSpec((tm, D), lambda i: (i, 0)), pl.BlockSpec((1, D), lambda i: (0, 0))],
        out_specs=pl.BlockSpec((tm, D), lambda i: (i, 0)),
        out_shape=jax.ShapeDtypeStruct((S, D), BF16), compiler_params=_cp(("parallel",)),
    )(x, w)


def _norm_bwd(x, w, dxn, dh, *, tm):
    S = x.shape[0]

    def body(x_ref, w_ref, dxn_ref, dh_ref, gx_ref, dw_ref):
        xv = x_ref[...]
        r = lax.rsqrt(jnp.mean(xv * xv, axis=-1, keepdims=True) + EPS)
        xh = xv * r
        dxn_v = dxn_ref[...]
        dxh = dxn_v * w_ref[...]
        gx_ref[...] = dh_ref[...] + r * (dxh - xh * jnp.mean(dxh * xh, axis=-1, keepdims=True))

        @pl.when(pl.program_id(0) == 0)
        def _():
            dw_ref[...] = jnp.zeros_like(dw_ref)

        dw_ref[0:1, :] += jnp.sum(dxn_v * xh, axis=0, keepdims=True)

    row = pl.BlockSpec((tm, D), lambda i: (i, 0))
    return pl.pallas_call(
        body, name="norm_bwd", grid=(S // tm,),
        in_specs=[row, pl.BlockSpec((1, D), lambda i: (0, 0)), row, row],
        out_specs=[row, pl.BlockSpec((8, D), lambda i: (0, 0))],
        out_shape=[jax.ShapeDtypeStruct((S, D), F32), jax.ShapeDtypeStruct((8, D), F32)],
        compiler_params=_cp(("arbitrary",)),
    )(x, w, dxn, dh)


def _sgu_core(u_ref, v_ref, z_ref, g_ref, b_ref, wm_ref, bias_ref, vnb_ref, mixed_ref, tm):
    v = v_ref[...].astype(F32)
    mu = jnp.mean(v, axis=-1, keepdims=True)
    vc = v - mu
    rs = lax.rsqrt(jnp.mean(vc * vc, axis=-1, keepdims=True) + EPS)
    vh = vc * rs
    vnb_ref[...] = (vh * g_ref[...] + b_ref[...]).astype(BF16)
    for blk in range(tm // SGU_BLOCK):
        rows = pl.ds(blk * SGU_BLOCK, SGU_BLOCK)
        for gi in range(SGU_GROUPS):
            cols = pl.ds(gi * LANE, LANE)
            mixed_ref[rows, cols] = _dot(wm_ref[gi], vnb_ref[rows, cols]) + bias_ref[:, cols]
    return vh, rs


def _sgu_fwd(proj, g, b, wm, bias_full, *, tm):
    S = proj.shape[0]

    def body(u_ref, v_ref, z_ref, g_ref, b_ref, wm_ref, bias_ref, y_ref, vnb_ref, mixed_ref):
        _sgu_core(u_ref, v_ref, z_ref, g_ref, b_ref, wm_ref, bias_ref, vnb_ref, mixed_ref, tm)
        z = z_ref[...].astype(F32)
        y_ref[...] = (u_ref[...].astype(F32) * mixed_ref[...] * (z * _sigmoid(z))).astype(BF16)

    seg = lambda off: pl.BlockSpec((tm, D), lambda i: (i, off // D))
    full = lambda a: pl.BlockSpec(a.shape, lambda i: (0,) * a.ndim)
    return pl.pallas_call(
        body, name="sgu_fwd", grid=(S // tm,),
        in_specs=[seg(OFF_U), seg(OFF_V), seg(OFF_ZA), full(g), full(b), full(wm), full(bias_full)],
        out_specs=pl.BlockSpec((tm, D), lambda i: (i, 0)),
        out_shape=jax.ShapeDtypeStruct((S, D), BF16),
        scratch_shapes=[pltpu.VMEM((tm, D), BF16), pltpu.VMEM((tm, D), F32)],
        compiler_params=_cp(("parallel",)),
    )(proj, proj, proj, g, b, wm, bias_full)


def _sgu_bwd(proj, dy, g, b, wm, wmT, bias_full, mask, sel, *, tm):
    S = proj.shape[0]
    nsteps = S // tm

    def body(u_ref, v_ref, z_ref, dy_ref, g_ref, b_ref, wm_ref, wmT_ref, bias_ref, mask_ref, sel_ref,
             dp_ref, dws_ref, dbs_ref, dg_ref, db_ref, vnb_ref, mixed_ref, dmb_ref, dvn_ref, dbias_ref):
        i = pl.program_id(0)

        @pl.when(i == 0)
        def _():
            dws_ref[...] = jnp.zeros_like(dws_ref)
            dg_ref[...] = jnp.zeros_like(dg_ref)
            db_ref[...] = jnp.zeros_like(db_ref)
            dbias_ref[...] = jnp.zeros_like(dbias_ref)

        vh, rs = _sgu_core(u_ref, v_ref, z_ref, g_ref, b_ref, wm_ref, bias_ref, vnb_ref, mixed_ref, tm)
        u = u_ref[...].astype(F32)
        z = z_ref[...].astype(F32)
        dy_v = dy_ref[...].astype(F32)
        mixed = mixed_ref[...]
        sg = _sigmoid(z)
        sz = z * sg
        dp_ref[:, 0:D] = (dy_v * mixed * sz).astype(BF16)
        dp_ref[:, 2 * D:3 * D] = (dy_v * u * mixed * (sg * (1.0 + z * (1.0 - sg)))).astype(BF16)
        dmixed = dy_v * u * sz
        dmb_ref[...] = dmixed.astype(BF16)
        for blk in range(tm // SGU_BLOCK):
            dbias_ref[...] += dmixed[blk * SGU_BLOCK:(blk + 1) * SGU_BLOCK, :]
        for blk in range(tm // SGU_BLOCK):
            rows = pl.ds(blk * SGU_BLOCK, SGU_BLOCK)
            for gi in range(SGU_GROUPS):
                cols = pl.ds(gi * LANE, LANE)
                dm = dmb_ref[rows, cols]
                dvn_ref[rows, cols] = _dot(wmT_ref[gi], dm)
                dws_ref[gi] += _dot_nt(dm, vnb_ref[rows, cols])
        dvn = dvn_ref[...]
        dg_ref[0:1, :] += jnp.sum(dvn * vh, axis=0, keepdims=True)
        db_ref[0:1, :] += jnp.sum(dvn, axis=0, keepdims=True)
        dvh = dvn * g_ref[...]
        dv = rs * (dvh - jnp.mean(dvh, axis=-1, keepdims=True) - vh * jnp.mean(dvh * vh, axis=-1, keepdims=True))
        dp_ref[:, D:2 * D] = dv.astype(BF16)

        @pl.when(i == nsteps - 1)
        def _():
            for gi in range(SGU_GROUPS):
                dws_ref[gi] = dws_ref[gi] * mask_ref[...]
            dbs_ref[...] = _dot(dbias_ref[...], sel_ref[...], precision=HI)

    seg = lambda off: pl.BlockSpec((tm, D), lambda i: (i, off // D))
    full = lambda a: pl.BlockSpec(a.shape, lambda i: (0,) * a.ndim)
    return pl.pallas_call(
        body, name="sgu_bwd", grid=(nsteps,),
        in_specs=[seg(OFF_U), seg(OFF_V), seg(OFF_ZA), pl.BlockSpec((tm, D), lambda i: (i, 0)),
                  full(g), full(b), full(wm), full(wmT), full(bias_full), full(mask), full(sel)],
        out_specs=[pl.BlockSpec((tm, 3 * D), lambda i: (i, 0)),
                   pl.BlockSpec((SGU_GROUPS, SGU_BLOCK, SGU_BLOCK), lambda i: (0, 0, 0)),
                   pl.BlockSpec((SGU_BLOCK, LANE), lambda i: (0, 0)),
                   pl.BlockSpec((8, D), lambda i: (0, 0)), pl.BlockSpec((8, D), lambda i: (0, 0))],
        out_shape=[jax.ShapeDtypeStruct((S, 3 * D), BF16),
                   jax.ShapeDtypeStruct((SGU_GROUPS, SGU_BLOCK, SGU_BLOCK), F32),
                   jax.ShapeDtypeStruct((SGU_BLOCK, LANE), F32),
                   jax.ShapeDtypeStruct((8, D), F32), jax.ShapeDtypeStruct((8, D), F32)],
        scratch_shapes=[pltpu.VMEM((tm, D), BF16), pltpu.VMEM((tm, D), F32), pltpu.VMEM((tm, D), BF16),
                        pltpu.VMEM((tm, D), F32), pltpu.VMEM((SGU_BLOCK, D), F32)],
        compiler_params=_cp(("arbitrary",)),
    )(proj, proj, proj, dy, g, b, wm, wmT, bias_full, mask, sel)


SSD_T = 4 * CHUNK
HALO = 8
HALO_BLK = 16


def _pair_masks():
    row = lax.broadcasted_iota(jnp.int32, (CHUNK, LANE), 0)
    lane = lax.broadcasted_iota(jnp.int32, (CHUNK, LANE), 1)
    pos = jnp.where(lane >= CHUNK, lane - CHUNK, lane)
    diag = (row == pos).astype(F32)
    causal = row >= pos
    lo = (lane < CHUNK).astype(F32)
    return diag, causal, lo, 1.0 - lo


def _ssd_chunk_fwd(c, ext_ref, shift_ref, dt_ref, cw_ref, cb_ref, dtb_ref, alog_ref, tri_ref, exp_ref):
    r0 = c * CHUNK
    win = ext_ref[pl.ds(r0, HALO_BLK + CHUNK), :]
    sh = _dot(shift_ref[...], win)
    taps = [sh[k * CHUNK:(k + 1) * CHUNK] for k in range(CONV_K - 1)] + [win[HALO_BLK:].astype(F32)]
    pre = cb_ref[...] + sum(cw_ref[k:k + 1, :] * taps[k] for k in range(CONV_K))
    sg = _sigmoid(pre)
    xc = pre * sg
    dtr = dt_ref[pl.ds(r0, CHUNK), :].astype(F32) + dtb_ref[...]
    dtv = _softplus(dtr)
    A = -jnp.exp(alog_ref[...])
    acs = _sel_left(tri_ref[...], dtv * A)
    both = _sel_right_k(jnp.concatenate([acs, dtv], axis=0), exp_ref[...])
    E, dtE = both[0:CHUNK], both[CHUNK:2 * CHUNK]
    return dict(taps=taps, pre=pre, sg=sg, xc=xc, dtr=dtr, dtv=dtv, A=A, E=E, dtE=dtE)


def _ssd_fwd(proj, conv_w, conv_b, dtb_p, alog_p, d_exp, norm_w, tri, expand, shift):
    S = proj.shape[0]
    T = SSD_T
    nsteps = S // T
    ncl = T // CHUNK

    def body(zb_ref, xbc_ref, halo_ref, dt_ref, cw_ref, cb_ref, dtb_ref, alog_ref, dexp_ref, nw_ref, tri_ref, exp_ref, shift_ref,
             y_ref, yb_ref, st_ref, ht_ref, ext_ref):
        i = pl.program_id(0)

        @pl.when(i == 0)
        def _():
            ht_ref[...] = jnp.zeros_like(ht_ref)
            ext_ref[0:HALO_BLK, :] = jnp.zeros((HALO_BLK, XBC_W), BF16)

        @pl.when(i > 0)
        def _():
            ext_ref[0:HALO_BLK, :] = halo_ref[...]

        ext_ref[HALO_BLK:HALO_BLK + T, :] = xbc_ref[...]
        diag, causal, lo, hi = _pair_masks()
        for c in range(ncl):
            q = _ssd_chunk_fwd(c, ext_ref, shift_ref, dt_ref, cw_ref, cb_ref, dtb_ref, alog_ref, tri_ref, exp_ref)
            rows = pl.ds(c * CHUNK, CHUNK)
            xc, E, dtE = q["xc"], q["E"], q["dtE"]
            xs = xc[:, 0:D]
            total = E[CHUNK - 1:CHUNK, :]
            x_dt = xs * dtE
            eE = jnp.exp(E)
            xw = x_dt * jnp.exp(total - E)
            st_ref[c] = ht_ref[...]
            for g in range(SSD_GROUPS):
                gc = slice(g * GROUP_W, (g + 1) * GROUP_W)
                Bg = xc[:, D + g * STATE:D + (g + 1) * STATE].astype(BF16)
                Cg = xc[:, D + SSD_GROUPS * STATE + g * STATE:D + SSD_GROUPS * STATE + (g + 1) * STATE].astype(BF16)
                cb2 = _dot_nt(Cg, jnp.concatenate([Bg, Bg], axis=0))
                htg = ht_ref[:, gc]
                y_ref[rows, gc] = eE[:, gc] * _dot(Cg, htg.astype(BF16)) + xs[:, gc] * dexp_ref[:, gc]
                for jj in range(GROUP_W // LANE):
                    pc = slice(g * GROUP_W + jj * LANE, g * GROUP_W + (jj + 1) * LANE)
                    Ej = E[:, pc]
                    e2 = jnp.sum(Ej * diag, axis=0, keepdims=True)
                    Mp = cb2 * jnp.exp(jnp.where(causal, Ej - e2, -1e30))
                    xj = x_dt[:, pc]
                    xbd = jnp.concatenate([xj * lo, xj * hi], axis=0).astype(BF16)
                    y_ref[rows, pc] += _dot(Mp.astype(BF16), xbd)
                ht_ref[:, gc] = jnp.exp(total[:, gc]) * htg + _dot_tn(Bg, xw[:, gc].astype(BF16))
            zb = zb_ref[rows, :].astype(F32)
            hh = y_ref[rows, :] * (zb * _sigmoid(zb))
            for g in range(SSD_GROUPS):
                gc = slice(g * GROUP_W, (g + 1) * GROUP_W)
                hg = hh[:, gc]
                r = lax.rsqrt(jnp.mean(hg * hg, axis=-1, keepdims=True) + EPS)
                yb_ref[rows, gc] = (hg * r * nw_ref[:, gc]).astype(BF16)

    full = lambda a: pl.BlockSpec(a.shape, lambda i: (0,) * a.ndim)
    hb = T // HALO_BLK
    return pl.pallas_call(
        body, name="ssd_fwd", grid=(nsteps,),
        in_specs=[pl.BlockSpec((T, D), lambda i: (i, OFF_ZB // D)),
                  pl.BlockSpec((T, XBC_W), lambda i: (i, OFF_XBC // XBC_W)),
                  pl.BlockSpec((HALO_BLK, XBC_W), lambda i: (jnp.maximum(i * hb - 1, 0), OFF_XBC // XBC_W)),
                  pl.BlockSpec((T, DT_W), lambda i: (i, OFF_DT // DT_W)),
                  full(conv_w), full(conv_b), full(dtb_p), full(alog_p), full(d_exp), full(norm_w), full(tri), full(expand),
                  full(shift)],
        out_specs=[pl.BlockSpec((T, D), lambda i: (i, 0)), pl.BlockSpec((T, D), lambda i: (i, 0)),
                   pl.BlockSpec((ncl, STATE, D), lambda i: (i, 0, 0))],
        out_shape=[jax.ShapeDtypeStruct((S, D), F32), jax.ShapeDtypeStruct((S, D), BF16),
                   jax.ShapeDtypeStruct((S // CHUNK, STATE, D), F32)],
        scratch_shapes=[pltpu.VMEM((STATE, D), F32), pltpu.VMEM((HALO_BLK + T, XBC_W), BF16)],
        compiler_params=_cp(("arbitrary",)),
    )(proj, proj, proj, proj, conv_w, conv_b, dtb_p, alog_p, d_exp, norm_w, tri, expand, shift)


def _ssd_bwd(proj, dyb, y, states, conv_w, conv_b, dtb_p, alog_p, d_exp, norm_w, tri, triT, expand, expandT, shift):
    S = proj.shape[0]
    T = SSD_T
    nsteps = S // T
    ncl = T // CHUNK
    SSD_W = SSD_PAD_W

    def body(zb_ref, xbc_ref, halo_ref, dt_ref, dyb_ref, y_ref, st_ref, cw_ref, cb_ref, dtb_ref, alog_ref, dexp_ref, nw_ref,
             tri_ref, triT_ref, exp_ref, expT_ref, shift_ref,
             dp_ref, dcw_ref, dcb_ref, ddtb_ref, dalog_ref, dD_ref, dnw_ref,
             dht_ref, ext_ref, dpre_ref, dy_s, dE_s, dxdt_s, dxc_s, dDacc_ref, dAacc_ref):
        i = pl.program_id(0)

        @pl.when(i == 0)
        def _():
            for r in (dht_ref, dcw_ref, dcb_ref, ddtb_ref, dnw_ref, dDacc_ref, dAacc_ref):
                r[...] = jnp.zeros_like(r)
            dpre_ref[T:T + HALO_BLK, :] = jnp.zeros((HALO_BLK, XBC_W), F32)

        @pl.when(i == nsteps - 1)
        def _():
            ext_ref[0:HALO_BLK, :] = jnp.zeros((HALO_BLK, XBC_W), BF16)

        @pl.when(i < nsteps - 1)
        def _():
            ext_ref[0:HALO_BLK, :] = halo_ref[...]

        ext_ref[HALO_BLK:HALO_BLK + T, :] = xbc_ref[...]
        diag, causal, lo, hi = _pair_masks()
        last_row = (lax.broadcasted_iota(jnp.int32, (CHUNK, 1), 0) == CHUNK - 1).astype(F32)
        for c in reversed(range(ncl)):
            q = _ssd_chunk_fwd(c, ext_ref, shift_ref, dt_ref, cw_ref, cb_ref, dtb_ref, alog_ref, tri_ref, exp_ref)
            rows = pl.ds(c * CHUNK, CHUNK)
            pre, sg, xc, dtr, dtv, A, E, dtE = (q[k] for k in ("pre", "sg", "xc", "dtr", "dtv", "A", "E", "dtE"))
            xs = xc[:, 0:D]
            total = E[CHUNK - 1:CHUNK, :]
            x_dt = xs * dtE
            eE = jnp.exp(E)
            wdec = jnp.exp(total - E)
            zb = zb_ref[rows, :].astype(F32)
            yv = y_ref[rows, :]
            sgz = _sigmoid(zb)
            sz = zb * sgz
            hh = yv * sz
            for g in range(SSD_GROUPS):
                gc = slice(g * GROUP_W, (g + 1) * GROUP_W)
                hg = hh[:, gc]
                r = lax.rsqrt(jnp.mean(hg * hg, axis=-1, keepdims=True) + EPS)
                dyb_g = dyb_ref[rows, gc].astype(F32)
                dn = dyb_g * nw_ref[:, gc]
                dnw_ref[0:1, gc] += jnp.sum(dyb_g * hg * r, axis=0, keepdims=True)
                dy_s[:, gc] = r * dn - hg * (r * r * r) * jnp.mean(dn * hg, axis=-1, keepdims=True)
            dhh = dy_s[...]
            dp_ref[rows, 0:D] = (dhh * yv * (sgz * (1.0 + zb * (1.0 - sgz)))).astype(BF16)
            dy = dhh * sz
            dy_s[...] = dy
            dDacc_ref[0:1, :] += jnp.sum(dy * xs, axis=0, keepdims=True)
            dxc_s[:, 0:D] = dy * dexp_ref[...]
            for g in range(SSD_GROUPS):
                gc = slice(g * GROUP_W, (g + 1) * GROUP_W)
                bcol = slice(D + g * STATE, D + (g + 1) * STATE)
                ccol = slice(D + SSD_GROUPS * STATE + g * STATE, D + SSD_GROUPS * STATE + (g + 1) * STATE)
                Bg = xc[:, bcol].astype(BF16)
                Cg = xc[:, ccol].astype(BF16)
                B2 = jnp.concatenate([Bg, Bg], axis=0)
                cb2 = _dot_nt(Cg, B2)
                htg = st_ref[c, :, gc]
                htb = htg.astype(BF16)
                dhn = dht_ref[:, gc]
                dhnb = dhn.astype(BF16)
                dyg = dy[:, gc]
                eEg = eE[:, gc]
                wg = wdec[:, gc]
                xdg = x_dt[:, gc]
                CH = _dot(Cg, htb)
                dCHb = (dyg * eEg).astype(BF16)
                dC = _dot_nt(dCHb, htb)
                dl = jnp.exp(total[:, gc])
                dht_prev = _dot_tn(Cg, dCHb) + dl * dhn
                dtot = jnp.sum(dhn * htg, axis=0, keepdims=True) * dl
                dxw = _dot(Bg, dhnb)
                dB = _dot_nt((xdg * wg).astype(BF16), dhnb)
                dwd = dxw * xdg * wg
                dtot = dtot + jnp.sum(dwd, axis=0, keepdims=True)
                dE_s[:, gc] = dyg * eEg * CH - dwd + last_row * dtot
                dxdt_s[:, gc] = dxw * wg
                dcb2 = jnp.zeros((CHUNK, LANE), F32)
                for jj in range(GROUP_W // LANE):
                    pc = slice(g * GROUP_W + jj * LANE, g * GROUP_W + (jj + 1) * LANE)
                    Ej = E[:, pc]
                    e2 = jnp.sum(Ej * diag, axis=0, keepdims=True)
                    Lp = jnp.exp(jnp.where(causal, Ej - e2, -1e30))
                    Mp = cb2 * Lp
                    xj = x_dt[:, pc]
                    xbd = jnp.concatenate([xj * lo, xj * hi], axis=0).astype(BF16)
                    dyj = dy[:, pc].astype(BF16)
                    dMp = _dot_nt(dyj, xbd)
                    dxbd = _dot_tn(Mp.astype(BF16), dyj)
                    dxdt_s[:, pc] += dxbd[0:CHUNK, :] * lo + dxbd[CHUNK:2 * CHUNK, :] * hi
                    dcb2 = dcb2 + dMp * Lp
                    dseg = dMp * Mp
                    dE_s[:, pc] += dseg - diag * jnp.sum(dseg, axis=0, keepdims=True)
                dcb2b = dcb2.astype(BF16)
                dC = dC + _dot(dcb2b, B2)
                dB2 = _dot_tn(dcb2b, Cg)
                dB = dB + dB2[0:CHUNK, :] + dB2[CHUNK:2 * CHUNK, :]
                dxc_s[:, bcol] = dB
                dxc_s[:, ccol] = dC
                dht_ref[:, gc] = dht_prev
            dx_dt = dxdt_s[...]
            dxc_s[:, 0:D] += dx_dt * dtE
            red = _sel_right(jnp.concatenate([dE_s[...], dx_dt * xs], axis=0), expT_ref[...])
            da = _sel_left(triT_ref[...], red[0:CHUNK, :])
            ddtv = red[CHUNK:2 * CHUNK, :] + da * A
            dAacc_ref[0:1, :] += jnp.sum(da * dtv, axis=0, keepdims=True)
            ddtr = ddtv * _sigmoid(dtr)
            ddtb_ref[0:1, :] += jnp.sum(ddtr, axis=0, keepdims=True)
            dp_ref[rows, D + XBC_W:D + XBC_W + DT_W] = ddtr.astype(BF16)
            dpre = dxc_s[...] * (sg * (1.0 + pre * (1.0 - sg)))
            dpre_ref[rows, :] = dpre
            dcb_ref[0:1, :] += jnp.sum(dpre, axis=0, keepdims=True)
            for k in range(CONV_K):
                dcw_ref[k:k + 1, :] += jnp.sum(dpre * q["taps"][k], axis=0, keepdims=True)
        dxbc = jnp.zeros((T, XBC_W), F32)
        for k in range(CONV_K):
            dxbc = dxbc + cw_ref[k:k + 1, :] * dpre_ref[pl.ds(CONV_K - 1 - k, T), :]
        dp_ref[:, D:D + XBC_W] = dxbc.astype(BF16)
        dp_ref[:, SEG_SSD[1]:SSD_W] = jnp.zeros((T, SSD_W - SEG_SSD[1]), BF16)
        dpre_ref[T:T + HALO, :] = dpre_ref[0:HALO, :]

        @pl.when(i == nsteps - 1)
        def _():
            dalog_ref[...] = dAacc_ref[...] * (-jnp.exp(alog_ref[...]))
            dD_ref[...] = _dot(dDacc_ref[...], expT_ref[...].astype(F32), precision=HI)

    full = lambda a: pl.BlockSpec(a.shape, lambda i: (0,) * a.ndim)
    hb = T // HALO_BLK
    rev = lambda i: nsteps - 1 - i
    acc = lambda w: pl.BlockSpec((8, w), lambda i: (0, 0))
    return pl.pallas_call(
        body, name="ssd_bwd", grid=(nsteps,),
        in_specs=[pl.BlockSpec((T, D), lambda i: (rev(i), OFF_ZB // D)),
                  pl.BlockSpec((T, XBC_W), lambda i: (rev(i), OFF_XBC // XBC_W)),
                  pl.BlockSpec((HALO_BLK, XBC_W), lambda i: (jnp.maximum(rev(i) * hb - 1, 0), OFF_XBC // XBC_W)),
                  pl.BlockSpec((T, DT_W), lambda i: (rev(i), OFF_DT // DT_W)),
                  pl.BlockSpec((T, D), lambda i: (rev(i), 0)), pl.BlockSpec((T, D), lambda i: (rev(i), 0)),
                  pl.BlockSpec((ncl, STATE, D), lambda i: (rev(i), 0, 0)),
                  full(conv_w), full(conv_b), full(dtb_p), full(alog_p), full(d_exp), full(norm_w),
                  full(tri), full(triT), full(expand), full(expandT), full(shift)],
        out_specs=[pl.BlockSpec((T, SSD_W), lambda i: (rev(i), 0)),
                   acc(XBC_W), acc(XBC_W), acc(DT_W), acc(DT_W), acc(DT_W), acc(D)],
        out_shape=[jax.ShapeDtypeStruct((S, SSD_W), BF16),
                   jax.ShapeDtypeStruct((8, XBC_W), F32), jax.ShapeDtypeStruct((8, XBC_W), F32),
                   jax.ShapeDtypeStruct((8, DT_W), F32), jax.ShapeDtypeStruct((8, DT_W), F32),
                   jax.ShapeDtypeStruct((8, DT_W), F32), jax.ShapeDtypeStruct((8, D), F32)],
        scratch_shapes=[pltpu.VMEM((STATE, D), F32), pltpu.VMEM((HALO_BLK + T, XBC_W), BF16), pltpu.VMEM((T + HALO_BLK, XBC_W), F32),
                        pltpu.VMEM((CHUNK, D), F32), pltpu.VMEM((CHUNK, D), F32), pltpu.VMEM((CHUNK, D), F32),
                        pltpu.VMEM((CHUNK, XBC_W), F32), pltpu.VMEM((8, D), F32), pltpu.VMEM((8, DT_W), F32)],
        compiler_params=_cp(("arbitrary",)),
    )(proj, proj, proj, proj, dyb, y, states, conv_w, conv_b, dtb_p, alog_p, d_exp, norm_w, tri, triT, expand, expandT, shift)


def _head(x, ya, yb, proj, target, gate_b, wout, fw, *, tm):
    S = x.shape[0]

    def body(x_ref, ya_ref, yb_ref, gl0_ref, gl1_ref, t_ref, gb_ref, w_ref, fw_ref,
             dh_ref, dhb_ref, mb_ref, dya_ref, dyb_ref, dgl_ref, loss_ref, dfw_ref, dgb_ref):
        @pl.when(pl.program_id(0) == 0)
        def _():
            loss_ref[...] = jnp.zeros_like(loss_ref)
            dfw_ref[...] = jnp.zeros_like(dfw_ref)
            dgb_ref[...] = jnp.zeros_like(dgb_ref)

        ya_v = ya_ref[...].astype(F32)
        yb_v = yb_ref[...].astype(F32)
        g0 = _sigmoid(gl0_ref[...].astype(F32) + gb_ref[:, 0:D])
        g1 = _sigmoid(gl1_ref[...].astype(F32) + gb_ref[:, D:2 * D])
        mb = (g0 * ya_v + g1 * yb_v).astype(BF16)
        mb_ref[...] = mb
        h = x_ref[...] + _dot(mb, w_ref[...])
        r = lax.rsqrt(jnp.mean(h * h, axis=-1, keepdims=True) + EPS)
        hn = h * r
        err = hn * fw_ref[...] - t_ref[...]
        loss_ref[...] += 0.5 * jnp.sum(jnp.mean(err * err, axis=-1, keepdims=True))
        dyf = err * (1.0 / D)
        dfw_ref[0:1, :] += jnp.sum(dyf * hn, axis=0, keepdims=True)
        dhn = dyf * fw_ref[...]
        dh = r * (dhn - hn * jnp.mean(dhn * hn, axis=-1, keepdims=True))
        dh_ref[...] = dh
        dhb = dh.astype(BF16)
        dhb_ref[...] = dhb
        dm = _dot_nt(dhb, w_ref[...])
        dya_ref[...] = (dm * g0).astype(BF16)
        dyb_ref[...] = (dm * g1).astype(BF16)
        dgl0 = dm * ya_v * g0 * (1.0 - g0)
        dgl1 = dm * yb_v * g1 * (1.0 - g1)
        dgl_ref[:, 0:D] = dgl0.astype(BF16)
        dgl_ref[:, D:2 * D] = dgl1.astype(BF16)
        dgb_ref[0:1, 0:D] += jnp.sum(dgl0, axis=0, keepdims=True)
        dgb_ref[0:1, D:2 * D] += jnp.sum(dgl1, axis=0, keepdims=True)

    row = pl.BlockSpec((tm, D), lambda i: (i, 0))
    seg = lambda off: pl.BlockSpec((tm, D), lambda i: (i, off // D))
    full = lambda a: pl.BlockSpec(a.shape, lambda i: (0,) * a.ndim)
    acc = lambda w: pl.BlockSpec((8, w), lambda i: (0, 0))
    return pl.pallas_call(
        body, name="head", grid=(S // tm,),
        in_specs=[row, row, row, seg(OFF_G0), seg(OFF_G1), row, full(gate_b), full(wout), full(fw)],
        out_specs=[row, row, row, row, row, pl.BlockSpec((tm, 2 * D), lambda i: (i, 0)), acc(LANE), acc(D), acc(2 * D)],
        out_shape=[jax.ShapeDtypeStruct((S, D), F32), jax.ShapeDtypeStruct((S, D), BF16), jax.ShapeDtypeStruct((S, D), BF16),
                   jax.ShapeDtypeStruct((S, D), BF16), jax.ShapeDtypeStruct((S, D), BF16), jax.ShapeDtypeStruct((S, 2 * D), BF16),
                   jax.ShapeDtypeStruct((8, LANE), F32), jax.ShapeDtypeStruct((8, D), F32), jax.ShapeDtypeStruct((8, 2 * D), F32)],
        compiler_params=_cp(("arbitrary",)),
    )(x, ya, yb, proj, proj, target, gate_b, wout, fw)


def _adam_update(g, w_ref, m_ref, v_ref, g_ref, d_ref, m2_ref, v2_ref):
    m2 = ADAM_B1 * m_ref[...] + (1.0 - ADAM_B1) * g
    v2 = ADAM_B2 * v_ref[...] + (1.0 - ADAM_B2) * (g * g)
    m_hat = m2 / (1.0 - ADAM_B1 ** ADAM_STEP)
    v_hat = v2 / (1.0 - ADAM_B2 ** ADAM_STEP)
    g_ref[...] = g
    d_ref[...] = -ADAM_LR * (m_hat / (jnp.sqrt(v_hat) + ADAM_EPS) + ADAM_WD * w_ref[...])
    m2_ref[...] = m2
    v2_ref[...] = v2


def _adamw_own(me, own, landed, w, m, v, *, tr, tc, name):
    _, R, C = landed.shape
    assert R % tr == 0 and C % tc == 0, (name, R, C, tr, tc)

    def body(me_ref, own_ref, p_ref, w_ref, m_ref, v_ref, g_ref, d_ref, m2_ref, v2_ref):
        mine = own_ref[0].astype(F32)
        g = jnp.where(me_ref[0] == 0, mine, p_ref[0].astype(F32))
        for k in range(1, N_DEV):
            g = g + jnp.where(me_ref[0] == k, mine, p_ref[k].astype(F32))
        _adam_update(g, w_ref, m_ref, v_ref, g_ref, d_ref, m2_ref, v2_ref)

    tile = pl.BlockSpec((tr, tc), lambda i, j, me_ref: (i, j))
    return pl.pallas_call(
        body, name=name,
        grid_spec=pltpu.PrefetchScalarGridSpec(
            num_scalar_prefetch=1, grid=(R // tr, C // tc),
            in_specs=[pl.BlockSpec((1, tr, tc), lambda i, j, me_ref: (me_ref[0], i, j)),
                      pl.BlockSpec((N_DEV, tr, tc), lambda i, j, me_ref: (0, i, j)), tile, tile, tile],
            out_specs=[tile, tile, tile, tile]),
        out_shape=[jax.ShapeDtypeStruct((R, C), F32)] * 4,
        compiler_params=_cp(("parallel", "parallel")),
    )(me, own, landed, w, m, v)


def _adamw(parts, w, m, v, *, tr, name):
    _, R, C = parts.shape
    assert R % tr == 0, (name, R, tr)

    def body(p_ref, w_ref, m_ref, v_ref, g_ref, d_ref, m2_ref, v2_ref):
        g = p_ref[0].astype(F32)
        for k in range(1, N_DEV):
            g = g + p_ref[k].astype(F32)
        _adam_update(g, w_ref, m_ref, v_ref, g_ref, d_ref, m2_ref, v2_ref)

    row = pl.BlockSpec((tr, C), lambda i: (i, 0))
    return pl.pallas_call(
        body, name=name, grid=(R // tr,),
        in_specs=[pl.BlockSpec((N_DEV, tr, C), lambda i: (0, i, 0)), row, row, row],
        out_specs=[row, row, row, row],
        out_shape=[jax.ShapeDtypeStruct((R, C), F32)] * 4,
        compiler_params=_cp(("parallel",)),
    )(parts, w, m, v)


def _place():
    x, y, c = lax.axis_index("x"), lax.axis_index("y"), lax.axis_index("c")
    return x, y, c


def _all_gather(arrs, *, name):
    n = len(arrs)

    def body(*refs):
        ins, outs = refs[:n], refs[n:2 * n]
        send_sems, recv_sems, local_sems = refs[2 * n:]
        x, y, c = _place()
        me, sibling = (x, y, c), (x, y, 1 - c)
        chips = [(1 - x, y), (x, 1 - y), (1 - x, 1 - y)]

        def idx(px, py, pc):
            return 4 * px + 2 * py + pc

        def copy(k, a, block, to, src=None):
            slab = outs[a].at[idx(*block)]
            return pltpu.make_async_remote_copy(
                src_ref=slab if src is None else src, dst_ref=slab,
                send_sem=send_sems.at[k, a], recv_sem=recv_sems.at[k, a], device_id=to, device_id_type=MESH)

        mine = [pltpu.make_async_copy(ins[a], outs[a].at[idx(*me)], local_sems.at[a]) for a in range(n)]
        for cp in mine:
            cp.start()
        first = []
        for a in range(n):
            first.append(copy(0, a, me, sibling, src=ins[a]))
            first += [copy(1 + j, a, me, (*chip, c), src=ins[a]) for j, chip in enumerate(chips)]
        for cp in first:
            cp.start()
        passed = []
        for j, chip in enumerate(chips):
            for a in range(n):
                copy(1 + j, a, (*chip, c), me).wait_recv()
                fwd = copy(4 + j, a, (*chip, c), sibling)
                fwd.start()
                passed.append(fwd)
        for a in range(n):
            copy(0, a, sibling, me).wait_recv()
            for j, chip in enumerate(chips):
                copy(4 + j, a, (*chip, 1 - c), me).wait_recv()
        for cp in first + passed:
            cp.wait_send()
        for cp in mine:
            cp.wait()

    anyspec = pl.BlockSpec(memory_space=pl.ANY)
    return pl.pallas_call(
        body, name=name,
        in_specs=[anyspec] * n, out_specs=[anyspec] * n,
        out_shape=[jax.ShapeDtypeStruct((N_DEV,) + a.shape, a.dtype) for a in arrs],
        scratch_shapes=[pltpu.SemaphoreType.DMA((7, n)), pltpu.SemaphoreType.DMA((7, n)), pltpu.SemaphoreType.DMA((n,))],
    )(*arrs)


W_ROWS = SEG_SSD[0] + SSD_PAD_W


GROUP = 16
INTERIOR = 1920


def _interior(k):
    lo = -(-(k * SHARD_IN) // GROUP) * GROUP
    hi = ((k + 1) * SHARD_IN) // GROUP * GROUP
    return lo, hi


def _dest_row(r):
    if r < REF_SGU_END:
        return r
    return r - REF_SGU_END + SEG_SSD[0] if r < REF_GATE_START else r - REF_GATE_START + SEG_GATE[0]


def _shard_pieces(k):
    lo_k, hi_k = _interior(k)
    out = []
    for lo, hi in ((0, REF_SGU_END), (REF_SGU_END, REF_GATE_START), (REF_GATE_START, W_IN)):
        a, b = max(lo, lo_k), min(hi, hi_k)
        if a < b:
            out.append((a - lo_k, b - a, _dest_row(a)))
    return out


GATHER_PARTS = 2


def _shard_parts(k):
    parts = [[] for _ in range(GATHER_PARTS)]
    for s0, n, d0 in _shard_pieces(k):
        step = -(-(n // GROUP) // GATHER_PARTS) * GROUP
        for p in range(GATHER_PARTS):
            a, b = min(p * step, n), min((p + 1) * step, n)
            if a < b:
                parts[p].append((s0 + a, b - a, d0 + a))
    return parts


def _patch_straddlers(wpT, heads, tails):
    for k in range(1, N_DEV):
        m = (k * SHARD_IN) % GROUP
        if m:
            group = jnp.concatenate([tails[k - 1, GROUP - m:], heads[k, :GROUP - m]], axis=0)
            wpT = lax.dynamic_update_slice(wpT, group, (_dest_row(k * SHARD_IN - m), 0))
    return wpT


def _gather_stages(k, win_ref, small, z_ref, n_zero, w_ref, send_sems, recv_sems, local_sems):
    x, y, c = k // 4, (k // 2) % 2, k % 2
    idx = lambda p: 4 * p[0] + 2 * p[1] + p[2]
    me, sib = (x, y, c), (x, y, 1 - c)
    xn, yn, dg = (1 - x, y, c), (x, 1 - y, c), (1 - x, 1 - y, c)
    parts = range(GATHER_PARTS)

    def copies(slot, block, to, part, own=False):
        kb = idx(block)
        out = []
        for j, (s0, n, d0) in enumerate(_shard_parts(kb)[part]):
            dst = w_ref.at[pl.ds(d0, n)]
            out.append((win_ref.at[pl.ds(s0, n)] if own else dst, dst, 2 * part + j))
        if part == 0:
            for j, (src, gathered) in enumerate(small):
                out.append((src if own else gathered.at[kb], gathered.at[kb], 2 * GATHER_PARTS + j))
        return [pltpu.make_async_remote_copy(src_ref=s, dst_ref=d, send_sem=send_sems.at[slot, j], recv_sem=recv_sems.at[slot, j],
                                             device_id=to, device_id_type=MESH) for s, d, j in out]

    def start(cps):
        for cp in cps:
            cp.start()

    def arrived(slot, block, part):
        for cp in copies(slot, block, me, part):
            cp.wait_recv()

    def local():
        pairs = [(win_ref.at[pl.ds(s0, n)], w_ref.at[pl.ds(d0, n)]) for s0, n, d0 in _shard_pieces(k)]
        pairs += [(src, gathered.at[k]) for src, gathered in small] + [(z_ref, w_ref.at[pl.ds(W_IN, n_zero)])]
        return [pltpu.make_async_copy(s, d, local_sems.at[j]) for j, (s, d) in enumerate(pairs)]

    relay = (xn, yn) if c == 1 else (yn, xn)

    def first():
        start(local())
        for p in parts:
            start(copies(0, me, sib, p, own=True) + copies(1, me, xn, p, own=True) + copies(2, me, yn, p, own=True))

    def hand_on():
        for p in parts:
            arrived(1, xn, p)
            start(copies(4, xn, sib, p))
            if c == 1:
                start(copies(3, *relay, p))
            arrived(2, yn, p)
            start(copies(5, yn, sib, p))
            if c == 0:
                start(copies(3, *relay, p))

    def finish():
        for p in parts:
            arrived(3, dg, p)
            start(copies(6, dg, sib, p))
        for p in parts:
            arrived(0, sib, p)
            arrived(4, (1 - x, y, 1 - c), p)
            arrived(5, (x, 1 - y, 1 - c), p)
            arrived(6, (1 - x, 1 - y, 1 - c), p)
        for p in parts:
            sent = (copies(0, me, sib, p, own=True) + copies(1, me, xn, p, own=True) + copies(2, me, yn, p, own=True)
                    + copies(3, *relay, p) + copies(4, xn, sib, p) + copies(5, yn, sib, p) + copies(6, dg, sib, p))
            for cp in sent:
                cp.wait_send()
        for cp in local():
            cp.wait()

    return first, hand_on, finish


def _gather_sems(n_small):
    n_arr = 2 * GATHER_PARTS + n_small
    return [pltpu.SemaphoreType.DMA((7, n_arr)), pltpu.SemaphoreType.DMA((7, n_arr)), pltpu.SemaphoreType.DMA((n_arr + 1,))]


def _gather_weights(win, head, tail, wout, cw, zeros):
    small_in = (wout, cw, head, tail)
    n_zero = zeros.shape[0]
    assert W_IN + n_zero == W_ROWS and W_IN % GROUP == 0

    def body(win_ref, wout_ref, cw_ref, head_ref, tail_ref, z_ref, w_ref, gout_ref, gcw_ref, ghead_ref, gtail_ref, *sems):
        x, y, c = _place()
        me = 4 * x + 2 * y + c
        small = ((wout_ref, gout_ref), (cw_ref, gcw_ref), (head_ref, ghead_ref), (tail_ref, gtail_ref))

        def run(k):
            for stage in _gather_stages(k, win_ref, small, z_ref, n_zero, w_ref, *sems):
                stage()

        for k in range(N_DEV):
            pl.when(me == k)(functools.partial(run, k))

    anyspec = pl.BlockSpec(memory_space=pl.ANY)
    return pl.pallas_call(
        body, name="gather_weights", in_specs=[anyspec] * 6, out_specs=[anyspec] * 5,
        out_shape=[jax.ShapeDtypeStruct((W_ROWS, D), win.dtype)]
        + [jax.ShapeDtypeStruct((N_DEV,) + a.shape, a.dtype) for a in small_in],
        scratch_shapes=_gather_sems(len(small_in)),
    )(win, wout, cw, head, tail, zeros)


_REL = [(dx, dy, dc) for dx in (0, 1) for dy in (0, 1) for dc in (0, 1)][1:]
_HBM = pl.BlockSpec(memory_space=pltpu.HBM)
_SEM = pl.BlockSpec(memory_space=pltpu.SEMAPHORE)
_EFFECT = pltpu.SideEffectType.DATAFLOW_SIDE_EFFECTING


def _peer(k):
    x, y, c = _place()
    dx, dy, dc = _REL[k]
    return (1 - x if dx else x, 1 - y if dy else y, 1 - c if dc else c)


def _exchange_start(parts, *, name):
    n = len(parts)

    def body(*refs):
        ins, lands = refs[:n], refs[n:2 * n]
        send_sems, recv_sems, token = refs[2 * n], refs[2 * n + 1], refs[-1]
        x, y, c = _place()
        me = 4 * x + 2 * y + c
        for a in range(n):
            for k in range(len(_REL)):
                px, py, pc = _peer(k)
                pltpu.make_async_remote_copy(
                    src_ref=ins[a].at[4 * px + 2 * py + pc], dst_ref=lands[a].at[me],
                    send_sem=send_sems.at[len(_REL) * a + k], recv_sem=recv_sems.at[len(_REL) * a + k],
                    device_id=(px, py, pc), device_id_type=MESH).start()
        token[...] = jnp.zeros_like(token)

    sem = pltpu.SemaphoreType.DMA((len(_REL) * n,))
    bufs = [pltpu.HBM(p.shape, p.dtype) for p in parts]
    outs = pl.pallas_call(
        body, name=name,
        out_shape=(sem, sem, *bufs, *bufs, jax.ShapeDtypeStruct((8, LANE), F32)),
        in_specs=(_HBM,) * (2 * n), out_specs=(_SEM, _SEM, *(_HBM,) * (2 * n), pl.BlockSpec(memory_space=pltpu.VMEM)),
        input_output_aliases={i: 2 + i for i in range(2 * n)},
        compiler_params=pltpu.CompilerParams(has_side_effects=_EFFECT),
    )(*[pltpu.with_memory_space_constraint(p, pltpu.HBM) for p in parts],
      *[pltpu.with_memory_space_constraint(lax.empty(p.shape, p.dtype), pltpu.HBM) for p in parts])
    return outs[0], outs[1], outs[2:2 + n], outs[2 + n:2 + 2 * n], outs[-1]


def _exchange_wait(send_sems, recv_sems, parts, lands, after, *, name):
    n = len(parts)

    def body(*refs):
        ins, lands_ = refs[:n], refs[n:2 * n]
        ssem, rsem = refs[2 * n], refs[2 * n + 1]
        for a in range(n):
            for k in range(len(_REL)):
                px, py, pc = _peer(k)
                p = 4 * px + 2 * py + pc
                cp = pltpu.make_async_remote_copy(
                    src_ref=ins[a].at[p], dst_ref=lands_[a].at[p],
                    send_sem=ssem.at[len(_REL) * a + k], recv_sem=rsem.at[len(_REL) * a + k],
                    device_id=(px, py, pc), device_id_type=MESH)
                cp.wait_send()
                cp.wait_recv()

    bufs = [pltpu.HBM(p.shape, p.dtype) for p in parts]
    outs = pl.pallas_call(
        body, name=name, out_shape=(*bufs, *bufs),
        in_specs=(*(_HBM,) * (2 * n), _SEM, _SEM, pl.BlockSpec(memory_space=pl.ANY)), out_specs=(_HBM,) * (2 * n),
        input_output_aliases={i: i for i in range(2 * n)},
        compiler_params=pltpu.CompilerParams(has_side_effects=_EFFECT),
    )(*parts, *lands, send_sems, recv_sems, after)
    return outs[:n], outs[n:]


WEIGHTS = ('norm_w', 'w_in', 'gate_b', 'sgu_norm_g', 'sgu_norm_b', 'sgu_w', 'sgu_b', 'conv_w', 'conv_b', 'dt_bias', 'A_log',
           'D_skip', 'ssd_norm_w', 'w_out', 'final_norm_w')
SHARDED = ('w_in', 'conv_w', 'w_out')
PACK_ROW = 8 * LANE


def _constants():
    tri = np.tril(np.ones((CHUNK, CHUNK), np.float32))
    expand = np.zeros((DT_W, D), np.float32)
    for h in range(HEADS):
        expand[h, h * HEADDIM:(h + 1) * HEADDIM] = 1.0
    sel = np.zeros((D, LANE), np.float32)
    for g in range(SGU_GROUPS):
        sel[g * LANE:(g + 1) * LANE, g] = 1.0
    pos_chunk = np.arange(SGU_BLOCK) // CHUNK
    mask = (pos_chunk[None, :] <= pos_chunk[:, None]).astype(np.float32)
    shift = np.zeros(((CONV_K - 1) * CHUNK, HALO_BLK + CHUNK), np.float32)
    for kk in range(CONV_K - 1):
        for t in range(CHUNK):
            shift[kk * CHUNK + t, HALO_BLK - (CONV_K - 1) + t + kk] = 1.0
    return dict(tri=jnp.asarray(tri, BF16), triT=jnp.asarray(tri.T.copy(), BF16), expand=jnp.asarray(np.tile(expand, (3, 1)), BF16),
                shift=jnp.asarray(shift, BF16),
                expandT=jnp.asarray(expand.T.copy(), BF16), sel=jnp.asarray(sel), mask=jnp.asarray(mask))


def _to_shards(segs):
    starts = np.cumsum([0] + [n for _, n in segs])
    assert starts[-1] == W_IN
    slabs = []
    for k in range(N_DEV):
        pieces = []
        for (s, n), s0 in zip(segs, starts[:-1]):
            lo, hi = max(k * SHARD_IN, s0), min((k + 1) * SHARD_IN, s0 + n)
            if lo < hi:
                pieces.append(s[lo - s0:hi - s0])
        slabs.append(jnp.concatenate(pieces, axis=0))
    return jnp.stack(slabs)


def _local_step(x2, tgt, wpT, wout, cw, p, exchange_small, exchange):
    S = x2.shape[0]
    k = _constants()
    xn = _norm_fwd(x2, p['norm_w'], tm=min(512, S))
    proj = _matmul(xn, wpT, trans_b=True, out_dtype=BF16, tm=min(1024, S), tn=2048, tk=D, name="in_proj")
    wm32 = p['sgu_w'][0] * k['mask']
    wm = wm32.astype(BF16)
    wmT = jnp.swapaxes(wm32, 1, 2).astype(BF16)
    bias_full = jnp.repeat(p['sgu_b'][0].T, LANE, axis=1)
    tm_sgu = min(256, S)
    ya = _sgu_fwd(proj, p['sgu_norm_g'], p['sgu_norm_b'], wm, bias_full, tm=tm_sgu)
    pad32 = lambda a: jnp.pad(a, ((0, 0), (0, DT_W - HEADS)))
    dtb_p, alog_p = pad32(p['dt_bias']), pad32(p['A_log'])
    d_exp = jnp.repeat(p['D_skip'], HEADDIM, axis=1)
    ssd_args = (cw, p['conv_b'], dtb_p, alog_p, d_exp, p['ssd_norm_w'])
    y, yb, states = _ssd_fwd(proj, *ssd_args, k['tri'], k['expand'], k['shift'])
    dh, dhb, mb, dya, dyb, dgl, loss, dfw, dgb = _head(
        x2, ya, yb, proj, tgt, p['gate_b'], wout, p['final_norm_w'][None, :], tm=min(256, S))
    dsgu, dws, dbsT, dsg, dsb = _sgu_bwd(proj, dya, p['sgu_norm_g'], p['sgu_norm_b'], wm, wmT, bias_full, k['mask'], k['sel'],
                                         tm=tm_sgu)
    dssd, dcw, dcb, ddtb, dalog, dD, dnw = _ssd_bwd(proj, dyb, y, states, *ssd_args, k['tri'], k['triT'], k['expand'], k['expandT'],
                                                    k['shift'])
    grads = dict(
        gate_b=dgb[0:1], sgu_norm_g=dsg[0:1], sgu_norm_b=dsb[0:1], sgu_w=dws[None],
        sgu_b=dbsT[:, :SGU_GROUPS].T[None], conv_w=dcw[0:CONV_K][None], conv_b=dcb[0:1], dt_bias=ddtb[0:1, :HEADS],
        A_log=dalog[0:1, :HEADS], D_skip=dD[0:1, :HEADS], ssd_norm_w=dnw[0:1], final_norm_w=dfw[0])
    token = exchange_small(loss[0, 0], grads)
    tk = min(4096, S)
    tn = 1024
    dwT_sgu = _matmul(dsgu, xn, trans_a=True, out_dtype=BF16, tm=1024, tn=tn, tk=tk, after=token, name="dw_in_sgu")
    dwT_gate = _matmul(dgl, xn, trans_a=True, out_dtype=BF16, tm=1024, tn=tn, tk=tk, name="dw_in_gate")
    dwT_ssd = _matmul(dssd, xn, trans_a=True, out_dtype=BF16, tm=1024, tn=tn, tk=tk, name="dw_in_ssd")
    dw_out = _matmul(mb, dhb, trans_a=True, out_dtype=BF16, tm=1024, tn=tn, tk=tk, name="dw_out")
    token = exchange([(dwT_sgu, SEG_SGU[1]), (dwT_ssd, W_IN - SEG_SSD[0]), (dwT_gate, SEG_GATE[1])], dw_out)
    tm = min(1024, S)
    dxn = _matmul(dsgu, wpT, tm=tm, tn=tn, tk=3072, after=token, name="dxn_sgu")
    dxn = _matmul(dgl, wpT, b_koff=SEG_GATE[0] // 2048, tm=tm, tn=tn, tk=2048, add=dxn, name="dxn_gate")
    dxn = _matmul(dssd, wpT, b_koff=SEG_SSD[0] // 2048, tm=tm, tn=tn, tk=2048, add=dxn, name="dxn_ssd")
    grad_x, dnorm = _norm_bwd(x2, p['norm_w'], dxn, dh, tm=min(256, S))
    return grad_x, dnorm[0:1]


def _pack(arrs):
    rows, offs, r = [], [], 0
    for a in arrs:
        n = a.size
        nr = -(-n // PACK_ROW) * 8
        rows.append(jnp.pad(a.reshape(-1).astype(F32), (0, nr * LANE - n)).reshape(nr, LANE))
        offs.append(r)
        r += nr
    return jnp.concatenate(rows, axis=0), offs


def kernel(x, norm_w, w_in, gate_b, sgu_norm_g, sgu_norm_b, sgu_w, sgu_b, conv_w, conv_b, dt_bias, A_log, D_skip, ssd_norm_w, w_out, final_norm_w, loss_target, m_norm_w, m_w_in, m_gate_b, m_sgu_norm_g, m_sgu_norm_b, m_sgu_w, m_sgu_b, m_conv_w, m_conv_b, m_dt_bias, m_A_log, m_D_skip, m_ssd_norm_w, m_w_out, m_final_norm_w, v_norm_w, v_w_in, v_gate_b, v_sgu_norm_g, v_sgu_norm_b, v_sgu_w, v_sgu_b, v_conv_w, v_conv_b, v_dt_bias, v_A_log, v_D_skip, v_ssd_norm_w, v_w_out, v_final_norm_w):
    w = dict(norm_w=norm_w, w_in=w_in, gate_b=gate_b, sgu_norm_g=sgu_norm_g, sgu_norm_b=sgu_norm_b, sgu_w=sgu_w, sgu_b=sgu_b,
             conv_w=conv_w, conv_b=conv_b, dt_bias=dt_bias, A_log=A_log, D_skip=D_skip, ssd_norm_w=ssd_norm_w, w_out=w_out,
             final_norm_w=final_norm_w)
    m = dict(norm_w=m_norm_w, w_in=m_w_in, gate_b=m_gate_b, sgu_norm_g=m_sgu_norm_g, sgu_norm_b=m_sgu_norm_b, sgu_w=m_sgu_w,
             sgu_b=m_sgu_b, conv_w=m_conv_w, conv_b=m_conv_b, dt_bias=m_dt_bias, A_log=m_A_log, D_skip=m_D_skip,
             ssd_norm_w=m_ssd_norm_w, w_out=m_w_out, final_norm_w=m_final_norm_w)
    v = dict(norm_w=v_norm_w, w_in=v_w_in, gate_b=v_gate_b, sgu_norm_g=v_sgu_norm_g, sgu_norm_b=v_sgu_norm_b, sgu_w=v_sgu_w,
             sgu_b=v_sgu_b, conv_w=v_conv_w, conv_b=v_conv_b, dt_bias=v_dt_bias, A_log=v_A_log, D_skip=v_D_skip,
             ssd_norm_w=v_ssd_norm_w, w_out=v_w_out, final_norm_w=v_final_norm_w)
    me = 4 * lax.axis_index("x") + 2 * lax.axis_index("y") + lax.axis_index("c")
    shard_cw = XBC_W // N_DEV

    tpose = lambda a: jnp.swapaxes(a[0], 0, 1)
    wT = tpose(w_in).astype(BF16)
    first_group = (GROUP - (me * SHARD_IN) % GROUP) % GROUP
    window = lax.dynamic_slice(jnp.pad(wT, ((0, GROUP), (0, 0))), (first_group, 0), (INTERIOR, D))
    wpT, g_out, g_cw, heads, tails = _gather_weights(window, wT[:GROUP], wT[SHARD_IN - GROUP:], w_out[0].astype(BF16),
                                                     conv_w[0], jnp.zeros((W_ROWS - W_IN, D), BF16))
    wpT = _patch_straddlers(wpT, heads, tails)
    wout_full = g_out.reshape(D, D)
    cw_full = jnp.swapaxes(g_cw, 0, 1).reshape(CONV_K, XBC_W)

    flight = {}

    small = [n for n in WEIGHTS if n not in SHARDED and n != 'norm_w']
    early = {}

    def exchange_small(loss_part, grads):
        early['packed'], early['offs'] = _pack([grads[n] for n in small] + [loss_part, grads['conv_w']])
        parts = [jnp.broadcast_to(early['packed'][None], (N_DEV,) + early['packed'].shape)]
        early['sems'], early['rsems'], early['parts'], early['lands'], token = _exchange_start(parts, name="small_start")
        return token

    def exchange(dw_inT_segs, dw_out):
        parts = [_to_shards(dw_inT_segs), dw_out.reshape(N_DEV, D // N_DEV, D)]
        flight['sems'], flight['rsems'], flight['parts'], flight['lands'], token = _exchange_start(parts, name="exchange_start")
        return token

    grad_x, dnorm = _local_step(x[0], loss_target[0], wpT, wout_full, cw_full, w, exchange_small, exchange)
    _, (land_small,) = _exchange_wait(early['sems'], early['rsems'], early['parts'], early['lands'], grad_x, name="small_wait")
    (own_in, own_out), (land_in, land_out) = _exchange_wait(
        flight['sems'], flight['rsems'], flight['parts'], flight['lands'], grad_x, name="exchange_wait")
    me_arr = jnp.reshape(me, (1,)).astype(jnp.int32)
    res = {}
    res['w_in'] = [jnp.swapaxes(o, 0, 1) for o in _adamw_own(
        me_arr, own_in, land_in, tpose(w_in), tpose(m_w_in), tpose(v_w_in), tr=SHARD_IN, tc=256, name="adamw_w_in")]
    res['w_out'] = _adamw_own(me_arr, own_out, land_out, w_out[0], m_w_out[0], v_w_out[0], tr=128, tc=D, name="adamw_w_out")

    (norm_parts,) = _all_gather([_pack([dnorm])[0]], name="gather_norm")
    norm_outs = _adamw(norm_parts, *[_pack([d['norm_w']])[0] for d in (w, m, v)], tr=norm_parts.shape[1], name="adamw_norm")
    res['norm_w'] = [o.reshape(-1)[:D].reshape(w['norm_w'].shape) for o in norm_outs]

    offs = early['offs']
    gathered = lax.dynamic_update_slice(land_small, early['packed'][None], (me, 0, 0))
    off_loss, off_cw = offs[-2], offs[-1]
    cw_parts = gathered[:, off_cw:, :].reshape(N_DEV, CONV_K, XBC_W)
    cw_parts = lax.dynamic_slice_in_dim(cw_parts, me * shard_cw, shard_cw, axis=2)
    cw_rows = _pack([cw_parts[0]])[0].shape[0]
    cw_parts = jnp.pad(cw_parts.reshape(N_DEV, -1), ((0, 0), (0, cw_rows * LANE - CONV_K * shard_cw))).reshape(N_DEV, cw_rows, LANE)
    parts = jnp.concatenate([gathered[:, :off_cw, :], cw_parts], axis=1)
    zero = jnp.zeros((), F32)
    packs = [_pack([d[n] for n in small] + [zero, d['conv_w']])[0] for d in (w, m, v)]
    outs = _adamw(parts, *packs, tr=parts.shape[1], name="adamw_small")

    def unpack(o, name):
        if name == 'conv_w':
            return o[off_cw:off_cw + cw_rows].reshape(-1)[:CONV_K * shard_cw].reshape(w['conv_w'].shape)
        r0 = offs[small.index(name)]
        n = w[name].size
        return o[r0:r0 + -(-n // PACK_ROW) * 8].reshape(-1)[:n].reshape(w[name].shape)

    for n in small + ['conv_w']:
        res[n] = [unpack(o, n) for o in outs]
    for n in ('w_in', 'w_out'):
        res[n] = [o[None] for o in res[n]]
    loss = outs[0][off_loss, 0]
    return (loss, grad_x[None], *[res[n][0] for n in WEIGHTS], *[res[n][1] for n in WEIGHTS],
            *[res[n][2] for n in WEIGHTS], *[res[n][3] for n in WEIGHTS])
```

```python
import functools

import numpy as np
import jax
import jax.numpy as jnp
from jax import lax
from jax.experimental import pallas as pl
from jax.experimental.pallas import tpu as pltpu

F32 = jnp.float32
BF16 = jnp.bfloat16
HI = lax.Precision.HIGHEST
MESH = pl.DeviceIdType.MESH

D = 2048
EPS = 1e-5
SGU_BLOCK = 128
SGU_GROUPS = 16
CHUNK = 64
HEADS = 32
HEADDIM = 64
SSD_GROUPS = 4
GROUP_W = D // SSD_GROUPS
STATE = 128
CONV_K = 4
XBC_W = D + 2 * SSD_GROUPS * STATE
W_IN = 15392
N_DEV = 8
SHARD_IN = W_IN // N_DEV
ADAM_LR, ADAM_B1, ADAM_B2, ADAM_EPS, ADAM_WD, ADAM_STEP = 0.001, 0.9, 0.999, 1e-08, 0.01, 10

REF_SGU_END = 3 * D
REF_GATE_START = W_IN - 2 * D
LANE = 128
DT_W = LANE
OFF_U, OFF_V, OFF_ZA, OFF_G0, OFF_G1, OFF_ZB = (i * D for i in range(6))
OFF_XBC = OFF_ZB + D
OFF_DT = OFF_XBC + XBC_W
SEG_SGU = (OFF_U, 3 * D)
SEG_GATE = (OFF_G0, 2 * D)
SEG_SSD = (OFF_ZB, D + XBC_W + DT_W)
WP = SEG_SSD[0] + SEG_SSD[1]
SSD_PAD_W = 3 * D
VMEM_BYTES = 64 * 1024 * 1024
VMEM_LIMIT = VMEM_BYTES - 8 * 1024 * 1024


def _cp(sem=None, vmem=VMEM_LIMIT):
    return pltpu.CompilerParams(dimension_semantics=sem, vmem_limit_bytes=vmem)


def _sigmoid(x):
    return 1.0 / (1.0 + jnp.exp(-x))


def _softplus(x):
    return jnp.maximum(x, 0.0) + jnp.log(1.0 + jnp.exp(-jnp.abs(x)))


def _dot(a, b, precision=None):
    return jnp.dot(a, b, preferred_element_type=F32, precision=precision)


def _dot_nt(a, b, precision=None):
    return lax.dot_general(a, b, (((1,), (1,)), ((), ())), preferred_element_type=F32, precision=precision)


def _dot_tn(a, b, precision=None):
    return lax.dot_general(a, b, (((0,), (0,)), ((), ())), preferred_element_type=F32, precision=precision)


def _split3(a):
    hi = a.astype(BF16)
    r = a - hi.astype(F32)
    mid = r.astype(BF16)
    return hi, mid, (r - mid.astype(F32)).astype(BF16)


def _sel_right(a, sel01):
    m = a.shape[0]
    r = _dot(jnp.concatenate(_split3(a), axis=0), sel01)
    return (r[0:m] + r[m:2 * m]) + r[2 * m:3 * m]


def _sel_right_k(a, sel01_x3):
    return _dot(jnp.concatenate(_split3(a), axis=1), sel01_x3)


def _sel_left(sel01, a):
    n = a.shape[1]
    r = _dot(sel01, jnp.concatenate(_split3(a), axis=1))
    return (r[:, 0:n] + r[:, n:2 * n]) + r[:, 2 * n:3 * n]


def _matmul(a, b, *, trans_a=False, trans_b=False, b_koff=0, out_dtype=F32, tm, tn, tk, add=None, after=None, name):
    K, M = a.shape if trans_a else a.shape[::-1]
    N = b.shape[0] if trans_b else b.shape[1]
    assert M % tm == 0 and N % tn == 0 and K % tk == 0 and not (trans_a and trans_b), (name, M, N, K, tm, tn, tk)
    nk = K // tk

    def body(*refs):
        a_ref, b_ref = refs[:2]
        add_ref = refs[2] if add is not None else None
        o_ref, acc_ref = refs[-2:]
        k = pl.program_id(2)
        if trans_a:
            part = _dot_tn(a_ref[...], b_ref[...])
        else:
            part = _dot_nt(a_ref[...], b_ref[...]) if trans_b else _dot(a_ref[...], b_ref[...])

        def result(r):
            if add_ref is not None:
                r = r + add_ref[...]
            return r.astype(out_dtype)

        if nk == 1:
            o_ref[...] = result(part)
        else:
            @pl.when(k == 0)
            def _():
                acc_ref[...] = part

            @pl.when(jnp.logical_and(k > 0, k < nk - 1))
            def _():
                acc_ref[...] += part

            @pl.when(k == nk - 1)
            def _():
                o_ref[...] = result(acc_ref[...] + part)

    in_specs = [pl.BlockSpec((tk, tm), lambda i, j, k: (k, i)) if trans_a else pl.BlockSpec((tm, tk), lambda i, j, k: (i, k)),
                pl.BlockSpec((tn, tk), lambda i, j, k: (j, k)) if trans_b else pl.BlockSpec((tk, tn), lambda i, j, k: (k + b_koff, j))]
    args = [a, b]
    if add is not None:
        in_specs.append(pl.BlockSpec((tm, tn), lambda i, j, k: (i, j)))
        args.append(add)
    if after is not None:
        in_specs.append(pl.BlockSpec(memory_space=pl.ANY))
        args.append(after)
    return pl.pallas_call(
        body, name=name, grid=(M // tm, N // tn, nk), in_specs=in_specs,
        out_specs=pl.BlockSpec((tm, tn), lambda i, j, k: (i, j)),
        out_shape=jax.ShapeDtypeStruct((M, N), out_dtype),
        scratch_shapes=[pltpu.VMEM((tm, tn), F32)],
        compiler_params=_cp(("parallel", "parallel", "arbitrary")),
    )(*args)


def _norm_fwd(x, w, *, tm):
    S = x.shape[0]

    def body(x_ref, w_ref, o_ref):
        xv = x_ref[...]
        r = lax.rsqrt(jnp.mean(xv * xv, axis=-1, keepdims=True) + EPS)
        o_ref[...] = (xv * r * w_ref[...]).astype(BF16)

    return pl.pallas_call(
        body, name="norm_fwd", grid=(S // tm,),
        in_specs=[pl.BlockSpec((tm, D), lambda i: (i, 0)), pl.BlockSpec((1, D), lambda i: (0, 0))],
        out_specs=pl.BlockSpec((tm, D), lambda i: (i, 0)),
        out_shape=jax.ShapeDtypeStruct((S, D), BF16), compiler_params=_cp(("parallel",)),
    )(x, w)


def _norm_bwd(x, w, dxn, dh, *, tm):
    S = x.shape[0]

    def body(x_ref, w_ref, dxn_ref, dh_ref, gx_ref, dw_ref):
        xv = x_ref[...]
        r = lax.rsqrt(jnp.mean(xv * xv, axis=-1, keepdims=True) + EPS)
        xh = xv * r
        dxn_v = dxn_ref[...]
        dxh = dxn_v * w_ref[...]
        gx_ref[...] = dh_ref[...] + r * (dxh - xh * jnp.mean(dxh * xh, axis=-1, keepdims=True))

        @pl.when(pl.program_id(0) == 0)
        def _():
            dw_ref[...] = jnp.zeros_like(dw_ref)

        dw_ref[0:1, :] += jnp.sum(dxn_v * xh, axis=0, keepdims=True)

    row = pl.BlockSpec((tm, D), lambda i: (i, 0))
    return pl.pallas_call(
        body, name="norm_bwd", grid=(S // tm,),
        in_specs=[row, pl.BlockSpec((1, D), lambda i: (0, 0)), row, row],
        out_specs=[row, pl.BlockSpec((8, D), lambda i: (0, 0))],
        out_shape=[jax.ShapeDtypeStruct((S, D), F32), jax.ShapeDtypeStruct((8, D), F32)],
        compiler_params=_cp(("arbitrary",)),
    )(x, w, dxn, dh)


def _sgu_core(u_ref, v_ref, z_ref, g_ref, b_ref, wm_ref, bias_ref, vnb_ref, mixed_ref, tm):
    v = v_ref[...].astype(F32)
    mu = jnp.mean(v, axis=-1, keepdims=True)
    vc = v - mu
    rs = lax.rsqrt(jnp.mean(vc * vc, axis=-1, keepdims=True) + EPS)
    vh = vc * rs
    vnb_ref[...] = (vh * g_ref[...] + b_ref[...]).astype(BF16)
    for blk in range(tm // SGU_BLOCK):
        rows = pl.ds(blk * SGU_BLOCK, SGU_BLOCK)
        for gi in range(SGU_GROUPS):
            cols = pl.ds(gi * LANE, LANE)
            mixed_ref[rows, cols] = _dot(wm_ref[gi], vnb_ref[rows, cols]) + bias_ref[:, cols]
    return vh, rs


def _sgu_fwd(proj, g, b, wm, bias_full, *, tm):
    S = proj.shape[0]

    def body(u_ref, v_ref, z_ref, g_ref, b_ref, wm_ref, bias_ref, y_ref, vnb_ref, mixed_ref):
        _sgu_core(u_ref, v_ref, z_ref, g_ref, b_ref, wm_ref, bias_ref, vnb_ref, mixed_ref, tm)
        z = z_ref[...].astype(F32)
        y_ref[...] = (u_ref[...].astype(F32) * mixed_ref[...] * (z * _sigmoid(z))).astype(BF16)

    seg = lambda off: pl.BlockSpec((tm, D), lambda i: (i, off // D))
    full = lambda a: pl.BlockSpec(a.shape, lambda i: (0,) * a.ndim)
    return pl.pallas_call(
        body, name="sgu_fwd", grid=(S // tm,),
        in_specs=[seg(OFF_U), seg(OFF_V), seg(OFF_ZA), full(g), full(b), full(wm), full(bias_full)],
        out_specs=pl.BlockSpec((tm, D), lambda i: (i, 0)),
        out_shape=jax.ShapeDtypeStruct((S, D), BF16),
        scratch_shapes=[pltpu.VMEM((tm, D), BF16), pltpu.VMEM((tm, D), F32)],
        compiler_params=_cp(("parallel",)),
    )(proj, proj, proj, g, b, wm, bias_full)


def _sgu_bwd(proj, dy, g, b, wm, wmT, bias_full, mask, sel, *, tm):
    S = proj.shape[0]
    nsteps = S // tm

    def body(u_ref, v_ref, z_ref, dy_ref, g_ref, b_ref, wm_ref, wmT_ref, bias_ref, mask_ref, sel_ref,
             dp_ref, dws_ref, dbs_ref, dg_ref, db_ref, vnb_ref, mixed_ref, dmb_ref, dvn_ref, dbias_ref):
        i = pl.program_id(0)

        @pl.when(i == 0)
        def _():
            dws_ref[...] = jnp.zeros_like(dws_ref)
            dg_ref[...] = jnp.zeros_like(dg_ref)
            db_ref[...] = jnp.zeros_like(db_ref)
            dbias_ref[...] = jnp.zeros_like(dbias_ref)

        vh, rs = _sgu_core(u_ref, v_ref, z_ref, g_ref, b_ref, wm_ref, bias_ref, vnb_ref, mixed_ref, tm)
        u = u_ref[...].astype(F32)
        z = z_ref[...].astype(F32)
        dy_v = dy_ref[...].astype(F32)
        mixed = mixed_ref[...]
        sg = _sigmoid(z)
        sz = z * sg
        dp_ref[:, 0:D] = (dy_v * mixed * sz).astype(BF16)
        dp_ref[:, 2 * D:3 * D] = (dy_v * u * mixed * (sg * (1.0 + z * (1.0 - sg)))).astype(BF16)
        dmixed = dy_v * u * sz
        dmb_ref[...] = dmixed.astype(BF16)
        for blk in range(tm // SGU_BLOCK):
            dbias_ref[...] += dmixed[blk * SGU_BLOCK:(blk + 1) * SGU_BLOCK, :]
        for blk in range(tm // SGU_BLOCK):
            rows = pl.ds(blk * SGU_BLOCK, SGU_BLOCK)
            for gi in range(SGU_GROUPS):
                cols = pl.ds(gi * LANE, LANE)
                dm = dmb_ref[rows, cols]
                dvn_ref[rows, cols] = _dot(wmT_ref[gi], dm)
                dws_ref[gi] += _dot_nt(dm, vnb_ref[rows, cols])
        dvn = dvn_ref[...]
        dg_ref[0:1, :] += jnp.sum(dvn * vh, axis=0, keepdims=True)
        db_ref[0:1, :] += jnp.sum(dvn, axis=0, keepdims=True)
        dvh = dvn * g_ref[...]
        dv = rs * (dvh - jnp.mean(dvh, axis=-1, keepdims=True) - vh * jnp.mean(dvh * vh, axis=-1, keepdims=True))
        dp_ref[:, D:2 * D] = dv.astype(BF16)

        @pl.when(i == nsteps - 1)
        def _():
            for gi in range(SGU_GROUPS):
                dws_ref[gi] = dws_ref[gi] * mask_ref[...]
            dbs_ref[...] = _dot(dbias_ref[...], sel_ref[...], precision=HI)

    seg = lambda off: pl.BlockSpec((tm, D), lambda i: (i, off // D))
    full = lambda a: pl.BlockSpec(a.shape, lambda i: (0,) * a.ndim)
    return pl.pallas_call(
        body, name="sgu_bwd", grid=(nsteps,),
        in_specs=[seg(OFF_U), seg(OFF_V), seg(OFF_ZA), pl.BlockSpec((tm, D), lambda i: (i, 0)),
                  full(g), full(b), full(wm), full(wmT), full(bias_full), full(mask), full(sel)],
        out_specs=[pl.BlockSpec((tm, 3 * D), lambda i: (i, 0)),
                   pl.BlockSpec((SGU_GROUPS, SGU_BLOCK, SGU_BLOCK), lambda i: (0, 0, 0)),
                   pl.BlockSpec((SGU_BLOCK, LANE), lambda i: (0, 0)),
                   pl.BlockSpec((8, D), lambda i: (0, 0)), pl.BlockSpec((8, D), lambda i: (0, 0))],
        out_shape=[jax.ShapeDtypeStruct((S, 3 * D), BF16),
                   jax.ShapeDtypeStruct((SGU_GROUPS, SGU_BLOCK, SGU_BLOCK), F32),
                   jax.ShapeDtypeStruct((SGU_BLOCK, LANE), F32),
                   jax.ShapeDtypeStruct((8, D), F32), jax.ShapeDtypeStruct((8, D), F32)],
        scratch_shapes=[pltpu.VMEM((tm, D), BF16), pltpu.VMEM((tm, D), F32), pltpu.VMEM((tm, D), BF16),
                        pltpu.VMEM((tm, D), F32), pltpu.VMEM((SGU_BLOCK, D), F32)],
        compiler_params=_cp(("arbitrary",)),
    )(proj, proj, proj, dy, g, b, wm, wmT, bias_full, mask, sel)


SSD_T = 2 * CHUNK
HALO = 8
HALO_BLK = 16


def _pair_masks():
    row = lax.broadcasted_iota(jnp.int32, (CHUNK, LANE), 0)
    lane = lax.broadcasted_iota(jnp.int32, (CHUNK, LANE), 1)
    pos = jnp.where(lane >= CHUNK, lane - CHUNK, lane)
    diag = (row == pos).astype(F32)
    causal = row >= pos
    lo = (lane < CHUNK).astype(F32)
    return diag, causal, lo, 1.0 - lo


def _ssd_chunk_fwd(c, ext_ref, shift_ref, dt_ref, cw_ref, cb_ref, dtb_ref, alog_ref, tri_ref, exp_ref):
    r0 = c * CHUNK
    win = ext_ref[pl.ds(r0, HALO_BLK + CHUNK), :]
    sh = _dot(shift_ref[...], win)
    taps = [sh[k * CHUNK:(k + 1) * CHUNK] for k in range(CONV_K - 1)] + [win[HALO_BLK:].astype(F32)]
    pre = cb_ref[...] + sum(cw_ref[k:k + 1, :] * taps[k] for k in range(CONV_K))
    sg = _sigmoid(pre)
    xc = pre * sg
    dtr = dt_ref[pl.ds(r0, CHUNK), :].astype(F32) + dtb_ref[...]
    dtv = _softplus(dtr)
    A = -jnp.exp(alog_ref[...])
    acs = _sel_left(tri_ref[...], dtv * A)
    both = _sel_right_k(jnp.concatenate([acs, dtv], axis=0), exp_ref[...])
    E, dtE = both[0:CHUNK], both[CHUNK:2 * CHUNK]
    return dict(taps=taps, pre=pre, sg=sg, xc=xc, dtr=dtr, dtv=dtv, A=A, E=E, dtE=dtE)


def _ssd_fwd(proj, conv_w, conv_b, dtb_p, alog_p, d_exp, norm_w, tri, expand, shift):
    S = proj.shape[0]
    T = SSD_T
    nsteps = S // T
    ncl = T // CHUNK

    def body(zb_ref, xbc_ref, halo_ref, dt_ref, cw_ref, cb_ref, dtb_ref, alog_ref, dexp_ref, nw_ref, tri_ref, exp_ref, shift_ref,
             y_ref, yb_ref, st_ref, ht_ref, ext_ref):
        i = pl.program_id(0)

        @pl.when(i == 0)
        def _():
            ht_ref[...] = jnp.zeros_like(ht_ref)
            ext_ref[0:HALO_BLK, :] = jnp.zeros((HALO_BLK, XBC_W), BF16)

        @pl.when(i > 0)
        def _():
            ext_ref[0:HALO_BLK, :] = halo_ref[...]

        ext_ref[HALO_BLK:HALO_BLK + T, :] = xbc_ref[...]
        diag, causal, lo, hi = _pair_masks()
        for c in range(ncl):
            q = _ssd_chunk_fwd(c, ext_ref, shift_ref, dt_ref, cw_ref, cb_ref, dtb_ref, alog_ref, tri_ref, exp_ref)
            rows = pl.ds(c * CHUNK, CHUNK)
            xc, E, dtE = q["xc"], q["E"], q["dtE"]
            xs = xc[:, 0:D]
            total = E[CHUNK - 1:CHUNK, :]
            x_dt = xs * dtE
            eE = jnp.exp(E)
            xw = x_dt * jnp.exp(total - E)
            st_ref[c] = ht_ref[...]
            for g in range(SSD_GROUPS):
                gc = slice(g * GROUP_W, (g + 1) * GROUP_W)
                Bg = xc[:, D + g * STATE:D + (g + 1) * STATE].astype(BF16)
                Cg = xc[:, D + SSD_GROUPS * STATE + g * STATE:D + SSD_GROUPS * STATE + (g + 1) * STATE].astype(BF16)
                cb2 = _dot_nt(Cg, jnp.concatenate([Bg, Bg], axis=0))
                htg = ht_ref[:, gc]
                y_ref[rows, gc] = eE[:, gc] * _dot(Cg, htg.astype(BF16)) + xs[:, gc] * dexp_ref[:, gc]
                for jj in range(GROUP_W // LANE):
                    pc = slice(g * GROUP_W + jj * LANE, g * GROUP_W + (jj + 1) * LANE)
                    Ej = E[:, pc]
                    e2 = jnp.sum(Ej * diag, axis=0, keepdims=True)
                    Mp = cb2 * jnp.exp(jnp.where(causal, Ej - e2, -1e30))
                    xj = x_dt[:, pc]
                    xbd = jnp.concatenate([xj * lo, xj * hi], axis=0).astype(BF16)
                    y_ref[rows, pc] += _dot(Mp.astype(BF16), xbd)
                ht_ref[:, gc] = jnp.exp(total[:, gc]) * htg + _dot_tn(Bg, xw[:, gc].astype(BF16))
            zb = zb_ref[rows, :].astype(F32)
            hh = y_ref[rows, :] * (zb * _sigmoid(zb))
            for g in range(SSD_GROUPS):
                gc = slice(g * GROUP_W, (g + 1) * GROUP_W)
                hg = hh[:, gc]
                r = lax.rsqrt(jnp.mean(hg * hg, axis=-1, keepdims=True) + EPS)
                yb_ref[rows, gc] = (hg * r * nw_ref[:, gc]).astype(BF16)

    full = lambda a: pl.BlockSpec(a.shape, lambda i: (0,) * a.ndim)
    hb = T // HALO_BLK
    return pl.pallas_call(
        body, name="ssd_fwd", grid=(nsteps,),
        in_specs=[pl.BlockSpec((T, D), lambda i: (i, OFF_ZB // D)),
                  pl.BlockSpec((T, XBC_W), lambda i: (i, OFF_XBC // XBC_W)),
                  pl.BlockSpec((HALO_BLK, XBC_W), lambda i: (jnp.maximum(i * hb - 1, 0), OFF_XBC // XBC_W)),
                  pl.BlockSpec((T, DT_W), lambda i: (i, OFF_DT // DT_W)),
                  full(conv_w), full(conv_b), full(dtb_p), full(alog_p), full(d_exp), full(norm_w), full(tri), full(expand),
                  full(shift)],
        out_specs=[pl.BlockSpec((T, D), lambda i: (i, 0)), pl.BlockSpec((T, D), lambda i: (i, 0)),
                   pl.BlockSpec((ncl, STATE, D), lambda i: (i, 0, 0))],
        out_shape=[jax.ShapeDtypeStruct((S, D), F32), jax.ShapeDtypeStruct((S, D), BF16),
                   jax.ShapeDtypeStruct((S // CHUNK, STATE, D), F32)],
        scratch_shapes=[pltpu.VMEM((STATE, D), F32), pltpu.VMEM((HALO_BLK + T, XBC_W), BF16)],
        compiler_params=_cp(("arbitrary",)),
    )(proj, proj, proj, proj, conv_w, conv_b, dtb_p, alog_p, d_exp, norm_w, tri, expand, shift)


def _ssd_bwd(proj, dyb, y, states, conv_w, conv_b, dtb_p, alog_p, d_exp, norm_w, tri, triT, expand, expandT, shift):
    S = proj.shape[0]
    T = SSD_T
    nsteps = S // T
    ncl = T // CHUNK
    SSD_W = SSD_PAD_W

    def body(zb_ref, xbc_ref, halo_ref, dt_ref, dyb_ref, y_ref, st_ref, cw_ref, cb_ref, dtb_ref, alog_ref, dexp_ref, nw_ref,
             tri_ref, triT_ref, exp_ref, expT_ref, shift_ref,
             dp_ref, dcw_ref, dcb_ref, ddtb_ref, dalog_ref, dD_ref, dnw_ref,
             dht_ref, ext_ref, dpre_ref, dy_s, dE_s, dxdt_s, dxc_s, dDacc_ref, dAacc_ref):
        i = pl.program_id(0)

        @pl.when(i == 0)
        def _():
            for r in (dht_ref, dcw_ref, dcb_ref, ddtb_ref, dnw_ref, dDacc_ref, dAacc_ref):
                r[...] = jnp.zeros_like(r)
            dpre_ref[T:T + HALO_BLK, :] = jnp.zeros((HALO_BLK, XBC_W), F32)

        @pl.when(i == nsteps - 1)
        def _():
            ext_ref[0:HALO_BLK, :] = jnp.zeros((HALO_BLK, XBC_W), BF16)

        @pl.when(i < nsteps - 1)
        def _():
            ext_ref[0:HALO_BLK, :] = halo_ref[...]

        ext_ref[HALO_BLK:HALO_BLK + T, :] = xbc_ref[...]
        diag, causal, lo, hi = _pair_masks()
        last_row = (lax.broadcasted_iota(jnp.int32, (CHUNK, 1), 0) == CHUNK - 1).astype(F32)
        for c in reversed(range(ncl)):
            q = _ssd_chunk_fwd(c, ext_ref, shift_ref, dt_ref, cw_ref, cb_ref, dtb_ref, alog_ref, tri_ref, exp_ref)
            rows = pl.ds(c * CHUNK, CHUNK)
            pre, sg, xc, dtr, dtv, A, E, dtE = (q[k] for k in ("pre", "sg", "xc", "dtr", "dtv", "A", "E", "dtE"))
            xs = xc[:, 0:D]
            total = E[CHUNK - 1:CHUNK, :]
            x_dt = xs * dtE
            eE = jnp.exp(E)
            wdec = jnp.exp(total - E)
            zb = zb_ref[rows, :].astype(F32)
            yv = y_ref[rows, :]
            sgz = _sigmoid(zb)
            sz = zb * sgz
            hh = yv * sz
            for g in range(SSD_GROUPS):
                gc = slice(g * GROUP_W, (g + 1) * GROUP_W)
                hg = hh[:, gc]
                r = lax.rsqrt(jnp.mean(hg * hg, axis=-1, keepdims=True) + EPS)
                dyb_g = dyb_ref[rows, gc].astype(F32)
                dn = dyb_g * nw_ref[:, gc]
                dnw_ref[0:1, gc] += jnp.sum(dyb_g * hg * r, axis=0, keepdims=True)
                dy_s[:, gc] = r * dn - hg * (r * r * r) * jnp.mean(dn * hg, axis=-1, keepdims=True)
            dhh = dy_s[...]
            dp_ref[rows, 0:D] = (dhh * yv * (sgz * (1.0 + zb * (1.0 - sgz)))).astype(BF16)
            dy = dhh * sz
            dy_s[...] = dy
            dDacc_ref[0:1, :] += jnp.sum(dy * xs, axis=0, keepdims=True)
            dxc_s[:, 0:D] = dy * dexp_ref[...]
            for g in range(SSD_GROUPS):
                gc = slice(g * GROUP_W, (g + 1) * GROUP_W)
                bcol = slice(D + g * STATE, D + (g + 1) * STATE)
                ccol = slice(D + SSD_GROUPS * STATE + g * STATE, D + SSD_GROUPS * STATE + (g + 1) * STATE)
                Bg = xc[:, bcol].astype(BF16)
                Cg = xc[:, ccol].astype(BF16)
                B2 = jnp.concatenate([Bg, Bg], axis=0)
                cb2 = _dot_nt(Cg, B2)
                htg = st_ref[c, :, gc]
                htb = htg.astype(BF16)
                dhn = dht_ref[:, gc]
                dhnb = dhn.astype(BF16)
                dyg = dy[:, gc]
                eEg = eE[:, gc]
                wg = wdec[:, gc]
                xdg = x_dt[:, gc]
                CH = _dot(Cg, htb)
                dCHb = (dyg * eEg).astype(BF16)
                dC = _dot_nt(dCHb, htb)
                dl = jnp.exp(total[:, gc])
                dht_prev = _dot_tn(Cg, dCHb) + dl * dhn
                dtot = jnp.sum(dhn * htg, axis=0, keepdims=True) * dl
                dxw = _dot(Bg, dhnb)
                dB = _dot_nt((xdg * wg).astype(BF16), dhnb)
                dwd = dxw * xdg * wg
                dtot = dtot + jnp.sum(dwd, axis=0, keepdims=True)
                dE_s[:, gc] = dyg * eEg * CH - dwd + last_row * dtot
                dxdt_s[:, gc] = dxw * wg
                dcb2 = jnp.zeros((CHUNK, LANE), F32)
                for jj in range(GROUP_W // LANE):
                    pc = slice(g * GROUP_W + jj * LANE, g * GROUP_W + (jj + 1) * LANE)
                    Ej = E[:, pc]
                    e2 = jnp.sum(Ej * diag, axis=0, keepdims=True)
                    Lp = jnp.exp(jnp.where(causal, Ej - e2, -1e30))
                    Mp = cb2 * Lp
                    xj = x_dt[:, pc]
                    xbd = jnp.concatenate([xj * lo, xj * hi], axis=0).astype(BF16)
                    dyj = dy[:, pc].astype(BF16)
                    dMp = _dot_nt(dyj, xbd)
                    dxbd = _dot_tn(Mp.astype(BF16), dyj)
                    dxdt_s[:, pc] += dxbd[0:CHUNK, :] * lo + dxbd[CHUNK:2 * CHUNK, :] * hi
                    dcb2 = dcb2 + dMp * Lp
                    dseg = dMp * Mp
                    dE_s[:, pc] += dseg - diag * jnp.sum(dseg, axis=0, keepdims=True)
                dcb2b = dcb2.astype(BF16)
                dC = dC + _dot(dcb2b, B2)
                dB2 = _dot_tn(dcb2b, Cg)
                dB = dB + dB2[0:CHUNK, :] + dB2[CHUNK:2 * CHUNK, :]
                dxc_s[:, bcol] = dB
                dxc_s[:, ccol] = dC
                dht_ref[:, gc] = dht_prev
            dx_dt = dxdt_s[...]
            dxc_s[:, 0:D] += dx_dt * dtE
            red = _sel_right(jnp.concatenate([dE_s[...], dx_dt * xs], axis=0), expT_ref[...])
            da = _sel_left(triT_ref[...], red[0:CHUNK, :])
            ddtv = red[CHUNK:2 * CHUNK, :] + da * A
            dAacc_ref[0:1, :] += jnp.sum(da * dtv, axis=0, keepdims=True)
            ddtr = ddtv * _sigmoid(dtr)
            ddtb_ref[0:1, :] += jnp.sum(ddtr, axis=0, keepdims=True)
            dp_ref[rows, D + XBC_W:D + XBC_W + DT_W] = ddtr.astype(BF16)
            dpre = dxc_s[...] * (sg * (1.0 + pre * (1.0 - sg)))
            dpre_ref[rows, :] = dpre
            dcb_ref[0:1, :] += jnp.sum(dpre, axis=0, keepdims=True)
            for k in range(CONV_K):
                dcw_ref[k:k + 1, :] += jnp.sum(dpre * q["taps"][k], axis=0, keepdims=True)
        dxbc = jnp.zeros((T, XBC_W), F32)
        for k in range(CONV_K):
            dxbc = dxbc + cw_ref[k:k + 1, :] * dpre_ref[pl.ds(CONV_K - 1 - k, T), :]
        dp_ref[:, D:D + XBC_W] = dxbc.astype(BF16)
        dp_ref[:, SEG_SSD[1]:SSD_W] = jnp.zeros((T, SSD_W - SEG_SSD[1]), BF16)
        dpre_ref[T:T + HALO, :] = dpre_ref[0:HALO, :]

        @pl.when(i == nsteps - 1)
        def _():
            dalog_ref[...] = dAacc_ref[...] * (-jnp.exp(alog_ref[...]))
            dD_ref[...] = _dot(dDacc_ref[...], expT_ref[...].astype(F32), precision=HI)

    full = lambda a: pl.BlockSpec(a.shape, lambda i: (0,) * a.ndim)
    hb = T // HALO_BLK
    rev = lambda i: nsteps - 1 - i
    acc = lambda w: pl.BlockSpec((8, w), lambda i: (0, 0))
    return pl.pallas_call(
        body, name="ssd_bwd", grid=(nsteps,),
        in_specs=[pl.BlockSpec((T, D), lambda i: (rev(i), OFF_ZB // D)),
                  pl.BlockSpec((T, XBC_W), lambda i: (rev(i), OFF_XBC // XBC_W)),
                  pl.BlockSpec((HALO_BLK, XBC_W), lambda i: (jnp.maximum(rev(i) * hb - 1, 0), OFF_XBC // XBC_W)),
                  pl.BlockSpec((T, DT_W), lambda i: (rev(i), OFF_DT // DT_W)),
                  pl.BlockSpec((T, D), lambda i: (rev(i), 0)), pl.BlockSpec((T, D), lambda i: (rev(i), 0)),
                  pl.BlockSpec((ncl, STATE, D), lambda i: (rev(i), 0, 0)),
                  full(conv_w), full(conv_b), full(dtb_p), full(alog_p), full(d_exp), full(norm_w),
                  full(tri), full(triT), full(expand), full(expandT), full(shift)],
        out_specs=[pl.BlockSpec((T, SSD_W), lambda i: (rev(i), 0)),
                   acc(XBC_W), acc(XBC_W), acc(DT_W), acc(DT_W), acc(DT_W), acc(D)],
        out_shape=[jax.ShapeDtypeStruct((S, SSD_W), BF16),
                   jax.ShapeDtypeStruct((8, XBC_W), F32), jax.ShapeDtypeStruct((8, XBC_W), F32),
                   jax.ShapeDtypeStruct((8, DT_W), F32), jax.ShapeDtypeStruct((8, DT_W), F32),
                   jax.ShapeDtypeStruct((8, DT_W), F32), jax.ShapeDtypeStruct((8, D), F32)],
        scratch_shapes=[pltpu.VMEM((STATE, D), F32), pltpu.VMEM((HALO_BLK + T, XBC_W), BF16), pltpu.VMEM((T + HALO_BLK, XBC_W), F32),
                        pltpu.VMEM((CHUNK, D), F32), pltpu.VMEM((CHUNK, D), F32), pltpu.VMEM((CHUNK, D), F32),
                        pltpu.VMEM((CHUNK, XBC_W), F32), pltpu.VMEM((8, D), F32), pltpu.VMEM((8, DT_W), F32)],
        compiler_params=_cp(("arbitrary",)),
    )(proj, proj, proj, proj, dyb, y, states, conv_w, conv_b, dtb_p, alog_p, d_exp, norm_w, tri, triT, expand, expandT, shift)


def _head(x, ya, yb, proj, target, gate_b, wout, fw, *, tm):
    S = x.shape[0]

    def body(x_ref, ya_ref, yb_ref, gl0_ref, gl1_ref, t_ref, gb_ref, w_ref, fw_ref,
             dh_ref, dhb_ref, mb_ref, dya_ref, dyb_ref, dgl_ref, loss_ref, dfw_ref, dgb_ref):
        @pl.when(pl.program_id(0) == 0)
        def _():
            loss_ref[...] = jnp.zeros_like(loss_ref)
            dfw_ref[...] = jnp.zeros_like(dfw_ref)
            dgb_ref[...] = jnp.zeros_like(dgb_ref)

        ya_v = ya_ref[...].astype(F32)
        yb_v = yb_ref[...].astype(F32)
        g0 = _sigmoid(gl0_ref[...].astype(F32) + gb_ref[:, 0:D])
        g1 = _sigmoid(gl1_ref[...].astype(F32) + gb_ref[:, D:2 * D])
        mb = (g0 * ya_v + g1 * yb_v).astype(BF16)
        mb_ref[...] = mb
        h = x_ref[...] + _dot(mb, w_ref[...])
        r = lax.rsqrt(jnp.mean(h * h, axis=-1, keepdims=True) + EPS)
        hn = h * r
        err = hn * fw_ref[...] - t_ref[...]
        loss_ref[...] += 0.5 * jnp.sum(jnp.mean(err * err, axis=-1, keepdims=True))
        dyf = err * (1.0 / D)
        dfw_ref[0:1, :] += jnp.sum(dyf * hn, axis=0, keepdims=True)
        dhn = dyf * fw_ref[...]
        dh = r * (dhn - hn * jnp.mean(dhn * hn, axis=-1, keepdims=True))
        dh_ref[...] = dh
        dhb = dh.astype(BF16)
        dhb_ref[...] = dhb
        dm = _dot_nt(dhb, w_ref[...])
        dya_ref[...] = (dm * g0).astype(BF16)
        dyb_ref[...] = (dm * g1).astype(BF16)
        dgl0 = dm * ya_v * g0 * (1.0 - g0)
        dgl1 = dm * yb_v * g1 * (1.0 - g1)
        dgl_ref[:, 0:D] = dgl0.astype(BF16)
        dgl_ref[:, D:2 * D] = dgl1.astype(BF16)
        dgb_ref[0:1, 0:D] += jnp.sum(dgl0, axis=0, keepdims=True)
        dgb_ref[0:1, D:2 * D] += jnp.sum(dgl1, axis=0, keepdims=True)

    row = pl.BlockSpec((tm, D), lambda i: (i, 0))
    seg = lambda off: pl.BlockSpec((tm, D), lambda i: (i, off // D))
    full = lambda a: pl.BlockSpec(a.shape, lambda i: (0,) * a.ndim)
    acc = lambda w: pl.BlockSpec((8, w), lambda i: (0, 0))
    return pl.pallas_call(
        body, name="head", grid=(S // tm,),
        in_specs=[row, row, row, seg(OFF_G0), seg(OFF_G1), row, full(gate_b), full(wout), full(fw)],
        out_specs=[row, row, row, row, row, pl.BlockSpec((tm, 2 * D), lambda i: (i, 0)), acc(LANE), acc(D), acc(2 * D)],
        out_shape=[jax.ShapeDtypeStruct((S, D), F32), jax.ShapeDtypeStruct((S, D), BF16), jax.ShapeDtypeStruct((S, D), BF16),
                   jax.ShapeDtypeStruct((S, D), BF16), jax.ShapeDtypeStruct((S, D), BF16), jax.ShapeDtypeStruct((S, 2 * D), BF16),
                   jax.ShapeDtypeStruct((8, LANE), F32), jax.ShapeDtypeStruct((8, D), F32), jax.ShapeDtypeStruct((8, 2 * D), F32)],
        compiler_params=_cp(("arbitrary",)),
    )(x, ya, yb, proj, proj, target, gate_b, wout, fw)


def _adam_update(g, w_ref, m_ref, v_ref, g_ref, d_ref, m2_ref, v2_ref):
    m2 = ADAM_B1 * m_ref[...] + (1.0 - ADAM_B1) * g
    v2 = ADAM_B2 * v_ref[...] + (1.0 - ADAM_B2) * (g * g)
    m_hat = m2 / (1.0 - ADAM_B1 ** ADAM_STEP)
    v_hat = v2 / (1.0 - ADAM_B2 ** ADAM_STEP)
    g_ref[...] = g
    d_ref[...] = -ADAM_LR * (m_hat / (jnp.sqrt(v_hat) + ADAM_EPS) + ADAM_WD * w_ref[...])
    m2_ref[...] = m2
    v2_ref[...] = v2


def _adamw_own(me, own, landed, w, m, v, *, tr, tc, name):
    _, R, C = landed.shape
    assert R % tr == 0 and C % tc == 0, (name, R, C, tr, tc)

    def body(me_ref, own_ref, p_ref, w_ref, m_ref, v_ref, g_ref, d_ref, m2_ref, v2_ref):
        mine = own_ref[0].astype(F32)
        g = jnp.where(me_ref[0] == 0, mine, p_ref[0].astype(F32))
        for k in range(1, N_DEV):
            g = g + jnp.where(me_ref[0] == k, mine, p_ref[k].astype(F32))
        _adam_update(g, w_ref, m_ref, v_ref, g_ref, d_ref, m2_ref, v2_ref)

    tile = pl.BlockSpec((tr, tc), lambda i, j, me_ref: (i, j))
    return pl.pallas_call(
        body, name=name,
        grid_spec=pltpu.PrefetchScalarGridSpec(
            num_scalar_prefetch=1, grid=(R // tr, C // tc),
            in_specs=[pl.BlockSpec((1, tr, tc), lambda i, j, me_ref: (me_ref[0], i, j)),
                      pl.BlockSpec((N_DEV, tr, tc), lambda i, j, me_ref: (0, i, j)), tile, tile, tile],
            out_specs=[tile, tile, tile, tile]),
        out_shape=[jax.ShapeDtypeStruct((R, C), F32)] * 4,
        compiler_params=_cp(("parallel", "parallel")),
    )(me, own, landed, w, m, v)


def _adamw(parts, w, m, v, *, tr, name):
    _, R, C = parts.shape
    assert R % tr == 0, (name, R, tr)

    def body(p_ref, w_ref, m_ref, v_ref, g_ref, d_ref, m2_ref, v2_ref):
        g = p_ref[0].astype(F32)
        for k in range(1, N_DEV):
            g = g + p_ref[k].astype(F32)
        _adam_update(g, w_ref, m_ref, v_ref, g_ref, d_ref, m2_ref, v2_ref)

    row = pl.BlockSpec((tr, C), lambda i: (i, 0))
    return pl.pallas_call(
        body, name=name, grid=(R // tr,),
        in_specs=[pl.BlockSpec((N_DEV, tr, C), lambda i: (0, i, 0)), row, row, row],
        out_specs=[row, row, row, row],
        out_shape=[jax.ShapeDtypeStruct((R, C), F32)] * 4,
        compiler_params=_cp(("parallel",)),
    )(parts, w, m, v)


def _place():
    x, y, c = lax.axis_index("x"), lax.axis_index("y"), lax.axis_index("c")
    return x, y, c


def _all_gather(arrs, *, name):
    n = len(arrs)

    def body(*refs):
        ins, outs = refs[:n], refs[n:2 * n]
        send_sems, recv_sems, local_sems = refs[2 * n:]
        x, y, c = _place()
        me, sibling = (x, y, c), (x, y, 1 - c)
        chips = [(1 - x, y), (x, 1 - y), (1 - x, 1 - y)]

        def idx(px, py, pc):
            return 4 * px + 2 * py + pc

        def copy(k, a, block, to, src=None):
            slab = outs[a].at[idx(*block)]
            return pltpu.make_async_remote_copy(
                src_ref=slab if src is None else src, dst_ref=slab,
                send_sem=send_sems.at[k, a], recv_sem=recv_sems.at[k, a], device_id=to, device_id_type=MESH)

        mine = [pltpu.make_async_copy(ins[a], outs[a].at[idx(*me)], local_sems.at[a]) for a in range(n)]
        for cp in mine:
            cp.start()
        first = []
        for a in range(n):
            first.append(copy(0, a, me, sibling, src=ins[a]))
            first += [copy(1 + j, a, me, (*chip, c), src=ins[a]) for j, chip in enumerate(chips)]
        for cp in first:
            cp.start()
        passed = []
        for j, chip in enumerate(chips):
            for a in range(n):
                copy(1 + j, a, (*chip, c), me).wait_recv()
                fwd = copy(4 + j, a, (*chip, c), sibling)
                fwd.start()
                passed.append(fwd)
        for a in range(n):
            copy(0, a, sibling, me).wait_recv()
            for j, chip in enumerate(chips):
                copy(4 + j, a, (*chip, 1 - c), me).wait_recv()
        for cp in first + passed:
            cp.wait_send()
        for cp in mine:
            cp.wait()

    anyspec = pl.BlockSpec(memory_space=pl.ANY)
    return pl.pallas_call(
        body, name=name,
        in_specs=[anyspec] * n, out_specs=[anyspec] * n,
        out_shape=[jax.ShapeDtypeStruct((N_DEV,) + a.shape, a.dtype) for a in arrs],
        scratch_shapes=[pltpu.SemaphoreType.DMA((7, n)), pltpu.SemaphoreType.DMA((7, n)), pltpu.SemaphoreType.DMA((n,))],
    )(*arrs)


W_ROWS = SEG_SSD[0] + SSD_PAD_W


GROUP = 16
INTERIOR = 1920


def _interior(k):
    lo = -(-(k * SHARD_IN) // GROUP) * GROUP
    hi = ((k + 1) * SHARD_IN) // GROUP * GROUP
    return lo, hi


def _dest_row(r):
    if r < REF_SGU_END:
        return r
    return r - REF_SGU_END + SEG_SSD[0] if r < REF_GATE_START else r - REF_GATE_START + SEG_GATE[0]


def _shard_pieces(k):
    lo_k, hi_k = _interior(k)
    out = []
    for lo, hi in ((0, REF_SGU_END), (REF_SGU_END, REF_GATE_START), (REF_GATE_START, W_IN)):
        a, b = max(lo, lo_k), min(hi, hi_k)
        if a < b:
            out.append((a - lo_k, b - a, _dest_row(a)))
    return out


GATHER_PARTS = 2


def _shard_parts(k):
    parts = [[] for _ in range(GATHER_PARTS)]
    for s0, n, d0 in _shard_pieces(k):
        step = -(-(n // GROUP) // GATHER_PARTS) * GROUP
        for p in range(GATHER_PARTS):
            a, b = min(p * step, n), min((p + 1) * step, n)
            if a < b:
                parts[p].append((s0 + a, b - a, d0 + a))
    return parts


def _patch_straddlers(wpT, heads, tails):
    for k in range(1, N_DEV):
        m = (k * SHARD_IN) % GROUP
        if m:
            group = jnp.concatenate([tails[k - 1, GROUP - m:], heads[k, :GROUP - m]], axis=0)
            wpT = lax.dynamic_update_slice(wpT, group, (_dest_row(k * SHARD_IN - m), 0))
    return wpT


def _gather_stages(k, win_ref, small, z_ref, n_zero, w_ref, send_sems, recv_sems, local_sems):
    x, y, c = k // 4, (k // 2) % 2, k % 2
    idx = lambda p: 4 * p[0] + 2 * p[1] + p[2]
    me, sib = (x, y, c), (x, y, 1 - c)
    xn, yn, dg = (1 - x, y, c), (x, 1 - y, c), (1 - x, 1 - y, c)
    parts = range(GATHER_PARTS)

    def copies(slot, block, to, part, own=False):
        kb = idx(block)
        out = []
        for j, (s0, n, d0) in enumerate(_shard_parts(kb)[part]):
            dst = w_ref.at[pl.ds(d0, n)]
            out.append((win_ref.at[pl.ds(s0, n)] if own else dst, dst, 2 * part + j))
        if part == 0:
            for j, (src, gathered) in enumerate(small):
                out.append((src if own else gathered.at[kb], gathered.at[kb], 2 * GATHER_PARTS + j))
        return [pltpu.make_async_remote_copy(src_ref=s, dst_ref=d, send_sem=send_sems.at[slot, j], recv_sem=recv_sems.at[slot, j],
                                             device_id=to, device_id_type=MESH) for s, d, j in out]

    def start(cps):
        for cp in cps:
            cp.start()

    def arrived(slot, block, part):
        for cp in copies(slot, block, me, part):
            cp.wait_recv()

    def local():
        pairs = [(win_ref.at[pl.ds(s0, n)], w_ref.at[pl.ds(d0, n)]) for s0, n, d0 in _shard_pieces(k)]
        pairs += [(src, gathered.at[k]) for src, gathered in small] + [(z_ref, w_ref.at[pl.ds(W_IN, n_zero)])]
        return [pltpu.make_async_copy(s, d, local_sems.at[j]) for j, (s, d) in enumerate(pairs)]

    relay = (xn, yn) if c == 1 else (yn, xn)

    def first():
        start(local())
        for p in parts:
            start(copies(0, me, sib, p, own=True) + copies(1, me, xn, p, own=True) + copies(2, me, yn, p, own=True))

    def hand_on():
        for p in parts:
            arrived(1, xn, p)
            start(copies(4, xn, sib, p))
            if c == 1:
                start(copies(3, *relay, p))
            arrived(2, yn, p)
            start(copies(5, yn, sib, p))
            if c == 0:
                start(copies(3, *relay, p))

    def finish():
        for p in parts:
            arrived(3, dg, p)
            start(copies(6, dg, sib, p))
        for p in parts:
            arrived(0, sib, p)
            arrived(4, (1 - x, y, 1 - c), p)
            arrived(5, (x, 1 - y, 1 - c), p)
            arrived(6, (1 - x, 1 - y, 1 - c), p)
        for p in parts:
            sent = (copies(0, me, sib, p, own=True) + copies(1, me, xn, p, own=True) + copies(2, me, yn, p, own=True)
                    + copies(3, *relay, p) + copies(4, xn, sib, p) + copies(5, yn, sib, p) + copies(6, dg, sib, p))
            for cp in sent:
                cp.wait_send()
        for cp in local():
            cp.wait()

    return first, hand_on, finish


def _gather_sems(n_small):
    n_arr = 2 * GATHER_PARTS + n_small
    return [pltpu.SemaphoreType.DMA((7, n_arr)), pltpu.SemaphoreType.DMA((7, n_arr)), pltpu.SemaphoreType.DMA((n_arr + 1,))]


def _gather_weights(win, head, tail, wout, cw, zeros):
    small_in = (wout, cw, head, tail)
    n_zero = zeros.shape[0]
    assert W_IN + n_zero == W_ROWS and W_IN % GROUP == 0

    def body(win_ref, wout_ref, cw_ref, head_ref, tail_ref, z_ref, w_ref, gout_ref, gcw_ref, ghead_ref, gtail_ref, *sems):
        x, y, c = _place()
        me = 4 * x + 2 * y + c
        small = ((wout_ref, gout_ref), (cw_ref, gcw_ref), (head_ref, ghead_ref), (tail_ref, gtail_ref))

        def run(k):
            for stage in _gather_stages(k, win_ref, small, z_ref, n_zero, w_ref, *sems):
                stage()

        for k in range(N_DEV):
            pl.when(me == k)(functools.partial(run, k))

    anyspec = pl.BlockSpec(memory_space=pl.ANY)
    return pl.pallas_call(
        body, name="gather_weights", in_specs=[anyspec] * 6, out_specs=[anyspec] * 5,
        out_shape=[jax.ShapeDtypeStruct((W_ROWS, D), win.dtype)]
        + [jax.ShapeDtypeStruct((N_DEV,) + a.shape, a.dtype) for a in small_in],
        scratch_shapes=_gather_sems(len(small_in)),
    )(win, wout, cw, head, tail, zeros)


_REL = [(dx, dy, dc) for dx in (0, 1) for dy in (0, 1) for dc in (0, 1)][1:]
_HBM = pl.BlockSpec(memory_space=pltpu.HBM)
_SEM = pl.BlockSpec(memory_space=pltpu.SEMAPHORE)
_EFFECT = pltpu.SideEffectType.DATAFLOW_SIDE_EFFECTING


def _peer(k):
    x, y, c = _place()
    dx, dy, dc = _REL[k]
    return (1 - x if dx else x, 1 - y if dy else y, 1 - c if dc else c)


def _exchange_start(parts, *, name):
    n = len(parts)

    def body(*refs):
        ins, lands = refs[:n], refs[n:2 * n]
        send_sems, recv_sems, token = refs[2 * n], refs[2 * n + 1], refs[-1]
        x, y, c = _place()
        me = 4 * x + 2 * y + c
        for a in range(n):
            for k in range(len(_REL)):
                px, py, pc = _peer(k)
                pltpu.make_async_remote_copy(
                    src_ref=ins[a].at[4 * px + 2 * py + pc], dst_ref=lands[a].at[me],
                    send_sem=send_sems.at[len(_REL) * a + k], recv_sem=recv_sems.at[len(_REL) * a + k],
                    device_id=(px, py, pc), device_id_type=MESH).start()
        token[...] = jnp.zeros_like(token)

    sem = pltpu.SemaphoreType.DMA((len(_REL) * n,))
    bufs = [pltpu.HBM(p.shape, p.dtype) for p in parts]
    outs = pl.pallas_call(
        body, name=name,
        out_shape=(sem, sem, *bufs, *bufs, jax.ShapeDtypeStruct((8, LANE), F32)),
        in_specs=(_HBM,) * (2 * n), out_specs=(_SEM, _SEM, *(_HBM,) * (2 * n), pl.BlockSpec(memory_space=pltpu.VMEM)),
        input_output_aliases={i: 2 + i for i in range(2 * n)},
        compiler_params=pltpu.CompilerParams(has_side_effects=_EFFECT),
    )(*[pltpu.with_memory_space_constraint(p, pltpu.HBM) for p in parts],
      *[pltpu.with_memory_space_constraint(lax.empty(p.shape, p.dtype), pltpu.HBM) for p in parts])
    return outs[0], outs[1], outs[2:2 + n], outs[2 + n:2 + 2 * n], outs[-1]


def _exchange_wait(send_sems, recv_sems, parts, lands, after, *, name):
    n = len(parts)

    def body(*refs):
        ins, lands_ = refs[:n], refs[n:2 * n]
        ssem, rsem = refs[2 * n], refs[2 * n + 1]
        for a in range(n):
            for k in range(len(_REL)):
                px, py, pc = _peer(k)
                p = 4 * px + 2 * py + pc
                cp = pltpu.make_async_remote_copy(
                    src_ref=ins[a].at[p], dst_ref=lands_[a].at[p],
                    send_sem=ssem.at[len(_REL) * a + k], recv_sem=rsem.at[len(_REL) * a + k],
                    device_id=(px, py, pc), device_id_type=MESH)
                cp.wait_send()
                cp.wait_recv()

    bufs = [pltpu.HBM(p.shape, p.dtype) for p in parts]
    outs = pl.pallas_call(
        body, name=name, out_shape=(*bufs, *bufs),
        in_specs=(*(_HBM,) * (2 * n), _SEM, _SEM, pl.BlockSpec(memory_space=pl.ANY)), out_specs=(_HBM,) * (2 * n),
        input_output_aliases={i: i for i in range(2 * n)},
        compiler_params=pltpu.CompilerParams(has_side_effects=_EFFECT),
    )(*parts, *lands, send_sems, recv_sems, after)
    return outs[:n], outs[n:]


WEIGHTS = ('norm_w', 'w_in', 'gate_b', 'sgu_norm_g', 'sgu_norm_b', 'sgu_w', 'sgu_b', 'conv_w', 'conv_b', 'dt_bias', 'A_log',
           'D_skip', 'ssd_norm_w', 'w_out', 'final_norm_w')
SHARDED = ('w_in', 'conv_w', 'w_out')
PACK_ROW = 8 * LANE


def _constants():
    tri = np.tril(np.ones((CHUNK, CHUNK), np.float32))
    expand = np.zeros((DT_W, D), np.float32)
    for h in range(HEADS):
        expand[h, h * HEADDIM:(h + 1) * HEADDIM] = 1.0
    sel = np.zeros((D, LANE), np.float32)
    for g in range(SGU_GROUPS):
        sel[g * LANE:(g + 1) * LANE, g] = 1.0
    pos_chunk = np.arange(SGU_BLOCK) // CHUNK
    mask = (pos_chunk[None, :] <= pos_chunk[:, None]).astype(np.float32)
    shift = np.zeros(((CONV_K - 1) * CHUNK, HALO_BLK + CHUNK), np.float32)
    for kk in range(CONV_K - 1):
        for t in range(CHUNK):
            shift[kk * CHUNK + t, HALO_BLK - (CONV_K - 1) + t + kk] = 1.0
    return dict(tri=jnp.asarray(tri, BF16), triT=jnp.asarray(tri.T.copy(), BF16), expand=jnp.asarray(np.tile(expand, (3, 1)), BF16),
                shift=jnp.asarray(shift, BF16),
                expandT=jnp.asarray(expand.T.copy(), BF16), sel=jnp.asarray(sel), mask=jnp.asarray(mask))


def _to_shards(segs):
    starts = np.cumsum([0] + [n for _, n in segs])
    assert starts[-1] == W_IN
    slabs = []
    for k in range(N_DEV):
        pieces = []
        for (s, n), s0 in zip(segs, starts[:-1]):
            lo, hi = max(k * SHARD_IN, s0), min((k + 1) * SHARD_IN, s0 + n)
            if lo < hi:
                pieces.append(s[lo - s0:hi - s0])
        slabs.append(jnp.concatenate(pieces, axis=0))
    return jnp.stack(slabs)


def _local_step(x2, tgt, wpT, wout, cw, p, exchange_small, exchange):
    S = x2.shape[0]
    k = _constants()
    xn = _norm_fwd(x2, p['norm_w'], tm=min(512, S))
    proj = _matmul(xn, wpT, trans_b=True, out_dtype=BF16, tm=min(1024, S), tn=2048, tk=D, name="in_proj")
    wm32 = p['sgu_w'][0] * k['mask']
    wm = wm32.astype(BF16)
    wmT = jnp.swapaxes(wm32, 1, 2).astype(BF16)
    bias_full = jnp.repeat(p['sgu_b'][0].T, LANE, axis=1)
    tm_sgu = min(256, S)
    ya = _sgu_fwd(proj, p['sgu_norm_g'], p['sgu_norm_b'], wm, bias_full, tm=tm_sgu)
    pad32 = lambda a: jnp.pad(a, ((0, 0), (0, DT_W - HEADS)))
    dtb_p, alog_p = pad32(p['dt_bias']), pad32(p['A_log'])
    d_exp = jnp.repeat(p['D_skip'], HEADDIM, axis=1)
    ssd_args = (cw, p['conv_b'], dtb_p, alog_p, d_exp, p['ssd_norm_w'])
    y, yb, states = _ssd_fwd(proj, *ssd_args, k['tri'], k['expand'], k['shift'])
    dh, dhb, mb, dya, dyb, dgl, loss, dfw, dgb = _head(
        x2, ya, yb, proj, tgt, p['gate_b'], wout, p['final_norm_w'][None, :], tm=min(256, S))
    dsgu, dws, dbsT, dsg, dsb = _sgu_bwd(proj, dya, p['sgu_norm_g'], p['sgu_norm_b'], wm, wmT, bias_full, k['mask'], k['sel'],
                                         tm=tm_sgu)
    dssd, dcw, dcb, ddtb, dalog, dD, dnw = _ssd_bwd(proj, dyb, y, states, *ssd_args, k['tri'], k['triT'], k['expand'], k['expandT'],
                                                    k['shift'])
    grads = dict(
        gate_b=dgb[0:1], sgu_norm_g=dsg[0:1], sgu_norm_b=dsb[0:1], sgu_w=dws[None],
        sgu_b=dbsT[:, :SGU_GROUPS].T[None], conv_w=dcw[0:CONV_K][None], conv_b=dcb[0:1], dt_bias=ddtb[0:1, :HEADS],
        A_log=dalog[0:1, :HEADS], D_skip=dD[0:1, :HEADS], ssd_norm_w=dnw[0:1], final_norm_w=dfw[0])
    token = exchange_small(loss[0, 0], grads)
    tk = min(4096, S)
    tn = 1024
    dwT_sgu = _matmul(dsgu, xn, trans_a=True, out_dtype=BF16, tm=1024, tn=tn, tk=tk, after=token, name="dw_in_sgu")
    dwT_gate = _matmul(dgl, xn, trans_a=True, out_dtype=BF16, tm=1024, tn=tn, tk=tk, name="dw_in_gate")
    dwT_ssd = _matmul(dssd, xn, trans_a=True, out_dtype=BF16, tm=1024, tn=tn, tk=tk, name="dw_in_ssd")
    dw_out = _matmul(mb, dhb, trans_a=True, out_dtype=BF16, tm=1024, tn=tn, tk=tk, name="dw_out")
    token = exchange([(dwT_sgu, SEG_SGU[1]), (dwT_ssd, W_IN - SEG_SSD[0]), (dwT_gate, SEG_GATE[1])], dw_out)
    tm = min(1024, S)
    dxn = _matmul(dsgu, wpT, tm=tm, tn=tn, tk=3072, after=token, name="dxn_sgu")
    dxn = _matmul(dgl, wpT, b_koff=SEG_GATE[0] // 2048, tm=tm, tn=tn, tk=2048, add=dxn, name="dxn_gate")
    dxn = _matmul(dssd, wpT, b_koff=SEG_SSD[0] // 2048, tm=tm, tn=tn, tk=2048, add=dxn, name="dxn_ssd")
    grad_x, dnorm = _norm_bwd(x2, p['norm_w'], dxn, dh, tm=min(512, S))
    return grad_x, dnorm[0:1]


def _pack(arrs):
    rows, offs, r = [], [], 0
    for a in arrs:
        n = a.size
        nr = -(-n // PACK_ROW) * 8
        rows.append(jnp.pad(a.reshape(-1).astype(F32), (0, nr * LANE - n)).reshape(nr, LANE))
        offs.append(r)
        r += nr
    return jnp.concatenate(rows, axis=0), offs


def kernel(x, norm_w, w_in, gate_b, sgu_norm_g, sgu_norm_b, sgu_w, sgu_b, conv_w, conv_b, dt_bias, A_log, D_skip, ssd_norm_w, w_out, final_norm_w, loss_target, m_norm_w, m_w_in, m_gate_b, m_sgu_norm_g, m_sgu_norm_b, m_sgu_w, m_sgu_b, m_conv_w, m_conv_b, m_dt_bias, m_A_log, m_D_skip, m_ssd_norm_w, m_w_out, m_final_norm_w, v_norm_w, v_w_in, v_gate_b, v_sgu_norm_g, v_sgu_norm_b, v_sgu_w, v_sgu_b, v_conv_w, v_conv_b, v_dt_bias, v_A_log, v_D_skip, v_ssd_norm_w, v_w_out, v_final_norm_w):
    w = dict(norm_w=norm_w, w_in=w_in, gate_b=gate_b, sgu_norm_g=sgu_norm_g, sgu_norm_b=sgu_norm_b, sgu_w=sgu_w, sgu_b=sgu_b,
             conv_w=conv_w, conv_b=conv_b, dt_bias=dt_bias, A_log=A_log, D_skip=D_skip, ssd_norm_w=ssd_norm_w, w_out=w_out,
             final_norm_w=final_norm_w)
    m = dict(norm_w=m_norm_w, w_in=m_w_in, gate_b=m_gate_b, sgu_norm_g=m_sgu_norm_g, sgu_norm_b=m_sgu_norm_b, sgu_w=m_sgu_w,
             sgu_b=m_sgu_b, conv_w=m_conv_w, conv_b=m_conv_b, dt_bias=m_dt_bias, A_log=m_A_log, D_skip=m_D_skip,
             ssd_norm_w=m_ssd_norm_w, w_out=m_w_out, final_norm_w=m_final_norm_w)
    v = dict(norm_w=v_norm_w, w_in=v_w_in, gate_b=v_gate_b, sgu_norm_g=v_sgu_norm_g, sgu_norm_b=v_sgu_norm_b, sgu_w=v_sgu_w,
             sgu_b=v_sgu_b, conv_w=v_conv_w, conv_b=v_conv_b, dt_bias=v_dt_bias, A_log=v_A_log, D_skip=v_D_skip,
             ssd_norm_w=v_ssd_norm_w, w_out=v_w_out, final_norm_w=v_final_norm_w)
    me = 4 * lax.axis_index("x") + 2 * lax.axis_index("y") + lax.axis_index("c")
    shard_cw = XBC_W // N_DEV

    tpose = lambda a: jnp.swapaxes(a[0], 0, 1)
    wT = tpose(w_in).astype(BF16)
    first_group = (GROUP - (me * SHARD_IN) % GROUP) % GROUP
    window = lax.dynamic_slice(jnp.pad(wT, ((0, GROUP), (0, 0))), (first_group, 0), (INTERIOR, D))
    wpT, g_out, g_cw, heads, tails = _gather_weights(window, wT[:GROUP], wT[SHARD_IN - GROUP:], w_out[0].astype(BF16),
                                                     conv_w[0], jnp.zeros((W_ROWS - W_IN, D), BF16))
    wpT = _patch_straddlers(wpT, heads, tails)
    wout_full = g_out.reshape(D, D)
    cw_full = jnp.swapaxes(g_cw, 0, 1).reshape(CONV_K, XBC_W)

    flight = {}

    small = [n for n in WEIGHTS if n not in SHARDED and n != 'norm_w']
    early = {}

    def exchange_small(loss_part, grads):
        early['packed'], early['offs'] = _pack([grads[n] for n in small] + [loss_part, grads['conv_w']])
        parts = [jnp.broadcast_to(early['packed'][None], (N_DEV,) + early['packed'].shape)]
        early['sems'], early['rsems'], early['parts'], early['lands'], token = _exchange_start(parts, name="small_start")
        return token

    def exchange(dw_inT_segs, dw_out):
        parts = [_to_shards(dw_inT_segs), dw_out.reshape(N_DEV, D // N_DEV, D)]
        flight['sems'], flight['rsems'], flight['parts'], flight['lands'], token = _exchange_start(parts, name="exchange_start")
        return token

    grad_x, dnorm = _local_step(x[0], loss_target[0], wpT, wout_full, cw_full, w, exchange_small, exchange)
    _, (land_small,) = _exchange_wait(early['sems'], early['rsems'], early['parts'], early['lands'], grad_x, name="small_wait")
    (own_in, own_out), (land_in, land_out) = _exchange_wait(
        flight['sems'], flight['rsems'], flight['parts'], flight['lands'], grad_x, name="exchange_wait")
    me_arr = jnp.reshape(me, (1,)).astype(jnp.int32)
    res = {}
    res['w_in'] = [jnp.swapaxes(o, 0, 1) for o in _adamw_own(
        me_arr, own_in, land_in, tpose(w_in), tpose(m_w_in), tpose(v_w_in), tr=SHARD_IN, tc=256, name="adamw_w_in")]
    res['w_out'] = _adamw_own(me_arr, own_out, land_out, w_out[0], m_w_out[0], v_w_out[0], tr=128, tc=D, name="adamw_w_out")

    (norm_parts,) = _all_gather([_pack([dnorm])[0]], name="gather_norm")
    norm_outs = _adamw(norm_parts, *[_pack([d['norm_w']])[0] for d in (w, m, v)], tr=norm_parts.shape[1], name="adamw_norm")
    res['norm_w'] = [o.reshape(-1)[:D].reshape(w['norm_w'].shape) for o in norm_outs]

    offs = early['offs']
    gathered = lax.dynamic_update_slice(land_small, early['packed'][None], (me, 0, 0))
    off_loss, off_cw = offs[-2], offs[-1]
    cw_parts = gathered[:, off_cw:, :].reshape(N_DEV, CONV_K, XBC_W)
    cw_parts = lax.dynamic_slice_in_dim(cw_parts, me * shard_cw, shard_cw, axis=2)
    cw_rows = _pack([cw_parts[0]])[0].shape[0]
    cw_parts = jnp.pad(cw_parts.reshape(N_DEV, -1), ((0, 0), (0, cw_rows * LANE - CONV_K * shard_cw))).reshape(N_DEV, cw_rows, LANE)
    parts = jnp.concatenate([gathered[:, :off_cw, :], cw_parts], axis=1)
    zero = jnp.zeros((), F32)
    packs = [_pack([d[n] for n in small] + [zero, d['conv_w']])[0] for d in (w, m, v)]
    outs = _adamw(parts, *packs, tr=parts.shape[1], name="adamw_small")

    def unpack(o, name):
        if name == 'conv_w':
            return o[off_cw:off_cw + cw_rows].reshape(-1)[:CONV_K * shard_cw].reshape(w['conv_w'].shape)
        r0 = offs[small.index(name)]
        n = w[name].size
        return o[r0:r0 + -(-n // PACK_ROW) * 8].reshape(-1)[:n].reshape(w[name].shape)

    for n in small + ['conv_w']:
        res[n] = [unpack(o, n) for o in outs]
    for n in ('w_in', 'w_out'):
        res[n] = [o[None] for o in res[n]]
    loss = outs[0][off_loss, 0]
    return (loss, grad_x[None], *[res[n][0] for n in WEIGHTS], *[res[n][1] for n in WEIGHTS],
            *[res[n][2] for n in WEIGHTS], *[res[n][3] for n in WEIGHTS])
```

```python
import functools

import numpy as np
import jax
import jax.numpy as jnp
from jax import lax
from jax.experimental import pallas as pl
from jax.experimental.pallas import tpu as pltpu

F32 = jnp.float32
BF16 = jnp.bfloat16
HI = lax.Precision.HIGHEST
MESH = pl.DeviceIdType.MESH

D = 2048
EPS = 1e-5
SGU_BLOCK = 128
SGU_GROUPS = 16
CHUNK = 64
HEADS = 32
HEADDIM = 64
SSD_GROUPS = 4
GROUP_W = D // SSD_GROUPS
STATE = 128
CONV_K = 4
XBC_W = D + 2 * SSD_GROUPS * STATE
W_IN = 15392
N_DEV = 8
SHARD_IN = W_IN // N_DEV
ADAM_LR, ADAM_B1, ADAM_B2, ADAM_EPS, ADAM_WD, ADAM_STEP = 0.001, 0.9, 0.999, 1e-08, 0.01, 10

REF_SGU_END = 3 * D
REF_GATE_START = W_IN - 2 * D
LANE = 128
DT_W = LANE
OFF_U, OFF_V, OFF_ZA, OFF_G0, OFF_G1, OFF_ZB = (i * D for i in range(6))
OFF_XBC = OFF_ZB + D
OFF_DT = OFF_XBC + XBC_W
SEG_SGU = (OFF_U, 3 * D)
SEG_GATE = (OFF_G0, 2 * D)
SEG_SSD = (OFF_ZB, D + XBC_W + DT_W)
WP = SEG_SSD[0] + SEG_SSD[1]
SSD_PAD_W = 3 * D
VMEM_BYTES = 64 * 1024 * 1024
VMEM_LIMIT = VMEM_BYTES - 8 * 1024 * 1024


def _cp(sem=None, vmem=VMEM_LIMIT):
    return pltpu.CompilerParams(dimension_semantics=sem, vmem_limit_bytes=vmem)


def _sigmoid(x):
    return 1.0 / (1.0 + jnp.exp(-x))


def _softplus(x):
    return jnp.maximum(x, 0.0) + jnp.log(1.0 + jnp.exp(-jnp.abs(x)))


def _dot(a, b, precision=None):
    return jnp.dot(a, b, preferred_element_type=F32, precision=precision)


def _dot_nt(a, b, precision=None):
    return lax.dot_general(a, b, (((1,), (1,)), ((), ())), preferred_element_type=F32, precision=precision)


def _dot_tn(a, b, precision=None):
    return lax.dot_general(a, b, (((0,), (0,)), ((), ())), preferred_element_type=F32, precision=precision)


def _split3(a):
    hi = a.astype(BF16)
    r = a - hi.astype(F32)
    mid = r.astype(BF16)
    return hi, mid, (r - mid.astype(F32)).astype(BF16)


def _sel_right(a, sel01):
    m = a.shape[0]
    r = _dot(jnp.concatenate(_split3(a), axis=0), sel01)
    return (r[0:m] + r[m:2 * m]) + r[2 * m:3 * m]


def _sel_right_k(a, sel01_x3):
    return _dot(jnp.concatenate(_split3(a), axis=1), sel01_x3)


def _sel_left(sel01, a):
    n = a.shape[1]
    r = _dot(sel01, jnp.concatenate(_split3(a), axis=1))
    return (r[:, 0:n] + r[:, n:2 * n]) + r[:, 2 * n:3 * n]


def _matmul(a, b, *, trans_a=False, trans_b=False, b_koff=0, out_dtype=F32, tm, tn, tk, add=None, after=None, name):
    K, M = a.shape if trans_a else a.shape[::-1]
    N = b.shape[0] if trans_b else b.shape[1]
    assert M % tm == 0 and N % tn == 0 and K % tk == 0 and not (trans_a and trans_b), (name, M, N, K, tm, tn, tk)
    nk = K // tk

    def body(*refs):
        a_ref, b_ref = refs[:2]
        add_ref = refs[2] if add is not None else None
        o_ref, acc_ref = refs[-2:]
        k = pl.program_id(2)
        if trans_a:
            part = _dot_tn(a_ref[...], b_ref[...])
        else:
            part = _dot_nt(a_ref[...], b_ref[...]) if trans_b else _dot(a_ref[...], b_ref[...])

        def result(r):
            if add_ref is not None:
                r = r + add_ref[...]
            return r.astype(out_dtype)

        if nk == 1:
            o_ref[...] = result(part)
        else:
            @pl.when(k == 0)
            def _():
                acc_ref[...] = part

            @pl.when(jnp.logical_and(k > 0, k < nk - 1))
            def _():
                acc_ref[...] += part

            @pl.when(k == nk - 1)
            def _():
                o_ref[...] = result(acc_ref[...] + part)

    in_specs = [pl.BlockSpec((tk, tm), lambda i, j, k: (k, i)) if trans_a else pl.BlockSpec((tm, tk), lambda i, j, k: (i, k)),
                pl.BlockSpec((tn, tk), lambda i, j, k: (j, k)) if trans_b else pl.BlockSpec((tk, tn), lambda i, j, k: (k + b_koff, j))]
    args = [a, b]
    if add is not None:
        in_specs.append(pl.BlockSpec((tm, tn), lambda i, j, k: (i, j)))
        args.append(add)
    if after is not None:
        in_specs.append(pl.BlockSpec(memory_space=pl.ANY))
        args.append(after)
    return pl.pallas_call(
        body, name=name, grid=(M // tm, N // tn, nk), in_specs=in_specs,
        out_specs=pl.BlockSpec((tm, tn), lambda i, j, k: (i, j)),
        out_shape=jax.ShapeDtypeStruct((M, N), out_dtype),
        scratch_shapes=[pltpu.VMEM((tm, tn), F32)],
        compiler_params=_cp(("parallel", "parallel", "arbitrary")),
    )(*args)


def _in_proj(x, w, wpT, *, tm, tn):
    S = x.shape[0]
    N = wpT.shape[0]
    assert S % tm == 0 and N % tn == 0, (S, N, tm, tn)

    def body(x_ref, w_ref, b_ref, xn_ref, o_ref, xs_ref):
        @pl.when(pl.program_id(1) == 0)
        def _():
            xv = x_ref[...]
            r = lax.rsqrt(jnp.mean(xv * xv, axis=-1, keepdims=True) + EPS)
            xs = (xv * r * w_ref[...]).astype(BF16)
            xs_ref[...] = xs
            xn_ref[...] = xs

        o_ref[...] = _dot_nt(xs_ref[...], b_ref[...]).astype(BF16)

    return pl.pallas_call(
        body, name="in_proj", grid=(S // tm, N // tn),
        in_specs=[pl.BlockSpec((tm, D), lambda i, j: (i, 0)), pl.BlockSpec((1, D), lambda i, j: (0, 0)),
                  pl.BlockSpec((tn, D), lambda i, j: (j, 0))],
        out_specs=[pl.BlockSpec((tm, D), lambda i, j: (i, 0)), pl.BlockSpec((tm, tn), lambda i, j: (i, j))],
        out_shape=[jax.ShapeDtypeStruct((S, D), BF16), jax.ShapeDtypeStruct((S, N), BF16)],
        scratch_shapes=[pltpu.VMEM((tm, D), BF16)],
        compiler_params=_cp(("parallel", "arbitrary")),
    )(x, w, wpT)


def _norm_bwd(x, w, dxn, dh, *, tm):
    S = x.shape[0]

    def body(x_ref, w_ref, dxn_ref, dh_ref, gx_ref, dw_ref):
        xv = x_ref[...]
        r = lax.rsqrt(jnp.mean(xv * xv, axis=-1, keepdims=True) + EPS)
        xh = xv * r
        dxn_v = dxn_ref[...]
        dxh = dxn_v * w_ref[...]
        gx_ref[...] = dh_ref[...] + r * (dxh - xh * jnp.mean(dxh * xh, axis=-1, keepdims=True))

        @pl.when(pl.program_id(0) == 0)
        def _():
            dw_ref[...] = jnp.zeros_like(dw_ref)

        dw_ref[0:1, :] += jnp.sum(dxn_v * xh, axis=0, keepdims=True)

    row = pl.BlockSpec((tm, D), lambda i: (i, 0))
    return pl.pallas_call(
        body, name="norm_bwd", grid=(S // tm,),
        in_specs=[row, pl.BlockSpec((1, D), lambda i: (0, 0)), row, row],
        out_specs=[row, pl.BlockSpec((8, D), lambda i: (0, 0))],
        out_shape=[jax.ShapeDtypeStruct((S, D), F32), jax.ShapeDtypeStruct((8, D), F32)],
        compiler_params=_cp(("arbitrary",)),
    )(x, w, dxn, dh)


def _sgu_core(u_ref, v_ref, z_ref, g_ref, b_ref, wm_ref, bias_ref, vnb_ref, mixed_ref, tm):
    v = v_ref[...].astype(F32)
    mu = jnp.mean(v, axis=-1, keepdims=True)
    vc = v - mu
    rs = lax.rsqrt(jnp.mean(vc * vc, axis=-1, keepdims=True) + EPS)
    vh = vc * rs
    vnb_ref[...] = (vh * g_ref[...] + b_ref[...]).astype(BF16)
    for blk in range(tm // SGU_BLOCK):
        rows = pl.ds(blk * SGU_BLOCK, SGU_BLOCK)
        for gi in range(SGU_GROUPS):
            cols = pl.ds(gi * LANE, LANE)
            mixed_ref[rows, cols] = _dot(wm_ref[gi], vnb_ref[rows, cols]) + bias_ref[:, cols]
    return vh, rs


def _sgu_fwd(proj, g, b, wm, bias_full, *, tm):
    S = proj.shape[0]

    def body(u_ref, v_ref, z_ref, g_ref, b_ref, wm_ref, bias_ref, y_ref, vnb_ref, mixed_ref):
        _sgu_core(u_ref, v_ref, z_ref, g_ref, b_ref, wm_ref, bias_ref, vnb_ref, mixed_ref, tm)
        z = z_ref[...].astype(F32)
        y_ref[...] = (u_ref[...].astype(F32) * mixed_ref[...] * (z * _sigmoid(z))).astype(BF16)

    seg = lambda off: pl.BlockSpec((tm, D), lambda i: (i, off // D))
    full = lambda a: pl.BlockSpec(a.shape, lambda i: (0,) * a.ndim)
    return pl.pallas_call(
        body, name="sgu_fwd", grid=(S // tm,),
        in_specs=[seg(OFF_U), seg(OFF_V), seg(OFF_ZA), full(g), full(b), full(wm), full(bias_full)],
        out_specs=pl.BlockSpec((tm, D), lambda i: (i, 0)),
        out_shape=jax.ShapeDtypeStruct((S, D), BF16),
        scratch_shapes=[pltpu.VMEM((tm, D), BF16), pltpu.VMEM((tm, D), F32)],
        compiler_params=_cp(("parallel",)),
    )(proj, proj, proj, g, b, wm, bias_full)


def _sgu_bwd(proj, dy, g, b, wm, wmT, bias_full, mask, sel, *, tm):
    S = proj.shape[0]
    nsteps = S // tm

    def body(u_ref, v_ref, z_ref, dy_ref, g_ref, b_ref, wm_ref, wmT_ref, bias_ref, mask_ref, sel_ref,
             dp_ref, dws_ref, dbs_ref, dg_ref, db_ref, vnb_ref, mixed_ref, dmb_ref, dvn_ref, dbias_ref):
        i = pl.program_id(0)

        @pl.when(i == 0)
        def _():
            dws_ref[...] = jnp.zeros_like(dws_ref)
            dg_ref[...] = jnp.zeros_like(dg_ref)
            db_ref[...] = jnp.zeros_like(db_ref)
            dbias_ref[...] = jnp.zeros_like(dbias_ref)

        vh, rs = _sgu_core(u_ref, v_ref, z_ref, g_ref, b_ref, wm_ref, bias_ref, vnb_ref, mixed_ref, tm)
        u = u_ref[...].astype(F32)
        z = z_ref[...].astype(F32)
        dy_v = dy_ref[...].astype(F32)
        mixed = mixed_ref[...]
        sg = _sigmoid(z)
        sz = z * sg
        dp_ref[:, 0:D] = (dy_v * mixed * sz).astype(BF16)
        dp_ref[:, 2 * D:3 * D] = (dy_v * u * mixed * (sg * (1.0 + z * (1.0 - sg)))).astype(BF16)
        dmixed = dy_v * u * sz
        dmb_ref[...] = dmixed.astype(BF16)
        for blk in range(tm // SGU_BLOCK):
            dbias_ref[...] += dmixed[blk * SGU_BLOCK:(blk + 1) * SGU_BLOCK, :]
        for blk in range(tm // SGU_BLOCK):
            rows = pl.ds(blk * SGU_BLOCK, SGU_BLOCK)
            for gi in range(SGU_GROUPS):
                cols = pl.ds(gi * LANE, LANE)
                dm = dmb_ref[rows, cols]
                dvn_ref[rows, cols] = _dot(wmT_ref[gi], dm)
                dws_ref[gi] += _dot_nt(dm, vnb_ref[rows, cols])
        dvn = dvn_ref[...]
        dg_ref[0:1, :] += jnp.sum(dvn * vh, axis=0, keepdims=True)
        db_ref[0:1, :] += jnp.sum(dvn, axis=0, keepdims=True)
        dvh = dvn * g_ref[...]
        dv = rs * (dvh - jnp.mean(dvh, axis=-1, keepdims=True) - vh * jnp.mean(dvh * vh, axis=-1, keepdims=True))
        dp_ref[:, D:2 * D] = dv.astype(BF16)

        @pl.when(i == nsteps - 1)
        def _():
            for gi in range(SGU_GROUPS):
                dws_ref[gi] = dws_ref[gi] * mask_ref[...]
            dbs_ref[...] = _dot(dbias_ref[...], sel_ref[...], precision=HI)

    seg = lambda off: pl.BlockSpec((tm, D), lambda i: (i, off // D))
    full = lambda a: pl.BlockSpec(a.shape, lambda i: (0,) * a.ndim)
    return pl.pallas_call(
        body, name="sgu_bwd", grid=(nsteps,),
        in_specs=[seg(OFF_U), seg(OFF_V), seg(OFF_ZA), pl.BlockSpec((tm, D), lambda i: (i, 0)),
                  full(g), full(b), full(wm), full(wmT), full(bias_full), full(mask), full(sel)],
        out_specs=[pl.BlockSpec((tm, 3 * D), lambda i: (i, 0)),
                   pl.BlockSpec((SGU_GROUPS, SGU_BLOCK, SGU_BLOCK), lambda i: (0, 0, 0)),
                   pl.BlockSpec((SGU_BLOCK, LANE), lambda i: (0, 0)),
                   pl.BlockSpec((8, D), lambda i: (0, 0)), pl.BlockSpec((8, D), lambda i: (0, 0))],
        out_shape=[jax.ShapeDtypeStruct((S, 3 * D), BF16),
                   jax.ShapeDtypeStruct((SGU_GROUPS, SGU_BLOCK, SGU_BLOCK), F32),
                   jax.ShapeDtypeStruct((SGU_BLOCK, LANE), F32),
                   jax.ShapeDtypeStruct((8, D), F32), jax.ShapeDtypeStruct((8, D), F32)],
        scratch_shapes=[pltpu.VMEM((tm, D), BF16), pltpu.VMEM((tm, D), F32), pltpu.VMEM((tm, D), BF16),
                        pltpu.VMEM((tm, D), F32), pltpu.VMEM((SGU_BLOCK, D), F32)],
        compiler_params=_cp(("arbitrary",)),
    )(proj, proj, proj, dy, g, b, wm, wmT, bias_full, mask, sel)


SSD_T = 2 * CHUNK
HALO = 8
HALO_BLK = 16


def _pair_masks():
    row = lax.broadcasted_iota(jnp.int32, (CHUNK, LANE), 0)
    lane = lax.broadcasted_iota(jnp.int32, (CHUNK, LANE), 1)
    pos = jnp.where(lane >= CHUNK, lane - CHUNK, lane)
    diag = (row == pos).astype(F32)
    causal = row >= pos
    lo = (lane < CHUNK).astype(F32)
    return diag, causal, lo, 1.0 - lo


def _ssd_chunk_fwd(c, ext_ref, shift_ref, dt_ref, cw_ref, cb_ref, dtb_ref, alog_ref, tri_ref, exp_ref):
    r0 = c * CHUNK
    win = ext_ref[pl.ds(r0, HALO_BLK + CHUNK), :]
    sh = _dot(shift_ref[...], win)
    taps = [sh[k * CHUNK:(k + 1) * CHUNK] for k in range(CONV_K - 1)] + [win[HALO_BLK:].astype(F32)]
    pre = cb_ref[...] + sum(cw_ref[k:k + 1, :] * taps[k] for k in range(CONV_K))
    sg = _sigmoid(pre)
    xc = pre * sg
    dtr = dt_ref[pl.ds(r0, CHUNK), :].astype(F32) + dtb_ref[...]
    dtv = _softplus(dtr)
    A = -jnp.exp(alog_ref[...])
    acs = _sel_left(tri_ref[...], dtv * A)
    both = _sel_right_k(jnp.concatenate([acs, dtv], axis=0), exp_ref[...])
    E, dtE = both[0:CHUNK], both[CHUNK:2 * CHUNK]
    return dict(taps=taps, pre=pre, sg=sg, xc=xc, dtr=dtr, dtv=dtv, A=A, E=E, dtE=dtE)


def _ssd_fwd(proj, conv_w, conv_b, dtb_p, alog_p, d_exp, norm_w, tri, expand, shift):
    S = proj.shape[0]
    T = SSD_T
    nsteps = S // T
    ncl = T // CHUNK

    def body(zb_ref, xbc_ref, halo_ref, dt_ref, cw_ref, cb_ref, dtb_ref, alog_ref, dexp_ref, nw_ref, tri_ref, exp_ref, shift_ref,
             y_ref, yb_ref, st_ref, ht_ref, ext_ref):
        i = pl.program_id(0)

        @pl.when(i == 0)
        def _():
            ht_ref[...] = jnp.zeros_like(ht_ref)
            ext_ref[0:HALO_BLK, :] = jnp.zeros((HALO_BLK, XBC_W), BF16)

        @pl.when(i > 0)
        def _():
            ext_ref[0:HALO_BLK, :] = halo_ref[...]

        ext_ref[HALO_BLK:HALO_BLK + T, :] = xbc_ref[...]
        diag, causal, lo, hi = _pair_masks()
        for c in range(ncl):
            q = _ssd_chunk_fwd(c, ext_ref, shift_ref, dt_ref, cw_ref, cb_ref, dtb_ref, alog_ref, tri_ref, exp_ref)
            rows = pl.ds(c * CHUNK, CHUNK)
            xc, E, dtE = q["xc"], q["E"], q["dtE"]
            xs = xc[:, 0:D]
            total = E[CHUNK - 1:CHUNK, :]
            x_dt = xs * dtE
            eE = jnp.exp(E)
            xw = x_dt * jnp.exp(total - E)
            st_ref[c] = ht_ref[...]
            for g in range(SSD_GROUPS):
                gc = slice(g * GROUP_W, (g + 1) * GROUP_W)
                Bg = xc[:, D + g * STATE:D + (g + 1) * STATE].astype(BF16)
                Cg = xc[:, D + SSD_GROUPS * STATE + g * STATE:D + SSD_GROUPS * STATE + (g + 1) * STATE].astype(BF16)
                cb2 = _dot_nt(Cg, jnp.concatenate([Bg, Bg], axis=0))
                htg = ht_ref[:, gc]
                y_ref[rows, gc] = eE[:, gc] * _dot(Cg, htg.astype(BF16)) + xs[:, gc] * dexp_ref[:, gc]
                for jj in range(GROUP_W // LANE):
                    pc = slice(g * GROUP_W + jj * LANE, g * GROUP_W + (jj + 1) * LANE)
                    Ej = E[:, pc]
                    e2 = jnp.sum(Ej * diag, axis=0, keepdims=True)
                    Mp = cb2 * jnp.exp(jnp.where(causal, Ej - e2, -1e30))
                    xj = x_dt[:, pc]
                    xbd = jnp.concatenate([xj * lo, xj * hi], axis=0).astype(BF16)
                    y_ref[rows, pc] += _dot(Mp.astype(BF16), xbd)
                ht_ref[:, gc] = jnp.exp(total[:, gc]) * htg + _dot_tn(Bg, xw[:, gc].astype(BF16))
            zb = zb_ref[rows, :].astype(F32)
            hh = y_ref[rows, :] * (zb * _sigmoid(zb))
            for g in range(SSD_GROUPS):
                gc = slice(g * GROUP_W, (g + 1) * GROUP_W)
                hg = hh[:, gc]
                r = lax.rsqrt(jnp.mean(hg * hg, axis=-1, keepdims=True) + EPS)
                yb_ref[rows, gc] = (hg * r * nw_ref[:, gc]).astype(BF16)

    full = lambda a: pl.BlockSpec(a.shape, lambda i: (0,) * a.ndim)
    hb = T // HALO_BLK
    return pl.pallas_call(
        body, name="ssd_fwd", grid=(nsteps,),
        in_specs=[pl.BlockSpec((T, D), lambda i: (i, OFF_ZB // D)),
                  pl.BlockSpec((T, XBC_W), lambda i: (i, OFF_XBC // XBC_W)),
                  pl.BlockSpec((HALO_BLK, XBC_W), lambda i: (jnp.maximum(i * hb - 1, 0), OFF_XBC // XBC_W)),
                  pl.BlockSpec((T, DT_W), lambda i: (i, OFF_DT // DT_W)),
                  full(conv_w), full(conv_b), full(dtb_p), full(alog_p), full(d_exp), full(norm_w), full(tri), full(expand),
                  full(shift)],
        out_specs=[pl.BlockSpec((T, D), lambda i: (i, 0)), pl.BlockSpec((T, D), lambda i: (i, 0)),
                   pl.BlockSpec((ncl, STATE, D), lambda i: (i, 0, 0))],
        out_shape=[jax.ShapeDtypeStruct((S, D), F32), jax.ShapeDtypeStruct((S, D), BF16),
                   jax.ShapeDtypeStruct((S // CHUNK, STATE, D), F32)],
        scratch_shapes=[pltpu.VMEM((STATE, D), F32), pltpu.VMEM((HALO_BLK + T, XBC_W), BF16)],
        compiler_params=_cp(("arbitrary",)),
    )(proj, proj, proj, proj, conv_w, conv_b, dtb_p, alog_p, d_exp, norm_w, tri, expand, shift)


def _ssd_bwd(proj, dyb, y, states, conv_w, conv_b, dtb_p, alog_p, d_exp, norm_w, tri, triT, expand, expandT, shift):
    S = proj.shape[0]
    T = SSD_T
    nsteps = S // T
    ncl = T // CHUNK
    SSD_W = SSD_PAD_W

    def body(zb_ref, xbc_ref, halo_ref, dt_ref, dyb_ref, y_ref, st_ref, cw_ref, cb_ref, dtb_ref, alog_ref, dexp_ref, nw_ref,
             tri_ref, triT_ref, exp_ref, expT_ref, shift_ref,
             dp_ref, dcw_ref, dcb_ref, ddtb_ref, dalog_ref, dD_ref, dnw_ref,
             dht_ref, ext_ref, dpre_ref, dy_s, dE_s, dxdt_s, dxc_s, dDacc_ref, dAacc_ref):
        i = pl.program_id(0)

        @pl.when(i == 0)
        def _():
            for r in (dht_ref, dcw_ref, dcb_ref, ddtb_ref, dnw_ref, dDacc_ref, dAacc_ref):
                r[...] = jnp.zeros_like(r)
            dpre_ref[T:T + HALO_BLK, :] = jnp.zeros((HALO_BLK, XBC_W), F32)

        @pl.when(i == nsteps - 1)
        def _():
            ext_ref[0:HALO_BLK, :] = jnp.zeros((HALO_BLK, XBC_W), BF16)

        @pl.when(i < nsteps - 1)
        def _():
            ext_ref[0:HALO_BLK, :] = halo_ref[...]

        ext_ref[HALO_BLK:HALO_BLK + T, :] = xbc_ref[...]
        diag, causal, lo, hi = _pair_masks()
        last_row = (lax.broadcasted_iota(jnp.int32, (CHUNK, 1), 0) == CHUNK - 1).astype(F32)
        for c in reversed(range(ncl)):
            q = _ssd_chunk_fwd(c, ext_ref, shift_ref, dt_ref, cw_ref, cb_ref, dtb_ref, alog_ref, tri_ref, exp_ref)
            rows = pl.ds(c * CHUNK, CHUNK)
            pre, sg, xc, dtr, dtv, A, E, dtE = (q[k] for k in ("pre", "sg", "xc", "dtr", "dtv", "A", "E", "dtE"))
            xs = xc[:, 0:D]
            total = E[CHUNK - 1:CHUNK, :]
            x_dt = xs * dtE
            eE = jnp.exp(E)
            wdec = jnp.exp(total - E)
            zb = zb_ref[rows, :].astype(F32)
            yv = y_ref[rows, :]
            sgz = _sigmoid(zb)
            sz = zb * sgz
            hh = yv * sz
            for g in range(SSD_GROUPS):
                gc = slice(g * GROUP_W, (g + 1) * GROUP_W)
                hg = hh[:, gc]
                r = lax.rsqrt(jnp.mean(hg * hg, axis=-1, keepdims=True) + EPS)
                dyb_g = dyb_ref[rows, gc].astype(F32)
                dn = dyb_g * nw_ref[:, gc]
                dnw_ref[0:1, gc] += jnp.sum(dyb_g * hg * r, axis=0, keepdims=True)
                dy_s[:, gc] = r * dn - hg * (r * r * r) * jnp.mean(dn * hg, axis=-1, keepdims=True)
            dhh = dy_s[...]
            dp_ref[rows, 0:D] = (dhh * yv * (sgz * (1.0 + zb * (1.0 - sgz)))).astype(BF16)
            dy = dhh * sz
            dy_s[...] = dy
            dDacc_ref[0:1, :] += jnp.sum(dy * xs, axis=0, keepdims=True)
            dxc_s[:, 0:D] = dy * dexp_ref[...]
            for g in range(SSD_GROUPS):
                gc = slice(g * GROUP_W, (g + 1) * GROUP_W)
                bcol = slice(D + g * STATE, D + (g + 1) * STATE)
                ccol = slice(D + SSD_GROUPS * STATE + g * STATE, D + SSD_GROUPS * STATE + (g + 1) * STATE)
                Bg = xc[:, bcol].astype(BF16)
                Cg = xc[:, ccol].astype(BF16)
                B2 = jnp.concatenate([Bg, Bg], axis=0)
                cb2 = _dot_nt(Cg, B2)
                htg = st_ref[c, :, gc]
                htb = htg.astype(BF16)
                dhn = dht_ref[:, gc]
                dhnb = dhn.astype(BF16)
                dyg = dy[:, gc]
                eEg = eE[:, gc]
                wg = wdec[:, gc]
                xdg = x_dt[:, gc]
                CH = _dot(Cg, htb)
                dCHb = (dyg * eEg).astype(BF16)
                dC = _dot_nt(dCHb, htb)
                dl = jnp.exp(total[:, gc])
                dht_prev = _dot_tn(Cg, dCHb) + dl * dhn
                dtot = jnp.sum(dhn * htg, axis=0, keepdims=True) * dl
                dxw = _dot(Bg, dhnb)
                dB = _dot_nt((xdg * wg).astype(BF16), dhnb)
                dwd = dxw * xdg * wg
                dtot = dtot + jnp.sum(dwd, axis=0, keepdims=True)
                dE_s[:, gc] = dyg * eEg * CH - dwd + last_row * dtot
                dxdt_s[:, gc] = dxw * wg
                dcb2 = jnp.zeros((CHUNK, LANE), F32)
                for jj in range(GROUP_W // LANE):
                    pc = slice(g * GROUP_W + jj * LANE, g * GROUP_W + (jj + 1) * LANE)
                    Ej = E[:, pc]
                    e2 = jnp.sum(Ej * diag, axis=0, keepdims=True)
                    Lp = jnp.exp(jnp.where(causal, Ej - e2, -1e30))
                    Mp = cb2 * Lp
                    xj = x_dt[:, pc]
                    xbd = jnp.concatenate([xj * lo, xj * hi], axis=0).astype(BF16)
                    dyj = dy[:, pc].astype(BF16)
                    dMp = _dot_nt(dyj, xbd)
                    dxbd = _dot_tn(Mp.astype(BF16), dyj)
                    dxdt_s[:, pc] += dxbd[0:CHUNK, :] * lo + dxbd[CHUNK:2 * CHUNK, :] * hi
                    dcb2 = dcb2 + dMp * Lp
                    dseg = dMp * Mp
                    dE_s[:, pc] += dseg - diag * jnp.sum(dseg, axis=0, keepdims=True)
                dcb2b = dcb2.astype(BF16)
                dC = dC + _dot(dcb2b, B2)
                dB2 = _dot_tn(dcb2b, Cg)
                dB = dB + dB2[0:CHUNK, :] + dB2[CHUNK:2 * CHUNK, :]
                dxc_s[:, bcol] = dB
                dxc_s[:, ccol] = dC
                dht_ref[:, gc] = dht_prev
            dx_dt = dxdt_s[...]
            dxc_s[:, 0:D] += dx_dt * dtE
            red = _sel_right(jnp.concatenate([dE_s[...], dx_dt * xs], axis=0), expT_ref[...])
            da = _sel_left(triT_ref[...], red[0:CHUNK, :])
            ddtv = red[CHUNK:2 * CHUNK, :] + da * A
            dAacc_ref[0:1, :] += jnp.sum(da * dtv, axis=0, keepdims=True)
            ddtr = ddtv * _sigmoid(dtr)
            ddtb_ref[0:1, :] += jnp.sum(ddtr, axis=0, keepdims=True)
            dp_ref[rows, D + XBC_W:D + XBC_W + DT_W] = ddtr.astype(BF16)
            dpre = dxc_s[...] * (sg * (1.0 + pre * (1.0 - sg)))
            dpre_ref[rows, :] = dpre
            dcb_ref[0:1, :] += jnp.sum(dpre, axis=0, keepdims=True)
            for k in range(CONV_K):
                dcw_ref[k:k + 1, :] += jnp.sum(dpre * q["taps"][k], axis=0, keepdims=True)
        dxbc = jnp.zeros((T, XBC_W), F32)
        for k in range(CONV_K):
            dxbc = dxbc + cw_ref[k:k + 1, :] * dpre_ref[pl.ds(CONV_K - 1 - k, T), :]
        dp_ref[:, D:D + XBC_W] = dxbc.astype(BF16)
        dp_ref[:, SEG_SSD[1]:SSD_W] = jnp.zeros((T, SSD_W - SEG_SSD[1]), BF16)
        dpre_ref[T:T + HALO, :] = dpre_ref[0:HALO, :]

        @pl.when(i == nsteps - 1)
        def _():
            dalog_ref[...] = dAacc_ref[...] * (-jnp.exp(alog_ref[...]))
            dD_ref[...] = _dot(dDacc_ref[...], expT_ref[...].astype(F32), precision=HI)

    full = lambda a: pl.BlockSpec(a.shape, lambda i: (0,) * a.ndim)
    hb = T // HALO_BLK
    rev = lambda i: nsteps - 1 - i
    acc = lambda w: pl.BlockSpec((8, w), lambda i: (0, 0))
    return pl.pallas_call(
        body, name="ssd_bwd", grid=(nsteps,),
        in_specs=[pl.BlockSpec((T, D), lambda i: (rev(i), OFF_ZB // D)),
                  pl.BlockSpec((T, XBC_W), lambda i: (rev(i), OFF_XBC // XBC_W)),
                  pl.BlockSpec((HALO_BLK, XBC_W), lambda i: (jnp.maximum(rev(i) * hb - 1, 0), OFF_XBC // XBC_W)),
                  pl.BlockSpec((T, DT_W), lambda i: (rev(i), OFF_DT // DT_W)),
                  pl.BlockSpec((T, D), lambda i: (rev(i), 0)), pl.BlockSpec((T, D), lambda i: (rev(i), 0)),
                  pl.BlockSpec((ncl, STATE, D), lambda i: (rev(i), 0, 0)),
                  full(conv_w), full(conv_b), full(dtb_p), full(alog_p), full(d_exp), full(norm_w),
                  full(tri), full(triT), full(expand), full(expandT), full(shift)],
        out_specs=[pl.BlockSpec((T, SSD_W), lambda i: (rev(i), 0)),
                   acc(XBC_W), acc(XBC_W), acc(DT_W), acc(DT_W), acc(DT_W), acc(D)],
        out_shape=[jax.ShapeDtypeStruct((S, SSD_W), BF16),
                   jax.ShapeDtypeStruct((8, XBC_W), F32), jax.ShapeDtypeStruct((8, XBC_W), F32),
                   jax.ShapeDtypeStruct((8, DT_W), F32), jax.ShapeDtypeStruct((8, DT_W), F32),
                   jax.ShapeDtypeStruct((8, DT_W), F32), jax.ShapeDtypeStruct((8, D), F32)],
        scratch_shapes=[pltpu.VMEM((STATE, D), F32), pltpu.VMEM((HALO_BLK + T, XBC_W), BF16), pltpu.VMEM((T + HALO_BLK, XBC_W), F32),
                        pltpu.VMEM((CHUNK, D), F32), pltpu.VMEM((CHUNK, D), F32), pltpu.VMEM((CHUNK, D), F32),
                        pltpu.VMEM((CHUNK, XBC_W), F32), pltpu.VMEM((8, D), F32), pltpu.VMEM((8, DT_W), F32)],
        compiler_params=_cp(("arbitrary",)),
    )(proj, proj, proj, proj, dyb, y, states, conv_w, conv_b, dtb_p, alog_p, d_exp, norm_w, tri, triT, expand, expandT, shift)


def _head(x, ya, yb, proj, target, gate_b, wout, fw, *, tm):
    S = x.shape[0]

    def body(x_ref, ya_ref, yb_ref, gl0_ref, gl1_ref, t_ref, gb_ref, w_ref, fw_ref,
             dh_ref, dhb_ref, mb_ref, dya_ref, dyb_ref, dgl_ref, loss_ref, dfw_ref, dgb_ref):
        @pl.when(pl.program_id(0) == 0)
        def _():
            loss_ref[...] = jnp.zeros_like(loss_ref)
            dfw_ref[...] = jnp.zeros_like(dfw_ref)
            dgb_ref[...] = jnp.zeros_like(dgb_ref)

        ya_v = ya_ref[...].astype(F32)
        yb_v = yb_ref[...].astype(F32)
        g0 = _sigmoid(gl0_ref[...].astype(F32) + gb_ref[:, 0:D])
        g1 = _sigmoid(gl1_ref[...].astype(F32) + gb_ref[:, D:2 * D])
        mb = (g0 * ya_v + g1 * yb_v).astype(BF16)
        mb_ref[...] = mb
        h = x_ref[...] + _dot(mb, w_ref[...])
        r = lax.rsqrt(jnp.mean(h * h, axis=-1, keepdims=True) + EPS)
        hn = h * r
        err = hn * fw_ref[...] - t_ref[...]
        loss_ref[...] += 0.5 * jnp.sum(jnp.mean(err * err, axis=-1, keepdims=True))
        dyf = err * (1.0 / D)
        dfw_ref[0:1, :] += jnp.sum(dyf * hn, axis=0, keepdims=True)
        dhn = dyf * fw_ref[...]
        dh = r * (dhn - hn * jnp.mean(dhn * hn, axis=-1, keepdims=True))
        dh_ref[...] = dh
        dhb = dh.astype(BF16)
        dhb_ref[...] = dhb
        dm = _dot_nt(dhb, w_ref[...])
        dya_ref[...] = (dm * g0).astype(BF16)
        dyb_ref[...] = (dm * g1).astype(BF16)
        dgl0 = dm * ya_v * g0 * (1.0 - g0)
        dgl1 = dm * yb_v * g1 * (1.0 - g1)
        dgl_ref[:, 0:D] = dgl0.astype(BF16)
        dgl_ref[:, D:2 * D] = dgl1.astype(BF16)
        dgb_ref[0:1, 0:D] += jnp.sum(dgl0, axis=0, keepdims=True)
        dgb_ref[0:1, D:2 * D] += jnp.sum(dgl1, axis=0, keepdims=True)

    row = pl.BlockSpec((tm, D), lambda i: (i, 0))
    seg = lambda off: pl.BlockSpec((tm, D), lambda i: (i, off // D))
    full = lambda a: pl.BlockSpec(a.shape, lambda i: (0,) * a.ndim)
    acc = lambda w: pl.BlockSpec((8, w), lambda i: (0, 0))
    return pl.pallas_call(
        body, name="head", grid=(S // tm,),
        in_specs=[row, row, row, seg(OFF_G0), seg(OFF_G1), row, full(gate_b), full(wout), full(fw)],
        out_specs=[row, row, row, row, row, pl.BlockSpec((tm, 2 * D), lambda i: (i, 0)), acc(LANE), acc(D), acc(2 * D)],
        out_shape=[jax.ShapeDtypeStruct((S, D), F32), jax.ShapeDtypeStruct((S, D), BF16), jax.ShapeDtypeStruct((S, D), BF16),
                   jax.ShapeDtypeStruct((S, D), BF16), jax.ShapeDtypeStruct((S, D), BF16), jax.ShapeDtypeStruct((S, 2 * D), BF16),
                   jax.ShapeDtypeStruct((8, LANE), F32), jax.ShapeDtypeStruct((8, D), F32), jax.ShapeDtypeStruct((8, 2 * D), F32)],
        compiler_params=_cp(("arbitrary",)),
    )(x, ya, yb, proj, proj, target, gate_b, wout, fw)


def _adam_update(g, w_ref, m_ref, v_ref, g_ref, d_ref, m2_ref, v2_ref):
    m2 = ADAM_B1 * m_ref[...] + (1.0 - ADAM_B1) * g
    v2 = ADAM_B2 * v_ref[...] + (1.0 - ADAM_B2) * (g * g)
    m_hat = m2 / (1.0 - ADAM_B1 ** ADAM_STEP)
    v_hat = v2 / (1.0 - ADAM_B2 ** ADAM_STEP)
    g_ref[...] = g
    d_ref[...] = -ADAM_LR * (m_hat / (jnp.sqrt(v_hat) + ADAM_EPS) + ADAM_WD * w_ref[...])
    m2_ref[...] = m2
    v2_ref[...] = v2


def _adamw_own(me, own, landed, w, m, v, *, tr, tc, name):
    _, R, C = landed.shape
    assert R % tr == 0 and C % tc == 0, (name, R, C, tr, tc)

    def body(me_ref, own_ref, p_ref, w_ref, m_ref, v_ref, g_ref, d_ref, m2_ref, v2_ref):
        mine = own_ref[0].astype(F32)
        g = jnp.where(me_ref[0] == 0, mine, p_ref[0].astype(F32))
        for k in range(1, N_DEV):
            g = g + jnp.where(me_ref[0] == k, mine, p_ref[k].astype(F32))
        _adam_update(g, w_ref, m_ref, v_ref, g_ref, d_ref, m2_ref, v2_ref)

    tile = pl.BlockSpec((tr, tc), lambda i, j, me_ref: (i, j))
    return pl.pallas_call(
        body, name=name,
        grid_spec=pltpu.PrefetchScalarGridSpec(
            num_scalar_prefetch=1, grid=(R // tr, C // tc),
            in_specs=[pl.BlockSpec((1, tr, tc), lambda i, j, me_ref: (me_ref[0], i, j)),
                      pl.BlockSpec((N_DEV, tr, tc), lambda i, j, me_ref: (0, i, j)), tile, tile, tile],
            out_specs=[tile, tile, tile, tile]),
        out_shape=[jax.ShapeDtypeStruct((R, C), F32)] * 4,
        compiler_params=_cp(("parallel", "parallel")),
    )(me, own, landed, w, m, v)


def _adamw(parts, w, m, v, *, tr, name):
    _, R, C = parts.shape
    assert R % tr == 0, (name, R, tr)

    def body(p_ref, w_ref, m_ref, v_ref, g_ref, d_ref, m2_ref, v2_ref):
        g = p_ref[0].astype(F32)
        for k in range(1, N_DEV):
            g = g + p_ref[k].astype(F32)
        _adam_update(g, w_ref, m_ref, v_ref, g_ref, d_ref, m2_ref, v2_ref)

    row = pl.BlockSpec((tr, C), lambda i: (i, 0))
    return pl.pallas_call(
        body, name=name, grid=(R // tr,),
        in_specs=[pl.BlockSpec((N_DEV, tr, C), lambda i: (0, i, 0)), row, row, row],
        out_specs=[row, row, row, row],
        out_shape=[jax.ShapeDtypeStruct((R, C), F32)] * 4,
        compiler_params=_cp(("parallel",)),
    )(parts, w, m, v)


def _place():
    x, y, c = lax.axis_index("x"), lax.axis_index("y"), lax.axis_index("c")
    return x, y, c


def _all_gather(arrs, *, name):
    n = len(arrs)

    def body(*refs):
        ins, outs = refs[:n], refs[n:2 * n]
        send_sems, recv_sems, local_sems = refs[2 * n:]
        x, y, c = _place()
        me, sibling = (x, y, c), (x, y, 1 - c)
        chips = [(1 - x, y), (x, 1 - y), (1 - x, 1 - y)]

        def idx(px, py, pc):
            return 4 * px + 2 * py + pc

        def copy(k, a, block, to, src=None):
            slab = outs[a].at[idx(*block)]
            return pltpu.make_async_remote_copy(
                src_ref=slab if src is None else src, dst_ref=slab,
                send_sem=send_sems.at[k, a], recv_sem=recv_sems.at[k, a], device_id=to, device_id_type=MESH)

        mine = [pltpu.make_async_copy(ins[a], outs[a].at[idx(*me)], local_sems.at[a]) for a in range(n)]
        for cp in mine:
            cp.start()
        first = []
        for a in range(n):
            first.append(copy(0, a, me, sibling, src=ins[a]))
            first += [copy(1 + j, a, me, (*chip, c), src=ins[a]) for j, chip in enumerate(chips)]
        for cp in first:
            cp.start()
        passed = []
        for j, chip in enumerate(chips):
            for a in range(n):
                copy(1 + j, a, (*chip, c), me).wait_recv()
                fwd = copy(4 + j, a, (*chip, c), sibling)
                fwd.start()
                passed.append(fwd)
        for a in range(n):
            copy(0, a, sibling, me).wait_recv()
            for j, chip in enumerate(chips):
                copy(4 + j, a, (*chip, 1 - c), me).wait_recv()
        for cp in first + passed:
            cp.wait_send()
        for cp in mine:
            cp.wait()

    anyspec = pl.BlockSpec(memory_space=pl.ANY)
    return pl.pallas_call(
        body, name=name,
        in_specs=[anyspec] * n, out_specs=[anyspec] * n,
        out_shape=[jax.ShapeDtypeStruct((N_DEV,) + a.shape, a.dtype) for a in arrs],
        scratch_shapes=[pltpu.SemaphoreType.DMA((7, n)), pltpu.SemaphoreType.DMA((7, n)), pltpu.SemaphoreType.DMA((n,))],
    )(*arrs)


W_ROWS = SEG_SSD[0] + SSD_PAD_W


GROUP = 16
INTERIOR = 1920


def _interior(k):
    lo = -(-(k * SHARD_IN) // GROUP) * GROUP
    hi = ((k + 1) * SHARD_IN) // GROUP * GROUP
    return lo, hi


def _dest_row(r):
    if r < REF_SGU_END:
        return r
    return r - REF_SGU_END + SEG_SSD[0] if r < REF_GATE_START else r - REF_GATE_START + SEG_GATE[0]


def _shard_pieces(k):
    lo_k, hi_k = _interior(k)
    out = []
    for lo, hi in ((0, REF_SGU_END), (REF_SGU_END, REF_GATE_START), (REF_GATE_START, W_IN)):
        a, b = max(lo, lo_k), min(hi, hi_k)
        if a < b:
            out.append((a - lo_k, b - a, _dest_row(a)))
    return out


GATHER_PARTS = 2


def _shard_parts(k):
    parts = [[] for _ in range(GATHER_PARTS)]
    for s0, n, d0 in _shard_pieces(k):
        step = -(-(n // GROUP) // GATHER_PARTS) * GROUP
        for p in range(GATHER_PARTS):
            a, b = min(p * step, n), min((p + 1) * step, n)
            if a < b:
                parts[p].append((s0 + a, b - a, d0 + a))
    return parts


def _patch_straddlers(wpT, heads, tails):
    for k in range(1, N_DEV):
        m = (k * SHARD_IN) % GROUP
        if m:
            group = jnp.concatenate([tails[k - 1, GROUP - m:], heads[k, :GROUP - m]], axis=0)
            wpT = lax.dynamic_update_slice(wpT, group, (_dest_row(k * SHARD_IN - m), 0))
    return wpT


def _gather_stages(k, win_ref, small, z_ref, n_zero, w_ref, send_sems, recv_sems, local_sems):
    x, y, c = k // 4, (k // 2) % 2, k % 2
    idx = lambda p: 4 * p[0] + 2 * p[1] + p[2]
    me, sib = (x, y, c), (x, y, 1 - c)
    xn, yn, dg = (1 - x, y, c), (x, 1 - y, c), (1 - x, 1 - y, c)
    parts = range(GATHER_PARTS)

    def copies(slot, block, to, part, own=False):
        kb = idx(block)
        out = []
        for j, (s0, n, d0) in enumerate(_shard_parts(kb)[part]):
            dst = w_ref.at[pl.ds(d0, n)]
            out.append((win_ref.at[pl.ds(s0, n)] if own else dst, dst, 2 * part + j))
        if part == 0:
            for j, (src, gathered) in enumerate(small):
                out.append((src if own else gathered.at[kb], gathered.at[kb], 2 * GATHER_PARTS + j))
        return [pltpu.make_async_remote_copy(src_ref=s, dst_ref=d, send_sem=send_sems.at[slot, j], recv_sem=recv_sems.at[slot, j],
                                             device_id=to, device_id_type=MESH) for s, d, j in out]

    def start(cps):
        for cp in cps:
            cp.start()

    def arrived(slot, block, part):
        for cp in copies(slot, block, me, part):
            cp.wait_recv()

    def local():
        pairs = [(win_ref.at[pl.ds(s0, n)], w_ref.at[pl.ds(d0, n)]) for s0, n, d0 in _shard_pieces(k)]
        pairs += [(src, gathered.at[k]) for src, gathered in small] + [(z_ref, w_ref.at[pl.ds(W_IN, n_zero)])]
        return [pltpu.make_async_copy(s, d, local_sems.at[j]) for j, (s, d) in enumerate(pairs)]

    relay = (xn, yn) if c == 1 else (yn, xn)

    def first():
        start(local())
        for p in parts:
            start(copies(0, me, sib, p, own=True) + copies(1, me, xn, p, own=True) + copies(2, me, yn, p, own=True))

    def hand_on():
        for p in parts:
            arrived(1, xn, p)
            start(copies(4, xn, sib, p))
            if c == 1:
                start(copies(3, *relay, p))
            arrived(2, yn, p)
            start(copies(5, yn, sib, p))
            if c == 0:
                start(copies(3, *relay, p))

    def finish():
        for p in parts:
            arrived(3, dg, p)
            start(copies(6, dg, sib, p))
        for p in parts:
            arrived(0, sib, p)
            arrived(4, (1 - x, y, 1 - c), p)
            arrived(5, (x, 1 - y, 1 - c), p)
            arrived(6, (1 - x, 1 - y, 1 - c), p)
        for p in parts:
            sent = (copies(0, me, sib, p, own=True) + copies(1, me, xn, p, own=True) + copies(2, me, yn, p, own=True)
                    + copies(3, *relay, p) + copies(4, xn, sib, p) + copies(5, yn, sib, p) + copies(6, dg, sib, p))
            for cp in sent:
                cp.wait_send()
        for cp in local():
            cp.wait()

    return first, hand_on, finish


def _gather_sems(n_small):
    n_arr = 2 * GATHER_PARTS + n_small
    return [pltpu.SemaphoreType.DMA((7, n_arr)), pltpu.SemaphoreType.DMA((7, n_arr)), pltpu.SemaphoreType.DMA((n_arr + 1,))]


def _gather_weights(win, head, tail, wout, cw, zeros):
    small_in = (wout, cw, head, tail)
    n_zero = zeros.shape[0]
    assert W_IN + n_zero == W_ROWS and W_IN % GROUP == 0

    def body(win_ref, wout_ref, cw_ref, head_ref, tail_ref, z_ref, w_ref, gout_ref, gcw_ref, ghead_ref, gtail_ref, *sems):
        x, y, c = _place()
        me = 4 * x + 2 * y + c
        small = ((wout_ref, gout_ref), (cw_ref, gcw_ref), (head_ref, ghead_ref), (tail_ref, gtail_ref))

        def run(k):
            for stage in _gather_stages(k, win_ref, small, z_ref, n_zero, w_ref, *sems):
                stage()

        for k in range(N_DEV):
            pl.when(me == k)(functools.partial(run, k))

    anyspec = pl.BlockSpec(memory_space=pl.ANY)
    return pl.pallas_call(
        body, name="gather_weights", in_specs=[anyspec] * 6, out_specs=[anyspec] * 5,
        out_shape=[jax.ShapeDtypeStruct((W_ROWS, D), win.dtype)]
        + [jax.ShapeDtypeStruct((N_DEV,) + a.shape, a.dtype) for a in small_in],
        scratch_shapes=_gather_sems(len(small_in)),
    )(win, wout, cw, head, tail, zeros)


_REL = [(dx, dy, dc) for dx in (0, 1) for dy in (0, 1) for dc in (0, 1)][1:]
_HBM = pl.BlockSpec(memory_space=pltpu.HBM)
_SEM = pl.BlockSpec(memory_space=pltpu.SEMAPHORE)
_EFFECT = pltpu.SideEffectType.DATAFLOW_SIDE_EFFECTING


def _peer(k):
    x, y, c = _place()
    dx, dy, dc = _REL[k]
    return (1 - x if dx else x, 1 - y if dy else y, 1 - c if dc else c)


def _exchange_start(parts, *, name):
    n = len(parts)

    def body(*refs):
        ins, lands = refs[:n], refs[n:2 * n]
        send_sems, recv_sems, token = refs[2 * n], refs[2 * n + 1], refs[-1]
        x, y, c = _place()
        me = 4 * x + 2 * y + c
        for a in range(n):
            for k in range(len(_REL)):
                px, py, pc = _peer(k)
                pltpu.make_async_remote_copy(
                    src_ref=ins[a].at[4 * px + 2 * py + pc], dst_ref=lands[a].at[me],
                    send_sem=send_sems.at[len(_REL) * a + k], recv_sem=recv_sems.at[len(_REL) * a + k],
                    device_id=(px, py, pc), device_id_type=MESH).start()
        token[...] = jnp.zeros_like(token)

    sem = pltpu.SemaphoreType.DMA((len(_REL) * n,))
    bufs = [pltpu.HBM(p.shape, p.dtype) for p in parts]
    outs = pl.pallas_call(
        body, name=name,
        out_shape=(sem, sem, *bufs, *bufs, jax.ShapeDtypeStruct((8, LANE), F32)),
        in_specs=(_HBM,) * (2 * n), out_specs=(_SEM, _SEM, *(_HBM,) * (2 * n), pl.BlockSpec(memory_space=pltpu.VMEM)),
        input_output_aliases={i: 2 + i for i in range(2 * n)},
        compiler_params=pltpu.CompilerParams(has_side_effects=_EFFECT),
    )(*[pltpu.with_memory_space_constraint(p, pltpu.HBM) for p in parts],
      *[pltpu.with_memory_space_constraint(lax.empty(p.shape, p.dtype), pltpu.HBM) for p in parts])
    return outs[0], outs[1], outs[2:2 + n], outs[2 + n:2 + 2 * n], outs[-1]


def _exchange_wait(send_sems, recv_sems, parts, lands, after, *, name):
    n = len(parts)

    def body(*refs):
        ins, lands_ = refs[:n], refs[n:2 * n]
        ssem, rsem = refs[2 * n], refs[2 * n + 1]
        for a in range(n):
            for k in range(len(_REL)):
                px, py, pc = _peer(k)
                p = 4 * px + 2 * py + pc
                cp = pltpu.make_async_remote_copy(
                    src_ref=ins[a].at[p], dst_ref=lands_[a].at[p],
                    send_sem=ssem.at[len(_REL) * a + k], recv_sem=rsem.at[len(_REL) * a + k],
                    device_id=(px, py, pc), device_id_type=MESH)
                cp.wait_send()
                cp.wait_recv()

    bufs = [pltpu.HBM(p.shape, p.dtype) for p in parts]
    outs = pl.pallas_call(
        body, name=name, out_shape=(*bufs, *bufs),
        in_specs=(*(_HBM,) * (2 * n), _SEM, _SEM, pl.BlockSpec(memory_space=pl.ANY)), out_specs=(_HBM,) * (2 * n),
        input_output_aliases={i: i for i in range(2 * n)},
        compiler_params=pltpu.CompilerParams(has_side_effects=_EFFECT),
    )(*parts, *lands, send_sems, recv_sems, after)
    return outs[:n], outs[n:]


WEIGHTS = ('norm_w', 'w_in', 'gate_b', 'sgu_norm_g', 'sgu_norm_b', 'sgu_w', 'sgu_b', 'conv_w', 'conv_b', 'dt_bias', 'A_log',
           'D_skip', 'ssd_norm_w', 'w_out', 'final_norm_w')
SHARDED = ('w_in', 'conv_w', 'w_out')
PACK_ROW = 8 * LANE


def _constants():
    tri = np.tril(np.ones((CHUNK, CHUNK), np.float32))
    expand = np.zeros((DT_W, D), np.float32)
    for h in range(HEADS):
        expand[h, h * HEADDIM:(h + 1) * HEADDIM] = 1.0
    sel = np.zeros((D, LANE), np.float32)
    for g in range(SGU_GROUPS):
        sel[g * LANE:(g + 1) * LANE, g] = 1.0
    pos_chunk = np.arange(SGU_BLOCK) // CHUNK
    mask = (pos_chunk[None, :] <= pos_chunk[:, None]).astype(np.float32)
    shift = np.zeros(((CONV_K - 1) * CHUNK, HALO_BLK + CHUNK), np.float32)
    for kk in range(CONV_K - 1):
        for t in range(CHUNK):
            shift[kk * CHUNK + t, HALO_BLK - (CONV_K - 1) + t + kk] = 1.0
    return dict(tri=jnp.asarray(tri, BF16), triT=jnp.asarray(tri.T.copy(), BF16), expand=jnp.asarray(np.tile(expand, (3, 1)), BF16),
                shift=jnp.asarray(shift, BF16),
                expandT=jnp.asarray(expand.T.copy(), BF16), sel=jnp.asarray(sel), mask=jnp.asarray(mask))


def _to_shards(segs):
    starts = np.cumsum([0] + [n for _, n in segs])
    assert starts[-1] == W_IN
    slabs = []
    for k in range(N_DEV):
        pieces = []
        for (s, n), s0 in zip(segs, starts[:-1]):
            lo, hi = max(k * SHARD_IN, s0), min((k + 1) * SHARD_IN, s0 + n)
            if lo < hi:
                pieces.append(s[lo - s0:hi - s0])
        slabs.append(jnp.concatenate(pieces, axis=0))
    return jnp.stack(slabs)


def _local_step(x2, tgt, wpT, wout, cw, p, exchange_small, exchange):
    S = x2.shape[0]
    k = _constants()
    xn, proj = _in_proj(x2, p['norm_w'], wpT, tm=min(1024, S), tn=2048)
    wm32 = p['sgu_w'][0] * k['mask']
    wm = wm32.astype(BF16)
    wmT = jnp.swapaxes(wm32, 1, 2).astype(BF16)
    bias_full = jnp.repeat(p['sgu_b'][0].T, LANE, axis=1)
    tm_sgu = min(256, S)
    ya = _sgu_fwd(proj, p['sgu_norm_g'], p['sgu_norm_b'], wm, bias_full, tm=tm_sgu)
    pad32 = lambda a: jnp.pad(a, ((0, 0), (0, DT_W - HEADS)))
    dtb_p, alog_p = pad32(p['dt_bias']), pad32(p['A_log'])
    d_exp = jnp.repeat(p['D_skip'], HEADDIM, axis=1)
    ssd_args = (cw, p['conv_b'], dtb_p, alog_p, d_exp, p['ssd_norm_w'])
    y, yb, states = _ssd_fwd(proj, *ssd_args, k['tri'], k['expand'], k['shift'])
    dh, dhb, mb, dya, dyb, dgl, loss, dfw, dgb = _head(
        x2, ya, yb, proj, tgt, p['gate_b'], wout, p['final_norm_w'][None, :], tm=min(256, S))
    dsgu, dws, dbsT, dsg, dsb = _sgu_bwd(proj, dya, p['sgu_norm_g'], p['sgu_norm_b'], wm, wmT, bias_full, k['mask'], k['sel'],
                                         tm=tm_sgu)
    dssd, dcw, dcb, ddtb, dalog, dD, dnw = _ssd_bwd(proj, dyb, y, states, *ssd_args, k['tri'], k['triT'], k['expand'], k['expandT'],
                                                    k['shift'])
    grads = dict(
        gate_b=dgb[0:1], sgu_norm_g=dsg[0:1], sgu_norm_b=dsb[0:1], sgu_w=dws[None],
        sgu_b=dbsT[:, :SGU_GROUPS].T[None], conv_w=dcw[0:CONV_K][None], conv_b=dcb[0:1], dt_bias=ddtb[0:1, :HEADS],
        A_log=dalog[0:1, :HEADS], D_skip=dD[0:1, :HEADS], ssd_norm_w=dnw[0:1], final_norm_w=dfw[0])
    token = exchange_small(loss[0, 0], grads)
    tk = min(4096, S)
    tn = 1024
    dwT_sgu = _matmul(dsgu, xn, trans_a=True, out_dtype=BF16, tm=1024, tn=tn, tk=tk, after=token, name="dw_in_sgu")
    dwT_gate = _matmul(dgl, xn, trans_a=True, out_dtype=BF16, tm=1024, tn=tn, tk=tk, name="dw_in_gate")
    dwT_ssd = _matmul(dssd, xn, trans_a=True, out_dtype=BF16, tm=1024, tn=tn, tk=tk, name="dw_in_ssd")
    dw_out = _matmul(mb, dhb, trans_a=True, out_dtype=BF16, tm=1024, tn=tn, tk=tk, name="dw_out")
    token = exchange([(dwT_sgu, SEG_SGU[1]), (dwT_ssd, W_IN - SEG_SSD[0]), (dwT_gate, SEG_GATE[1])], dw_out)
    tm = min(1024, S)
    dxn = _matmul(dsgu, wpT, tm=tm, tn=tn, tk=3072, after=token, name="dxn_sgu")
    dxn = _matmul(dgl, wpT, b_koff=SEG_GATE[0] // 2048, tm=tm, tn=tn, tk=2048, add=dxn, name="dxn_gate")
    dxn = _matmul(dssd, wpT, b_koff=SEG_SSD[0] // 2048, tm=tm, tn=tn, tk=2048, add=dxn, name="dxn_ssd")
    grad_x, dnorm = _norm_bwd(x2, p['norm_w'], dxn, dh, tm=min(256, S))
    return grad_x, dnorm[0:1]


def _pack(arrs):
    rows, offs, r = [], [], 0
    for a in arrs:
        n = a.size
        nr = -(-n // PACK_ROW) * 8
        rows.append(jnp.pad(a.reshape(-1).astype(F32), (0, nr * LANE - n)).reshape(nr, LANE))
        offs.append(r)
        r += nr
    return jnp.concatenate(rows, axis=0), offs


def kernel(x, norm_w, w_in, gate_b, sgu_norm_g, sgu_norm_b, sgu_w, sgu_b, conv_w, conv_b, dt_bias, A_log, D_skip, ssd_norm_w, w_out, final_norm_w, loss_target, m_norm_w, m_w_in, m_gate_b, m_sgu_norm_g, m_sgu_norm_b, m_sgu_w, m_sgu_b, m_conv_w, m_conv_b, m_dt_bias, m_A_log, m_D_skip, m_ssd_norm_w, m_w_out, m_final_norm_w, v_norm_w, v_w_in, v_gate_b, v_sgu_norm_g, v_sgu_norm_b, v_sgu_w, v_sgu_b, v_conv_w, v_conv_b, v_dt_bias, v_A_log, v_D_skip, v_ssd_norm_w, v_w_out, v_final_norm_w):
    w = dict(norm_w=norm_w, w_in=w_in, gate_b=gate_b, sgu_norm_g=sgu_norm_g, sgu_norm_b=sgu_norm_b, sgu_w=sgu_w, sgu_b=sgu_b,
             conv_w=conv_w, conv_b=conv_b, dt_bias=dt_bias, A_log=A_log, D_skip=D_skip, ssd_norm_w=ssd_norm_w, w_out=w_out,
             final_norm_w=final_norm_w)
    m = dict(norm_w=m_norm_w, w_in=m_w_in, gate_b=m_gate_b, sgu_norm_g=m_sgu_norm_g, sgu_norm_b=m_sgu_norm_b, sgu_w=m_sgu_w,
             sgu_b=m_sgu_b, conv_w=m_conv_w, conv_b=m_conv_b, dt_bias=m_dt_bias, A_log=m_A_log, D_skip=m_D_skip,
             ssd_norm_w=m_ssd_norm_w, w_out=m_w_out, final_norm_w=m_final_norm_w)
    v = dict(norm_w=v_norm_w, w_in=v_w_in, gate_b=v_gate_b, sgu_norm_g=v_sgu_norm_g, sgu_norm_b=v_sgu_norm_b, sgu_w=v_sgu_w,
             sgu_b=v_sgu_b, conv_w=v_conv_w, conv_b=v_conv_b, dt_bias=v_dt_bias, A_log=v_A_log, D_skip=v_D_skip,
             ssd_norm_w=v_ssd_norm_w, w_out=v_w_out, final_norm_w=v_final_norm_w)
    me = 4 * lax.axis_index("x") + 2 * lax.axis_index("y") + lax.axis_index("c")
    shard_cw = XBC_W // N_DEV

    tpose = lambda a: jnp.swapaxes(a[0], 0, 1)
    wT = tpose(w_in).astype(BF16)
    first_group = (GROUP - (me * SHARD_IN) % GROUP) % GROUP
    window = lax.dynamic_slice(jnp.pad(wT, ((0, GROUP), (0, 0))), (first_group, 0), (INTERIOR, D))
    wpT, g_out, g_cw, heads, tails = _gather_weights(window, wT[:GROUP], wT[SHARD_IN - GROUP:], w_out[0].astype(BF16),
                                                     conv_w[0], jnp.zeros((W_ROWS - W_IN, D), BF16))
    wpT = _patch_straddlers(wpT, heads, tails)
    wout_full = g_out.reshape(D, D)
    cw_full = jnp.swapaxes(g_cw, 0, 1).reshape(CONV_K, XBC_W)

    flight = {}

    small = [n for n in WEIGHTS if n not in SHARDED and n != 'norm_w']
    early = {}

    def exchange_small(loss_part, grads):
        early['packed'], early['offs'] = _pack([grads[n] for n in small] + [loss_part, grads['conv_w']])
        parts = [jnp.broadcast_to(early['packed'][None], (N_DEV,) + early['packed'].shape)]
        early['sems'], early['rsems'], early['parts'], early['lands'], token = _exchange_start(parts, name="small_start")
        return token

    def exchange(dw_inT_segs, dw_out):
        parts = [_to_shards(dw_inT_segs), dw_out.reshape(N_DEV, D // N_DEV, D)]
        flight['sems'], flight['rsems'], flight['parts'], flight['lands'], token = _exchange_start(parts, name="exchange_start")
        return token

    grad_x, dnorm = _local_step(x[0], loss_target[0], wpT, wout_full, cw_full, w, exchange_small, exchange)
    _, (land_small,) = _exchange_wait(early['sems'], early['rsems'], early['parts'], early['lands'], grad_x, name="small_wait")
    (own_in, own_out), (land_in, land_out) = _exchange_wait(
        flight['sems'], flight['rsems'], flight['parts'], flight['lands'], grad_x, name="exchange_wait")
    me_arr = jnp.reshape(me, (1,)).astype(jnp.int32)
    res = {}
    res['w_in'] = [jnp.swapaxes(o, 0, 1) for o in _adamw_own(
        me_arr, own_in, land_in, tpose(w_in), tpose(m_w_in), tpose(v_w_in), tr=SHARD_IN, tc=256, name="adamw_w_in")]
    res['w_out'] = _adamw_own(me_arr, own_out, land_out, w_out[0], m_w_out[0], v_w_out[0], tr=128, tc=D, name="adamw_w_out")

    (norm_parts,) = _all_gather([_pack([dnorm])[0]], name="gather_norm")
    norm_outs = _adamw(norm_parts, *[_pack([d['norm_w']])[0] for d in (w, m, v)], tr=norm_parts.shape[1], name="adamw_norm")
    res['norm_w'] = [o.reshape(-1)[:D].reshape(w['norm_w'].shape) for o in norm_outs]

    offs = early['offs']
    gathered = lax.dynamic_update_slice(land_small, early['packed'][None], (me, 0, 0))
    off_loss, off_cw = offs[-2], offs[-1]
    cw_parts = gathered[:, off_cw:, :].reshape(N_DEV, CONV_K, XBC_W)
    cw_parts = lax.dynamic_slice_in_dim(cw_parts, me * shard_cw, shard_cw, axis=2)
    cw_rows = _pack([cw_parts[0]])[0].shape[0]
    cw_parts = jnp.pad(cw_parts.reshape(N_DEV, -1), ((0, 0), (0, cw_rows * LANE - CONV_K * shard_cw))).reshape(N_DEV, cw_rows, LANE)
    parts = jnp.concatenate([gathered[:, :off_cw, :], cw_parts], axis=1)
    zero = jnp.zeros((), F32)
    packs = [_pack([d[n] for n in small] + [zero, d['conv_w']])[0] for d in (w, m, v)]
    outs = _adamw(parts, *packs, tr=parts.shape[1], name="adamw_small")

    def unpack(o, name):
        if name == 'conv_w':
            return o[off_cw:off_cw + cw_rows].reshape(-1)[:CONV_K * shard_cw].reshape(w['conv_w'].shape)
        r0 = offs[small.index(name)]
        n = w[name].size
        return o[r0:r0 + -(-n // PACK_ROW) * 8].reshape(-1)[:n].reshape(w[name].shape)

    for n in small + ['conv_w']:
        res[n] = [unpack(o, n) for o in outs]
    for n in ('w_in', 'w_out'):
        res[n] = [o[None] for o in res[n]]
    loss = outs[0][off_loss, 0]
    return (loss, grad_x[None], *[res[n][0] for n in WEIGHTS], *[res[n][1] for n in WEIGHTS],
            *[res[n][2] for n in WEIGHTS], *[res[n][3] for n in WEIGHTS])
```

```python
import functools

import numpy as np
import jax
import jax.numpy as jnp
from jax import lax
from jax.experimental import pallas as pl
from jax.experimental.pallas import tpu as pltpu

F32 = jnp.float32
BF16 = jnp.bfloat16
HI = lax.Precision.HIGHEST
MESH = pl.DeviceIdType.MESH

D = 2048
EPS = 1e-5
SGU_BLOCK = 128
SGU_GROUPS = 16
CHUNK = 64
HEADS = 32
HEADDIM = 64
SSD_GROUPS = 4
GROUP_W = D // SSD_GROUPS
STATE = 128
CONV_K = 4
XBC_W = D + 2 * SSD_GROUPS * STATE
W_IN = 15392
N_DEV = 8
SHARD_IN = W_IN // N_DEV
ADAM_LR, ADAM_B1, ADAM_B2, ADAM_EPS, ADAM_WD, ADAM_STEP = 0.001, 0.9, 0.999, 1e-08, 0.01, 10

REF_SGU_END = 3 * D
REF_GATE_START = W_IN - 2 * D
LANE = 128
DT_W = LANE
OFF_U, OFF_V, OFF_ZA, OFF_G0, OFF_G1, OFF_ZB = (i * D for i in range(6))
OFF_XBC = OFF_ZB + D
OFF_DT = OFF_XBC + XBC_W
SEG_SGU = (OFF_U, 3 * D)
SEG_GATE = (OFF_G0, 2 * D)
SEG_SSD = (OFF_ZB, D + XBC_W + DT_W)
WP = SEG_SSD[0] + SEG_SSD[1]
SSD_PAD_W = 3 * D
VMEM_BYTES = 64 * 1024 * 1024
VMEM_LIMIT = VMEM_BYTES - 8 * 1024 * 1024


def _cp(sem=None, vmem=VMEM_LIMIT):
    return pltpu.CompilerParams(dimension_semantics=sem, vmem_limit_bytes=vmem)


def _sigmoid(x):
    return 1.0 / (1.0 + jnp.exp(-x))


def _softplus(x):
    return jnp.maximum(x, 0.0) + jnp.log(1.0 + jnp.exp(-jnp.abs(x)))


def _dot(a, b, precision=None):
    return jnp.dot(a, b, preferred_element_type=F32, precision=precision)


def _dot_nt(a, b, precision=None):
    return lax.dot_general(a, b, (((1,), (1,)), ((), ())), preferred_element_type=F32, precision=precision)


def _dot_tn(a, b, precision=None):
    return lax.dot_general(a, b, (((0,), (0,)), ((), ())), preferred_element_type=F32, precision=precision)


def _split3(a):
    hi = a.astype(BF16)
    r = a - hi.astype(F32)
    mid = r.astype(BF16)
    return hi, mid, (r - mid.astype(F32)).astype(BF16)


def _sel_right(a, sel01):
    m = a.shape[0]
    r = _dot(jnp.concatenate(_split3(a), axis=0), sel01)
    return (r[0:m] + r[m:2 * m]) + r[2 * m:3 * m]


def _sel_right_k(a, sel01_x3):
    return _dot(jnp.concatenate(_split3(a), axis=1), sel01_x3)


def _sel_left(sel01, a):
    n = a.shape[1]
    r = _dot(sel01, jnp.concatenate(_split3(a), axis=1))
    return (r[:, 0:n] + r[:, n:2 * n]) + r[:, 2 * n:3 * n]


def _matmul(a, b, *, trans_a=False, trans_b=False, b_koff=0, out_dtype=F32, tm, tn, tk, add=None, after=None, name):
    K, M = a.shape if trans_a else a.shape[::-1]
    N = b.shape[0] if trans_b else b.shape[1]
    assert M % tm == 0 and N % tn == 0 and K % tk == 0 and not (trans_a and trans_b), (name, M, N, K, tm, tn, tk)
    nk = K // tk

    def body(*refs):
        a_ref, b_ref = refs[:2]
        add_ref = refs[2] if add is not None else None
        o_ref, acc_ref = refs[-2:]
        k = pl.program_id(2)
        if trans_a:
            part = _dot_tn(a_ref[...], b_ref[...])
        else:
            part = _dot_nt(a_ref[...], b_ref[...]) if trans_b else _dot(a_ref[...], b_ref[...])

        def result(r):
            if add_ref is not None:
                r = r + add_ref[...]
            return r.astype(out_dtype)

        if nk == 1:
            o_ref[...] = result(part)
        else:
            @pl.when(k == 0)
            def _():
                acc_ref[...] = part

            @pl.when(jnp.logical_and(k > 0, k < nk - 1))
            def _():
                acc_ref[...] += part

            @pl.when(k == nk - 1)
            def _():
                o_ref[...] = result(acc_ref[...] + part)

    in_specs = [pl.BlockSpec((tk, tm), lambda i, j, k: (k, i)) if trans_a else pl.BlockSpec((tm, tk), lambda i, j, k: (i, k)),
                pl.BlockSpec((tn, tk), lambda i, j, k: (j, k)) if trans_b else pl.BlockSpec((tk, tn), lambda i, j, k: (k + b_koff, j))]
    args = [a, b]
    if add is not None:
        in_specs.append(pl.BlockSpec((tm, tn), lambda i, j, k: (i, j)))
        args.append(add)
    if after is not None:
        in_specs.append(pl.BlockSpec(memory_space=pl.ANY))
        args.append(after)
    return pl.pallas_call(
        body, name=name, grid=(M // tm, N // tn, nk), in_specs=in_specs,
        out_specs=pl.BlockSpec((tm, tn), lambda i, j, k: (i, j)),
        out_shape=jax.ShapeDtypeStruct((M, N), out_dtype),
        scratch_shapes=[pltpu.VMEM((tm, tn), F32)],
        compiler_params=_cp(("parallel", "parallel", "arbitrary")),
    )(*args)


def _in_proj(x, w, wpT, *, tm, tn):
    S = x.shape[0]
    N = wpT.shape[0]
    assert S % tm == 0 and N % tn == 0, (S, N, tm, tn)

    def body(x_ref, w_ref, b_ref, xn_ref, o_ref, xs_ref):
        @pl.when(pl.program_id(1) == 0)
        def _():
            xv = x_ref[...]
            r = lax.rsqrt(jnp.mean(xv * xv, axis=-1, keepdims=True) + EPS)
            xs = (xv * r * w_ref[...]).astype(BF16)
            xs_ref[...] = xs
            xn_ref[...] = xs

        o_ref[...] = _dot_nt(xs_ref[...], b_ref[...]).astype(BF16)

    return pl.pallas_call(
        body, name="in_proj", grid=(S // tm, N // tn),
        in_specs=[pl.BlockSpec((tm, D), lambda i, j: (i, 0)), pl.BlockSpec((1, D), lambda i, j: (0, 0)),
                  pl.BlockSpec((tn, D), lambda i, j: (j, 0))],
        out_specs=[pl.BlockSpec((tm, D), lambda i, j: (i, 0)), pl.BlockSpec((tm, tn), lambda i, j: (i, j))],
        out_shape=[jax.ShapeDtypeStruct((S, D), BF16), jax.ShapeDtypeStruct((S, N), BF16)],
        scratch_shapes=[pltpu.VMEM((tm, D), BF16)],
        compiler_params=_cp(("parallel", "arbitrary")),
    )(x, w, wpT)


def _norm_bwd(x, w, dxn, dh, *, tm):
    S = x.shape[0]

    def body(x_ref, w_ref, dxn_ref, dh_ref, gx_ref, dw_ref):
        xv = x_ref[...]
        r = lax.rsqrt(jnp.mean(xv * xv, axis=-1, keepdims=True) + EPS)
        xh = xv * r
        dxn_v = dxn_ref[...]
        dxh = dxn_v * w_ref[...]
        gx_ref[...] = dh_ref[...] + r * (dxh - xh * jnp.mean(dxh * xh, axis=-1, keepdims=True))

        @pl.when(pl.program_id(0) == 0)
        def _():
            dw_ref[...] = jnp.zeros_like(dw_ref)

        dw_ref[0:1, :] += jnp.sum(dxn_v * xh, axis=0, keepdims=True)

    row = pl.BlockSpec((tm, D), lambda i: (i, 0))
    return pl.pallas_call(
        body, name="norm_bwd", grid=(S // tm,),
        in_specs=[row, pl.BlockSpec((1, D), lambda i: (0, 0)), row, row],
        out_specs=[row, pl.BlockSpec((8, D), lambda i: (0, 0))],
        out_shape=[jax.ShapeDtypeStruct((S, D), F32), jax.ShapeDtypeStruct((8, D), F32)],
        compiler_params=_cp(("arbitrary",)),
    )(x, w, dxn, dh)


def _sgu_core(u_ref, v_ref, z_ref, g_ref, b_ref, wm_ref, bias_ref, vnb_ref, mixed_ref, tm):
    v = v_ref[...].astype(F32)
    mu = jnp.mean(v, axis=-1, keepdims=True)
    vc = v - mu
    rs = lax.rsqrt(jnp.mean(vc * vc, axis=-1, keepdims=True) + EPS)
    vh = vc * rs
    vnb_ref[...] = (vh * g_ref[...] + b_ref[...]).astype(BF16)
    for blk in range(tm // SGU_BLOCK):
        rows = pl.ds(blk * SGU_BLOCK, SGU_BLOCK)
        for gi in range(SGU_GROUPS):
            cols = pl.ds(gi * LANE, LANE)
            mixed_ref[rows, cols] = _dot(wm_ref[gi], vnb_ref[rows, cols]) + bias_ref[:, cols]
    return vh, rs


def _sgu_fwd(proj, g, b, wm, bias_full, *, tm):
    S = proj.shape[0]

    def body(u_ref, v_ref, z_ref, g_ref, b_ref, wm_ref, bias_ref, y_ref, vnb_ref, mixed_ref):
        _sgu_core(u_ref, v_ref, z_ref, g_ref, b_ref, wm_ref, bias_ref, vnb_ref, mixed_ref, tm)
        z = z_ref[...].astype(F32)
        y_ref[...] = (u_ref[...].astype(F32) * mixed_ref[...] * (z * _sigmoid(z))).astype(BF16)

    seg = lambda off: pl.BlockSpec((tm, D), lambda i: (i, off // D))
    full = lambda a: pl.BlockSpec(a.shape, lambda i: (0,) * a.ndim)
    return pl.pallas_call(
        body, name="sgu_fwd", grid=(S // tm,),
        in_specs=[seg(OFF_U), seg(OFF_V), seg(OFF_ZA), full(g), full(b), full(wm), full(bias_full)],
        out_specs=pl.BlockSpec((tm, D), lambda i: (i, 0)),
        out_shape=jax.ShapeDtypeStruct((S, D), BF16),
        scratch_shapes=[pltpu.VMEM((tm, D), BF16), pltpu.VMEM((tm, D), F32)],
        compiler_params=_cp(("parallel",)),
    )(proj, proj, proj, g, b, wm, bias_full)


def _sgu_bwd(proj, dy, g, b, wm, wmT, bias_full, mask, sel, *, tm):
    S = proj.shape[0]
    nsteps = S // tm

    def body(u_ref, v_ref, z_ref, dy_ref, g_ref, b_ref, wm_ref, wmT_ref, bias_ref, mask_ref, sel_ref,
             dp_ref, dws_ref, dbs_ref, dg_ref, db_ref, vnb_ref, mixed_ref, dmb_ref, dvn_ref, dbias_ref):
        i = pl.program_id(0)

        @pl.when(i == 0)
        def _():
            dws_ref[...] = jnp.zeros_like(dws_ref)
            dg_ref[...] = jnp.zeros_like(dg_ref)
            db_ref[...] = jnp.zeros_like(db_ref)
            dbias_ref[...] = jnp.zeros_like(dbias_ref)

        vh, rs = _sgu_core(u_ref, v_ref, z_ref, g_ref, b_ref, wm_ref, bias_ref, vnb_ref, mixed_ref, tm)
        u = u_ref[...].astype(F32)
        z = z_ref[...].astype(F32)
        dy_v = dy_ref[...].astype(F32)
        mixed = mixed_ref[...]
        sg = _sigmoid(z)
        sz = z * sg
        dp_ref[:, 0:D] = (dy_v * mixed * sz).astype(BF16)
        dp_ref[:, 2 * D:3 * D] = (dy_v * u * mixed * (sg * (1.0 + z * (1.0 - sg)))).astype(BF16)
        dmixed = dy_v * u * sz
        dmb_ref[...] = dmixed.astype(BF16)
        for blk in range(tm // SGU_BLOCK):
            dbias_ref[...] += dmixed[blk * SGU_BLOCK:(blk + 1) * SGU_BLOCK, :]
        for blk in range(tm // SGU_BLOCK):
            rows = pl.ds(blk * SGU_BLOCK, SGU_BLOCK)
            for gi in range(SGU_GROUPS):
                cols = pl.ds(gi * LANE, LANE)
                dm = dmb_ref[rows, cols]
                dvn_ref[rows, cols] = _dot(wmT_ref[gi], dm)
                dws_ref[gi] += _dot_nt(dm, vnb_ref[rows, cols])
        dvn = dvn_ref[...]
        dg_ref[0:1, :] += jnp.sum(dvn * vh, axis=0, keepdims=True)
        db_ref[0:1, :] += jnp.sum(dvn, axis=0, keepdims=True)
        dvh = dvn * g_ref[...]
        dv = rs * (dvh - jnp.mean(dvh, axis=-1, keepdims=True) - vh * jnp.mean(dvh * vh, axis=-1, keepdims=True))
        dp_ref[:, D:2 * D] = dv.astype(BF16)

        @pl.when(i == nsteps - 1)
        def _():
            for gi in range(SGU_GROUPS):
                dws_ref[gi] = dws_ref[gi] * mask_ref[...]
            dbs_ref[...] = _dot(dbias_ref[...], sel_ref[...], precision=HI)

    seg = lambda off: pl.BlockSpec((tm, D), lambda i: (i, off // D))
    full = lambda a: pl.BlockSpec(a.shape, lambda i: (0,) * a.ndim)
    return pl.pallas_call(
        body, name="sgu_bwd", grid=(nsteps,),
        in_specs=[seg(OFF_U), seg(OFF_V), seg(OFF_ZA), pl.BlockSpec((tm, D), lambda i: (i, 0)),
                  full(g), full(b), full(wm), full(wmT), full(bias_full), full(mask), full(sel)],
        out_specs=[pl.BlockSpec((tm, 3 * D), lambda i: (i, 0)),
                   pl.BlockSpec((SGU_GROUPS, SGU_BLOCK, SGU_BLOCK), lambda i: (0, 0, 0)),
                   pl.BlockSpec((SGU_BLOCK, LANE), lambda i: (0, 0)),
                   pl.BlockSpec((8, D), lambda i: (0, 0)), pl.BlockSpec((8, D), lambda i: (0, 0))],
        out_shape=[jax.ShapeDtypeStruct((S, 3 * D), BF16),
                   jax.ShapeDtypeStruct((SGU_GROUPS, SGU_BLOCK, SGU_BLOCK), F32),
                   jax.ShapeDtypeStruct((SGU_BLOCK, LANE), F32),
                   jax.ShapeDtypeStruct((8, D), F32), jax.ShapeDtypeStruct((8, D), F32)],
        scratch_shapes=[pltpu.VMEM((tm, D), BF16), pltpu.VMEM((tm, D), F32), pltpu.VMEM((tm, D), BF16),
                        pltpu.VMEM((tm, D), F32), pltpu.VMEM((SGU_BLOCK, D), F32)],
        compiler_params=_cp(("arbitrary",)),
    )(proj, proj, proj, dy, g, b, wm, wmT, bias_full, mask, sel)


SSD_T = 2 * CHUNK
HALO = 8
HALO_BLK = 16


def _pair_masks():
    row = lax.broadcasted_iota(jnp.int32, (CHUNK, LANE), 0)
    lane = lax.broadcasted_iota(jnp.int32, (CHUNK, LANE), 1)
    pos = jnp.where(lane >= CHUNK, lane - CHUNK, lane)
    diag = (row == pos).astype(F32)
    causal = row >= pos
    lo = (lane < CHUNK).astype(F32)
    return diag, causal, lo, 1.0 - lo


def _ssd_chunk_fwd(c, ext_ref, shift_ref, dt_ref, cw_ref, cb_ref, dtb_ref, alog_ref, tri_ref, exp_ref):
    r0 = c * CHUNK
    win = ext_ref[pl.ds(r0, HALO_BLK + CHUNK), :]
    sh = _dot(shift_ref[...], win)
    taps = [sh[k * CHUNK:(k + 1) * CHUNK] for k in range(CONV_K - 1)] + [win[HALO_BLK:].astype(F32)]
    pre = cb_ref[...] + sum(cw_ref[k:k + 1, :] * taps[k] for k in range(CONV_K))
    sg = _sigmoid(pre)
    xc = pre * sg
    dtr = dt_ref[pl.ds(r0, CHUNK), :].astype(F32) + dtb_ref[...]
    dtv = _softplus(dtr)
    A = -jnp.exp(alog_ref[...])
    acs = _sel_left(tri_ref[...], dtv * A)
    both = _sel_right_k(jnp.concatenate([acs, dtv], axis=0), exp_ref[...])
    E, dtE = both[0:CHUNK], both[CHUNK:2 * CHUNK]
    return dict(taps=taps, pre=pre, sg=sg, xc=xc, dtr=dtr, dtv=dtv, A=A, E=E, dtE=dtE)


def _ssd_fwd(proj, conv_w, conv_b, dtb_p, alog_p, d_exp, norm_w, tri, expand, shift):
    S = proj.shape[0]
    T = SSD_T
    nsteps = S // T
    ncl = T // CHUNK

    def body(zb_ref, xbc_ref, halo_ref, dt_ref, cw_ref, cb_ref, dtb_ref, alog_ref, dexp_ref, nw_ref, tri_ref, exp_ref, shift_ref,
             y_ref, yb_ref, st_ref, ht_ref, ext_ref):
        i = pl.program_id(0)

        @pl.when(i == 0)
        def _():
            ht_ref[...] = jnp.zeros_like(ht_ref)
            ext_ref[0:HALO_BLK, :] = jnp.zeros((HALO_BLK, XBC_W), BF16)

        @pl.when(i > 0)
        def _():
            ext_ref[0:HALO_BLK, :] = halo_ref[...]

        ext_ref[HALO_BLK:HALO_BLK + T, :] = xbc_ref[...]
        diag, causal, lo, hi = _pair_masks()
        for c in range(ncl):
            q = _ssd_chunk_fwd(c, ext_ref, shift_ref, dt_ref, cw_ref, cb_ref, dtb_ref, alog_ref, tri_ref, exp_ref)
            rows = pl.ds(c * CHUNK, CHUNK)
            xc, E, dtE = q["xc"], q["E"], q["dtE"]
            xs = xc[:, 0:D]
            total = E[CHUNK - 1:CHUNK, :]
            x_dt = xs * dtE
            eE = jnp.exp(E)
            xw = x_dt * jnp.exp(total - E)
            st_ref[c] = ht_ref[...]
            for g in range(SSD_GROUPS):
                gc = slice(g * GROUP_W, (g + 1) * GROUP_W)
                Bg = xc[:, D + g * STATE:D + (g + 1) * STATE].astype(BF16)
                Cg = xc[:, D + SSD_GROUPS * STATE + g * STATE:D + SSD_GROUPS * STATE + (g + 1) * STATE].astype(BF16)
                cb2 = _dot_nt(Cg, jnp.concatenate([Bg, Bg], axis=0))
                htg = ht_ref[:, gc]
                y_ref[rows, gc] = eE[:, gc] * _dot(Cg, htg.astype(BF16)) + xs[:, gc] * dexp_ref[:, gc]
                for jj in range(GROUP_W // LANE):
                    pc = slice(g * GROUP_W + jj * LANE, g * GROUP_W + (jj + 1) * LANE)
                    Ej = E[:, pc]
                    e2 = jnp.sum(Ej * diag, axis=0, keepdims=True)
                    Mp = cb2 * jnp.exp(jnp.where(causal, Ej - e2, -1e30))
                    xj = x_dt[:, pc]
                    xbd = jnp.concatenate([xj * lo, xj * hi], axis=0).astype(BF16)
                    y_ref[rows, pc] += _dot(Mp.astype(BF16), xbd)
                ht_ref[:, gc] = jnp.exp(total[:, gc]) * htg + _dot_tn(Bg, xw[:, gc].astype(BF16))
            zb = zb_ref[rows, :].astype(F32)
            hh = y_ref[rows, :] * (zb * _sigmoid(zb))
            for g in range(SSD_GROUPS):
                gc = slice(g * GROUP_W, (g + 1) * GROUP_W)
                hg = hh[:, gc]
                r = lax.rsqrt(jnp.mean(hg * hg, axis=-1, keepdims=True) + EPS)
                yb_ref[rows, gc] = (hg * r * nw_ref[:, gc]).astype(BF16)

    full = lambda a: pl.BlockSpec(a.shape, lambda i: (0,) * a.ndim)
    hb = T // HALO_BLK
    return pl.pallas_call(
        body, name="ssd_fwd", grid=(nsteps,),
        in_specs=[pl.BlockSpec((T, D), lambda i: (i, OFF_ZB // D)),
                  pl.BlockSpec((T, XBC_W), lambda i: (i, OFF_XBC // XBC_W)),
                  pl.BlockSpec((HALO_BLK, XBC_W), lambda i: (jnp.maximum(i * hb - 1, 0), OFF_XBC // XBC_W)),
                  pl.BlockSpec((T, DT_W), lambda i: (i, OFF_DT // DT_W)),
                  full(conv_w), full(conv_b), full(dtb_p), full(alog_p), full(d_exp), full(norm_w), full(tri), full(expand),
                  full(shift)],
        out_specs=[pl.BlockSpec((T, D), lambda i: (i, 0)), pl.BlockSpec((T, D), lambda i: (i, 0)),
                   pl.BlockSpec((ncl, STATE, D), lambda i: (i, 0, 0))],
        out_shape=[jax.ShapeDtypeStruct((S, D), F32), jax.ShapeDtypeStruct((S, D), BF16),
                   jax.ShapeDtypeStruct((S // CHUNK, STATE, D), F32)],
        scratch_shapes=[pltpu.VMEM((STATE, D), F32), pltpu.VMEM((HALO_BLK + T, XBC_W), BF16)],
        compiler_params=_cp(("arbitrary",)),
    )(proj, proj, proj, proj, conv_w, conv_b, dtb_p, alog_p, d_exp, norm_w, tri, expand, shift)


def _ssd_bwd(proj, dyb, y, states, conv_w, conv_b, dtb_p, alog_p, d_exp, norm_w, tri, triT, expand, expandT, shift):
    S = proj.shape[0]
    T = SSD_T
    nsteps = S // T
    ncl = T // CHUNK
    SSD_W = SSD_PAD_W

    def body(zb_ref, xbc_ref, halo_ref, dt_ref, dyb_ref, y_ref, st_ref, cw_ref, cb_ref, dtb_ref, alog_ref, dexp_ref, nw_ref,
             tri_ref, triT_ref, exp_ref, expT_ref, shift_ref,
             dp_ref, dcw_ref, dcb_ref, ddtb_ref, dalog_ref, dD_ref, dnw_ref,
             dht_ref, ext_ref, dpre_ref, dy_s, dE_s, dxdt_s, dxc_s, dDacc_ref, dAacc_ref):
        i = pl.program_id(0)

        @pl.when(i == 0)
        def _():
            for r in (dht_ref, dcw_ref, dcb_ref, ddtb_ref, dnw_ref, dDacc_ref, dAacc_ref):
                r[...] = jnp.zeros_like(r)
            dpre_ref[T:T + HALO_BLK, :] = jnp.zeros((HALO_BLK, XBC_W), F32)

        @pl.when(i == nsteps - 1)
        def _():
            ext_ref[0:HALO_BLK, :] = jnp.zeros((HALO_BLK, XBC_W), BF16)

        @pl.when(i < nsteps - 1)
        def _():
            ext_ref[0:HALO_BLK, :] = halo_ref[...]

        ext_ref[HALO_BLK:HALO_BLK + T, :] = xbc_ref[...]
        diag, causal, lo, hi = _pair_masks()
        last_row = (lax.broadcasted_iota(jnp.int32, (CHUNK, 1), 0) == CHUNK - 1).astype(F32)
        for c in reversed(range(ncl)):
            q = _ssd_chunk_fwd(c, ext_ref, shift_ref, dt_ref, cw_ref, cb_ref, dtb_ref, alog_ref, tri_ref, exp_ref)
            rows = pl.ds(c * CHUNK, CHUNK)
            pre, sg, xc, dtr, dtv, A, E, dtE = (q[k] for k in ("pre", "sg", "xc", "dtr", "dtv", "A", "E", "dtE"))
            xs = xc[:, 0:D]
            total = E[CHUNK - 1:CHUNK, :]
            x_dt = xs * dtE
            eE = jnp.exp(E)
            wdec = jnp.exp(total - E)
            zb = zb_ref[rows, :].astype(F32)
            yv = y_ref[rows, :]
            sgz = _sigmoid(zb)
            sz = zb * sgz
            hh = yv * sz
            for g in range(SSD_GROUPS):
                gc = slice(g * GROUP_W, (g + 1) * GROUP_W)
                hg = hh[:, gc]
                r = lax.rsqrt(jnp.mean(hg * hg, axis=-1, keepdims=True) + EPS)
                dyb_g = dyb_ref[rows, gc].astype(F32)
                dn = dyb_g * nw_ref[:, gc]
                dnw_ref[0:1, gc] += jnp.sum(dyb_g * hg * r, axis=0, keepdims=True)
                dy_s[:, gc] = r * dn - hg * (r * r * r) * jnp.mean(dn * hg, axis=-1, keepdims=True)
            dhh = dy_s[...]
            dp_ref[rows, 0:D] = (dhh * yv * (sgz * (1.0 + zb * (1.0 - sgz)))).astype(BF16)
            dy = dhh * sz
            dy_s[...] = dy
            dDacc_ref[0:1, :] += jnp.sum(dy * xs, axis=0, keepdims=True)
            dxc_s[:, 0:D] = dy * dexp_ref[...]
            for g in range(SSD_GROUPS):
                gc = slice(g * GROUP_W, (g + 1) * GROUP_W)
                bcol = slice(D + g * STATE, D + (g + 1) * STATE)
                ccol = slice(D + SSD_GROUPS * STATE + g * STATE, D + SSD_GROUPS * STATE + (g + 1) * STATE)
                Bg = xc[:, bcol].astype(BF16)
                Cg = xc[:, ccol].astype(BF16)
                B2 = jnp.concatenate([Bg, Bg], axis=0)
                cb2 = _dot_nt(Cg, B2)
                htg = st_ref[c, :, gc]
                htb = htg.astype(BF16)
                dhn = dht_ref[:, gc]
                dhnb = dhn.astype(BF16)
                dyg = dy[:, gc]
                eEg = eE[:, gc]
                wg = wdec[:, gc]
                xdg = x_dt[:, gc]
                CH = _dot(Cg, htb)
                dCHb = (dyg * eEg).astype(BF16)
                dC = _dot_nt(dCHb, htb)
                dl = jnp.exp(total[:, gc])
                dht_prev = _dot_tn(Cg, dCHb) + dl * dhn
                dtot = jnp.sum(dhn * htg, axis=0, keepdims=True) * dl
                dxw = _dot(Bg, dhnb)
                dB = _dot_nt((xdg * wg).astype(BF16), dhnb)
                dwd = dxw * xdg * wg
                dtot = dtot + jnp.sum(dwd, axis=0, keepdims=True)
                dE_s[:, gc] = dyg * eEg * CH - dwd + last_row * dtot
                dxdt_s[:, gc] = dxw * wg
                dcb2 = jnp.zeros((CHUNK, LANE), F32)
                for jj in range(GROUP_W // LANE):
                    pc = slice(g * GROUP_W + jj * LANE, g * GROUP_W + (jj + 1) * LANE)
                    Ej = E[:, pc]
                    e2 = jnp.sum(Ej * diag, axis=0, keepdims=True)
                    Lp = jnp.exp(jnp.where(causal, Ej - e2, -1e30))
                    Mp = cb2 * Lp
                    xj = x_dt[:, pc]
                    xbd = jnp.concatenate([xj * lo, xj * hi], axis=0).astype(BF16)
                    dyj = dy[:, pc].astype(BF16)
                    dMp = _dot_nt(dyj, xbd)
                    dxbd = _dot_tn(Mp.astype(BF16), dyj)
                    dxdt_s[:, pc] += dxbd[0:CHUNK, :] * lo + dxbd[CHUNK:2 * CHUNK, :] * hi
                    dcb2 = dcb2 + dMp * Lp
                    dseg = dMp * Mp
                    dE_s[:, pc] += dseg - diag * jnp.sum(dseg, axis=0, keepdims=True)
                dcb2b = dcb2.astype(BF16)
                dC = dC + _dot(dcb2b, B2)
                dB2 = _dot_tn(dcb2b, Cg)
                dB = dB + dB2[0:CHUNK, :] + dB2[CHUNK:2 * CHUNK, :]
                dxc_s[:, bcol] = dB
                dxc_s[:, ccol] = dC
                dht_ref[:, gc] = dht_prev
            dx_dt = dxdt_s[...]
            dxc_s[:, 0:D] += dx_dt * dtE
            red = _sel_right(jnp.concatenate([dE_s[...], dx_dt * xs], axis=0), expT_ref[...])
            da = _sel_left(triT_ref[...], red[0:CHUNK, :])
            ddtv = red[CHUNK:2 * CHUNK, :] + da * A
            dAacc_ref[0:1, :] += jnp.sum(da * dtv, axis=0, keepdims=True)
            ddtr = ddtv * _sigmoid(dtr)
            ddtb_ref[0:1, :] += jnp.sum(ddtr, axis=0, keepdims=True)
            dp_ref[rows, D + XBC_W:D + XBC_W + DT_W] = ddtr.astype(BF16)
            dpre = dxc_s[...] * (sg * (1.0 + pre * (1.0 - sg)))
            dpre_ref[rows, :] = dpre
            dcb_ref[0:1, :] += jnp.sum(dpre, axis=0, keepdims=True)
            for k in range(CONV_K):
                dcw_ref[k:k + 1, :] += jnp.sum(dpre * q["taps"][k], axis=0, keepdims=True)
        dxbc = jnp.zeros((T, XBC_W), F32)
        for k in range(CONV_K):
            dxbc = dxbc + cw_ref[k:k + 1, :] * dpre_ref[pl.ds(CONV_K - 1 - k, T), :]
        dp_ref[:, D:D + XBC_W] = dxbc.astype(BF16)
        dp_ref[:, SEG_SSD[1]:SSD_W] = jnp.zeros((T, SSD_W - SEG_SSD[1]), BF16)
        dpre_ref[T:T + HALO, :] = dpre_ref[0:HALO, :]

        @pl.when(i == nsteps - 1)
        def _():
            dalog_ref[...] = dAacc_ref[...] * (-jnp.exp(alog_ref[...]))
            dD_ref[...] = _dot(dDacc_ref[...], expT_ref[...].astype(F32), precision=HI)

    full = lambda a: pl.BlockSpec(a.shape, lambda i: (0,) * a.ndim)
    hb = T // HALO_BLK
    rev = lambda i: nsteps - 1 - i
    acc = lambda w: pl.BlockSpec((8, w), lambda i: (0, 0))
    return pl.pallas_call(
        body, name="ssd_bwd", grid=(nsteps,),
        in_specs=[pl.BlockSpec((T, D), lambda i: (rev(i), OFF_ZB // D)),
                  pl.BlockSpec((T, XBC_W), lambda i: (rev(i), OFF_XBC // XBC_W)),
                  pl.BlockSpec((HALO_BLK, XBC_W), lambda i: (jnp.maximum(rev(i) * hb - 1, 0), OFF_XBC // XBC_W)),
                  pl.BlockSpec((T, DT_W), lambda i: (rev(i), OFF_DT // DT_W)),
                  pl.BlockSpec((T, D), lambda i: (rev(i), 0)), pl.BlockSpec((T, D), lambda i: (rev(i), 0)),
                  pl.BlockSpec((ncl, STATE, D), lambda i: (rev(i), 0, 0)),
                  full(conv_w), full(conv_b), full(dtb_p), full(alog_p), full(d_exp), full(norm_w),
                  full(tri), full(triT), full(expand), full(expandT), full(shift)],
        out_specs=[pl.BlockSpec((T, SSD_W), lambda i: (rev(i), 0)),
                   acc(XBC_W), acc(XBC_W), acc(DT_W), acc(DT_W), acc(DT_W), acc(D)],
        out_shape=[jax.ShapeDtypeStruct((S, SSD_W), BF16),
                   jax.ShapeDtypeStruct((8, XBC_W), F32), jax.ShapeDtypeStruct((8, XBC_W), F32),
                   jax.ShapeDtypeStruct((8, DT_W), F32), jax.ShapeDtypeStruct((8, DT_W), F32),
                   jax.ShapeDtypeStruct((8, DT_W), F32), jax.ShapeDtypeStruct((8, D), F32)],
        scratch_shapes=[pltpu.VMEM((STATE, D), F32), pltpu.VMEM((HALO_BLK + T, XBC_W), BF16), pltpu.VMEM((T + HALO_BLK, XBC_W), F32),
                        pltpu.VMEM((CHUNK, D), F32), pltpu.VMEM((CHUNK, D), F32), pltpu.VMEM((CHUNK, D), F32),
                        pltpu.VMEM((CHUNK, XBC_W), F32), pltpu.VMEM((8, D), F32), pltpu.VMEM((8, DT_W), F32)],
        compiler_params=_cp(("arbitrary",)),
    )(proj, proj, proj, proj, dyb, y, states, conv_w, conv_b, dtb_p, alog_p, d_exp, norm_w, tri, triT, expand, expandT, shift)


def _head(x, ya, yb, proj, target, gate_b, wout, fw, *, tm):
    S = x.shape[0]

    def body(x_ref, ya_ref, yb_ref, gl0_ref, gl1_ref, t_ref, gb_ref, w_ref, fw_ref,
             dh_ref, dhb_ref, mb_ref, dya_ref, dyb_ref, dgl_ref, loss_ref, dfw_ref, dgb_ref):
        @pl.when(pl.program_id(0) == 0)
        def _():
            loss_ref[...] = jnp.zeros_like(loss_ref)
            dfw_ref[...] = jnp.zeros_like(dfw_ref)
            dgb_ref[...] = jnp.zeros_like(dgb_ref)

        ya_v = ya_ref[...].astype(F32)
        yb_v = yb_ref[...].astype(F32)
        g0 = _sigmoid(gl0_ref[...].astype(F32) + gb_ref[:, 0:D])
        g1 = _sigmoid(gl1_ref[...].astype(F32) + gb_ref[:, D:2 * D])
        mb = (g0 * ya_v + g1 * yb_v).astype(BF16)
        mb_ref[...] = mb
        h = x_ref[...] + _dot(mb, w_ref[...])
        r = lax.rsqrt(jnp.mean(h * h, axis=-1, keepdims=True) + EPS)
        hn = h * r
        err = hn * fw_ref[...] - t_ref[...]
        loss_ref[...] += 0.5 * jnp.sum(jnp.mean(err * err, axis=-1, keepdims=True))
        dyf = err * (1.0 / D)
        dfw_ref[0:1, :] += jnp.sum(dyf * hn, axis=0, keepdims=True)
        dhn = dyf * fw_ref[...]
        dh = r * (dhn - hn * jnp.mean(dhn * hn, axis=-1, keepdims=True))
        dh_ref[...] = dh
        dhb = dh.astype(BF16)
        dhb_ref[...] = dhb
        dm = _dot_nt(dhb, w_ref[...])
        dya_ref[...] = (dm * g0).astype(BF16)
        dyb_ref[...] = (dm * g1).astype(BF16)
        dgl0 = dm * ya_v * g0 * (1.0 - g0)
        dgl1 = dm * yb_v * g1 * (1.0 - g1)
        dgl_ref[:, 0:D] = dgl0.astype(BF16)
        dgl_ref[:, D:2 * D] = dgl1.astype(BF16)
        dgb_ref[0:1, 0:D] += jnp.sum(dgl0, axis=0, keepdims=True)
        dgb_ref[0:1, D:2 * D] += jnp.sum(dgl1, axis=0, keepdims=True)

    row = pl.BlockSpec((tm, D), lambda i: (i, 0))
    seg = lambda off: pl.BlockSpec((tm, D), lambda i: (i, off // D))
    full = lambda a: pl.BlockSpec(a.shape, lambda i: (0,) * a.ndim)
    acc = lambda w: pl.BlockSpec((8, w), lambda i: (0, 0))
    return pl.pallas_call(
        body, name="head", grid=(S // tm,),
        in_specs=[row, row, row, seg(OFF_G0), seg(OFF_G1), row, full(gate_b), full(wout), full(fw)],
        out_specs=[row, row, row, row, row, pl.BlockSpec((tm, 2 * D), lambda i: (i, 0)), acc(LANE), acc(D), acc(2 * D)],
        out_shape=[jax.ShapeDtypeStruct((S, D), F32), jax.ShapeDtypeStruct((S, D), BF16), jax.ShapeDtypeStruct((S, D), BF16),
                   jax.ShapeDtypeStruct((S, D), BF16), jax.ShapeDtypeStruct((S, D), BF16), jax.ShapeDtypeStruct((S, 2 * D), BF16),
                   jax.ShapeDtypeStruct((8, LANE), F32), jax.ShapeDtypeStruct((8, D), F32), jax.ShapeDtypeStruct((8, 2 * D), F32)],
        compiler_params=_cp(("arbitrary",)),
    )(x, ya, yb, proj, proj, target, gate_b, wout, fw)


def _adam_update(g, w_ref, m_ref, v_ref, g_ref, d_ref, m2_ref, v2_ref):
    m2 = ADAM_B1 * m_ref[...] + (1.0 - ADAM_B1) * g
    v2 = ADAM_B2 * v_ref[...] + (1.0 - ADAM_B2) * (g * g)
    m_hat = m2 / (1.0 - ADAM_B1 ** ADAM_STEP)
    v_hat = v2 / (1.0 - ADAM_B2 ** ADAM_STEP)
    g_ref[...] = g
    d_ref[...] = -ADAM_LR * (m_hat / (jnp.sqrt(v_hat) + ADAM_EPS) + ADAM_WD * w_ref[...])
    m2_ref[...] = m2
    v2_ref[...] = v2


def _adamw_own(me, own, landed, w, m, v, *, tr, tc, name):
    _, R, C = landed.shape
    assert R % tr == 0 and C % tc == 0, (name, R, C, tr, tc)

    def body(me_ref, own_ref, p_ref, w_ref, m_ref, v_ref, g_ref, d_ref, m2_ref, v2_ref):
        mine = own_ref[0].astype(F32)
        g = jnp.where(me_ref[0] == 0, mine, p_ref[0].astype(F32))
        for k in range(1, N_DEV):
            g = g + jnp.where(me_ref[0] == k, mine, p_ref[k].astype(F32))
        _adam_update(g, w_ref, m_ref, v_ref, g_ref, d_ref, m2_ref, v2_ref)

    tile = pl.BlockSpec((tr, tc), lambda i, j, me_ref: (i, j))
    return pl.pallas_call(
        body, name=name,
        grid_spec=pltpu.PrefetchScalarGridSpec(
            num_scalar_prefetch=1, grid=(R // tr, C // tc),
            in_specs=[pl.BlockSpec((1, tr, tc), lambda i, j, me_ref: (me_ref[0], i, j)),
                      pl.BlockSpec((N_DEV, tr, tc), lambda i, j, me_ref: (0, i, j)), tile, tile, tile],
            out_specs=[tile, tile, tile, tile]),
        out_shape=[jax.ShapeDtypeStruct((R, C), F32)] * 4,
        compiler_params=_cp(("parallel", "parallel")),
    )(me, own, landed, w, m, v)


def _adamw(parts, w, m, v, *, tr, name):
    _, R, C = parts.shape
    assert R % tr == 0, (name, R, tr)

    def body(p_ref, w_ref, m_ref, v_ref, g_ref, d_ref, m2_ref, v2_ref):
        g = p_ref[0].astype(F32)
        for k in range(1, N_DEV):
            g = g + p_ref[k].astype(F32)
        _adam_update(g, w_ref, m_ref, v_ref, g_ref, d_ref, m2_ref, v2_ref)

    row = pl.BlockSpec((tr, C), lambda i: (i, 0))
    return pl.pallas_call(
        body, name=name, grid=(R // tr,),
        in_specs=[pl.BlockSpec((N_DEV, tr, C), lambda i: (0, i, 0)), row, row, row],
        out_specs=[row, row, row, row],
        out_shape=[jax.ShapeDtypeStruct((R, C), F32)] * 4,
        compiler_params=_cp(("parallel",)),
    )(parts, w, m, v)


def _place():
    x, y, c = lax.axis_index("x"), lax.axis_index("y"), lax.axis_index("c")
    return x, y, c


def _all_gather(arrs, *, name):
    n = len(arrs)

    def body(*refs):
        ins, outs = refs[:n], refs[n:2 * n]
        send_sems, recv_sems, local_sems = refs[2 * n:]
        x, y, c = _place()
        me, sibling = (x, y, c), (x, y, 1 - c)
        chips = [(1 - x, y), (x, 1 - y), (1 - x, 1 - y)]

        def idx(px, py, pc):
            return 4 * px + 2 * py + pc

        def copy(k, a, block, to, src=None):
            slab = outs[a].at[idx(*block)]
            return pltpu.make_async_remote_copy(
                src_ref=slab if src is None else src, dst_ref=slab,
                send_sem=send_sems.at[k, a], recv_sem=recv_sems.at[k, a], device_id=to, device_id_type=MESH)

        mine = [pltpu.make_async_copy(ins[a], outs[a].at[idx(*me)], local_sems.at[a]) for a in range(n)]
        for cp in mine:
            cp.start()
        first = []
        for a in range(n):
            first.append(copy(0, a, me, sibling, src=ins[a]))
            first += [copy(1 + j, a, me, (*chip, c), src=ins[a]) for j, chip in enumerate(chips)]
        for cp in first:
            cp.start()
        passed = []
        for j, chip in enumerate(chips):
            for a in range(n):
                copy(1 + j, a, (*chip, c), me).wait_recv()
                fwd = copy(4 + j, a, (*chip, c), sibling)
                fwd.start()
                passed.append(fwd)
        for a in range(n):
            copy(0, a, sibling, me).wait_recv()
            for j, chip in enumerate(chips):
                copy(4 + j, a, (*chip, 1 - c), me).wait_recv()
        for cp in first + passed:
            cp.wait_send()
        for cp in mine:
            cp.wait()

    anyspec = pl.BlockSpec(memory_space=pl.ANY)
    return pl.pallas_call(
        body, name=name,
        in_specs=[anyspec] * n, out_specs=[anyspec] * n,
        out_shape=[jax.ShapeDtypeStruct((N_DEV,) + a.shape, a.dtype) for a in arrs],
        scratch_shapes=[pltpu.SemaphoreType.DMA((7, n)), pltpu.SemaphoreType.DMA((7, n)), pltpu.SemaphoreType.DMA((n,))],
    )(*arrs)


W_ROWS = SEG_SSD[0] + SSD_PAD_W


GROUP = 16
INTERIOR = 1920


def _interior(k):
    lo = -(-(k * SHARD_IN) // GROUP) * GROUP
    hi = ((k + 1) * SHARD_IN) // GROUP * GROUP
    return lo, hi


def _dest_row(r):
    if r < REF_SGU_END:
        return r
    return r - REF_SGU_END + SEG_SSD[0] if r < REF_GATE_START else r - REF_GATE_START + SEG_GATE[0]


def _shard_pieces(k):
    lo_k, hi_k = _interior(k)
    out = []
    for lo, hi in ((0, REF_SGU_END), (REF_SGU_END, REF_GATE_START), (REF_GATE_START, W_IN)):
        a, b = max(lo, lo_k), min(hi, hi_k)
        if a < b:
            out.append((a - lo_k, b - a, _dest_row(a)))
    return out


GATHER_PARTS = 2


def _shard_parts(k):
    parts = [[] for _ in range(GATHER_PARTS)]
    for s0, n, d0 in _shard_pieces(k):
        step = -(-(n // GROUP) // GATHER_PARTS) * GROUP
        for p in range(GATHER_PARTS):
            a, b = min(p * step, n), min((p + 1) * step, n)
            if a < b:
                parts[p].append((s0 + a, b - a, d0 + a))
    return parts


def _patch_straddlers(wpT, heads, tails):
    for k in range(1, N_DEV):
        m = (k * SHARD_IN) % GROUP
        if m:
            group = jnp.concatenate([tails[k - 1, GROUP - m:], heads[k, :GROUP - m]], axis=0)
            wpT = lax.dynamic_update_slice(wpT, group, (_dest_row(k * SHARD_IN - m), 0))
    return wpT


def _gather_stages(k, win_ref, small, z_ref, n_zero, w_ref, send_sems, recv_sems, local_sems):
    x, y, c = k // 4, (k // 2) % 2, k % 2
    idx = lambda p: 4 * p[0] + 2 * p[1] + p[2]
    me, sib = (x, y, c), (x, y, 1 - c)
    xn, yn, dg = (1 - x, y, c), (x, 1 - y, c), (1 - x, 1 - y, c)
    parts = range(GATHER_PARTS)

    def copies(slot, block, to, part, own=False):
        kb = idx(block)
        out = []
        for j, (s0, n, d0) in enumerate(_shard_parts(kb)[part]):
            dst = w_ref.at[pl.ds(d0, n)]
            out.append((win_ref.at[pl.ds(s0, n)] if own else dst, dst, 2 * part + j))
        if part == 0:
            for j, (src, gathered) in enumerate(small):
                out.append((src if own else gathered.at[kb], gathered.at[kb], 2 * GATHER_PARTS + j))
        return [pltpu.make_async_remote_copy(src_ref=s, dst_ref=d, send_sem=send_sems.at[slot, j], recv_sem=recv_sems.at[slot, j],
                                             device_id=to, device_id_type=MESH) for s, d, j in out]

    def start(cps):
        for cp in cps:
            cp.start()

    def arrived(slot, block, part):
        for cp in copies(slot, block, me, part):
            cp.wait_recv()

    def local():
        pairs = [(win_ref.at[pl.ds(s0, n)], w_ref.at[pl.ds(d0, n)]) for s0, n, d0 in _shard_pieces(k)]
        pairs += [(src, gathered.at[k]) for src, gathered in small] + [(z_ref, w_ref.at[pl.ds(W_IN, n_zero)])]
        return [pltpu.make_async_copy(s, d, local_sems.at[j]) for j, (s, d) in enumerate(pairs)]

    relay = (xn, yn) if c == 1 else (yn, xn)

    def first():
        start(local())
        for p in parts:
            start(copies(0, me, sib, p, own=True) + copies(1, me, xn, p, own=True) + copies(2, me, yn, p, own=True))

    def hand_on():
        for p in parts:
            arrived(1, xn, p)
            start(copies(4, xn, sib, p))
            if c == 1:
                start(copies(3, *relay, p))
            arrived(2, yn, p)
            start(copies(5, yn, sib, p))
            if c == 0:
                start(copies(3, *relay, p))

    def finish():
        for p in parts:
            arrived(3, dg, p)
            start(copies(6, dg, sib, p))
        for p in parts:
            arrived(0, sib, p)
            arrived(4, (1 - x, y, 1 - c), p)
            arrived(5, (x, 1 - y, 1 - c), p)
            arrived(6, (1 - x, 1 - y, 1 - c), p)
        for p in parts:
            sent = (copies(0, me, sib, p, own=True) + copies(1, me, xn, p, own=True) + copies(2, me, yn, p, own=True)
                    + copies(3, *relay, p) + copies(4, xn, sib, p) + copies(5, yn, sib, p) + copies(6, dg, sib, p))
            for cp in sent:
                cp.wait_send()
        for cp in local():
            cp.wait()

    return first, hand_on, finish


def _gather_sems(n_small):
    n_arr = 2 * GATHER_PARTS + n_small
    return [pltpu.SemaphoreType.DMA((7, n_arr)), pltpu.SemaphoreType.DMA((7, n_arr)), pltpu.SemaphoreType.DMA((n_arr + 1,))]


def _gather_weights(win, head, tail, wout, cw, zeros):
    small_in = (wout, cw, head, tail)
    n_zero = zeros.shape[0]
    assert W_IN + n_zero == W_ROWS and W_IN % GROUP == 0

    def body(win_ref, wout_ref, cw_ref, head_ref, tail_ref, z_ref, w_ref, gout_ref, gcw_ref, ghead_ref, gtail_ref, *sems):
        x, y, c = _place()
        me = 4 * x + 2 * y + c
        small = ((wout_ref, gout_ref), (cw_ref, gcw_ref), (head_ref, ghead_ref), (tail_ref, gtail_ref))

        def run(k):
            for stage in _gather_stages(k, win_ref, small, z_ref, n_zero, w_ref, *sems):
                stage()

        for k in range(N_DEV):
            pl.when(me == k)(functools.partial(run, k))

    anyspec = pl.BlockSpec(memory_space=pl.ANY)
    return pl.pallas_call(
        body, name="gather_weights", in_specs=[anyspec] * 6, out_specs=[anyspec] * 5,
        out_shape=[jax.ShapeDtypeStruct((W_ROWS, D), win.dtype)]
        + [jax.ShapeDtypeStruct((N_DEV,) + a.shape, a.dtype) for a in small_in],
        scratch_shapes=_gather_sems(len(small_in)),
    )(win, wout, cw, head, tail, zeros)


_REL = [(dx, dy, dc) for dx in (0, 1) for dy in (0, 1) for dc in (0, 1)][1:]
_HBM = pl.BlockSpec(memory_space=pltpu.HBM)
_SEM = pl.BlockSpec(memory_space=pltpu.SEMAPHORE)
_EFFECT = pltpu.SideEffectType.DATAFLOW_SIDE_EFFECTING


def _peer(k):
    x, y, c = _place()
    dx, dy, dc = _REL[k]
    return (1 - x if dx else x, 1 - y if dy else y, 1 - c if dc else c)


def _exchange_start(parts, *, name):
    n = len(parts)

    def body(*refs):
        ins, lands = refs[:n], refs[n:2 * n]
        send_sems, recv_sems, token = refs[2 * n], refs[2 * n + 1], refs[-1]
        x, y, c = _place()
        me = 4 * x + 2 * y + c
        for a in range(n):
            for k in range(len(_REL)):
                px, py, pc = _peer(k)
                pltpu.make_async_remote_copy(
                    src_ref=ins[a].at[4 * px + 2 * py + pc], dst_ref=lands[a].at[me],
                    send_sem=send_sems.at[len(_REL) * a + k], recv_sem=recv_sems.at[len(_REL) * a + k],
                    device_id=(px, py, pc), device_id_type=MESH).start()
        token[...] = jnp.zeros_like(token)

    sem = pltpu.SemaphoreType.DMA((len(_REL) * n,))
    bufs = [pltpu.HBM(p.shape, p.dtype) for p in parts]
    outs = pl.pallas_call(
        body, name=name,
        out_shape=(sem, sem, *bufs, *bufs, jax.ShapeDtypeStruct((8, LANE), F32)),
        in_specs=(_HBM,) * (2 * n), out_specs=(_SEM, _SEM, *(_HBM,) * (2 * n), pl.BlockSpec(memory_space=pltpu.VMEM)),
        input_output_aliases={i: 2 + i for i in range(2 * n)},
        compiler_params=pltpu.CompilerParams(has_side_effects=_EFFECT),
    )(*[pltpu.with_memory_space_constraint(p, pltpu.HBM) for p in parts],
      *[pltpu.with_memory_space_constraint(lax.empty(p.shape, p.dtype), pltpu.HBM) for p in parts])
    return outs[0], outs[1], outs[2:2 + n], outs[2 + n:2 + 2 * n], outs[-1]


def _exchange_wait(send_sems, recv_sems, parts, lands, after, *, name):
    n = len(parts)

    def body(*refs):
        ins, lands_ = refs[:n], refs[n:2 * n]
        ssem, rsem = refs[2 * n], refs[2 * n + 1]
        for a in range(n):
            for k in range(len(_REL)):
                px, py, pc = _peer(k)
                p = 4 * px + 2 * py + pc
                cp = pltpu.make_async_remote_copy(
                    src_ref=ins[a].at[p], dst_ref=lands_[a].at[p],
                    send_sem=ssem.at[len(_REL) * a + k], recv_sem=rsem.at[len(_REL) * a + k],
                    device_id=(px, py, pc), device_id_type=MESH)
                cp.wait_send()
                cp.wait_recv()

    bufs = [pltpu.HBM(p.shape, p.dtype) for p in parts]
    outs = pl.pallas_call(
        body, name=name, out_shape=(*bufs, *bufs),
        in_specs=(*(_HBM,) * (2 * n), _SEM, _SEM, pl.BlockSpec(memory_space=pl.ANY)), out_specs=(_HBM,) * (2 * n),
        input_output_aliases={i: i for i in range(2 * n)},
        compiler_params=pltpu.CompilerParams(has_side_effects=_EFFECT),
    )(*parts, *lands, send_sems, recv_sems, after)
    return outs[:n], outs[n:]


WEIGHTS = ('norm_w', 'w_in', 'gate_b', 'sgu_norm_g', 'sgu_norm_b', 'sgu_w', 'sgu_b', 'conv_w', 'conv_b', 'dt_bias', 'A_log',
           'D_skip', 'ssd_norm_w', 'w_out', 'final_norm_w')
SHARDED = ('w_in', 'conv_w', 'w_out')
PACK_ROW = 8 * LANE


def _constants():
    tri = np.tril(np.ones((CHUNK, CHUNK), np.float32))
    expand = np.zeros((DT_W, D), np.float32)
    for h in range(HEADS):
        expand[h, h * HEADDIM:(h + 1) * HEADDIM] = 1.0
    sel = np.zeros((D, LANE), np.float32)
    for g in range(SGU_GROUPS):
        sel[g * LANE:(g + 1) * LANE, g] = 1.0
    pos_chunk = np.arange(SGU_BLOCK) // CHUNK
    mask = (pos_chunk[None, :] <= pos_chunk[:, None]).astype(np.float32)
    shift = np.zeros(((CONV_K - 1) * CHUNK, HALO_BLK + CHUNK), np.float32)
    for kk in range(CONV_K - 1):
        for t in range(CHUNK):
            shift[kk * CHUNK + t, HALO_BLK - (CONV_K - 1) + t + kk] = 1.0
    return dict(tri=jnp.asarray(tri, BF16), triT=jnp.asarray(tri.T.copy(), BF16), expand=jnp.asarray(np.tile(expand, (3, 1)), BF16),
                shift=jnp.asarray(shift, BF16),
                expandT=jnp.asarray(expand.T.copy(), BF16), sel=jnp.asarray(sel), mask=jnp.asarray(mask))


def _to_shards(segs):
    starts = np.cumsum([0] + [n for _, n in segs])
    assert starts[-1] == W_IN
    slabs = []
    for k in range(N_DEV):
        pieces = []
        for (s, n), s0 in zip(segs, starts[:-1]):
            lo, hi = max(k * SHARD_IN, s0), min((k + 1) * SHARD_IN, s0 + n)
            if lo < hi:
                pieces.append(s[lo - s0:hi - s0])
        slabs.append(jnp.concatenate(pieces, axis=0))
    return jnp.stack(slabs)


def _local_step(x2, tgt, wpT, wout, cw, p, exchange_small, exchange):
    S = x2.shape[0]
    k = _constants()
    xn, proj = _in_proj(x2, p['norm_w'], wpT, tm=min(1024, S), tn=2048)
    wm32 = p['sgu_w'][0] * k['mask']
    wm = wm32.astype(BF16)
    wmT = jnp.swapaxes(wm32, 1, 2).astype(BF16)
    bias_full = jnp.repeat(p['sgu_b'][0].T, LANE, axis=1)
    tm_sgu = min(256, S)
    ya = _sgu_fwd(proj, p['sgu_norm_g'], p['sgu_norm_b'], wm, bias_full, tm=tm_sgu)
    pad32 = lambda a: jnp.pad(a, ((0, 0), (0, DT_W - HEADS)))
    dtb_p, alog_p = pad32(p['dt_bias']), pad32(p['A_log'])
    d_exp = jnp.repeat(p['D_skip'], HEADDIM, axis=1)
    ssd_args = (cw, p['conv_b'], dtb_p, alog_p, d_exp, p['ssd_norm_w'])
    y, yb, states = _ssd_fwd(proj, *ssd_args, k['tri'], k['expand'], k['shift'])
    dh, dhb, mb, dya, dyb, dgl, loss, dfw, dgb = _head(
        x2, ya, yb, proj, tgt, p['gate_b'], wout, p['final_norm_w'][None, :], tm=min(256, S))
    dsgu, dws, dbsT, dsg, dsb = _sgu_bwd(proj, dya, p['sgu_norm_g'], p['sgu_norm_b'], wm, wmT, bias_full, k['mask'], k['sel'],
                                         tm=tm_sgu)
    dssd, dcw, dcb, ddtb, dalog, dD, dnw = _ssd_bwd(proj, dyb, y, states, *ssd_args, k['tri'], k['triT'], k['expand'], k['expandT'],
                                                    k['shift'])
    grads = dict(
        gate_b=dgb[0:1], sgu_norm_g=dsg[0:1], sgu_norm_b=dsb[0:1], sgu_w=dws[None],
        sgu_b=dbsT[:, :SGU_GROUPS].T[None], conv_w=dcw[0:CONV_K][None], conv_b=dcb[0:1], dt_bias=ddtb[0:1, :HEADS],
        A_log=dalog[0:1, :HEADS], D_skip=dD[0:1, :HEADS], ssd_norm_w=dnw[0:1], final_norm_w=dfw[0])
    token = exchange_small(loss[0, 0], grads)
    tw = dict(trans_a=True, out_dtype=BF16, tm=1024, tn=512, tk=S)
    dwT_sgu = _matmul(dsgu, xn, after=token, name="dw_in_sgu", **tw)
    dwT_gate = _matmul(dgl, xn, name="dw_in_gate", **tw)
    dwT_ssd = _matmul(dssd, xn, name="dw_in_ssd", **tw)
    dw_out = _matmul(mb, dhb, name="dw_out", **tw)
    tn = 1024
    token = exchange([(dwT_sgu, SEG_SGU[1]), (dwT_ssd, W_IN - SEG_SSD[0]), (dwT_gate, SEG_GATE[1])], dw_out)
    tm = min(1024, S)
    dxn = _matmul(dsgu, wpT, tm=tm, tn=tn, tk=3072, after=token, name="dxn_sgu")
    dxn = _matmul(dgl, wpT, b_koff=SEG_GATE[0] // 2048, tm=tm, tn=tn, tk=2048, add=dxn, name="dxn_gate")
    dxn = _matmul(dssd, wpT, b_koff=SEG_SSD[0] // 2048, tm=tm, tn=tn, tk=2048, add=dxn, name="dxn_ssd")
    grad_x, dnorm = _norm_bwd(x2, p['norm_w'], dxn, dh, tm=min(256, S))
    return grad_x, dnorm[0:1]


def _pack(arrs):
    rows, offs, r = [], [], 0
    for a in arrs:
        n = a.size
        nr = -(-n // PACK_ROW) * 8
        rows.append(jnp.pad(a.reshape(-1).astype(F32), (0, nr * LANE - n)).reshape(nr, LANE))
        offs.append(r)
        r += nr
    return jnp.concatenate(rows, axis=0), offs


def kernel(x, norm_w, w_in, gate_b, sgu_norm_g, sgu_norm_b, sgu_w, sgu_b, conv_w, conv_b, dt_bias, A_log, D_skip, ssd_norm_w, w_out, final_norm_w, loss_target, m_norm_w, m_w_in, m_gate_b, m_sgu_norm_g, m_sgu_norm_b, m_sgu_w, m_sgu_b, m_conv_w, m_conv_b, m_dt_bias, m_A_log, m_D_skip, m_ssd_norm_w, m_w_out, m_final_norm_w, v_norm_w, v_w_in, v_gate_b, v_sgu_norm_g, v_sgu_norm_b, v_sgu_w, v_sgu_b, v_conv_w, v_conv_b, v_dt_bias, v_A_log, v_D_skip, v_ssd_norm_w, v_w_out, v_final_norm_w):
    w = dict(norm_w=norm_w, w_in=w_in, gate_b=gate_b, sgu_norm_g=sgu_norm_g, sgu_norm_b=sgu_norm_b, sgu_w=sgu_w, sgu_b=sgu_b,
             conv_w=conv_w, conv_b=conv_b, dt_bias=dt_bias, A_log=A_log, D_skip=D_skip, ssd_norm_w=ssd_norm_w, w_out=w_out,
             final_norm_w=final_norm_w)
    m = dict(norm_w=m_norm_w, w_in=m_w_in, gate_b=m_gate_b, sgu_norm_g=m_sgu_norm_g, sgu_norm_b=m_sgu_norm_b, sgu_w=m_sgu_w,
             sgu_b=m_sgu_b, conv_w=m_conv_w, conv_b=m_conv_b, dt_bias=m_dt_bias, A_log=m_A_log, D_skip=m_D_skip,
             ssd_norm_w=m_ssd_norm_w, w_out=m_w_out, final_norm_w=m_final_norm_w)
    v = dict(norm_w=v_norm_w, w_in=v_w_in, gate_b=v_gate_b, sgu_norm_g=v_sgu_norm_g, sgu_norm_b=v_sgu_norm_b, sgu_w=v_sgu_w,
             sgu_b=v_sgu_b, conv_w=v_conv_w, conv_b=v_conv_b, dt_bias=v_dt_bias, A_log=v_A_log, D_skip=v_D_skip,
             ssd_norm_w=v_ssd_norm_w, w_out=v_w_out, final_norm_w=v_final_norm_w)
    me = 4 * lax.axis_index("x") + 2 * lax.axis_index("y") + lax.axis_index("c")
    shard_cw = XBC_W // N_DEV

    tpose = lambda a: jnp.swapaxes(a[0], 0, 1)
    wT = tpose(w_in).astype(BF16)
    first_group = (GROUP - (me * SHARD_IN) % GROUP) % GROUP
    window = lax.dynamic_slice(jnp.pad(wT, ((0, GROUP), (0, 0))), (first_group, 0), (INTERIOR, D))
    wpT, g_out, g_cw, heads, tails = _gather_weights(window, wT[:GROUP], wT[SHARD_IN - GROUP:], w_out[0].astype(BF16),
                                                     conv_w[0], jnp.zeros((W_ROWS - W_IN, D), BF16))
    wpT = _patch_straddlers(wpT, heads, tails)
    wout_full = g_out.reshape(D, D)
    cw_full = jnp.swapaxes(g_cw, 0, 1).reshape(CONV_K, XBC_W)

    flight = {}

    small = [n for n in WEIGHTS if n not in SHARDED and n != 'norm_w']
    early = {}

    def exchange_small(loss_part, grads):
        early['packed'], early['offs'] = _pack([grads[n] for n in small] + [loss_part, grads['conv_w']])
        parts = [jnp.broadcast_to(early['packed'][None], (N_DEV,) + early['packed'].shape)]
        early['sems'], early['rsems'], early['parts'], early['lands'], token = _exchange_start(parts, name="small_start")
        return token

    def exchange(dw_inT_segs, dw_out):
        parts = [_to_shards(dw_inT_segs), dw_out.reshape(N_DEV, D // N_DEV, D)]
        flight['sems'], flight['rsems'], flight['parts'], flight['lands'], token = _exchange_start(parts, name="exchange_start")
        return token

    grad_x, dnorm = _local_step(x[0], loss_target[0], wpT, wout_full, cw_full, w, exchange_small, exchange)
    _, (land_small,) = _exchange_wait(early['sems'], early['rsems'], early['parts'], early['lands'], grad_x, name="small_wait")
    (own_in, own_out), (land_in, land_out) = _exchange_wait(
        flight['sems'], flight['rsems'], flight['parts'], flight['lands'], grad_x, name="exchange_wait")
    me_arr = jnp.reshape(me, (1,)).astype(jnp.int32)
    res = {}
    res['w_in'] = [jnp.swapaxes(o, 0, 1) for o in _adamw_own(
        me_arr, own_in, land_in, tpose(w_in), tpose(m_w_in), tpose(v_w_in), tr=SHARD_IN, tc=256, name="adamw_w_in")]
    res['w_out'] = _adamw_own(me_arr, own_out, land_out, w_out[0], m_w_out[0], v_w_out[0], tr=128, tc=D, name="adamw_w_out")

    (norm_parts,) = _all_gather([_pack([dnorm])[0]], name="gather_norm")
    norm_outs = _adamw(norm_parts, *[_pack([d['norm_w']])[0] for d in (w, m, v)], tr=norm_parts.shape[1], name="adamw_norm")
    res['norm_w'] = [o.reshape(-1)[:D].reshape(w['norm_w'].shape) for o in norm_outs]

    offs = early['offs']
    gathered = lax.dynamic_update_slice(land_small, early['packed'][None], (me, 0, 0))
    off_loss, off_cw = offs[-2], offs[-1]
    cw_parts = gathered[:, off_cw:, :].reshape(N_DEV, CONV_K, XBC_W)
    cw_parts = lax.dynamic_slice_in_dim(cw_parts, me * shard_cw, shard_cw, axis=2)
    cw_rows = _pack([cw_parts[0]])[0].shape[0]
    cw_parts = jnp.pad(cw_parts.reshape(N_DEV, -1), ((0, 0), (0, cw_rows * LANE - CONV_K * shard_cw))).reshape(N_DEV, cw_rows, LANE)
    parts = jnp.concatenate([gathered[:, :off_cw, :], cw_parts], axis=1)
    zero = jnp.zeros((), F32)
    packs = [_pack([d[n] for n in small] + [zero, d['conv_w']])[0] for d in (w, m, v)]
    outs = _adamw(parts, *packs, tr=parts.shape[1], name="adamw_small")

    def unpack(o, name):
        if name == 'conv_w':
            return o[off_cw:off_cw + cw_rows].reshape(-1)[:CONV_K * shard_cw].reshape(w['conv_w'].shape)
        r0 = offs[small.index(name)]
        n = w[name].size
        return o[r0:r0 + -(-n // PACK_ROW) * 8].reshape(-1)[:n].reshape(w[name].shape)

    for n in small + ['conv_w']:
        res[n] = [unpack(o, n) for o in outs]
    for n in ('w_in', 'w_out'):
        res[n] = [o[None] for o in res[n]]
    loss = outs[0][off_loss, 0]
    return (loss, grad_x[None], *[res[n][0] for n in WEIGHTS], *[res[n][1] for n in WEIGHTS],
            *[res[n][2] for n in WEIGHTS], *[res[n][3] for n in WEIGHTS])
```

```python
import functools

import numpy as np
import jax
import jax.numpy as jnp
from jax import lax
from jax.experimental import pallas as pl
from jax.experimental.pallas import tpu as pltpu

F32 = jnp.float32
BF16 = jnp.bfloat16
HI = lax.Precision.HIGHEST
MESH = pl.DeviceIdType.MESH

D = 2048
EPS = 1e-5
SGU_BLOCK = 128
SGU_GROUPS = 16
CHUNK = 64
HEADS = 32
HEADDIM = 64
SSD_GROUPS = 4
GROUP_W = D // SSD_GROUPS
STATE = 128
CONV_K = 4
XBC_W = D + 2 * SSD_GROUPS * STATE
W_IN = 15392
N_DEV = 8
SHARD_IN = W_IN // N_DEV
ADAM_LR, ADAM_B1, ADAM_B2, ADAM_EPS, ADAM_WD, ADAM_STEP = 0.001, 0.9, 0.999, 1e-08, 0.01, 10

REF_SGU_END = 3 * D
REF_GATE_START = W_IN - 2 * D
LANE = 128
DT_W = LANE
OFF_U, OFF_V, OFF_ZA, OFF_G0, OFF_G1, OFF_ZB = (i * D for i in range(6))
OFF_XBC = OFF_ZB + D
OFF_DT = OFF_XBC + XBC_W
SEG_SGU = (OFF_U, 3 * D)
SEG_GATE = (OFF_G0, 2 * D)
SEG_SSD = (OFF_ZB, D + XBC_W + DT_W)
WP = SEG_SSD[0] + SEG_SSD[1]
SSD_PAD_W = 3 * D
VMEM_BYTES = 64 * 1024 * 1024
VMEM_LIMIT = VMEM_BYTES - 8 * 1024 * 1024


def _cp(sem=None, vmem=VMEM_LIMIT):
    return pltpu.CompilerParams(dimension_semantics=sem, vmem_limit_bytes=vmem)


def _sigmoid(x):
    return 1.0 / (1.0 + jnp.exp(-x))


def _softplus(x):
    return jnp.maximum(x, 0.0) + jnp.log(1.0 + jnp.exp(-jnp.abs(x)))


def _dot(a, b, precision=None):
    return jnp.dot(a, b, preferred_element_type=F32, precision=precision)


def _dot_nt(a, b, precision=None):
    return lax.dot_general(a, b, (((1,), (1,)), ((), ())), preferred_element_type=F32, precision=precision)


def _dot_tn(a, b, precision=None):
    return lax.dot_general(a, b, (((0,), (0,)), ((), ())), preferred_element_type=F32, precision=precision)


def _split3(a):
    hi = a.astype(BF16)
    r = a - hi.astype(F32)
    mid = r.astype(BF16)
    return hi, mid, (r - mid.astype(F32)).astype(BF16)


def _sel_right(a, sel01):
    m = a.shape[0]
    r = _dot(jnp.concatenate(_split3(a), axis=0), sel01)
    return (r[0:m] + r[m:2 * m]) + r[2 * m:3 * m]


def _sel_right_k(a, sel01_x3):
    return _dot(jnp.concatenate(_split3(a), axis=1), sel01_x3)


def _sel_left(sel01, a):
    n = a.shape[1]
    r = _dot(sel01, jnp.concatenate(_split3(a), axis=1))
    return (r[:, 0:n] + r[:, n:2 * n]) + r[:, 2 * n:3 * n]


def _matmul(a, b, *, trans_a=False, trans_b=False, b_koff=0, out_dtype=F32, tm, tn, tk, add=None, after=None, name):
    K, M = a.shape if trans_a else a.shape[::-1]
    N = b.shape[0] if trans_b else b.shape[1]
    assert M % tm == 0 and N % tn == 0 and K % tk == 0 and not (trans_a and trans_b), (name, M, N, K, tm, tn, tk)
    nk = K // tk

    def body(*refs):
        a_ref, b_ref = refs[:2]
        add_ref = refs[2] if add is not None else None
        o_ref, acc_ref = refs[-2:]
        k = pl.program_id(2)
        if trans_a:
            part = _dot_tn(a_ref[...], b_ref[...])
        else:
            part = _dot_nt(a_ref[...], b_ref[...]) if trans_b else _dot(a_ref[...], b_ref[...])

        def result(r):
            if add_ref is not None:
                r = r + add_ref[...]
            return r.astype(out_dtype)

        if nk == 1:
            o_ref[...] = result(part)
        else:
            @pl.when(k == 0)
            def _():
                acc_ref[...] = part

            @pl.when(jnp.logical_and(k > 0, k < nk - 1))
            def _():
                acc_ref[...] += part

            @pl.when(k == nk - 1)
            def _():
                o_ref[...] = result(acc_ref[...] + part)

    in_specs = [pl.BlockSpec((tk, tm), lambda i, j, k: (k, i)) if trans_a else pl.BlockSpec((tm, tk), lambda i, j, k: (i, k)),
                pl.BlockSpec((tn, tk), lambda i, j, k: (j, k)) if trans_b else pl.BlockSpec((tk, tn), lambda i, j, k: (k + b_koff, j))]
    args = [a, b]
    if add is not None:
        in_specs.append(pl.BlockSpec((tm, tn), lambda i, j, k: (i, j)))
        args.append(add)
    if after is not None:
        in_specs.append(pl.BlockSpec(memory_space=pl.ANY))
        args.append(after)
    return pl.pallas_call(
        body, name=name, grid=(M // tm, N // tn, nk), in_specs=in_specs,
        out_specs=pl.BlockSpec((tm, tn), lambda i, j, k: (i, j)),
        out_shape=jax.ShapeDtypeStruct((M, N), out_dtype),
        scratch_shapes=[pltpu.VMEM((tm, tn), F32)],
        compiler_params=_cp(("parallel", "parallel", "arbitrary")),
    )(*args)


def _in_proj(x, w, wpT, *, tm, tn):
    S = x.shape[0]
    N = wpT.shape[0]
    assert S % tm == 0 and N % tn == 0, (S, N, tm, tn)

    def body(x_ref, w_ref, b_ref, xn_ref, o_ref, xs_ref):
        @pl.when(pl.program_id(1) == 0)
        def _():
            xv = x_ref[...]
            r = lax.rsqrt(jnp.mean(xv * xv, axis=-1, keepdims=True) + EPS)
            xs = (xv * r * w_ref[...]).astype(BF16)
            xs_ref[...] = xs
            xn_ref[...] = xs

        o_ref[...] = _dot_nt(xs_ref[...], b_ref[...]).astype(BF16)

    return pl.pallas_call(
        body, name="in_proj", grid=(S // tm, N // tn),
        in_specs=[pl.BlockSpec((tm, D), lambda i, j: (i, 0)), pl.BlockSpec((1, D), lambda i, j: (0, 0)),
                  pl.BlockSpec((tn, D), lambda i, j: (j, 0))],
        out_specs=[pl.BlockSpec((tm, D), lambda i, j: (i, 0)), pl.BlockSpec((tm, tn), lambda i, j: (i, j))],
        out_shape=[jax.ShapeDtypeStruct((S, D), BF16), jax.ShapeDtypeStruct((S, N), BF16)],
        scratch_shapes=[pltpu.VMEM((tm, D), BF16)],
        compiler_params=_cp(("parallel", "arbitrary")),
    )(x, w, wpT)


def _norm_bwd(x, w, dxn, dh, *, tm):
    S = x.shape[0]

    def body(x_ref, w_ref, dxn_ref, dh_ref, gx_ref, dw_ref):
        xv = x_ref[...]
        r = lax.rsqrt(jnp.mean(xv * xv, axis=-1, keepdims=True) + EPS)
        xh = xv * r
        dxn_v = dxn_ref[...]
        dxh = dxn_v * w_ref[...]
        gx_ref[...] = dh_ref[...] + r * (dxh - xh * jnp.mean(dxh * xh, axis=-1, keepdims=True))

        @pl.when(pl.program_id(0) == 0)
        def _():
            dw_ref[...] = jnp.zeros_like(dw_ref)

        dw_ref[0:1, :] += jnp.sum(dxn_v * xh, axis=0, keepdims=True)

    row = pl.BlockSpec((tm, D), lambda i: (i, 0))
    return pl.pallas_call(
        body, name="norm_bwd", grid=(S // tm,),
        in_specs=[row, pl.BlockSpec((1, D), lambda i: (0, 0)), row, row],
        out_specs=[row, pl.BlockSpec((8, D), lambda i: (0, 0))],
        out_shape=[jax.ShapeDtypeStruct((S, D), F32), jax.ShapeDtypeStruct((8, D), F32)],
        compiler_params=_cp(("arbitrary",)),
    )(x, w, dxn, dh)


def _sgu_core(u_ref, v_ref, z_ref, g_ref, b_ref, wm_ref, bias_ref, vnb_ref, mixed_ref, tm):
    v = v_ref[...].astype(F32)
    mu = jnp.mean(v, axis=-1, keepdims=True)
    vc = v - mu
    rs = lax.rsqrt(jnp.mean(vc * vc, axis=-1, keepdims=True) + EPS)
    vh = vc * rs
    vnb_ref[...] = (vh * g_ref[...] + b_ref[...]).astype(BF16)
    for blk in range(tm // SGU_BLOCK):
        rows = pl.ds(blk * SGU_BLOCK, SGU_BLOCK)
        for gi in range(SGU_GROUPS):
            cols = pl.ds(gi * LANE, LANE)
            mixed_ref[rows, cols] = _dot(wm_ref[gi], vnb_ref[rows, cols]) + bias_ref[:, cols]
    return vh, rs


def _sgu_fwd(proj, g, b, wm, bias_full, *, tm):
    S = proj.shape[0]

    def body(u_ref, v_ref, z_ref, g_ref, b_ref, wm_ref, bias_ref, y_ref, vnb_ref, mixed_ref):
        _sgu_core(u_ref, v_ref, z_ref, g_ref, b_ref, wm_ref, bias_ref, vnb_ref, mixed_ref, tm)
        z = z_ref[...].astype(F32)
        y_ref[...] = (u_ref[...].astype(F32) * mixed_ref[...] * (z * _sigmoid(z))).astype(BF16)

    seg = lambda off: pl.BlockSpec((tm, D), lambda i: (i, off // D))
    full = lambda a: pl.BlockSpec(a.shape, lambda i: (0,) * a.ndim)
    return pl.pallas_call(
        body, name="sgu_fwd", grid=(S // tm,),
        in_specs=[seg(OFF_U), seg(OFF_V), seg(OFF_ZA), full(g), full(b), full(wm), full(bias_full)],
        out_specs=pl.BlockSpec((tm, D), lambda i: (i, 0)),
        out_shape=jax.ShapeDtypeStruct((S, D), BF16),
        scratch_shapes=[pltpu.VMEM((tm, D), BF16), pltpu.VMEM((tm, D), F32)],
        compiler_params=_cp(("parallel",)),
    )(proj, proj, proj, g, b, wm, bias_full)


def _sgu_bwd(proj, dy, g, b, wm, wmT, bias_full, mask, sel, *, tm):
    S = proj.shape[0]
    nsteps = S // tm

    def body(u_ref, v_ref, z_ref, dy_ref, g_ref, b_ref, wm_ref, wmT_ref, bias_ref, mask_ref, sel_ref,
             dp_ref, dws_ref, dbs_ref, dg_ref, db_ref, vnb_ref, mixed_ref, dmb_ref, dvn_ref, dbias_ref):
        i = pl.program_id(0)

        @pl.when(i == 0)
        def _():
            dws_ref[...] = jnp.zeros_like(dws_ref)
            dg_ref[...] = jnp.zeros_like(dg_ref)
            db_ref[...] = jnp.zeros_like(db_ref)
            dbias_ref[...] = jnp.zeros_like(dbias_ref)

        vh, rs = _sgu_core(u_ref, v_ref, z_ref, g_ref, b_ref, wm_ref, bias_ref, vnb_ref, mixed_ref, tm)
        u = u_ref[...].astype(F32)
        z = z_ref[...].astype(F32)
        dy_v = dy_ref[...].astype(F32)
        mixed = mixed_ref[...]
        sg = _sigmoid(z)
        sz = z * sg
        dp_ref[:, 0:D] = (dy_v * mixed * sz).astype(BF16)
        dp_ref[:, 2 * D:3 * D] = (dy_v * u * mixed * (sg * (1.0 + z * (1.0 - sg)))).astype(BF16)
        dmixed = dy_v * u * sz
        dmb_ref[...] = dmixed.astype(BF16)
        for blk in range(tm // SGU_BLOCK):
            dbias_ref[...] += dmixed[blk * SGU_BLOCK:(blk + 1) * SGU_BLOCK, :]
        for blk in range(tm // SGU_BLOCK):
            rows = pl.ds(blk * SGU_BLOCK, SGU_BLOCK)
            for gi in range(SGU_GROUPS):
                cols = pl.ds(gi * LANE, LANE)
                dm = dmb_ref[rows, cols]
                dvn_ref[rows, cols] = _dot(wmT_ref[gi], dm)
                dws_ref[gi] += _dot_nt(dm, vnb_ref[rows, cols])
        dvn = dvn_ref[...]
        dg_ref[0:1, :] += jnp.sum(dvn * vh, axis=0, keepdims=True)
        db_ref[0:1, :] += jnp.sum(dvn, axis=0, keepdims=True)
        dvh = dvn * g_ref[...]
        dv = rs * (dvh - jnp.mean(dvh, axis=-1, keepdims=True) - vh * jnp.mean(dvh * vh, axis=-1, keepdims=True))
        dp_ref[:, D:2 * D] = dv.astype(BF16)

        @pl.when(i == nsteps - 1)
        def _():
            for gi in range(SGU_GROUPS):
                dws_ref[gi] = dws_ref[gi] * mask_ref[...]
            dbs_ref[...] = _dot(dbias_ref[...], sel_ref[...], precision=HI)

    seg = lambda off: pl.BlockSpec((tm, D), lambda i: (i, off // D))
    full = lambda a: pl.BlockSpec(a.shape, lambda i: (0,) * a.ndim)
    return pl.pallas_call(
        body, name="sgu_bwd", grid=(nsteps,),
        in_specs=[seg(OFF_U), seg(OFF_V), seg(OFF_ZA), pl.BlockSpec((tm, D), lambda i: (i, 0)),
                  full(g), full(b), full(wm), full(wmT), full(bias_full), full(mask), full(sel)],
        out_specs=[pl.BlockSpec((tm, 3 * D), lambda i: (i, 0)),
                   pl.BlockSpec((SGU_GROUPS, SGU_BLOCK, SGU_BLOCK), lambda i: (0, 0, 0)),
                   pl.BlockSpec((SGU_BLOCK, LANE), lambda i: (0, 0)),
                   pl.BlockSpec((8, D), lambda i: (0, 0)), pl.BlockSpec((8, D), lambda i: (0, 0))],
        out_shape=[jax.ShapeDtypeStruct((S, 3 * D), BF16),
                   jax.ShapeDtypeStruct((SGU_GROUPS, SGU_BLOCK, SGU_BLOCK), F32),
                   jax.ShapeDtypeStruct((SGU_BLOCK, LANE), F32),
                   jax.ShapeDtypeStruct((8, D), F32), jax.ShapeDtypeStruct((8, D), F32)],
        scratch_shapes=[pltpu.VMEM((tm, D), BF16), pltpu.VMEM((tm, D), F32), pltpu.VMEM((tm, D), BF16),
                        pltpu.VMEM((tm, D), F32), pltpu.VMEM((SGU_BLOCK, D), F32)],
        compiler_params=_cp(("arbitrary",)),
    )(proj, proj, proj, dy, g, b, wm, wmT, bias_full, mask, sel)


SSD_T = 2 * CHUNK
HALO = 8
HALO_BLK = 16


def _pair_masks():
    row = lax.broadcasted_iota(jnp.int32, (CHUNK, LANE), 0)
    lane = lax.broadcasted_iota(jnp.int32, (CHUNK, LANE), 1)
    pos = jnp.where(lane >= CHUNK, lane - CHUNK, lane)
    diag = (row == pos).astype(F32)
    causal = row >= pos
    lo = (lane < CHUNK).astype(F32)
    return diag, causal, lo, 1.0 - lo


def _ssd_chunk_fwd(c, ext_ref, shift_ref, dt_ref, cw_ref, cb_ref, dtb_ref, alog_ref, tri_ref, exp_ref):
    r0 = c * CHUNK
    win = ext_ref[pl.ds(r0, HALO_BLK + CHUNK), :]
    sh = _dot(shift_ref[...], win)
    taps = [sh[k * CHUNK:(k + 1) * CHUNK] for k in range(CONV_K - 1)] + [win[HALO_BLK:].astype(F32)]
    pre = cb_ref[...] + sum(cw_ref[k:k + 1, :] * taps[k] for k in range(CONV_K))
    sg = _sigmoid(pre)
    xc = pre * sg
    dtr = dt_ref[pl.ds(r0, CHUNK), :].astype(F32) + dtb_ref[...]
    dtv = _softplus(dtr)
    A = -jnp.exp(alog_ref[...])
    acs = _sel_left(tri_ref[...], dtv * A)
    both = _sel_right_k(jnp.concatenate([acs, dtv], axis=0), exp_ref[...])
    E, dtE = both[0:CHUNK], both[CHUNK:2 * CHUNK]
    return dict(taps=taps, pre=pre, sg=sg, xc=xc, dtr=dtr, dtv=dtv, A=A, E=E, dtE=dtE)


def _ssd_fwd(proj, conv_w, conv_b, dtb_p, alog_p, d_exp, norm_w, tri, expand, shift):
    S = proj.shape[0]
    T = SSD_T
    nsteps = S // T
    ncl = T // CHUNK

    def body(zb_ref, xbc_ref, halo_ref, dt_ref, cw_ref, cb_ref, dtb_ref, alog_ref, dexp_ref, nw_ref, tri_ref, exp_ref, shift_ref,
             y_ref, yb_ref, st_ref, ht_ref, ext_ref):
        i = pl.program_id(0)

        @pl.when(i == 0)
        def _():
            ht_ref[...] = jnp.zeros_like(ht_ref)
            ext_ref[0:HALO_BLK, :] = jnp.zeros((HALO_BLK, XBC_W), BF16)

        @pl.when(i > 0)
        def _():
            ext_ref[0:HALO_BLK, :] = halo_ref[...]

        ext_ref[HALO_BLK:HALO_BLK + T, :] = xbc_ref[...]
        diag, causal, lo, hi = _pair_masks()
        for c in range(ncl):
            q = _ssd_chunk_fwd(c, ext_ref, shift_ref, dt_ref, cw_ref, cb_ref, dtb_ref, alog_ref, tri_ref, exp_ref)
            rows = pl.ds(c * CHUNK, CHUNK)
            xc, E, dtE = q["xc"], q["E"], q["dtE"]
            xs = xc[:, 0:D]
            total = E[CHUNK - 1:CHUNK, :]
            x_dt = xs * dtE
            eE = jnp.exp(E)
            xw = x_dt * jnp.exp(total - E)
            st_ref[c] = ht_ref[...]
            for g in range(SSD_GROUPS):
                gc = slice(g * GROUP_W, (g + 1) * GROUP_W)
                Bg = xc[:, D + g * STATE:D + (g + 1) * STATE].astype(BF16)
                Cg = xc[:, D + SSD_GROUPS * STATE + g * STATE:D + SSD_GROUPS * STATE + (g + 1) * STATE].astype(BF16)
                cb2 = _dot_nt(Cg, jnp.concatenate([Bg, Bg], axis=0))
                htg = ht_ref[:, gc]
                y_ref[rows, gc] = eE[:, gc] * _dot(Cg, htg.astype(BF16)) + xs[:, gc] * dexp_ref[:, gc]
                for jj in range(GROUP_W // LANE):
                    pc = slice(g * GROUP_W + jj * LANE, g * GROUP_W + (jj + 1) * LANE)
                    Ej = E[:, pc]
                    e2 = jnp.sum(Ej * diag, axis=0, keepdims=True)
                    Mp = cb2 * jnp.exp(jnp.where(causal, Ej - e2, -1e30))
                    xj = x_dt[:, pc]
                    xbd = jnp.concatenate([xj * lo, xj * hi], axis=0).astype(BF16)
                    y_ref[rows, pc] += _dot(Mp.astype(BF16), xbd)
                ht_ref[:, gc] = jnp.exp(total[:, gc]) * htg + _dot_tn(Bg, xw[:, gc].astype(BF16))
            zb = zb_ref[rows, :].astype(F32)
            hh = y_ref[rows, :] * (zb * _sigmoid(zb))
            for g in range(SSD_GROUPS):
                gc = slice(g * GROUP_W, (g + 1) * GROUP_W)
                hg = hh[:, gc]
                r = lax.rsqrt(jnp.mean(hg * hg, axis=-1, keepdims=True) + EPS)
                yb_ref[rows, gc] = (hg * r * nw_ref[:, gc]).astype(BF16)

    full = lambda a: pl.BlockSpec(a.shape, lambda i: (0,) * a.ndim)
    hb = T // HALO_BLK
    return pl.pallas_call(
        body, name="ssd_fwd", grid=(nsteps,),
        in_specs=[pl.BlockSpec((T, D), lambda i: (i, OFF_ZB // D)),
                  pl.BlockSpec((T, XBC_W), lambda i: (i, OFF_XBC // XBC_W)),
                  pl.BlockSpec((HALO_BLK, XBC_W), lambda i: (jnp.maximum(i * hb - 1, 0), OFF_XBC // XBC_W)),
                  pl.BlockSpec((T, DT_W), lambda i: (i, OFF_DT // DT_W)),
                  full(conv_w), full(conv_b), full(dtb_p), full(alog_p), full(d_exp), full(norm_w), full(tri), full(expand),
                  full(shift)],
        out_specs=[pl.BlockSpec((T, D), lambda i: (i, 0)), pl.BlockSpec((T, D), lambda i: (i, 0)),
                   pl.BlockSpec((ncl, STATE, D), lambda i: (i, 0, 0))],
        out_shape=[jax.ShapeDtypeStruct((S, D), F32), jax.ShapeDtypeStruct((S, D), BF16),
                   jax.ShapeDtypeStruct((S // CHUNK, STATE, D), F32)],
        scratch_shapes=[pltpu.VMEM((STATE, D), F32), pltpu.VMEM((HALO_BLK + T, XBC_W), BF16)],
        compiler_params=_cp(("arbitrary",)),
    )(proj, proj, proj, proj, conv_w, conv_b, dtb_p, alog_p, d_exp, norm_w, tri, expand, shift)


def _ssd_bwd(proj, dyb, y, states, conv_w, conv_b, dtb_p, alog_p, d_exp, norm_w, tri, triT, expand, expandT, shift):
    S = proj.shape[0]
    T = SSD_T
    nsteps = S // T
    ncl = T // CHUNK
    SSD_W = SSD_PAD_W

    def body(zb_ref, xbc_ref, halo_ref, dt_ref, dyb_ref, y_ref, st_ref, cw_ref, cb_ref, dtb_ref, alog_ref, dexp_ref, nw_ref,
             tri_ref, triT_ref, exp_ref, expT_ref, shift_ref,
             dp_ref, dcw_ref, dcb_ref, ddtb_ref, dalog_ref, dD_ref, dnw_ref,
             dht_ref, ext_ref, dpre_ref, dy_s, dE_s, dxdt_s, dxc_s, dDacc_ref, dAacc_ref):
        i = pl.program_id(0)

        @pl.when(i == 0)
        def _():
            for r in (dht_ref, dcw_ref, dcb_ref, ddtb_ref, dnw_ref, dDacc_ref, dAacc_ref):
                r[...] = jnp.zeros_like(r)
            dpre_ref[T:T + HALO_BLK, :] = jnp.zeros((HALO_BLK, XBC_W), F32)

        @pl.when(i == nsteps - 1)
        def _():
            ext_ref[0:HALO_BLK, :] = jnp.zeros((HALO_BLK, XBC_W), BF16)

        @pl.when(i < nsteps - 1)
        def _():
            ext_ref[0:HALO_BLK, :] = halo_ref[...]

        ext_ref[HALO_BLK:HALO_BLK + T, :] = xbc_ref[...]
        diag, causal, lo, hi = _pair_masks()
        last_row = (lax.broadcasted_iota(jnp.int32, (CHUNK, 1), 0) == CHUNK - 1).astype(F32)
        for c in reversed(range(ncl)):
            q = _ssd_chunk_fwd(c, ext_ref, shift_ref, dt_ref, cw_ref, cb_ref, dtb_ref, alog_ref, tri_ref, exp_ref)
            rows = pl.ds(c * CHUNK, CHUNK)
            pre, sg, xc, dtr, dtv, A, E, dtE = (q[k] for k in ("pre", "sg", "xc", "dtr", "dtv", "A", "E", "dtE"))
            xs = xc[:, 0:D]
            total = E[CHUNK - 1:CHUNK, :]
            x_dt = xs * dtE
            eE = jnp.exp(E)
            wdec = jnp.exp(total - E)
            zb = zb_ref[rows, :].astype(F32)
            yv = y_ref[rows, :]
            sgz = _sigmoid(zb)
            sz = zb * sgz
            hh = yv * sz
            for g in range(SSD_GROUPS):
                gc = slice(g * GROUP_W, (g + 1) * GROUP_W)
                hg = hh[:, gc]
                r = lax.rsqrt(jnp.mean(hg * hg, axis=-1, keepdims=True) + EPS)
                dyb_g = dyb_ref[rows, gc].astype(F32)
                dn = dyb_g * nw_ref[:, gc]
                dnw_ref[0:1, gc] += jnp.sum(dyb_g * hg * r, axis=0, keepdims=True)
                dy_s[:, gc] = r * dn - hg * (r * r * r) * jnp.mean(dn * hg, axis=-1, keepdims=True)
            dhh = dy_s[...]
            dp_ref[rows, 0:D] = (dhh * yv * (sgz * (1.0 + zb * (1.0 - sgz)))).astype(BF16)
            dy = dhh * sz
            dy_s[...] = dy
            dDacc_ref[0:1, :] += jnp.sum(dy * xs, axis=0, keepdims=True)
            dxc_s[:, 0:D] = dy * dexp_ref[...]
            for g in range(SSD_GROUPS):
                gc = slice(g * GROUP_W, (g + 1) * GROUP_W)
                bcol = slice(D + g * STATE, D + (g + 1) * STATE)
                ccol = slice(D + SSD_GROUPS * STATE + g * STATE, D + SSD_GROUPS * STATE + (g + 1) * STATE)
                Bg = xc[:, bcol].astype(BF16)
                Cg = xc[:, ccol].astype(BF16)
                B2 = jnp.concatenate([Bg, Bg], axis=0)
                cb2 = _dot_nt(Cg, B2)
                htg = st_ref[c, :, gc]
                htb = htg.astype(BF16)
                dhn = dht_ref[:, gc]
                dhnb = dhn.astype(BF16)
                dyg = dy[:, gc]
                eEg = eE[:, gc]
                wg = wdec[:, gc]
                xdg = x_dt[:, gc]
                CH = _dot(Cg, htb)
                dCHb = (dyg * eEg).astype(BF16)
                dC = _dot_nt(dCHb, htb)
                dl = jnp.exp(total[:, gc])
                dht_prev = _dot_tn(Cg, dCHb) + dl * dhn
                dtot = jnp.sum(dhn * htg, axis=0, keepdims=True) * dl
                dxw = _dot(Bg, dhnb)
                dB = _dot_nt((xdg * wg).astype(BF16), dhnb)
                dwd = dxw * xdg * wg
                dtot = dtot + jnp.sum(dwd, axis=0, keepdims=True)
                dE_s[:, gc] = dyg * eEg * CH - dwd + last_row * dtot
                dxdt_s[:, gc] = dxw * wg
                dcb2 = jnp.zeros((CHUNK, LANE), F32)
                for jj in range(GROUP_W // LANE):
                    pc = slice(g * GROUP_W + jj * LANE, g * GROUP_W + (jj + 1) * LANE)
                    Ej = E[:, pc]
                    e2 = jnp.sum(Ej * diag, axis=0, keepdims=True)
                    Lp = jnp.exp(jnp.where(causal, Ej - e2, -1e30))
                    Mp = cb2 * Lp
                    xj = x_dt[:, pc]
                    xbd = jnp.concatenate([xj * lo, xj * hi], axis=0).astype(BF16)
                    dyj = dy[:, pc].astype(BF16)
                    dMp = _dot_nt(dyj, xbd)
                    dxbd = _dot_tn(Mp.astype(BF16), dyj)
                    dxdt_s[:, pc] += dxbd[0:CHUNK, :] * lo + dxbd[CHUNK:2 * CHUNK, :] * hi
                    dcb2 = dcb2 + dMp * Lp
                    dseg = dMp * Mp
                    dE_s[:, pc] += dseg - diag * jnp.sum(dseg, axis=0, keepdims=True)
                dcb2b = dcb2.astype(BF16)
                dC = dC + _dot(dcb2b, B2)
                dB2 = _dot_tn(dcb2b, Cg)
                dB = dB + dB2[0:CHUNK, :] + dB2[CHUNK:2 * CHUNK, :]
                dxc_s[:, bcol] = dB
                dxc_s[:, ccol] = dC
                dht_ref[:, gc] = dht_prev
            dx_dt = dxdt_s[...]
            dxc_s[:, 0:D] += dx_dt * dtE
            red = _sel_right(jnp.concatenate([dE_s[...], dx_dt * xs], axis=0), expT_ref[...])
            da = _sel_left(triT_ref[...], red[0:CHUNK, :])
            ddtv = red[CHUNK:2 * CHUNK, :] + da * A
            dAacc_ref[0:1, :] += jnp.sum(da * dtv, axis=0, keepdims=True)
            ddtr = ddtv * _sigmoid(dtr)
            ddtb_ref[0:1, :] += jnp.sum(ddtr, axis=0, keepdims=True)
            dp_ref[rows, D + XBC_W:D + XBC_W + DT_W] = ddtr.astype(BF16)
            dpre = dxc_s[...] * (sg * (1.0 + pre * (1.0 - sg)))
            dpre_ref[rows, :] = dpre
            dcb_ref[0:1, :] += jnp.sum(dpre, axis=0, keepdims=True)
            for k in range(CONV_K):
                dcw_ref[k:k + 1, :] += jnp.sum(dpre * q["taps"][k], axis=0, keepdims=True)
        dxbc = jnp.zeros((T, XBC_W), F32)
        for k in range(CONV_K):
            dxbc = dxbc + cw_ref[k:k + 1, :] * dpre_ref[pl.ds(CONV_K - 1 - k, T), :]
        dp_ref[:, D:D + XBC_W] = dxbc.astype(BF16)
        dp_ref[:, SEG_SSD[1]:SSD_W] = jnp.zeros((T, SSD_W - SEG_SSD[1]), BF16)
        dpre_ref[T:T + HALO, :] = dpre_ref[0:HALO, :]

        @pl.when(i == nsteps - 1)
        def _():
            dalog_ref[...] = dAacc_ref[...] * (-jnp.exp(alog_ref[...]))
            dD_ref[...] = _dot(dDacc_ref[...], expT_ref[...].astype(F32), precision=HI)

    full = lambda a: pl.BlockSpec(a.shape, lambda i: (0,) * a.ndim)
    hb = T // HALO_BLK
    rev = lambda i: nsteps - 1 - i
    acc = lambda w: pl.BlockSpec((8, w), lambda i: (0, 0))
    return pl.pallas_call(
        body, name="ssd_bwd", grid=(nsteps,),
        in_specs=[pl.BlockSpec((T, D), lambda i: (rev(i), OFF_ZB // D)),
                  pl.BlockSpec((T, XBC_W), lambda i: (rev(i), OFF_XBC // XBC_W)),
                  pl.BlockSpec((HALO_BLK, XBC_W), lambda i: (jnp.maximum(rev(i) * hb - 1, 0), OFF_XBC // XBC_W)),
                  pl.BlockSpec((T, DT_W), lambda i: (rev(i), OFF_DT // DT_W)),
                  pl.BlockSpec((T, D), lambda i: (rev(i), 0)), pl.BlockSpec((T, D), lambda i: (rev(i), 0)),
                  pl.BlockSpec((ncl, STATE, D), lambda i: (rev(i), 0, 0)),
                  full(conv_w), full(conv_b), full(dtb_p), full(alog_p), full(d_exp), full(norm_w),
                  full(tri), full(triT), full(expand), full(expandT), full(shift)],
        out_specs=[pl.BlockSpec((T, SSD_W), lambda i: (rev(i), 0)),
                   acc(XBC_W), acc(XBC_W), acc(DT_W), acc(DT_W), acc(DT_W), acc(D)],
        out_shape=[jax.ShapeDtypeStruct((S, SSD_W), BF16),
                   jax.ShapeDtypeStruct((8, XBC_W), F32), jax.ShapeDtypeStruct((8, XBC_W), F32),
                   jax.ShapeDtypeStruct((8, DT_W), F32), jax.ShapeDtypeStruct((8, DT_W), F32),
                   jax.ShapeDtypeStruct((8, DT_W), F32), jax.ShapeDtypeStruct((8, D), F32)],
        scratch_shapes=[pltpu.VMEM((STATE, D), F32), pltpu.VMEM((HALO_BLK + T, XBC_W), BF16), pltpu.VMEM((T + HALO_BLK, XBC_W), F32),
                        pltpu.VMEM((CHUNK, D), F32), pltpu.VMEM((CHUNK, D), F32), pltpu.VMEM((CHUNK, D), F32),
                        pltpu.VMEM((CHUNK, XBC_W), F32), pltpu.VMEM((8, D), F32), pltpu.VMEM((8, DT_W), F32)],
        compiler_params=_cp(("arbitrary",)),
    )(proj, proj, proj, proj, dyb, y, states, conv_w, conv_b, dtb_p, alog_p, d_exp, norm_w, tri, triT, expand, expandT, shift)


def _head(x, ya, yb, proj, target, gate_b, wout, fw, *, tm):
    S = x.shape[0]

    def body(x_ref, ya_ref, yb_ref, gl0_ref, gl1_ref, t_ref, gb_ref, w_ref, fw_ref,
             dh_ref, dhb_ref, mb_ref, dya_ref, dyb_ref, dgl_ref, loss_ref, dfw_ref, dgb_ref):
        @pl.when(pl.program_id(0) == 0)
        def _():
            loss_ref[...] = jnp.zeros_like(loss_ref)
            dfw_ref[...] = jnp.zeros_like(dfw_ref)
            dgb_ref[...] = jnp.zeros_like(dgb_ref)

        ya_v = ya_ref[...].astype(F32)
        yb_v = yb_ref[...].astype(F32)
        g0 = _sigmoid(gl0_ref[...].astype(F32) + gb_ref[:, 0:D])
        g1 = _sigmoid(gl1_ref[...].astype(F32) + gb_ref[:, D:2 * D])
        mb = (g0 * ya_v + g1 * yb_v).astype(BF16)
        mb_ref[...] = mb
        h = x_ref[...] + _dot(mb, w_ref[...])
        r = lax.rsqrt(jnp.mean(h * h, axis=-1, keepdims=True) + EPS)
        hn = h * r
        err = hn * fw_ref[...] - t_ref[...]
        loss_ref[...] += 0.5 * jnp.sum(jnp.mean(err * err, axis=-1, keepdims=True))
        dyf = err * (1.0 / D)
        dfw_ref[0:1, :] += jnp.sum(dyf * hn, axis=0, keepdims=True)
        dhn = dyf * fw_ref[...]
        dh = r * (dhn - hn * jnp.mean(dhn * hn, axis=-1, keepdims=True))
        dh_ref[...] = dh
        dhb = dh.astype(BF16)
        dhb_ref[...] = dhb
        dm = _dot_nt(dhb, w_ref[...])
        dya_ref[...] = (dm * g0).astype(BF16)
        dyb_ref[...] = (dm * g1).astype(BF16)
        dgl0 = dm * ya_v * g0 * (1.0 - g0)
        dgl1 = dm * yb_v * g1 * (1.0 - g1)
        dgl_ref[:, 0:D] = dgl0.astype(BF16)
        dgl_ref[:, D:2 * D] = dgl1.astype(BF16)
        dgb_ref[0:1, 0:D] += jnp.sum(dgl0, axis=0, keepdims=True)
        dgb_ref[0:1, D:2 * D] += jnp.sum(dgl1, axis=0, keepdims=True)

    row = pl.BlockSpec((tm, D), lambda i: (i, 0))
    seg = lambda off: pl.BlockSpec((tm, D), lambda i: (i, off // D))
    full = lambda a: pl.BlockSpec(a.shape, lambda i: (0,) * a.ndim)
    acc = lambda w: pl.BlockSpec((8, w), lambda i: (0, 0))
    return pl.pallas_call(
        body, name="head", grid=(S // tm,),
        in_specs=[row, row, row, seg(OFF_G0), seg(OFF_G1), row, full(gate_b), full(wout), full(fw)],
        out_specs=[row, row, row, row, row, pl.BlockSpec((tm, 2 * D), lambda i: (i, 0)), acc(LANE), acc(D), acc(2 * D)],
        out_shape=[jax.ShapeDtypeStruct((S, D), F32), jax.ShapeDtypeStruct((S, D), BF16), jax.ShapeDtypeStruct((S, D), BF16),
                   jax.ShapeDtypeStruct((S, D), BF16), jax.ShapeDtypeStruct((S, D), BF16), jax.ShapeDtypeStruct((S, 2 * D), BF16),
                   jax.ShapeDtypeStruct((8, LANE), F32), jax.ShapeDtypeStruct((8, D), F32), jax.ShapeDtypeStruct((8, 2 * D), F32)],
        compiler_params=_cp(("arbitrary",)),
    )(x, ya, yb, proj, proj, target, gate_b, wout, fw)


def _adam_update(g, w_ref, m_ref, v_ref, g_ref, d_ref, m2_ref, v2_ref):
    m2 = ADAM_B1 * m_ref[...] + (1.0 - ADAM_B1) * g
    v2 = ADAM_B2 * v_ref[...] + (1.0 - ADAM_B2) * (g * g)
    m_hat = m2 / (1.0 - ADAM_B1 ** ADAM_STEP)
    v_hat = v2 / (1.0 - ADAM_B2 ** ADAM_STEP)
    g_ref[...] = g
    d_ref[...] = -ADAM_LR * (m_hat / (jnp.sqrt(v_hat) + ADAM_EPS) + ADAM_WD * w_ref[...])
    m2_ref[...] = m2
    v2_ref[...] = v2


def _adamw_own(me, own, landed, w, m, v, *, tr, tc, name):
    _, R, C = landed.shape
    assert R % tr == 0 and C % tc == 0, (name, R, C, tr, tc)

    def body(me_ref, own_ref, p_ref, w_ref, m_ref, v_ref, g_ref, d_ref, m2_ref, v2_ref):
        mine = own_ref[0].astype(F32)
        g = jnp.where(me_ref[0] == 0, mine, p_ref[0].astype(F32))
        for k in range(1, N_DEV):
            g = g + jnp.where(me_ref[0] == k, mine, p_ref[k].astype(F32))
        _adam_update(g, w_ref, m_ref, v_ref, g_ref, d_ref, m2_ref, v2_ref)

    tile = pl.BlockSpec((tr, tc), lambda i, j, me_ref: (i, j))
    return pl.pallas_call(
        body, name=name,
        grid_spec=pltpu.PrefetchScalarGridSpec(
            num_scalar_prefetch=1, grid=(R // tr, C // tc),
            in_specs=[pl.BlockSpec((1, tr, tc), lambda i, j, me_ref: (me_ref[0], i, j)),
                      pl.BlockSpec((N_DEV, tr, tc), lambda i, j, me_ref: (0, i, j)), tile, tile, tile],
            out_specs=[tile, tile, tile, tile]),
        out_shape=[jax.ShapeDtypeStruct((R, C), F32)] * 4,
        compiler_params=_cp(("parallel", "parallel")),
    )(me, own, landed, w, m, v)


def _adamw(parts, w, m, v, *, tr, name):
    _, R, C = parts.shape
    assert R % tr == 0, (name, R, tr)

    def body(p_ref, w_ref, m_ref, v_ref, g_ref, d_ref, m2_ref, v2_ref):
        g = p_ref[0].astype(F32)
        for k in range(1, N_DEV):
            g = g + p_ref[k].astype(F32)
        _adam_update(g, w_ref, m_ref, v_ref, g_ref, d_ref, m2_ref, v2_ref)

    row = pl.BlockSpec((tr, C), lambda i: (i, 0))
    return pl.pallas_call(
        body, name=name, grid=(R // tr,),
        in_specs=[pl.BlockSpec((N_DEV, tr, C), lambda i: (0, i, 0)), row, row, row],
        out_specs=[row, row, row, row],
        out_shape=[jax.ShapeDtypeStruct((R, C), F32)] * 4,
        compiler_params=_cp(("parallel",)),
    )(parts, w, m, v)


def _place():
    x, y, c = lax.axis_index("x"), lax.axis_index("y"), lax.axis_index("c")
    return x, y, c


def _all_gather(arrs, *, name):
    n = len(arrs)

    def body(*refs):
        ins, outs = refs[:n], refs[n:2 * n]
        send_sems, recv_sems, local_sems = refs[2 * n:]
        x, y, c = _place()
        me, sibling = (x, y, c), (x, y, 1 - c)
        chips = [(1 - x, y), (x, 1 - y), (1 - x, 1 - y)]

        def idx(px, py, pc):
            return 4 * px + 2 * py + pc

        def copy(k, a, block, to, src=None):
            slab = outs[a].at[idx(*block)]
            return pltpu.make_async_remote_copy(
                src_ref=slab if src is None else src, dst_ref=slab,
                send_sem=send_sems.at[k, a], recv_sem=recv_sems.at[k, a], device_id=to, device_id_type=MESH)

        mine = [pltpu.make_async_copy(ins[a], outs[a].at[idx(*me)], local_sems.at[a]) for a in range(n)]
        for cp in mine:
            cp.start()
        first = []
        for a in range(n):
            first.append(copy(0, a, me, sibling, src=ins[a]))
            first += [copy(1 + j, a, me, (*chip, c), src=ins[a]) for j, chip in enumerate(chips)]
        for cp in first:
            cp.start()
        passed = []
        for j, chip in enumerate(chips):
            for a in range(n):
                copy(1 + j, a, (*chip, c), me).wait_recv()
                fwd = copy(4 + j, a, (*chip, c), sibling)
                fwd.start()
                passed.append(fwd)
        for a in range(n):
            copy(0, a, sibling, me).wait_recv()
            for j, chip in enumerate(chips):
                copy(4 + j, a, (*chip, 1 - c), me).wait_recv()
        for cp in first + passed:
            cp.wait_send()
        for cp in mine:
            cp.wait()

    anyspec = pl.BlockSpec(memory_space=pl.ANY)
    return pl.pallas_call(
        body, name=name,
        in_specs=[anyspec] * n, out_specs=[anyspec] * n,
        out_shape=[jax.ShapeDtypeStruct((N_DEV,) + a.shape, a.dtype) for a in arrs],
        scratch_shapes=[pltpu.SemaphoreType.DMA((7, n)), pltpu.SemaphoreType.DMA((7, n)), pltpu.SemaphoreType.DMA((n,))],
    )(*arrs)


W_ROWS = SEG_SSD[0] + SSD_PAD_W


GROUP = 16
INTERIOR = 1920


def _interior(k):
    lo = -(-(k * SHARD_IN) // GROUP) * GROUP
    hi = ((k + 1) * SHARD_IN) // GROUP * GROUP
    return lo, hi


def _dest_row(r):
    if r < REF_SGU_END:
        return r
    return r - REF_SGU_END + SEG_SSD[0] if r < REF_GATE_START else r - REF_GATE_START + SEG_GATE[0]


def _shard_pieces(k):
    lo_k, hi_k = _interior(k)
    out = []
    for lo, hi in ((0, REF_SGU_END), (REF_SGU_END, REF_GATE_START), (REF_GATE_START, W_IN)):
        a, b = max(lo, lo_k), min(hi, hi_k)
        if a < b:
            out.append((a - lo_k, b - a, _dest_row(a)))
    return out


GATHER_PARTS = 1


def _shard_parts(k):
    parts = [[] for _ in range(GATHER_PARTS)]
    for s0, n, d0 in _shard_pieces(k):
        step = -(-(n // GROUP) // GATHER_PARTS) * GROUP
        for p in range(GATHER_PARTS):
            a, b = min(p * step, n), min((p + 1) * step, n)
            if a < b:
                parts[p].append((s0 + a, b - a, d0 + a))
    return parts


def _patch_straddlers(wpT, heads, tails):
    for k in range(1, N_DEV):
        m = (k * SHARD_IN) % GROUP
        if m:
            group = jnp.concatenate([tails[k - 1, GROUP - m:], heads[k, :GROUP - m]], axis=0)
            wpT = lax.dynamic_update_slice(wpT, group, (_dest_row(k * SHARD_IN - m), 0))
    return wpT


def _gather_stages(k, win_ref, small, z_ref, n_zero, w_ref, send_sems, recv_sems, local_sems):
    x, y, c = k // 4, (k // 2) % 2, k % 2
    idx = lambda p: 4 * p[0] + 2 * p[1] + p[2]
    me, sib = (x, y, c), (x, y, 1 - c)
    xn, yn, dg = (1 - x, y, c), (x, 1 - y, c), (1 - x, 1 - y, c)
    parts = range(GATHER_PARTS)

    def copies(slot, block, to, part, own=False):
        kb = idx(block)
        out = []
        for j, (s0, n, d0) in enumerate(_shard_parts(kb)[part]):
            dst = w_ref.at[pl.ds(d0, n)]
            out.append((win_ref.at[pl.ds(s0, n)] if own else dst, dst, 2 * part + j))
        if part == 0:
            for j, (src, gathered) in enumerate(small):
                out.append((src if own else gathered.at[kb], gathered.at[kb], 2 * GATHER_PARTS + j))
        return [pltpu.make_async_remote_copy(src_ref=s, dst_ref=d, send_sem=send_sems.at[slot, j], recv_sem=recv_sems.at[slot, j],
                                             device_id=to, device_id_type=MESH) for s, d, j in out]

    def start(cps):
        for cp in cps:
            cp.start()

    def arrived(slot, block, part):
        for cp in copies(slot, block, me, part):
            cp.wait_recv()

    def local():
        pairs = [(win_ref.at[pl.ds(s0, n)], w_ref.at[pl.ds(d0, n)]) for s0, n, d0 in _shard_pieces(k)]
        pairs += [(src, gathered.at[k]) for src, gathered in small] + [(z_ref, w_ref.at[pl.ds(W_IN, n_zero)])]
        return [pltpu.make_async_copy(s, d, local_sems.at[j]) for j, (s, d) in enumerate(pairs)]

    relay = (xn, yn) if c == 1 else (yn, xn)

    def first():
        start(local())
        for p in parts:
            start(copies(0, me, sib, p, own=True) + copies(1, me, xn, p, own=True) + copies(2, me, yn, p, own=True))

    def hand_on():
        for p in parts:
            arrived(1, xn, p)
            start(copies(4, xn, sib, p))
            if c == 1:
                start(copies(3, *relay, p))
            arrived(2, yn, p)
            start(copies(5, yn, sib, p))
            if c == 0:
                start(copies(3, *relay, p))

    def finish():
        for p in parts:
            arrived(3, dg, p)
            start(copies(6, dg, sib, p))
        for p in parts:
            arrived(0, sib, p)
            arrived(4, (1 - x, y, 1 - c), p)
            arrived(5, (x, 1 - y, 1 - c), p)
            arrived(6, (1 - x, 1 - y, 1 - c), p)
        for p in parts:
            sent = (copies(0, me, sib, p, own=True) + copies(1, me, xn, p, own=True) + copies(2, me, yn, p, own=True)
                    + copies(3, *relay, p) + copies(4, xn, sib, p) + copies(5, yn, sib, p) + copies(6, dg, sib, p))
            for cp in sent:
                cp.wait_send()
        for cp in local():
            cp.wait()

    return first, hand_on, finish


def _gather_sems(n_small):
    n_arr = 2 * GATHER_PARTS + n_small
    return [pltpu.SemaphoreType.DMA((7, n_arr)), pltpu.SemaphoreType.DMA((7, n_arr)), pltpu.SemaphoreType.DMA((n_arr + 1,))]


def _gather_weights(win, head, tail, wout, cw, zeros):
    small_in = (wout, cw, head, tail)
    n_zero = zeros.shape[0]
    assert W_IN + n_zero == W_ROWS and W_IN % GROUP == 0

    def body(win_ref, wout_ref, cw_ref, head_ref, tail_ref, z_ref, w_ref, gout_ref, gcw_ref, ghead_ref, gtail_ref, *sems):
        x, y, c = _place()
        me = 4 * x + 2 * y + c
        small = ((wout_ref, gout_ref), (cw_ref, gcw_ref), (head_ref, ghead_ref), (tail_ref, gtail_ref))

        def run(k):
            for stage in _gather_stages(k, win_ref, small, z_ref, n_zero, w_ref, *sems):
                stage()

        for k in range(N_DEV):
            pl.when(me == k)(functools.partial(run, k))

    anyspec = pl.BlockSpec(memory_space=pl.ANY)
    return pl.pallas_call(
        body, name="gather_weights", in_specs=[anyspec] * 6, out_specs=[anyspec] * 5,
        out_shape=[jax.ShapeDtypeStruct((W_ROWS, D), win.dtype)]
        + [jax.ShapeDtypeStruct((N_DEV,) + a.shape, a.dtype) for a in small_in],
        scratch_shapes=_gather_sems(len(small_in)),
    )(win, wout, cw, head, tail, zeros)


_REL = [(dx, dy, dc) for dx in (0, 1) for dy in (0, 1) for dc in (0, 1)][1:]
_HBM = pl.BlockSpec(memory_space=pltpu.HBM)
_SEM = pl.BlockSpec(memory_space=pltpu.SEMAPHORE)
_EFFECT = pltpu.SideEffectType.DATAFLOW_SIDE_EFFECTING


def _peer(k):
    x, y, c = _place()
    dx, dy, dc = _REL[k]
    return (1 - x if dx else x, 1 - y if dy else y, 1 - c if dc else c)


def _exchange_start(parts, *, name):
    n = len(parts)

    def body(*refs):
        ins, lands = refs[:n], refs[n:2 * n]
        send_sems, recv_sems, token = refs[2 * n], refs[2 * n + 1], refs[-1]
        x, y, c = _place()
        me = 4 * x + 2 * y + c
        for a in range(n):
            for k in range(len(_REL)):
                px, py, pc = _peer(k)
                pltpu.make_async_remote_copy(
                    src_ref=ins[a].at[4 * px + 2 * py + pc], dst_ref=lands[a].at[me],
                    send_sem=send_sems.at[len(_REL) * a + k], recv_sem=recv_sems.at[len(_REL) * a + k],
                    device_id=(px, py, pc), device_id_type=MESH).start()
        token[...] = jnp.zeros_like(token)

    sem = pltpu.SemaphoreType.DMA((len(_REL) * n,))
    bufs = [pltpu.HBM(p.shape, p.dtype) for p in parts]
    outs = pl.pallas_call(
        body, name=name,
        out_shape=(sem, sem, *bufs, *bufs, jax.ShapeDtypeStruct((8, LANE), F32)),
        in_specs=(_HBM,) * (2 * n), out_specs=(_SEM, _SEM, *(_HBM,) * (2 * n), pl.BlockSpec(memory_space=pltpu.VMEM)),
        input_output_aliases={i: 2 + i for i in range(2 * n)},
        compiler_params=pltpu.CompilerParams(has_side_effects=_EFFECT),
    )(*[pltpu.with_memory_space_constraint(p, pltpu.HBM) for p in parts],
      *[pltpu.with_memory_space_constraint(lax.empty(p.shape, p.dtype), pltpu.HBM) for p in parts])
    return outs[0], outs[1], outs[2:2 + n], outs[2 + n:2 + 2 * n], outs[-1]


def _exchange_wait(send_sems, recv_sems, parts, lands, after, *, name):
    n = len(parts)

    def body(*refs):
        ins, lands_ = refs[:n], refs[n:2 * n]
        ssem, rsem = refs[2 * n], refs[2 * n + 1]
        for a in range(n):
            for k in range(len(_REL)):
                px, py, pc = _peer(k)
                p = 4 * px + 2 * py + pc
                cp = pltpu.make_async_remote_copy(
                    src_ref=ins[a].at[p], dst_ref=lands_[a].at[p],
                    send_sem=ssem.at[len(_REL) * a + k], recv_sem=rsem.at[len(_REL) * a + k],
                    device_id=(px, py, pc), device_id_type=MESH)
                cp.wait_send()
                cp.wait_recv()

    bufs = [pltpu.HBM(p.shape, p.dtype) for p in parts]
    outs = pl.pallas_call(
        body, name=name, out_shape=(*bufs, *bufs),
        in_specs=(*(_HBM,) * (2 * n), _SEM, _SEM, pl.BlockSpec(memory_space=pl.ANY)), out_specs=(_HBM,) * (2 * n),
        input_output_aliases={i: i for i in range(2 * n)},
        compiler_params=pltpu.CompilerParams(has_side_effects=_EFFECT),
    )(*parts, *lands, send_sems, recv_sems, after)
    return outs[:n], outs[n:]


WEIGHTS = ('norm_w', 'w_in', 'gate_b', 'sgu_norm_g', 'sgu_norm_b', 'sgu_w', 'sgu_b', 'conv_w', 'conv_b', 'dt_bias', 'A_log',
           'D_skip', 'ssd_norm_w', 'w_out', 'final_norm_w')
SHARDED = ('w_in', 'conv_w', 'w_out')
PACK_ROW = 8 * LANE


def _constants():
    tri = np.tril(np.ones((CHUNK, CHUNK), np.float32))
    expand = np.zeros((DT_W, D), np.float32)
    for h in range(HEADS):
        expand[h, h * HEADDIM:(h + 1) * HEADDIM] = 1.0
    sel = np.zeros((D, LANE), np.float32)
    for g in range(SGU_GROUPS):
        sel[g * LANE:(g + 1) * LANE, g] = 1.0
    pos_chunk = np.arange(SGU_BLOCK) // CHUNK
    mask = (pos_chunk[None, :] <= pos_chunk[:, None]).astype(np.float32)
    shift = np.zeros(((CONV_K - 1) * CHUNK, HALO_BLK + CHUNK), np.float32)
    for kk in range(CONV_K - 1):
        for t in range(CHUNK):
            shift[kk * CHUNK + t, HALO_BLK - (CONV_K - 1) + t + kk] = 1.0
    return dict(tri=jnp.asarray(tri, BF16), triT=jnp.asarray(tri.T.copy(), BF16), expand=jnp.asarray(np.tile(expand, (3, 1)), BF16),
                shift=jnp.asarray(shift, BF16),
                expandT=jnp.asarray(expand.T.copy(), BF16), sel=jnp.asarray(sel), mask=jnp.asarray(mask))


def _to_shards(segs):
    starts = np.cumsum([0] + [n for _, n in segs])
    assert starts[-1] == W_IN
    slabs = []
    for k in range(N_DEV):
        pieces = []
        for (s, n), s0 in zip(segs, starts[:-1]):
            lo, hi = max(k * SHARD_IN, s0), min((k + 1) * SHARD_IN, s0 + n)
            if lo < hi:
                pieces.append(s[lo - s0:hi - s0])
        slabs.append(jnp.concatenate(pieces, axis=0))
    return jnp.stack(slabs)


def _local_step(x2, tgt, wpT, wout, cw, p, exchange_small, exchange):
    S = x2.shape[0]
    k = _constants()
    xn, proj = _in_proj(x2, p['norm_w'], wpT, tm=min(1024, S), tn=2048)
    wm32 = p['sgu_w'][0] * k['mask']
    wm = wm32.astype(BF16)
    wmT = jnp.swapaxes(wm32, 1, 2).astype(BF16)
    bias_full = jnp.repeat(p['sgu_b'][0].T, LANE, axis=1)
    tm_sgu = min(512, S)
    ya = _sgu_fwd(proj, p['sgu_norm_g'], p['sgu_norm_b'], wm, bias_full, tm=tm_sgu)
    pad32 = lambda a: jnp.pad(a, ((0, 0), (0, DT_W - HEADS)))
    dtb_p, alog_p = pad32(p['dt_bias']), pad32(p['A_log'])
    d_exp = jnp.repeat(p['D_skip'], HEADDIM, axis=1)
    ssd_args = (cw, p['conv_b'], dtb_p, alog_p, d_exp, p['ssd_norm_w'])
    y, yb, states = _ssd_fwd(proj, *ssd_args, k['tri'], k['expand'], k['shift'])
    dh, dhb, mb, dya, dyb, dgl, loss, dfw, dgb = _head(
        x2, ya, yb, proj, tgt, p['gate_b'], wout, p['final_norm_w'][None, :], tm=min(256, S))
    dsgu, dws, dbsT, dsg, dsb = _sgu_bwd(proj, dya, p['sgu_norm_g'], p['sgu_norm_b'], wm, wmT, bias_full, k['mask'], k['sel'],
                                         tm=tm_sgu)
    dssd, dcw, dcb, ddtb, dalog, dD, dnw = _ssd_bwd(proj, dyb, y, states, *ssd_args, k['tri'], k['triT'], k['expand'], k['expandT'],
                                                    k['shift'])
    grads = dict(
        gate_b=dgb[0:1], sgu_norm_g=dsg[0:1], sgu_norm_b=dsb[0:1], sgu_w=dws[None],
        sgu_b=dbsT[:, :SGU_GROUPS].T[None], conv_w=dcw[0:CONV_K][None], conv_b=dcb[0:1], dt_bias=ddtb[0:1, :HEADS],
        A_log=dalog[0:1, :HEADS], D_skip=dD[0:1, :HEADS], ssd_norm_w=dnw[0:1], final_norm_w=dfw[0])
    token = exchange_small(loss[0, 0], grads)
    tw = dict(trans_a=True, out_dtype=BF16, tm=1024, tn=512, tk=S)
    dwT_sgu = _matmul(dsgu, xn, after=token, name="dw_in_sgu", **tw)
    dwT_gate = _matmul(dgl, xn, name="dw_in_gate", **tw)
    dwT_ssd = _matmul(dssd, xn, name="dw_in_ssd", **tw)
    dw_out = _matmul(mb, dhb, name="dw_out", **tw)
    tn = 1024
    token = exchange([(dwT_sgu, SEG_SGU[1]), (dwT_ssd, W_IN - SEG_SSD[0]), (dwT_gate, SEG_GATE[1])], dw_out)
    tm = min(1024, S)
    dxn = _matmul(dsgu, wpT, tm=tm, tn=tn, tk=3072, after=token, name="dxn_sgu")
    dxn = _matmul(dgl, wpT, b_koff=SEG_GATE[0] // 2048, tm=tm, tn=tn, tk=2048, add=dxn, name="dxn_gate")
    dxn = _matmul(dssd, wpT, b_koff=SEG_SSD[0] // 2048, tm=tm, tn=tn, tk=2048, add=dxn, name="dxn_ssd")
    grad_x, dnorm = _norm_bwd(x2, p['norm_w'], dxn, dh, tm=min(256, S))
    return grad_x, dnorm[0:1]


def _pack(arrs):
    rows, offs, r = [], [], 0
    for a in arrs:
        n = a.size
        nr = -(-n // PACK_ROW) * 8
        rows.append(jnp.pad(a.reshape(-1).astype(F32), (0, nr * LANE - n)).reshape(nr, LANE))
        offs.append(r)
        r += nr
    return jnp.concatenate(rows, axis=0), offs


def kernel(x, norm_w, w_in, gate_b, sgu_norm_g, sgu_norm_b, sgu_w, sgu_b, conv_w, conv_b, dt_bias, A_log, D_skip, ssd_norm_w, w_out, final_norm_w, loss_target, m_norm_w, m_w_in, m_gate_b, m_sgu_norm_g, m_sgu_norm_b, m_sgu_w, m_sgu_b, m_conv_w, m_conv_b, m_dt_bias, m_A_log, m_D_skip, m_ssd_norm_w, m_w_out, m_final_norm_w, v_norm_w, v_w_in, v_gate_b, v_sgu_norm_g, v_sgu_norm_b, v_sgu_w, v_sgu_b, v_conv_w, v_conv_b, v_dt_bias, v_A_log, v_D_skip, v_ssd_norm_w, v_w_out, v_final_norm_w):
    w = dict(norm_w=norm_w, w_in=w_in, gate_b=gate_b, sgu_norm_g=sgu_norm_g, sgu_norm_b=sgu_norm_b, sgu_w=sgu_w, sgu_b=sgu_b,
             conv_w=conv_w, conv_b=conv_b, dt_bias=dt_bias, A_log=A_log, D_skip=D_skip, ssd_norm_w=ssd_norm_w, w_out=w_out,
             final_norm_w=final_norm_w)
    m = dict(norm_w=m_norm_w, w_in=m_w_in, gate_b=m_gate_b, sgu_norm_g=m_sgu_norm_g, sgu_norm_b=m_sgu_norm_b, sgu_w=m_sgu_w,
             sgu_b=m_sgu_b, conv_w=m_conv_w, conv_b=m_conv_b, dt_bias=m_dt_bias, A_log=m_A_log, D_skip=m_D_skip,
             ssd_norm_w=m_ssd_norm_w, w_out=m_w_out, final_norm_w=m_final_norm_w)
    v = dict(norm_w=v_norm_w, w_in=v_w_in, gate_b=v_gate_b, sgu_norm_g=v_sgu_norm_g, sgu_norm_b=v_sgu_norm_b, sgu_w=v_sgu_w,
             sgu_b=v_sgu_b, conv_w=v_conv_w, conv_b=v_conv_b, dt_bias=v_dt_bias, A_log=v_A_log, D_skip=v_D_skip,
             ssd_norm_w=v_ssd_norm_w, w_out=v_w_out, final_norm_w=v_final_norm_w)
    me = 4 * lax.axis_index("x") + 2 * lax.axis_index("y") + lax.axis_index("c")
    shard_cw = XBC_W // N_DEV

    tpose = lambda a: jnp.swapaxes(a[0], 0, 1)
    wT = tpose(w_in).astype(BF16)
    first_group = (GROUP - (me * SHARD_IN) % GROUP) % GROUP
    window = lax.dynamic_slice(jnp.pad(wT, ((0, GROUP), (0, 0))), (first_group, 0), (INTERIOR, D))
    wpT, g_out, g_cw, heads, tails = _gather_weights(window, wT[:GROUP], wT[SHARD_IN - GROUP:], w_out[0].astype(BF16),
                                                     conv_w[0], jnp.zeros((W_ROWS - W_IN, D), BF16))
    wpT = _patch_straddlers(wpT, heads, tails)
    wout_full = g_out.reshape(D, D)
    cw_full = jnp.swapaxes(g_cw, 0, 1).reshape(CONV_K, XBC_W)

    flight = {}

    small = [n for n in WEIGHTS if n not in SHARDED and n != 'norm_w']
    early = {}

    def exchange_small(loss_part, grads):
        early['packed'], early['offs'] = _pack([grads[n] for n in small] + [loss_part, grads['conv_w']])
        parts = [jnp.broadcast_to(early['packed'][None], (N_DEV,) + early['packed'].shape)]
        early['sems'], early['rsems'], early['parts'], early['lands'], token = _exchange_start(parts, name="small_start")
        return token

    def exchange(dw_inT_segs, dw_out):
        parts = [_to_shards(dw_inT_segs), dw_out.reshape(N_DEV, D // N_DEV, D)]
        flight['sems'], flight['rsems'], flight['parts'], flight['lands'], token = _exchange_start(parts, name="exchange_start")
        return token

    grad_x, dnorm = _local_step(x[0], loss_target[0], wpT, wout_full, cw_full, w, exchange_small, exchange)
    _, (land_small,) = _exchange_wait(early['sems'], early['rsems'], early['parts'], early['lands'], grad_x, name="small_wait")
    (own_in, own_out), (land_in, land_out) = _exchange_wait(
        flight['sems'], flight['rsems'], flight['parts'], flight['lands'], grad_x, name="exchange_wait")
    me_arr = jnp.reshape(me, (1,)).astype(jnp.int32)
    res = {}
    res['w_in'] = [jnp.swapaxes(o, 0, 1) for o in _adamw_own(
        me_arr, own_in, land_in, tpose(w_in), tpose(m_w_in), tpose(v_w_in), tr=SHARD_IN, tc=256, name="adamw_w_in")]
    res['w_out'] = _adamw_own(me_arr, own_out, land_out, w_out[0], m_w_out[0], v_w_out[0], tr=128, tc=D, name="adamw_w_out")

    (norm_parts,) = _all_gather([_pack([dnorm])[0]], name="gather_norm")
    norm_outs = _adamw(norm_parts, *[_pack([d['norm_w']])[0] for d in (w, m, v)], tr=norm_parts.shape[1], name="adamw_norm")
    res['norm_w'] = [o.reshape(-1)[:D].reshape(w['norm_w'].shape) for o in norm_outs]

    offs = early['offs']
    gathered = lax.dynamic_update_slice(land_small, early['packed'][None], (me, 0, 0))
    off_loss, off_cw = offs[-2], offs[-1]
    cw_parts = gathered[:, off_cw:, :].reshape(N_DEV, CONV_K, XBC_W)
    cw_parts = lax.dynamic_slice_in_dim(cw_parts, me * shard_cw, shard_cw, axis=2)
    cw_rows = _pack([cw_parts[0]])[0].shape[0]
    cw_parts = jnp.pad(cw_parts.reshape(N_DEV, -1), ((0, 0), (0, cw_rows * LANE - CONV_K * shard_cw))).reshape(N_DEV, cw_rows, LANE)
    parts = jnp.concatenate([gathered[:, :off_cw, :], cw_parts], axis=1)
    zero = jnp.zeros((), F32)
    packs = [_pack([d[n] for n in small] + [zero, d['conv_w']])[0] for d in (w, m, v)]
    outs = _adamw(parts, *packs, tr=parts.shape[1], name="adamw_small")

    def unpack(o, name):
        if name == 'conv_w':
            return o[off_cw:off_cw + cw_rows].reshape(-1)[:CONV_K * shard_cw].reshape(w['conv_w'].shape)
        r0 = offs[small.index(name)]
        n = w[name].size
        return o[r0:r0 + -(-n // PACK_ROW) * 8].reshape(-1)[:n].reshape(w[name].shape)

    for n in small + ['conv_w']:
        res[n] = [unpack(o, n) for o in outs]
    for n in ('w_in', 'w_out'):
        res[n] = [o[None] for o in res[n]]
    loss = outs[0][off_loss, 0]
    return (loss, grad_x[None], *[res[n][0] for n in WEIGHTS], *[res[n][1] for n in WEIGHTS],
            *[res[n][2] for n in WEIGHTS], *[res[n][3] for n in WEIGHTS])
```

```python
import functools

import numpy as np
import jax
import jax.numpy as jnp
from jax import lax
from jax.experimental import pallas as pl
from jax.experimental.pallas import tpu as pltpu

F32 = jnp.float32
BF16 = jnp.bfloat16
HI = lax.Precision.HIGHEST
MESH = pl.DeviceIdType.MESH

D = 2048
EPS = 1e-5
SGU_BLOCK = 128
SGU_GROUPS = 16
CHUNK = 64
HEADS = 32
HEADDIM = 64
SSD_GROUPS = 4
GROUP_W = D // SSD_GROUPS
STATE = 128
CONV_K = 4
XBC_W = D + 2 * SSD_GROUPS * STATE
W_IN = 15392
N_DEV = 8
SHARD_IN = W_IN // N_DEV
ADAM_LR, ADAM_B1, ADAM_B2, ADAM_EPS, ADAM_WD, ADAM_STEP = 0.001, 0.9, 0.999, 1e-08, 0.01, 10

REF_SGU_END = 3 * D
REF_GATE_START = W_IN - 2 * D
LANE = 128
DT_W = LANE
OFF_U, OFF_V, OFF_ZA, OFF_G0, OFF_G1, OFF_ZB = (i * D for i in range(6))
OFF_XBC = OFF_ZB + D
OFF_DT = OFF_XBC + XBC_W
SEG_SGU = (OFF_U, 3 * D)
SEG_GATE = (OFF_G0, 2 * D)
SEG_SSD = (OFF_ZB, D + XBC_W + DT_W)
WP = SEG_SSD[0] + SEG_SSD[1]
SSD_PAD_W = 3 * D
VMEM_BYTES = 64 * 1024 * 1024
VMEM_LIMIT = VMEM_BYTES - 8 * 1024 * 1024


def _cp(sem=None, vmem=VMEM_LIMIT):
    return pltpu.CompilerParams(dimension_semantics=sem, vmem_limit_bytes=vmem)


def _sigmoid(x):
    return 1.0 / (1.0 + jnp.exp(-x))


def _softplus(x):
    return jnp.maximum(x, 0.0) + jnp.log(1.0 + jnp.exp(-jnp.abs(x)))


def _dot(a, b, precision=None):
    return jnp.dot(a, b, preferred_element_type=F32, precision=precision)


def _dot_nt(a, b, precision=None):
    return lax.dot_general(a, b, (((1,), (1,)), ((), ())), preferred_element_type=F32, precision=precision)


def _dot_tn(a, b, precision=None):
    return lax.dot_general(a, b, (((0,), (0,)), ((), ())), preferred_element_type=F32, precision=precision)


def _split3(a):
    hi = a.astype(BF16)
    r = a - hi.astype(F32)
    mid = r.astype(BF16)
    return hi, mid, (r - mid.astype(F32)).astype(BF16)


def _sel_right(a, sel01):
    m = a.shape[0]
    r = _dot(jnp.concatenate(_split3(a), axis=0), sel01)
    return (r[0:m] + r[m:2 * m]) + r[2 * m:3 * m]


def _sel_right_k(a, sel01_x3):
    return _dot(jnp.concatenate(_split3(a), axis=1), sel01_x3)


def _sel_left(sel01, a):
    n = a.shape[1]
    r = _dot(sel01, jnp.concatenate(_split3(a), axis=1))
    return (r[:, 0:n] + r[:, n:2 * n]) + r[:, 2 * n:3 * n]


def _matmul(a, b, *, trans_a=False, trans_b=False, b_koff=0, out_dtype=F32, tm, tn, tk, add=None, after=None, name):
    K, M = a.shape if trans_a else a.shape[::-1]
    N = b.shape[0] if trans_b else b.shape[1]
    assert M % tm == 0 and N % tn == 0 and K % tk == 0 and not (trans_a and trans_b), (name, M, N, K, tm, tn, tk)
    nk = K // tk

    def body(*refs):
        a_ref, b_ref = refs[:2]
        add_ref = refs[2] if add is not None else None
        o_ref, acc_ref = refs[-2:]
        k = pl.program_id(2)
        if trans_a:
            part = _dot_tn(a_ref[...], b_ref[...])
        else:
            part = _dot_nt(a_ref[...], b_ref[...]) if trans_b else _dot(a_ref[...], b_ref[...])

        def result(r):
            if add_ref is not None:
                r = r + add_ref[...]
            return r.astype(out_dtype)

        if nk == 1:
            o_ref[...] = result(part)
        else:
            @pl.when(k == 0)
            def _():
                acc_ref[...] = part

            @pl.when(jnp.logical_and(k > 0, k < nk - 1))
            def _():
                acc_ref[...] += part

            @pl.when(k == nk - 1)
            def _():
                o_ref[...] = result(acc_ref[...] + part)

    in_specs = [pl.BlockSpec((tk, tm), lambda i, j, k: (k, i)) if trans_a else pl.BlockSpec((tm, tk), lambda i, j, k: (i, k)),
                pl.BlockSpec((tn, tk), lambda i, j, k: (j, k)) if trans_b else pl.BlockSpec((tk, tn), lambda i, j, k: (k + b_koff, j))]
    args = [a, b]
    if add is not None:
        in_specs.append(pl.BlockSpec((tm, tn), lambda i, j, k: (i, j)))
        args.append(add)
    if after is not None:
        in_specs.append(pl.BlockSpec(memory_space=pl.ANY))
        args.append(after)
    return pl.pallas_call(
        body, name=name, grid=(M // tm, N // tn, nk), in_specs=in_specs,
        out_specs=pl.BlockSpec((tm, tn), lambda i, j, k: (i, j)),
        out_shape=jax.ShapeDtypeStruct((M, N), out_dtype),
        scratch_shapes=[pltpu.VMEM((tm, tn), F32)],
        compiler_params=_cp(("parallel", "parallel", "arbitrary")),
    )(*args)


def _dxn(segs, wpT, after, *, tm, tn, tk):
    S = segs[0][0].shape[0]
    N = wpT.shape[1]
    steps = [a.shape[1] // tk for a, _ in segs]
    firsts = [sum(steps[:q]) for q in range(len(segs))]
    nk = sum(steps)
    assert all(a.shape[1] % tk == 0 and r0 % tk == 0 for a, r0 in segs) and S % tm == 0 and N % tn == 0

    def body(*refs):
        a_refs, b_ref, o_ref, acc_ref = refs[:len(segs)], refs[len(segs)], refs[-2], refs[-1]
        k = pl.program_id(2)
        for a_ref, first, n in zip(a_refs, firsts, steps):
            @pl.when(jnp.logical_and(k >= first, k < first + n))
            def _(a_ref=a_ref):
                part = _dot(a_ref[...], b_ref[...])

                @pl.when(k == 0)
                def _():
                    acc_ref[...] = part

                @pl.when(jnp.logical_and(k > 0, k < nk - 1))
                def _():
                    acc_ref[...] += part

                @pl.when(k == nk - 1)
                def _():
                    o_ref[...] = acc_ref[...] + part

    def a_spec(first, n):
        return pl.BlockSpec((tm, tk), lambda i, j, k: (i, jnp.clip(k - first, 0, n - 1)))

    def b_block(k):
        return sum(jnp.where(jnp.logical_and(k >= first, k < first + n), r0 // tk + k - first, 0)
                   for (_, r0), first, n in zip(segs, firsts, steps))

    return pl.pallas_call(
        body, name="dxn", grid=(S // tm, N // tn, nk),
        in_specs=[a_spec(first, n) for first, n in zip(firsts, steps)]
        + [pl.BlockSpec((tk, tn), lambda i, j, k: (b_block(k), j))] + [pl.BlockSpec(memory_space=pl.ANY)] * (after is not None),
        out_specs=pl.BlockSpec((tm, tn), lambda i, j, k: (i, j)),
        out_shape=jax.ShapeDtypeStruct((S, N), F32),
        scratch_shapes=[pltpu.VMEM((tm, tn), F32)],
        compiler_params=_cp(("parallel", "parallel", "arbitrary")),
    )(*[a for a, _ in segs], wpT, *([after] if after is not None else []))


def _in_proj(x, w, wpT, *, tm, tn):
    S = x.shape[0]
    N = wpT.shape[0]
    assert S % tm == 0 and N % tn == 0, (S, N, tm, tn)

    def body(x_ref, w_ref, b_ref, xn_ref, o_ref, xs_ref):
        @pl.when(pl.program_id(1) == 0)
        def _():
            xv = x_ref[...]
            r = lax.rsqrt(jnp.mean(xv * xv, axis=-1, keepdims=True) + EPS)
            xs = (xv * r * w_ref[...]).astype(BF16)
            xs_ref[...] = xs
            xn_ref[...] = xs

        o_ref[...] = _dot_nt(xs_ref[...], b_ref[...]).astype(BF16)

    return pl.pallas_call(
        body, name="in_proj", grid=(S // tm, N // tn),
        in_specs=[pl.BlockSpec((tm, D), lambda i, j: (i, 0)), pl.BlockSpec((1, D), lambda i, j: (0, 0)),
                  pl.BlockSpec((tn, D), lambda i, j: (j, 0))],
        out_specs=[pl.BlockSpec((tm, D), lambda i, j: (i, 0)), pl.BlockSpec((tm, tn), lambda i, j: (i, j))],
        out_shape=[jax.ShapeDtypeStruct((S, D), BF16), jax.ShapeDtypeStruct((S, N), BF16)],
        scratch_shapes=[pltpu.VMEM((tm, D), BF16)],
        compiler_params=_cp(("parallel", "arbitrary")),
    )(x, w, wpT)


def _norm_bwd(x, w, dxn, dh, *, tm):
    S = x.shape[0]

    def body(x_ref, w_ref, dxn_ref, dh_ref, gx_ref, dw_ref):
        xv = x_ref[...]
        r = lax.rsqrt(jnp.mean(xv * xv, axis=-1, keepdims=True) + EPS)
        xh = xv * r
        dxn_v = dxn_ref[...]
        dxh = dxn_v * w_ref[...]
        gx_ref[...] = dh_ref[...] + r * (dxh - xh * jnp.mean(dxh * xh, axis=-1, keepdims=True))

        @pl.when(pl.program_id(0) == 0)
        def _():
            dw_ref[...] = jnp.zeros_like(dw_ref)

        dw_ref[0:1, :] += jnp.sum(dxn_v * xh, axis=0, keepdims=True)

    row = pl.BlockSpec((tm, D), lambda i: (i, 0))
    return pl.pallas_call(
        body, name="norm_bwd", grid=(S // tm,),
        in_specs=[row, pl.BlockSpec((1, D), lambda i: (0, 0)), row, row],
        out_specs=[row, pl.BlockSpec((8, D), lambda i: (0, 0))],
        out_shape=[jax.ShapeDtypeStruct((S, D), F32), jax.ShapeDtypeStruct((8, D), F32)],
        compiler_params=_cp(("arbitrary",)),
    )(x, w, dxn, dh)


def _sgu_core(u_ref, v_ref, z_ref, g_ref, b_ref, wm_ref, bias_ref, vnb_ref, mixed_ref, tm):
    v = v_ref[...].astype(F32)
    mu = jnp.mean(v, axis=-1, keepdims=True)
    vc = v - mu
    rs = lax.rsqrt(jnp.mean(vc * vc, axis=-1, keepdims=True) + EPS)
    vh = vc * rs
    vnb_ref[...] = (vh * g_ref[...] + b_ref[...]).astype(BF16)
    for blk in range(tm // SGU_BLOCK):
        rows = pl.ds(blk * SGU_BLOCK, SGU_BLOCK)
        for gi in range(SGU_GROUPS):
            cols = pl.ds(gi * LANE, LANE)
            mixed_ref[rows, cols] = _dot(wm_ref[gi], vnb_ref[rows, cols]) + bias_ref[:, cols]
    return vh, rs


def _sgu_fwd(proj, g, b, wm, bias_full, *, tm):
    S = proj.shape[0]

    def body(u_ref, v_ref, z_ref, g_ref, b_ref, wm_ref, bias_ref, y_ref, vnb_ref, mixed_ref):
        _sgu_core(u_ref, v_ref, z_ref, g_ref, b_ref, wm_ref, bias_ref, vnb_ref, mixed_ref, tm)
        z = z_ref[...].astype(F32)
        y_ref[...] = (u_ref[...].astype(F32) * mixed_ref[...] * (z * _sigmoid(z))).astype(BF16)

    seg = lambda off: pl.BlockSpec((tm, D), lambda i: (i, off // D))
    full = lambda a: pl.BlockSpec(a.shape, lambda i: (0,) * a.ndim)
    return pl.pallas_call(
        body, name="sgu_fwd", grid=(S // tm,),
        in_specs=[seg(OFF_U), seg(OFF_V), seg(OFF_ZA), full(g), full(b), full(wm), full(bias_full)],
        out_specs=pl.BlockSpec((tm, D), lambda i: (i, 0)),
        out_shape=jax.ShapeDtypeStruct((S, D), BF16),
        scratch_shapes=[pltpu.VMEM((tm, D), BF16), pltpu.VMEM((tm, D), F32)],
        compiler_params=_cp(("parallel",)),
    )(proj, proj, proj, g, b, wm, bias_full)


def _sgu_bwd(proj, dy, g, b, wm, wmT, bias_full, mask, sel, *, tm):
    S = proj.shape[0]
    nsteps = S // tm

    def body(u_ref, v_ref, z_ref, dy_ref, g_ref, b_ref, wm_ref, wmT_ref, bias_ref, mask_ref, sel_ref,
             dp_ref, dws_ref, dbs_ref, dg_ref, db_ref, vnb_ref, mixed_ref, dmb_ref, dvn_ref, dbias_ref):
        i = pl.program_id(0)

        @pl.when(i == 0)
        def _():
            dws_ref[...] = jnp.zeros_like(dws_ref)
            dg_ref[...] = jnp.zeros_like(dg_ref)
            db_ref[...] = jnp.zeros_like(db_ref)
            dbias_ref[...] = jnp.zeros_like(dbias_ref)

        vh, rs = _sgu_core(u_ref, v_ref, z_ref, g_ref, b_ref, wm_ref, bias_ref, vnb_ref, mixed_ref, tm)
        u = u_ref[...].astype(F32)
        z = z_ref[...].astype(F32)
        dy_v = dy_ref[...].astype(F32)
        mixed = mixed_ref[...]
        sg = _sigmoid(z)
        sz = z * sg
        dp_ref[:, 0:D] = (dy_v * mixed * sz).astype(BF16)
        dp_ref[:, 2 * D:3 * D] = (dy_v * u * mixed * (sg * (1.0 + z * (1.0 - sg)))).astype(BF16)
        dmixed = dy_v * u * sz
        dmb_ref[...] = dmixed.astype(BF16)
        for blk in range(tm // SGU_BLOCK):
            dbias_ref[...] += dmixed[blk * SGU_BLOCK:(blk + 1) * SGU_BLOCK, :]
        for blk in range(tm // SGU_BLOCK):
            rows = pl.ds(blk * SGU_BLOCK, SGU_BLOCK)
            for gi in range(SGU_GROUPS):
                cols = pl.ds(gi * LANE, LANE)
                dm = dmb_ref[rows, cols]
                dvn_ref[rows, cols] = _dot(wmT_ref[gi], dm)
                dws_ref[gi] += _dot_nt(dm, vnb_ref[rows, cols])
        dvn = dvn_ref[...]
        dg_ref[0:1, :] += jnp.sum(dvn * vh, axis=0, keepdims=True)
        db_ref[0:1, :] += jnp.sum(dvn, axis=0, keepdims=True)
        dvh = dvn * g_ref[...]
        dv = rs * (dvh - jnp.mean(dvh, axis=-1, keepdims=True) - vh * jnp.mean(dvh * vh, axis=-1, keepdims=True))
        dp_ref[:, D:2 * D] = dv.astype(BF16)

        @pl.when(i == nsteps - 1)
        def _():
            for gi in range(SGU_GROUPS):
                dws_ref[gi] = dws_ref[gi] * mask_ref[...]
            dbs_ref[...] = _dot(dbias_ref[...], sel_ref[...], precision=HI)

    seg = lambda off: pl.BlockSpec((tm, D), lambda i: (i, off // D))
    full = lambda a: pl.BlockSpec(a.shape, lambda i: (0,) * a.ndim)
    return pl.pallas_call(
        body, name="sgu_bwd", grid=(nsteps,),
        in_specs=[seg(OFF_U), seg(OFF_V), seg(OFF_ZA), pl.BlockSpec((tm, D), lambda i: (i, 0)),
                  full(g), full(b), full(wm), full(wmT), full(bias_full), full(mask), full(sel)],
        out_specs=[pl.BlockSpec((tm, 3 * D), lambda i: (i, 0)),
                   pl.BlockSpec((SGU_GROUPS, SGU_BLOCK, SGU_BLOCK), lambda i: (0, 0, 0)),
                   pl.BlockSpec((SGU_BLOCK, LANE), lambda i: (0, 0)),
                   pl.BlockSpec((8, D), lambda i: (0, 0)), pl.BlockSpec((8, D), lambda i: (0, 0))],
        out_shape=[jax.ShapeDtypeStruct((S, 3 * D), BF16),
                   jax.ShapeDtypeStruct((SGU_GROUPS, SGU_BLOCK, SGU_BLOCK), F32),
                   jax.ShapeDtypeStruct((SGU_BLOCK, LANE), F32),
                   jax.ShapeDtypeStruct((8, D), F32), jax.ShapeDtypeStruct((8, D), F32)],
        scratch_shapes=[pltpu.VMEM((tm, D), BF16), pltpu.VMEM((tm, D), F32), pltpu.VMEM((tm, D), BF16),
                        pltpu.VMEM((tm, D), F32), pltpu.VMEM((SGU_BLOCK, D), F32)],
        compiler_params=_cp(("arbitrary",)),
    )(proj, proj, proj, dy, g, b, wm, wmT, bias_full, mask, sel)


SSD_T = 2 * CHUNK
HALO = 8
HALO_BLK = 16


def _pair_masks():
    row = lax.broadcasted_iota(jnp.int32, (CHUNK, LANE), 0)
    lane = lax.broadcasted_iota(jnp.int32, (CHUNK, LANE), 1)
    pos = jnp.where(lane >= CHUNK, lane - CHUNK, lane)
    diag = (row == pos).astype(F32)
    causal = row >= pos
    lo = (lane < CHUNK).astype(F32)
    return diag, causal, lo, 1.0 - lo


def _ssd_chunk_fwd(c, ext_ref, shift_ref, dt_ref, cw_ref, cb_ref, dtb_ref, alog_ref, tri_ref, exp_ref):
    r0 = c * CHUNK
    win = ext_ref[pl.ds(r0, HALO_BLK + CHUNK), :]
    sh = _dot(shift_ref[...], win)
    taps = [sh[k * CHUNK:(k + 1) * CHUNK] for k in range(CONV_K - 1)] + [win[HALO_BLK:].astype(F32)]
    pre = cb_ref[...] + sum(cw_ref[k:k + 1, :] * taps[k] for k in range(CONV_K))
    sg = _sigmoid(pre)
    xc = pre * sg
    dtr = dt_ref[pl.ds(r0, CHUNK), :].astype(F32) + dtb_ref[...]
    dtv = _softplus(dtr)
    A = -jnp.exp(alog_ref[...])
    acs = _sel_left(tri_ref[...], dtv * A)
    both = _sel_right_k(jnp.concatenate([acs, dtv], axis=0), exp_ref[...])
    E, dtE = both[0:CHUNK], both[CHUNK:2 * CHUNK]
    return dict(taps=taps, pre=pre, sg=sg, xc=xc, dtr=dtr, dtv=dtv, A=A, E=E, dtE=dtE)


def _ssd_fwd(proj, conv_w, conv_b, dtb_p, alog_p, d_exp, norm_w, tri, expand, shift):
    S = proj.shape[0]
    T = SSD_T
    nsteps = S // T
    ncl = T // CHUNK

    def body(zb_ref, xbc_ref, halo_ref, dt_ref, cw_ref, cb_ref, dtb_ref, alog_ref, dexp_ref, nw_ref, tri_ref, exp_ref, shift_ref,
             y_ref, yb_ref, st_ref, ht_ref, ext_ref):
        i = pl.program_id(0)

        @pl.when(i == 0)
        def _():
            ht_ref[...] = jnp.zeros_like(ht_ref)
            ext_ref[0:HALO_BLK, :] = jnp.zeros((HALO_BLK, XBC_W), BF16)

        @pl.when(i > 0)
        def _():
            ext_ref[0:HALO_BLK, :] = halo_ref[...]

        ext_ref[HALO_BLK:HALO_BLK + T, :] = xbc_ref[...]
        diag, causal, lo, hi = _pair_masks()
        for c in range(ncl):
            q = _ssd_chunk_fwd(c, ext_ref, shift_ref, dt_ref, cw_ref, cb_ref, dtb_ref, alog_ref, tri_ref, exp_ref)
            rows = pl.ds(c * CHUNK, CHUNK)
            xc, E, dtE = q["xc"], q["E"], q["dtE"]
            xs = xc[:, 0:D]
            total = E[CHUNK - 1:CHUNK, :]
            x_dt = xs * dtE
            eE = jnp.exp(E)
            xw = x_dt * jnp.exp(total - E)
            st_ref[c] = ht_ref[...]
            for g in range(SSD_GROUPS):
                gc = slice(g * GROUP_W, (g + 1) * GROUP_W)
                Bg = xc[:, D + g * STATE:D + (g + 1) * STATE].astype(BF16)
                Cg = xc[:, D + SSD_GROUPS * STATE + g * STATE:D + SSD_GROUPS * STATE + (g + 1) * STATE].astype(BF16)
                cb2 = _dot_nt(Cg, jnp.concatenate([Bg, Bg], axis=0))
                htg = ht_ref[:, gc]
                y_ref[rows, gc] = eE[:, gc] * _dot(Cg, htg.astype(BF16)) + xs[:, gc] * dexp_ref[:, gc]
                for jj in range(GROUP_W // LANE):
                    pc = slice(g * GROUP_W + jj * LANE, g * GROUP_W + (jj + 1) * LANE)
                    Ej = E[:, pc]
                    e2 = jnp.sum(Ej * diag, axis=0, keepdims=True)
                    Mp = cb2 * jnp.exp(jnp.where(causal, Ej - e2, -1e30))
                    xj = x_dt[:, pc]
                    xbd = jnp.concatenate([xj * lo, xj * hi], axis=0).astype(BF16)
                    y_ref[rows, pc] += _dot(Mp.astype(BF16), xbd)
                ht_ref[:, gc] = jnp.exp(total[:, gc]) * htg + _dot_tn(Bg, xw[:, gc].astype(BF16))
            zb = zb_ref[rows, :].astype(F32)
            hh = y_ref[rows, :] * (zb * _sigmoid(zb))
            for g in range(SSD_GROUPS):
                gc = slice(g * GROUP_W, (g + 1) * GROUP_W)
                hg = hh[:, gc]
                r = lax.rsqrt(jnp.mean(hg * hg, axis=-1, keepdims=True) + EPS)
                yb_ref[rows, gc] = (hg * r * nw_ref[:, gc]).astype(BF16)

    full = lambda a: pl.BlockSpec(a.shape, lambda i: (0,) * a.ndim)
    hb = T // HALO_BLK
    return pl.pallas_call(
        body, name="ssd_fwd", grid=(nsteps,),
        in_specs=[pl.BlockSpec((T, D), lambda i: (i, OFF_ZB // D)),
                  pl.BlockSpec((T, XBC_W), lambda i: (i, OFF_XBC // XBC_W)),
                  pl.BlockSpec((HALO_BLK, XBC_W), lambda i: (jnp.maximum(i * hb - 1, 0), OFF_XBC // XBC_W)),
                  pl.BlockSpec((T, DT_W), lambda i: (i, OFF_DT // DT_W)),
                  full(conv_w), full(conv_b), full(dtb_p), full(alog_p), full(d_exp), full(norm_w), full(tri), full(expand),
                  full(shift)],
        out_specs=[pl.BlockSpec((T, D), lambda i: (i, 0)), pl.BlockSpec((T, D), lambda i: (i, 0)),
                   pl.BlockSpec((ncl, STATE, D), lambda i: (i, 0, 0))],
        out_shape=[jax.ShapeDtypeStruct((S, D), F32), jax.ShapeDtypeStruct((S, D), BF16),
                   jax.ShapeDtypeStruct((S // CHUNK, STATE, D), F32)],
        scratch_shapes=[pltpu.VMEM((STATE, D), F32), pltpu.VMEM((HALO_BLK + T, XBC_W), BF16)],
        compiler_params=_cp(("arbitrary",)),
    )(proj, proj, proj, proj, conv_w, conv_b, dtb_p, alog_p, d_exp, norm_w, tri, expand, shift)


def _ssd_bwd(proj, dyb, y, states, conv_w, conv_b, dtb_p, alog_p, d_exp, norm_w, tri, triT, expand, expandT, shift):
    S = proj.shape[0]
    T = SSD_T
    nsteps = S // T
    ncl = T // CHUNK
    SSD_W = SSD_PAD_W

    def body(zb_ref, xbc_ref, halo_ref, dt_ref, dyb_ref, y_ref, st_ref, cw_ref, cb_ref, dtb_ref, alog_ref, dexp_ref, nw_ref,
             tri_ref, triT_ref, exp_ref, expT_ref, shift_ref,
             dp_ref, dcw_ref, dcb_ref, ddtb_ref, dalog_ref, dD_ref, dnw_ref,
             dht_ref, ext_ref, dpre_ref, dy_s, dE_s, dxdt_s, dxc_s, dDacc_ref, dAacc_ref):
        i = pl.program_id(0)

        @pl.when(i == 0)
        def _():
            for r in (dht_ref, dcw_ref, dcb_ref, ddtb_ref, dnw_ref, dDacc_ref, dAacc_ref):
                r[...] = jnp.zeros_like(r)
            dpre_ref[T:T + HALO_BLK, :] = jnp.zeros((HALO_BLK, XBC_W), F32)

        @pl.when(i == nsteps - 1)
        def _():
            ext_ref[0:HALO_BLK, :] = jnp.zeros((HALO_BLK, XBC_W), BF16)

        @pl.when(i < nsteps - 1)
        def _():
            ext_ref[0:HALO_BLK, :] = halo_ref[...]

        ext_ref[HALO_BLK:HALO_BLK + T, :] = xbc_ref[...]
        diag, causal, lo, hi = _pair_masks()
        last_row = (lax.broadcasted_iota(jnp.int32, (CHUNK, 1), 0) == CHUNK - 1).astype(F32)
        for c in reversed(range(ncl)):
            q = _ssd_chunk_fwd(c, ext_ref, shift_ref, dt_ref, cw_ref, cb_ref, dtb_ref, alog_ref, tri_ref, exp_ref)
            rows = pl.ds(c * CHUNK, CHUNK)
            pre, sg, xc, dtr, dtv, A, E, dtE = (q[k] for k in ("pre", "sg", "xc", "dtr", "dtv", "A", "E", "dtE"))
            xs = xc[:, 0:D]
            total = E[CHUNK - 1:CHUNK, :]
            x_dt = xs * dtE
            eE = jnp.exp(E)
            wdec = jnp.exp(total - E)
            zb = zb_ref[rows, :].astype(F32)
            yv = y_ref[rows, :]
            sgz = _sigmoid(zb)
            sz = zb * sgz
            hh = yv * sz
            for g in range(SSD_GROUPS):
                gc = slice(g * GROUP_W, (g + 1) * GROUP_W)
                hg = hh[:, gc]
                r = lax.rsqrt(jnp.mean(hg * hg, axis=-1, keepdims=True) + EPS)
                dyb_g = dyb_ref[rows, gc].astype(F32)
                dn = dyb_g * nw_ref[:, gc]
                dnw_ref[0:1, gc] += jnp.sum(dyb_g * hg * r, axis=0, keepdims=True)
                dy_s[:, gc] = r * dn - hg * (r * r * r) * jnp.mean(dn * hg, axis=-1, keepdims=True)
            dhh = dy_s[...]
            dp_ref[rows, 0:D] = (dhh * yv * (sgz * (1.0 + zb * (1.0 - sgz)))).astype(BF16)
            dy = dhh * sz
            dy_s[...] = dy
            dDacc_ref[0:1, :] += jnp.sum(dy * xs, axis=0, keepdims=True)
            dxc_s[:, 0:D] = dy * dexp_ref[...]
            for g in range(SSD_GROUPS):
                gc = slice(g * GROUP_W, (g + 1) * GROUP_W)
                bcol = slice(D + g * STATE, D + (g + 1) * STATE)
                ccol = slice(D + SSD_GROUPS * STATE + g * STATE, D + SSD_GROUPS * STATE + (g + 1) * STATE)
                Bg = xc[:, bcol].astype(BF16)
                Cg = xc[:, ccol].astype(BF16)
                B2 = jnp.concatenate([Bg, Bg], axis=0)
                cb2 = _dot_nt(Cg, B2)
                htg = st_ref[c, :, gc]
                htb = htg.astype(BF16)
                dhn = dht_ref[:, gc]
                dhnb = dhn.astype(BF16)
                dyg = dy[:, gc]
                eEg = eE[:, gc]
                wg = wdec[:, gc]
                xdg = x_dt[:, gc]
                CH = _dot(Cg, htb)
                dCHb = (dyg * eEg).astype(BF16)
                dC = _dot_nt(dCHb, htb)
                dl = jnp.exp(total[:, gc])
                dht_prev = _dot_tn(Cg, dCHb) + dl * dhn
                dtot = jnp.sum(dhn * htg, axis=0, keepdims=True) * dl
                dxw = _dot(Bg, dhnb)
                dB = _dot_nt((xdg * wg).astype(BF16), dhnb)
                dwd = dxw * xdg * wg
                dtot = dtot + jnp.sum(dwd, axis=0, keepdims=True)
                dE_s[:, gc] = dyg * eEg * CH - dwd + last_row * dtot
                dxdt_s[:, gc] = dxw * wg
                dcb2 = jnp.zeros((CHUNK, LANE), F32)
                for jj in range(GROUP_W // LANE):
                    pc = slice(g * GROUP_W + jj * LANE, g * GROUP_W + (jj + 1) * LANE)
                    Ej = E[:, pc]
                    e2 = jnp.sum(Ej * diag, axis=0, keepdims=True)
                    Lp = jnp.exp(jnp.where(causal, Ej - e2, -1e30))
                    Mp = cb2 * Lp
                    xj = x_dt[:, pc]
                    xbd = jnp.concatenate([xj * lo, xj * hi], axis=0).astype(BF16)
                    dyj = dy[:, pc].astype(BF16)
                    dMp = _dot_nt(dyj, xbd)
                    dxbd = _dot_tn(Mp.astype(BF16), dyj)
                    dxdt_s[:, pc] += dxbd[0:CHUNK, :] * lo + dxbd[CHUNK:2 * CHUNK, :] * hi
                    dcb2 = dcb2 + dMp * Lp
                    dseg = dMp * Mp
                    dE_s[:, pc] += dseg - diag * jnp.sum(dseg, axis=0, keepdims=True)
                dcb2b = dcb2.astype(BF16)
                dC = dC + _dot(dcb2b, B2)
                dB2 = _dot_tn(dcb2b, Cg)
                dB = dB + dB2[0:CHUNK, :] + dB2[CHUNK:2 * CHUNK, :]
                dxc_s[:, bcol] = dB
                dxc_s[:, ccol] = dC
                dht_ref[:, gc] = dht_prev
            dx_dt = dxdt_s[...]
            dxc_s[:, 0:D] += dx_dt * dtE
            red = _sel_right(jnp.concatenate([dE_s[...], dx_dt * xs], axis=0), expT_ref[...])
            da = _sel_left(triT_ref[...], red[0:CHUNK, :])
            ddtv = red[CHUNK:2 * CHUNK, :] + da * A
            dAacc_ref[0:1, :] += jnp.sum(da * dtv, axis=0, keepdims=True)
            ddtr = ddtv * _sigmoid(dtr)
            ddtb_ref[0:1, :] += jnp.sum(ddtr, axis=0, keepdims=True)
            dp_ref[rows, D + XBC_W:D + XBC_W + DT_W] = ddtr.astype(BF16)
            dpre = dxc_s[...] * (sg * (1.0 + pre * (1.0 - sg)))
            dpre_ref[rows, :] = dpre
            dcb_ref[0:1, :] += jnp.sum(dpre, axis=0, keepdims=True)
            for k in range(CONV_K):
                dcw_ref[k:k + 1, :] += jnp.sum(dpre * q["taps"][k], axis=0, keepdims=True)
        dxbc = jnp.zeros((T, XBC_W), F32)
        for k in range(CONV_K):
            dxbc = dxbc + cw_ref[k:k + 1, :] * dpre_ref[pl.ds(CONV_K - 1 - k, T), :]
        dp_ref[:, D:D + XBC_W] = dxbc.astype(BF16)
        dp_ref[:, SEG_SSD[1]:SSD_W] = jnp.zeros((T, SSD_W - SEG_SSD[1]), BF16)
        dpre_ref[T:T + HALO, :] = dpre_ref[0:HALO, :]

        @pl.when(i == nsteps - 1)
        def _():
            dalog_ref[...] = dAacc_ref[...] * (-jnp.exp(alog_ref[...]))
            dD_ref[...] = _dot(dDacc_ref[...], expT_ref[...].astype(F32), precision=HI)

    full = lambda a: pl.BlockSpec(a.shape, lambda i: (0,) * a.ndim)
    hb = T // HALO_BLK
    rev = lambda i: nsteps - 1 - i
    acc = lambda w: pl.BlockSpec((8, w), lambda i: (0, 0))
    return pl.pallas_call(
        body, name="ssd_bwd", grid=(nsteps,),
        in_specs=[pl.BlockSpec((T, D), lambda i: (rev(i), OFF_ZB // D)),
                  pl.BlockSpec((T, XBC_W), lambda i: (rev(i), OFF_XBC // XBC_W)),
                  pl.BlockSpec((HALO_BLK, XBC_W), lambda i: (jnp.maximum(rev(i) * hb - 1, 0), OFF_XBC // XBC_W)),
                  pl.BlockSpec((T, DT_W), lambda i: (rev(i), OFF_DT // DT_W)),
                  pl.BlockSpec((T, D), lambda i: (rev(i), 0)), pl.BlockSpec((T, D), lambda i: (rev(i), 0)),
                  pl.BlockSpec((ncl, STATE, D), lambda i: (rev(i), 0, 0)),
                  full(conv_w), full(conv_b), full(dtb_p), full(alog_p), full(d_exp), full(norm_w),
                  full(tri), full(triT), full(expand), full(expandT), full(shift)],
        out_specs=[pl.BlockSpec((T, SSD_W), lambda i: (rev(i), 0)),
                   acc(XBC_W), acc(XBC_W), acc(DT_W), acc(DT_W), acc(DT_W), acc(D)],
        out_shape=[jax.ShapeDtypeStruct((S, SSD_W), BF16),
                   jax.ShapeDtypeStruct((8, XBC_W), F32), jax.ShapeDtypeStruct((8, XBC_W), F32),
                   jax.ShapeDtypeStruct((8, DT_W), F32), jax.ShapeDtypeStruct((8, DT_W), F32),
                   jax.ShapeDtypeStruct((8, DT_W), F32), jax.ShapeDtypeStruct((8, D), F32)],
        scratch_shapes=[pltpu.VMEM((STATE, D), F32), pltpu.VMEM((HALO_BLK + T, XBC_W), BF16), pltpu.VMEM((T + HALO_BLK, XBC_W), F32),
                        pltpu.VMEM((CHUNK, D), F32), pltpu.VMEM((CHUNK, D), F32), pltpu.VMEM((CHUNK, D), F32),
                        pltpu.VMEM((CHUNK, XBC_W), F32), pltpu.VMEM((8, D), F32), pltpu.VMEM((8, DT_W), F32)],
        compiler_params=_cp(("arbitrary",)),
    )(proj, proj, proj, proj, dyb, y, states, conv_w, conv_b, dtb_p, alog_p, d_exp, norm_w, tri, triT, expand, expandT, shift)


def _head(x, ya, yb, proj, target, gate_b, wout, fw, *, tm):
    S = x.shape[0]

    def body(x_ref, ya_ref, yb_ref, gl0_ref, gl1_ref, t_ref, gb_ref, w_ref, fw_ref,
             dh_ref, dhb_ref, mb_ref, dya_ref, dyb_ref, dgl_ref, loss_ref, dfw_ref, dgb_ref):
        @pl.when(pl.program_id(0) == 0)
        def _():
            loss_ref[...] = jnp.zeros_like(loss_ref)
            dfw_ref[...] = jnp.zeros_like(dfw_ref)
            dgb_ref[...] = jnp.zeros_like(dgb_ref)

        ya_v = ya_ref[...].astype(F32)
        yb_v = yb_ref[...].astype(F32)
        g0 = _sigmoid(gl0_ref[...].astype(F32) + gb_ref[:, 0:D])
        g1 = _sigmoid(gl1_ref[...].astype(F32) + gb_ref[:, D:2 * D])
        mb = (g0 * ya_v + g1 * yb_v).astype(BF16)
        mb_ref[...] = mb
        h = x_ref[...] + _dot(mb, w_ref[...])
        r = lax.rsqrt(jnp.mean(h * h, axis=-1, keepdims=True) + EPS)
        hn = h * r
        err = hn * fw_ref[...] - t_ref[...]
        loss_ref[...] += 0.5 * jnp.sum(jnp.mean(err * err, axis=-1, keepdims=True))
        dyf = err * (1.0 / D)
        dfw_ref[0:1, :] += jnp.sum(dyf * hn, axis=0, keepdims=True)
        dhn = dyf * fw_ref[...]
        dh = r * (dhn - hn * jnp.mean(dhn * hn, axis=-1, keepdims=True))
        dh_ref[...] = dh
        dhb = dh.astype(BF16)
        dhb_ref[...] = dhb
        dm = _dot_nt(dhb, w_ref[...])
        dya_ref[...] = (dm * g0).astype(BF16)
        dyb_ref[...] = (dm * g1).astype(BF16)
        dgl0 = dm * ya_v * g0 * (1.0 - g0)
        dgl1 = dm * yb_v * g1 * (1.0 - g1)
        dgl_ref[:, 0:D] = dgl0.astype(BF16)
        dgl_ref[:, D:2 * D] = dgl1.astype(BF16)
        dgb_ref[0:1, 0:D] += jnp.sum(dgl0, axis=0, keepdims=True)
        dgb_ref[0:1, D:2 * D] += jnp.sum(dgl1, axis=0, keepdims=True)

    row = pl.BlockSpec((tm, D), lambda i: (i, 0))
    seg = lambda off: pl.BlockSpec((tm, D), lambda i: (i, off // D))
    full = lambda a: pl.BlockSpec(a.shape, lambda i: (0,) * a.ndim)
    acc = lambda w: pl.BlockSpec((8, w), lambda i: (0, 0))
    return pl.pallas_call(
        body, name="head", grid=(S // tm,),
        in_specs=[row, row, row, seg(OFF_G0), seg(OFF_G1), row, full(gate_b), full(wout), full(fw)],
        out_specs=[row, row, row, row, row, pl.BlockSpec((tm, 2 * D), lambda i: (i, 0)), acc(LANE), acc(D), acc(2 * D)],
        out_shape=[jax.ShapeDtypeStruct((S, D), F32), jax.ShapeDtypeStruct((S, D), BF16), jax.ShapeDtypeStruct((S, D), BF16),
                   jax.ShapeDtypeStruct((S, D), BF16), jax.ShapeDtypeStruct((S, D), BF16), jax.ShapeDtypeStruct((S, 2 * D), BF16),
                   jax.ShapeDtypeStruct((8, LANE), F32), jax.ShapeDtypeStruct((8, D), F32), jax.ShapeDtypeStruct((8, 2 * D), F32)],
        compiler_params=_cp(("arbitrary",)),
    )(x, ya, yb, proj, proj, target, gate_b, wout, fw)


def _adam_update(g, w_ref, m_ref, v_ref, g_ref, d_ref, m2_ref, v2_ref):
    m2 = ADAM_B1 * m_ref[...] + (1.0 - ADAM_B1) * g
    v2 = ADAM_B2 * v_ref[...] + (1.0 - ADAM_B2) * (g * g)
    m_hat = m2 / (1.0 - ADAM_B1 ** ADAM_STEP)
    v_hat = v2 / (1.0 - ADAM_B2 ** ADAM_STEP)
    g_ref[...] = g
    d_ref[...] = -ADAM_LR * (m_hat / (jnp.sqrt(v_hat) + ADAM_EPS) + ADAM_WD * w_ref[...])
    m2_ref[...] = m2
    v2_ref[...] = v2


def _adamw_own(me, own, landed, w, m, v, *, tr, tc, name):
    _, R, C = landed.shape
    assert R % tr == 0 and C % tc == 0, (name, R, C, tr, tc)

    def body(me_ref, own_ref, p_ref, w_ref, m_ref, v_ref, g_ref, d_ref, m2_ref, v2_ref):
        mine = own_ref[0].astype(F32)
        g = jnp.where(me_ref[0] == 0, mine, p_ref[0].astype(F32))
        for k in range(1, N_DEV):
            g = g + jnp.where(me_ref[0] == k, mine, p_ref[k].astype(F32))
        _adam_update(g, w_ref, m_ref, v_ref, g_ref, d_ref, m2_ref, v2_ref)

    tile = pl.BlockSpec((tr, tc), lambda i, j, me_ref: (i, j))
    return pl.pallas_call(
        body, name=name,
        grid_spec=pltpu.PrefetchScalarGridSpec(
            num_scalar_prefetch=1, grid=(R // tr, C // tc),
            in_specs=[pl.BlockSpec((1, tr, tc), lambda i, j, me_ref: (me_ref[0], i, j)),
                      pl.BlockSpec((N_DEV, tr, tc), lambda i, j, me_ref: (0, i, j)), tile, tile, tile],
            out_specs=[tile, tile, tile, tile]),
        out_shape=[jax.ShapeDtypeStruct((R, C), F32)] * 4,
        compiler_params=_cp(("parallel", "parallel")),
    )(me, own, landed, w, m, v)


def _adamw(parts, w, m, v, *, tr, name):
    _, R, C = parts.shape
    assert R % tr == 0, (name, R, tr)

    def body(p_ref, w_ref, m_ref, v_ref, g_ref, d_ref, m2_ref, v2_ref):
        g = p_ref[0].astype(F32)
        for k in range(1, N_DEV):
            g = g + p_ref[k].astype(F32)
        _adam_update(g, w_ref, m_ref, v_ref, g_ref, d_ref, m2_ref, v2_ref)

    row = pl.BlockSpec((tr, C), lambda i: (i, 0))
    return pl.pallas_call(
        body, name=name, grid=(R // tr,),
        in_specs=[pl.BlockSpec((N_DEV, tr, C), lambda i: (0, i, 0)), row, row, row],
        out_specs=[row, row, row, row],
        out_shape=[jax.ShapeDtypeStruct((R, C), F32)] * 4,
        compiler_params=_cp(("parallel",)),
    )(parts, w, m, v)


def _place():
    x, y, c = lax.axis_index("x"), lax.axis_index("y"), lax.axis_index("c")
    return x, y, c


def _all_gather(arrs, *, name):
    n = len(arrs)

    def body(*refs):
        ins, outs = refs[:n], refs[n:2 * n]
        send_sems, recv_sems, local_sems = refs[2 * n:]
        x, y, c = _place()
        me, sibling = (x, y, c), (x, y, 1 - c)
        chips = [(1 - x, y), (x, 1 - y), (1 - x, 1 - y)]

        def idx(px, py, pc):
            return 4 * px + 2 * py + pc

        def copy(k, a, block, to, src=None):
            slab = outs[a].at[idx(*block)]
            return pltpu.make_async_remote_copy(
                src_ref=slab if src is None else src, dst_ref=slab,
                send_sem=send_sems.at[k, a], recv_sem=recv_sems.at[k, a], device_id=to, device_id_type=MESH)

        mine = [pltpu.make_async_copy(ins[a], outs[a].at[idx(*me)], local_sems.at[a]) for a in range(n)]
        for cp in mine:
            cp.start()
        first = []
        for a in range(n):
            first.append(copy(0, a, me, sibling, src=ins[a]))
            first += [copy(1 + j, a, me, (*chip, c), src=ins[a]) for j, chip in enumerate(chips)]
        for cp in first:
            cp.start()
        passed = []
        for j, chip in enumerate(chips):
            for a in range(n):
                copy(1 + j, a, (*chip, c), me).wait_recv()
                fwd = copy(4 + j, a, (*chip, c), sibling)
                fwd.start()
                passed.append(fwd)
        for a in range(n):
            copy(0, a, sibling, me).wait_recv()
            for j, chip in enumerate(chips):
                copy(4 + j, a, (*chip, 1 - c), me).wait_recv()
        for cp in first + passed:
            cp.wait_send()
        for cp in mine:
            cp.wait()

    anyspec = pl.BlockSpec(memory_space=pl.ANY)
    return pl.pallas_call(
        body, name=name,
        in_specs=[anyspec] * n, out_specs=[anyspec] * n,
        out_shape=[jax.ShapeDtypeStruct((N_DEV,) + a.shape, a.dtype) for a in arrs],
        scratch_shapes=[pltpu.SemaphoreType.DMA((7, n)), pltpu.SemaphoreType.DMA((7, n)), pltpu.SemaphoreType.DMA((n,))],
    )(*arrs)


W_ROWS = SEG_SSD[0] + SSD_PAD_W


GROUP = 16
INTERIOR = 1920


def _interior(k):
    lo = -(-(k * SHARD_IN) // GROUP) * GROUP
    hi = ((k + 1) * SHARD_IN) // GROUP * GROUP
    return lo, hi


def _dest_row(r):
    if r < REF_SGU_END:
        return r
    return r - REF_SGU_END + SEG_SSD[0] if r < REF_GATE_START else r - REF_GATE_START + SEG_GATE[0]


def _shard_pieces(k):
    lo_k, hi_k = _interior(k)
    out = []
    for lo, hi in ((0, REF_SGU_END), (REF_SGU_END, REF_GATE_START), (REF_GATE_START, W_IN)):
        a, b = max(lo, lo_k), min(hi, hi_k)
        if a < b:
            out.append((a - lo_k, b - a, _dest_row(a)))
    return out


GATHER_PARTS = 1


def _shard_parts(k):
    parts = [[] for _ in range(GATHER_PARTS)]
    for s0, n, d0 in _shard_pieces(k):
        step = -(-(n // GROUP) // GATHER_PARTS) * GROUP
        for p in range(GATHER_PARTS):
            a, b = min(p * step, n), min((p + 1) * step, n)
            if a < b:
                parts[p].append((s0 + a, b - a, d0 + a))
    return parts


def _patch_straddlers(wpT, heads, tails):
    for k in range(1, N_DEV):
        m = (k * SHARD_IN) % GROUP
        if m:
            group = jnp.concatenate([tails[k - 1, GROUP - m:], heads[k, :GROUP - m]], axis=0)
            wpT = lax.dynamic_update_slice(wpT, group, (_dest_row(k * SHARD_IN - m), 0))
    return wpT


def _gather_stages(k, win_ref, small, z_ref, n_zero, w_ref, send_sems, recv_sems, local_sems):
    x, y, c = k // 4, (k // 2) % 2, k % 2
    idx = lambda p: 4 * p[0] + 2 * p[1] + p[2]
    me, sib = (x, y, c), (x, y, 1 - c)
    xn, yn, dg = (1 - x, y, c), (x, 1 - y, c), (1 - x, 1 - y, c)
    parts = range(GATHER_PARTS)

    def copies(slot, block, to, part, own=False):
        kb = idx(block)
        out = []
        for j, (s0, n, d0) in enumerate(_shard_parts(kb)[part]):
            dst = w_ref.at[pl.ds(d0, n)]
            out.append((win_ref.at[pl.ds(s0, n)] if own else dst, dst, 2 * part + j))
        if part == 0:
            for j, (src, gathered) in enumerate(small):
                out.append((src if own else gathered.at[kb], gathered.at[kb], 2 * GATHER_PARTS + j))
        return [pltpu.make_async_remote_copy(src_ref=s, dst_ref=d, send_sem=send_sems.at[slot, j], recv_sem=recv_sems.at[slot, j],
                                             device_id=to, device_id_type=MESH) for s, d, j in out]

    def start(cps):
        for cp in cps:
            cp.start()

    def arrived(slot, block, part):
        for cp in copies(slot, block, me, part):
            cp.wait_recv()

    def local():
        pairs = [(win_ref.at[pl.ds(s0, n)], w_ref.at[pl.ds(d0, n)]) for s0, n, d0 in _shard_pieces(k)]
        pairs += [(src, gathered.at[k]) for src, gathered in small] + [(z_ref, w_ref.at[pl.ds(W_IN, n_zero)])]
        return [pltpu.make_async_copy(s, d, local_sems.at[j]) for j, (s, d) in enumerate(pairs)]

    relay = (xn, yn) if c == 1 else (yn, xn)

    def first():
        start(local())
        for p in parts:
            start(copies(0, me, sib, p, own=True) + copies(1, me, xn, p, own=True) + copies(2, me, yn, p, own=True))

    def hand_on():
        for p in parts:
            arrived(1, xn, p)
            start(copies(4, xn, sib, p))
            if c == 1:
                start(copies(3, *relay, p))
            arrived(2, yn, p)
            start(copies(5, yn, sib, p))
            if c == 0:
                start(copies(3, *relay, p))

    def finish():
        for p in parts:
            arrived(3, dg, p)
            start(copies(6, dg, sib, p))
        for p in parts:
            arrived(0, sib, p)
            arrived(4, (1 - x, y, 1 - c), p)
            arrived(5, (x, 1 - y, 1 - c), p)
            arrived(6, (1 - x, 1 - y, 1 - c), p)
        for p in parts:
            sent = (copies(0, me, sib, p, own=True) + copies(1, me, xn, p, own=True) + copies(2, me, yn, p, own=True)
                    + copies(3, *relay, p) + copies(4, xn, sib, p) + copies(5, yn, sib, p) + copies(6, dg, sib, p))
            for cp in sent:
                cp.wait_send()
        for cp in local():
            cp.wait()

    return first, hand_on, finish


def _gather_sems(n_small):
    n_arr = 2 * GATHER_PARTS + n_small
    return [pltpu.SemaphoreType.DMA((7, n_arr)), pltpu.SemaphoreType.DMA((7, n_arr)), pltpu.SemaphoreType.DMA((n_arr + 1,))]


def _gather_weights(win, head, tail, wout, cw, zeros):
    small_in = (wout, cw, head, tail)
    n_zero = zeros.shape[0]
    assert W_IN + n_zero == W_ROWS and W_IN % GROUP == 0

    def body(win_ref, wout_ref, cw_ref, head_ref, tail_ref, z_ref, w_ref, gout_ref, gcw_ref, ghead_ref, gtail_ref, *sems):
        x, y, c = _place()
        me = 4 * x + 2 * y + c
        small = ((wout_ref, gout_ref), (cw_ref, gcw_ref), (head_ref, ghead_ref), (tail_ref, gtail_ref))

        def run(k):
            for stage in _gather_stages(k, win_ref, small, z_ref, n_zero, w_ref, *sems):
                stage()

        for k in range(N_DEV):
            pl.when(me == k)(functools.partial(run, k))

    anyspec = pl.BlockSpec(memory_space=pl.ANY)
    return pl.pallas_call(
        body, name="gather_weights", in_specs=[anyspec] * 6, out_specs=[anyspec] * 5,
        out_shape=[jax.ShapeDtypeStruct((W_ROWS, D), win.dtype)]
        + [jax.ShapeDtypeStruct((N_DEV,) + a.shape, a.dtype) for a in small_in],
        scratch_shapes=_gather_sems(len(small_in)),
    )(win, wout, cw, head, tail, zeros)


_REL = [(dx, dy, dc) for dx in (0, 1) for dy in (0, 1) for dc in (0, 1)][1:]
_HBM = pl.BlockSpec(memory_space=pltpu.HBM)
_SEM = pl.BlockSpec(memory_space=pltpu.SEMAPHORE)
_EFFECT = pltpu.SideEffectType.DATAFLOW_SIDE_EFFECTING


def _peer(k):
    x, y, c = _place()
    dx, dy, dc = _REL[k]
    return (1 - x if dx else x, 1 - y if dy else y, 1 - c if dc else c)


def _exchange_start(parts, *, name):
    n = len(parts)

    def body(*refs):
        ins, lands = refs[:n], refs[n:2 * n]
        send_sems, recv_sems, token = refs[2 * n], refs[2 * n + 1], refs[-1]
        x, y, c = _place()
        me = 4 * x + 2 * y + c
        for a in range(n):
            for k in range(len(_REL)):
                px, py, pc = _peer(k)
                pltpu.make_async_remote_copy(
                    src_ref=ins[a].at[4 * px + 2 * py + pc], dst_ref=lands[a].at[me],
                    send_sem=send_sems.at[len(_REL) * a + k], recv_sem=recv_sems.at[len(_REL) * a + k],
                    device_id=(px, py, pc), device_id_type=MESH).start()
        token[...] = jnp.zeros_like(token)

    sem = pltpu.SemaphoreType.DMA((len(_REL) * n,))
    bufs = [pltpu.HBM(p.shape, p.dtype) for p in parts]
    outs = pl.pallas_call(
        body, name=name,
        out_shape=(sem, sem, *bufs, *bufs, jax.ShapeDtypeStruct((8, LANE), F32)),
        in_specs=(_HBM,) * (2 * n), out_specs=(_SEM, _SEM, *(_HBM,) * (2 * n), pl.BlockSpec(memory_space=pltpu.VMEM)),
        input_output_aliases={i: 2 + i for i in range(2 * n)},
        compiler_params=pltpu.CompilerParams(has_side_effects=_EFFECT),
    )(*[pltpu.with_memory_space_constraint(p, pltpu.HBM) for p in parts],
      *[pltpu.with_memory_space_constraint(lax.empty(p.shape, p.dtype), pltpu.HBM) for p in parts])
    return outs[0], outs[1], outs[2:2 + n], outs[2 + n:2 + 2 * n], outs[-1]


def _exchange_wait(send_sems, recv_sems, parts, lands, after, *, name):
    n = len(parts)

    def body(*refs):
        ins, lands_ = refs[:n], refs[n:2 * n]
        ssem, rsem = refs[2 * n], refs[2 * n + 1]
        for a in range(n):
            for k in range(len(_REL)):
                px, py, pc = _peer(k)
                p = 4 * px + 2 * py + pc
                cp = pltpu.make_async_remote_copy(
                    src_ref=ins[a].at[p], dst_ref=lands_[a].at[p],
                    send_sem=ssem.at[len(_REL) * a + k], recv_sem=rsem.at[len(_REL) * a + k],
                    device_id=(px, py, pc), device_id_type=MESH)
                cp.wait_send()
                cp.wait_recv()

    bufs = [pltpu.HBM(p.shape, p.dtype) for p in parts]
    outs = pl.pallas_call(
        body, name=name, out_shape=(*bufs, *bufs),
        in_specs=(*(_HBM,) * (2 * n), _SEM, _SEM, pl.BlockSpec(memory_space=pl.ANY)), out_specs=(_HBM,) * (2 * n),
        input_output_aliases={i: i for i in range(2 * n)},
        compiler_params=pltpu.CompilerParams(has_side_effects=_EFFECT),
    )(*parts, *lands, send_sems, recv_sems, after)
    return outs[:n], outs[n:]


WEIGHTS = ('norm_w', 'w_in', 'gate_b', 'sgu_norm_g', 'sgu_norm_b', 'sgu_w', 'sgu_b', 'conv_w', 'conv_b', 'dt_bias', 'A_log',
           'D_skip', 'ssd_norm_w', 'w_out', 'final_norm_w')
SHARDED = ('w_in', 'conv_w', 'w_out')
PACK_ROW = 8 * LANE


def _constants():
    tri = np.tril(np.ones((CHUNK, CHUNK), np.float32))
    expand = np.zeros((DT_W, D), np.float32)
    for h in range(HEADS):
        expand[h, h * HEADDIM:(h + 1) * HEADDIM] = 1.0
    sel = np.zeros((D, LANE), np.float32)
    for g in range(SGU_GROUPS):
        sel[g * LANE:(g + 1) * LANE, g] = 1.0
    pos_chunk = np.arange(SGU_BLOCK) // CHUNK
    mask = (pos_chunk[None, :] <= pos_chunk[:, None]).astype(np.float32)
    shift = np.zeros(((CONV_K - 1) * CHUNK, HALO_BLK + CHUNK), np.float32)
    for kk in range(CONV_K - 1):
        for t in range(CHUNK):
            shift[kk * CHUNK + t, HALO_BLK - (CONV_K - 1) + t + kk] = 1.0
    return dict(tri=jnp.asarray(tri, BF16), triT=jnp.asarray(tri.T.copy(), BF16), expand=jnp.asarray(np.tile(expand, (3, 1)), BF16),
                shift=jnp.asarray(shift, BF16),
                expandT=jnp.asarray(expand.T.copy(), BF16), sel=jnp.asarray(sel), mask=jnp.asarray(mask))


def _to_shards(segs):
    starts = np.cumsum([0] + [n for _, n in segs])
    assert starts[-1] == W_IN
    slabs = []
    for k in range(N_DEV):
        pieces = []
        for (s, n), s0 in zip(segs, starts[:-1]):
            lo, hi = max(k * SHARD_IN, s0), min((k + 1) * SHARD_IN, s0 + n)
            if lo < hi:
                pieces.append(s[lo - s0:hi - s0])
        slabs.append(jnp.concatenate(pieces, axis=0))
    return jnp.stack(slabs)


def _local_step(x2, tgt, wpT, wout, cw, p, exchange_small, exchange):
    S = x2.shape[0]
    k = _constants()
    xn, proj = _in_proj(x2, p['norm_w'], wpT, tm=min(1024, S), tn=2048)
    wm32 = p['sgu_w'][0] * k['mask']
    wm = wm32.astype(BF16)
    wmT = jnp.swapaxes(wm32, 1, 2).astype(BF16)
    bias_full = jnp.repeat(p['sgu_b'][0].T, LANE, axis=1)
    tm_sgu = min(512, S)
    ya = _sgu_fwd(proj, p['sgu_norm_g'], p['sgu_norm_b'], wm, bias_full, tm=tm_sgu)
    pad32 = lambda a: jnp.pad(a, ((0, 0), (0, DT_W - HEADS)))
    dtb_p, alog_p = pad32(p['dt_bias']), pad32(p['A_log'])
    d_exp = jnp.repeat(p['D_skip'], HEADDIM, axis=1)
    ssd_args = (cw, p['conv_b'], dtb_p, alog_p, d_exp, p['ssd_norm_w'])
    y, yb, states = _ssd_fwd(proj, *ssd_args, k['tri'], k['expand'], k['shift'])
    dh, dhb, mb, dya, dyb, dgl, loss, dfw, dgb = _head(
        x2, ya, yb, proj, tgt, p['gate_b'], wout, p['final_norm_w'][None, :], tm=min(256, S))
    dsgu, dws, dbsT, dsg, dsb = _sgu_bwd(proj, dya, p['sgu_norm_g'], p['sgu_norm_b'], wm, wmT, bias_full, k['mask'], k['sel'],
                                         tm=tm_sgu)
    dssd, dcw, dcb, ddtb, dalog, dD, dnw = _ssd_bwd(proj, dyb, y, states, *ssd_args, k['tri'], k['triT'], k['expand'], k['expandT'],
                                                    k['shift'])
    grads = dict(
        gate_b=dgb[0:1], sgu_norm_g=dsg[0:1], sgu_norm_b=dsb[0:1], sgu_w=dws[None],
        sgu_b=dbsT[:, :SGU_GROUPS].T[None], conv_w=dcw[0:CONV_K][None], conv_b=dcb[0:1], dt_bias=ddtb[0:1, :HEADS],
        A_log=dalog[0:1, :HEADS], D_skip=dD[0:1, :HEADS], ssd_norm_w=dnw[0:1], final_norm_w=dfw[0])
    tw = dict(trans_a=True, out_dtype=BF16, tm=1024, tn=512, tk=S)
    dw_out = _matmul(mb, dhb, name="dw_out", **tw)
    token = exchange_small(loss[0, 0], grads, dw_out)
    dwT_sgu = _matmul(dsgu, xn, after=token, name="dw_in_sgu", **tw)
    dwT_gate = _matmul(dgl, xn, name="dw_in_gate", **tw)
    dwT_ssd = _matmul(dssd, xn, name="dw_in_ssd", **tw)
    token = exchange([(dwT_sgu, SEG_SGU[1]), (dwT_ssd, W_IN - SEG_SSD[0]), (dwT_gate, SEG_GATE[1])])
    dxn = _dxn([(dsgu, SEG_SGU[0]), (dgl, SEG_GATE[0]), (dssd, SEG_SSD[0])], wpT, token, tm=min(1024, S), tn=1024, tk=2048)
    grad_x, dnorm = _norm_bwd(x2, p['norm_w'], dxn, dh, tm=min(256, S))
    return grad_x, dnorm[0:1]


def _pack(arrs):
    rows, offs, r = [], [], 0
    for a in arrs:
        n = a.size
        nr = -(-n // PACK_ROW) * 8
        rows.append(jnp.pad(a.reshape(-1).astype(F32), (0, nr * LANE - n)).reshape(nr, LANE))
        offs.append(r)
        r += nr
    return jnp.concatenate(rows, axis=0), offs


def kernel(x, norm_w, w_in, gate_b, sgu_norm_g, sgu_norm_b, sgu_w, sgu_b, conv_w, conv_b, dt_bias, A_log, D_skip, ssd_norm_w, w_out, final_norm_w, loss_target, m_norm_w, m_w_in, m_gate_b, m_sgu_norm_g, m_sgu_norm_b, m_sgu_w, m_sgu_b, m_conv_w, m_conv_b, m_dt_bias, m_A_log, m_D_skip, m_ssd_norm_w, m_w_out, m_final_norm_w, v_norm_w, v_w_in, v_gate_b, v_sgu_norm_g, v_sgu_norm_b, v_sgu_w, v_sgu_b, v_conv_w, v_conv_b, v_dt_bias, v_A_log, v_D_skip, v_ssd_norm_w, v_w_out, v_final_norm_w):
    w = dict(norm_w=norm_w, w_in=w_in, gate_b=gate_b, sgu_norm_g=sgu_norm_g, sgu_norm_b=sgu_norm_b, sgu_w=sgu_w, sgu_b=sgu_b,
             conv_w=conv_w, conv_b=conv_b, dt_bias=dt_bias, A_log=A_log, D_skip=D_skip, ssd_norm_w=ssd_norm_w, w_out=w_out,
             final_norm_w=final_norm_w)
    m = dict(norm_w=m_norm_w, w_in=m_w_in, gate_b=m_gate_b, sgu_norm_g=m_sgu_norm_g, sgu_norm_b=m_sgu_norm_b, sgu_w=m_sgu_w,
             sgu_b=m_sgu_b, conv_w=m_conv_w, conv_b=m_conv_b, dt_bias=m_dt_bias, A_log=m_A_log, D_skip=m_D_skip,
             ssd_norm_w=m_ssd_norm_w, w_out=m_w_out, final_norm_w=m_final_norm_w)
    v = dict(norm_w=v_norm_w, w_in=v_w_in, gate_b=v_gate_b, sgu_norm_g=v_sgu_norm_g, sgu_norm_b=v_sgu_norm_b, sgu_w=v_sgu_w,
             sgu_b=v_sgu_b, conv_w=v_conv_w, conv_b=v_conv_b, dt_bias=v_dt_bias, A_log=v_A_log, D_skip=v_D_skip,
             ssd_norm_w=v_ssd_norm_w, w_out=v_w_out, final_norm_w=v_final_norm_w)
    me = 4 * lax.axis_index("x") + 2 * lax.axis_index("y") + lax.axis_index("c")
    shard_cw = XBC_W // N_DEV

    tpose = lambda a: jnp.swapaxes(a[0], 0, 1)
    wT = tpose(w_in).astype(BF16)
    first_group = (GROUP - (me * SHARD_IN) % GROUP) % GROUP
    window = lax.dynamic_slice(jnp.pad(wT, ((0, GROUP), (0, 0))), (first_group, 0), (INTERIOR, D))
    wpT, g_out, g_cw, heads, tails = _gather_weights(window, wT[:GROUP], wT[SHARD_IN - GROUP:], w_out[0].astype(BF16),
                                                     conv_w[0], jnp.zeros((W_ROWS - W_IN, D), BF16))
    wpT = _patch_straddlers(wpT, heads, tails)
    wout_full = g_out.reshape(D, D)
    cw_full = jnp.swapaxes(g_cw, 0, 1).reshape(CONV_K, XBC_W)

    flight = {}

    small = [n for n in WEIGHTS if n not in SHARDED and n != 'norm_w']
    early = {}

    def exchange_small(loss_part, grads, dw_out):
        early['packed'], early['offs'] = _pack([grads[n] for n in small] + [loss_part, grads['conv_w']])
        parts = [jnp.broadcast_to(early['packed'][None], (N_DEV,) + early['packed'].shape), dw_out.reshape(N_DEV, D // N_DEV, D)]
        early['sems'], early['rsems'], early['parts'], early['lands'], token = _exchange_start(parts, name="small_start")
        return token

    def exchange(dw_inT_segs):
        parts = [_to_shards(dw_inT_segs)]
        flight['sems'], flight['rsems'], flight['parts'], flight['lands'], token = _exchange_start(parts, name="exchange_start")
        return token

    grad_x, dnorm = _local_step(x[0], loss_target[0], wpT, wout_full, cw_full, w, exchange_small, exchange)
    (_, own_out), (land_small, land_out) = _exchange_wait(
        early['sems'], early['rsems'], early['parts'], early['lands'], grad_x, name="small_wait")
    (own_in,), (land_in,) = _exchange_wait(
        flight['sems'], flight['rsems'], flight['parts'], flight['lands'], grad_x, name="exchange_wait")
    me_arr = jnp.reshape(me, (1,)).astype(jnp.int32)
    res = {}
    res['w_in'] = [jnp.swapaxes(o, 0, 1) for o in _adamw_own(
        me_arr, own_in, land_in, tpose(w_in), tpose(m_w_in), tpose(v_w_in), tr=SHARD_IN, tc=256, name="adamw_w_in")]
    res['w_out'] = _adamw_own(me_arr, own_out, land_out, w_out[0], m_w_out[0], v_w_out[0], tr=128, tc=D, name="adamw_w_out")

    (norm_parts,) = _all_gather([_pack([dnorm])[0]], name="gather_norm")
    norm_outs = _adamw(norm_parts, *[_pack([d['norm_w']])[0] for d in (w, m, v)], tr=norm_parts.shape[1], name="adamw_norm")
    res['norm_w'] = [o.reshape(-1)[:D].reshape(w['norm_w'].shape) for o in norm_outs]

    offs = early['offs']
    gathered = lax.dynamic_update_slice(land_small, early['packed'][None], (me, 0, 0))
    off_loss, off_cw = offs[-2], offs[-1]
    cw_parts = gathered[:, off_cw:, :].reshape(N_DEV, CONV_K, XBC_W)
    cw_parts = lax.dynamic_slice_in_dim(cw_parts, me * shard_cw, shard_cw, axis=2)
    cw_rows = _pack([cw_parts[0]])[0].shape[0]
    cw_parts = jnp.pad(cw_parts.reshape(N_DEV, -1), ((0, 0), (0, cw_rows * LANE - CONV_K * shard_cw))).reshape(N_DEV, cw_rows, LANE)
    parts = jnp.concatenate([gathered[:, :off_cw, :], cw_parts], axis=1)
    zero = jnp.zeros((), F32)
    packs = [_pack([d[n] for n in small] + [zero, d['conv_w']])[0] for d in (w, m, v)]
    outs = _adamw(parts, *packs, tr=parts.shape[1], name="adamw_small")

    def unpack(o, name):
        if name == 'conv_w':
            return o[off_cw:off_cw + cw_rows].reshape(-1)[:CONV_K * shard_cw].reshape(w['conv_w'].shape)
        r0 = offs[small.index(name)]
        n = w[name].size
        return o[r0:r0 + -(-n // PACK_ROW) * 8].reshape(-1)[:n].reshape(w[name].shape)

    for n in small + ['conv_w']:
        res[n] = [unpack(o, n) for o in outs]
    for n in ('w_in', 'w_out'):
        res[n] = [o[None] for o in res[n]]
    loss = outs[0][off_loss, 0]
    return (loss, grad_x[None], *[res[n][0] for n in WEIGHTS], *[res[n][1] for n in WEIGHTS],
            *[res[n][2] for n in WEIGHTS], *[res[n][3] for n in WEIGHTS])
```

```python
import functools

import numpy as np
import jax
import jax.numpy as jnp
from jax import lax
from jax.experimental import pallas as pl
from jax.experimental.pallas import tpu as pltpu

F32 = jnp.float32
BF16 = jnp.bfloat16
HI = lax.Precision.HIGHEST
MESH = pl.DeviceIdType.MESH

D = 2048
EPS = 1e-5
SGU_BLOCK = 128
SGU_GROUPS = 16
CHUNK = 64
HEADS = 32
HEADDIM = 64
SSD_GROUPS = 4
GROUP_W = D // SSD_GROUPS
STATE = 128
CONV_K = 4
XBC_W = D + 2 * SSD_GROUPS * STATE
W_IN = 15392
N_DEV = 8
SHARD_IN = W_IN // N_DEV
ADAM_LR, ADAM_B1, ADAM_B2, ADAM_EPS, ADAM_WD, ADAM_STEP = 0.001, 0.9, 0.999, 1e-08, 0.01, 10

REF_SGU_END = 3 * D
REF_GATE_START = W_IN - 2 * D
LANE = 128
DT_W = LANE
OFF_U, OFF_V, OFF_ZA, OFF_G0, OFF_G1, OFF_ZB = (i * D for i in range(6))
OFF_XBC = OFF_ZB + D
OFF_DT = OFF_XBC + XBC_W
SEG_SGU = (OFF_U, 3 * D)
SEG_GATE = (OFF_G0, 2 * D)
SEG_SSD = (OFF_ZB, D + XBC_W + DT_W)
WP = SEG_SSD[0] + SEG_SSD[1]
SSD_PAD_W = 3 * D
VMEM_BYTES = 64 * 1024 * 1024
VMEM_LIMIT = VMEM_BYTES - 8 * 1024 * 1024


def _cp(sem=None, vmem=VMEM_LIMIT):
    return pltpu.CompilerParams(dimension_semantics=sem, vmem_limit_bytes=vmem)


def _sigmoid(x):
    return 1.0 / (1.0 + jnp.exp(-x))


def _softplus(x):
    return jnp.maximum(x, 0.0) + jnp.log(1.0 + jnp.exp(-jnp.abs(x)))


def _dot(a, b, precision=None):
    return jnp.dot(a, b, preferred_element_type=F32, precision=precision)


def _dot_nt(a, b, precision=None):
    return lax.dot_general(a, b, (((1,), (1,)), ((), ())), preferred_element_type=F32, precision=precision)


def _dot_tn(a, b, precision=None):
    return lax.dot_general(a, b, (((0,), (0,)), ((), ())), preferred_element_type=F32, precision=precision)


def _split3(a):
    hi = a.astype(BF16)
    r = a - hi.astype(F32)
    mid = r.astype(BF16)
    return hi, mid, (r - mid.astype(F32)).astype(BF16)


def _sel_right(a, sel01):
    m = a.shape[0]
    r = _dot(jnp.concatenate(_split3(a), axis=0), sel01)
    return (r[0:m] + r[m:2 * m]) + r[2 * m:3 * m]


def _sel_right_k(a, sel01_x3):
    return _dot(jnp.concatenate(_split3(a), axis=1), sel01_x3)


def _sel_left(sel01, a):
    n = a.shape[1]
    r = _dot(sel01, jnp.concatenate(_split3(a), axis=1))
    return (r[:, 0:n] + r[:, n:2 * n]) + r[:, 2 * n:3 * n]


def _matmul(a, b, *, trans_a=False, trans_b=False, b_koff=0, out_dtype=F32, tm, tn, tk, add=None, after=None, name):
    K, M = a.shape if trans_a else a.shape[::-1]
    N = b.shape[0] if trans_b else b.shape[1]
    assert M % tm == 0 and N % tn == 0 and K % tk == 0 and not (trans_a and trans_b), (name, M, N, K, tm, tn, tk)
    nk = K // tk

    def body(*refs):
        a_ref, b_ref = refs[:2]
        add_ref = refs[2] if add is not None else None
        o_ref, acc_ref = refs[-2:]
        k = pl.program_id(2)
        if trans_a:
            part = _dot_tn(a_ref[...], b_ref[...])
        else:
            part = _dot_nt(a_ref[...], b_ref[...]) if trans_b else _dot(a_ref[...], b_ref[...])

        def result(r):
            if add_ref is not None:
                r = r + add_ref[...]
            return r.astype(out_dtype)

        if nk == 1:
            o_ref[...] = result(part)
        else:
            @pl.when(k == 0)
            def _():
                acc_ref[...] = part

            @pl.when(jnp.logical_and(k > 0, k < nk - 1))
            def _():
                acc_ref[...] += part

            @pl.when(k == nk - 1)
            def _():
                o_ref[...] = result(acc_ref[...] + part)

    in_specs = [pl.BlockSpec((tk, tm), lambda i, j, k: (k, i)) if trans_a else pl.BlockSpec((tm, tk), lambda i, j, k: (i, k)),
                pl.BlockSpec((tn, tk), lambda i, j, k: (j, k)) if trans_b else pl.BlockSpec((tk, tn), lambda i, j, k: (k + b_koff, j))]
    args = [a, b]
    if add is not None:
        in_specs.append(pl.BlockSpec((tm, tn), lambda i, j, k: (i, j)))
        args.append(add)
    if after is not None:
        in_specs.append(pl.BlockSpec(memory_space=pl.ANY))
        args.append(after)
    return pl.pallas_call(
        body, name=name, grid=(M // tm, N // tn, nk), in_specs=in_specs,
        out_specs=pl.BlockSpec((tm, tn), lambda i, j, k: (i, j)),
        out_shape=jax.ShapeDtypeStruct((M, N), out_dtype),
        scratch_shapes=[pltpu.VMEM((tm, tn), F32)],
        compiler_params=_cp(("parallel", "parallel", "arbitrary")),
    )(*args)


def _in_proj(x, w, wpT, *, tm, tn):
    S = x.shape[0]
    N = wpT.shape[0]
    assert S % tm == 0 and N % tn == 0, (S, N, tm, tn)

    def body(x_ref, w_ref, b_ref, xn_ref, o_ref, xs_ref):
        @pl.when(pl.program_id(1) == 0)
        def _():
            xv = x_ref[...]
            r = lax.rsqrt(jnp.mean(xv * xv, axis=-1, keepdims=True) + EPS)
            xs = (xv * r * w_ref[...]).astype(BF16)
            xs_ref[...] = xs
            xn_ref[...] = xs

        o_ref[...] = _dot_nt(xs_ref[...], b_ref[...]).astype(BF16)

    return pl.pallas_call(
        body, name="in_proj", grid=(S // tm, N // tn),
        in_specs=[pl.BlockSpec((tm, D), lambda i, j: (i, 0)), pl.BlockSpec((1, D), lambda i, j: (0, 0)),
                  pl.BlockSpec((tn, D), lambda i, j: (j, 0))],
        out_specs=[pl.BlockSpec((tm, D), lambda i, j: (i, 0)), pl.BlockSpec((tm, tn), lambda i, j: (i, j))],
        out_shape=[jax.ShapeDtypeStruct((S, D), BF16), jax.ShapeDtypeStruct((S, N), BF16)],
        scratch_shapes=[pltpu.VMEM((tm, D), BF16)],
        compiler_params=_cp(("parallel", "arbitrary")),
    )(x, w, wpT)


def _norm_bwd(x, w, dxn, dh, *, tm):
    S = x.shape[0]

    def body(x_ref, w_ref, dxn_ref, dh_ref, gx_ref, dw_ref):
        xv = x_ref[...]
        r = lax.rsqrt(jnp.mean(xv * xv, axis=-1, keepdims=True) + EPS)
        xh = xv * r
        dxn_v = dxn_ref[...]
        dxh = dxn_v * w_ref[...]
        gx_ref[...] = dh_ref[...] + r * (dxh - xh * jnp.mean(dxh * xh, axis=-1, keepdims=True))

        @pl.when(pl.program_id(0) == 0)
        def _():
            dw_ref[...] = jnp.zeros_like(dw_ref)

        dw_ref[0:1, :] += jnp.sum(dxn_v * xh, axis=0, keepdims=True)

    row = pl.BlockSpec((tm, D), lambda i: (i, 0))
    return pl.pallas_call(
        body, name="norm_bwd", grid=(S // tm,),
        in_specs=[row, pl.BlockSpec((1, D), lambda i: (0, 0)), row, row],
        out_specs=[row, pl.BlockSpec((8, D), lambda i: (0, 0))],
        out_shape=[jax.ShapeDtypeStruct((S, D), F32), jax.ShapeDtypeStruct((8, D), F32)],
        compiler_params=_cp(("arbitrary",)),
    )(x, w, dxn, dh)


def _sgu_core(u_ref, v_ref, z_ref, g_ref, b_ref, wm_ref, bias_ref, vnb_ref, mixed_ref, tm):
    v = v_ref[...].astype(F32)
    mu = jnp.mean(v, axis=-1, keepdims=True)
    vc = v - mu
    rs = lax.rsqrt(jnp.mean(vc * vc, axis=-1, keepdims=True) + EPS)
    vh = vc * rs
    vnb_ref[...] = (vh * g_ref[...] + b_ref[...]).astype(BF16)
    for blk in range(tm // SGU_BLOCK):
        rows = pl.ds(blk * SGU_BLOCK, SGU_BLOCK)
        for gi in range(SGU_GROUPS):
            cols = pl.ds(gi * LANE, LANE)
            mixed_ref[rows, cols] = _dot(wm_ref[gi], vnb_ref[rows, cols]) + bias_ref[:, cols]
    return vh, rs


def _sgu_fwd(proj, g, b, wm, bias_full, *, tm):
    S = proj.shape[0]

    def body(u_ref, v_ref, z_ref, g_ref, b_ref, wm_ref, bias_ref, y_ref, vnb_ref, mixed_ref):
        _sgu_core(u_ref, v_ref, z_ref, g_ref, b_ref, wm_ref, bias_ref, vnb_ref, mixed_ref, tm)
        z = z_ref[...].astype(F32)
        y_ref[...] = (u_ref[...].astype(F32) * mixed_ref[...] * (z * _sigmoid(z))).astype(BF16)

    seg = lambda off: pl.BlockSpec((tm, D), lambda i: (i, off // D))
    full = lambda a: pl.BlockSpec(a.shape, lambda i: (0,) * a.ndim)
    return pl.pallas_call(
        body, name="sgu_fwd", grid=(S // tm,),
        in_specs=[seg(OFF_U), seg(OFF_V), seg(OFF_ZA), full(g), full(b), full(wm), full(bias_full)],
        out_specs=pl.BlockSpec((tm, D), lambda i: (i, 0)),
        out_shape=jax.ShapeDtypeStruct((S, D), BF16),
        scratch_shapes=[pltpu.VMEM((tm, D), BF16), pltpu.VMEM((tm, D), F32)],
        compiler_params=_cp(("parallel",)),
    )(proj, proj, proj, g, b, wm, bias_full)


def _sgu_bwd(proj, dy, g, b, wm, wmT, bias_full, mask, sel, *, tm):
    S = proj.shape[0]
    nsteps = S // tm

    def body(u_ref, v_ref, z_ref, dy_ref, g_ref, b_ref, wm_ref, wmT_ref, bias_ref, mask_ref, sel_ref,
             dp_ref, dws_ref, dbs_ref, dg_ref, db_ref, vnb_ref, mixed_ref, dmb_ref, dvn_ref, dbias_ref):
        i = pl.program_id(0)

        @pl.when(i == 0)
        def _():
            dws_ref[...] = jnp.zeros_like(dws_ref)
            dg_ref[...] = jnp.zeros_like(dg_ref)
            db_ref[...] = jnp.zeros_like(db_ref)
            dbias_ref[...] = jnp.zeros_like(dbias_ref)

        vh, rs = _sgu_core(u_ref, v_ref, z_ref, g_ref, b_ref, wm_ref, bias_ref, vnb_ref, mixed_ref, tm)
        u = u_ref[...].astype(F32)
        z = z_ref[...].astype(F32)
        dy_v = dy_ref[...].astype(F32)
        mixed = mixed_ref[...]
        sg = _sigmoid(z)
        sz = z * sg
        dp_ref[:, 0:D] = (dy_v * mixed * sz).astype(BF16)
        dp_ref[:, 2 * D:3 * D] = (dy_v * u * mixed * (sg * (1.0 + z * (1.0 - sg)))).astype(BF16)
        dmixed = dy_v * u * sz
        dmb_ref[...] = dmixed.astype(BF16)
        for blk in range(tm // SGU_BLOCK):
            dbias_ref[...] += dmixed[blk * SGU_BLOCK:(blk + 1) * SGU_BLOCK, :]
        for blk in range(tm // SGU_BLOCK):
            rows = pl.ds(blk * SGU_BLOCK, SGU_BLOCK)
            for gi in range(SGU_GROUPS):
                cols = pl.ds(gi * LANE, LANE)
                dm = dmb_ref[rows, cols]
                dvn_ref[rows, cols] = _dot(wmT_ref[gi], dm)
                dws_ref[gi] += _dot_nt(dm, vnb_ref[rows, cols])
        dvn = dvn_ref[...]
        dg_ref[0:1, :] += jnp.sum(dvn * vh, axis=0, keepdims=True)
        db_ref[0:1, :] += jnp.sum(dvn, axis=0, keepdims=True)
        dvh = dvn * g_ref[...]
        dv = rs * (dvh - jnp.mean(dvh, axis=-1, keepdims=True) - vh * jnp.mean(dvh * vh, axis=-1, keepdims=True))
        dp_ref[:, D:2 * D] = dv.astype(BF16)

        @pl.when(i == nsteps - 1)
        def _():
            for gi in range(SGU_GROUPS):
                dws_ref[gi] = dws_ref[gi] * mask_ref[...]
            dbs_ref[...] = _dot(dbias_ref[...], sel_ref[...], precision=HI)

    seg = lambda off: pl.BlockSpec((tm, D), lambda i: (i, off // D))
    full = lambda a: pl.BlockSpec(a.shape, lambda i: (0,) * a.ndim)
    return pl.pallas_call(
        body, name="sgu_bwd", grid=(nsteps,),
        in_specs=[seg(OFF_U), seg(OFF_V), seg(OFF_ZA), pl.BlockSpec((tm, D), lambda i: (i, 0)),
                  full(g), full(b), full(wm), full(wmT), full(bias_full), full(mask), full(sel)],
        out_specs=[pl.BlockSpec((tm, 3 * D), lambda i: (i, 0)),
                   pl.BlockSpec((SGU_GROUPS, SGU_BLOCK, SGU_BLOCK), lambda i: (0, 0, 0)),
                   pl.BlockSpec((SGU_BLOCK, LANE), lambda i: (0, 0)),
                   pl.BlockSpec((8, D), lambda i: (0, 0)), pl.BlockSpec((8, D), lambda i: (0, 0))],
        out_shape=[jax.ShapeDtypeStruct((S, 3 * D), BF16),
                   jax.ShapeDtypeStruct((SGU_GROUPS, SGU_BLOCK, SGU_BLOCK), F32),
                   jax.ShapeDtypeStruct((SGU_BLOCK, LANE), F32),
                   jax.ShapeDtypeStruct((8, D), F32), jax.ShapeDtypeStruct((8, D), F32)],
        scratch_shapes=[pltpu.VMEM((tm, D), BF16), pltpu.VMEM((tm, D), F32), pltpu.VMEM((tm, D), BF16),
                        pltpu.VMEM((tm, D), F32), pltpu.VMEM((SGU_BLOCK, D), F32)],
        compiler_params=_cp(("arbitrary",)),
    )(proj, proj, proj, dy, g, b, wm, wmT, bias_full, mask, sel)


SSD_T = 2 * CHUNK
HALO = 8
HALO_BLK = 16


def _pair_masks():
    row = lax.broadcasted_iota(jnp.int32, (CHUNK, LANE), 0)
    lane = lax.broadcasted_iota(jnp.int32, (CHUNK, LANE), 1)
    pos = jnp.where(lane >= CHUNK, lane - CHUNK, lane)
    diag = (row == pos).astype(F32)
    causal = row >= pos
    lo = (lane < CHUNK).astype(F32)
    return diag, causal, lo, 1.0 - lo


def _ssd_chunk_fwd(c, ext_ref, shift_ref, dt_ref, cw_ref, cb_ref, dtb_ref, alog_ref, tri_ref, exp_ref):
    r0 = c * CHUNK
    win = ext_ref[pl.ds(r0, HALO_BLK + CHUNK), :]
    sh = _dot(shift_ref[...], win)
    taps = [sh[k * CHUNK:(k + 1) * CHUNK] for k in range(CONV_K - 1)] + [win[HALO_BLK:].astype(F32)]
    pre = cb_ref[...] + sum(cw_ref[k:k + 1, :] * taps[k] for k in range(CONV_K))
    sg = _sigmoid(pre)
    xc = pre * sg
    dtr = dt_ref[pl.ds(r0, CHUNK), :].astype(F32) + dtb_ref[...]
    dtv = _softplus(dtr)
    A = -jnp.exp(alog_ref[...])
    acs = _sel_left(tri_ref[...], dtv * A)
    both = _sel_right_k(jnp.concatenate([acs, dtv], axis=0), exp_ref[...])
    E, dtE = both[0:CHUNK], both[CHUNK:2 * CHUNK]
    return dict(taps=taps, pre=pre, sg=sg, xc=xc, dtr=dtr, dtv=dtv, A=A, E=E, dtE=dtE)


def _ssd_fwd(proj, conv_w, conv_b, dtb_p, alog_p, d_exp, norm_w, tri, expand, shift):
    S = proj.shape[0]
    T = SSD_T
    nsteps = S // T
    ncl = T // CHUNK

    def body(zb_ref, xbc_ref, halo_ref, dt_ref, cw_ref, cb_ref, dtb_ref, alog_ref, dexp_ref, nw_ref, tri_ref, exp_ref, shift_ref,
             y_ref, yb_ref, st_ref, ht_ref, ext_ref):
        i = pl.program_id(0)

        @pl.when(i == 0)
        def _():
            ht_ref[...] = jnp.zeros_like(ht_ref)
            ext_ref[0:HALO_BLK, :] = jnp.zeros((HALO_BLK, XBC_W), BF16)

        @pl.when(i > 0)
        def _():
            ext_ref[0:HALO_BLK, :] = halo_ref[...]

        ext_ref[HALO_BLK:HALO_BLK + T, :] = xbc_ref[...]
        diag, causal, lo, hi = _pair_masks()
        for c in range(ncl):
            q = _ssd_chunk_fwd(c, ext_ref, shift_ref, dt_ref, cw_ref, cb_ref, dtb_ref, alog_ref, tri_ref, exp_ref)
            rows = pl.ds(c * CHUNK, CHUNK)
            xc, E, dtE = q["xc"], q["E"], q["dtE"]
            xs = xc[:, 0:D]
            total = E[CHUNK - 1:CHUNK, :]
            x_dt = xs * dtE
            eE = jnp.exp(E)
            xw = x_dt * jnp.exp(total - E)
            st_ref[c] = ht_ref[...]
            for g in range(SSD_GROUPS):
                gc = slice(g * GROUP_W, (g + 1) * GROUP_W)
                Bg = xc[:, D + g * STATE:D + (g + 1) * STATE].astype(BF16)
                Cg = xc[:, D + SSD_GROUPS * STATE + g * STATE:D + SSD_GROUPS * STATE + (g + 1) * STATE].astype(BF16)
                cb2 = _dot_nt(Cg, jnp.concatenate([Bg, Bg], axis=0))
                htg = ht_ref[:, gc]
                y_ref[rows, gc] = eE[:, gc] * _dot(Cg, htg.astype(BF16)) + xs[:, gc] * dexp_ref[:, gc]
                for jj in range(GROUP_W // LANE):
                    pc = slice(g * GROUP_W + jj * LANE, g * GROUP_W + (jj + 1) * LANE)
                    Ej = E[:, pc]
                    e2 = jnp.sum(Ej * diag, axis=0, keepdims=True)
                    Mp = cb2 * jnp.exp(jnp.where(causal, Ej - e2, -1e30))
                    xj = x_dt[:, pc]
                    xbd = jnp.concatenate([xj * lo, xj * hi], axis=0).astype(BF16)
                    y_ref[rows, pc] += _dot(Mp.astype(BF16), xbd)
                ht_ref[:, gc] = jnp.exp(total[:, gc]) * htg + _dot_tn(Bg, xw[:, gc].astype(BF16))
            zb = zb_ref[rows, :].astype(F32)
            hh = y_ref[rows, :] * (zb * _sigmoid(zb))
            for g in range(SSD_GROUPS):
                gc = slice(g * GROUP_W, (g + 1) * GROUP_W)
                hg = hh[:, gc]
                r = lax.rsqrt(jnp.mean(hg * hg, axis=-1, keepdims=True) + EPS)
                yb_ref[rows, gc] = (hg * r * nw_ref[:, gc]).astype(BF16)

    full = lambda a: pl.BlockSpec(a.shape, lambda i: (0,) * a.ndim)
    hb = T // HALO_BLK
    return pl.pallas_call(
        body, name="ssd_fwd", grid=(nsteps,),
        in_specs=[pl.BlockSpec((T, D), lambda i: (i, OFF_ZB // D)),
                  pl.BlockSpec((T, XBC_W), lambda i: (i, OFF_XBC // XBC_W)),
                  pl.BlockSpec((HALO_BLK, XBC_W), lambda i: (jnp.maximum(i * hb - 1, 0), OFF_XBC // XBC_W)),
                  pl.BlockSpec((T, DT_W), lambda i: (i, OFF_DT // DT_W)),
                  full(conv_w), full(conv_b), full(dtb_p), full(alog_p), full(d_exp), full(norm_w), full(tri), full(expand),
                  full(shift)],
        out_specs=[pl.BlockSpec((T, D), lambda i: (i, 0)), pl.BlockSpec((T, D), lambda i: (i, 0)),
                   pl.BlockSpec((ncl, STATE, D), lambda i: (i, 0, 0))],
        out_shape=[jax.ShapeDtypeStruct((S, D), F32), jax.ShapeDtypeStruct((S, D), BF16),
                   jax.ShapeDtypeStruct((S // CHUNK, STATE, D), F32)],
        scratch_shapes=[pltpu.VMEM((STATE, D), F32), pltpu.VMEM((HALO_BLK + T, XBC_W), BF16)],
        compiler_params=_cp(("arbitrary",)),
    )(proj, proj, proj, proj, conv_w, conv_b, dtb_p, alog_p, d_exp, norm_w, tri, expand, shift)


def _ssd_bwd(proj, dyb, y, states, conv_w, conv_b, dtb_p, alog_p, d_exp, norm_w, tri, triT, expand, expandT, shift):
    S = proj.shape[0]
    T = SSD_T
    nsteps = S // T
    ncl = T // CHUNK
    SSD_W = SSD_PAD_W

    def body(zb_ref, xbc_ref, halo_ref, dt_ref, dyb_ref, y_ref, st_ref, cw_ref, cb_ref, dtb_ref, alog_ref, dexp_ref, nw_ref,
             tri_ref, triT_ref, exp_ref, expT_ref, shift_ref,
             dp_ref, dcw_ref, dcb_ref, ddtb_ref, dalog_ref, dD_ref, dnw_ref,
             dht_ref, ext_ref, dpre_ref, dy_s, dE_s, dxdt_s, dxc_s, dDacc_ref, dAacc_ref):
        i = pl.program_id(0)

        @pl.when(i == 0)
        def _():
            for r in (dht_ref, dcw_ref, dcb_ref, ddtb_ref, dnw_ref, dDacc_ref, dAacc_ref):
                r[...] = jnp.zeros_like(r)
            dpre_ref[T:T + HALO_BLK, :] = jnp.zeros((HALO_BLK, XBC_W), F32)

        @pl.when(i == nsteps - 1)
        def _():
            ext_ref[0:HALO_BLK, :] = jnp.zeros((HALO_BLK, XBC_W), BF16)

        @pl.when(i < nsteps - 1)
        def _():
            ext_ref[0:HALO_BLK, :] = halo_ref[...]

        ext_ref[HALO_BLK:HALO_BLK + T, :] = xbc_ref[...]
        diag, causal, lo, hi = _pair_masks()
        last_row = (lax.broadcasted_iota(jnp.int32, (CHUNK, 1), 0) == CHUNK - 1).astype(F32)
        for c in reversed(range(ncl)):
            q = _ssd_chunk_fwd(c, ext_ref, shift_ref, dt_ref, cw_ref, cb_ref, dtb_ref, alog_ref, tri_ref, exp_ref)
            rows = pl.ds(c * CHUNK, CHUNK)
            pre, sg, xc, dtr, dtv, A, E, dtE = (q[k] for k in ("pre", "sg", "xc", "dtr", "dtv", "A", "E", "dtE"))
            xs = xc[:, 0:D]
            total = E[CHUNK - 1:CHUNK, :]
            x_dt = xs * dtE
            eE = jnp.exp(E)
            wdec = jnp.exp(total - E)
            zb = zb_ref[rows, :].astype(F32)
            yv = y_ref[rows, :]
            sgz = _sigmoid(zb)
            sz = zb * sgz
            hh = yv * sz
            for g in range(SSD_GROUPS):
                gc = slice(g * GROUP_W, (g + 1) * GROUP_W)
                hg = hh[:, gc]
                r = lax.rsqrt(jnp.mean(hg * hg, axis=-1, keepdims=True) + EPS)
                dyb_g = dyb_ref[rows, gc].astype(F32)
                dn = dyb_g * nw_ref[:, gc]
                dnw_ref[0:1, gc] += jnp.sum(dyb_g * hg * r, axis=0, keepdims=True)
                dy_s[:, gc] = r * dn - hg * (r * r * r) * jnp.mean(dn * hg, axis=-1, keepdims=True)
            dhh = dy_s[...]
            dp_ref[rows, 0:D] = (dhh * yv * (sgz * (1.0 + zb * (1.0 - sgz)))).astype(BF16)
            dy = dhh * sz
            dy_s[...] = dy
            dDacc_ref[0:1, :] += jnp.sum(dy * xs, axis=0, keepdims=True)
            dxc_s[:, 0:D] = dy * dexp_ref[...]
            for g in range(SSD_GROUPS):
                gc = slice(g * GROUP_W, (g + 1) * GROUP_W)
                bcol = slice(D + g * STATE, D + (g + 1) * STATE)
                ccol = slice(D + SSD_GROUPS * STATE + g * STATE, D + SSD_GROUPS * STATE + (g + 1) * STATE)
                Bg = xc[:, bcol].astype(BF16)
                Cg = xc[:, ccol].astype(BF16)
                B2 = jnp.concatenate([Bg, Bg], axis=0)
                cb2 = _dot_nt(Cg, B2)
                htg = st_ref[c, :, gc]
                htb = htg.astype(BF16)
                dhn = dht_ref[:, gc]
                dhnb = dhn.astype(BF16)
                dyg = dy[:, gc]
                eEg = eE[:, gc]
                wg = wdec[:, gc]
                xdg = x_dt[:, gc]
                CH = _dot(Cg, htb)
                dCHb = (dyg * eEg).astype(BF16)
                dC = _dot_nt(dCHb, htb)
                dl = jnp.exp(total[:, gc])
                dht_prev = _dot_tn(Cg, dCHb) + dl * dhn
                dtot = jnp.sum(dhn * htg, axis=0, keepdims=True) * dl
                dxw = _dot(Bg, dhnb)
                dB = _dot_nt((xdg * wg).astype(BF16), dhnb)
                dwd = dxw * xdg * wg
                dtot = dtot + jnp.sum(dwd, axis=0, keepdims=True)
                dE_s[:, gc] = dyg * eEg * CH - dwd + last_row * dtot
                dxdt_s[:, gc] = dxw * wg
                dcb2 = jnp.zeros((CHUNK, LANE), F32)
                for jj in range(GROUP_W // LANE):
                    pc = slice(g * GROUP_W + jj * LANE, g * GROUP_W + (jj + 1) * LANE)
                    Ej = E[:, pc]
                    e2 = jnp.sum(Ej * diag, axis=0, keepdims=True)
                    Lp = jnp.exp(jnp.where(causal, Ej - e2, -1e30))
                    Mp = cb2 * Lp
                    xj = x_dt[:, pc]
                    xbd = jnp.concatenate([xj * lo, xj * hi], axis=0).astype(BF16)
                    dyj = dy[:, pc].astype(BF16)
                    dMp = _dot_nt(dyj, xbd)
                    dxbd = _dot_tn(Mp.astype(BF16), dyj)
                    dxdt_s[:, pc] += dxbd[0:CHUNK, :] * lo + dxbd[CHUNK:2 * CHUNK, :] * hi
                    dcb2 = dcb2 + dMp * Lp
                    dseg = dMp * Mp
                    dE_s[:, pc] += dseg - diag * jnp.sum(dseg, axis=0, keepdims=True)
                dcb2b = dcb2.astype(BF16)
                dC = dC + _dot(dcb2b, B2)
                dB2 = _dot_tn(dcb2b, Cg)
                dB = dB + dB2[0:CHUNK, :] + dB2[CHUNK:2 * CHUNK, :]
                dxc_s[:, bcol] = dB
                dxc_s[:, ccol] = dC
                dht_ref[:, gc] = dht_prev
            dx_dt = dxdt_s[...]
            dxc_s[:, 0:D] += dx_dt * dtE
            red = _sel_right(jnp.concatenate([dE_s[...], dx_dt * xs], axis=0), expT_ref[...])
            da = _sel_left(triT_ref[...], red[0:CHUNK, :])
            ddtv = red[CHUNK:2 * CHUNK, :] + da * A
            dAacc_ref[0:1, :] += jnp.sum(da * dtv, axis=0, keepdims=True)
            ddtr = ddtv * _sigmoid(dtr)
            ddtb_ref[0:1, :] += jnp.sum(ddtr, axis=0, keepdims=True)
            dp_ref[rows, D + XBC_W:D + XBC_W + DT_W] = ddtr.astype(BF16)
            dpre = dxc_s[...] * (sg * (1.0 + pre * (1.0 - sg)))
            dpre_ref[rows, :] = dpre
            dcb_ref[0:1, :] += jnp.sum(dpre, axis=0, keepdims=True)
            for k in range(CONV_K):
                dcw_ref[k:k + 1, :] += jnp.sum(dpre * q["taps"][k], axis=0, keepdims=True)
        dxbc = jnp.zeros((T, XBC_W), F32)
        for k in range(CONV_K):
            dxbc = dxbc + cw_ref[k:k + 1, :] * dpre_ref[pl.ds(CONV_K - 1 - k, T), :]
        dp_ref[:, D:D + XBC_W] = dxbc.astype(BF16)
        dp_ref[:, SEG_SSD[1]:SSD_W] = jnp.zeros((T, SSD_W - SEG_SSD[1]), BF16)
        dpre_ref[T:T + HALO, :] = dpre_ref[0:HALO, :]

        @pl.when(i == nsteps - 1)
        def _():
            dalog_ref[...] = dAacc_ref[...] * (-jnp.exp(alog_ref[...]))
            dD_ref[...] = _dot(dDacc_ref[...], expT_ref[...].astype(F32), precision=HI)

    full = lambda a: pl.BlockSpec(a.shape, lambda i: (0,) * a.ndim)
    hb = T // HALO_BLK
    rev = lambda i: nsteps - 1 - i
    acc = lambda w: pl.BlockSpec((8, w), lambda i: (0, 0))
    return pl.pallas_call(
        body, name="ssd_bwd", grid=(nsteps,),
        in_specs=[pl.BlockSpec((T, D), lambda i: (rev(i), OFF_ZB // D)),
                  pl.BlockSpec((T, XBC_W), lambda i: (rev(i), OFF_XBC // XBC_W)),
                  pl.BlockSpec((HALO_BLK, XBC_W), lambda i: (jnp.maximum(rev(i) * hb - 1, 0), OFF_XBC // XBC_W)),
                  pl.BlockSpec((T, DT_W), lambda i: (rev(i), OFF_DT // DT_W)),
                  pl.BlockSpec((T, D), lambda i: (rev(i), 0)), pl.BlockSpec((T, D), lambda i: (rev(i), 0)),
                  pl.BlockSpec((ncl, STATE, D), lambda i: (rev(i), 0, 0)),
                  full(conv_w), full(conv_b), full(dtb_p), full(alog_p), full(d_exp), full(norm_w),
                  full(tri), full(triT), full(expand), full(expandT), full(shift)],
        out_specs=[pl.BlockSpec((T, SSD_W), lambda i: (rev(i), 0)),
                   acc(XBC_W), acc(XBC_W), acc(DT_W), acc(DT_W), acc(DT_W), acc(D)],
        out_shape=[jax.ShapeDtypeStruct((S, SSD_W), BF16),
                   jax.ShapeDtypeStruct((8, XBC_W), F32), jax.ShapeDtypeStruct((8, XBC_W), F32),
                   jax.ShapeDtypeStruct((8, DT_W), F32), jax.ShapeDtypeStruct((8, DT_W), F32),
                   jax.ShapeDtypeStruct((8, DT_W), F32), jax.ShapeDtypeStruct((8, D), F32)],
        scratch_shapes=[pltpu.VMEM((STATE, D), F32), pltpu.VMEM((HALO_BLK + T, XBC_W), BF16), pltpu.VMEM((T + HALO_BLK, XBC_W), F32),
                        pltpu.VMEM((CHUNK, D), F32), pltpu.VMEM((CHUNK, D), F32), pltpu.VMEM((CHUNK, D), F32),
                        pltpu.VMEM((CHUNK, XBC_W), F32), pltpu.VMEM((8, D), F32), pltpu.VMEM((8, DT_W), F32)],
        compiler_params=_cp(("arbitrary",)),
    )(proj, proj, proj, proj, dyb, y, states, conv_w, conv_b, dtb_p, alog_p, d_exp, norm_w, tri, triT, expand, expandT, shift)


def _head(x, ya, yb, proj, target, gate_b, wout, fw, *, tm):
    S = x.shape[0]

    def body(x_ref, ya_ref, yb_ref, gl0_ref, gl1_ref, t_ref, gb_ref, w_ref, fw_ref,
             dh_ref, dhb_ref, mb_ref, dya_ref, dyb_ref, dgl_ref, loss_ref, dfw_ref, dgb_ref):
        @pl.when(pl.program_id(0) == 0)
        def _():
            loss_ref[...] = jnp.zeros_like(loss_ref)
            dfw_ref[...] = jnp.zeros_like(dfw_ref)
            dgb_ref[...] = jnp.zeros_like(dgb_ref)

        ya_v = ya_ref[...].astype(F32)
        yb_v = yb_ref[...].astype(F32)
        g0 = _sigmoid(gl0_ref[...].astype(F32) + gb_ref[:, 0:D])
        g1 = _sigmoid(gl1_ref[...].astype(F32) + gb_ref[:, D:2 * D])
        mb = (g0 * ya_v + g1 * yb_v).astype(BF16)
        mb_ref[...] = mb
        h = x_ref[...] + _dot(mb, w_ref[...])
        r = lax.rsqrt(jnp.mean(h * h, axis=-1, keepdims=True) + EPS)
        hn = h * r
        err = hn * fw_ref[...] - t_ref[...]
        loss_ref[...] += 0.5 * jnp.sum(jnp.mean(err * err, axis=-1, keepdims=True))
        dyf = err * (1.0 / D)
        dfw_ref[0:1, :] += jnp.sum(dyf * hn, axis=0, keepdims=True)
        dhn = dyf * fw_ref[...]
        dh = r * (dhn - hn * jnp.mean(dhn * hn, axis=-1, keepdims=True))
        dh_ref[...] = dh
        dhb = dh.astype(BF16)
        dhb_ref[...] = dhb
        dm = _dot_nt(dhb, w_ref[...])
        dya_ref[...] = (dm * g0).astype(BF16)
        dyb_ref[...] = (dm * g1).astype(BF16)
        dgl0 = dm * ya_v * g0 * (1.0 - g0)
        dgl1 = dm * yb_v * g1 * (1.0 - g1)
        dgl_ref[:, 0:D] = dgl0.astype(BF16)
        dgl_ref[:, D:2 * D] = dgl1.astype(BF16)
        dgb_ref[0:1, 0:D] += jnp.sum(dgl0, axis=0, keepdims=True)
        dgb_ref[0:1, D:2 * D] += jnp.sum(dgl1, axis=0, keepdims=True)

    row = pl.BlockSpec((tm, D), lambda i: (i, 0))
    seg = lambda off: pl.BlockSpec((tm, D), lambda i: (i, off // D))
    full = lambda a: pl.BlockSpec(a.shape, lambda i: (0,) * a.ndim)
    acc = lambda w: pl.BlockSpec((8, w), lambda i: (0, 0))
    return pl.pallas_call(
        body, name="head", grid=(S // tm,),
        in_specs=[row, row, row, seg(OFF_G0), seg(OFF_G1), row, full(gate_b), full(wout), full(fw)],
        out_specs=[row, row, row, row, row, pl.BlockSpec((tm, 2 * D), lambda i: (i, 0)), acc(LANE), acc(D), acc(2 * D)],
        out_shape=[jax.ShapeDtypeStruct((S, D), F32), jax.ShapeDtypeStruct((S, D), BF16), jax.ShapeDtypeStruct((S, D), BF16),
                   jax.ShapeDtypeStruct((S, D), BF16), jax.ShapeDtypeStruct((S, D), BF16), jax.ShapeDtypeStruct((S, 2 * D), BF16),
                   jax.ShapeDtypeStruct((8, LANE), F32), jax.ShapeDtypeStruct((8, D), F32), jax.ShapeDtypeStruct((8, 2 * D), F32)],
        compiler_params=_cp(("arbitrary",)),
    )(x, ya, yb, proj, proj, target, gate_b, wout, fw)


def _adam_update(g, w_ref, m_ref, v_ref, g_ref, d_ref, m2_ref, v2_ref):
    m2 = ADAM_B1 * m_ref[...] + (1.0 - ADAM_B1) * g
    v2 = ADAM_B2 * v_ref[...] + (1.0 - ADAM_B2) * (g * g)
    m_hat = m2 / (1.0 - ADAM_B1 ** ADAM_STEP)
    v_hat = v2 / (1.0 - ADAM_B2 ** ADAM_STEP)
    g_ref[...] = g
    d_ref[...] = -ADAM_LR * (m_hat / (jnp.sqrt(v_hat) + ADAM_EPS) + ADAM_WD * w_ref[...])
    m2_ref[...] = m2
    v2_ref[...] = v2


def _adamw_own(me, own, landed, w, m, v, *, tr, tc, name):
    _, R, C = landed.shape
    assert R % tr == 0 and C % tc == 0, (name, R, C, tr, tc)

    def body(me_ref, own_ref, p_ref, w_ref, m_ref, v_ref, g_ref, d_ref, m2_ref, v2_ref):
        mine = own_ref[0].astype(F32)
        g = jnp.where(me_ref[0] == 0, mine, p_ref[0].astype(F32))
        for k in range(1, N_DEV):
            g = g + jnp.where(me_ref[0] == k, mine, p_ref[k].astype(F32))
        _adam_update(g, w_ref, m_ref, v_ref, g_ref, d_ref, m2_ref, v2_ref)

    tile = pl.BlockSpec((tr, tc), lambda i, j, me_ref: (i, j))
    return pl.pallas_call(
        body, name=name,
        grid_spec=pltpu.PrefetchScalarGridSpec(
            num_scalar_prefetch=1, grid=(R // tr, C // tc),
            in_specs=[pl.BlockSpec((1, tr, tc), lambda i, j, me_ref: (me_ref[0], i, j)),
                      pl.BlockSpec((N_DEV, tr, tc), lambda i, j, me_ref: (0, i, j)), tile, tile, tile],
            out_specs=[tile, tile, tile, tile]),
        out_shape=[jax.ShapeDtypeStruct((R, C), F32)] * 4,
        compiler_params=_cp(("parallel", "parallel")),
    )(me, own, landed, w, m, v)


def _adamw(parts, w, m, v, *, tr, name):
    _, R, C = parts.shape
    assert R % tr == 0, (name, R, tr)

    def body(p_ref, w_ref, m_ref, v_ref, g_ref, d_ref, m2_ref, v2_ref):
        g = p_ref[0].astype(F32)
        for k in range(1, N_DEV):
            g = g + p_ref[k].astype(F32)
        _adam_update(g, w_ref, m_ref, v_ref, g_ref, d_ref, m2_ref, v2_ref)

    row = pl.BlockSpec((tr, C), lambda i: (i, 0))
    return pl.pallas_call(
        body, name=name, grid=(R // tr,),
        in_specs=[pl.BlockSpec((N_DEV, tr, C), lambda i: (0, i, 0)), row, row, row],
        out_specs=[row, row, row, row],
        out_shape=[jax.ShapeDtypeStruct((R, C), F32)] * 4,
        compiler_params=_cp(("parallel",)),
    )(parts, w, m, v)


def _place():
    x, y, c = lax.axis_index("x"), lax.axis_index("y"), lax.axis_index("c")
    return x, y, c


def _all_gather(arrs, *, name):
    n = len(arrs)

    def body(*refs):
        ins, outs = refs[:n], refs[n:2 * n]
        send_sems, recv_sems, local_sems = refs[2 * n:]
        x, y, c = _place()
        me, sibling = (x, y, c), (x, y, 1 - c)
        chips = [(1 - x, y), (x, 1 - y), (1 - x, 1 - y)]

        def idx(px, py, pc):
            return 4 * px + 2 * py + pc

        def copy(k, a, block, to, src=None):
            slab = outs[a].at[idx(*block)]
            return pltpu.make_async_remote_copy(
                src_ref=slab if src is None else src, dst_ref=slab,
                send_sem=send_sems.at[k, a], recv_sem=recv_sems.at[k, a], device_id=to, device_id_type=MESH)

        mine = [pltpu.make_async_copy(ins[a], outs[a].at[idx(*me)], local_sems.at[a]) for a in range(n)]
        for cp in mine:
            cp.start()
        first = []
        for a in range(n):
            first.append(copy(0, a, me, sibling, src=ins[a]))
            first += [copy(1 + j, a, me, (*chip, c), src=ins[a]) for j, chip in enumerate(chips)]
        for cp in first:
            cp.start()
        passed = []
        for j, chip in enumerate(chips):
            for a in range(n):
                copy(1 + j, a, (*chip, c), me).wait_recv()
                fwd = copy(4 + j, a, (*chip, c), sibling)
                fwd.start()
                passed.append(fwd)
        for a in range(n):
            copy(0, a, sibling, me).wait_recv()
            for j, chip in enumerate(chips):
                copy(4 + j, a, (*chip, 1 - c), me).wait_recv()
        for cp in first + passed:
            cp.wait_send()
        for cp in mine:
            cp.wait()

    anyspec = pl.BlockSpec(memory_space=pl.ANY)
    return pl.pallas_call(
        body, name=name,
        in_specs=[anyspec] * n, out_specs=[anyspec] * n,
        out_shape=[jax.ShapeDtypeStruct((N_DEV,) + a.shape, a.dtype) for a in arrs],
        scratch_shapes=[pltpu.SemaphoreType.DMA((7, n)), pltpu.SemaphoreType.DMA((7, n)), pltpu.SemaphoreType.DMA((n,))],
    )(*arrs)


W_ROWS = SEG_SSD[0] + SSD_PAD_W


GROUP = 16
INTERIOR = 1920


def _interior(k):
    lo = -(-(k * SHARD_IN) // GROUP) * GROUP
    hi = ((k + 1) * SHARD_IN) // GROUP * GROUP
    return lo, hi


def _dest_row(r):
    if r < REF_SGU_END:
        return r
    return r - REF_SGU_END + SEG_SSD[0] if r < REF_GATE_START else r - REF_GATE_START + SEG_GATE[0]


def _shard_pieces(k):
    lo_k, hi_k = _interior(k)
    out = []
    for lo, hi in ((0, REF_SGU_END), (REF_SGU_END, REF_GATE_START), (REF_GATE_START, W_IN)):
        a, b = max(lo, lo_k), min(hi, hi_k)
        if a < b:
            out.append((a - lo_k, b - a, _dest_row(a)))
    return out


GATHER_PARTS = 1


def _shard_parts(k):
    parts = [[] for _ in range(GATHER_PARTS)]
    for s0, n, d0 in _shard_pieces(k):
        step = -(-(n // GROUP) // GATHER_PARTS) * GROUP
        for p in range(GATHER_PARTS):
            a, b = min(p * step, n), min((p + 1) * step, n)
            if a < b:
                parts[p].append((s0 + a, b - a, d0 + a))
    return parts


def _patch_straddlers(wpT, heads, tails):
    for k in range(1, N_DEV):
        m = (k * SHARD_IN) % GROUP
        if m:
            group = jnp.concatenate([tails[k - 1, GROUP - m:], heads[k, :GROUP - m]], axis=0)
            wpT = lax.dynamic_update_slice(wpT, group, (_dest_row(k * SHARD_IN - m), 0))
    return wpT


def _gather_stages(k, win_ref, small, z_ref, n_zero, w_ref, send_sems, recv_sems, local_sems):
    x, y, c = k // 4, (k // 2) % 2, k % 2
    idx = lambda p: 4 * p[0] + 2 * p[1] + p[2]
    me, sib = (x, y, c), (x, y, 1 - c)
    xn, yn, dg = (1 - x, y, c), (x, 1 - y, c), (1 - x, 1 - y, c)
    parts = range(GATHER_PARTS)

    def copies(slot, block, to, part, own=False):
        kb = idx(block)
        out = []
        for j, (s0, n, d0) in enumerate(_shard_parts(kb)[part]):
            dst = w_ref.at[pl.ds(d0, n)]
            out.append((win_ref.at[pl.ds(s0, n)] if own else dst, dst, 2 * part + j))
        if part == 0:
            for j, (src, gathered) in enumerate(small):
                out.append((src if own else gathered.at[kb], gathered.at[kb], 2 * GATHER_PARTS + j))
        return [pltpu.make_async_remote_copy(src_ref=s, dst_ref=d, send_sem=send_sems.at[slot, j], recv_sem=recv_sems.at[slot, j],
                                             device_id=to, device_id_type=MESH) for s, d, j in out]

    def start(cps):
        for cp in cps:
            cp.start()

    def arrived(slot, block, part):
        for cp in copies(slot, block, me, part):
            cp.wait_recv()

    def local():
        pairs = [(win_ref.at[pl.ds(s0, n)], w_ref.at[pl.ds(d0, n)]) for s0, n, d0 in _shard_pieces(k)]
        pairs += [(src, gathered.at[k]) for src, gathered in small] + [(z_ref, w_ref.at[pl.ds(W_IN, n_zero)])]
        return [pltpu.make_async_copy(s, d, local_sems.at[j]) for j, (s, d) in enumerate(pairs)]

    relay = (xn, yn) if c == 1 else (yn, xn)

    def first():
        start(local())
        for p in parts:
            start(copies(0, me, sib, p, own=True) + copies(1, me, xn, p, own=True) + copies(2, me, yn, p, own=True))

    def hand_on():
        for p in parts:
            arrived(1, xn, p)
            start(copies(4, xn, sib, p))
            if c == 1:
                start(copies(3, *relay, p))
            arrived(2, yn, p)
            start(copies(5, yn, sib, p))
            if c == 0:
                start(copies(3, *relay, p))

    def finish():
        for p in parts:
            arrived(3, dg, p)
            start(copies(6, dg, sib, p))
        for p in parts:
            arrived(0, sib, p)
            arrived(4, (1 - x, y, 1 - c), p)
            arrived(5, (x, 1 - y, 1 - c), p)
            arrived(6, (1 - x, 1 - y, 1 - c), p)
        for p in parts:
            sent = (copies(0, me, sib, p, own=True) + copies(1, me, xn, p, own=True) + copies(2, me, yn, p, own=True)
                    + copies(3, *relay, p) + copies(4, xn, sib, p) + copies(5, yn, sib, p) + copies(6, dg, sib, p))
            for cp in sent:
                cp.wait_send()
        for cp in local():
            cp.wait()

    return first, hand_on, finish


def _gather_sems(n_small):
    n_arr = 2 * GATHER_PARTS + n_small
    return [pltpu.SemaphoreType.DMA((7, n_arr)), pltpu.SemaphoreType.DMA((7, n_arr)), pltpu.SemaphoreType.DMA((n_arr + 1,))]


def _gather_weights(win, head, tail, wout, cw, zeros):
    small_in = (wout, cw, head, tail)
    n_zero = zeros.shape[0]
    assert W_IN + n_zero == W_ROWS and W_IN % GROUP == 0

    def body(win_ref, wout_ref, cw_ref, head_ref, tail_ref, z_ref, w_ref, gout_ref, gcw_ref, ghead_ref, gtail_ref, *sems):
        x, y, c = _place()
        me = 4 * x + 2 * y + c
        small = ((wout_ref, gout_ref), (cw_ref, gcw_ref), (head_ref, ghead_ref), (tail_ref, gtail_ref))

        def run(k):
            for stage in _gather_stages(k, win_ref, small, z_ref, n_zero, w_ref, *sems):
                stage()

        for k in range(N_DEV):
            pl.when(me == k)(functools.partial(run, k))

    anyspec = pl.BlockSpec(memory_space=pl.ANY)
    return pl.pallas_call(
        body, name="gather_weights", in_specs=[anyspec] * 6, out_specs=[anyspec] * 5,
        out_shape=[jax.ShapeDtypeStruct((W_ROWS, D), win.dtype)]
        + [jax.ShapeDtypeStruct((N_DEV,) + a.shape, a.dtype) for a in small_in],
        scratch_shapes=_gather_sems(len(small_in)),
    )(win, wout, cw, head, tail, zeros)


_REL = [(dx, dy, dc) for dx in (0, 1) for dy in (0, 1) for dc in (0, 1)][1:]
_HBM = pl.BlockSpec(memory_space=pltpu.HBM)
_SEM = pl.BlockSpec(memory_space=pltpu.SEMAPHORE)
_EFFECT = pltpu.SideEffectType.DATAFLOW_SIDE_EFFECTING


def _peer(k):
    x, y, c = _place()
    dx, dy, dc = _REL[k]
    return (1 - x if dx else x, 1 - y if dy else y, 1 - c if dc else c)


def _exchange_start(parts, *, name):
    n = len(parts)

    def body(*refs):
        ins, lands = refs[:n], refs[n:2 * n]
        send_sems, recv_sems, token = refs[2 * n], refs[2 * n + 1], refs[-1]
        x, y, c = _place()
        me = 4 * x + 2 * y + c
        for a in range(n):
            for k in range(len(_REL)):
                px, py, pc = _peer(k)
                pltpu.make_async_remote_copy(
                    src_ref=ins[a].at[4 * px + 2 * py + pc], dst_ref=lands[a].at[me],
                    send_sem=send_sems.at[len(_REL) * a + k], recv_sem=recv_sems.at[len(_REL) * a + k],
                    device_id=(px, py, pc), device_id_type=MESH).start()
        token[...] = jnp.zeros_like(token)

    sem = pltpu.SemaphoreType.DMA((len(_REL) * n,))
    bufs = [pltpu.HBM(p.shape, p.dtype) for p in parts]
    outs = pl.pallas_call(
        body, name=name,
        out_shape=(sem, sem, *bufs, *bufs, jax.ShapeDtypeStruct((8, LANE), F32)),
        in_specs=(_HBM,) * (2 * n), out_specs=(_SEM, _SEM, *(_HBM,) * (2 * n), pl.BlockSpec(memory_space=pltpu.VMEM)),
        input_output_aliases={i: 2 + i for i in range(2 * n)},
        compiler_params=pltpu.CompilerParams(has_side_effects=_EFFECT),
    )(*[pltpu.with_memory_space_constraint(p, pltpu.HBM) for p in parts],
      *[pltpu.with_memory_space_constraint(lax.empty(p.shape, p.dtype), pltpu.HBM) for p in parts])
    return outs[0], outs[1], outs[2:2 + n], outs[2 + n:2 + 2 * n], outs[-1]


def _exchange_wait(send_sems, recv_sems, parts, lands, after, *, name):
    n = len(parts)

    def body(*refs):
        ins, lands_ = refs[:n], refs[n:2 * n]
        ssem, rsem = refs[2 * n], refs[2 * n + 1]
        for a in range(n):
            for k in range(len(_REL)):
                px, py, pc = _peer(k)
                p = 4 * px + 2 * py + pc
                cp = pltpu.make_async_remote_copy(
                    src_ref=ins[a].at[p], dst_ref=lands_[a].at[p],
                    send_sem=ssem.at[len(_REL) * a + k], recv_sem=rsem.at[len(_REL) * a + k],
                    device_id=(px, py, pc), device_id_type=MESH)
                cp.wait_send()
                cp.wait_recv()

    bufs = [pltpu.HBM(p.shape, p.dtype) for p in parts]
    outs = pl.pallas_call(
        body, name=name, out_shape=(*bufs, *bufs),
        in_specs=(*(_HBM,) * (2 * n), _SEM, _SEM, pl.BlockSpec(memory_space=pl.ANY)), out_specs=(_HBM,) * (2 * n),
        input_output_aliases={i: i for i in range(2 * n)},
        compiler_params=pltpu.CompilerParams(has_side_effects=_EFFECT),
    )(*parts, *lands, send_sems, recv_sems, after)
    return outs[:n], outs[n:]


WEIGHTS = ('norm_w', 'w_in', 'gate_b', 'sgu_norm_g', 'sgu_norm_b', 'sgu_w', 'sgu_b', 'conv_w', 'conv_b', 'dt_bias', 'A_log',
           'D_skip', 'ssd_norm_w', 'w_out', 'final_norm_w')
SHARDED = ('w_in', 'conv_w', 'w_out')
PACK_ROW = 8 * LANE


def _constants():
    tri = np.tril(np.ones((CHUNK, CHUNK), np.float32))
    expand = np.zeros((DT_W, D), np.float32)
    for h in range(HEADS):
        expand[h, h * HEADDIM:(h + 1) * HEADDIM] = 1.0
    sel = np.zeros((D, LANE), np.float32)
    for g in range(SGU_GROUPS):
        sel[g * LANE:(g + 1) * LANE, g] = 1.0
    pos_chunk = np.arange(SGU_BLOCK) // CHUNK
    mask = (pos_chunk[None, :] <= pos_chunk[:, None]).astype(np.float32)
    shift = np.zeros(((CONV_K - 1) * CHUNK, HALO_BLK + CHUNK), np.float32)
    for kk in range(CONV_K - 1):
        for t in range(CHUNK):
            shift[kk * CHUNK + t, HALO_BLK - (CONV_K - 1) + t + kk] = 1.0
    return dict(tri=jnp.asarray(tri, BF16), triT=jnp.asarray(tri.T.copy(), BF16), expand=jnp.asarray(np.tile(expand, (3, 1)), BF16),
                shift=jnp.asarray(shift, BF16),
                expandT=jnp.asarray(expand.T.copy(), BF16), sel=jnp.asarray(sel), mask=jnp.asarray(mask))


def _to_shards(segs):
    starts = np.cumsum([0] + [n for _, n in segs])
    assert starts[-1] == W_IN
    slabs = []
    for k in range(N_DEV):
        pieces = []
        for (s, n), s0 in zip(segs, starts[:-1]):
            lo, hi = max(k * SHARD_IN, s0), min((k + 1) * SHARD_IN, s0 + n)
            if lo < hi:
                pieces.append(s[lo - s0:hi - s0])
        slabs.append(jnp.concatenate(pieces, axis=0))
    return jnp.stack(slabs)


def _local_step(x2, tgt, wpT, wout, cw, p, exchange_small, exchange):
    S = x2.shape[0]
    k = _constants()
    xn, proj = _in_proj(x2, p['norm_w'], wpT, tm=min(1024, S), tn=2048)
    wm32 = p['sgu_w'][0] * k['mask']
    wm = wm32.astype(BF16)
    wmT = jnp.swapaxes(wm32, 1, 2).astype(BF16)
    bias_full = jnp.repeat(p['sgu_b'][0].T, LANE, axis=1)
    tm_sgu = min(512, S)
    ya = _sgu_fwd(proj, p['sgu_norm_g'], p['sgu_norm_b'], wm, bias_full, tm=tm_sgu)
    pad32 = lambda a: jnp.pad(a, ((0, 0), (0, DT_W - HEADS)))
    dtb_p, alog_p = pad32(p['dt_bias']), pad32(p['A_log'])
    d_exp = jnp.repeat(p['D_skip'], HEADDIM, axis=1)
    ssd_args = (cw, p['conv_b'], dtb_p, alog_p, d_exp, p['ssd_norm_w'])
    y, yb, states = _ssd_fwd(proj, *ssd_args, k['tri'], k['expand'], k['shift'])
    dh, dhb, mb, dya, dyb, dgl, loss, dfw, dgb = _head(
        x2, ya, yb, proj, tgt, p['gate_b'], wout, p['final_norm_w'][None, :], tm=min(256, S))
    dsgu, dws, dbsT, dsg, dsb = _sgu_bwd(proj, dya, p['sgu_norm_g'], p['sgu_norm_b'], wm, wmT, bias_full, k['mask'], k['sel'],
                                         tm=tm_sgu)
    dssd, dcw, dcb, ddtb, dalog, dD, dnw = _ssd_bwd(proj, dyb, y, states, *ssd_args, k['tri'], k['triT'], k['expand'], k['expandT'],
                                                    k['shift'])
    grads = dict(
        gate_b=dgb[0:1], sgu_norm_g=dsg[0:1], sgu_norm_b=dsb[0:1], sgu_w=dws[None],
        sgu_b=dbsT[:, :SGU_GROUPS].T[None], conv_w=dcw[0:CONV_K][None], conv_b=dcb[0:1], dt_bias=ddtb[0:1, :HEADS],
        A_log=dalog[0:1, :HEADS], D_skip=dD[0:1, :HEADS], ssd_norm_w=dnw[0:1], final_norm_w=dfw[0])
    tw = dict(trans_a=True, out_dtype=BF16, tm=1024, tn=512, tk=S)
    dw_out = _matmul(mb, dhb, name="dw_out", **tw)
    token = exchange_small(loss[0, 0], grads, dw_out)
    dwT_sgu = _matmul(dsgu, xn, after=token, name="dw_in_sgu", **tw)
    dwT_gate = _matmul(dgl, xn, name="dw_in_gate", **tw)
    dwT_ssd = _matmul(dssd, xn, name="dw_in_ssd", **tw)
    token = exchange([(dwT_sgu, SEG_SGU[1]), (dwT_ssd, W_IN - SEG_SSD[0]), (dwT_gate, SEG_GATE[1])])
    tm, tn = min(1024, S), 1024
    dxn = _matmul(dsgu, wpT, tm=tm, tn=tn, tk=3072, after=token, name="dxn_sgu")
    dxn = _matmul(dgl, wpT, b_koff=SEG_GATE[0] // 2048, tm=tm, tn=tn, tk=2048, add=dxn, name="dxn_gate")
    dxn = _matmul(dssd, wpT, b_koff=SEG_SSD[0] // 2048, tm=tm, tn=tn, tk=2048, add=dxn, name="dxn_ssd")
    grad_x, dnorm = _norm_bwd(x2, p['norm_w'], dxn, dh, tm=min(256, S))
    return grad_x, dnorm[0:1]


def _pack(arrs):
    rows, offs, r = [], [], 0
    for a in arrs:
        n = a.size
        nr = -(-n // PACK_ROW) * 8
        rows.append(jnp.pad(a.reshape(-1).astype(F32), (0, nr * LANE - n)).reshape(nr, LANE))
        offs.append(r)
        r += nr
    return jnp.concatenate(rows, axis=0), offs


def kernel(x, norm_w, w_in, gate_b, sgu_norm_g, sgu_norm_b, sgu_w, sgu_b, conv_w, conv_b, dt_bias, A_log, D_skip, ssd_norm_w, w_out, final_norm_w, loss_target, m_norm_w, m_w_in, m_gate_b, m_sgu_norm_g, m_sgu_norm_b, m_sgu_w, m_sgu_b, m_conv_w, m_conv_b, m_dt_bias, m_A_log, m_D_skip, m_ssd_norm_w, m_w_out, m_final_norm_w, v_norm_w, v_w_in, v_gate_b, v_sgu_norm_g, v_sgu_norm_b, v_sgu_w, v_sgu_b, v_conv_w, v_conv_b, v_dt_bias, v_A_log, v_D_skip, v_ssd_norm_w, v_w_out, v_final_norm_w):
    w = dict(norm_w=norm_w, w_in=w_in, gate_b=gate_b, sgu_norm_g=sgu_norm_g, sgu_norm_b=sgu_norm_b, sgu_w=sgu_w, sgu_b=sgu_b,
             conv_w=conv_w, conv_b=conv_b, dt_bias=dt_bias, A_log=A_log, D_skip=D_skip, ssd_norm_w=ssd_norm_w, w_out=w_out,
             final_norm_w=final_norm_w)
    m = dict(norm_w=m_norm_w, w_in=m_w_in, gate_b=m_gate_b, sgu_norm_g=m_sgu_norm_g, sgu_norm_b=m_sgu_norm_b, sgu_w=m_sgu_w,
             sgu_b=m_sgu_b, conv_w=m_conv_w, conv_b=m_conv_b, dt_bias=m_dt_bias, A_log=m_A_log, D_skip=m_D_skip,
             ssd_norm_w=m_ssd_norm_w, w_out=m_w_out, final_norm_w=m_final_norm_w)
    v = dict(norm_w=v_norm_w, w_in=v_w_in, gate_b=v_gate_b, sgu_norm_g=v_sgu_norm_g, sgu_norm_b=v_sgu_norm_b, sgu_w=v_sgu_w,
             sgu_b=v_sgu_b, conv_w=v_conv_w, conv_b=v_conv_b, dt_bias=v_dt_bias, A_log=v_A_log, D_skip=v_D_skip,
             ssd_norm_w=v_ssd_norm_w, w_out=v_w_out, final_norm_w=v_final_norm_w)
    me = 4 * lax.axis_index("x") + 2 * lax.axis_index("y") + lax.axis_index("c")
    shard_cw = XBC_W // N_DEV

    tpose = lambda a: jnp.swapaxes(a[0], 0, 1)
    wT = tpose(w_in).astype(BF16)
    first_group = (GROUP - (me * SHARD_IN) % GROUP) % GROUP
    window = lax.dynamic_slice(jnp.pad(wT, ((0, GROUP), (0, 0))), (first_group, 0), (INTERIOR, D))
    wpT, g_out, g_cw, heads, tails = _gather_weights(window, wT[:GROUP], wT[SHARD_IN - GROUP:], w_out[0].astype(BF16),
                                                     conv_w[0], jnp.zeros((W_ROWS - W_IN, D), BF16))
    wpT = _patch_straddlers(wpT, heads, tails)
    wout_full = g_out.reshape(D, D)
    cw_full = jnp.swapaxes(g_cw, 0, 1).reshape(CONV_K, XBC_W)

    flight = {}

    small = [n for n in WEIGHTS if n not in SHARDED and n != 'norm_w']
    early = {}

    def exchange_small(loss_part, grads, dw_out):
        early['packed'], early['offs'] = _pack([grads[n] for n in small] + [loss_part, grads['conv_w']])
        parts = [jnp.broadcast_to(early['packed'][None], (N_DEV,) + early['packed'].shape), dw_out.reshape(N_DEV, D // N_DEV, D)]
        early['sems'], early['rsems'], early['parts'], early['lands'], token = _exchange_start(parts, name="small_start")
        return token

    def exchange(dw_inT_segs):
        parts = [_to_shards(dw_inT_segs)]
        flight['sems'], flight['rsems'], flight['parts'], flight['lands'], token = _exchange_start(parts, name="exchange_start")
        return token

    grad_x, dnorm = _local_step(x[0], loss_target[0], wpT, wout_full, cw_full, w, exchange_small, exchange)
    (_, own_out), (land_small, land_out) = _exchange_wait(
        early['sems'], early['rsems'], early['parts'], early['lands'], grad_x, name="small_wait")
    (own_in,), (land_in,) = _exchange_wait(
        flight['sems'], flight['rsems'], flight['parts'], flight['lands'], grad_x, name="exchange_wait")
    me_arr = jnp.reshape(me, (1,)).astype(jnp.int32)
    res = {}
    res['w_in'] = [jnp.swapaxes(o, 0, 1) for o in _adamw_own(
        me_arr, own_in, land_in, tpose(w_in), tpose(m_w_in), tpose(v_w_in), tr=SHARD_IN, tc=256, name="adamw_w_in")]
    res['w_out'] = _adamw_own(me_arr, own_out, land_out, w_out[0], m_w_out[0], v_w_out[0], tr=128, tc=D, name="adamw_w_out")

    (norm_parts,) = _all_gather([_pack([dnorm])[0]], name="gather_norm")
    norm_outs = _adamw(norm_parts, *[_pack([d['norm_w']])[0] for d in (w, m, v)], tr=norm_parts.shape[1], name="adamw_norm")
    res['norm_w'] = [o.reshape(-1)[:D].reshape(w['norm_w'].shape) for o in norm_outs]

    offs = early['offs']
    gathered = lax.dynamic_update_slice(land_small, early['packed'][None], (me, 0, 0))
    off_loss, off_cw = offs[-2], offs[-1]
    cw_parts = gathered[:, off_cw:, :].reshape(N_DEV, CONV_K, XBC_W)
    cw_parts = lax.dynamic_slice_in_dim(cw_parts, me * shard_cw, shard_cw, axis=2)
    cw_rows = _pack([cw_parts[0]])[0].shape[0]
    cw_parts = jnp.pad(cw_parts.reshape(N_DEV, -1), ((0, 0), (0, cw_rows * LANE - CONV_K * shard_cw))).reshape(N_DEV, cw_rows, LANE)
    parts = jnp.concatenate([gathered[:, :off_cw, :], cw_parts], axis=1)
    zero = jnp.zeros((), F32)
    packs = [_pack([d[n] for n in small] + [zero, d['conv_w']])[0] for d in (w, m, v)]
    outs = _adamw(parts, *packs, tr=parts.shape[1], name="adamw_small")

    def unpack(o, name):
        if name == 'conv_w':
            return o[off_cw:off_cw + cw_rows].reshape(-1)[:CONV_K * shard_cw].reshape(w['conv_w'].shape)
        r0 = offs[small.index(name)]
        n = w[name].size
        return o[r0:r0 + -(-n // PACK_ROW) * 8].reshape(-1)[:n].reshape(w[name].shape)

    for n in small + ['conv_w']:
        res[n] = [unpack(o, n) for o in outs]
    for n in ('w_in', 'w_out'):
        res[n] = [o[None] for o in res[n]]
    loss = outs[0][off_loss, 0]
    return (loss, grad_x[None], *[res[n][0] for n in WEIGHTS], *[res[n][1] for n in WEIGHTS],
            *[res[n][2] for n in WEIGHTS], *[res[n][3] for n in WEIGHTS])
```

```python
import functools

import numpy as np
import jax
import jax.numpy as jnp
from jax import lax
from jax.experimental import pallas as pl
from jax.experimental.pallas import tpu as pltpu

F32 = jnp.float32
BF16 = jnp.bfloat16
HI = lax.Precision.HIGHEST
MESH = pl.DeviceIdType.MESH

D = 2048
EPS = 1e-5
SGU_BLOCK = 128
SGU_GROUPS = 16
CHUNK = 64
HEADS = 32
HEADDIM = 64
SSD_GROUPS = 4
GROUP_W = D // SSD_GROUPS
STATE = 128
CONV_K = 4
XBC_W = D + 2 * SSD_GROUPS * STATE
W_IN = 15392
N_DEV = 8
SHARD_IN = W_IN // N_DEV
ADAM_LR, ADAM_B1, ADAM_B2, ADAM_EPS, ADAM_WD, ADAM_STEP = 0.001, 0.9, 0.999, 1e-08, 0.01, 10

REF_SGU_END = 3 * D
REF_GATE_START = W_IN - 2 * D
LANE = 128
DT_W = LANE
OFF_U, OFF_V, OFF_ZA, OFF_G0, OFF_G1, OFF_ZB = (i * D for i in range(6))
OFF_XBC = OFF_ZB + D
OFF_DT = OFF_XBC + XBC_W
SEG_SGU = (OFF_U, 3 * D)
SEG_GATE = (OFF_G0, 2 * D)
SEG_SSD = (OFF_ZB, D + XBC_W + DT_W)
WP = SEG_SSD[0] + SEG_SSD[1]
SSD_PAD_W = 3 * D
VMEM_BYTES = 64 * 1024 * 1024
VMEM_LIMIT = VMEM_BYTES - 8 * 1024 * 1024


def _cp(sem=None, vmem=VMEM_LIMIT):
    return pltpu.CompilerParams(dimension_semantics=sem, vmem_limit_bytes=vmem)


def _sigmoid(x):
    return 1.0 / (1.0 + jnp.exp(-x))


def _softplus(x):
    return jnp.maximum(x, 0.0) + jnp.log(1.0 + jnp.exp(-jnp.abs(x)))


def _dot(a, b, precision=None):
    return jnp.dot(a, b, preferred_element_type=F32, precision=precision)


def _dot_nt(a, b, precision=None):
    return lax.dot_general(a, b, (((1,), (1,)), ((), ())), preferred_element_type=F32, precision=precision)


def _dot_tn(a, b, precision=None):
    return lax.dot_general(a, b, (((0,), (0,)), ((), ())), preferred_element_type=F32, precision=precision)


def _split3(a):
    hi = a.astype(BF16)
    r = a - hi.astype(F32)
    mid = r.astype(BF16)
    return hi, mid, (r - mid.astype(F32)).astype(BF16)


def _sel_right(a, sel01):
    m = a.shape[0]
    r = _dot(jnp.concatenate(_split3(a), axis=0), sel01)
    return (r[0:m] + r[m:2 * m]) + r[2 * m:3 * m]


def _sel_right_k(a, sel01_x3):
    return _dot(jnp.concatenate(_split3(a), axis=1), sel01_x3)


def _sel_left(sel01, a):
    n = a.shape[1]
    r = _dot(sel01, jnp.concatenate(_split3(a), axis=1))
    return (r[:, 0:n] + r[:, n:2 * n]) + r[:, 2 * n:3 * n]


def _matmul(a, b, *, trans_a=False, trans_b=False, b_koff=0, out_dtype=F32, tm, tn, tk, add=None, after=None, name):
    K, M = a.shape if trans_a else a.shape[::-1]
    N = b.shape[0] if trans_b else b.shape[1]
    assert M % tm == 0 and N % tn == 0 and K % tk == 0 and not (trans_a and trans_b), (name, M, N, K, tm, tn, tk)
    nk = K // tk

    def body(*refs):
        a_ref, b_ref = refs[:2]
        add_ref = refs[2] if add is not None else None
        o_ref, acc_ref = refs[-2:]
        k = pl.program_id(2)
        if trans_a:
            part = _dot_tn(a_ref[...], b_ref[...])
        else:
            part = _dot_nt(a_ref[...], b_ref[...]) if trans_b else _dot(a_ref[...], b_ref[...])

        def result(r):
            if add_ref is not None:
                r = r + add_ref[...]
            return r.astype(out_dtype)

        if nk == 1:
            o_ref[...] = result(part)
        else:
            @pl.when(k == 0)
            def _():
                acc_ref[...] = part

            @pl.when(jnp.logical_and(k > 0, k < nk - 1))
            def _():
                acc_ref[...] += part

            @pl.when(k == nk - 1)
            def _():
                o_ref[...] = result(acc_ref[...] + part)

    in_specs = [pl.BlockSpec((tk, tm), lambda i, j, k: (k, i)) if trans_a else pl.BlockSpec((tm, tk), lambda i, j, k: (i, k)),
                pl.BlockSpec((tn, tk), lambda i, j, k: (j, k)) if trans_b else pl.BlockSpec((tk, tn), lambda i, j, k: (k + b_koff, j))]
    args = [a, b]
    if add is not None:
        in_specs.append(pl.BlockSpec((tm, tn), lambda i, j, k: (i, j)))
        args.append(add)
    if after is not None:
        in_specs.append(pl.BlockSpec(memory_space=pl.ANY))
        args.append(after)
    return pl.pallas_call(
        body, name=name, grid=(M // tm, N // tn, nk), in_specs=in_specs,
        out_specs=pl.BlockSpec((tm, tn), lambda i, j, k: (i, j)),
        out_shape=jax.ShapeDtypeStruct((M, N), out_dtype),
        scratch_shapes=[pltpu.VMEM((tm, tn), F32)],
        compiler_params=_cp(("parallel", "parallel", "arbitrary")),
    )(*args)


def _in_proj(x, w, wpT, *, tm, tn):
    S = x.shape[0]
    N = wpT.shape[0]
    assert S % tm == 0 and N % tn == 0, (S, N, tm, tn)

    def body(x_ref, w_ref, b_ref, xn_ref, o_ref, xs_ref):
        @pl.when(pl.program_id(1) == 0)
        def _():
            xv = x_ref[...]
            r = lax.rsqrt(jnp.mean(xv * xv, axis=-1, keepdims=True) + EPS)
            xs = (xv * r * w_ref[...]).astype(BF16)
            xs_ref[...] = xs
            xn_ref[...] = xs

        o_ref[...] = _dot_nt(xs_ref[...], b_ref[...]).astype(BF16)

    return pl.pallas_call(
        body, name="in_proj", grid=(S // tm, N // tn),
        in_specs=[pl.BlockSpec((tm, D), lambda i, j: (i, 0)), pl.BlockSpec((1, D), lambda i, j: (0, 0)),
                  pl.BlockSpec((tn, D), lambda i, j: (j, 0))],
        out_specs=[pl.BlockSpec((tm, D), lambda i, j: (i, 0)), pl.BlockSpec((tm, tn), lambda i, j: (i, j))],
        out_shape=[jax.ShapeDtypeStruct((S, D), BF16), jax.ShapeDtypeStruct((S, N), BF16)],
        scratch_shapes=[pltpu.VMEM((tm, D), BF16)],
        compiler_params=_cp(("parallel", "arbitrary")),
    )(x, w, wpT)


def _norm_bwd(x, w, dxn, dh, *, tm):
    S = x.shape[0]

    def body(x_ref, w_ref, dxn_ref, dh_ref, gx_ref, dw_ref):
        xv = x_ref[...]
        r = lax.rsqrt(jnp.mean(xv * xv, axis=-1, keepdims=True) + EPS)
        xh = xv * r
        dxn_v = dxn_ref[...]
        dxh = dxn_v * w_ref[...]
        gx_ref[...] = dh_ref[...] + r * (dxh - xh * jnp.mean(dxh * xh, axis=-1, keepdims=True))

        @pl.when(pl.program_id(0) == 0)
        def _():
            dw_ref[...] = jnp.zeros_like(dw_ref)

        dw_ref[0:1, :] += jnp.sum(dxn_v * xh, axis=0, keepdims=True)

    row = pl.BlockSpec((tm, D), lambda i: (i, 0))
    return pl.pallas_call(
        body, name="norm_bwd", grid=(S // tm,),
        in_specs=[row, pl.BlockSpec((1, D), lambda i: (0, 0)), row, row],
        out_specs=[row, pl.BlockSpec((8, D), lambda i: (0, 0))],
        out_shape=[jax.ShapeDtypeStruct((S, D), F32), jax.ShapeDtypeStruct((8, D), F32)],
        compiler_params=_cp(("arbitrary",)),
    )(x, w, dxn, dh)


def _sgu_core(u_ref, v_ref, z_ref, g_ref, b_ref, wm_ref, bias_ref, vnb_ref, mixed_ref, tm):
    v = v_ref[...].astype(F32)
    mu = jnp.mean(v, axis=-1, keepdims=True)
    vc = v - mu
    rs = lax.rsqrt(jnp.mean(vc * vc, axis=-1, keepdims=True) + EPS)
    vh = vc * rs
    vnb_ref[...] = (vh * g_ref[...] + b_ref[...]).astype(BF16)
    for blk in range(tm // SGU_BLOCK):
        rows = pl.ds(blk * SGU_BLOCK, SGU_BLOCK)
        for gi in range(SGU_GROUPS):
            cols = pl.ds(gi * LANE, LANE)
            mixed_ref[rows, cols] = _dot(wm_ref[gi], vnb_ref[rows, cols]) + bias_ref[:, cols]
    return vh, rs


def _sgu_fwd(proj, g, b, wm, bias_full, *, tm):
    S = proj.shape[0]

    def body(u_ref, v_ref, z_ref, g_ref, b_ref, wm_ref, bias_ref, y_ref, vnb_ref, mixed_ref):
        _sgu_core(u_ref, v_ref, z_ref, g_ref, b_ref, wm_ref, bias_ref, vnb_ref, mixed_ref, tm)
        z = z_ref[...].astype(F32)
        y_ref[...] = (u_ref[...].astype(F32) * mixed_ref[...] * (z * _sigmoid(z))).astype(BF16)

    seg = lambda off: pl.BlockSpec((tm, D), lambda i: (i, off // D))
    full = lambda a: pl.BlockSpec(a.shape, lambda i: (0,) * a.ndim)
    return pl.pallas_call(
        body, name="sgu_fwd", grid=(S // tm,),
        in_specs=[seg(OFF_U), seg(OFF_V), seg(OFF_ZA), full(g), full(b), full(wm), full(bias_full)],
        out_specs=pl.BlockSpec((tm, D), lambda i: (i, 0)),
        out_shape=jax.ShapeDtypeStruct((S, D), BF16),
        scratch_shapes=[pltpu.VMEM((tm, D), BF16), pltpu.VMEM((tm, D), F32)],
        compiler_params=_cp(("parallel",)),
    )(proj, proj, proj, g, b, wm, bias_full)


def _sgu_bwd(proj, dy, g, b, wm, wmT, bias_full, mask, sel, *, tm):
    S = proj.shape[0]
    nsteps = S // tm

    def body(u_ref, v_ref, z_ref, dy_ref, g_ref, b_ref, wm_ref, wmT_ref, bias_ref, mask_ref, sel_ref,
             dp_ref, dws_ref, dbs_ref, dg_ref, db_ref, vnb_ref, mixed_ref, dmb_ref, dvn_ref, dbias_ref):
        i = pl.program_id(0)

        @pl.when(i == 0)
        def _():
            dws_ref[...] = jnp.zeros_like(dws_ref)
            dg_ref[...] = jnp.zeros_like(dg_ref)
            db_ref[...] = jnp.zeros_like(db_ref)
            dbias_ref[...] = jnp.zeros_like(dbias_ref)

        vh, rs = _sgu_core(u_ref, v_ref, z_ref, g_ref, b_ref, wm_ref, bias_ref, vnb_ref, mixed_ref, tm)
        u = u_ref[...].astype(F32)
        z = z_ref[...].astype(F32)
        dy_v = dy_ref[...].astype(F32)
        mixed = mixed_ref[...]
        sg = _sigmoid(z)
        sz = z * sg
        dp_ref[:, 0:D] = (dy_v * mixed * sz).astype(BF16)
        dp_ref[:, 2 * D:3 * D] = (dy_v * u * mixed * (sg * (1.0 + z * (1.0 - sg)))).astype(BF16)
        dmixed = dy_v * u * sz
        dmb_ref[...] = dmixed.astype(BF16)
        for blk in range(tm // SGU_BLOCK):
            dbias_ref[...] += dmixed[blk * SGU_BLOCK:(blk + 1) * SGU_BLOCK, :]
        for blk in range(tm // SGU_BLOCK):
            rows = pl.ds(blk * SGU_BLOCK, SGU_BLOCK)
            for gi in range(SGU_GROUPS):
                cols = pl.ds(gi * LANE, LANE)
                dm = dmb_ref[rows, cols]
                dvn_ref[rows, cols] = _dot(wmT_ref[gi], dm)
                dws_ref[gi] += _dot_nt(dm, vnb_ref[rows, cols])
        dvn = dvn_ref[...]
        dg_ref[0:1, :] += jnp.sum(dvn * vh, axis=0, keepdims=True)
        db_ref[0:1, :] += jnp.sum(dvn, axis=0, keepdims=True)
        dvh = dvn * g_ref[...]
        dv = rs * (dvh - jnp.mean(dvh, axis=-1, keepdims=True) - vh * jnp.mean(dvh * vh, axis=-1, keepdims=True))
        dp_ref[:, D:2 * D] = dv.astype(BF16)

        @pl.when(i == nsteps - 1)
        def _():
            for gi in range(SGU_GROUPS):
                dws_ref[gi] = dws_ref[gi] * mask_ref[...]
            dbs_ref[...] = _dot(dbias_ref[...], sel_ref[...], precision=HI)

    seg = lambda off: pl.BlockSpec((tm, D), lambda i: (i, off // D))
    full = lambda a: pl.BlockSpec(a.shape, lambda i: (0,) * a.ndim)
    return pl.pallas_call(
        body, name="sgu_bwd", grid=(nsteps,),
        in_specs=[seg(OFF_U), seg(OFF_V), seg(OFF_ZA), pl.BlockSpec((tm, D), lambda i: (i, 0)),
                  full(g), full(b), full(wm), full(wmT), full(bias_full), full(mask), full(sel)],
        out_specs=[pl.BlockSpec((tm, 3 * D), lambda i: (i, 0)),
                   pl.BlockSpec((SGU_GROUPS, SGU_BLOCK, SGU_BLOCK), lambda i: (0, 0, 0)),
                   pl.BlockSpec((SGU_BLOCK, LANE), lambda i: (0, 0)),
                   pl.BlockSpec((8, D), lambda i: (0, 0)), pl.BlockSpec((8, D), lambda i: (0, 0))],
        out_shape=[jax.ShapeDtypeStruct((S, 3 * D), BF16),
                   jax.ShapeDtypeStruct((SGU_GROUPS, SGU_BLOCK, SGU_BLOCK), F32),
                   jax.ShapeDtypeStruct((SGU_BLOCK, LANE), F32),
                   jax.ShapeDtypeStruct((8, D), F32), jax.ShapeDtypeStruct((8, D), F32)],
        scratch_shapes=[pltpu.VMEM((tm, D), BF16), pltpu.VMEM((tm, D), F32), pltpu.VMEM((tm, D), BF16),
                        pltpu.VMEM((tm, D), F32), pltpu.VMEM((SGU_BLOCK, D), F32)],
        compiler_params=_cp(("arbitrary",)),
    )(proj, proj, proj, dy, g, b, wm, wmT, bias_full, mask, sel)


SSD_T = 2 * CHUNK
HALO = 8
HALO_BLK = 16


def _pair_masks():
    row = lax.broadcasted_iota(jnp.int32, (CHUNK, LANE), 0)
    lane = lax.broadcasted_iota(jnp.int32, (CHUNK, LANE), 1)
    pos = jnp.where(lane >= CHUNK, lane - CHUNK, lane)
    diag = (row == pos).astype(F32)
    causal = row >= pos
    lo = (lane < CHUNK).astype(F32)
    return diag, causal, lo, 1.0 - lo


def _ssd_chunk_fwd(c, ext_ref, shift_ref, dt_ref, cw_ref, cb_ref, dtb_ref, alog_ref, tri_ref, exp_ref):
    r0 = c * CHUNK
    win = ext_ref[pl.ds(r0, HALO_BLK + CHUNK), :]
    sh = _dot(shift_ref[...], win)
    taps = [sh[k * CHUNK:(k + 1) * CHUNK] for k in range(CONV_K - 1)] + [win[HALO_BLK:].astype(F32)]
    pre = cb_ref[...] + sum(cw_ref[k:k + 1, :] * taps[k] for k in range(CONV_K))
    sg = _sigmoid(pre)
    xc = pre * sg
    dtr = dt_ref[pl.ds(r0, CHUNK), :].astype(F32) + dtb_ref[...]
    dtv = _softplus(dtr)
    A = -jnp.exp(alog_ref[...])
    acs = _sel_left(tri_ref[...], dtv * A)
    both = _sel_right_k(jnp.concatenate([acs, dtv], axis=0), exp_ref[...])
    E, dtE = both[0:CHUNK], both[CHUNK:2 * CHUNK]
    return dict(taps=taps, pre=pre, sg=sg, xc=xc, dtr=dtr, dtv=dtv, A=A, E=E, dtE=dtE)


def _ssd_fwd(proj, conv_w, conv_b, dtb_p, alog_p, d_exp, norm_w, tri, expand, shift):
    S = proj.shape[0]
    T = SSD_T
    nsteps = S // T
    ncl = T // CHUNK

    def body(zb_ref, xbc_ref, halo_ref, dt_ref, cw_ref, cb_ref, dtb_ref, alog_ref, dexp_ref, nw_ref, tri_ref, exp_ref, shift_ref,
             y_ref, yb_ref, st_ref, ht_ref, ext_ref):
        i = pl.program_id(0)

        @pl.when(i == 0)
        def _():
            ht_ref[...] = jnp.zeros_like(ht_ref)
            ext_ref[0:HALO_BLK, :] = jnp.zeros((HALO_BLK, XBC_W), BF16)

        @pl.when(i > 0)
        def _():
            ext_ref[0:HALO_BLK, :] = halo_ref[...]

        ext_ref[HALO_BLK:HALO_BLK + T, :] = xbc_ref[...]
        diag, causal, lo, hi = _pair_masks()
        for c in range(ncl):
            q = _ssd_chunk_fwd(c, ext_ref, shift_ref, dt_ref, cw_ref, cb_ref, dtb_ref, alog_ref, tri_ref, exp_ref)
            rows = pl.ds(c * CHUNK, CHUNK)
            xc, E, dtE = q["xc"], q["E"], q["dtE"]
            xs = xc[:, 0:D]
            total = E[CHUNK - 1:CHUNK, :]
            x_dt = xs * dtE
            eE = jnp.exp(E)
            xw = x_dt * jnp.exp(total - E)
            st_ref[c] = ht_ref[...]
            for g in range(SSD_GROUPS):
                gc = slice(g * GROUP_W, (g + 1) * GROUP_W)
                Bg = xc[:, D + g * STATE:D + (g + 1) * STATE].astype(BF16)
                Cg = xc[:, D + SSD_GROUPS * STATE + g * STATE:D + SSD_GROUPS * STATE + (g + 1) * STATE].astype(BF16)
                cb2 = _dot_nt(Cg, jnp.concatenate([Bg, Bg], axis=0))
                htg = ht_ref[:, gc]
                y_ref[rows, gc] = eE[:, gc] * _dot(Cg, htg.astype(BF16)) + xs[:, gc] * dexp_ref[:, gc]
                for jj in range(GROUP_W // LANE):
                    pc = slice(g * GROUP_W + jj * LANE, g * GROUP_W + (jj + 1) * LANE)
                    Ej = E[:, pc]
                    e2 = jnp.sum(Ej * diag, axis=0, keepdims=True)
                    Mp = cb2 * jnp.exp(jnp.where(causal, Ej - e2, -1e30))
                    xj = x_dt[:, pc]
                    xbd = jnp.concatenate([xj * lo, xj * hi], axis=0).astype(BF16)
                    y_ref[rows, pc] += _dot(Mp.astype(BF16), xbd)
                ht_ref[:, gc] = jnp.exp(total[:, gc]) * htg + _dot_tn(Bg, xw[:, gc].astype(BF16))
            zb = zb_ref[rows, :].astype(F32)
            hh = y_ref[rows, :] * (zb * _sigmoid(zb))
            for g in range(SSD_GROUPS):
                gc = slice(g * GROUP_W, (g + 1) * GROUP_W)
                hg = hh[:, gc]
                r = lax.rsqrt(jnp.mean(hg * hg, axis=-1, keepdims=True) + EPS)
                yb_ref[rows, gc] = (hg * r * nw_ref[:, gc]).astype(BF16)

    full = lambda a: pl.BlockSpec(a.shape, lambda i: (0,) * a.ndim)
    hb = T // HALO_BLK
    return pl.pallas_call(
        body, name="ssd_fwd", grid=(nsteps,),
        in_specs=[pl.BlockSpec((T, D), lambda i: (i, OFF_ZB // D)),
                  pl.BlockSpec((T, XBC_W), lambda i: (i, OFF_XBC // XBC_W)),
                  pl.BlockSpec((HALO_BLK, XBC_W), lambda i: (jnp.maximum(i * hb - 1, 0), OFF_XBC // XBC_W)),
                  pl.BlockSpec((T, DT_W), lambda i: (i, OFF_DT // DT_W)),
                  full(conv_w), full(conv_b), full(dtb_p), full(alog_p), full(d_exp), full(norm_w), full(tri), full(expand),
                  full(shift)],
        out_specs=[pl.BlockSpec((T, D), lambda i: (i, 0)), pl.BlockSpec((T, D), lambda i: (i, 0)),
                   pl.BlockSpec((ncl, STATE, D), lambda i: (i, 0, 0))],
        out_shape=[jax.ShapeDtypeStruct((S, D), F32), jax.ShapeDtypeStruct((S, D), BF16),
                   jax.ShapeDtypeStruct((S // CHUNK, STATE, D), F32)],
        scratch_shapes=[pltpu.VMEM((STATE, D), F32), pltpu.VMEM((HALO_BLK + T, XBC_W), BF16)],
        compiler_params=_cp(("arbitrary",)),
    )(proj, proj, proj, proj, conv_w, conv_b, dtb_p, alog_p, d_exp, norm_w, tri, expand, shift)


def _ssd_bwd(proj, dyb, y, states, conv_w, conv_b, dtb_p, alog_p, d_exp, norm_w, tri, triT, expand, expandT, shift):
    S = proj.shape[0]
    T = SSD_T
    nsteps = S // T
    ncl = T // CHUNK
    SSD_W = SSD_PAD_W

    def body(zb_ref, xbc_ref, halo_ref, dt_ref, dyb_ref, y_ref, st_ref, cw_ref, cb_ref, dtb_ref, alog_ref, dexp_ref, nw_ref,
             tri_ref, triT_ref, exp_ref, expT_ref, shift_ref,
             dp_ref, dcw_ref, dcb_ref, ddtb_ref, dalog_ref, dD_ref, dnw_ref,
             dht_ref, ext_ref, dpre_ref, dy_s, dE_s, dxdt_s, dxc_s, dDacc_ref, dAacc_ref):
        i = pl.program_id(0)

        @pl.when(i == 0)
        def _():
            for r in (dht_ref, dcw_ref, dcb_ref, ddtb_ref, dnw_ref, dDacc_ref, dAacc_ref):
                r[...] = jnp.zeros_like(r)
            dpre_ref[T:T + HALO_BLK, :] = jnp.zeros((HALO_BLK, XBC_W), F32)

        @pl.when(i == nsteps - 1)
        def _():
            ext_ref[0:HALO_BLK, :] = jnp.zeros((HALO_BLK, XBC_W), BF16)

        @pl.when(i < nsteps - 1)
        def _():
            ext_ref[0:HALO_BLK, :] = halo_ref[...]

        ext_ref[HALO_BLK:HALO_BLK + T, :] = xbc_ref[...]
        diag, causal, lo, hi = _pair_masks()
        last_row = (lax.broadcasted_iota(jnp.int32, (CHUNK, 1), 0) == CHUNK - 1).astype(F32)
        for c in reversed(range(ncl)):
            q = _ssd_chunk_fwd(c, ext_ref, shift_ref, dt_ref, cw_ref, cb_ref, dtb_ref, alog_ref, tri_ref, exp_ref)
            rows = pl.ds(c * CHUNK, CHUNK)
            pre, sg, xc, dtr, dtv, A, E, dtE = (q[k] for k in ("pre", "sg", "xc", "dtr", "dtv", "A", "E", "dtE"))
            xs = xc[:, 0:D]
            total = E[CHUNK - 1:CHUNK, :]
            x_dt = xs * dtE
            eE = jnp.exp(E)
            wdec = jnp.exp(total - E)
            zb = zb_ref[rows, :].astype(F32)
            yv = y_ref[rows, :]
            sgz = _sigmoid(zb)
            sz = zb * sgz
            hh = yv * sz
            for g in range(SSD_GROUPS):
                gc = slice(g * GROUP_W, (g + 1) * GROUP_W)
                hg = hh[:, gc]
                r = lax.rsqrt(jnp.mean(hg * hg, axis=-1, keepdims=True) + EPS)
                dyb_g = dyb_ref[rows, gc].astype(F32)
                dn = dyb_g * nw_ref[:, gc]
                dnw_ref[0:1, gc] += jnp.sum(dyb_g * hg * r, axis=0, keepdims=True)
                dy_s[:, gc] = r * dn - hg * (r * r * r) * jnp.mean(dn * hg, axis=-1, keepdims=True)
            dhh = dy_s[...]
            dp_ref[rows, 0:D] = (dhh * yv * (sgz * (1.0 + zb * (1.0 - sgz)))).astype(BF16)
            dy = dhh * sz
            dy_s[...] = dy
            dDacc_ref[0:1, :] += jnp.sum(dy * xs, axis=0, keepdims=True)
            dxc_s[:, 0:D] = dy * dexp_ref[...]
            for g in range(SSD_GROUPS):
                gc = slice(g * GROUP_W, (g + 1) * GROUP_W)
                bcol = slice(D + g * STATE, D + (g + 1) * STATE)
                ccol = slice(D + SSD_GROUPS * STATE + g * STATE, D + SSD_GROUPS * STATE + (g + 1) * STATE)
                Bg = xc[:, bcol].astype(BF16)
                Cg = xc[:, ccol].astype(BF16)
                B2 = jnp.concatenate([Bg, Bg], axis=0)
                cb2 = _dot_nt(Cg, B2)
                htg = st_ref[c, :, gc]
                htb = htg.astype(BF16)
                dhn = dht_ref[:, gc]
                dhnb = dhn.astype(BF16)
                dyg = dy[:, gc]
                eEg = eE[:, gc]
                wg = wdec[:, gc]
                xdg = x_dt[:, gc]
                CH = _dot(Cg, htb)
                dCHb = (dyg * eEg).astype(BF16)
                dC = _dot_nt(dCHb, htb)
                dl = jnp.exp(total[:, gc])
                dht_prev = _dot_tn(Cg, dCHb) + dl * dhn
                dtot = jnp.sum(dhn * htg, axis=0, keepdims=True) * dl
                dxw = _dot(Bg, dhnb)
                dB = _dot_nt((xdg * wg).astype(BF16), dhnb)
                dwd = dxw * xdg * wg
                dtot = dtot + jnp.sum(dwd, axis=0, keepdims=True)
                dE_s[:, gc] = dyg * eEg * CH - dwd + last_row * dtot
                dxdt_s[:, gc] = dxw * wg
                dcb2 = jnp.zeros((CHUNK, LANE), F32)
                for jj in range(GROUP_W // LANE):
                    pc = slice(g * GROUP_W + jj * LANE, g * GROUP_W + (jj + 1) * LANE)
                    Ej = E[:, pc]
                    e2 = jnp.sum(Ej * diag, axis=0, keepdims=True)
                    Lp = jnp.exp(jnp.where(causal, Ej - e2, -1e30))
                    Mp = cb2 * Lp
                    xj = x_dt[:, pc]
                    xbd = jnp.concatenate([xj * lo, xj * hi], axis=0).astype(BF16)
                    dyj = dy[:, pc].astype(BF16)
                    dMp = _dot_nt(dyj, xbd)
                    dxbd = _dot_tn(Mp.astype(BF16), dyj)
                    dxdt_s[:, pc] += dxbd[0:CHUNK, :] * lo + dxbd[CHUNK:2 * CHUNK, :] * hi
                    dcb2 = dcb2 + dMp * Lp
                    dseg = dMp * Mp
                    dE_s[:, pc] += dseg - diag * jnp.sum(dseg, axis=0, keepdims=True)
                dcb2b = dcb2.astype(BF16)
                dC = dC + _dot(dcb2b, B2)
                dB2 = _dot_tn(dcb2b, Cg)
                dB = dB + dB2[0:CHUNK, :] + dB2[CHUNK:2 * CHUNK, :]
                dxc_s[:, bcol] = dB
                dxc_s[:, ccol] = dC
                dht_ref[:, gc] = dht_prev
            dx_dt = dxdt_s[...]
            dxc_s[:, 0:D] += dx_dt * dtE
            red = _sel_right(jnp.concatenate([dE_s[...], dx_dt * xs], axis=0), expT_ref[...])
            da = _sel_left(triT_ref[...], red[0:CHUNK, :])
            ddtv = red[CHUNK:2 * CHUNK, :] + da * A
            dAacc_ref[0:1, :] += jnp.sum(da * dtv, axis=0, keepdims=True)
            ddtr = ddtv * _sigmoid(dtr)
            ddtb_ref[0:1, :] += jnp.sum(ddtr, axis=0, keepdims=True)
            dp_ref[rows, D + XBC_W:D + XBC_W + DT_W] = ddtr.astype(BF16)
            dpre = dxc_s[...] * (sg * (1.0 + pre * (1.0 - sg)))
            dpre_ref[rows, :] = dpre
            dcb_ref[0:1, :] += jnp.sum(dpre, axis=0, keepdims=True)
            for k in range(CONV_K):
                dcw_ref[k:k + 1, :] += jnp.sum(dpre * q["taps"][k], axis=0, keepdims=True)
        dxbc = jnp.zeros((T, XBC_W), F32)
        for k in range(CONV_K):
            dxbc = dxbc + cw_ref[k:k + 1, :] * dpre_ref[pl.ds(CONV_K - 1 - k, T), :]
        dp_ref[:, D:D + XBC_W] = dxbc.astype(BF16)
        dp_ref[:, SEG_SSD[1]:SSD_W] = jnp.zeros((T, SSD_W - SEG_SSD[1]), BF16)
        dpre_ref[T:T + HALO, :] = dpre_ref[0:HALO, :]

        @pl.when(i == nsteps - 1)
        def _():
            dalog_ref[...] = dAacc_ref[...] * (-jnp.exp(alog_ref[...]))
            dD_ref[...] = _dot(dDacc_ref[...], expT_ref[...].astype(F32), precision=HI)

    full = lambda a: pl.BlockSpec(a.shape, lambda i: (0,) * a.ndim)
    hb = T // HALO_BLK
    rev = lambda i: nsteps - 1 - i
    acc = lambda w: pl.BlockSpec((8, w), lambda i: (0, 0))
    return pl.pallas_call(
        body, name="ssd_bwd", grid=(nsteps,),
        in_specs=[pl.BlockSpec((T, D), lambda i: (rev(i), OFF_ZB // D)),
                  pl.BlockSpec((T, XBC_W), lambda i: (rev(i), OFF_XBC // XBC_W)),
                  pl.BlockSpec((HALO_BLK, XBC_W), lambda i: (jnp.maximum(rev(i) * hb - 1, 0), OFF_XBC // XBC_W)),
                  pl.BlockSpec((T, DT_W), lambda i: (rev(i), OFF_DT // DT_W)),
                  pl.BlockSpec((T, D), lambda i: (rev(i), 0)), pl.BlockSpec((T, D), lambda i: (rev(i), 0)),
                  pl.BlockSpec((ncl, STATE, D), lambda i: (rev(i), 0, 0)),
                  full(conv_w), full(conv_b), full(dtb_p), full(alog_p), full(d_exp), full(norm_w),
                  full(tri), full(triT), full(expand), full(expandT), full(shift)],
        out_specs=[pl.BlockSpec((T, SSD_W), lambda i: (rev(i), 0)),
                   acc(XBC_W), acc(XBC_W), acc(DT_W), acc(DT_W), acc(DT_W), acc(D)],
        out_shape=[jax.ShapeDtypeStruct((S, SSD_W), BF16),
                   jax.ShapeDtypeStruct((8, XBC_W), F32), jax.ShapeDtypeStruct((8, XBC_W), F32),
                   jax.ShapeDtypeStruct((8, DT_W), F32), jax.ShapeDtypeStruct((8, DT_W), F32),
                   jax.ShapeDtypeStruct((8, DT_W), F32), jax.ShapeDtypeStruct((8, D), F32)],
        scratch_shapes=[pltpu.VMEM((STATE, D), F32), pltpu.VMEM((HALO_BLK + T, XBC_W), BF16), pltpu.VMEM((T + HALO_BLK, XBC_W), F32),
                        pltpu.VMEM((CHUNK, D), F32), pltpu.VMEM((CHUNK, D), F32), pltpu.VMEM((CHUNK, D), F32),
                        pltpu.VMEM((CHUNK, XBC_W), F32), pltpu.VMEM((8, D), F32), pltpu.VMEM((8, DT_W), F32)],
        compiler_params=_cp(("arbitrary",)),
    )(proj, proj, proj, proj, dyb, y, states, conv_w, conv_b, dtb_p, alog_p, d_exp, norm_w, tri, triT, expand, expandT, shift)


def _head(x, ya, yb, proj, target, gate_b, wout, fw, *, tm):
    S = x.shape[0]

    def body(x_ref, ya_ref, yb_ref, gl0_ref, gl1_ref, t_ref, gb_ref, w_ref, fw_ref,
             dh_ref, dhb_ref, mb_ref, dya_ref, dyb_ref, dgl_ref, loss_ref, dfw_ref, dgb_ref):
        @pl.when(pl.program_id(0) == 0)
        def _():
            loss_ref[...] = jnp.zeros_like(loss_ref)
            dfw_ref[...] = jnp.zeros_like(dfw_ref)
            dgb_ref[...] = jnp.zeros_like(dgb_ref)

        ya_v = ya_ref[...].astype(F32)
        yb_v = yb_ref[...].astype(F32)
        g0 = _sigmoid(gl0_ref[...].astype(F32) + gb_ref[:, 0:D])
        g1 = _sigmoid(gl1_ref[...].astype(F32) + gb_ref[:, D:2 * D])
        mb = (g0 * ya_v + g1 * yb_v).astype(BF16)
        mb_ref[...] = mb
        h = x_ref[...] + _dot(mb, w_ref[...])
        r = lax.rsqrt(jnp.mean(h * h, axis=-1, keepdims=True) + EPS)
        hn = h * r
        err = hn * fw_ref[...] - t_ref[...]
        loss_ref[...] += 0.5 * jnp.sum(jnp.mean(err * err, axis=-1, keepdims=True))
        dyf = err * (1.0 / D)
        dfw_ref[0:1, :] += jnp.sum(dyf * hn, axis=0, keepdims=True)
        dhn = dyf * fw_ref[...]
        dh = r * (dhn - hn * jnp.mean(dhn * hn, axis=-1, keepdims=True))
        dh_ref[...] = dh
        dhb = dh.astype(BF16)
        dhb_ref[...] = dhb
        dm = _dot_nt(dhb, w_ref[...])
        dya_ref[...] = (dm * g0).astype(BF16)
        dyb_ref[...] = (dm * g1).astype(BF16)
        dgl0 = dm * ya_v * g0 * (1.0 - g0)
        dgl1 = dm * yb_v * g1 * (1.0 - g1)
        dgl_ref[:, 0:D] = dgl0.astype(BF16)
        dgl_ref[:, D:2 * D] = dgl1.astype(BF16)
        dgb_ref[0:1, 0:D] += jnp.sum(dgl0, axis=0, keepdims=True)
        dgb_ref[0:1, D:2 * D] += jnp.sum(dgl1, axis=0, keepdims=True)

    row = pl.BlockSpec((tm, D), lambda i: (i, 0))
    seg = lambda off: pl.BlockSpec((tm, D), lambda i: (i, off // D))
    full = lambda a: pl.BlockSpec(a.shape, lambda i: (0,) * a.ndim)
    acc = lambda w: pl.BlockSpec((8, w), lambda i: (0, 0))
    return pl.pallas_call(
        body, name="head", grid=(S // tm,),
        in_specs=[row, row, row, seg(OFF_G0), seg(OFF_G1), row, full(gate_b), full(wout), full(fw)],
        out_specs=[row, row, row, row, row, pl.BlockSpec((tm, 2 * D), lambda i: (i, 0)), acc(LANE), acc(D), acc(2 * D)],
        out_shape=[jax.ShapeDtypeStruct((S, D), F32), jax.ShapeDtypeStruct((S, D), BF16), jax.ShapeDtypeStruct((S, D), BF16),
                   jax.ShapeDtypeStruct((S, D), BF16), jax.ShapeDtypeStruct((S, D), BF16), jax.ShapeDtypeStruct((S, 2 * D), BF16),
                   jax.ShapeDtypeStruct((8, LANE), F32), jax.ShapeDtypeStruct((8, D), F32), jax.ShapeDtypeStruct((8, 2 * D), F32)],
        compiler_params=_cp(("arbitrary",)),
    )(x, ya, yb, proj, proj, target, gate_b, wout, fw)


def _adam_update(g, w_ref, m_ref, v_ref, g_ref, d_ref, m2_ref, v2_ref):
    m2 = ADAM_B1 * m_ref[...] + (1.0 - ADAM_B1) * g
    v2 = ADAM_B2 * v_ref[...] + (1.0 - ADAM_B2) * (g * g)
    m_hat = m2 / (1.0 - ADAM_B1 ** ADAM_STEP)
    v_hat = v2 / (1.0 - ADAM_B2 ** ADAM_STEP)
    g_ref[...] = g
    d_ref[...] = -ADAM_LR * (m_hat / (jnp.sqrt(v_hat) + ADAM_EPS) + ADAM_WD * w_ref[...])
    m2_ref[...] = m2
    v2_ref[...] = v2


def _adamw_own(me, own, landed, w, m, v, *, tr, tc, name):
    _, R, C = landed.shape
    assert R % tr == 0 and C % tc == 0, (name, R, C, tr, tc)

    def body(me_ref, own_ref, p_ref, w_ref, m_ref, v_ref, g_ref, d_ref, m2_ref, v2_ref):
        mine = own_ref[0].astype(F32)
        g = jnp.where(me_ref[0] == 0, mine, p_ref[0].astype(F32))
        for k in range(1, N_DEV):
            g = g + jnp.where(me_ref[0] == k, mine, p_ref[k].astype(F32))
        _adam_update(g, w_ref, m_ref, v_ref, g_ref, d_ref, m2_ref, v2_ref)

    tile = pl.BlockSpec((tr, tc), lambda i, j, me_ref: (i, j))
    return pl.pallas_call(
        body, name=name,
        grid_spec=pltpu.PrefetchScalarGridSpec(
            num_scalar_prefetch=1, grid=(R // tr, C // tc),
            in_specs=[pl.BlockSpec((1, tr, tc), lambda i, j, me_ref: (me_ref[0], i, j)),
                      pl.BlockSpec((N_DEV, tr, tc), lambda i, j, me_ref: (0, i, j)), tile, tile, tile],
            out_specs=[tile, tile, tile, tile]),
        out_shape=[jax.ShapeDtypeStruct((R, C), F32)] * 4,
        compiler_params=_cp(("parallel", "parallel")),
    )(me, own, landed, w, m, v)


def _adamw(parts, w, m, v, *, tr, name):
    _, R, C = parts.shape
    assert R % tr == 0, (name, R, tr)

    def body(p_ref, w_ref, m_ref, v_ref, g_ref, d_ref, m2_ref, v2_ref):
        g = p_ref[0].astype(F32)
        for k in range(1, N_DEV):
            g = g + p_ref[k].astype(F32)
        _adam_update(g, w_ref, m_ref, v_ref, g_ref, d_ref, m2_ref, v2_ref)

    row = pl.BlockSpec((tr, C), lambda i: (i, 0))
    return pl.pallas_call(
        body, name=name, grid=(R // tr,),
        in_specs=[pl.BlockSpec((N_DEV, tr, C), lambda i: (0, i, 0)), row, row, row],
        out_specs=[row, row, row, row],
        out_shape=[jax.ShapeDtypeStruct((R, C), F32)] * 4,
        compiler_params=_cp(("parallel",)),
    )(parts, w, m, v)


def _place():
    x, y, c = lax.axis_index("x"), lax.axis_index("y"), lax.axis_index("c")
    return x, y, c


def _all_gather(arrs, *, name):
    n = len(arrs)

    def body(*refs):
        ins, outs = refs[:n], refs[n:2 * n]
        send_sems, recv_sems, local_sems = refs[2 * n:]
        x, y, c = _place()
        me, sibling = (x, y, c), (x, y, 1 - c)
        chips = [(1 - x, y), (x, 1 - y), (1 - x, 1 - y)]

        def idx(px, py, pc):
            return 4 * px + 2 * py + pc

        def copy(k, a, block, to, src=None):
            slab = outs[a].at[idx(*block)]
            return pltpu.make_async_remote_copy(
                src_ref=slab if src is None else src, dst_ref=slab,
                send_sem=send_sems.at[k, a], recv_sem=recv_sems.at[k, a], device_id=to, device_id_type=MESH)

        mine = [pltpu.make_async_copy(ins[a], outs[a].at[idx(*me)], local_sems.at[a]) for a in range(n)]
        for cp in mine:
            cp.start()
        first = []
        for a in range(n):
            first.append(copy(0, a, me, sibling, src=ins[a]))
            first += [copy(1 + j, a, me, (*chip, c), src=ins[a]) for j, chip in enumerate(chips)]
        for cp in first:
            cp.start()
        passed = []
        for j, chip in enumerate(chips):
            for a in range(n):
                copy(1 + j, a, (*chip, c), me).wait_recv()
                fwd = copy(4 + j, a, (*chip, c), sibling)
                fwd.start()
                passed.append(fwd)
        for a in range(n):
            copy(0, a, sibling, me).wait_recv()
            for j, chip in enumerate(chips):
                copy(4 + j, a, (*chip, 1 - c), me).wait_recv()
        for cp in first + passed:
            cp.wait_send()
        for cp in mine:
            cp.wait()

    anyspec = pl.BlockSpec(memory_space=pl.ANY)
    return pl.pallas_call(
        body, name=name,
        in_specs=[anyspec] * n, out_specs=[anyspec] * n,
        out_shape=[jax.ShapeDtypeStruct((N_DEV,) + a.shape, a.dtype) for a in arrs],
        scratch_shapes=[pltpu.SemaphoreType.DMA((7, n)), pltpu.SemaphoreType.DMA((7, n)), pltpu.SemaphoreType.DMA((n,))],
    )(*arrs)


W_ROWS = SEG_SSD[0] + SSD_PAD_W


GROUP = 16
INTERIOR = 1920


def _interior(k):
    lo = -(-(k * SHARD_IN) // GROUP) * GROUP
    hi = ((k + 1) * SHARD_IN) // GROUP * GROUP
    return lo, hi


def _dest_row(r):
    if r < REF_SGU_END:
        return r
    return r - REF_SGU_END + SEG_SSD[0] if r < REF_GATE_START else r - REF_GATE_START + SEG_GATE[0]


def _shard_pieces(k):
    lo_k, hi_k = _interior(k)
    out = []
    for lo, hi in ((0, REF_SGU_END), (REF_SGU_END, REF_GATE_START), (REF_GATE_START, W_IN)):
        a, b = max(lo, lo_k), min(hi, hi_k)
        if a < b:
            out.append((a - lo_k, b - a, _dest_row(a)))
    return out


GATHER_PARTS = 1


def _shard_parts(k):
    parts = [[] for _ in range(GATHER_PARTS)]
    for s0, n, d0 in _shard_pieces(k):
        step = -(-(n // GROUP) // GATHER_PARTS) * GROUP
        for p in range(GATHER_PARTS):
            a, b = min(p * step, n), min((p + 1) * step, n)
            if a < b:
                parts[p].append((s0 + a, b - a, d0 + a))
    return parts


def _patch_straddlers(wpT, heads, tails):
    for k in range(1, N_DEV):
        m = (k * SHARD_IN) % GROUP
        if m:
            group = jnp.concatenate([tails[k - 1, GROUP - m:], heads[k, :GROUP - m]], axis=0)
            wpT = lax.dynamic_update_slice(wpT, group, (_dest_row(k * SHARD_IN - m), 0))
    return wpT


def _gather_stages(k, win_ref, small, z_ref, n_zero, w_ref, send_sems, recv_sems, local_sems):
    x, y, c = k // 4, (k // 2) % 2, k % 2
    idx = lambda p: 4 * p[0] + 2 * p[1] + p[2]
    me, sib = (x, y, c), (x, y, 1 - c)
    xn, yn, dg = (1 - x, y, c), (x, 1 - y, c), (1 - x, 1 - y, c)
    parts = range(GATHER_PARTS)

    def copies(slot, block, to, part, own=False):
        kb = idx(block)
        out = []
        for j, (s0, n, d0) in enumerate(_shard_parts(kb)[part]):
            dst = w_ref.at[pl.ds(d0, n)]
            out.append((win_ref.at[pl.ds(s0, n)] if own else dst, dst, 2 * part + j))
        if part == 0:
            for j, (src, gathered) in enumerate(small):
                out.append((src if own else gathered.at[kb], gathered.at[kb], 2 * GATHER_PARTS + j))
        return [pltpu.make_async_remote_copy(src_ref=s, dst_ref=d, send_sem=send_sems.at[slot, j], recv_sem=recv_sems.at[slot, j],
                                             device_id=to, device_id_type=MESH) for s, d, j in out]

    def start(cps):
        for cp in cps:
            cp.start()

    def arrived(slot, block, part):
        for cp in copies(slot, block, me, part):
            cp.wait_recv()

    def local():
        pairs = [(win_ref.at[pl.ds(s0, n)], w_ref.at[pl.ds(d0, n)]) for s0, n, d0 in _shard_pieces(k)]
        pairs += [(src, gathered.at[k]) for src, gathered in small] + [(z_ref, w_ref.at[pl.ds(W_IN, n_zero)])]
        return [pltpu.make_async_copy(s, d, local_sems.at[j]) for j, (s, d) in enumerate(pairs)]

    relay = (xn, yn) if c == 1 else (yn, xn)

    def first():
        start(local())
        for p in parts:
            start(copies(0, me, sib, p, own=True) + copies(1, me, xn, p, own=True) + copies(2, me, yn, p, own=True))

    def hand_on():
        for p in parts:
            arrived(1, xn, p)
            start(copies(4, xn, sib, p))
            if c == 1:
                start(copies(3, *relay, p))
            arrived(2, yn, p)
            start(copies(5, yn, sib, p))
            if c == 0:
                start(copies(3, *relay, p))

    def finish():
        for p in parts:
            arrived(3, dg, p)
            start(copies(6, dg, sib, p))
        for p in parts:
            arrived(0, sib, p)
            arrived(4, (1 - x, y, 1 - c), p)
            arrived(5, (x, 1 - y, 1 - c), p)
            arrived(6, (1 - x, 1 - y, 1 - c), p)
        for p in parts:
            sent = (copies(0, me, sib, p, own=True) + copies(1, me, xn, p, own=True) + copies(2, me, yn, p, own=True)
                    + copies(3, *relay, p) + copies(4, xn, sib, p) + copies(5, yn, sib, p) + copies(6, dg, sib, p))
            for cp in sent:
                cp.wait_send()
        for cp in local():
            cp.wait()

    return first, hand_on, finish


def _gather_sems(n_small):
    n_arr = 2 * GATHER_PARTS + n_small
    return [pltpu.SemaphoreType.DMA((7, n_arr)), pltpu.SemaphoreType.DMA((7, n_arr)), pltpu.SemaphoreType.DMA((n_arr + 1,))]


def _gather_weights(win, head, tail, wout, cw, zeros):
    small_in = (wout, cw, head, tail)
    n_zero = zeros.shape[0]
    assert W_IN + n_zero == W_ROWS and W_IN % GROUP == 0

    def body(win_ref, wout_ref, cw_ref, head_ref, tail_ref, z_ref, w_ref, gout_ref, gcw_ref, ghead_ref, gtail_ref, *sems):
        x, y, c = _place()
        me = 4 * x + 2 * y + c
        small = ((wout_ref, gout_ref), (cw_ref, gcw_ref), (head_ref, ghead_ref), (tail_ref, gtail_ref))

        def run(k):
            for stage in _gather_stages(k, win_ref, small, z_ref, n_zero, w_ref, *sems):
                stage()

        for k in range(N_DEV):
            pl.when(me == k)(functools.partial(run, k))

    anyspec = pl.BlockSpec(memory_space=pl.ANY)
    return pl.pallas_call(
        body, name="gather_weights", in_specs=[anyspec] * 6, out_specs=[anyspec] * 5,
        out_shape=[jax.ShapeDtypeStruct((W_ROWS, D), win.dtype)]
        + [jax.ShapeDtypeStruct((N_DEV,) + a.shape, a.dtype) for a in small_in],
        scratch_shapes=_gather_sems(len(small_in)),
    )(win, wout, cw, head, tail, zeros)


_REL = [(dx, dy, dc) for dx in (0, 1) for dy in (0, 1) for dc in (0, 1)][1:]
_HBM = pl.BlockSpec(memory_space=pltpu.HBM)
_SEM = pl.BlockSpec(memory_space=pltpu.SEMAPHORE)
_EFFECT = pltpu.SideEffectType.DATAFLOW_SIDE_EFFECTING


def _peer(k):
    x, y, c = _place()
    dx, dy, dc = _REL[k]
    return (1 - x if dx else x, 1 - y if dy else y, 1 - c if dc else c)


def _exchange_start(parts, *, name):
    n = len(parts)

    def body(*refs):
        ins, lands = refs[:n], refs[n:2 * n]
        send_sems, recv_sems, token = refs[2 * n], refs[2 * n + 1], refs[-1]
        x, y, c = _place()
        me = 4 * x + 2 * y + c
        for a in range(n):
            for k in range(len(_REL)):
                px, py, pc = _peer(k)
                pltpu.make_async_remote_copy(
                    src_ref=ins[a].at[4 * px + 2 * py + pc], dst_ref=lands[a].at[me],
                    send_sem=send_sems.at[len(_REL) * a + k], recv_sem=recv_sems.at[len(_REL) * a + k],
                    device_id=(px, py, pc), device_id_type=MESH).start()
        token[...] = jnp.zeros_like(token)

    sem = pltpu.SemaphoreType.DMA((len(_REL) * n,))
    bufs = [pltpu.HBM(p.shape, p.dtype) for p in parts]
    outs = pl.pallas_call(
        body, name=name,
        out_shape=(sem, sem, *bufs, *bufs, jax.ShapeDtypeStruct((8, LANE), F32)),
        in_specs=(_HBM,) * (2 * n), out_specs=(_SEM, _SEM, *(_HBM,) * (2 * n), pl.BlockSpec(memory_space=pltpu.VMEM)),
        input_output_aliases={i: 2 + i for i in range(2 * n)},
        compiler_params=pltpu.CompilerParams(has_side_effects=_EFFECT),
    )(*[pltpu.with_memory_space_constraint(p, pltpu.HBM) for p in parts],
      *[pltpu.with_memory_space_constraint(lax.empty(p.shape, p.dtype), pltpu.HBM) for p in parts])
    return outs[0], outs[1], outs[2:2 + n], outs[2 + n:2 + 2 * n], outs[-1]


def _exchange_wait(send_sems, recv_sems, parts, lands, after, *, name):
    n = len(parts)

    def body(*refs):
        ins, lands_ = refs[:n], refs[n:2 * n]
        ssem, rsem = refs[2 * n], refs[2 * n + 1]
        for a in range(n):
            for k in range(len(_REL)):
                px, py, pc = _peer(k)
                p = 4 * px + 2 * py + pc
                cp = pltpu.make_async_remote_copy(
                    src_ref=ins[a].at[p], dst_ref=lands_[a].at[p],
                    send_sem=ssem.at[len(_REL) * a + k], recv_sem=rsem.at[len(_REL) * a + k],
                    device_id=(px, py, pc), device_id_type=MESH)
                cp.wait_send()
                cp.wait_recv()

    bufs = [pltpu.HBM(p.shape, p.dtype) for p in parts]
    outs = pl.pallas_call(
        body, name=name, out_shape=(*bufs, *bufs),
        in_specs=(*(_HBM,) * (2 * n), _SEM, _SEM, pl.BlockSpec(memory_space=pl.ANY)), out_specs=(_HBM,) * (2 * n),
        input_output_aliases={i: i for i in range(2 * n)},
        compiler_params=pltpu.CompilerParams(has_side_effects=_EFFECT),
    )(*parts, *lands, send_sems, recv_sems, after)
    return outs[:n], outs[n:]


WEIGHTS = ('norm_w', 'w_in', 'gate_b', 'sgu_norm_g', 'sgu_norm_b', 'sgu_w', 'sgu_b', 'conv_w', 'conv_b', 'dt_bias', 'A_log',
           'D_skip', 'ssd_norm_w', 'w_out', 'final_norm_w')
SHARDED = ('w_in', 'conv_w', 'w_out')
PACK_ROW = 8 * LANE


def _constants():
    tri = np.tril(np.ones((CHUNK, CHUNK), np.float32))
    expand = np.zeros((DT_W, D), np.float32)
    for h in range(HEADS):
        expand[h, h * HEADDIM:(h + 1) * HEADDIM] = 1.0
    sel = np.zeros((D, LANE), np.float32)
    for g in range(SGU_GROUPS):
        sel[g * LANE:(g + 1) * LANE, g] = 1.0
    pos_chunk = np.arange(SGU_BLOCK) // CHUNK
    mask = (pos_chunk[None, :] <= pos_chunk[:, None]).astype(np.float32)
    shift = np.zeros(((CONV_K - 1) * CHUNK, HALO_BLK + CHUNK), np.float32)
    for kk in range(CONV_K - 1):
        for t in range(CHUNK):
            shift[kk * CHUNK + t, HALO_BLK - (CONV_K - 1) + t + kk] = 1.0
    return dict(tri=jnp.asarray(tri, BF16), triT=jnp.asarray(tri.T.copy(), BF16), expand=jnp.asarray(np.tile(expand, (3, 1)), BF16),
                shift=jnp.asarray(shift, BF16),
                expandT=jnp.asarray(expand.T.copy(), BF16), sel=jnp.asarray(sel), mask=jnp.asarray(mask))


def _to_shards(segs):
    starts = np.cumsum([0] + [n for _, n in segs])
    assert starts[-1] == W_IN
    slabs = []
    for k in range(N_DEV):
        pieces = []
        for (s, n), s0 in zip(segs, starts[:-1]):
            lo, hi = max(k * SHARD_IN, s0), min((k + 1) * SHARD_IN, s0 + n)
            if lo < hi:
                pieces.append(s[lo - s0:hi - s0])
        slabs.append(jnp.concatenate(pieces, axis=0))
    return jnp.stack(slabs)


def _local_step(x2, tgt, wpT, wout, cw, p, exchange_small, exchange):
    S = x2.shape[0]
    k = _constants()
    xn, proj = _in_proj(x2, p['norm_w'], wpT, tm=min(1024, S), tn=2048)
    wm32 = p['sgu_w'][0] * k['mask']
    wm = wm32.astype(BF16)
    wmT = jnp.swapaxes(wm32, 1, 2).astype(BF16)
    bias_full = jnp.repeat(p['sgu_b'][0].T, LANE, axis=1)
    tm_sgu = min(512, S)
    ya = _sgu_fwd(proj, p['sgu_norm_g'], p['sgu_norm_b'], wm, bias_full, tm=tm_sgu)
    pad32 = lambda a: jnp.pad(a, ((0, 0), (0, DT_W - HEADS)))
    dtb_p, alog_p = pad32(p['dt_bias']), pad32(p['A_log'])
    d_exp = jnp.repeat(p['D_skip'], HEADDIM, axis=1)
    ssd_args = (cw, p['conv_b'], dtb_p, alog_p, d_exp, p['ssd_norm_w'])
    y, yb, states = _ssd_fwd(proj, *ssd_args, k['tri'], k['expand'], k['shift'])
    dh, dhb, mb, dya, dyb, dgl, loss, dfw, dgb = _head(
        x2, ya, yb, proj, tgt, p['gate_b'], wout, p['final_norm_w'][None, :], tm=min(256, S))
    dsgu, dws, dbsT, dsg, dsb = _sgu_bwd(proj, dya, p['sgu_norm_g'], p['sgu_norm_b'], wm, wmT, bias_full, k['mask'], k['sel'],
                                         tm=tm_sgu)
    dssd, dcw, dcb, ddtb, dalog, dD, dnw = _ssd_bwd(proj, dyb, y, states, *ssd_args, k['tri'], k['triT'], k['expand'], k['expandT'],
                                                    k['shift'])
    grads = dict(
        gate_b=dgb[0:1], sgu_norm_g=dsg[0:1], sgu_norm_b=dsb[0:1], sgu_w=dws[None],
        sgu_b=dbsT[:, :SGU_GROUPS].T[None], conv_w=dcw[0:CONV_K][None], conv_b=dcb[0:1], dt_bias=ddtb[0:1, :HEADS],
        A_log=dalog[0:1, :HEADS], D_skip=dD[0:1, :HEADS], ssd_norm_w=dnw[0:1], final_norm_w=dfw[0])
    tw = dict(trans_a=True, out_dtype=BF16, tm=1024, tn=512, tk=S)
    dw_out = _matmul(mb, dhb, name="dw_out", **tw)
    token = exchange_small(loss[0, 0], grads, dw_out)
    dwT_sgu = _matmul(dsgu, xn, after=token, name="dw_in_sgu", **tw)
    dwT_gate = _matmul(dgl, xn, name="dw_in_gate", **tw)
    dwT_ssd = _matmul(dssd, xn, name="dw_in_ssd", **tw)
    token = exchange([(dwT_sgu, SEG_SGU[1]), (dwT_ssd, W_IN - SEG_SSD[0]), (dwT_gate, SEG_GATE[1])])
    tm, tn = min(1024, S), 1024
    dxn = _matmul(dsgu, wpT, tm=tm, tn=512, tk=SEG_SGU[1], after=token, name="dxn_sgu")
    dxn = _matmul(dgl, wpT, b_koff=SEG_GATE[0] // 2048, tm=tm, tn=tn, tk=2048, add=dxn, name="dxn_gate")
    dxn = _matmul(dssd, wpT, b_koff=SEG_SSD[0] // 2048, tm=tm, tn=tn, tk=2048, add=dxn, name="dxn_ssd")
    grad_x, dnorm = _norm_bwd(x2, p['norm_w'], dxn, dh, tm=min(256, S))
    return grad_x, dnorm[0:1]


def _pack(arrs):
    rows, offs, r = [], [], 0
    for a in arrs:
        n = a.size
        nr = -(-n // PACK_ROW) * 8
        rows.append(jnp.pad(a.reshape(-1).astype(F32), (0, nr * LANE - n)).reshape(nr, LANE))
        offs.append(r)
        r += nr
    return jnp.concatenate(rows, axis=0), offs


def kernel(x, norm_w, w_in, gate_b, sgu_norm_g, sgu_norm_b, sgu_w, sgu_b, conv_w, conv_b, dt_bias, A_log, D_skip, ssd_norm_w, w_out, final_norm_w, loss_target, m_norm_w, m_w_in, m_gate_b, m_sgu_norm_g, m_sgu_norm_b, m_sgu_w, m_sgu_b, m_conv_w, m_conv_b, m_dt_bias, m_A_log, m_D_skip, m_ssd_norm_w, m_w_out, m_final_norm_w, v_norm_w, v_w_in, v_gate_b, v_sgu_norm_g, v_sgu_norm_b, v_sgu_w, v_sgu_b, v_conv_w, v_conv_b, v_dt_bias, v_A_log, v_D_skip, v_ssd_norm_w, v_w_out, v_final_norm_w):
    w = dict(norm_w=norm_w, w_in=w_in, gate_b=gate_b, sgu_norm_g=sgu_norm_g, sgu_norm_b=sgu_norm_b, sgu_w=sgu_w, sgu_b=sgu_b,
             conv_w=conv_w, conv_b=conv_b, dt_bias=dt_bias, A_log=A_log, D_skip=D_skip, ssd_norm_w=ssd_norm_w, w_out=w_out,
             final_norm_w=final_norm_w)
    m = dict(norm_w=m_norm_w, w_in=m_w_in, gate_b=m_gate_b, sgu_norm_g=m_sgu_norm_g, sgu_norm_b=m_sgu_norm_b, sgu_w=m_sgu_w,
             sgu_b=m_sgu_b, conv_w=m_conv_w, conv_b=m_conv_b, dt_bias=m_dt_bias, A_log=m_A_log, D_skip=m_D_skip,
             ssd_norm_w=m_ssd_norm_w, w_out=m_w_out, final_norm_w=m_final_norm_w)
    v = dict(norm_w=v_norm_w, w_in=v_w_in, gate_b=v_gate_b, sgu_norm_g=v_sgu_norm_g, sgu_norm_b=v_sgu_norm_b, sgu_w=v_sgu_w,
             sgu_b=v_sgu_b, conv_w=v_conv_w, conv_b=v_conv_b, dt_bias=v_dt_bias, A_log=v_A_log, D_skip=v_D_skip,
             ssd_norm_w=v_ssd_norm_w, w_out=v_w_out, final_norm_w=v_final_norm_w)
    me = 4 * lax.axis_index("x") + 2 * lax.axis_index("y") + lax.axis_index("c")
    shard_cw = XBC_W // N_DEV

    tpose = lambda a: jnp.swapaxes(a[0], 0, 1)
    wT = tpose(w_in).astype(BF16)
    first_group = (GROUP - (me * SHARD_IN) % GROUP) % GROUP
    window = lax.dynamic_slice(jnp.pad(wT, ((0, GROUP), (0, 0))), (first_group, 0), (INTERIOR, D))
    wpT, g_out, g_cw, heads, tails = _gather_weights(window, wT[:GROUP], wT[SHARD_IN - GROUP:], w_out[0].astype(BF16),
                                                     conv_w[0], jnp.zeros((W_ROWS - W_IN, D), BF16))
    wpT = _patch_straddlers(wpT, heads, tails)
    wout_full = g_out.reshape(D, D)
    cw_full = jnp.swapaxes(g_cw, 0, 1).reshape(CONV_K, XBC_W)

    flight = {}

    small = [n for n in WEIGHTS if n not in SHARDED and n != 'norm_w']
    early = {}

    def exchange_small(loss_part, grads, dw_out):
        early['packed'], early['offs'] = _pack([grads[n] for n in small] + [loss_part, grads['conv_w']])
        parts = [jnp.broadcast_to(early['packed'][None], (N_DEV,) + early['packed'].shape), dw_out.reshape(N_DEV, D // N_DEV, D)]
        early['sems'], early['rsems'], early['parts'], early['lands'], token = _exchange_start(parts, name="small_start")
        return token

    def exchange(dw_inT_segs):
        parts = [_to_shards(dw_inT_segs)]
        flight['sems'], flight['rsems'], flight['parts'], flight['lands'], token = _exchange_start(parts, name="exchange_start")
        return token

    grad_x, dnorm = _local_step(x[0], loss_target[0], wpT, wout_full, cw_full, w, exchange_small, exchange)
    (_, own_out), (land_small, land_out) = _exchange_wait(
        early['sems'], early['rsems'], early['parts'], early['lands'], grad_x, name="small_wait")
    (own_in,), (land_in,) = _exchange_wait(
        flight['sems'], flight['rsems'], flight['parts'], flight['lands'], grad_x, name="exchange_wait")
    me_arr = jnp.reshape(me, (1,)).astype(jnp.int32)
    res = {}
    res['w_in'] = [jnp.swapaxes(o, 0, 1) for o in _adamw_own(
        me_arr, own_in, land_in, tpose(w_in), tpose(m_w_in), tpose(v_w_in), tr=SHARD_IN, tc=256, name="adamw_w_in")]
    res['w_out'] = _adamw_own(me_arr, own_out, land_out, w_out[0], m_w_out[0], v_w_out[0], tr=128, tc=D, name="adamw_w_out")

    (norm_parts,) = _all_gather([_pack([dnorm])[0]], name="gather_norm")
    norm_outs = _adamw(norm_parts, *[_pack([d['norm_w']])[0] for d in (w, m, v)], tr=norm_parts.shape[1], name="adamw_norm")
    res['norm_w'] = [o.reshape(-1)[:D].reshape(w['norm_w'].shape) for o in norm_outs]

    offs = early['offs']
    gathered = lax.dynamic_update_slice(land_small, early['packed'][None], (me, 0, 0))
    off_loss, off_cw = offs[-2], offs[-1]
    cw_parts = gathered[:, off_cw:, :].reshape(N_DEV, CONV_K, XBC_W)
    cw_parts = lax.dynamic_slice_in_dim(cw_parts, me * shard_cw, shard_cw, axis=2)
    cw_rows = _pack([cw_parts[0]])[0].shape[0]
    cw_parts = jnp.pad(cw_parts.reshape(N_DEV, -1), ((0, 0), (0, cw_rows * LANE - CONV_K * shard_cw))).reshape(N_DEV, cw_rows, LANE)
    parts = jnp.concatenate([gathered[:, :off_cw, :], cw_parts], axis=1)
    zero = jnp.zeros((), F32)
    packs = [_pack([d[n] for n in small] + [zero, d['conv_w']])[0] for d in (w, m, v)]
    outs = _adamw(parts, *packs, tr=parts.shape[1], name="adamw_small")

    def unpack(o, name):
        if name == 'conv_w':
            return o[off_cw:off_cw + cw_rows].reshape(-1)[:CONV_K * shard_cw].reshape(w['conv_w'].shape)
        r0 = offs[small.index(name)]
        n = w[name].size
        return o[r0:r0 + -(-n // PACK_ROW) * 8].reshape(-1)[:n].reshape(w[name].shape)

    for n in small + ['conv_w']:
        res[n] = [unpack(o, n) for o in outs]
    for n in ('w_in', 'w_out'):
        res[n] = [o[None] for o in res[n]]
    loss = outs[0][off_loss, 0]
    return (loss, grad_x[None], *[res[n][0] for n in WEIGHTS], *[res[n][1] for n in WEIGHTS],
            *[res[n][2] for n in WEIGHTS], *[res[n][3] for n in WEIGHTS])
```

```python
import functools

import numpy as np
import jax
import jax.numpy as jnp
from jax import lax
from jax.experimental import pallas as pl
from jax.experimental.pallas import tpu as pltpu

F32 = jnp.float32
BF16 = jnp.bfloat16
HI = lax.Precision.HIGHEST
MESH = pl.DeviceIdType.MESH

D = 2048
EPS = 1e-5
SGU_BLOCK = 128
SGU_GROUPS = 16
CHUNK = 64
HEADS = 32
HEADDIM = 64
SSD_GROUPS = 4
GROUP_W = D // SSD_GROUPS
STATE = 128
CONV_K = 4
XBC_W = D + 2 * SSD_GROUPS * STATE
W_IN = 15392
N_DEV = 8
SHARD_IN = W_IN // N_DEV
ADAM_LR, ADAM_B1, ADAM_B2, ADAM_EPS, ADAM_WD, ADAM_STEP = 0.001, 0.9, 0.999, 1e-08, 0.01, 10

REF_SGU_END = 3 * D
REF_GATE_START = W_IN - 2 * D
LANE = 128
DT_W = LANE
OFF_U, OFF_V, OFF_ZA, OFF_G0, OFF_G1, OFF_ZB = (i * D for i in range(6))
OFF_XBC = OFF_ZB + D
OFF_DT = OFF_XBC + XBC_W
SEG_SGU = (OFF_U, 3 * D)
SEG_GATE = (OFF_G0, 2 * D)
SEG_SSD = (OFF_ZB, D + XBC_W + DT_W)
WP = SEG_SSD[0] + SEG_SSD[1]
SSD_PAD_W = 3 * D
VMEM_BYTES = 64 * 1024 * 1024
VMEM_LIMIT = VMEM_BYTES - 8 * 1024 * 1024


def _cp(sem=None, vmem=VMEM_LIMIT):
    return pltpu.CompilerParams(dimension_semantics=sem, vmem_limit_bytes=vmem)


def _sigmoid(x):
    return 1.0 / (1.0 + jnp.exp(-x))


def _softplus(x):
    return jnp.maximum(x, 0.0) + jnp.log(1.0 + jnp.exp(-jnp.abs(x)))


def _dot(a, b, precision=None):
    return jnp.dot(a, b, preferred_element_type=F32, precision=precision)


def _dot_nt(a, b, precision=None):
    return lax.dot_general(a, b, (((1,), (1,)), ((), ())), preferred_element_type=F32, precision=precision)


def _dot_tn(a, b, precision=None):
    return lax.dot_general(a, b, (((0,), (0,)), ((), ())), preferred_element_type=F32, precision=precision)


def _split3(a):
    hi = a.astype(BF16)
    r = a - hi.astype(F32)
    mid = r.astype(BF16)
    return hi, mid, (r - mid.astype(F32)).astype(BF16)


def _sel_right(a, sel01):
    m = a.shape[0]
    r = _dot(jnp.concatenate(_split3(a), axis=0), sel01)
    return (r[0:m] + r[m:2 * m]) + r[2 * m:3 * m]


def _sel_right_k(a, sel01_x3):
    return _dot(jnp.concatenate(_split3(a), axis=1), sel01_x3)


def _sel_left(sel01, a):
    n = a.shape[1]
    r = _dot(sel01, jnp.concatenate(_split3(a), axis=1))
    return (r[:, 0:n] + r[:, n:2 * n]) + r[:, 2 * n:3 * n]


def _matmul(a, b, *, trans_a=False, trans_b=False, b_koff=0, out_dtype=F32, tm, tn, tk, add=None, after=None, name):
    K, M = a.shape if trans_a else a.shape[::-1]
    N = b.shape[0] if trans_b else b.shape[1]
    assert M % tm == 0 and N % tn == 0 and K % tk == 0 and not (trans_a and trans_b), (name, M, N, K, tm, tn, tk)
    nk = K // tk

    def body(*refs):
        a_ref, b_ref = refs[:2]
        add_ref = refs[2] if add is not None else None
        o_ref, acc_ref = refs[-2:]
        k = pl.program_id(2)
        if trans_a:
            part = _dot_tn(a_ref[...], b_ref[...])
        else:
            part = _dot_nt(a_ref[...], b_ref[...]) if trans_b else _dot(a_ref[...], b_ref[...])

        def result(r):
            if add_ref is not None:
                r = r + add_ref[...]
            return r.astype(out_dtype)

        if nk == 1:
            o_ref[...] = result(part)
        else:
            @pl.when(k == 0)
            def _():
                acc_ref[...] = part

            @pl.when(jnp.logical_and(k > 0, k < nk - 1))
            def _():
                acc_ref[...] += part

            @pl.when(k == nk - 1)
            def _():
                o_ref[...] = result(acc_ref[...] + part)

    in_specs = [pl.BlockSpec((tk, tm), lambda i, j, k: (k, i)) if trans_a else pl.BlockSpec((tm, tk), lambda i, j, k: (i, k)),
                pl.BlockSpec((tn, tk), lambda i, j, k: (j, k)) if trans_b else pl.BlockSpec((tk, tn), lambda i, j, k: (k + b_koff, j))]
    args = [a, b]
    if add is not None:
        in_specs.append(pl.BlockSpec((tm, tn), lambda i, j, k: (i, j)))
        args.append(add)
    if after is not None:
        in_specs.append(pl.BlockSpec(memory_space=pl.ANY))
        args.append(after)
    return pl.pallas_call(
        body, name=name, grid=(M // tm, N // tn, nk), in_specs=in_specs,
        out_specs=pl.BlockSpec((tm, tn), lambda i, j, k: (i, j)),
        out_shape=jax.ShapeDtypeStruct((M, N), out_dtype),
        scratch_shapes=[pltpu.VMEM((tm, tn), F32)],
        compiler_params=_cp(("parallel", "parallel", "arbitrary")),
    )(*args)


def _in_proj(x, w, wpT, *, tm, tn):
    S = x.shape[0]
    N = wpT.shape[0]
    assert S % tm == 0 and N % tn == 0, (S, N, tm, tn)

    def body(x_ref, w_ref, b_ref, xn_ref, o_ref, xs_ref):
        @pl.when(pl.program_id(1) == 0)
        def _():
            xv = x_ref[...]
            r = lax.rsqrt(jnp.mean(xv * xv, axis=-1, keepdims=True) + EPS)
            xs = (xv * r * w_ref[...]).astype(BF16)
            xs_ref[...] = xs
            xn_ref[...] = xs

        o_ref[...] = _dot_nt(xs_ref[...], b_ref[...]).astype(BF16)

    return pl.pallas_call(
        body, name="in_proj", grid=(S // tm, N // tn),
        in_specs=[pl.BlockSpec((tm, D), lambda i, j: (i, 0)), pl.BlockSpec((1, D), lambda i, j: (0, 0)),
                  pl.BlockSpec((tn, D), lambda i, j: (j, 0))],
        out_specs=[pl.BlockSpec((tm, D), lambda i, j: (i, 0)), pl.BlockSpec((tm, tn), lambda i, j: (i, j))],
        out_shape=[jax.ShapeDtypeStruct((S, D), BF16), jax.ShapeDtypeStruct((S, N), BF16)],
        scratch_shapes=[pltpu.VMEM((tm, D), BF16)],
        compiler_params=_cp(("parallel", "arbitrary")),
    )(x, w, wpT)


def _norm_bwd(x, w, dxn, dh, *, tm):
    S = x.shape[0]

    def body(x_ref, w_ref, dxn_ref, dh_ref, gx_ref, dw_ref):
        xv = x_ref[...]
        r = lax.rsqrt(jnp.mean(xv * xv, axis=-1, keepdims=True) + EPS)
        xh = xv * r
        dxn_v = dxn_ref[...]
        dxh = dxn_v * w_ref[...]
        gx_ref[...] = dh_ref[...] + r * (dxh - xh * jnp.mean(dxh * xh, axis=-1, keepdims=True))

        @pl.when(pl.program_id(0) == 0)
        def _():
            dw_ref[...] = jnp.zeros_like(dw_ref)

        dw_ref[0:1, :] += jnp.sum(dxn_v * xh, axis=0, keepdims=True)

    row = pl.BlockSpec((tm, D), lambda i: (i, 0))
    return pl.pallas_call(
        body, name="norm_bwd", grid=(S // tm,),
        in_specs=[row, pl.BlockSpec((1, D), lambda i: (0, 0)), row, row],
        out_specs=[row, pl.BlockSpec((8, D), lambda i: (0, 0))],
        out_shape=[jax.ShapeDtypeStruct((S, D), F32), jax.ShapeDtypeStruct((8, D), F32)],
        compiler_params=_cp(("arbitrary",)),
    )(x, w, dxn, dh)


def _sgu_core(u_ref, v_ref, z_ref, g_ref, b_ref, wm_ref, bias_ref, vnb_ref, mixed_ref, tm):
    v = v_ref[...].astype(F32)
    mu = jnp.mean(v, axis=-1, keepdims=True)
    vc = v - mu
    rs = lax.rsqrt(jnp.mean(vc * vc, axis=-1, keepdims=True) + EPS)
    vh = vc * rs
    vnb_ref[...] = (vh * g_ref[...] + b_ref[...]).astype(BF16)
    for blk in range(tm // SGU_BLOCK):
        rows = pl.ds(blk * SGU_BLOCK, SGU_BLOCK)
        for gi in range(SGU_GROUPS):
            cols = pl.ds(gi * LANE, LANE)
            mixed_ref[rows, cols] = _dot(wm_ref[gi], vnb_ref[rows, cols]) + bias_ref[:, cols]
    return vh, rs


def _sgu_fwd(proj, g, b, wm, bias_full, *, tm):
    S = proj.shape[0]

    def body(u_ref, v_ref, z_ref, g_ref, b_ref, wm_ref, bias_ref, y_ref, vnb_ref, mixed_ref):
        _sgu_core(u_ref, v_ref, z_ref, g_ref, b_ref, wm_ref, bias_ref, vnb_ref, mixed_ref, tm)
        z = z_ref[...].astype(F32)
        y_ref[...] = (u_ref[...].astype(F32) * mixed_ref[...] * (z * _sigmoid(z))).astype(BF16)

    seg = lambda off: pl.BlockSpec((tm, D), lambda i: (i, off // D))
    full = lambda a: pl.BlockSpec(a.shape, lambda i: (0,) * a.ndim)
    return pl.pallas_call(
        body, name="sgu_fwd", grid=(S // tm,),
        in_specs=[seg(OFF_U), seg(OFF_V), seg(OFF_ZA), full(g), full(b), full(wm), full(bias_full)],
        out_specs=pl.BlockSpec((tm, D), lambda i: (i, 0)),
        out_shape=jax.ShapeDtypeStruct((S, D), BF16),
        scratch_shapes=[pltpu.VMEM((tm, D), BF16), pltpu.VMEM((tm, D), F32)],
        compiler_params=_cp(("parallel",)),
    )(proj, proj, proj, g, b, wm, bias_full)


def _sgu_bwd(proj, dy, g, b, wm, wmT, bias_full, mask, sel, *, tm):
    S = proj.shape[0]
    nsteps = S // tm

    def body(u_ref, v_ref, z_ref, dy_ref, g_ref, b_ref, wm_ref, wmT_ref, bias_ref, mask_ref, sel_ref,
             dp_ref, dws_ref, dbs_ref, dg_ref, db_ref, vnb_ref, mixed_ref, dmb_ref, dvn_ref, dbias_ref):
        i = pl.program_id(0)

        @pl.when(i == 0)
        def _():
            dws_ref[...] = jnp.zeros_like(dws_ref)
            dg_ref[...] = jnp.zeros_like(dg_ref)
            db_ref[...] = jnp.zeros_like(db_ref)
            dbias_ref[...] = jnp.zeros_like(dbias_ref)

        vh, rs = _sgu_core(u_ref, v_ref, z_ref, g_ref, b_ref, wm_ref, bias_ref, vnb_ref, mixed_ref, tm)
        u = u_ref[...].astype(F32)
        z = z_ref[...].astype(F32)
        dy_v = dy_ref[...].astype(F32)
        mixed = mixed_ref[...]
        sg = _sigmoid(z)
        sz = z * sg
        dp_ref[:, 0:D] = (dy_v * mixed * sz).astype(BF16)
        dp_ref[:, 2 * D:3 * D] = (dy_v * u * mixed * (sg * (1.0 + z * (1.0 - sg)))).astype(BF16)
        dmixed = dy_v * u * sz
        dmb_ref[...] = dmixed.astype(BF16)
        for blk in range(tm // SGU_BLOCK):
            dbias_ref[...] += dmixed[blk * SGU_BLOCK:(blk + 1) * SGU_BLOCK, :]
        for blk in range(tm // SGU_BLOCK):
            rows = pl.ds(blk * SGU_BLOCK, SGU_BLOCK)
            for gi in range(SGU_GROUPS):
                cols = pl.ds(gi * LANE, LANE)
                dm = dmb_ref[rows, cols]
                dvn_ref[rows, cols] = _dot(wmT_ref[gi], dm)
                dws_ref[gi] += _dot_nt(dm, vnb_ref[rows, cols])
        dvn = dvn_ref[...]
        dg_ref[0:1, :] += jnp.sum(dvn * vh, axis=0, keepdims=True)
        db_ref[0:1, :] += jnp.sum(dvn, axis=0, keepdims=True)
        dvh = dvn * g_ref[...]
        dv = rs * (dvh - jnp.mean(dvh, axis=-1, keepdims=True) - vh * jnp.mean(dvh * vh, axis=-1, keepdims=True))
        dp_ref[:, D:2 * D] = dv.astype(BF16)

        @pl.when(i == nsteps - 1)
        def _():
            for gi in range(SGU_GROUPS):
                dws_ref[gi] = dws_ref[gi] * mask_ref[...]
            dbs_ref[...] = _dot(dbias_ref[...], sel_ref[...], precision=HI)

    seg = lambda off: pl.BlockSpec((tm, D), lambda i: (i, off // D))
    full = lambda a: pl.BlockSpec(a.shape, lambda i: (0,) * a.ndim)
    return pl.pallas_call(
        body, name="sgu_bwd", grid=(nsteps,),
        in_specs=[seg(OFF_U), seg(OFF_V), seg(OFF_ZA), pl.BlockSpec((tm, D), lambda i: (i, 0)),
                  full(g), full(b), full(wm), full(wmT), full(bias_full), full(mask), full(sel)],
        out_specs=[pl.BlockSpec((tm, 3 * D), lambda i: (i, 0)),
                   pl.BlockSpec((SGU_GROUPS, SGU_BLOCK, SGU_BLOCK), lambda i: (0, 0, 0)),
                   pl.BlockSpec((SGU_BLOCK, LANE), lambda i: (0, 0)),
                   pl.BlockSpec((8, D), lambda i: (0, 0)), pl.BlockSpec((8, D), lambda i: (0, 0))],
        out_shape=[jax.ShapeDtypeStruct((S, 3 * D), BF16),
                   jax.ShapeDtypeStruct((SGU_GROUPS, SGU_BLOCK, SGU_BLOCK), F32),
                   jax.ShapeDtypeStruct((SGU_BLOCK, LANE), F32),
                   jax.ShapeDtypeStruct((8, D), F32), jax.ShapeDtypeStruct((8, D), F32)],
        scratch_shapes=[pltpu.VMEM((tm, D), BF16), pltpu.VMEM((tm, D), F32), pltpu.VMEM((tm, D), BF16),
                        pltpu.VMEM((tm, D), F32), pltpu.VMEM((SGU_BLOCK, D), F32)],
        compiler_params=_cp(("arbitrary",)),
    )(proj, proj, proj, dy, g, b, wm, wmT, bias_full, mask, sel)


SSD_T = 2 * CHUNK
HALO = 8
HALO_BLK = 16


def _pair_masks():
    row = lax.broadcasted_iota(jnp.int32, (CHUNK, LANE), 0)
    lane = lax.broadcasted_iota(jnp.int32, (CHUNK, LANE), 1)
    pos = jnp.where(lane >= CHUNK, lane - CHUNK, lane)
    diag = (row == pos).astype(F32)
    causal = row >= pos
    lo = (lane < CHUNK).astype(F32)
    return diag, causal, lo, 1.0 - lo


def _ssd_chunk_fwd(c, ext_ref, shift_ref, dt_ref, cw_ref, cb_ref, dtb_ref, alog_ref, tri_ref, exp_ref):
    r0 = c * CHUNK
    win = ext_ref[pl.ds(r0, HALO_BLK + CHUNK), :]
    sh = _dot(shift_ref[...], win)
    taps = [sh[k * CHUNK:(k + 1) * CHUNK] for k in range(CONV_K - 1)] + [win[HALO_BLK:].astype(F32)]
    pre = cb_ref[...] + sum(cw_ref[k:k + 1, :] * taps[k] for k in range(CONV_K))
    sg = _sigmoid(pre)
    xc = pre * sg
    dtr = dt_ref[pl.ds(r0, CHUNK), :].astype(F32) + dtb_ref[...]
    dtv = _softplus(dtr)
    A = -jnp.exp(alog_ref[...])
    acs = _sel_left(tri_ref[...], dtv * A)
    both = _sel_right_k(jnp.concatenate([acs, dtv], axis=0), exp_ref[...])
    E, dtE = both[0:CHUNK], both[CHUNK:2 * CHUNK]
    return dict(taps=taps, pre=pre, sg=sg, xc=xc, dtr=dtr, dtv=dtv, A=A, E=E, dtE=dtE)


def _ssd_fwd(proj, conv_w, conv_b, dtb_p, alog_p, d_exp, norm_w, tri, expand, shift):
    S = proj.shape[0]
    T = SSD_T
    nsteps = S // T
    ncl = T // CHUNK

    def body(zb_ref, xbc_ref, halo_ref, dt_ref, cw_ref, cb_ref, dtb_ref, alog_ref, dexp_ref, nw_ref, tri_ref, exp_ref, shift_ref,
             y_ref, yb_ref, st_ref, ht_ref, ext_ref):
        i = pl.program_id(0)

        @pl.when(i == 0)
        def _():
            ht_ref[...] = jnp.zeros_like(ht_ref)
            ext_ref[0:HALO_BLK, :] = jnp.zeros((HALO_BLK, XBC_W), BF16)

        @pl.when(i > 0)
        def _():
            ext_ref[0:HALO_BLK, :] = halo_ref[...]

        ext_ref[HALO_BLK:HALO_BLK + T, :] = xbc_ref[...]
        diag, causal, lo, hi = _pair_masks()
        for c in range(ncl):
            q = _ssd_chunk_fwd(c, ext_ref, shift_ref, dt_ref, cw_ref, cb_ref, dtb_ref, alog_ref, tri_ref, exp_ref)
            rows = pl.ds(c * CHUNK, CHUNK)
            xc, E, dtE = q["xc"], q["E"], q["dtE"]
            xs = xc[:, 0:D]
            total = E[CHUNK - 1:CHUNK, :]
            x_dt = xs * dtE
            eE = jnp.exp(E)
            xw = x_dt * jnp.exp(total - E)
            st_ref[c] = ht_ref[...]
            for g in range(SSD_GROUPS):
                gc = slice(g * GROUP_W, (g + 1) * GROUP_W)
                Bg = xc[:, D + g * STATE:D + (g + 1) * STATE].astype(BF16)
                Cg = xc[:, D + SSD_GROUPS * STATE + g * STATE:D + SSD_GROUPS * STATE + (g + 1) * STATE].astype(BF16)
                cb2 = _dot_nt(Cg, jnp.concatenate([Bg, Bg], axis=0))
                htg = ht_ref[:, gc]
                y_ref[rows, gc] = eE[:, gc] * _dot(Cg, htg.astype(BF16)) + xs[:, gc] * dexp_ref[:, gc]
                for jj in range(GROUP_W // LANE):
                    pc = slice(g * GROUP_W + jj * LANE, g * GROUP_W + (jj + 1) * LANE)
                    Ej = E[:, pc]
                    e2 = jnp.sum(Ej * diag, axis=0, keepdims=True)
                    Mp = cb2 * jnp.exp(jnp.where(causal, Ej - e2, -1e30))
                    xj = x_dt[:, pc]
                    xbd = jnp.concatenate([xj * lo, xj * hi], axis=0).astype(BF16)
                    y_ref[rows, pc] += _dot(Mp.astype(BF16), xbd)
                ht_ref[:, gc] = jnp.exp(total[:, gc]) * htg + _dot_tn(Bg, xw[:, gc].astype(BF16))
            zb = zb_ref[rows, :].astype(F32)
            hh = y_ref[rows, :] * (zb * _sigmoid(zb))
            for g in range(SSD_GROUPS):
                gc = slice(g * GROUP_W, (g + 1) * GROUP_W)
                hg = hh[:, gc]
                r = lax.rsqrt(jnp.mean(hg * hg, axis=-1, keepdims=True) + EPS)
                yb_ref[rows, gc] = (hg * r * nw_ref[:, gc]).astype(BF16)

    full = lambda a: pl.BlockSpec(a.shape, lambda i: (0,) * a.ndim)
    hb = T // HALO_BLK
    return pl.pallas_call(
        body, name="ssd_fwd", grid=(nsteps,),
        in_specs=[pl.BlockSpec((T, D), lambda i: (i, OFF_ZB // D)),
                  pl.BlockSpec((T, XBC_W), lambda i: (i, OFF_XBC // XBC_W)),
                  pl.BlockSpec((HALO_BLK, XBC_W), lambda i: (jnp.maximum(i * hb - 1, 0), OFF_XBC // XBC_W)),
                  pl.BlockSpec((T, DT_W), lambda i: (i, OFF_DT // DT_W)),
                  full(conv_w), full(conv_b), full(dtb_p), full(alog_p), full(d_exp), full(norm_w), full(tri), full(expand),
                  full(shift)],
        out_specs=[pl.BlockSpec((T, D), lambda i: (i, 0)), pl.BlockSpec((T, D), lambda i: (i, 0)),
                   pl.BlockSpec((ncl, STATE, D), lambda i: (i, 0, 0))],
        out_shape=[jax.ShapeDtypeStruct((S, D), F32), jax.ShapeDtypeStruct((S, D), BF16),
                   jax.ShapeDtypeStruct((S // CHUNK, STATE, D), F32)],
        scratch_shapes=[pltpu.VMEM((STATE, D), F32), pltpu.VMEM((HALO_BLK + T, XBC_W), BF16)],
        compiler_params=_cp(("arbitrary",)),
    )(proj, proj, proj, proj, conv_w, conv_b, dtb_p, alog_p, d_exp, norm_w, tri, expand, shift)


def _ssd_bwd(proj, dyb, y, states, conv_w, conv_b, dtb_p, alog_p, d_exp, norm_w, tri, triT, expand, expandT, shift):
    S = proj.shape[0]
    T = SSD_T
    nsteps = S // T
    ncl = T // CHUNK
    SSD_W = SSD_PAD_W

    def body(zb_ref, xbc_ref, halo_ref, dt_ref, dyb_ref, y_ref, st_ref, cw_ref, cb_ref, dtb_ref, alog_ref, dexp_ref, nw_ref,
             tri_ref, triT_ref, exp_ref, expT_ref, shift_ref,
             dp_ref, dcw_ref, dcb_ref, ddtb_ref, dalog_ref, dD_ref, dnw_ref,
             dht_ref, ext_ref, dpre_ref, dy_s, dE_s, dxdt_s, dxc_s, dDacc_ref, dAacc_ref):
        i = pl.program_id(0)

        @pl.when(i == 0)
        def _():
            for r in (dht_ref, dcw_ref, dcb_ref, ddtb_ref, dnw_ref, dDacc_ref, dAacc_ref):
                r[...] = jnp.zeros_like(r)
            dpre_ref[T:T + HALO_BLK, :] = jnp.zeros((HALO_BLK, XBC_W), F32)

        @pl.when(i == nsteps - 1)
        def _():
            ext_ref[0:HALO_BLK, :] = jnp.zeros((HALO_BLK, XBC_W), BF16)

        @pl.when(i < nsteps - 1)
        def _():
            ext_ref[0:HALO_BLK, :] = halo_ref[...]

        ext_ref[HALO_BLK:HALO_BLK + T, :] = xbc_ref[...]
        diag, causal, lo, hi = _pair_masks()
        last_row = (lax.broadcasted_iota(jnp.int32, (CHUNK, 1), 0) == CHUNK - 1).astype(F32)
        for c in reversed(range(ncl)):
            q = _ssd_chunk_fwd(c, ext_ref, shift_ref, dt_ref, cw_ref, cb_ref, dtb_ref, alog_ref, tri_ref, exp_ref)
            rows = pl.ds(c * CHUNK, CHUNK)
            pre, sg, xc, dtr, dtv, A, E, dtE = (q[k] for k in ("pre", "sg", "xc", "dtr", "dtv", "A", "E", "dtE"))
            xs = xc[:, 0:D]
            total = E[CHUNK - 1:CHUNK, :]
            x_dt = xs * dtE
            eE = jnp.exp(E)
            wdec = jnp.exp(total - E)
            zb = zb_ref[rows, :].astype(F32)
            yv = y_ref[rows, :]
            sgz = _sigmoid(zb)
            sz = zb * sgz
            hh = yv * sz
            for g in range(SSD_GROUPS):
                gc = slice(g * GROUP_W, (g + 1) * GROUP_W)
                hg = hh[:, gc]
                r = lax.rsqrt(jnp.mean(hg * hg, axis=-1, keepdims=True) + EPS)
                dyb_g = dyb_ref[rows, gc].astype(F32)
                dn = dyb_g * nw_ref[:, gc]
                dnw_ref[0:1, gc] += jnp.sum(dyb_g * hg * r, axis=0, keepdims=True)
                dy_s[:, gc] = r * dn - hg * (r * r * r) * jnp.mean(dn * hg, axis=-1, keepdims=True)
            dhh = dy_s[...]
            dp_ref[rows, 0:D] = (dhh * yv * (sgz * (1.0 + zb * (1.0 - sgz)))).astype(BF16)
            dy = dhh * sz
            dy_s[...] = dy
            dDacc_ref[0:1, :] += jnp.sum(dy * xs, axis=0, keepdims=True)
            dxc_s[:, 0:D] = dy * dexp_ref[...]
            for g in range(SSD_GROUPS):
                gc = slice(g * GROUP_W, (g + 1) * GROUP_W)
                bcol = slice(D + g * STATE, D + (g + 1) * STATE)
                ccol = slice(D + SSD_GROUPS * STATE + g * STATE, D + SSD_GROUPS * STATE + (g + 1) * STATE)
                Bg = xc[:, bcol].astype(BF16)
                Cg = xc[:, ccol].astype(BF16)
                B2 = jnp.concatenate([Bg, Bg], axis=0)
                cb2 = _dot_nt(Cg, B2)
                htg = st_ref[c, :, gc]
                htb = htg.astype(BF16)
                dhn = dht_ref[:, gc]
                dhnb = dhn.astype(BF16)
                dyg = dy[:, gc]
                eEg = eE[:, gc]
                wg = wdec[:, gc]
                xdg = x_dt[:, gc]
                CH = _dot(Cg, htb)
                dCHb = (dyg * eEg).astype(BF16)
                dC = _dot_nt(dCHb, htb)
                dl = jnp.exp(total[:, gc])
                dht_prev = _dot_tn(Cg, dCHb) + dl * dhn
                dtot = jnp.sum(dhn * htg, axis=0, keepdims=True) * dl
                dxw = _dot(Bg, dhnb)
                dB = _dot_nt((xdg * wg).astype(BF16), dhnb)
                dwd = dxw * xdg * wg
                dtot = dtot + jnp.sum(dwd, axis=0, keepdims=True)
                dE_s[:, gc] = dyg * eEg * CH - dwd + last_row * dtot
                dxdt_s[:, gc] = dxw * wg
                dcb2 = jnp.zeros((CHUNK, LANE), F32)
                for jj in range(GROUP_W // LANE):
                    pc = slice(g * GROUP_W + jj * LANE, g * GROUP_W + (jj + 1) * LANE)
                    Ej = E[:, pc]
                    e2 = jnp.sum(Ej * diag, axis=0, keepdims=True)
                    Lp = jnp.exp(jnp.where(causal, Ej - e2, -1e30))
                    Mp = cb2 * Lp
                    xj = x_dt[:, pc]
                    xbd = jnp.concatenate([xj * lo, xj * hi], axis=0).astype(BF16)
                    dyj = dy[:, pc].astype(BF16)
                    dMp = _dot_nt(dyj, xbd)
                    dxbd = _dot_tn(Mp.astype(BF16), dyj)
                    dxdt_s[:, pc] += dxbd[0:CHUNK, :] * lo + dxbd[CHUNK:2 * CHUNK, :] * hi
                    dcb2 = dcb2 + dMp * Lp
                    dseg = dMp * Mp
                    dE_s[:, pc] += dseg - diag * jnp.sum(dseg, axis=0, keepdims=True)
                dcb2b = dcb2.astype(BF16)
                dC = dC + _dot(dcb2b, B2)
                dB2 = _dot_tn(dcb2b, Cg)
                dB = dB + dB2[0:CHUNK, :] + dB2[CHUNK:2 * CHUNK, :]
                dxc_s[:, bcol] = dB
                dxc_s[:, ccol] = dC
                dht_ref[:, gc] = dht_prev
            dx_dt = dxdt_s[...]
            dxc_s[:, 0:D] += dx_dt * dtE
            red = _sel_right(jnp.concatenate([dE_s[...], dx_dt * xs], axis=0), expT_ref[...])
            da = _sel_left(triT_ref[...], red[0:CHUNK, :])
            ddtv = red[CHUNK:2 * CHUNK, :] + da * A
            dAacc_ref[0:1, :] += jnp.sum(da * dtv, axis=0, keepdims=True)
            ddtr = ddtv * _sigmoid(dtr)
            ddtb_ref[0:1, :] += jnp.sum(ddtr, axis=0, keepdims=True)
            dp_ref[rows, D + XBC_W:D + XBC_W + DT_W] = ddtr.astype(BF16)
            dpre = dxc_s[...] * (sg * (1.0 + pre * (1.0 - sg)))
            dpre_ref[rows, :] = dpre
            dcb_ref[0:1, :] += jnp.sum(dpre, axis=0, keepdims=True)
            for k in range(CONV_K):
                dcw_ref[k:k + 1, :] += jnp.sum(dpre * q["taps"][k], axis=0, keepdims=True)
        dxbc = jnp.zeros((T, XBC_W), F32)
        for k in range(CONV_K):
            dxbc = dxbc + cw_ref[k:k + 1, :] * dpre_ref[pl.ds(CONV_K - 1 - k, T), :]
        dp_ref[:, D:D + XBC_W] = dxbc.astype(BF16)
        dp_ref[:, SEG_SSD[1]:SSD_W] = jnp.zeros((T, SSD_W - SEG_SSD[1]), BF16)
        dpre_ref[T:T + HALO, :] = dpre_ref[0:HALO, :]

        @pl.when(i == nsteps - 1)
        def _():
            dalog_ref[...] = dAacc_ref[...] * (-jnp.exp(alog_ref[...]))
            dD_ref[...] = _dot(dDacc_ref[...], expT_ref[...].astype(F32), precision=HI)

    full = lambda a: pl.BlockSpec(a.shape, lambda i: (0,) * a.ndim)
    hb = T // HALO_BLK
    rev = lambda i: nsteps - 1 - i
    acc = lambda w: pl.BlockSpec((8, w), lambda i: (0, 0))
    return pl.pallas_call(
        body, name="ssd_bwd", grid=(nsteps,),
        in_specs=[pl.BlockSpec((T, D), lambda i: (rev(i), OFF_ZB // D)),
                  pl.BlockSpec((T, XBC_W), lambda i: (rev(i), OFF_XBC // XBC_W)),
                  pl.BlockSpec((HALO_BLK, XBC_W), lambda i: (jnp.maximum(rev(i) * hb - 1, 0), OFF_XBC // XBC_W)),
                  pl.BlockSpec((T, DT_W), lambda i: (rev(i), OFF_DT // DT_W)),
                  pl.BlockSpec((T, D), lambda i: (rev(i), 0)), pl.BlockSpec((T, D), lambda i: (rev(i), 0)),
                  pl.BlockSpec((ncl, STATE, D), lambda i: (rev(i), 0, 0)),
                  full(conv_w), full(conv_b), full(dtb_p), full(alog_p), full(d_exp), full(norm_w),
                  full(tri), full(triT), full(expand), full(expandT), full(shift)],
        out_specs=[pl.BlockSpec((T, SSD_W), lambda i: (rev(i), 0)),
                   acc(XBC_W), acc(XBC_W), acc(DT_W), acc(DT_W), acc(DT_W), acc(D)],
        out_shape=[jax.ShapeDtypeStruct((S, SSD_W), BF16),
                   jax.ShapeDtypeStruct((8, XBC_W), F32), jax.ShapeDtypeStruct((8, XBC_W), F32),
                   jax.ShapeDtypeStruct((8, DT_W), F32), jax.ShapeDtypeStruct((8, DT_W), F32),
                   jax.ShapeDtypeStruct((8, DT_W), F32), jax.ShapeDtypeStruct((8, D), F32)],
        scratch_shapes=[pltpu.VMEM((STATE, D), F32), pltpu.VMEM((HALO_BLK + T, XBC_W), BF16), pltpu.VMEM((T + HALO_BLK, XBC_W), F32),
                        pltpu.VMEM((CHUNK, D), F32), pltpu.VMEM((CHUNK, D), F32), pltpu.VMEM((CHUNK, D), F32),
                        pltpu.VMEM((CHUNK, XBC_W), F32), pltpu.VMEM((8, D), F32), pltpu.VMEM((8, DT_W), F32)],
        compiler_params=_cp(("arbitrary",)),
    )(proj, proj, proj, proj, dyb, y, states, conv_w, conv_b, dtb_p, alog_p, d_exp, norm_w, tri, triT, expand, expandT, shift)


def _head(x, ya, yb, proj, target, gate_b, wout, fw, *, tm):
    S = x.shape[0]

    def body(x_ref, ya_ref, yb_ref, gl0_ref, gl1_ref, t_ref, gb_ref, w_ref, fw_ref,
             dh_ref, dhb_ref, mb_ref, dya_ref, dyb_ref, dgl_ref, loss_ref, dfw_ref, dgb_ref):
        @pl.when(pl.program_id(0) == 0)
        def _():
            loss_ref[...] = jnp.zeros_like(loss_ref)
            dfw_ref[...] = jnp.zeros_like(dfw_ref)
            dgb_ref[...] = jnp.zeros_like(dgb_ref)

        ya_v = ya_ref[...].astype(F32)
        yb_v = yb_ref[...].astype(F32)
        g0 = _sigmoid(gl0_ref[...].astype(F32) + gb_ref[:, 0:D])
        g1 = _sigmoid(gl1_ref[...].astype(F32) + gb_ref[:, D:2 * D])
        mb = (g0 * ya_v + g1 * yb_v).astype(BF16)
        mb_ref[...] = mb
        h = x_ref[...] + _dot(mb, w_ref[...])
        r = lax.rsqrt(jnp.mean(h * h, axis=-1, keepdims=True) + EPS)
        hn = h * r
        err = hn * fw_ref[...] - t_ref[...]
        loss_ref[...] += 0.5 * jnp.sum(jnp.mean(err * err, axis=-1, keepdims=True))
        dyf = err * (1.0 / D)
        dfw_ref[0:1, :] += jnp.sum(dyf * hn, axis=0, keepdims=True)
        dhn = dyf * fw_ref[...]
        dh = r * (dhn - hn * jnp.mean(dhn * hn, axis=-1, keepdims=True))
        dh_ref[...] = dh
        dhb = dh.astype(BF16)
        dhb_ref[...] = dhb
        dm = _dot_nt(dhb, w_ref[...])
        dya_ref[...] = (dm * g0).astype(BF16)
        dyb_ref[...] = (dm * g1).astype(BF16)
        dgl0 = dm * ya_v * g0 * (1.0 - g0)
        dgl1 = dm * yb_v * g1 * (1.0 - g1)
        dgl_ref[:, 0:D] = dgl0.astype(BF16)
        dgl_ref[:, D:2 * D] = dgl1.astype(BF16)
        dgb_ref[0:1, 0:D] += jnp.sum(dgl0, axis=0, keepdims=True)
        dgb_ref[0:1, D:2 * D] += jnp.sum(dgl1, axis=0, keepdims=True)

    row = pl.BlockSpec((tm, D), lambda i: (i, 0))
    seg = lambda off: pl.BlockSpec((tm, D), lambda i: (i, off // D))
    full = lambda a: pl.BlockSpec(a.shape, lambda i: (0,) * a.ndim)
    acc = lambda w: pl.BlockSpec((8, w), lambda i: (0, 0))
    return pl.pallas_call(
        body, name="head", grid=(S // tm,),
        in_specs=[row, row, row, seg(OFF_G0), seg(OFF_G1), row, full(gate_b), full(wout), full(fw)],
        out_specs=[row, row, row, row, row, pl.BlockSpec((tm, 2 * D), lambda i: (i, 0)), acc(LANE), acc(D), acc(2 * D)],
        out_shape=[jax.ShapeDtypeStruct((S, D), F32), jax.ShapeDtypeStruct((S, D), BF16), jax.ShapeDtypeStruct((S, D), BF16),
                   jax.ShapeDtypeStruct((S, D), BF16), jax.ShapeDtypeStruct((S, D), BF16), jax.ShapeDtypeStruct((S, 2 * D), BF16),
                   jax.ShapeDtypeStruct((8, LANE), F32), jax.ShapeDtypeStruct((8, D), F32), jax.ShapeDtypeStruct((8, 2 * D), F32)],
        compiler_params=_cp(("arbitrary",)),
    )(x, ya, yb, proj, proj, target, gate_b, wout, fw)


def _adam_update(g, w_ref, m_ref, v_ref, g_ref, d_ref, m2_ref, v2_ref):
    m2 = ADAM_B1 * m_ref[...] + (1.0 - ADAM_B1) * g
    v2 = ADAM_B2 * v_ref[...] + (1.0 - ADAM_B2) * (g * g)
    m_hat = m2 / (1.0 - ADAM_B1 ** ADAM_STEP)
    v_hat = v2 / (1.0 - ADAM_B2 ** ADAM_STEP)
    g_ref[...] = g
    d_ref[...] = -ADAM_LR * (m_hat / (jnp.sqrt(v_hat) + ADAM_EPS) + ADAM_WD * w_ref[...])
    m2_ref[...] = m2
    v2_ref[...] = v2


def _adamw_own(me, own, landed, w, m, v, *, tr, tc, name):
    _, R, C = landed.shape
    assert R % tr == 0 and C % tc == 0, (name, R, C, tr, tc)

    def body(me_ref, own_ref, p_ref, w_ref, m_ref, v_ref, g_ref, d_ref, m2_ref, v2_ref):
        mine = own_ref[0].astype(F32)
        g = jnp.where(me_ref[0] == 0, mine, p_ref[0].astype(F32))
        for k in range(1, N_DEV):
            g = g + jnp.where(me_ref[0] == k, mine, p_ref[k].astype(F32))
        _adam_update(g, w_ref, m_ref, v_ref, g_ref, d_ref, m2_ref, v2_ref)

    tile = pl.BlockSpec((tr, tc), lambda i, j, me_ref: (i, j))
    return pl.pallas_call(
        body, name=name,
        grid_spec=pltpu.PrefetchScalarGridSpec(
            num_scalar_prefetch=1, grid=(R // tr, C // tc),
            in_specs=[pl.BlockSpec((1, tr, tc), lambda i, j, me_ref: (me_ref[0], i, j)),
                      pl.BlockSpec((N_DEV, tr, tc), lambda i, j, me_ref: (0, i, j)), tile, tile, tile],
            out_specs=[tile, tile, tile, tile]),
        out_shape=[jax.ShapeDtypeStruct((R, C), F32)] * 4,
        compiler_params=_cp(("parallel", "parallel")),
    )(me, own, landed, w, m, v)


def _adamw(parts, w, m, v, *, tr, name):
    _, R, C = parts.shape
    assert R % tr == 0, (name, R, tr)

    def body(p_ref, w_ref, m_ref, v_ref, g_ref, d_ref, m2_ref, v2_ref):
        g = p_ref[0].astype(F32)
        for k in range(1, N_DEV):
            g = g + p_ref[k].astype(F32)
        _adam_update(g, w_ref, m_ref, v_ref, g_ref, d_ref, m2_ref, v2_ref)

    row = pl.BlockSpec((tr, C), lambda i: (i, 0))
    return pl.pallas_call(
        body, name=name, grid=(R // tr,),
        in_specs=[pl.BlockSpec((N_DEV, tr, C), lambda i: (0, i, 0)), row, row, row],
        out_specs=[row, row, row, row],
        out_shape=[jax.ShapeDtypeStruct((R, C), F32)] * 4,
        compiler_params=_cp(("parallel",)),
    )(parts, w, m, v)


def _place():
    x, y, c = lax.axis_index("x"), lax.axis_index("y"), lax.axis_index("c")
    return x, y, c


def _all_gather(arrs, *, name):
    n = len(arrs)

    def body(*refs):
        ins, outs = refs[:n], refs[n:2 * n]
        send_sems, recv_sems, local_sems = refs[2 * n:]
        x, y, c = _place()
        me, sibling = (x, y, c), (x, y, 1 - c)
        chips = [(1 - x, y), (x, 1 - y), (1 - x, 1 - y)]

        def idx(px, py, pc):
            return 4 * px + 2 * py + pc

        def copy(k, a, block, to, src=None):
            slab = outs[a].at[idx(*block)]
            return pltpu.make_async_remote_copy(
                src_ref=slab if src is None else src, dst_ref=slab,
                send_sem=send_sems.at[k, a], recv_sem=recv_sems.at[k, a], device_id=to, device_id_type=MESH)

        mine = [pltpu.make_async_copy(ins[a], outs[a].at[idx(*me)], local_sems.at[a]) for a in range(n)]
        for cp in mine:
            cp.start()
        first = []
        for a in range(n):
            first.append(copy(0, a, me, sibling, src=ins[a]))
            first += [copy(1 + j, a, me, (*chip, c), src=ins[a]) for j, chip in enumerate(chips)]
        for cp in first:
            cp.start()
        passed = []
        for j, chip in enumerate(chips):
            for a in range(n):
                copy(1 + j, a, (*chip, c), me).wait_recv()
                fwd = copy(4 + j, a, (*chip, c), sibling)
                fwd.start()
                passed.append(fwd)
        for a in range(n):
            copy(0, a, sibling, me).wait_recv()
            for j, chip in enumerate(chips):
                copy(4 + j, a, (*chip, 1 - c), me).wait_recv()
        for cp in first + passed:
            cp.wait_send()
        for cp in mine:
            cp.wait()

    anyspec = pl.BlockSpec(memory_space=pl.ANY)
    return pl.pallas_call(
        body, name=name,
        in_specs=[anyspec] * n, out_specs=[anyspec] * n,
        out_shape=[jax.ShapeDtypeStruct((N_DEV,) + a.shape, a.dtype) for a in arrs],
        scratch_shapes=[pltpu.SemaphoreType.DMA((7, n)), pltpu.SemaphoreType.DMA((7, n)), pltpu.SemaphoreType.DMA((n,))],
    )(*arrs)


W_ROWS = SEG_SSD[0] + SSD_PAD_W


GROUP = 16
INTERIOR = 1920


def _interior(k):
    lo = -(-(k * SHARD_IN) // GROUP) * GROUP
    hi = ((k + 1) * SHARD_IN) // GROUP * GROUP
    return lo, hi


def _dest_row(r):
    if r < REF_SGU_END:
        return r
    return r - REF_SGU_END + SEG_SSD[0] if r < REF_GATE_START else r - REF_GATE_START + SEG_GATE[0]


def _shard_pieces(k):
    lo_k, hi_k = _interior(k)
    out = []
    for lo, hi in ((0, REF_SGU_END), (REF_SGU_END, REF_GATE_START), (REF_GATE_START, W_IN)):
        a, b = max(lo, lo_k), min(hi, hi_k)
        if a < b:
            out.append((a - lo_k, b - a, _dest_row(a)))
    return out


GATHER_PARTS = 1


def _shard_parts(k):
    parts = [[] for _ in range(GATHER_PARTS)]
    for s0, n, d0 in _shard_pieces(k):
        step = -(-(n // GROUP) // GATHER_PARTS) * GROUP
        for p in range(GATHER_PARTS):
            a, b = min(p * step, n), min((p + 1) * step, n)
            if a < b:
                parts[p].append((s0 + a, b - a, d0 + a))
    return parts


def _patch_straddlers(wpT, heads, tails):
    for k in range(1, N_DEV):
        m = (k * SHARD_IN) % GROUP
        if m:
            group = jnp.concatenate([tails[k - 1, GROUP - m:], heads[k, :GROUP - m]], axis=0)
            wpT = lax.dynamic_update_slice(wpT, group, (_dest_row(k * SHARD_IN - m), 0))
    return wpT


def _gather_stages(k, win_ref, small, z_ref, n_zero, w_ref, send_sems, recv_sems, local_sems, stage_ref, stage_sems):
    x, y, c = k // 4, (k // 2) % 2, k % 2
    idx = lambda p: 4 * p[0] + 2 * p[1] + p[2]
    me, sib = (x, y, c), (x, y, 1 - c)
    xn, yn, dg = (1 - x, y, c), (x, 1 - y, c), (1 - x, 1 - y, c)
    parts = range(GATHER_PARTS)

    stage_of = {0: 0, 4: 1, 5: 2, 6: 3}

    def stage(slot, block, part, own=False):
        cps = [pltpu.make_async_copy(win_ref.at[pl.ds(s0, n)] if own else w_ref.at[pl.ds(d0, n)],
                                     stage_ref.at[stage_of[slot], pl.ds(s0, n)], stage_sems.at[j])
               for j, (s0, n, d0) in enumerate(_shard_parts(idx(block))[part])]
        for cp in cps:
            cp.start()
        for cp in cps:
            cp.wait()

    def copies(slot, block, to, part, own=False):
        kb = idx(block)
        out = []
        for j, (s0, n, d0) in enumerate(_shard_parts(kb)[part]):
            dst = w_ref.at[pl.ds(d0, n)]
            if slot in stage_of:
                src = stage_ref.at[stage_of[slot], pl.ds(s0, n)]
            else:
                src = win_ref.at[pl.ds(s0, n)] if own else dst
            out.append((src, dst, 2 * part + j))
        if part == 0:
            for j, (src, gathered) in enumerate(small):
                out.append((src if own else gathered.at[kb], gathered.at[kb], 2 * GATHER_PARTS + j))
        return [pltpu.make_async_remote_copy(src_ref=s, dst_ref=d, send_sem=send_sems.at[slot, j], recv_sem=recv_sems.at[slot, j],
                                             device_id=to, device_id_type=MESH) for s, d, j in out]

    def start(cps):
        for cp in cps:
            cp.start()

    def arrived(slot, block, part):
        for cp in copies(slot, block, me, part):
            cp.wait_recv()

    def local():
        pairs = [(win_ref.at[pl.ds(s0, n)], w_ref.at[pl.ds(d0, n)]) for s0, n, d0 in _shard_pieces(k)]
        pairs += [(src, gathered.at[k]) for src, gathered in small] + [(z_ref, w_ref.at[pl.ds(W_IN, n_zero)])]
        return [pltpu.make_async_copy(s, d, local_sems.at[j]) for j, (s, d) in enumerate(pairs)]

    relay = (xn, yn) if c == 1 else (yn, xn)

    def first():
        start(local())
        for p in parts:
            start(copies(1, me, xn, p, own=True) + copies(2, me, yn, p, own=True))
            stage(0, me, p, own=True)
            start(copies(0, me, sib, p, own=True))

    def hand_on():
        for p in parts:
            arrived(1, xn, p)
            if c == 1:
                start(copies(3, *relay, p))
            stage(4, xn, p)
            start(copies(4, xn, sib, p))
            arrived(2, yn, p)
            if c == 0:
                start(copies(3, *relay, p))
            stage(5, yn, p)
            start(copies(5, yn, sib, p))

    def finish():
        for p in parts:
            arrived(3, dg, p)
            stage(6, dg, p)
            start(copies(6, dg, sib, p))
        for p in parts:
            arrived(0, sib, p)
            arrived(4, (1 - x, y, 1 - c), p)
            arrived(5, (x, 1 - y, 1 - c), p)
            arrived(6, (1 - x, 1 - y, 1 - c), p)
        for p in parts:
            sent = (copies(0, me, sib, p, own=True) + copies(1, me, xn, p, own=True) + copies(2, me, yn, p, own=True)
                    + copies(3, *relay, p) + copies(4, xn, sib, p) + copies(5, yn, sib, p) + copies(6, dg, sib, p))
            for cp in sent:
                cp.wait_send()
        for cp in local():
            cp.wait()

    return first, hand_on, finish


def _gather_scratch(n_small, dtype):
    n_arr = 2 * GATHER_PARTS + n_small
    return [pltpu.SemaphoreType.DMA((7, n_arr)), pltpu.SemaphoreType.DMA((7, n_arr)), pltpu.SemaphoreType.DMA((n_arr + 1,)),
            pltpu.VMEM((4, INTERIOR, D), dtype), pltpu.SemaphoreType.DMA((2,))]


def _gather_weights(win, head, tail, wout, cw, zeros):
    small_in = (wout, cw, head, tail)
    n_zero = zeros.shape[0]
    assert W_IN + n_zero == W_ROWS and W_IN % GROUP == 0

    def body(win_ref, wout_ref, cw_ref, head_ref, tail_ref, z_ref, w_ref, gout_ref, gcw_ref, ghead_ref, gtail_ref, *sems):
        x, y, c = _place()
        me = 4 * x + 2 * y + c
        small = ((wout_ref, gout_ref), (cw_ref, gcw_ref), (head_ref, ghead_ref), (tail_ref, gtail_ref))

        def run(k):
            for stage in _gather_stages(k, win_ref, small, z_ref, n_zero, w_ref, *sems):
                stage()

        for k in range(N_DEV):
            pl.when(me == k)(functools.partial(run, k))

    anyspec = pl.BlockSpec(memory_space=pl.ANY)
    return pl.pallas_call(
        body, name="gather_weights", in_specs=[anyspec] * 6, out_specs=[anyspec] * 5,
        out_shape=[jax.ShapeDtypeStruct((W_ROWS, D), win.dtype)]
        + [jax.ShapeDtypeStruct((N_DEV,) + a.shape, a.dtype) for a in small_in],
        scratch_shapes=_gather_scratch(len(small_in), win.dtype),
        compiler_params=_cp(),
    )(win, wout, cw, head, tail, zeros)


_REL = [(dx, dy, dc) for dx in (0, 1) for dy in (0, 1) for dc in (0, 1)][1:]
_HBM = pl.BlockSpec(memory_space=pltpu.HBM)
_SEM = pl.BlockSpec(memory_space=pltpu.SEMAPHORE)
_EFFECT = pltpu.SideEffectType.DATAFLOW_SIDE_EFFECTING


def _peer(k):
    x, y, c = _place()
    dx, dy, dc = _REL[k]
    return (1 - x if dx else x, 1 - y if dy else y, 1 - c if dc else c)


def _exchange_start(parts, *, name):
    n = len(parts)

    def body(*refs):
        ins, lands = refs[:n], refs[n:2 * n]
        send_sems, recv_sems, token = refs[2 * n], refs[2 * n + 1], refs[-1]
        x, y, c = _place()
        me = 4 * x + 2 * y + c
        for a in range(n):
            for k in range(len(_REL)):
                px, py, pc = _peer(k)
                pltpu.make_async_remote_copy(
                    src_ref=ins[a].at[4 * px + 2 * py + pc], dst_ref=lands[a].at[me],
                    send_sem=send_sems.at[len(_REL) * a + k], recv_sem=recv_sems.at[len(_REL) * a + k],
                    device_id=(px, py, pc), device_id_type=MESH).start()
        token[...] = jnp.zeros_like(token)

    sem = pltpu.SemaphoreType.DMA((len(_REL) * n,))
    bufs = [pltpu.HBM(p.shape, p.dtype) for p in parts]
    outs = pl.pallas_call(
        body, name=name,
        out_shape=(sem, sem, *bufs, *bufs, jax.ShapeDtypeStruct((8, LANE), F32)),
        in_specs=(_HBM,) * (2 * n), out_specs=(_SEM, _SEM, *(_HBM,) * (2 * n), pl.BlockSpec(memory_space=pltpu.VMEM)),
        input_output_aliases={i: 2 + i for i in range(2 * n)},
        compiler_params=pltpu.CompilerParams(has_side_effects=_EFFECT),
    )(*[pltpu.with_memory_space_constraint(p, pltpu.HBM) for p in parts],
      *[pltpu.with_memory_space_constraint(lax.empty(p.shape, p.dtype), pltpu.HBM) for p in parts])
    return outs[0], outs[1], outs[2:2 + n], outs[2 + n:2 + 2 * n], outs[-1]


def _exchange_wait(send_sems, recv_sems, parts, lands, after, *, name):
    n = len(parts)

    def body(*refs):
        ins, lands_ = refs[:n], refs[n:2 * n]
        ssem, rsem = refs[2 * n], refs[2 * n + 1]
        for a in range(n):
            for k in range(len(_REL)):
                px, py, pc = _peer(k)
                p = 4 * px + 2 * py + pc
                cp = pltpu.make_async_remote_copy(
                    src_ref=ins[a].at[p], dst_ref=lands_[a].at[p],
                    send_sem=ssem.at[len(_REL) * a + k], recv_sem=rsem.at[len(_REL) * a + k],
                    device_id=(px, py, pc), device_id_type=MESH)
                cp.wait_send()
                cp.wait_recv()

    bufs = [pltpu.HBM(p.shape, p.dtype) for p in parts]
    outs = pl.pallas_call(
        body, name=name, out_shape=(*bufs, *bufs),
        in_specs=(*(_HBM,) * (2 * n), _SEM, _SEM, pl.BlockSpec(memory_space=pl.ANY)), out_specs=(_HBM,) * (2 * n),
        input_output_aliases={i: i for i in range(2 * n)},
        compiler_params=pltpu.CompilerParams(has_side_effects=_EFFECT),
    )(*parts, *lands, send_sems, recv_sems, after)
    return outs[:n], outs[n:]


WEIGHTS = ('norm_w', 'w_in', 'gate_b', 'sgu_norm_g', 'sgu_norm_b', 'sgu_w', 'sgu_b', 'conv_w', 'conv_b', 'dt_bias', 'A_log',
           'D_skip', 'ssd_norm_w', 'w_out', 'final_norm_w')
SHARDED = ('w_in', 'conv_w', 'w_out')
PACK_ROW = 8 * LANE


def _constants():
    tri = np.tril(np.ones((CHUNK, CHUNK), np.float32))
    expand = np.zeros((DT_W, D), np.float32)
    for h in range(HEADS):
        expand[h, h * HEADDIM:(h + 1) * HEADDIM] = 1.0
    sel = np.zeros((D, LANE), np.float32)
    for g in range(SGU_GROUPS):
        sel[g * LANE:(g + 1) * LANE, g] = 1.0
    pos_chunk = np.arange(SGU_BLOCK) // CHUNK
    mask = (pos_chunk[None, :] <= pos_chunk[:, None]).astype(np.float32)
    shift = np.zeros(((CONV_K - 1) * CHUNK, HALO_BLK + CHUNK), np.float32)
    for kk in range(CONV_K - 1):
        for t in range(CHUNK):
            shift[kk * CHUNK + t, HALO_BLK - (CONV_K - 1) + t + kk] = 1.0
    return dict(tri=jnp.asarray(tri, BF16), triT=jnp.asarray(tri.T.copy(), BF16), expand=jnp.asarray(np.tile(expand, (3, 1)), BF16),
                shift=jnp.asarray(shift, BF16),
                expandT=jnp.asarray(expand.T.copy(), BF16), sel=jnp.asarray(sel), mask=jnp.asarray(mask))


def _to_shards(segs):
    starts = np.cumsum([0] + [n for _, n in segs])
    assert starts[-1] == W_IN
    slabs = []
    for k in range(N_DEV):
        pieces = []
        for (s, n), s0 in zip(segs, starts[:-1]):
            lo, hi = max(k * SHARD_IN, s0), min((k + 1) * SHARD_IN, s0 + n)
            if lo < hi:
                pieces.append(s[lo - s0:hi - s0])
        slabs.append(jnp.concatenate(pieces, axis=0))
    return jnp.stack(slabs)


def _local_step(x2, tgt, wpT, wout, cw, p, exchange_small, exchange):
    S = x2.shape[0]
    k = _constants()
    xn, proj = _in_proj(x2, p['norm_w'], wpT, tm=min(1024, S), tn=2048)
    wm32 = p['sgu_w'][0] * k['mask']
    wm = wm32.astype(BF16)
    wmT = jnp.swapaxes(wm32, 1, 2).astype(BF16)
    bias_full = jnp.repeat(p['sgu_b'][0].T, LANE, axis=1)
    tm_sgu = min(512, S)
    ya = _sgu_fwd(proj, p['sgu_norm_g'], p['sgu_norm_b'], wm, bias_full, tm=tm_sgu)
    pad32 = lambda a: jnp.pad(a, ((0, 0), (0, DT_W - HEADS)))
    dtb_p, alog_p = pad32(p['dt_bias']), pad32(p['A_log'])
    d_exp = jnp.repeat(p['D_skip'], HEADDIM, axis=1)
    ssd_args = (cw, p['conv_b'], dtb_p, alog_p, d_exp, p['ssd_norm_w'])
    y, yb, states = _ssd_fwd(proj, *ssd_args, k['tri'], k['expand'], k['shift'])
    dh, dhb, mb, dya, dyb, dgl, loss, dfw, dgb = _head(
        x2, ya, yb, proj, tgt, p['gate_b'], wout, p['final_norm_w'][None, :], tm=min(256, S))
    dsgu, dws, dbsT, dsg, dsb = _sgu_bwd(proj, dya, p['sgu_norm_g'], p['sgu_norm_b'], wm, wmT, bias_full, k['mask'], k['sel'],
                                         tm=tm_sgu)
    dssd, dcw, dcb, ddtb, dalog, dD, dnw = _ssd_bwd(proj, dyb, y, states, *ssd_args, k['tri'], k['triT'], k['expand'], k['expandT'],
                                                    k['shift'])
    grads = dict(
        gate_b=dgb[0:1], sgu_norm_g=dsg[0:1], sgu_norm_b=dsb[0:1], sgu_w=dws[None],
        sgu_b=dbsT[:, :SGU_GROUPS].T[None], conv_w=dcw[0:CONV_K][None], conv_b=dcb[0:1], dt_bias=ddtb[0:1, :HEADS],
        A_log=dalog[0:1, :HEADS], D_skip=dD[0:1, :HEADS], ssd_norm_w=dnw[0:1], final_norm_w=dfw[0])
    tw = dict(trans_a=True, out_dtype=BF16, tm=1024, tn=512, tk=S)
    dw_out = _matmul(mb, dhb, name="dw_out", **tw)
    token = exchange_small(loss[0, 0], grads, dw_out)
    dwT_sgu = _matmul(dsgu, xn, after=token, name="dw_in_sgu", **tw)
    dwT_gate = _matmul(dgl, xn, name="dw_in_gate", **tw)
    dwT_ssd = _matmul(dssd, xn, name="dw_in_ssd", **tw)
    token = exchange([(dwT_sgu, SEG_SGU[1]), (dwT_ssd, W_IN - SEG_SSD[0]), (dwT_gate, SEG_GATE[1])])
    tm, tn = min(1024, S), 1024
    dxn = _matmul(dsgu, wpT, tm=tm, tn=512, tk=SEG_SGU[1], after=token, name="dxn_sgu")
    dxn = _matmul(dgl, wpT, b_koff=SEG_GATE[0] // 2048, tm=tm, tn=tn, tk=2048, add=dxn, name="dxn_gate")
    dxn = _matmul(dssd, wpT, b_koff=SEG_SSD[0] // 2048, tm=tm, tn=tn, tk=2048, add=dxn, name="dxn_ssd")
    grad_x, dnorm = _norm_bwd(x2, p['norm_w'], dxn, dh, tm=min(256, S))
    return grad_x, dnorm[0:1]


def _pack(arrs):
    rows, offs, r = [], [], 0
    for a in arrs:
        n = a.size
        nr = -(-n // PACK_ROW) * 8
        rows.append(jnp.pad(a.reshape(-1).astype(F32), (0, nr * LANE - n)).reshape(nr, LANE))
        offs.append(r)
        r += nr
    return jnp.concatenate(rows, axis=0), offs


def kernel(x, norm_w, w_in, gate_b, sgu_norm_g, sgu_norm_b, sgu_w, sgu_b, conv_w, conv_b, dt_bias, A_log, D_skip, ssd_norm_w, w_out, final_norm_w, loss_target, m_norm_w, m_w_in, m_gate_b, m_sgu_norm_g, m_sgu_norm_b, m_sgu_w, m_sgu_b, m_conv_w, m_conv_b, m_dt_bias, m_A_log, m_D_skip, m_ssd_norm_w, m_w_out, m_final_norm_w, v_norm_w, v_w_in, v_gate_b, v_sgu_norm_g, v_sgu_norm_b, v_sgu_w, v_sgu_b, v_conv_w, v_conv_b, v_dt_bias, v_A_log, v_D_skip, v_ssd_norm_w, v_w_out, v_final_norm_w):
    w = dict(norm_w=norm_w, w_in=w_in, gate_b=gate_b, sgu_norm_g=sgu_norm_g, sgu_norm_b=sgu_norm_b, sgu_w=sgu_w, sgu_b=sgu_b,
             conv_w=conv_w, conv_b=conv_b, dt_bias=dt_bias, A_log=A_log, D_skip=D_skip, ssd_norm_w=ssd_norm_w, w_out=w_out,
             final_norm_w=final_norm_w)
    m = dict(norm_w=m_norm_w, w_in=m_w_in, gate_b=m_gate_b, sgu_norm_g=m_sgu_norm_g, sgu_norm_b=m_sgu_norm_b, sgu_w=m_sgu_w,
             sgu_b=m_sgu_b, conv_w=m_conv_w, conv_b=m_conv_b, dt_bias=m_dt_bias, A_log=m_A_log, D_skip=m_D_skip,
             ssd_norm_w=m_ssd_norm_w, w_out=m_w_out, final_norm_w=m_final_norm_w)
    v = dict(norm_w=v_norm_w, w_in=v_w_in, gate_b=v_gate_b, sgu_norm_g=v_sgu_norm_g, sgu_norm_b=v_sgu_norm_b, sgu_w=v_sgu_w,
             sgu_b=v_sgu_b, conv_w=v_conv_w, conv_b=v_conv_b, dt_bias=v_dt_bias, A_log=v_A_log, D_skip=v_D_skip,
             ssd_norm_w=v_ssd_norm_w, w_out=v_w_out, final_norm_w=v_final_norm_w)
    me = 4 * lax.axis_index("x") + 2 * lax.axis_index("y") + lax.axis_index("c")
    shard_cw = XBC_W // N_DEV

    tpose = lambda a: jnp.swapaxes(a[0], 0, 1)
    wT = tpose(w_in).astype(BF16)
    first_group = (GROUP - (me * SHARD_IN) % GROUP) % GROUP
    window = lax.dynamic_slice(jnp.pad(wT, ((0, GROUP), (0, 0))), (first_group, 0), (INTERIOR, D))
    wpT, g_out, g_cw, heads, tails = _gather_weights(window, wT[:GROUP], wT[SHARD_IN - GROUP:], w_out[0].astype(BF16),
                                                     conv_w[0], jnp.zeros((W_ROWS - W_IN, D), BF16))
    wpT = _patch_straddlers(wpT, heads, tails)
    wout_full = g_out.reshape(D, D)
    cw_full = jnp.swapaxes(g_cw, 0, 1).reshape(CONV_K, XBC_W)

    flight = {}

    small = [n for n in WEIGHTS if n not in SHARDED and n != 'norm_w']
    early = {}

    def exchange_small(loss_part, grads, dw_out):
        early['packed'], early['offs'] = _pack([grads[n] for n in small] + [loss_part, grads['conv_w']])
        parts = [jnp.broadcast_to(early['packed'][None], (N_DEV,) + early['packed'].shape), dw_out.reshape(N_DEV, D // N_DEV, D)]
        early['sems'], early['rsems'], early['parts'], early['lands'], token = _exchange_start(parts, name="small_start")
        return token

    def exchange(dw_inT_segs):
        parts = [_to_shards(dw_inT_segs)]
        flight['sems'], flight['rsems'], flight['parts'], flight['lands'], token = _exchange_start(parts, name="exchange_start")
        return token

    grad_x, dnorm = _local_step(x[0], loss_target[0], wpT, wout_full, cw_full, w, exchange_small, exchange)
    (_, own_out), (land_small, land_out) = _exchange_wait(
        early['sems'], early['rsems'], early['parts'], early['lands'], grad_x, name="small_wait")
    (own_in,), (land_in,) = _exchange_wait(
        flight['sems'], flight['rsems'], flight['parts'], flight['lands'], grad_x, name="exchange_wait")
    me_arr = jnp.reshape(me, (1,)).astype(jnp.int32)
    res = {}
    res['w_in'] = [jnp.swapaxes(o, 0, 1) for o in _adamw_own(
        me_arr, own_in, land_in, tpose(w_in), tpose(m_w_in), tpose(v_w_in), tr=SHARD_IN, tc=256, name="adamw_w_in")]
    res['w_out'] = _adamw_own(me_arr, own_out, land_out, w_out[0], m_w_out[0], v_w_out[0], tr=128, tc=D, name="adamw_w_out")

    (norm_parts,) = _all_gather([_pack([dnorm])[0]], name="gather_norm")
    norm_outs = _adamw(norm_parts, *[_pack([d['norm_w']])[0] for d in (w, m, v)], tr=norm_parts.shape[1], name="adamw_norm")
    res['norm_w'] = [o.reshape(-1)[:D].reshape(w['norm_w'].shape) for o in norm_outs]

    offs = early['offs']
    gathered = lax.dynamic_update_slice(land_small, early['packed'][None], (me, 0, 0))
    off_loss, off_cw = offs[-2], offs[-1]
    cw_parts = gathered[:, off_cw:, :].reshape(N_DEV, CONV_K, XBC_W)
    cw_parts = lax.dynamic_slice_in_dim(cw_parts, me * shard_cw, shard_cw, axis=2)
    cw_rows = _pack([cw_parts[0]])[0].shape[0]
    cw_parts = jnp.pad(cw_parts.reshape(N_DEV, -1), ((0, 0), (0, cw_rows * LANE - CONV_K * shard_cw))).reshape(N_DEV, cw_rows, LANE)
    parts = jnp.concatenate([gathered[:, :off_cw, :], cw_parts], axis=1)
    zero = jnp.zeros((), F32)
    packs = [_pack([d[n] for n in small] + [zero, d['conv_w']])[0] for d in (w, m, v)]
    outs = _adamw(parts, *packs, tr=parts.shape[1], name="adamw_small")

    def unpack(o, name):
        if name == 'conv_w':
            return o[off_cw:off_cw + cw_rows].reshape(-1)[:CONV_K * shard_cw].reshape(w['conv_w'].shape)
        r0 = offs[small.index(name)]
        n = w[name].size
        return o[r0:r0 + -(-n // PACK_ROW) * 8].reshape(-1)[:n].reshape(w[name].shape)

    for n in small + ['conv_w']:
        res[n] = [unpack(o, n) for o in outs]
    for n in ('w_in', 'w_out'):
        res[n] = [o[None] for o in res[n]]
    loss = outs[0][off_loss, 0]
    return (loss, grad_x[None], *[res[n][0] for n in WEIGHTS], *[res[n][1] for n in WEIGHTS],
            *[res[n][2] for n in WEIGHTS], *[res[n][3] for n in WEIGHTS])
```

```python
import functools

import numpy as np
import jax
import jax.numpy as jnp
from jax import lax
from jax.experimental import pallas as pl
from jax.experimental.pallas import tpu as pltpu

F32 = jnp.float32
BF16 = jnp.bfloat16
HI = lax.Precision.HIGHEST
MESH = pl.DeviceIdType.MESH

D = 2048
EPS = 1e-5
SGU_BLOCK = 128
SGU_GROUPS = 16
CHUNK = 64
HEADS = 32
HEADDIM = 64
SSD_GROUPS = 4
GROUP_W = D // SSD_GROUPS
STATE = 128
CONV_K = 4
XBC_W = D + 2 * SSD_GROUPS * STATE
W_IN = 15392
N_DEV = 8
SHARD_IN = W_IN // N_DEV
ADAM_LR, ADAM_B1, ADAM_B2, ADAM_EPS, ADAM_WD, ADAM_STEP = 0.001, 0.9, 0.999, 1e-08, 0.01, 10

REF_SGU_END = 3 * D
REF_GATE_START = W_IN - 2 * D
LANE = 128
DT_W = LANE
OFF_U, OFF_V, OFF_ZA, OFF_G0, OFF_G1, OFF_ZB = (i * D for i in range(6))
OFF_XBC = OFF_ZB + D
OFF_DT = OFF_XBC + XBC_W
SEG_SGU = (OFF_U, 3 * D)
SEG_GATE = (OFF_G0, 2 * D)
SEG_SSD = (OFF_ZB, D + XBC_W + DT_W)
WP = SEG_SSD[0] + SEG_SSD[1]
SSD_PAD_W = 3 * D
VMEM_BYTES = 64 * 1024 * 1024
VMEM_LIMIT = VMEM_BYTES - 8 * 1024 * 1024


def _cp(sem=None, vmem=VMEM_LIMIT):
    return pltpu.CompilerParams(dimension_semantics=sem, vmem_limit_bytes=vmem)


def _sigmoid(x):
    return 1.0 / (1.0 + jnp.exp(-x))


def _softplus(x):
    return jnp.maximum(x, 0.0) + jnp.log(1.0 + jnp.exp(-jnp.abs(x)))


def _dot(a, b, precision=None):
    return jnp.dot(a, b, preferred_element_type=F32, precision=precision)


def _dot_nt(a, b, precision=None):
    return lax.dot_general(a, b, (((1,), (1,)), ((), ())), preferred_element_type=F32, precision=precision)


def _dot_tn(a, b, precision=None):
    return lax.dot_general(a, b, (((0,), (0,)), ((), ())), preferred_element_type=F32, precision=precision)


def _split3(a):
    hi = a.astype(BF16)
    r = a - hi.astype(F32)
    mid = r.astype(BF16)
    return hi, mid, (r - mid.astype(F32)).astype(BF16)


def _sel_right(a, sel01):
    m = a.shape[0]
    r = _dot(jnp.concatenate(_split3(a), axis=0), sel01)
    return (r[0:m] + r[m:2 * m]) + r[2 * m:3 * m]


def _sel_right_k(a, sel01_x3):
    return _dot(jnp.concatenate(_split3(a), axis=1), sel01_x3)


def _sel_left(sel01, a):
    n = a.shape[1]
    r = _dot(sel01, jnp.concatenate(_split3(a), axis=1))
    return (r[:, 0:n] + r[:, n:2 * n]) + r[:, 2 * n:3 * n]


def _matmul(a, b, *, trans_a=False, trans_b=False, b_koff=0, out_dtype=F32, tm, tn, tk, add=None, after=None, name):
    K, M = a.shape if trans_a else a.shape[::-1]
    N = b.shape[0] if trans_b else b.shape[1]
    assert M % tm == 0 and N % tn == 0 and K % tk == 0 and not (trans_a and trans_b), (name, M, N, K, tm, tn, tk)
    nk = K // tk

    def body(*refs):
        a_ref, b_ref = refs[:2]
        add_ref = refs[2] if add is not None else None
        o_ref, acc_ref = refs[-2:]
        k = pl.program_id(2)
        if trans_a:
            part = _dot_tn(a_ref[...], b_ref[...])
        else:
            part = _dot_nt(a_ref[...], b_ref[...]) if trans_b else _dot(a_ref[...], b_ref[...])

        def result(r):
            if add_ref is not None:
                r = r + add_ref[...]
            return r.astype(out_dtype)

        if nk == 1:
            o_ref[...] = result(part)
        else:
            @pl.when(k == 0)
            def _():
                acc_ref[...] = part

            @pl.when(jnp.logical_and(k > 0, k < nk - 1))
            def _():
                acc_ref[...] += part

            @pl.when(k == nk - 1)
            def _():
                o_ref[...] = result(acc_ref[...] + part)

    in_specs = [pl.BlockSpec((tk, tm), lambda i, j, k: (k, i)) if trans_a else pl.BlockSpec((tm, tk), lambda i, j, k: (i, k)),
                pl.BlockSpec((tn, tk), lambda i, j, k: (j, k)) if trans_b else pl.BlockSpec((tk, tn), lambda i, j, k: (k + b_koff, j))]
    args = [a, b]
    if add is not None:
        in_specs.append(pl.BlockSpec((tm, tn), lambda i, j, k: (i, j)))
        args.append(add)
    if after is not None:
        in_specs.append(pl.BlockSpec(memory_space=pl.ANY))
        args.append(after)
    return pl.pallas_call(
        body, name=name, grid=(M // tm, N // tn, nk), in_specs=in_specs,
        out_specs=pl.BlockSpec((tm, tn), lambda i, j, k: (i, j)),
        out_shape=jax.ShapeDtypeStruct((M, N), out_dtype),
        scratch_shapes=[pltpu.VMEM((tm, tn), F32)],
        compiler_params=_cp(("parallel", "parallel", "arbitrary")),
    )(*args)


DW_BUFS = 3


def _dw_in(segs, xn, after, *, tm):
    S = xn.shape[0]
    n = len(segs)
    assert all(a.shape[0] == S and a.shape[1] % tm == 0 for a in segs)

    def body(*refs):
        a_refs, xn_hbm = refs[:n], refs[n]
        o_refs = refs[-(n + 5):-5]
        xn_ref, abuf, obuf, asem, osem = refs[-5:]
        xn_copy = pltpu.make_async_copy(xn_hbm, xn_ref, osem.at[2])
        xn_copy.start()
        for q, (a_ref, o_ref) in enumerate(zip(a_refs, o_refs)):
            ntile = a_ref.shape[1] // tm

            def first(t):
                return t * tm if isinstance(t, int) else pl.multiple_of(t * tm, tm)

            def fetch(t, a_ref=a_ref):
                return pltpu.make_async_copy(a_ref.at[:, pl.ds(first(t), tm)], abuf.at[t % DW_BUFS], asem.at[t % DW_BUFS])

            def write(t, o_ref=o_ref):
                return pltpu.make_async_copy(obuf.at[t % 2], o_ref.at[pl.ds(first(t), tm), :], osem.at[t % 2])

            for t in range(min(DW_BUFS - 1, ntile)):
                fetch(t).start()
            if q == 0:
                xn_copy.wait()

            def step(t, carry, fetch=fetch, write=write, ntile=ntile):
                fetch(t).wait()

                @pl.when(t + DW_BUFS - 1 < ntile)
                def _():
                    fetch(t + DW_BUFS - 1).start()

                res = _dot_tn(abuf[t % DW_BUFS], xn_ref[...]).astype(BF16)

                @pl.when(t >= 2)
                def _():
                    write(t - 2).wait()

                obuf[t % 2] = res
                write(t).start()
                return carry

            lax.fori_loop(0, ntile, step, 0)
            for t in range(max(ntile - 2, 0), ntile):
                write(t).wait()

    anyspec = pl.BlockSpec(memory_space=pl.ANY)
    return pl.pallas_call(
        body, name="dw_in", in_specs=[anyspec] * (n + 1 + (after is not None)), out_specs=[anyspec] * n,
        out_shape=[jax.ShapeDtypeStruct((a.shape[1], D), BF16) for a in segs],
        scratch_shapes=[pltpu.VMEM((S, D), BF16), pltpu.VMEM((DW_BUFS, S, tm), BF16), pltpu.VMEM((2, tm, D), BF16),
                        pltpu.SemaphoreType.DMA((DW_BUFS,)), pltpu.SemaphoreType.DMA((3,))],
        compiler_params=_cp(),
    )(*segs, xn, *([after] if after is not None else []))


def _in_proj(x, w, wpT, *, tm, tn):
    S = x.shape[0]
    N = wpT.shape[0]
    assert S % tm == 0 and N % tn == 0, (S, N, tm, tn)

    def body(x_ref, w_ref, b_ref, xn_ref, o_ref, xs_ref):
        @pl.when(pl.program_id(1) == 0)
        def _():
            xv = x_ref[...]
            r = lax.rsqrt(jnp.mean(xv * xv, axis=-1, keepdims=True) + EPS)
            xs = (xv * r * w_ref[...]).astype(BF16)
            xs_ref[...] = xs
            xn_ref[...] = xs

        o_ref[...] = _dot_nt(xs_ref[...], b_ref[...]).astype(BF16)

    return pl.pallas_call(
        body, name="in_proj", grid=(S // tm, N // tn),
        in_specs=[pl.BlockSpec((tm, D), lambda i, j: (i, 0)), pl.BlockSpec((1, D), lambda i, j: (0, 0)),
                  pl.BlockSpec((tn, D), lambda i, j: (j, 0))],
        out_specs=[pl.BlockSpec((tm, D), lambda i, j: (i, 0)), pl.BlockSpec((tm, tn), lambda i, j: (i, j))],
        out_shape=[jax.ShapeDtypeStruct((S, D), BF16), jax.ShapeDtypeStruct((S, N), BF16)],
        scratch_shapes=[pltpu.VMEM((tm, D), BF16)],
        compiler_params=_cp(("parallel", "arbitrary")),
    )(x, w, wpT)


def _norm_bwd(x, w, dxn, dh, *, tm):
    S = x.shape[0]

    def body(x_ref, w_ref, dxn_ref, dh_ref, gx_ref, dw_ref):
        xv = x_ref[...]
        r = lax.rsqrt(jnp.mean(xv * xv, axis=-1, keepdims=True) + EPS)
        xh = xv * r
        dxn_v = dxn_ref[...]
        dxh = dxn_v * w_ref[...]
        gx_ref[...] = dh_ref[...] + r * (dxh - xh * jnp.mean(dxh * xh, axis=-1, keepdims=True))

        @pl.when(pl.program_id(0) == 0)
        def _():
            dw_ref[...] = jnp.zeros_like(dw_ref)

        dw_ref[0:1, :] += jnp.sum(dxn_v * xh, axis=0, keepdims=True)

    row = pl.BlockSpec((tm, D), lambda i: (i, 0))
    return pl.pallas_call(
        body, name="norm_bwd", grid=(S // tm,),
        in_specs=[row, pl.BlockSpec((1, D), lambda i: (0, 0)), row, row],
        out_specs=[row, pl.BlockSpec((8, D), lambda i: (0, 0))],
        out_shape=[jax.ShapeDtypeStruct((S, D), F32), jax.ShapeDtypeStruct((8, D), F32)],
        compiler_params=_cp(("arbitrary",)),
    )(x, w, dxn, dh)


def _sgu_core(u_ref, v_ref, z_ref, g_ref, b_ref, wm_ref, bias_ref, vnb_ref, mixed_ref, tm):
    v = v_ref[...].astype(F32)
    mu = jnp.mean(v, axis=-1, keepdims=True)
    vc = v - mu
    rs = lax.rsqrt(jnp.mean(vc * vc, axis=-1, keepdims=True) + EPS)
    vh = vc * rs
    vnb_ref[...] = (vh * g_ref[...] + b_ref[...]).astype(BF16)
    for blk in range(tm // SGU_BLOCK):
        rows = pl.ds(blk * SGU_BLOCK, SGU_BLOCK)
        for gi in range(SGU_GROUPS):
            cols = pl.ds(gi * LANE, LANE)
            mixed_ref[rows, cols] = _dot(wm_ref[gi], vnb_ref[rows, cols]) + bias_ref[:, cols]
    return vh, rs


def _sgu_fwd(proj, g, b, wm, bias_full, *, tm):
    S = proj.shape[0]

    def body(u_ref, v_ref, z_ref, g_ref, b_ref, wm_ref, bias_ref, y_ref, vnb_ref, mixed_ref):
        _sgu_core(u_ref, v_ref, z_ref, g_ref, b_ref, wm_ref, bias_ref, vnb_ref, mixed_ref, tm)
        z = z_ref[...].astype(F32)
        y_ref[...] = (u_ref[...].astype(F32) * mixed_ref[...] * (z * _sigmoid(z))).astype(BF16)

    seg = lambda off: pl.BlockSpec((tm, D), lambda i: (i, off // D))
    full = lambda a: pl.BlockSpec(a.shape, lambda i: (0,) * a.ndim)
    return pl.pallas_call(
        body, name="sgu_fwd", grid=(S // tm,),
        in_specs=[seg(OFF_U), seg(OFF_V), seg(OFF_ZA), full(g), full(b), full(wm), full(bias_full)],
        out_specs=pl.BlockSpec((tm, D), lambda i: (i, 0)),
        out_shape=jax.ShapeDtypeStruct((S, D), BF16),
        scratch_shapes=[pltpu.VMEM((tm, D), BF16), pltpu.VMEM((tm, D), F32)],
        compiler_params=_cp(("parallel",)),
    )(proj, proj, proj, g, b, wm, bias_full)


def _sgu_bwd(proj, dy, g, b, wm, wmT, bias_full, mask, sel, *, tm):
    S = proj.shape[0]
    nsteps = S // tm

    def body(u_ref, v_ref, z_ref, dy_ref, g_ref, b_ref, wm_ref, wmT_ref, bias_ref, mask_ref, sel_ref,
             dp_ref, dws_ref, dbs_ref, dg_ref, db_ref, vnb_ref, mixed_ref, dmb_ref, dvn_ref, dbias_ref):
        i = pl.program_id(0)

        @pl.when(i == 0)
        def _():
            dws_ref[...] = jnp.zeros_like(dws_ref)
            dg_ref[...] = jnp.zeros_like(dg_ref)
            db_ref[...] = jnp.zeros_like(db_ref)
            dbias_ref[...] = jnp.zeros_like(dbias_ref)

        vh, rs = _sgu_core(u_ref, v_ref, z_ref, g_ref, b_ref, wm_ref, bias_ref, vnb_ref, mixed_ref, tm)
        u = u_ref[...].astype(F32)
        z = z_ref[...].astype(F32)
        dy_v = dy_ref[...].astype(F32)
        mixed = mixed_ref[...]
        sg = _sigmoid(z)
        sz = z * sg
        dp_ref[:, 0:D] = (dy_v * mixed * sz).astype(BF16)
        dp_ref[:, 2 * D:3 * D] = (dy_v * u * mixed * (sg * (1.0 + z * (1.0 - sg)))).astype(BF16)
        dmixed = dy_v * u * sz
        dmb_ref[...] = dmixed.astype(BF16)
        for blk in range(tm // SGU_BLOCK):
            dbias_ref[...] += dmixed[blk * SGU_BLOCK:(blk + 1) * SGU_BLOCK, :]
        for blk in range(tm // SGU_BLOCK):
            rows = pl.ds(blk * SGU_BLOCK, SGU_BLOCK)
            for gi in range(SGU_GROUPS):
                cols = pl.ds(gi * LANE, LANE)
                dm = dmb_ref[rows, cols]
                dvn_ref[rows, cols] = _dot(wmT_ref[gi], dm)
                dws_ref[gi] += _dot_nt(dm, vnb_ref[rows, cols])
        dvn = dvn_ref[...]
        dg_ref[0:1, :] += jnp.sum(dvn * vh, axis=0, keepdims=True)
        db_ref[0:1, :] += jnp.sum(dvn, axis=0, keepdims=True)
        dvh = dvn * g_ref[...]
        dv = rs * (dvh - jnp.mean(dvh, axis=-1, keepdims=True) - vh * jnp.mean(dvh * vh, axis=-1, keepdims=True))
        dp_ref[:, D:2 * D] = dv.astype(BF16)

        @pl.when(i == nsteps - 1)
        def _():
            for gi in range(SGU_GROUPS):
                dws_ref[gi] = dws_ref[gi] * mask_ref[...]
            dbs_ref[...] = _dot(dbias_ref[...], sel_ref[...], precision=HI)

    seg = lambda off: pl.BlockSpec((tm, D), lambda i: (i, off // D))
    full = lambda a: pl.BlockSpec(a.shape, lambda i: (0,) * a.ndim)
    return pl.pallas_call(
        body, name="sgu_bwd", grid=(nsteps,),
        in_specs=[seg(OFF_U), seg(OFF_V), seg(OFF_ZA), pl.BlockSpec((tm, D), lambda i: (i, 0)),
                  full(g), full(b), full(wm), full(wmT), full(bias_full), full(mask), full(sel)],
        out_specs=[pl.BlockSpec((tm, 3 * D), lambda i: (i, 0)),
                   pl.BlockSpec((SGU_GROUPS, SGU_BLOCK, SGU_BLOCK), lambda i: (0, 0, 0)),
                   pl.BlockSpec((SGU_BLOCK, LANE), lambda i: (0, 0)),
                   pl.BlockSpec((8, D), lambda i: (0, 0)), pl.BlockSpec((8, D), lambda i: (0, 0))],
        out_shape=[jax.ShapeDtypeStruct((S, 3 * D), BF16),
                   jax.ShapeDtypeStruct((SGU_GROUPS, SGU_BLOCK, SGU_BLOCK), F32),
                   jax.ShapeDtypeStruct((SGU_BLOCK, LANE), F32),
                   jax.ShapeDtypeStruct((8, D), F32), jax.ShapeDtypeStruct((8, D), F32)],
        scratch_shapes=[pltpu.VMEM((tm, D), BF16), pltpu.VMEM((tm, D), F32), pltpu.VMEM((tm, D), BF16),
                        pltpu.VMEM((tm, D), F32), pltpu.VMEM((SGU_BLOCK, D), F32)],
        compiler_params=_cp(("arbitrary",)),
    )(proj, proj, proj, dy, g, b, wm, wmT, bias_full, mask, sel)


SSD_T = 2 * CHUNK
HALO = 8
HALO_BLK = 16


def _pair_masks():
    row = lax.broadcasted_iota(jnp.int32, (CHUNK, LANE), 0)
    lane = lax.broadcasted_iota(jnp.int32, (CHUNK, LANE), 1)
    pos = jnp.where(lane >= CHUNK, lane - CHUNK, lane)
    diag = (row == pos).astype(F32)
    causal = row >= pos
    lo = (lane < CHUNK).astype(F32)
    return diag, causal, lo, 1.0 - lo


def _ssd_chunk_fwd(c, ext_ref, shift_ref, dt_ref, cw_ref, cb_ref, dtb_ref, alog_ref, tri_ref, exp_ref):
    r0 = c * CHUNK
    win = ext_ref[pl.ds(r0, HALO_BLK + CHUNK), :]
    sh = _dot(shift_ref[...], win)
    taps = [sh[k * CHUNK:(k + 1) * CHUNK] for k in range(CONV_K - 1)] + [win[HALO_BLK:].astype(F32)]
    pre = cb_ref[...] + sum(cw_ref[k:k + 1, :] * taps[k] for k in range(CONV_K))
    sg = _sigmoid(pre)
    xc = pre * sg
    dtr = dt_ref[pl.ds(r0, CHUNK), :].astype(F32) + dtb_ref[...]
    dtv = _softplus(dtr)
    A = -jnp.exp(alog_ref[...])
    acs = _sel_left(tri_ref[...], dtv * A)
    both = _sel_right_k(jnp.concatenate([acs, dtv], axis=0), exp_ref[...])
    E, dtE = both[0:CHUNK], both[CHUNK:2 * CHUNK]
    return dict(taps=taps, pre=pre, sg=sg, xc=xc, dtr=dtr, dtv=dtv, A=A, E=E, dtE=dtE)


def _ssd_fwd(proj, conv_w, conv_b, dtb_p, alog_p, d_exp, norm_w, tri, expand, shift):
    S = proj.shape[0]
    T = SSD_T
    nsteps = S // T
    ncl = T // CHUNK

    def body(zb_ref, xbc_ref, halo_ref, dt_ref, cw_ref, cb_ref, dtb_ref, alog_ref, dexp_ref, nw_ref, tri_ref, exp_ref, shift_ref,
             y_ref, yb_ref, st_ref, ht_ref, ext_ref):
        i = pl.program_id(0)

        @pl.when(i == 0)
        def _():
            ht_ref[...] = jnp.zeros_like(ht_ref)
            ext_ref[0:HALO_BLK, :] = jnp.zeros((HALO_BLK, XBC_W), BF16)

        @pl.when(i > 0)
        def _():
            ext_ref[0:HALO_BLK, :] = halo_ref[...]

        ext_ref[HALO_BLK:HALO_BLK + T, :] = xbc_ref[...]
        diag, causal, lo, hi = _pair_masks()
        for c in range(ncl):
            q = _ssd_chunk_fwd(c, ext_ref, shift_ref, dt_ref, cw_ref, cb_ref, dtb_ref, alog_ref, tri_ref, exp_ref)
            rows = pl.ds(c * CHUNK, CHUNK)
            xc, E, dtE = q["xc"], q["E"], q["dtE"]
            xs = xc[:, 0:D]
            total = E[CHUNK - 1:CHUNK, :]
            x_dt = xs * dtE
            eE = jnp.exp(E)
            xw = x_dt * jnp.exp(total - E)
            st_ref[c] = ht_ref[...]
            for g in range(SSD_GROUPS):
                gc = slice(g * GROUP_W, (g + 1) * GROUP_W)
                Bg = xc[:, D + g * STATE:D + (g + 1) * STATE].astype(BF16)
                Cg = xc[:, D + SSD_GROUPS * STATE + g * STATE:D + SSD_GROUPS * STATE + (g + 1) * STATE].astype(BF16)
                cb2 = _dot_nt(Cg, jnp.concatenate([Bg, Bg], axis=0))
                htg = ht_ref[:, gc]
                y_ref[rows, gc] = eE[:, gc] * _dot(Cg, htg.astype(BF16)) + xs[:, gc] * dexp_ref[:, gc]
                for jj in range(GROUP_W // LANE):
                    pc = slice(g * GROUP_W + jj * LANE, g * GROUP_W + (jj + 1) * LANE)
                    Ej = E[:, pc]
                    e2 = jnp.sum(Ej * diag, axis=0, keepdims=True)
                    Mp = cb2 * jnp.exp(jnp.where(causal, Ej - e2, -1e30))
                    xj = x_dt[:, pc]
                    xbd = jnp.concatenate([xj * lo, xj * hi], axis=0).astype(BF16)
                    y_ref[rows, pc] += _dot(Mp.astype(BF16), xbd)
                ht_ref[:, gc] = jnp.exp(total[:, gc]) * htg + _dot_tn(Bg, xw[:, gc].astype(BF16))
            zb = zb_ref[rows, :].astype(F32)
            hh = y_ref[rows, :] * (zb * _sigmoid(zb))
            for g in range(SSD_GROUPS):
                gc = slice(g * GROUP_W, (g + 1) * GROUP_W)
                hg = hh[:, gc]
                r = lax.rsqrt(jnp.mean(hg * hg, axis=-1, keepdims=True) + EPS)
                yb_ref[rows, gc] = (hg * r * nw_ref[:, gc]).astype(BF16)

    full = lambda a: pl.BlockSpec(a.shape, lambda i: (0,) * a.ndim)
    hb = T // HALO_BLK
    return pl.pallas_call(
        body, name="ssd_fwd", grid=(nsteps,),
        in_specs=[pl.BlockSpec((T, D), lambda i: (i, OFF_ZB // D)),
                  pl.BlockSpec((T, XBC_W), lambda i: (i, OFF_XBC // XBC_W)),
                  pl.BlockSpec((HALO_BLK, XBC_W), lambda i: (jnp.maximum(i * hb - 1, 0), OFF_XBC // XBC_W)),
                  pl.BlockSpec((T, DT_W), lambda i: (i, OFF_DT // DT_W)),
                  full(conv_w), full(conv_b), full(dtb_p), full(alog_p), full(d_exp), full(norm_w), full(tri), full(expand),
                  full(shift)],
        out_specs=[pl.BlockSpec((T, D), lambda i: (i, 0)), pl.BlockSpec((T, D), lambda i: (i, 0)),
                   pl.BlockSpec((ncl, STATE, D), lambda i: (i, 0, 0))],
        out_shape=[jax.ShapeDtypeStruct((S, D), F32), jax.ShapeDtypeStruct((S, D), BF16),
                   jax.ShapeDtypeStruct((S // CHUNK, STATE, D), F32)],
        scratch_shapes=[pltpu.VMEM((STATE, D), F32), pltpu.VMEM((HALO_BLK + T, XBC_W), BF16)],
        compiler_params=_cp(("arbitrary",)),
    )(proj, proj, proj, proj, conv_w, conv_b, dtb_p, alog_p, d_exp, norm_w, tri, expand, shift)


def _ssd_bwd(proj, dyb, y, states, conv_w, conv_b, dtb_p, alog_p, d_exp, norm_w, tri, triT, expand, expandT, shift):
    S = proj.shape[0]
    T = SSD_T
    nsteps = S // T
    ncl = T // CHUNK
    SSD_W = SSD_PAD_W

    def body(zb_ref, xbc_ref, halo_ref, dt_ref, dyb_ref, y_ref, st_ref, cw_ref, cb_ref, dtb_ref, alog_ref, dexp_ref, nw_ref,
             tri_ref, triT_ref, exp_ref, expT_ref, shift_ref,
             dp_ref, dcw_ref, dcb_ref, ddtb_ref, dalog_ref, dD_ref, dnw_ref,
             dht_ref, ext_ref, dpre_ref, dy_s, dE_s, dxdt_s, dxc_s, dDacc_ref, dAacc_ref):
        i = pl.program_id(0)

        @pl.when(i == 0)
        def _():
            for r in (dht_ref, dcw_ref, dcb_ref, ddtb_ref, dnw_ref, dDacc_ref, dAacc_ref):
                r[...] = jnp.zeros_like(r)
            dpre_ref[T:T + HALO_BLK, :] = jnp.zeros((HALO_BLK, XBC_W), F32)

        @pl.when(i == nsteps - 1)
        def _():
            ext_ref[0:HALO_BLK, :] = jnp.zeros((HALO_BLK, XBC_W), BF16)

        @pl.when(i < nsteps - 1)
        def _():
            ext_ref[0:HALO_BLK, :] = halo_ref[...]

        ext_ref[HALO_BLK:HALO_BLK + T, :] = xbc_ref[...]
        diag, causal, lo, hi = _pair_masks()
        last_row = (lax.broadcasted_iota(jnp.int32, (CHUNK, 1), 0) == CHUNK - 1).astype(F32)
        for c in reversed(range(ncl)):
            q = _ssd_chunk_fwd(c, ext_ref, shift_ref, dt_ref, cw_ref, cb_ref, dtb_ref, alog_ref, tri_ref, exp_ref)
            rows = pl.ds(c * CHUNK, CHUNK)
            pre, sg, xc, dtr, dtv, A, E, dtE = (q[k] for k in ("pre", "sg", "xc", "dtr", "dtv", "A", "E", "dtE"))
            xs = xc[:, 0:D]
            total = E[CHUNK - 1:CHUNK, :]
            x_dt = xs * dtE
            eE = jnp.exp(E)
            wdec = jnp.exp(total - E)
            zb = zb_ref[rows, :].astype(F32)
            yv = y_ref[rows, :]
            sgz = _sigmoid(zb)
            sz = zb * sgz
            hh = yv * sz
            for g in range(SSD_GROUPS):
                gc = slice(g * GROUP_W, (g + 1) * GROUP_W)
                hg = hh[:, gc]
                r = lax.rsqrt(jnp.mean(hg * hg, axis=-1, keepdims=True) + EPS)
                dyb_g = dyb_ref[rows, gc].astype(F32)
                dn = dyb_g * nw_ref[:, gc]
                dnw_ref[0:1, gc] += jnp.sum(dyb_g * hg * r, axis=0, keepdims=True)
                dy_s[:, gc] = r * dn - hg * (r * r * r) * jnp.mean(dn * hg, axis=-1, keepdims=True)
            dhh = dy_s[...]
            dp_ref[rows, 0:D] = (dhh * yv * (sgz * (1.0 + zb * (1.0 - sgz)))).astype(BF16)
            dy = dhh * sz
            dy_s[...] = dy
            dDacc_ref[0:1, :] += jnp.sum(dy * xs, axis=0, keepdims=True)
            dxc_s[:, 0:D] = dy * dexp_ref[...]
            for g in range(SSD_GROUPS):
                gc = slice(g * GROUP_W, (g + 1) * GROUP_W)
                bcol = slice(D + g * STATE, D + (g + 1) * STATE)
                ccol = slice(D + SSD_GROUPS * STATE + g * STATE, D + SSD_GROUPS * STATE + (g + 1) * STATE)
                Bg = xc[:, bcol].astype(BF16)
                Cg = xc[:, ccol].astype(BF16)
                B2 = jnp.concatenate([Bg, Bg], axis=0)
                cb2 = _dot_nt(Cg, B2)
                htg = st_ref[c, :, gc]
                htb = htg.astype(BF16)
                dhn = dht_ref[:, gc]
                dhnb = dhn.astype(BF16)
                dyg = dy[:, gc]
                eEg = eE[:, gc]
                wg = wdec[:, gc]
                xdg = x_dt[:, gc]
                CH = _dot(Cg, htb)
                dCHb = (dyg * eEg).astype(BF16)
                dC = _dot_nt(dCHb, htb)
                dl = jnp.exp(total[:, gc])
                dht_prev = _dot_tn(Cg, dCHb) + dl * dhn
                dtot = jnp.sum(dhn * htg, axis=0, keepdims=True) * dl
                dxw = _dot(Bg, dhnb)
                dB = _dot_nt((xdg * wg).astype(BF16), dhnb)
                dwd = dxw * xdg * wg
                dtot = dtot + jnp.sum(dwd, axis=0, keepdims=True)
                dE_s[:, gc] = dyg * eEg * CH - dwd + last_row * dtot
                dxdt_s[:, gc] = dxw * wg
                dcb2 = jnp.zeros((CHUNK, LANE), F32)
                for jj in range(GROUP_W // LANE):
                    pc = slice(g * GROUP_W + jj * LANE, g * GROUP_W + (jj + 1) * LANE)
                    Ej = E[:, pc]
                    e2 = jnp.sum(Ej * diag, axis=0, keepdims=True)
                    Lp = jnp.exp(jnp.where(causal, Ej - e2, -1e30))
                    Mp = cb2 * Lp
                    xj = x_dt[:, pc]
                    xbd = jnp.concatenate([xj * lo, xj * hi], axis=0).astype(BF16)
                    dyj = dy[:, pc].astype(BF16)
                    dMp = _dot_nt(dyj, xbd)
                    dxbd = _dot_tn(Mp.astype(BF16), dyj)
                    dxdt_s[:, pc] += dxbd[0:CHUNK, :] * lo + dxbd[CHUNK:2 * CHUNK, :] * hi
                    dcb2 = dcb2 + dMp * Lp
                    dseg = dMp * Mp
                    dE_s[:, pc] += dseg - diag * jnp.sum(dseg, axis=0, keepdims=True)
                dcb2b = dcb2.astype(BF16)
                dC = dC + _dot(dcb2b, B2)
                dB2 = _dot_tn(dcb2b, Cg)
                dB = dB + dB2[0:CHUNK, :] + dB2[CHUNK:2 * CHUNK, :]
                dxc_s[:, bcol] = dB
                dxc_s[:, ccol] = dC
                dht_ref[:, gc] = dht_prev
            dx_dt = dxdt_s[...]
            dxc_s[:, 0:D] += dx_dt * dtE
            red = _sel_right(jnp.concatenate([dE_s[...], dx_dt * xs], axis=0), expT_ref[...])
            da = _sel_left(triT_ref[...], red[0:CHUNK, :])
            ddtv = red[CHUNK:2 * CHUNK, :] + da * A
            dAacc_ref[0:1, :] += jnp.sum(da * dtv, axis=0, keepdims=True)
            ddtr = ddtv * _sigmoid(dtr)
            ddtb_ref[0:1, :] += jnp.sum(ddtr, axis=0, keepdims=True)
            dp_ref[rows, D + XBC_W:D + XBC_W + DT_W] = ddtr.astype(BF16)
            dpre = dxc_s[...] * (sg * (1.0 + pre * (1.0 - sg)))
            dpre_ref[rows, :] = dpre
            dcb_ref[0:1, :] += jnp.sum(dpre, axis=0, keepdims=True)
            for k in range(CONV_K):
                dcw_ref[k:k + 1, :] += jnp.sum(dpre * q["taps"][k], axis=0, keepdims=True)
        dxbc = jnp.zeros((T, XBC_W), F32)
        for k in range(CONV_K):
            dxbc = dxbc + cw_ref[k:k + 1, :] * dpre_ref[pl.ds(CONV_K - 1 - k, T), :]
        dp_ref[:, D:D + XBC_W] = dxbc.astype(BF16)
        dp_ref[:, SEG_SSD[1]:SSD_W] = jnp.zeros((T, SSD_W - SEG_SSD[1]), BF16)
        dpre_ref[T:T + HALO, :] = dpre_ref[0:HALO, :]

        @pl.when(i == nsteps - 1)
        def _():
            dalog_ref[...] = dAacc_ref[...] * (-jnp.exp(alog_ref[...]))
            dD_ref[...] = _dot(dDacc_ref[...], expT_ref[...].astype(F32), precision=HI)

    full = lambda a: pl.BlockSpec(a.shape, lambda i: (0,) * a.ndim)
    hb = T // HALO_BLK
    rev = lambda i: nsteps - 1 - i
    acc = lambda w: pl.BlockSpec((8, w), lambda i: (0, 0))
    return pl.pallas_call(
        body, name="ssd_bwd", grid=(nsteps,),
        in_specs=[pl.BlockSpec((T, D), lambda i: (rev(i), OFF_ZB // D)),
                  pl.BlockSpec((T, XBC_W), lambda i: (rev(i), OFF_XBC // XBC_W)),
                  pl.BlockSpec((HALO_BLK, XBC_W), lambda i: (jnp.maximum(rev(i) * hb - 1, 0), OFF_XBC // XBC_W)),
                  pl.BlockSpec((T, DT_W), lambda i: (rev(i), OFF_DT // DT_W)),
                  pl.BlockSpec((T, D), lambda i: (rev(i), 0)), pl.BlockSpec((T, D), lambda i: (rev(i), 0)),
                  pl.BlockSpec((ncl, STATE, D), lambda i: (rev(i), 0, 0)),
                  full(conv_w), full(conv_b), full(dtb_p), full(alog_p), full(d_exp), full(norm_w),
                  full(tri), full(triT), full(expand), full(expandT), full(shift)],
        out_specs=[pl.BlockSpec((T, SSD_W), lambda i: (rev(i), 0)),
                   acc(XBC_W), acc(XBC_W), acc(DT_W), acc(DT_W), acc(DT_W), acc(D)],
        out_shape=[jax.ShapeDtypeStruct((S, SSD_W), BF16),
                   jax.ShapeDtypeStruct((8, XBC_W), F32), jax.ShapeDtypeStruct((8, XBC_W), F32),
                   jax.ShapeDtypeStruct((8, DT_W), F32), jax.ShapeDtypeStruct((8, DT_W), F32),
                   jax.ShapeDtypeStruct((8, DT_W), F32), jax.ShapeDtypeStruct((8, D), F32)],
        scratch_shapes=[pltpu.VMEM((STATE, D), F32), pltpu.VMEM((HALO_BLK + T, XBC_W), BF16), pltpu.VMEM((T + HALO_BLK, XBC_W), F32),
                        pltpu.VMEM((CHUNK, D), F32), pltpu.VMEM((CHUNK, D), F32), pltpu.VMEM((CHUNK, D), F32),
                        pltpu.VMEM((CHUNK, XBC_W), F32), pltpu.VMEM((8, D), F32), pltpu.VMEM((8, DT_W), F32)],
        compiler_params=_cp(("arbitrary",)),
    )(proj, proj, proj, proj, dyb, y, states, conv_w, conv_b, dtb_p, alog_p, d_exp, norm_w, tri, triT, expand, expandT, shift)


def _head(x, ya, yb, proj, target, gate_b, wout, fw, *, tm):
    S = x.shape[0]

    def body(x_ref, ya_ref, yb_ref, gl0_ref, gl1_ref, t_ref, gb_ref, w_ref, fw_ref,
             dh_ref, dhb_ref, mb_ref, dya_ref, dyb_ref, dgl_ref, loss_ref, dfw_ref, dgb_ref):
        @pl.when(pl.program_id(0) == 0)
        def _():
            loss_ref[...] = jnp.zeros_like(loss_ref)
            dfw_ref[...] = jnp.zeros_like(dfw_ref)
            dgb_ref[...] = jnp.zeros_like(dgb_ref)

        ya_v = ya_ref[...].astype(F32)
        yb_v = yb_ref[...].astype(F32)
        g0 = _sigmoid(gl0_ref[...].astype(F32) + gb_ref[:, 0:D])
        g1 = _sigmoid(gl1_ref[...].astype(F32) + gb_ref[:, D:2 * D])
        mb = (g0 * ya_v + g1 * yb_v).astype(BF16)
        mb_ref[...] = mb
        h = x_ref[...] + _dot(mb, w_ref[...])
        r = lax.rsqrt(jnp.mean(h * h, axis=-1, keepdims=True) + EPS)
        hn = h * r
        err = hn * fw_ref[...] - t_ref[...]
        loss_ref[...] += 0.5 * jnp.sum(jnp.mean(err * err, axis=-1, keepdims=True))
        dyf = err * (1.0 / D)
        dfw_ref[0:1, :] += jnp.sum(dyf * hn, axis=0, keepdims=True)
        dhn = dyf * fw_ref[...]
        dh = r * (dhn - hn * jnp.mean(dhn * hn, axis=-1, keepdims=True))
        dh_ref[...] = dh
        dhb = dh.astype(BF16)
        dhb_ref[...] = dhb
        dm = _dot_nt(dhb, w_ref[...])
        dya_ref[...] = (dm * g0).astype(BF16)
        dyb_ref[...] = (dm * g1).astype(BF16)
        dgl0 = dm * ya_v * g0 * (1.0 - g0)
        dgl1 = dm * yb_v * g1 * (1.0 - g1)
        dgl_ref[:, 0:D] = dgl0.astype(BF16)
        dgl_ref[:, D:2 * D] = dgl1.astype(BF16)
        dgb_ref[0:1, 0:D] += jnp.sum(dgl0, axis=0, keepdims=True)
        dgb_ref[0:1, D:2 * D] += jnp.sum(dgl1, axis=0, keepdims=True)

    row = pl.BlockSpec((tm, D), lambda i: (i, 0))
    seg = lambda off: pl.BlockSpec((tm, D), lambda i: (i, off // D))
    full = lambda a: pl.BlockSpec(a.shape, lambda i: (0,) * a.ndim)
    acc = lambda w: pl.BlockSpec((8, w), lambda i: (0, 0))
    return pl.pallas_call(
        body, name="head", grid=(S // tm,),
        in_specs=[row, row, row, seg(OFF_G0), seg(OFF_G1), row, full(gate_b), full(wout), full(fw)],
        out_specs=[row, row, row, row, row, pl.BlockSpec((tm, 2 * D), lambda i: (i, 0)), acc(LANE), acc(D), acc(2 * D)],
        out_shape=[jax.ShapeDtypeStruct((S, D), F32), jax.ShapeDtypeStruct((S, D), BF16), jax.ShapeDtypeStruct((S, D), BF16),
                   jax.ShapeDtypeStruct((S, D), BF16), jax.ShapeDtypeStruct((S, D), BF16), jax.ShapeDtypeStruct((S, 2 * D), BF16),
                   jax.ShapeDtypeStruct((8, LANE), F32), jax.ShapeDtypeStruct((8, D), F32), jax.ShapeDtypeStruct((8, 2 * D), F32)],
        compiler_params=_cp(("arbitrary",)),
    )(x, ya, yb, proj, proj, target, gate_b, wout, fw)


def _adam_update(g, w_ref, m_ref, v_ref, g_ref, d_ref, m2_ref, v2_ref):
    m2 = ADAM_B1 * m_ref[...] + (1.0 - ADAM_B1) * g
    v2 = ADAM_B2 * v_ref[...] + (1.0 - ADAM_B2) * (g * g)
    m_hat = m2 / (1.0 - ADAM_B1 ** ADAM_STEP)
    v_hat = v2 / (1.0 - ADAM_B2 ** ADAM_STEP)
    g_ref[...] = g
    d_ref[...] = -ADAM_LR * (m_hat / (jnp.sqrt(v_hat) + ADAM_EPS) + ADAM_WD * w_ref[...])
    m2_ref[...] = m2
    v2_ref[...] = v2


def _adamw_own(me, own, landed, w, m, v, *, tr, tc, name):
    _, R, C = landed.shape
    assert R % tr == 0 and C % tc == 0, (name, R, C, tr, tc)

    def body(me_ref, own_ref, p_ref, w_ref, m_ref, v_ref, g_ref, d_ref, m2_ref, v2_ref):
        mine = own_ref[0].astype(F32)
        g = jnp.where(me_ref[0] == 0, mine, p_ref[0].astype(F32))
        for k in range(1, N_DEV):
            g = g + jnp.where(me_ref[0] == k, mine, p_ref[k].astype(F32))
        _adam_update(g, w_ref, m_ref, v_ref, g_ref, d_ref, m2_ref, v2_ref)

    tile = pl.BlockSpec((tr, tc), lambda i, j, me_ref: (i, j))
    return pl.pallas_call(
        body, name=name,
        grid_spec=pltpu.PrefetchScalarGridSpec(
            num_scalar_prefetch=1, grid=(R // tr, C // tc),
            in_specs=[pl.BlockSpec((1, tr, tc), lambda i, j, me_ref: (me_ref[0], i, j)),
                      pl.BlockSpec((N_DEV, tr, tc), lambda i, j, me_ref: (0, i, j)), tile, tile, tile],
            out_specs=[tile, tile, tile, tile]),
        out_shape=[jax.ShapeDtypeStruct((R, C), F32)] * 4,
        compiler_params=_cp(("parallel", "parallel")),
    )(me, own, landed, w, m, v)


def _adamw(parts, w, m, v, *, tr, name):
    _, R, C = parts.shape
    assert R % tr == 0, (name, R, tr)

    def body(p_ref, w_ref, m_ref, v_ref, g_ref, d_ref, m2_ref, v2_ref):
        g = p_ref[0].astype(F32)
        for k in range(1, N_DEV):
            g = g + p_ref[k].astype(F32)
        _adam_update(g, w_ref, m_ref, v_ref, g_ref, d_ref, m2_ref, v2_ref)

    row = pl.BlockSpec((tr, C), lambda i: (i, 0))
    return pl.pallas_call(
        body, name=name, grid=(R // tr,),
        in_specs=[pl.BlockSpec((N_DEV, tr, C), lambda i: (0, i, 0)), row, row, row],
        out_specs=[row, row, row, row],
        out_shape=[jax.ShapeDtypeStruct((R, C), F32)] * 4,
        compiler_params=_cp(("parallel",)),
    )(parts, w, m, v)


def _place():
    x, y, c = lax.axis_index("x"), lax.axis_index("y"), lax.axis_index("c")
    return x, y, c


def _all_gather(arrs, *, name):
    n = len(arrs)

    def body(*refs):
        ins, outs = refs[:n], refs[n:2 * n]
        send_sems, recv_sems, local_sems = refs[2 * n:]
        x, y, c = _place()
        me, sibling = (x, y, c), (x, y, 1 - c)
        chips = [(1 - x, y), (x, 1 - y), (1 - x, 1 - y)]

        def idx(px, py, pc):
            return 4 * px + 2 * py + pc

        def copy(k, a, block, to, src=None):
            slab = outs[a].at[idx(*block)]
            return pltpu.make_async_remote_copy(
                src_ref=slab if src is None else src, dst_ref=slab,
                send_sem=send_sems.at[k, a], recv_sem=recv_sems.at[k, a], device_id=to, device_id_type=MESH)

        mine = [pltpu.make_async_copy(ins[a], outs[a].at[idx(*me)], local_sems.at[a]) for a in range(n)]
        for cp in mine:
            cp.start()
        first = []
        for a in range(n):
            first.append(copy(0, a, me, sibling, src=ins[a]))
            first += [copy(1 + j, a, me, (*chip, c), src=ins[a]) for j, chip in enumerate(chips)]
        for cp in first:
            cp.start()
        passed = []
        for j, chip in enumerate(chips):
            for a in range(n):
                copy(1 + j, a, (*chip, c), me).wait_recv()
                fwd = copy(4 + j, a, (*chip, c), sibling)
                fwd.start()
                passed.append(fwd)
        for a in range(n):
            copy(0, a, sibling, me).wait_recv()
            for j, chip in enumerate(chips):
                copy(4 + j, a, (*chip, 1 - c), me).wait_recv()
        for cp in first + passed:
            cp.wait_send()
        for cp in mine:
            cp.wait()

    anyspec = pl.BlockSpec(memory_space=pl.ANY)
    return pl.pallas_call(
        body, name=name,
        in_specs=[anyspec] * n, out_specs=[anyspec] * n,
        out_shape=[jax.ShapeDtypeStruct((N_DEV,) + a.shape, a.dtype) for a in arrs],
        scratch_shapes=[pltpu.SemaphoreType.DMA((7, n)), pltpu.SemaphoreType.DMA((7, n)), pltpu.SemaphoreType.DMA((n,))],
    )(*arrs)


W_ROWS = SEG_SSD[0] + SSD_PAD_W


GROUP = 16
INTERIOR = 1920


def _interior(k):
    lo = -(-(k * SHARD_IN) // GROUP) * GROUP
    hi = ((k + 1) * SHARD_IN) // GROUP * GROUP
    return lo, hi


def _dest_row(r):
    if r < REF_SGU_END:
        return r
    return r - REF_SGU_END + SEG_SSD[0] if r < REF_GATE_START else r - REF_GATE_START + SEG_GATE[0]


def _shard_pieces(k):
    lo_k, hi_k = _interior(k)
    out = []
    for lo, hi in ((0, REF_SGU_END), (REF_SGU_END, REF_GATE_START), (REF_GATE_START, W_IN)):
        a, b = max(lo, lo_k), min(hi, hi_k)
        if a < b:
            out.append((a - lo_k, b - a, _dest_row(a)))
    return out


GATHER_PARTS = 1


def _shard_parts(k):
    parts = [[] for _ in range(GATHER_PARTS)]
    for s0, n, d0 in _shard_pieces(k):
        step = -(-(n // GROUP) // GATHER_PARTS) * GROUP
        for p in range(GATHER_PARTS):
            a, b = min(p * step, n), min((p + 1) * step, n)
            if a < b:
                parts[p].append((s0 + a, b - a, d0 + a))
    return parts


def _patch_straddlers(wpT, heads, tails):
    for k in range(1, N_DEV):
        m = (k * SHARD_IN) % GROUP
        if m:
            group = jnp.concatenate([tails[k - 1, GROUP - m:], heads[k, :GROUP - m]], axis=0)
            wpT = lax.dynamic_update_slice(wpT, group, (_dest_row(k * SHARD_IN - m), 0))
    return wpT


def _gather_stages(k, win_ref, small, z_ref, n_zero, w_ref, send_sems, recv_sems, local_sems):
    x, y, c = k // 4, (k // 2) % 2, k % 2
    idx = lambda p: 4 * p[0] + 2 * p[1] + p[2]
    me, sib = (x, y, c), (x, y, 1 - c)
    xn, yn, dg = (1 - x, y, c), (x, 1 - y, c), (1 - x, 1 - y, c)
    parts = range(GATHER_PARTS)

    def copies(slot, block, to, part, own=False):
        kb = idx(block)
        out = []
        for j, (s0, n, d0) in enumerate(_shard_parts(kb)[part]):
            dst = w_ref.at[pl.ds(d0, n)]
            out.append((win_ref.at[pl.ds(s0, n)] if own else dst, dst, 2 * part + j))
        if part == 0:
            for j, (src, gathered) in enumerate(small):
                out.append((src if own else gathered.at[kb], gathered.at[kb], 2 * GATHER_PARTS + j))
        return [pltpu.make_async_remote_copy(src_ref=s, dst_ref=d, send_sem=send_sems.at[slot, j], recv_sem=recv_sems.at[slot, j],
                                             device_id=to, device_id_type=MESH) for s, d, j in out]

    def start(cps):
        for cp in cps:
            cp.start()

    def arrived(slot, block, part):
        for cp in copies(slot, block, me, part):
            cp.wait_recv()

    def local():
        pairs = [(win_ref.at[pl.ds(s0, n)], w_ref.at[pl.ds(d0, n)]) for s0, n, d0 in _shard_pieces(k)]
        pairs += [(src, gathered.at[k]) for src, gathered in small] + [(z_ref, w_ref.at[pl.ds(W_IN, n_zero)])]
        return [pltpu.make_async_copy(s, d, local_sems.at[j]) for j, (s, d) in enumerate(pairs)]

    relay = (xn, yn) if c == 1 else (yn, xn)

    def first():
        start(local())
        for p in parts:
            start(copies(0, me, sib, p, own=True) + copies(1, me, xn, p, own=True) + copies(2, me, yn, p, own=True))

    def hand_on():
        for p in parts:
            arrived(1, xn, p)
            start(copies(4, xn, sib, p))
            if c == 1:
                start(copies(3, *relay, p))
            arrived(2, yn, p)
            start(copies(5, yn, sib, p))
            if c == 0:
                start(copies(3, *relay, p))

    def finish():
        for p in parts:
            arrived(3, dg, p)
            start(copies(6, dg, sib, p))
        for p in parts:
            arrived(0, sib, p)
            arrived(4, (1 - x, y, 1 - c), p)
            arrived(5, (x, 1 - y, 1 - c), p)
            arrived(6, (1 - x, 1 - y, 1 - c), p)
        for p in parts:
            sent = (copies(0, me, sib, p, own=True) + copies(1, me, xn, p, own=True) + copies(2, me, yn, p, own=True)
                    + copies(3, *relay, p) + copies(4, xn, sib, p) + copies(5, yn, sib, p) + copies(6, dg, sib, p))
            for cp in sent:
                cp.wait_send()
        for cp in local():
            cp.wait()

    return first, hand_on, finish


def _gather_sems(n_small):
    n_arr = 2 * GATHER_PARTS + n_small
    return [pltpu.SemaphoreType.DMA((7, n_arr)), pltpu.SemaphoreType.DMA((7, n_arr)), pltpu.SemaphoreType.DMA((n_arr + 1,))]


def _gather_weights(win, head, tail, wout, cw, zeros):
    small_in = (wout, cw, head, tail)
    n_zero = zeros.shape[0]
    assert W_IN + n_zero == W_ROWS and W_IN % GROUP == 0

    def body(win_ref, wout_ref, cw_ref, head_ref, tail_ref, z_ref, w_ref, gout_ref, gcw_ref, ghead_ref, gtail_ref, *sems):
        x, y, c = _place()
        me = 4 * x + 2 * y + c
        small = ((wout_ref, gout_ref), (cw_ref, gcw_ref), (head_ref, ghead_ref), (tail_ref, gtail_ref))

        def run(k):
            for stage in _gather_stages(k, win_ref, small, z_ref, n_zero, w_ref, *sems):
                stage()

        for k in range(N_DEV):
            pl.when(me == k)(functools.partial(run, k))

    anyspec = pl.BlockSpec(memory_space=pl.ANY)
    return pl.pallas_call(
        body, name="gather_weights", in_specs=[anyspec] * 6, out_specs=[anyspec] * 5,
        out_shape=[jax.ShapeDtypeStruct((W_ROWS, D), win.dtype)]
        + [jax.ShapeDtypeStruct((N_DEV,) + a.shape, a.dtype) for a in small_in],
        scratch_shapes=_gather_sems(len(small_in)),
    )(win, wout, cw, head, tail, zeros)


_REL = [(dx, dy, dc) for dx in (0, 1) for dy in (0, 1) for dc in (0, 1)][1:]
_HBM = pl.BlockSpec(memory_space=pltpu.HBM)
_SEM = pl.BlockSpec(memory_space=pltpu.SEMAPHORE)
_EFFECT = pltpu.SideEffectType.DATAFLOW_SIDE_EFFECTING


def _peer(k):
    x, y, c = _place()
    dx, dy, dc = _REL[k]
    return (1 - x if dx else x, 1 - y if dy else y, 1 - c if dc else c)


def _exchange_start(parts, *, name):
    n = len(parts)

    def body(*refs):
        ins, lands = refs[:n], refs[n:2 * n]
        send_sems, recv_sems, token = refs[2 * n], refs[2 * n + 1], refs[-1]
        x, y, c = _place()
        me = 4 * x + 2 * y + c
        for a in range(n):
            for k in range(len(_REL)):
                px, py, pc = _peer(k)
                pltpu.make_async_remote_copy(
                    src_ref=ins[a].at[4 * px + 2 * py + pc], dst_ref=lands[a].at[me],
                    send_sem=send_sems.at[len(_REL) * a + k], recv_sem=recv_sems.at[len(_REL) * a + k],
                    device_id=(px, py, pc), device_id_type=MESH).start()
        token[...] = jnp.zeros_like(token)

    sem = pltpu.SemaphoreType.DMA((len(_REL) * n,))
    bufs = [pltpu.HBM(p.shape, p.dtype) for p in parts]
    outs = pl.pallas_call(
        body, name=name,
        out_shape=(sem, sem, *bufs, *bufs, jax.ShapeDtypeStruct((8, LANE), F32)),
        in_specs=(_HBM,) * (2 * n), out_specs=(_SEM, _SEM, *(_HBM,) * (2 * n), pl.BlockSpec(memory_space=pltpu.VMEM)),
        input_output_aliases={i: 2 + i for i in range(2 * n)},
        compiler_params=pltpu.CompilerParams(has_side_effects=_EFFECT),
    )(*[pltpu.with_memory_space_constraint(p, pltpu.HBM) for p in parts],
      *[pltpu.with_memory_space_constraint(lax.empty(p.shape, p.dtype), pltpu.HBM) for p in parts])
    return outs[0], outs[1], outs[2:2 + n], outs[2 + n:2 + 2 * n], outs[-1]


def _exchange_wait(send_sems, recv_sems, parts, lands, after, *, name):
    n = len(parts)

    def body(*refs):
        ins, lands_ = refs[:n], refs[n:2 * n]
        ssem, rsem = refs[2 * n], refs[2 * n + 1]
        for a in range(n):
            for k in range(len(_REL)):
                px, py, pc = _peer(k)
                p = 4 * px + 2 * py + pc
                cp = pltpu.make_async_remote_copy(
                    src_ref=ins[a].at[p], dst_ref=lands_[a].at[p],
                    send_sem=ssem.at[len(_REL) * a + k], recv_sem=rsem.at[len(_REL) * a + k],
                    device_id=(px, py, pc), device_id_type=MESH)
                cp.wait_send()
                cp.wait_recv()

    bufs = [pltpu.HBM(p.shape, p.dtype) for p in parts]
    outs = pl.pallas_call(
        body, name=name, out_shape=(*bufs, *bufs),
        in_specs=(*(_HBM,) * (2 * n), _SEM, _SEM, pl.BlockSpec(memory_space=pl.ANY)), out_specs=(_HBM,) * (2 * n),
        input_output_aliases={i: i for i in range(2 * n)},
        compiler_params=pltpu.CompilerParams(has_side_effects=_EFFECT),
    )(*parts, *lands, send_sems, recv_sems, after)
    return outs[:n], outs[n:]


WEIGHTS = ('norm_w', 'w_in', 'gate_b', 'sgu_norm_g', 'sgu_norm_b', 'sgu_w', 'sgu_b', 'conv_w', 'conv_b', 'dt_bias', 'A_log',
           'D_skip', 'ssd_norm_w', 'w_out', 'final_norm_w')
SHARDED = ('w_in', 'conv_w', 'w_out')
PACK_ROW = 8 * LANE


def _constants():
    tri = np.tril(np.ones((CHUNK, CHUNK), np.float32))
    expand = np.zeros((DT_W, D), np.float32)
    for h in range(HEADS):
        expand[h, h * HEADDIM:(h + 1) * HEADDIM] = 1.0
    sel = np.zeros((D, LANE), np.float32)
    for g in range(SGU_GROUPS):
        sel[g * LANE:(g + 1) * LANE, g] = 1.0
    pos_chunk = np.arange(SGU_BLOCK) // CHUNK
    mask = (pos_chunk[None, :] <= pos_chunk[:, None]).astype(np.float32)
    shift = np.zeros(((CONV_K - 1) * CHUNK, HALO_BLK + CHUNK), np.float32)
    for kk in range(CONV_K - 1):
        for t in range(CHUNK):
            shift[kk * CHUNK + t, HALO_BLK - (CONV_K - 1) + t + kk] = 1.0
    return dict(tri=jnp.asarray(tri, BF16), triT=jnp.asarray(tri.T.copy(), BF16), expand=jnp.asarray(np.tile(expand, (3, 1)), BF16),
                shift=jnp.asarray(shift, BF16),
                expandT=jnp.asarray(expand.T.copy(), BF16), sel=jnp.asarray(sel), mask=jnp.asarray(mask))


def _to_shards(segs):
    starts = np.cumsum([0] + [n for _, n in segs])
    assert starts[-1] == W_IN
    slabs = []
    for k in range(N_DEV):
        pieces = []
        for (s, n), s0 in zip(segs, starts[:-1]):
            lo, hi = max(k * SHARD_IN, s0), min((k + 1) * SHARD_IN, s0 + n)
            if lo < hi:
                pieces.append(s[lo - s0:hi - s0])
        slabs.append(jnp.concatenate(pieces, axis=0))
    return jnp.stack(slabs)


def _local_step(x2, tgt, wpT, wout, cw, p, exchange_small, exchange):
    S = x2.shape[0]
    k = _constants()
    xn, proj = _in_proj(x2, p['norm_w'], wpT, tm=min(1024, S), tn=2048)
    wm32 = p['sgu_w'][0] * k['mask']
    wm = wm32.astype(BF16)
    wmT = jnp.swapaxes(wm32, 1, 2).astype(BF16)
    bias_full = jnp.repeat(p['sgu_b'][0].T, LANE, axis=1)
    tm_sgu = min(512, S)
    ya = _sgu_fwd(proj, p['sgu_norm_g'], p['sgu_norm_b'], wm, bias_full, tm=tm_sgu)
    pad32 = lambda a: jnp.pad(a, ((0, 0), (0, DT_W - HEADS)))
    dtb_p, alog_p = pad32(p['dt_bias']), pad32(p['A_log'])
    d_exp = jnp.repeat(p['D_skip'], HEADDIM, axis=1)
    ssd_args = (cw, p['conv_b'], dtb_p, alog_p, d_exp, p['ssd_norm_w'])
    y, yb, states = _ssd_fwd(proj, *ssd_args, k['tri'], k['expand'], k['shift'])
    dh, dhb, mb, dya, dyb, dgl, loss, dfw, dgb = _head(
        x2, ya, yb, proj, tgt, p['gate_b'], wout, p['final_norm_w'][None, :], tm=min(256, S))
    dsgu, dws, dbsT, dsg, dsb = _sgu_bwd(proj, dya, p['sgu_norm_g'], p['sgu_norm_b'], wm, wmT, bias_full, k['mask'], k['sel'],
                                         tm=tm_sgu)
    dssd, dcw, dcb, ddtb, dalog, dD, dnw = _ssd_bwd(proj, dyb, y, states, *ssd_args, k['tri'], k['triT'], k['expand'], k['expandT'],
                                                    k['shift'])
    grads = dict(
        gate_b=dgb[0:1], sgu_norm_g=dsg[0:1], sgu_norm_b=dsb[0:1], sgu_w=dws[None],
        sgu_b=dbsT[:, :SGU_GROUPS].T[None], conv_w=dcw[0:CONV_K][None], conv_b=dcb[0:1], dt_bias=ddtb[0:1, :HEADS],
        A_log=dalog[0:1, :HEADS], D_skip=dD[0:1, :HEADS], ssd_norm_w=dnw[0:1], final_norm_w=dfw[0])
    tw = dict(trans_a=True, out_dtype=BF16, tm=1024, tn=512, tk=S)
    dw_out = _matmul(mb, dhb, name="dw_out", **tw)
    token = exchange_small(loss[0, 0], grads, dw_out)
    dwT_sgu, dwT_gate, dwT_ssd = _dw_in([dsgu, dgl, dssd], xn, token, tm=256)
    token = exchange([(dwT_sgu, SEG_SGU[1]), (dwT_ssd, W_IN - SEG_SSD[0]), (dwT_gate, SEG_GATE[1])])
    tm, tn = min(1024, S), 1024
    dxn = _matmul(dsgu, wpT, tm=tm, tn=512, tk=SEG_SGU[1], after=token, name="dxn_sgu")
    dxn = _matmul(dgl, wpT, b_koff=SEG_GATE[0] // 2048, tm=tm, tn=tn, tk=2048, add=dxn, name="dxn_gate")
    dxn = _matmul(dssd, wpT, b_koff=SEG_SSD[0] // 2048, tm=tm, tn=tn, tk=2048, add=dxn, name="dxn_ssd")
    grad_x, dnorm = _norm_bwd(x2, p['norm_w'], dxn, dh, tm=min(256, S))
    return grad_x, dnorm[0:1]


def _pack(arrs):
    rows, offs, r = [], [], 0
    for a in arrs:
        n = a.size
        nr = -(-n // PACK_ROW) * 8
        rows.append(jnp.pad(a.reshape(-1).astype(F32), (0, nr * LANE - n)).reshape(nr, LANE))
        offs.append(r)
        r += nr
    return jnp.concatenate(rows, axis=0), offs


def kernel(x, norm_w, w_in, gate_b, sgu_norm_g, sgu_norm_b, sgu_w, sgu_b, conv_w, conv_b, dt_bias, A_log, D_skip, ssd_norm_w, w_out, final_norm_w, loss_target, m_norm_w, m_w_in, m_gate_b, m_sgu_norm_g, m_sgu_norm_b, m_sgu_w, m_sgu_b, m_conv_w, m_conv_b, m_dt_bias, m_A_log, m_D_skip, m_ssd_norm_w, m_w_out, m_final_norm_w, v_norm_w, v_w_in, v_gate_b, v_sgu_norm_g, v_sgu_norm_b, v_sgu_w, v_sgu_b, v_conv_w, v_conv_b, v_dt_bias, v_A_log, v_D_skip, v_ssd_norm_w, v_w_out, v_final_norm_w):
    w = dict(norm_w=norm_w, w_in=w_in, gate_b=gate_b, sgu_norm_g=sgu_norm_g, sgu_norm_b=sgu_norm_b, sgu_w=sgu_w, sgu_b=sgu_b,
             conv_w=conv_w, conv_b=conv_b, dt_bias=dt_bias, A_log=A_log, D_skip=D_skip, ssd_norm_w=ssd_norm_w, w_out=w_out,
             final_norm_w=final_norm_w)
    m = dict(norm_w=m_norm_w, w_in=m_w_in, gate_b=m_gate_b, sgu_norm_g=m_sgu_norm_g, sgu_norm_b=m_sgu_norm_b, sgu_w=m_sgu_w,
             sgu_b=m_sgu_b, conv_w=m_conv_w, conv_b=m_conv_b, dt_bias=m_dt_bias, A_log=m_A_log, D_skip=m_D_skip,
             ssd_norm_w=m_ssd_norm_w, w_out=m_w_out, final_norm_w=m_final_norm_w)
    v = dict(norm_w=v_norm_w, w_in=v_w_in, gate_b=v_gate_b, sgu_norm_g=v_sgu_norm_g, sgu_norm_b=v_sgu_norm_b, sgu_w=v_sgu_w,
             sgu_b=v_sgu_b, conv_w=v_conv_w, conv_b=v_conv_b, dt_bias=v_dt_bias, A_log=v_A_log, D_skip=v_D_skip,
             ssd_norm_w=v_ssd_norm_w, w_out=v_w_out, final_norm_w=v_final_norm_w)
    me = 4 * lax.axis_index("x") + 2 * lax.axis_index("y") + lax.axis_index("c")
    shard_cw = XBC_W // N_DEV

    tpose = lambda a: jnp.swapaxes(a[0], 0, 1)
    wT = tpose(w_in).astype(BF16)
    first_group = (GROUP - (me * SHARD_IN) % GROUP) % GROUP
    window = lax.dynamic_slice(jnp.pad(wT, ((0, GROUP), (0, 0))), (first_group, 0), (INTERIOR, D))
    wpT, g_out, g_cw, heads, tails = _gather_weights(window, wT[:GROUP], wT[SHARD_IN - GROUP:], w_out[0].astype(BF16),
                                                     conv_w[0], jnp.zeros((W_ROWS - W_IN, D), BF16))
    wpT = _patch_straddlers(wpT, heads, tails)
    wout_full = g_out.reshape(D, D)
    cw_full = jnp.swapaxes(g_cw, 0, 1).reshape(CONV_K, XBC_W)

    flight = {}

    small = [n for n in WEIGHTS if n not in SHARDED and n != 'norm_w']
    early = {}

    def exchange_small(loss_part, grads, dw_out):
        early['packed'], early['offs'] = _pack([grads[n] for n in small] + [loss_part, grads['conv_w']])
        parts = [jnp.broadcast_to(early['packed'][None], (N_DEV,) + early['packed'].shape), dw_out.reshape(N_DEV, D // N_DEV, D)]
        early['sems'], early['rsems'], early['parts'], early['lands'], token = _exchange_start(parts, name="small_start")
        return token

    def exchange(dw_inT_segs):
        parts = [_to_shards(dw_inT_segs)]
        flight['sems'], flight['rsems'], flight['parts'], flight['lands'], token = _exchange_start(parts, name="exchange_start")
        return token

    grad_x, dnorm = _local_step(x[0], loss_target[0], wpT, wout_full, cw_full, w, exchange_small, exchange)
    (_, own_out), (land_small, land_out) = _exchange_wait(
        early['sems'], early['rsems'], early['parts'], early['lands'], grad_x, name="small_wait")
    (own_in,), (land_in,) = _exchange_wait(
        flight['sems'], flight['rsems'], flight['parts'], flight['lands'], grad_x, name="exchange_wait")
    me_arr = jnp.reshape(me, (1,)).astype(jnp.int32)
    res = {}
    res['w_in'] = [jnp.swapaxes(o, 0, 1) for o in _adamw_own(
        me_arr, own_in, land_in, tpose(w_in), tpose(m_w_in), tpose(v_w_in), tr=SHARD_IN, tc=256, name="adamw_w_in")]
    res['w_out'] = _adamw_own(me_arr, own_out, land_out, w_out[0], m_w_out[0], v_w_out[0], tr=128, tc=D, name="adamw_w_out")

    (norm_parts,) = _all_gather([_pack([dnorm])[0]], name="gather_norm")
    norm_outs = _adamw(norm_parts, *[_pack([d['norm_w']])[0] for d in (w, m, v)], tr=norm_parts.shape[1], name="adamw_norm")
    res['norm_w'] = [o.reshape(-1)[:D].reshape(w['norm_w'].shape) for o in norm_outs]

    offs = early['offs']
    gathered = lax.dynamic_update_slice(land_small, early['packed'][None], (me, 0, 0))
    off_loss, off_cw = offs[-2], offs[-1]
    cw_parts = gathered[:, off_cw:, :].reshape(N_DEV, CONV_K, XBC_W)
    cw_parts = lax.dynamic_slice_in_dim(cw_parts, me * shard_cw, shard_cw, axis=2)
    cw_rows = _pack([cw_parts[0]])[0].shape[0]
    cw_parts = jnp.pad(cw_parts.reshape(N_DEV, -1), ((0, 0), (0, cw_rows * LANE - CONV_K * shard_cw))).reshape(N_DEV, cw_rows, LANE)
    parts = jnp.concatenate([gathered[:, :off_cw, :], cw_parts], axis=1)
    zero = jnp.zeros((), F32)
    packs = [_pack([d[n] for n in small] + [zero, d['conv_w']])[0] for d in (w, m, v)]
    outs = _adamw(parts, *packs, tr=parts.shape[1], name="adamw_small")

    def unpack(o, name):
        if name == 'conv_w':
            return o[off_cw:off_cw + cw_rows].reshape(-1)[:CONV_K * shard_cw].reshape(w['conv_w'].shape)
        r0 = offs[small.index(name)]
        n = w[name].size
        return o[r0:r0 + -(-n // PACK_ROW) * 8].reshape(-1)[:n].reshape(w[name].shape)

    for n in small + ['conv_w']:
        res[n] = [unpack(o, n) for o in outs]
    for n in ('w_in', 'w_out'):
        res[n] = [o[None] for o in res[n]]
    loss = outs[0][off_loss, 0]
    return (loss, grad_x[None], *[res[n][0] for n in WEIGHTS], *[res[n][1] for n in WEIGHTS],
            *[res[n][2] for n in WEIGHTS], *[res[n][3] for n in WEIGHTS])
```

```python
import functools

import numpy as np
import jax
import jax.numpy as jnp
from jax import lax
from jax.experimental import pallas as pl
from jax.experimental.pallas import tpu as pltpu

F32 = jnp.float32
BF16 = jnp.bfloat16
HI = lax.Precision.HIGHEST
MESH = pl.DeviceIdType.MESH

D = 2048
EPS = 1e-5
SGU_BLOCK = 128
SGU_GROUPS = 16
CHUNK = 64
HEADS = 32
HEADDIM = 64
SSD_GROUPS = 4
GROUP_W = D // SSD_GROUPS
STATE = 128
CONV_K = 4
XBC_W = D + 2 * SSD_GROUPS * STATE
W_IN = 15392
N_DEV = 8
SHARD_IN = W_IN // N_DEV
ADAM_LR, ADAM_B1, ADAM_B2, ADAM_EPS, ADAM_WD, ADAM_STEP = 0.001, 0.9, 0.999, 1e-08, 0.01, 10

REF_SGU_END = 3 * D
REF_GATE_START = W_IN - 2 * D
LANE = 128
DT_W = LANE
OFF_U, OFF_V, OFF_ZA, OFF_G0, OFF_G1, OFF_ZB = (i * D for i in range(6))
OFF_XBC = OFF_ZB + D
OFF_DT = OFF_XBC + XBC_W
SEG_SGU = (OFF_U, 3 * D)
SEG_GATE = (OFF_G0, 2 * D)
SEG_SSD = (OFF_ZB, D + XBC_W + DT_W)
WP = SEG_SSD[0] + SEG_SSD[1]
SSD_PAD_W = 3 * D
VMEM_BYTES = 64 * 1024 * 1024
VMEM_LIMIT = VMEM_BYTES - 8 * 1024 * 1024


def _cp(sem=None, vmem=VMEM_LIMIT):
    return pltpu.CompilerParams(dimension_semantics=sem, vmem_limit_bytes=vmem)


def _sigmoid(x):
    return 1.0 / (1.0 + jnp.exp(-x))


def _softplus(x):
    return jnp.maximum(x, 0.0) + jnp.log(1.0 + jnp.exp(-jnp.abs(x)))


def _dot(a, b, precision=None):
    return jnp.dot(a, b, preferred_element_type=F32, precision=precision)


def _dot_nt(a, b, precision=None):
    return lax.dot_general(a, b, (((1,), (1,)), ((), ())), preferred_element_type=F32, precision=precision)


def _dot_tn(a, b, precision=None):
    return lax.dot_general(a, b, (((0,), (0,)), ((), ())), preferred_element_type=F32, precision=precision)


def _split3(a):
    hi = a.astype(BF16)
    r = a - hi.astype(F32)
    mid = r.astype(BF16)
    return hi, mid, (r - mid.astype(F32)).astype(BF16)


def _sel_right(a, sel01):
    m = a.shape[0]
    r = _dot(jnp.concatenate(_split3(a), axis=0), sel01)
    return (r[0:m] + r[m:2 * m]) + r[2 * m:3 * m]


def _sel_right_k(a, sel01_x3):
    return _dot(jnp.concatenate(_split3(a), axis=1), sel01_x3)


def _sel_left(sel01, a):
    n = a.shape[1]
    r = _dot(sel01, jnp.concatenate(_split3(a), axis=1))
    return (r[:, 0:n] + r[:, n:2 * n]) + r[:, 2 * n:3 * n]


def _matmul(a, b, *, trans_a=False, trans_b=False, b_koff=0, out_dtype=F32, tm, tn, tk, add=None, after=None, name):
    K, M = a.shape if trans_a else a.shape[::-1]
    N = b.shape[0] if trans_b else b.shape[1]
    assert M % tm == 0 and N % tn == 0 and K % tk == 0 and not (trans_a and trans_b), (name, M, N, K, tm, tn, tk)
    nk = K // tk

    def body(*refs):
        a_ref, b_ref = refs[:2]
        add_ref = refs[2] if add is not None else None
        o_ref, acc_ref = refs[-2:]
        k = pl.program_id(2)
        if trans_a:
            part = _dot_tn(a_ref[...], b_ref[...])
        else:
            part = _dot_nt(a_ref[...], b_ref[...]) if trans_b else _dot(a_ref[...], b_ref[...])

        def result(r):
            if add_ref is not None:
                r = r + add_ref[...]
            return r.astype(out_dtype)

        if nk == 1:
            o_ref[...] = result(part)
        else:
            @pl.when(k == 0)
            def _():
                acc_ref[...] = part

            @pl.when(jnp.logical_and(k > 0, k < nk - 1))
            def _():
                acc_ref[...] += part

            @pl.when(k == nk - 1)
            def _():
                o_ref[...] = result(acc_ref[...] + part)

    in_specs = [pl.BlockSpec((tk, tm), lambda i, j, k: (k, i)) if trans_a else pl.BlockSpec((tm, tk), lambda i, j, k: (i, k)),
                pl.BlockSpec((tn, tk), lambda i, j, k: (j, k)) if trans_b else pl.BlockSpec((tk, tn), lambda i, j, k: (k + b_koff, j))]
    args = [a, b]
    if add is not None:
        in_specs.append(pl.BlockSpec((tm, tn), lambda i, j, k: (i, j)))
        args.append(add)
    if after is not None:
        in_specs.append(pl.BlockSpec(memory_space=pl.ANY))
        args.append(after)
    return pl.pallas_call(
        body, name=name, grid=(M // tm, N // tn, nk), in_specs=in_specs,
        out_specs=pl.BlockSpec((tm, tn), lambda i, j, k: (i, j)),
        out_shape=jax.ShapeDtypeStruct((M, N), out_dtype),
        scratch_shapes=[pltpu.VMEM((tm, tn), F32)],
        compiler_params=_cp(("parallel", "parallel", "arbitrary")),
    )(*args)


def _dxn_last_norm(seg, wpT, row0, dxn, x, dh, w, *, tm):
    S, cols = seg.shape
    assert S % tm == 0 and row0 % GROUP == 0
    ntile = S // tm

    def body(seg_hbm, w_hbm, dxn_hbm, x_hbm, dh_hbm, nw_ref, gx_hbm, dw_ref, b_ref, abuf, pbuf, xbuf, hbuf, obuf, isem, osem):
        b_copy = pltpu.make_async_copy(w_hbm.at[pl.ds(row0, cols), :], b_ref, osem.at[2])
        b_copy.start()

        def first(t):
            return t * tm if isinstance(t, int) else pl.multiple_of(t * tm, tm)

        def fetch(t):
            rows, slot = pl.ds(first(t), tm), t % 2
            return [pltpu.make_async_copy(src.at[rows, :], buf.at[slot], isem.at[slot, j])
                    for j, (src, buf) in enumerate(((seg_hbm, abuf), (dxn_hbm, pbuf), (x_hbm, xbuf), (dh_hbm, hbuf)))]

        def write(t):
            return pltpu.make_async_copy(obuf.at[t % 2], gx_hbm.at[pl.ds(first(t), tm), :], osem.at[t % 2])

        for cp in fetch(0):
            cp.start()
        dw_ref[...] = jnp.zeros_like(dw_ref)
        b_copy.wait()

        def step(t, carry):
            slot = t % 2
            for cp in fetch(t):
                cp.wait()

            @pl.when(t + 1 < ntile)
            def _():
                for cp in fetch(t + 1):
                    cp.start()

            dxn_v = pbuf[slot] + _dot(abuf[slot], b_ref[...])
            xv = xbuf[slot]
            r = lax.rsqrt(jnp.mean(xv * xv, axis=-1, keepdims=True) + EPS)
            xh = xv * r
            dxh = dxn_v * nw_ref[...]
            gx = hbuf[slot] + r * (dxh - xh * jnp.mean(dxh * xh, axis=-1, keepdims=True))
            dw_ref[0:1, :] += jnp.sum(dxn_v * xh, axis=0, keepdims=True)

            @pl.when(t >= 2)
            def _():
                write(t - 2).wait()

            obuf[slot] = gx
            write(t).start()
            return carry

        lax.fori_loop(0, ntile, step, 0)
        for t in range(max(ntile - 2, 0), ntile):
            write(t).wait()

    anyspec = pl.BlockSpec(memory_space=pl.ANY)
    vmem = pl.BlockSpec(memory_space=pltpu.VMEM)
    return pl.pallas_call(
        body, name="dxn_last_norm", in_specs=[anyspec] * 5 + [vmem], out_specs=[anyspec, vmem],
        out_shape=[jax.ShapeDtypeStruct((S, D), F32), jax.ShapeDtypeStruct((8, D), F32)],
        scratch_shapes=[pltpu.VMEM((cols, D), BF16), pltpu.VMEM((2, tm, cols), BF16), pltpu.VMEM((2, tm, D), F32),
                        pltpu.VMEM((2, tm, D), F32), pltpu.VMEM((2, tm, D), F32), pltpu.VMEM((2, tm, D), F32),
                        pltpu.SemaphoreType.DMA((2, 4)), pltpu.SemaphoreType.DMA((3,))],
        compiler_params=_cp(),
    )(seg, wpT, dxn, x, dh, w)


DW_BUFS = 3


def _dw_in(segs, xn, after, *, tm):
    S = xn.shape[0]
    n = len(segs)
    assert all(a.shape[0] == S and a.shape[1] % tm == 0 for a in segs)

    def body(*refs):
        a_refs, xn_hbm = refs[:n], refs[n]
        o_refs = refs[-(n + 5):-5]
        xn_ref, abuf, obuf, asem, osem = refs[-5:]
        xn_copy = pltpu.make_async_copy(xn_hbm, xn_ref, osem.at[2])
        xn_copy.start()
        for q, (a_ref, o_ref) in enumerate(zip(a_refs, o_refs)):
            ntile = a_ref.shape[1] // tm

            def first(t):
                return t * tm if isinstance(t, int) else pl.multiple_of(t * tm, tm)

            def fetch(t, a_ref=a_ref):
                return pltpu.make_async_copy(a_ref.at[:, pl.ds(first(t), tm)], abuf.at[t % DW_BUFS], asem.at[t % DW_BUFS])

            def write(t, o_ref=o_ref):
                return pltpu.make_async_copy(obuf.at[t % 2], o_ref.at[pl.ds(first(t), tm), :], osem.at[t % 2])

            for t in range(min(DW_BUFS - 1, ntile)):
                fetch(t).start()
            if q == 0:
                xn_copy.wait()

            def step(t, carry, fetch=fetch, write=write, ntile=ntile):
                fetch(t).wait()

                @pl.when(t + DW_BUFS - 1 < ntile)
                def _():
                    fetch(t + DW_BUFS - 1).start()

                res = _dot_tn(abuf[t % DW_BUFS], xn_ref[...]).astype(BF16)

                @pl.when(t >= 2)
                def _():
                    write(t - 2).wait()

                obuf[t % 2] = res
                write(t).start()
                return carry

            lax.fori_loop(0, ntile, step, 0)
            for t in range(max(ntile - 2, 0), ntile):
                write(t).wait()

    anyspec = pl.BlockSpec(memory_space=pl.ANY)
    return pl.pallas_call(
        body, name="dw_in", in_specs=[anyspec] * (n + 1 + (after is not None)), out_specs=[anyspec] * n,
        out_shape=[jax.ShapeDtypeStruct((a.shape[1], D), BF16) for a in segs],
        scratch_shapes=[pltpu.VMEM((S, D), BF16), pltpu.VMEM((DW_BUFS, S, tm), BF16), pltpu.VMEM((2, tm, D), BF16),
                        pltpu.SemaphoreType.DMA((DW_BUFS,)), pltpu.SemaphoreType.DMA((3,))],
        compiler_params=_cp(),
    )(*segs, xn, *([after] if after is not None else []))


def _in_proj(x, w, wpT, *, tm, tn):
    S = x.shape[0]
    N = wpT.shape[0]
    assert S % tm == 0 and N % tn == 0, (S, N, tm, tn)

    def body(x_ref, w_ref, b_ref, xn_ref, o_ref, xs_ref):
        @pl.when(pl.program_id(1) == 0)
        def _():
            xv = x_ref[...]
            r = lax.rsqrt(jnp.mean(xv * xv, axis=-1, keepdims=True) + EPS)
            xs = (xv * r * w_ref[...]).astype(BF16)
            xs_ref[...] = xs
            xn_ref[...] = xs

        o_ref[...] = _dot_nt(xs_ref[...], b_ref[...]).astype(BF16)

    return pl.pallas_call(
        body, name="in_proj", grid=(S // tm, N // tn),
        in_specs=[pl.BlockSpec((tm, D), lambda i, j: (i, 0)), pl.BlockSpec((1, D), lambda i, j: (0, 0)),
                  pl.BlockSpec((tn, D), lambda i, j: (j, 0))],
        out_specs=[pl.BlockSpec((tm, D), lambda i, j: (i, 0)), pl.BlockSpec((tm, tn), lambda i, j: (i, j))],
        out_shape=[jax.ShapeDtypeStruct((S, D), BF16), jax.ShapeDtypeStruct((S, N), BF16)],
        scratch_shapes=[pltpu.VMEM((tm, D), BF16)],
        compiler_params=_cp(("parallel", "arbitrary")),
    )(x, w, wpT)


def _sgu_core(u_ref, v_ref, z_ref, g_ref, b_ref, wm_ref, bias_ref, vnb_ref, mixed_ref, tm):
    v = v_ref[...].astype(F32)
    mu = jnp.mean(v, axis=-1, keepdims=True)
    vc = v - mu
    rs = lax.rsqrt(jnp.mean(vc * vc, axis=-1, keepdims=True) + EPS)
    vh = vc * rs
    vnb_ref[...] = (vh * g_ref[...] + b_ref[...]).astype(BF16)
    for blk in range(tm // SGU_BLOCK):
        rows = pl.ds(blk * SGU_BLOCK, SGU_BLOCK)
        for gi in range(SGU_GROUPS):
            cols = pl.ds(gi * LANE, LANE)
            mixed_ref[rows, cols] = _dot(wm_ref[gi], vnb_ref[rows, cols]) + bias_ref[:, cols]
    return vh, rs


def _sgu_fwd(proj, g, b, wm, bias_full, *, tm):
    S = proj.shape[0]

    def body(u_ref, v_ref, z_ref, g_ref, b_ref, wm_ref, bias_ref, y_ref, vnb_ref, mixed_ref):
        _sgu_core(u_ref, v_ref, z_ref, g_ref, b_ref, wm_ref, bias_ref, vnb_ref, mixed_ref, tm)
        z = z_ref[...].astype(F32)
        y_ref[...] = (u_ref[...].astype(F32) * mixed_ref[...] * (z * _sigmoid(z))).astype(BF16)

    seg = lambda off: pl.BlockSpec((tm, D), lambda i: (i, off // D))
    full = lambda a: pl.BlockSpec(a.shape, lambda i: (0,) * a.ndim)
    return pl.pallas_call(
        body, name="sgu_fwd", grid=(S // tm,),
        in_specs=[seg(OFF_U), seg(OFF_V), seg(OFF_ZA), full(g), full(b), full(wm), full(bias_full)],
        out_specs=pl.BlockSpec((tm, D), lambda i: (i, 0)),
        out_shape=jax.ShapeDtypeStruct((S, D), BF16),
        scratch_shapes=[pltpu.VMEM((tm, D), BF16), pltpu.VMEM((tm, D), F32)],
        compiler_params=_cp(("parallel",)),
    )(proj, proj, proj, g, b, wm, bias_full)


def _sgu_bwd(proj, dy, g, b, wm, wmT, bias_full, mask, sel, *, tm):
    S = proj.shape[0]
    nsteps = S // tm

    def body(u_ref, v_ref, z_ref, dy_ref, g_ref, b_ref, wm_ref, wmT_ref, bias_ref, mask_ref, sel_ref,
             dp_ref, dws_ref, dbs_ref, dg_ref, db_ref, vnb_ref, mixed_ref, dmb_ref, dvn_ref, dbias_ref):
        i = pl.program_id(0)

        @pl.when(i == 0)
        def _():
            dws_ref[...] = jnp.zeros_like(dws_ref)
            dg_ref[...] = jnp.zeros_like(dg_ref)
            db_ref[...] = jnp.zeros_like(db_ref)
            dbias_ref[...] = jnp.zeros_like(dbias_ref)

        vh, rs = _sgu_core(u_ref, v_ref, z_ref, g_ref, b_ref, wm_ref, bias_ref, vnb_ref, mixed_ref, tm)
        u = u_ref[...].astype(F32)
        z = z_ref[...].astype(F32)
        dy_v = dy_ref[...].astype(F32)
        mixed = mixed_ref[...]
        sg = _sigmoid(z)
        sz = z * sg
        dp_ref[:, 0:D] = (dy_v * mixed * sz).astype(BF16)
        dp_ref[:, 2 * D:3 * D] = (dy_v * u * mixed * (sg * (1.0 + z * (1.0 - sg)))).astype(BF16)
        dmixed = dy_v * u * sz
        dmb_ref[...] = dmixed.astype(BF16)
        for blk in range(tm // SGU_BLOCK):
            dbias_ref[...] += dmixed[blk * SGU_BLOCK:(blk + 1) * SGU_BLOCK, :]
        for blk in range(tm // SGU_BLOCK):
            rows = pl.ds(blk * SGU_BLOCK, SGU_BLOCK)
            for gi in range(SGU_GROUPS):
                cols = pl.ds(gi * LANE, LANE)
                dm = dmb_ref[rows, cols]
                dvn_ref[rows, cols] = _dot(wmT_ref[gi], dm)
                dws_ref[gi] += _dot_nt(dm, vnb_ref[rows, cols])
        dvn = dvn_ref[...]
        dg_ref[0:1, :] += jnp.sum(dvn * vh, axis=0, keepdims=True)
        db_ref[0:1, :] += jnp.sum(dvn, axis=0, keepdims=True)
        dvh = dvn * g_ref[...]
        dv = rs * (dvh - jnp.mean(dvh, axis=-1, keepdims=True) - vh * jnp.mean(dvh * vh, axis=-1, keepdims=True))
        dp_ref[:, D:2 * D] = dv.astype(BF16)

        @pl.when(i == nsteps - 1)
        def _():
            for gi in range(SGU_GROUPS):
                dws_ref[gi] = dws_ref[gi] * mask_ref[...]
            dbs_ref[...] = _dot(dbias_ref[...], sel_ref[...], precision=HI)

    seg = lambda off: pl.BlockSpec((tm, D), lambda i: (i, off // D))
    full = lambda a: pl.BlockSpec(a.shape, lambda i: (0,) * a.ndim)
    return pl.pallas_call(
        body, name="sgu_bwd", grid=(nsteps,),
        in_specs=[seg(OFF_U), seg(OFF_V), seg(OFF_ZA), pl.BlockSpec((tm, D), lambda i: (i, 0)),
                  full(g), full(b), full(wm), full(wmT), full(bias_full), full(mask), full(sel)],
        out_specs=[pl.BlockSpec((tm, 3 * D), lambda i: (i, 0)),
                   pl.BlockSpec((SGU_GROUPS, SGU_BLOCK, SGU_BLOCK), lambda i: (0, 0, 0)),
                   pl.BlockSpec((SGU_BLOCK, LANE), lambda i: (0, 0)),
                   pl.BlockSpec((8, D), lambda i: (0, 0)), pl.BlockSpec((8, D), lambda i: (0, 0))],
        out_shape=[jax.ShapeDtypeStruct((S, 3 * D), BF16),
                   jax.ShapeDtypeStruct((SGU_GROUPS, SGU_BLOCK, SGU_BLOCK), F32),
                   jax.ShapeDtypeStruct((SGU_BLOCK, LANE), F32),
                   jax.ShapeDtypeStruct((8, D), F32), jax.ShapeDtypeStruct((8, D), F32)],
        scratch_shapes=[pltpu.VMEM((tm, D), BF16), pltpu.VMEM((tm, D), F32), pltpu.VMEM((tm, D), BF16),
                        pltpu.VMEM((tm, D), F32), pltpu.VMEM((SGU_BLOCK, D), F32)],
        compiler_params=_cp(("arbitrary",)),
    )(proj, proj, proj, dy, g, b, wm, wmT, bias_full, mask, sel)


SSD_T = 2 * CHUNK
HALO = 8
HALO_BLK = 16


def _pair_masks():
    row = lax.broadcasted_iota(jnp.int32, (CHUNK, LANE), 0)
    lane = lax.broadcasted_iota(jnp.int32, (CHUNK, LANE), 1)
    pos = jnp.where(lane >= CHUNK, lane - CHUNK, lane)
    diag = (row == pos).astype(F32)
    causal = row >= pos
    lo = (lane < CHUNK).astype(F32)
    return diag, causal, lo, 1.0 - lo


def _ssd_chunk_fwd(c, ext_ref, shift_ref, dt_ref, cw_ref, cb_ref, dtb_ref, alog_ref, tri_ref, exp_ref):
    r0 = c * CHUNK
    win = ext_ref[pl.ds(r0, HALO_BLK + CHUNK), :]
    sh = _dot(shift_ref[...], win)
    taps = [sh[k * CHUNK:(k + 1) * CHUNK] for k in range(CONV_K - 1)] + [win[HALO_BLK:].astype(F32)]
    pre = cb_ref[...] + sum(cw_ref[k:k + 1, :] * taps[k] for k in range(CONV_K))
    sg = _sigmoid(pre)
    xc = pre * sg
    dtr = dt_ref[pl.ds(r0, CHUNK), :].astype(F32) + dtb_ref[...]
    dtv = _softplus(dtr)
    A = -jnp.exp(alog_ref[...])
    acs = _sel_left(tri_ref[...], dtv * A)
    both = _sel_right_k(jnp.concatenate([acs, dtv], axis=0), exp_ref[...])
    E, dtE = both[0:CHUNK], both[CHUNK:2 * CHUNK]
    return dict(taps=taps, pre=pre, sg=sg, xc=xc, dtr=dtr, dtv=dtv, A=A, E=E, dtE=dtE)


def _ssd_fwd(proj, conv_w, conv_b, dtb_p, alog_p, d_exp, norm_w, tri, expand, shift):
    S = proj.shape[0]
    T = SSD_T
    nsteps = S // T
    ncl = T // CHUNK

    def body(zb_ref, xbc_ref, halo_ref, dt_ref, cw_ref, cb_ref, dtb_ref, alog_ref, dexp_ref, nw_ref, tri_ref, exp_ref, shift_ref,
             y_ref, yb_ref, st_ref, ht_ref, ext_ref):
        i = pl.program_id(0)

        @pl.when(i == 0)
        def _():
            ht_ref[...] = jnp.zeros_like(ht_ref)
            ext_ref[0:HALO_BLK, :] = jnp.zeros((HALO_BLK, XBC_W), BF16)

        @pl.when(i > 0)
        def _():
            ext_ref[0:HALO_BLK, :] = halo_ref[...]

        ext_ref[HALO_BLK:HALO_BLK + T, :] = xbc_ref[...]
        diag, causal, lo, hi = _pair_masks()
        for c in range(ncl):
            q = _ssd_chunk_fwd(c, ext_ref, shift_ref, dt_ref, cw_ref, cb_ref, dtb_ref, alog_ref, tri_ref, exp_ref)
            rows = pl.ds(c * CHUNK, CHUNK)
            xc, E, dtE = q["xc"], q["E"], q["dtE"]
            xs = xc[:, 0:D]
            total = E[CHUNK - 1:CHUNK, :]
            x_dt = xs * dtE
            eE = jnp.exp(E)
            xw = x_dt * jnp.exp(total - E)
            st_ref[c] = ht_ref[...]
            for g in range(SSD_GROUPS):
                gc = slice(g * GROUP_W, (g + 1) * GROUP_W)
                Bg = xc[:, D + g * STATE:D + (g + 1) * STATE].astype(BF16)
                Cg = xc[:, D + SSD_GROUPS * STATE + g * STATE:D + SSD_GROUPS * STATE + (g + 1) * STATE].astype(BF16)
                cb2 = _dot_nt(Cg, jnp.concatenate([Bg, Bg], axis=0))
                htg = ht_ref[:, gc]
                y_ref[rows, gc] = eE[:, gc] * _dot(Cg, htg.astype(BF16)) + xs[:, gc] * dexp_ref[:, gc]
                for jj in range(GROUP_W // LANE):
                    pc = slice(g * GROUP_W + jj * LANE, g * GROUP_W + (jj + 1) * LANE)
                    Ej = E[:, pc]
                    e2 = jnp.sum(Ej * diag, axis=0, keepdims=True)
                    Mp = cb2 * jnp.exp(jnp.where(causal, Ej - e2, -1e30))
                    xj = x_dt[:, pc]
                    xbd = jnp.concatenate([xj * lo, xj * hi], axis=0).astype(BF16)
                    y_ref[rows, pc] += _dot(Mp.astype(BF16), xbd)
                ht_ref[:, gc] = jnp.exp(total[:, gc]) * htg + _dot_tn(Bg, xw[:, gc].astype(BF16))
            zb = zb_ref[rows, :].astype(F32)
            hh = y_ref[rows, :] * (zb * _sigmoid(zb))
            for g in range(SSD_GROUPS):
                gc = slice(g * GROUP_W, (g + 1) * GROUP_W)
                hg = hh[:, gc]
                r = lax.rsqrt(jnp.mean(hg * hg, axis=-1, keepdims=True) + EPS)
                yb_ref[rows, gc] = (hg * r * nw_ref[:, gc]).astype(BF16)

    full = lambda a: pl.BlockSpec(a.shape, lambda i: (0,) * a.ndim)
    hb = T // HALO_BLK
    return pl.pallas_call(
        body, name="ssd_fwd", grid=(nsteps,),
        in_specs=[pl.BlockSpec((T, D), lambda i: (i, OFF_ZB // D)),
                  pl.BlockSpec((T, XBC_W), lambda i: (i, OFF_XBC // XBC_W)),
                  pl.BlockSpec((HALO_BLK, XBC_W), lambda i: (jnp.maximum(i * hb - 1, 0), OFF_XBC // XBC_W)),
                  pl.BlockSpec((T, DT_W), lambda i: (i, OFF_DT // DT_W)),
                  full(conv_w), full(conv_b), full(dtb_p), full(alog_p), full(d_exp), full(norm_w), full(tri), full(expand),
                  full(shift)],
        out_specs=[pl.BlockSpec((T, D), lambda i: (i, 0)), pl.BlockSpec((T, D), lambda i: (i, 0)),
                   pl.BlockSpec((ncl, STATE, D), lambda i: (i, 0, 0))],
        out_shape=[jax.ShapeDtypeStruct((S, D), F32), jax.ShapeDtypeStruct((S, D), BF16),
                   jax.ShapeDtypeStruct((S // CHUNK, STATE, D), F32)],
        scratch_shapes=[pltpu.VMEM((STATE, D), F32), pltpu.VMEM((HALO_BLK + T, XBC_W), BF16)],
        compiler_params=_cp(("arbitrary",)),
    )(proj, proj, proj, proj, conv_w, conv_b, dtb_p, alog_p, d_exp, norm_w, tri, expand, shift)


def _ssd_bwd(proj, dyb, y, states, conv_w, conv_b, dtb_p, alog_p, d_exp, norm_w, tri, triT, expand, expandT, shift):
    S = proj.shape[0]
    T = SSD_T
    nsteps = S // T
    ncl = T // CHUNK
    SSD_W = SSD_PAD_W

    def body(zb_ref, xbc_ref, halo_ref, dt_ref, dyb_ref, y_ref, st_ref, cw_ref, cb_ref, dtb_ref, alog_ref, dexp_ref, nw_ref,
             tri_ref, triT_ref, exp_ref, expT_ref, shift_ref,
             dp_ref, dcw_ref, dcb_ref, ddtb_ref, dalog_ref, dD_ref, dnw_ref,
             dht_ref, ext_ref, dpre_ref, dy_s, dE_s, dxdt_s, dxc_s, dDacc_ref, dAacc_ref):
        i = pl.program_id(0)

        @pl.when(i == 0)
        def _():
            for r in (dht_ref, dcw_ref, dcb_ref, ddtb_ref, dnw_ref, dDacc_ref, dAacc_ref):
                r[...] = jnp.zeros_like(r)
            dpre_ref[T:T + HALO_BLK, :] = jnp.zeros((HALO_BLK, XBC_W), F32)

        @pl.when(i == nsteps - 1)
        def _():
            ext_ref[0:HALO_BLK, :] = jnp.zeros((HALO_BLK, XBC_W), BF16)

        @pl.when(i < nsteps - 1)
        def _():
            ext_ref[0:HALO_BLK, :] = halo_ref[...]

        ext_ref[HALO_BLK:HALO_BLK + T, :] = xbc_ref[...]
        diag, causal, lo, hi = _pair_masks()
        last_row = (lax.broadcasted_iota(jnp.int32, (CHUNK, 1), 0) == CHUNK - 1).astype(F32)
        for c in reversed(range(ncl)):
            q = _ssd_chunk_fwd(c, ext_ref, shift_ref, dt_ref, cw_ref, cb_ref, dtb_ref, alog_ref, tri_ref, exp_ref)
            rows = pl.ds(c * CHUNK, CHUNK)
            pre, sg, xc, dtr, dtv, A, E, dtE = (q[k] for k in ("pre", "sg", "xc", "dtr", "dtv", "A", "E", "dtE"))
            xs = xc[:, 0:D]
            total = E[CHUNK - 1:CHUNK, :]
            x_dt = xs * dtE
            eE = jnp.exp(E)
            wdec = jnp.exp(total - E)
            zb = zb_ref[rows, :].astype(F32)
            yv = y_ref[rows, :]
            sgz = _sigmoid(zb)
            sz = zb * sgz
            hh = yv * sz
            for g in range(SSD_GROUPS):
                gc = slice(g * GROUP_W, (g + 1) * GROUP_W)
                hg = hh[:, gc]
                r = lax.rsqrt(jnp.mean(hg * hg, axis=-1, keepdims=True) + EPS)
                dyb_g = dyb_ref[rows, gc].astype(F32)
                dn = dyb_g * nw_ref[:, gc]
                dnw_ref[0:1, gc] += jnp.sum(dyb_g * hg * r, axis=0, keepdims=True)
                dy_s[:, gc] = r * dn - hg * (r * r * r) * jnp.mean(dn * hg, axis=-1, keepdims=True)
            dhh = dy_s[...]
            dp_ref[rows, 0:D] = (dhh * yv * (sgz * (1.0 + zb * (1.0 - sgz)))).astype(BF16)
            dy = dhh * sz
            dy_s[...] = dy
            dDacc_ref[0:1, :] += jnp.sum(dy * xs, axis=0, keepdims=True)
            dxc_s[:, 0:D] = dy * dexp_ref[...]
            for g in range(SSD_GROUPS):
                gc = slice(g * GROUP_W, (g + 1) * GROUP_W)
                bcol = slice(D + g * STATE, D + (g + 1) * STATE)
                ccol = slice(D + SSD_GROUPS * STATE + g * STATE, D + SSD_GROUPS * STATE + (g + 1) * STATE)
                Bg = xc[:, bcol].astype(BF16)
                Cg = xc[:, ccol].astype(BF16)
                B2 = jnp.concatenate([Bg, Bg], axis=0)
                cb2 = _dot_nt(Cg, B2)
                htg = st_ref[c, :, gc]
                htb = htg.astype(BF16)
                dhn = dht_ref[:, gc]
                dhnb = dhn.astype(BF16)
                dyg = dy[:, gc]
                eEg = eE[:, gc]
                wg = wdec[:, gc]
                xdg = x_dt[:, gc]
                CH = _dot(Cg, htb)
                dCHb = (dyg * eEg).astype(BF16)
                dC = _dot_nt(dCHb, htb)
                dl = jnp.exp(total[:, gc])
                dht_prev = _dot_tn(Cg, dCHb) + dl * dhn
                dtot = jnp.sum(dhn * htg, axis=0, keepdims=True) * dl
                dxw = _dot(Bg, dhnb)
                dB = _dot_nt((xdg * wg).astype(BF16), dhnb)
                dwd = dxw * xdg * wg
                dtot = dtot + jnp.sum(dwd, axis=0, keepdims=True)
                dE_s[:, gc] = dyg * eEg * CH - dwd + last_row * dtot
                dxdt_s[:, gc] = dxw * wg
                dcb2 = jnp.zeros((CHUNK, LANE), F32)
                for jj in range(GROUP_W // LANE):
                    pc = slice(g * GROUP_W + jj * LANE, g * GROUP_W + (jj + 1) * LANE)
                    Ej = E[:, pc]
                    e2 = jnp.sum(Ej * diag, axis=0, keepdims=True)
                    Lp = jnp.exp(jnp.where(causal, Ej - e2, -1e30))
                    Mp = cb2 * Lp
                    xj = x_dt[:, pc]
                    xbd = jnp.concatenate([xj * lo, xj * hi], axis=0).astype(BF16)
                    dyj = dy[:, pc].astype(BF16)
                    dMp = _dot_nt(dyj, xbd)
                    dxbd = _dot_tn(Mp.astype(BF16), dyj)
                    dxdt_s[:, pc] += dxbd[0:CHUNK, :] * lo + dxbd[CHUNK:2 * CHUNK, :] * hi
                    dcb2 = dcb2 + dMp * Lp
                    dseg = dMp * Mp
                    dE_s[:, pc] += dseg - diag * jnp.sum(dseg, axis=0, keepdims=True)
                dcb2b = dcb2.astype(BF16)
                dC = dC + _dot(dcb2b, B2)
                dB2 = _dot_tn(dcb2b, Cg)
                dB = dB + dB2[0:CHUNK, :] + dB2[CHUNK:2 * CHUNK, :]
                dxc_s[:, bcol] = dB
                dxc_s[:, ccol] = dC
                dht_ref[:, gc] = dht_prev
            dx_dt = dxdt_s[...]
            dxc_s[:, 0:D] += dx_dt * dtE
            red = _sel_right(jnp.concatenate([dE_s[...], dx_dt * xs], axis=0), expT_ref[...])
            da = _sel_left(triT_ref[...], red[0:CHUNK, :])
            ddtv = red[CHUNK:2 * CHUNK, :] + da * A
            dAacc_ref[0:1, :] += jnp.sum(da * dtv, axis=0, keepdims=True)
            ddtr = ddtv * _sigmoid(dtr)
            ddtb_ref[0:1, :] += jnp.sum(ddtr, axis=0, keepdims=True)
            dp_ref[rows, D + XBC_W:D + XBC_W + DT_W] = ddtr.astype(BF16)
            dpre = dxc_s[...] * (sg * (1.0 + pre * (1.0 - sg)))
            dpre_ref[rows, :] = dpre
            dcb_ref[0:1, :] += jnp.sum(dpre, axis=0, keepdims=True)
            for k in range(CONV_K):
                dcw_ref[k:k + 1, :] += jnp.sum(dpre * q["taps"][k], axis=0, keepdims=True)
        dxbc = jnp.zeros((T, XBC_W), F32)
        for k in range(CONV_K):
            dxbc = dxbc + cw_ref[k:k + 1, :] * dpre_ref[pl.ds(CONV_K - 1 - k, T), :]
        dp_ref[:, D:D + XBC_W] = dxbc.astype(BF16)
        dp_ref[:, SEG_SSD[1]:SSD_W] = jnp.zeros((T, SSD_W - SEG_SSD[1]), BF16)
        dpre_ref[T:T + HALO, :] = dpre_ref[0:HALO, :]

        @pl.when(i == nsteps - 1)
        def _():
            dalog_ref[...] = dAacc_ref[...] * (-jnp.exp(alog_ref[...]))
            dD_ref[...] = _dot(dDacc_ref[...], expT_ref[...].astype(F32), precision=HI)

    full = lambda a: pl.BlockSpec(a.shape, lambda i: (0,) * a.ndim)
    hb = T // HALO_BLK
    rev = lambda i: nsteps - 1 - i
    acc = lambda w: pl.BlockSpec((8, w), lambda i: (0, 0))
    return pl.pallas_call(
        body, name="ssd_bwd", grid=(nsteps,),
        in_specs=[pl.BlockSpec((T, D), lambda i: (rev(i), OFF_ZB // D)),
                  pl.BlockSpec((T, XBC_W), lambda i: (rev(i), OFF_XBC // XBC_W)),
                  pl.BlockSpec((HALO_BLK, XBC_W), lambda i: (jnp.maximum(rev(i) * hb - 1, 0), OFF_XBC // XBC_W)),
                  pl.BlockSpec((T, DT_W), lambda i: (rev(i), OFF_DT // DT_W)),
                  pl.BlockSpec((T, D), lambda i: (rev(i), 0)), pl.BlockSpec((T, D), lambda i: (rev(i), 0)),
                  pl.BlockSpec((ncl, STATE, D), lambda i: (rev(i), 0, 0)),
                  full(conv_w), full(conv_b), full(dtb_p), full(alog_p), full(d_exp), full(norm_w),
                  full(tri), full(triT), full(expand), full(expandT), full(shift)],
        out_specs=[pl.BlockSpec((T, SSD_W), lambda i: (rev(i), 0)),
                   acc(XBC_W), acc(XBC_W), acc(DT_W), acc(DT_W), acc(DT_W), acc(D)],
        out_shape=[jax.ShapeDtypeStruct((S, SSD_W), BF16),
                   jax.ShapeDtypeStruct((8, XBC_W), F32), jax.ShapeDtypeStruct((8, XBC_W), F32),
                   jax.ShapeDtypeStruct((8, DT_W), F32), jax.ShapeDtypeStruct((8, DT_W), F32),
                   jax.ShapeDtypeStruct((8, DT_W), F32), jax.ShapeDtypeStruct((8, D), F32)],
        scratch_shapes=[pltpu.VMEM((STATE, D), F32), pltpu.VMEM((HALO_BLK + T, XBC_W), BF16), pltpu.VMEM((T + HALO_BLK, XBC_W), F32),
                        pltpu.VMEM((CHUNK, D), F32), pltpu.VMEM((CHUNK, D), F32), pltpu.VMEM((CHUNK, D), F32),
                        pltpu.VMEM((CHUNK, XBC_W), F32), pltpu.VMEM((8, D), F32), pltpu.VMEM((8, DT_W), F32)],
        compiler_params=_cp(("arbitrary",)),
    )(proj, proj, proj, proj, dyb, y, states, conv_w, conv_b, dtb_p, alog_p, d_exp, norm_w, tri, triT, expand, expandT, shift)


def _head(x, ya, yb, proj, target, gate_b, wout, fw, *, tm):
    S = x.shape[0]

    def body(x_ref, ya_ref, yb_ref, gl0_ref, gl1_ref, t_ref, gb_ref, w_ref, fw_ref,
             dh_ref, dhb_ref, mb_ref, dya_ref, dyb_ref, dgl_ref, loss_ref, dfw_ref, dgb_ref):
        @pl.when(pl.program_id(0) == 0)
        def _():
            loss_ref[...] = jnp.zeros_like(loss_ref)
            dfw_ref[...] = jnp.zeros_like(dfw_ref)
            dgb_ref[...] = jnp.zeros_like(dgb_ref)

        ya_v = ya_ref[...].astype(F32)
        yb_v = yb_ref[...].astype(F32)
        g0 = _sigmoid(gl0_ref[...].astype(F32) + gb_ref[:, 0:D])
        g1 = _sigmoid(gl1_ref[...].astype(F32) + gb_ref[:, D:2 * D])
        mb = (g0 * ya_v + g1 * yb_v).astype(BF16)
        mb_ref[...] = mb
        h = x_ref[...] + _dot(mb, w_ref[...])
        r = lax.rsqrt(jnp.mean(h * h, axis=-1, keepdims=True) + EPS)
        hn = h * r
        err = hn * fw_ref[...] - t_ref[...]
        loss_ref[...] += 0.5 * jnp.sum(jnp.mean(err * err, axis=-1, keepdims=True))
        dyf = err * (1.0 / D)
        dfw_ref[0:1, :] += jnp.sum(dyf * hn, axis=0, keepdims=True)
        dhn = dyf * fw_ref[...]
        dh = r * (dhn - hn * jnp.mean(dhn * hn, axis=-1, keepdims=True))
        dh_ref[...] = dh
        dhb = dh.astype(BF16)
        dhb_ref[...] = dhb
        dm = _dot_nt(dhb, w_ref[...])
        dya_ref[...] = (dm * g0).astype(BF16)
        dyb_ref[...] = (dm * g1).astype(BF16)
        dgl0 = dm * ya_v * g0 * (1.0 - g0)
        dgl1 = dm * yb_v * g1 * (1.0 - g1)
        dgl_ref[:, 0:D] = dgl0.astype(BF16)
        dgl_ref[:, D:2 * D] = dgl1.astype(BF16)
        dgb_ref[0:1, 0:D] += jnp.sum(dgl0, axis=0, keepdims=True)
        dgb_ref[0:1, D:2 * D] += jnp.sum(dgl1, axis=0, keepdims=True)

    row = pl.BlockSpec((tm, D), lambda i: (i, 0))
    seg = lambda off: pl.BlockSpec((tm, D), lambda i: (i, off // D))
    full = lambda a: pl.BlockSpec(a.shape, lambda i: (0,) * a.ndim)
    acc = lambda w: pl.BlockSpec((8, w), lambda i: (0, 0))
    return pl.pallas_call(
        body, name="head", grid=(S // tm,),
        in_specs=[row, row, row, seg(OFF_G0), seg(OFF_G1), row, full(gate_b), full(wout), full(fw)],
        out_specs=[row, row, row, row, row, pl.BlockSpec((tm, 2 * D), lambda i: (i, 0)), acc(LANE), acc(D), acc(2 * D)],
        out_shape=[jax.ShapeDtypeStruct((S, D), F32), jax.ShapeDtypeStruct((S, D), BF16), jax.ShapeDtypeStruct((S, D), BF16),
                   jax.ShapeDtypeStruct((S, D), BF16), jax.ShapeDtypeStruct((S, D), BF16), jax.ShapeDtypeStruct((S, 2 * D), BF16),
                   jax.ShapeDtypeStruct((8, LANE), F32), jax.ShapeDtypeStruct((8, D), F32), jax.ShapeDtypeStruct((8, 2 * D), F32)],
        compiler_params=_cp(("arbitrary",)),
    )(x, ya, yb, proj, proj, target, gate_b, wout, fw)


def _adam_update(g, w_ref, m_ref, v_ref, g_ref, d_ref, m2_ref, v2_ref):
    m2 = ADAM_B1 * m_ref[...] + (1.0 - ADAM_B1) * g
    v2 = ADAM_B2 * v_ref[...] + (1.0 - ADAM_B2) * (g * g)
    m_hat = m2 / (1.0 - ADAM_B1 ** ADAM_STEP)
    v_hat = v2 / (1.0 - ADAM_B2 ** ADAM_STEP)
    g_ref[...] = g
    d_ref[...] = -ADAM_LR * (m_hat / (jnp.sqrt(v_hat) + ADAM_EPS) + ADAM_WD * w_ref[...])
    m2_ref[...] = m2
    v2_ref[...] = v2


def _adamw_own(me, own, landed, w, m, v, *, tr, tc, name):
    _, R, C = landed.shape
    assert R % tr == 0 and C % tc == 0, (name, R, C, tr, tc)

    def body(me_ref, own_ref, p_ref, w_ref, m_ref, v_ref, g_ref, d_ref, m2_ref, v2_ref):
        mine = own_ref[0].astype(F32)
        g = jnp.where(me_ref[0] == 0, mine, p_ref[0].astype(F32))
        for k in range(1, N_DEV):
            g = g + jnp.where(me_ref[0] == k, mine, p_ref[k].astype(F32))
        _adam_update(g, w_ref, m_ref, v_ref, g_ref, d_ref, m2_ref, v2_ref)

    tile = pl.BlockSpec((tr, tc), lambda i, j, me_ref: (i, j))
    return pl.pallas_call(
        body, name=name,
        grid_spec=pltpu.PrefetchScalarGridSpec(
            num_scalar_prefetch=1, grid=(R // tr, C // tc),
            in_specs=[pl.BlockSpec((1, tr, tc), lambda i, j, me_ref: (me_ref[0], i, j)),
                      pl.BlockSpec((N_DEV, tr, tc), lambda i, j, me_ref: (0, i, j)), tile, tile, tile],
            out_specs=[tile, tile, tile, tile]),
        out_shape=[jax.ShapeDtypeStruct((R, C), F32)] * 4,
        compiler_params=_cp(("parallel", "parallel")),
    )(me, own, landed, w, m, v)


def _adamw(parts, w, m, v, *, tr, name):
    _, R, C = parts.shape
    assert R % tr == 0, (name, R, tr)

    def body(p_ref, w_ref, m_ref, v_ref, g_ref, d_ref, m2_ref, v2_ref):
        g = p_ref[0].astype(F32)
        for k in range(1, N_DEV):
            g = g + p_ref[k].astype(F32)
        _adam_update(g, w_ref, m_ref, v_ref, g_ref, d_ref, m2_ref, v2_ref)

    row = pl.BlockSpec((tr, C), lambda i: (i, 0))
    return pl.pallas_call(
        body, name=name, grid=(R // tr,),
        in_specs=[pl.BlockSpec((N_DEV, tr, C), lambda i: (0, i, 0)), row, row, row],
        out_specs=[row, row, row, row],
        out_shape=[jax.ShapeDtypeStruct((R, C), F32)] * 4,
        compiler_params=_cp(("parallel",)),
    )(parts, w, m, v)


def _place():
    x, y, c = lax.axis_index("x"), lax.axis_index("y"), lax.axis_index("c")
    return x, y, c


def _all_gather(arrs, *, name):
    n = len(arrs)

    def body(*refs):
        ins, outs = refs[:n], refs[n:2 * n]
        send_sems, recv_sems, local_sems = refs[2 * n:]
        x, y, c = _place()
        me, sibling = (x, y, c), (x, y, 1 - c)
        chips = [(1 - x, y), (x, 1 - y), (1 - x, 1 - y)]

        def idx(px, py, pc):
            return 4 * px + 2 * py + pc

        def copy(k, a, block, to, src=None):
            slab = outs[a].at[idx(*block)]
            return pltpu.make_async_remote_copy(
                src_ref=slab if src is None else src, dst_ref=slab,
                send_sem=send_sems.at[k, a], recv_sem=recv_sems.at[k, a], device_id=to, device_id_type=MESH)

        mine = [pltpu.make_async_copy(ins[a], outs[a].at[idx(*me)], local_sems.at[a]) for a in range(n)]
        for cp in mine:
            cp.start()
        first = []
        for a in range(n):
            first.append(copy(0, a, me, sibling, src=ins[a]))
            first += [copy(1 + j, a, me, (*chip, c), src=ins[a]) for j, chip in enumerate(chips)]
        for cp in first:
            cp.start()
        passed = []
        for j, chip in enumerate(chips):
            for a in range(n):
                copy(1 + j, a, (*chip, c), me).wait_recv()
                fwd = copy(4 + j, a, (*chip, c), sibling)
                fwd.start()
                passed.append(fwd)
        for a in range(n):
            copy(0, a, sibling, me).wait_recv()
            for j, chip in enumerate(chips):
                copy(4 + j, a, (*chip, 1 - c), me).wait_recv()
        for cp in first + passed:
            cp.wait_send()
        for cp in mine:
            cp.wait()

    anyspec = pl.BlockSpec(memory_space=pl.ANY)
    return pl.pallas_call(
        body, name=name,
        in_specs=[anyspec] * n, out_specs=[anyspec] * n,
        out_shape=[jax.ShapeDtypeStruct((N_DEV,) + a.shape, a.dtype) for a in arrs],
        scratch_shapes=[pltpu.SemaphoreType.DMA((7, n)), pltpu.SemaphoreType.DMA((7, n)), pltpu.SemaphoreType.DMA((n,))],
    )(*arrs)


W_ROWS = SEG_SSD[0] + SSD_PAD_W


GROUP = 16
INTERIOR = 1920


def _interior(k):
    lo = -(-(k * SHARD_IN) // GROUP) * GROUP
    hi = ((k + 1) * SHARD_IN) // GROUP * GROUP
    return lo, hi


def _dest_row(r):
    if r < REF_SGU_END:
        return r
    return r - REF_SGU_END + SEG_SSD[0] if r < REF_GATE_START else r - REF_GATE_START + SEG_GATE[0]


def _shard_pieces(k):
    lo_k, hi_k = _interior(k)
    out = []
    for lo, hi in ((0, REF_SGU_END), (REF_SGU_END, REF_GATE_START), (REF_GATE_START, W_IN)):
        a, b = max(lo, lo_k), min(hi, hi_k)
        if a < b:
            out.append((a - lo_k, b - a, _dest_row(a)))
    return out


GATHER_PARTS = 1


def _shard_parts(k):
    parts = [[] for _ in range(GATHER_PARTS)]
    for s0, n, d0 in _shard_pieces(k):
        step = -(-(n // GROUP) // GATHER_PARTS) * GROUP
        for p in range(GATHER_PARTS):
            a, b = min(p * step, n), min((p + 1) * step, n)
            if a < b:
                parts[p].append((s0 + a, b - a, d0 + a))
    return parts


def _patch_straddlers(wpT, heads, tails):
    for k in range(1, N_DEV):
        m = (k * SHARD_IN) % GROUP
        if m:
            group = jnp.concatenate([tails[k - 1, GROUP - m:], heads[k, :GROUP - m]], axis=0)
            wpT = lax.dynamic_update_slice(wpT, group, (_dest_row(k * SHARD_IN - m), 0))
    return wpT


def _gather_stages(k, win_ref, small, z_ref, n_zero, w_ref, send_sems, recv_sems, local_sems):
    x, y, c = k // 4, (k // 2) % 2, k % 2
    idx = lambda p: 4 * p[0] + 2 * p[1] + p[2]
    me, sib = (x, y, c), (x, y, 1 - c)
    xn, yn, dg = (1 - x, y, c), (x, 1 - y, c), (1 - x, 1 - y, c)
    parts = range(GATHER_PARTS)

    def copies(slot, block, to, part, own=False):
        kb = idx(block)
        out = []
        for j, (s0, n, d0) in enumerate(_shard_parts(kb)[part]):
            dst = w_ref.at[pl.ds(d0, n)]
            out.append((win_ref.at[pl.ds(s0, n)] if own else dst, dst, 2 * part + j))
        if part == 0:
            for j, (src, gathered) in enumerate(small):
                out.append((src if own else gathered.at[kb], gathered.at[kb], 2 * GATHER_PARTS + j))
        return [pltpu.make_async_remote_copy(src_ref=s, dst_ref=d, send_sem=send_sems.at[slot, j], recv_sem=recv_sems.at[slot, j],
                                             device_id=to, device_id_type=MESH) for s, d, j in out]

    def start(cps):
        for cp in cps:
            cp.start()

    def arrived(slot, block, part):
        for cp in copies(slot, block, me, part):
            cp.wait_recv()

    def local():
        pairs = [(win_ref.at[pl.ds(s0, n)], w_ref.at[pl.ds(d0, n)]) for s0, n, d0 in _shard_pieces(k)]
        pairs += [(src, gathered.at[k]) for src, gathered in small] + [(z_ref, w_ref.at[pl.ds(W_IN, n_zero)])]
        return [pltpu.make_async_copy(s, d, local_sems.at[j]) for j, (s, d) in enumerate(pairs)]

    relay = (xn, yn) if c == 1 else (yn, xn)

    def first():
        start(local())
        for p in parts:
            start(copies(0, me, sib, p, own=True) + copies(1, me, xn, p, own=True) + copies(2, me, yn, p, own=True))

    def hand_on():
        for p in parts:
            arrived(1, xn, p)
            start(copies(4, xn, sib, p))
            if c == 1:
                start(copies(3, *relay, p))
            arrived(2, yn, p)
            start(copies(5, yn, sib, p))
            if c == 0:
                start(copies(3, *relay, p))

    def finish():
        for p in parts:
            arrived(3, dg, p)
            start(copies(6, dg, sib, p))
        for p in parts:
            arrived(0, sib, p)
            arrived(4, (1 - x, y, 1 - c), p)
            arrived(5, (x, 1 - y, 1 - c), p)
            arrived(6, (1 - x, 1 - y, 1 - c), p)
        for p in parts:
            sent = (copies(0, me, sib, p, own=True) + copies(1, me, xn, p, own=True) + copies(2, me, yn, p, own=True)
                    + copies(3, *relay, p) + copies(4, xn, sib, p) + copies(5, yn, sib, p) + copies(6, dg, sib, p))
            for cp in sent:
                cp.wait_send()
        for cp in local():
            cp.wait()

    return first, hand_on, finish


def _gather_sems(n_small):
    n_arr = 2 * GATHER_PARTS + n_small
    return [pltpu.SemaphoreType.DMA((7, n_arr)), pltpu.SemaphoreType.DMA((7, n_arr)), pltpu.SemaphoreType.DMA((n_arr + 1,))]


def _gather_weights(win, head, tail, wout, cw, zeros):
    small_in = (wout, cw, head, tail)
    n_zero = zeros.shape[0]
    assert W_IN + n_zero == W_ROWS and W_IN % GROUP == 0

    def body(win_ref, wout_ref, cw_ref, head_ref, tail_ref, z_ref, w_ref, gout_ref, gcw_ref, ghead_ref, gtail_ref, *sems):
        x, y, c = _place()
        me = 4 * x + 2 * y + c
        small = ((wout_ref, gout_ref), (cw_ref, gcw_ref), (head_ref, ghead_ref), (tail_ref, gtail_ref))

        def run(k):
            for stage in _gather_stages(k, win_ref, small, z_ref, n_zero, w_ref, *sems):
                stage()

        for k in range(N_DEV):
            pl.when(me == k)(functools.partial(run, k))

    anyspec = pl.BlockSpec(memory_space=pl.ANY)
    return pl.pallas_call(
        body, name="gather_weights", in_specs=[anyspec] * 6, out_specs=[anyspec] * 5,
        out_shape=[jax.ShapeDtypeStruct((W_ROWS, D), win.dtype)]
        + [jax.ShapeDtypeStruct((N_DEV,) + a.shape, a.dtype) for a in small_in],
        scratch_shapes=_gather_sems(len(small_in)),
    )(win, wout, cw, head, tail, zeros)


_REL = [(dx, dy, dc) for dx in (0, 1) for dy in (0, 1) for dc in (0, 1)][1:]
_HBM = pl.BlockSpec(memory_space=pltpu.HBM)
_SEM = pl.BlockSpec(memory_space=pltpu.SEMAPHORE)
_EFFECT = pltpu.SideEffectType.DATAFLOW_SIDE_EFFECTING


def _peer(k):
    x, y, c = _place()
    dx, dy, dc = _REL[k]
    return (1 - x if dx else x, 1 - y if dy else y, 1 - c if dc else c)


def _exchange_start(parts, *, name):
    n = len(parts)

    def body(*refs):
        ins, lands = refs[:n], refs[n:2 * n]
        send_sems, recv_sems, token = refs[2 * n], refs[2 * n + 1], refs[-1]
        x, y, c = _place()
        me = 4 * x + 2 * y + c
        for a in range(n):
            for k in range(len(_REL)):
                px, py, pc = _peer(k)
                pltpu.make_async_remote_copy(
                    src_ref=ins[a].at[4 * px + 2 * py + pc], dst_ref=lands[a].at[me],
                    send_sem=send_sems.at[len(_REL) * a + k], recv_sem=recv_sems.at[len(_REL) * a + k],
                    device_id=(px, py, pc), device_id_type=MESH).start()
        token[...] = jnp.zeros_like(token)

    sem = pltpu.SemaphoreType.DMA((len(_REL) * n,))
    bufs = [pltpu.HBM(p.shape, p.dtype) for p in parts]
    outs = pl.pallas_call(
        body, name=name,
        out_shape=(sem, sem, *bufs, *bufs, jax.ShapeDtypeStruct((8, LANE), F32)),
        in_specs=(_HBM,) * (2 * n), out_specs=(_SEM, _SEM, *(_HBM,) * (2 * n), pl.BlockSpec(memory_space=pltpu.VMEM)),
        input_output_aliases={i: 2 + i for i in range(2 * n)},
        compiler_params=pltpu.CompilerParams(has_side_effects=_EFFECT),
    )(*[pltpu.with_memory_space_constraint(p, pltpu.HBM) for p in parts],
      *[pltpu.with_memory_space_constraint(lax.empty(p.shape, p.dtype), pltpu.HBM) for p in parts])
    return outs[0], outs[1], outs[2:2 + n], outs[2 + n:2 + 2 * n], outs[-1]


def _exchange_wait(send_sems, recv_sems, parts, lands, after, *, name):
    n = len(parts)

    def body(*refs):
        ins, lands_ = refs[:n], refs[n:2 * n]
        ssem, rsem = refs[2 * n], refs[2 * n + 1]
        for a in range(n):
            for k in range(len(_REL)):
                px, py, pc = _peer(k)
                p = 4 * px + 2 * py + pc
                cp = pltpu.make_async_remote_copy(
                    src_ref=ins[a].at[p], dst_ref=lands_[a].at[p],
                    send_sem=ssem.at[len(_REL) * a + k], recv_sem=rsem.at[len(_REL) * a + k],
                    device_id=(px, py, pc), device_id_type=MESH)
                cp.wait_send()
                cp.wait_recv()

    bufs = [pltpu.HBM(p.shape, p.dtype) for p in parts]
    outs = pl.pallas_call(
        body, name=name, out_shape=(*bufs, *bufs),
        in_specs=(*(_HBM,) * (2 * n), _SEM, _SEM, pl.BlockSpec(memory_space=pl.ANY)), out_specs=(_HBM,) * (2 * n),
        input_output_aliases={i: i for i in range(2 * n)},
        compiler_params=pltpu.CompilerParams(has_side_effects=_EFFECT),
    )(*parts, *lands, send_sems, recv_sems, after)
    return outs[:n], outs[n:]


WEIGHTS = ('norm_w', 'w_in', 'gate_b', 'sgu_norm_g', 'sgu_norm_b', 'sgu_w', 'sgu_b', 'conv_w', 'conv_b', 'dt_bias', 'A_log',
           'D_skip', 'ssd_norm_w', 'w_out', 'final_norm_w')
SHARDED = ('w_in', 'conv_w', 'w_out')
PACK_ROW = 8 * LANE


def _constants():
    tri = np.tril(np.ones((CHUNK, CHUNK), np.float32))
    expand = np.zeros((DT_W, D), np.float32)
    for h in range(HEADS):
        expand[h, h * HEADDIM:(h + 1) * HEADDIM] = 1.0
    sel = np.zeros((D, LANE), np.float32)
    for g in range(SGU_GROUPS):
        sel[g * LANE:(g + 1) * LANE, g] = 1.0
    pos_chunk = np.arange(SGU_BLOCK) // CHUNK
    mask = (pos_chunk[None, :] <= pos_chunk[:, None]).astype(np.float32)
    shift = np.zeros(((CONV_K - 1) * CHUNK, HALO_BLK + CHUNK), np.float32)
    for kk in range(CONV_K - 1):
        for t in range(CHUNK):
            shift[kk * CHUNK + t, HALO_BLK - (CONV_K - 1) + t + kk] = 1.0
    return dict(tri=jnp.asarray(tri, BF16), triT=jnp.asarray(tri.T.copy(), BF16), expand=jnp.asarray(np.tile(expand, (3, 1)), BF16),
                shift=jnp.asarray(shift, BF16),
                expandT=jnp.asarray(expand.T.copy(), BF16), sel=jnp.asarray(sel), mask=jnp.asarray(mask))


def _to_shards(segs):
    starts = np.cumsum([0] + [n for _, n in segs])
    assert starts[-1] == W_IN
    slabs = []
    for k in range(N_DEV):
        pieces = []
        for (s, n), s0 in zip(segs, starts[:-1]):
            lo, hi = max(k * SHARD_IN, s0), min((k + 1) * SHARD_IN, s0 + n)
            if lo < hi:
                pieces.append(s[lo - s0:hi - s0])
        slabs.append(jnp.concatenate(pieces, axis=0))
    return jnp.stack(slabs)


def _local_step(x2, tgt, wpT, wout, cw, p, exchange_small, exchange):
    S = x2.shape[0]
    k = _constants()
    xn, proj = _in_proj(x2, p['norm_w'], wpT, tm=min(1024, S), tn=2048)
    wm32 = p['sgu_w'][0] * k['mask']
    wm = wm32.astype(BF16)
    wmT = jnp.swapaxes(wm32, 1, 2).astype(BF16)
    bias_full = jnp.repeat(p['sgu_b'][0].T, LANE, axis=1)
    tm_sgu = min(512, S)
    ya = _sgu_fwd(proj, p['sgu_norm_g'], p['sgu_norm_b'], wm, bias_full, tm=tm_sgu)
    pad32 = lambda a: jnp.pad(a, ((0, 0), (0, DT_W - HEADS)))
    dtb_p, alog_p = pad32(p['dt_bias']), pad32(p['A_log'])
    d_exp = jnp.repeat(p['D_skip'], HEADDIM, axis=1)
    ssd_args = (cw, p['conv_b'], dtb_p, alog_p, d_exp, p['ssd_norm_w'])
    y, yb, states = _ssd_fwd(proj, *ssd_args, k['tri'], k['expand'], k['shift'])
    dh, dhb, mb, dya, dyb, dgl, loss, dfw, dgb = _head(
        x2, ya, yb, proj, tgt, p['gate_b'], wout, p['final_norm_w'][None, :], tm=min(256, S))
    dsgu, dws, dbsT, dsg, dsb = _sgu_bwd(proj, dya, p['sgu_norm_g'], p['sgu_norm_b'], wm, wmT, bias_full, k['mask'], k['sel'],
                                         tm=tm_sgu)
    dssd, dcw, dcb, ddtb, dalog, dD, dnw = _ssd_bwd(proj, dyb, y, states, *ssd_args, k['tri'], k['triT'], k['expand'], k['expandT'],
                                                    k['shift'])
    grads = dict(
        gate_b=dgb[0:1], sgu_norm_g=dsg[0:1], sgu_norm_b=dsb[0:1], sgu_w=dws[None],
        sgu_b=dbsT[:, :SGU_GROUPS].T[None], conv_w=dcw[0:CONV_K][None], conv_b=dcb[0:1], dt_bias=ddtb[0:1, :HEADS],
        A_log=dalog[0:1, :HEADS], D_skip=dD[0:1, :HEADS], ssd_norm_w=dnw[0:1], final_norm_w=dfw[0])
    tw = dict(trans_a=True, out_dtype=BF16, tm=1024, tn=512, tk=S)
    dw_out = _matmul(mb, dhb, name="dw_out", **tw)
    token = exchange_small(loss[0, 0], grads, dw_out)
    dwT_sgu, dwT_gate, dwT_ssd = _dw_in([dsgu, dgl, dssd], xn, token, tm=256)
    token = exchange([(dwT_sgu, SEG_SGU[1]), (dwT_ssd, W_IN - SEG_SSD[0]), (dwT_gate, SEG_GATE[1])])
    tm, tn = min(1024, S), 1024
    dxn = _matmul(dsgu, wpT, tm=tm, tn=512, tk=SEG_SGU[1], after=token, name="dxn_sgu")
    dxn = _matmul(dgl, wpT, b_koff=SEG_GATE[0] // 2048, tm=tm, tn=tn, tk=2048, add=dxn, name="dxn_gate")
    grad_x, dnorm = _dxn_last_norm(dssd, wpT, SEG_SSD[0], dxn, x2, dh, p['norm_w'], tm=min(256, S))
    return grad_x, dnorm[0:1]


def _pack(arrs):
    rows, offs, r = [], [], 0
    for a in arrs:
        n = a.size
        nr = -(-n // PACK_ROW) * 8
        rows.append(jnp.pad(a.reshape(-1).astype(F32), (0, nr * LANE - n)).reshape(nr, LANE))
        offs.append(r)
        r += nr
    return jnp.concatenate(rows, axis=0), offs


def kernel(x, norm_w, w_in, gate_b, sgu_norm_g, sgu_norm_b, sgu_w, sgu_b, conv_w, conv_b, dt_bias, A_log, D_skip, ssd_norm_w, w_out, final_norm_w, loss_target, m_norm_w, m_w_in, m_gate_b, m_sgu_norm_g, m_sgu_norm_b, m_sgu_w, m_sgu_b, m_conv_w, m_conv_b, m_dt_bias, m_A_log, m_D_skip, m_ssd_norm_w, m_w_out, m_final_norm_w, v_norm_w, v_w_in, v_gate_b, v_sgu_norm_g, v_sgu_norm_b, v_sgu_w, v_sgu_b, v_conv_w, v_conv_b, v_dt_bias, v_A_log, v_D_skip, v_ssd_norm_w, v_w_out, v_final_norm_w):
    w = dict(norm_w=norm_w, w_in=w_in, gate_b=gate_b, sgu_norm_g=sgu_norm_g, sgu_norm_b=sgu_norm_b, sgu_w=sgu_w, sgu_b=sgu_b,
             conv_w=conv_w, conv_b=conv_b, dt_bias=dt_bias, A_log=A_log, D_skip=D_skip, ssd_norm_w=ssd_norm_w, w_out=w_out,
             final_norm_w=final_norm_w)
    m = dict(norm_w=m_norm_w, w_in=m_w_in, gate_b=m_gate_b, sgu_norm_g=m_sgu_norm_g, sgu_norm_b=m_sgu_norm_b, sgu_w=m_sgu_w,
             sgu_b=m_sgu_b, conv_w=m_conv_w, conv_b=m_conv_b, dt_bias=m_dt_bias, A_log=m_A_log, D_skip=m_D_skip,
             ssd_norm_w=m_ssd_norm_w, w_out=m_w_out, final_norm_w=m_final_norm_w)
    v = dict(norm_w=v_norm_w, w_in=v_w_in, gate_b=v_gate_b, sgu_norm_g=v_sgu_norm_g, sgu_norm_b=v_sgu_norm_b, sgu_w=v_sgu_w,
             sgu_b=v_sgu_b, conv_w=v_conv_w, conv_b=v_conv_b, dt_bias=v_dt_bias, A_log=v_A_log, D_skip=v_D_skip,
             ssd_norm_w=v_ssd_norm_w, w_out=v_w_out, final_norm_w=v_final_norm_w)
    me = 4 * lax.axis_index("x") + 2 * lax.axis_index("y") + lax.axis_index("c")
    shard_cw = XBC_W // N_DEV

    tpose = lambda a: jnp.swapaxes(a[0], 0, 1)
    wT = tpose(w_in).astype(BF16)
    first_group = (GROUP - (me * SHARD_IN) % GROUP) % GROUP
    window = lax.dynamic_slice(jnp.pad(wT, ((0, GROUP), (0, 0))), (first_group, 0), (INTERIOR, D))
    wpT, g_out, g_cw, heads, tails = _gather_weights(window, wT[:GROUP], wT[SHARD_IN - GROUP:], w_out[0].astype(BF16),
                                                     conv_w[0], jnp.zeros((W_ROWS - W_IN, D), BF16))
    wpT = _patch_straddlers(wpT, heads, tails)
    wout_full = g_out.reshape(D, D)
    cw_full = jnp.swapaxes(g_cw, 0, 1).reshape(CONV_K, XBC_W)

    flight = {}

    small = [n for n in WEIGHTS if n not in SHARDED and n != 'norm_w']
    early = {}

    def exchange_small(loss_part, grads, dw_out):
        early['packed'], early['offs'] = _pack([grads[n] for n in small] + [loss_part, grads['conv_w']])
        parts = [jnp.broadcast_to(early['packed'][None], (N_DEV,) + early['packed'].shape), dw_out.reshape(N_DEV, D // N_DEV, D)]
        early['sems'], early['rsems'], early['parts'], early['lands'], token = _exchange_start(parts, name="small_start")
        return token

    def exchange(dw_inT_segs):
        parts = [_to_shards(dw_inT_segs)]
        flight['sems'], flight['rsems'], flight['parts'], flight['lands'], token = _exchange_start(parts, name="exchange_start")
        return token

    grad_x, dnorm = _local_step(x[0], loss_target[0], wpT, wout_full, cw_full, w, exchange_small, exchange)
    (_, own_out), (land_small, land_out) = _exchange_wait(
        early['sems'], early['rsems'], early['parts'], early['lands'], grad_x, name="small_wait")
    (own_in,), (land_in,) = _exchange_wait(
        flight['sems'], flight['rsems'], flight['parts'], flight['lands'], grad_x, name="exchange_wait")
    me_arr = jnp.reshape(me, (1,)).astype(jnp.int32)
    res = {}
    res['w_in'] = [jnp.swapaxes(o, 0, 1) for o in _adamw_own(
        me_arr, own_in, land_in, tpose(w_in), tpose(m_w_in), tpose(v_w_in), tr=SHARD_IN, tc=256, name="adamw_w_in")]
    res['w_out'] = _adamw_own(me_arr, own_out, land_out, w_out[0], m_w_out[0], v_w_out[0], tr=128, tc=D, name="adamw_w_out")

    (norm_parts,) = _all_gather([_pack([dnorm])[0]], name="gather_norm")
    norm_outs = _adamw(norm_parts, *[_pack([d['norm_w']])[0] for d in (w, m, v)], tr=norm_parts.shape[1], name="adamw_norm")
    res['norm_w'] = [o.reshape(-1)[:D].reshape(w['norm_w'].shape) for o in norm_outs]

    offs = early['offs']
    gathered = lax.dynamic_update_slice(land_small, early['packed'][None], (me, 0, 0))
    off_loss, off_cw = offs[-2], offs[-1]
    cw_parts = gathered[:, off_cw:, :].reshape(N_DEV, CONV_K, XBC_W)
    cw_parts = lax.dynamic_slice_in_dim(cw_parts, me * shard_cw, shard_cw, axis=2)
    cw_rows = _pack([cw_parts[0]])[0].shape[0]
    cw_parts = jnp.pad(cw_parts.reshape(N_DEV, -1), ((0, 0), (0, cw_rows * LANE - CONV_K * shard_cw))).reshape(N_DEV, cw_rows, LANE)
    parts = jnp.concatenate([gathered[:, :off_cw, :], cw_parts], axis=1)
    zero = jnp.zeros((), F32)
    packs = [_pack([d[n] for n in small] + [zero, d['conv_w']])[0] for d in (w, m, v)]
    outs = _adamw(parts, *packs, tr=parts.shape[1], name="adamw_small")

    def unpack(o, name):
        if name == 'conv_w':
            return o[off_cw:off_cw + cw_rows].reshape(-1)[:CONV_K * shard_cw].reshape(w['conv_w'].shape)
        r0 = offs[small.index(name)]
        n = w[name].size
        return o[r0:r0 + -(-n // PACK_ROW) * 8].reshape(-1)[:n].reshape(w[name].shape)

    for n in small + ['conv_w']:
        res[n] = [unpack(o, n) for o in outs]
    for n in ('w_in', 'w_out'):
        res[n] = [o[None] for o in res[n]]
    loss = outs[0][off_loss, 0]
    return (loss, grad_x[None], *[res[n][0] for n in WEIGHTS], *[res[n][1] for n in WEIGHTS],
            *[res[n][2] for n in WEIGHTS], *[res[n][3] for n in WEIGHTS])
```

```python
import functools

import numpy as np
import jax
import jax.numpy as jnp
from jax import lax
from jax.experimental import pallas as pl
from jax.experimental.pallas import tpu as pltpu

F32 = jnp.float32
BF16 = jnp.bfloat16
HI = lax.Precision.HIGHEST
MESH = pl.DeviceIdType.MESH

D = 2048
EPS = 1e-5
SGU_BLOCK = 128
SGU_GROUPS = 16
CHUNK = 64
HEADS = 32
HEADDIM = 64
SSD_GROUPS = 4
GROUP_W = D // SSD_GROUPS
STATE = 128
CONV_K = 4
XBC_W = D + 2 * SSD_GROUPS * STATE
W_IN = 15392
N_DEV = 8
SHARD_IN = W_IN // N_DEV
ADAM_LR, ADAM_B1, ADAM_B2, ADAM_EPS, ADAM_WD, ADAM_STEP = 0.001, 0.9, 0.999, 1e-08, 0.01, 10

REF_SGU_END = 3 * D
REF_GATE_START = W_IN - 2 * D
LANE = 128
DT_W = LANE
OFF_U, OFF_V, OFF_ZA, OFF_G0, OFF_G1, OFF_ZB = (i * D for i in range(6))
OFF_XBC = OFF_ZB + D
OFF_DT = OFF_XBC + XBC_W
SEG_SGU = (OFF_U, 3 * D)
SEG_GATE = (OFF_G0, 2 * D)
SEG_SSD = (OFF_ZB, D + XBC_W + DT_W)
WP = SEG_SSD[0] + SEG_SSD[1]
SSD_PAD_W = 3 * D
VMEM_BYTES = 64 * 1024 * 1024
VMEM_LIMIT = VMEM_BYTES - 8 * 1024 * 1024


def _cp(sem=None, vmem=VMEM_LIMIT):
    return pltpu.CompilerParams(dimension_semantics=sem, vmem_limit_bytes=vmem)


def _sigmoid(x):
    return 1.0 / (1.0 + jnp.exp(-x))


def _softplus(x):
    return jnp.maximum(x, 0.0) + jnp.log(1.0 + jnp.exp(-jnp.abs(x)))


def _dot(a, b, precision=None):
    return jnp.dot(a, b, preferred_element_type=F32, precision=precision)


def _dot_nt(a, b, precision=None):
    return lax.dot_general(a, b, (((1,), (1,)), ((), ())), preferred_element_type=F32, precision=precision)


def _dot_tn(a, b, precision=None):
    return lax.dot_general(a, b, (((0,), (0,)), ((), ())), preferred_element_type=F32, precision=precision)


def _split3(a):
    hi = a.astype(BF16)
    r = a - hi.astype(F32)
    mid = r.astype(BF16)
    return hi, mid, (r - mid.astype(F32)).astype(BF16)


def _sel_right(a, sel01):
    m = a.shape[0]
    r = _dot(jnp.concatenate(_split3(a), axis=0), sel01)
    return (r[0:m] + r[m:2 * m]) + r[2 * m:3 * m]


def _sel_right_k(a, sel01_x3):
    return _dot(jnp.concatenate(_split3(a), axis=1), sel01_x3)


def _sel_left(sel01, a):
    n = a.shape[1]
    r = _dot(sel01, jnp.concatenate(_split3(a), axis=1))
    return (r[:, 0:n] + r[:, n:2 * n]) + r[:, 2 * n:3 * n]


def _matmul(a, b, *, trans_a=False, trans_b=False, b_koff=0, out_dtype=F32, tm, tn, tk, add=None, after=None, name):
    K, M = a.shape if trans_a else a.shape[::-1]
    N = b.shape[0] if trans_b else b.shape[1]
    assert M % tm == 0 and N % tn == 0 and K % tk == 0 and not (trans_a and trans_b), (name, M, N, K, tm, tn, tk)
    nk = K // tk

    def body(*refs):
        a_ref, b_ref = refs[:2]
        add_ref = refs[2] if add is not None else None
        o_ref, acc_ref = refs[-2:]
        k = pl.program_id(2)
        if trans_a:
            part = _dot_tn(a_ref[...], b_ref[...])
        else:
            part = _dot_nt(a_ref[...], b_ref[...]) if trans_b else _dot(a_ref[...], b_ref[...])

        def result(r):
            if add_ref is not None:
                r = r + add_ref[...]
            return r.astype(out_dtype)

        if nk == 1:
            o_ref[...] = result(part)
        else:
            @pl.when(k == 0)
            def _():
                acc_ref[...] = part

            @pl.when(jnp.logical_and(k > 0, k < nk - 1))
            def _():
                acc_ref[...] += part

            @pl.when(k == nk - 1)
            def _():
                o_ref[...] = result(acc_ref[...] + part)

    in_specs = [pl.BlockSpec((tk, tm), lambda i, j, k: (k, i)) if trans_a else pl.BlockSpec((tm, tk), lambda i, j, k: (i, k)),
                pl.BlockSpec((tn, tk), lambda i, j, k: (j, k)) if trans_b else pl.BlockSpec((tk, tn), lambda i, j, k: (k + b_koff, j))]
    args = [a, b]
    if add is not None:
        in_specs.append(pl.BlockSpec((tm, tn), lambda i, j, k: (i, j)))
        args.append(add)
    if after is not None:
        in_specs.append(pl.BlockSpec(memory_space=pl.ANY))
        args.append(after)
    return pl.pallas_call(
        body, name=name, grid=(M // tm, N // tn, nk), in_specs=in_specs,
        out_specs=pl.BlockSpec((tm, tn), lambda i, j, k: (i, j)),
        out_shape=jax.ShapeDtypeStruct((M, N), out_dtype),
        scratch_shapes=[pltpu.VMEM((tm, tn), F32)],
        compiler_params=_cp(("parallel", "parallel", "arbitrary")),
    )(*args)


def _dxn_last_norm(seg, wpT, row0, dxn, x, dh, w, *, tm):
    S, cols = seg.shape
    assert S % tm == 0 and row0 % GROUP == 0
    ntile = S // tm

    def body(seg_hbm, w_hbm, dxn_hbm, x_hbm, dh_hbm, nw_ref, gx_hbm, dw_ref, b_ref, abuf, pbuf, xbuf, hbuf, obuf, isem, osem):
        b_copy = pltpu.make_async_copy(w_hbm.at[pl.ds(row0, cols), :], b_ref, osem.at[2])
        b_copy.start()

        def first(t):
            return t * tm if isinstance(t, int) else pl.multiple_of(t * tm, tm)

        def fetch(t):
            rows, slot = pl.ds(first(t), tm), t % 2
            return [pltpu.make_async_copy(src.at[rows, :], buf.at[slot], isem.at[slot, j])
                    for j, (src, buf) in enumerate(((seg_hbm, abuf), (dxn_hbm, pbuf), (x_hbm, xbuf), (dh_hbm, hbuf)))]

        def write(t):
            return pltpu.make_async_copy(obuf.at[t % 2], gx_hbm.at[pl.ds(first(t), tm), :], osem.at[t % 2])

        for cp in fetch(0):
            cp.start()
        dw_ref[...] = jnp.zeros_like(dw_ref)
        b_copy.wait()

        def step(t, carry):
            slot = t % 2
            for cp in fetch(t):
                cp.wait()

            @pl.when(t + 1 < ntile)
            def _():
                for cp in fetch(t + 1):
                    cp.start()

            dxn_v = pbuf[slot] + _dot(abuf[slot], b_ref[...])
            xv = xbuf[slot]
            r = lax.rsqrt(jnp.mean(xv * xv, axis=-1, keepdims=True) + EPS)
            xh = xv * r
            dxh = dxn_v * nw_ref[...]
            gx = hbuf[slot] + r * (dxh - xh * jnp.mean(dxh * xh, axis=-1, keepdims=True))
            dw_ref[0:1, :] += jnp.sum(dxn_v * xh, axis=0, keepdims=True)

            @pl.when(t >= 2)
            def _():
                write(t - 2).wait()

            obuf[slot] = gx
            write(t).start()
            return carry

        lax.fori_loop(0, ntile, step, 0)
        for t in range(max(ntile - 2, 0), ntile):
            write(t).wait()

    anyspec = pl.BlockSpec(memory_space=pl.ANY)
    vmem = pl.BlockSpec(memory_space=pltpu.VMEM)
    return pl.pallas_call(
        body, name="dxn_last_norm", in_specs=[anyspec] * 5 + [vmem], out_specs=[anyspec, vmem],
        out_shape=[jax.ShapeDtypeStruct((S, D), F32), jax.ShapeDtypeStruct((8, D), F32)],
        scratch_shapes=[pltpu.VMEM((cols, D), BF16), pltpu.VMEM((2, tm, cols), BF16), pltpu.VMEM((2, tm, D), F32),
                        pltpu.VMEM((2, tm, D), F32), pltpu.VMEM((2, tm, D), F32), pltpu.VMEM((2, tm, D), F32),
                        pltpu.SemaphoreType.DMA((2, 4)), pltpu.SemaphoreType.DMA((3,))],
        compiler_params=_cp(),
    )(seg, wpT, dxn, x, dh, w)


DW_BUFS = 3


def _dw_in(segs, xn, after, *, tm):
    S = xn.shape[0]
    n = len(segs)
    assert all(a.shape[0] == S and a.shape[1] % tm == 0 for a in segs)

    def body(*refs):
        a_refs, xn_hbm = refs[:n], refs[n]
        o_refs = refs[-(n + 5):-5]
        xn_ref, abuf, obuf, asem, osem = refs[-5:]
        xn_copy = pltpu.make_async_copy(xn_hbm, xn_ref, osem.at[2])
        xn_copy.start()
        for q, (a_ref, o_ref) in enumerate(zip(a_refs, o_refs)):
            ntile = a_ref.shape[1] // tm

            def first(t):
                return t * tm if isinstance(t, int) else pl.multiple_of(t * tm, tm)

            def fetch(t, a_ref=a_ref):
                return pltpu.make_async_copy(a_ref.at[:, pl.ds(first(t), tm)], abuf.at[t % DW_BUFS], asem.at[t % DW_BUFS])

            def write(t, o_ref=o_ref):
                return pltpu.make_async_copy(obuf.at[t % 2], o_ref.at[pl.ds(first(t), tm), :], osem.at[t % 2])

            for t in range(min(DW_BUFS - 1, ntile)):
                fetch(t).start()
            if q == 0:
                xn_copy.wait()

            def step(t, carry, fetch=fetch, write=write, ntile=ntile):
                fetch(t).wait()

                @pl.when(t + DW_BUFS - 1 < ntile)
                def _():
                    fetch(t + DW_BUFS - 1).start()

                res = _dot_tn(abuf[t % DW_BUFS], xn_ref[...]).astype(BF16)

                @pl.when(t >= 2)
                def _():
                    write(t - 2).wait()

                obuf[t % 2] = res
                write(t).start()
                return carry

            lax.fori_loop(0, ntile, step, 0)
            for t in range(max(ntile - 2, 0), ntile):
                write(t).wait()

    anyspec = pl.BlockSpec(memory_space=pl.ANY)
    return pl.pallas_call(
        body, name="dw_in", in_specs=[anyspec] * (n + 1 + (after is not None)), out_specs=[anyspec] * n,
        out_shape=[jax.ShapeDtypeStruct((a.shape[1], D), BF16) for a in segs],
        scratch_shapes=[pltpu.VMEM((S, D), BF16), pltpu.VMEM((DW_BUFS, S, tm), BF16), pltpu.VMEM((2, tm, D), BF16),
                        pltpu.SemaphoreType.DMA((DW_BUFS,)), pltpu.SemaphoreType.DMA((3,))],
        compiler_params=_cp(),
    )(*segs, xn, *([after] if after is not None else []))


def _in_proj(x, w, wpT, *, tm, tn):
    S = x.shape[0]
    N = wpT.shape[0]
    assert S % tm == 0 and N % tn == 0, (S, N, tm, tn)

    def body(x_ref, w_ref, b_ref, xn_ref, o_ref, xs_ref):
        @pl.when(pl.program_id(1) == 0)
        def _():
            xv = x_ref[...]
            r = lax.rsqrt(jnp.mean(xv * xv, axis=-1, keepdims=True) + EPS)
            xs = (xv * r * w_ref[...]).astype(BF16)
            xs_ref[...] = xs
            xn_ref[...] = xs

        o_ref[...] = _dot_nt(xs_ref[...], b_ref[...]).astype(BF16)

    return pl.pallas_call(
        body, name="in_proj", grid=(S // tm, N // tn),
        in_specs=[pl.BlockSpec((tm, D), lambda i, j: (i, 0)), pl.BlockSpec((1, D), lambda i, j: (0, 0)),
                  pl.BlockSpec((tn, D), lambda i, j: (j, 0))],
        out_specs=[pl.BlockSpec((tm, D), lambda i, j: (i, 0)), pl.BlockSpec((tm, tn), lambda i, j: (i, j))],
        out_shape=[jax.ShapeDtypeStruct((S, D), BF16), jax.ShapeDtypeStruct((S, N), BF16)],
        scratch_shapes=[pltpu.VMEM((tm, D), BF16)],
        compiler_params=_cp(("parallel", "arbitrary")),
    )(x, w, wpT)


def _sgu_core(u_ref, v_ref, z_ref, g_ref, b_ref, wm_ref, bias_ref, vnb_ref, mixed_ref, tm):
    v = v_ref[...].astype(F32)
    mu = jnp.mean(v, axis=-1, keepdims=True)
    vc = v - mu
    rs = lax.rsqrt(jnp.mean(vc * vc, axis=-1, keepdims=True) + EPS)
    vh = vc * rs
    vnb_ref[...] = (vh * g_ref[...] + b_ref[...]).astype(BF16)
    for blk in range(tm // SGU_BLOCK):
        rows = pl.ds(blk * SGU_BLOCK, SGU_BLOCK)
        for gi in range(SGU_GROUPS):
            cols = pl.ds(gi * LANE, LANE)
            mixed_ref[rows, cols] = _dot(wm_ref[gi], vnb_ref[rows, cols]) + bias_ref[:, cols]
    return vh, rs


def _sgu_fwd(proj, g, b, wm, bias_full, *, tm):
    S = proj.shape[0]

    def body(u_ref, v_ref, z_ref, g_ref, b_ref, wm_ref, bias_ref, y_ref, vnb_ref, mixed_ref):
        _sgu_core(u_ref, v_ref, z_ref, g_ref, b_ref, wm_ref, bias_ref, vnb_ref, mixed_ref, tm)
        z = z_ref[...].astype(F32)
        y_ref[...] = (u_ref[...].astype(F32) * mixed_ref[...] * (z * _sigmoid(z))).astype(BF16)

    seg = lambda off: pl.BlockSpec((tm, D), lambda i: (i, off // D))
    full = lambda a: pl.BlockSpec(a.shape, lambda i: (0,) * a.ndim)
    return pl.pallas_call(
        body, name="sgu_fwd", grid=(S // tm,),
        in_specs=[seg(OFF_U), seg(OFF_V), seg(OFF_ZA), full(g), full(b), full(wm), full(bias_full)],
        out_specs=pl.BlockSpec((tm, D), lambda i: (i, 0)),
        out_shape=jax.ShapeDtypeStruct((S, D), BF16),
        scratch_shapes=[pltpu.VMEM((tm, D), BF16), pltpu.VMEM((tm, D), F32)],
        compiler_params=_cp(("parallel",)),
    )(proj, proj, proj, g, b, wm, bias_full)


def _sgu_bwd(proj, dy, g, b, wm, wmT, bias_full, mask, sel, *, tm):
    S = proj.shape[0]
    nsteps = S // tm

    def body(u_ref, v_ref, z_ref, dy_ref, g_ref, b_ref, wm_ref, wmT_ref, bias_ref, mask_ref, sel_ref,
             dp_ref, dws_ref, dbs_ref, dg_ref, db_ref, vnb_ref, mixed_ref, dmb_ref, dvn_ref, dbias_ref):
        i = pl.program_id(0)

        @pl.when(i == 0)
        def _():
            dws_ref[...] = jnp.zeros_like(dws_ref)
            dg_ref[...] = jnp.zeros_like(dg_ref)
            db_ref[...] = jnp.zeros_like(db_ref)
            dbias_ref[...] = jnp.zeros_like(dbias_ref)

        vh, rs = _sgu_core(u_ref, v_ref, z_ref, g_ref, b_ref, wm_ref, bias_ref, vnb_ref, mixed_ref, tm)
        u = u_ref[...].astype(F32)
        z = z_ref[...].astype(F32)
        dy_v = dy_ref[...].astype(F32)
        mixed = mixed_ref[...]
        sg = _sigmoid(z)
        sz = z * sg
        dp_ref[:, 0:D] = (dy_v * mixed * sz).astype(BF16)
        dp_ref[:, 2 * D:3 * D] = (dy_v * u * mixed * (sg * (1.0 + z * (1.0 - sg)))).astype(BF16)
        dmixed = dy_v * u * sz
        dmb_ref[...] = dmixed.astype(BF16)
        for blk in range(tm // SGU_BLOCK):
            dbias_ref[...] += dmixed[blk * SGU_BLOCK:(blk + 1) * SGU_BLOCK, :]
        for blk in range(tm // SGU_BLOCK):
            rows = pl.ds(blk * SGU_BLOCK, SGU_BLOCK)
            for gi in range(SGU_GROUPS):
                cols = pl.ds(gi * LANE, LANE)
                dm = dmb_ref[rows, cols]
                dvn_ref[rows, cols] = _dot(wmT_ref[gi], dm)
                dws_ref[gi] += _dot_nt(dm, vnb_ref[rows, cols])
        dvn = dvn_ref[...]
        dg_ref[0:1, :] += jnp.sum(dvn * vh, axis=0, keepdims=True)
        db_ref[0:1, :] += jnp.sum(dvn, axis=0, keepdims=True)
        dvh = dvn * g_ref[...]
        dv = rs * (dvh - jnp.mean(dvh, axis=-1, keepdims=True) - vh * jnp.mean(dvh * vh, axis=-1, keepdims=True))
        dp_ref[:, D:2 * D] = dv.astype(BF16)

        @pl.when(i == nsteps - 1)
        def _():
            for gi in range(SGU_GROUPS):
                dws_ref[gi] = dws_ref[gi] * mask_ref[...]
            dbs_ref[...] = _dot(dbias_ref[...], sel_ref[...], precision=HI)

    seg = lambda off: pl.BlockSpec((tm, D), lambda i: (i, off // D))
    full = lambda a: pl.BlockSpec(a.shape, lambda i: (0,) * a.ndim)
    return pl.pallas_call(
        body, name="sgu_bwd", grid=(nsteps,),
        in_specs=[seg(OFF_U), seg(OFF_V), seg(OFF_ZA), pl.BlockSpec((tm, D), lambda i: (i, 0)),
                  full(g), full(b), full(wm), full(wmT), full(bias_full), full(mask), full(sel)],
        out_specs=[pl.BlockSpec((tm, 3 * D), lambda i: (i, 0)),
                   pl.BlockSpec((SGU_GROUPS, SGU_BLOCK, SGU_BLOCK), lambda i: (0, 0, 0)),
                   pl.BlockSpec((SGU_BLOCK, LANE), lambda i: (0, 0)),
                   pl.BlockSpec((8, D), lambda i: (0, 0)), pl.BlockSpec((8, D), lambda i: (0, 0))],
        out_shape=[jax.ShapeDtypeStruct((S, 3 * D), BF16),
                   jax.ShapeDtypeStruct((SGU_GROUPS, SGU_BLOCK, SGU_BLOCK), F32),
                   jax.ShapeDtypeStruct((SGU_BLOCK, LANE), F32),
                   jax.ShapeDtypeStruct((8, D), F32), jax.ShapeDtypeStruct((8, D), F32)],
        scratch_shapes=[pltpu.VMEM((tm, D), BF16), pltpu.VMEM((tm, D), F32), pltpu.VMEM((tm, D), BF16),
                        pltpu.VMEM((tm, D), F32), pltpu.VMEM((SGU_BLOCK, D), F32)],
        compiler_params=_cp(("arbitrary",)),
    )(proj, proj, proj, dy, g, b, wm, wmT, bias_full, mask, sel)


SSD_T = 2 * CHUNK
HALO = 8
HALO_BLK = 16


def _pair_masks():
    row = lax.broadcasted_iota(jnp.int32, (CHUNK, LANE), 0)
    lane = lax.broadcasted_iota(jnp.int32, (CHUNK, LANE), 1)
    pos = jnp.where(lane >= CHUNK, lane - CHUNK, lane)
    diag = (row == pos).astype(F32)
    causal = row >= pos
    lo = (lane < CHUNK).astype(F32)
    return diag, causal, lo, 1.0 - lo


def _ssd_chunk_fwd(c, ext_ref, shift_ref, dt_ref, cw_ref, cb_ref, dtb_ref, alog_ref, tri_ref, exp_ref):
    r0 = c * CHUNK
    win = ext_ref[pl.ds(r0, HALO_BLK + CHUNK), :]
    sh = _dot(shift_ref[...], win)
    taps = [sh[k * CHUNK:(k + 1) * CHUNK] for k in range(CONV_K - 1)] + [win[HALO_BLK:].astype(F32)]
    pre = cb_ref[...] + sum(cw_ref[k:k + 1, :] * taps[k] for k in range(CONV_K))
    sg = _sigmoid(pre)
    xc = pre * sg
    dtr = dt_ref[pl.ds(r0, CHUNK), :].astype(F32) + dtb_ref[...]
    dtv = _softplus(dtr)
    A = -jnp.exp(alog_ref[...])
    acs = _sel_left(tri_ref[...], dtv * A)
    both = _sel_right_k(jnp.concatenate([acs, dtv], axis=0), exp_ref[...])
    E, dtE = both[0:CHUNK], both[CHUNK:2 * CHUNK]
    return dict(taps=taps, pre=pre, sg=sg, xc=xc, dtr=dtr, dtv=dtv, A=A, E=E, dtE=dtE)


def _ssd_fwd(proj, conv_w, conv_b, dtb_p, alog_p, d_exp, norm_w, tri, expand, shift):
    S = proj.shape[0]
    T = SSD_T
    nsteps = S // T
    ncl = T // CHUNK

    def body(zb_ref, xbc_ref, halo_ref, dt_ref, cw_ref, cb_ref, dtb_ref, alog_ref, dexp_ref, nw_ref, tri_ref, exp_ref, shift_ref,
             y_ref, yb_ref, st_ref, ht_ref, ext_ref):
        i = pl.program_id(0)

        @pl.when(i == 0)
        def _():
            ht_ref[...] = jnp.zeros_like(ht_ref)
            ext_ref[0:HALO_BLK, :] = jnp.zeros((HALO_BLK, XBC_W), BF16)

        @pl.when(i > 0)
        def _():
            ext_ref[0:HALO_BLK, :] = halo_ref[...]

        ext_ref[HALO_BLK:HALO_BLK + T, :] = xbc_ref[...]
        diag, causal, lo, hi = _pair_masks()
        for c in range(ncl):
            q = _ssd_chunk_fwd(c, ext_ref, shift_ref, dt_ref, cw_ref, cb_ref, dtb_ref, alog_ref, tri_ref, exp_ref)
            rows = pl.ds(c * CHUNK, CHUNK)
            xc, E, dtE = q["xc"], q["E"], q["dtE"]
            xs = xc[:, 0:D]
            total = E[CHUNK - 1:CHUNK, :]
            x_dt = xs * dtE
            eE = jnp.exp(E)
            xw = x_dt * jnp.exp(total - E)
            st_ref[c] = ht_ref[...]
            for g in range(SSD_GROUPS):
                gc = slice(g * GROUP_W, (g + 1) * GROUP_W)
                Bg = xc[:, D + g * STATE:D + (g + 1) * STATE].astype(BF16)
                Cg = xc[:, D + SSD_GROUPS * STATE + g * STATE:D + SSD_GROUPS * STATE + (g + 1) * STATE].astype(BF16)
                cb2 = _dot_nt(Cg, jnp.concatenate([Bg, Bg], axis=0))
                htg = ht_ref[:, gc]
                y_ref[rows, gc] = eE[:, gc] * _dot(Cg, htg.astype(BF16)) + xs[:, gc] * dexp_ref[:, gc]
                for jj in range(GROUP_W // LANE):
                    pc = slice(g * GROUP_W + jj * LANE, g * GROUP_W + (jj + 1) * LANE)
                    Ej = E[:, pc]
                    e2 = jnp.sum(Ej * diag, axis=0, keepdims=True)
                    Mp = cb2 * jnp.exp(jnp.where(causal, Ej - e2, -1e30))
                    xj = x_dt[:, pc]
                    xbd = jnp.concatenate([xj * lo, xj * hi], axis=0).astype(BF16)
                    y_ref[rows, pc] += _dot(Mp.astype(BF16), xbd)
                ht_ref[:, gc] = jnp.exp(total[:, gc]) * htg + _dot_tn(Bg, xw[:, gc].astype(BF16))
            zb = zb_ref[rows, :].astype(F32)
            hh = y_ref[rows, :] * (zb * _sigmoid(zb))
            for g in range(SSD_GROUPS):
                gc = slice(g * GROUP_W, (g + 1) * GROUP_W)
                hg = hh[:, gc]
                r = lax.rsqrt(jnp.mean(hg * hg, axis=-1, keepdims=True) + EPS)
                yb_ref[rows, gc] = (hg * r * nw_ref[:, gc]).astype(BF16)

    full = lambda a: pl.BlockSpec(a.shape, lambda i: (0,) * a.ndim)
    hb = T // HALO_BLK
    return pl.pallas_call(
        body, name="ssd_fwd", grid=(nsteps,),
        in_specs=[pl.BlockSpec((T, D), lambda i: (i, OFF_ZB // D)),
                  pl.BlockSpec((T, XBC_W), lambda i: (i, OFF_XBC // XBC_W)),
                  pl.BlockSpec((HALO_BLK, XBC_W), lambda i: (jnp.maximum(i * hb - 1, 0), OFF_XBC // XBC_W)),
                  pl.BlockSpec((T, DT_W), lambda i: (i, OFF_DT // DT_W)),
                  full(conv_w), full(conv_b), full(dtb_p), full(alog_p), full(d_exp), full(norm_w), full(tri), full(expand),
                  full(shift)],
        out_specs=[pl.BlockSpec((T, D), lambda i: (i, 0)), pl.BlockSpec((T, D), lambda i: (i, 0)),
                   pl.BlockSpec((ncl, STATE, D), lambda i: (i, 0, 0))],
        out_shape=[jax.ShapeDtypeStruct((S, D), F32), jax.ShapeDtypeStruct((S, D), BF16),
                   jax.ShapeDtypeStruct((S // CHUNK, STATE, D), F32)],
        scratch_shapes=[pltpu.VMEM((STATE, D), F32), pltpu.VMEM((HALO_BLK + T, XBC_W), BF16)],
        compiler_params=_cp(("arbitrary",)),
    )(proj, proj, proj, proj, conv_w, conv_b, dtb_p, alog_p, d_exp, norm_w, tri, expand, shift)


def _ssd_bwd(proj, dyb, y, states, conv_w, conv_b, dtb_p, alog_p, d_exp, norm_w, tri, triT, expand, expandT, shift):
    S = proj.shape[0]
    T = SSD_T
    nsteps = S // T
    ncl = T // CHUNK
    SSD_W = SSD_PAD_W

    def body(zb_ref, xbc_ref, halo_ref, dt_ref, dyb_ref, y_ref, st_ref, cw_ref, cb_ref, dtb_ref, alog_ref, dexp_ref, nw_ref,
             tri_ref, triT_ref, exp_ref, expT_ref, shift_ref,
             dp_ref, dcw_ref, dcb_ref, ddtb_ref, dalog_ref, dD_ref, dnw_ref,
             dht_ref, ext_ref, dpre_ref, dy_s, dE_s, dxdt_s, dxc_s, dDacc_ref, dAacc_ref):
        i = pl.program_id(0)

        @pl.when(i == 0)
        def _():
            for r in (dht_ref, dcw_ref, dcb_ref, ddtb_ref, dnw_ref, dDacc_ref, dAacc_ref):
                r[...] = jnp.zeros_like(r)
            dpre_ref[T:T + HALO_BLK, :] = jnp.zeros((HALO_BLK, XBC_W), F32)

        @pl.when(i == nsteps - 1)
        def _():
            ext_ref[0:HALO_BLK, :] = jnp.zeros((HALO_BLK, XBC_W), BF16)

        @pl.when(i < nsteps - 1)
        def _():
            ext_ref[0:HALO_BLK, :] = halo_ref[...]

        ext_ref[HALO_BLK:HALO_BLK + T, :] = xbc_ref[...]
        diag, causal, lo, hi = _pair_masks()
        last_row = (lax.broadcasted_iota(jnp.int32, (CHUNK, 1), 0) == CHUNK - 1).astype(F32)
        for c in reversed(range(ncl)):
            q = _ssd_chunk_fwd(c, ext_ref, shift_ref, dt_ref, cw_ref, cb_ref, dtb_ref, alog_ref, tri_ref, exp_ref)
            rows = pl.ds(c * CHUNK, CHUNK)
            pre, sg, xc, dtr, dtv, A, E, dtE = (q[k] for k in ("pre", "sg", "xc", "dtr", "dtv", "A", "E", "dtE"))
            xs = xc[:, 0:D]
            total = E[CHUNK - 1:CHUNK, :]
            x_dt = xs * dtE
            eE = jnp.exp(E)
            wdec = jnp.exp(total - E)
            zb = zb_ref[rows, :].astype(F32)
            yv = y_ref[rows, :]
            sgz = _sigmoid(zb)
            sz = zb * sgz
            hh = yv * sz
            for g in range(SSD_GROUPS):
                gc = slice(g * GROUP_W, (g + 1) * GROUP_W)
                hg = hh[:, gc]
                r = lax.rsqrt(jnp.mean(hg * hg, axis=-1, keepdims=True) + EPS)
                dyb_g = dyb_ref[rows, gc].astype(F32)
                dn = dyb_g * nw_ref[:, gc]
                dnw_ref[0:1, gc] += jnp.sum(dyb_g * hg * r, axis=0, keepdims=True)
                dy_s[:, gc] = r * dn - hg * (r * r * r) * jnp.mean(dn * hg, axis=-1, keepdims=True)
            dhh = dy_s[...]
            dp_ref[rows, 0:D] = (dhh * yv * (sgz * (1.0 + zb * (1.0 - sgz)))).astype(BF16)
            dy = dhh * sz
            dy_s[...] = dy
            dDacc_ref[0:1, :] += jnp.sum(dy * xs, axis=0, keepdims=True)
            dxc_s[:, 0:D] = dy * dexp_ref[...]
            for g in range(SSD_GROUPS):
                gc = slice(g * GROUP_W, (g + 1) * GROUP_W)
                bcol = slice(D + g * STATE, D + (g + 1) * STATE)
                ccol = slice(D + SSD_GROUPS * STATE + g * STATE, D + SSD_GROUPS * STATE + (g + 1) * STATE)
                Bg = xc[:, bcol].astype(BF16)
                Cg = xc[:, ccol].astype(BF16)
                B2 = jnp.concatenate([Bg, Bg], axis=0)
                cb2 = _dot_nt(Cg, B2)
                htg = st_ref[c, :, gc]
                htb = htg.astype(BF16)
                dhn = dht_ref[:, gc]
                dhnb = dhn.astype(BF16)
                dyg = dy[:, gc]
                eEg = eE[:, gc]
                wg = wdec[:, gc]
                xdg = x_dt[:, gc]
                CH = _dot(Cg, htb)
                dCHb = (dyg * eEg).astype(BF16)
                dC = _dot_nt(dCHb, htb)
                dl = jnp.exp(total[:, gc])
                dht_prev = _dot_tn(Cg, dCHb) + dl * dhn
                dtot = jnp.sum(dhn * htg, axis=0, keepdims=True) * dl
                dxw = _dot(Bg, dhnb)
                dB = _dot_nt((xdg * wg).astype(BF16), dhnb)
                dwd = dxw * xdg * wg
                dtot = dtot + jnp.sum(dwd, axis=0, keepdims=True)
                dE_s[:, gc] = dyg * eEg * CH - dwd + last_row * dtot
                dxdt_s[:, gc] = dxw * wg
                dcb2 = jnp.zeros((CHUNK, LANE), F32)
                for jj in range(GROUP_W // LANE):
                    pc = slice(g * GROUP_W + jj * LANE, g * GROUP_W + (jj + 1) * LANE)
                    Ej = E[:, pc]
                    e2 = jnp.sum(Ej * diag, axis=0, keepdims=True)
                    Lp = jnp.exp(jnp.where(causal, Ej - e2, -1e30))
                    Mp = cb2 * Lp
                    xj = x_dt[:, pc]
                    xbd = jnp.concatenate([xj * lo, xj * hi], axis=0).astype(BF16)
                    dyj = dy[:, pc].astype(BF16)
                    dMp = _dot_nt(dyj, xbd)
                    dxbd = _dot_tn(Mp.astype(BF16), dyj)
                    dxdt_s[:, pc] += dxbd[0:CHUNK, :] * lo + dxbd[CHUNK:2 * CHUNK, :] * hi
                    dcb2 = dcb2 + dMp * Lp
                    dseg = dMp * Mp
                    dE_s[:, pc] += dseg - diag * jnp.sum(dseg, axis=0, keepdims=True)
                dcb2b = dcb2.astype(BF16)
                dC = dC + _dot(dcb2b, B2)
                dB2 = _dot_tn(dcb2b, Cg)
                dB = dB + dB2[0:CHUNK, :] + dB2[CHUNK:2 * CHUNK, :]
                dxc_s[:, bcol] = dB
                dxc_s[:, ccol] = dC
                dht_ref[:, gc] = dht_prev
            dx_dt = dxdt_s[...]
            dxc_s[:, 0:D] += dx_dt * dtE
            red = _sel_right(jnp.concatenate([dE_s[...], dx_dt * xs], axis=0), expT_ref[...])
            da = _sel_left(triT_ref[...], red[0:CHUNK, :])
            ddtv = red[CHUNK:2 * CHUNK, :] + da * A
            dAacc_ref[0:1, :] += jnp.sum(da * dtv, axis=0, keepdims=True)
            ddtr = ddtv * _sigmoid(dtr)
            ddtb_ref[0:1, :] += jnp.sum(ddtr, axis=0, keepdims=True)
            dp_ref[rows, D + XBC_W:D + XBC_W + DT_W] = ddtr.astype(BF16)
            dpre = dxc_s[...] * (sg * (1.0 + pre * (1.0 - sg)))
            dpre_ref[rows, :] = dpre
            dcb_ref[0:1, :] += jnp.sum(dpre, axis=0, keepdims=True)
            for k in range(CONV_K):
                dcw_ref[k:k + 1, :] += jnp.sum(dpre * q["taps"][k], axis=0, keepdims=True)
        dxbc = jnp.zeros((T, XBC_W), F32)
        for k in range(CONV_K):
            dxbc = dxbc + cw_ref[k:k + 1, :] * dpre_ref[pl.ds(CONV_K - 1 - k, T), :]
        dp_ref[:, D:D + XBC_W] = dxbc.astype(BF16)
        dp_ref[:, SEG_SSD[1]:SSD_W] = jnp.zeros((T, SSD_W - SEG_SSD[1]), BF16)
        dpre_ref[T:T + HALO, :] = dpre_ref[0:HALO, :]

        @pl.when(i == nsteps - 1)
        def _():
            dalog_ref[...] = dAacc_ref[...] * (-jnp.exp(alog_ref[...]))
            dD_ref[...] = _dot(dDacc_ref[...], expT_ref[...].astype(F32), precision=HI)

    full = lambda a: pl.BlockSpec(a.shape, lambda i: (0,) * a.ndim)
    hb = T // HALO_BLK
    rev = lambda i: nsteps - 1 - i
    acc = lambda w: pl.BlockSpec((8, w), lambda i: (0, 0))
    return pl.pallas_call(
        body, name="ssd_bwd", grid=(nsteps,),
        in_specs=[pl.BlockSpec((T, D), lambda i: (rev(i), OFF_ZB // D)),
                  pl.BlockSpec((T, XBC_W), lambda i: (rev(i), OFF_XBC // XBC_W)),
                  pl.BlockSpec((HALO_BLK, XBC_W), lambda i: (jnp.maximum(rev(i) * hb - 1, 0), OFF_XBC // XBC_W)),
                  pl.BlockSpec((T, DT_W), lambda i: (rev(i), OFF_DT // DT_W)),
                  pl.BlockSpec((T, D), lambda i: (rev(i), 0)), pl.BlockSpec((T, D), lambda i: (rev(i), 0)),
                  pl.BlockSpec((ncl, STATE, D), lambda i: (rev(i), 0, 0)),
                  full(conv_w), full(conv_b), full(dtb_p), full(alog_p), full(d_exp), full(norm_w),
                  full(tri), full(triT), full(expand), full(expandT), full(shift)],
        out_specs=[pl.BlockSpec((T, SSD_W), lambda i: (rev(i), 0)),
                   acc(XBC_W), acc(XBC_W), acc(DT_W), acc(DT_W), acc(DT_W), acc(D)],
        out_shape=[jax.ShapeDtypeStruct((S, SSD_W), BF16),
                   jax.ShapeDtypeStruct((8, XBC_W), F32), jax.ShapeDtypeStruct((8, XBC_W), F32),
                   jax.ShapeDtypeStruct((8, DT_W), F32), jax.ShapeDtypeStruct((8, DT_W), F32),
                   jax.ShapeDtypeStruct((8, DT_W), F32), jax.ShapeDtypeStruct((8, D), F32)],
        scratch_shapes=[pltpu.VMEM((STATE, D), F32), pltpu.VMEM((HALO_BLK + T, XBC_W), BF16), pltpu.VMEM((T + HALO_BLK, XBC_W), F32),
                        pltpu.VMEM((CHUNK, D), F32), pltpu.VMEM((CHUNK, D), F32), pltpu.VMEM((CHUNK, D), F32),
                        pltpu.VMEM((CHUNK, XBC_W), F32), pltpu.VMEM((8, D), F32), pltpu.VMEM((8, DT_W), F32)],
        compiler_params=_cp(("arbitrary",)),
    )(proj, proj, proj, proj, dyb, y, states, conv_w, conv_b, dtb_p, alog_p, d_exp, norm_w, tri, triT, expand, expandT, shift)


def _head(x, ya, yb, proj, target, gate_b, wout, fw, *, tm):
    S = x.shape[0]

    def body(x_ref, ya_ref, yb_ref, gl0_ref, gl1_ref, t_ref, gb_ref, w_ref, fw_ref,
             dh_ref, dhb_ref, mb_ref, dya_ref, dyb_ref, dgl_ref, loss_ref, dfw_ref, dgb_ref):
        @pl.when(pl.program_id(0) == 0)
        def _():
            loss_ref[...] = jnp.zeros_like(loss_ref)
            dfw_ref[...] = jnp.zeros_like(dfw_ref)
            dgb_ref[...] = jnp.zeros_like(dgb_ref)

        ya_v = ya_ref[...].astype(F32)
        yb_v = yb_ref[...].astype(F32)
        g0 = _sigmoid(gl0_ref[...].astype(F32) + gb_ref[:, 0:D])
        g1 = _sigmoid(gl1_ref[...].astype(F32) + gb_ref[:, D:2 * D])
        mb = (g0 * ya_v + g1 * yb_v).astype(BF16)
        mb_ref[...] = mb
        h = x_ref[...] + _dot(mb, w_ref[...])
        r = lax.rsqrt(jnp.mean(h * h, axis=-1, keepdims=True) + EPS)
        hn = h * r
        err = hn * fw_ref[...] - t_ref[...]
        loss_ref[...] += 0.5 * jnp.sum(jnp.mean(err * err, axis=-1, keepdims=True))
        dyf = err * (1.0 / D)
        dfw_ref[0:1, :] += jnp.sum(dyf * hn, axis=0, keepdims=True)
        dhn = dyf * fw_ref[...]
        dh = r * (dhn - hn * jnp.mean(dhn * hn, axis=-1, keepdims=True))
        dh_ref[...] = dh
        dhb = dh.astype(BF16)
        dhb_ref[...] = dhb
        dm = _dot_nt(dhb, w_ref[...])
        dya_ref[...] = (dm * g0).astype(BF16)
        dyb_ref[...] = (dm * g1).astype(BF16)
        dgl0 = dm * ya_v * g0 * (1.0 - g0)
        dgl1 = dm * yb_v * g1 * (1.0 - g1)
        dgl_ref[:, 0:D] = dgl0.astype(BF16)
        dgl_ref[:, D:2 * D] = dgl1.astype(BF16)
        dgb_ref[0:1, 0:D] += jnp.sum(dgl0, axis=0, keepdims=True)
        dgb_ref[0:1, D:2 * D] += jnp.sum(dgl1, axis=0, keepdims=True)

    row = pl.BlockSpec((tm, D), lambda i: (i, 0))
    seg = lambda off: pl.BlockSpec((tm, D), lambda i: (i, off // D))
    full = lambda a: pl.BlockSpec(a.shape, lambda i: (0,) * a.ndim)
    acc = lambda w: pl.BlockSpec((8, w), lambda i: (0, 0))
    return pl.pallas_call(
        body, name="head", grid=(S // tm,),
        in_specs=[row, row, row, seg(OFF_G0), seg(OFF_G1), row, full(gate_b), full(wout), full(fw)],
        out_specs=[row, row, row, row, row, pl.BlockSpec((tm, 2 * D), lambda i: (i, 0)), acc(LANE), acc(D), acc(2 * D)],
        out_shape=[jax.ShapeDtypeStruct((S, D), F32), jax.ShapeDtypeStruct((S, D), BF16), jax.ShapeDtypeStruct((S, D), BF16),
                   jax.ShapeDtypeStruct((S, D), BF16), jax.ShapeDtypeStruct((S, D), BF16), jax.ShapeDtypeStruct((S, 2 * D), BF16),
                   jax.ShapeDtypeStruct((8, LANE), F32), jax.ShapeDtypeStruct((8, D), F32), jax.ShapeDtypeStruct((8, 2 * D), F32)],
        compiler_params=_cp(("arbitrary",)),
    )(x, ya, yb, proj, proj, target, gate_b, wout, fw)


def _adam_update(g, w_ref, m_ref, v_ref, g_ref, d_ref, m2_ref, v2_ref):
    m2 = ADAM_B1 * m_ref[...] + (1.0 - ADAM_B1) * g
    v2 = ADAM_B2 * v_ref[...] + (1.0 - ADAM_B2) * (g * g)
    m_hat = m2 / (1.0 - ADAM_B1 ** ADAM_STEP)
    v_hat = v2 / (1.0 - ADAM_B2 ** ADAM_STEP)
    g_ref[...] = g
    d_ref[...] = -ADAM_LR * (m_hat / (jnp.sqrt(v_hat) + ADAM_EPS) + ADAM_WD * w_ref[...])
    m2_ref[...] = m2
    v2_ref[...] = v2


def _adamw_own(me, own, landed, w, m, v, *, tr, tc, name):
    _, R, C = landed.shape
    assert R % tr == 0 and C % tc == 0, (name, R, C, tr, tc)

    def body(me_ref, own_ref, p_ref, w_ref, m_ref, v_ref, g_ref, d_ref, m2_ref, v2_ref):
        mine = own_ref[0].astype(F32)
        g = jnp.where(me_ref[0] == 0, mine, p_ref[0].astype(F32))
        for k in range(1, N_DEV):
            g = g + jnp.where(me_ref[0] == k, mine, p_ref[k].astype(F32))
        _adam_update(g, w_ref, m_ref, v_ref, g_ref, d_ref, m2_ref, v2_ref)

    tile = pl.BlockSpec((tr, tc), lambda i, j, me_ref: (i, j))
    return pl.pallas_call(
        body, name=name,
        grid_spec=pltpu.PrefetchScalarGridSpec(
            num_scalar_prefetch=1, grid=(R // tr, C // tc),
            in_specs=[pl.BlockSpec((1, tr, tc), lambda i, j, me_ref: (me_ref[0], i, j)),
                      pl.BlockSpec((N_DEV, tr, tc), lambda i, j, me_ref: (0, i, j)), tile, tile, tile],
            out_specs=[tile, tile, tile, tile]),
        out_shape=[jax.ShapeDtypeStruct((R, C), F32)] * 4,
        compiler_params=_cp(("parallel", "parallel")),
    )(me, own, landed, w, m, v)


def _adamw(parts, w, m, v, *, tr, name):
    _, R, C = parts.shape
    assert R % tr == 0, (name, R, tr)

    def body(p_ref, w_ref, m_ref, v_ref, g_ref, d_ref, m2_ref, v2_ref):
        g = p_ref[0].astype(F32)
        for k in range(1, N_DEV):
            g = g + p_ref[k].astype(F32)
        _adam_update(g, w_ref, m_ref, v_ref, g_ref, d_ref, m2_ref, v2_ref)

    row = pl.BlockSpec((tr, C), lambda i: (i, 0))
    return pl.pallas_call(
        body, name=name, grid=(R // tr,),
        in_specs=[pl.BlockSpec((N_DEV, tr, C), lambda i: (0, i, 0)), row, row, row],
        out_specs=[row, row, row, row],
        out_shape=[jax.ShapeDtypeStruct((R, C), F32)] * 4,
        compiler_params=_cp(("parallel",)),
    )(parts, w, m, v)


def _place():
    x, y, c = lax.axis_index("x"), lax.axis_index("y"), lax.axis_index("c")
    return x, y, c


def _all_gather(arrs, *, name):
    n = len(arrs)

    def body(*refs):
        ins, outs = refs[:n], refs[n:2 * n]
        send_sems, recv_sems, local_sems = refs[2 * n:]
        x, y, c = _place()
        me, sibling = (x, y, c), (x, y, 1 - c)
        chips = [(1 - x, y), (x, 1 - y), (1 - x, 1 - y)]

        def idx(px, py, pc):
            return 4 * px + 2 * py + pc

        def copy(k, a, block, to, src=None):
            slab = outs[a].at[idx(*block)]
            return pltpu.make_async_remote_copy(
                src_ref=slab if src is None else src, dst_ref=slab,
                send_sem=send_sems.at[k, a], recv_sem=recv_sems.at[k, a], device_id=to, device_id_type=MESH)

        mine = [pltpu.make_async_copy(ins[a], outs[a].at[idx(*me)], local_sems.at[a]) for a in range(n)]
        for cp in mine:
            cp.start()
        first = []
        for a in range(n):
            first.append(copy(0, a, me, sibling, src=ins[a]))
            first += [copy(1 + j, a, me, (*chip, c), src=ins[a]) for j, chip in enumerate(chips)]
        for cp in first:
            cp.start()
        passed = []
        for j, chip in enumerate(chips):
            for a in range(n):
                copy(1 + j, a, (*chip, c), me).wait_recv()
                fwd = copy(4 + j, a, (*chip, c), sibling)
                fwd.start()
                passed.append(fwd)
        for a in range(n):
            copy(0, a, sibling, me).wait_recv()
            for j, chip in enumerate(chips):
                copy(4 + j, a, (*chip, 1 - c), me).wait_recv()
        for cp in first + passed:
            cp.wait_send()
        for cp in mine:
            cp.wait()

    anyspec = pl.BlockSpec(memory_space=pl.ANY)
    return pl.pallas_call(
        body, name=name,
        in_specs=[anyspec] * n, out_specs=[anyspec] * n,
        out_shape=[jax.ShapeDtypeStruct((N_DEV,) + a.shape, a.dtype) for a in arrs],
        scratch_shapes=[pltpu.SemaphoreType.DMA((7, n)), pltpu.SemaphoreType.DMA((7, n)), pltpu.SemaphoreType.DMA((n,))],
    )(*arrs)


W_ROWS = SEG_SSD[0] + SSD_PAD_W


GROUP = 16
INTERIOR = 1920


def _interior(k):
    lo = -(-(k * SHARD_IN) // GROUP) * GROUP
    hi = ((k + 1) * SHARD_IN) // GROUP * GROUP
    return lo, hi


def _dest_row(r):
    if r < REF_SGU_END:
        return r
    return r - REF_SGU_END + SEG_SSD[0] if r < REF_GATE_START else r - REF_GATE_START + SEG_GATE[0]


def _shard_pieces(k):
    lo_k, hi_k = _interior(k)
    out = []
    for lo, hi in ((0, REF_SGU_END), (REF_SGU_END, REF_GATE_START), (REF_GATE_START, W_IN)):
        a, b = max(lo, lo_k), min(hi, hi_k)
        if a < b:
            out.append((a - lo_k, b - a, _dest_row(a)))
    return out


GATHER_PARTS = 1


def _shard_parts(k):
    parts = [[] for _ in range(GATHER_PARTS)]
    for s0, n, d0 in _shard_pieces(k):
        step = -(-(n // GROUP) // GATHER_PARTS) * GROUP
        for p in range(GATHER_PARTS):
            a, b = min(p * step, n), min((p + 1) * step, n)
            if a < b:
                parts[p].append((s0 + a, b - a, d0 + a))
    return parts


def _patch_straddlers(wpT, heads, tails):
    todo = [(k, (k * SHARD_IN) % GROUP) for k in range(1, N_DEV) if (k * SHARD_IN) % GROUP]

    def body(w_old, h_ref, t_ref, w_hbm, gbuf, sem):
        del w_old
        row = lax.broadcasted_iota(jnp.int32, (GROUP, GROUP), 0)
        col = lax.broadcasted_iota(jnp.int32, (GROUP, GROUP), 1)
        copies = []
        for j, (k, m) in enumerate(todo):
            from_tail = jnp.where((row < m) & (col == row + (GROUP - m)), 1.0, 0.0).astype(BF16)
            from_head = jnp.where((row >= m) & (col == row - m), 1.0, 0.0).astype(BF16)
            gbuf[j] = (_dot(from_tail, t_ref[k - 1]) + _dot(from_head, h_ref[k])).astype(BF16)
            copies.append(pltpu.make_async_copy(
                gbuf.at[j], w_hbm.at[pl.ds(_dest_row(k * SHARD_IN - m), GROUP), :], sem.at[j]))
            copies[-1].start()
        for c in copies:
            c.wait()

    vmem = pl.BlockSpec(memory_space=pltpu.VMEM)
    anyspec = pl.BlockSpec(memory_space=pl.ANY)
    return pl.pallas_call(
        body, name="patch_straddlers", in_specs=[anyspec, vmem, vmem], out_specs=anyspec,
        out_shape=jax.ShapeDtypeStruct(wpT.shape, wpT.dtype), input_output_aliases={0: 0},
        scratch_shapes=[pltpu.VMEM((len(todo), GROUP, D), BF16), pltpu.SemaphoreType.DMA((len(todo),))],
        compiler_params=_cp(),
    )(wpT, heads, tails)


def _gather_stages(k, win_ref, small, z_ref, n_zero, w_ref, send_sems, recv_sems, local_sems):
    x, y, c = k // 4, (k // 2) % 2, k % 2
    idx = lambda p: 4 * p[0] + 2 * p[1] + p[2]
    me, sib = (x, y, c), (x, y, 1 - c)
    xn, yn, dg = (1 - x, y, c), (x, 1 - y, c), (1 - x, 1 - y, c)
    parts = range(GATHER_PARTS)

    def copies(slot, block, to, part, own=False):
        kb = idx(block)
        out = []
        for j, (s0, n, d0) in enumerate(_shard_parts(kb)[part]):
            dst = w_ref.at[pl.ds(d0, n)]
            out.append((win_ref.at[pl.ds(s0, n)] if own else dst, dst, 2 * part + j))
        if part == 0:
            for j, (src, gathered) in enumerate(small):
                out.append((src if own else gathered.at[kb], gathered.at[kb], 2 * GATHER_PARTS + j))
        return [pltpu.make_async_remote_copy(src_ref=s, dst_ref=d, send_sem=send_sems.at[slot, j], recv_sem=recv_sems.at[slot, j],
                                             device_id=to, device_id_type=MESH) for s, d, j in out]

    def start(cps):
        for cp in cps:
            cp.start()

    def arrived(slot, block, part):
        for cp in copies(slot, block, me, part):
            cp.wait_recv()

    def local():
        pairs = [(win_ref.at[pl.ds(s0, n)], w_ref.at[pl.ds(d0, n)]) for s0, n, d0 in _shard_pieces(k)]
        pairs += [(src, gathered.at[k]) for src, gathered in small] + [(z_ref, w_ref.at[pl.ds(W_IN, n_zero)])]
        return [pltpu.make_async_copy(s, d, local_sems.at[j]) for j, (s, d) in enumerate(pairs)]

    relay = (xn, yn) if c == 1 else (yn, xn)

    def first():
        start(local())
        for p in parts:
            start(copies(0, me, sib, p, own=True) + copies(1, me, xn, p, own=True) + copies(2, me, yn, p, own=True))

    def hand_on():
        for p in parts:
            arrived(1, xn, p)
            start(copies(4, xn, sib, p))
            if c == 1:
                start(copies(3, *relay, p))
            arrived(2, yn, p)
            start(copies(5, yn, sib, p))
            if c == 0:
                start(copies(3, *relay, p))

    def finish():
        for p in parts:
            arrived(3, dg, p)
            start(copies(6, dg, sib, p))
        for p in parts:
            arrived(0, sib, p)
            arrived(4, (1 - x, y, 1 - c), p)
            arrived(5, (x, 1 - y, 1 - c), p)
            arrived(6, (1 - x, 1 - y, 1 - c), p)
        for p in parts:
            sent = (copies(0, me, sib, p, own=True) + copies(1, me, xn, p, own=True) + copies(2, me, yn, p, own=True)
                    + copies(3, *relay, p) + copies(4, xn, sib, p) + copies(5, yn, sib, p) + copies(6, dg, sib, p))
            for cp in sent:
                cp.wait_send()
        for cp in local():
            cp.wait()

    return first, hand_on, finish


def _gather_sems(n_small):
    n_arr = 2 * GATHER_PARTS + n_small
    return [pltpu.SemaphoreType.DMA((7, n_arr)), pltpu.SemaphoreType.DMA((7, n_arr)), pltpu.SemaphoreType.DMA((n_arr + 1,))]


def _gather_weights(win, head, tail, wout, cw, zeros):
    small_in = (wout, cw, head, tail)
    n_zero = zeros.shape[0]
    assert W_IN + n_zero == W_ROWS and W_IN % GROUP == 0

    def body(win_ref, wout_ref, cw_ref, head_ref, tail_ref, z_ref, w_ref, gout_ref, gcw_ref, ghead_ref, gtail_ref, *sems):
        x, y, c = _place()
        me = 4 * x + 2 * y + c
        small = ((wout_ref, gout_ref), (cw_ref, gcw_ref), (head_ref, ghead_ref), (tail_ref, gtail_ref))

        def run(k):
            for stage in _gather_stages(k, win_ref, small, z_ref, n_zero, w_ref, *sems):
                stage()

        for k in range(N_DEV):
            pl.when(me == k)(functools.partial(run, k))

    anyspec = pl.BlockSpec(memory_space=pl.ANY)
    return pl.pallas_call(
        body, name="gather_weights", in_specs=[anyspec] * 6, out_specs=[anyspec] * 5,
        out_shape=[jax.ShapeDtypeStruct((W_ROWS, D), win.dtype)]
        + [jax.ShapeDtypeStruct((N_DEV,) + a.shape, a.dtype) for a in small_in],
        scratch_shapes=_gather_sems(len(small_in)),
    )(win, wout, cw, head, tail, zeros)


_REL = [(dx, dy, dc) for dx in (0, 1) for dy in (0, 1) for dc in (0, 1)][1:]
_HBM = pl.BlockSpec(memory_space=pltpu.HBM)
_SEM = pl.BlockSpec(memory_space=pltpu.SEMAPHORE)
_EFFECT = pltpu.SideEffectType.DATAFLOW_SIDE_EFFECTING


def _peer(k):
    x, y, c = _place()
    dx, dy, dc = _REL[k]
    return (1 - x if dx else x, 1 - y if dy else y, 1 - c if dc else c)


def _exchange_start(parts, *, name):
    n = len(parts)

    def body(*refs):
        ins, lands = refs[:n], refs[n:2 * n]
        send_sems, recv_sems, token = refs[2 * n], refs[2 * n + 1], refs[-1]
        x, y, c = _place()
        me = 4 * x + 2 * y + c
        for a in range(n):
            for k in range(len(_REL)):
                px, py, pc = _peer(k)
                pltpu.make_async_remote_copy(
                    src_ref=ins[a].at[4 * px + 2 * py + pc], dst_ref=lands[a].at[me],
                    send_sem=send_sems.at[len(_REL) * a + k], recv_sem=recv_sems.at[len(_REL) * a + k],
                    device_id=(px, py, pc), device_id_type=MESH).start()
        token[...] = jnp.zeros_like(token)

    sem = pltpu.SemaphoreType.DMA((len(_REL) * n,))
    bufs = [pltpu.HBM(p.shape, p.dtype) for p in parts]
    outs = pl.pallas_call(
        body, name=name,
        out_shape=(sem, sem, *bufs, *bufs, jax.ShapeDtypeStruct((8, LANE), F32)),
        in_specs=(_HBM,) * (2 * n), out_specs=(_SEM, _SEM, *(_HBM,) * (2 * n), pl.BlockSpec(memory_space=pltpu.VMEM)),
        input_output_aliases={i: 2 + i for i in range(2 * n)},
        compiler_params=pltpu.CompilerParams(has_side_effects=_EFFECT),
    )(*[pltpu.with_memory_space_constraint(p, pltpu.HBM) for p in parts],
      *[pltpu.with_memory_space_constraint(lax.empty(p.shape, p.dtype), pltpu.HBM) for p in parts])
    return outs[0], outs[1], outs[2:2 + n], outs[2 + n:2 + 2 * n], outs[-1]


def _exchange_wait(send_sems, recv_sems, parts, lands, after, *, name):
    n = len(parts)

    def body(*refs):
        ins, lands_ = refs[:n], refs[n:2 * n]
        ssem, rsem = refs[2 * n], refs[2 * n + 1]
        for a in range(n):
            for k in range(len(_REL)):
                px, py, pc = _peer(k)
                p = 4 * px + 2 * py + pc
                cp = pltpu.make_async_remote_copy(
                    src_ref=ins[a].at[p], dst_ref=lands_[a].at[p],
                    send_sem=ssem.at[len(_REL) * a + k], recv_sem=rsem.at[len(_REL) * a + k],
                    device_id=(px, py, pc), device_id_type=MESH)
                cp.wait_send()
                cp.wait_recv()

    bufs = [pltpu.HBM(p.shape, p.dtype) for p in parts]
    outs = pl.pallas_call(
        body, name=name, out_shape=(*bufs, *bufs),
        in_specs=(*(_HBM,) * (2 * n), _SEM, _SEM, pl.BlockSpec(memory_space=pl.ANY)), out_specs=(_HBM,) * (2 * n),
        input_output_aliases={i: i for i in range(2 * n)},
        compiler_params=pltpu.CompilerParams(has_side_effects=_EFFECT),
    )(*parts, *lands, send_sems, recv_sems, after)
    return outs[:n], outs[n:]


WEIGHTS = ('norm_w', 'w_in', 'gate_b', 'sgu_norm_g', 'sgu_norm_b', 'sgu_w', 'sgu_b', 'conv_w', 'conv_b', 'dt_bias', 'A_log',
           'D_skip', 'ssd_norm_w', 'w_out', 'final_norm_w')
SHARDED = ('w_in', 'conv_w', 'w_out')
PACK_ROW = 8 * LANE


def _constants():
    tri = np.tril(np.ones((CHUNK, CHUNK), np.float32))
    expand = np.zeros((DT_W, D), np.float32)
    for h in range(HEADS):
        expand[h, h * HEADDIM:(h + 1) * HEADDIM] = 1.0
    sel = np.zeros((D, LANE), np.float32)
    for g in range(SGU_GROUPS):
        sel[g * LANE:(g + 1) * LANE, g] = 1.0
    pos_chunk = np.arange(SGU_BLOCK) // CHUNK
    mask = (pos_chunk[None, :] <= pos_chunk[:, None]).astype(np.float32)
    shift = np.zeros(((CONV_K - 1) * CHUNK, HALO_BLK + CHUNK), np.float32)
    for kk in range(CONV_K - 1):
        for t in range(CHUNK):
            shift[kk * CHUNK + t, HALO_BLK - (CONV_K - 1) + t + kk] = 1.0
    return dict(tri=jnp.asarray(tri, BF16), triT=jnp.asarray(tri.T.copy(), BF16), expand=jnp.asarray(np.tile(expand, (3, 1)), BF16),
                shift=jnp.asarray(shift, BF16),
                expandT=jnp.asarray(expand.T.copy(), BF16), sel=jnp.asarray(sel), mask=jnp.asarray(mask))


def _to_shards(segs):
    starts = np.cumsum([0] + [n for _, n in segs])
    assert starts[-1] == W_IN
    slabs = []
    for k in range(N_DEV):
        pieces = []
        for (s, n), s0 in zip(segs, starts[:-1]):
            lo, hi = max(k * SHARD_IN, s0), min((k + 1) * SHARD_IN, s0 + n)
            if lo < hi:
                pieces.append(s[lo - s0:hi - s0])
        slabs.append(jnp.concatenate(pieces, axis=0))
    return jnp.stack(slabs)


def _local_step(x2, tgt, wpT, wout, cw, p, exchange_small, exchange):
    S = x2.shape[0]
    k = _constants()
    xn, proj = _in_proj(x2, p['norm_w'], wpT, tm=min(1024, S), tn=2048)
    wm32 = p['sgu_w'][0] * k['mask']
    wm = wm32.astype(BF16)
    wmT = jnp.swapaxes(wm32, 1, 2).astype(BF16)
    bias_full = jnp.repeat(p['sgu_b'][0].T, LANE, axis=1)
    tm_sgu = min(512, S)
    ya = _sgu_fwd(proj, p['sgu_norm_g'], p['sgu_norm_b'], wm, bias_full, tm=tm_sgu)
    pad32 = lambda a: jnp.pad(a, ((0, 0), (0, DT_W - HEADS)))
    dtb_p, alog_p = pad32(p['dt_bias']), pad32(p['A_log'])
    d_exp = jnp.repeat(p['D_skip'], HEADDIM, axis=1)
    ssd_args = (cw, p['conv_b'], dtb_p, alog_p, d_exp, p['ssd_norm_w'])
    y, yb, states = _ssd_fwd(proj, *ssd_args, k['tri'], k['expand'], k['shift'])
    dh, dhb, mb, dya, dyb, dgl, loss, dfw, dgb = _head(
        x2, ya, yb, proj, tgt, p['gate_b'], wout, p['final_norm_w'][None, :], tm=min(256, S))
    dsgu, dws, dbsT, dsg, dsb = _sgu_bwd(proj, dya, p['sgu_norm_g'], p['sgu_norm_b'], wm, wmT, bias_full, k['mask'], k['sel'],
                                         tm=tm_sgu)
    dssd, dcw, dcb, ddtb, dalog, dD, dnw = _ssd_bwd(proj, dyb, y, states, *ssd_args, k['tri'], k['triT'], k['expand'], k['expandT'],
                                                    k['shift'])
    grads = dict(
        gate_b=dgb[0:1], sgu_norm_g=dsg[0:1], sgu_norm_b=dsb[0:1], sgu_w=dws[None],
        sgu_b=dbsT[:, :SGU_GROUPS].T[None], conv_w=dcw[0:CONV_K][None], conv_b=dcb[0:1], dt_bias=ddtb[0:1, :HEADS],
        A_log=dalog[0:1, :HEADS], D_skip=dD[0:1, :HEADS], ssd_norm_w=dnw[0:1], final_norm_w=dfw[0])
    tw = dict(trans_a=True, out_dtype=BF16, tm=1024, tn=512, tk=S)
    dw_out = _matmul(mb, dhb, name="dw_out", **tw)
    token = exchange_small(loss[0, 0], grads, dw_out)
    dwT_sgu, dwT_gate, dwT_ssd = _dw_in([dsgu, dgl, dssd], xn, token, tm=256)
    token = exchange([(dwT_sgu, SEG_SGU[1]), (dwT_ssd, W_IN - SEG_SSD[0]), (dwT_gate, SEG_GATE[1])])
    tm, tn = min(1024, S), 1024
    dxn = _matmul(dsgu, wpT, tm=tm, tn=512, tk=SEG_SGU[1], after=token, name="dxn_sgu")
    dxn = _matmul(dgl, wpT, b_koff=SEG_GATE[0] // 2048, tm=tm, tn=tn, tk=2048, add=dxn, name="dxn_gate")
    grad_x, dnorm = _dxn_last_norm(dssd, wpT, SEG_SSD[0], dxn, x2, dh, p['norm_w'], tm=min(256, S))
    return grad_x, dnorm[0:1]


def _pack(arrs):
    rows, offs, r = [], [], 0
    for a in arrs:
        n = a.size
        nr = -(-n // PACK_ROW) * 8
        rows.append(jnp.pad(a.reshape(-1).astype(F32), (0, nr * LANE - n)).reshape(nr, LANE))
        offs.append(r)
        r += nr
    return jnp.concatenate(rows, axis=0), offs


def kernel(x, norm_w, w_in, gate_b, sgu_norm_g, sgu_norm_b, sgu_w, sgu_b, conv_w, conv_b, dt_bias, A_log, D_skip, ssd_norm_w, w_out, final_norm_w, loss_target, m_norm_w, m_w_in, m_gate_b, m_sgu_norm_g, m_sgu_norm_b, m_sgu_w, m_sgu_b, m_conv_w, m_conv_b, m_dt_bias, m_A_log, m_D_skip, m_ssd_norm_w, m_w_out, m_final_norm_w, v_norm_w, v_w_in, v_gate_b, v_sgu_norm_g, v_sgu_norm_b, v_sgu_w, v_sgu_b, v_conv_w, v_conv_b, v_dt_bias, v_A_log, v_D_skip, v_ssd_norm_w, v_w_out, v_final_norm_w):
    w = dict(norm_w=norm_w, w_in=w_in, gate_b=gate_b, sgu_norm_g=sgu_norm_g, sgu_norm_b=sgu_norm_b, sgu_w=sgu_w, sgu_b=sgu_b,
             conv_w=conv_w, conv_b=conv_b, dt_bias=dt_bias, A_log=A_log, D_skip=D_skip, ssd_norm_w=ssd_norm_w, w_out=w_out,
             final_norm_w=final_norm_w)
    m = dict(norm_w=m_norm_w, w_in=m_w_in, gate_b=m_gate_b, sgu_norm_g=m_sgu_norm_g, sgu_norm_b=m_sgu_norm_b, sgu_w=m_sgu_w,
             sgu_b=m_sgu_b, conv_w=m_conv_w, conv_b=m_conv_b, dt_bias=m_dt_bias, A_log=m_A_log, D_skip=m_D_skip,
             ssd_norm_w=m_ssd_norm_w, w_out=m_w_out, final_norm_w=m_final_norm_w)
    v = dict(norm_w=v_norm_w, w_in=v_w_in, gate_b=v_gate_b, sgu_norm_g=v_sgu_norm_g, sgu_norm_b=v_sgu_norm_b, sgu_w=v_sgu_w,
             sgu_b=v_sgu_b, conv_w=v_conv_w, conv_b=v_conv_b, dt_bias=v_dt_bias, A_log=v_A_log, D_skip=v_D_skip,
             ssd_norm_w=v_ssd_norm_w, w_out=v_w_out, final_norm_w=v_final_norm_w)
    me = 4 * lax.axis_index("x") + 2 * lax.axis_index("y") + lax.axis_index("c")
    shard_cw = XBC_W // N_DEV

    tpose = lambda a: jnp.swapaxes(a[0], 0, 1)
    wT = tpose(w_in).astype(BF16)
    first_group = (GROUP - (me * SHARD_IN) % GROUP) % GROUP
    window = lax.dynamic_slice(jnp.pad(wT, ((0, GROUP), (0, 0))), (first_group, 0), (INTERIOR, D))
    wpT, g_out, g_cw, heads, tails = _gather_weights(window, wT[:GROUP], wT[SHARD_IN - GROUP:], w_out[0].astype(BF16),
                                                     conv_w[0], jnp.zeros((W_ROWS - W_IN, D), BF16))
    wpT = _patch_straddlers(wpT, heads, tails)
    wout_full = g_out.reshape(D, D)
    cw_full = jnp.swapaxes(g_cw, 0, 1).reshape(CONV_K, XBC_W)

    flight = {}

    small = [n for n in WEIGHTS if n not in SHARDED and n != 'norm_w']
    early = {}

    def exchange_small(loss_part, grads, dw_out):
        early['packed'], early['offs'] = _pack([grads[n] for n in small] + [loss_part, grads['conv_w']])
        parts = [jnp.broadcast_to(early['packed'][None], (N_DEV,) + early['packed'].shape), dw_out.reshape(N_DEV, D // N_DEV, D)]
        early['sems'], early['rsems'], early['parts'], early['lands'], token = _exchange_start(parts, name="small_start")
        return token

    def exchange(dw_inT_segs):
        parts = [_to_shards(dw_inT_segs)]
        flight['sems'], flight['rsems'], flight['parts'], flight['lands'], token = _exchange_start(parts, name="exchange_start")
        return token

    grad_x, dnorm = _local_step(x[0], loss_target[0], wpT, wout_full, cw_full, w, exchange_small, exchange)
    (_, own_out), (land_small, land_out) = _exchange_wait(
        early['sems'], early['rsems'], early['parts'], early['lands'], grad_x, name="small_wait")
    (own_in,), (land_in,) = _exchange_wait(
        flight['sems'], flight['rsems'], flight['parts'], flight['lands'], grad_x, name="exchange_wait")
    me_arr = jnp.reshape(me, (1,)).astype(jnp.int32)
    res = {}
    res['w_in'] = [jnp.swapaxes(o, 0, 1) for o in _adamw_own(
        me_arr, own_in, land_in, tpose(w_in), tpose(m_w_in), tpose(v_w_in), tr=SHARD_IN, tc=256, name="adamw_w_in")]
    res['w_out'] = _adamw_own(me_arr, own_out, land_out, w_out[0], m_w_out[0], v_w_out[0], tr=128, tc=D, name="adamw_w_out")

    (norm_parts,) = _all_gather([_pack([dnorm])[0]], name="gather_norm")
    norm_outs = _adamw(norm_parts, *[_pack([d['norm_w']])[0] for d in (w, m, v)], tr=norm_parts.shape[1], name="adamw_norm")
    res['norm_w'] = [o.reshape(-1)[:D].reshape(w['norm_w'].shape) for o in norm_outs]

    offs = early['offs']
    gathered = lax.dynamic_update_slice(land_small, early['packed'][None], (me, 0, 0))
    off_loss, off_cw = offs[-2], offs[-1]
    cw_parts = gathered[:, off_cw:, :].reshape(N_DEV, CONV_K, XBC_W)
    cw_parts = lax.dynamic_slice_in_dim(cw_parts, me * shard_cw, shard_cw, axis=2)
    cw_rows = _pack([cw_parts[0]])[0].shape[0]
    cw_parts = jnp.pad(cw_parts.reshape(N_DEV, -1), ((0, 0), (0, cw_rows * LANE - CONV_K * shard_cw))).reshape(N_DEV, cw_rows, LANE)
    parts = jnp.concatenate([gathered[:, :off_cw, :], cw_parts], axis=1)
    zero = jnp.zeros((), F32)
    packs = [_pack([d[n] for n in small] + [zero, d['conv_w']])[0] for d in (w, m, v)]
    outs = _adamw(parts, *packs, tr=parts.shape[1], name="adamw_small")

    def unpack(o, name):
        if name == 'conv_w':
            return o[off_cw:off_cw + cw_rows].reshape(-1)[:CONV_K * shard_cw].reshape(w['conv_w'].shape)
        r0 = offs[small.index(name)]
        n = w[name].size
        return o[r0:r0 + -(-n // PACK_ROW) * 8].reshape(-1)[:n].reshape(w[name].shape)

    for n in small + ['conv_w']:
        res[n] = [unpack(o, n) for o in outs]
    for n in ('w_in', 'w_out'):
        res[n] = [o[None] for o in res[n]]
    loss = outs[0][off_loss, 0]
    return (loss, grad_x[None], *[res[n][0] for n in WEIGHTS], *[res[n][1] for n in WEIGHTS],
            *[res[n][2] for n in WEIGHTS], *[res[n][3] for n in WEIGHTS])
```

```python
import functools

import numpy as np
import jax
import jax.numpy as jnp
from jax import lax
from jax.experimental import pallas as pl
from jax.experimental.pallas import tpu as pltpu

F32 = jnp.float32
BF16 = jnp.bfloat16
HI = lax.Precision.HIGHEST
MESH = pl.DeviceIdType.MESH

D = 2048
EPS = 1e-5
SGU_BLOCK = 128
SGU_GROUPS = 16
CHUNK = 64
HEADS = 32
HEADDIM = 64
SSD_GROUPS = 4
GROUP_W = D // SSD_GROUPS
STATE = 128
CONV_K = 4
XBC_W = D + 2 * SSD_GROUPS * STATE
W_IN = 15392
N_DEV = 8
SHARD_IN = W_IN // N_DEV
ADAM_LR, ADAM_B1, ADAM_B2, ADAM_EPS, ADAM_WD, ADAM_STEP = 0.001, 0.9, 0.999, 1e-08, 0.01, 10

REF_SGU_END = 3 * D
REF_GATE_START = W_IN - 2 * D
LANE = 128
DT_W = LANE
OFF_U, OFF_V, OFF_ZA, OFF_G0, OFF_G1, OFF_ZB = (i * D for i in range(6))
OFF_XBC = OFF_ZB + D
OFF_DT = OFF_XBC + XBC_W
SEG_SGU = (OFF_U, 3 * D)
SEG_GATE = (OFF_G0, 2 * D)
SEG_SSD = (OFF_ZB, D + XBC_W + DT_W)
WP = SEG_SSD[0] + SEG_SSD[1]
SSD_PAD_W = 3 * D
VMEM_BYTES = 64 * 1024 * 1024
VMEM_LIMIT = VMEM_BYTES - 8 * 1024 * 1024


def _cp(sem=None, vmem=VMEM_LIMIT):
    return pltpu.CompilerParams(dimension_semantics=sem, vmem_limit_bytes=vmem)


def _sigmoid(x):
    return 1.0 / (1.0 + jnp.exp(-x))


def _softplus(x):
    return jnp.maximum(x, 0.0) + jnp.log(1.0 + jnp.exp(-jnp.abs(x)))


def _dot(a, b, precision=None):
    return jnp.dot(a, b, preferred_element_type=F32, precision=precision)


def _dot_nt(a, b, precision=None):
    return lax.dot_general(a, b, (((1,), (1,)), ((), ())), preferred_element_type=F32, precision=precision)


def _dot_tn(a, b, precision=None):
    return lax.dot_general(a, b, (((0,), (0,)), ((), ())), preferred_element_type=F32, precision=precision)


def _split3(a):
    hi = a.astype(BF16)
    r = a - hi.astype(F32)
    mid = r.astype(BF16)
    return hi, mid, (r - mid.astype(F32)).astype(BF16)


def _sel_right(a, sel01):
    m = a.shape[0]
    r = _dot(jnp.concatenate(_split3(a), axis=0), sel01)
    return (r[0:m] + r[m:2 * m]) + r[2 * m:3 * m]


def _sel_right_k(a, sel01_x3):
    return _dot(jnp.concatenate(_split3(a), axis=1), sel01_x3)


def _sel_left(sel01, a):
    n = a.shape[1]
    r = _dot(sel01, jnp.concatenate(_split3(a), axis=1))
    return (r[:, 0:n] + r[:, n:2 * n]) + r[:, 2 * n:3 * n]


def _matmul(a, b, *, trans_a=False, trans_b=False, b_koff=0, out_dtype=F32, tm, tn, tk, add=None, after=None, name):
    K, M = a.shape if trans_a else a.shape[::-1]
    N = b.shape[0] if trans_b else b.shape[1]
    assert M % tm == 0 and N % tn == 0 and K % tk == 0 and not (trans_a and trans_b), (name, M, N, K, tm, tn, tk)
    nk = K // tk

    def body(*refs):
        a_ref, b_ref = refs[:2]
        add_ref = refs[2] if add is not None else None
        o_ref, acc_ref = refs[-2:]
        k = pl.program_id(2)
        if trans_a:
            part = _dot_tn(a_ref[...], b_ref[...])
        else:
            part = _dot_nt(a_ref[...], b_ref[...]) if trans_b else _dot(a_ref[...], b_ref[...])

        def result(r):
            if add_ref is not None:
                r = r + add_ref[...]
            return r.astype(out_dtype)

        if nk == 1:
            o_ref[...] = result(part)
        else:
            @pl.when(k == 0)
            def _():
                acc_ref[...] = part

            @pl.when(jnp.logical_and(k > 0, k < nk - 1))
            def _():
                acc_ref[...] += part

            @pl.when(k == nk - 1)
            def _():
                o_ref[...] = result(acc_ref[...] + part)

    in_specs = [pl.BlockSpec((tk, tm), lambda i, j, k: (k, i)) if trans_a else pl.BlockSpec((tm, tk), lambda i, j, k: (i, k)),
                pl.BlockSpec((tn, tk), lambda i, j, k: (j, k)) if trans_b else pl.BlockSpec((tk, tn), lambda i, j, k: (k + b_koff, j))]
    args = [a, b]
    if add is not None:
        in_specs.append(pl.BlockSpec((tm, tn), lambda i, j, k: (i, j)))
        args.append(add)
    if after is not None:
        in_specs.append(pl.BlockSpec(memory_space=pl.ANY))
        args.append(after)
    return pl.pallas_call(
        body, name=name, grid=(M // tm, N // tn, nk), in_specs=in_specs,
        out_specs=pl.BlockSpec((tm, tn), lambda i, j, k: (i, j)),
        out_shape=jax.ShapeDtypeStruct((M, N), out_dtype),
        scratch_shapes=[pltpu.VMEM((tm, tn), F32)],
        compiler_params=_cp(("parallel", "parallel", "arbitrary")),
    )(*args)


def _dxn_last_norm(seg, wpT, row0, dxn, x, dh, w, *, tm):
    S, cols = seg.shape
    assert S % tm == 0 and row0 % GROUP == 0
    ntile = S // tm

    def body(seg_hbm, w_hbm, dxn_hbm, x_hbm, dh_hbm, nw_ref, gx_hbm, dw_ref, b_ref, abuf, pbuf, xbuf, hbuf, obuf, isem, osem):
        b_copy = pltpu.make_async_copy(w_hbm.at[pl.ds(row0, cols), :], b_ref, osem.at[2])
        b_copy.start()

        def first(t):
            return t * tm if isinstance(t, int) else pl.multiple_of(t * tm, tm)

        def fetch(t):
            rows, slot = pl.ds(first(t), tm), t % 2
            return [pltpu.make_async_copy(src.at[rows, :], buf.at[slot], isem.at[slot, j])
                    for j, (src, buf) in enumerate(((seg_hbm, abuf), (dxn_hbm, pbuf), (x_hbm, xbuf), (dh_hbm, hbuf)))]

        def write(t):
            return pltpu.make_async_copy(obuf.at[t % 2], gx_hbm.at[pl.ds(first(t), tm), :], osem.at[t % 2])

        for cp in fetch(0):
            cp.start()
        dw_ref[...] = jnp.zeros_like(dw_ref)
        b_copy.wait()

        def step(t, carry):
            slot = t % 2
            for cp in fetch(t):
                cp.wait()

            @pl.when(t + 1 < ntile)
            def _():
                for cp in fetch(t + 1):
                    cp.start()

            dxn_v = pbuf[slot] + _dot(abuf[slot], b_ref[...])
            xv = xbuf[slot]
            r = lax.rsqrt(jnp.mean(xv * xv, axis=-1, keepdims=True) + EPS)
            xh = xv * r
            dxh = dxn_v * nw_ref[...]
            gx = hbuf[slot] + r * (dxh - xh * jnp.mean(dxh * xh, axis=-1, keepdims=True))
            dw_ref[0:1, :] += jnp.sum(dxn_v * xh, axis=0, keepdims=True)

            @pl.when(t >= 2)
            def _():
                write(t - 2).wait()

            obuf[slot] = gx
            write(t).start()
            return carry

        lax.fori_loop(0, ntile, step, 0)
        for t in range(max(ntile - 2, 0), ntile):
            write(t).wait()

    anyspec = pl.BlockSpec(memory_space=pl.ANY)
    vmem = pl.BlockSpec(memory_space=pltpu.VMEM)
    return pl.pallas_call(
        body, name="dxn_last_norm", in_specs=[anyspec] * 5 + [vmem], out_specs=[anyspec, vmem],
        out_shape=[jax.ShapeDtypeStruct((S, D), F32), jax.ShapeDtypeStruct((8, D), F32)],
        scratch_shapes=[pltpu.VMEM((cols, D), BF16), pltpu.VMEM((2, tm, cols), BF16), pltpu.VMEM((2, tm, D), F32),
                        pltpu.VMEM((2, tm, D), F32), pltpu.VMEM((2, tm, D), F32), pltpu.VMEM((2, tm, D), F32),
                        pltpu.SemaphoreType.DMA((2, 4)), pltpu.SemaphoreType.DMA((3,))],
        compiler_params=_cp(),
    )(seg, wpT, dxn, x, dh, w)


DW_BUFS = 3


def _dw_in(segs, xn, after, *, tm):
    S = xn.shape[0]
    n = len(segs)
    assert all(a.shape[0] == S and a.shape[1] % tm == 0 for a in segs)

    def body(*refs):
        a_refs, xn_hbm = refs[:n], refs[n]
        o_refs = refs[-(n + 5):-5]
        xn_ref, abuf, obuf, asem, osem = refs[-5:]
        xn_copy = pltpu.make_async_copy(xn_hbm, xn_ref, osem.at[2])
        xn_copy.start()
        for q, (a_ref, o_ref) in enumerate(zip(a_refs, o_refs)):
            ntile = a_ref.shape[1] // tm

            def first(t):
                return t * tm if isinstance(t, int) else pl.multiple_of(t * tm, tm)

            def fetch(t, a_ref=a_ref):
                return pltpu.make_async_copy(a_ref.at[:, pl.ds(first(t), tm)], abuf.at[t % DW_BUFS], asem.at[t % DW_BUFS])

            def write(t, o_ref=o_ref):
                return pltpu.make_async_copy(obuf.at[t % 2], o_ref.at[pl.ds(first(t), tm), :], osem.at[t % 2])

            for t in range(min(DW_BUFS - 1, ntile)):
                fetch(t).start()
            if q == 0:
                xn_copy.wait()

            def step(t, carry, fetch=fetch, write=write, ntile=ntile):
                fetch(t).wait()

                @pl.when(t + DW_BUFS - 1 < ntile)
                def _():
                    fetch(t + DW_BUFS - 1).start()

                res = _dot_tn(abuf[t % DW_BUFS], xn_ref[...]).astype(BF16)

                @pl.when(t >= 2)
                def _():
                    write(t - 2).wait()

                obuf[t % 2] = res
                write(t).start()
                return carry

            lax.fori_loop(0, ntile, step, 0)
            for t in range(max(ntile - 2, 0), ntile):
                write(t).wait()

    anyspec = pl.BlockSpec(memory_space=pl.ANY)
    return pl.pallas_call(
        body, name="dw_in", in_specs=[anyspec] * (n + 1 + (after is not None)), out_specs=[anyspec] * n,
        out_shape=[jax.ShapeDtypeStruct((a.shape[1], D), BF16) for a in segs],
        scratch_shapes=[pltpu.VMEM((S, D), BF16), pltpu.VMEM((DW_BUFS, S, tm), BF16), pltpu.VMEM((2, tm, D), BF16),
                        pltpu.SemaphoreType.DMA((DW_BUFS,)), pltpu.SemaphoreType.DMA((3,))],
        compiler_params=_cp(),
    )(*segs, xn, *([after] if after is not None else []))


def _in_proj(x, w, wpT, *, tm, tn):
    S = x.shape[0]
    N = wpT.shape[0]
    assert S % tm == 0 and N % tn == 0, (S, N, tm, tn)

    def body(x_ref, w_ref, b_ref, xn_ref, o_ref, xs_ref):
        @pl.when(pl.program_id(1) == 0)
        def _():
            xv = x_ref[...]
            r = lax.rsqrt(jnp.mean(xv * xv, axis=-1, keepdims=True) + EPS)
            xs = (xv * r * w_ref[...]).astype(BF16)
            xs_ref[...] = xs
            xn_ref[...] = xs

        o_ref[...] = _dot_nt(xs_ref[...], b_ref[...]).astype(BF16)

    return pl.pallas_call(
        body, name="in_proj", grid=(S // tm, N // tn),
        in_specs=[pl.BlockSpec((tm, D), lambda i, j: (i, 0)), pl.BlockSpec((1, D), lambda i, j: (0, 0)),
                  pl.BlockSpec((tn, D), lambda i, j: (j, 0))],
        out_specs=[pl.BlockSpec((tm, D), lambda i, j: (i, 0)), pl.BlockSpec((tm, tn), lambda i, j: (i, j))],
        out_shape=[jax.ShapeDtypeStruct((S, D), BF16), jax.ShapeDtypeStruct((S, N), BF16)],
        scratch_shapes=[pltpu.VMEM((tm, D), BF16)],
        compiler_params=_cp(("parallel", "arbitrary")),
    )(x, w, wpT)


def _sgu_core(u_ref, v_ref, z_ref, g_ref, b_ref, wm_ref, bias_ref, vnb_ref, mixed_ref, tm):
    v = v_ref[...].astype(F32)
    mu = jnp.mean(v, axis=-1, keepdims=True)
    vc = v - mu
    rs = lax.rsqrt(jnp.mean(vc * vc, axis=-1, keepdims=True) + EPS)
    vh = vc * rs
    vnb_ref[...] = (vh * g_ref[...] + b_ref[...]).astype(BF16)
    for blk in range(tm // SGU_BLOCK):
        rows = pl.ds(blk * SGU_BLOCK, SGU_BLOCK)
        for gi in range(SGU_GROUPS):
            cols = pl.ds(gi * LANE, LANE)
            mixed_ref[rows, cols] = _dot(wm_ref[gi], vnb_ref[rows, cols]) + bias_ref[:, cols]
    return vh, rs


def _sgu_fwd(proj, g, b, wm, bias_full, *, tm):
    S = proj.shape[0]

    def body(u_ref, v_ref, z_ref, g_ref, b_ref, wm_ref, bias_ref, y_ref, vnb_ref, mixed_ref):
        _sgu_core(u_ref, v_ref, z_ref, g_ref, b_ref, wm_ref, bias_ref, vnb_ref, mixed_ref, tm)
        z = z_ref[...].astype(F32)
        y_ref[...] = (u_ref[...].astype(F32) * mixed_ref[...] * (z * _sigmoid(z))).astype(BF16)

    seg = lambda off: pl.BlockSpec((tm, D), lambda i: (i, off // D))
    full = lambda a: pl.BlockSpec(a.shape, lambda i: (0,) * a.ndim)
    return pl.pallas_call(
        body, name="sgu_fwd", grid=(S // tm,),
        in_specs=[seg(OFF_U), seg(OFF_V), seg(OFF_ZA), full(g), full(b), full(wm), full(bias_full)],
        out_specs=pl.BlockSpec((tm, D), lambda i: (i, 0)),
        out_shape=jax.ShapeDtypeStruct((S, D), BF16),
        scratch_shapes=[pltpu.VMEM((tm, D), BF16), pltpu.VMEM((tm, D), F32)],
        compiler_params=_cp(("parallel",)),
    )(proj, proj, proj, g, b, wm, bias_full)


def _sgu_bwd(proj, dy, g, b, wm, wmT, bias_full, mask, sel, *, tm):
    S = proj.shape[0]
    nsteps = S // tm

    def body(u_ref, v_ref, z_ref, dy_ref, g_ref, b_ref, wm_ref, wmT_ref, bias_ref, mask_ref, sel_ref,
             dp_ref, dws_ref, dbs_ref, dg_ref, db_ref, vnb_ref, mixed_ref, dmb_ref, dvn_ref, dbias_ref):
        i = pl.program_id(0)

        @pl.when(i == 0)
        def _():
            dws_ref[...] = jnp.zeros_like(dws_ref)
            dg_ref[...] = jnp.zeros_like(dg_ref)
            db_ref[...] = jnp.zeros_like(db_ref)
            dbias_ref[...] = jnp.zeros_like(dbias_ref)

        vh, rs = _sgu_core(u_ref, v_ref, z_ref, g_ref, b_ref, wm_ref, bias_ref, vnb_ref, mixed_ref, tm)
        u = u_ref[...].astype(F32)
        z = z_ref[...].astype(F32)
        dy_v = dy_ref[...].astype(F32)
        mixed = mixed_ref[...]
        sg = _sigmoid(z)
        sz = z * sg
        dp_ref[:, 0:D] = (dy_v * mixed * sz).astype(BF16)
        dp_ref[:, 2 * D:3 * D] = (dy_v * u * mixed * (sg * (1.0 + z * (1.0 - sg)))).astype(BF16)
        dmixed = dy_v * u * sz
        dmb_ref[...] = dmixed.astype(BF16)
        for blk in range(tm // SGU_BLOCK):
            dbias_ref[...] += dmixed[blk * SGU_BLOCK:(blk + 1) * SGU_BLOCK, :]
        for blk in range(tm // SGU_BLOCK):
            rows = pl.ds(blk * SGU_BLOCK, SGU_BLOCK)
            for gi in range(SGU_GROUPS):
                cols = pl.ds(gi * LANE, LANE)
                dm = dmb_ref[rows, cols]
                dvn_ref[rows, cols] = _dot(wmT_ref[gi], dm)
                dws_ref[gi] += _dot_nt(dm, vnb_ref[rows, cols])
        dvn = dvn_ref[...]
        dg_ref[0:1, :] += jnp.sum(dvn * vh, axis=0, keepdims=True)
        db_ref[0:1, :] += jnp.sum(dvn, axis=0, keepdims=True)
        dvh = dvn * g_ref[...]
        dv = rs * (dvh - jnp.mean(dvh, axis=-1, keepdims=True) - vh * jnp.mean(dvh * vh, axis=-1, keepdims=True))
        dp_ref[:, D:2 * D] = dv.astype(BF16)

        @pl.when(i == nsteps - 1)
        def _():
            for gi in range(SGU_GROUPS):
                dws_ref[gi] = dws_ref[gi] * mask_ref[...]
            dbs_ref[...] = _dot(dbias_ref[...], sel_ref[...], precision=HI)

    seg = lambda off: pl.BlockSpec((tm, D), lambda i: (i, off // D))
    full = lambda a: pl.BlockSpec(a.shape, lambda i: (0,) * a.ndim)
    return pl.pallas_call(
        body, name="sgu_bwd", grid=(nsteps,),
        in_specs=[seg(OFF_U), seg(OFF_V), seg(OFF_ZA), pl.BlockSpec((tm, D), lambda i: (i, 0)),
                  full(g), full(b), full(wm), full(wmT), full(bias_full), full(mask), full(sel)],
        out_specs=[pl.BlockSpec((tm, 3 * D), lambda i: (i, 0)),
                   pl.BlockSpec((SGU_GROUPS, SGU_BLOCK, SGU_BLOCK), lambda i: (0, 0, 0)),
                   pl.BlockSpec((SGU_BLOCK, LANE), lambda i: (0, 0)),
                   pl.BlockSpec((8, D), lambda i: (0, 0)), pl.BlockSpec((8, D), lambda i: (0, 0))],
        out_shape=[jax.ShapeDtypeStruct((S, 3 * D), BF16),
                   jax.ShapeDtypeStruct((SGU_GROUPS, SGU_BLOCK, SGU_BLOCK), F32),
                   jax.ShapeDtypeStruct((SGU_BLOCK, LANE), F32),
                   jax.ShapeDtypeStruct((8, D), F32), jax.ShapeDtypeStruct((8, D), F32)],
        scratch_shapes=[pltpu.VMEM((tm, D), BF16), pltpu.VMEM((tm, D), F32), pltpu.VMEM((tm, D), BF16),
                        pltpu.VMEM((tm, D), F32), pltpu.VMEM((SGU_BLOCK, D), F32)],
        compiler_params=_cp(("arbitrary",)),
    )(proj, proj, proj, dy, g, b, wm, wmT, bias_full, mask, sel)


SSD_T = 2 * CHUNK
HALO = 8
HALO_BLK = 16


def _pair_masks():
    row = lax.broadcasted_iota(jnp.int32, (CHUNK, LANE), 0)
    lane = lax.broadcasted_iota(jnp.int32, (CHUNK, LANE), 1)
    pos = jnp.where(lane >= CHUNK, lane - CHUNK, lane)
    diag = (row == pos).astype(F32)
    causal = row >= pos
    lo = (lane < CHUNK).astype(F32)
    return diag, causal, lo, 1.0 - lo


def _ssd_chunk_fwd(c, ext_ref, shift_ref, dt_ref, cw_ref, cb_ref, dtb_ref, alog_ref, tri_ref, exp_ref):
    r0 = c * CHUNK
    win = ext_ref[pl.ds(r0, HALO_BLK + CHUNK), :]
    sh = _dot(shift_ref[...], win)
    taps = [sh[k * CHUNK:(k + 1) * CHUNK] for k in range(CONV_K - 1)] + [win[HALO_BLK:].astype(F32)]
    pre = cb_ref[...] + sum(cw_ref[k:k + 1, :] * taps[k] for k in range(CONV_K))
    sg = _sigmoid(pre)
    xc = pre * sg
    dtr = dt_ref[pl.ds(r0, CHUNK), :].astype(F32) + dtb_ref[...]
    dtv = _softplus(dtr)
    A = -jnp.exp(alog_ref[...])
    acs = _sel_left(tri_ref[...], dtv * A)
    both = _sel_right_k(jnp.concatenate([acs, dtv], axis=0), exp_ref[...])
    E, dtE = both[0:CHUNK], both[CHUNK:2 * CHUNK]
    return dict(taps=taps, pre=pre, sg=sg, xc=xc, dtr=dtr, dtv=dtv, A=A, E=E, dtE=dtE)


def _ssd_fwd(proj, conv_w, conv_b, dtb_p, alog_p, d_exp, norm_w, tri, expand, shift):
    S = proj.shape[0]
    T = SSD_T
    nsteps = S // T
    ncl = T // CHUNK

    def body(zb_ref, xbc_ref, halo_ref, dt_ref, cw_ref, cb_ref, dtb_ref, alog_ref, dexp_ref, nw_ref, tri_ref, exp_ref, shift_ref,
             y_ref, yb_ref, st_ref, ht_ref, ext_ref):
        i = pl.program_id(0)

        @pl.when(i == 0)
        def _():
            ht_ref[...] = jnp.zeros_like(ht_ref)
            ext_ref[0:HALO_BLK, :] = jnp.zeros((HALO_BLK, XBC_W), BF16)

        @pl.when(i > 0)
        def _():
            ext_ref[0:HALO_BLK, :] = halo_ref[...]

        ext_ref[HALO_BLK:HALO_BLK + T, :] = xbc_ref[...]
        diag, causal, lo, hi = _pair_masks()
        for c in range(ncl):
            q = _ssd_chunk_fwd(c, ext_ref, shift_ref, dt_ref, cw_ref, cb_ref, dtb_ref, alog_ref, tri_ref, exp_ref)
            rows = pl.ds(c * CHUNK, CHUNK)
            xc, E, dtE = q["xc"], q["E"], q["dtE"]
            xs = xc[:, 0:D]
            total = E[CHUNK - 1:CHUNK, :]
            x_dt = xs * dtE
            eE = jnp.exp(E)
            xw = x_dt * jnp.exp(total - E)
            st_ref[c] = ht_ref[...]
            for g in range(SSD_GROUPS):
                gc = slice(g * GROUP_W, (g + 1) * GROUP_W)
                Bg = xc[:, D + g * STATE:D + (g + 1) * STATE].astype(BF16)
                Cg = xc[:, D + SSD_GROUPS * STATE + g * STATE:D + SSD_GROUPS * STATE + (g + 1) * STATE].astype(BF16)
                cb2 = _dot_nt(Cg, jnp.concatenate([Bg, Bg], axis=0))
                htg = ht_ref[:, gc]
                y_ref[rows, gc] = eE[:, gc] * _dot(Cg, htg.astype(BF16)) + xs[:, gc] * dexp_ref[:, gc]
                for jj in range(GROUP_W // LANE):
                    pc = slice(g * GROUP_W + jj * LANE, g * GROUP_W + (jj + 1) * LANE)
                    Ej = E[:, pc]
                    e2 = jnp.sum(Ej * diag, axis=0, keepdims=True)
                    Mp = cb2 * jnp.exp(jnp.where(causal, Ej - e2, -1e30))
                    xj = x_dt[:, pc]
                    xbd = jnp.concatenate([xj * lo, xj * hi], axis=0).astype(BF16)
                    y_ref[rows, pc] += _dot(Mp.astype(BF16), xbd)
                ht_ref[:, gc] = jnp.exp(total[:, gc]) * htg + _dot_tn(Bg, xw[:, gc].astype(BF16))
            zb = zb_ref[rows, :].astype(F32)
            hh = y_ref[rows, :] * (zb * _sigmoid(zb))
            for g in range(SSD_GROUPS):
                gc = slice(g * GROUP_W, (g + 1) * GROUP_W)
                hg = hh[:, gc]
                r = lax.rsqrt(jnp.mean(hg * hg, axis=-1, keepdims=True) + EPS)
                yb_ref[rows, gc] = (hg * r * nw_ref[:, gc]).astype(BF16)

    full = lambda a: pl.BlockSpec(a.shape, lambda i: (0,) * a.ndim)
    hb = T // HALO_BLK
    return pl.pallas_call(
        body, name="ssd_fwd", grid=(nsteps,),
        in_specs=[pl.BlockSpec((T, D), lambda i: (i, OFF_ZB // D)),
                  pl.BlockSpec((T, XBC_W), lambda i: (i, OFF_XBC // XBC_W)),
                  pl.BlockSpec((HALO_BLK, XBC_W), lambda i: (jnp.maximum(i * hb - 1, 0), OFF_XBC // XBC_W)),
                  pl.BlockSpec((T, DT_W), lambda i: (i, OFF_DT // DT_W)),
                  full(conv_w), full(conv_b), full(dtb_p), full(alog_p), full(d_exp), full(norm_w), full(tri), full(expand),
                  full(shift)],
        out_specs=[pl.BlockSpec((T, D), lambda i: (i, 0)), pl.BlockSpec((T, D), lambda i: (i, 0)),
                   pl.BlockSpec((ncl, STATE, D), lambda i: (i, 0, 0))],
        out_shape=[jax.ShapeDtypeStruct((S, D), F32), jax.ShapeDtypeStruct((S, D), BF16),
                   jax.ShapeDtypeStruct((S // CHUNK, STATE, D), F32)],
        scratch_shapes=[pltpu.VMEM((STATE, D), F32), pltpu.VMEM((HALO_BLK + T, XBC_W), BF16)],
        compiler_params=_cp(("arbitrary",)),
    )(proj, proj, proj, proj, conv_w, conv_b, dtb_p, alog_p, d_exp, norm_w, tri, expand, shift)


def _ssd_bwd(proj, dyb, y, states, conv_w, conv_b, dtb_p, alog_p, d_exp, norm_w, tri, triT, expand, expandT, shift):
    S = proj.shape[0]
    T = SSD_T
    nsteps = S // T
    ncl = T // CHUNK
    SSD_W = SSD_PAD_W

    def body(zb_ref, xbc_ref, halo_ref, dt_ref, dyb_ref, y_ref, st_ref, cw_ref, cb_ref, dtb_ref, alog_ref, dexp_ref, nw_ref,
             tri_ref, triT_ref, exp_ref, expT_ref, shift_ref,
             dp_ref, dcw_ref, dcb_ref, ddtb_ref, dalog_ref, dD_ref, dnw_ref,
             dht_ref, ext_ref, dpre_ref, dy_s, dE_s, dxdt_s, dxc_s, dDacc_ref, dAacc_ref):
        i = pl.program_id(0)

        @pl.when(i == 0)
        def _():
            for r in (dht_ref, dcw_ref, dcb_ref, ddtb_ref, dnw_ref, dDacc_ref, dAacc_ref):
                r[...] = jnp.zeros_like(r)
            dpre_ref[T:T + HALO_BLK, :] = jnp.zeros((HALO_BLK, XBC_W), F32)

        @pl.when(i == nsteps - 1)
        def _():
            ext_ref[0:HALO_BLK, :] = jnp.zeros((HALO_BLK, XBC_W), BF16)

        @pl.when(i < nsteps - 1)
        def _():
            ext_ref[0:HALO_BLK, :] = halo_ref[...]

        ext_ref[HALO_BLK:HALO_BLK + T, :] = xbc_ref[...]
        diag, causal, lo, hi = _pair_masks()
        last_row = (lax.broadcasted_iota(jnp.int32, (CHUNK, 1), 0) == CHUNK - 1).astype(F32)
        for c in reversed(range(ncl)):
            q = _ssd_chunk_fwd(c, ext_ref, shift_ref, dt_ref, cw_ref, cb_ref, dtb_ref, alog_ref, tri_ref, exp_ref)
            rows = pl.ds(c * CHUNK, CHUNK)
            pre, sg, xc, dtr, dtv, A, E, dtE = (q[k] for k in ("pre", "sg", "xc", "dtr", "dtv", "A", "E", "dtE"))
            xs = xc[:, 0:D]
            total = E[CHUNK - 1:CHUNK, :]
            x_dt = xs * dtE
            eE = jnp.exp(E)
            wdec = jnp.exp(total - E)
            zb = zb_ref[rows, :].astype(F32)
            yv = y_ref[rows, :]
            sgz = _sigmoid(zb)
            sz = zb * sgz
            hh = yv * sz
            for g in range(SSD_GROUPS):
                gc = slice(g * GROUP_W, (g + 1) * GROUP_W)
                hg = hh[:, gc]
                r = lax.rsqrt(jnp.mean(hg * hg, axis=-1, keepdims=True) + EPS)
                dyb_g = dyb_ref[rows, gc].astype(F32)
                dn = dyb_g * nw_ref[:, gc]
                dnw_ref[0:1, gc] += jnp.sum(dyb_g * hg * r, axis=0, keepdims=True)
                dy_s[:, gc] = r * dn - hg * (r * r * r) * jnp.mean(dn * hg, axis=-1, keepdims=True)
            dhh = dy_s[...]
            dp_ref[rows, 0:D] = (dhh * yv * (sgz * (1.0 + zb * (1.0 - sgz)))).astype(BF16)
            dy = dhh * sz
            dy_s[...] = dy
            dDacc_ref[0:1, :] += jnp.sum(dy * xs, axis=0, keepdims=True)
            dxc_s[:, 0:D] = dy * dexp_ref[...]
            for g in range(SSD_GROUPS):
                gc = slice(g * GROUP_W, (g + 1) * GROUP_W)
                bcol = slice(D + g * STATE, D + (g + 1) * STATE)
                ccol = slice(D + SSD_GROUPS * STATE + g * STATE, D + SSD_GROUPS * STATE + (g + 1) * STATE)
                Bg = xc[:, bcol].astype(BF16)
                Cg = xc[:, ccol].astype(BF16)
                B2 = jnp.concatenate([Bg, Bg], axis=0)
                cb2 = _dot_nt(Cg, B2)
                htg = st_ref[c, :, gc]
                htb = htg.astype(BF16)
                dhn = dht_ref[:, gc]
                dhnb = dhn.astype(BF16)
                dyg = dy[:, gc]
                eEg = eE[:, gc]
                wg = wdec[:, gc]
                xdg = x_dt[:, gc]
                CH = _dot(Cg, htb)
                dCHb = (dyg * eEg).astype(BF16)
                dC = _dot_nt(dCHb, htb)
                dl = jnp.exp(total[:, gc])
                dht_prev = _dot_tn(Cg, dCHb) + dl * dhn
                dtot = jnp.sum(dhn * htg, axis=0, keepdims=True) * dl
                dxw = _dot(Bg, dhnb)
                dB = _dot_nt((xdg * wg).astype(BF16), dhnb)
                dwd = dxw * xdg * wg
                dtot = dtot + jnp.sum(dwd, axis=0, keepdims=True)
                dE_s[:, gc] = dyg * eEg * CH - dwd + last_row * dtot
                dxdt_s[:, gc] = dxw * wg
                dcb2 = jnp.zeros((CHUNK, LANE), F32)
                for jj in range(GROUP_W // LANE):
                    pc = slice(g * GROUP_W + jj * LANE, g * GROUP_W + (jj + 1) * LANE)
                    Ej = E[:, pc]
                    e2 = jnp.sum(Ej * diag, axis=0, keepdims=True)
                    Lp = jnp.exp(jnp.where(causal, Ej - e2, -1e30))
                    Mp = cb2 * Lp
                    xj = x_dt[:, pc]
                    xbd = jnp.concatenate([xj * lo, xj * hi], axis=0).astype(BF16)
                    dyj = dy[:, pc].astype(BF16)
                    dMp = _dot_nt(dyj, xbd)
                    dxbd = _dot_tn(Mp.astype(BF16), dyj)
                    dxdt_s[:, pc] += dxbd[0:CHUNK, :] * lo + dxbd[CHUNK:2 * CHUNK, :] * hi
                    dcb2 = dcb2 + dMp * Lp
                    dseg = dMp * Mp
                    dE_s[:, pc] += dseg - diag * jnp.sum(dseg, axis=0, keepdims=True)
                dcb2b = dcb2.astype(BF16)
                dC = dC + _dot(dcb2b, B2)
                dB2 = _dot_tn(dcb2b, Cg)
                dB = dB + dB2[0:CHUNK, :] + dB2[CHUNK:2 * CHUNK, :]
                dxc_s[:, bcol] = dB
                dxc_s[:, ccol] = dC
                dht_ref[:, gc] = dht_prev
            dx_dt = dxdt_s[...]
            dxc_s[:, 0:D] += dx_dt * dtE
            red = _sel_right(jnp.concatenate([dE_s[...], dx_dt * xs], axis=0), expT_ref[...])
            da = _sel_left(triT_ref[...], red[0:CHUNK, :])
            ddtv = red[CHUNK:2 * CHUNK, :] + da * A
            dAacc_ref[0:1, :] += jnp.sum(da * dtv, axis=0, keepdims=True)
            ddtr = ddtv * _sigmoid(dtr)
            ddtb_ref[0:1, :] += jnp.sum(ddtr, axis=0, keepdims=True)
            dp_ref[rows, D + XBC_W:D + XBC_W + DT_W] = ddtr.astype(BF16)
            dpre = dxc_s[...] * (sg * (1.0 + pre * (1.0 - sg)))
            dpre_ref[rows, :] = dpre
            dcb_ref[0:1, :] += jnp.sum(dpre, axis=0, keepdims=True)
            for k in range(CONV_K):
                dcw_ref[k:k + 1, :] += jnp.sum(dpre * q["taps"][k], axis=0, keepdims=True)
        dxbc = jnp.zeros((T, XBC_W), F32)
        for k in range(CONV_K):
            dxbc = dxbc + cw_ref[k:k + 1, :] * dpre_ref[pl.ds(CONV_K - 1 - k, T), :]
        dp_ref[:, D:D + XBC_W] = dxbc.astype(BF16)
        dp_ref[:, SEG_SSD[1]:SSD_W] = jnp.zeros((T, SSD_W - SEG_SSD[1]), BF16)
        dpre_ref[T:T + HALO, :] = dpre_ref[0:HALO, :]

        @pl.when(i == nsteps - 1)
        def _():
            dalog_ref[...] = dAacc_ref[...] * (-jnp.exp(alog_ref[...]))
            dD_ref[...] = _dot(dDacc_ref[...], expT_ref[...].astype(F32), precision=HI)

    full = lambda a: pl.BlockSpec(a.shape, lambda i: (0,) * a.ndim)
    hb = T // HALO_BLK
    rev = lambda i: nsteps - 1 - i
    acc = lambda w: pl.BlockSpec((8, w), lambda i: (0, 0))
    return pl.pallas_call(
        body, name="ssd_bwd", grid=(nsteps,),
        in_specs=[pl.BlockSpec((T, D), lambda i: (rev(i), OFF_ZB // D)),
                  pl.BlockSpec((T, XBC_W), lambda i: (rev(i), OFF_XBC // XBC_W)),
                  pl.BlockSpec((HALO_BLK, XBC_W), lambda i: (jnp.maximum(rev(i) * hb - 1, 0), OFF_XBC // XBC_W)),
                  pl.BlockSpec((T, DT_W), lambda i: (rev(i), OFF_DT // DT_W)),
                  pl.BlockSpec((T, D), lambda i: (rev(i), 0)), pl.BlockSpec((T, D), lambda i: (rev(i), 0)),
                  pl.BlockSpec((ncl, STATE, D), lambda i: (rev(i), 0, 0)),
                  full(conv_w), full(conv_b), full(dtb_p), full(alog_p), full(d_exp), full(norm_w),
                  full(tri), full(triT), full(expand), full(expandT), full(shift)],
        out_specs=[pl.BlockSpec((T, SSD_W), lambda i: (rev(i), 0)),
                   acc(XBC_W), acc(XBC_W), acc(DT_W), acc(DT_W), acc(DT_W), acc(D)],
        out_shape=[jax.ShapeDtypeStruct((S, SSD_W), BF16),
                   jax.ShapeDtypeStruct((8, XBC_W), F32), jax.ShapeDtypeStruct((8, XBC_W), F32),
                   jax.ShapeDtypeStruct((8, DT_W), F32), jax.ShapeDtypeStruct((8, DT_W), F32),
                   jax.ShapeDtypeStruct((8, DT_W), F32), jax.ShapeDtypeStruct((8, D), F32)],
        scratch_shapes=[pltpu.VMEM((STATE, D), F32), pltpu.VMEM((HALO_BLK + T, XBC_W), BF16), pltpu.VMEM((T + HALO_BLK, XBC_W), F32),
                        pltpu.VMEM((CHUNK, D), F32), pltpu.VMEM((CHUNK, D), F32), pltpu.VMEM((CHUNK, D), F32),
                        pltpu.VMEM((CHUNK, XBC_W), F32), pltpu.VMEM((8, D), F32), pltpu.VMEM((8, DT_W), F32)],
        compiler_params=_cp(("arbitrary",)),
    )(proj, proj, proj, proj, dyb, y, states, conv_w, conv_b, dtb_p, alog_p, d_exp, norm_w, tri, triT, expand, expandT, shift)


def _head(x, ya, yb, proj, target, gate_b, wout, fw, *, tm):
    S = x.shape[0]

    def body(x_ref, ya_ref, yb_ref, gl0_ref, gl1_ref, t_ref, gb_ref, w_ref, fw_ref,
             dh_ref, dhb_ref, mb_ref, dya_ref, dyb_ref, dgl_ref, loss_ref, dfw_ref, dgb_ref):
        @pl.when(pl.program_id(0) == 0)
        def _():
            loss_ref[...] = jnp.zeros_like(loss_ref)
            dfw_ref[...] = jnp.zeros_like(dfw_ref)
            dgb_ref[...] = jnp.zeros_like(dgb_ref)

        ya_v = ya_ref[...].astype(F32)
        yb_v = yb_ref[...].astype(F32)
        g0 = _sigmoid(gl0_ref[...].astype(F32) + gb_ref[:, 0:D])
        g1 = _sigmoid(gl1_ref[...].astype(F32) + gb_ref[:, D:2 * D])
        mb = (g0 * ya_v + g1 * yb_v).astype(BF16)
        mb_ref[...] = mb
        h = x_ref[...] + _dot(mb, w_ref[...])
        r = lax.rsqrt(jnp.mean(h * h, axis=-1, keepdims=True) + EPS)
        hn = h * r
        err = hn * fw_ref[...] - t_ref[...]
        loss_ref[...] += 0.5 * jnp.sum(jnp.mean(err * err, axis=-1, keepdims=True))
        dyf = err * (1.0 / D)
        dfw_ref[0:1, :] += jnp.sum(dyf * hn, axis=0, keepdims=True)
        dhn = dyf * fw_ref[...]
        dh = r * (dhn - hn * jnp.mean(dhn * hn, axis=-1, keepdims=True))
        dh_ref[...] = dh
        dhb = dh.astype(BF16)
        dhb_ref[...] = dhb
        dm = _dot_nt(dhb, w_ref[...])
        dya_ref[...] = (dm * g0).astype(BF16)
        dyb_ref[...] = (dm * g1).astype(BF16)
        dgl0 = dm * ya_v * g0 * (1.0 - g0)
        dgl1 = dm * yb_v * g1 * (1.0 - g1)
        dgl_ref[:, 0:D] = dgl0.astype(BF16)
        dgl_ref[:, D:2 * D] = dgl1.astype(BF16)
        dgb_ref[0:1, 0:D] += jnp.sum(dgl0, axis=0, keepdims=True)
        dgb_ref[0:1, D:2 * D] += jnp.sum(dgl1, axis=0, keepdims=True)

    row = pl.BlockSpec((tm, D), lambda i: (i, 0))
    seg = lambda off: pl.BlockSpec((tm, D), lambda i: (i, off // D))
    full = lambda a: pl.BlockSpec(a.shape, lambda i: (0,) * a.ndim)
    acc = lambda w: pl.BlockSpec((8, w), lambda i: (0, 0))
    return pl.pallas_call(
        body, name="head", grid=(S // tm,),
        in_specs=[row, row, row, seg(OFF_G0), seg(OFF_G1), row, full(gate_b), full(wout), full(fw)],
        out_specs=[row, row, row, row, row, pl.BlockSpec((tm, 2 * D), lambda i: (i, 0)), acc(LANE), acc(D), acc(2 * D)],
        out_shape=[jax.ShapeDtypeStruct((S, D), F32), jax.ShapeDtypeStruct((S, D), BF16), jax.ShapeDtypeStruct((S, D), BF16),
                   jax.ShapeDtypeStruct((S, D), BF16), jax.ShapeDtypeStruct((S, D), BF16), jax.ShapeDtypeStruct((S, 2 * D), BF16),
                   jax.ShapeDtypeStruct((8, LANE), F32), jax.ShapeDtypeStruct((8, D), F32), jax.ShapeDtypeStruct((8, 2 * D), F32)],
        compiler_params=_cp(("arbitrary",)),
    )(x, ya, yb, proj, proj, target, gate_b, wout, fw)


def _adam_update(g, w_ref, m_ref, v_ref, g_ref, d_ref, m2_ref, v2_ref):
    m2 = ADAM_B1 * m_ref[...] + (1.0 - ADAM_B1) * g
    v2 = ADAM_B2 * v_ref[...] + (1.0 - ADAM_B2) * (g * g)
    m_hat = m2 / (1.0 - ADAM_B1 ** ADAM_STEP)
    v_hat = v2 / (1.0 - ADAM_B2 ** ADAM_STEP)
    g_ref[...] = g
    d_ref[...] = -ADAM_LR * (m_hat / (jnp.sqrt(v_hat) + ADAM_EPS) + ADAM_WD * w_ref[...])
    m2_ref[...] = m2
    v2_ref[...] = v2


def _adamw_own(me, own, landed, w, m, v, *, tr, tc, name):
    _, R, C = landed.shape
    assert R % tr == 0 and C % tc == 0, (name, R, C, tr, tc)

    def body(me_ref, own_ref, p_ref, w_ref, m_ref, v_ref, g_ref, d_ref, m2_ref, v2_ref):
        mine = own_ref[0].astype(F32)
        g = jnp.where(me_ref[0] == 0, mine, p_ref[0].astype(F32))
        for k in range(1, N_DEV):
            g = g + jnp.where(me_ref[0] == k, mine, p_ref[k].astype(F32))
        _adam_update(g, w_ref, m_ref, v_ref, g_ref, d_ref, m2_ref, v2_ref)

    tile = pl.BlockSpec((tr, tc), lambda i, j, me_ref: (i, j))
    return pl.pallas_call(
        body, name=name,
        grid_spec=pltpu.PrefetchScalarGridSpec(
            num_scalar_prefetch=1, grid=(R // tr, C // tc),
            in_specs=[pl.BlockSpec((1, tr, tc), lambda i, j, me_ref: (me_ref[0], i, j)),
                      pl.BlockSpec((N_DEV, tr, tc), lambda i, j, me_ref: (0, i, j)), tile, tile, tile],
            out_specs=[tile, tile, tile, tile]),
        out_shape=[jax.ShapeDtypeStruct((R, C), F32)] * 4,
        compiler_params=_cp(("parallel", "parallel")),
    )(me, own, landed, w, m, v)


def _adamw(parts, w, m, v, *, tr, name):
    _, R, C = parts.shape
    assert R % tr == 0, (name, R, tr)

    def body(p_ref, w_ref, m_ref, v_ref, g_ref, d_ref, m2_ref, v2_ref):
        g = p_ref[0].astype(F32)
        for k in range(1, N_DEV):
            g = g + p_ref[k].astype(F32)
        _adam_update(g, w_ref, m_ref, v_ref, g_ref, d_ref, m2_ref, v2_ref)

    row = pl.BlockSpec((tr, C), lambda i: (i, 0))
    return pl.pallas_call(
        body, name=name, grid=(R // tr,),
        in_specs=[pl.BlockSpec((N_DEV, tr, C), lambda i: (0, i, 0)), row, row, row],
        out_specs=[row, row, row, row],
        out_shape=[jax.ShapeDtypeStruct((R, C), F32)] * 4,
        compiler_params=_cp(("parallel",)),
    )(parts, w, m, v)


def _place():
    x, y, c = lax.axis_index("x"), lax.axis_index("y"), lax.axis_index("c")
    return x, y, c


def _all_gather(arrs, *, name):
    n = len(arrs)

    def body(*refs):
        ins, outs = refs[:n], refs[n:2 * n]
        send_sems, recv_sems, local_sems = refs[2 * n:]
        x, y, c = _place()
        me, sibling = (x, y, c), (x, y, 1 - c)
        chips = [(1 - x, y), (x, 1 - y), (1 - x, 1 - y)]

        def idx(px, py, pc):
            return 4 * px + 2 * py + pc

        def copy(k, a, block, to, src=None):
            slab = outs[a].at[idx(*block)]
            return pltpu.make_async_remote_copy(
                src_ref=slab if src is None else src, dst_ref=slab,
                send_sem=send_sems.at[k, a], recv_sem=recv_sems.at[k, a], device_id=to, device_id_type=MESH)

        mine = [pltpu.make_async_copy(ins[a], outs[a].at[idx(*me)], local_sems.at[a]) for a in range(n)]
        for cp in mine:
            cp.start()
        first = []
        for a in range(n):
            first.append(copy(0, a, me, sibling, src=ins[a]))
            first += [copy(1 + j, a, me, (*chip, c), src=ins[a]) for j, chip in enumerate(chips)]
        for cp in first:
            cp.start()
        passed = []
        for j, chip in enumerate(chips):
            for a in range(n):
                copy(1 + j, a, (*chip, c), me).wait_recv()
                fwd = copy(4 + j, a, (*chip, c), sibling)
                fwd.start()
                passed.append(fwd)
        for a in range(n):
            copy(0, a, sibling, me).wait_recv()
            for j, chip in enumerate(chips):
                copy(4 + j, a, (*chip, 1 - c), me).wait_recv()
        for cp in first + passed:
            cp.wait_send()
        for cp in mine:
            cp.wait()

    anyspec = pl.BlockSpec(memory_space=pl.ANY)
    return pl.pallas_call(
        body, name=name,
        in_specs=[anyspec] * n, out_specs=[anyspec] * n,
        out_shape=[jax.ShapeDtypeStruct((N_DEV,) + a.shape, a.dtype) for a in arrs],
        scratch_shapes=[pltpu.SemaphoreType.DMA((7, n)), pltpu.SemaphoreType.DMA((7, n)), pltpu.SemaphoreType.DMA((n,))],
    )(*arrs)


W_ROWS = SEG_SSD[0] + SSD_PAD_W


GROUP = 16
INTERIOR = 1920


def _interior(k):
    lo = -(-(k * SHARD_IN) // GROUP) * GROUP
    hi = ((k + 1) * SHARD_IN) // GROUP * GROUP
    return lo, hi


def _dest_row(r):
    if r < REF_SGU_END:
        return r
    return r - REF_SGU_END + SEG_SSD[0] if r < REF_GATE_START else r - REF_GATE_START + SEG_GATE[0]


def _shard_pieces(k):
    lo_k, hi_k = _interior(k)
    out = []
    for lo, hi in ((0, REF_SGU_END), (REF_SGU_END, REF_GATE_START), (REF_GATE_START, W_IN)):
        a, b = max(lo, lo_k), min(hi, hi_k)
        if a < b:
            out.append((a - lo_k, b - a, _dest_row(a)))
    return out


GATHER_PARTS = 1


def _shard_parts(k):
    parts = [[] for _ in range(GATHER_PARTS)]
    for s0, n, d0 in _shard_pieces(k):
        step = -(-(n // GROUP) // GATHER_PARTS) * GROUP
        for p in range(GATHER_PARTS):
            a, b = min(p * step, n), min((p + 1) * step, n)
            if a < b:
                parts[p].append((s0 + a, b - a, d0 + a))
    return parts


def _patch_straddlers(wpT, heads, tails):
    todo = [(k, (k * SHARD_IN) % GROUP) for k in range(1, N_DEV) if (k * SHARD_IN) % GROUP]

    def body(w_old, h_ref, t_ref, w_hbm, gbuf, sem):
        del w_old
        row = lax.broadcasted_iota(jnp.int32, (GROUP, GROUP), 0)
        col = lax.broadcasted_iota(jnp.int32, (GROUP, GROUP), 1)
        copies = []
        for j, (k, m) in enumerate(todo):
            from_tail = jnp.where((row < m) & (col == row + (GROUP - m)), 1.0, 0.0).astype(BF16)
            from_head = jnp.where((row >= m) & (col == row - m), 1.0, 0.0).astype(BF16)
            gbuf[j] = (_dot(from_tail, t_ref[k - 1]) + _dot(from_head, h_ref[k])).astype(BF16)
            copies.append(pltpu.make_async_copy(
                gbuf.at[j], w_hbm.at[pl.ds(_dest_row(k * SHARD_IN - m), GROUP), :], sem.at[j]))
            copies[-1].start()
        for c in copies:
            c.wait()

    vmem = pl.BlockSpec(memory_space=pltpu.VMEM)
    anyspec = pl.BlockSpec(memory_space=pl.ANY)
    return pl.pallas_call(
        body, name="patch_straddlers", in_specs=[anyspec, vmem, vmem], out_specs=anyspec,
        out_shape=jax.ShapeDtypeStruct(wpT.shape, wpT.dtype), input_output_aliases={0: 0},
        scratch_shapes=[pltpu.VMEM((len(todo), GROUP, D), BF16), pltpu.SemaphoreType.DMA((len(todo),))],
        compiler_params=_cp(),
    )(wpT, heads, tails)


def _gather_stages(k, win_ref, small, z_ref, n_zero, w_ref, send_sems, recv_sems, local_sems):
    x, y, c = k // 4, (k // 2) % 2, k % 2
    idx = lambda p: 4 * p[0] + 2 * p[1] + p[2]
    me, sib = (x, y, c), (x, y, 1 - c)
    xn, yn, dg = (1 - x, y, c), (x, 1 - y, c), (1 - x, 1 - y, c)
    parts = range(GATHER_PARTS)

    def copies(slot, block, to, part, own=False):
        kb = idx(block)
        out = []
        for j, (s0, n, d0) in enumerate(_shard_parts(kb)[part]):
            dst = w_ref.at[pl.ds(d0, n)]
            out.append((win_ref.at[pl.ds(s0, n)] if own else dst, dst, 2 * part + j))
        if part == 0:
            for j, (src, gathered) in enumerate(small):
                out.append((src if own else gathered.at[kb], gathered.at[kb], 2 * GATHER_PARTS + j))
        return [pltpu.make_async_remote_copy(src_ref=s, dst_ref=d, send_sem=send_sems.at[slot, j], recv_sem=recv_sems.at[slot, j],
                                             device_id=to, device_id_type=MESH) for s, d, j in out]

    def start(cps):
        for cp in cps:
            cp.start()

    def arrived(slot, block, part):
        for cp in copies(slot, block, me, part):
            cp.wait_recv()

    def local():
        pairs = [(win_ref.at[pl.ds(s0, n)], w_ref.at[pl.ds(d0, n)]) for s0, n, d0 in _shard_pieces(k)]
        pairs += [(src, gathered.at[k]) for src, gathered in small] + [(z_ref, w_ref.at[pl.ds(W_IN, n_zero)])]
        return [pltpu.make_async_copy(s, d, local_sems.at[j]) for j, (s, d) in enumerate(pairs)]

    relay = (xn, yn) if c == 1 else (yn, xn)

    def first():
        start(local())
        for p in parts:
            start(copies(0, me, sib, p, own=True) + copies(1, me, xn, p, own=True) + copies(2, me, yn, p, own=True))

    def hand_on():
        for p in parts:
            arrived(1, xn, p)
            start(copies(4, xn, sib, p))
            if c == 1:
                start(copies(3, *relay, p))
            arrived(2, yn, p)
            start(copies(5, yn, sib, p))
            if c == 0:
                start(copies(3, *relay, p))

    def finish():
        for p in parts:
            arrived(3, dg, p)
            start(copies(6, dg, sib, p))
        for p in parts:
            arrived(0, sib, p)
            arrived(4, (1 - x, y, 1 - c), p)
            arrived(5, (x, 1 - y, 1 - c), p)
            arrived(6, (1 - x, 1 - y, 1 - c), p)
        for p in parts:
            sent = (copies(0, me, sib, p, own=True) + copies(1, me, xn, p, own=True) + copies(2, me, yn, p, own=True)
                    + copies(3, *relay, p) + copies(4, xn, sib, p) + copies(5, yn, sib, p) + copies(6, dg, sib, p))
            for cp in sent:
                cp.wait_send()
        for cp in local():
            cp.wait()

    return first, hand_on, finish


def _gather_sems(n_small):
    n_arr = 2 * GATHER_PARTS + n_small
    return [pltpu.SemaphoreType.DMA((7, n_arr)), pltpu.SemaphoreType.DMA((7, n_arr)), pltpu.SemaphoreType.DMA((n_arr + 1,))]


def _gather_weights(win, head, tail, wout, cw, zeros):
    small_in = (wout, cw, head, tail)
    n_zero = zeros.shape[0]
    assert W_IN + n_zero == W_ROWS and W_IN % GROUP == 0

    def body(win_ref, wout_ref, cw_ref, head_ref, tail_ref, z_ref, w_ref, gout_ref, gcw_ref, ghead_ref, gtail_ref, *sems):
        x, y, c = _place()
        me = 4 * x + 2 * y + c
        small = ((wout_ref, gout_ref), (cw_ref, gcw_ref), (head_ref, ghead_ref), (tail_ref, gtail_ref))

        def run(k):
            for stage in _gather_stages(k, win_ref, small, z_ref, n_zero, w_ref, *sems):
                stage()

        for k in range(N_DEV):
            pl.when(me == k)(functools.partial(run, k))

    anyspec = pl.BlockSpec(memory_space=pl.ANY)
    return pl.pallas_call(
        body, name="gather_weights", in_specs=[anyspec] * 6, out_specs=[anyspec] * 5,
        out_shape=[jax.ShapeDtypeStruct((W_ROWS, D), win.dtype)]
        + [jax.ShapeDtypeStruct((N_DEV,) + a.shape, a.dtype) for a in small_in],
        scratch_shapes=_gather_sems(len(small_in)),
    )(win, wout, cw, head, tail, zeros)


_REL = [(dx, dy, dc) for dx in (0, 1) for dy in (0, 1) for dc in (0, 1)][1:]
_HBM = pl.BlockSpec(memory_space=pltpu.HBM)
_SEM = pl.BlockSpec(memory_space=pltpu.SEMAPHORE)
_EFFECT = pltpu.SideEffectType.DATAFLOW_SIDE_EFFECTING


def _peer(k):
    x, y, c = _place()
    dx, dy, dc = _REL[k]
    return (1 - x if dx else x, 1 - y if dy else y, 1 - c if dc else c)


def _exchange_start(parts, *, name):
    n = len(parts)

    def body(*refs):
        ins, lands = refs[:n], refs[n:2 * n]
        send_sems, recv_sems, token = refs[2 * n], refs[2 * n + 1], refs[-1]
        x, y, c = _place()
        me = 4 * x + 2 * y + c
        for a in range(n):
            for k in range(len(_REL)):
                px, py, pc = _peer(k)
                pltpu.make_async_remote_copy(
                    src_ref=ins[a].at[4 * px + 2 * py + pc], dst_ref=lands[a].at[me],
                    send_sem=send_sems.at[len(_REL) * a + k], recv_sem=recv_sems.at[len(_REL) * a + k],
                    device_id=(px, py, pc), device_id_type=MESH).start()
        token[...] = jnp.zeros_like(token)

    sem = pltpu.SemaphoreType.DMA((len(_REL) * n,))
    bufs = [pltpu.HBM(p.shape, p.dtype) for p in parts]
    outs = pl.pallas_call(
        body, name=name,
        out_shape=(sem, sem, *bufs, *bufs, jax.ShapeDtypeStruct((8, LANE), F32)),
        in_specs=(_HBM,) * (2 * n), out_specs=(_SEM, _SEM, *(_HBM,) * (2 * n), pl.BlockSpec(memory_space=pltpu.VMEM)),
        input_output_aliases={i: 2 + i for i in range(2 * n)},
        compiler_params=pltpu.CompilerParams(has_side_effects=_EFFECT),
    )(*[pltpu.with_memory_space_constraint(p, pltpu.HBM) for p in parts],
      *[pltpu.with_memory_space_constraint(lax.empty(p.shape, p.dtype), pltpu.HBM) for p in parts])
    return outs[0], outs[1], outs[2:2 + n], outs[2 + n:2 + 2 * n], outs[-1]


def _exchange_wait(send_sems, recv_sems, parts, lands, after, *, name):
    n = len(parts)

    def body(*refs):
        ins, lands_ = refs[:n], refs[n:2 * n]
        ssem, rsem = refs[2 * n], refs[2 * n + 1]
        for a in range(n):
            for k in range(len(_REL)):
                px, py, pc = _peer(k)
                p = 4 * px + 2 * py + pc
                cp = pltpu.make_async_remote_copy(
                    src_ref=ins[a].at[p], dst_ref=lands_[a].at[p],
                    send_sem=ssem.at[len(_REL) * a + k], recv_sem=rsem.at[len(_REL) * a + k],
                    device_id=(px, py, pc), device_id_type=MESH)
                cp.wait_send()
                cp.wait_recv()

    bufs = [pltpu.HBM(p.shape, p.dtype) for p in parts]
    outs = pl.pallas_call(
        body, name=name, out_shape=(*bufs, *bufs),
        in_specs=(*(_HBM,) * (2 * n), _SEM, _SEM, pl.BlockSpec(memory_space=pl.ANY)), out_specs=(_HBM,) * (2 * n),
        input_output_aliases={i: i for i in range(2 * n)},
        compiler_params=pltpu.CompilerParams(has_side_effects=_EFFECT),
    )(*parts, *lands, send_sems, recv_sems, after)
    return outs[:n], outs[n:]


WEIGHTS = ('norm_w', 'w_in', 'gate_b', 'sgu_norm_g', 'sgu_norm_b', 'sgu_w', 'sgu_b', 'conv_w', 'conv_b', 'dt_bias', 'A_log',
           'D_skip', 'ssd_norm_w', 'w_out', 'final_norm_w')
SHARDED = ('w_in', 'conv_w', 'w_out')
PACK_ROW = 8 * LANE


def _constants():
    tri = np.tril(np.ones((CHUNK, CHUNK), np.float32))
    expand = np.zeros((DT_W, D), np.float32)
    for h in range(HEADS):
        expand[h, h * HEADDIM:(h + 1) * HEADDIM] = 1.0
    sel = np.zeros((D, LANE), np.float32)
    for g in range(SGU_GROUPS):
        sel[g * LANE:(g + 1) * LANE, g] = 1.0
    pos_chunk = np.arange(SGU_BLOCK) // CHUNK
    mask = (pos_chunk[None, :] <= pos_chunk[:, None]).astype(np.float32)
    shift = np.zeros(((CONV_K - 1) * CHUNK, HALO_BLK + CHUNK), np.float32)
    for kk in range(CONV_K - 1):
        for t in range(CHUNK):
            shift[kk * CHUNK + t, HALO_BLK - (CONV_K - 1) + t + kk] = 1.0
    return dict(tri=jnp.asarray(tri, BF16), triT=jnp.asarray(tri.T.copy(), BF16), expand=jnp.asarray(np.tile(expand, (3, 1)), BF16),
                shift=jnp.asarray(shift, BF16),
                expandT=jnp.asarray(expand.T.copy(), BF16), sel=jnp.asarray(sel), mask=jnp.asarray(mask))


def _to_shards(segs):
    starts = np.cumsum([0] + [n for _, n in segs])
    assert starts[-1] == W_IN
    return jnp.concatenate([s[:n] for s, n in segs], axis=0).reshape(N_DEV, SHARD_IN, D)


def _local_step(x2, tgt, wpT, wout, cw, p, exchange_small, exchange):
    S = x2.shape[0]
    k = _constants()
    xn, proj = _in_proj(x2, p['norm_w'], wpT, tm=min(1024, S), tn=2048)
    wm32 = p['sgu_w'][0] * k['mask']
    wm = wm32.astype(BF16)
    wmT = jnp.swapaxes(wm32, 1, 2).astype(BF16)
    bias_full = jnp.repeat(p['sgu_b'][0].T, LANE, axis=1)
    tm_sgu = min(512, S)
    ya = _sgu_fwd(proj, p['sgu_norm_g'], p['sgu_norm_b'], wm, bias_full, tm=tm_sgu)
    pad32 = lambda a: jnp.pad(a, ((0, 0), (0, DT_W - HEADS)))
    dtb_p, alog_p = pad32(p['dt_bias']), pad32(p['A_log'])
    d_exp = jnp.repeat(p['D_skip'], HEADDIM, axis=1)
    ssd_args = (cw, p['conv_b'], dtb_p, alog_p, d_exp, p['ssd_norm_w'])
    y, yb, states = _ssd_fwd(proj, *ssd_args, k['tri'], k['expand'], k['shift'])
    dh, dhb, mb, dya, dyb, dgl, loss, dfw, dgb = _head(
        x2, ya, yb, proj, tgt, p['gate_b'], wout, p['final_norm_w'][None, :], tm=min(256, S))
    dsgu, dws, dbsT, dsg, dsb = _sgu_bwd(proj, dya, p['sgu_norm_g'], p['sgu_norm_b'], wm, wmT, bias_full, k['mask'], k['sel'],
                                         tm=tm_sgu)
    dssd, dcw, dcb, ddtb, dalog, dD, dnw = _ssd_bwd(proj, dyb, y, states, *ssd_args, k['tri'], k['triT'], k['expand'], k['expandT'],
                                                    k['shift'])
    grads = dict(
        gate_b=dgb[0:1], sgu_norm_g=dsg[0:1], sgu_norm_b=dsb[0:1], sgu_w=dws[None],
        sgu_b=dbsT[:, :SGU_GROUPS].T[None], conv_w=dcw[0:CONV_K][None], conv_b=dcb[0:1], dt_bias=ddtb[0:1, :HEADS],
        A_log=dalog[0:1, :HEADS], D_skip=dD[0:1, :HEADS], ssd_norm_w=dnw[0:1], final_norm_w=dfw[0])
    tw = dict(trans_a=True, out_dtype=BF16, tm=1024, tn=512, tk=S)
    dw_out = _matmul(mb, dhb, name="dw_out", **tw)
    token = exchange_small(loss[0, 0], grads, dw_out)
    dwT_sgu, dwT_gate, dwT_ssd = _dw_in([dsgu, dgl, dssd], xn, token, tm=256)
    token = exchange([(dwT_sgu, SEG_SGU[1]), (dwT_ssd, W_IN - SEG_SSD[0]), (dwT_gate, SEG_GATE[1])])
    tm, tn = min(1024, S), 1024
    dxn = _matmul(dsgu, wpT, tm=tm, tn=512, tk=SEG_SGU[1], after=token, name="dxn_sgu")
    dxn = _matmul(dgl, wpT, b_koff=SEG_GATE[0] // 2048, tm=tm, tn=tn, tk=2048, add=dxn, name="dxn_gate")
    grad_x, dnorm = _dxn_last_norm(dssd, wpT, SEG_SSD[0], dxn, x2, dh, p['norm_w'], tm=min(256, S))
    return grad_x, dnorm[0:1]


def _pack(arrs):
    rows, offs, r = [], [], 0
    for a in arrs:
        n = a.size
        nr = -(-n // PACK_ROW) * 8
        rows.append(jnp.pad(a.reshape(-1).astype(F32), (0, nr * LANE - n)).reshape(nr, LANE))
        offs.append(r)
        r += nr
    return jnp.concatenate(rows, axis=0), offs


def kernel(x, norm_w, w_in, gate_b, sgu_norm_g, sgu_norm_b, sgu_w, sgu_b, conv_w, conv_b, dt_bias, A_log, D_skip, ssd_norm_w, w_out, final_norm_w, loss_target, m_norm_w, m_w_in, m_gate_b, m_sgu_norm_g, m_sgu_norm_b, m_sgu_w, m_sgu_b, m_conv_w, m_conv_b, m_dt_bias, m_A_log, m_D_skip, m_ssd_norm_w, m_w_out, m_final_norm_w, v_norm_w, v_w_in, v_gate_b, v_sgu_norm_g, v_sgu_norm_b, v_sgu_w, v_sgu_b, v_conv_w, v_conv_b, v_dt_bias, v_A_log, v_D_skip, v_ssd_norm_w, v_w_out, v_final_norm_w):
    w = dict(norm_w=norm_w, w_in=w_in, gate_b=gate_b, sgu_norm_g=sgu_norm_g, sgu_norm_b=sgu_norm_b, sgu_w=sgu_w, sgu_b=sgu_b,
             conv_w=conv_w, conv_b=conv_b, dt_bias=dt_bias, A_log=A_log, D_skip=D_skip, ssd_norm_w=ssd_norm_w, w_out=w_out,
             final_norm_w=final_norm_w)
    m = dict(norm_w=m_norm_w, w_in=m_w_in, gate_b=m_gate_b, sgu_norm_g=m_sgu_norm_g, sgu_norm_b=m_sgu_norm_b, sgu_w=m_sgu_w,
             sgu_b=m_sgu_b, conv_w=m_conv_w, conv_b=m_conv_b, dt_bias=m_dt_bias, A_log=m_A_log, D_skip=m_D_skip,
             ssd_norm_w=m_ssd_norm_w, w_out=m_w_out, final_norm_w=m_final_norm_w)
    v = dict(norm_w=v_norm_w, w_in=v_w_in, gate_b=v_gate_b, sgu_norm_g=v_sgu_norm_g, sgu_norm_b=v_sgu_norm_b, sgu_w=v_sgu_w,
             sgu_b=v_sgu_b, conv_w=v_conv_w, conv_b=v_conv_b, dt_bias=v_dt_bias, A_log=v_A_log, D_skip=v_D_skip,
             ssd_norm_w=v_ssd_norm_w, w_out=v_w_out, final_norm_w=v_final_norm_w)
    me = 4 * lax.axis_index("x") + 2 * lax.axis_index("y") + lax.axis_index("c")
    shard_cw = XBC_W // N_DEV

    tpose = lambda a: jnp.swapaxes(a[0], 0, 1)
    wT = tpose(w_in).astype(BF16)
    first_group = (GROUP - (me * SHARD_IN) % GROUP) % GROUP
    window = lax.dynamic_slice(jnp.pad(wT, ((0, GROUP), (0, 0))), (first_group, 0), (INTERIOR, D))
    wpT, g_out, g_cw, heads, tails = _gather_weights(window, wT[:GROUP], wT[SHARD_IN - GROUP:], w_out[0].astype(BF16),
                                                     conv_w[0], jnp.zeros((W_ROWS - W_IN, D), BF16))
    wpT = _patch_straddlers(wpT, heads, tails)
    wout_full = g_out.reshape(D, D)
    cw_full = jnp.swapaxes(g_cw, 0, 1).reshape(CONV_K, XBC_W)

    flight = {}

    small = [n for n in WEIGHTS if n not in SHARDED and n != 'norm_w']
    early = {}

    def exchange_small(loss_part, grads, dw_out):
        early['packed'], early['offs'] = _pack([grads[n] for n in small] + [loss_part, grads['conv_w']])
        parts = [jnp.broadcast_to(early['packed'][None], (N_DEV,) + early['packed'].shape), dw_out.reshape(N_DEV, D // N_DEV, D)]
        early['sems'], early['rsems'], early['parts'], early['lands'], token = _exchange_start(parts, name="small_start")
        return token

    def exchange(dw_inT_segs):
        parts = [_to_shards(dw_inT_segs)]
        flight['sems'], flight['rsems'], flight['parts'], flight['lands'], token = _exchange_start(parts, name="exchange_start")
        return token

    grad_x, dnorm = _local_step(x[0], loss_target[0], wpT, wout_full, cw_full, w, exchange_small, exchange)
    (_, own_out), (land_small, land_out) = _exchange_wait(
        early['sems'], early['rsems'], early['parts'], early['lands'], grad_x, name="small_wait")
    (own_in,), (land_in,) = _exchange_wait(
        flight['sems'], flight['rsems'], flight['parts'], flight['lands'], grad_x, name="exchange_wait")
    me_arr = jnp.reshape(me, (1,)).astype(jnp.int32)
    res = {}
    res['w_in'] = [jnp.swapaxes(o, 0, 1) for o in _adamw_own(
        me_arr, own_in, land_in, tpose(w_in), tpose(m_w_in), tpose(v_w_in), tr=SHARD_IN, tc=256, name="adamw_w_in")]
    res['w_out'] = _adamw_own(me_arr, own_out, land_out, w_out[0], m_w_out[0], v_w_out[0], tr=128, tc=D, name="adamw_w_out")

    (norm_parts,) = _all_gather([_pack([dnorm])[0]], name="gather_norm")
    norm_outs = _adamw(norm_parts, *[_pack([d['norm_w']])[0] for d in (w, m, v)], tr=norm_parts.shape[1], name="adamw_norm")
    res['norm_w'] = [o.reshape(-1)[:D].reshape(w['norm_w'].shape) for o in norm_outs]

    offs = early['offs']
    gathered = lax.dynamic_update_slice(land_small, early['packed'][None], (me, 0, 0))
    off_loss, off_cw = offs[-2], offs[-1]
    cw_parts = gathered[:, off_cw:, :].reshape(N_DEV, CONV_K, XBC_W)
    cw_parts = lax.dynamic_slice_in_dim(cw_parts, me * shard_cw, shard_cw, axis=2)
    cw_rows = _pack([cw_parts[0]])[0].shape[0]
    cw_parts = jnp.pad(cw_parts.reshape(N_DEV, -1), ((0, 0), (0, cw_rows * LANE - CONV_K * shard_cw))).reshape(N_DEV, cw_rows, LANE)
    parts = jnp.concatenate([gathered[:, :off_cw, :], cw_parts], axis=1)
    zero = jnp.zeros((), F32)
    packs = [_pack([d[n] for n in small] + [zero, d['conv_w']])[0] for d in (w, m, v)]
    outs = _adamw(parts, *packs, tr=parts.shape[1], name="adamw_small")

    def unpack(o, name):
        if name == 'conv_w':
            return o[off_cw:off_cw + cw_rows].reshape(-1)[:CONV_K * shard_cw].reshape(w['conv_w'].shape)
        r0 = offs[small.index(name)]
        n = w[name].size
        return o[r0:r0 + -(-n // PACK_ROW) * 8].reshape(-1)[:n].reshape(w[name].shape)

    for n in small + ['conv_w']:
        res[n] = [unpack(o, n) for o in outs]
    for n in ('w_in', 'w_out'):
        res[n] = [o[None] for o in res[n]]
    loss = outs[0][off_loss, 0]
    return (loss, grad_x[None], *[res[n][0] for n in WEIGHTS], *[res[n][1] for n in WEIGHTS],
            *[res[n][2] for n in WEIGHTS], *[res[n][3] for n in WEIGHTS])
```

```python
import functools

import numpy as np
import jax
import jax.numpy as jnp
from jax import lax
from jax.experimental import pallas as pl
from jax.experimental.pallas import tpu as pltpu

F32 = jnp.float32
BF16 = jnp.bfloat16
HI = lax.Precision.HIGHEST
MESH = pl.DeviceIdType.MESH

D = 2048
EPS = 1e-5
SGU_BLOCK = 128
SGU_GROUPS = 16
CHUNK = 64
HEADS = 32
HEADDIM = 64
SSD_GROUPS = 4
GROUP_W = D // SSD_GROUPS
STATE = 128
CONV_K = 4
XBC_W = D + 2 * SSD_GROUPS * STATE
W_IN = 15392
N_DEV = 8
SHARD_IN = W_IN // N_DEV
ADAM_LR, ADAM_B1, ADAM_B2, ADAM_EPS, ADAM_WD, ADAM_STEP = 0.001, 0.9, 0.999, 1e-08, 0.01, 10

REF_SGU_END = 3 * D
REF_GATE_START = W_IN - 2 * D
LANE = 128
DT_W = LANE
OFF_U, OFF_V, OFF_ZA, OFF_G0, OFF_G1, OFF_ZB = (i * D for i in range(6))
OFF_XBC = OFF_ZB + D
OFF_DT = OFF_XBC + XBC_W
SEG_SGU = (OFF_U, 3 * D)
SEG_GATE = (OFF_G0, 2 * D)
SEG_SSD = (OFF_ZB, D + XBC_W + DT_W)
WP = SEG_SSD[0] + SEG_SSD[1]
SSD_PAD_W = 3 * D
VMEM_BYTES = 64 * 1024 * 1024
VMEM_LIMIT = VMEM_BYTES - 8 * 1024 * 1024


def _cp(sem=None, vmem=VMEM_LIMIT):
    return pltpu.CompilerParams(dimension_semantics=sem, vmem_limit_bytes=vmem)


def _sigmoid(x):
    return 1.0 / (1.0 + jnp.exp(-x))


def _softplus(x):
    return jnp.maximum(x, 0.0) + jnp.log(1.0 + jnp.exp(-jnp.abs(x)))


def _dot(a, b, precision=None):
    return jnp.dot(a, b, preferred_element_type=F32, precision=precision)


def _dot_nt(a, b, precision=None):
    return lax.dot_general(a, b, (((1,), (1,)), ((), ())), preferred_element_type=F32, precision=precision)


def _dot_tn(a, b, precision=None):
    return lax.dot_general(a, b, (((0,), (0,)), ((), ())), preferred_element_type=F32, precision=precision)


def _split3(a):
    hi = a.astype(BF16)
    r = a - hi.astype(F32)
    mid = r.astype(BF16)
    return hi, mid, (r - mid.astype(F32)).astype(BF16)


def _sel_right(a, sel01):
    m = a.shape[0]
    r = _dot(jnp.concatenate(_split3(a), axis=0), sel01)
    return (r[0:m] + r[m:2 * m]) + r[2 * m:3 * m]


def _sel_right_k(a, sel01_x3):
    return _dot(jnp.concatenate(_split3(a), axis=1), sel01_x3)


def _sel_left(sel01, a):
    n = a.shape[1]
    r = _dot(sel01, jnp.concatenate(_split3(a), axis=1))
    return (r[:, 0:n] + r[:, n:2 * n]) + r[:, 2 * n:3 * n]


def _matmul(a, b, *, trans_a=False, trans_b=False, b_koff=0, out_dtype=F32, tm, tn, tk, add=None, after=None, name):
    K, M = a.shape if trans_a else a.shape[::-1]
    N = b.shape[0] if trans_b else b.shape[1]
    assert M % tm == 0 and N % tn == 0 and K % tk == 0 and not (trans_a and trans_b), (name, M, N, K, tm, tn, tk)
    nk = K // tk

    def body(*refs):
        a_ref, b_ref = refs[:2]
        add_ref = refs[2] if add is not None else None
        o_ref, acc_ref = refs[-2:]
        k = pl.program_id(2)
        if trans_a:
            part = _dot_tn(a_ref[...], b_ref[...])
        else:
            part = _dot_nt(a_ref[...], b_ref[...]) if trans_b else _dot(a_ref[...], b_ref[...])

        def result(r):
            if add_ref is not None:
                r = r + add_ref[...]
            return r.astype(out_dtype)

        if nk == 1:
            o_ref[...] = result(part)
        else:
            @pl.when(k == 0)
            def _():
                acc_ref[...] = part

            @pl.when(jnp.logical_and(k > 0, k < nk - 1))
            def _():
                acc_ref[...] += part

            @pl.when(k == nk - 1)
            def _():
                o_ref[...] = result(acc_ref[...] + part)

    in_specs = [pl.BlockSpec((tk, tm), lambda i, j, k: (k, i)) if trans_a else pl.BlockSpec((tm, tk), lambda i, j, k: (i, k)),
                pl.BlockSpec((tn, tk), lambda i, j, k: (j, k)) if trans_b else pl.BlockSpec((tk, tn), lambda i, j, k: (k + b_koff, j))]
    args = [a, b]
    if add is not None:
        in_specs.append(pl.BlockSpec((tm, tn), lambda i, j, k: (i, j)))
        args.append(add)
    if after is not None:
        in_specs.append(pl.BlockSpec(memory_space=pl.ANY))
        args.append(after)
    return pl.pallas_call(
        body, name=name, grid=(M // tm, N // tn, nk), in_specs=in_specs,
        out_specs=pl.BlockSpec((tm, tn), lambda i, j, k: (i, j)),
        out_shape=jax.ShapeDtypeStruct((M, N), out_dtype),
        scratch_shapes=[pltpu.VMEM((tm, tn), F32)],
        compiler_params=_cp(("parallel", "parallel", "arbitrary")),
    )(*args)


def _dxn_last_norm(seg, wpT, row0, dxn, x, dh, w, *, tm):
    S, cols = seg.shape
    assert S % tm == 0 and row0 % GROUP == 0
    ntile = S // tm

    def body(seg_hbm, w_hbm, dxn_hbm, x_hbm, dh_hbm, nw_ref, gx_hbm, dw_ref, b_ref, abuf, pbuf, xbuf, hbuf, obuf, isem, osem):
        b_copy = pltpu.make_async_copy(w_hbm.at[pl.ds(row0, cols), :], b_ref, osem.at[2])
        b_copy.start()

        def first(t):
            return t * tm if isinstance(t, int) else pl.multiple_of(t * tm, tm)

        def fetch(t):
            rows, slot = pl.ds(first(t), tm), t % 2
            return [pltpu.make_async_copy(src.at[rows, :], buf.at[slot], isem.at[slot, j])
                    for j, (src, buf) in enumerate(((seg_hbm, abuf), (dxn_hbm, pbuf), (x_hbm, xbuf), (dh_hbm, hbuf)))]

        def write(t):
            return pltpu.make_async_copy(obuf.at[t % 2], gx_hbm.at[pl.ds(first(t), tm), :], osem.at[t % 2])

        for cp in fetch(0):
            cp.start()
        dw_ref[...] = jnp.zeros_like(dw_ref)
        b_copy.wait()

        def step(t, carry):
            slot = t % 2
            for cp in fetch(t):
                cp.wait()

            @pl.when(t + 1 < ntile)
            def _():
                for cp in fetch(t + 1):
                    cp.start()

            dxn_v = pbuf[slot] + _dot(abuf[slot], b_ref[...])
            xv = xbuf[slot]
            r = lax.rsqrt(jnp.mean(xv * xv, axis=-1, keepdims=True) + EPS)
            xh = xv * r
            dxh = dxn_v * nw_ref[...]
            gx = hbuf[slot] + r * (dxh - xh * jnp.mean(dxh * xh, axis=-1, keepdims=True))
            dw_ref[0:1, :] += jnp.sum(dxn_v * xh, axis=0, keepdims=True)

            @pl.when(t >= 2)
            def _():
                write(t - 2).wait()

            obuf[slot] = gx
            write(t).start()
            return carry

        lax.fori_loop(0, ntile, step, 0)
        for t in range(max(ntile - 2, 0), ntile):
            write(t).wait()

    anyspec = pl.BlockSpec(memory_space=pl.ANY)
    vmem = pl.BlockSpec(memory_space=pltpu.VMEM)
    return pl.pallas_call(
        body, name="dxn_last_norm", in_specs=[anyspec] * 5 + [vmem], out_specs=[anyspec, vmem],
        out_shape=[jax.ShapeDtypeStruct((S, D), F32), jax.ShapeDtypeStruct((8, D), F32)],
        scratch_shapes=[pltpu.VMEM((cols, D), BF16), pltpu.VMEM((2, tm, cols), BF16), pltpu.VMEM((2, tm, D), F32),
                        pltpu.VMEM((2, tm, D), F32), pltpu.VMEM((2, tm, D), F32), pltpu.VMEM((2, tm, D), F32),
                        pltpu.SemaphoreType.DMA((2, 4)), pltpu.SemaphoreType.DMA((3,))],
        compiler_params=_cp(),
    )(seg, wpT, dxn, x, dh, w)


DW_BUFS = 3


def _dw_in(segs, xn, after, *, tm):
    S = xn.shape[0]
    n = len(segs)
    assert all(a.shape[0] == S and a.shape[1] % tm == 0 for a in segs)

    def body(*refs):
        a_refs, xn_hbm = refs[:n], refs[n]
        o_refs = refs[-(n + 5):-5]
        xn_ref, abuf, obuf, asem, osem = refs[-5:]
        xn_copy = pltpu.make_async_copy(xn_hbm, xn_ref, osem.at[2])
        xn_copy.start()
        for q, (a_ref, o_ref) in enumerate(zip(a_refs, o_refs)):
            ntile = a_ref.shape[1] // tm

            def first(t):
                return t * tm if isinstance(t, int) else pl.multiple_of(t * tm, tm)

            def fetch(t, a_ref=a_ref):
                return pltpu.make_async_copy(a_ref.at[:, pl.ds(first(t), tm)], abuf.at[t % DW_BUFS], asem.at[t % DW_BUFS])

            def write(t, o_ref=o_ref):
                return pltpu.make_async_copy(obuf.at[t % 2], o_ref.at[pl.ds(first(t), tm), :], osem.at[t % 2])

            for t in range(min(DW_BUFS - 1, ntile)):
                fetch(t).start()
            if q == 0:
                xn_copy.wait()

            def step(t, carry, fetch=fetch, write=write, ntile=ntile):
                fetch(t).wait()

                @pl.when(t + DW_BUFS - 1 < ntile)
                def _():
                    fetch(t + DW_BUFS - 1).start()

                res = _dot_tn(abuf[t % DW_BUFS], xn_ref[...]).astype(BF16)

                @pl.when(t >= 2)
                def _():
                    write(t - 2).wait()

                obuf[t % 2] = res
                write(t).start()
                return carry

            lax.fori_loop(0, ntile, step, 0)
            for t in range(max(ntile - 2, 0), ntile):
                write(t).wait()

    anyspec = pl.BlockSpec(memory_space=pl.ANY)
    return pl.pallas_call(
        body, name="dw_in", in_specs=[anyspec] * (n + 1 + (after is not None)), out_specs=[anyspec] * n,
        out_shape=[jax.ShapeDtypeStruct((a.shape[1], D), BF16) for a in segs],
        scratch_shapes=[pltpu.VMEM((S, D), BF16), pltpu.VMEM((DW_BUFS, S, tm), BF16), pltpu.VMEM((2, tm, D), BF16),
                        pltpu.SemaphoreType.DMA((DW_BUFS,)), pltpu.SemaphoreType.DMA((3,))],
        compiler_params=_cp(),
    )(*segs, xn, *([after] if after is not None else []))


def _in_proj(x, w, wpT, *, tm, tn):
    S = x.shape[0]
    N = wpT.shape[0]
    assert S % tm == 0 and N % tn == 0, (S, N, tm, tn)

    def body(x_ref, w_ref, b_ref, xn_ref, o_ref, xs_ref):
        @pl.when(pl.program_id(1) == 0)
        def _():
            xv = x_ref[...]
            r = lax.rsqrt(jnp.mean(xv * xv, axis=-1, keepdims=True) + EPS)
            xs = (xv * r * w_ref[...]).astype(BF16)
            xs_ref[...] = xs
            xn_ref[...] = xs

        o_ref[...] = _dot_nt(xs_ref[...], b_ref[...]).astype(BF16)

    return pl.pallas_call(
        body, name="in_proj", grid=(S // tm, N // tn),
        in_specs=[pl.BlockSpec((tm, D), lambda i, j: (i, 0)), pl.BlockSpec((1, D), lambda i, j: (0, 0)),
                  pl.BlockSpec((tn, D), lambda i, j: (j, 0))],
        out_specs=[pl.BlockSpec((tm, D), lambda i, j: (i, 0)), pl.BlockSpec((tm, tn), lambda i, j: (i, j))],
        out_shape=[jax.ShapeDtypeStruct((S, D), BF16), jax.ShapeDtypeStruct((S, N), BF16)],
        scratch_shapes=[pltpu.VMEM((tm, D), BF16)],
        compiler_params=_cp(("parallel", "arbitrary")),
    )(x, w, wpT)


def _sgu_core(u_ref, v_ref, z_ref, g_ref, b_ref, wm_ref, bias_ref, vnb_ref, mixed_ref, tm):
    v = v_ref[...].astype(F32)
    mu = jnp.mean(v, axis=-1, keepdims=True)
    vc = v - mu
    rs = lax.rsqrt(jnp.mean(vc * vc, axis=-1, keepdims=True) + EPS)
    vh = vc * rs
    vnb_ref[...] = (vh * g_ref[...] + b_ref[...]).astype(BF16)
    for blk in range(tm // SGU_BLOCK):
        rows = pl.ds(blk * SGU_BLOCK, SGU_BLOCK)
        for gi in range(SGU_GROUPS):
            cols = pl.ds(gi * LANE, LANE)
            mixed_ref[rows, cols] = _dot(wm_ref[gi], vnb_ref[rows, cols]) + bias_ref[:, cols]
    return vh, rs


def _sgu_fwd(proj, g, b, wm, bias_full, *, tm):
    S = proj.shape[0]

    def body(u_ref, v_ref, z_ref, g_ref, b_ref, wm_ref, bias_ref, y_ref, vnb_ref, mixed_ref):
        _sgu_core(u_ref, v_ref, z_ref, g_ref, b_ref, wm_ref, bias_ref, vnb_ref, mixed_ref, tm)
        z = z_ref[...].astype(F32)
        y_ref[...] = (u_ref[...].astype(F32) * mixed_ref[...] * (z * _sigmoid(z))).astype(BF16)

    seg = lambda off: pl.BlockSpec((tm, D), lambda i: (i, off // D))
    full = lambda a: pl.BlockSpec(a.shape, lambda i: (0,) * a.ndim)
    return pl.pallas_call(
        body, name="sgu_fwd", grid=(S // tm,),
        in_specs=[seg(OFF_U), seg(OFF_V), seg(OFF_ZA), full(g), full(b), full(wm), full(bias_full)],
        out_specs=pl.BlockSpec((tm, D), lambda i: (i, 0)),
        out_shape=jax.ShapeDtypeStruct((S, D), BF16),
        scratch_shapes=[pltpu.VMEM((tm, D), BF16), pltpu.VMEM((tm, D), F32)],
        compiler_params=_cp(("parallel",)),
    )(proj, proj, proj, g, b, wm, bias_full)


def _sgu_bwd(proj, dy, g, b, wm, wmT, bias_full, mask, sel, *, tm):
    S = proj.shape[0]
    nsteps = S // tm

    def body(u_ref, v_ref, z_ref, dy_ref, g_ref, b_ref, wm_ref, wmT_ref, bias_ref, mask_ref, sel_ref,
             dp_ref, dws_ref, dbs_ref, dg_ref, db_ref, vnb_ref, mixed_ref, dmb_ref, dvn_ref, dbias_ref):
        i = pl.program_id(0)

        @pl.when(i == 0)
        def _():
            dws_ref[...] = jnp.zeros_like(dws_ref)
            dg_ref[...] = jnp.zeros_like(dg_ref)
            db_ref[...] = jnp.zeros_like(db_ref)
            dbias_ref[...] = jnp.zeros_like(dbias_ref)

        vh, rs = _sgu_core(u_ref, v_ref, z_ref, g_ref, b_ref, wm_ref, bias_ref, vnb_ref, mixed_ref, tm)
        u = u_ref[...].astype(F32)
        z = z_ref[...].astype(F32)
        dy_v = dy_ref[...].astype(F32)
        mixed = mixed_ref[...]
        sg = _sigmoid(z)
        sz = z * sg
        dp_ref[:, 0:D] = (dy_v * mixed * sz).astype(BF16)
        dp_ref[:, 2 * D:3 * D] = (dy_v * u * mixed * (sg * (1.0 + z * (1.0 - sg)))).astype(BF16)
        dmixed = dy_v * u * sz
        dmb_ref[...] = dmixed.astype(BF16)
        for blk in range(tm // SGU_BLOCK):
            dbias_ref[...] += dmixed[blk * SGU_BLOCK:(blk + 1) * SGU_BLOCK, :]
        for blk in range(tm // SGU_BLOCK):
            rows = pl.ds(blk * SGU_BLOCK, SGU_BLOCK)
            for gi in range(SGU_GROUPS):
                cols = pl.ds(gi * LANE, LANE)
                dm = dmb_ref[rows, cols]
                dvn_ref[rows, cols] = _dot(wmT_ref[gi], dm)
                dws_ref[gi] += _dot_nt(dm, vnb_ref[rows, cols])
        dvn = dvn_ref[...]
        dg_ref[0:1, :] += jnp.sum(dvn * vh, axis=0, keepdims=True)
        db_ref[0:1, :] += jnp.sum(dvn, axis=0, keepdims=True)
        dvh = dvn * g_ref[...]
        dv = rs * (dvh - jnp.mean(dvh, axis=-1, keepdims=True) - vh * jnp.mean(dvh * vh, axis=-1, keepdims=True))
        dp_ref[:, D:2 * D] = dv.astype(BF16)

        @pl.when(i == nsteps - 1)
        def _():
            for gi in range(SGU_GROUPS):
                dws_ref[gi] = dws_ref[gi] * mask_ref[...]
            dbs_ref[...] = _dot(dbias_ref[...], sel_ref[...], precision=HI)

    seg = lambda off: pl.BlockSpec((tm, D), lambda i: (i, off // D))
    full = lambda a: pl.BlockSpec(a.shape, lambda i: (0,) * a.ndim)
    return pl.pallas_call(
        body, name="sgu_bwd", grid=(nsteps,),
        in_specs=[seg(OFF_U), seg(OFF_V), seg(OFF_ZA), pl.BlockSpec((tm, D), lambda i: (i, 0)),
                  full(g), full(b), full(wm), full(wmT), full(bias_full), full(mask), full(sel)],
        out_specs=[pl.BlockSpec((tm, 3 * D), lambda i: (i, 0)),
                   pl.BlockSpec((SGU_GROUPS, SGU_BLOCK, SGU_BLOCK), lambda i: (0, 0, 0)),
                   pl.BlockSpec((SGU_BLOCK, LANE), lambda i: (0, 0)),
                   pl.BlockSpec((8, D), lambda i: (0, 0)), pl.BlockSpec((8, D), lambda i: (0, 0))],
        out_shape=[jax.ShapeDtypeStruct((S, 3 * D), BF16),
                   jax.ShapeDtypeStruct((SGU_GROUPS, SGU_BLOCK, SGU_BLOCK), F32),
                   jax.ShapeDtypeStruct((SGU_BLOCK, LANE), F32),
                   jax.ShapeDtypeStruct((8, D), F32), jax.ShapeDtypeStruct((8, D), F32)],
        scratch_shapes=[pltpu.VMEM((tm, D), BF16), pltpu.VMEM((tm, D), F32), pltpu.VMEM((tm, D), BF16),
                        pltpu.VMEM((tm, D), F32), pltpu.VMEM((SGU_BLOCK, D), F32)],
        compiler_params=_cp(("arbitrary",)),
    )(proj, proj, proj, dy, g, b, wm, wmT, bias_full, mask, sel)


SSD_T = 2 * CHUNK
HALO = 8
HALO_BLK = 16


def _pair_masks():
    row = lax.broadcasted_iota(jnp.int32, (CHUNK, LANE), 0)
    lane = lax.broadcasted_iota(jnp.int32, (CHUNK, LANE), 1)
    pos = jnp.where(lane >= CHUNK, lane - CHUNK, lane)
    diag = (row == pos).astype(F32)
    causal = row >= pos
    lo = (lane < CHUNK).astype(F32)
    return diag, causal, lo, 1.0 - lo


def _ssd_chunk_fwd(c, ext_ref, shift_ref, dt_ref, cw_ref, cb_ref, dtb_ref, alog_ref, tri_ref, exp_ref):
    r0 = c * CHUNK
    win = ext_ref[pl.ds(r0, HALO_BLK + CHUNK), :]
    sh = _dot(shift_ref[...], win)
    taps = [sh[k * CHUNK:(k + 1) * CHUNK] for k in range(CONV_K - 1)] + [win[HALO_BLK:].astype(F32)]
    pre = cb_ref[...] + sum(cw_ref[k:k + 1, :] * taps[k] for k in range(CONV_K))
    sg = _sigmoid(pre)
    xc = pre * sg
    dtr = dt_ref[pl.ds(r0, CHUNK), :].astype(F32) + dtb_ref[...]
    dtv = _softplus(dtr)
    A = -jnp.exp(alog_ref[...])
    acs = _sel_left(tri_ref[...], dtv * A)
    both = _sel_right_k(jnp.concatenate([acs, dtv], axis=0), exp_ref[...])
    E, dtE = both[0:CHUNK], both[CHUNK:2 * CHUNK]
    return dict(taps=taps, pre=pre, sg=sg, xc=xc, dtr=dtr, dtv=dtv, A=A, E=E, dtE=dtE)


def _ssd_fwd(proj, conv_w, conv_b, dtb_p, alog_p, d_exp, norm_w, tri, expand, shift):
    S = proj.shape[0]
    T = SSD_T
    nsteps = S // T
    ncl = T // CHUNK

    def body(zb_ref, xbc_ref, halo_ref, dt_ref, cw_ref, cb_ref, dtb_ref, alog_ref, dexp_ref, nw_ref, tri_ref, exp_ref, shift_ref,
             y_ref, yb_ref, st_ref, ht_ref, ext_ref):
        i = pl.program_id(0)

        @pl.when(i == 0)
        def _():
            ht_ref[...] = jnp.zeros_like(ht_ref)
            ext_ref[0:HALO_BLK, :] = jnp.zeros((HALO_BLK, XBC_W), BF16)

        @pl.when(i > 0)
        def _():
            ext_ref[0:HALO_BLK, :] = halo_ref[...]

        ext_ref[HALO_BLK:HALO_BLK + T, :] = xbc_ref[...]
        diag, causal, lo, hi = _pair_masks()
        for c in range(ncl):
            q = _ssd_chunk_fwd(c, ext_ref, shift_ref, dt_ref, cw_ref, cb_ref, dtb_ref, alog_ref, tri_ref, exp_ref)
            rows = pl.ds(c * CHUNK, CHUNK)
            xc, E, dtE = q["xc"], q["E"], q["dtE"]
            xs = xc[:, 0:D]
            total = E[CHUNK - 1:CHUNK, :]
            x_dt = xs * dtE
            eE = jnp.exp(E)
            xw = x_dt * jnp.exp(total - E)
            st_ref[c] = ht_ref[...]
            for g in range(SSD_GROUPS):
                gc = slice(g * GROUP_W, (g + 1) * GROUP_W)
                Bg = xc[:, D + g * STATE:D + (g + 1) * STATE].astype(BF16)
                Cg = xc[:, D + SSD_GROUPS * STATE + g * STATE:D + SSD_GROUPS * STATE + (g + 1) * STATE].astype(BF16)
                cb2 = _dot_nt(Cg, jnp.concatenate([Bg, Bg], axis=0))
                htg = ht_ref[:, gc]
                y_ref[rows, gc] = eE[:, gc] * _dot(Cg, htg.astype(BF16)) + xs[:, gc] * dexp_ref[:, gc]
                for jj in range(GROUP_W // LANE):
                    pc = slice(g * GROUP_W + jj * LANE, g * GROUP_W + (jj + 1) * LANE)
                    Ej = E[:, pc]
                    e2 = jnp.sum(Ej * diag, axis=0, keepdims=True)
                    Mp = cb2 * jnp.exp(jnp.where(causal, Ej - e2, -1e30))
                    xj = x_dt[:, pc]
                    xbd = jnp.concatenate([xj * lo, xj * hi], axis=0).astype(BF16)
                    y_ref[rows, pc] += _dot(Mp.astype(BF16), xbd)
                ht_ref[:, gc] = jnp.exp(total[:, gc]) * htg + _dot_tn(Bg, xw[:, gc].astype(BF16))
            zb = zb_ref[rows, :].astype(F32)
            hh = y_ref[rows, :] * (zb * _sigmoid(zb))
            for g in range(SSD_GROUPS):
                gc = slice(g * GROUP_W, (g + 1) * GROUP_W)
                hg = hh[:, gc]
                r = lax.rsqrt(jnp.mean(hg * hg, axis=-1, keepdims=True) + EPS)
                yb_ref[rows, gc] = (hg * r * nw_ref[:, gc]).astype(BF16)

    full = lambda a: pl.BlockSpec(a.shape, lambda i: (0,) * a.ndim)
    hb = T // HALO_BLK
    return pl.pallas_call(
        body, name="ssd_fwd", grid=(nsteps,),
        in_specs=[pl.BlockSpec((T, D), lambda i: (i, OFF_ZB // D)),
                  pl.BlockSpec((T, XBC_W), lambda i: (i, OFF_XBC // XBC_W)),
                  pl.BlockSpec((HALO_BLK, XBC_W), lambda i: (jnp.maximum(i * hb - 1, 0), OFF_XBC // XBC_W)),
                  pl.BlockSpec((T, DT_W), lambda i: (i, OFF_DT // DT_W)),
                  full(conv_w), full(conv_b), full(dtb_p), full(alog_p), full(d_exp), full(norm_w), full(tri), full(expand),
                  full(shift)],
        out_specs=[pl.BlockSpec((T, D), lambda i: (i, 0)), pl.BlockSpec((T, D), lambda i: (i, 0)),
                   pl.BlockSpec((ncl, STATE, D), lambda i: (i, 0, 0))],
        out_shape=[jax.ShapeDtypeStruct((S, D), F32), jax.ShapeDtypeStruct((S, D), BF16),
                   jax.ShapeDtypeStruct((S // CHUNK, STATE, D), F32)],
        scratch_shapes=[pltpu.VMEM((STATE, D), F32), pltpu.VMEM((HALO_BLK + T, XBC_W), BF16)],
        compiler_params=_cp(("arbitrary",)),
    )(proj, proj, proj, proj, conv_w, conv_b, dtb_p, alog_p, d_exp, norm_w, tri, expand, shift)


def _ssd_bwd(proj, dyb, y, states, conv_w, conv_b, dtb_p, alog_p, d_exp, norm_w, tri, triT, expand, expandT, shift):
    S = proj.shape[0]
    T = SSD_T
    nsteps = S // T
    ncl = T // CHUNK
    SSD_W = SSD_PAD_W

    def body(zb_ref, xbc_ref, halo_ref, dt_ref, dyb_ref, y_ref, st_ref, cw_ref, cb_ref, dtb_ref, alog_ref, dexp_ref, nw_ref,
             tri_ref, triT_ref, exp_ref, expT_ref, shift_ref,
             dp_ref, dcw_ref, dcb_ref, ddtb_ref, dalog_ref, dD_ref, dnw_ref,
             dht_ref, ext_ref, dpre_ref, dy_s, dE_s, dxdt_s, dxc_s, dDacc_ref, dAacc_ref):
        i = pl.program_id(0)

        @pl.when(i == 0)
        def _():
            for r in (dht_ref, dcw_ref, dcb_ref, ddtb_ref, dnw_ref, dDacc_ref, dAacc_ref):
                r[...] = jnp.zeros_like(r)
            dpre_ref[T:T + HALO_BLK, :] = jnp.zeros((HALO_BLK, XBC_W), F32)

        @pl.when(i == nsteps - 1)
        def _():
            ext_ref[0:HALO_BLK, :] = jnp.zeros((HALO_BLK, XBC_W), BF16)

        @pl.when(i < nsteps - 1)
        def _():
            ext_ref[0:HALO_BLK, :] = halo_ref[...]

        ext_ref[HALO_BLK:HALO_BLK + T, :] = xbc_ref[...]
        diag, causal, lo, hi = _pair_masks()
        last_row = (lax.broadcasted_iota(jnp.int32, (CHUNK, 1), 0) == CHUNK - 1).astype(F32)
        for c in reversed(range(ncl)):
            q = _ssd_chunk_fwd(c, ext_ref, shift_ref, dt_ref, cw_ref, cb_ref, dtb_ref, alog_ref, tri_ref, exp_ref)
            rows = pl.ds(c * CHUNK, CHUNK)
            pre, sg, xc, dtr, dtv, A, E, dtE = (q[k] for k in ("pre", "sg", "xc", "dtr", "dtv", "A", "E", "dtE"))
            xs = xc[:, 0:D]
            total = E[CHUNK - 1:CHUNK, :]
            x_dt = xs * dtE
            eE = jnp.exp(E)
            wdec = jnp.exp(total - E)
            zb = zb_ref[rows, :].astype(F32)
            yv = y_ref[rows, :]
            sgz = _sigmoid(zb)
            sz = zb * sgz
            hh = yv * sz
            for g in range(SSD_GROUPS):
                gc = slice(g * GROUP_W, (g + 1) * GROUP_W)
                hg = hh[:, gc]
                r = lax.rsqrt(jnp.mean(hg * hg, axis=-1, keepdims=True) + EPS)
                dyb_g = dyb_ref[rows, gc].astype(F32)
                dn = dyb_g * nw_ref[:, gc]
                dnw_ref[0:1, gc] += jnp.sum(dyb_g * hg * r, axis=0, keepdims=True)
                dy_s[:, gc] = r * dn - hg * (r * r * r) * jnp.mean(dn * hg, axis=-1, keepdims=True)
            dhh = dy_s[...]
            dp_ref[rows, 0:D] = (dhh * yv * (sgz * (1.0 + zb * (1.0 - sgz)))).astype(BF16)
            dy = dhh * sz
            dy_s[...] = dy
            dDacc_ref[0:1, :] += jnp.sum(dy * xs, axis=0, keepdims=True)
            dxc_s[:, 0:D] = dy * dexp_ref[...]
            for g in range(SSD_GROUPS):
                gc = slice(g * GROUP_W, (g + 1) * GROUP_W)
                bcol = slice(D + g * STATE, D + (g + 1) * STATE)
                ccol = slice(D + SSD_GROUPS * STATE + g * STATE, D + SSD_GROUPS * STATE + (g + 1) * STATE)
                Bg = xc[:, bcol].astype(BF16)
                Cg = xc[:, ccol].astype(BF16)
                B2 = jnp.concatenate([Bg, Bg], axis=0)
                cb2 = _dot_nt(Cg, B2)
                htg = st_ref[c, :, gc]
                htb = htg.astype(BF16)
                dhn = dht_ref[:, gc]
                dhnb = dhn.astype(BF16)
                dyg = dy[:, gc]
                eEg = eE[:, gc]
                wg = wdec[:, gc]
                xdg = x_dt[:, gc]
                CH = _dot(Cg, htb)
                dCHb = (dyg * eEg).astype(BF16)
                dC = _dot_nt(dCHb, htb)
                dl = jnp.exp(total[:, gc])
                dht_prev = _dot_tn(Cg, dCHb) + dl * dhn
                dtot = jnp.sum(dhn * htg, axis=0, keepdims=True) * dl
                dxw = _dot(Bg, dhnb)
                dB = _dot_nt((xdg * wg).astype(BF16), dhnb)
                dwd = dxw * xdg * wg
                dtot = dtot + jnp.sum(dwd, axis=0, keepdims=True)
                dE_s[:, gc] = dyg * eEg * CH - dwd + last_row * dtot
                dxdt_s[:, gc] = dxw * wg
                dcb2 = jnp.zeros((CHUNK, LANE), F32)
                for jj in range(GROUP_W // LANE):
                    pc = slice(g * GROUP_W + jj * LANE, g * GROUP_W + (jj + 1) * LANE)
                    Ej = E[:, pc]
                    e2 = jnp.sum(Ej * diag, axis=0, keepdims=True)
                    Lp = jnp.exp(jnp.where(causal, Ej - e2, -1e30))
                    Mp = cb2 * Lp
                    xj = x_dt[:, pc]
                    xbd = jnp.concatenate([xj * lo, xj * hi], axis=0).astype(BF16)
                    dyj = dy[:, pc].astype(BF16)
                    dMp = _dot_nt(dyj, xbd)
                    dxbd = _dot_tn(Mp.astype(BF16), dyj)
                    dxdt_s[:, pc] += dxbd[0:CHUNK, :] * lo + dxbd[CHUNK:2 * CHUNK, :] * hi
                    dcb2 = dcb2 + dMp * Lp
                    dseg = dMp * Mp
                    dE_s[:, pc] += dseg - diag * jnp.sum(dseg, axis=0, keepdims=True)
                dcb2b = dcb2.astype(BF16)
                dC = dC + _dot(dcb2b, B2)
                dB2 = _dot_tn(dcb2b, Cg)
                dB = dB + dB2[0:CHUNK, :] + dB2[CHUNK:2 * CHUNK, :]
                dxc_s[:, bcol] = dB
                dxc_s[:, ccol] = dC
                dht_ref[:, gc] = dht_prev
            dx_dt = dxdt_s[...]
            dxc_s[:, 0:D] += dx_dt * dtE
            red = _sel_right(jnp.concatenate([dE_s[...], dx_dt * xs], axis=0), expT_ref[...])
            da = _sel_left(triT_ref[...], red[0:CHUNK, :])
            ddtv = red[CHUNK:2 * CHUNK, :] + da * A
            dAacc_ref[0:1, :] += jnp.sum(da * dtv, axis=0, keepdims=True)
            ddtr = ddtv * _sigmoid(dtr)
            ddtb_ref[0:1, :] += jnp.sum(ddtr, axis=0, keepdims=True)
            dp_ref[rows, D + XBC_W:D + XBC_W + DT_W] = ddtr.astype(BF16)
            dpre = dxc_s[...] * (sg * (1.0 + pre * (1.0 - sg)))
            dpre_ref[rows, :] = dpre
            dcb_ref[0:1, :] += jnp.sum(dpre, axis=0, keepdims=True)
            for k in range(CONV_K):
                dcw_ref[k:k + 1, :] += jnp.sum(dpre * q["taps"][k], axis=0, keepdims=True)
        dxbc = jnp.zeros((T, XBC_W), F32)
        for k in range(CONV_K):
            dxbc = dxbc + cw_ref[k:k + 1, :] * dpre_ref[pl.ds(CONV_K - 1 - k, T), :]
        dp_ref[:, D:D + XBC_W] = dxbc.astype(BF16)
        dp_ref[:, SEG_SSD[1]:SSD_W] = jnp.zeros((T, SSD_W - SEG_SSD[1]), BF16)
        dpre_ref[T:T + HALO, :] = dpre_ref[0:HALO, :]

        @pl.when(i == nsteps - 1)
        def _():
            dalog_ref[...] = dAacc_ref[...] * (-jnp.exp(alog_ref[...]))
            dD_ref[...] = _dot(dDacc_ref[...], expT_ref[...].astype(F32), precision=HI)

    full = lambda a: pl.BlockSpec(a.shape, lambda i: (0,) * a.ndim)
    hb = T // HALO_BLK
    rev = lambda i: nsteps - 1 - i
    acc = lambda w: pl.BlockSpec((8, w), lambda i: (0, 0))
    return pl.pallas_call(
        body, name="ssd_bwd", grid=(nsteps,),
        in_specs=[pl.BlockSpec((T, D), lambda i: (rev(i), OFF_ZB // D)),
                  pl.BlockSpec((T, XBC_W), lambda i: (rev(i), OFF_XBC // XBC_W)),
                  pl.BlockSpec((HALO_BLK, XBC_W), lambda i: (jnp.maximum(rev(i) * hb - 1, 0), OFF_XBC // XBC_W)),
                  pl.BlockSpec((T, DT_W), lambda i: (rev(i), OFF_DT // DT_W)),
                  pl.BlockSpec((T, D), lambda i: (rev(i), 0)), pl.BlockSpec((T, D), lambda i: (rev(i), 0)),
                  pl.BlockSpec((ncl, STATE, D), lambda i: (rev(i), 0, 0)),
                  full(conv_w), full(conv_b), full(dtb_p), full(alog_p), full(d_exp), full(norm_w),
                  full(tri), full(triT), full(expand), full(expandT), full(shift)],
        out_specs=[pl.BlockSpec((T, SSD_W), lambda i: (rev(i), 0)),
                   acc(XBC_W), acc(XBC_W), acc(DT_W), acc(DT_W), acc(DT_W), acc(D)],
        out_shape=[jax.ShapeDtypeStruct((S, SSD_W), BF16),
                   jax.ShapeDtypeStruct((8, XBC_W), F32), jax.ShapeDtypeStruct((8, XBC_W), F32),
                   jax.ShapeDtypeStruct((8, DT_W), F32), jax.ShapeDtypeStruct((8, DT_W), F32),
                   jax.ShapeDtypeStruct((8, DT_W), F32), jax.ShapeDtypeStruct((8, D), F32)],
        scratch_shapes=[pltpu.VMEM((STATE, D), F32), pltpu.VMEM((HALO_BLK + T, XBC_W), BF16), pltpu.VMEM((T + HALO_BLK, XBC_W), F32),
                        pltpu.VMEM((CHUNK, D), F32), pltpu.VMEM((CHUNK, D), F32), pltpu.VMEM((CHUNK, D), F32),
                        pltpu.VMEM((CHUNK, XBC_W), F32), pltpu.VMEM((8, D), F32), pltpu.VMEM((8, DT_W), F32)],
        compiler_params=_cp(("arbitrary",)),
    )(proj, proj, proj, proj, dyb, y, states, conv_w, conv_b, dtb_p, alog_p, d_exp, norm_w, tri, triT, expand, expandT, shift)


def _head(x, ya, yb, proj, target, gate_b, wout, fw, *, tm):
    S = x.shape[0]

    def body(x_ref, ya_ref, yb_ref, gl0_ref, gl1_ref, t_ref, gb_ref, w_ref, fw_ref,
             dh_ref, dhb_ref, mb_ref, dya_ref, dyb_ref, dgl_ref, loss_ref, dfw_ref, dgb_ref):
        @pl.when(pl.program_id(0) == 0)
        def _():
            loss_ref[...] = jnp.zeros_like(loss_ref)
            dfw_ref[...] = jnp.zeros_like(dfw_ref)
            dgb_ref[...] = jnp.zeros_like(dgb_ref)

        ya_v = ya_ref[...].astype(F32)
        yb_v = yb_ref[...].astype(F32)
        g0 = _sigmoid(gl0_ref[...].astype(F32) + gb_ref[:, 0:D])
        g1 = _sigmoid(gl1_ref[...].astype(F32) + gb_ref[:, D:2 * D])
        mb = (g0 * ya_v + g1 * yb_v).astype(BF16)
        mb_ref[...] = mb
        h = x_ref[...] + _dot(mb, w_ref[...])
        r = lax.rsqrt(jnp.mean(h * h, axis=-1, keepdims=True) + EPS)
        hn = h * r
        err = hn * fw_ref[...] - t_ref[...]
        loss_ref[...] += 0.5 * jnp.sum(jnp.mean(err * err, axis=-1, keepdims=True))
        dyf = err * (1.0 / D)
        dfw_ref[0:1, :] += jnp.sum(dyf * hn, axis=0, keepdims=True)
        dhn = dyf * fw_ref[...]
        dh = r * (dhn - hn * jnp.mean(dhn * hn, axis=-1, keepdims=True))
        dh_ref[...] = dh
        dhb = dh.astype(BF16)
        dhb_ref[...] = dhb
        dm = _dot_nt(dhb, w_ref[...])
        dya_ref[...] = (dm * g0).astype(BF16)
        dyb_ref[...] = (dm * g1).astype(BF16)
        dgl0 = dm * ya_v * g0 * (1.0 - g0)
        dgl1 = dm * yb_v * g1 * (1.0 - g1)
        dgl_ref[:, 0:D] = dgl0.astype(BF16)
        dgl_ref[:, D:2 * D] = dgl1.astype(BF16)
        dgb_ref[0:1, 0:D] += jnp.sum(dgl0, axis=0, keepdims=True)
        dgb_ref[0:1, D:2 * D] += jnp.sum(dgl1, axis=0, keepdims=True)

    row = pl.BlockSpec((tm, D), lambda i: (i, 0))
    seg = lambda off: pl.BlockSpec((tm, D), lambda i: (i, off // D))
    full = lambda a: pl.BlockSpec(a.shape, lambda i: (0,) * a.ndim)
    acc = lambda w: pl.BlockSpec((8, w), lambda i: (0, 0))
    return pl.pallas_call(
        body, name="head", grid=(S // tm,),
        in_specs=[row, row, row, seg(OFF_G0), seg(OFF_G1), row, full(gate_b), full(wout), full(fw)],
        out_specs=[row, row, row, row, row, pl.BlockSpec((tm, 2 * D), lambda i: (i, 0)), acc(LANE), acc(D), acc(2 * D)],
        out_shape=[jax.ShapeDtypeStruct((S, D), F32), jax.ShapeDtypeStruct((S, D), BF16), jax.ShapeDtypeStruct((S, D), BF16),
                   jax.ShapeDtypeStruct((S, D), BF16), jax.ShapeDtypeStruct((S, D), BF16), jax.ShapeDtypeStruct((S, 2 * D), BF16),
                   jax.ShapeDtypeStruct((8, LANE), F32), jax.ShapeDtypeStruct((8, D), F32), jax.ShapeDtypeStruct((8, 2 * D), F32)],
        compiler_params=_cp(("arbitrary",)),
    )(x, ya, yb, proj, proj, target, gate_b, wout, fw)


def _adam_update(g, w_ref, m_ref, v_ref, g_ref, d_ref, m2_ref, v2_ref):
    m2 = ADAM_B1 * m_ref[...] + (1.0 - ADAM_B1) * g
    v2 = ADAM_B2 * v_ref[...] + (1.0 - ADAM_B2) * (g * g)
    m_hat = m2 / (1.0 - ADAM_B1 ** ADAM_STEP)
    v_hat = v2 / (1.0 - ADAM_B2 ** ADAM_STEP)
    g_ref[...] = g
    d_ref[...] = -ADAM_LR * (m_hat / (jnp.sqrt(v_hat) + ADAM_EPS) + ADAM_WD * w_ref[...])
    m2_ref[...] = m2
    v2_ref[...] = v2


def _adamw_own(me, own, landed, w, m, v, *, tr, tc, name):
    _, R, C = landed.shape
    assert R % tr == 0 and C % tc == 0, (name, R, C, tr, tc)

    def body(me_ref, own_ref, p_ref, w_ref, m_ref, v_ref, g_ref, d_ref, m2_ref, v2_ref):
        mine = own_ref[0].astype(F32)
        g = jnp.where(me_ref[0] == 0, mine, p_ref[0].astype(F32))
        for k in range(1, N_DEV):
            g = g + jnp.where(me_ref[0] == k, mine, p_ref[k].astype(F32))
        _adam_update(g, w_ref, m_ref, v_ref, g_ref, d_ref, m2_ref, v2_ref)

    tile = pl.BlockSpec((tr, tc), lambda i, j, me_ref: (i, j))
    return pl.pallas_call(
        body, name=name,
        grid_spec=pltpu.PrefetchScalarGridSpec(
            num_scalar_prefetch=1, grid=(R // tr, C // tc),
            in_specs=[pl.BlockSpec((1, tr, tc), lambda i, j, me_ref: (me_ref[0], i, j)),
                      pl.BlockSpec((N_DEV, tr, tc), lambda i, j, me_ref: (0, i, j)), tile, tile, tile],
            out_specs=[tile, tile, tile, tile]),
        out_shape=[jax.ShapeDtypeStruct((R, C), F32)] * 4,
        compiler_params=_cp(("parallel", "parallel")),
    )(me, own, landed, w, m, v)


def _adamw(parts, w, m, v, *, tr, name):
    _, R, C = parts.shape
    assert R % tr == 0, (name, R, tr)

    def body(p_ref, w_ref, m_ref, v_ref, g_ref, d_ref, m2_ref, v2_ref):
        g = p_ref[0].astype(F32)
        for k in range(1, N_DEV):
            g = g + p_ref[k].astype(F32)
        _adam_update(g, w_ref, m_ref, v_ref, g_ref, d_ref, m2_ref, v2_ref)

    row = pl.BlockSpec((tr, C), lambda i: (i, 0))
    return pl.pallas_call(
        body, name=name, grid=(R // tr,),
        in_specs=[pl.BlockSpec((N_DEV, tr, C), lambda i: (0, i, 0)), row, row, row],
        out_specs=[row, row, row, row],
        out_shape=[jax.ShapeDtypeStruct((R, C), F32)] * 4,
        compiler_params=_cp(("parallel",)),
    )(parts, w, m, v)


def _place():
    x, y, c = lax.axis_index("x"), lax.axis_index("y"), lax.axis_index("c")
    return x, y, c


def _all_gather(arrs, *, name):
    n = len(arrs)

    def body(*refs):
        ins, outs = refs[:n], refs[n:2 * n]
        send_sems, recv_sems, local_sems = refs[2 * n:]
        x, y, c = _place()
        me, sibling = (x, y, c), (x, y, 1 - c)
        chips = [(1 - x, y), (x, 1 - y), (1 - x, 1 - y)]

        def idx(px, py, pc):
            return 4 * px + 2 * py + pc

        def copy(k, a, block, to, src=None):
            slab = outs[a].at[idx(*block)]
            return pltpu.make_async_remote_copy(
                src_ref=slab if src is None else src, dst_ref=slab,
                send_sem=send_sems.at[k, a], recv_sem=recv_sems.at[k, a], device_id=to, device_id_type=MESH)

        mine = [pltpu.make_async_copy(ins[a], outs[a].at[idx(*me)], local_sems.at[a]) for a in range(n)]
        for cp in mine:
            cp.start()
        first = []
        for a in range(n):
            first.append(copy(0, a, me, sibling, src=ins[a]))
            first += [copy(1 + j, a, me, (*chip, c), src=ins[a]) for j, chip in enumerate(chips)]
        for cp in first:
            cp.start()
        passed = []
        for j, chip in enumerate(chips):
            for a in range(n):
                copy(1 + j, a, (*chip, c), me).wait_recv()
                fwd = copy(4 + j, a, (*chip, c), sibling)
                fwd.start()
                passed.append(fwd)
        for a in range(n):
            copy(0, a, sibling, me).wait_recv()
            for j, chip in enumerate(chips):
                copy(4 + j, a, (*chip, 1 - c), me).wait_recv()
        for cp in first + passed:
            cp.wait_send()
        for cp in mine:
            cp.wait()

    anyspec = pl.BlockSpec(memory_space=pl.ANY)
    return pl.pallas_call(
        body, name=name,
        in_specs=[anyspec] * n, out_specs=[anyspec] * n,
        out_shape=[jax.ShapeDtypeStruct((N_DEV,) + a.shape, a.dtype) for a in arrs],
        scratch_shapes=[pltpu.SemaphoreType.DMA((7, n)), pltpu.SemaphoreType.DMA((7, n)), pltpu.SemaphoreType.DMA((n,))],
    )(*arrs)


W_ROWS = SEG_SSD[0] + SSD_PAD_W


GROUP = 16
INTERIOR = 1920


def _interior(k):
    lo = -(-(k * SHARD_IN) // GROUP) * GROUP
    hi = ((k + 1) * SHARD_IN) // GROUP * GROUP
    return lo, hi


def _dest_row(r):
    if r < REF_SGU_END:
        return r
    return r - REF_SGU_END + SEG_SSD[0] if r < REF_GATE_START else r - REF_GATE_START + SEG_GATE[0]


def _shard_pieces(k):
    lo_k, hi_k = _interior(k)
    out = []
    for lo, hi in ((0, REF_SGU_END), (REF_SGU_END, REF_GATE_START), (REF_GATE_START, W_IN)):
        a, b = max(lo, lo_k), min(hi, hi_k)
        if a < b:
            out.append((a - lo_k, b - a, _dest_row(a)))
    return out


GATHER_PARTS = 1


def _shard_parts(k):
    parts = [[] for _ in range(GATHER_PARTS)]
    for s0, n, d0 in _shard_pieces(k):
        step = -(-(n // GROUP) // GATHER_PARTS) * GROUP
        for p in range(GATHER_PARTS):
            a, b = min(p * step, n), min((p + 1) * step, n)
            if a < b:
                parts[p].append((s0 + a, b - a, d0 + a))
    return parts


def _patch_straddlers(wpT, heads, tails):
    for k in range(1, N_DEV):
        m = (k * SHARD_IN) % GROUP
        if m:
            group = jnp.concatenate([tails[k - 1, GROUP - m:], heads[k, :GROUP - m]], axis=0)
            wpT = lax.dynamic_update_slice(wpT, group, (_dest_row(k * SHARD_IN - m), 0))
    return wpT


def _gather_stages(k, win_ref, small, z_ref, n_zero, w_ref, send_sems, recv_sems, local_sems):
    x, y, c = k // 4, (k // 2) % 2, k % 2
    idx = lambda p: 4 * p[0] + 2 * p[1] + p[2]
    me, sib = (x, y, c), (x, y, 1 - c)
    xn, yn, dg = (1 - x, y, c), (x, 1 - y, c), (1 - x, 1 - y, c)
    parts = range(GATHER_PARTS)

    def copies(slot, block, to, part, own=False):
        kb = idx(block)
        out = []
        for j, (s0, n, d0) in enumerate(_shard_parts(kb)[part]):
            dst = w_ref.at[pl.ds(d0, n)]
            out.append((win_ref.at[pl.ds(s0, n)] if own else dst, dst, 2 * part + j))
        if part == 0:
            for j, (src, gathered) in enumerate(small):
                out.append((src if own else gathered.at[kb], gathered.at[kb], 2 * GATHER_PARTS + j))
        return [pltpu.make_async_remote_copy(src_ref=s, dst_ref=d, send_sem=send_sems.at[slot, j], recv_sem=recv_sems.at[slot, j],
                                             device_id=to, device_id_type=MESH) for s, d, j in out]

    def start(cps):
        for cp in cps:
            cp.start()

    def arrived(slot, block, part):
        for cp in copies(slot, block, me, part):
            cp.wait_recv()

    def local():
        pairs = [(win_ref.at[pl.ds(s0, n)], w_ref.at[pl.ds(d0, n)]) for s0, n, d0 in _shard_pieces(k)]
        pairs += [(src, gathered.at[k]) for src, gathered in small] + [(z_ref, w_ref.at[pl.ds(W_IN, n_zero)])]
        return [pltpu.make_async_copy(s, d, local_sems.at[j]) for j, (s, d) in enumerate(pairs)]

    relay = (xn, yn) if c == 1 else (yn, xn)

    def first():
        start(local())
        for p in parts:
            start(copies(0, me, sib, p, own=True) + copies(1, me, xn, p, own=True) + copies(2, me, yn, p, own=True))

    def hand_on():
        for p in parts:
            arrived(1, xn, p)
            start(copies(4, xn, sib, p))
            if c == 1:
                start(copies(3, *relay, p))
            arrived(2, yn, p)
            start(copies(5, yn, sib, p))
            if c == 0:
                start(copies(3, *relay, p))

    def finish():
        for p in parts:
            arrived(3, dg, p)
            start(copies(6, dg, sib, p))
        for p in parts:
            arrived(0, sib, p)
            arrived(4, (1 - x, y, 1 - c), p)
            arrived(5, (x, 1 - y, 1 - c), p)
            arrived(6, (1 - x, 1 - y, 1 - c), p)
        for p in parts:
            sent = (copies(0, me, sib, p, own=True) + copies(1, me, xn, p, own=True) + copies(2, me, yn, p, own=True)
                    + copies(3, *relay, p) + copies(4, xn, sib, p) + copies(5, yn, sib, p) + copies(6, dg, sib, p))
            for cp in sent:
                cp.wait_send()
        for cp in local():
            cp.wait()

    return first, hand_on, finish


def _gather_sems(n_small):
    n_arr = 2 * GATHER_PARTS + n_small
    return [pltpu.SemaphoreType.DMA((7, n_arr)), pltpu.SemaphoreType.DMA((7, n_arr)), pltpu.SemaphoreType.DMA((n_arr + 1,))]


def _gather_weights(win, head, tail, wout, cw, zeros):
    small_in = (wout, cw, head, tail)
    n_zero = zeros.shape[0]
    assert W_IN + n_zero == W_ROWS and W_IN % GROUP == 0

    def body(win_ref, wout_ref, cw_ref, head_ref, tail_ref, z_ref, w_ref, gout_ref, gcw_ref, ghead_ref, gtail_ref, *sems):
        x, y, c = _place()
        me = 4 * x + 2 * y + c
        small = ((wout_ref, gout_ref), (cw_ref, gcw_ref), (head_ref, ghead_ref), (tail_ref, gtail_ref))

        def run(k):
            for stage in _gather_stages(k, win_ref, small, z_ref, n_zero, w_ref, *sems):
                stage()

        for k in range(N_DEV):
            pl.when(me == k)(functools.partial(run, k))

    anyspec = pl.BlockSpec(memory_space=pl.ANY)
    return pl.pallas_call(
        body, name="gather_weights", in_specs=[anyspec] * 6, out_specs=[anyspec] * 5,
        out_shape=[jax.ShapeDtypeStruct((W_ROWS, D), win.dtype)]
        + [jax.ShapeDtypeStruct((N_DEV,) + a.shape, a.dtype) for a in small_in],
        scratch_shapes=_gather_sems(len(small_in)),
    )(win, wout, cw, head, tail, zeros)


_REL = [(dx, dy, dc) for dx in (0, 1) for dy in (0, 1) for dc in (0, 1)][1:]
_HBM = pl.BlockSpec(memory_space=pltpu.HBM)
_SEM = pl.BlockSpec(memory_space=pltpu.SEMAPHORE)
_EFFECT = pltpu.SideEffectType.DATAFLOW_SIDE_EFFECTING


def _peer(k):
    x, y, c = _place()
    dx, dy, dc = _REL[k]
    return (1 - x if dx else x, 1 - y if dy else y, 1 - c if dc else c)


def _exchange_start(parts, *, name):
    n = len(parts)

    def body(*refs):
        ins, lands = refs[:n], refs[n:2 * n]
        send_sems, recv_sems, token = refs[2 * n], refs[2 * n + 1], refs[-1]
        x, y, c = _place()
        me = 4 * x + 2 * y + c
        for a in range(n):
            for k in range(len(_REL)):
                px, py, pc = _peer(k)
                pltpu.make_async_remote_copy(
                    src_ref=ins[a].at[4 * px + 2 * py + pc], dst_ref=lands[a].at[me],
                    send_sem=send_sems.at[len(_REL) * a + k], recv_sem=recv_sems.at[len(_REL) * a + k],
                    device_id=(px, py, pc), device_id_type=MESH).start()
        token[...] = jnp.zeros_like(token)

    sem = pltpu.SemaphoreType.DMA((len(_REL) * n,))
    bufs = [pltpu.HBM(p.shape, p.dtype) for p in parts]
    outs = pl.pallas_call(
        body, name=name,
        out_shape=(sem, sem, *bufs, *bufs, jax.ShapeDtypeStruct((8, LANE), F32)),
        in_specs=(_HBM,) * (2 * n), out_specs=(_SEM, _SEM, *(_HBM,) * (2 * n), pl.BlockSpec(memory_space=pltpu.VMEM)),
        input_output_aliases={i: 2 + i for i in range(2 * n)},
        compiler_params=pltpu.CompilerParams(has_side_effects=_EFFECT),
    )(*[pltpu.with_memory_space_constraint(p, pltpu.HBM) for p in parts],
      *[pltpu.with_memory_space_constraint(lax.empty(p.shape, p.dtype), pltpu.HBM) for p in parts])
    return outs[0], outs[1], outs[2:2 + n], outs[2 + n:2 + 2 * n], outs[-1]


def _exchange_wait(send_sems, recv_sems, parts, lands, after, *, name):
    n = len(parts)

    def body(*refs):
        ins, lands_ = refs[:n], refs[n:2 * n]
        ssem, rsem = refs[2 * n], refs[2 * n + 1]
        for a in range(n):
            for k in range(len(_REL)):
                px, py, pc = _peer(k)
                p = 4 * px + 2 * py + pc
                cp = pltpu.make_async_remote_copy(
                    src_ref=ins[a].at[p], dst_ref=lands_[a].at[p],
                    send_sem=ssem.at[len(_REL) * a + k], recv_sem=rsem.at[len(_REL) * a + k],
                    device_id=(px, py, pc), device_id_type=MESH)
                cp.wait_send()
                cp.wait_recv()

    bufs = [pltpu.HBM(p.shape, p.dtype) for p in parts]
    outs = pl.pallas_call(
        body, name=name, out_shape=(*bufs, *bufs),
        in_specs=(*(_HBM,) * (2 * n), _SEM, _SEM, pl.BlockSpec(memory_space=pl.ANY)), out_specs=(_HBM,) * (2 * n),
        input_output_aliases={i: i for i in range(2 * n)},
        compiler_params=pltpu.CompilerParams(has_side_effects=_EFFECT),
    )(*parts, *lands, send_sems, recv_sems, after)
    return outs[:n], outs[n:]


WEIGHTS = ('norm_w', 'w_in', 'gate_b', 'sgu_norm_g', 'sgu_norm_b', 'sgu_w', 'sgu_b', 'conv_w', 'conv_b', 'dt_bias', 'A_log',
           'D_skip', 'ssd_norm_w', 'w_out', 'final_norm_w')
SHARDED = ('w_in', 'conv_w', 'w_out')
PACK_ROW = 8 * LANE


def _constants():
    tri = np.tril(np.ones((CHUNK, CHUNK), np.float32))
    expand = np.zeros((DT_W, D), np.float32)
    for h in range(HEADS):
        expand[h, h * HEADDIM:(h + 1) * HEADDIM] = 1.0
    sel = np.zeros((D, LANE), np.float32)
    for g in range(SGU_GROUPS):
        sel[g * LANE:(g + 1) * LANE, g] = 1.0
    pos_chunk = np.arange(SGU_BLOCK) // CHUNK
    mask = (pos_chunk[None, :] <= pos_chunk[:, None]).astype(np.float32)
    shift = np.zeros(((CONV_K - 1) * CHUNK, HALO_BLK + CHUNK), np.float32)
    for kk in range(CONV_K - 1):
        for t in range(CHUNK):
            shift[kk * CHUNK + t, HALO_BLK - (CONV_K - 1) + t + kk] = 1.0
    return dict(tri=jnp.asarray(tri, BF16), triT=jnp.asarray(tri.T.copy(), BF16), expand=jnp.asarray(np.tile(expand, (3, 1)), BF16),
                shift=jnp.asarray(shift, BF16),
                expandT=jnp.asarray(expand.T.copy(), BF16), sel=jnp.asarray(sel), mask=jnp.asarray(mask))


def _to_shards(segs):
    starts = np.cumsum([0] + [n for _, n in segs])
    assert starts[-1] == W_IN
    slabs = []
    for k in range(N_DEV):
        pieces = []
        for (s, n), s0 in zip(segs, starts[:-1]):
            lo, hi = max(k * SHARD_IN, s0), min((k + 1) * SHARD_IN, s0 + n)
            if lo < hi:
                pieces.append(s[lo - s0:hi - s0])
        slabs.append(jnp.concatenate(pieces, axis=0))
    return jnp.stack(slabs)


def _local_step(x2, tgt, wpT, wout, cw, p, exchange_small, exchange):
    S = x2.shape[0]
    k = _constants()
    xn, proj = _in_proj(x2, p['norm_w'], wpT, tm=min(1024, S), tn=2048)
    wm32 = p['sgu_w'][0] * k['mask']
    wm = wm32.astype(BF16)
    wmT = jnp.swapaxes(wm32, 1, 2).astype(BF16)
    bias_full = jnp.repeat(p['sgu_b'][0].T, LANE, axis=1)
    tm_sgu = min(512, S)
    ya = _sgu_fwd(proj, p['sgu_norm_g'], p['sgu_norm_b'], wm, bias_full, tm=tm_sgu)
    pad32 = lambda a: jnp.pad(a, ((0, 0), (0, DT_W - HEADS)))
    dtb_p, alog_p = pad32(p['dt_bias']), pad32(p['A_log'])
    d_exp = jnp.repeat(p['D_skip'], HEADDIM, axis=1)
    ssd_args = (cw, p['conv_b'], dtb_p, alog_p, d_exp, p['ssd_norm_w'])
    y, yb, states = _ssd_fwd(proj, *ssd_args, k['tri'], k['expand'], k['shift'])
    dh, dhb, mb, dya, dyb, dgl, loss, dfw, dgb = _head(
        x2, ya, yb, proj, tgt, p['gate_b'], wout, p['final_norm_w'][None, :], tm=min(256, S))
    dsgu, dws, dbsT, dsg, dsb = _sgu_bwd(proj, dya, p['sgu_norm_g'], p['sgu_norm_b'], wm, wmT, bias_full, k['mask'], k['sel'],
                                         tm=tm_sgu)
    dssd, dcw, dcb, ddtb, dalog, dD, dnw = _ssd_bwd(proj, dyb, y, states, *ssd_args, k['tri'], k['triT'], k['expand'], k['expandT'],
                                                    k['shift'])
    grads = dict(
        gate_b=dgb[0:1], sgu_norm_g=dsg[0:1], sgu_norm_b=dsb[0:1], sgu_w=dws[None],
        sgu_b=dbsT[:, :SGU_GROUPS].T[None], conv_w=dcw[0:CONV_K][None], conv_b=dcb[0:1], dt_bias=ddtb[0:1, :HEADS],
        A_log=dalog[0:1, :HEADS], D_skip=dD[0:1, :HEADS], ssd_norm_w=dnw[0:1], final_norm_w=dfw[0])
    tw = dict(trans_a=True, out_dtype=BF16, tm=1024, tn=512, tk=S)
    dw_out = _matmul(mb, dhb, name="dw_out", **tw)
    token = exchange_small(loss[0, 0], grads, dw_out)
    dwT_sgu, dwT_gate, dwT_ssd = _dw_in([dsgu, dgl, dssd], xn, token, tm=256)
    token = exchange([(dwT_sgu, SEG_SGU[1]), (dwT_ssd, W_IN - SEG_SSD[0]), (dwT_gate, SEG_GATE[1])])
    tm, tn = min(1024, S), 1024
    dxn = _matmul(dsgu, wpT, tm=tm, tn=512, tk=SEG_SGU[1], after=token, name="dxn_sgu")
    dxn = _matmul(dgl, wpT, b_koff=SEG_GATE[0] // 2048, tm=tm, tn=tn, tk=2048, add=dxn, name="dxn_gate")
    grad_x, dnorm = _dxn_last_norm(dssd, wpT, SEG_SSD[0], dxn, x2, dh, p['norm_w'], tm=min(256, S))
    return grad_x, dnorm[0:1]


def _pack(arrs):
    rows, offs, r = [], [], 0
    for a in arrs:
        n = a.size
        nr = -(-n // PACK_ROW) * 8
        rows.append(jnp.pad(a.reshape(-1).astype(F32), (0, nr * LANE - n)).reshape(nr, LANE))
        offs.append(r)
        r += nr
    return jnp.concatenate(rows, axis=0), offs


def kernel(x, norm_w, w_in, gate_b, sgu_norm_g, sgu_norm_b, sgu_w, sgu_b, conv_w, conv_b, dt_bias, A_log, D_skip, ssd_norm_w, w_out, final_norm_w, loss_target, m_norm_w, m_w_in, m_gate_b, m_sgu_norm_g, m_sgu_norm_b, m_sgu_w, m_sgu_b, m_conv_w, m_conv_b, m_dt_bias, m_A_log, m_D_skip, m_ssd_norm_w, m_w_out, m_final_norm_w, v_norm_w, v_w_in, v_gate_b, v_sgu_norm_g, v_sgu_norm_b, v_sgu_w, v_sgu_b, v_conv_w, v_conv_b, v_dt_bias, v_A_log, v_D_skip, v_ssd_norm_w, v_w_out, v_final_norm_w):
    w = dict(norm_w=norm_w, w_in=w_in, gate_b=gate_b, sgu_norm_g=sgu_norm_g, sgu_norm_b=sgu_norm_b, sgu_w=sgu_w, sgu_b=sgu_b,
             conv_w=conv_w, conv_b=conv_b, dt_bias=dt_bias, A_log=A_log, D_skip=D_skip, ssd_norm_w=ssd_norm_w, w_out=w_out,
             final_norm_w=final_norm_w)
    m = dict(norm_w=m_norm_w, w_in=m_w_in, gate_b=m_gate_b, sgu_norm_g=m_sgu_norm_g, sgu_norm_b=m_sgu_norm_b, sgu_w=m_sgu_w,
             sgu_b=m_sgu_b, conv_w=m_conv_w, conv_b=m_conv_b, dt_bias=m_dt_bias, A_log=m_A_log, D_skip=m_D_skip,
             ssd_norm_w=m_ssd_norm_w, w_out=m_w_out, final_norm_w=m_final_norm_w)
    v = dict(norm_w=v_norm_w, w_in=v_w_in, gate_b=v_gate_b, sgu_norm_g=v_sgu_norm_g, sgu_norm_b=v_sgu_norm_b, sgu_w=v_sgu_w,
             sgu_b=v_sgu_b, conv_w=v_conv_w, conv_b=v_conv_b, dt_bias=v_dt_bias, A_log=v_A_log, D_skip=v_D_skip,
             ssd_norm_w=v_ssd_norm_w, w_out=v_w_out, final_norm_w=v_final_norm_w)
    me = 4 * lax.axis_index("x") + 2 * lax.axis_index("y") + lax.axis_index("c")
    shard_cw = XBC_W // N_DEV

    tpose = lambda a: jnp.swapaxes(a[0], 0, 1)
    wT = tpose(w_in).astype(BF16)
    first_group = (GROUP - (me * SHARD_IN) % GROUP) % GROUP
    window = lax.dynamic_slice(jnp.pad(wT, ((0, GROUP), (0, 0))), (first_group, 0), (INTERIOR, D))
    wpT, g_out, g_cw, heads, tails = _gather_weights(window, wT[:GROUP], wT[SHARD_IN - GROUP:], w_out[0].astype(BF16),
                                                     conv_w[0], jnp.zeros((W_ROWS - W_IN, D), BF16))
    wpT = _patch_straddlers(wpT, heads, tails)
    wout_full = g_out.reshape(D, D)
    cw_full = jnp.swapaxes(g_cw, 0, 1).reshape(CONV_K, XBC_W)

    flight = {}

    small = [n for n in WEIGHTS if n not in SHARDED and n != 'norm_w']
    early = {}

    def exchange_small(loss_part, grads, dw_out):
        early['packed'], early['offs'] = _pack([grads[n] for n in small] + [loss_part, grads['conv_w']])
        parts = [jnp.broadcast_to(early['packed'][None], (N_DEV,) + early['packed'].shape), dw_out.reshape(N_DEV, D // N_DEV, D)]
        early['sems'], early['rsems'], early['parts'], early['lands'], token = _exchange_start(parts, name="small_start")
        return token

    def exchange(dw_inT_segs):
        parts = [_to_shards(dw_inT_segs)]
        flight['sems'], flight['rsems'], flight['parts'], flight['lands'], token = _exchange_start(parts, name="exchange_start")
        return token

    grad_x, dnorm = _local_step(x[0], loss_target[0], wpT, wout_full, cw_full, w, exchange_small, exchange)
    norm_packed = _pack([dnorm])[0]
    norm_sems, norm_rsems, norm_src, norm_lands, _ = _exchange_start(
        [jnp.broadcast_to(norm_packed[None], (N_DEV,) + norm_packed.shape)], name="norm_start")
    (_, own_out), (land_small, land_out) = _exchange_wait(
        early['sems'], early['rsems'], early['parts'], early['lands'], grad_x, name="small_wait")
    (own_in,), (land_in,) = _exchange_wait(
        flight['sems'], flight['rsems'], flight['parts'], flight['lands'], grad_x, name="exchange_wait")
    me_arr = jnp.reshape(me, (1,)).astype(jnp.int32)
    res = {}
    res['w_in'] = [jnp.swapaxes(o, 0, 1) for o in _adamw_own(
        me_arr, own_in, land_in, tpose(w_in), tpose(m_w_in), tpose(v_w_in), tr=SHARD_IN, tc=256, name="adamw_w_in")]
    res['w_out'] = _adamw_own(me_arr, own_out, land_out, w_out[0], m_w_out[0], v_w_out[0], tr=128, tc=D, name="adamw_w_out")

    _, (land_norm,) = _exchange_wait(norm_sems, norm_rsems, norm_src, norm_lands, res['w_out'][0], name="norm_wait")
    norm_parts = lax.dynamic_update_slice(land_norm, norm_packed[None], (me, 0, 0))
    norm_outs = _adamw(norm_parts, *[_pack([d['norm_w']])[0] for d in (w, m, v)], tr=norm_parts.shape[1], name="adamw_norm")
    res['norm_w'] = [o.reshape(-1)[:D].reshape(w['norm_w'].shape) for o in norm_outs]

    offs = early['offs']
    gathered = lax.dynamic_update_slice(land_small, early['packed'][None], (me, 0, 0))
    off_loss, off_cw = offs[-2], offs[-1]
    cw_parts = gathered[:, off_cw:, :].reshape(N_DEV, CONV_K, XBC_W)
    cw_parts = lax.dynamic_slice_in_dim(cw_parts, me * shard_cw, shard_cw, axis=2)
    cw_rows = _pack([cw_parts[0]])[0].shape[0]
    cw_parts = jnp.pad(cw_parts.reshape(N_DEV, -1), ((0, 0), (0, cw_rows * LANE - CONV_K * shard_cw))).reshape(N_DEV, cw_rows, LANE)
    parts = jnp.concatenate([gathered[:, :off_cw, :], cw_parts], axis=1)
    zero = jnp.zeros((), F32)
    packs = [_pack([d[n] for n in small] + [zero, d['conv_w']])[0] for d in (w, m, v)]
    outs = _adamw(parts, *packs, tr=parts.shape[1], name="adamw_small")

    def unpack(o, name):
        if name == 'conv_w':
            return o[off_cw:off_cw + cw_rows].reshape(-1)[:CONV_K * shard_cw].reshape(w['conv_w'].shape)
        r0 = offs[small.index(name)]
        n = w[name].size
        return o[r0:r0 + -(-n // PACK_ROW) * 8].reshape(-1)[:n].reshape(w[name].shape)

    for n in small + ['conv_w']:
        res[n] = [unpack(o, n) for o in outs]
    for n in ('w_in', 'w_out'):
        res[n] = [o[None] for o in res[n]]
    loss = outs[0][off_loss, 0]
    return (loss, grad_x[None], *[res[n][0] for n in WEIGHTS], *[res[n][1] for n in WEIGHTS],
            *[res[n][2] for n in WEIGHTS], *[res[n][3] for n in WEIGHTS])
```

```python
import functools

import numpy as np
import jax
import jax.numpy as jnp
from jax import lax
from jax.experimental import pallas as pl
from jax.experimental.pallas import tpu as pltpu

F32 = jnp.float32
BF16 = jnp.bfloat16
HI = lax.Precision.HIGHEST
MESH = pl.DeviceIdType.MESH

D = 2048
EPS = 1e-5
SGU_BLOCK = 128
SGU_GROUPS = 16
CHUNK = 64
HEADS = 32
HEADDIM = 64
SSD_GROUPS = 4
GROUP_W = D // SSD_GROUPS
STATE = 128
CONV_K = 4
XBC_W = D + 2 * SSD_GROUPS * STATE
W_IN = 15392
N_DEV = 8
SHARD_IN = W_IN // N_DEV
ADAM_LR, ADAM_B1, ADAM_B2, ADAM_EPS, ADAM_WD, ADAM_STEP = 0.001, 0.9, 0.999, 1e-08, 0.01, 10

REF_SGU_END = 3 * D
REF_GATE_START = W_IN - 2 * D
LANE = 128
DT_W = LANE
OFF_U, OFF_V, OFF_ZA, OFF_G0, OFF_G1, OFF_ZB = (i * D for i in range(6))
OFF_XBC = OFF_ZB + D
OFF_DT = OFF_XBC + XBC_W
SEG_SGU = (OFF_U, 3 * D)
SEG_GATE = (OFF_G0, 2 * D)
SEG_SSD = (OFF_ZB, D + XBC_W + DT_W)
WP = SEG_SSD[0] + SEG_SSD[1]
SSD_PAD_W = 3 * D
VMEM_BYTES = 64 * 1024 * 1024
VMEM_LIMIT = VMEM_BYTES - 8 * 1024 * 1024


def _cp(sem=None, vmem=VMEM_LIMIT):
    return pltpu.CompilerParams(dimension_semantics=sem, vmem_limit_bytes=vmem)


def _sigmoid(x):
    return 1.0 / (1.0 + jnp.exp(-x))


def _softplus(x):
    return jnp.maximum(x, 0.0) + jnp.log(1.0 + jnp.exp(-jnp.abs(x)))


def _dot(a, b, precision=None):
    return jnp.dot(a, b, preferred_element_type=F32, precision=precision)


def _dot_nt(a, b, precision=None):
    return lax.dot_general(a, b, (((1,), (1,)), ((), ())), preferred_element_type=F32, precision=precision)


def _dot_tn(a, b, precision=None):
    return lax.dot_general(a, b, (((0,), (0,)), ((), ())), preferred_element_type=F32, precision=precision)


def _split3(a):
    hi = a.astype(BF16)
    r = a - hi.astype(F32)
    mid = r.astype(BF16)
    return hi, mid, (r - mid.astype(F32)).astype(BF16)


def _sel_right(a, sel01):
    m = a.shape[0]
    r = _dot(jnp.concatenate(_split3(a), axis=0), sel01)
    return (r[0:m] + r[m:2 * m]) + r[2 * m:3 * m]


def _sel_right_k(a, sel01_x3):
    return _dot(jnp.concatenate(_split3(a), axis=1), sel01_x3)


def _sel_left(sel01, a):
    n = a.shape[1]
    r = _dot(sel01, jnp.concatenate(_split3(a), axis=1))
    return (r[:, 0:n] + r[:, n:2 * n]) + r[:, 2 * n:3 * n]


def _matmul(a, b, *, trans_a=False, trans_b=False, b_koff=0, out_dtype=F32, tm, tn, tk, add=None, after=None, name):
    K, M = a.shape if trans_a else a.shape[::-1]
    N = b.shape[0] if trans_b else b.shape[1]
    assert M % tm == 0 and N % tn == 0 and K % tk == 0 and not (trans_a and trans_b), (name, M, N, K, tm, tn, tk)
    nk = K // tk

    def body(*refs):
        a_ref, b_ref = refs[:2]
        add_ref = refs[2] if add is not None else None
        o_ref, acc_ref = refs[-2:]
        k = pl.program_id(2)
        if trans_a:
            part = _dot_tn(a_ref[...], b_ref[...])
        else:
            part = _dot_nt(a_ref[...], b_ref[...]) if trans_b else _dot(a_ref[...], b_ref[...])

        def result(r):
            if add_ref is not None:
                r = r + add_ref[...]
            return r.astype(out_dtype)

        if nk == 1:
            o_ref[...] = result(part)
        else:
            @pl.when(k == 0)
            def _():
                acc_ref[...] = part

            @pl.when(jnp.logical_and(k > 0, k < nk - 1))
            def _():
                acc_ref[...] += part

            @pl.when(k == nk - 1)
            def _():
                o_ref[...] = result(acc_ref[...] + part)

    in_specs = [pl.BlockSpec((tk, tm), lambda i, j, k: (k, i)) if trans_a else pl.BlockSpec((tm, tk), lambda i, j, k: (i, k)),
                pl.BlockSpec((tn, tk), lambda i, j, k: (j, k)) if trans_b else pl.BlockSpec((tk, tn), lambda i, j, k: (k + b_koff, j))]
    args = [a, b]
    if add is not None:
        in_specs.append(pl.BlockSpec((tm, tn), lambda i, j, k: (i, j)))
        args.append(add)
    if after is not None:
        in_specs.append(pl.BlockSpec(memory_space=pl.ANY))
        args.append(after)
    return pl.pallas_call(
        body, name=name, grid=(M // tm, N // tn, nk), in_specs=in_specs,
        out_specs=pl.BlockSpec((tm, tn), lambda i, j, k: (i, j)),
        out_shape=jax.ShapeDtypeStruct((M, N), out_dtype),
        scratch_shapes=[pltpu.VMEM((tm, tn), F32)],
        compiler_params=_cp(("parallel", "parallel", "arbitrary")),
    )(*args)


def _dxn_last_norm(seg, wpT, row0, dxn, x, dh, w, *, tm):
    S, cols = seg.shape
    assert S % tm == 0 and row0 % GROUP == 0
    ntile = S // tm

    def body(seg_hbm, w_hbm, dxn_hbm, x_hbm, dh_hbm, nw_ref, gx_hbm, dw_ref, b_ref, abuf, pbuf, xbuf, hbuf, obuf, isem, osem):
        b_copy = pltpu.make_async_copy(w_hbm.at[pl.ds(row0, cols), :], b_ref, osem.at[2])
        b_copy.start()

        def first(t):
            return t * tm if isinstance(t, int) else pl.multiple_of(t * tm, tm)

        def fetch(t):
            rows, slot = pl.ds(first(t), tm), t % 2
            return [pltpu.make_async_copy(src.at[rows, :], buf.at[slot], isem.at[slot, j])
                    for j, (src, buf) in enumerate(((seg_hbm, abuf), (dxn_hbm, pbuf), (x_hbm, xbuf), (dh_hbm, hbuf)))]

        def write(t):
            return pltpu.make_async_copy(obuf.at[t % 2], gx_hbm.at[pl.ds(first(t), tm), :], osem.at[t % 2])

        for cp in fetch(0):
            cp.start()
        dw_ref[...] = jnp.zeros_like(dw_ref)
        b_copy.wait()

        def step(t, carry):
            slot = t % 2
            for cp in fetch(t):
                cp.wait()

            @pl.when(t + 1 < ntile)
            def _():
                for cp in fetch(t + 1):
                    cp.start()

            dxn_v = pbuf[slot] + _dot(abuf[slot], b_ref[...])
            xv = xbuf[slot]
            r = lax.rsqrt(jnp.mean(xv * xv, axis=-1, keepdims=True) + EPS)
            xh = xv * r
            dxh = dxn_v * nw_ref[...]
            gx = hbuf[slot] + r * (dxh - xh * jnp.mean(dxh * xh, axis=-1, keepdims=True))
            dw_ref[0:1, :] += jnp.sum(dxn_v * xh, axis=0, keepdims=True)

            @pl.when(t >= 2)
            def _():
                write(t - 2).wait()

            obuf[slot] = gx
            write(t).start()
            return carry

        lax.fori_loop(0, ntile, step, 0)
        for t in range(max(ntile - 2, 0), ntile):
            write(t).wait()

    anyspec = pl.BlockSpec(memory_space=pl.ANY)
    vmem = pl.BlockSpec(memory_space=pltpu.VMEM)
    return pl.pallas_call(
        body, name="dxn_last_norm", in_specs=[anyspec] * 5 + [vmem], out_specs=[anyspec, vmem],
        out_shape=[jax.ShapeDtypeStruct((S, D), F32), jax.ShapeDtypeStruct((8, D), F32)],
        scratch_shapes=[pltpu.VMEM((cols, D), BF16), pltpu.VMEM((2, tm, cols), BF16), pltpu.VMEM((2, tm, D), F32),
                        pltpu.VMEM((2, tm, D), F32), pltpu.VMEM((2, tm, D), F32), pltpu.VMEM((2, tm, D), F32),
                        pltpu.SemaphoreType.DMA((2, 4)), pltpu.SemaphoreType.DMA((3,))],
        compiler_params=_cp(),
    )(seg, wpT, dxn, x, dh, w)


DW_BUFS = 3


def _dw_in(segs, xn, after, *, tm):
    S = xn.shape[0]
    n = len(segs)
    assert all(a.shape[0] == S and a.shape[1] % tm == 0 for a in segs)

    def body(*refs):
        a_refs, xn_hbm = refs[:n], refs[n]
        o_refs = refs[-(n + 5):-5]
        xn_ref, abuf, obuf, asem, osem = refs[-5:]
        xn_copy = pltpu.make_async_copy(xn_hbm, xn_ref, osem.at[2])
        xn_copy.start()
        for q, (a_ref, o_ref) in enumerate(zip(a_refs, o_refs)):
            ntile = a_ref.shape[1] // tm

            def first(t):
                return t * tm if isinstance(t, int) else pl.multiple_of(t * tm, tm)

            def fetch(t, a_ref=a_ref):
                return pltpu.make_async_copy(a_ref.at[:, pl.ds(first(t), tm)], abuf.at[t % DW_BUFS], asem.at[t % DW_BUFS])

            def write(t, o_ref=o_ref):
                return pltpu.make_async_copy(obuf.at[t % 2], o_ref.at[pl.ds(first(t), tm), :], osem.at[t % 2])

            for t in range(min(DW_BUFS - 1, ntile)):
                fetch(t).start()
            if q == 0:
                xn_copy.wait()

            def step(t, carry, fetch=fetch, write=write, ntile=ntile):
                fetch(t).wait()

                @pl.when(t + DW_BUFS - 1 < ntile)
                def _():
                    fetch(t + DW_BUFS - 1).start()

                res = _dot_tn(abuf[t % DW_BUFS], xn_ref[...]).astype(BF16)

                @pl.when(t >= 2)
                def _():
                    write(t - 2).wait()

                obuf[t % 2] = res
                write(t).start()
                return carry

            lax.fori_loop(0, ntile, step, 0)
            for t in range(max(ntile - 2, 0), ntile):
                write(t).wait()

    anyspec = pl.BlockSpec(memory_space=pl.ANY)
    return pl.pallas_call(
        body, name="dw_in", in_specs=[anyspec] * (n + 1 + (after is not None)), out_specs=[anyspec] * n,
        out_shape=[jax.ShapeDtypeStruct((a.shape[1], D), BF16) for a in segs],
        scratch_shapes=[pltpu.VMEM((S, D), BF16), pltpu.VMEM((DW_BUFS, S, tm), BF16), pltpu.VMEM((2, tm, D), BF16),
                        pltpu.SemaphoreType.DMA((DW_BUFS,)), pltpu.SemaphoreType.DMA((3,))],
        compiler_params=_cp(),
    )(*segs, xn, *([after] if after is not None else []))


def _in_proj(x, w, wpT, *, tm, tn):
    S = x.shape[0]
    N = wpT.shape[0]
    assert S % tm == 0 and N % tn == 0, (S, N, tm, tn)

    def body(x_ref, w_ref, b_ref, xn_ref, o_ref, xs_ref):
        @pl.when(pl.program_id(1) == 0)
        def _():
            xv = x_ref[...]
            r = lax.rsqrt(jnp.mean(xv * xv, axis=-1, keepdims=True) + EPS)
            xs = (xv * r * w_ref[...]).astype(BF16)
            xs_ref[...] = xs
            xn_ref[...] = xs

        o_ref[...] = _dot_nt(xs_ref[...], b_ref[...]).astype(BF16)

    return pl.pallas_call(
        body, name="in_proj", grid=(S // tm, N // tn),
        in_specs=[pl.BlockSpec((tm, D), lambda i, j: (i, 0)), pl.BlockSpec((1, D), lambda i, j: (0, 0)),
                  pl.BlockSpec((tn, D), lambda i, j: (j, 0))],
        out_specs=[pl.BlockSpec((tm, D), lambda i, j: (i, 0)), pl.BlockSpec((tm, tn), lambda i, j: (i, j))],
        out_shape=[jax.ShapeDtypeStruct((S, D), BF16), jax.ShapeDtypeStruct((S, N), BF16)],
        scratch_shapes=[pltpu.VMEM((tm, D), BF16)],
        compiler_params=_cp(("parallel", "arbitrary")),
    )(x, w, wpT)


def _sgu_core(u_ref, v_ref, z_ref, g_ref, b_ref, wm_ref, bias_ref, vnb_ref, mixed_ref, tm):
    v = v_ref[...].astype(F32)
    mu = jnp.mean(v, axis=-1, keepdims=True)
    vc = v - mu
    rs = lax.rsqrt(jnp.mean(vc * vc, axis=-1, keepdims=True) + EPS)
    vh = vc * rs
    vnb_ref[...] = (vh * g_ref[...] + b_ref[...]).astype(BF16)
    for blk in range(tm // SGU_BLOCK):
        rows = pl.ds(blk * SGU_BLOCK, SGU_BLOCK)
        for gi in range(SGU_GROUPS):
            cols = pl.ds(gi * LANE, LANE)
            mixed_ref[rows, cols] = _dot(wm_ref[gi], vnb_ref[rows, cols]) + bias_ref[:, cols]
    return vh, rs


def _sgu_fwd(proj, g, b, wm, bias_full, *, tm):
    S = proj.shape[0]

    def body(u_ref, v_ref, z_ref, g_ref, b_ref, wm_ref, bias_ref, y_ref, vnb_ref, mixed_ref):
        _sgu_core(u_ref, v_ref, z_ref, g_ref, b_ref, wm_ref, bias_ref, vnb_ref, mixed_ref, tm)
        z = z_ref[...].astype(F32)
        y_ref[...] = (u_ref[...].astype(F32) * mixed_ref[...] * (z * _sigmoid(z))).astype(BF16)

    seg = lambda off: pl.BlockSpec((tm, D), lambda i: (i, off // D))
    full = lambda a: pl.BlockSpec(a.shape, lambda i: (0,) * a.ndim)
    return pl.pallas_call(
        body, name="sgu_fwd", grid=(S // tm,),
        in_specs=[seg(OFF_U), seg(OFF_V), seg(OFF_ZA), full(g), full(b), full(wm), full(bias_full)],
        out_specs=pl.BlockSpec((tm, D), lambda i: (i, 0)),
        out_shape=jax.ShapeDtypeStruct((S, D), BF16),
        scratch_shapes=[pltpu.VMEM((tm, D), BF16), pltpu.VMEM((tm, D), F32)],
        compiler_params=_cp(("parallel",)),
    )(proj, proj, proj, g, b, wm, bias_full)


def _sgu_bwd(proj, dy, g, b, wm, wmT, bias_full, mask, sel, *, tm):
    S = proj.shape[0]
    nsteps = S // tm

    def body(u_ref, v_ref, z_ref, dy_ref, g_ref, b_ref, wm_ref, wmT_ref, bias_ref, mask_ref, sel_ref,
             dp_ref, dws_ref, dbs_ref, dg_ref, db_ref, vnb_ref, mixed_ref, dmb_ref, dvn_ref, dbias_ref):
        i = pl.program_id(0)

        @pl.when(i == 0)
        def _():
            dws_ref[...] = jnp.zeros_like(dws_ref)
            dg_ref[...] = jnp.zeros_like(dg_ref)
            db_ref[...] = jnp.zeros_like(db_ref)
            dbias_ref[...] = jnp.zeros_like(dbias_ref)

        vh, rs = _sgu_core(u_ref, v_ref, z_ref, g_ref, b_ref, wm_ref, bias_ref, vnb_ref, mixed_ref, tm)
        u = u_ref[...].astype(F32)
        z = z_ref[...].astype(F32)
        dy_v = dy_ref[...].astype(F32)
        mixed = mixed_ref[...]
        sg = _sigmoid(z)
        sz = z * sg
        dp_ref[:, 0:D] = (dy_v * mixed * sz).astype(BF16)
        dp_ref[:, 2 * D:3 * D] = (dy_v * u * mixed * (sg * (1.0 + z * (1.0 - sg)))).astype(BF16)
        dmixed = dy_v * u * sz
        dmb_ref[...] = dmixed.astype(BF16)
        for blk in range(tm // SGU_BLOCK):
            dbias_ref[...] += dmixed[blk * SGU_BLOCK:(blk + 1) * SGU_BLOCK, :]
        for blk in range(tm // SGU_BLOCK):
            rows = pl.ds(blk * SGU_BLOCK, SGU_BLOCK)
            for gi in range(SGU_GROUPS):
                cols = pl.ds(gi * LANE, LANE)
                dm = dmb_ref[rows, cols]
                dvn_ref[rows, cols] = _dot(wmT_ref[gi], dm)
                dws_ref[gi] += _dot_nt(dm, vnb_ref[rows, cols])
        dvn = dvn_ref[...]
        dg_ref[0:1, :] += jnp.sum(dvn * vh, axis=0, keepdims=True)
        db_ref[0:1, :] += jnp.sum(dvn, axis=0, keepdims=True)
        dvh = dvn * g_ref[...]
        dv = rs * (dvh - jnp.mean(dvh, axis=-1, keepdims=True) - vh * jnp.mean(dvh * vh, axis=-1, keepdims=True))
        dp_ref[:, D:2 * D] = dv.astype(BF16)

        @pl.when(i == nsteps - 1)
        def _():
            for gi in range(SGU_GROUPS):
                dws_ref[gi] = dws_ref[gi] * mask_ref[...]
            dbs_ref[...] = _dot(dbias_ref[...], sel_ref[...], precision=HI)

    seg = lambda off: pl.BlockSpec((tm, D), lambda i: (i, off // D))
    full = lambda a: pl.BlockSpec(a.shape, lambda i: (0,) * a.ndim)
    return pl.pallas_call(
        body, name="sgu_bwd", grid=(nsteps,),
        in_specs=[seg(OFF_U), seg(OFF_V), seg(OFF_ZA), pl.BlockSpec((tm, D), lambda i: (i, 0)),
                  full(g), full(b), full(wm), full(wmT), full(bias_full), full(mask), full(sel)],
        out_specs=[pl.BlockSpec((tm, 3 * D), lambda i: (i, 0)),
                   pl.BlockSpec((SGU_GROUPS, SGU_BLOCK, SGU_BLOCK), lambda i: (0, 0, 0)),
                   pl.BlockSpec((SGU_BLOCK, LANE), lambda i: (0, 0)),
                   pl.BlockSpec((8, D), lambda i: (0, 0)), pl.BlockSpec((8, D), lambda i: (0, 0))],
        out_shape=[jax.ShapeDtypeStruct((S, 3 * D), BF16),
                   jax.ShapeDtypeStruct((SGU_GROUPS, SGU_BLOCK, SGU_BLOCK), F32),
                   jax.ShapeDtypeStruct((SGU_BLOCK, LANE), F32),
                   jax.ShapeDtypeStruct((8, D), F32), jax.ShapeDtypeStruct((8, D), F32)],
        scratch_shapes=[pltpu.VMEM((tm, D), BF16), pltpu.VMEM((tm, D), F32), pltpu.VMEM((tm, D), BF16),
                        pltpu.VMEM((tm, D), F32), pltpu.VMEM((SGU_BLOCK, D), F32)],
        compiler_params=_cp(("arbitrary",)),
    )(proj, proj, proj, dy, g, b, wm, wmT, bias_full, mask, sel)


SSD_T = 2 * CHUNK
HALO = 8
HALO_BLK = 16


def _pair_masks():
    row = lax.broadcasted_iota(jnp.int32, (CHUNK, LANE), 0)
    lane = lax.broadcasted_iota(jnp.int32, (CHUNK, LANE), 1)
    pos = jnp.where(lane >= CHUNK, lane - CHUNK, lane)
    diag = (row == pos).astype(F32)
    causal = row >= pos
    lo = (lane < CHUNK).astype(F32)
    return diag, causal, lo, 1.0 - lo


def _ssd_chunk_fwd(c, ext_ref, shift_ref, dt_ref, cw_ref, cb_ref, dtb_ref, alog_ref, tri_ref, exp_ref):
    r0 = c * CHUNK
    win = ext_ref[pl.ds(r0, HALO_BLK + CHUNK), :]
    sh = _dot(shift_ref[...], win)
    taps = [sh[k * CHUNK:(k + 1) * CHUNK] for k in range(CONV_K - 1)] + [win[HALO_BLK:].astype(F32)]
    pre = cb_ref[...] + sum(cw_ref[k:k + 1, :] * taps[k] for k in range(CONV_K))
    sg = _sigmoid(pre)
    xc = pre * sg
    dtr = dt_ref[pl.ds(r0, CHUNK), :].astype(F32) + dtb_ref[...]
    dtv = _softplus(dtr)
    A = -jnp.exp(alog_ref[...])
    acs = _sel_left(tri_ref[...], dtv * A)
    both = _sel_right_k(jnp.concatenate([acs, dtv], axis=0), exp_ref[...])
    E, dtE = both[0:CHUNK], both[CHUNK:2 * CHUNK]
    return dict(taps=taps, pre=pre, sg=sg, xc=xc, dtr=dtr, dtv=dtv, A=A, E=E, dtE=dtE)


def _ssd_fwd(proj, conv_w, conv_b, dtb_p, alog_p, d_exp, norm_w, tri, expand, shift):
    S = proj.shape[0]
    T = SSD_T
    nsteps = S // T
    ncl = T // CHUNK

    def body(zb_ref, xbc_ref, halo_ref, dt_ref, cw_ref, cb_ref, dtb_ref, alog_ref, dexp_ref, nw_ref, tri_ref, exp_ref, shift_ref,
             y_ref, yb_ref, st_ref, ht_ref, ext_ref):
        i = pl.program_id(0)

        @pl.when(i == 0)
        def _():
            ht_ref[...] = jnp.zeros_like(ht_ref)
            ext_ref[0:HALO_BLK, :] = jnp.zeros((HALO_BLK, XBC_W), BF16)

        @pl.when(i > 0)
        def _():
            ext_ref[0:HALO_BLK, :] = halo_ref[...]

        ext_ref[HALO_BLK:HALO_BLK + T, :] = xbc_ref[...]
        diag, causal, lo, hi = _pair_masks()
        for c in range(ncl):
            q = _ssd_chunk_fwd(c, ext_ref, shift_ref, dt_ref, cw_ref, cb_ref, dtb_ref, alog_ref, tri_ref, exp_ref)
            rows = pl.ds(c * CHUNK, CHUNK)
            xc, E, dtE = q["xc"], q["E"], q["dtE"]
            xs = xc[:, 0:D]
            total = E[CHUNK - 1:CHUNK, :]
            x_dt = xs * dtE
            eE = jnp.exp(E)
            xw = x_dt * jnp.exp(total - E)
            st_ref[c] = ht_ref[...]
            for g in range(SSD_GROUPS):
                gc = slice(g * GROUP_W, (g + 1) * GROUP_W)
                Bg = xc[:, D + g * STATE:D + (g + 1) * STATE].astype(BF16)
                Cg = xc[:, D + SSD_GROUPS * STATE + g * STATE:D + SSD_GROUPS * STATE + (g + 1) * STATE].astype(BF16)
                cb2 = _dot_nt(Cg, jnp.concatenate([Bg, Bg], axis=0))
                htg = ht_ref[:, gc]
                y_ref[rows, gc] = eE[:, gc] * _dot(Cg, htg.astype(BF16)) + xs[:, gc] * dexp_ref[:, gc]
                for jj in range(GROUP_W // LANE):
                    pc = slice(g * GROUP_W + jj * LANE, g * GROUP_W + (jj + 1) * LANE)
                    Ej = E[:, pc]
                    e2 = jnp.sum(Ej * diag, axis=0, keepdims=True)
                    Mp = cb2 * jnp.exp(jnp.where(causal, Ej - e2, -1e30))
                    xj = x_dt[:, pc]
                    xbd = jnp.concatenate([xj * lo, xj * hi], axis=0).astype(BF16)
                    y_ref[rows, pc] += _dot(Mp.astype(BF16), xbd)
                ht_ref[:, gc] = jnp.exp(total[:, gc]) * htg + _dot_tn(Bg, xw[:, gc].astype(BF16))
            zb = zb_ref[rows, :].astype(F32)
            hh = y_ref[rows, :] * (zb * _sigmoid(zb))
            for g in range(SSD_GROUPS):
                gc = slice(g * GROUP_W, (g + 1) * GROUP_W)
                hg = hh[:, gc]
                r = lax.rsqrt(jnp.mean(hg * hg, axis=-1, keepdims=True) + EPS)
                yb_ref[rows, gc] = (hg * r * nw_ref[:, gc]).astype(BF16)

    full = lambda a: pl.BlockSpec(a.shape, lambda i: (0,) * a.ndim)
    hb = T // HALO_BLK
    return pl.pallas_call(
        body, name="ssd_fwd", grid=(nsteps,),
        in_specs=[pl.BlockSpec((T, D), lambda i: (i, OFF_ZB // D)),
                  pl.BlockSpec((T, XBC_W), lambda i: (i, OFF_XBC // XBC_W)),
                  pl.BlockSpec((HALO_BLK, XBC_W), lambda i: (jnp.maximum(i * hb - 1, 0), OFF_XBC // XBC_W)),
                  pl.BlockSpec((T, DT_W), lambda i: (i, OFF_DT // DT_W)),
                  full(conv_w), full(conv_b), full(dtb_p), full(alog_p), full(d_exp), full(norm_w), full(tri), full(expand),
                  full(shift)],
        out_specs=[pl.BlockSpec((T, D), lambda i: (i, 0)), pl.BlockSpec((T, D), lambda i: (i, 0)),
                   pl.BlockSpec((ncl, STATE, D), lambda i: (i, 0, 0))],
        out_shape=[jax.ShapeDtypeStruct((S, D), F32), jax.ShapeDtypeStruct((S, D), BF16),
                   jax.ShapeDtypeStruct((S // CHUNK, STATE, D), F32)],
        scratch_shapes=[pltpu.VMEM((STATE, D), F32), pltpu.VMEM((HALO_BLK + T, XBC_W), BF16)],
        compiler_params=_cp(("arbitrary",)),
    )(proj, proj, proj, proj, conv_w, conv_b, dtb_p, alog_p, d_exp, norm_w, tri, expand, shift)


def _ssd_bwd(proj, dyb, y, states, conv_w, conv_b, dtb_p, alog_p, d_exp, norm_w, tri, triT, expand, expandT, shift):
    S = proj.shape[0]
    T = SSD_T
    nsteps = S // T
    ncl = T // CHUNK
    SSD_W = SSD_PAD_W

    def body(zb_ref, xbc_ref, halo_ref, dt_ref, dyb_ref, y_ref, st_ref, cw_ref, cb_ref, dtb_ref, alog_ref, dexp_ref, nw_ref,
             tri_ref, triT_ref, exp_ref, expT_ref, shift_ref,
             dp_ref, dcw_ref, dcb_ref, ddtb_ref, dalog_ref, dD_ref, dnw_ref,
             dht_ref, ext_ref, dpre_ref, dy_s, dE_s, dxdt_s, dxc_s, dDacc_ref, dAacc_ref):
        i = pl.program_id(0)

        @pl.when(i == 0)
        def _():
            for r in (dht_ref, dcw_ref, dcb_ref, ddtb_ref, dnw_ref, dDacc_ref, dAacc_ref):
                r[...] = jnp.zeros_like(r)
            dpre_ref[T:T + HALO_BLK, :] = jnp.zeros((HALO_BLK, XBC_W), F32)

        @pl.when(i == nsteps - 1)
        def _():
            ext_ref[0:HALO_BLK, :] = jnp.zeros((HALO_BLK, XBC_W), BF16)

        @pl.when(i < nsteps - 1)
        def _():
            ext_ref[0:HALO_BLK, :] = halo_ref[...]

        ext_ref[HALO_BLK:HALO_BLK + T, :] = xbc_ref[...]
        diag, causal, lo, hi = _pair_masks()
        last_row = (lax.broadcasted_iota(jnp.int32, (CHUNK, 1), 0) == CHUNK - 1).astype(F32)
        for c in reversed(range(ncl)):
            q = _ssd_chunk_fwd(c, ext_ref, shift_ref, dt_ref, cw_ref, cb_ref, dtb_ref, alog_ref, tri_ref, exp_ref)
            rows = pl.ds(c * CHUNK, CHUNK)
            pre, sg, xc, dtr, dtv, A, E, dtE = (q[k] for k in ("pre", "sg", "xc", "dtr", "dtv", "A", "E", "dtE"))
            xs = xc[:, 0:D]
            total = E[CHUNK - 1:CHUNK, :]
            x_dt = xs * dtE
            eE = jnp.exp(E)
            wdec = jnp.exp(total - E)
            zb = zb_ref[rows, :].astype(F32)
            yv = y_ref[rows, :]
            sgz = _sigmoid(zb)
            sz = zb * sgz
            hh = yv * sz
            for g in range(SSD_GROUPS):
                gc = slice(g * GROUP_W, (g + 1) * GROUP_W)
                hg = hh[:, gc]
                r = lax.rsqrt(jnp.mean(hg * hg, axis=-1, keepdims=True) + EPS)
                dyb_g = dyb_ref[rows, gc].astype(F32)
                dn = dyb_g * nw_ref[:, gc]
                dnw_ref[0:1, gc] += jnp.sum(dyb_g * hg * r, axis=0, keepdims=True)
                dy_s[:, gc] = r * dn - hg * (r * r * r) * jnp.mean(dn * hg, axis=-1, keepdims=True)
            dhh = dy_s[...]
            dp_ref[rows, 0:D] = (dhh * yv * (sgz * (1.0 + zb * (1.0 - sgz)))).astype(BF16)
            dy = dhh * sz
            dy_s[...] = dy
            dDacc_ref[0:1, :] += jnp.sum(dy * xs, axis=0, keepdims=True)
            dxc_s[:, 0:D] = dy * dexp_ref[...]
            for g in range(SSD_GROUPS):
                gc = slice(g * GROUP_W, (g + 1) * GROUP_W)
                bcol = slice(D + g * STATE, D + (g + 1) * STATE)
                ccol = slice(D + SSD_GROUPS * STATE + g * STATE, D + SSD_GROUPS * STATE + (g + 1) * STATE)
                Bg = xc[:, bcol].astype(BF16)
                Cg = xc[:, ccol].astype(BF16)
                B2 = jnp.concatenate([Bg, Bg], axis=0)
                cb2 = _dot_nt(Cg, B2)
                htg = st_ref[c, :, gc]
                htb = htg.astype(BF16)
                dhn = dht_ref[:, gc]
                dhnb = dhn.astype(BF16)
                dyg = dy[:, gc]
                eEg = eE[:, gc]
                wg = wdec[:, gc]
                xdg = x_dt[:, gc]
                CH = _dot(Cg, htb)
                dCHb = (dyg * eEg).astype(BF16)
                dC = _dot_nt(dCHb, htb)
                dl = jnp.exp(total[:, gc])
                dht_prev = _dot_tn(Cg, dCHb) + dl * dhn
                dtot = jnp.sum(dhn * htg, axis=0, keepdims=True) * dl
                dxw = _dot(Bg, dhnb)
                dB = _dot_nt((xdg * wg).astype(BF16), dhnb)
                dwd = dxw * xdg * wg
                dtot = dtot + jnp.sum(dwd, axis=0, keepdims=True)
                dE_s[:, gc] = dyg * eEg * CH - dwd + last_row * dtot
                dxdt_s[:, gc] = dxw * wg
                dcb2 = jnp.zeros((CHUNK, LANE), F32)
                for jj in range(GROUP_W // LANE):
                    pc = slice(g * GROUP_W + jj * LANE, g * GROUP_W + (jj + 1) * LANE)
                    Ej = E[:, pc]
                    e2 = jnp.sum(Ej * diag, axis=0, keepdims=True)
                    Lp = jnp.exp(jnp.where(causal, Ej - e2, -1e30))
                    Mp = cb2 * Lp
                    xj = x_dt[:, pc]
                    xbd = jnp.concatenate([xj * lo, xj * hi], axis=0).astype(BF16)
                    dyj = dy[:, pc].astype(BF16)
                    dMp = _dot_nt(dyj, xbd)
                    dxbd = _dot_tn(Mp.astype(BF16), dyj)
                    dxdt_s[:, pc] += dxbd[0:CHUNK, :] * lo + dxbd[CHUNK:2 * CHUNK, :] * hi
                    dcb2 = dcb2 + dMp * Lp
                    dseg = dMp * Mp
                    dE_s[:, pc] += dseg - diag * jnp.sum(dseg, axis=0, keepdims=True)
                dcb2b = dcb2.astype(BF16)
                dC = dC + _dot(dcb2b, B2)
                dB2 = _dot_tn(dcb2b, Cg)
                dB = dB + dB2[0:CHUNK, :] + dB2[CHUNK:2 * CHUNK, :]
                dxc_s[:, bcol] = dB
                dxc_s[:, ccol] = dC
                dht_ref[:, gc] = dht_prev
            dx_dt = dxdt_s[...]
            dxc_s[:, 0:D] += dx_dt * dtE
            red = _sel_right(jnp.concatenate([dE_s[...], dx_dt * xs], axis=0), expT_ref[...])
            da = _sel_left(triT_ref[...], red[0:CHUNK, :])
            ddtv = red[CHUNK:2 * CHUNK, :] + da * A
            dAacc_ref[0:1, :] += jnp.sum(da * dtv, axis=0, keepdims=True)
            ddtr = ddtv * _sigmoid(dtr)
            ddtb_ref[0:1, :] += jnp.sum(ddtr, axis=0, keepdims=True)
            dp_ref[rows, D + XBC_W:D + XBC_W + DT_W] = ddtr.astype(BF16)
            dpre = dxc_s[...] * (sg * (1.0 + pre * (1.0 - sg)))
            dpre_ref[rows, :] = dpre
            dcb_ref[0:1, :] += jnp.sum(dpre, axis=0, keepdims=True)
            for k in range(CONV_K):
                dcw_ref[k:k + 1, :] += jnp.sum(dpre * q["taps"][k], axis=0, keepdims=True)
        dxbc = jnp.zeros((T, XBC_W), F32)
        for k in range(CONV_K):
            dxbc = dxbc + cw_ref[k:k + 1, :] * dpre_ref[pl.ds(CONV_K - 1 - k, T), :]
        dp_ref[:, D:D + XBC_W] = dxbc.astype(BF16)
        dp_ref[:, SEG_SSD[1]:SSD_W] = jnp.zeros((T, SSD_W - SEG_SSD[1]), BF16)
        dpre_ref[T:T + HALO, :] = dpre_ref[0:HALO, :]

        @pl.when(i == nsteps - 1)
        def _():
            dalog_ref[...] = dAacc_ref[...] * (-jnp.exp(alog_ref[...]))
            dD_ref[...] = _dot(dDacc_ref[...], expT_ref[...].astype(F32), precision=HI)

    full = lambda a: pl.BlockSpec(a.shape, lambda i: (0,) * a.ndim)
    hb = T // HALO_BLK
    rev = lambda i: nsteps - 1 - i
    acc = lambda w: pl.BlockSpec((8, w), lambda i: (0, 0))
    return pl.pallas_call(
        body, name="ssd_bwd", grid=(nsteps,),
        in_specs=[pl.BlockSpec((T, D), lambda i: (rev(i), OFF_ZB // D)),
                  pl.BlockSpec((T, XBC_W), lambda i: (rev(i), OFF_XBC // XBC_W)),
                  pl.BlockSpec((HALO_BLK, XBC_W), lambda i: (jnp.maximum(rev(i) * hb - 1, 0), OFF_XBC // XBC_W)),
                  pl.BlockSpec((T, DT_W), lambda i: (rev(i), OFF_DT // DT_W)),
                  pl.BlockSpec((T, D), lambda i: (rev(i), 0)), pl.BlockSpec((T, D), lambda i: (rev(i), 0)),
                  pl.BlockSpec((ncl, STATE, D), lambda i: (rev(i), 0, 0)),
                  full(conv_w), full(conv_b), full(dtb_p), full(alog_p), full(d_exp), full(norm_w),
                  full(tri), full(triT), full(expand), full(expandT), full(shift)],
        out_specs=[pl.BlockSpec((T, SSD_W), lambda i: (rev(i), 0)),
                   acc(XBC_W), acc(XBC_W), acc(DT_W), acc(DT_W), acc(DT_W), acc(D)],
        out_shape=[jax.ShapeDtypeStruct((S, SSD_W), BF16),
                   jax.ShapeDtypeStruct((8, XBC_W), F32), jax.ShapeDtypeStruct((8, XBC_W), F32),
                   jax.ShapeDtypeStruct((8, DT_W), F32), jax.ShapeDtypeStruct((8, DT_W), F32),
                   jax.ShapeDtypeStruct((8, DT_W), F32), jax.ShapeDtypeStruct((8, D), F32)],
        scratch_shapes=[pltpu.VMEM((STATE, D), F32), pltpu.VMEM((HALO_BLK + T, XBC_W), BF16), pltpu.VMEM((T + HALO_BLK, XBC_W), F32),
                        pltpu.VMEM((CHUNK, D), F32), pltpu.VMEM((CHUNK, D), F32), pltpu.VMEM((CHUNK, D), F32),
                        pltpu.VMEM((CHUNK, XBC_W), F32), pltpu.VMEM((8, D), F32), pltpu.VMEM((8, DT_W), F32)],
        compiler_params=_cp(("arbitrary",)),
    )(proj, proj, proj, proj, dyb, y, states, conv_w, conv_b, dtb_p, alog_p, d_exp, norm_w, tri, triT, expand, expandT, shift)


def _head(x, ya, yb, proj, target, gate_b, wout, fw, *, tm):
    S = x.shape[0]

    def body(x_ref, ya_ref, yb_ref, gl0_ref, gl1_ref, t_ref, gb_ref, w_ref, fw_ref,
             dh_ref, dhb_ref, mb_ref, dya_ref, dyb_ref, dgl_ref, loss_ref, dfw_ref, dgb_ref):
        @pl.when(pl.program_id(0) == 0)
        def _():
            loss_ref[...] = jnp.zeros_like(loss_ref)
            dfw_ref[...] = jnp.zeros_like(dfw_ref)
            dgb_ref[...] = jnp.zeros_like(dgb_ref)

        ya_v = ya_ref[...].astype(F32)
        yb_v = yb_ref[...].astype(F32)
        g0 = _sigmoid(gl0_ref[...].astype(F32) + gb_ref[:, 0:D])
        g1 = _sigmoid(gl1_ref[...].astype(F32) + gb_ref[:, D:2 * D])
        mb = (g0 * ya_v + g1 * yb_v).astype(BF16)
        mb_ref[...] = mb
        h = x_ref[...] + _dot(mb, w_ref[...])
        r = lax.rsqrt(jnp.mean(h * h, axis=-1, keepdims=True) + EPS)
        hn = h * r
        err = hn * fw_ref[...] - t_ref[...]
        loss_ref[...] += 0.5 * jnp.sum(jnp.mean(err * err, axis=-1, keepdims=True))
        dyf = err * (1.0 / D)
        dfw_ref[0:1, :] += jnp.sum(dyf * hn, axis=0, keepdims=True)
        dhn = dyf * fw_ref[...]
        dh = r * (dhn - hn * jnp.mean(dhn * hn, axis=-1, keepdims=True))
        dh_ref[...] = dh
        dhb = dh.astype(BF16)
        dhb_ref[...] = dhb
        dm = _dot_nt(dhb, w_ref[...])
        dya_ref[...] = (dm * g0).astype(BF16)
        dyb_ref[...] = (dm * g1).astype(BF16)
        dgl0 = dm * ya_v * g0 * (1.0 - g0)
        dgl1 = dm * yb_v * g1 * (1.0 - g1)
        dgl_ref[:, 0:D] = dgl0.astype(BF16)
        dgl_ref[:, D:2 * D] = dgl1.astype(BF16)
        dgb_ref[0:1, 0:D] += jnp.sum(dgl0, axis=0, keepdims=True)
        dgb_ref[0:1, D:2 * D] += jnp.sum(dgl1, axis=0, keepdims=True)

    row = pl.BlockSpec((tm, D), lambda i: (i, 0))
    seg = lambda off: pl.BlockSpec((tm, D), lambda i: (i, off // D))
    full = lambda a: pl.BlockSpec(a.shape, lambda i: (0,) * a.ndim)
    acc = lambda w: pl.BlockSpec((8, w), lambda i: (0, 0))
    return pl.pallas_call(
        body, name="head", grid=(S // tm,),
        in_specs=[row, row, row, seg(OFF_G0), seg(OFF_G1), row, full(gate_b), full(wout), full(fw)],
        out_specs=[row, row, row, row, row, pl.BlockSpec((tm, 2 * D), lambda i: (i, 0)), acc(LANE), acc(D), acc(2 * D)],
        out_shape=[jax.ShapeDtypeStruct((S, D), F32), jax.ShapeDtypeStruct((S, D), BF16), jax.ShapeDtypeStruct((S, D), BF16),
                   jax.ShapeDtypeStruct((S, D), BF16), jax.ShapeDtypeStruct((S, D), BF16), jax.ShapeDtypeStruct((S, 2 * D), BF16),
                   jax.ShapeDtypeStruct((8, LANE), F32), jax.ShapeDtypeStruct((8, D), F32), jax.ShapeDtypeStruct((8, 2 * D), F32)],
        compiler_params=_cp(("arbitrary",)),
    )(x, ya, yb, proj, proj, target, gate_b, wout, fw)


def _adam_update(g, w_ref, m_ref, v_ref, g_ref, d_ref, m2_ref, v2_ref):
    m2 = ADAM_B1 * m_ref[...] + (1.0 - ADAM_B1) * g
    v2 = ADAM_B2 * v_ref[...] + (1.0 - ADAM_B2) * (g * g)
    m_hat = m2 / (1.0 - ADAM_B1 ** ADAM_STEP)
    v_hat = v2 / (1.0 - ADAM_B2 ** ADAM_STEP)
    g_ref[...] = g
    d_ref[...] = -ADAM_LR * (m_hat / (jnp.sqrt(v_hat) + ADAM_EPS) + ADAM_WD * w_ref[...])
    m2_ref[...] = m2
    v2_ref[...] = v2


def _adamw_own(me, own, landed, w, m, v, *, tr, tc, name):
    _, R, C = landed.shape
    assert R % tr == 0 and C % tc == 0, (name, R, C, tr, tc)

    def body(me_ref, own_ref, p_ref, w_ref, m_ref, v_ref, g_ref, d_ref, m2_ref, v2_ref):
        mine = own_ref[0].astype(F32)
        g = jnp.where(me_ref[0] == 0, mine, p_ref[0].astype(F32))
        for k in range(1, N_DEV):
            g = g + jnp.where(me_ref[0] == k, mine, p_ref[k].astype(F32))
        _adam_update(g, w_ref, m_ref, v_ref, g_ref, d_ref, m2_ref, v2_ref)

    tile = pl.BlockSpec((tr, tc), lambda i, j, me_ref: (i, j))
    return pl.pallas_call(
        body, name=name,
        grid_spec=pltpu.PrefetchScalarGridSpec(
            num_scalar_prefetch=1, grid=(R // tr, C // tc),
            in_specs=[pl.BlockSpec((1, tr, tc), lambda i, j, me_ref: (me_ref[0], i, j)),
                      pl.BlockSpec((N_DEV, tr, tc), lambda i, j, me_ref: (0, i, j)), tile, tile, tile],
            out_specs=[tile, tile, tile, tile]),
        out_shape=[jax.ShapeDtypeStruct((R, C), F32)] * 4,
        compiler_params=_cp(("parallel", "parallel")),
    )(me, own, landed, w, m, v)


def _adamw(parts, w, m, v, *, tr, name):
    _, R, C = parts.shape
    assert R % tr == 0, (name, R, tr)

    def body(p_ref, w_ref, m_ref, v_ref, g_ref, d_ref, m2_ref, v2_ref):
        g = p_ref[0].astype(F32)
        for k in range(1, N_DEV):
            g = g + p_ref[k].astype(F32)
        _adam_update(g, w_ref, m_ref, v_ref, g_ref, d_ref, m2_ref, v2_ref)

    row = pl.BlockSpec((tr, C), lambda i: (i, 0))
    return pl.pallas_call(
        body, name=name, grid=(R // tr,),
        in_specs=[pl.BlockSpec((N_DEV, tr, C), lambda i: (0, i, 0)), row, row, row],
        out_specs=[row, row, row, row],
        out_shape=[jax.ShapeDtypeStruct((R, C), F32)] * 4,
        compiler_params=_cp(("parallel",)),
    )(parts, w, m, v)


def _place():
    x, y, c = lax.axis_index("x"), lax.axis_index("y"), lax.axis_index("c")
    return x, y, c


def _all_gather(arrs, *, name):
    n = len(arrs)

    def body(*refs):
        ins, outs = refs[:n], refs[n:2 * n]
        send_sems, recv_sems, local_sems = refs[2 * n:]
        x, y, c = _place()
        me, sibling = (x, y, c), (x, y, 1 - c)
        chips = [(1 - x, y), (x, 1 - y), (1 - x, 1 - y)]

        def idx(px, py, pc):
            return 4 * px + 2 * py + pc

        def copy(k, a, block, to, src=None):
            slab = outs[a].at[idx(*block)]
            return pltpu.make_async_remote_copy(
                src_ref=slab if src is None else src, dst_ref=slab,
                send_sem=send_sems.at[k, a], recv_sem=recv_sems.at[k, a], device_id=to, device_id_type=MESH)

        mine = [pltpu.make_async_copy(ins[a], outs[a].at[idx(*me)], local_sems.at[a]) for a in range(n)]
        for cp in mine:
            cp.start()
        first = []
        for a in range(n):
            first.append(copy(0, a, me, sibling, src=ins[a]))
            first += [copy(1 + j, a, me, (*chip, c), src=ins[a]) for j, chip in enumerate(chips)]
        for cp in first:
            cp.start()
        passed = []
        for j, chip in enumerate(chips):
            for a in range(n):
                copy(1 + j, a, (*chip, c), me).wait_recv()
                fwd = copy(4 + j, a, (*chip, c), sibling)
                fwd.start()
                passed.append(fwd)
        for a in range(n):
            copy(0, a, sibling, me).wait_recv()
            for j, chip in enumerate(chips):
                copy(4 + j, a, (*chip, 1 - c), me).wait_recv()
        for cp in first + passed:
            cp.wait_send()
        for cp in mine:
            cp.wait()

    anyspec = pl.BlockSpec(memory_space=pl.ANY)
    return pl.pallas_call(
        body, name=name,
        in_specs=[anyspec] * n, out_specs=[anyspec] * n,
        out_shape=[jax.ShapeDtypeStruct((N_DEV,) + a.shape, a.dtype) for a in arrs],
        scratch_shapes=[pltpu.SemaphoreType.DMA((7, n)), pltpu.SemaphoreType.DMA((7, n)), pltpu.SemaphoreType.DMA((n,))],
    )(*arrs)


W_ROWS = SEG_SSD[0] + SSD_PAD_W


GROUP = 16
INTERIOR = 1920


def _interior(k):
    lo = -(-(k * SHARD_IN) // GROUP) * GROUP
    hi = ((k + 1) * SHARD_IN) // GROUP * GROUP
    return lo, hi


def _dest_row(r):
    if r < REF_SGU_END:
        return r
    return r - REF_SGU_END + SEG_SSD[0] if r < REF_GATE_START else r - REF_GATE_START + SEG_GATE[0]


def _shard_pieces(k):
    lo_k, hi_k = _interior(k)
    out = []
    for lo, hi in ((0, REF_SGU_END), (REF_SGU_END, REF_GATE_START), (REF_GATE_START, W_IN)):
        a, b = max(lo, lo_k), min(hi, hi_k)
        if a < b:
            out.append((a - lo_k, b - a, _dest_row(a)))
    return out


GATHER_PARTS = 1


def _shard_parts(k):
    parts = [[] for _ in range(GATHER_PARTS)]
    for s0, n, d0 in _shard_pieces(k):
        step = -(-(n // GROUP) // GATHER_PARTS) * GROUP
        for p in range(GATHER_PARTS):
            a, b = min(p * step, n), min((p + 1) * step, n)
            if a < b:
                parts[p].append((s0 + a, b - a, d0 + a))
    return parts


def _patch_straddlers(wpT, heads, tails):
    for k in range(1, N_DEV):
        m = (k * SHARD_IN) % GROUP
        if m:
            group = jnp.concatenate([tails[k - 1, GROUP - m:], heads[k, :GROUP - m]], axis=0)
            wpT = lax.dynamic_update_slice(wpT, group, (_dest_row(k * SHARD_IN - m), 0))
    return wpT


def _gather_stages(k, win_ref, small, z_ref, n_zero, w_ref, send_sems, recv_sems, local_sems):
    x, y, c = k // 4, (k // 2) % 2, k % 2
    idx = lambda p: 4 * p[0] + 2 * p[1] + p[2]
    me, sib = (x, y, c), (x, y, 1 - c)
    xn, yn, dg = (1 - x, y, c), (x, 1 - y, c), (1 - x, 1 - y, c)
    parts = range(GATHER_PARTS)

    def copies(slot, block, to, part, own=False):
        kb = idx(block)
        out = []
        for j, (s0, n, d0) in enumerate(_shard_parts(kb)[part]):
            dst = w_ref.at[pl.ds(d0, n)]
            out.append((win_ref.at[pl.ds(s0, n)] if own else dst, dst, 2 * part + j))
        if part == 0:
            for j, (src, gathered) in enumerate(small):
                out.append((src if own else gathered.at[kb], gathered.at[kb], 2 * GATHER_PARTS + j))
        return [pltpu.make_async_remote_copy(src_ref=s, dst_ref=d, send_sem=send_sems.at[slot, j], recv_sem=recv_sems.at[slot, j],
                                             device_id=to, device_id_type=MESH) for s, d, j in out]

    def start(cps):
        for cp in cps:
            cp.start()

    def arrived(slot, block, part):
        for cp in copies(slot, block, me, part):
            cp.wait_recv()

    def local():
        pairs = [(win_ref.at[pl.ds(s0, n)], w_ref.at[pl.ds(d0, n)]) for s0, n, d0 in _shard_pieces(k)]
        pairs += [(src, gathered.at[k]) for src, gathered in small] + [(z_ref, w_ref.at[pl.ds(W_IN, n_zero)])]
        return [pltpu.make_async_copy(s, d, local_sems.at[j]) for j, (s, d) in enumerate(pairs)]

    relay = (xn, yn) if c == 1 else (yn, xn)

    def first():
        start(local())
        for p in parts:
            start(copies(0, me, sib, p, own=True) + copies(1, me, xn, p, own=True) + copies(2, me, yn, p, own=True))

    def hand_on():
        for p in parts:
            arrived(1, xn, p)
            start(copies(4, xn, sib, p))
            if c == 1:
                start(copies(3, *relay, p))
            arrived(2, yn, p)
            start(copies(5, yn, sib, p))
            if c == 0:
                start(copies(3, *relay, p))

    def finish():
        for p in parts:
            arrived(3, dg, p)
            start(copies(6, dg, sib, p))
        for p in parts:
            arrived(0, sib, p)
            arrived(4, (1 - x, y, 1 - c), p)
            arrived(5, (x, 1 - y, 1 - c), p)
            arrived(6, (1 - x, 1 - y, 1 - c), p)
        for p in parts:
            sent = (copies(0, me, sib, p, own=True) + copies(1, me, xn, p, own=True) + copies(2, me, yn, p, own=True)
                    + copies(3, *relay, p) + copies(4, xn, sib, p) + copies(5, yn, sib, p) + copies(6, dg, sib, p))
            for cp in sent:
                cp.wait_send()
        for cp in local():
            cp.wait()

    return first, hand_on, finish


def _gather_sems(n_small):
    n_arr = 2 * GATHER_PARTS + n_small
    return [pltpu.SemaphoreType.DMA((7, n_arr)), pltpu.SemaphoreType.DMA((7, n_arr)), pltpu.SemaphoreType.DMA((n_arr + 1,))]


def _gather_weights(win, head, tail, wout, cw, zeros):
    small_in = (wout, cw, head, tail)
    n_zero = zeros.shape[0]
    assert W_IN + n_zero == W_ROWS and W_IN % GROUP == 0

    def body(win_ref, wout_ref, cw_ref, head_ref, tail_ref, z_ref, w_ref, gout_ref, gcw_ref, ghead_ref, gtail_ref, *sems):
        x, y, c = _place()
        me = 4 * x + 2 * y + c
        small = ((wout_ref, gout_ref), (cw_ref, gcw_ref), (head_ref, ghead_ref), (tail_ref, gtail_ref))

        def run(k):
            for stage in _gather_stages(k, win_ref, small, z_ref, n_zero, w_ref, *sems):
                stage()

        for k in range(N_DEV):
            pl.when(me == k)(functools.partial(run, k))

    anyspec = pl.BlockSpec(memory_space=pl.ANY)
    return pl.pallas_call(
        body, name="gather_weights", in_specs=[anyspec] * 6, out_specs=[anyspec] * 5,
        out_shape=[jax.ShapeDtypeStruct((W_ROWS, D), win.dtype)]
        + [jax.ShapeDtypeStruct((N_DEV,) + a.shape, a.dtype) for a in small_in],
        scratch_shapes=_gather_sems(len(small_in)),
    )(win, wout, cw, head, tail, zeros)


_REL = [(dx, dy, dc) for dx in (0, 1) for dy in (0, 1) for dc in (0, 1)][1:]
_HBM = pl.BlockSpec(memory_space=pltpu.HBM)
_SEM = pl.BlockSpec(memory_space=pltpu.SEMAPHORE)
_EFFECT = pltpu.SideEffectType.DATAFLOW_SIDE_EFFECTING


def _peer(k):
    x, y, c = _place()
    dx, dy, dc = _REL[k]
    return (1 - x if dx else x, 1 - y if dy else y, 1 - c if dc else c)


def _exchange_start(parts, *, name):
    n = len(parts)

    def body(*refs):
        ins, lands = refs[:n], refs[n:2 * n]
        send_sems, recv_sems, token = refs[2 * n], refs[2 * n + 1], refs[-1]
        x, y, c = _place()
        me = 4 * x + 2 * y + c
        for a in range(n):
            for k in range(len(_REL)):
                px, py, pc = _peer(k)
                pltpu.make_async_remote_copy(
                    src_ref=ins[a].at[4 * px + 2 * py + pc], dst_ref=lands[a].at[me],
                    send_sem=send_sems.at[len(_REL) * a + k], recv_sem=recv_sems.at[len(_REL) * a + k],
                    device_id=(px, py, pc), device_id_type=MESH).start()
        token[...] = jnp.zeros_like(token)

    sem = pltpu.SemaphoreType.DMA((len(_REL) * n,))
    bufs = [pltpu.HBM(p.shape, p.dtype) for p in parts]
    outs = pl.pallas_call(
        body, name=name,
        out_shape=(sem, sem, *bufs, *bufs, jax.ShapeDtypeStruct((8, LANE), F32)),
        in_specs=(_HBM,) * (2 * n), out_specs=(_SEM, _SEM, *(_HBM,) * (2 * n), pl.BlockSpec(memory_space=pltpu.VMEM)),
        input_output_aliases={i: 2 + i for i in range(2 * n)},
        compiler_params=pltpu.CompilerParams(has_side_effects=_EFFECT),
    )(*[pltpu.with_memory_space_constraint(p, pltpu.HBM) for p in parts],
      *[pltpu.with_memory_space_constraint(lax.empty(p.shape, p.dtype), pltpu.HBM) for p in parts])
    return outs[0], outs[1], outs[2:2 + n], outs[2 + n:2 + 2 * n], outs[-1]


def _exchange_wait(send_sems, recv_sems, parts, lands, after, *, name):
    n = len(parts)

    def body(*refs):
        ins, lands_ = refs[:n], refs[n:2 * n]
        ssem, rsem = refs[2 * n], refs[2 * n + 1]
        for a in range(n):
            for k in range(len(_REL)):
                px, py, pc = _peer(k)
                p = 4 * px + 2 * py + pc
                cp = pltpu.make_async_remote_copy(
                    src_ref=ins[a].at[p], dst_ref=lands_[a].at[p],
                    send_sem=ssem.at[len(_REL) * a + k], recv_sem=rsem.at[len(_REL) * a + k],
                    device_id=(px, py, pc), device_id_type=MESH)
                cp.wait_send()
                cp.wait_recv()

    bufs = [pltpu.HBM(p.shape, p.dtype) for p in parts]
    outs = pl.pallas_call(
        body, name=name, out_shape=(*bufs, *bufs),
        in_specs=(*(_HBM,) * (2 * n), _SEM, _SEM, pl.BlockSpec(memory_space=pl.ANY)), out_specs=(_HBM,) * (2 * n),
        input_output_aliases={i: i for i in range(2 * n)},
        compiler_params=pltpu.CompilerParams(has_side_effects=_EFFECT),
    )(*parts, *lands, send_sems, recv_sems, after)
    return outs[:n], outs[n:]


WEIGHTS = ('norm_w', 'w_in', 'gate_b', 'sgu_norm_g', 'sgu_norm_b', 'sgu_w', 'sgu_b', 'conv_w', 'conv_b', 'dt_bias', 'A_log',
           'D_skip', 'ssd_norm_w', 'w_out', 'final_norm_w')
SHARDED = ('w_in', 'conv_w', 'w_out')
PACK_ROW = 8 * LANE


def _constants():
    tri = np.tril(np.ones((CHUNK, CHUNK), np.float32))
    expand = np.zeros((DT_W, D), np.float32)
    for h in range(HEADS):
        expand[h, h * HEADDIM:(h + 1) * HEADDIM] = 1.0
    sel = np.zeros((D, LANE), np.float32)
    for g in range(SGU_GROUPS):
        sel[g * LANE:(g + 1) * LANE, g] = 1.0
    pos_chunk = np.arange(SGU_BLOCK) // CHUNK
    mask = (pos_chunk[None, :] <= pos_chunk[:, None]).astype(np.float32)
    shift = np.zeros(((CONV_K - 1) * CHUNK, HALO_BLK + CHUNK), np.float32)
    for kk in range(CONV_K - 1):
        for t in range(CHUNK):
            shift[kk * CHUNK + t, HALO_BLK - (CONV_K - 1) + t + kk] = 1.0
    return dict(tri=jnp.asarray(tri, BF16), triT=jnp.asarray(tri.T.copy(), BF16), expand=jnp.asarray(np.tile(expand, (3, 1)), BF16),
                shift=jnp.asarray(shift, BF16),
                expandT=jnp.asarray(expand.T.copy(), BF16), sel=jnp.asarray(sel), mask=jnp.asarray(mask))


def _to_shards(segs):
    starts = np.cumsum([0] + [n for _, n in segs])
    assert starts[-1] == W_IN
    slabs = []
    for k in range(N_DEV):
        pieces = []
        for (s, n), s0 in zip(segs, starts[:-1]):
            lo, hi = max(k * SHARD_IN, s0), min((k + 1) * SHARD_IN, s0 + n)
            if lo < hi:
                pieces.append(s[lo - s0:hi - s0])
        slabs.append(jnp.concatenate(pieces, axis=0))
    return jnp.stack(slabs)


def _local_step(x2, tgt, wpT, wout, cw, p, exchange_small, exchange):
    S = x2.shape[0]
    k = _constants()
    xn, proj = _in_proj(x2, p['norm_w'], wpT, tm=min(1024, S), tn=2048)
    wm32 = p['sgu_w'][0] * k['mask']
    wm = wm32.astype(BF16)
    wmT = jnp.swapaxes(wm32, 1, 2).astype(BF16)
    bias_full = jnp.repeat(p['sgu_b'][0].T, LANE, axis=1)
    tm_sgu = min(512, S)
    ya = _sgu_fwd(proj, p['sgu_norm_g'], p['sgu_norm_b'], wm, bias_full, tm=tm_sgu)
    pad32 = lambda a: jnp.pad(a, ((0, 0), (0, DT_W - HEADS)))
    dtb_p, alog_p = pad32(p['dt_bias']), pad32(p['A_log'])
    d_exp = jnp.repeat(p['D_skip'], HEADDIM, axis=1)
    ssd_args = (cw, p['conv_b'], dtb_p, alog_p, d_exp, p['ssd_norm_w'])
    y, yb, states = _ssd_fwd(proj, *ssd_args, k['tri'], k['expand'], k['shift'])
    dh, dhb, mb, dya, dyb, dgl, loss, dfw, dgb = _head(
        x2, ya, yb, proj, tgt, p['gate_b'], wout, p['final_norm_w'][None, :], tm=min(256, S))
    dsgu, dws, dbsT, dsg, dsb = _sgu_bwd(proj, dya, p['sgu_norm_g'], p['sgu_norm_b'], wm, wmT, bias_full, k['mask'], k['sel'],
                                         tm=tm_sgu)
    dssd, dcw, dcb, ddtb, dalog, dD, dnw = _ssd_bwd(proj, dyb, y, states, *ssd_args, k['tri'], k['triT'], k['expand'], k['expandT'],
                                                    k['shift'])
    grads = dict(
        gate_b=dgb[0:1], sgu_norm_g=dsg[0:1], sgu_norm_b=dsb[0:1], sgu_w=dws[None],
        sgu_b=dbsT[:, :SGU_GROUPS].T[None], conv_w=dcw[0:CONV_K][None], conv_b=dcb[0:1], dt_bias=ddtb[0:1, :HEADS],
        A_log=dalog[0:1, :HEADS], D_skip=dD[0:1, :HEADS], ssd_norm_w=dnw[0:1], final_norm_w=dfw[0])
    tw = dict(trans_a=True, out_dtype=BF16, tm=1024, tn=512, tk=S)
    dw_out = _matmul(mb, dhb, name="dw_out", **tw)
    token = exchange_small(loss[0, 0], grads, dw_out)
    dwT_sgu, dwT_gate, dwT_ssd = _dw_in([dsgu, dgl, dssd], xn, token, tm=256)
    token = exchange([(dwT_sgu, SEG_SGU[1]), (dwT_ssd, W_IN - SEG_SSD[0]), (dwT_gate, SEG_GATE[1])])
    tm, tn = min(1024, S), 1024
    dxn = _matmul(dsgu, wpT, tm=tm, tn=512, tk=SEG_SGU[1], after=token, name="dxn_sgu")
    dxn = _matmul(dgl, wpT, b_koff=SEG_GATE[0] // 2048, tm=tm, tn=tn, tk=2048, add=dxn, name="dxn_gate")
    grad_x, dnorm = _dxn_last_norm(dssd, wpT, SEG_SSD[0], dxn, x2, dh, p['norm_w'], tm=min(256, S))
    return grad_x, dnorm[0:1]


def _pack(arrs):
    rows, offs, r = [], [], 0
    for a in arrs:
        n = a.size
        nr = -(-n // PACK_ROW) * 8
        rows.append(jnp.pad(a.reshape(-1).astype(F32), (0, nr * LANE - n)).reshape(nr, LANE))
        offs.append(r)
        r += nr
    return jnp.concatenate(rows, axis=0), offs


def kernel(x, norm_w, w_in, gate_b, sgu_norm_g, sgu_norm_b, sgu_w, sgu_b, conv_w, conv_b, dt_bias, A_log, D_skip, ssd_norm_w, w_out, final_norm_w, loss_target, m_norm_w, m_w_in, m_gate_b, m_sgu_norm_g, m_sgu_norm_b, m_sgu_w, m_sgu_b, m_conv_w, m_conv_b, m_dt_bias, m_A_log, m_D_skip, m_ssd_norm_w, m_w_out, m_final_norm_w, v_norm_w, v_w_in, v_gate_b, v_sgu_norm_g, v_sgu_norm_b, v_sgu_w, v_sgu_b, v_conv_w, v_conv_b, v_dt_bias, v_A_log, v_D_skip, v_ssd_norm_w, v_w_out, v_final_norm_w):
    w = dict(norm_w=norm_w, w_in=w_in, gate_b=gate_b, sgu_norm_g=sgu_norm_g, sgu_norm_b=sgu_norm_b, sgu_w=sgu_w, sgu_b=sgu_b,
             conv_w=conv_w, conv_b=conv_b, dt_bias=dt_bias, A_log=A_log, D_skip=D_skip, ssd_norm_w=ssd_norm_w, w_out=w_out,
             final_norm_w=final_norm_w)
    m = dict(norm_w=m_norm_w, w_in=m_w_in, gate_b=m_gate_b, sgu_norm_g=m_sgu_norm_g, sgu_norm_b=m_sgu_norm_b, sgu_w=m_sgu_w,
             sgu_b=m_sgu_b, conv_w=m_conv_w, conv_b=m_conv_b, dt_bias=m_dt_bias, A_log=m_A_log, D_skip=m_D_skip,
             ssd_norm_w=m_ssd_norm_w, w_out=m_w_out, final_norm_w=m_final_norm_w)
    v = dict(norm_w=v_norm_w, w_in=v_w_in, gate_b=v_gate_b, sgu_norm_g=v_sgu_norm_g, sgu_norm_b=v_sgu_norm_b, sgu_w=v_sgu_w,
             sgu_b=v_sgu_b, conv_w=v_conv_w, conv_b=v_conv_b, dt_bias=v_dt_bias, A_log=v_A_log, D_skip=v_D_skip,
             ssd_norm_w=v_ssd_norm_w, w_out=v_w_out, final_norm_w=v_final_norm_w)
    me = 4 * lax.axis_index("x") + 2 * lax.axis_index("y") + lax.axis_index("c")
    shard_cw = XBC_W // N_DEV

    tpose = lambda a: jnp.swapaxes(a[0], 0, 1)
    wT = tpose(w_in).astype(BF16)
    first_group = (GROUP - (me * SHARD_IN) % GROUP) % GROUP
    window = lax.dynamic_slice(jnp.pad(wT, ((0, GROUP), (0, 0))), (first_group, 0), (INTERIOR, D))
    wpT, g_out, g_cw, heads, tails = _gather_weights(window, wT[:GROUP], wT[SHARD_IN - GROUP:], w_out[0].astype(BF16),
                                                     conv_w[0], jnp.zeros((W_ROWS - W_IN, D), BF16))
    wpT = _patch_straddlers(wpT, heads, tails)
    wout_full = g_out.reshape(D, D)
    cw_full = jnp.swapaxes(g_cw, 0, 1).reshape(CONV_K, XBC_W)

    flight = {}

    small = [n for n in WEIGHTS if n not in SHARDED and n != 'norm_w']
    early = {}

    def exchange_small(loss_part, grads, dw_out):
        early['packed'], early['offs'] = _pack([grads[n] for n in small] + [loss_part, grads['conv_w']])
        parts = [jnp.broadcast_to(early['packed'][None], (N_DEV,) + early['packed'].shape), dw_out.reshape(N_DEV, D // N_DEV, D)]
        early['sems'], early['rsems'], early['parts'], early['lands'], token = _exchange_start(parts, name="small_start")
        return token

    def exchange(dw_inT_segs):
        parts = [_to_shards(dw_inT_segs)]
        flight['sems'], flight['rsems'], flight['parts'], flight['lands'], token = _exchange_start(parts, name="exchange_start")
        return token

    grad_x, dnorm = _local_step(x[0], loss_target[0], wpT, wout_full, cw_full, w, exchange_small, exchange)
    norm_packed = _pack([dnorm])[0]
    norm_sems, norm_rsems, norm_src, norm_lands, _ = _exchange_start(
        [jnp.broadcast_to(norm_packed[None], (N_DEV,) + norm_packed.shape)], name="norm_start")
    (_, own_out), (land_small, land_out) = _exchange_wait(
        early['sems'], early['rsems'], early['parts'], early['lands'], grad_x, name="small_wait")
    (own_in,), (land_in,) = _exchange_wait(
        flight['sems'], flight['rsems'], flight['parts'], flight['lands'], grad_x, name="exchange_wait")
    me_arr = jnp.reshape(me, (1,)).astype(jnp.int32)
    res = {}
    res['w_in'] = [jnp.swapaxes(o, 0, 1) for o in _adamw_own(
        me_arr, own_in, land_in, tpose(w_in), tpose(m_w_in), tpose(v_w_in), tr=SHARD_IN, tc=256, name="adamw_w_in")]
    res['w_out'] = _adamw_own(me_arr, own_out, land_out, w_out[0], m_w_out[0], v_w_out[0], tr=128, tc=D, name="adamw_w_out")

    _, (land_norm,) = _exchange_wait(norm_sems, norm_rsems, norm_src, norm_lands, res['w_out'][0], name="norm_wait")
    norm_parts = lax.dynamic_update_slice(land_norm, norm_packed[None], (me, 0, 0))

    offs = early['offs']
    gathered = lax.dynamic_update_slice(land_small, early['packed'][None], (me, 0, 0))
    off_loss, off_cw = offs[-2], offs[-1]
    cw_parts = gathered[:, off_cw:, :].reshape(N_DEV, CONV_K, XBC_W)
    cw_parts = lax.dynamic_slice_in_dim(cw_parts, me * shard_cw, shard_cw, axis=2)
    cw_rows = _pack([cw_parts[0]])[0].shape[0]
    cw_parts = jnp.pad(cw_parts.reshape(N_DEV, -1), ((0, 0), (0, cw_rows * LANE - CONV_K * shard_cw))).reshape(N_DEV, cw_rows, LANE)
    off_norm = off_cw + cw_rows
    parts = jnp.concatenate([gathered[:, :off_cw, :], cw_parts, norm_parts], axis=1)
    zero = jnp.zeros((), F32)
    packs = [jnp.concatenate([_pack([d[n] for n in small] + [zero, d['conv_w']])[0], _pack([d['norm_w']])[0]], axis=0)
             for d in (w, m, v)]
    outs = _adamw(parts, *packs, tr=parts.shape[1], name="adamw_small")
    res['norm_w'] = [o[off_norm:].reshape(-1)[:D].reshape(w['norm_w'].shape) for o in outs]

    def unpack(o, name):
        if name == 'conv_w':
            return o[off_cw:off_cw + cw_rows].reshape(-1)[:CONV_K * shard_cw].reshape(w['conv_w'].shape)
        r0 = offs[small.index(name)]
        n = w[name].size
        return o[r0:r0 + -(-n // PACK_ROW) * 8].reshape(-1)[:n].reshape(w[name].shape)

    for n in small + ['conv_w']:
        res[n] = [unpack(o, n) for o in outs]
    for n in ('w_in', 'w_out'):
        res[n] = [o[None] for o in res[n]]
    loss = outs[0][off_loss, 0]
    return (loss, grad_x[None], *[res[n][0] for n in WEIGHTS], *[res[n][1] for n in WEIGHTS],
            *[res[n][2] for n in WEIGHTS], *[res[n][3] for n in WEIGHTS])
```
